```python
import jax, jax.numpy as jnp
from jax import lax
import numpy as np

D_MODEL = 1024
BATCH = 8
SEQ = 8192
DEPTH = 2

N_MIXERS = 2
N_HGRN_LAYERS = (DEPTH + 1) // 2
N_ATTN_LAYERS = DEPTH // 2

HGRN_EXPAND = 128
HGRN_HEADS = D_MODEL // HGRN_EXPAND
HGRN_KEY_DIM = HGRN_EXPAND
HGRN_VAL_DIM = D_MODEL // HGRN_HEADS
HGRN_QK_WIDTH = HGRN_HEADS * HGRN_KEY_DIM
HGRN_V_WIDTH = HGRN_HEADS * HGRN_VAL_DIM
HGRN_IN_WIDTH = 2 * HGRN_QK_WIDTH + 2 * HGRN_V_WIDTH
HGRN_CHUNK = 64

ATTN_HEAD_DIM = 128
DILATED_GROUPS = ((128, 1), (512, 4), (2048, 16))
HEADS_PER_GROUP = 4
N_GROUPS = len(DILATED_GROUPS)
ATTN_HEADS = HEADS_PER_GROUP * N_GROUPS
ATTN_WIDTH = ATTN_HEADS * ATTN_HEAD_DIM
ROPE_THETA = 10000.0

D_FF = ((8 * D_MODEL // 3 + 255) // 256) * 256

NORM_EPS = 1e-6

kernel_name = "hgrn2_dilated_swa_interleaved_trunk"


def rmsnorm(x, gain):
    xf = x.astype(jnp.float32)
    y = xf * lax.rsqrt(jnp.mean(xf * xf, axis=-1, keepdims=True) + NORM_EPS)
    return (y * gain.astype(jnp.float32)).astype(x.dtype)


def rope_tables(seq_len):
    inv_freq = 1.0 / (ROPE_THETA ** (jnp.arange(0, ATTN_HEAD_DIM, 2, dtype=jnp.float32) / ATTN_HEAD_DIM))
    pos = jnp.arange(seq_len, dtype=jnp.float32)
    ang = pos[:, None] * inv_freq[None, :]
    return jnp.cos(ang)[:, None, :], jnp.sin(ang)[:, None, :]


def apply_rope(x, cos, sin):
    xf = x.astype(jnp.float32)
    x1, x2 = jnp.split(xf, 2, axis=-1)
    out = jnp.concatenate([x1 * cos - x2 * sin, x2 * cos + x1 * sin], axis=-1)
    return out.astype(x.dtype)


def hgrn2_mixer(u, w_in, lower_bound, out_gain, w_out):
    B, S, _ = u.shape
    H, K, V, C = HGRN_HEADS, HGRN_KEY_DIM, HGRN_VAL_DIM, HGRN_CHUNK
    proj = u @ w_in
    q, f, i, g = jnp.split(proj, [HGRN_QK_WIDTH, 2 * HGRN_QK_WIDTH, 2 * HGRN_QK_WIDTH + HGRN_V_WIDTH], axis=-1)
    lb = lower_bound.astype(jnp.float32)
    forget = lb + (1.0 - lb) * jax.nn.sigmoid(f.astype(jnp.float32))
    key = 1.0 - forget
    log_f = jnp.log(forget)
    query = jax.nn.silu(q.astype(jnp.float32))
    value = i.astype(jnp.float32)

    def to_chunks(t, d):
        return t.reshape(B, S // C, C, H, d).transpose(1, 0, 3, 2, 4)

    xs = (to_chunks(query, K), to_chunks(key, K), to_chunks(value, V), to_chunks(log_f, K))
    causal = jnp.tril(jnp.ones((C, C), dtype=bool))[:, :, None]

    def chunk_step(state, inp):
        qc, kc, vc, gc = inp
        b = jnp.cumsum(gc, axis=2)
        o_inter = jnp.einsum('bhck,bhkv->bhcv', qc * jnp.exp(b), state)
        diff = b[:, :, :, None, :] - b[:, :, None, :, :]
        decay = jnp.exp(jnp.where(causal, diff, -jnp.inf))
        scores = jnp.einsum('bhtk,bhsk,bhtsk->bhts', qc, kc, decay)
        o_intra = jnp.einsum('bhts,bhsv->bhtv', scores, vc)
        b_last = b[:, :, -1, :]
        k_to_end = kc * jnp.exp(b_last[:, :, None, :] - b)
        new_state = jnp.exp(b_last)[..., None] * state + jnp.einsum('bhck,bhcv->bhkv', k_to_end, vc)
        return new_state, o_inter + o_intra

    state0 = jnp.zeros((B, H, K, V), dtype=jnp.float32)
    _, o = lax.scan(chunk_step, state0, xs)
    o = o.transpose(1, 0, 3, 2, 4).reshape(B, S, H, V)
    o = rmsnorm(o, out_gain)
    o = o * jax.nn.silu(g.astype(jnp.float32)).reshape(B, S, H, V)
    return o.reshape(B, S, H * V).astype(u.dtype) @ w_out


def dilated_window_group(q, k, v, window, dilation):
    B, S, Hg, D = q.shape
    span = window // dilation
    L = S // dilation
    nb = -(-L // span)
    Lp = nb * span

    def to_blocks(t):
        t = t.reshape(B, L, dilation, Hg, D).transpose(0, 2, 3, 1, 4)
        t = jnp.pad(t, ((0, 0), (0, 0), (0, 0), (0, Lp - L), (0, 0)))
        return t.reshape(B, dilation, Hg, nb, span, D)

    qb, kb, vb = to_blocks(q), to_blocks(k), to_blocks(v)

    def with_prev(t):
        prev = jnp.pad(t, ((0, 0), (0, 0), (0, 0), (1, 0), (0, 0), (0, 0)))[:, :, :, :-1]
        return jnp.concatenate([prev, t], axis=4)

    kw, vw = with_prev(kb), with_prev(vb)
    scores = jnp.einsum('bdhnqe,bdhnke->bdhnqk', qb, kw, preferred_element_type=jnp.float32) * (D ** -0.5)
    blk = jnp.arange(nb)[:, None, None] * span
    qpos = blk + jnp.arange(span)[None, :, None]
    kpos = blk - span + jnp.arange(2 * span)[None, None, :]
    mask = (kpos <= qpos) & (kpos >= qpos - span) & (kpos >= 0)
    scores = jnp.where(mask, scores, -jnp.inf)
    m = jnp.max(scores, axis=-1, keepdims=True)
    p = jnp.exp(scores - m)
    l = jnp.sum(p, axis=-1, keepdims=True)
    out = jnp.einsum('bdhnqk,bdhnke->bdhnqe', p, vw.astype(jnp.float32)) / l
    lse = (m + jnp.log(l))[..., 0]
    out = out.reshape(B, dilation, Hg, Lp, D)[:, :, :, :L].transpose(0, 3, 1, 2, 4).reshape(B, S, Hg, D)
    lse = lse.reshape(B, dilation, Hg, Lp)[:, :, :, :L].transpose(0, 3, 1, 2).reshape(B, S, Hg)
    return out, lse


def dilated_attention_mixer(u, w_qkv, w_out, cos, sin):
    B, S, _ = u.shape
    qkv = (u @ w_qkv).reshape(B, S, 3, ATTN_HEADS, ATTN_HEAD_DIM)
    q = apply_rope(qkv[:, :, 0], cos, sin)
    k = apply_rope(qkv[:, :, 1], cos, sin)
    v = qkv[:, :, 2]
    outs, lses = [], []
    for gi, (window, dilation) in enumerate(DILATED_GROUPS):
        hs = slice(gi * HEADS_PER_GROUP, (gi + 1) * HEADS_PER_GROUP)
        o_g, lse_g = dilated_window_group(q[:, :, hs], k[:, :, hs], v[:, :, hs], window, dilation)
        outs.append(o_g)
        lses.append(lse_g)
    o = jnp.stack(outs, axis=2)
    lse = jnp.stack(lses, axis=2)
    alpha = jax.nn.softmax(lse, axis=2)
    o = (o * alpha[..., None]).reshape(B, S, ATTN_WIDTH).astype(u.dtype)
    return o @ w_out


def swiglu_ffn(u, w_in, w_down):
    gate, up = jnp.split(u @ w_in, 2, axis=-1)
    return (jax.nn.silu(gate) * up) @ w_down


def _fwd_setup_inputs(seed: int = 0) -> dict:
    key = jax.random.key(seed)
    ks = jax.random.split(key, 13)
    f32 = jnp.float32

    def dense(k, shape, fan_in):
        return jax.random.normal(k, shape, f32) * (fan_in ** -0.5)

    def gain(k, shape):
        return 1.0 + 0.02 * jax.random.normal(k, shape, f32)

    return {
        "x": jax.random.normal(ks[0], (BATCH, SEQ, D_MODEL), f32),
        "norm_mix": gain(ks[1], (DEPTH, D_MODEL)),
        "norm_ffn": gain(ks[2], (DEPTH, D_MODEL)),
        "hgrn_w_in": dense(ks[3], (N_HGRN_LAYERS, D_MODEL, HGRN_IN_WIDTH), D_MODEL),
        "hgrn_lb_logits": 0.5 * jax.random.normal(ks[4], (DEPTH + 1, HGRN_QK_WIDTH), f32),
        "hgrn_out_norm": gain(ks[5], (N_HGRN_LAYERS, HGRN_VAL_DIM)),
        "hgrn_w_out": dense(ks[6], (N_HGRN_LAYERS, HGRN_V_WIDTH, D_MODEL), HGRN_V_WIDTH),
        "attn_w_qkv": dense(ks[7], (N_ATTN_LAYERS, D_MODEL, 3 * ATTN_WIDTH), D_MODEL),
        "attn_w_out": dense(ks[8], (N_ATTN_LAYERS, ATTN_WIDTH, D_MODEL), ATTN_WIDTH),
        "ffn_w_in": dense(ks[9], (DEPTH, D_MODEL, 2 * D_FF), D_MODEL),
        "ffn_w_down": dense(ks[10], (DEPTH, D_FF, D_MODEL), D_FF),
        "final_norm": gain(ks[11], (D_MODEL,)),
    }


def _fwd_reference(x, norm_mix, norm_ffn, hgrn_w_in, hgrn_lb_logits, hgrn_out_norm, hgrn_w_out,
              attn_w_qkv, attn_w_out, ffn_w_in, ffn_w_down, final_norm):
    lb_table = jnp.cumsum(jax.nn.softmax(hgrn_lb_logits.astype(jnp.float32), axis=0), axis=0)
    cos, sin = rope_tables(x.shape[1])
    h = x
    for layer in range(DEPTH):
        u = rmsnorm(h, norm_mix[layer])
        if layer % N_MIXERS == 0:
            a = layer // N_MIXERS
            mix = hgrn2_mixer(u, hgrn_w_in[a], lb_table[layer], hgrn_out_norm[a], hgrn_w_out[a])
        else:
            a = layer // N_MIXERS
            mix = dilated_attention_mixer(u, attn_w_qkv[a], attn_w_out[a], cos, sin)
        h = h + mix
        h = h + swiglu_ffn(rmsnorm(h, norm_ffn[layer]), ffn_w_in[layer], ffn_w_down[layer])
    return rmsnorm(h, final_norm)


import jax as _jax
import jax.numpy as _jnp

TWIN_FORMAT = 'train_step'
FWD_PARAMS = ['x', 'norm_mix', 'norm_ffn', 'hgrn_w_in', 'hgrn_lb_logits', 'hgrn_out_norm', 'hgrn_w_out', 'attn_w_qkv', 'attn_w_out', 'ffn_w_in', 'ffn_w_down', 'final_norm']
TWIN_WEIGHTS = ['norm_mix', 'norm_ffn', 'hgrn_w_in', 'hgrn_lb_logits', 'hgrn_out_norm', 'hgrn_w_out', 'attn_w_qkv', 'attn_w_out', 'ffn_w_in', 'ffn_w_down', 'final_norm']
TWIN_DIFF_INPUT = 'x'
TWIN_INPUTS = ['x', 'norm_mix', 'norm_ffn', 'hgrn_w_in', 'hgrn_lb_logits', 'hgrn_out_norm', 'hgrn_w_out', 'attn_w_qkv', 'attn_w_out', 'ffn_w_in', 'ffn_w_down', 'final_norm', 'loss_target', 'm_norm_mix', 'm_norm_ffn', 'm_hgrn_w_in', 'm_hgrn_lb_logits', 'm_hgrn_out_norm', 'm_hgrn_w_out', 'm_attn_w_qkv', 'm_attn_w_out', 'm_ffn_w_in', 'm_ffn_w_down', 'm_final_norm', 'v_norm_mix', 'v_norm_ffn', 'v_hgrn_w_in', 'v_hgrn_lb_logits', 'v_hgrn_out_norm', 'v_hgrn_w_out', 'v_attn_w_qkv', 'v_attn_w_out', 'v_ffn_w_in', 'v_ffn_w_down', 'v_final_norm']
TWIN_OUTPUTS = ['loss', 'grad_x', 'grad_norm_mix', 'grad_norm_ffn', 'grad_hgrn_w_in', 'grad_hgrn_lb_logits', 'grad_hgrn_out_norm', 'grad_hgrn_w_out', 'grad_attn_w_qkv', 'grad_attn_w_out', 'grad_ffn_w_in', 'grad_ffn_w_down', 'grad_final_norm', 'delta_norm_mix', 'delta_norm_ffn', 'delta_hgrn_w_in', 'delta_hgrn_lb_logits', 'delta_hgrn_out_norm', 'delta_hgrn_w_out', 'delta_attn_w_qkv', 'delta_attn_w_out', 'delta_ffn_w_in', 'delta_ffn_w_down', 'delta_final_norm', 'new_m_norm_mix', 'new_m_norm_ffn', 'new_m_hgrn_w_in', 'new_m_hgrn_lb_logits', 'new_m_hgrn_out_norm', 'new_m_hgrn_w_out', 'new_m_attn_w_qkv', 'new_m_attn_w_out', 'new_m_ffn_w_in', 'new_m_ffn_w_down', 'new_m_final_norm', 'new_v_norm_mix', 'new_v_norm_ffn', 'new_v_hgrn_w_in', 'new_v_hgrn_lb_logits', 'new_v_hgrn_out_norm', 'new_v_hgrn_w_out', 'new_v_attn_w_qkv', 'new_v_attn_w_out', 'new_v_ffn_w_in', 'new_v_ffn_w_down', 'new_v_final_norm']
TWIN_LEAF_KINDS = {'loss': 'loss', 'grad_x': 'grad_x', 'grad_norm_mix': 'grad_w', 'grad_norm_ffn': 'grad_w', 'grad_hgrn_w_in': 'grad_w', 'grad_hgrn_lb_logits': 'grad_w', 'grad_hgrn_out_norm': 'grad_w', 'grad_hgrn_w_out': 'grad_w', 'grad_attn_w_qkv': 'grad_w', 'grad_attn_w_out': 'grad_w', 'grad_ffn_w_in': 'grad_w', 'grad_ffn_w_down': 'grad_w', 'grad_final_norm': 'grad_w', 'delta_norm_mix': 'delta_w', 'delta_norm_ffn': 'delta_w', 'delta_hgrn_w_in': 'delta_w', 'delta_hgrn_lb_logits': 'delta_w', 'delta_hgrn_out_norm': 'delta_w', 'delta_hgrn_w_out': 'delta_w', 'delta_attn_w_qkv': 'delta_w', 'delta_attn_w_out': 'delta_w', 'delta_ffn_w_in': 'delta_w', 'delta_ffn_w_down': 'delta_w', 'delta_final_norm': 'delta_w', 'new_m_norm_mix': 'new_m', 'new_m_norm_ffn': 'new_m', 'new_m_hgrn_w_in': 'new_m', 'new_m_hgrn_lb_logits': 'new_m', 'new_m_hgrn_out_norm': 'new_m', 'new_m_hgrn_w_out': 'new_m', 'new_m_attn_w_qkv': 'new_m', 'new_m_attn_w_out': 'new_m', 'new_m_ffn_w_in': 'new_m', 'new_m_ffn_w_down': 'new_m', 'new_m_final_norm': 'new_m', 'new_v_norm_mix': 'new_v', 'new_v_norm_ffn': 'new_v', 'new_v_hgrn_w_in': 'new_v', 'new_v_hgrn_lb_logits': 'new_v', 'new_v_hgrn_out_norm': 'new_v', 'new_v_hgrn_w_out': 'new_v', 'new_v_attn_w_qkv': 'new_v', 'new_v_attn_w_out': 'new_v', 'new_v_ffn_w_in': 'new_v', 'new_v_ffn_w_down': 'new_v', 'new_v_final_norm': 'new_v'}


def _forward(args):
    return _fwd_reference(*[args[k] for k in FWD_PARAMS])


def _output_shape():
    def fwd():
        inp = _fwd_setup_inputs(0)
        return _fwd_reference(*[inp[k] for k in FWD_PARAMS])
    out = _jax.eval_shape(fwd)
    return out.shape, out.dtype

N_MICROBATCH = 1
ADAM_LR = 0.001
ADAM_B1 = 0.9
ADAM_B2 = 0.999
ADAM_EPS = 1e-08
ADAM_WD = 0.01
ADAM_STEP = 10
PER_EXAMPLE_BATCH_AXIS = {'x': 0, 'loss_target': 0}
SHARED_INPUTS = []
_WEIGHT_DTYPES = {'norm_mix': _jnp.float32, 'norm_ffn': _jnp.float32, 'hgrn_w_in': _jnp.float32, 'hgrn_lb_logits': _jnp.float32, 'hgrn_out_norm': _jnp.float32, 'hgrn_w_out': _jnp.float32, 'attn_w_qkv': _jnp.float32, 'attn_w_out': _jnp.float32, 'ffn_w_in': _jnp.float32, 'ffn_w_down': _jnp.float32, 'final_norm': _jnp.float32}
MOMENT_SCALE = {'norm_mix': 1.632916e-01, 'norm_ffn': 1.638536e-01, 'hgrn_w_in': 1.126292e-01, 'hgrn_lb_logits': 7.436946e-03, 'hgrn_out_norm': 4.352156e-01, 'hgrn_w_out': 1.552158e-01, 'attn_w_qkv': 1.688943e-02, 'attn_w_out': 2.271278e-02, 'ffn_w_in': 7.040400e-02, 'ffn_w_down': 1.148325e-01, 'final_norm': 6.403716e+01}


def _to_microbatches(a, axis):
    t = _jnp.moveaxis(a, axis, 0)
    t = t.reshape((N_MICROBATCH, t.shape[0] // N_MICROBATCH) + t.shape[1:])
    return _jnp.moveaxis(t, 1, axis + 1)


def setup_inputs(seed: int = 0) -> dict:
    inp = _fwd_setup_inputs(seed)
    key = _jax.random.fold_in(_jax.random.key(seed), 7919)
    shape, _ = _output_shape()
    out = dict(inp)
    out["loss_target"] = _jax.random.normal(_jax.random.fold_in(key, 0), shape, _jnp.float32)
    for i, name in enumerate(TWIN_WEIGHTS):
        w = inp[name].astype(_jnp.float32)
        if MOMENT_SCALE is None:
            s = _jnp.sqrt(_jnp.mean(_jnp.square(w)) + 1e-30)
        else:
            s = MOMENT_SCALE[name]
        km, kv = _jax.random.split(_jax.random.fold_in(key, i + 1))
        out[name] = w
        out["m_" + name] = s * _jax.random.normal(km, w.shape, _jnp.float32)
        out["v_" + name] = (s * s) * _jax.random.uniform(kv, w.shape, _jnp.float32, 0.5, 1.5)
    if N_MICROBATCH > 1:
        for name, axis in PER_EXAMPLE_BATCH_AXIS.items():
            out[name] = _to_microbatches(out[name], axis)
    return {'x': out['x'], 'norm_mix': out['norm_mix'], 'norm_ffn': out['norm_ffn'], 'hgrn_w_in': out['hgrn_w_in'], 'hgrn_lb_logits': out['hgrn_lb_logits'], 'hgrn_out_norm': out['hgrn_out_norm'], 'hgrn_w_out': out['hgrn_w_out'], 'attn_w_qkv': out['attn_w_qkv'], 'attn_w_out': out['attn_w_out'], 'ffn_w_in': out['ffn_w_in'], 'ffn_w_down': out['ffn_w_down'], 'final_norm': out['final_norm'], 'loss_target': out['loss_target'], 'm_norm_mix': out['m_norm_mix'], 'm_norm_ffn': out['m_norm_ffn'], 'm_hgrn_w_in': out['m_hgrn_w_in'], 'm_hgrn_lb_logits': out['m_hgrn_lb_logits'], 'm_hgrn_out_norm': out['m_hgrn_out_norm'], 'm_hgrn_w_out': out['m_hgrn_w_out'], 'm_attn_w_qkv': out['m_attn_w_qkv'], 'm_attn_w_out': out['m_attn_w_out'], 'm_ffn_w_in': out['m_ffn_w_in'], 'm_ffn_w_down': out['m_ffn_w_down'], 'm_final_norm': out['m_final_norm'], 'v_norm_mix': out['v_norm_mix'], 'v_norm_ffn': out['v_norm_ffn'], 'v_hgrn_w_in': out['v_hgrn_w_in'], 'v_hgrn_lb_logits': out['v_hgrn_lb_logits'], 'v_hgrn_out_norm': out['v_hgrn_out_norm'], 'v_hgrn_w_out': out['v_hgrn_w_out'], 'v_attn_w_qkv': out['v_attn_w_qkv'], 'v_attn_w_out': out['v_attn_w_out'], 'v_ffn_w_in': out['v_ffn_w_in'], 'v_ffn_w_down': out['v_ffn_w_down'], 'v_final_norm': out['v_final_norm']}


def _loss(weights, diff, rest, loss_target):
    with _jax.named_scope("forward"):
        args = {**rest, TWIN_DIFF_INPUT: diff, **{k: w.astype(_WEIGHT_DTYPES[k]) for k, w in weights.items()}}
        y = _forward(args)
    with _jax.named_scope("loss_head"):
        err = _jnp.square(y.astype(_jnp.float32) - loss_target)
        return 0.5 * _jnp.sum(_jnp.mean(err, axis=-1)) if err.ndim else 0.5 * err


def _adamw(w, g, m, v):
    m = ADAM_B1 * m + (1.0 - ADAM_B1) * g
    v = ADAM_B2 * v + (1.0 - ADAM_B2) * _jnp.square(g)
    m_hat = m / (1.0 - ADAM_B1 ** ADAM_STEP)
    v_hat = v / (1.0 - ADAM_B2 ** ADAM_STEP)
    delta = -ADAM_LR * (m_hat / (_jnp.sqrt(v_hat) + ADAM_EPS) + ADAM_WD * w)
    return delta, m, v


def reference(x, norm_mix, norm_ffn, hgrn_w_in, hgrn_lb_logits, hgrn_out_norm, hgrn_w_out, attn_w_qkv, attn_w_out, ffn_w_in, ffn_w_down, final_norm, loss_target, m_norm_mix, m_norm_ffn, m_hgrn_w_in, m_hgrn_lb_logits, m_hgrn_out_norm, m_hgrn_w_out, m_attn_w_qkv, m_attn_w_out, m_ffn_w_in, m_ffn_w_down, m_final_norm, v_norm_mix, v_norm_ffn, v_hgrn_w_in, v_hgrn_lb_logits, v_hgrn_out_norm, v_hgrn_w_out, v_attn_w_qkv, v_attn_w_out, v_ffn_w_in, v_ffn_w_down, v_final_norm):
    given = dict(x=x, norm_mix=norm_mix, norm_ffn=norm_ffn, hgrn_w_in=hgrn_w_in, hgrn_lb_logits=hgrn_lb_logits, hgrn_out_norm=hgrn_out_norm, hgrn_w_out=hgrn_w_out, attn_w_qkv=attn_w_qkv, attn_w_out=attn_w_out, ffn_w_in=ffn_w_in, ffn_w_down=ffn_w_down, final_norm=final_norm, loss_target=loss_target, m_norm_mix=m_norm_mix, m_norm_ffn=m_norm_ffn, m_hgrn_w_in=m_hgrn_w_in, m_hgrn_lb_logits=m_hgrn_lb_logits, m_hgrn_out_norm=m_hgrn_out_norm, m_hgrn_w_out=m_hgrn_w_out, m_attn_w_qkv=m_attn_w_qkv, m_attn_w_out=m_attn_w_out, m_ffn_w_in=m_ffn_w_in, m_ffn_w_down=m_ffn_w_down, m_final_norm=m_final_norm, v_norm_mix=v_norm_mix, v_norm_ffn=v_norm_ffn, v_hgrn_w_in=v_hgrn_w_in, v_hgrn_lb_logits=v_hgrn_lb_logits, v_hgrn_out_norm=v_hgrn_out_norm, v_hgrn_w_out=v_hgrn_w_out, v_attn_w_qkv=v_attn_w_qkv, v_attn_w_out=v_attn_w_out, v_ffn_w_in=v_ffn_w_in, v_ffn_w_down=v_ffn_w_down, v_final_norm=v_final_norm)
    weights = {n: given[n] for n in TWIN_WEIGHTS}
    shared = {n: given[n] for n in SHARED_INPUTS}
    per_example = {n: given[n] for n in ['x']}
    grad_fn = _jax.value_and_grad(_loss, argnums=(0, 1))

    def one_microbatch(ex, loss_target):
        ex = dict(ex)
        diff = ex.pop(TWIN_DIFF_INPUT)
        return grad_fn(weights, diff, {**shared, **ex}, loss_target)

    if N_MICROBATCH == 1:
        loss, (grad_w, grad_x) = one_microbatch(per_example, given["loss_target"])
    else:
        def body(carry, xs):
            loss_sum, grad_sum = carry
            l_k, (gw_k, gx_k) = one_microbatch(xs[0], xs[1])
            with _jax.named_scope("update"):
                return (loss_sum + l_k, _jax.tree.map(_jnp.add, grad_sum, gw_k)), gx_k

        init = (_jnp.zeros((), _jnp.float32), _jax.tree.map(_jnp.zeros_like, weights))
        (loss, grad_w), grad_x = _jax.lax.scan(body, init, (per_example, given["loss_target"]))
    with _jax.named_scope("update"):
        delta_w, new_m, new_v = {}, {}, {}
        for n in TWIN_WEIGHTS:
            delta_w[n], new_m[n], new_v[n] = _adamw(weights[n], grad_w[n], given["m_" + n], given["v_" + n])
    return (loss, grad_x, *[grad_w[n] for n in TWIN_WEIGHTS], *[delta_w[n] for n in TWIN_WEIGHTS],
            *[new_m[n] for n in TWIN_WEIGHTS], *[new_v[n] for n in TWIN_WEIGHTS])
```

```python
import functools

import jax
import jax.numpy as jnp
from jax import lax
from jax.experimental import pallas as pl
from jax.experimental.pallas import tpu as pltpu

F32 = jnp.float32
BF16 = jnp.bfloat16

D_MODEL = 1024
N_DEV = 8
NORM_EPS = 1e-6

HGRN_HEADS = 8
HGRN_DIM = 128
HGRN_CHUNK = 64
HGRN_EXP_CLAMP = 60.0

ATTN_DIM = 128
ATTN_BLOCK = 128
ATTN_GROUP_HEADS = 4
ATTN_GROUP_WIDTH = ATTN_GROUP_HEADS * ATTN_DIM
ATTN_DILATIONS = (1, 4, 16)
ATTN_WIDTH = 3 * ATTN_GROUP_WIDTH
ROPE_THETA = 10000.0
NEG_BIG = -1e30

D_FF = 2816

ADAM_LR = 0.001
ADAM_B1 = 0.9
ADAM_B2 = 0.999
ADAM_EPS = 1e-08
ADAM_WD = 0.01
ADAM_STEP = 10

VMEM_LIMIT = 48 * 1024 * 1024

NT = (((1,), (1,)), ((), ()))
NN = (((1,), (0,)), ((), ()))
TN = (((0,), (0,)), ((), ()))


def _dot(a, b, dims):
    return lax.dot_general(a, b, dims, preferred_element_type=F32)


def _params(*sem):
    return pltpu.CompilerParams(dimension_semantics=sem, vmem_limit_bytes=VMEM_LIMIT)


def _pick_tile(n, cap, mult):
    best = None
    for t in range(mult, min(n, cap) + 1, mult):
        if n % t == 0:
            best = t
    assert best is not None, (n, cap, mult)
    return best


def _sigmoid(x):
    return 1.0 / (1.0 + jnp.exp(-x))


def _mm_nt(a, w, *, n_out, tn, w_block, out_dtype, name, rope=None):
    M, K = a.shape
    tm = _pick_tile(M, 1024, 16)
    nj = n_out // tn

    def body(*refs):
        if rope is None:
            a_ref, w_ref, o_ref = refs
        else:
            a_ref, w_ref, cos_ref, sin_ref, o_ref = refs
        acc = _dot(a_ref[...], w_ref[...], NT)
        if rope is None:
            o_ref[...] = acc.astype(out_dtype)
        else:
            j = pl.program_id(1)

            @pl.when(j < rope[2])
            def _():
                cos = cos_ref[...]
                sin = sin_ref[...]
                for h in range(tn // ATTN_DIM):
                    xh = acc[:, h * ATTN_DIM:(h + 1) * ATTN_DIM]
                    rot = pltpu.roll(xh, ATTN_DIM // 2, 1)
                    o_ref[:, h * ATTN_DIM:(h + 1) * ATTN_DIM] = (xh * cos + rot * sin).astype(out_dtype)

            @pl.when(j >= rope[2])
            def _():
                o_ref[...] = acc.astype(out_dtype)

    in_specs = [pl.BlockSpec((tm, K), lambda i, j: (i, 0)),
                pl.BlockSpec((tn, K), lambda i, j: (w_block(j), 0))]
    args = [a, w]
    if rope is not None:
        in_specs += [pl.BlockSpec((tm, ATTN_DIM), lambda i, j: (i, 0))] * 2
        args += [rope[0], rope[1]]
    return pl.pallas_call(
        body, out_shape=jax.ShapeDtypeStruct((M, n_out), out_dtype), grid=(M // tm, nj),
        in_specs=in_specs, out_specs=pl.BlockSpec((tm, tn), lambda i, j: (i, j)),
        compiler_params=_params("parallel", "arbitrary"), name=name)(*args)


def _mm_nn(a, w, resid, *, tk, w_block, name):
    M, R = a.shape
    N = w.shape[1]
    tm = _pick_tile(M, 1024, 16)
    nk = R // tk

    def body(*refs):
        if resid is None:
            a_ref, w_ref, o_ref, acc_ref = refs
        else:
            a_ref, w_ref, r_ref, o_ref, acc_ref = refs
        k = pl.program_id(1)
        part = _dot(a_ref[...], w_ref[...], NN)

        @pl.when(k == 0)
        def _():
            acc_ref[...] = part if resid is None else part + r_ref[...]

        @pl.when(k > 0)
        def _():
            acc_ref[...] += part

        @pl.when(k == nk - 1)
        def _():
            o_ref[...] = acc_ref[...]

    in_specs = [pl.BlockSpec((tm, tk), lambda i, k: (i, k)),
                pl.BlockSpec((tk, N), lambda i, k: (w_block(k), 0))]
    args = [a, w]
    if resid is not None:
        in_specs.append(pl.BlockSpec((tm, N), lambda i, k: (i, 0)))
        args.append(resid)
    return pl.pallas_call(
        body, out_shape=jax.ShapeDtypeStruct((M, N), F32), grid=(M // tm, nk),
        in_specs=in_specs, out_specs=pl.BlockSpec((tm, N), lambda i, k: (i, 0)),
        scratch_shapes=[pltpu.VMEM((tm, N), F32)],
        compiler_params=_params("parallel", "arbitrary"), name=name)(*args)


def _mm_tn(a, b, *, tr, name):
    T, R = a.shape
    N = b.shape[1]
    tt = _pick_tile(T, 1024, 16)
    nt = T // tt

    def body(a_ref, b_ref, o_ref, acc_ref):
        t = pl.program_id(1)
        part = _dot(a_ref[...], b_ref[...], TN)

        @pl.when(t == 0)
        def _():
            acc_ref[...] = part

        @pl.when(t > 0)
        def _():
            acc_ref[...] += part

        @pl.when(t == nt - 1)
        def _():
            o_ref[...] = acc_ref[...].astype(BF16)

    return pl.pallas_call(
        body, out_shape=jax.ShapeDtypeStruct((R, N), BF16), grid=(R // tr, nt),
        in_specs=[pl.BlockSpec((tt, tr), lambda r, t: (t, r)),
                  pl.BlockSpec((tt, N), lambda r, t: (t, 0))],
        out_specs=pl.BlockSpec((tr, N), lambda r, t: (r, 0)),
        scratch_shapes=[pltpu.VMEM((tr, N), F32)],
        compiler_params=_params("parallel", "arbitrary"), name=name)(a, b)


def _rms_fwd(x, gain, name):
    T = x.shape[0]
    tm = _pick_tile(T, 512, 16)

    def body(x_ref, g_ref, u_ref):
        xv = x_ref[...]
        rstd = lax.rsqrt(jnp.mean(xv * xv, axis=-1, keepdims=True) + NORM_EPS)
        u_ref[...] = (xv * rstd * g_ref[...]).astype(BF16)

    return pl.pallas_call(
        body, out_shape=jax.ShapeDtypeStruct((T, D_MODEL), BF16), grid=(T // tm,),
        in_specs=[pl.BlockSpec((tm, D_MODEL), lambda i: (i, 0)), pl.BlockSpec((1, D_MODEL), lambda i: (0, 0))],
        out_specs=pl.BlockSpec((tm, D_MODEL), lambda i: (i, 0)),
        compiler_params=_params("parallel"), name=name)(x, gain)


def _rms_bwd(x, gain, du, dres, name):
    T = x.shape[0]
    tm = _pick_tile(T, 512, 16)

    def body(x_ref, g_ref, du_ref, dres_ref, dx_ref, dxb_ref, dg_ref):
        @pl.when(pl.program_id(0) == 0)
        def _():
            dg_ref[...] = jnp.zeros_like(dg_ref)

        xv = x_ref[...]
        rstd = lax.rsqrt(jnp.mean(xv * xv, axis=-1, keepdims=True) + NORM_EPS)
        n = xv * rstd
        du = du_ref[...]
        dg_ref[...] += jnp.sum(du * n, axis=0, keepdims=True)
        dn = du * g_ref[...]
        dx = dres_ref[...] + rstd * (dn - n * jnp.mean(dn * n, axis=-1, keepdims=True))
        dx_ref[...] = dx
        dxb_ref[...] = dx.astype(BF16)

    row = pl.BlockSpec((tm, D_MODEL), lambda i: (i, 0))
    vec = pl.BlockSpec((1, D_MODEL), lambda i: (0, 0))
    return pl.pallas_call(
        body,
        out_shape=(jax.ShapeDtypeStruct((T, D_MODEL), F32), jax.ShapeDtypeStruct((T, D_MODEL), BF16),
                   jax.ShapeDtypeStruct((1, D_MODEL), F32)),
        grid=(T // tm,), in_specs=[row, vec, row, row], out_specs=(row, row, vec),
        compiler_params=_params("arbitrary"), name=name)(x, gain, du, dres)


def _loss_head(h, target, gain, name):
    T = h.shape[0]
    tm = _pick_tile(T, 512, 16)
    inv_f = 1.0 / D_MODEL

    def body(h_ref, t_ref, g_ref, dh_ref, dhb_ref, dg_ref, loss_ref):
        @pl.when(pl.program_id(0) == 0)
        def _():
            dg_ref[...] = jnp.zeros_like(dg_ref)
            loss_ref[...] = jnp.zeros_like(loss_ref)

        hv = h_ref[...]
        g = g_ref[...]
        rstd = lax.rsqrt(jnp.mean(hv * hv, axis=-1, keepdims=True) + NORM_EPS)
        n = hv * rstd
        err = n * g - t_ref[...]
        loss_ref[...] += (0.5 * inv_f) * jnp.sum(err * err, axis=0, keepdims=True)
        dy = err * inv_f
        dg_ref[...] += jnp.sum(dy * n, axis=0, keepdims=True)
        dn = dy * g
        dh = rstd * (dn - n * jnp.mean(dn * n, axis=-1, keepdims=True))
        dh_ref[...] = dh
        dhb_ref[...] = dh.astype(BF16)

    row = pl.BlockSpec((tm, D_MODEL), lambda i: (i, 0))
    vec = pl.BlockSpec((1, D_MODEL), lambda i: (0, 0))
    return pl.pallas_call(
        body,
        out_shape=(jax.ShapeDtypeStruct((T, D_MODEL), F32), jax.ShapeDtypeStruct((T, D_MODEL), BF16),
                   jax.ShapeDtypeStruct((1, D_MODEL), F32), jax.ShapeDtypeStruct((1, D_MODEL), F32)),
        grid=(T // tm,), in_specs=[row, row, vec], out_specs=(row, row, vec, vec),
        compiler_params=_params("arbitrary"), name=name)(h, target, gain)


def _swiglu_fwd(gu, name):
    T = gu.shape[0]
    tm = _pick_tile(T, 256, 16)

    def body(gate_ref, up_ref, a_ref):
        gate = gate_ref[...]
        a_ref[...] = (gate * _sigmoid(gate) * up_ref[...]).astype(BF16)

    return pl.pallas_call(
        body, out_shape=jax.ShapeDtypeStruct((T, D_FF), BF16), grid=(T // tm,),
        in_specs=[pl.BlockSpec((tm, D_FF), lambda i: (i, 0)), pl.BlockSpec((tm, D_FF), lambda i: (i, 1))],
        out_specs=pl.BlockSpec((tm, D_FF), lambda i: (i, 0)),
        compiler_params=_params("parallel"), name=name)(gu, gu)


def _swiglu_bwd(gu, da, name):
    T = gu.shape[0]
    tm = _pick_tile(T, 256, 16)

    def body(gate_ref, up_ref, da_ref, dgu_ref):
        gate = gate_ref[...]
        da_v = da_ref[...]
        sg = _sigmoid(gate)
        dgu_ref[:, 0:D_FF] = (da_v * up_ref[...] * (sg * (1.0 + gate * (1.0 - sg)))).astype(BF16)
        dgu_ref[:, D_FF:2 * D_FF] = (da_v * gate * sg).astype(BF16)

    half = pl.BlockSpec((tm, D_FF), lambda i: (i, 0))
    return pl.pallas_call(
        body, out_shape=jax.ShapeDtypeStruct((T, 2 * D_FF), BF16), grid=(T // tm,),
        in_specs=[half, pl.BlockSpec((tm, D_FF), lambda i: (i, 1)), half],
        out_specs=pl.BlockSpec((tm, 2 * D_FF), lambda i: (i, 0)),
        compiler_params=_params("parallel"), name=name)(gu, gu, da)


def _tri(n, lower):
    r = lax.broadcasted_iota(jnp.int32, (n, n), 0)
    c = lax.broadcasted_iota(jnp.int32, (n, n), 1)
    return (c <= r) if lower else (c >= r)


def _running_sum(x, lower):
    n = x.shape[0]
    tri = _tri(n, lower).astype(F32)
    return lax.dot_general(tri, x, NN, precision=lax.Precision.HIGHEST, preferred_element_type=F32)


def _hgrn_gates(q_raw, f_raw, lb):
    C = q_raw.shape[0]
    sig_f = _sigmoid(f_raw)
    forget = lb + (1.0 - lb) * sig_f
    key = 1.0 - forget
    log_f = jnp.log(forget)
    b = _running_sum(log_f, True)
    first_half = lax.broadcasted_iota(jnp.int32, log_f.shape, 0) < C // 2
    r = jnp.sum(jnp.where(first_half, log_f, 0.0), axis=0, keepdims=True)
    b_last = jnp.sum(log_f, axis=0, keepdims=True)
    e_a = jnp.exp(jnp.minimum(b - r, HGRN_EXP_CLAMP))
    e_b = jnp.exp(jnp.minimum(r - b, HGRN_EXP_CLAMP))
    e_q = jnp.exp(b)
    e_k = jnp.exp(b_last - b)
    sig_q = _sigmoid(q_raw)
    query = q_raw * sig_q
    return dict(sig_f=sig_f, forget=forget, sig_q=sig_q, e_a=e_a, e_b=e_b, e_q=e_q, e_k=e_k,
                e_last=jnp.exp(b_last), q_a=query * e_a, k_b=key * e_b, q_hat=query * e_q, k_til=key * e_k)


def _hgrn_fwd(proj, lb, gain, name):
    T = proj.shape[0]
    C = HGRN_CHUNK
    H, HD = HGRN_HEADS, HGRN_DIM

    def body(q_ref, f_ref, i_ref, g_ref, lb_ref, gain_ref, og_ref, o_ref, st_ref, s_scr):
        @pl.when(pl.program_id(0) == 0)
        def _():
            s_scr[...] = jnp.zeros_like(s_scr)

        st_ref[0] = s_scr[...]
        gt = _hgrn_gates(q_ref[...], f_ref[...], lb_ref[...])
        causal = _tri(C, True)
        gain_v = gain_ref[...]
        for h in range(H):
            sl = slice(h * HD, (h + 1) * HD)
            v = i_ref[:, sl].astype(BF16)
            p = jnp.where(causal, _dot(gt["q_a"][:, sl].astype(BF16), gt["k_b"][:, sl].astype(BF16), NT), 0.0)
            s_t = s_scr[h]
            o = _dot(p.astype(BF16), v, NN) + _dot(gt["q_hat"][:, sl].astype(BF16), s_t.astype(BF16), NT)
            s_scr[h] = gt["e_last"][:, sl] * s_t + _dot(v, gt["k_til"][:, sl].astype(BF16), TN)
            o_ref[:, sl] = o
            rstd = lax.rsqrt(jnp.mean(o * o, axis=-1, keepdims=True) + NORM_EPS)
            g_raw = g_ref[:, sl]
            og_ref[:, sl] = (o * rstd * gain_v * (g_raw * _sigmoid(g_raw))).astype(BF16)

    col = lambda j: pl.BlockSpec((C, D_MODEL), lambda c: (c, j))
    row = pl.BlockSpec((C, D_MODEL), lambda c: (c, 0))
    return pl.pallas_call(
        body,
        out_shape=(jax.ShapeDtypeStruct((T, D_MODEL), BF16), jax.ShapeDtypeStruct((T, D_MODEL), F32),
                   jax.ShapeDtypeStruct((T // C, H, HD, HD), F32)),
        grid=(T // C,),
        in_specs=[col(0), col(1), col(2), col(3), pl.BlockSpec((1, D_MODEL), lambda c: (0, 0)),
                  pl.BlockSpec((1, HD), lambda c: (0, 0))],
        out_specs=(row, row, pl.BlockSpec((1, H, HD, HD), lambda c: (c, 0, 0, 0))),
        scratch_shapes=[pltpu.VMEM((H, HD, HD), F32)],
        compiler_params=_params("arbitrary"), name=name)(proj, proj, proj, proj, lb, gain)


def _hgrn_bwd(proj, o_pre, d_og, states, lb, gain, name):
    T = proj.shape[0]
    C = HGRN_CHUNK
    H, HD = HGRN_HEADS, HGRN_DIM
    NC = T // C

    def body(q_ref, f_ref, i_ref, g_ref, o_ref, dog_ref, st_ref, lb_ref, gain_ref,
             dproj_ref, dlb_ref, dgain_ref, ds_scr, dq_scr, dk_scr, db_scr):
        @pl.when(pl.program_id(0) == 0)
        def _():
            ds_scr[...] = jnp.zeros_like(ds_scr)
            dlb_ref[...] = jnp.zeros_like(dlb_ref)
            dgain_ref[...] = jnp.zeros_like(dgain_ref)

        lbv = lb_ref[...]
        q_raw = q_ref[...]
        gt = _hgrn_gates(q_raw, f_ref[...], lbv)
        causal = _tri(C, True)
        last_row = lax.broadcasted_iota(jnp.int32, (C, HD), 0) == C - 1
        gain_v = gain_ref[...]
        dgain = jnp.zeros((1, HD), F32)
        for h in range(H):
            sl = slice(h * HD, (h + 1) * HD)
            o = o_ref[:, sl]
            rstd = lax.rsqrt(jnp.mean(o * o, axis=-1, keepdims=True) + NORM_EPS)
            n = o * rstd
            g_raw = g_ref[:, sl]
            sg = _sigmoid(g_raw)
            d_out = dog_ref[:, sl]
            dproj_ref[:, 3 * D_MODEL + h * HD:3 * D_MODEL + (h + 1) * HD] = (
                d_out * n * gain_v * (sg * (1.0 + g_raw * (1.0 - sg)))).astype(BF16)
            dy = d_out * (g_raw * sg)
            dgain = dgain + jnp.sum(dy * n, axis=0, keepdims=True)
            dn = dy * gain_v
            do = (rstd * (dn - n * jnp.mean(dn * n, axis=-1, keepdims=True))).astype(BF16)
            q_a, k_b = gt["q_a"][:, sl], gt["k_b"][:, sl]
            q_hat, k_til = gt["q_hat"][:, sl], gt["k_til"][:, sl]
            q_ab, k_bb = q_a.astype(BF16), k_b.astype(BF16)
            v = i_ref[:, sl].astype(BF16)
            s_t = st_ref[0, h]
            ds_t = ds_scr[h]
            ds_b = ds_t.astype(BF16)
            e_last = gt["e_last"][:, sl]
            p = jnp.where(causal, _dot(q_ab, k_bb, NT), 0.0).astype(BF16)
            dp = jnp.where(causal, _dot(do, v, NT), 0.0).astype(BF16)
            dv = _dot(p, do, TN) + _dot(k_til.astype(BF16), ds_b, NT)
            dq_a = _dot(dp, k_bb, NN)
            dk_b = _dot(dp, q_ab, TN)
            dq_hat = _dot(do, s_t.astype(BF16), NN)
            dk_til = _dot(v, ds_b, NN)
            ds_scr[h] = _dot(do, q_hat.astype(BF16), TN) + e_last * ds_t
            db_last = jnp.sum(ds_t * e_last * s_t, axis=0, keepdims=True) + jnp.sum(
                dk_til * k_til, axis=0, keepdims=True)
            dproj_ref[:, 2 * D_MODEL + h * HD:2 * D_MODEL + (h + 1) * HD] = dv.astype(BF16)
            dq_scr[:, sl] = dq_a * gt["e_a"][:, sl] + dq_hat * gt["e_q"][:, sl]
            dk_scr[:, sl] = dk_b * gt["e_b"][:, sl] + dk_til * gt["e_k"][:, sl]
            db = dq_a * q_ab.astype(F32) + dq_hat * q_hat - dk_b * k_bb.astype(F32) - dk_til * k_til
            db_scr[:, sl] = db + jnp.where(last_row, db_last, 0.0)
        dgain_ref[...] += dgain
        dlogf = _running_sum(db_scr[...], False)
        sig_f, forget, sig_q = gt["sig_f"], gt["forget"], gt["sig_q"]
        dforget = dlogf / forget - dk_scr[...]
        dproj_ref[:, D_MODEL:2 * D_MODEL] = (dforget * (1.0 - lbv) * sig_f * (1.0 - sig_f)).astype(BF16)
        dlb_ref[...] += jnp.sum(dforget * (1.0 - sig_f), axis=0, keepdims=True)
        dproj_ref[:, 0:D_MODEL] = (dq_scr[...] * (sig_q * (1.0 + q_raw * (1.0 - sig_q)))).astype(BF16)

    col = lambda j: pl.BlockSpec((C, D_MODEL), lambda c: (NC - 1 - c, j))
    row = pl.BlockSpec((C, D_MODEL), lambda c: (NC - 1 - c, 0))
    return pl.pallas_call(
        body,
        out_shape=(jax.ShapeDtypeStruct((T, 4 * D_MODEL), BF16), jax.ShapeDtypeStruct((1, D_MODEL), F32),
                   jax.ShapeDtypeStruct((1, HD), F32)),
        grid=(NC,),
        in_specs=[col(0), col(1), col(2), col(3), row, row,
                  pl.BlockSpec((1, H, HD, HD), lambda c: (NC - 1 - c, 0, 0, 0)),
                  pl.BlockSpec((1, D_MODEL), lambda c: (0, 0)), pl.BlockSpec((1, HD), lambda c: (0, 0))],
        out_specs=(pl.BlockSpec((C, 4 * D_MODEL), lambda c: (NC - 1 - c, 0)),
                   pl.BlockSpec((1, D_MODEL), lambda c: (0, 0)), pl.BlockSpec((1, HD), lambda c: (0, 0))),
        scratch_shapes=[pltpu.VMEM((H, HD, HD), F32), pltpu.VMEM((C, D_MODEL), F32),
                        pltpu.VMEM((C, D_MODEL), F32), pltpu.VMEM((C, D_MODEL), F32)],
        compiler_params=_params("arbitrary"), name=name)(proj, proj, proj, proj, o_pre, d_og, states, lb, gain)


def _attn_masks():
    r = lax.broadcasted_iota(jnp.int32, (ATTN_BLOCK, ATTN_BLOCK), 0)
    c = lax.broadcasted_iota(jnp.int32, (ATTN_BLOCK, ATTN_BLOCK), 1)
    return c >= r, c <= r


def _attn_fwd(qkv, dilation, name):
    T = qkv.shape[0]
    nb = T // dilation // ATTN_BLOCK
    W = ATTN_GROUP_WIDTH
    scale = ATTN_DIM ** -0.5

    def body(q_ref, kp_ref, kc_ref, vp_ref, vc_ref, o_ref, lse_ref):
        no_prev = jnp.where(pl.program_id(1) > 0, 0.0, NEG_BIG)
        m_prev, m_cur = _attn_masks()
        for h in range(ATTN_GROUP_HEADS):
            sl = slice(h * ATTN_DIM, (h + 1) * ATTN_DIM)
            q = q_ref[:, sl]
            s_p = jnp.where(m_prev, _dot(q, kp_ref[:, sl], NT) * scale + no_prev, NEG_BIG)
            s_c = jnp.where(m_cur, _dot(q, kc_ref[:, sl], NT) * scale, NEG_BIG)
            m = jnp.maximum(jnp.max(s_p, axis=-1, keepdims=True), jnp.max(s_c, axis=-1, keepdims=True))
            p_p = jnp.exp(s_p - m)
            p_c = jnp.exp(s_c - m)
            l = jnp.sum(p_p, axis=-1, keepdims=True) + jnp.sum(p_c, axis=-1, keepdims=True)
            acc = _dot(p_p.astype(BF16), vp_ref[:, sl], NN) + _dot(p_c.astype(BF16), vc_ref[:, sl], NN)
            o_ref[:, sl] = acc / l
            lse_ref[:, sl] = jnp.broadcast_to(m + jnp.log(l), (ATTN_BLOCK, ATTN_DIM))

    blk = lambda col, prev: pl.BlockSpec(
        (ATTN_BLOCK, W), lambda s, n: (s * nb + (jnp.maximum(n - 1, 0) if prev else n), col))
    out = pl.BlockSpec((ATTN_BLOCK, W), lambda s, n: (s * nb + n, 0))
    return pl.pallas_call(
        body, out_shape=(jax.ShapeDtypeStruct((T, W), F32),) * 2, grid=(dilation, nb),
        in_specs=[blk(0, False), blk(1, True), blk(1, False), blk(2, True), blk(2, False)],
        out_specs=(out, out), compiler_params=_params("parallel", "arbitrary"), name=name)(qkv, qkv, qkv, qkv, qkv)


def _attn_bwd(qkv, d_out, lse, delta, cos, sin, dilation, name):
    T = qkv.shape[0]
    nb = T // dilation // ATTN_BLOCK
    W = ATTN_GROUP_WIDTH
    scale = ATTN_DIM ** -0.5

    def unrope(x, cos_v, sin_v):
        return x * cos_v + pltpu.roll(x * sin_v, ATTN_DIM // 2, 1)

    def body(q_ref, kp_ref, kc_ref, vp_ref, vc_ref, do_ref, lse_ref, dl_ref, cos_ref, sin_ref,
             out_ref, dq_scr, dk_scr, dv_scr):
        n = pl.program_id(1)
        cos_v, sin_v = cos_ref[...], sin_ref[...]

        @pl.when(n > 0)
        def _():
            for h in range(ATTN_GROUP_HEADS):
                sl = slice(h * ATTN_DIM, (h + 1) * ATTN_DIM)
                out_ref[:, sl] = unrope(dq_scr[:, sl], cos_v, sin_v).astype(BF16)

        @pl.when(n == nb)
        def _():
            for h in range(ATTN_GROUP_HEADS):
                sl = slice(h * ATTN_DIM, (h + 1) * ATTN_DIM)
                out_ref[:, W + h * ATTN_DIM:W + (h + 1) * ATTN_DIM] = unrope(dk_scr[:, sl], cos_v, sin_v).astype(BF16)
                out_ref[:, 2 * W + h * ATTN_DIM:2 * W + (h + 1) * ATTN_DIM] = dv_scr[:, sl].astype(BF16)

        @pl.when(n == 0)
        def _():
            dk_scr[...] = jnp.zeros_like(dk_scr)
            dv_scr[...] = jnp.zeros_like(dv_scr)

        @pl.when(n < nb)
        def _():
            has_prev = n > 0
            no_prev = jnp.where(has_prev, 0.0, NEG_BIG)
            m_prev, m_cur = _attn_masks()
            for h in range(ATTN_GROUP_HEADS):
                sl = slice(h * ATTN_DIM, (h + 1) * ATTN_DIM)
                q, k_p, k_c, v_p, v_c = q_ref[:, sl], kp_ref[:, sl], kc_ref[:, sl], vp_ref[:, sl], vc_ref[:, sl]
                do = do_ref[:, sl]
                lse_v, dl_v = lse_ref[:, sl], dl_ref[:, sl]
                p_p = jnp.where(m_prev, jnp.exp(_dot(q, k_p, NT) * scale - lse_v + no_prev), 0.0)
                p_c = jnp.where(m_cur, jnp.exp(_dot(q, k_c, NT) * scale - lse_v), 0.0)
                ds_p = (p_p * (_dot(do, v_p, NT) - dl_v) * scale).astype(BF16)
                ds_c = (p_c * (_dot(do, v_c, NT) - dl_v) * scale).astype(BF16)
                dk_prev = dk_scr[:, sl] + _dot(ds_p, q, TN)
                dv_prev = dv_scr[:, sl] + _dot(p_p.astype(BF16), do, TN)

                @pl.when(has_prev)
                def _():
                    out_ref[:, W + h * ATTN_DIM:W + (h + 1) * ATTN_DIM] = unrope(dk_prev, cos_v, sin_v).astype(BF16)
                    out_ref[:, 2 * W + h * ATTN_DIM:2 * W + (h + 1) * ATTN_DIM] = dv_prev.astype(BF16)

                dq_scr[:, sl] = _dot(ds_p, k_p, NN) + _dot(ds_c, k_c, NN)
                dk_scr[:, sl] = _dot(ds_c, q, TN)
                dv_scr[:, sl] = _dot(p_c.astype(BF16), do, TN)

    def cur(n):
        return jnp.minimum(n, nb - 1)

    def late(n):
        return jnp.maximum(n - 1, 0)

    qkv_blk = lambda col, prev: pl.BlockSpec(
        (ATTN_BLOCK, W), lambda s, n: (s * nb + (jnp.maximum(cur(n) - 1, 0) if prev else cur(n)), col))
    row = pl.BlockSpec((ATTN_BLOCK, W), lambda s, n: (s * nb + cur(n), 0))
    tab = pl.BlockSpec((ATTN_BLOCK, ATTN_DIM), lambda s, n: (s * nb + late(n), 0))
    return pl.pallas_call(
        body, out_shape=jax.ShapeDtypeStruct((T, 3 * W), BF16), grid=(dilation, nb + 1),
        in_specs=[qkv_blk(0, False), qkv_blk(1, True), qkv_blk(1, False), qkv_blk(2, True), qkv_blk(2, False),
                  row, row, row, tab, tab],
        out_specs=pl.BlockSpec((ATTN_BLOCK, 3 * W), lambda s, n: (s * nb + late(n), 0)),
        scratch_shapes=[pltpu.VMEM((ATTN_BLOCK, W), F32)] * 3,
        compiler_params=_params("parallel", "arbitrary"), name=name)(
            qkv, qkv, qkv, qkv, qkv, d_out, lse, delta, cos, sin)


def _attn_merge_fwd(outs, lses, name):
    T = outs[0].shape[0]
    W = ATTN_GROUP_WIDTH
    tm = _pick_tile(T, 512, 16)

    def body(o0, o1, o2, l0, l1, l2, oc_ref, lse_ref):
        ls = [l0[...], l1[...], l2[...]]
        m = jnp.maximum(jnp.maximum(ls[0], ls[1]), ls[2])
        tot = m + jnp.log(jnp.exp(ls[0] - m) + jnp.exp(ls[1] - m) + jnp.exp(ls[2] - m))
        lse_ref[...] = tot
        for g, o in enumerate((o0, o1, o2)):
            oc_ref[:, g * W:(g + 1) * W] = (o[...] * jnp.exp(ls[g] - tot)).astype(BF16)

    blk = pl.BlockSpec((tm, W), lambda i: (i, 0))
    return pl.pallas_call(
        body, out_shape=(jax.ShapeDtypeStruct((T, 3 * W), BF16), jax.ShapeDtypeStruct((T, W), F32)),
        grid=(T // tm,), in_specs=[blk] * 6, out_specs=(pl.BlockSpec((tm, 3 * W), lambda i: (i, 0)), blk),
        compiler_params=_params("parallel"), name=name)(*outs, *lses)


def _attn_merge_bwd(d_oc, oc, name):
    T = d_oc.shape[0]
    W = ATTN_GROUP_WIDTH
    tm = _pick_tile(T, 512, 16)

    def body(d_ref, o_ref, delta_ref, db_ref):
        d = d_ref[...]
        db_ref[...] = d.astype(BF16)
        prod = d * o_ref[...].astype(F32)
        for h in range(ATTN_GROUP_HEADS):
            tot = jnp.zeros((tm, 1), F32)
            for g in range(3):
                lo = g * W + h * ATTN_DIM
                tot = tot + jnp.sum(prod[:, lo:lo + ATTN_DIM], axis=-1, keepdims=True)
            delta_ref[:, h * ATTN_DIM:(h + 1) * ATTN_DIM] = jnp.broadcast_to(tot, (tm, ATTN_DIM))

    wide = pl.BlockSpec((tm, 3 * W), lambda i: (i, 0))
    return pl.pallas_call(
        body, out_shape=(jax.ShapeDtypeStruct((T, W), F32), jax.ShapeDtypeStruct((T, 3 * W), BF16)),
        grid=(T // tm,), in_specs=[wide, wide], out_specs=(pl.BlockSpec((tm, W), lambda i: (i, 0)), wide),
        compiler_params=_params("parallel"), name=name)(d_oc, oc)


def _to_residues(x, d):
    if d == 1:
        return x
    T, C = x.shape
    return x.reshape(T // d, d, C).transpose(1, 0, 2).reshape(T, C)


def _from_residues(x, d):
    if d == 1:
        return x
    T, C = x.shape
    return x.reshape(d, T // d, C).transpose(1, 0, 2).reshape(T, C)


def _rope_tables(T):
    inv_freq = 1.0 / (ROPE_THETA ** (jnp.arange(0, ATTN_DIM, 2, dtype=F32) / ATTN_DIM))
    ang = jnp.arange(T, dtype=F32)[:, None] * inv_freq[None, :]
    cos, sin = jnp.cos(ang), jnp.sin(ang)
    return jnp.concatenate([cos, cos], axis=1), jnp.concatenate([-sin, sin], axis=1)


def _local_step(x, target, norm_mix, norm_ffn, lb, out_gain, final_gain, w):
    T = x.shape[0]
    ident = lambda j: j
    g_mix = [norm_mix[0:1], norm_mix[1:2]]
    g_ffn = [norm_ffn[0:1], norm_ffn[1:2]]

    def ffn_fwd(h, layer):
        n = _rms_fwd(h, g_ffn[layer], f"ffn{layer}_norm")
        gu = _mm_nt(n, w[f"ffn_in{layer}"], n_out=2 * D_FF, tn=512, w_block=ident, out_dtype=F32,
                    name=f"ffn{layer}_in")
        a = _swiglu_fwd(gu, f"ffn{layer}_act")
        out = _mm_nn(a, w[f"ffn_down{layer}"], h, tk=1408, w_block=ident, name=f"ffn{layer}_down")
        return out, (n, gu, a)

    def ffn_bwd(h, saved, dh, dhb, layer, grads):
        n, gu, a = saved
        da = _mm_nt(dhb, w[f"ffn_down{layer}"], n_out=D_FF, tn=1408, w_block=ident, out_dtype=F32,
                    name=f"ffn{layer}_down_dx")
        dgu = _swiglu_bwd(gu, da, f"ffn{layer}_act_bwd")
        grads[f"ffn_down{layer}"] = _mm_tn(a, dhb, tr=1408, name=f"ffn{layer}_down_dw")
        grads[f"ffn_in{layer}"] = _mm_tn(dgu, n, tr=512, name=f"ffn{layer}_in_dw")
        dn = _mm_nn(dgu, w[f"ffn_in{layer}"], None, tk=1408, w_block=ident, name=f"ffn{layer}_in_dx")
        return _rms_bwd(h, g_ffn[layer], dn, dh, f"ffn{layer}_norm_bwd")

    u0 = _rms_fwd(x, g_mix[0], "hgrn_norm")
    proj = _mm_nt(u0, w["hgrn_in"], n_out=4 * D_MODEL, tn=512, w_block=ident, out_dtype=F32, name="hgrn_in")
    og, o_pre, states = _hgrn_fwd(proj, lb, out_gain, "hgrn_fwd")
    h1 = _mm_nn(og, w["hgrn_out"], x, tk=1024, w_block=ident, name="hgrn_out")
    h2, ffn0 = ffn_fwd(h1, 0)

    u1 = _rms_fwd(h2, g_mix[1], "attn_norm")
    cos, sin = _rope_tables(T)
    u1_g, qkv_g, cos_g, sin_g, outs, lses = [], [], [], [], [], []
    for g, d in enumerate(ATTN_DILATIONS):
        u1_g.append(_to_residues(u1, d))
        cos_g.append(_to_residues(cos, d))
        sin_g.append(_to_residues(sin, d))
        qkv_g.append(_mm_nt(u1_g[g], w["qkv"], n_out=ATTN_WIDTH, tn=ATTN_GROUP_WIDTH,
                            w_block=functools.partial(lambda j, g: 3 * j + g, g=g), out_dtype=BF16,
                            name=f"attn_qkv{g}", rope=(cos_g[g], sin_g[g], 2)))
        o_g, lse_g = _attn_fwd(qkv_g[g], d, f"attn_fwd{g}")
        outs.append(_from_residues(o_g, d))
        lses.append(_from_residues(lse_g, d))
    oc, lse_all = _attn_merge_fwd(outs, lses, "attn_merge")
    h3 = _mm_nn(oc, w["attn_out"], h2, tk=512, w_block=ident, name="attn_out")
    h4, ffn1 = ffn_fwd(h3, 1)

    grads = {}
    dh4, dh4b, d_final, loss_part = _loss_head(h4, target, final_gain, "loss_head")
    dh3, dh3b, d_ffn1 = ffn_bwd(h3, ffn1, dh4, dh4b, 1, grads)

    d_oc = _mm_nt(dh3b, w["attn_out"], n_out=ATTN_WIDTH, tn=512, w_block=ident, out_dtype=F32, name="attn_out_dx")
    grads["attn_out"] = _mm_tn(oc, dh3b, tr=512, name="attn_out_dw")
    delta, d_ocb = _attn_merge_bwd(d_oc, oc, "attn_merge_bwd")
    du1 = None
    qkv_pieces = []
    for g, d in enumerate(ATTN_DILATIONS):
        W = ATTN_GROUP_WIDTH
        dqkv = _attn_bwd(qkv_g[g], _to_residues(d_ocb[:, g * W:(g + 1) * W], d), _to_residues(lse_all, d),
                         _to_residues(delta, d), cos_g[g], sin_g[g], d, f"attn_bwd{g}")
        qkv_pieces.append(_mm_tn(dqkv, u1_g[g], tr=512, name=f"attn_qkv_dw{g}"))
        du1_g = _mm_nn(dqkv, w["qkv"], None, tk=W, w_block=functools.partial(lambda k, g: 3 * k + g, g=g),
                       name=f"attn_qkv_dx{g}")
        du1_g = _from_residues(du1_g, d)
        du1 = du1_g if du1 is None else du1 + du1_g
    grads["qkv"] = jnp.stack([p.reshape(3, ATTN_GROUP_WIDTH, D_MODEL) for p in qkv_pieces], axis=1).reshape(
        3 * ATTN_WIDTH, D_MODEL)
    dh2, dh2b, d_mix1 = _rms_bwd(h2, g_mix[1], du1, dh3, "attn_norm_bwd")

    dh1, dh1b, d_ffn0 = ffn_bwd(h1, ffn0, dh2, dh2b, 0, grads)

    d_og = _mm_nt(dh1b, w["hgrn_out"], n_out=D_MODEL, tn=512, w_block=ident, out_dtype=F32, name="hgrn_out_dx")
    grads["hgrn_out"] = _mm_tn(og, dh1b, tr=512, name="hgrn_out_dw")
    dproj, d_lb, d_out_gain = _hgrn_bwd(proj, o_pre, d_og, states, lb, out_gain, "hgrn_bwd")
    grads["hgrn_in"] = _mm_tn(dproj, u0, tr=512, name="hgrn_in_dw")
    du0 = _mm_nn(dproj, w["hgrn_in"], None, tk=1024, w_block=ident, name="hgrn_in_dx")
    dx, _, d_mix0 = _rms_bwd(x, g_mix[0], du0, dh1, "hgrn_norm_bwd")

    small = dict(norm_mix0=d_mix0, norm_mix1=d_mix1, norm_ffn0=d_ffn0, norm_ffn1=d_ffn1, lb=d_lb,
                 out_gain=d_out_gain, final=d_final, loss=loss_part)
    return dx, grads, small


WEIGHT_NAMES = ("hgrn_in", "hgrn_out", "qkv", "attn_out", "ffn_in0", "ffn_in1", "ffn_down0", "ffn_down1")
MESH_IDS = pl.DeviceIdType.MESH
HBM_SPEC = pl.BlockSpec(memory_space=pl.ANY)


def _all_gather(shards, name):
    nw = len(shards)

    def body(*refs):
        ins, outs = refs[:nw], refs[nw:2 * nw]
        send_sems, recv_sems, local_sems = refs[2 * nw:]
        x, y, c = lax.axis_index("x"), lax.axis_index("y"), lax.axis_index("c")
        me, sibling = (x, y, c), (x, y, 1 - c)
        chips = [(1 - x, y), (x, 1 - y), (1 - x, 1 - y)]

        def block(w, px, py, pc):
            return outs[w].at[4 * px + 2 * py + pc]

        def copy(w, k, owner, to, src=None):
            return pltpu.make_async_remote_copy(
                src_ref=block(w, *owner) if src is None else src, dst_ref=block(w, *owner),
                send_sem=send_sems.at[w, k], recv_sem=recv_sems.at[w, k], device_id=to, device_id_type=MESH_IDS)

        mine = [pltpu.make_async_copy(ins[w], block(w, *me), local_sems.at[w]) for w in range(nw)]
        for cp in mine:
            cp.start()
        first = []
        for w in range(nw):
            first.append(copy(w, 0, me, sibling, src=ins[w]))
            first += [copy(w, 1 + j, me, (*chip, c), src=ins[w]) for j, chip in enumerate(chips)]
        for cp in first:
            cp.start()
        passed = []
        for j, chip in enumerate(chips):
            for w in range(nw):
                copy(w, 1 + j, (*chip, c), me).wait_recv()
                fwd = copy(w, 4 + j, (*chip, c), sibling)
                fwd.start()
                passed.append(fwd)
        for w in range(nw):
            copy(w, 0, sibling, me).wait_recv()
            for j, chip in enumerate(chips):
                copy(w, 4 + j, (*chip, 1 - c), me).wait_recv()
        for cp in first + passed:
            cp.wait_send()
        for cp in mine:
            cp.wait()

    return pl.pallas_call(
        body, out_shape=[jax.ShapeDtypeStruct((N_DEV,) + s.shape, s.dtype) for s in shards],
        in_specs=[HBM_SPEC] * nw, out_specs=[HBM_SPEC] * nw,
        scratch_shapes=[pltpu.SemaphoreType.DMA((nw, 7)), pltpu.SemaphoreType.DMA((nw, 7)),
                        pltpu.SemaphoreType.DMA((nw,))],
        name=name)(*shards)


def _exchange(parts, small, name):
    nw = len(parts)
    offsets = [(dx, dy, dc) for dx in (0, 1) for dy in (0, 1) for dc in (0, 1)][1:]

    def body(*refs):
        ins, small_in = refs[:nw], refs[nw]
        outs, small_out = refs[nw + 1:2 * nw + 1], refs[2 * nw + 1]
        send_sems, recv_sems, local_sems = refs[2 * nw + 2:]
        x, y, c = lax.axis_index("x"), lax.axis_index("y"), lax.axis_index("c")
        me = 4 * x + 2 * y + c

        def peer(k):
            dx, dy, dc = offsets[k]
            return (1 - x if dx else x, 1 - y if dy else y, 1 - c if dc else c)

        def copy(w, k):
            px, py, pc = peer(k)
            pid = 4 * px + 2 * py + pc
            src, dst = (small_in, small_out.at[me]) if w == nw else (ins[w].at[pid], outs[w].at[me])
            return pltpu.make_async_remote_copy(
                src_ref=src, dst_ref=dst, send_sem=send_sems.at[w, k], recv_sem=recv_sems.at[w, k],
                device_id=(px, py, pc), device_id_type=MESH_IDS)

        mine = [pltpu.make_async_copy(ins[w].at[me], outs[w].at[me], local_sems.at[w]) for w in range(nw)]
        mine.append(pltpu.make_async_copy(small_in, small_out.at[me], local_sems.at[nw]))
        for cp in mine:
            cp.start()
        sends = [copy(w, k) for k in range(7) for w in range(nw + 1)]
        for cp in sends:
            cp.start()
        for cp in sends:
            cp.wait_recv()
        for cp in sends:
            cp.wait_send()
        for cp in mine:
            cp.wait()

    out_shape = [jax.ShapeDtypeStruct(p.shape, p.dtype) for p in parts]
    out_shape.append(jax.ShapeDtypeStruct((N_DEV,) + small.shape, small.dtype))
    res = pl.pallas_call(
        body, out_shape=out_shape, in_specs=[HBM_SPEC] * (nw + 1), out_specs=[HBM_SPEC] * (nw + 1),
        scratch_shapes=[pltpu.SemaphoreType.DMA((nw + 1, 7)), pltpu.SemaphoreType.DMA((nw + 1, 7)),
                        pltpu.SemaphoreType.DMA((nw + 1,))],
        name=name)(*parts, small)
    return res[:nw], res[nw]


def _sum_blocks(recv, name):
    rows = recv.shape[1]
    tr = _pick_tile(rows, 256, 16)

    def body(r_ref, g_ref):
        acc = r_ref[0].astype(F32)
        for j in range(1, N_DEV):
            acc = acc + r_ref[j].astype(F32)
        g_ref[...] = acc

    return pl.pallas_call(
        body, out_shape=jax.ShapeDtypeStruct((rows, D_MODEL), F32), grid=(rows // tr,),
        in_specs=[pl.BlockSpec((N_DEV, tr, D_MODEL), lambda i: (0, i, 0))],
        out_specs=pl.BlockSpec((tr, D_MODEL), lambda i: (i, 0)),
        compiler_params=_params("parallel"), name=name)(recv)


def _adamw_math(w, g, m, v):
    m_new = ADAM_B1 * m + (1.0 - ADAM_B1) * g
    v_new = ADAM_B2 * v + (1.0 - ADAM_B2) * (g * g)
    m_hat = m_new / (1.0 - ADAM_B1 ** ADAM_STEP)
    v_hat = v_new / (1.0 - ADAM_B2 ** ADAM_STEP)
    delta = -ADAM_LR * (m_hat / (jnp.sqrt(v_hat) + ADAM_EPS) + ADAM_WD * w)
    return delta, m_new, v_new


def _adamw(w, g, m, v, name):
    rows, cols = w.shape
    tr = _pick_tile(rows, 256, 8)

    def body(w_ref, g_ref, m_ref, v_ref, d_ref, mo_ref, vo_ref):
        d_ref[...], mo_ref[...], vo_ref[...] = _adamw_math(w_ref[...], g_ref[...], m_ref[...], v_ref[...])

    blk = pl.BlockSpec((tr, cols), lambda i: (i, 0))
    return pl.pallas_call(
        body, out_shape=(jax.ShapeDtypeStruct((rows, cols), F32),) * 3, grid=(rows // tr,),
        in_specs=[blk] * 4, out_specs=(blk,) * 3, compiler_params=_params("parallel"), name=name)(w, g, m, v)


ROW_MIX, ROW_FFN, ROW_LB, ROW_OUT_GAIN, ROW_FINAL = 0, 2, 4, 7, 8
PART_MIX, PART_FFN, PART_LB, PART_OUT_GAIN, PART_FINAL, PART_LOSS = 0, 2, 4, 5, 6, 7


def _small_update(parts_all, w, m, v, name):
    def body(p_ref, w_ref, m_ref, v_ref, g_ref, d_ref, mo_ref, vo_ref, loss_ref):
        def total(row, n=1):
            tot = p_ref[0, row:row + n, :]
            for j in range(1, N_DEV):
                tot = tot + p_ref[j, row:row + n, :]
            return tot

        logits = [w_ref[ROW_LB + i:ROW_LB + i + 1, :] for i in range(3)]
        mx = jnp.maximum(jnp.maximum(logits[0], logits[1]), logits[2])
        ex = [jnp.exp(l - mx) for l in logits]
        den = ex[0] + ex[1] + ex[2]
        prob = [e / den for e in ex]
        d_lb = total(PART_LB)
        g_ref[...] = jnp.zeros_like(g_ref)
        g_ref[ROW_MIX:ROW_MIX + 2, :] = total(PART_MIX, 2)
        g_ref[ROW_FFN:ROW_FFN + 2, :] = total(PART_FFN, 2)
        for i in range(3):
            g_ref[ROW_LB + i:ROW_LB + i + 1, :] = prob[i] * ((d_lb if i == 0 else 0.0) - prob[0] * d_lb)
        g_ref[ROW_OUT_GAIN:ROW_OUT_GAIN + 1, :] = total(PART_OUT_GAIN)
        g_ref[ROW_FINAL:ROW_FINAL + 1, :] = total(PART_FINAL)
        d_ref[...], mo_ref[...], vo_ref[...] = _adamw_math(w_ref[...], g_ref[...], m_ref[...], v_ref[...])
        loss_ref[...] = jnp.sum(total(PART_LOSS), axis=-1, keepdims=True)

    packed = jax.ShapeDtypeStruct((16, D_MODEL), F32)
    return pl.pallas_call(
        body, out_shape=(packed, packed, packed, packed, jax.ShapeDtypeStruct((1, 1), F32)),
        compiler_params=pltpu.CompilerParams(vmem_limit_bytes=VMEM_LIMIT), name=name)(parts_all, w, m, v)


def _pack_small(norm_mix, norm_ffn, lb_logits, out_gain, final):
    pad = jnp.zeros((1, D_MODEL - HGRN_DIM), F32)
    return jnp.concatenate([norm_mix, norm_ffn, lb_logits, jnp.concatenate([out_gain, pad], axis=1),
                            final.reshape(1, D_MODEL), jnp.zeros((16 - ROW_FINAL - 1, D_MODEL), F32)], axis=0)


def _unpack_small(p):
    return (p[ROW_MIX:ROW_MIX + 2], p[ROW_FFN:ROW_FFN + 2], p[ROW_LB:ROW_LB + 3],
            p[ROW_OUT_GAIN:ROW_OUT_GAIN + 1, :HGRN_DIM], p[ROW_FINAL])


def _lower_bound(lb_logits, name):
    def body(l_ref, o_ref):
        logits = [l_ref[i:i + 1, :] for i in range(3)]
        mx = jnp.maximum(jnp.maximum(logits[0], logits[1]), logits[2])
        ex = [jnp.exp(l - mx) for l in logits]
        o_ref[...] = ex[0] / (ex[0] + ex[1] + ex[2])

    return pl.pallas_call(body, out_shape=jax.ShapeDtypeStruct((1, D_MODEL), F32), name=name)(lb_logits)


def kernel(x, norm_mix, norm_ffn, hgrn_w_in, hgrn_lb_logits, hgrn_out_norm, hgrn_w_out, attn_w_qkv, attn_w_out, ffn_w_in, ffn_w_down, final_norm, loss_target, m_norm_mix, m_norm_ffn, m_hgrn_w_in, m_hgrn_lb_logits, m_hgrn_out_norm, m_hgrn_w_out, m_attn_w_qkv, m_attn_w_out, m_ffn_w_in, m_ffn_w_down, m_final_norm, v_norm_mix, v_norm_ffn, v_hgrn_w_in, v_hgrn_lb_logits, v_hgrn_out_norm, v_hgrn_w_out, v_attn_w_qkv, v_attn_w_out, v_ffn_w_in, v_ffn_w_down, v_final_norm):
    col_sharded = {"hgrn_in": hgrn_w_in[0], "qkv": attn_w_qkv[0], "ffn_in0": ffn_w_in[0], "ffn_in1": ffn_w_in[1]}
    row_sharded = {"hgrn_out": hgrn_w_out[0], "attn_out": attn_w_out[0], "ffn_down0": ffn_w_down[0],
                   "ffn_down1": ffn_w_down[1]}
    shards = [(col_sharded[n].T if n in col_sharded else row_sharded[n]).astype(BF16) for n in WEIGHT_NAMES]
    gathered = _all_gather(shards, "weights_all_gather")
    w = {n: g.reshape(-1, D_MODEL) for n, g in zip(WEIGHT_NAMES, gathered)}

    lb = _lower_bound(hgrn_lb_logits, "hgrn_lower_bound")
    grad_x, grads, small = _local_step(x[0], loss_target[0], norm_mix, norm_ffn, lb, hgrn_out_norm, final_norm.reshape(1, D_MODEL), w)

    pad = jnp.zeros((1, D_MODEL - HGRN_DIM), F32)
    small_part = jnp.concatenate(
        [small["norm_mix0"], small["norm_mix1"], small["norm_ffn0"], small["norm_ffn1"], small["lb"],
         jnp.concatenate([small["out_gain"], pad], axis=1), small["final"], small["loss"]], axis=0)
    parts = [grads[n].reshape(N_DEV, -1, D_MODEL) for n in WEIGHT_NAMES]
    received, small_all = _exchange(parts, small_part, "grads_exchange")

    masters = {"hgrn_in": (hgrn_w_in[0], m_hgrn_w_in[0], v_hgrn_w_in[0]),
               "hgrn_out": (hgrn_w_out[0], m_hgrn_w_out[0], v_hgrn_w_out[0]),
               "qkv": (attn_w_qkv[0], m_attn_w_qkv[0], v_attn_w_qkv[0]),
               "attn_out": (attn_w_out[0], m_attn_w_out[0], v_attn_w_out[0]),
               "ffn_in0": (ffn_w_in[0], m_ffn_w_in[0], v_ffn_w_in[0]),
               "ffn_in1": (ffn_w_in[1], m_ffn_w_in[1], v_ffn_w_in[1]),
               "ffn_down0": (ffn_w_down[0], m_ffn_w_down[0], v_ffn_w_down[0]),
               "ffn_down1": (ffn_w_down[1], m_ffn_w_down[1], v_ffn_w_down[1])}
    res = {}
    for n, recv in zip(WEIGHT_NAMES, received):
        g = _sum_blocks(recv, f"{n}_grad_sum")
        if n in col_sharded:
            g = g.T
        wv, mv, vv = masters[n]
        res[n] = (g,) + tuple(_adamw(wv, g, mv, vv, f"{n}_adamw"))

    def single(n):
        return [t[None] for t in res[n]]

    def pair(n):
        return [jnp.stack([a, b]) for a, b in zip(res[n + "0"], res[n + "1"])]

    big = dict(hgrn_w_in=single("hgrn_in"), hgrn_w_out=single("hgrn_out"), attn_w_qkv=single("qkv"),
               attn_w_out=single("attn_out"), ffn_w_in=pair("ffn_in"), ffn_w_down=pair("ffn_down"))

    w_small = _pack_small(norm_mix, norm_ffn, hgrn_lb_logits, hgrn_out_norm, final_norm)
    m_small = _pack_small(m_norm_mix, m_norm_ffn, m_hgrn_lb_logits, m_hgrn_out_norm, m_final_norm)
    v_small = _pack_small(v_norm_mix, v_norm_ffn, v_hgrn_lb_logits, v_hgrn_out_norm, v_final_norm)
    g_s, d_s, m_s, v_s, loss = _small_update(small_all, w_small, m_small, v_small, "small_update")
    small_out = [_unpack_small(t) for t in (g_s, d_s, m_s, v_s)]

    def group(i):
        s = small_out[i]
        return (s[0], s[1], big["hgrn_w_in"][i], s[2], s[3], big["hgrn_w_out"][i], big["attn_w_qkv"][i],
                big["attn_w_out"][i], big["ffn_w_in"][i], big["ffn_w_down"][i], s[4])

    return (loss.reshape(()), grad_x[None], *group(0), *group(1), *group(2), *group(3))
```

```python
import functools

import jax
import jax.numpy as jnp
from jax import lax
from jax.experimental import pallas as pl
from jax.experimental.pallas import tpu as pltpu

F32 = jnp.float32
BF16 = jnp.bfloat16

D_MODEL = 1024
N_DEV = 8
NORM_EPS = 1e-6

HGRN_HEADS = 8
HGRN_DIM = 128
HGRN_CHUNK = 64
HGRN_EXP_CLAMP = 60.0

ATTN_DIM = 128
ATTN_BLOCK = 128
ATTN_GROUP_HEADS = 4
ATTN_GROUP_WIDTH = ATTN_GROUP_HEADS * ATTN_DIM
ATTN_DILATIONS = (1, 4, 16)
ATTN_WIDTH = 3 * ATTN_GROUP_WIDTH
ROPE_THETA = 10000.0
NEG_BIG = -1e30

D_FF = 2816

ADAM_LR = 0.001
ADAM_B1 = 0.9
ADAM_B2 = 0.999
ADAM_EPS = 1e-08
ADAM_WD = 0.01
ADAM_STEP = 10

VMEM_LIMIT = 48 * 1024 * 1024

NT = (((1,), (1,)), ((), ()))
NN = (((1,), (0,)), ((), ()))
TN = (((0,), (0,)), ((), ()))


def _dot(a, b, dims):
    return lax.dot_general(a, b, dims, preferred_element_type=F32)


def _params(*sem):
    return pltpu.CompilerParams(dimension_semantics=sem, vmem_limit_bytes=VMEM_LIMIT)


def _pick_tile(n, cap, mult):
    best = None
    for t in range(mult, min(n, cap) + 1, mult):
        if n % t == 0:
            best = t
    assert best is not None, (n, cap, mult)
    return best


def _sigmoid(x):
    return 1.0 / (1.0 + jnp.exp(-x))


def _mm_nt(a, w, *, n_out, tn, w_block, out_dtype, name, rope=None):
    M, K = a.shape
    tm = _pick_tile(M, 1024, 16)
    nj = n_out // tn

    def body(*refs):
        if rope is None:
            a_ref, w_ref, o_ref = refs
        else:
            a_ref, w_ref, cos_ref, sin_ref, o_ref = refs
        acc = _dot(a_ref[...], w_ref[...], NT)
        if rope is None:
            o_ref[...] = acc.astype(out_dtype)
        else:
            j = pl.program_id(1)

            @pl.when(j < rope[2])
            def _():
                cos = cos_ref[...]
                sin = sin_ref[...]
                for h in range(tn // ATTN_DIM):
                    xh = acc[:, h * ATTN_DIM:(h + 1) * ATTN_DIM]
                    rot = pltpu.roll(xh, ATTN_DIM // 2, 1)
                    o_ref[:, h * ATTN_DIM:(h + 1) * ATTN_DIM] = (xh * cos + rot * sin).astype(out_dtype)

            @pl.when(j >= rope[2])
            def _():
                o_ref[...] = acc.astype(out_dtype)

    in_specs = [pl.BlockSpec((tm, K), lambda i, j: (i, 0)),
                pl.BlockSpec((tn, K), lambda i, j: (w_block(j), 0))]
    args = [a, w]
    if rope is not None:
        in_specs += [pl.BlockSpec((tm, ATTN_DIM), lambda i, j: (i, 0))] * 2
        args += [rope[0], rope[1]]
    return pl.pallas_call(
        body, out_shape=jax.ShapeDtypeStruct((M, n_out), out_dtype), grid=(M // tm, nj),
        in_specs=in_specs, out_specs=pl.BlockSpec((tm, tn), lambda i, j: (i, j)),
        compiler_params=_params("parallel", "arbitrary"), name=name)(*args)


def _mm_nn(a, w, resid, *, tk, w_block, name):
    M, R = a.shape
    N = w.shape[1]
    tm = _pick_tile(M, 1024, 16)
    nk = R // tk

    def body(*refs):
        if resid is None:
            a_ref, w_ref, o_ref, acc_ref = refs
        else:
            a_ref, w_ref, r_ref, o_ref, acc_ref = refs
        k = pl.program_id(1)
        part = _dot(a_ref[...], w_ref[...], NN)

        @pl.when(k == 0)
        def _():
            acc_ref[...] = part if resid is None else part + r_ref[...]

        @pl.when(k > 0)
        def _():
            acc_ref[...] += part

        @pl.when(k == nk - 1)
        def _():
            o_ref[...] = acc_ref[...]

    in_specs = [pl.BlockSpec((tm, tk), lambda i, k: (i, k)),
                pl.BlockSpec((tk, N), lambda i, k: (w_block(k), 0))]
    args = [a, w]
    if resid is not None:
        in_specs.append(pl.BlockSpec((tm, N), lambda i, k: (i, 0)))
        args.append(resid)
    return pl.pallas_call(
        body, out_shape=jax.ShapeDtypeStruct((M, N), F32), grid=(M // tm, nk),
        in_specs=in_specs, out_specs=pl.BlockSpec((tm, N), lambda i, k: (i, 0)),
        scratch_shapes=[pltpu.VMEM((tm, N), F32)],
        compiler_params=_params("parallel", "arbitrary"), name=name)(*args)


def _mm_tn(a, b, *, tr, name):
    T, R = a.shape
    N = b.shape[1]
    tt = _pick_tile(T, 1024, 16)
    nt = T // tt

    def body(a_ref, b_ref, o_ref, acc_ref):
        t = pl.program_id(1)
        part = _dot(a_ref[...], b_ref[...], TN)

        @pl.when(t == 0)
        def _():
            acc_ref[...] = part

        @pl.when(t > 0)
        def _():
            acc_ref[...] += part

        @pl.when(t == nt - 1)
        def _():
            o_ref[...] = acc_ref[...].astype(BF16)

    return pl.pallas_call(
        body, out_shape=jax.ShapeDtypeStruct((R, N), BF16), grid=(R // tr, nt),
        in_specs=[pl.BlockSpec((tt, tr), lambda r, t: (t, r)),
                  pl.BlockSpec((tt, N), lambda r, t: (t, 0))],
        out_specs=pl.BlockSpec((tr, N), lambda r, t: (r, 0)),
        scratch_shapes=[pltpu.VMEM((tr, N), F32)],
        compiler_params=_params("parallel", "arbitrary"), name=name)(a, b)


def _rms_fwd(x, gain, name):
    T = x.shape[0]
    tm = _pick_tile(T, 512, 16)

    def body(x_ref, g_ref, u_ref):
        xv = x_ref[...]
        rstd = lax.rsqrt(jnp.mean(xv * xv, axis=-1, keepdims=True) + NORM_EPS)
        u_ref[...] = (xv * rstd * g_ref[...]).astype(BF16)

    return pl.pallas_call(
        body, out_shape=jax.ShapeDtypeStruct((T, D_MODEL), BF16), grid=(T // tm,),
        in_specs=[pl.BlockSpec((tm, D_MODEL), lambda i: (i, 0)), pl.BlockSpec((1, D_MODEL), lambda i: (0, 0))],
        out_specs=pl.BlockSpec((tm, D_MODEL), lambda i: (i, 0)),
        compiler_params=_params("parallel"), name=name)(x, gain)


def _rms_bwd(x, gain, du, dres, name):
    T = x.shape[0]
    tm = _pick_tile(T, 512, 16)

    def body(x_ref, g_ref, du_ref, dres_ref, dx_ref, dxb_ref, dg_ref):
        @pl.when(pl.program_id(0) == 0)
        def _():
            dg_ref[...] = jnp.zeros_like(dg_ref)

        xv = x_ref[...]
        rstd = lax.rsqrt(jnp.mean(xv * xv, axis=-1, keepdims=True) + NORM_EPS)
        n = xv * rstd
        du = du_ref[...]
        dg_ref[...] += jnp.sum(du * n, axis=0, keepdims=True)
        dn = du * g_ref[...]
        dx = dres_ref[...] + rstd * (dn - n * jnp.mean(dn * n, axis=-1, keepdims=True))
        dx_ref[...] = dx
        dxb_ref[...] = dx.astype(BF16)

    row = pl.BlockSpec((tm, D_MODEL), lambda i: (i, 0))
    vec = pl.BlockSpec((1, D_MODEL), lambda i: (0, 0))
    return pl.pallas_call(
        body,
        out_shape=(jax.ShapeDtypeStruct((T, D_MODEL), F32), jax.ShapeDtypeStruct((T, D_MODEL), BF16),
                   jax.ShapeDtypeStruct((1, D_MODEL), F32)),
        grid=(T // tm,), in_specs=[row, vec, row, row], out_specs=(row, row, vec),
        compiler_params=_params("arbitrary"), name=name)(x, gain, du, dres)


def _loss_head(h, target, gain, name):
    T = h.shape[0]
    tm = _pick_tile(T, 512, 16)
    inv_f = 1.0 / D_MODEL

    def body(h_ref, t_ref, g_ref, dh_ref, dhb_ref, dg_ref, loss_ref):
        @pl.when(pl.program_id(0) == 0)
        def _():
            dg_ref[...] = jnp.zeros_like(dg_ref)
            loss_ref[...] = jnp.zeros_like(loss_ref)

        hv = h_ref[...]
        g = g_ref[...]
        rstd = lax.rsqrt(jnp.mean(hv * hv, axis=-1, keepdims=True) + NORM_EPS)
        n = hv * rstd
        err = n * g - t_ref[...]
        loss_ref[...] += (0.5 * inv_f) * jnp.sum(err * err, axis=0, keepdims=True)
        dy = err * inv_f
        dg_ref[...] += jnp.sum(dy * n, axis=0, keepdims=True)
        dn = dy * g
        dh = rstd * (dn - n * jnp.mean(dn * n, axis=-1, keepdims=True))
        dh_ref[...] = dh
        dhb_ref[...] = dh.astype(BF16)

    row = pl.BlockSpec((tm, D_MODEL), lambda i: (i, 0))
    vec = pl.BlockSpec((1, D_MODEL), lambda i: (0, 0))
    return pl.pallas_call(
        body,
        out_shape=(jax.ShapeDtypeStruct((T, D_MODEL), F32), jax.ShapeDtypeStruct((T, D_MODEL), BF16),
                   jax.ShapeDtypeStruct((1, D_MODEL), F32), jax.ShapeDtypeStruct((1, D_MODEL), F32)),
        grid=(T // tm,), in_specs=[row, row, vec], out_specs=(row, row, vec, vec),
        compiler_params=_params("arbitrary"), name=name)(h, target, gain)


def _swiglu_fwd(gu, name):
    T = gu.shape[0]
    tm = _pick_tile(T, 256, 16)

    def body(gate_ref, up_ref, a_ref):
        gate = gate_ref[...]
        a_ref[...] = (gate * _sigmoid(gate) * up_ref[...]).astype(BF16)

    return pl.pallas_call(
        body, out_shape=jax.ShapeDtypeStruct((T, D_FF), BF16), grid=(T // tm,),
        in_specs=[pl.BlockSpec((tm, D_FF), lambda i: (i, 0)), pl.BlockSpec((tm, D_FF), lambda i: (i, 1))],
        out_specs=pl.BlockSpec((tm, D_FF), lambda i: (i, 0)),
        compiler_params=_params("parallel"), name=name)(gu, gu)


def _swiglu_bwd(gu, da, name):
    T = gu.shape[0]
    tm = _pick_tile(T, 256, 16)

    def body(gate_ref, up_ref, da_ref, dgu_ref):
        gate = gate_ref[...]
        da_v = da_ref[...]
        sg = _sigmoid(gate)
        dgu_ref[:, 0:D_FF] = (da_v * up_ref[...] * (sg * (1.0 + gate * (1.0 - sg)))).astype(BF16)
        dgu_ref[:, D_FF:2 * D_FF] = (da_v * gate * sg).astype(BF16)

    half = pl.BlockSpec((tm, D_FF), lambda i: (i, 0))
    return pl.pallas_call(
        body, out_shape=jax.ShapeDtypeStruct((T, 2 * D_FF), BF16), grid=(T // tm,),
        in_specs=[half, pl.BlockSpec((tm, D_FF), lambda i: (i, 1)), half],
        out_specs=pl.BlockSpec((tm, 2 * D_FF), lambda i: (i, 0)),
        compiler_params=_params("parallel"), name=name)(gu, gu, da)


def _tri(n, lower):
    r = lax.broadcasted_iota(jnp.int32, (n, n), 0)
    c = lax.broadcasted_iota(jnp.int32, (n, n), 1)
    return (c <= r) if lower else (c >= r)


def _running_sum(x, lower):
    n = x.shape[0]
    tri = _tri(n, lower).astype(F32)
    return lax.dot_general(tri, x, NN, precision=lax.Precision.HIGHEST, preferred_element_type=F32)


def _hgrn_gates(q_raw, f_raw, lb):
    C = q_raw.shape[0]
    sig_f = _sigmoid(f_raw)
    forget = lb + (1.0 - lb) * sig_f
    key = 1.0 - forget
    log_f = jnp.log(forget)
    b = _running_sum(log_f, True)
    first_half = lax.broadcasted_iota(jnp.int32, log_f.shape, 0) < C // 2
    r = jnp.sum(jnp.where(first_half, log_f, 0.0), axis=0, keepdims=True)
    b_last = jnp.sum(log_f, axis=0, keepdims=True)
    e_a = jnp.exp(jnp.minimum(b - r, HGRN_EXP_CLAMP))
    e_b = jnp.exp(jnp.minimum(r - b, HGRN_EXP_CLAMP))
    e_q = jnp.exp(b)
    e_k = jnp.exp(b_last - b)
    sig_q = _sigmoid(q_raw)
    query = q_raw * sig_q
    return dict(sig_f=sig_f, forget=forget, sig_q=sig_q, e_a=e_a, e_b=e_b, e_q=e_q, e_k=e_k,
                e_last=jnp.exp(b_last), q_a=query * e_a, k_b=key * e_b, q_hat=query * e_q, k_til=key * e_k)


def _hgrn_fwd(proj, lb, gain, name):
    T = proj.shape[0]
    C = HGRN_CHUNK
    H, HD = HGRN_HEADS, HGRN_DIM

    def body(q_ref, f_ref, i_ref, g_ref, lb_ref, gain_ref, og_ref, o_ref, st_ref, s_scr):
        @pl.when(pl.program_id(0) == 0)
        def _():
            s_scr[...] = jnp.zeros_like(s_scr)

        st_ref[0] = s_scr[...]
        gt = _hgrn_gates(q_ref[...], f_ref[...], lb_ref[...])
        causal = _tri(C, True)
        gain_v = gain_ref[...]
        for h in range(H):
            sl = slice(h * HD, (h + 1) * HD)
            v = i_ref[:, sl].astype(BF16)
            p = jnp.where(causal, _dot(gt["q_a"][:, sl].astype(BF16), gt["k_b"][:, sl].astype(BF16), NT), 0.0)
            s_t = s_scr[h]
            o = _dot(p.astype(BF16), v, NN) + _dot(gt["q_hat"][:, sl].astype(BF16), s_t.astype(BF16), NT)
            s_scr[h] = gt["e_last"][:, sl] * s_t + _dot(v, gt["k_til"][:, sl].astype(BF16), TN)
            o_ref[:, sl] = o
            rstd = lax.rsqrt(jnp.mean(o * o, axis=-1, keepdims=True) + NORM_EPS)
            g_raw = g_ref[:, sl]
            og_ref[:, sl] = (o * rstd * gain_v * (g_raw * _sigmoid(g_raw))).astype(BF16)

    col = lambda j: pl.BlockSpec((C, D_MODEL), lambda c: (c, j))
    row = pl.BlockSpec((C, D_MODEL), lambda c: (c, 0))
    return pl.pallas_call(
        body,
        out_shape=(jax.ShapeDtypeStruct((T, D_MODEL), BF16), jax.ShapeDtypeStruct((T, D_MODEL), F32),
                   jax.ShapeDtypeStruct((T // C, H, HD, HD), F32)),
        grid=(T // C,),
        in_specs=[col(0), col(1), col(2), col(3), pl.BlockSpec((1, D_MODEL), lambda c: (0, 0)),
                  pl.BlockSpec((1, HD), lambda c: (0, 0))],
        out_specs=(row, row, pl.BlockSpec((1, H, HD, HD), lambda c: (c, 0, 0, 0))),
        scratch_shapes=[pltpu.VMEM((H, HD, HD), F32)],
        compiler_params=_params("arbitrary"), name=name)(proj, proj, proj, proj, lb, gain)


def _hgrn_bwd(proj, o_pre, d_og, states, lb, gain, name):
    T = proj.shape[0]
    C = HGRN_CHUNK
    H, HD = HGRN_HEADS, HGRN_DIM
    NC = T // C

    def body(q_ref, f_ref, i_ref, g_ref, o_ref, dog_ref, st_ref, lb_ref, gain_ref,
             dproj_ref, dlb_ref, dgain_ref, ds_scr, dq_scr, dk_scr, db_scr):
        @pl.when(pl.program_id(0) == 0)
        def _():
            ds_scr[...] = jnp.zeros_like(ds_scr)
            dlb_ref[...] = jnp.zeros_like(dlb_ref)
            dgain_ref[...] = jnp.zeros_like(dgain_ref)

        lbv = lb_ref[...]
        q_raw = q_ref[...]
        gt = _hgrn_gates(q_raw, f_ref[...], lbv)
        causal = _tri(C, True)
        last_row = lax.broadcasted_iota(jnp.int32, (C, HD), 0) == C - 1
        gain_v = gain_ref[...]
        dgain = jnp.zeros((1, HD), F32)
        for h in range(H):
            sl = slice(h * HD, (h + 1) * HD)
            o = o_ref[:, sl]
            rstd = lax.rsqrt(jnp.mean(o * o, axis=-1, keepdims=True) + NORM_EPS)
            n = o * rstd
            g_raw = g_ref[:, sl]
            sg = _sigmoid(g_raw)
            d_out = dog_ref[:, sl]
            dproj_ref[:, 3 * D_MODEL + h * HD:3 * D_MODEL + (h + 1) * HD] = (
                d_out * n * gain_v * (sg * (1.0 + g_raw * (1.0 - sg)))).astype(BF16)
            dy = d_out * (g_raw * sg)
            dgain = dgain + jnp.sum(dy * n, axis=0, keepdims=True)
            dn = dy * gain_v
            do = (rstd * (dn - n * jnp.mean(dn * n, axis=-1, keepdims=True))).astype(BF16)
            q_a, k_b = gt["q_a"][:, sl], gt["k_b"][:, sl]
            q_hat, k_til = gt["q_hat"][:, sl], gt["k_til"][:, sl]
            q_ab, k_bb = q_a.astype(BF16), k_b.astype(BF16)
            v = i_ref[:, sl].astype(BF16)
            s_t = st_ref[0, h]
            ds_t = ds_scr[h]
            ds_b = ds_t.astype(BF16)
            e_last = gt["e_last"][:, sl]
            p = jnp.where(causal, _dot(q_ab, k_bb, NT), 0.0).astype(BF16)
            dp = jnp.where(causal, _dot(do, v, NT), 0.0).astype(BF16)
            dv = _dot(p, do, TN) + _dot(k_til.astype(BF16), ds_b, NT)
            dq_a = _dot(dp, k_bb, NN)
            dk_b = _dot(dp, q_ab, TN)
            dq_hat = _dot(do, s_t.astype(BF16), NN)
            dk_til = _dot(v, ds_b, NN)
            ds_scr[h] = _dot(do, q_hat.astype(BF16), TN) + e_last * ds_t
            db_last = jnp.sum(ds_t * e_last * s_t, axis=0, keepdims=True) + jnp.sum(
                dk_til * k_til, axis=0, keepdims=True)
            dproj_ref[:, 2 * D_MODEL + h * HD:2 * D_MODEL + (h + 1) * HD] = dv.astype(BF16)
            dq_scr[:, sl] = dq_a * gt["e_a"][:, sl] + dq_hat * gt["e_q"][:, sl]
            dk_scr[:, sl] = dk_b * gt["e_b"][:, sl] + dk_til * gt["e_k"][:, sl]
            db = dq_a * q_ab.astype(F32) + dq_hat * q_hat - dk_b * k_bb.astype(F32) - dk_til * k_til
            db_scr[:, sl] = db + jnp.where(last_row, db_last, 0.0)
        dgain_ref[...] += dgain
        dlogf = _running_sum(db_scr[...], False)
        sig_f, forget, sig_q = gt["sig_f"], gt["forget"], gt["sig_q"]
        dforget = dlogf / forget - dk_scr[...]
        dproj_ref[:, D_MODEL:2 * D_MODEL] = (dforget * (1.0 - lbv) * sig_f * (1.0 - sig_f)).astype(BF16)
        dlb_ref[...] += jnp.sum(dforget * (1.0 - sig_f), axis=0, keepdims=True)
        dproj_ref[:, 0:D_MODEL] = (dq_scr[...] * (sig_q * (1.0 + q_raw * (1.0 - sig_q)))).astype(BF16)

    col = lambda j: pl.BlockSpec((C, D_MODEL), lambda c: (NC - 1 - c, j))
    row = pl.BlockSpec((C, D_MODEL), lambda c: (NC - 1 - c, 0))
    return pl.pallas_call(
        body,
        out_shape=(jax.ShapeDtypeStruct((T, 4 * D_MODEL), BF16), jax.ShapeDtypeStruct((1, D_MODEL), F32),
                   jax.ShapeDtypeStruct((1, HD), F32)),
        grid=(NC,),
        in_specs=[col(0), col(1), col(2), col(3), row, row,
                  pl.BlockSpec((1, H, HD, HD), lambda c: (NC - 1 - c, 0, 0, 0)),
                  pl.BlockSpec((1, D_MODEL), lambda c: (0, 0)), pl.BlockSpec((1, HD), lambda c: (0, 0))],
        out_specs=(pl.BlockSpec((C, 4 * D_MODEL), lambda c: (NC - 1 - c, 0)),
                   pl.BlockSpec((1, D_MODEL), lambda c: (0, 0)), pl.BlockSpec((1, HD), lambda c: (0, 0))),
        scratch_shapes=[pltpu.VMEM((H, HD, HD), F32), pltpu.VMEM((C, D_MODEL), F32),
                        pltpu.VMEM((C, D_MODEL), F32), pltpu.VMEM((C, D_MODEL), F32)],
        compiler_params=_params("arbitrary"), name=name)(proj, proj, proj, proj, o_pre, d_og, states, lb, gain)


def _attn_masks():
    r = lax.broadcasted_iota(jnp.int32, (ATTN_BLOCK, ATTN_BLOCK), 0)
    c = lax.broadcasted_iota(jnp.int32, (ATTN_BLOCK, ATTN_BLOCK), 1)
    return c >= r, c <= r


def _attn_fwd(qkv, dilation, name):
    T = qkv.shape[0]
    nb = T // dilation // ATTN_BLOCK
    W = ATTN_GROUP_WIDTH
    scale = ATTN_DIM ** -0.5

    def body(q_ref, kp_ref, kc_ref, vp_ref, vc_ref, o_ref, lse_ref):
        no_prev = jnp.where(pl.program_id(1) > 0, 0.0, NEG_BIG)
        m_prev, m_cur = _attn_masks()
        for h in range(ATTN_GROUP_HEADS):
            sl = slice(h * ATTN_DIM, (h + 1) * ATTN_DIM)
            q = q_ref[:, sl]
            s_p = jnp.where(m_prev, _dot(q, kp_ref[:, sl], NT) * scale + no_prev, NEG_BIG)
            s_c = jnp.where(m_cur, _dot(q, kc_ref[:, sl], NT) * scale, NEG_BIG)
            m = jnp.maximum(jnp.max(s_p, axis=-1, keepdims=True), jnp.max(s_c, axis=-1, keepdims=True))
            p_p = jnp.exp(s_p - m)
            p_c = jnp.exp(s_c - m)
            l = jnp.sum(p_p, axis=-1, keepdims=True) + jnp.sum(p_c, axis=-1, keepdims=True)
            acc = _dot(p_p.astype(BF16), vp_ref[:, sl], NN) + _dot(p_c.astype(BF16), vc_ref[:, sl], NN)
            o_ref[:, sl] = acc / l
            lse_ref[:, sl] = jnp.broadcast_to(m + jnp.log(l), (ATTN_BLOCK, ATTN_DIM))

    blk = lambda col, prev: pl.BlockSpec(
        (ATTN_BLOCK, W), lambda s, n: (s * nb + (jnp.maximum(n - 1, 0) if prev else n), col))
    out = pl.BlockSpec((ATTN_BLOCK, W), lambda s, n: (s * nb + n, 0))
    return pl.pallas_call(
        body, out_shape=(jax.ShapeDtypeStruct((T, W), F32),) * 2, grid=(dilation, nb),
        in_specs=[blk(0, False), blk(1, True), blk(1, False), blk(2, True), blk(2, False)],
        out_specs=(out, out), compiler_params=_params("parallel", "arbitrary"), name=name)(qkv, qkv, qkv, qkv, qkv)


def _attn_bwd(qkv, d_out, lse, delta, cos, sin, dilation, name):
    T = qkv.shape[0]
    nb = T // dilation // ATTN_BLOCK
    W = ATTN_GROUP_WIDTH
    scale = ATTN_DIM ** -0.5

    def unrope(x, cos_v, sin_v):
        return x * cos_v + pltpu.roll(x * sin_v, ATTN_DIM // 2, 1)

    def body(q_ref, kp_ref, kc_ref, vp_ref, vc_ref, do_ref, lse_ref, dl_ref, cos_ref, sin_ref,
             out_ref, dq_scr, dk_scr, dv_scr):
        n = pl.program_id(1)
        cos_v, sin_v = cos_ref[...], sin_ref[...]

        @pl.when(n > 0)
        def _():
            for h in range(ATTN_GROUP_HEADS):
                sl = slice(h * ATTN_DIM, (h + 1) * ATTN_DIM)
                out_ref[:, sl] = unrope(dq_scr[:, sl], cos_v, sin_v).astype(BF16)

        @pl.when(n == nb)
        def _():
            for h in range(ATTN_GROUP_HEADS):
                sl = slice(h * ATTN_DIM, (h + 1) * ATTN_DIM)
                out_ref[:, W + h * ATTN_DIM:W + (h + 1) * ATTN_DIM] = unrope(dk_scr[:, sl], cos_v, sin_v).astype(BF16)
                out_ref[:, 2 * W + h * ATTN_DIM:2 * W + (h + 1) * ATTN_DIM] = dv_scr[:, sl].astype(BF16)

        @pl.when(n == 0)
        def _():
            dk_scr[...] = jnp.zeros_like(dk_scr)
            dv_scr[...] = jnp.zeros_like(dv_scr)

        @pl.when(n < nb)
        def _():
            has_prev = n > 0
            no_prev = jnp.where(has_prev, 0.0, NEG_BIG)
            m_prev, m_cur = _attn_masks()
            for h in range(ATTN_GROUP_HEADS):
                sl = slice(h * ATTN_DIM, (h + 1) * ATTN_DIM)
                q, k_p, k_c, v_p, v_c = q_ref[:, sl], kp_ref[:, sl], kc_ref[:, sl], vp_ref[:, sl], vc_ref[:, sl]
                do = do_ref[:, sl]
                lse_v, dl_v = lse_ref[:, sl], dl_ref[:, sl]
                p_p = jnp.where(m_prev, jnp.exp(_dot(q, k_p, NT) * scale - lse_v + no_prev), 0.0)
                p_c = jnp.where(m_cur, jnp.exp(_dot(q, k_c, NT) * scale - lse_v), 0.0)
                ds_p = (p_p * (_dot(do, v_p, NT) - dl_v) * scale).astype(BF16)
                ds_c = (p_c * (_dot(do, v_c, NT) - dl_v) * scale).astype(BF16)
                dk_prev = dk_scr[:, sl] + _dot(ds_p, q, TN)
                dv_prev = dv_scr[:, sl] + _dot(p_p.astype(BF16), do, TN)

                @pl.when(has_prev)
                def _():
                    out_ref[:, W + h * ATTN_DIM:W + (h + 1) * ATTN_DIM] = unrope(dk_prev, cos_v, sin_v).astype(BF16)
                    out_ref[:, 2 * W + h * ATTN_DIM:2 * W + (h + 1) * ATTN_DIM] = dv_prev.astype(BF16)

                dq_scr[:, sl] = _dot(ds_p, k_p, NN) + _dot(ds_c, k_c, NN)
                dk_scr[:, sl] = _dot(ds_c, q, TN)
                dv_scr[:, sl] = _dot(p_c.astype(BF16), do, TN)

    def cur(n):
        return jnp.minimum(n, nb - 1)

    def late(n):
        return jnp.maximum(n - 1, 0)

    qkv_blk = lambda col, prev: pl.BlockSpec(
        (ATTN_BLOCK, W), lambda s, n: (s * nb + (jnp.maximum(cur(n) - 1, 0) if prev else cur(n)), col))
    row = pl.BlockSpec((ATTN_BLOCK, W), lambda s, n: (s * nb + cur(n), 0))
    tab = pl.BlockSpec((ATTN_BLOCK, ATTN_DIM), lambda s, n: (s * nb + late(n), 0))
    return pl.pallas_call(
        body, out_shape=jax.ShapeDtypeStruct((T, 3 * W), BF16), grid=(dilation, nb + 1),
        in_specs=[qkv_blk(0, False), qkv_blk(1, True), qkv_blk(1, False), qkv_blk(2, True), qkv_blk(2, False),
                  row, row, row, tab, tab],
        out_specs=pl.BlockSpec((ATTN_BLOCK, 3 * W), lambda s, n: (s * nb + late(n), 0)),
        scratch_shapes=[pltpu.VMEM((ATTN_BLOCK, W), F32)] * 3,
        compiler_params=_params("parallel", "arbitrary"), name=name)(
            qkv, qkv, qkv, qkv, qkv, d_out, lse, delta, cos, sin)


def _attn_merge_fwd(outs, lses, name):
    T = outs[0].shape[0]
    W = ATTN_GROUP_WIDTH
    tm = _pick_tile(T, 512, 16)

    def body(o0, o1, o2, l0, l1, l2, oc_ref, lse_ref):
        ls = [l0[...], l1[...], l2[...]]
        m = jnp.maximum(jnp.maximum(ls[0], ls[1]), ls[2])
        tot = m + jnp.log(jnp.exp(ls[0] - m) + jnp.exp(ls[1] - m) + jnp.exp(ls[2] - m))
        lse_ref[...] = tot
        for g, o in enumerate((o0, o1, o2)):
            oc_ref[:, g * W:(g + 1) * W] = (o[...] * jnp.exp(ls[g] - tot)).astype(BF16)

    blk = pl.BlockSpec((tm, W), lambda i: (i, 0))
    return pl.pallas_call(
        body, out_shape=(jax.ShapeDtypeStruct((T, 3 * W), BF16), jax.ShapeDtypeStruct((T, W), F32)),
        grid=(T // tm,), in_specs=[blk] * 6, out_specs=(pl.BlockSpec((tm, 3 * W), lambda i: (i, 0)), blk),
        compiler_params=_params("parallel"), name=name)(*outs, *lses)


def _attn_merge_bwd(d_oc, oc, name):
    T = d_oc.shape[0]
    W = ATTN_GROUP_WIDTH
    tm = _pick_tile(T, 512, 16)

    def body(d_ref, o_ref, delta_ref, db_ref):
        d = d_ref[...]
        db_ref[...] = d.astype(BF16)
        prod = d * o_ref[...].astype(F32)
        for h in range(ATTN_GROUP_HEADS):
            tot = jnp.zeros((tm, 1), F32)
            for g in range(3):
                lo = g * W + h * ATTN_DIM
                tot = tot + jnp.sum(prod[:, lo:lo + ATTN_DIM], axis=-1, keepdims=True)
            delta_ref[:, h * ATTN_DIM:(h + 1) * ATTN_DIM] = jnp.broadcast_to(tot, (tm, ATTN_DIM))

    wide = pl.BlockSpec((tm, 3 * W), lambda i: (i, 0))
    return pl.pallas_call(
        body, out_shape=(jax.ShapeDtypeStruct((T, W), F32), jax.ShapeDtypeStruct((T, 3 * W), BF16)),
        grid=(T // tm,), in_specs=[wide, wide], out_specs=(pl.BlockSpec((tm, W), lambda i: (i, 0)), wide),
        compiler_params=_params("parallel"), name=name)(d_oc, oc)


def _to_residues(x, d):
    if d == 1:
        return x
    T, C = x.shape
    return x.reshape(T // d, d, C).transpose(1, 0, 2).reshape(T, C)


def _from_residues(x, d):
    if d == 1:
        return x
    T, C = x.shape
    return x.reshape(d, T // d, C).transpose(1, 0, 2).reshape(T, C)


def _rope_tables(T):
    inv_freq = 1.0 / (ROPE_THETA ** (jnp.arange(0, ATTN_DIM, 2, dtype=F32) / ATTN_DIM))
    ang = jnp.arange(T, dtype=F32)[:, None] * inv_freq[None, :]
    cos, sin = jnp.cos(ang), jnp.sin(ang)
    return jnp.concatenate([cos, cos], axis=1), jnp.concatenate([-sin, sin], axis=1)


WEIGHT_GROUPS = {"hgrn": ("hgrn_in", "hgrn_out"), "ffn0": ("ffn_in0", "ffn_down0"),
                 "attn": ("qkv", "attn_out"), "ffn1": ("ffn_in1", "ffn_down1")}


def _local_step(x, target, norm_mix, norm_ffn, lb, out_gain, final_gain, fetch, publish):
    T = x.shape[0]
    ident = lambda j: j
    g_mix = [norm_mix[0:1], norm_mix[1:2]]
    g_ffn = [norm_ffn[0:1], norm_ffn[1:2]]
    w = {}

    def ffn_fwd(h, layer):
        n = _rms_fwd(h, g_ffn[layer], f"ffn{layer}_norm")
        w.update(fetch(f"ffn{layer}", [n]))
        gu = _mm_nt(n, w[f"ffn_in{layer}"], n_out=2 * D_FF, tn=512, w_block=ident, out_dtype=F32,
                    name=f"ffn{layer}_in")
        a = _swiglu_fwd(gu, f"ffn{layer}_act")
        out = _mm_nn(a, w[f"ffn_down{layer}"], h, tk=1408, w_block=ident, name=f"ffn{layer}_down")
        return out, (n, gu, a)

    def ffn_bwd(h, saved, dh, dhb, layer):
        n, gu, a = saved
        da = _mm_nt(dhb, w[f"ffn_down{layer}"], n_out=D_FF, tn=1408, w_block=ident, out_dtype=F32,
                    name=f"ffn{layer}_down_dx")
        dgu = _swiglu_bwd(gu, da, f"ffn{layer}_act_bwd")
        grads = {f"ffn_down{layer}": _mm_tn(a, dhb, tr=1408, name=f"ffn{layer}_down_dw"),
                 f"ffn_in{layer}": _mm_tn(dgu, n, tr=512, name=f"ffn{layer}_in_dw")}
        zero = publish(f"ffn{layer}", grads)
        dn = _mm_nn(dgu, w[f"ffn_in{layer}"], None, tk=1408, w_block=ident, name=f"ffn{layer}_in_dx")
        return _rms_bwd(h, g_ffn[layer] + zero, dn, dh, f"ffn{layer}_norm_bwd")

    u0 = _rms_fwd(x, g_mix[0], "hgrn_norm")
    w.update(fetch("hgrn", [u0]))
    proj = _mm_nt(u0, w["hgrn_in"], n_out=4 * D_MODEL, tn=512, w_block=ident, out_dtype=F32, name="hgrn_in")
    og, o_pre, states = _hgrn_fwd(proj, lb, out_gain, "hgrn_fwd")
    h1 = _mm_nn(og, w["hgrn_out"], x, tk=1024, w_block=ident, name="hgrn_out")
    h2, ffn0 = ffn_fwd(h1, 0)

    u1 = _rms_fwd(h2, g_mix[1], "attn_norm")
    w.update(fetch("attn", [u1]))
    cos, sin = _rope_tables(T)
    u1_g, qkv_g, cos_g, sin_g, outs, lses = [], [], [], [], [], []
    for g, d in enumerate(ATTN_DILATIONS):
        u1_g.append(_to_residues(u1, d))
        cos_g.append(_to_residues(cos, d))
        sin_g.append(_to_residues(sin, d))
        qkv_g.append(_mm_nt(u1_g[g], w["qkv"], n_out=ATTN_WIDTH, tn=ATTN_GROUP_WIDTH,
                            w_block=functools.partial(lambda j, g: 3 * j + g, g=g), out_dtype=BF16,
                            name=f"attn_qkv{g}", rope=(cos_g[g], sin_g[g], 2)))
        o_g, lse_g = _attn_fwd(qkv_g[g], d, f"attn_fwd{g}")
        outs.append(_from_residues(o_g, d))
        lses.append(_from_residues(lse_g, d))
    oc, lse_all = _attn_merge_fwd(outs, lses, "attn_merge")
    h3 = _mm_nn(oc, w["attn_out"], h2, tk=512, w_block=ident, name="attn_out")
    h4, ffn1 = ffn_fwd(h3, 1)

    dh4, dh4b, d_final, loss_part = _loss_head(h4, target, final_gain, "loss_head")
    dh3, dh3b, d_ffn1 = ffn_bwd(h3, ffn1, dh4, dh4b, 1)

    d_oc = _mm_nt(dh3b, w["attn_out"], n_out=ATTN_WIDTH, tn=512, w_block=ident, out_dtype=F32, name="attn_out_dx")
    grad_attn_out = _mm_tn(oc, dh3b, tr=512, name="attn_out_dw")
    delta, d_ocb = _attn_merge_bwd(d_oc, oc, "attn_merge_bwd")
    du1 = None
    qkv_pieces = []
    for g, d in enumerate(ATTN_DILATIONS):
        W = ATTN_GROUP_WIDTH
        dqkv = _attn_bwd(qkv_g[g], _to_residues(d_ocb[:, g * W:(g + 1) * W], d), _to_residues(lse_all, d),
                         _to_residues(delta, d), cos_g[g], sin_g[g], d, f"attn_bwd{g}")
        qkv_pieces.append(_mm_tn(dqkv, u1_g[g], tr=512, name=f"attn_qkv_dw{g}"))
        du1_g = _mm_nn(dqkv, w["qkv"], None, tk=W, w_block=functools.partial(lambda k, g: 3 * k + g, g=g),
                       name=f"attn_qkv_dx{g}")
        du1_g = _from_residues(du1_g, d)
        du1 = du1_g if du1 is None else du1 + du1_g
    grad_qkv = jnp.stack([p.reshape(3, ATTN_GROUP_WIDTH, D_MODEL) for p in qkv_pieces], axis=1).reshape(
        3 * ATTN_WIDTH, D_MODEL)
    zero = publish("attn", {"qkv": grad_qkv, "attn_out": grad_attn_out})
    dh2, dh2b, d_mix1 = _rms_bwd(h2, g_mix[1] + zero, du1, dh3, "attn_norm_bwd")

    dh1, dh1b, d_ffn0 = ffn_bwd(h1, ffn0, dh2, dh2b, 0)

    d_og = _mm_nt(dh1b, w["hgrn_out"], n_out=D_MODEL, tn=512, w_block=ident, out_dtype=F32, name="hgrn_out_dx")
    grad_hgrn_out = _mm_tn(og, dh1b, tr=512, name="hgrn_out_dw")
    dproj, d_lb, d_out_gain = _hgrn_bwd(proj, o_pre, d_og, states, lb, out_gain, "hgrn_bwd")
    zero = publish("hgrn", {"hgrn_in": _mm_tn(dproj, u0, tr=512, name="hgrn_in_dw"), "hgrn_out": grad_hgrn_out})
    du0 = _mm_nn(dproj, w["hgrn_in"], None, tk=1024, w_block=ident, name="hgrn_in_dx")
    dx, _, d_mix0 = _rms_bwd(x, g_mix[0] + zero, du0, dh1, "hgrn_norm_bwd")

    small = dict(norm_mix0=d_mix0, norm_mix1=d_mix1, norm_ffn0=d_ffn0, norm_ffn1=d_ffn1, lb=d_lb,
                 out_gain=d_out_gain, final=d_final, loss=loss_part)
    return dx, small


WEIGHT_NAMES = ("hgrn_in", "hgrn_out", "qkv", "attn_out", "ffn_in0", "ffn_in1", "ffn_down0", "ffn_down1")
MESH_IDS = pl.DeviceIdType.MESH
HBM_SPEC = pl.BlockSpec(memory_space=pl.ANY)


SEM_SPEC = pl.BlockSpec(memory_space=pltpu.SEMAPHORE)
LAND_SPEC = pl.BlockSpec(memory_space=pltpu.HBM)
N_PEERS = N_DEV - 1
PEER_OFFSETS = [(dx, dy, dc) for dx in (0, 1) for dy in (0, 1) for dc in (0, 1)][1:]


def _mesh_place():
    x, y, c = lax.axis_index("x"), lax.axis_index("y"), lax.axis_index("c")
    peers = []
    for dx, dy, dc in PEER_OFFSETS:
        px, py, pc = (1 - x if dx else x), (1 - y if dy else y), (1 - c if dc else c)
        peers.append(((px, py, pc), 4 * px + 2 * py + pc))
    return 4 * x + 2 * y + c, peers


def _start_copies(srcs, groups, scatter, name):
    nw, ng = len(srcs), len(groups)
    land_shapes = [(s.shape if scatter else (N_DEV,) + s.shape) for s in srcs]

    def body(*refs):
        src_refs, land_refs = refs[:nw], refs[nw:2 * nw]
        sems = refs[2 * nw:2 * nw + 2 * ng]
        local_sems = refs[-1]
        me, peers = _mesh_place()
        own = [pltpu.make_async_copy(src_refs[w].at[me] if scatter else src_refs[w], land_refs[w].at[me],
                                     local_sems.at[w]) for w in range(nw)]
        for cp in own:
            cp.start()
        for gi, group in enumerate(groups):
            for i, w in enumerate(group):
                for k, (peer, pid) in enumerate(peers):
                    pltpu.make_async_remote_copy(
                        src_ref=src_refs[w].at[pid] if scatter else src_refs[w], dst_ref=land_refs[w].at[me],
                        send_sem=sems[2 * gi].at[i * N_PEERS + k], recv_sem=sems[2 * gi + 1].at[i * N_PEERS + k],
                        device_id=peer, device_id_type=MESH_IDS).start()
        for cp in own:
            cp.wait()
        token_ref = refs[2 * nw + 2 * ng + 2 * nw]
        token_ref[...] = jnp.zeros_like(token_ref)

    sem_shapes = []
    for group in groups:
        sem_shapes += [pltpu.SemaphoreType.DMA((len(group) * N_PEERS,))] * 2
    out_shape = (sem_shapes + [pltpu.HBM(s.shape, s.dtype) for s in srcs]
                 + [pltpu.HBM(shape, s.dtype) for shape, s in zip(land_shapes, srcs)]
                 + [jax.ShapeDtypeStruct((8, 128), F32)])
    operands = [pltpu.with_memory_space_constraint(s, pltpu.HBM) for s in srcs]
    operands += [pltpu.with_memory_space_constraint(lax.empty(shape, s.dtype), pltpu.HBM)
                 for shape, s in zip(land_shapes, srcs)]
    res = pl.pallas_call(
        body, out_shape=out_shape, in_specs=[LAND_SPEC] * (2 * nw),
        out_specs=[SEM_SPEC] * (2 * ng) + [LAND_SPEC] * (2 * nw) + [pl.BlockSpec(memory_space=pltpu.VMEM)],
        input_output_aliases={i: 2 * ng + i for i in range(2 * nw)},
        scratch_shapes=[pltpu.SemaphoreType.DMA((nw,))],
        compiler_params=pltpu.CompilerParams(has_side_effects=pltpu.SideEffectType.DATAFLOW_SIDE_EFFECTING),
        name=name)(*operands)
    sems = [(res[2 * gi], res[2 * gi + 1]) for gi in range(ng)]
    return sems, list(res[2 * ng:2 * ng + nw]), list(res[2 * ng + nw:2 * ng + 2 * nw]), res[-1][0:1, 0:1]


def _wait_copies(sems, srcs, lands, after, scatter, name):
    n = len(srcs)

    def body(*refs):
        src_refs, land_refs = refs[:n], refs[n:2 * n]
        send_sems, recv_sems = refs[2 * n], refs[2 * n + 1]
        _, peers = _mesh_place()
        for i in range(n):
            for k, (peer, _) in enumerate(peers):
                copy = pltpu.make_async_remote_copy(
                    src_ref=src_refs[i].at[0] if scatter else src_refs[i], dst_ref=land_refs[i].at[0],
                    send_sem=send_sems.at[i * N_PEERS + k], recv_sem=recv_sems.at[i * N_PEERS + k],
                    device_id=peer, device_id_type=MESH_IDS)
                copy.wait_send()
                copy.wait_recv()

    arrays = list(srcs) + list(lands)
    res = pl.pallas_call(
        body, out_shape=[pltpu.HBM(a.shape, a.dtype) for a in arrays],
        in_specs=[LAND_SPEC] * (2 * n) + [SEM_SPEC] * 2 + [HBM_SPEC] * len(after),
        out_specs=[LAND_SPEC] * (2 * n), input_output_aliases={i: i for i in range(2 * n)},
        compiler_params=pltpu.CompilerParams(has_side_effects=pltpu.SideEffectType.DATAFLOW_SIDE_EFFECTING),
        name=name)(*arrays, sems[0], sems[1], *after)
    return list(res[n:])


def _gather_small(block, name):
    def body(in_ref, out_ref, send_sems, recv_sems, local_sem):
        me, peers = _mesh_place()
        own = pltpu.make_async_copy(in_ref, out_ref.at[me], local_sem)
        own.start()
        sends = [pltpu.make_async_remote_copy(
            src_ref=in_ref, dst_ref=out_ref.at[me], send_sem=send_sems.at[k], recv_sem=recv_sems.at[k],
            device_id=peer, device_id_type=MESH_IDS) for k, (peer, _) in enumerate(peers)]
        for cp in sends:
            cp.start()
        for cp in sends:
            cp.wait_recv()
        for cp in sends:
            cp.wait_send()
        own.wait()

    return pl.pallas_call(
        body, out_shape=jax.ShapeDtypeStruct((N_DEV,) + block.shape, block.dtype),
        in_specs=[HBM_SPEC], out_specs=HBM_SPEC,
        scratch_shapes=[pltpu.SemaphoreType.DMA((N_PEERS,)), pltpu.SemaphoreType.DMA((N_PEERS,)),
                        pltpu.SemaphoreType.DMA],
        name=name)(block)


def _sum_blocks(recv, name):
    rows = recv.shape[1]
    tr = _pick_tile(rows, 256, 16)

    def body(r_ref, g_ref):
        acc = r_ref[0].astype(F32)
        for j in range(1, N_DEV):
            acc = acc + r_ref[j].astype(F32)
        g_ref[...] = acc

    return pl.pallas_call(
        body, out_shape=jax.ShapeDtypeStruct((rows, D_MODEL), F32), grid=(rows // tr,),
        in_specs=[pl.BlockSpec((N_DEV, tr, D_MODEL), lambda i: (0, i, 0))],
        out_specs=pl.BlockSpec((tr, D_MODEL), lambda i: (i, 0)),
        compiler_params=_params("parallel"), name=name)(recv)


def _adamw_math(w, g, m, v):
    m_new = ADAM_B1 * m + (1.0 - ADAM_B1) * g
    v_new = ADAM_B2 * v + (1.0 - ADAM_B2) * (g * g)
    m_hat = m_new / (1.0 - ADAM_B1 ** ADAM_STEP)
    v_hat = v_new / (1.0 - ADAM_B2 ** ADAM_STEP)
    delta = -ADAM_LR * (m_hat / (jnp.sqrt(v_hat) + ADAM_EPS) + ADAM_WD * w)
    return delta, m_new, v_new


def _adamw(w, g, m, v, name):
    rows, cols = w.shape
    tr = _pick_tile(rows, 256, 8)

    def body(w_ref, g_ref, m_ref, v_ref, d_ref, mo_ref, vo_ref):
        d_ref[...], mo_ref[...], vo_ref[...] = _adamw_math(w_ref[...], g_ref[...], m_ref[...], v_ref[...])

    blk = pl.BlockSpec((tr, cols), lambda i: (i, 0))
    return pl.pallas_call(
        body, out_shape=(jax.ShapeDtypeStruct((rows, cols), F32),) * 3, grid=(rows // tr,),
        in_specs=[blk] * 4, out_specs=(blk,) * 3, compiler_params=_params("parallel"), name=name)(w, g, m, v)


ROW_MIX, ROW_FFN, ROW_LB, ROW_OUT_GAIN, ROW_FINAL = 0, 2, 4, 7, 8
PART_MIX, PART_FFN, PART_LB, PART_OUT_GAIN, PART_FINAL, PART_LOSS = 0, 2, 4, 5, 6, 7


def _small_update(parts_all, w, m, v, name):
    def body(p_ref, w_ref, m_ref, v_ref, g_ref, d_ref, mo_ref, vo_ref, loss_ref):
        def total(row, n=1):
            tot = p_ref[0, row:row + n, :]
            for j in range(1, N_DEV):
                tot = tot + p_ref[j, row:row + n, :]
            return tot

        logits = [w_ref[ROW_LB + i:ROW_LB + i + 1, :] for i in range(3)]
        mx = jnp.maximum(jnp.maximum(logits[0], logits[1]), logits[2])
        ex = [jnp.exp(l - mx) for l in logits]
        den = ex[0] + ex[1] + ex[2]
        prob = [e / den for e in ex]
        d_lb = total(PART_LB)
        g_ref[...] = jnp.zeros_like(g_ref)
        g_ref[ROW_MIX:ROW_MIX + 2, :] = total(PART_MIX, 2)
        g_ref[ROW_FFN:ROW_FFN + 2, :] = total(PART_FFN, 2)
        for i in range(3):
            g_ref[ROW_LB + i:ROW_LB + i + 1, :] = prob[i] * ((d_lb if i == 0 else 0.0) - prob[0] * d_lb)
        g_ref[ROW_OUT_GAIN:ROW_OUT_GAIN + 1, :] = total(PART_OUT_GAIN)
        g_ref[ROW_FINAL:ROW_FINAL + 1, :] = total(PART_FINAL)
        d_ref[...], mo_ref[...], vo_ref[...] = _adamw_math(w_ref[...], g_ref[...], m_ref[...], v_ref[...])
        loss_ref[...] = jnp.sum(total(PART_LOSS), axis=-1, keepdims=True)

    packed = jax.ShapeDtypeStruct((16, D_MODEL), F32)
    return pl.pallas_call(
        body, out_shape=(packed, packed, packed, packed, jax.ShapeDtypeStruct((1, 1), F32)),
        compiler_params=pltpu.CompilerParams(vmem_limit_bytes=VMEM_LIMIT), name=name)(parts_all, w, m, v)


def _pack_small(norm_mix, norm_ffn, lb_logits, out_gain, final):
    pad = jnp.zeros((1, D_MODEL - HGRN_DIM), F32)
    return jnp.concatenate([norm_mix, norm_ffn, lb_logits, jnp.concatenate([out_gain, pad], axis=1),
                            final.reshape(1, D_MODEL), jnp.zeros((16 - ROW_FINAL - 1, D_MODEL), F32)], axis=0)


def _unpack_small(p):
    return (p[ROW_MIX:ROW_MIX + 2], p[ROW_FFN:ROW_FFN + 2], p[ROW_LB:ROW_LB + 3],
            p[ROW_OUT_GAIN:ROW_OUT_GAIN + 1, :HGRN_DIM], p[ROW_FINAL])


def _lower_bound(lb_logits, name):
    def body(l_ref, o_ref):
        logits = [l_ref[i:i + 1, :] for i in range(3)]
        mx = jnp.maximum(jnp.maximum(logits[0], logits[1]), logits[2])
        ex = [jnp.exp(l - mx) for l in logits]
        o_ref[...] = ex[0] / (ex[0] + ex[1] + ex[2])

    return pl.pallas_call(body, out_shape=jax.ShapeDtypeStruct((1, D_MODEL), F32), name=name)(lb_logits)


def kernel(x, norm_mix, norm_ffn, hgrn_w_in, hgrn_lb_logits, hgrn_out_norm, hgrn_w_out, attn_w_qkv, attn_w_out, ffn_w_in, ffn_w_down, final_norm, loss_target, m_norm_mix, m_norm_ffn, m_hgrn_w_in, m_hgrn_lb_logits, m_hgrn_out_norm, m_hgrn_w_out, m_attn_w_qkv, m_attn_w_out, m_ffn_w_in, m_ffn_w_down, m_final_norm, v_norm_mix, v_norm_ffn, v_hgrn_w_in, v_hgrn_lb_logits, v_hgrn_out_norm, v_hgrn_w_out, v_attn_w_qkv, v_attn_w_out, v_ffn_w_in, v_ffn_w_down, v_final_norm):
    col_sharded = {"hgrn_in": hgrn_w_in[0], "qkv": attn_w_qkv[0], "ffn_in0": ffn_w_in[0], "ffn_in1": ffn_w_in[1]}
    row_sharded = {"hgrn_out": hgrn_w_out[0], "attn_out": attn_w_out[0], "ffn_down0": ffn_w_down[0],
                   "ffn_down1": ffn_w_down[1]}
    order = [n for group in WEIGHT_GROUPS.values() for n in group]
    shards = [(col_sharded[n].T if n in col_sharded else row_sharded[n]).astype(BF16) for n in order]
    index_groups = [[order.index(n) for n in group] for group in WEIGHT_GROUPS.values()]
    w_sems, w_srcs, w_lands, _ = _start_copies(shards, index_groups, False, "weights_gather_start")

    def fetch(group, after):
        gi = list(WEIGHT_GROUPS).index(group)
        idx = index_groups[gi]
        lands = _wait_copies(w_sems[gi], [w_srcs[i] for i in idx], [w_lands[i] for i in idx], after, False,
                             f"weights_gather_wait_{group}")
        return {n: land.reshape(-1, D_MODEL) for n, land in zip(WEIGHT_GROUPS[group], lands)}

    in_flight = {}

    def publish(group, grads):
        names = WEIGHT_GROUPS[group]
        parts = [grads[n].reshape(N_DEV, -1, D_MODEL) for n in names]
        sems, srcs, lands, zero = _start_copies(parts, [list(range(len(names)))], True, f"grads_send_start_{group}")
        in_flight[group] = (sems[0], srcs, lands)
        return zero

    lb = _lower_bound(hgrn_lb_logits, "hgrn_lower_bound")
    grad_x, small = _local_step(x[0], loss_target[0], norm_mix, norm_ffn, lb, hgrn_out_norm,
                                final_norm.reshape(1, D_MODEL), fetch, publish)

    pad = jnp.zeros((1, D_MODEL - HGRN_DIM), F32)
    small_part = jnp.concatenate(
        [small["norm_mix0"], small["norm_mix1"], small["norm_ffn0"], small["norm_ffn1"], small["lb"],
         jnp.concatenate([small["out_gain"], pad], axis=1), small["final"], small["loss"]], axis=0)
    small_all = _gather_small(small_part, "small_grads_gather")
    received = {}
    for group in ("ffn1", "attn", "ffn0", "hgrn"):
        sems, srcs, lands = in_flight[group]
        lands = _wait_copies(sems, srcs, lands, [small_all], True, f"grads_send_wait_{group}")
        received.update(zip(WEIGHT_GROUPS[group], lands))

    masters = {"hgrn_in": (hgrn_w_in[0], m_hgrn_w_in[0], v_hgrn_w_in[0]),
               "hgrn_out": (hgrn_w_out[0], m_hgrn_w_out[0], v_hgrn_w_out[0]),
               "qkv": (attn_w_qkv[0], m_attn_w_qkv[0], v_attn_w_qkv[0]),
               "attn_out": (attn_w_out[0], m_attn_w_out[0], v_attn_w_out[0]),
               "ffn_in0": (ffn_w_in[0], m_ffn_w_in[0], v_ffn_w_in[0]),
               "ffn_in1": (ffn_w_in[1], m_ffn_w_in[1], v_ffn_w_in[1]),
               "ffn_down0": (ffn_w_down[0], m_ffn_w_down[0], v_ffn_w_down[0]),
               "ffn_down1": (ffn_w_down[1], m_ffn_w_down[1], v_ffn_w_down[1])}
    res = {}
    for n in WEIGHT_NAMES:
        g = _sum_blocks(received[n], f"{n}_grad_sum")
        if n in col_sharded:
            g = g.T
        wv, mv, vv = masters[n]
        res[n] = (g,) + tuple(_adamw(wv, g, mv, vv, f"{n}_adamw"))

    def single(n):
        return [t[None] for t in res[n]]

    def pair(n):
        return [jnp.stack([a, b]) for a, b in zip(res[n + "0"], res[n + "1"])]

    big = dict(hgrn_w_in=single("hgrn_in"), hgrn_w_out=single("hgrn_out"), attn_w_qkv=single("qkv"),
               attn_w_out=single("attn_out"), ffn_w_in=pair("ffn_in"), ffn_w_down=pair("ffn_down"))

    w_small = _pack_small(norm_mix, norm_ffn, hgrn_lb_logits, hgrn_out_norm, final_norm)
    m_small = _pack_small(m_norm_mix, m_norm_ffn, m_hgrn_lb_logits, m_hgrn_out_norm, m_final_norm)
    v_small = _pack_small(v_norm_mix, v_norm_ffn, v_hgrn_lb_logits, v_hgrn_out_norm, v_final_norm)
    g_s, d_s, m_s, v_s, loss = _small_update(small_all, w_small, m_small, v_small, "small_update")
    small_out = [_unpack_small(t) for t in (g_s, d_s, m_s, v_s)]

    def group(i):
        s = small_out[i]
        return (s[0], s[1], big["hgrn_w_in"][i], s[2], s[3], big["hgrn_w_out"][i], big["attn_w_qkv"][i],
                big["attn_w_out"][i], big["ffn_w_in"][i], big["ffn_w_down"][i], s[4])

    return (loss.reshape(()), grad_x[None], *group(0), *group(1), *group(2), *group(3))
```

```python
import functools

import jax
import jax.numpy as jnp
from jax import lax
from jax.experimental import pallas as pl
from jax.experimental.pallas import tpu as pltpu

F32 = jnp.float32
BF16 = jnp.bfloat16

D_MODEL = 1024
N_DEV = 8
NORM_EPS = 1e-6

HGRN_HEADS = 8
HGRN_DIM = 128
HGRN_CHUNK = 64
HGRN_EXP_CLAMP = 60.0

ATTN_DIM = 128
ATTN_BLOCK = 128
ATTN_GROUP_HEADS = 4
ATTN_GROUP_WIDTH = ATTN_GROUP_HEADS * ATTN_DIM
ATTN_DILATIONS = (1, 4, 16)
ATTN_WIDTH = 3 * ATTN_GROUP_WIDTH
ROPE_THETA = 10000.0
NEG_BIG = -1e30

D_FF = 2816

ADAM_LR = 0.001
ADAM_B1 = 0.9
ADAM_B2 = 0.999
ADAM_EPS = 1e-08
ADAM_WD = 0.01
ADAM_STEP = 10

VMEM_LIMIT = 48 * 1024 * 1024

NT = (((1,), (1,)), ((), ()))
NN = (((1,), (0,)), ((), ()))
TN = (((0,), (0,)), ((), ()))


def _dot(a, b, dims):
    return lax.dot_general(a, b, dims, preferred_element_type=F32)


def _params(*sem):
    return pltpu.CompilerParams(dimension_semantics=sem, vmem_limit_bytes=VMEM_LIMIT)


def _pick_tile(n, cap, mult):
    best = None
    for t in range(mult, min(n, cap) + 1, mult):
        if n % t == 0:
            best = t
    assert best is not None, (n, cap, mult)
    return best


def _sigmoid(x):
    return 1.0 / (1.0 + jnp.exp(-x))


def _mm_nt(a, w, *, n_out, tn, w_block, out_dtype, name, rope=None):
    M, K = a.shape
    tm = _pick_tile(M, 1024, 16)
    nj = n_out // tn

    def body(*refs):
        if rope is None:
            a_ref, w_ref, o_ref = refs
        else:
            a_ref, w_ref, cos_ref, sin_ref, o_ref = refs
        acc = _dot(a_ref[...], w_ref[...], NT)
        if rope is None:
            o_ref[...] = acc.astype(out_dtype)
        else:
            j = pl.program_id(1)

            @pl.when(j < rope[2])
            def _():
                cos = cos_ref[...]
                sin = sin_ref[...]
                for h in range(tn // ATTN_DIM):
                    xh = acc[:, h * ATTN_DIM:(h + 1) * ATTN_DIM]
                    rot = pltpu.roll(xh, ATTN_DIM // 2, 1)
                    o_ref[:, h * ATTN_DIM:(h + 1) * ATTN_DIM] = (xh * cos + rot * sin).astype(out_dtype)

            @pl.when(j >= rope[2])
            def _():
                o_ref[...] = acc.astype(out_dtype)

    in_specs = [pl.BlockSpec((tm, K), lambda i, j: (i, 0)),
                pl.BlockSpec((tn, K), lambda i, j: (w_block(j), 0))]
    args = [a, w]
    if rope is not None:
        in_specs += [pl.BlockSpec((tm, ATTN_DIM), lambda i, j: (i, 0))] * 2
        args += [rope[0], rope[1]]
    return pl.pallas_call(
        body, out_shape=jax.ShapeDtypeStruct((M, n_out), out_dtype), grid=(M // tm, nj),
        in_specs=in_specs, out_specs=pl.BlockSpec((tm, tn), lambda i, j: (i, j)),
        compiler_params=_params("parallel", "arbitrary"), name=name)(*args)


def _mm_nn(a, w, resid, *, tk, w_block, name):
    M, R = a.shape
    N = w.shape[1]
    tm = _pick_tile(M, 1024, 16)
    nk = R // tk

    def body(*refs):
        if resid is None:
            a_ref, w_ref, o_ref, acc_ref = refs
        else:
            a_ref, w_ref, r_ref, o_ref, acc_ref = refs
        k = pl.program_id(1)
        part = _dot(a_ref[...], w_ref[...], NN)

        @pl.when(k == 0)
        def _():
            acc_ref[...] = part if resid is None else part + r_ref[...]

        @pl.when(k > 0)
        def _():
            acc_ref[...] += part

        @pl.when(k == nk - 1)
        def _():
            o_ref[...] = acc_ref[...]

    in_specs = [pl.BlockSpec((tm, tk), lambda i, k: (i, k)),
                pl.BlockSpec((tk, N), lambda i, k: (w_block(k), 0))]
    args = [a, w]
    if resid is not None:
        in_specs.append(pl.BlockSpec((tm, N), lambda i, k: (i, 0)))
        args.append(resid)
    return pl.pallas_call(
        body, out_shape=jax.ShapeDtypeStruct((M, N), F32), grid=(M // tm, nk),
        in_specs=in_specs, out_specs=pl.BlockSpec((tm, N), lambda i, k: (i, 0)),
        scratch_shapes=[pltpu.VMEM((tm, N), F32)],
        compiler_params=_params("parallel", "arbitrary"), name=name)(*args)


def _mm_nn2(a0, a1, w, *, tk, name):
    M, R0 = a0.shape
    N = w.shape[1]
    tm = _pick_tile(M, 1024, 16)
    n0, n1 = R0 // tk, a1.shape[1] // tk
    nk = n0 + n1

    def body(a0_ref, a1_ref, w_ref, o_ref, acc_ref):
        k = pl.program_id(1)

        @pl.when(k == 0)
        def _():
            acc_ref[...] = jnp.zeros_like(acc_ref)

        @pl.when(k < n0)
        def _():
            acc_ref[...] += _dot(a0_ref[...], w_ref[...], NN)

        @pl.when(k >= n0)
        def _():
            acc_ref[...] += _dot(a1_ref[...], w_ref[...], NN)

        @pl.when(k == nk - 1)
        def _():
            o_ref[...] = acc_ref[...]

    in_specs = [pl.BlockSpec((tm, tk), lambda i, k: (i, jnp.minimum(k, n0 - 1))),
                pl.BlockSpec((tm, tk), lambda i, k: (i, jnp.maximum(k - n0, 0))),
                pl.BlockSpec((tk, N), lambda i, k: (k, 0))]
    return pl.pallas_call(
        body, out_shape=jax.ShapeDtypeStruct((M, N), F32), grid=(M // tm, nk),
        in_specs=in_specs, out_specs=pl.BlockSpec((tm, N), lambda i, k: (i, 0)),
        scratch_shapes=[pltpu.VMEM((tm, N), F32)],
        compiler_params=_params("parallel", "arbitrary"), name=name)(a0, a1, w)


def _mm_tn(a, b, *, tr, name, into=None, row_tile=0, rows=None):
    T, R = a.shape
    N = b.shape[1]
    tt = _pick_tile(T, 1024, 16)
    nt = T // tt
    rows = R if rows is None else rows

    def body(a_ref, b_ref, *refs):
        o_ref, acc_ref = refs[-2:]
        t = pl.program_id(1)
        part = _dot(a_ref[...], b_ref[...], TN)

        @pl.when(t == 0)
        def _():
            acc_ref[...] = part

        @pl.when(t > 0)
        def _():
            acc_ref[...] += part

        @pl.when(t == nt - 1)
        def _():
            o_ref[...] = acc_ref[...].astype(BF16)

    in_specs = [pl.BlockSpec((tt, tr), lambda r, t: (t, r)), pl.BlockSpec((tt, N), lambda r, t: (t, 0))]
    args = [a, b]
    if into is not None:
        in_specs.append(HBM_SPEC)
        args.append(into)
    return pl.pallas_call(
        body, out_shape=jax.ShapeDtypeStruct((rows, N), BF16), grid=(R // tr, nt),
        in_specs=in_specs, out_specs=pl.BlockSpec((tr, N), lambda r, t: (row_tile + r, 0)),
        input_output_aliases={} if into is None else {2: 0},
        scratch_shapes=[pltpu.VMEM((tr, N), F32)],
        compiler_params=_params("parallel", "arbitrary"), name=name)(*args)


def _rms_fwd(x, gain, name):
    T = x.shape[0]
    tm = _pick_tile(T, 512, 16)

    def body(x_ref, g_ref, u_ref):
        xv = x_ref[...]
        rstd = lax.rsqrt(jnp.mean(xv * xv, axis=-1, keepdims=True) + NORM_EPS)
        u_ref[...] = (xv * rstd * g_ref[...]).astype(BF16)

    return pl.pallas_call(
        body, out_shape=jax.ShapeDtypeStruct((T, D_MODEL), BF16), grid=(T // tm,),
        in_specs=[pl.BlockSpec((tm, D_MODEL), lambda i: (i, 0)), pl.BlockSpec((1, D_MODEL), lambda i: (0, 0))],
        out_specs=pl.BlockSpec((tm, D_MODEL), lambda i: (i, 0)),
        compiler_params=_params("parallel"), name=name)(x, gain)


def _rms_bwd(x, gain, du, dres, name):
    T = x.shape[0]
    tm = _pick_tile(T, 512, 16)

    def body(x_ref, g_ref, du_ref, dres_ref, dx_ref, dxb_ref, dg_ref):
        @pl.when(pl.program_id(0) == 0)
        def _():
            dg_ref[...] = jnp.zeros_like(dg_ref)

        xv = x_ref[...]
        rstd = lax.rsqrt(jnp.mean(xv * xv, axis=-1, keepdims=True) + NORM_EPS)
        n = xv * rstd
        du = du_ref[...]
        dg_ref[...] += jnp.sum(du * n, axis=0, keepdims=True)
        dn = du * g_ref[...]
        dx = dres_ref[...] + rstd * (dn - n * jnp.mean(dn * n, axis=-1, keepdims=True))
        dx_ref[...] = dx
        dxb_ref[...] = dx.astype(BF16)

    row = pl.BlockSpec((tm, D_MODEL), lambda i: (i, 0))
    vec = pl.BlockSpec((1, D_MODEL), lambda i: (0, 0))
    return pl.pallas_call(
        body,
        out_shape=(jax.ShapeDtypeStruct((T, D_MODEL), F32), jax.ShapeDtypeStruct((T, D_MODEL), BF16),
                   jax.ShapeDtypeStruct((1, D_MODEL), F32)),
        grid=(T // tm,), in_specs=[row, vec, row, row], out_specs=(row, row, vec),
        compiler_params=_params("arbitrary"), name=name)(x, gain, du, dres)


def _loss_head(h, target, gain, name):
    T = h.shape[0]
    tm = _pick_tile(T, 512, 16)
    inv_f = 1.0 / D_MODEL

    def body(h_ref, t_ref, g_ref, dh_ref, dhb_ref, dg_ref, loss_ref):
        @pl.when(pl.program_id(0) == 0)
        def _():
            dg_ref[...] = jnp.zeros_like(dg_ref)
            loss_ref[...] = jnp.zeros_like(loss_ref)

        hv = h_ref[...]
        g = g_ref[...]
        rstd = lax.rsqrt(jnp.mean(hv * hv, axis=-1, keepdims=True) + NORM_EPS)
        n = hv * rstd
        err = n * g - t_ref[...]
        loss_ref[...] += (0.5 * inv_f) * jnp.sum(err * err, axis=0, keepdims=True)
        dy = err * inv_f
        dg_ref[...] += jnp.sum(dy * n, axis=0, keepdims=True)
        dn = dy * g
        dh = rstd * (dn - n * jnp.mean(dn * n, axis=-1, keepdims=True))
        dh_ref[...] = dh
        dhb_ref[...] = dh.astype(BF16)

    row = pl.BlockSpec((tm, D_MODEL), lambda i: (i, 0))
    vec = pl.BlockSpec((1, D_MODEL), lambda i: (0, 0))
    return pl.pallas_call(
        body,
        out_shape=(jax.ShapeDtypeStruct((T, D_MODEL), F32), jax.ShapeDtypeStruct((T, D_MODEL), BF16),
                   jax.ShapeDtypeStruct((1, D_MODEL), F32), jax.ShapeDtypeStruct((1, D_MODEL), F32)),
        grid=(T // tm,), in_specs=[row, row, vec], out_specs=(row, row, vec, vec),
        compiler_params=_params("arbitrary"), name=name)(h, target, gain)


FFN_TILE = 256


def _ffn_in(h, gain, w_in, name):
    T = h.shape[0]
    tm = _pick_tile(T, 1024, 16)
    tn = FFN_TILE
    nj = D_FF // tn

    def body(h_ref, g_ref, wg_ref, wu_ref, n_ref, gate_ref, up_ref, a_ref):
        @pl.when(pl.program_id(1) == 0)
        def _():
            hv = h_ref[...]
            rstd = lax.rsqrt(jnp.mean(hv * hv, axis=-1, keepdims=True) + NORM_EPS)
            n_ref[...] = (hv * rstd * g_ref[...]).astype(BF16)

        n = n_ref[...]
        gate = _dot(n, wg_ref[...], NT)
        up = _dot(n, wu_ref[...], NT)
        gate_ref[...] = gate.astype(BF16)
        up_ref[...] = up.astype(BF16)
        a_ref[...] = (gate * _sigmoid(gate) * up).astype(BF16)

    row = pl.BlockSpec((tm, D_MODEL), lambda i, j: (i, 0))
    tile = pl.BlockSpec((tm, tn), lambda i, j: (i, j))
    wide = jax.ShapeDtypeStruct((T, D_FF), BF16)
    return pl.pallas_call(
        body, out_shape=(jax.ShapeDtypeStruct((T, D_MODEL), BF16), wide, wide, wide), grid=(T // tm, nj),
        in_specs=[row, pl.BlockSpec((1, D_MODEL), lambda i, j: (0, 0)),
                  pl.BlockSpec((tn, D_MODEL), lambda i, j: (j, 0)),
                  pl.BlockSpec((tn, D_MODEL), lambda i, j: (nj + j, 0))],
        out_specs=(row, tile, tile, tile),
        compiler_params=_params("parallel", "arbitrary"), name=name)(h, gain, w_in, w_in)


def _ffn_down_dx(dhb, w_down, gate, up, name):
    T = dhb.shape[0]
    tm = _pick_tile(T, 1024, 16)
    tn = FFN_TILE

    def body(dh_ref, w_ref, gate_ref, up_ref, dgate_ref, dup_ref):
        da = _dot(dh_ref[...], w_ref[...], NT)
        gate = gate_ref[...].astype(F32)
        sg = _sigmoid(gate)
        dgate_ref[...] = (da * up_ref[...].astype(F32) * (sg * (1.0 + gate * (1.0 - sg)))).astype(BF16)
        dup_ref[...] = (da * gate * sg).astype(BF16)

    tile = pl.BlockSpec((tm, tn), lambda i, j: (i, j))
    wide = jax.ShapeDtypeStruct((T, D_FF), BF16)
    return pl.pallas_call(
        body, out_shape=(wide, wide), grid=(T // tm, D_FF // tn),
        in_specs=[pl.BlockSpec((tm, D_MODEL), lambda i, j: (i, 0)),
                  pl.BlockSpec((tn, D_MODEL), lambda i, j: (j, 0)), tile, tile],
        out_specs=(tile, tile), compiler_params=_params("parallel", "arbitrary"), name=name)(dhb, w_down, gate, up)


def _tri(n, lower):
    r = lax.broadcasted_iota(jnp.int32, (n, n), 0)
    c = lax.broadcasted_iota(jnp.int32, (n, n), 1)
    return (c <= r) if lower else (c >= r)


def _running_sum(x, lower):
    n = x.shape[0]
    tri = _tri(n, lower).astype(F32)
    return lax.dot_general(tri, x, NN, precision=lax.Precision.HIGHEST, preferred_element_type=F32)


def _hgrn_gates(q_raw, f_raw, lb):
    C = q_raw.shape[0]
    sig_f = _sigmoid(f_raw)
    forget = lb + (1.0 - lb) * sig_f
    key = 1.0 - forget
    log_f = jnp.log(forget)
    b = _running_sum(log_f, True)
    first_half = lax.broadcasted_iota(jnp.int32, log_f.shape, 0) < C // 2
    r = jnp.sum(jnp.where(first_half, log_f, 0.0), axis=0, keepdims=True)
    b_last = jnp.sum(log_f, axis=0, keepdims=True)
    e_a = jnp.exp(jnp.minimum(b - r, HGRN_EXP_CLAMP))
    e_b = jnp.exp(jnp.minimum(r - b, HGRN_EXP_CLAMP))
    e_q = jnp.exp(b)
    e_k = jnp.exp(b_last - b)
    sig_q = _sigmoid(q_raw)
    query = q_raw * sig_q
    return dict(sig_f=sig_f, forget=forget, sig_q=sig_q, e_a=e_a, e_b=e_b, e_q=e_q, e_k=e_k,
                e_last=jnp.exp(b_last), q_a=query * e_a, k_b=key * e_b, q_hat=query * e_q, k_til=key * e_k)


def _hgrn_fwd(proj, lb, gain, name):
    T = proj.shape[0]
    C = HGRN_CHUNK
    H, HD = HGRN_HEADS, HGRN_DIM

    def body(q_ref, f_ref, i_ref, g_ref, lb_ref, gain_ref, og_ref, o_ref, st_ref, s_scr):
        @pl.when(pl.program_id(0) == 0)
        def _():
            s_scr[...] = jnp.zeros_like(s_scr)

        st_ref[0] = s_scr[...]
        gt = _hgrn_gates(q_ref[...], f_ref[...], lb_ref[...])
        causal = _tri(C, True)
        gain_v = gain_ref[...]
        for h in range(H):
            sl = slice(h * HD, (h + 1) * HD)
            v = i_ref[:, sl].astype(BF16)
            p = jnp.where(causal, _dot(gt["q_a"][:, sl].astype(BF16), gt["k_b"][:, sl].astype(BF16), NT), 0.0)
            s_t = s_scr[h]
            o = _dot(p.astype(BF16), v, NN) + _dot(gt["q_hat"][:, sl].astype(BF16), s_t.astype(BF16), NT)
            s_scr[h] = gt["e_last"][:, sl] * s_t + _dot(v, gt["k_til"][:, sl].astype(BF16), TN)
            o_ref[:, sl] = o
            rstd = lax.rsqrt(jnp.mean(o * o, axis=-1, keepdims=True) + NORM_EPS)
            g_raw = g_ref[:, sl]
            og_ref[:, sl] = (o * rstd * gain_v * (g_raw * _sigmoid(g_raw))).astype(BF16)

    col = lambda j: pl.BlockSpec((C, D_MODEL), lambda c: (c, j))
    row = pl.BlockSpec((C, D_MODEL), lambda c: (c, 0))
    return pl.pallas_call(
        body,
        out_shape=(jax.ShapeDtypeStruct((T, D_MODEL), BF16), jax.ShapeDtypeStruct((T, D_MODEL), F32),
                   jax.ShapeDtypeStruct((T // C, H, HD, HD), F32)),
        grid=(T // C,),
        in_specs=[col(0), col(1), col(2), col(3), pl.BlockSpec((1, D_MODEL), lambda c: (0, 0)),
                  pl.BlockSpec((1, HD), lambda c: (0, 0))],
        out_specs=(row, row, pl.BlockSpec((1, H, HD, HD), lambda c: (c, 0, 0, 0))),
        scratch_shapes=[pltpu.VMEM((H, HD, HD), F32)],
        compiler_params=_params("arbitrary"), name=name)(proj, proj, proj, proj, lb, gain)


def _hgrn_bwd(proj, o_pre, d_og, states, lb, gain, name):
    T = proj.shape[0]
    C = HGRN_CHUNK
    H, HD = HGRN_HEADS, HGRN_DIM
    NC = T // C

    def body(q_ref, f_ref, i_ref, g_ref, o_ref, dog_ref, st_ref, lb_ref, gain_ref,
             dproj_ref, dlb_ref, dgain_ref, ds_scr, dq_scr, dk_scr, db_scr):
        @pl.when(pl.program_id(0) == 0)
        def _():
            ds_scr[...] = jnp.zeros_like(ds_scr)
            dlb_ref[...] = jnp.zeros_like(dlb_ref)
            dgain_ref[...] = jnp.zeros_like(dgain_ref)

        lbv = lb_ref[...]
        q_raw = q_ref[...]
        gt = _hgrn_gates(q_raw, f_ref[...], lbv)
        causal = _tri(C, True)
        last_row = lax.broadcasted_iota(jnp.int32, (C, HD), 0) == C - 1
        gain_v = gain_ref[...]
        dgain = jnp.zeros((1, HD), F32)
        for h in range(H):
            sl = slice(h * HD, (h + 1) * HD)
            o = o_ref[:, sl]
            rstd = lax.rsqrt(jnp.mean(o * o, axis=-1, keepdims=True) + NORM_EPS)
            n = o * rstd
            g_raw = g_ref[:, sl]
            sg = _sigmoid(g_raw)
            d_out = dog_ref[:, sl]
            dproj_ref[:, 3 * D_MODEL + h * HD:3 * D_MODEL + (h + 1) * HD] = (
                d_out * n * gain_v * (sg * (1.0 + g_raw * (1.0 - sg)))).astype(BF16)
            dy = d_out * (g_raw * sg)
            dgain = dgain + jnp.sum(dy * n, axis=0, keepdims=True)
            dn = dy * gain_v
            do = (rstd * (dn - n * jnp.mean(dn * n, axis=-1, keepdims=True))).astype(BF16)
            q_a, k_b = gt["q_a"][:, sl], gt["k_b"][:, sl]
            q_hat, k_til = gt["q_hat"][:, sl], gt["k_til"][:, sl]
            q_ab, k_bb = q_a.astype(BF16), k_b.astype(BF16)
            v = i_ref[:, sl].astype(BF16)
            s_t = st_ref[0, h]
            ds_t = ds_scr[h]
            ds_b = ds_t.astype(BF16)
            e_last = gt["e_last"][:, sl]
            p = jnp.where(causal, _dot(q_ab, k_bb, NT), 0.0).astype(BF16)
            dp = jnp.where(causal, _dot(do, v, NT), 0.0).astype(BF16)
            dv = _dot(p, do, TN) + _dot(k_til.astype(BF16), ds_b, NT)
            dq_a = _dot(dp, k_bb, NN)
            dk_b = _dot(dp, q_ab, TN)
            dq_hat = _dot(do, s_t.astype(BF16), NN)
            dk_til = _dot(v, ds_b, NN)
            ds_scr[h] = _dot(do, q_hat.astype(BF16), TN) + e_last * ds_t
            db_last = jnp.sum(ds_t * e_last * s_t, axis=0, keepdims=True) + jnp.sum(
                dk_til * k_til, axis=0, keepdims=True)
            dproj_ref[:, 2 * D_MODEL + h * HD:2 * D_MODEL + (h + 1) * HD] = dv.astype(BF16)
            dq_scr[:, sl] = dq_a * gt["e_a"][:, sl] + dq_hat * gt["e_q"][:, sl]
            dk_scr[:, sl] = dk_b * gt["e_b"][:, sl] + dk_til * gt["e_k"][:, sl]
            db = dq_a * q_ab.astype(F32) + dq_hat * q_hat - dk_b * k_bb.astype(F32) - dk_til * k_til
            db_scr[:, sl] = db + jnp.where(last_row, db_last, 0.0)
        dgain_ref[...] += dgain
        dlogf = _running_sum(db_scr[...], False)
        sig_f, forget, sig_q = gt["sig_f"], gt["forget"], gt["sig_q"]
        dforget = dlogf / forget - dk_scr[...]
        dproj_ref[:, D_MODEL:2 * D_MODEL] = (dforget * (1.0 - lbv) * sig_f * (1.0 - sig_f)).astype(BF16)
        dlb_ref[...] += jnp.sum(dforget * (1.0 - sig_f), axis=0, keepdims=True)
        dproj_ref[:, 0:D_MODEL] = (dq_scr[...] * (sig_q * (1.0 + q_raw * (1.0 - sig_q)))).astype(BF16)

    col = lambda j: pl.BlockSpec((C, D_MODEL), lambda c: (NC - 1 - c, j))
    row = pl.BlockSpec((C, D_MODEL), lambda c: (NC - 1 - c, 0))
    return pl.pallas_call(
        body,
        out_shape=(jax.ShapeDtypeStruct((T, 4 * D_MODEL), BF16), jax.ShapeDtypeStruct((1, D_MODEL), F32),
                   jax.ShapeDtypeStruct((1, HD), F32)),
        grid=(NC,),
        in_specs=[col(0), col(1), col(2), col(3), row, row,
                  pl.BlockSpec((1, H, HD, HD), lambda c: (NC - 1 - c, 0, 0, 0)),
                  pl.BlockSpec((1, D_MODEL), lambda c: (0, 0)), pl.BlockSpec((1, HD), lambda c: (0, 0))],
        out_specs=(pl.BlockSpec((C, 4 * D_MODEL), lambda c: (NC - 1 - c, 0)),
                   pl.BlockSpec((1, D_MODEL), lambda c: (0, 0)), pl.BlockSpec((1, HD), lambda c: (0, 0))),
        scratch_shapes=[pltpu.VMEM((H, HD, HD), F32), pltpu.VMEM((C, D_MODEL), F32),
                        pltpu.VMEM((C, D_MODEL), F32), pltpu.VMEM((C, D_MODEL), F32)],
        compiler_params=_params("arbitrary"), name=name)(proj, proj, proj, proj, o_pre, d_og, states, lb, gain)


def _attn_masks():
    r = lax.broadcasted_iota(jnp.int32, (ATTN_BLOCK, ATTN_BLOCK), 0)
    c = lax.broadcasted_iota(jnp.int32, (ATTN_BLOCK, ATTN_BLOCK), 1)
    return c >= r, c <= r


def _attn_fwd(qkv, dilation, name):
    T = qkv.shape[0]
    nb = T // dilation // ATTN_BLOCK
    W = ATTN_GROUP_WIDTH
    scale = ATTN_DIM ** -0.5

    def body(q_ref, kp_ref, kc_ref, vp_ref, vc_ref, o_ref, lse_ref):
        no_prev = jnp.where(pl.program_id(1) > 0, 0.0, NEG_BIG)
        m_prev, m_cur = _attn_masks()
        for h in range(ATTN_GROUP_HEADS):
            sl = slice(h * ATTN_DIM, (h + 1) * ATTN_DIM)
            q = q_ref[:, sl]
            s_p = jnp.where(m_prev, _dot(q, kp_ref[:, sl], NT) * scale + no_prev, NEG_BIG)
            s_c = jnp.where(m_cur, _dot(q, kc_ref[:, sl], NT) * scale, NEG_BIG)
            m = jnp.maximum(jnp.max(s_p, axis=-1, keepdims=True), jnp.max(s_c, axis=-1, keepdims=True))
            p_p = jnp.exp(s_p - m)
            p_c = jnp.exp(s_c - m)
            l = jnp.sum(p_p, axis=-1, keepdims=True) + jnp.sum(p_c, axis=-1, keepdims=True)
            acc = _dot(p_p.astype(BF16), vp_ref[:, sl], NN) + _dot(p_c.astype(BF16), vc_ref[:, sl], NN)
            o_ref[:, sl] = acc / l
            lse_ref[:, sl] = jnp.broadcast_to(m + jnp.log(l), (ATTN_BLOCK, ATTN_DIM))

    blk = lambda col, prev: pl.BlockSpec(
        (ATTN_BLOCK, W), lambda s, n: (s * nb + (jnp.maximum(n - 1, 0) if prev else n), col))
    out = pl.BlockSpec((ATTN_BLOCK, W), lambda s, n: (s * nb + n, 0))
    return pl.pallas_call(
        body, out_shape=(jax.ShapeDtypeStruct((T, W), F32),) * 2, grid=(dilation, nb),
        in_specs=[blk(0, False), blk(1, True), blk(1, False), blk(2, True), blk(2, False)],
        out_specs=(out, out), compiler_params=_params("parallel", "arbitrary"), name=name)(qkv, qkv, qkv, qkv, qkv)


def _attn_bwd(qkv, d_out, lse, delta, cos, sin, dilation, name):
    T = qkv.shape[0]
    nb = T // dilation // ATTN_BLOCK
    W = ATTN_GROUP_WIDTH
    scale = ATTN_DIM ** -0.5

    def unrope(x, cos_v, sin_v):
        return x * cos_v + pltpu.roll(x * sin_v, ATTN_DIM // 2, 1)

    def body(q_ref, kp_ref, kc_ref, vp_ref, vc_ref, do_ref, lse_ref, dl_ref, cos_ref, sin_ref,
             out_ref, dq_scr, dk_scr, dv_scr):
        n = pl.program_id(1)
        cos_v, sin_v = cos_ref[...], sin_ref[...]

        @pl.when(n > 0)
        def _():
            for h in range(ATTN_GROUP_HEADS):
                sl = slice(h * ATTN_DIM, (h + 1) * ATTN_DIM)
                out_ref[:, sl] = unrope(dq_scr[:, sl], cos_v, sin_v).astype(BF16)

        @pl.when(n == nb)
        def _():
            for h in range(ATTN_GROUP_HEADS):
                sl = slice(h * ATTN_DIM, (h + 1) * ATTN_DIM)
                out_ref[:, W + h * ATTN_DIM:W + (h + 1) * ATTN_DIM] = unrope(dk_scr[:, sl], cos_v, sin_v).astype(BF16)
                out_ref[:, 2 * W + h * ATTN_DIM:2 * W + (h + 1) * ATTN_DIM] = dv_scr[:, sl].astype(BF16)

        @pl.when(n == 0)
        def _():
            dk_scr[...] = jnp.zeros_like(dk_scr)
            dv_scr[...] = jnp.zeros_like(dv_scr)

        @pl.when(n < nb)
        def _():
            has_prev = n > 0
            no_prev = jnp.where(has_prev, 0.0, NEG_BIG)
            m_prev, m_cur = _attn_masks()
            for h in range(ATTN_GROUP_HEADS):
                sl = slice(h * ATTN_DIM, (h + 1) * ATTN_DIM)
                q, k_p, k_c, v_p, v_c = q_ref[:, sl], kp_ref[:, sl], kc_ref[:, sl], vp_ref[:, sl], vc_ref[:, sl]
                do = do_ref[:, sl]
                lse_v, dl_v = lse_ref[:, sl], dl_ref[:, sl]
                p_p = jnp.where(m_prev, jnp.exp(_dot(q, k_p, NT) * scale - lse_v + no_prev), 0.0)
                p_c = jnp.where(m_cur, jnp.exp(_dot(q, k_c, NT) * scale - lse_v), 0.0)
                ds_p = (p_p * (_dot(do, v_p, NT) - dl_v) * scale).astype(BF16)
                ds_c = (p_c * (_dot(do, v_c, NT) - dl_v) * scale).astype(BF16)
                dk_prev = dk_scr[:, sl] + _dot(ds_p, q, TN)
                dv_prev = dv_scr[:, sl] + _dot(p_p.astype(BF16), do, TN)

                @pl.when(has_prev)
                def _():
                    out_ref[:, W + h * ATTN_DIM:W + (h + 1) * ATTN_DIM] = unrope(dk_prev, cos_v, sin_v).astype(BF16)
                    out_ref[:, 2 * W + h * ATTN_DIM:2 * W + (h + 1) * ATTN_DIM] = dv_prev.astype(BF16)

                dq_scr[:, sl] = _dot(ds_p, k_p, NN) + _dot(ds_c, k_c, NN)
                dk_scr[:, sl] = _dot(ds_c, q, TN)
                dv_scr[:, sl] = _dot(p_c.astype(BF16), do, TN)

    def cur(n):
        return jnp.minimum(n, nb - 1)

    def late(n):
        return jnp.maximum(n - 1, 0)

    qkv_blk = lambda col, prev: pl.BlockSpec(
        (ATTN_BLOCK, W), lambda s, n: (s * nb + (jnp.maximum(cur(n) - 1, 0) if prev else cur(n)), col))
    row = pl.BlockSpec((ATTN_BLOCK, W), lambda s, n: (s * nb + cur(n), 0))
    tab = pl.BlockSpec((ATTN_BLOCK, ATTN_DIM), lambda s, n: (s * nb + late(n), 0))
    return pl.pallas_call(
        body, out_shape=jax.ShapeDtypeStruct((T, 3 * W), BF16), grid=(dilation, nb + 1),
        in_specs=[qkv_blk(0, False), qkv_blk(1, True), qkv_blk(1, False), qkv_blk(2, True), qkv_blk(2, False),
                  row, row, row, tab, tab],
        out_specs=pl.BlockSpec((ATTN_BLOCK, 3 * W), lambda s, n: (s * nb + late(n), 0)),
        scratch_shapes=[pltpu.VMEM((ATTN_BLOCK, W), F32)] * 3,
        compiler_params=_params("parallel", "arbitrary"), name=name)(
            qkv, qkv, qkv, qkv, qkv, d_out, lse, delta, cos, sin)


def _attn_merge_fwd(outs, lses, name):
    T = outs[0].shape[0]
    W = ATTN_GROUP_WIDTH
    tm = _pick_tile(T, 512, 16)

    def body(o0, o1, o2, l0, l1, l2, oc_ref, lse_ref):
        ls = [l0[...], l1[...], l2[...]]
        m = jnp.maximum(jnp.maximum(ls[0], ls[1]), ls[2])
        tot = m + jnp.log(jnp.exp(ls[0] - m) + jnp.exp(ls[1] - m) + jnp.exp(ls[2] - m))
        lse_ref[...] = tot
        for g, o in enumerate((o0, o1, o2)):
            oc_ref[:, g * W:(g + 1) * W] = (o[...] * jnp.exp(ls[g] - tot)).astype(BF16)

    blk = pl.BlockSpec((tm, W), lambda i: (i, 0))
    return pl.pallas_call(
        body, out_shape=(jax.ShapeDtypeStruct((T, 3 * W), BF16), jax.ShapeDtypeStruct((T, W), F32)),
        grid=(T // tm,), in_specs=[blk] * 6, out_specs=(pl.BlockSpec((tm, 3 * W), lambda i: (i, 0)), blk),
        compiler_params=_params("parallel"), name=name)(*outs, *lses)


def _attn_merge_bwd(d_oc, oc, name):
    T = d_oc.shape[0]
    W = ATTN_GROUP_WIDTH
    tm = _pick_tile(T, 512, 16)

    def body(d_ref, o_ref, delta_ref, db_ref):
        d = d_ref[...]
        db_ref[...] = d.astype(BF16)
        prod = d * o_ref[...].astype(F32)
        for h in range(ATTN_GROUP_HEADS):
            tot = jnp.zeros((tm, 1), F32)
            for g in range(3):
                lo = g * W + h * ATTN_DIM
                tot = tot + jnp.sum(prod[:, lo:lo + ATTN_DIM], axis=-1, keepdims=True)
            delta_ref[:, h * ATTN_DIM:(h + 1) * ATTN_DIM] = jnp.broadcast_to(tot, (tm, ATTN_DIM))

    wide = pl.BlockSpec((tm, 3 * W), lambda i: (i, 0))
    return pl.pallas_call(
        body, out_shape=(jax.ShapeDtypeStruct((T, W), F32), jax.ShapeDtypeStruct((T, 3 * W), BF16)),
        grid=(T // tm,), in_specs=[wide, wide], out_specs=(pl.BlockSpec((tm, W), lambda i: (i, 0)), wide),
        compiler_params=_params("parallel"), name=name)(d_oc, oc)


def _to_residues(x, d):
    if d == 1:
        return x
    T, C = x.shape
    return x.reshape(T // d, d, C).transpose(1, 0, 2).reshape(T, C)


def _from_residues(x, d):
    if d == 1:
        return x
    T, C = x.shape
    return x.reshape(d, T // d, C).transpose(1, 0, 2).reshape(T, C)


def _rope_tables(T):
    inv_freq = 1.0 / (ROPE_THETA ** (jnp.arange(0, ATTN_DIM, 2, dtype=F32) / ATTN_DIM))
    ang = jnp.arange(T, dtype=F32)[:, None] * inv_freq[None, :]
    cos, sin = jnp.cos(ang), jnp.sin(ang)
    return jnp.concatenate([cos, cos], axis=1), jnp.concatenate([-sin, sin], axis=1)


WEIGHT_GROUPS = {"hgrn": ("hgrn_in", "hgrn_out"), "ffn0": ("ffn_in0", "ffn_down0"),
                 "attn": ("qkv", "attn_out"), "ffn1": ("ffn_in1", "ffn_down1")}


def _local_step(x, target, norm_mix, norm_ffn, lb, out_gain, final_gain, fetch, publish):
    T = x.shape[0]
    ident = lambda j: j
    g_mix = [norm_mix[0:1], norm_mix[1:2]]
    g_ffn = [norm_ffn[0:1], norm_ffn[1:2]]
    w = {}

    def ffn_fwd(h, layer, before):
        w.update(fetch(f"ffn{layer}", [before]))
        n, gate, up, a = _ffn_in(h, g_ffn[layer], w[f"ffn_in{layer}"], f"ffn{layer}_in")
        out = _mm_nn(a, w[f"ffn_down{layer}"], h, tk=1408, w_block=ident, name=f"ffn{layer}_down")
        return out, (n, gate, up, a)

    def ffn_bwd(h, saved, dh, dhb, layer):
        n, gate, up, a = saved
        dgate, dup = _ffn_down_dx(dhb, w[f"ffn_down{layer}"], gate, up, f"ffn{layer}_down_dx")
        grad_in = _mm_tn(dgate, n, tr=1408, name=f"ffn{layer}_in_dw_gate", rows=2 * D_FF)
        grad_in = _mm_tn(dup, n, tr=1408, name=f"ffn{layer}_in_dw_up", into=grad_in, row_tile=2, rows=2 * D_FF)
        grads = {f"ffn_down{layer}": _mm_tn(a, dhb, tr=1408, name=f"ffn{layer}_down_dw"), f"ffn_in{layer}": grad_in}
        zero = publish(f"ffn{layer}", grads)
        dn = _mm_nn2(dgate, dup, w[f"ffn_in{layer}"], tk=1408, name=f"ffn{layer}_in_dx")
        return _rms_bwd(h, g_ffn[layer] + zero, dn, dh, f"ffn{layer}_norm_bwd")

    u0 = _rms_fwd(x, g_mix[0], "hgrn_norm")
    w.update(fetch("hgrn", [u0]))
    proj = _mm_nt(u0, w["hgrn_in"], n_out=4 * D_MODEL, tn=1024, w_block=ident, out_dtype=F32, name="hgrn_in")
    og, o_pre, states = _hgrn_fwd(proj, lb, out_gain, "hgrn_fwd")
    h1 = _mm_nn(og, w["hgrn_out"], x, tk=1024, w_block=ident, name="hgrn_out")
    h2, ffn0 = ffn_fwd(h1, 0, og)

    u1 = _rms_fwd(h2, g_mix[1], "attn_norm")
    w.update(fetch("attn", [u1]))
    cos, sin = _rope_tables(T)
    u1_g, qkv_g, cos_g, sin_g, outs, lses = [], [], [], [], [], []
    for g, d in enumerate(ATTN_DILATIONS):
        u1_g.append(_to_residues(u1, d))
        cos_g.append(_to_residues(cos, d))
        sin_g.append(_to_residues(sin, d))
        qkv_g.append(_mm_nt(u1_g[g], w["qkv"], n_out=ATTN_WIDTH, tn=ATTN_GROUP_WIDTH,
                            w_block=functools.partial(lambda j, g: 3 * j + g, g=g), out_dtype=BF16,
                            name=f"attn_qkv{g}", rope=(cos_g[g], sin_g[g], 2)))
        o_g, lse_g = _attn_fwd(qkv_g[g], d, f"attn_fwd{g}")
        outs.append(_from_residues(o_g, d))
        lses.append(_from_residues(lse_g, d))
    oc, lse_all = _attn_merge_fwd(outs, lses, "attn_merge")
    h3 = _mm_nn(oc, w["attn_out"], h2, tk=ATTN_WIDTH, w_block=ident, name="attn_out")
    h4, ffn1 = ffn_fwd(h3, 1, oc)

    dh4, dh4b, d_final, loss_part = _loss_head(h4, target, final_gain, "loss_head")
    dh3, dh3b, d_ffn1 = ffn_bwd(h3, ffn1, dh4, dh4b, 1)

    d_oc = _mm_nt(dh3b, w["attn_out"], n_out=ATTN_WIDTH, tn=ATTN_WIDTH, w_block=ident, out_dtype=F32,
                  name="attn_out_dx")
    grad_attn_out = _mm_tn(oc, dh3b, tr=ATTN_WIDTH, name="attn_out_dw")
    delta, d_ocb = _attn_merge_bwd(d_oc, oc, "attn_merge_bwd")
    du1 = None
    qkv_pieces = []
    for g, d in enumerate(ATTN_DILATIONS):
        W = ATTN_GROUP_WIDTH
        dqkv = _attn_bwd(qkv_g[g], _to_residues(d_ocb[:, g * W:(g + 1) * W], d), _to_residues(lse_all, d),
                         _to_residues(delta, d), cos_g[g], sin_g[g], d, f"attn_bwd{g}")
        qkv_pieces.append(_mm_tn(dqkv, u1_g[g], tr=ATTN_WIDTH, name=f"attn_qkv_dw{g}"))
        du1_g = _mm_nn(dqkv, w["qkv"], None, tk=W, w_block=functools.partial(lambda k, g: 3 * k + g, g=g),
                       name=f"attn_qkv_dx{g}")
        du1_g = _from_residues(du1_g, d)
        du1 = du1_g if du1 is None else du1 + du1_g
    grad_qkv = jnp.stack([p.reshape(3, ATTN_GROUP_WIDTH, D_MODEL) for p in qkv_pieces], axis=1).reshape(
        3 * ATTN_WIDTH, D_MODEL)
    zero = publish("attn", {"qkv": grad_qkv, "attn_out": grad_attn_out})
    dh2, dh2b, d_mix1 = _rms_bwd(h2, g_mix[1] + zero, du1, dh3, "attn_norm_bwd")

    dh1, dh1b, d_ffn0 = ffn_bwd(h1, ffn0, dh2, dh2b, 0)

    d_og = _mm_nt(dh1b, w["hgrn_out"], n_out=D_MODEL, tn=D_MODEL, w_block=ident, out_dtype=F32, name="hgrn_out_dx")
    grad_hgrn_out = _mm_tn(og, dh1b, tr=D_MODEL, name="hgrn_out_dw")
    dproj, d_lb, d_out_gain = _hgrn_bwd(proj, o_pre, d_og, states, lb, out_gain, "hgrn_bwd")
    zero = publish("hgrn", {"hgrn_in": _mm_tn(dproj, u0, tr=1024, name="hgrn_in_dw"), "hgrn_out": grad_hgrn_out})
    du0 = _mm_nn(dproj, w["hgrn_in"], None, tk=2048, w_block=ident, name="hgrn_in_dx")
    dx, _, d_mix0 = _rms_bwd(x, g_mix[0] + zero, du0, dh1, "hgrn_norm_bwd")

    small = dict(norm_mix0=d_mix0, norm_mix1=d_mix1, norm_ffn0=d_ffn0, norm_ffn1=d_ffn1, lb=d_lb,
                 out_gain=d_out_gain, final=d_final, loss=loss_part)
    return dx, small


WEIGHT_NAMES = ("hgrn_in", "hgrn_out", "qkv", "attn_out", "ffn_in0", "ffn_in1", "ffn_down0", "ffn_down1")
MESH_IDS = pl.DeviceIdType.MESH
HBM_SPEC = pl.BlockSpec(memory_space=pl.ANY)


SEM_SPEC = pl.BlockSpec(memory_space=pltpu.SEMAPHORE)
LAND_SPEC = pl.BlockSpec(memory_space=pltpu.HBM)
N_PEERS = N_DEV - 1
PEER_OFFSETS = [(dx, dy, dc) for dx in (0, 1) for dy in (0, 1) for dc in (0, 1)][1:]


def _mesh_place():
    x, y, c = lax.axis_index("x"), lax.axis_index("y"), lax.axis_index("c")
    peers = []
    for dx, dy, dc in PEER_OFFSETS:
        px, py, pc = (1 - x if dx else x), (1 - y if dy else y), (1 - c if dc else c)
        peers.append(((px, py, pc), 4 * px + 2 * py + pc))
    return 4 * x + 2 * y + c, peers


def _start_copies(srcs, groups, scatter, name):
    nw, ng = len(srcs), len(groups)
    land_shapes = [(s.shape if scatter else (N_DEV,) + s.shape) for s in srcs]

    def body(*refs):
        src_refs, land_refs = refs[:nw], refs[nw:2 * nw]
        sems = refs[2 * nw:2 * nw + 2 * ng]
        local_sems = refs[-1]
        me, peers = _mesh_place()
        own = [pltpu.make_async_copy(src_refs[w].at[me] if scatter else src_refs[w], land_refs[w].at[me],
                                     local_sems.at[w]) for w in range(nw)]
        for cp in own:
            cp.start()
        for gi, group in enumerate(groups):
            for i, w in enumerate(group):
                for k, (peer, pid) in enumerate(peers):
                    pltpu.make_async_remote_copy(
                        src_ref=src_refs[w].at[pid] if scatter else src_refs[w], dst_ref=land_refs[w].at[me],
                        send_sem=sems[2 * gi].at[i * N_PEERS + k], recv_sem=sems[2 * gi + 1].at[i * N_PEERS + k],
                        device_id=peer, device_id_type=MESH_IDS).start()
        for cp in own:
            cp.wait()
        token_ref = refs[2 * nw + 2 * ng + 2 * nw]
        token_ref[...] = jnp.zeros_like(token_ref)

    sem_shapes = []
    for group in groups:
        sem_shapes += [pltpu.SemaphoreType.DMA((len(group) * N_PEERS,))] * 2
    out_shape = (sem_shapes + [pltpu.HBM(s.shape, s.dtype) for s in srcs]
                 + [pltpu.HBM(shape, s.dtype) for shape, s in zip(land_shapes, srcs)]
                 + [jax.ShapeDtypeStruct((8, 128), F32)])
    operands = [pltpu.with_memory_space_constraint(s, pltpu.HBM) for s in srcs]
    operands += [pltpu.with_memory_space_constraint(lax.empty(shape, s.dtype), pltpu.HBM)
                 for shape, s in zip(land_shapes, srcs)]
    res = pl.pallas_call(
        body, out_shape=out_shape, in_specs=[LAND_SPEC] * (2 * nw),
        out_specs=[SEM_SPEC] * (2 * ng) + [LAND_SPEC] * (2 * nw) + [pl.BlockSpec(memory_space=pltpu.VMEM)],
        input_output_aliases={i: 2 * ng + i for i in range(2 * nw)},
        scratch_shapes=[pltpu.SemaphoreType.DMA((nw,))],
        compiler_params=pltpu.CompilerParams(has_side_effects=pltpu.SideEffectType.DATAFLOW_SIDE_EFFECTING),
        name=name)(*operands)
    sems = [(res[2 * gi], res[2 * gi + 1]) for gi in range(ng)]
    return sems, list(res[2 * ng:2 * ng + nw]), list(res[2 * ng + nw:2 * ng + 2 * nw]), res[-1][0:1, 0:1]


def _wait_copies(sems, srcs, lands, after, scatter, name):
    n = len(srcs)

    def body(*refs):
        src_refs, land_refs = refs[:n], refs[n:2 * n]
        send_sems, recv_sems = refs[2 * n], refs[2 * n + 1]
        _, peers = _mesh_place()
        for i in range(n):
            for k, (peer, _) in enumerate(peers):
                copy = pltpu.make_async_remote_copy(
                    src_ref=src_refs[i].at[0] if scatter else src_refs[i], dst_ref=land_refs[i].at[0],
                    send_sem=send_sems.at[i * N_PEERS + k], recv_sem=recv_sems.at[i * N_PEERS + k],
                    device_id=peer, device_id_type=MESH_IDS)
                copy.wait_send()
                copy.wait_recv()

    arrays = list(srcs) + list(lands)
    res = pl.pallas_call(
        body, out_shape=[pltpu.HBM(a.shape, a.dtype) for a in arrays],
        in_specs=[LAND_SPEC] * (2 * n) + [SEM_SPEC] * 2 + [HBM_SPEC] * len(after),
        out_specs=[LAND_SPEC] * (2 * n), input_output_aliases={i: i for i in range(2 * n)},
        compiler_params=pltpu.CompilerParams(has_side_effects=pltpu.SideEffectType.DATAFLOW_SIDE_EFFECTING),
        name=name)(*arrays, sems[0], sems[1], *after)
    return list(res[n:])


def _gather_small(block, name):
    def body(in_ref, out_ref, send_sems, recv_sems, local_sem):
        me, peers = _mesh_place()
        own = pltpu.make_async_copy(in_ref, out_ref.at[me], local_sem)
        own.start()
        sends = [pltpu.make_async_remote_copy(
            src_ref=in_ref, dst_ref=out_ref.at[me], send_sem=send_sems.at[k], recv_sem=recv_sems.at[k],
            device_id=peer, device_id_type=MESH_IDS) for k, (peer, _) in enumerate(peers)]
        for cp in sends:
            cp.start()
        for cp in sends:
            cp.wait_recv()
        for cp in sends:
            cp.wait_send()
        own.wait()

    return pl.pallas_call(
        body, out_shape=jax.ShapeDtypeStruct((N_DEV,) + block.shape, block.dtype),
        in_specs=[HBM_SPEC], out_specs=HBM_SPEC,
        scratch_shapes=[pltpu.SemaphoreType.DMA((N_PEERS,)), pltpu.SemaphoreType.DMA((N_PEERS,)),
                        pltpu.SemaphoreType.DMA],
        name=name)(block)


def _sum_blocks(recv, name):
    rows = recv.shape[1]
    tr = _pick_tile(rows, 256, 16)

    def body(r_ref, g_ref):
        acc = r_ref[0].astype(F32)
        for j in range(1, N_DEV):
            acc = acc + r_ref[j].astype(F32)
        g_ref[...] = acc

    return pl.pallas_call(
        body, out_shape=jax.ShapeDtypeStruct((rows, D_MODEL), F32), grid=(rows // tr,),
        in_specs=[pl.BlockSpec((N_DEV, tr, D_MODEL), lambda i: (0, i, 0))],
        out_specs=pl.BlockSpec((tr, D_MODEL), lambda i: (i, 0)),
        compiler_params=_params("parallel"), name=name)(recv)


def _adamw_math(w, g, m, v):
    m_new = ADAM_B1 * m + (1.0 - ADAM_B1) * g
    v_new = ADAM_B2 * v + (1.0 - ADAM_B2) * (g * g)
    m_hat = m_new / (1.0 - ADAM_B1 ** ADAM_STEP)
    v_hat = v_new / (1.0 - ADAM_B2 ** ADAM_STEP)
    delta = -ADAM_LR * (m_hat / (jnp.sqrt(v_hat) + ADAM_EPS) + ADAM_WD * w)
    return delta, m_new, v_new


def _adamw(w, g, m, v, name):
    rows, cols = w.shape
    tr = _pick_tile(rows, 256, 8)

    def body(w_ref, g_ref, m_ref, v_ref, d_ref, mo_ref, vo_ref):
        d_ref[...], mo_ref[...], vo_ref[...] = _adamw_math(w_ref[...], g_ref[...], m_ref[...], v_ref[...])

    blk = pl.BlockSpec((tr, cols), lambda i: (i, 0))
    return pl.pallas_call(
        body, out_shape=(jax.ShapeDtypeStruct((rows, cols), F32),) * 3, grid=(rows // tr,),
        in_specs=[blk] * 4, out_specs=(blk,) * 3, compiler_params=_params("parallel"), name=name)(w, g, m, v)


ROW_MIX, ROW_FFN, ROW_LB, ROW_OUT_GAIN, ROW_FINAL = 0, 2, 4, 7, 8
PART_MIX, PART_FFN, PART_LB, PART_OUT_GAIN, PART_FINAL, PART_LOSS = 0, 2, 4, 5, 6, 7


def _small_update(parts_all, w, m, v, name):
    def body(p_ref, w_ref, m_ref, v_ref, g_ref, d_ref, mo_ref, vo_ref, loss_ref):
        def total(row, n=1):
            tot = p_ref[0, row:row + n, :]
            for j in range(1, N_DEV):
                tot = tot + p_ref[j, row:row + n, :]
            return tot

        logits = [w_ref[ROW_LB + i:ROW_LB + i + 1, :] for i in range(3)]
        mx = jnp.maximum(jnp.maximum(logits[0], logits[1]), logits[2])
        ex = [jnp.exp(l - mx) for l in logits]
        den = ex[0] + ex[1] + ex[2]
        prob = [e / den for e in ex]
        d_lb = total(PART_LB)
        g_ref[...] = jnp.zeros_like(g_ref)
        g_ref[ROW_MIX:ROW_MIX + 2, :] = total(PART_MIX, 2)
        g_ref[ROW_FFN:ROW_FFN + 2, :] = total(PART_FFN, 2)
        for i in range(3):
            g_ref[ROW_LB + i:ROW_LB + i + 1, :] = prob[i] * ((d_lb if i == 0 else 0.0) - prob[0] * d_lb)
        g_ref[ROW_OUT_GAIN:ROW_OUT_GAIN + 1, :] = total(PART_OUT_GAIN)
        g_ref[ROW_FINAL:ROW_FINAL + 1, :] = total(PART_FINAL)
        d_ref[...], mo_ref[...], vo_ref[...] = _adamw_math(w_ref[...], g_ref[...], m_ref[...], v_ref[...])
        loss_ref[...] = jnp.sum(total(PART_LOSS), axis=-1, keepdims=True)

    packed = jax.ShapeDtypeStruct((16, D_MODEL), F32)
    return pl.pallas_call(
        body, out_shape=(packed, packed, packed, packed, jax.ShapeDtypeStruct((1, 1), F32)),
        compiler_params=pltpu.CompilerParams(vmem_limit_bytes=VMEM_LIMIT), name=name)(parts_all, w, m, v)


def _pack_small(norm_mix, norm_ffn, lb_logits, out_gain, final):
    pad = jnp.zeros((1, D_MODEL - HGRN_DIM), F32)
    return jnp.concatenate([norm_mix, norm_ffn, lb_logits, jnp.concatenate([out_gain, pad], axis=1),
                            final.reshape(1, D_MODEL), jnp.zeros((16 - ROW_FINAL - 1, D_MODEL), F32)], axis=0)


def _unpack_small(p):
    return (p[ROW_MIX:ROW_MIX + 2], p[ROW_FFN:ROW_FFN + 2], p[ROW_LB:ROW_LB + 3],
            p[ROW_OUT_GAIN:ROW_OUT_GAIN + 1, :HGRN_DIM], p[ROW_FINAL])


def _lower_bound(lb_logits, name):
    def body(l_ref, o_ref):
        logits = [l_ref[i:i + 1, :] for i in range(3)]
        mx = jnp.maximum(jnp.maximum(logits[0], logits[1]), logits[2])
        ex = [jnp.exp(l - mx) for l in logits]
        o_ref[...] = ex[0] / (ex[0] + ex[1] + ex[2])

    return pl.pallas_call(body, out_shape=jax.ShapeDtypeStruct((1, D_MODEL), F32), name=name)(lb_logits)


def kernel(x, norm_mix, norm_ffn, hgrn_w_in, hgrn_lb_logits, hgrn_out_norm, hgrn_w_out, attn_w_qkv, attn_w_out, ffn_w_in, ffn_w_down, final_norm, loss_target, m_norm_mix, m_norm_ffn, m_hgrn_w_in, m_hgrn_lb_logits, m_hgrn_out_norm, m_hgrn_w_out, m_attn_w_qkv, m_attn_w_out, m_ffn_w_in, m_ffn_w_down, m_final_norm, v_norm_mix, v_norm_ffn, v_hgrn_w_in, v_hgrn_lb_logits, v_hgrn_out_norm, v_hgrn_w_out, v_attn_w_qkv, v_attn_w_out, v_ffn_w_in, v_ffn_w_down, v_final_norm):
    col_sharded = {"hgrn_in": hgrn_w_in[0], "qkv": attn_w_qkv[0], "ffn_in0": ffn_w_in[0], "ffn_in1": ffn_w_in[1]}
    row_sharded = {"hgrn_out": hgrn_w_out[0], "attn_out": attn_w_out[0], "ffn_down0": ffn_w_down[0],
                   "ffn_down1": ffn_w_down[1]}
    order = [n for group in WEIGHT_GROUPS.values() for n in group]
    shards = [(col_sharded[n].T if n in col_sharded else row_sharded[n]).astype(BF16) for n in order]
    index_groups = [[order.index(n) for n in group] for group in WEIGHT_GROUPS.values()]
    w_sems, w_srcs, w_lands, _ = _start_copies(shards, index_groups, False, "weights_gather_start")

    def fetch(group, after):
        gi = list(WEIGHT_GROUPS).index(group)
        idx = index_groups[gi]
        lands = _wait_copies(w_sems[gi], [w_srcs[i] for i in idx], [w_lands[i] for i in idx], after, False,
                             f"weights_gather_wait_{group}")
        return {n: land.reshape(-1, D_MODEL) for n, land in zip(WEIGHT_GROUPS[group], lands)}

    in_flight = {}

    def publish(group, grads):
        names = WEIGHT_GROUPS[group]
        parts = [grads[n].reshape(N_DEV, -1, D_MODEL) for n in names]
        sems, srcs, lands, zero = _start_copies(parts, [list(range(len(names)))], True, f"grads_send_start_{group}")
        in_flight[group] = (sems[0], srcs, lands)
        return zero

    lb = _lower_bound(hgrn_lb_logits, "hgrn_lower_bound")
    grad_x, small = _local_step(x[0], loss_target[0], norm_mix, norm_ffn, lb, hgrn_out_norm,
                                final_norm.reshape(1, D_MODEL), fetch, publish)

    pad = jnp.zeros((1, D_MODEL - HGRN_DIM), F32)
    small_part = jnp.concatenate(
        [small["norm_mix0"], small["norm_mix1"], small["norm_ffn0"], small["norm_ffn1"], small["lb"],
         jnp.concatenate([small["out_gain"], pad], axis=1), small["final"], small["loss"]], axis=0)
    small_all = _gather_small(small_part, "small_grads_gather")
    received = {}
    for group in ("ffn1", "attn", "ffn0", "hgrn"):
        sems, srcs, lands = in_flight[group]
        lands = _wait_copies(sems, srcs, lands, [small_all], True, f"grads_send_wait_{group}")
        received.update(zip(WEIGHT_GROUPS[group], lands))

    masters = {"hgrn_in": (hgrn_w_in[0], m_hgrn_w_in[0], v_hgrn_w_in[0]),
               "hgrn_out": (hgrn_w_out[0], m_hgrn_w_out[0], v_hgrn_w_out[0]),
               "qkv": (attn_w_qkv[0], m_attn_w_qkv[0], v_attn_w_qkv[0]),
               "attn_out": (attn_w_out[0], m_attn_w_out[0], v_attn_w_out[0]),
               "ffn_in0": (ffn_w_in[0], m_ffn_w_in[0], v_ffn_w_in[0]),
               "ffn_in1": (ffn_w_in[1], m_ffn_w_in[1], v_ffn_w_in[1]),
               "ffn_down0": (ffn_w_down[0], m_ffn_w_down[0], v_ffn_w_down[0]),
               "ffn_down1": (ffn_w_down[1], m_ffn_w_down[1], v_ffn_w_down[1])}
    res = {}
    for n in WEIGHT_NAMES:
        g = _sum_blocks(received[n], f"{n}_grad_sum")
        if n in col_sharded:
            g = g.T
        wv, mv, vv = masters[n]
        res[n] = (g,) + tuple(_adamw(wv, g, mv, vv, f"{n}_adamw"))

    def single(n):
        return [t[None] for t in res[n]]

    def pair(n):
        return [jnp.stack([a, b]) for a, b in zip(res[n + "0"], res[n + "1"])]

    big = dict(hgrn_w_in=single("hgrn_in"), hgrn_w_out=single("hgrn_out"), attn_w_qkv=single("qkv"),
               attn_w_out=single("attn_out"), ffn_w_in=pair("ffn_in"), ffn_w_down=pair("ffn_down"))

    w_small = _pack_small(norm_mix, norm_ffn, hgrn_lb_logits, hgrn_out_norm, final_norm)
    m_small = _pack_small(m_norm_mix, m_norm_ffn, m_hgrn_lb_logits, m_hgrn_out_norm, m_final_norm)
    v_small = _pack_small(v_norm_mix, v_norm_ffn, v_hgrn_lb_logits, v_hgrn_out_norm, v_final_norm)
    g_s, d_s, m_s, v_s, loss = _small_update(small_all, w_small, m_small, v_small, "small_update")
    small_out = [_unpack_small(t) for t in (g_s, d_s, m_s, v_s)]

    def group(i):
        s = small_out[i]
        return (s[0], s[1], big["hgrn_w_in"][i], s[2], s[3], big["hgrn_w_out"][i], big["attn_w_qkv"][i],
                big["attn_w_out"][i], big["ffn_w_in"][i], big["ffn_w_down"][i], s[4])

    return (loss.reshape(()), grad_x[None], *group(0), *group(1), *group(2), *group(3))
```

```python
import functools

import jax
import jax.numpy as jnp
from jax import lax
from jax.experimental import pallas as pl
from jax.experimental.pallas import tpu as pltpu

F32 = jnp.float32
BF16 = jnp.bfloat16

D_MODEL = 1024
N_DEV = 8
NORM_EPS = 1e-6

HGRN_HEADS = 8
HGRN_DIM = 128
HGRN_CHUNK = 64
HGRN_EXP_CLAMP = 60.0

ATTN_DIM = 128
ATTN_BLOCK = 128
ATTN_GROUP_HEADS = 4
ATTN_GROUP_WIDTH = ATTN_GROUP_HEADS * ATTN_DIM
ATTN_DILATIONS = (1, 4, 16)
ATTN_WIDTH = 3 * ATTN_GROUP_WIDTH
ROPE_THETA = 10000.0
NEG_BIG = -1e30

D_FF = 2816

ADAM_LR = 0.001
ADAM_B1 = 0.9
ADAM_B2 = 0.999
ADAM_EPS = 1e-08
ADAM_WD = 0.01
ADAM_STEP = 10

VMEM_LIMIT = 48 * 1024 * 1024

NT = (((1,), (1,)), ((), ()))
NN = (((1,), (0,)), ((), ()))
TN = (((0,), (0,)), ((), ()))


def _dot(a, b, dims):
    return lax.dot_general(a, b, dims, preferred_element_type=F32)


def _params(*sem):
    return pltpu.CompilerParams(dimension_semantics=sem, vmem_limit_bytes=VMEM_LIMIT)


def _pick_tile(n, cap, mult):
    best = None
    for t in range(mult, min(n, cap) + 1, mult):
        if n % t == 0:
            best = t
    assert best is not None, (n, cap, mult)
    return best


def _sigmoid(x):
    return 1.0 / (1.0 + jnp.exp(-x))


ROW_TILE = 512
COL_CHUNK = 512
GRAD_TILE = 256


def _whole(shape, index_map):
    return pl.BlockSpec(shape, index_map, pipeline_mode=pl.Buffered(1))


def _part_specs(parts, n_cols):
    return [_whole((rows, n_cols), functools.partial(lambda i, b: (b, 0), b=blk)) for _, rows, blk in parts]


def _mm_nt(a, w_parts, *, out_dtype, name, rope=None):
    M, K = a.shape
    tm = _pick_tile(M, ROW_TILE, 16)
    widths = [rows for _, rows, _ in w_parts]
    n_parts = len(w_parts)

    def body(*refs):
        a_ref, w_refs, o_ref = refs[0], refs[1:1 + n_parts], refs[-1]
        av = a_ref[...]
        off = 0
        for p, w_ref in enumerate(w_refs):
            for c0 in range(0, widths[p], COL_CHUNK):
                cw = min(COL_CHUNK, widths[p] - c0)
                acc = _dot(av, w_ref[c0:c0 + cw, :], NT)
                if rope is not None and p < rope[2]:
                    cos, sin = refs[1 + n_parts][...], refs[2 + n_parts][...]
                    for h0 in range(0, cw, ATTN_DIM):
                        xh = acc[:, h0:h0 + ATTN_DIM]
                        rot = pltpu.roll(xh, ATTN_DIM // 2, 1)
                        o_ref[:, off + c0 + h0:off + c0 + h0 + ATTN_DIM] = (xh * cos + rot * sin).astype(out_dtype)
                else:
                    o_ref[:, off + c0:off + c0 + cw] = acc.astype(out_dtype)
            off += widths[p]

    in_specs = [pl.BlockSpec((tm, K), lambda i: (i, 0))] + _part_specs(w_parts, K)
    args = [a] + [w for w, _, _ in w_parts]
    if rope is not None:
        in_specs += [pl.BlockSpec((tm, ATTN_DIM), lambda i: (i, 0))] * 2
        args += [rope[0], rope[1]]
    return pl.pallas_call(
        body, out_shape=jax.ShapeDtypeStruct((M, sum(widths)), out_dtype), grid=(M // tm,),
        in_specs=in_specs, out_specs=pl.BlockSpec((tm, sum(widths)), lambda i: (i, 0)),
        compiler_params=_params("parallel"), name=name)(*args)


def _mm_nn(a_list, w_parts_list, resid, *, name):
    M = a_list[0].shape[0]
    tm = _pick_tile(M, ROW_TILE, 16)
    n_a = len(a_list)
    flat_parts = [p for parts in w_parts_list for p in parts]

    def body(*refs):
        a_refs, w_refs, o_ref = refs[:n_a], refs[n_a:n_a + len(flat_parts)], refs[-1]
        acc = None
        wi = 0
        for a_ref, parts in zip(a_refs, w_parts_list):
            off = 0
            for _, rows, _ in parts:
                term = _dot(a_ref[:, off:off + rows], w_refs[wi][...], NN)
                acc = term if acc is None else acc + term
                off += rows
                wi += 1
        if resid is not None:
            acc = acc + refs[-2][...]
        o_ref[...] = acc

    row = pl.BlockSpec((tm, D_MODEL), lambda i: (i, 0))
    in_specs = [pl.BlockSpec((tm, a.shape[1]), lambda i: (i, 0)) for a in a_list] + _part_specs(flat_parts, D_MODEL)
    args = list(a_list) + [w for w, _, _ in flat_parts]
    if resid is not None:
        in_specs.append(row)
        args.append(resid)
    return pl.pallas_call(
        body, out_shape=jax.ShapeDtypeStruct((M, D_MODEL), F32), grid=(M // tm,),
        in_specs=in_specs, out_specs=row, compiler_params=_params("parallel"), name=name)(*args)


def _mm_tn(a, b, *, name, into=None, row_tile=0, rows=None):
    T, R = a.shape
    N = b.shape[1]
    tr = GRAD_TILE
    rows = R if rows is None else rows

    def body(a_ref, b_ref, *refs):
        refs[-1][...] = _dot(a_ref[...], b_ref[...], TN).astype(BF16)

    in_specs = [pl.BlockSpec((T, tr), lambda r: (0, r)), _whole((T, N), lambda r: (0, 0))]
    args = [a, b]
    if into is not None:
        in_specs.append(HBM_SPEC)
        args.append(into)
    return pl.pallas_call(
        body, out_shape=jax.ShapeDtypeStruct((rows, N), BF16), grid=(R // tr,),
        in_specs=in_specs, out_specs=pl.BlockSpec((tr, N), lambda r: (row_tile + r, 0)),
        input_output_aliases={} if into is None else {2: 0},
        compiler_params=_params("parallel"), name=name)(*args)


def _rms_fwd(x, gain, name):
    T = x.shape[0]
    tm = _pick_tile(T, 512, 16)

    def body(x_ref, g_ref, u_ref):
        xv = x_ref[...]
        rstd = lax.rsqrt(jnp.mean(xv * xv, axis=-1, keepdims=True) + NORM_EPS)
        u_ref[...] = (xv * rstd * g_ref[...]).astype(BF16)

    return pl.pallas_call(
        body, out_shape=jax.ShapeDtypeStruct((T, D_MODEL), BF16), grid=(T // tm,),
        in_specs=[pl.BlockSpec((tm, D_MODEL), lambda i: (i, 0)), pl.BlockSpec((1, D_MODEL), lambda i: (0, 0))],
        out_specs=pl.BlockSpec((tm, D_MODEL), lambda i: (i, 0)),
        compiler_params=_params("parallel"), name=name)(x, gain)


def _rms_bwd(x, gain, du, dres, name):
    T = x.shape[0]
    tm = _pick_tile(T, 512, 16)

    def body(x_ref, g_ref, du_ref, dres_ref, dx_ref, dxb_ref, dg_ref):
        @pl.when(pl.program_id(0) == 0)
        def _():
            dg_ref[...] = jnp.zeros_like(dg_ref)

        xv = x_ref[...]
        rstd = lax.rsqrt(jnp.mean(xv * xv, axis=-1, keepdims=True) + NORM_EPS)
        n = xv * rstd
        du = du_ref[...]
        dg_ref[...] += jnp.sum(du * n, axis=0, keepdims=True)
        dn = du * g_ref[...]
        dx = dres_ref[...] + rstd * (dn - n * jnp.mean(dn * n, axis=-1, keepdims=True))
        dx_ref[...] = dx
        dxb_ref[...] = dx.astype(BF16)

    row = pl.BlockSpec((tm, D_MODEL), lambda i: (i, 0))
    vec = pl.BlockSpec((1, D_MODEL), lambda i: (0, 0))
    return pl.pallas_call(
        body,
        out_shape=(jax.ShapeDtypeStruct((T, D_MODEL), F32), jax.ShapeDtypeStruct((T, D_MODEL), BF16),
                   jax.ShapeDtypeStruct((1, D_MODEL), F32)),
        grid=(T // tm,), in_specs=[row, vec, row, row], out_specs=(row, row, vec),
        compiler_params=_params("arbitrary"), name=name)(x, gain, du, dres)


def _loss_head(h, target, gain, name):
    T = h.shape[0]
    tm = _pick_tile(T, 512, 16)
    inv_f = 1.0 / D_MODEL

    def body(h_ref, t_ref, g_ref, dh_ref, dhb_ref, dg_ref, loss_ref):
        @pl.when(pl.program_id(0) == 0)
        def _():
            dg_ref[...] = jnp.zeros_like(dg_ref)
            loss_ref[...] = jnp.zeros_like(loss_ref)

        hv = h_ref[...]
        g = g_ref[...]
        rstd = lax.rsqrt(jnp.mean(hv * hv, axis=-1, keepdims=True) + NORM_EPS)
        n = hv * rstd
        err = n * g - t_ref[...]
        loss_ref[...] += (0.5 * inv_f) * jnp.sum(err * err, axis=0, keepdims=True)
        dy = err * inv_f
        dg_ref[...] += jnp.sum(dy * n, axis=0, keepdims=True)
        dn = dy * g
        dh = rstd * (dn - n * jnp.mean(dn * n, axis=-1, keepdims=True))
        dh_ref[...] = dh
        dhb_ref[...] = dh.astype(BF16)

    row = pl.BlockSpec((tm, D_MODEL), lambda i: (i, 0))
    vec = pl.BlockSpec((1, D_MODEL), lambda i: (0, 0))
    return pl.pallas_call(
        body,
        out_shape=(jax.ShapeDtypeStruct((T, D_MODEL), F32), jax.ShapeDtypeStruct((T, D_MODEL), BF16),
                   jax.ShapeDtypeStruct((1, D_MODEL), F32), jax.ShapeDtypeStruct((1, D_MODEL), F32)),
        grid=(T // tm,), in_specs=[row, row, vec], out_specs=(row, row, vec, vec),
        compiler_params=_params("arbitrary"), name=name)(h, target, gain)


FFN_TILE = 256


def _ffn_in(h, gain, w_in, name):
    T = h.shape[0]
    tm = _pick_tile(T, ROW_TILE, 16)

    def body(h_ref, g_ref, w_ref, n_ref, gate_ref, up_ref, a_ref):
        hv = h_ref[...]
        rstd = lax.rsqrt(jnp.mean(hv * hv, axis=-1, keepdims=True) + NORM_EPS)
        n = (hv * rstd * g_ref[...]).astype(BF16)
        n_ref[...] = n
        for c0 in range(0, D_FF, FFN_TILE):
            cols = slice(c0, c0 + FFN_TILE)
            gate = _dot(n, w_ref[c0:c0 + FFN_TILE, :], NT)
            up = _dot(n, w_ref[D_FF + c0:D_FF + c0 + FFN_TILE, :], NT)
            gate_ref[:, cols] = gate.astype(BF16)
            up_ref[:, cols] = up.astype(BF16)
            a_ref[:, cols] = (gate * _sigmoid(gate) * up).astype(BF16)

    row = pl.BlockSpec((tm, D_MODEL), lambda i: (i, 0))
    wide = pl.BlockSpec((tm, D_FF), lambda i: (i, 0))
    wide_shape = jax.ShapeDtypeStruct((T, D_FF), BF16)
    return pl.pallas_call(
        body, out_shape=(jax.ShapeDtypeStruct((T, D_MODEL), BF16), wide_shape, wide_shape, wide_shape),
        grid=(T // tm,),
        in_specs=[row, pl.BlockSpec((1, D_MODEL), lambda i: (0, 0)), _whole((2 * D_FF, D_MODEL), lambda i: (0, 0))],
        out_specs=(row, wide, wide, wide), compiler_params=_params("parallel"), name=name)(h, gain, w_in)


def _ffn_down_dx(dhb, w_down, gate, up, name):
    T = dhb.shape[0]
    tm = _pick_tile(T, ROW_TILE, 16)

    def body(dh_ref, w_ref, gate_ref, up_ref, dgate_ref, dup_ref):
        dh = dh_ref[...]
        for c0 in range(0, D_FF, FFN_TILE):
            cols = slice(c0, c0 + FFN_TILE)
            da = _dot(dh, w_ref[c0:c0 + FFN_TILE, :], NT)
            gate = gate_ref[:, cols].astype(F32)
            sg = _sigmoid(gate)
            dgate_ref[:, cols] = (da * up_ref[:, cols].astype(F32) * (sg * (1.0 + gate * (1.0 - sg)))).astype(BF16)
            dup_ref[:, cols] = (da * gate * sg).astype(BF16)

    wide = pl.BlockSpec((tm, D_FF), lambda i: (i, 0))
    wide_shape = jax.ShapeDtypeStruct((T, D_FF), BF16)
    return pl.pallas_call(
        body, out_shape=(wide_shape, wide_shape), grid=(T // tm,),
        in_specs=[pl.BlockSpec((tm, D_MODEL), lambda i: (i, 0)), _whole((D_FF, D_MODEL), lambda i: (0, 0)), wide, wide],
        out_specs=(wide, wide), compiler_params=_params("parallel"), name=name)(dhb, w_down, gate, up)


def _tri(n, lower):
    r = lax.broadcasted_iota(jnp.int32, (n, n), 0)
    c = lax.broadcasted_iota(jnp.int32, (n, n), 1)
    return (c <= r) if lower else (c >= r)


def _running_sum(x, lower):
    n = x.shape[0]
    tri = _tri(n, lower).astype(F32)
    return lax.dot_general(tri, x, NN, precision=lax.Precision.HIGHEST, preferred_element_type=F32)


def _hgrn_gates(q_raw, f_raw, lb):
    C = q_raw.shape[0]
    sig_f = _sigmoid(f_raw)
    forget = lb + (1.0 - lb) * sig_f
    key = 1.0 - forget
    log_f = jnp.log(forget)
    b = _running_sum(log_f, True)
    first_half = lax.broadcasted_iota(jnp.int32, log_f.shape, 0) < C // 2
    r = jnp.sum(jnp.where(first_half, log_f, 0.0), axis=0, keepdims=True)
    b_last = jnp.sum(log_f, axis=0, keepdims=True)
    e_a = jnp.exp(jnp.minimum(b - r, HGRN_EXP_CLAMP))
    e_b = jnp.exp(jnp.minimum(r - b, HGRN_EXP_CLAMP))
    e_q = jnp.exp(b)
    e_k = jnp.exp(b_last - b)
    sig_q = _sigmoid(q_raw)
    query = q_raw * sig_q
    return dict(sig_f=sig_f, forget=forget, sig_q=sig_q, e_a=e_a, e_b=e_b, e_q=e_q, e_k=e_k,
                e_last=jnp.exp(b_last), q_a=query * e_a, k_b=key * e_b, q_hat=query * e_q, k_til=key * e_k)


def _hgrn_fwd(proj, lb, gain, name):
    T = proj.shape[0]
    C = HGRN_CHUNK
    H, HD = HGRN_HEADS, HGRN_DIM

    def body(q_ref, f_ref, i_ref, g_ref, lb_ref, gain_ref, og_ref, o_ref, st_ref, s_scr):
        @pl.when(pl.program_id(0) == 0)
        def _():
            s_scr[...] = jnp.zeros_like(s_scr)

        st_ref[0] = s_scr[...]
        gt = _hgrn_gates(q_ref[...], f_ref[...], lb_ref[...])
        causal = _tri(C, True)
        gain_v = gain_ref[...]
        for h in range(H):
            sl = slice(h * HD, (h + 1) * HD)
            v = i_ref[:, sl].astype(BF16)
            p = jnp.where(causal, _dot(gt["q_a"][:, sl].astype(BF16), gt["k_b"][:, sl].astype(BF16), NT), 0.0)
            s_t = s_scr[h]
            o = _dot(p.astype(BF16), v, NN) + _dot(gt["q_hat"][:, sl].astype(BF16), s_t.astype(BF16), NT)
            s_scr[h] = gt["e_last"][:, sl] * s_t + _dot(v, gt["k_til"][:, sl].astype(BF16), TN)
            o_ref[:, sl] = o
            rstd = lax.rsqrt(jnp.mean(o * o, axis=-1, keepdims=True) + NORM_EPS)
            g_raw = g_ref[:, sl]
            og_ref[:, sl] = (o * rstd * gain_v * (g_raw * _sigmoid(g_raw))).astype(BF16)

    col = lambda j: pl.BlockSpec((C, D_MODEL), lambda c: (c, j))
    row = pl.BlockSpec((C, D_MODEL), lambda c: (c, 0))
    return pl.pallas_call(
        body,
        out_shape=(jax.ShapeDtypeStruct((T, D_MODEL), BF16), jax.ShapeDtypeStruct((T, D_MODEL), F32),
                   jax.ShapeDtypeStruct((T // C, H, HD, HD), F32)),
        grid=(T // C,),
        in_specs=[col(0), col(1), col(2), col(3), pl.BlockSpec((1, D_MODEL), lambda c: (0, 0)),
                  pl.BlockSpec((1, HD), lambda c: (0, 0))],
        out_specs=(row, row, pl.BlockSpec((1, H, HD, HD), lambda c: (c, 0, 0, 0))),
        scratch_shapes=[pltpu.VMEM((H, HD, HD), F32)],
        compiler_params=_params("arbitrary"), name=name)(proj, proj, proj, proj, lb, gain)


def _hgrn_bwd(proj, o_pre, d_og, states, lb, gain, name):
    T = proj.shape[0]
    C = HGRN_CHUNK
    H, HD = HGRN_HEADS, HGRN_DIM
    NC = T // C

    def body(q_ref, f_ref, i_ref, g_ref, o_ref, dog_ref, st_ref, lb_ref, gain_ref,
             dproj_ref, dlb_ref, dgain_ref, ds_scr, dq_scr, dk_scr, db_scr):
        @pl.when(pl.program_id(0) == 0)
        def _():
            ds_scr[...] = jnp.zeros_like(ds_scr)
            dlb_ref[...] = jnp.zeros_like(dlb_ref)
            dgain_ref[...] = jnp.zeros_like(dgain_ref)

        lbv = lb_ref[...]
        q_raw = q_ref[...]
        gt = _hgrn_gates(q_raw, f_ref[...], lbv)
        causal = _tri(C, True)
        last_row = lax.broadcasted_iota(jnp.int32, (C, HD), 0) == C - 1
        gain_v = gain_ref[...]
        dgain = jnp.zeros((1, HD), F32)
        for h in range(H):
            sl = slice(h * HD, (h + 1) * HD)
            o = o_ref[:, sl]
            rstd = lax.rsqrt(jnp.mean(o * o, axis=-1, keepdims=True) + NORM_EPS)
            n = o * rstd
            g_raw = g_ref[:, sl]
            sg = _sigmoid(g_raw)
            d_out = dog_ref[:, sl]
            dproj_ref[:, 3 * D_MODEL + h * HD:3 * D_MODEL + (h + 1) * HD] = (
                d_out * n * gain_v * (sg * (1.0 + g_raw * (1.0 - sg)))).astype(BF16)
            dy = d_out * (g_raw * sg)
            dgain = dgain + jnp.sum(dy * n, axis=0, keepdims=True)
            dn = dy * gain_v
            do = (rstd * (dn - n * jnp.mean(dn * n, axis=-1, keepdims=True))).astype(BF16)
            q_a, k_b = gt["q_a"][:, sl], gt["k_b"][:, sl]
            q_hat, k_til = gt["q_hat"][:, sl], gt["k_til"][:, sl]
            q_ab, k_bb = q_a.astype(BF16), k_b.astype(BF16)
            v = i_ref[:, sl].astype(BF16)
            s_t = st_ref[0, h]
            ds_t = ds_scr[h]
            ds_b = ds_t.astype(BF16)
            e_last = gt["e_last"][:, sl]
            p = jnp.where(causal, _dot(q_ab, k_bb, NT), 0.0).astype(BF16)
            dp = jnp.where(causal, _dot(do, v, NT), 0.0).astype(BF16)
            dv = _dot(p, do, TN) + _dot(k_til.astype(BF16), ds_b, NT)
            dq_a = _dot(dp, k_bb, NN)
            dk_b = _dot(dp, q_ab, TN)
            dq_hat = _dot(do, s_t.astype(BF16), NN)
            dk_til = _dot(v, ds_b, NN)
            ds_scr[h] = _dot(do, q_hat.astype(BF16), TN) + e_last * ds_t
            db_last = jnp.sum(ds_t * e_last * s_t, axis=0, keepdims=True) + jnp.sum(
                dk_til * k_til, axis=0, keepdims=True)
            dproj_ref[:, 2 * D_MODEL + h * HD:2 * D_MODEL + (h + 1) * HD] = dv.astype(BF16)
            dq_scr[:, sl] = dq_a * gt["e_a"][:, sl] + dq_hat * gt["e_q"][:, sl]
            dk_scr[:, sl] = dk_b * gt["e_b"][:, sl] + dk_til * gt["e_k"][:, sl]
            db = dq_a * q_ab.astype(F32) + dq_hat * q_hat - dk_b * k_bb.astype(F32) - dk_til * k_til
            db_scr[:, sl] = db + jnp.where(last_row, db_last, 0.0)
        dgain_ref[...] += dgain
        dlogf = _running_sum(db_scr[...], False)
        sig_f, forget, sig_q = gt["sig_f"], gt["forget"], gt["sig_q"]
        dforget = dlogf / forget - dk_scr[...]
        dproj_ref[:, D_MODEL:2 * D_MODEL] = (dforget * (1.0 - lbv) * sig_f * (1.0 - sig_f)).astype(BF16)
        dlb_ref[...] += jnp.sum(dforget * (1.0 - sig_f), axis=0, keepdims=True)
        dproj_ref[:, 0:D_MODEL] = (dq_scr[...] * (sig_q * (1.0 + q_raw * (1.0 - sig_q)))).astype(BF16)

    col = lambda j: pl.BlockSpec((C, D_MODEL), lambda c: (NC - 1 - c, j))
    row = pl.BlockSpec((C, D_MODEL), lambda c: (NC - 1 - c, 0))
    return pl.pallas_call(
        body,
        out_shape=(jax.ShapeDtypeStruct((T, 4 * D_MODEL), BF16), jax.ShapeDtypeStruct((1, D_MODEL), F32),
                   jax.ShapeDtypeStruct((1, HD), F32)),
        grid=(NC,),
        in_specs=[col(0), col(1), col(2), col(3), row, row,
                  pl.BlockSpec((1, H, HD, HD), lambda c: (NC - 1 - c, 0, 0, 0)),
                  pl.BlockSpec((1, D_MODEL), lambda c: (0, 0)), pl.BlockSpec((1, HD), lambda c: (0, 0))],
        out_specs=(pl.BlockSpec((C, 4 * D_MODEL), lambda c: (NC - 1 - c, 0)),
                   pl.BlockSpec((1, D_MODEL), lambda c: (0, 0)), pl.BlockSpec((1, HD), lambda c: (0, 0))),
        scratch_shapes=[pltpu.VMEM((H, HD, HD), F32), pltpu.VMEM((C, D_MODEL), F32),
                        pltpu.VMEM((C, D_MODEL), F32), pltpu.VMEM((C, D_MODEL), F32)],
        compiler_params=_params("arbitrary"), name=name)(proj, proj, proj, proj, o_pre, d_og, states, lb, gain)


def _attn_masks():
    r = lax.broadcasted_iota(jnp.int32, (ATTN_BLOCK, ATTN_BLOCK), 0)
    c = lax.broadcasted_iota(jnp.int32, (ATTN_BLOCK, ATTN_BLOCK), 1)
    return c >= r, c <= r


def _attn_fwd(qkv, dilation, name):
    T = qkv.shape[0]
    nb = T // dilation // ATTN_BLOCK
    W = ATTN_GROUP_WIDTH
    scale = ATTN_DIM ** -0.5

    def body(q_ref, kp_ref, kc_ref, vp_ref, vc_ref, o_ref, lse_ref):
        no_prev = jnp.where(pl.program_id(1) > 0, 0.0, NEG_BIG)
        m_prev, m_cur = _attn_masks()
        for h in range(ATTN_GROUP_HEADS):
            sl = slice(h * ATTN_DIM, (h + 1) * ATTN_DIM)
            q = q_ref[:, sl]
            s_p = jnp.where(m_prev, _dot(q, kp_ref[:, sl], NT) * scale + no_prev, NEG_BIG)
            s_c = jnp.where(m_cur, _dot(q, kc_ref[:, sl], NT) * scale, NEG_BIG)
            m = jnp.maximum(jnp.max(s_p, axis=-1, keepdims=True), jnp.max(s_c, axis=-1, keepdims=True))
            p_p = jnp.exp(s_p - m)
            p_c = jnp.exp(s_c - m)
            l = jnp.sum(p_p, axis=-1, keepdims=True) + jnp.sum(p_c, axis=-1, keepdims=True)
            acc = _dot(p_p.astype(BF16), vp_ref[:, sl], NN) + _dot(p_c.astype(BF16), vc_ref[:, sl], NN)
            o_ref[:, sl] = acc / l
            lse_ref[:, sl] = jnp.broadcast_to(m + jnp.log(l), (ATTN_BLOCK, ATTN_DIM))

    blk = lambda col, prev: pl.BlockSpec(
        (ATTN_BLOCK, W), lambda s, n: (s * nb + (jnp.maximum(n - 1, 0) if prev else n), col))
    out = pl.BlockSpec((ATTN_BLOCK, W), lambda s, n: (s * nb + n, 0))
    return pl.pallas_call(
        body, out_shape=(jax.ShapeDtypeStruct((T, W), F32),) * 2, grid=(dilation, nb),
        in_specs=[blk(0, False), blk(1, True), blk(1, False), blk(2, True), blk(2, False)],
        out_specs=(out, out), compiler_params=_params("parallel", "arbitrary"), name=name)(qkv, qkv, qkv, qkv, qkv)


def _attn_bwd(qkv, d_out, lse, delta, cos, sin, dilation, name):
    T = qkv.shape[0]
    nb = T // dilation // ATTN_BLOCK
    W = ATTN_GROUP_WIDTH
    scale = ATTN_DIM ** -0.5

    def unrope(x, cos_v, sin_v):
        return x * cos_v + pltpu.roll(x * sin_v, ATTN_DIM // 2, 1)

    def body(q_ref, kp_ref, kc_ref, vp_ref, vc_ref, do_ref, lse_ref, dl_ref, cos_ref, sin_ref,
             out_ref, dq_scr, dk_scr, dv_scr):
        n = pl.program_id(1)
        cos_v, sin_v = cos_ref[...], sin_ref[...]

        @pl.when(n > 0)
        def _():
            for h in range(ATTN_GROUP_HEADS):
                sl = slice(h * ATTN_DIM, (h + 1) * ATTN_DIM)
                out_ref[:, sl] = unrope(dq_scr[:, sl], cos_v, sin_v).astype(BF16)

        @pl.when(n == nb)
        def _():
            for h in range(ATTN_GROUP_HEADS):
                sl = slice(h * ATTN_DIM, (h + 1) * ATTN_DIM)
                out_ref[:, W + h * ATTN_DIM:W + (h + 1) * ATTN_DIM] = unrope(dk_scr[:, sl], cos_v, sin_v).astype(BF16)
                out_ref[:, 2 * W + h * ATTN_DIM:2 * W + (h + 1) * ATTN_DIM] = dv_scr[:, sl].astype(BF16)

        @pl.when(n == 0)
        def _():
            dk_scr[...] = jnp.zeros_like(dk_scr)
            dv_scr[...] = jnp.zeros_like(dv_scr)

        @pl.when(n < nb)
        def _():
            has_prev = n > 0
            no_prev = jnp.where(has_prev, 0.0, NEG_BIG)
            m_prev, m_cur = _attn_masks()
            for h in range(ATTN_GROUP_HEADS):
                sl = slice(h * ATTN_DIM, (h + 1) * ATTN_DIM)
                q, k_p, k_c, v_p, v_c = q_ref[:, sl], kp_ref[:, sl], kc_ref[:, sl], vp_ref[:, sl], vc_ref[:, sl]
                do = do_ref[:, sl]
                lse_v, dl_v = lse_ref[:, sl], dl_ref[:, sl]
                p_p = jnp.where(m_prev, jnp.exp(_dot(q, k_p, NT) * scale - lse_v + no_prev), 0.0)
                p_c = jnp.where(m_cur, jnp.exp(_dot(q, k_c, NT) * scale - lse_v), 0.0)
                ds_p = (p_p * (_dot(do, v_p, NT) - dl_v) * scale).astype(BF16)
                ds_c = (p_c * (_dot(do, v_c, NT) - dl_v) * scale).astype(BF16)
                dk_prev = dk_scr[:, sl] + _dot(ds_p, q, TN)
                dv_prev = dv_scr[:, sl] + _dot(p_p.astype(BF16), do, TN)
                out_ref[:, W + h * ATTN_DIM:W + (h + 1) * ATTN_DIM] = unrope(dk_prev, cos_v, sin_v).astype(BF16)
                out_ref[:, 2 * W + h * ATTN_DIM:2 * W + (h + 1) * ATTN_DIM] = dv_prev.astype(BF16)
                dq_scr[:, sl] = _dot(ds_p, k_p, NN) + _dot(ds_c, k_c, NN)
                dk_scr[:, sl] = _dot(ds_c, q, TN)
                dv_scr[:, sl] = _dot(p_c.astype(BF16), do, TN)

    def cur(n):
        return jnp.minimum(n, nb - 1)

    def late(n):
        return jnp.maximum(n - 1, 0)

    qkv_blk = lambda col, prev: pl.BlockSpec(
        (ATTN_BLOCK, W), lambda s, n: (s * nb + (jnp.maximum(cur(n) - 1, 0) if prev else cur(n)), col))
    row = pl.BlockSpec((ATTN_BLOCK, W), lambda s, n: (s * nb + cur(n), 0))
    tab = pl.BlockSpec((ATTN_BLOCK, ATTN_DIM), lambda s, n: (s * nb + late(n), 0))
    return pl.pallas_call(
        body, out_shape=jax.ShapeDtypeStruct((T, 3 * W), BF16), grid=(dilation, nb + 1),
        in_specs=[qkv_blk(0, False), qkv_blk(1, True), qkv_blk(1, False), qkv_blk(2, True), qkv_blk(2, False),
                  row, row, row, tab, tab],
        out_specs=pl.BlockSpec((ATTN_BLOCK, 3 * W), lambda s, n: (s * nb + late(n), 0)),
        scratch_shapes=[pltpu.VMEM((ATTN_BLOCK, W), F32)] * 3,
        compiler_params=_params("parallel", "arbitrary"), name=name)(
            qkv, qkv, qkv, qkv, qkv, d_out, lse, delta, cos, sin)


def _attn_merge_fwd(outs, lses, name):
    T = outs[0].shape[0]
    W = ATTN_GROUP_WIDTH
    tm = _pick_tile(T, 512, 16)

    def body(o0, o1, o2, l0, l1, l2, oc_ref, lse_ref):
        ls = [l0[...], l1[...], l2[...]]
        m = jnp.maximum(jnp.maximum(ls[0], ls[1]), ls[2])
        tot = m + jnp.log(jnp.exp(ls[0] - m) + jnp.exp(ls[1] - m) + jnp.exp(ls[2] - m))
        lse_ref[...] = tot
        for g, o in enumerate((o0, o1, o2)):
            oc_ref[:, g * W:(g + 1) * W] = (o[...] * jnp.exp(ls[g] - tot)).astype(BF16)

    blk = pl.BlockSpec((tm, W), lambda i: (i, 0))
    return pl.pallas_call(
        body, out_shape=(jax.ShapeDtypeStruct((T, 3 * W), BF16), jax.ShapeDtypeStruct((T, W), F32)),
        grid=(T // tm,), in_specs=[blk] * 6, out_specs=(pl.BlockSpec((tm, 3 * W), lambda i: (i, 0)), blk),
        compiler_params=_params("parallel"), name=name)(*outs, *lses)


def _attn_merge_bwd(d_oc, oc, name):
    T = d_oc.shape[0]
    W = ATTN_GROUP_WIDTH
    tm = _pick_tile(T, 512, 16)

    def body(d_ref, o_ref, delta_ref, db_ref):
        d = d_ref[...]
        db_ref[...] = d.astype(BF16)
        prod = d * o_ref[...].astype(F32)
        for h in range(ATTN_GROUP_HEADS):
            tot = jnp.zeros((tm, 1), F32)
            for g in range(3):
                lo = g * W + h * ATTN_DIM
                tot = tot + jnp.sum(prod[:, lo:lo + ATTN_DIM], axis=-1, keepdims=True)
            delta_ref[:, h * ATTN_DIM:(h + 1) * ATTN_DIM] = jnp.broadcast_to(tot, (tm, ATTN_DIM))

    wide = pl.BlockSpec((tm, 3 * W), lambda i: (i, 0))
    return pl.pallas_call(
        body, out_shape=(jax.ShapeDtypeStruct((T, W), F32), jax.ShapeDtypeStruct((T, 3 * W), BF16)),
        grid=(T // tm,), in_specs=[wide, wide], out_specs=(pl.BlockSpec((tm, W), lambda i: (i, 0)), wide),
        compiler_params=_params("parallel"), name=name)(d_oc, oc)


def _to_residues(x, d):
    if d == 1:
        return x
    T, C = x.shape
    return x.reshape(T // d, d, C).transpose(1, 0, 2).reshape(T, C)


def _from_residues(x, d):
    if d == 1:
        return x
    T, C = x.shape
    return x.reshape(d, T // d, C).transpose(1, 0, 2).reshape(T, C)


def _rope_tables(T):
    inv_freq = 1.0 / (ROPE_THETA ** (jnp.arange(0, ATTN_DIM, 2, dtype=F32) / ATTN_DIM))
    ang = jnp.arange(T, dtype=F32)[:, None] * inv_freq[None, :]
    cos, sin = jnp.cos(ang), jnp.sin(ang)
    return jnp.concatenate([cos, cos], axis=1), jnp.concatenate([-sin, sin], axis=1)


WEIGHT_GROUPS = {"hgrn": ("hgrn_in", "hgrn_out"), "ffn0": ("ffn_in0", "ffn_down0"),
                 "attn": ("qkv", "attn_out"), "ffn1": ("ffn_in1", "ffn_down1")}


def _local_step(x, target, norm_mix, norm_ffn, lb, out_gain, final_gain, fetch, publish):
    T = x.shape[0]
    g_mix = [norm_mix[0:1], norm_mix[1:2]]
    g_ffn = [norm_ffn[0:1], norm_ffn[1:2]]
    w = {}

    def whole(name):
        return [(w[name], w[name].shape[0], 0)]

    def qkv_parts(g):
        return [(w["qkv"], ATTN_GROUP_WIDTH, 3 * j + g) for j in range(3)]

    def ffn_fwd(h, layer, before):
        w.update(fetch(f"ffn{layer}", [before]))
        n, gate, up, a = _ffn_in(h, g_ffn[layer], w[f"ffn_in{layer}"], f"ffn{layer}_in")
        out = _mm_nn([a], [whole(f"ffn_down{layer}")], h, name=f"ffn{layer}_down")
        return out, (n, gate, up, a)

    def ffn_bwd(h, saved, dh, dhb, layer):
        n, gate, up, a = saved
        w_in = w[f"ffn_in{layer}"]
        dgate, dup = _ffn_down_dx(dhb, w[f"ffn_down{layer}"], gate, up, f"ffn{layer}_down_dx")
        grad_in = _mm_tn(dgate, n, name=f"ffn{layer}_in_dw_gate", rows=2 * D_FF)
        grad_in = _mm_tn(dup, n, name=f"ffn{layer}_in_dw_up", into=grad_in, row_tile=D_FF // GRAD_TILE, rows=2 * D_FF)
        grads = {f"ffn_down{layer}": _mm_tn(a, dhb, name=f"ffn{layer}_down_dw"), f"ffn_in{layer}": grad_in}
        zero = publish(f"ffn{layer}", grads)
        dn = _mm_nn([dgate, dup], [[(w_in, D_FF, 0)], [(w_in, D_FF, 1)]], None, name=f"ffn{layer}_in_dx")
        return _rms_bwd(h, g_ffn[layer] + zero, dn, dh, f"ffn{layer}_norm_bwd")

    u0 = _rms_fwd(x, g_mix[0], "hgrn_norm")
    w.update(fetch("hgrn", [u0]))
    proj = _mm_nt(u0, whole("hgrn_in"), out_dtype=F32, name="hgrn_in")
    og, o_pre, states = _hgrn_fwd(proj, lb, out_gain, "hgrn_fwd")
    h1 = _mm_nn([og], [whole("hgrn_out")], x, name="hgrn_out")
    h2, ffn0 = ffn_fwd(h1, 0, og)

    u1 = _rms_fwd(h2, g_mix[1], "attn_norm")
    w.update(fetch("attn", [u1]))
    cos, sin = _rope_tables(T)
    u1_g, qkv_g, cos_g, sin_g, outs, lses = [], [], [], [], [], []
    for g, d in enumerate(ATTN_DILATIONS):
        u1_g.append(_to_residues(u1, d))
        cos_g.append(_to_residues(cos, d))
        sin_g.append(_to_residues(sin, d))
        qkv_g.append(_mm_nt(u1_g[g], qkv_parts(g), out_dtype=BF16, name=f"attn_qkv{g}",
                            rope=(cos_g[g], sin_g[g], 2)))
        o_g, lse_g = _attn_fwd(qkv_g[g], d, f"attn_fwd{g}")
        outs.append(_from_residues(o_g, d))
        lses.append(_from_residues(lse_g, d))
    oc, lse_all = _attn_merge_fwd(outs, lses, "attn_merge")
    h3 = _mm_nn([oc], [whole("attn_out")], h2, name="attn_out")
    h4, ffn1 = ffn_fwd(h3, 1, oc)

    dh4, dh4b, d_final, loss_part = _loss_head(h4, target, final_gain, "loss_head")
    dh3, dh3b, d_ffn1 = ffn_bwd(h3, ffn1, dh4, dh4b, 1)

    d_oc = _mm_nt(dh3b, whole("attn_out"), out_dtype=F32, name="attn_out_dx")
    grad_attn_out = _mm_tn(oc, dh3b, name="attn_out_dw")
    delta, d_ocb = _attn_merge_bwd(d_oc, oc, "attn_merge_bwd")
    du1 = None
    qkv_pieces = []
    for g, d in enumerate(ATTN_DILATIONS):
        W = ATTN_GROUP_WIDTH
        dqkv = _attn_bwd(qkv_g[g], _to_residues(d_ocb[:, g * W:(g + 1) * W], d), _to_residues(lse_all, d),
                         _to_residues(delta, d), cos_g[g], sin_g[g], d, f"attn_bwd{g}")
        qkv_pieces.append(_mm_tn(dqkv, u1_g[g], name=f"attn_qkv_dw{g}"))
        du1_g = _mm_nn([dqkv], [qkv_parts(g)], None, name=f"attn_qkv_dx{g}")
        du1_g = _from_residues(du1_g, d)
        du1 = du1_g if du1 is None else du1 + du1_g
    grad_qkv = jnp.stack([p.reshape(3, ATTN_GROUP_WIDTH, D_MODEL) for p in qkv_pieces], axis=1).reshape(
        3 * ATTN_WIDTH, D_MODEL)
    zero = publish("attn", {"qkv": grad_qkv, "attn_out": grad_attn_out})
    dh2, dh2b, d_mix1 = _rms_bwd(h2, g_mix[1] + zero, du1, dh3, "attn_norm_bwd")

    dh1, dh1b, d_ffn0 = ffn_bwd(h1, ffn0, dh2, dh2b, 0)

    d_og = _mm_nt(dh1b, whole("hgrn_out"), out_dtype=F32, name="hgrn_out_dx")
    grad_hgrn_out = _mm_tn(og, dh1b, name="hgrn_out_dw")
    dproj, d_lb, d_out_gain = _hgrn_bwd(proj, o_pre, d_og, states, lb, out_gain, "hgrn_bwd")
    zero = publish("hgrn", {"hgrn_in": _mm_tn(dproj, u0, name="hgrn_in_dw"), "hgrn_out": grad_hgrn_out})
    du0 = _mm_nn([dproj], [whole("hgrn_in")], None, name="hgrn_in_dx")
    dx, _, d_mix0 = _rms_bwd(x, g_mix[0] + zero, du0, dh1, "hgrn_norm_bwd")

    small = dict(norm_mix0=d_mix0, norm_mix1=d_mix1, norm_ffn0=d_ffn0, norm_ffn1=d_ffn1, lb=d_lb,
                 out_gain=d_out_gain, final=d_final, loss=loss_part)
    return dx, small


WEIGHT_NAMES = ("hgrn_in", "hgrn_out", "qkv", "attn_out", "ffn_in0", "ffn_in1", "ffn_down0", "ffn_down1")
MESH_IDS = pl.DeviceIdType.MESH
HBM_SPEC = pl.BlockSpec(memory_space=pl.ANY)


SEM_SPEC = pl.BlockSpec(memory_space=pltpu.SEMAPHORE)
LAND_SPEC = pl.BlockSpec(memory_space=pltpu.HBM)
N_PEERS = N_DEV - 1
PEER_OFFSETS = [(dx, dy, dc) for dx in (0, 1) for dy in (0, 1) for dc in (0, 1)][1:]


def _mesh_place():
    x, y, c = lax.axis_index("x"), lax.axis_index("y"), lax.axis_index("c")
    peers = []
    for dx, dy, dc in PEER_OFFSETS:
        px, py, pc = (1 - x if dx else x), (1 - y if dy else y), (1 - c if dc else c)
        peers.append(((px, py, pc), 4 * px + 2 * py + pc))
    return 4 * x + 2 * y + c, peers


def _start_copies(srcs, groups, scatter, name):
    nw, ng = len(srcs), len(groups)
    land_shapes = [(s.shape if scatter else (N_DEV,) + s.shape) for s in srcs]

    def body(*refs):
        src_refs, land_refs = refs[:nw], refs[nw:2 * nw]
        sems = refs[2 * nw:2 * nw + 2 * ng]
        local_sems = refs[-1]
        me, peers = _mesh_place()
        own = [pltpu.make_async_copy(src_refs[w].at[me] if scatter else src_refs[w], land_refs[w].at[me],
                                     local_sems.at[w]) for w in range(nw)]
        for cp in own:
            cp.start()
        for gi, group in enumerate(groups):
            for i, w in enumerate(group):
                for k, (peer, pid) in enumerate(peers):
                    pltpu.make_async_remote_copy(
                        src_ref=src_refs[w].at[pid] if scatter else src_refs[w], dst_ref=land_refs[w].at[me],
                        send_sem=sems[2 * gi].at[i * N_PEERS + k], recv_sem=sems[2 * gi + 1].at[i * N_PEERS + k],
                        device_id=peer, device_id_type=MESH_IDS).start()
        for cp in own:
            cp.wait()
        token_ref = refs[2 * nw + 2 * ng + 2 * nw]
        token_ref[...] = jnp.zeros_like(token_ref)

    sem_shapes = []
    for group in groups:
        sem_shapes += [pltpu.SemaphoreType.DMA((len(group) * N_PEERS,))] * 2
    out_shape = (sem_shapes + [pltpu.HBM(s.shape, s.dtype) for s in srcs]
                 + [pltpu.HBM(shape, s.dtype) for shape, s in zip(land_shapes, srcs)]
                 + [jax.ShapeDtypeStruct((8, 128), F32)])
    operands = [pltpu.with_memory_space_constraint(s, pltpu.HBM) for s in srcs]
    operands += [pltpu.with_memory_space_constraint(lax.empty(shape, s.dtype), pltpu.HBM)
                 for shape, s in zip(land_shapes, srcs)]
    res = pl.pallas_call(
        body, out_shape=out_shape, in_specs=[LAND_SPEC] * (2 * nw),
        out_specs=[SEM_SPEC] * (2 * ng) + [LAND_SPEC] * (2 * nw) + [pl.BlockSpec(memory_space=pltpu.VMEM)],
        input_output_aliases={i: 2 * ng + i for i in range(2 * nw)},
        scratch_shapes=[pltpu.SemaphoreType.DMA((nw,))],
        compiler_params=pltpu.CompilerParams(has_side_effects=pltpu.SideEffectType.DATAFLOW_SIDE_EFFECTING),
        name=name)(*operands)
    sems = [(res[2 * gi], res[2 * gi + 1]) for gi in range(ng)]
    return sems, list(res[2 * ng:2 * ng + nw]), list(res[2 * ng + nw:2 * ng + 2 * nw]), res[-1][0:1, 0:1]


def _wait_copies(sems, srcs, lands, after, scatter, name):
    n = len(srcs)

    def body(*refs):
        src_refs, land_refs = refs[:n], refs[n:2 * n]
        send_sems, recv_sems = refs[2 * n], refs[2 * n + 1]
        _, peers = _mesh_place()
        for i in range(n):
            for k, (peer, _) in enumerate(peers):
                copy = pltpu.make_async_remote_copy(
                    src_ref=src_refs[i].at[0] if scatter else src_refs[i], dst_ref=land_refs[i].at[0],
                    send_sem=send_sems.at[i * N_PEERS + k], recv_sem=recv_sems.at[i * N_PEERS + k],
                    device_id=peer, device_id_type=MESH_IDS)
                copy.wait_send()
                copy.wait_recv()

    arrays = list(srcs) + list(lands)
    res = pl.pallas_call(
        body, out_shape=[pltpu.HBM(a.shape, a.dtype) for a in arrays],
        in_specs=[LAND_SPEC] * (2 * n) + [SEM_SPEC] * 2 + [HBM_SPEC] * len(after),
        out_specs=[LAND_SPEC] * (2 * n), input_output_aliases={i: i for i in range(2 * n)},
        compiler_params=pltpu.CompilerParams(has_side_effects=pltpu.SideEffectType.DATAFLOW_SIDE_EFFECTING),
        name=name)(*arrays, sems[0], sems[1], *after)
    return list(res[n:])


def _gather_small(block, name):
    def body(in_ref, out_ref, send_sems, recv_sems, local_sem):
        me, peers = _mesh_place()
        own = pltpu.make_async_copy(in_ref, out_ref.at[me], local_sem)
        own.start()
        sends = [pltpu.make_async_remote_copy(
            src_ref=in_ref, dst_ref=out_ref.at[me], send_sem=send_sems.at[k], recv_sem=recv_sems.at[k],
            device_id=peer, device_id_type=MESH_IDS) for k, (peer, _) in enumerate(peers)]
        for cp in sends:
            cp.start()
        for cp in sends:
            cp.wait_recv()
        for cp in sends:
            cp.wait_send()
        own.wait()

    return pl.pallas_call(
        body, out_shape=jax.ShapeDtypeStruct((N_DEV,) + block.shape, block.dtype),
        in_specs=[HBM_SPEC], out_specs=HBM_SPEC,
        scratch_shapes=[pltpu.SemaphoreType.DMA((N_PEERS,)), pltpu.SemaphoreType.DMA((N_PEERS,)),
                        pltpu.SemaphoreType.DMA],
        name=name)(block)


def _sum_blocks(recv, name):
    rows = recv.shape[1]
    tr = _pick_tile(rows, 256, 16)

    def body(r_ref, g_ref):
        acc = r_ref[0].astype(F32)
        for j in range(1, N_DEV):
            acc = acc + r_ref[j].astype(F32)
        g_ref[...] = acc

    return pl.pallas_call(
        body, out_shape=jax.ShapeDtypeStruct((rows, D_MODEL), F32), grid=(rows // tr,),
        in_specs=[pl.BlockSpec((N_DEV, tr, D_MODEL), lambda i: (0, i, 0))],
        out_specs=pl.BlockSpec((tr, D_MODEL), lambda i: (i, 0)),
        compiler_params=_params("parallel"), name=name)(recv)


def _adamw_math(w, g, m, v):
    m_new = ADAM_B1 * m + (1.0 - ADAM_B1) * g
    v_new = ADAM_B2 * v + (1.0 - ADAM_B2) * (g * g)
    m_hat = m_new / (1.0 - ADAM_B1 ** ADAM_STEP)
    v_hat = v_new / (1.0 - ADAM_B2 ** ADAM_STEP)
    delta = -ADAM_LR * (m_hat / (jnp.sqrt(v_hat) + ADAM_EPS) + ADAM_WD * w)
    return delta, m_new, v_new


def _adamw(w, g, m, v, name):
    rows, cols = w.shape
    tr = _pick_tile(rows, 256, 8)

    def body(w_ref, g_ref, m_ref, v_ref, d_ref, mo_ref, vo_ref):
        d_ref[...], mo_ref[...], vo_ref[...] = _adamw_math(w_ref[...], g_ref[...], m_ref[...], v_ref[...])

    blk = pl.BlockSpec((tr, cols), lambda i: (i, 0))
    return pl.pallas_call(
        body, out_shape=(jax.ShapeDtypeStruct((rows, cols), F32),) * 3, grid=(rows // tr,),
        in_specs=[blk] * 4, out_specs=(blk,) * 3, compiler_params=_params("parallel"), name=name)(w, g, m, v)


ROW_MIX, ROW_FFN, ROW_LB, ROW_OUT_GAIN, ROW_FINAL = 0, 2, 4, 7, 8
PART_MIX, PART_FFN, PART_LB, PART_OUT_GAIN, PART_FINAL, PART_LOSS = 0, 2, 4, 5, 6, 7


def _small_update(parts_all, w, m, v, name):
    def body(p_ref, w_ref, m_ref, v_ref, g_ref, d_ref, mo_ref, vo_ref, loss_ref):
        def total(row, n=1):
            tot = p_ref[0, row:row + n, :]
            for j in range(1, N_DEV):
                tot = tot + p_ref[j, row:row + n, :]
            return tot

        logits = [w_ref[ROW_LB + i:ROW_LB + i + 1, :] for i in range(3)]
        mx = jnp.maximum(jnp.maximum(logits[0], logits[1]), logits[2])
        ex = [jnp.exp(l - mx) for l in logits]
        den = ex[0] + ex[1] + ex[2]
        prob = [e / den for e in ex]
        d_lb = total(PART_LB)
        g_ref[...] = jnp.zeros_like(g_ref)
        g_ref[ROW_MIX:ROW_MIX + 2, :] = total(PART_MIX, 2)
        g_ref[ROW_FFN:ROW_FFN + 2, :] = total(PART_FFN, 2)
        for i in range(3):
            g_ref[ROW_LB + i:ROW_LB + i + 1, :] = prob[i] * ((d_lb if i == 0 else 0.0) - prob[0] * d_lb)
        g_ref[ROW_OUT_GAIN:ROW_OUT_GAIN + 1, :] = total(PART_OUT_GAIN)
        g_ref[ROW_FINAL:ROW_FINAL + 1, :] = total(PART_FINAL)
        d_ref[...], mo_ref[...], vo_ref[...] = _adamw_math(w_ref[...], g_ref[...], m_ref[...], v_ref[...])
        loss_ref[...] = jnp.sum(total(PART_LOSS), axis=-1, keepdims=True)

    packed = jax.ShapeDtypeStruct((16, D_MODEL), F32)
    return pl.pallas_call(
        body, out_shape=(packed, packed, packed, packed, jax.ShapeDtypeStruct((1, 1), F32)),
        compiler_params=pltpu.CompilerParams(vmem_limit_bytes=VMEM_LIMIT), name=name)(parts_all, w, m, v)


def _pack_small(norm_mix, norm_ffn, lb_logits, out_gain, final):
    pad = jnp.zeros((1, D_MODEL - HGRN_DIM), F32)
    return jnp.concatenate([norm_mix, norm_ffn, lb_logits, jnp.concatenate([out_gain, pad], axis=1),
                            final.reshape(1, D_MODEL), jnp.zeros((16 - ROW_FINAL - 1, D_MODEL), F32)], axis=0)


def _unpack_small(p):
    return (p[ROW_MIX:ROW_MIX + 2], p[ROW_FFN:ROW_FFN + 2], p[ROW_LB:ROW_LB + 3],
            p[ROW_OUT_GAIN:ROW_OUT_GAIN + 1, :HGRN_DIM], p[ROW_FINAL])


def _lower_bound(lb_logits, name):
    def body(l_ref, o_ref):
        logits = [l_ref[i:i + 1, :] for i in range(3)]
        mx = jnp.maximum(jnp.maximum(logits[0], logits[1]), logits[2])
        ex = [jnp.exp(l - mx) for l in logits]
        o_ref[...] = ex[0] / (ex[0] + ex[1] + ex[2])

    return pl.pallas_call(body, out_shape=jax.ShapeDtypeStruct((1, D_MODEL), F32), name=name)(lb_logits)


def kernel(x, norm_mix, norm_ffn, hgrn_w_in, hgrn_lb_logits, hgrn_out_norm, hgrn_w_out, attn_w_qkv, attn_w_out, ffn_w_in, ffn_w_down, final_norm, loss_target, m_norm_mix, m_norm_ffn, m_hgrn_w_in, m_hgrn_lb_logits, m_hgrn_out_norm, m_hgrn_w_out, m_attn_w_qkv, m_attn_w_out, m_ffn_w_in, m_ffn_w_down, m_final_norm, v_norm_mix, v_norm_ffn, v_hgrn_w_in, v_hgrn_lb_logits, v_hgrn_out_norm, v_hgrn_w_out, v_attn_w_qkv, v_attn_w_out, v_ffn_w_in, v_ffn_w_down, v_final_norm):
    col_sharded = {"hgrn_in": hgrn_w_in[0], "qkv": attn_w_qkv[0], "ffn_in0": ffn_w_in[0], "ffn_in1": ffn_w_in[1]}
    row_sharded = {"hgrn_out": hgrn_w_out[0], "attn_out": attn_w_out[0], "ffn_down0": ffn_w_down[0],
                   "ffn_down1": ffn_w_down[1]}
    order = [n for group in WEIGHT_GROUPS.values() for n in group]
    shards = [(col_sharded[n].T if n in col_sharded else row_sharded[n]).astype(BF16) for n in order]
    index_groups = [[order.index(n) for n in group] for group in WEIGHT_GROUPS.values()]
    w_sems, w_srcs, w_lands, _ = _start_copies(shards, index_groups, False, "weights_gather_start")

    def fetch(group, after):
        gi = list(WEIGHT_GROUPS).index(group)
        idx = index_groups[gi]
        lands = _wait_copies(w_sems[gi], [w_srcs[i] for i in idx], [w_lands[i] for i in idx], after, False,
                             f"weights_gather_wait_{group}")
        return {n: land.reshape(-1, D_MODEL) for n, land in zip(WEIGHT_GROUPS[group], lands)}

    in_flight = {}

    def publish(group, grads):
        names = WEIGHT_GROUPS[group]
        parts = [grads[n].reshape(N_DEV, -1, D_MODEL) for n in names]
        sems, srcs, lands, zero = _start_copies(parts, [list(range(len(names)))], True, f"grads_send_start_{group}")
        in_flight[group] = (sems[0], srcs, lands)
        return zero

    lb = _lower_bound(hgrn_lb_logits, "hgrn_lower_bound")
    grad_x, small = _local_step(x[0], loss_target[0], norm_mix, norm_ffn, lb, hgrn_out_norm,
                                final_norm.reshape(1, D_MODEL), fetch, publish)

    pad = jnp.zeros((1, D_MODEL - HGRN_DIM), F32)
    small_part = jnp.concatenate(
        [small["norm_mix0"], small["norm_mix1"], small["norm_ffn0"], small["norm_ffn1"], small["lb"],
         jnp.concatenate([small["out_gain"], pad], axis=1), small["final"], small["loss"]], axis=0)
    small_all = _gather_small(small_part, "small_grads_gather")
    received = {}
    for group in ("ffn1", "attn", "ffn0", "hgrn"):
        sems, srcs, lands = in_flight[group]
        lands = _wait_copies(sems, srcs, lands, [small_all], True, f"grads_send_wait_{group}")
        received.update(zip(WEIGHT_GROUPS[group], lands))

    masters = {"hgrn_in": (hgrn_w_in[0], m_hgrn_w_in[0], v_hgrn_w_in[0]),
               "hgrn_out": (hgrn_w_out[0], m_hgrn_w_out[0], v_hgrn_w_out[0]),
               "qkv": (attn_w_qkv[0], m_attn_w_qkv[0], v_attn_w_qkv[0]),
               "attn_out": (attn_w_out[0], m_attn_w_out[0], v_attn_w_out[0]),
               "ffn_in0": (ffn_w_in[0], m_ffn_w_in[0], v_ffn_w_in[0]),
               "ffn_in1": (ffn_w_in[1], m_ffn_w_in[1], v_ffn_w_in[1]),
               "ffn_down0": (ffn_w_down[0], m_ffn_w_down[0], v_ffn_w_down[0]),
               "ffn_down1": (ffn_w_down[1], m_ffn_w_down[1], v_ffn_w_down[1])}
    res = {}
    for n in WEIGHT_NAMES:
        g = _sum_blocks(received[n], f"{n}_grad_sum")
        if n in col_sharded:
            g = g.T
        wv, mv, vv = masters[n]
        res[n] = (g,) + tuple(_adamw(wv, g, mv, vv, f"{n}_adamw"))

    def single(n):
        return [t[None] for t in res[n]]

    def pair(n):
        return [jnp.stack([a, b]) for a, b in zip(res[n + "0"], res[n + "1"])]

    big = dict(hgrn_w_in=single("hgrn_in"), hgrn_w_out=single("hgrn_out"), attn_w_qkv=single("qkv"),
               attn_w_out=single("attn_out"), ffn_w_in=pair("ffn_in"), ffn_w_down=pair("ffn_down"))

    w_small = _pack_small(norm_mix, norm_ffn, hgrn_lb_logits, hgrn_out_norm, final_norm)
    m_small = _pack_small(m_norm_mix, m_norm_ffn, m_hgrn_lb_logits, m_hgrn_out_norm, m_final_norm)
    v_small = _pack_small(v_norm_mix, v_norm_ffn, v_hgrn_lb_logits, v_hgrn_out_norm, v_final_norm)
    g_s, d_s, m_s, v_s, loss = _small_update(small_all, w_small, m_small, v_small, "small_update")
    small_out = [_unpack_small(t) for t in (g_s, d_s, m_s, v_s)]

    def group(i):
        s = small_out[i]
        return (s[0], s[1], big["hgrn_w_in"][i], s[2], s[3], big["hgrn_w_out"][i], big["attn_w_qkv"][i],
                big["attn_w_out"][i], big["ffn_w_in"][i], big["ffn_w_down"][i], s[4])

    return (loss.reshape(()), grad_x[None], *group(0), *group(1), *group(2), *group(3))
```

```python
import functools

import jax
import jax.numpy as jnp
from jax import lax
from jax.experimental import pallas as pl
from jax.experimental.pallas import tpu as pltpu

F32 = jnp.float32
BF16 = jnp.bfloat16

D_MODEL = 1024
N_DEV = 8
NORM_EPS = 1e-6

HGRN_HEADS = 8
HGRN_DIM = 128
HGRN_CHUNK = 64
HGRN_EXP_CLAMP = 60.0

ATTN_DIM = 128
ATTN_BLOCK = 128
ATTN_GROUP_HEADS = 4
ATTN_GROUP_WIDTH = ATTN_GROUP_HEADS * ATTN_DIM
ATTN_DILATIONS = (1, 4, 16)
ATTN_WIDTH = 3 * ATTN_GROUP_WIDTH
ROPE_THETA = 10000.0
NEG_BIG = -1e30

D_FF = 2816

ADAM_LR = 0.001
ADAM_B1 = 0.9
ADAM_B2 = 0.999
ADAM_EPS = 1e-08
ADAM_WD = 0.01
ADAM_STEP = 10

VMEM_LIMIT = 48 * 1024 * 1024

NT = (((1,), (1,)), ((), ()))
NN = (((1,), (0,)), ((), ()))
TN = (((0,), (0,)), ((), ()))


def _dot(a, b, dims):
    return lax.dot_general(a, b, dims, preferred_element_type=F32)


def _params(*sem):
    return pltpu.CompilerParams(dimension_semantics=sem, vmem_limit_bytes=VMEM_LIMIT)


def _pick_tile(n, cap, mult):
    best = None
    for t in range(mult, min(n, cap) + 1, mult):
        if n % t == 0:
            best = t
    assert best is not None, (n, cap, mult)
    return best


def _sigmoid(x):
    return 1.0 / (1.0 + jnp.exp(-x))


ROW_TILE = 512
COL_CHUNK = 512
GRAD_TILE = 256


def _whole(shape, index_map):
    return pl.BlockSpec(shape, index_map, pipeline_mode=pl.Buffered(1))


def _part_specs(parts, n_cols):
    return [_whole((rows, n_cols), functools.partial(lambda i, b: (b, 0), b=blk)) for _, rows, blk in parts]


def _mm_nt(a, w_parts, *, out_dtype, name, rope=None):
    M, K = a.shape
    tm = _pick_tile(M, ROW_TILE, 16)
    widths = [rows for _, rows, _ in w_parts]
    n_parts = len(w_parts)

    def body(*refs):
        a_ref, w_refs, o_ref = refs[0], refs[1:1 + n_parts], refs[-1]
        av = a_ref[...]
        off = 0
        for p, w_ref in enumerate(w_refs):
            for c0 in range(0, widths[p], COL_CHUNK):
                cw = min(COL_CHUNK, widths[p] - c0)
                acc = _dot(av, w_ref[c0:c0 + cw, :], NT)
                if rope is not None and p < rope[2]:
                    cos, sin = refs[1 + n_parts][...], refs[2 + n_parts][...]
                    for h0 in range(0, cw, ATTN_DIM):
                        xh = acc[:, h0:h0 + ATTN_DIM]
                        rot = pltpu.roll(xh, ATTN_DIM // 2, 1)
                        o_ref[:, off + c0 + h0:off + c0 + h0 + ATTN_DIM] = (xh * cos + rot * sin).astype(out_dtype)
                else:
                    o_ref[:, off + c0:off + c0 + cw] = acc.astype(out_dtype)
            off += widths[p]

    in_specs = [pl.BlockSpec((tm, K), lambda i: (i, 0))] + _part_specs(w_parts, K)
    args = [a] + [w for w, _, _ in w_parts]
    if rope is not None:
        in_specs += [pl.BlockSpec((tm, ATTN_DIM), lambda i: (i, 0))] * 2
        args += [rope[0], rope[1]]
    return pl.pallas_call(
        body, out_shape=jax.ShapeDtypeStruct((M, sum(widths)), out_dtype), grid=(M // tm,),
        in_specs=in_specs, out_specs=pl.BlockSpec((tm, sum(widths)), lambda i: (i, 0)),
        compiler_params=_params("parallel"), name=name)(*args)


def _mm_nn(a_list, w_parts_list, resid, *, name):
    M = a_list[0].shape[0]
    tm = _pick_tile(M, ROW_TILE, 16)
    n_a = len(a_list)
    flat_parts = [p for parts in w_parts_list for p in parts]

    def body(*refs):
        a_refs, w_refs, o_ref = refs[:n_a], refs[n_a:n_a + len(flat_parts)], refs[-1]
        acc = None
        wi = 0
        for a_ref, parts in zip(a_refs, w_parts_list):
            off = 0
            for _, rows, _ in parts:
                term = _dot(a_ref[:, off:off + rows], w_refs[wi][...], NN)
                acc = term if acc is None else acc + term
                off += rows
                wi += 1
        if resid is not None:
            acc = acc + refs[-2][...]
        o_ref[...] = acc

    row = pl.BlockSpec((tm, D_MODEL), lambda i: (i, 0))
    in_specs = [pl.BlockSpec((tm, a.shape[1]), lambda i: (i, 0)) for a in a_list] + _part_specs(flat_parts, D_MODEL)
    args = list(a_list) + [w for w, _, _ in flat_parts]
    if resid is not None:
        in_specs.append(row)
        args.append(resid)
    return pl.pallas_call(
        body, out_shape=jax.ShapeDtypeStruct((M, D_MODEL), F32), grid=(M // tm,),
        in_specs=in_specs, out_specs=row, compiler_params=_params("parallel"), name=name)(*args)


def _mm_tn(a, b, *, name, into=None, row_tile=0, rows=None):
    T, R = a.shape
    N = b.shape[1]
    tr = GRAD_TILE
    rows = R if rows is None else rows

    def body(a_ref, b_ref, *refs):
        refs[-1][...] = _dot(a_ref[...], b_ref[...], TN).astype(BF16)

    in_specs = [pl.BlockSpec((T, tr), lambda r: (0, r)), _whole((T, N), lambda r: (0, 0))]
    args = [a, b]
    if into is not None:
        in_specs.append(HBM_SPEC)
        args.append(into)
    return pl.pallas_call(
        body, out_shape=jax.ShapeDtypeStruct((rows, N), BF16), grid=(R // tr,),
        in_specs=in_specs, out_specs=pl.BlockSpec((tr, N), lambda r: (row_tile + r, 0)),
        input_output_aliases={} if into is None else {2: 0},
        compiler_params=_params("parallel"), name=name)(*args)


def _rms_fwd(x, gain, name):
    T = x.shape[0]
    tm = _pick_tile(T, 512, 16)

    def body(x_ref, g_ref, u_ref):
        xv = x_ref[...]
        rstd = lax.rsqrt(jnp.mean(xv * xv, axis=-1, keepdims=True) + NORM_EPS)
        u_ref[...] = (xv * rstd * g_ref[...]).astype(BF16)

    return pl.pallas_call(
        body, out_shape=jax.ShapeDtypeStruct((T, D_MODEL), BF16), grid=(T // tm,),
        in_specs=[pl.BlockSpec((tm, D_MODEL), lambda i: (i, 0)), pl.BlockSpec((1, D_MODEL), lambda i: (0, 0))],
        out_specs=pl.BlockSpec((tm, D_MODEL), lambda i: (i, 0)),
        compiler_params=_params("parallel"), name=name)(x, gain)


def _rms_bwd(x, gain, dus, dres, name, dilations=(1,)):
    T = x.shape[0]
    tm = _pick_tile(T, PERM_TILE, 16 * max(dilations))
    n_du = len(dus)

    def body(x_ref, g_ref, *refs):
        du_refs, dres_ref = refs[:n_du], refs[n_du]
        dx_ref, dxb_ref, dg_ref, du_scr = refs[n_du + 1:]

        @pl.when(pl.program_id(0) == 0)
        def _():
            dg_ref[...] = jnp.zeros_like(dg_ref)

        if tuple(dilations) == (1,):
            du = du_refs[0][...]
        else:
            for i, (d, du_ref) in enumerate(zip(dilations, du_refs)):
                for j in range(D_MODEL // LANES):
                    lanes = slice(j * LANES, (j + 1) * LANES)
                    if d == 1:
                        du_scr[j] = du_ref[:, lanes] if i == 0 else du_scr[j] + du_ref[:, lanes]
                        continue
                    blk = du_scr.at[j]
                    for r in range(d):
                        rows = _class_rows(r, d, tm)
                        blk[rows, :] = du_ref[r, :, lanes] if i == 0 else blk[rows, :] + du_ref[r, :, lanes]
            du = jnp.concatenate([du_scr[j] for j in range(D_MODEL // LANES)], axis=1)
        xv = x_ref[...]
        rstd = lax.rsqrt(jnp.mean(xv * xv, axis=-1, keepdims=True) + NORM_EPS)
        n = xv * rstd
        dg_ref[...] += jnp.sum(du * n, axis=0, keepdims=True)
        dn = du * g_ref[...]
        dx = dres_ref[...] + rstd * (dn - n * jnp.mean(dn * n, axis=-1, keepdims=True))
        dx_ref[...] = dx
        dxb_ref[...] = dx.astype(BF16)

    row = pl.BlockSpec((tm, D_MODEL), lambda i: (i, 0))
    vec = pl.BlockSpec((1, D_MODEL), lambda i: (0, 0))
    return pl.pallas_call(
        body,
        out_shape=(jax.ShapeDtypeStruct((T, D_MODEL), F32), jax.ShapeDtypeStruct((T, D_MODEL), BF16),
                   jax.ShapeDtypeStruct((1, D_MODEL), F32)),
        grid=(T // tm,), in_specs=[row, vec] + [_residue_spec(d, tm, D_MODEL) for d in dilations] + [row],
        out_specs=(row, row, vec), scratch_shapes=[pltpu.VMEM((D_MODEL // LANES, tm, LANES), F32)],
        compiler_params=_params("arbitrary"), name=name)(
            x, gain, *[_residue_view(du, d) for du, d in zip(dus, dilations)], dres)


def _loss_head(h, target, gain, name):
    T = h.shape[0]
    tm = _pick_tile(T, 512, 16)
    inv_f = 1.0 / D_MODEL

    def body(h_ref, t_ref, g_ref, dh_ref, dhb_ref, dg_ref, loss_ref):
        @pl.when(pl.program_id(0) == 0)
        def _():
            dg_ref[...] = jnp.zeros_like(dg_ref)
            loss_ref[...] = jnp.zeros_like(loss_ref)

        hv = h_ref[...]
        g = g_ref[...]
        rstd = lax.rsqrt(jnp.mean(hv * hv, axis=-1, keepdims=True) + NORM_EPS)
        n = hv * rstd
        err = n * g - t_ref[...]
        loss_ref[...] += (0.5 * inv_f) * jnp.sum(err * err, axis=0, keepdims=True)
        dy = err * inv_f
        dg_ref[...] += jnp.sum(dy * n, axis=0, keepdims=True)
        dn = dy * g
        dh = rstd * (dn - n * jnp.mean(dn * n, axis=-1, keepdims=True))
        dh_ref[...] = dh
        dhb_ref[...] = dh.astype(BF16)

    row = pl.BlockSpec((tm, D_MODEL), lambda i: (i, 0))
    vec = pl.BlockSpec((1, D_MODEL), lambda i: (0, 0))
    return pl.pallas_call(
        body,
        out_shape=(jax.ShapeDtypeStruct((T, D_MODEL), F32), jax.ShapeDtypeStruct((T, D_MODEL), BF16),
                   jax.ShapeDtypeStruct((1, D_MODEL), F32), jax.ShapeDtypeStruct((1, D_MODEL), F32)),
        grid=(T // tm,), in_specs=[row, row, vec], out_specs=(row, row, vec, vec),
        compiler_params=_params("arbitrary"), name=name)(h, target, gain)


FFN_TILE = 256


def _ffn_in(h, gain, w_in, name):
    T = h.shape[0]
    tm = _pick_tile(T, ROW_TILE, 16)

    def body(h_ref, g_ref, w_ref, n_ref, gate_ref, up_ref, a_ref):
        hv = h_ref[...]
        rstd = lax.rsqrt(jnp.mean(hv * hv, axis=-1, keepdims=True) + NORM_EPS)
        n = (hv * rstd * g_ref[...]).astype(BF16)
        n_ref[...] = n
        for c0 in range(0, D_FF, FFN_TILE):
            cols = slice(c0, c0 + FFN_TILE)
            gate = _dot(n, w_ref[c0:c0 + FFN_TILE, :], NT)
            up = _dot(n, w_ref[D_FF + c0:D_FF + c0 + FFN_TILE, :], NT)
            gate_ref[:, cols] = gate.astype(BF16)
            up_ref[:, cols] = up.astype(BF16)
            a_ref[:, cols] = (gate * _sigmoid(gate) * up).astype(BF16)

    row = pl.BlockSpec((tm, D_MODEL), lambda i: (i, 0))
    wide = pl.BlockSpec((tm, D_FF), lambda i: (i, 0))
    wide_shape = jax.ShapeDtypeStruct((T, D_FF), BF16)
    return pl.pallas_call(
        body, out_shape=(jax.ShapeDtypeStruct((T, D_MODEL), BF16), wide_shape, wide_shape, wide_shape),
        grid=(T // tm,),
        in_specs=[row, pl.BlockSpec((1, D_MODEL), lambda i: (0, 0)), _whole((2 * D_FF, D_MODEL), lambda i: (0, 0))],
        out_specs=(row, wide, wide, wide), compiler_params=_params("parallel"), name=name)(h, gain, w_in)


def _ffn_down_dx(dhb, w_down, gate, up, name):
    T = dhb.shape[0]
    tm = _pick_tile(T, ROW_TILE, 16)

    def body(dh_ref, w_ref, gate_ref, up_ref, dgate_ref, dup_ref):
        dh = dh_ref[...]
        for c0 in range(0, D_FF, FFN_TILE):
            cols = slice(c0, c0 + FFN_TILE)
            da = _dot(dh, w_ref[c0:c0 + FFN_TILE, :], NT)
            gate = gate_ref[:, cols].astype(F32)
            sg = _sigmoid(gate)
            dgate_ref[:, cols] = (da * up_ref[:, cols].astype(F32) * (sg * (1.0 + gate * (1.0 - sg)))).astype(BF16)
            dup_ref[:, cols] = (da * gate * sg).astype(BF16)

    wide = pl.BlockSpec((tm, D_FF), lambda i: (i, 0))
    wide_shape = jax.ShapeDtypeStruct((T, D_FF), BF16)
    return pl.pallas_call(
        body, out_shape=(wide_shape, wide_shape), grid=(T // tm,),
        in_specs=[pl.BlockSpec((tm, D_MODEL), lambda i: (i, 0)), _whole((D_FF, D_MODEL), lambda i: (0, 0)), wide, wide],
        out_specs=(wide, wide), compiler_params=_params("parallel"), name=name)(dhb, w_down, gate, up)


def _tri(n, lower):
    r = lax.broadcasted_iota(jnp.int32, (n, n), 0)
    c = lax.broadcasted_iota(jnp.int32, (n, n), 1)
    return (c <= r) if lower else (c >= r)


def _running_sum(x, lower):
    n = x.shape[0]
    tri = _tri(n, lower).astype(F32)
    return lax.dot_general(tri, x, NN, precision=lax.Precision.HIGHEST, preferred_element_type=F32)


def _hgrn_gates(q_raw, f_raw, lb):
    C = q_raw.shape[0]
    sig_f = _sigmoid(f_raw)
    forget = lb + (1.0 - lb) * sig_f
    key = 1.0 - forget
    log_f = jnp.log(forget)
    b = _running_sum(log_f, True)
    first_half = lax.broadcasted_iota(jnp.int32, log_f.shape, 0) < C // 2
    r = jnp.sum(jnp.where(first_half, log_f, 0.0), axis=0, keepdims=True)
    b_last = jnp.sum(log_f, axis=0, keepdims=True)
    e_a = jnp.exp(jnp.minimum(b - r, HGRN_EXP_CLAMP))
    e_b = jnp.exp(jnp.minimum(r - b, HGRN_EXP_CLAMP))
    e_q = jnp.exp(b)
    e_k = jnp.exp(b_last - b)
    sig_q = _sigmoid(q_raw)
    query = q_raw * sig_q
    return dict(sig_f=sig_f, forget=forget, sig_q=sig_q, e_a=e_a, e_b=e_b, e_q=e_q, e_k=e_k,
                e_last=jnp.exp(b_last), q_a=query * e_a, k_b=key * e_b, q_hat=query * e_q, k_til=key * e_k)


def _hgrn_fwd(proj, lb, gain, name):
    T = proj.shape[0]
    C = HGRN_CHUNK
    H, HD = HGRN_HEADS, HGRN_DIM

    def body(q_ref, f_ref, i_ref, g_ref, lb_ref, gain_ref, og_ref, o_ref, st_ref, s_scr):
        @pl.when(pl.program_id(0) == 0)
        def _():
            s_scr[...] = jnp.zeros_like(s_scr)

        st_ref[0] = s_scr[...]
        gt = _hgrn_gates(q_ref[...], f_ref[...], lb_ref[...])
        causal = _tri(C, True)
        gain_v = gain_ref[...]
        for h in range(H):
            sl = slice(h * HD, (h + 1) * HD)
            v = i_ref[:, sl].astype(BF16)
            p = jnp.where(causal, _dot(gt["q_a"][:, sl].astype(BF16), gt["k_b"][:, sl].astype(BF16), NT), 0.0)
            s_t = s_scr[h]
            o = _dot(p.astype(BF16), v, NN) + _dot(gt["q_hat"][:, sl].astype(BF16), s_t.astype(BF16), NT)
            s_scr[h] = gt["e_last"][:, sl] * s_t + _dot(v, gt["k_til"][:, sl].astype(BF16), TN)
            o_ref[:, sl] = o
            rstd = lax.rsqrt(jnp.mean(o * o, axis=-1, keepdims=True) + NORM_EPS)
            g_raw = g_ref[:, sl]
            og_ref[:, sl] = (o * rstd * gain_v * (g_raw * _sigmoid(g_raw))).astype(BF16)

    col = lambda j: pl.BlockSpec((C, D_MODEL), lambda c: (c, j))
    row = pl.BlockSpec((C, D_MODEL), lambda c: (c, 0))
    return pl.pallas_call(
        body,
        out_shape=(jax.ShapeDtypeStruct((T, D_MODEL), BF16), jax.ShapeDtypeStruct((T, D_MODEL), F32),
                   jax.ShapeDtypeStruct((T // C, H, HD, HD), F32)),
        grid=(T // C,),
        in_specs=[col(0), col(1), col(2), col(3), pl.BlockSpec((1, D_MODEL), lambda c: (0, 0)),
                  pl.BlockSpec((1, HD), lambda c: (0, 0))],
        out_specs=(row, row, pl.BlockSpec((1, H, HD, HD), lambda c: (c, 0, 0, 0))),
        scratch_shapes=[pltpu.VMEM((H, HD, HD), F32)],
        compiler_params=_params("arbitrary"), name=name)(proj, proj, proj, proj, lb, gain)


def _hgrn_bwd(proj, o_pre, d_og, states, lb, gain, name):
    T = proj.shape[0]
    C = HGRN_CHUNK
    H, HD = HGRN_HEADS, HGRN_DIM
    NC = T // C

    def body(q_ref, f_ref, i_ref, g_ref, o_ref, dog_ref, st_ref, lb_ref, gain_ref,
             dproj_ref, dlb_ref, dgain_ref, ds_scr, dq_scr, dk_scr, db_scr):
        @pl.when(pl.program_id(0) == 0)
        def _():
            ds_scr[...] = jnp.zeros_like(ds_scr)
            dlb_ref[...] = jnp.zeros_like(dlb_ref)
            dgain_ref[...] = jnp.zeros_like(dgain_ref)

        lbv = lb_ref[...]
        q_raw = q_ref[...]
        gt = _hgrn_gates(q_raw, f_ref[...], lbv)
        causal = _tri(C, True)
        last_row = lax.broadcasted_iota(jnp.int32, (C, HD), 0) == C - 1
        gain_v = gain_ref[...]
        dgain = jnp.zeros((1, HD), F32)
        for h in range(H):
            sl = slice(h * HD, (h + 1) * HD)
            o = o_ref[:, sl]
            rstd = lax.rsqrt(jnp.mean(o * o, axis=-1, keepdims=True) + NORM_EPS)
            n = o * rstd
            g_raw = g_ref[:, sl]
            sg = _sigmoid(g_raw)
            d_out = dog_ref[:, sl]
            dproj_ref[:, 3 * D_MODEL + h * HD:3 * D_MODEL + (h + 1) * HD] = (
                d_out * n * gain_v * (sg * (1.0 + g_raw * (1.0 - sg)))).astype(BF16)
            dy = d_out * (g_raw * sg)
            dgain = dgain + jnp.sum(dy * n, axis=0, keepdims=True)
            dn = dy * gain_v
            do = (rstd * (dn - n * jnp.mean(dn * n, axis=-1, keepdims=True))).astype(BF16)
            q_a, k_b = gt["q_a"][:, sl], gt["k_b"][:, sl]
            q_hat, k_til = gt["q_hat"][:, sl], gt["k_til"][:, sl]
            q_ab, k_bb = q_a.astype(BF16), k_b.astype(BF16)
            v = i_ref[:, sl].astype(BF16)
            s_t = st_ref[0, h]
            ds_t = ds_scr[h]
            ds_b = ds_t.astype(BF16)
            e_last = gt["e_last"][:, sl]
            p = jnp.where(causal, _dot(q_ab, k_bb, NT), 0.0).astype(BF16)
            dp = jnp.where(causal, _dot(do, v, NT), 0.0).astype(BF16)
            dv = _dot(p, do, TN) + _dot(k_til.astype(BF16), ds_b, NT)
            dq_a = _dot(dp, k_bb, NN)
            dk_b = _dot(dp, q_ab, TN)
            dq_hat = _dot(do, s_t.astype(BF16), NN)
            dk_til = _dot(v, ds_b, NN)
            ds_scr[h] = _dot(do, q_hat.astype(BF16), TN) + e_last * ds_t
            db_last = jnp.sum(ds_t * e_last * s_t, axis=0, keepdims=True) + jnp.sum(
                dk_til * k_til, axis=0, keepdims=True)
            dproj_ref[:, 2 * D_MODEL + h * HD:2 * D_MODEL + (h + 1) * HD] = dv.astype(BF16)
            dq_scr[:, sl] = dq_a * gt["e_a"][:, sl] + dq_hat * gt["e_q"][:, sl]
            dk_scr[:, sl] = dk_b * gt["e_b"][:, sl] + dk_til * gt["e_k"][:, sl]
            db = dq_a * q_ab.astype(F32) + dq_hat * q_hat - dk_b * k_bb.astype(F32) - dk_til * k_til
            db_scr[:, sl] = db + jnp.where(last_row, db_last, 0.0)
        dgain_ref[...] += dgain
        dlogf = _running_sum(db_scr[...], False)
        sig_f, forget, sig_q = gt["sig_f"], gt["forget"], gt["sig_q"]
        dforget = dlogf / forget - dk_scr[...]
        dproj_ref[:, D_MODEL:2 * D_MODEL] = (dforget * (1.0 - lbv) * sig_f * (1.0 - sig_f)).astype(BF16)
        dlb_ref[...] += jnp.sum(dforget * (1.0 - sig_f), axis=0, keepdims=True)
        dproj_ref[:, 0:D_MODEL] = (dq_scr[...] * (sig_q * (1.0 + q_raw * (1.0 - sig_q)))).astype(BF16)

    col = lambda j: pl.BlockSpec((C, D_MODEL), lambda c: (NC - 1 - c, j))
    row = pl.BlockSpec((C, D_MODEL), lambda c: (NC - 1 - c, 0))
    return pl.pallas_call(
        body,
        out_shape=(jax.ShapeDtypeStruct((T, 4 * D_MODEL), BF16), jax.ShapeDtypeStruct((1, D_MODEL), F32),
                   jax.ShapeDtypeStruct((1, HD), F32)),
        grid=(NC,),
        in_specs=[col(0), col(1), col(2), col(3), row, row,
                  pl.BlockSpec((1, H, HD, HD), lambda c: (NC - 1 - c, 0, 0, 0)),
                  pl.BlockSpec((1, D_MODEL), lambda c: (0, 0)), pl.BlockSpec((1, HD), lambda c: (0, 0))],
        out_specs=(pl.BlockSpec((C, 4 * D_MODEL), lambda c: (NC - 1 - c, 0)),
                   pl.BlockSpec((1, D_MODEL), lambda c: (0, 0)), pl.BlockSpec((1, HD), lambda c: (0, 0))),
        scratch_shapes=[pltpu.VMEM((H, HD, HD), F32), pltpu.VMEM((C, D_MODEL), F32),
                        pltpu.VMEM((C, D_MODEL), F32), pltpu.VMEM((C, D_MODEL), F32)],
        compiler_params=_params("arbitrary"), name=name)(proj, proj, proj, proj, o_pre, d_og, states, lb, gain)


def _attn_masks():
    r = lax.broadcasted_iota(jnp.int32, (ATTN_BLOCK, ATTN_BLOCK), 0)
    c = lax.broadcasted_iota(jnp.int32, (ATTN_BLOCK, ATTN_BLOCK), 1)
    return c >= r, c <= r


def _attn_fwd(qkv, dilation, name):
    T = qkv.shape[0]
    nb = T // dilation // ATTN_BLOCK
    W = ATTN_GROUP_WIDTH
    scale = ATTN_DIM ** -0.5

    def body(q_ref, kp_ref, kc_ref, vp_ref, vc_ref, o_ref, lse_ref):
        no_prev = jnp.where(pl.program_id(1) > 0, 0.0, NEG_BIG)
        m_prev, m_cur = _attn_masks()
        for h in range(ATTN_GROUP_HEADS):
            sl = slice(h * ATTN_DIM, (h + 1) * ATTN_DIM)
            q = q_ref[:, sl]
            s_p = jnp.where(m_prev, _dot(q, kp_ref[:, sl], NT) * scale + no_prev, NEG_BIG)
            s_c = jnp.where(m_cur, _dot(q, kc_ref[:, sl], NT) * scale, NEG_BIG)
            m = jnp.maximum(jnp.max(s_p, axis=-1, keepdims=True), jnp.max(s_c, axis=-1, keepdims=True))
            p_p = jnp.exp(s_p - m)
            p_c = jnp.exp(s_c - m)
            l = jnp.sum(p_p, axis=-1, keepdims=True) + jnp.sum(p_c, axis=-1, keepdims=True)
            acc = _dot(p_p.astype(BF16), vp_ref[:, sl], NN) + _dot(p_c.astype(BF16), vc_ref[:, sl], NN)
            o_ref[:, sl] = acc / l
            lse_ref[:, sl] = jnp.broadcast_to(m + jnp.log(l), (ATTN_BLOCK, ATTN_DIM))

    blk = lambda col, prev: pl.BlockSpec(
        (ATTN_BLOCK, W), lambda s, n: (s * nb + (jnp.maximum(n - 1, 0) if prev else n), col))
    out = pl.BlockSpec((ATTN_BLOCK, W), lambda s, n: (s * nb + n, 0))
    return pl.pallas_call(
        body, out_shape=(jax.ShapeDtypeStruct((T, W), F32),) * 2, grid=(dilation, nb),
        in_specs=[blk(0, False), blk(1, True), blk(1, False), blk(2, True), blk(2, False)],
        out_specs=(out, out), compiler_params=_params("parallel", "arbitrary"), name=name)(qkv, qkv, qkv, qkv, qkv)


def _attn_bwd(qkv, d_out, lse, delta, cos, sin, dilation, name):
    T = qkv.shape[0]
    nb = T // dilation // ATTN_BLOCK
    W = ATTN_GROUP_WIDTH
    scale = ATTN_DIM ** -0.5

    def unrope(x, cos_v, sin_v):
        return x * cos_v + pltpu.roll(x * sin_v, ATTN_DIM // 2, 1)

    def body(q_ref, kp_ref, kc_ref, vp_ref, vc_ref, do_ref, lse_ref, dl_ref, cos_ref, sin_ref,
             out_ref, dq_scr, dk_scr, dv_scr):
        n = pl.program_id(1)
        cos_v, sin_v = cos_ref[...], sin_ref[...]

        @pl.when(n > 0)
        def _():
            for h in range(ATTN_GROUP_HEADS):
                sl = slice(h * ATTN_DIM, (h + 1) * ATTN_DIM)
                out_ref[:, sl] = unrope(dq_scr[:, sl], cos_v, sin_v).astype(BF16)

        @pl.when(n == nb)
        def _():
            for h in range(ATTN_GROUP_HEADS):
                sl = slice(h * ATTN_DIM, (h + 1) * ATTN_DIM)
                out_ref[:, W + h * ATTN_DIM:W + (h + 1) * ATTN_DIM] = unrope(dk_scr[:, sl], cos_v, sin_v).astype(BF16)
                out_ref[:, 2 * W + h * ATTN_DIM:2 * W + (h + 1) * ATTN_DIM] = dv_scr[:, sl].astype(BF16)

        @pl.when(n == 0)
        def _():
            dk_scr[...] = jnp.zeros_like(dk_scr)
            dv_scr[...] = jnp.zeros_like(dv_scr)

        @pl.when(n < nb)
        def _():
            has_prev = n > 0
            no_prev = jnp.where(has_prev, 0.0, NEG_BIG)
            m_prev, m_cur = _attn_masks()
            for h in range(ATTN_GROUP_HEADS):
                sl = slice(h * ATTN_DIM, (h + 1) * ATTN_DIM)
                q, k_p, k_c, v_p, v_c = q_ref[:, sl], kp_ref[:, sl], kc_ref[:, sl], vp_ref[:, sl], vc_ref[:, sl]
                do = do_ref[:, sl]
                lse_v, dl_v = lse_ref[:, sl], dl_ref[:, sl]
                p_p = jnp.where(m_prev, jnp.exp(_dot(q, k_p, NT) * scale - lse_v + no_prev), 0.0)
                p_c = jnp.where(m_cur, jnp.exp(_dot(q, k_c, NT) * scale - lse_v), 0.0)
                ds_p = (p_p * (_dot(do, v_p, NT) - dl_v) * scale).astype(BF16)
                ds_c = (p_c * (_dot(do, v_c, NT) - dl_v) * scale).astype(BF16)
                dk_prev = dk_scr[:, sl] + _dot(ds_p, q, TN)
                dv_prev = dv_scr[:, sl] + _dot(p_p.astype(BF16), do, TN)
                out_ref[:, W + h * ATTN_DIM:W + (h + 1) * ATTN_DIM] = unrope(dk_prev, cos_v, sin_v).astype(BF16)
                out_ref[:, 2 * W + h * ATTN_DIM:2 * W + (h + 1) * ATTN_DIM] = dv_prev.astype(BF16)
                dq_scr[:, sl] = _dot(ds_p, k_p, NN) + _dot(ds_c, k_c, NN)
                dk_scr[:, sl] = _dot(ds_c, q, TN)
                dv_scr[:, sl] = _dot(p_c.astype(BF16), do, TN)

    def cur(n):
        return jnp.minimum(n, nb - 1)

    def late(n):
        return jnp.maximum(n - 1, 0)

    qkv_blk = lambda col, prev: pl.BlockSpec(
        (ATTN_BLOCK, W), lambda s, n: (s * nb + (jnp.maximum(cur(n) - 1, 0) if prev else cur(n)), col))
    row = pl.BlockSpec((ATTN_BLOCK, W), lambda s, n: (s * nb + cur(n), 0))
    tab = pl.BlockSpec((ATTN_BLOCK, ATTN_DIM), lambda s, n: (s * nb + late(n), 0))
    return pl.pallas_call(
        body, out_shape=jax.ShapeDtypeStruct((T, 3 * W), BF16), grid=(dilation, nb + 1),
        in_specs=[qkv_blk(0, False), qkv_blk(1, True), qkv_blk(1, False), qkv_blk(2, True), qkv_blk(2, False),
                  row, row, row, tab, tab],
        out_specs=pl.BlockSpec((ATTN_BLOCK, 3 * W), lambda s, n: (s * nb + late(n), 0)),
        scratch_shapes=[pltpu.VMEM((ATTN_BLOCK, W), F32)] * 3,
        compiler_params=_params("parallel", "arbitrary"), name=name)(
            qkv, qkv, qkv, qkv, qkv, d_out, lse, delta, cos, sin)


PERM_TILE = 512
LANES = 128


def _residue_view(x, d):
    return x if d == 1 else x.reshape(d, x.shape[0] // d, x.shape[1])


def _residue_spec(d, tm, cols):
    if d == 1:
        return pl.BlockSpec((tm, cols), lambda i: (i, 0))
    return pl.BlockSpec((d, tm // d, cols), lambda i: (0, i, 0))


def _residue_shape(T, d, cols, dtype):
    return jax.ShapeDtypeStruct((T, cols) if d == 1 else (d, T // d, cols), dtype)


def _class_rows(r, d, tm):
    return pl.ds(r, tm // d, stride=d)


def _attn_norm(h, gain, cos, sin, name):
    T = h.shape[0]
    tm = _pick_tile(T, PERM_TILE, 16 * max(ATTN_DILATIONS))
    dils = ATTN_DILATIONS

    def body(h_ref, g_ref, cos_ref, sin_ref, *refs):
        u_refs, c_refs, s_refs, u_scr = refs[0:3], refs[3:6], refs[6:9], refs[9]
        hv = h_ref[...]
        rstd = lax.rsqrt(jnp.mean(hv * hv, axis=-1, keepdims=True) + NORM_EPS)
        u = hv * rstd * g_ref[...]
        for j in range(D_MODEL // LANES):
            u_scr[j] = u[:, j * LANES:(j + 1) * LANES]
        for d, u_ref, c_ref, s_ref in zip(dils, u_refs, c_refs, s_refs):
            if d == 1:
                u_ref[...] = u.astype(BF16)
                c_ref[...] = cos_ref[...]
                s_ref[...] = sin_ref[...]
                continue
            for r in range(d):
                rows = _class_rows(r, d, tm)
                for j in range(D_MODEL // LANES):
                    u_ref[r, :, j * LANES:(j + 1) * LANES] = u_scr.at[j][rows, :].astype(BF16)
                c_ref[r] = cos_ref[rows, :]
                s_ref[r] = sin_ref[rows, :]

    row = pl.BlockSpec((tm, D_MODEL), lambda i: (i, 0))
    tab = pl.BlockSpec((tm, ATTN_DIM), lambda i: (i, 0))
    res = pl.pallas_call(
        body,
        out_shape=([_residue_shape(T, d, D_MODEL, BF16) for d in dils]
                   + [_residue_shape(T, d, ATTN_DIM, F32) for d in dils] * 2),
        grid=(T // tm,), in_specs=[row, pl.BlockSpec((1, D_MODEL), lambda i: (0, 0)), tab, tab],
        out_specs=([_residue_spec(d, tm, D_MODEL) for d in dils] + [_residue_spec(d, tm, ATTN_DIM) for d in dils] * 2),
        scratch_shapes=[pltpu.VMEM((D_MODEL // LANES, tm, LANES), F32)],
        compiler_params=_params("parallel"), name=name)(h, gain, cos, sin)
    flat = [r.reshape(T, r.shape[-1]) for r in res]
    return flat[0:3], flat[3:6], flat[6:9]


def _attn_merge_fwd(outs, lses, name):
    T = outs[0].shape[0]
    W = ATTN_GROUP_WIDTH
    tm = _pick_tile(T, PERM_TILE, 16 * max(ATTN_DILATIONS))
    dils = ATTN_DILATIONS

    def body(*refs):
        o_refs, l_refs, oc_ref, lse_refs = refs[0:3], refs[3:6], refs[6], refs[7:10]
        o_scr, l_scr, t_scr = refs[10:13]
        nh = ATTN_GROUP_HEADS
        for g, d in enumerate(dils):
            for j in range(nh):
                lanes = slice(j * LANES, (j + 1) * LANES)
                if d == 1:
                    o_scr[g * nh + j] = o_refs[g][:, lanes]
                    l_scr[g * nh + j] = l_refs[g][:, lanes]
                    continue
                for r in range(d):
                    rows = _class_rows(r, d, tm)
                    o_scr.at[g * nh + j][rows, :] = o_refs[g][r, :, lanes]
                    l_scr.at[g * nh + j][rows, :] = l_refs[g][r, :, lanes]
        for j in range(nh):
            lanes = slice(j * LANES, (j + 1) * LANES)
            ls = [l_scr[g * nh + j] for g in range(3)]
            m = jnp.maximum(jnp.maximum(ls[0], ls[1]), ls[2])
            tot = m + jnp.log(jnp.exp(ls[0] - m) + jnp.exp(ls[1] - m) + jnp.exp(ls[2] - m))
            t_scr[j] = tot
            for g, d in enumerate(dils):
                oc_ref[:, g * W + j * LANES:g * W + (j + 1) * LANES] = (
                    o_scr[g * nh + j] * jnp.exp(ls[g] - tot)).astype(BF16)
                if d == 1:
                    lse_refs[g][:, lanes] = tot
                    continue
                for r in range(d):
                    lse_refs[g][r, :, lanes] = t_scr.at[j][_class_rows(r, d, tm), :]

    in_blk = [_residue_spec(d, tm, W) for d in dils]
    n_blk = 3 * ATTN_GROUP_HEADS
    res = pl.pallas_call(
        body, out_shape=[jax.ShapeDtypeStruct((T, 3 * W), BF16)] + [_residue_shape(T, d, W, F32) for d in dils],
        grid=(T // tm,), in_specs=in_blk * 2,
        out_specs=[pl.BlockSpec((tm, 3 * W), lambda i: (i, 0))] + in_blk,
        scratch_shapes=[pltpu.VMEM((n_blk, tm, LANES), F32), pltpu.VMEM((n_blk, tm, LANES), F32),
                        pltpu.VMEM((ATTN_GROUP_HEADS, tm, LANES), F32)],
        compiler_params=_params("parallel"), name=name)(
            *[_residue_view(o, d) for o, d in zip(outs, dils)], *[_residue_view(l, d) for l, d in zip(lses, dils)])
    return res[0], [r.reshape(T, W) for r in res[1:]]


def _attn_merge_bwd(d_oc, oc, name):
    T = d_oc.shape[0]
    W = ATTN_GROUP_WIDTH
    tm = _pick_tile(T, PERM_TILE, 16 * max(ATTN_DILATIONS))
    dils = ATTN_DILATIONS

    def body(d_ref, o_ref, *refs):
        delta_refs, db_refs, dl_scr, d_scr = refs[0:3], refs[3:6], refs[6], refs[7]
        nh = ATTN_GROUP_HEADS
        for j in range(nh):
            tot = jnp.zeros((tm, 1), F32)
            for g in range(3):
                cols = slice(g * W + j * LANES, g * W + (j + 1) * LANES)
                d_blk = d_ref[:, cols]
                d_scr[g * nh + j] = d_blk
                tot = tot + jnp.sum(d_blk * o_ref[:, cols].astype(F32), axis=-1, keepdims=True)
            dl_scr[j] = jnp.broadcast_to(tot, (tm, LANES))
        for g, d in enumerate(dils):
            for j in range(nh):
                lanes = slice(j * LANES, (j + 1) * LANES)
                if d == 1:
                    delta_refs[g][:, lanes] = dl_scr[j]
                    db_refs[g][:, lanes] = d_scr[g * nh + j].astype(BF16)
                    continue
                for r in range(d):
                    rows = _class_rows(r, d, tm)
                    delta_refs[g][r, :, lanes] = dl_scr.at[j][rows, :]
                    db_refs[g][r, :, lanes] = d_scr.at[g * nh + j][rows, :].astype(BF16)

    wide = pl.BlockSpec((tm, 3 * W), lambda i: (i, 0))
    out_blk = [_residue_spec(d, tm, W) for d in dils]
    res = pl.pallas_call(
        body, out_shape=[_residue_shape(T, d, W, F32) for d in dils] + [_residue_shape(T, d, W, BF16) for d in dils],
        grid=(T // tm,), in_specs=[wide, wide], out_specs=out_blk * 2,
        scratch_shapes=[pltpu.VMEM((ATTN_GROUP_HEADS, tm, LANES), F32),
                        pltpu.VMEM((3 * ATTN_GROUP_HEADS, tm, LANES), F32)],
        compiler_params=_params("parallel"), name=name)(d_oc, oc)
    flat = [r.reshape(T, W) for r in res]
    return flat[0:3], flat[3:6]


def _rope_tables(T):
    inv_freq = 1.0 / (ROPE_THETA ** (jnp.arange(0, ATTN_DIM, 2, dtype=F32) / ATTN_DIM))
    ang = jnp.arange(T, dtype=F32)[:, None] * inv_freq[None, :]
    cos, sin = jnp.cos(ang), jnp.sin(ang)
    return jnp.concatenate([cos, cos], axis=1), jnp.concatenate([-sin, sin], axis=1)


WEIGHT_GROUPS = {"hgrn": ("hgrn_in", "hgrn_out"), "ffn0": ("ffn_in0", "ffn_down0"),
                 "attn": ("qkv", "attn_out"), "ffn1": ("ffn_in1", "ffn_down1")}


def _local_step(x, target, norm_mix, norm_ffn, lb, out_gain, final_gain, fetch, publish):
    T = x.shape[0]
    g_mix = [norm_mix[0:1], norm_mix[1:2]]
    g_ffn = [norm_ffn[0:1], norm_ffn[1:2]]
    w = {}

    def whole(name):
        return [(w[name], w[name].shape[0], 0)]

    def qkv_parts(g):
        return [(w["qkv"], ATTN_GROUP_WIDTH, 3 * j + g) for j in range(3)]

    def ffn_fwd(h, layer, before):
        w.update(fetch(f"ffn{layer}", [before]))
        n, gate, up, a = _ffn_in(h, g_ffn[layer], w[f"ffn_in{layer}"], f"ffn{layer}_in")
        out = _mm_nn([a], [whole(f"ffn_down{layer}")], h, name=f"ffn{layer}_down")
        return out, (n, gate, up, a)

    def ffn_bwd(h, saved, dh, dhb, layer):
        n, gate, up, a = saved
        w_in = w[f"ffn_in{layer}"]
        dgate, dup = _ffn_down_dx(dhb, w[f"ffn_down{layer}"], gate, up, f"ffn{layer}_down_dx")
        grad_in = _mm_tn(dgate, n, name=f"ffn{layer}_in_dw_gate", rows=2 * D_FF)
        grad_in = _mm_tn(dup, n, name=f"ffn{layer}_in_dw_up", into=grad_in, row_tile=D_FF // GRAD_TILE, rows=2 * D_FF)
        grads = {f"ffn_down{layer}": _mm_tn(a, dhb, name=f"ffn{layer}_down_dw"), f"ffn_in{layer}": grad_in}
        zero = publish(f"ffn{layer}", grads)
        dn = _mm_nn([dgate, dup], [[(w_in, D_FF, 0)], [(w_in, D_FF, 1)]], None, name=f"ffn{layer}_in_dx")
        return _rms_bwd(h, g_ffn[layer] + zero, [dn], dh, f"ffn{layer}_norm_bwd")

    u0 = _rms_fwd(x, g_mix[0], "hgrn_norm")
    w.update(fetch("hgrn", [u0]))
    proj = _mm_nt(u0, whole("hgrn_in"), out_dtype=F32, name="hgrn_in")
    og, o_pre, states = _hgrn_fwd(proj, lb, out_gain, "hgrn_fwd")
    h1 = _mm_nn([og], [whole("hgrn_out")], x, name="hgrn_out")
    h2, ffn0 = ffn_fwd(h1, 0, og)

    cos, sin = _rope_tables(T)
    u1_g, cos_g, sin_g = _attn_norm(h2, g_mix[1], cos, sin, "attn_norm")
    w.update(fetch("attn", [u1_g[0]]))
    qkv_g, outs, lses = [], [], []
    for g, d in enumerate(ATTN_DILATIONS):
        qkv_g.append(_mm_nt(u1_g[g], qkv_parts(g), out_dtype=BF16, name=f"attn_qkv{g}",
                            rope=(cos_g[g], sin_g[g], 2)))
        o_g, lse_g = _attn_fwd(qkv_g[g], d, f"attn_fwd{g}")
        outs.append(o_g)
        lses.append(lse_g)
    oc, lse_all = _attn_merge_fwd(outs, lses, "attn_merge")
    h3 = _mm_nn([oc], [whole("attn_out")], h2, name="attn_out")
    h4, ffn1 = ffn_fwd(h3, 1, oc)

    dh4, dh4b, d_final, loss_part = _loss_head(h4, target, final_gain, "loss_head")
    dh3, dh3b, d_ffn1 = ffn_bwd(h3, ffn1, dh4, dh4b, 1)

    d_oc = _mm_nt(dh3b, whole("attn_out"), out_dtype=F32, name="attn_out_dx")
    grad_attn_out = _mm_tn(oc, dh3b, name="attn_out_dw")
    delta, d_ocb = _attn_merge_bwd(d_oc, oc, "attn_merge_bwd")
    du1, qkv_pieces = [], []
    for g, d in enumerate(ATTN_DILATIONS):
        dqkv = _attn_bwd(qkv_g[g], d_ocb[g], lse_all[g], delta[g], cos_g[g], sin_g[g], d, f"attn_bwd{g}")
        qkv_pieces.append(_mm_tn(dqkv, u1_g[g], name=f"attn_qkv_dw{g}"))
        du1.append(_mm_nn([dqkv], [qkv_parts(g)], None, name=f"attn_qkv_dx{g}"))
    grad_qkv = jnp.stack([p.reshape(3, ATTN_GROUP_WIDTH, D_MODEL) for p in qkv_pieces], axis=1).reshape(
        3 * ATTN_WIDTH, D_MODEL)
    zero = publish("attn", {"qkv": grad_qkv, "attn_out": grad_attn_out})
    dh2, dh2b, d_mix1 = _rms_bwd(h2, g_mix[1] + zero, du1, dh3, "attn_norm_bwd", ATTN_DILATIONS)

    dh1, dh1b, d_ffn0 = ffn_bwd(h1, ffn0, dh2, dh2b, 0)

    d_og = _mm_nt(dh1b, whole("hgrn_out"), out_dtype=F32, name="hgrn_out_dx")
    grad_hgrn_out = _mm_tn(og, dh1b, name="hgrn_out_dw")
    dproj, d_lb, d_out_gain = _hgrn_bwd(proj, o_pre, d_og, states, lb, out_gain, "hgrn_bwd")
    zero = publish("hgrn", {"hgrn_in": _mm_tn(dproj, u0, name="hgrn_in_dw"), "hgrn_out": grad_hgrn_out})
    du0 = _mm_nn([dproj], [whole("hgrn_in")], None, name="hgrn_in_dx")
    dx, _, d_mix0 = _rms_bwd(x, g_mix[0] + zero, [du0], dh1, "hgrn_norm_bwd")

    small = dict(norm_mix0=d_mix0, norm_mix1=d_mix1, norm_ffn0=d_ffn0, norm_ffn1=d_ffn1, lb=d_lb,
                 out_gain=d_out_gain, final=d_final, loss=loss_part)
    return dx, small


WEIGHT_NAMES = ("hgrn_in", "hgrn_out", "qkv", "attn_out", "ffn_in0", "ffn_in1", "ffn_down0", "ffn_down1")
MESH_IDS = pl.DeviceIdType.MESH
HBM_SPEC = pl.BlockSpec(memory_space=pl.ANY)


SEM_SPEC = pl.BlockSpec(memory_space=pltpu.SEMAPHORE)
LAND_SPEC = pl.BlockSpec(memory_space=pltpu.HBM)
N_PEERS = N_DEV - 1
PEER_OFFSETS = [(dx, dy, dc) for dx in (0, 1) for dy in (0, 1) for dc in (0, 1)][1:]


def _mesh_place():
    x, y, c = lax.axis_index("x"), lax.axis_index("y"), lax.axis_index("c")
    peers = []
    for dx, dy, dc in PEER_OFFSETS:
        px, py, pc = (1 - x if dx else x), (1 - y if dy else y), (1 - c if dc else c)
        peers.append(((px, py, pc), 4 * px + 2 * py + pc))
    return 4 * x + 2 * y + c, peers


def _start_copies(srcs, groups, scatter, name):
    nw, ng = len(srcs), len(groups)
    land_shapes = [(s.shape if scatter else (N_DEV,) + s.shape) for s in srcs]

    def body(*refs):
        src_refs, land_refs = refs[:nw], refs[nw:2 * nw]
        sems = refs[2 * nw:2 * nw + 2 * ng]
        local_sems = refs[-1]
        me, peers = _mesh_place()
        own = [pltpu.make_async_copy(src_refs[w].at[me] if scatter else src_refs[w], land_refs[w].at[me],
                                     local_sems.at[w]) for w in range(nw)]
        for cp in own:
            cp.start()
        for gi, group in enumerate(groups):
            for i, w in enumerate(group):
                for k, (peer, pid) in enumerate(peers):
                    pltpu.make_async_remote_copy(
                        src_ref=src_refs[w].at[pid] if scatter else src_refs[w], dst_ref=land_refs[w].at[me],
                        send_sem=sems[2 * gi].at[i * N_PEERS + k], recv_sem=sems[2 * gi + 1].at[i * N_PEERS + k],
                        device_id=peer, device_id_type=MESH_IDS).start()
        for cp in own:
            cp.wait()
        token_ref = refs[2 * nw + 2 * ng + 2 * nw]
        token_ref[...] = jnp.zeros_like(token_ref)

    sem_shapes = []
    for group in groups:
        sem_shapes += [pltpu.SemaphoreType.DMA((len(group) * N_PEERS,))] * 2
    out_shape = (sem_shapes + [pltpu.HBM(s.shape, s.dtype) for s in srcs]
                 + [pltpu.HBM(shape, s.dtype) for shape, s in zip(land_shapes, srcs)]
                 + [jax.ShapeDtypeStruct((8, 128), F32)])
    operands = [pltpu.with_memory_space_constraint(s, pltpu.HBM) for s in srcs]
    operands += [pltpu.with_memory_space_constraint(lax.empty(shape, s.dtype), pltpu.HBM)
                 for shape, s in zip(land_shapes, srcs)]
    res = pl.pallas_call(
        body, out_shape=out_shape, in_specs=[LAND_SPEC] * (2 * nw),
        out_specs=[SEM_SPEC] * (2 * ng) + [LAND_SPEC] * (2 * nw) + [pl.BlockSpec(memory_space=pltpu.VMEM)],
        input_output_aliases={i: 2 * ng + i for i in range(2 * nw)},
        scratch_shapes=[pltpu.SemaphoreType.DMA((nw,))],
        compiler_params=pltpu.CompilerParams(has_side_effects=pltpu.SideEffectType.DATAFLOW_SIDE_EFFECTING),
        name=name)(*operands)
    sems = [(res[2 * gi], res[2 * gi + 1]) for gi in range(ng)]
    return sems, list(res[2 * ng:2 * ng + nw]), list(res[2 * ng + nw:2 * ng + 2 * nw]), res[-1][0:1, 0:1]


def _wait_copies(sems, srcs, lands, after, scatter, name):
    n = len(srcs)

    def body(*refs):
        src_refs, land_refs = refs[:n], refs[n:2 * n]
        send_sems, recv_sems = refs[2 * n], refs[2 * n + 1]
        _, peers = _mesh_place()
        for i in range(n):
            for k, (peer, _) in enumerate(peers):
                copy = pltpu.make_async_remote_copy(
                    src_ref=src_refs[i].at[0] if scatter else src_refs[i], dst_ref=land_refs[i].at[0],
                    send_sem=send_sems.at[i * N_PEERS + k], recv_sem=recv_sems.at[i * N_PEERS + k],
                    device_id=peer, device_id_type=MESH_IDS)
                copy.wait_send()
                copy.wait_recv()

    arrays = list(srcs) + list(lands)
    res = pl.pallas_call(
        body, out_shape=[pltpu.HBM(a.shape, a.dtype) for a in arrays],
        in_specs=[LAND_SPEC] * (2 * n) + [SEM_SPEC] * 2 + [HBM_SPEC] * len(after),
        out_specs=[LAND_SPEC] * (2 * n), input_output_aliases={i: i for i in range(2 * n)},
        compiler_params=pltpu.CompilerParams(has_side_effects=pltpu.SideEffectType.DATAFLOW_SIDE_EFFECTING),
        name=name)(*arrays, sems[0], sems[1], *after)
    return list(res[n:])


def _gather_small(block, name):
    def body(in_ref, out_ref, send_sems, recv_sems, local_sem):
        me, peers = _mesh_place()
        own = pltpu.make_async_copy(in_ref, out_ref.at[me], local_sem)
        own.start()
        sends = [pltpu.make_async_remote_copy(
            src_ref=in_ref, dst_ref=out_ref.at[me], send_sem=send_sems.at[k], recv_sem=recv_sems.at[k],
            device_id=peer, device_id_type=MESH_IDS) for k, (peer, _) in enumerate(peers)]
        for cp in sends:
            cp.start()
        for cp in sends:
            cp.wait_recv()
        for cp in sends:
            cp.wait_send()
        own.wait()

    return pl.pallas_call(
        body, out_shape=jax.ShapeDtypeStruct((N_DEV,) + block.shape, block.dtype),
        in_specs=[HBM_SPEC], out_specs=HBM_SPEC,
        scratch_shapes=[pltpu.SemaphoreType.DMA((N_PEERS,)), pltpu.SemaphoreType.DMA((N_PEERS,)),
                        pltpu.SemaphoreType.DMA],
        name=name)(block)


def _sum_blocks(recv, name):
    rows = recv.shape[1]
    tr = _pick_tile(rows, 256, 16)

    def body(r_ref, g_ref):
        acc = r_ref[0].astype(F32)
        for j in range(1, N_DEV):
            acc = acc + r_ref[j].astype(F32)
        g_ref[...] = acc

    return pl.pallas_call(
        body, out_shape=jax.ShapeDtypeStruct((rows, D_MODEL), F32), grid=(rows // tr,),
        in_specs=[pl.BlockSpec((N_DEV, tr, D_MODEL), lambda i: (0, i, 0))],
        out_specs=pl.BlockSpec((tr, D_MODEL), lambda i: (i, 0)),
        compiler_params=_params("parallel"), name=name)(recv)


def _adamw_math(w, g, m, v):
    m_new = ADAM_B1 * m + (1.0 - ADAM_B1) * g
    v_new = ADAM_B2 * v + (1.0 - ADAM_B2) * (g * g)
    m_hat = m_new / (1.0 - ADAM_B1 ** ADAM_STEP)
    v_hat = v_new / (1.0 - ADAM_B2 ** ADAM_STEP)
    delta = -ADAM_LR * (m_hat / (jnp.sqrt(v_hat) + ADAM_EPS) + ADAM_WD * w)
    return delta, m_new, v_new


def _adamw(w, g, m, v, name):
    rows, cols = w.shape
    tr = _pick_tile(rows, 256, 8)

    def body(w_ref, g_ref, m_ref, v_ref, d_ref, mo_ref, vo_ref):
        d_ref[...], mo_ref[...], vo_ref[...] = _adamw_math(w_ref[...], g_ref[...], m_ref[...], v_ref[...])

    blk = pl.BlockSpec((tr, cols), lambda i: (i, 0))
    return pl.pallas_call(
        body, out_shape=(jax.ShapeDtypeStruct((rows, cols), F32),) * 3, grid=(rows // tr,),
        in_specs=[blk] * 4, out_specs=(blk,) * 3, compiler_params=_params("parallel"), name=name)(w, g, m, v)


ROW_MIX, ROW_FFN, ROW_LB, ROW_OUT_GAIN, ROW_FINAL = 0, 2, 4, 7, 8
PART_MIX, PART_FFN, PART_LB, PART_OUT_GAIN, PART_FINAL, PART_LOSS = 0, 2, 4, 5, 6, 7


def _small_update(parts_all, w, m, v, name):
    def body(p_ref, w_ref, m_ref, v_ref, g_ref, d_ref, mo_ref, vo_ref, loss_ref):
        def total(row, n=1):
            tot = p_ref[0, row:row + n, :]
            for j in range(1, N_DEV):
                tot = tot + p_ref[j, row:row + n, :]
            return tot

        logits = [w_ref[ROW_LB + i:ROW_LB + i + 1, :] for i in range(3)]
        mx = jnp.maximum(jnp.maximum(logits[0], logits[1]), logits[2])
        ex = [jnp.exp(l - mx) for l in logits]
        den = ex[0] + ex[1] + ex[2]
        prob = [e / den for e in ex]
        d_lb = total(PART_LB)
        g_ref[...] = jnp.zeros_like(g_ref)
        g_ref[ROW_MIX:ROW_MIX + 2, :] = total(PART_MIX, 2)
        g_ref[ROW_FFN:ROW_FFN + 2, :] = total(PART_FFN, 2)
        for i in range(3):
            g_ref[ROW_LB + i:ROW_LB + i + 1, :] = prob[i] * ((d_lb if i == 0 else 0.0) - prob[0] * d_lb)
        g_ref[ROW_OUT_GAIN:ROW_OUT_GAIN + 1, :] = total(PART_OUT_GAIN)
        g_ref[ROW_FINAL:ROW_FINAL + 1, :] = total(PART_FINAL)
        d_ref[...], mo_ref[...], vo_ref[...] = _adamw_math(w_ref[...], g_ref[...], m_ref[...], v_ref[...])
        loss_ref[...] = jnp.sum(total(PART_LOSS), axis=-1, keepdims=True)

    packed = jax.ShapeDtypeStruct((16, D_MODEL), F32)
    return pl.pallas_call(
        body, out_shape=(packed, packed, packed, packed, jax.ShapeDtypeStruct((1, 1), F32)),
        compiler_params=pltpu.CompilerParams(vmem_limit_bytes=VMEM_LIMIT), name=name)(parts_all, w, m, v)


def _pack_small(norm_mix, norm_ffn, lb_logits, out_gain, final):
    pad = jnp.zeros((1, D_MODEL - HGRN_DIM), F32)
    return jnp.concatenate([norm_mix, norm_ffn, lb_logits, jnp.concatenate([out_gain, pad], axis=1),
                            final.reshape(1, D_MODEL), jnp.zeros((16 - ROW_FINAL - 1, D_MODEL), F32)], axis=0)


def _unpack_small(p):
    return (p[ROW_MIX:ROW_MIX + 2], p[ROW_FFN:ROW_FFN + 2], p[ROW_LB:ROW_LB + 3],
            p[ROW_OUT_GAIN:ROW_OUT_GAIN + 1, :HGRN_DIM], p[ROW_FINAL])


def _lower_bound(lb_logits, name):
    def body(l_ref, o_ref):
        logits = [l_ref[i:i + 1, :] for i in range(3)]
        mx = jnp.maximum(jnp.maximum(logits[0], logits[1]), logits[2])
        ex = [jnp.exp(l - mx) for l in logits]
        o_ref[...] = ex[0] / (ex[0] + ex[1] + ex[2])

    return pl.pallas_call(body, out_shape=jax.ShapeDtypeStruct((1, D_MODEL), F32), name=name)(lb_logits)


def kernel(x, norm_mix, norm_ffn, hgrn_w_in, hgrn_lb_logits, hgrn_out_norm, hgrn_w_out, attn_w_qkv, attn_w_out, ffn_w_in, ffn_w_down, final_norm, loss_target, m_norm_mix, m_norm_ffn, m_hgrn_w_in, m_hgrn_lb_logits, m_hgrn_out_norm, m_hgrn_w_out, m_attn_w_qkv, m_attn_w_out, m_ffn_w_in, m_ffn_w_down, m_final_norm, v_norm_mix, v_norm_ffn, v_hgrn_w_in, v_hgrn_lb_logits, v_hgrn_out_norm, v_hgrn_w_out, v_attn_w_qkv, v_attn_w_out, v_ffn_w_in, v_ffn_w_down, v_final_norm):
    col_sharded = {"hgrn_in": hgrn_w_in[0], "qkv": attn_w_qkv[0], "ffn_in0": ffn_w_in[0], "ffn_in1": ffn_w_in[1]}
    row_sharded = {"hgrn_out": hgrn_w_out[0], "attn_out": attn_w_out[0], "ffn_down0": ffn_w_down[0],
                   "ffn_down1": ffn_w_down[1]}
    order = [n for group in WEIGHT_GROUPS.values() for n in group]
    shards = [(col_sharded[n].T if n in col_sharded else row_sharded[n]).astype(BF16) for n in order]
    index_groups = [[order.index(n) for n in group] for group in WEIGHT_GROUPS.values()]
    w_sems, w_srcs, w_lands, _ = _start_copies(shards, index_groups, False, "weights_gather_start")

    def fetch(group, after):
        gi = list(WEIGHT_GROUPS).index(group)
        idx = index_groups[gi]
        lands = _wait_copies(w_sems[gi], [w_srcs[i] for i in idx], [w_lands[i] for i in idx], after, False,
                             f"weights_gather_wait_{group}")
        return {n: land.reshape(-1, D_MODEL) for n, land in zip(WEIGHT_GROUPS[group], lands)}

    in_flight = {}

    def publish(group, grads):
        names = WEIGHT_GROUPS[group]
        parts = [grads[n].reshape(N_DEV, -1, D_MODEL) for n in names]
        sems, srcs, lands, zero = _start_copies(parts, [list(range(len(names)))], True, f"grads_send_start_{group}")
        in_flight[group] = (sems[0], srcs, lands)
        return zero

    lb = _lower_bound(hgrn_lb_logits, "hgrn_lower_bound")
    grad_x, small = _local_step(x[0], loss_target[0], norm_mix, norm_ffn, lb, hgrn_out_norm,
                                final_norm.reshape(1, D_MODEL), fetch, publish)

    pad = jnp.zeros((1, D_MODEL - HGRN_DIM), F32)
    small_part = jnp.concatenate(
        [small["norm_mix0"], small["norm_mix1"], small["norm_ffn0"], small["norm_ffn1"], small["lb"],
         jnp.concatenate([small["out_gain"], pad], axis=1), small["final"], small["loss"]], axis=0)
    small_all = _gather_small(small_part, "small_grads_gather")
    received = {}
    for group in ("ffn1", "attn", "ffn0", "hgrn"):
        sems, srcs, lands = in_flight[group]
        lands = _wait_copies(sems, srcs, lands, [small_all], True, f"grads_send_wait_{group}")
        received.update(zip(WEIGHT_GROUPS[group], lands))

    masters = {"hgrn_in": (hgrn_w_in[0], m_hgrn_w_in[0], v_hgrn_w_in[0]),
               "hgrn_out": (hgrn_w_out[0], m_hgrn_w_out[0], v_hgrn_w_out[0]),
               "qkv": (attn_w_qkv[0], m_attn_w_qkv[0], v_attn_w_qkv[0]),
               "attn_out": (attn_w_out[0], m_attn_w_out[0], v_attn_w_out[0]),
               "ffn_in0": (ffn_w_in[0], m_ffn_w_in[0], v_ffn_w_in[0]),
               "ffn_in1": (ffn_w_in[1], m_ffn_w_in[1], v_ffn_w_in[1]),
               "ffn_down0": (ffn_w_down[0], m_ffn_w_down[0], v_ffn_w_down[0]),
               "ffn_down1": (ffn_w_down[1], m_ffn_w_down[1], v_ffn_w_down[1])}
    res = {}
    for n in WEIGHT_NAMES:
        g = _sum_blocks(received[n], f"{n}_grad_sum")
        if n in col_sharded:
            g = g.T
        wv, mv, vv = masters[n]
        res[n] = (g,) + tuple(_adamw(wv, g, mv, vv, f"{n}_adamw"))

    def single(n):
        return [t[None] for t in res[n]]

    def pair(n):
        return [jnp.stack([a, b]) for a, b in zip(res[n + "0"], res[n + "1"])]

    big = dict(hgrn_w_in=single("hgrn_in"), hgrn_w_out=single("hgrn_out"), attn_w_qkv=single("qkv"),
               attn_w_out=single("attn_out"), ffn_w_in=pair("ffn_in"), ffn_w_down=pair("ffn_down"))

    w_small = _pack_small(norm_mix, norm_ffn, hgrn_lb_logits, hgrn_out_norm, final_norm)
    m_small = _pack_small(m_norm_mix, m_norm_ffn, m_hgrn_lb_logits, m_hgrn_out_norm, m_final_norm)
    v_small = _pack_small(v_norm_mix, v_norm_ffn, v_hgrn_lb_logits, v_hgrn_out_norm, v_final_norm)
    g_s, d_s, m_s, v_s, loss = _small_update(small_all, w_small, m_small, v_small, "small_update")
    small_out = [_unpack_small(t) for t in (g_s, d_s, m_s, v_s)]

    def group(i):
        s = small_out[i]
        return (s[0], s[1], big["hgrn_w_in"][i], s[2], s[3], big["hgrn_w_out"][i], big["attn_w_qkv"][i],
                big["attn_w_out"][i], big["ffn_w_in"][i], big["ffn_w_down"][i], s[4])

    return (loss.reshape(()), grad_x[None], *group(0), *group(1), *group(2), *group(3))
```

```python
import functools

import jax
import jax.numpy as jnp
from jax import lax
from jax.experimental import pallas as pl
from jax.experimental.pallas import tpu as pltpu
from jax.experimental.pallas import tpu_sc as plsc

F32 = jnp.float32
BF16 = jnp.bfloat16

D_MODEL = 1024
N_DEV = 8
NORM_EPS = 1e-6

HGRN_HEADS = 8
HGRN_DIM = 128
HGRN_CHUNK = 64
HGRN_EXP_CLAMP = 60.0

ATTN_DIM = 128
ATTN_BLOCK = 128
ATTN_GROUP_HEADS = 4
ATTN_GROUP_WIDTH = ATTN_GROUP_HEADS * ATTN_DIM
ATTN_DILATIONS = (1, 4, 16)
ATTN_WIDTH = 3 * ATTN_GROUP_WIDTH
ROPE_THETA = 10000.0
NEG_BIG = -1e30

D_FF = 2816

ADAM_LR = 0.001
ADAM_B1 = 0.9
ADAM_B2 = 0.999
ADAM_EPS = 1e-08
ADAM_WD = 0.01
ADAM_STEP = 10

VMEM_LIMIT = 48 * 1024 * 1024

NT = (((1,), (1,)), ((), ()))
NN = (((1,), (0,)), ((), ()))
TN = (((0,), (0,)), ((), ()))


def _dot(a, b, dims):
    return lax.dot_general(a, b, dims, preferred_element_type=F32)


def _params(*sem):
    return pltpu.CompilerParams(dimension_semantics=sem, vmem_limit_bytes=VMEM_LIMIT)


def _pick_tile(n, cap, mult):
    best = None
    for t in range(mult, min(n, cap) + 1, mult):
        if n % t == 0:
            best = t
    assert best is not None, (n, cap, mult)
    return best


def _sigmoid(x):
    return 1.0 / (1.0 + jnp.exp(-x))


ROW_TILE = 512
COL_CHUNK = 512
GRAD_TILE = 256


def _whole(shape, index_map):
    return pl.BlockSpec(shape, index_map, pipeline_mode=pl.Buffered(1))


def _part_specs(parts, n_cols):
    return [_whole((rows, n_cols), functools.partial(lambda i, b: (b, 0), b=blk)) for _, rows, blk in parts]


def _mm_nt(a, w_parts, *, out_dtype, name, rope=None):
    M, K = a.shape
    tm = _pick_tile(M, ROW_TILE, 16)
    widths = [rows for _, rows, _ in w_parts]
    n_parts = len(w_parts)

    def body(*refs):
        a_ref, w_refs, o_ref = refs[0], refs[1:1 + n_parts], refs[-1]
        av = a_ref[...]
        off = 0
        for p, w_ref in enumerate(w_refs):
            for c0 in range(0, widths[p], COL_CHUNK):
                cw = min(COL_CHUNK, widths[p] - c0)
                acc = _dot(av, w_ref[c0:c0 + cw, :], NT)
                if rope is not None and p < rope[2]:
                    cos, sin = refs[1 + n_parts][...], refs[2 + n_parts][...]
                    for h0 in range(0, cw, ATTN_DIM):
                        xh = acc[:, h0:h0 + ATTN_DIM]
                        rot = pltpu.roll(xh, ATTN_DIM // 2, 1)
                        o_ref[:, off + c0 + h0:off + c0 + h0 + ATTN_DIM] = (xh * cos + rot * sin).astype(out_dtype)
                else:
                    o_ref[:, off + c0:off + c0 + cw] = acc.astype(out_dtype)
            off += widths[p]

    in_specs = [pl.BlockSpec((tm, K), lambda i: (i, 0))] + _part_specs(w_parts, K)
    args = [a] + [w for w, _, _ in w_parts]
    if rope is not None:
        in_specs += [pl.BlockSpec((tm, ATTN_DIM), lambda i: (i, 0))] * 2
        args += [rope[0], rope[1]]
    return pl.pallas_call(
        body, out_shape=jax.ShapeDtypeStruct((M, sum(widths)), out_dtype), grid=(M // tm,),
        in_specs=in_specs, out_specs=pl.BlockSpec((tm, sum(widths)), lambda i: (i, 0)),
        compiler_params=_params("parallel"), name=name)(*args)


def _mm_nn(a_list, w_parts_list, resid, *, name):
    M = a_list[0].shape[0]
    tm = _pick_tile(M, ROW_TILE, 16)
    n_a = len(a_list)
    flat_parts = [p for parts in w_parts_list for p in parts]

    def body(*refs):
        a_refs, w_refs, o_ref = refs[:n_a], refs[n_a:n_a + len(flat_parts)], refs[-1]
        acc = None
        wi = 0
        for a_ref, parts in zip(a_refs, w_parts_list):
            off = 0
            for _, rows, _ in parts:
                term = _dot(a_ref[:, off:off + rows], w_refs[wi][...], NN)
                acc = term if acc is None else acc + term
                off += rows
                wi += 1
        if resid is not None:
            acc = acc + refs[-2][...]
        o_ref[...] = acc

    row = pl.BlockSpec((tm, D_MODEL), lambda i: (i, 0))
    in_specs = [pl.BlockSpec((tm, a.shape[1]), lambda i: (i, 0)) for a in a_list] + _part_specs(flat_parts, D_MODEL)
    args = list(a_list) + [w for w, _, _ in flat_parts]
    if resid is not None:
        in_specs.append(row)
        args.append(resid)
    return pl.pallas_call(
        body, out_shape=jax.ShapeDtypeStruct((M, D_MODEL), F32), grid=(M // tm,),
        in_specs=in_specs, out_specs=row, compiler_params=_params("parallel"), name=name)(*args)


def _mm_tn(a, b, *, name, into=None, row_tile=0, rows=None):
    T, R = a.shape
    N = b.shape[1]
    tr = GRAD_TILE
    rows = R if rows is None else rows

    def body(a_ref, b_ref, *refs):
        refs[-1][...] = _dot(a_ref[...], b_ref[...], TN).astype(BF16)

    in_specs = [pl.BlockSpec((T, tr), lambda r: (0, r)), _whole((T, N), lambda r: (0, 0))]
    args = [a, b]
    if into is not None:
        in_specs.append(HBM_SPEC)
        args.append(into)
    return pl.pallas_call(
        body, out_shape=jax.ShapeDtypeStruct((rows, N), BF16), grid=(R // tr,),
        in_specs=in_specs, out_specs=pl.BlockSpec((tr, N), lambda r: (row_tile + r, 0)),
        input_output_aliases={} if into is None else {2: 0},
        compiler_params=_params("parallel"), name=name)(*args)


def _rms_fwd(x, gain, name):
    T = x.shape[0]
    tm = _pick_tile(T, 512, 16)

    def body(x_ref, g_ref, u_ref):
        xv = x_ref[...]
        rstd = lax.rsqrt(jnp.mean(xv * xv, axis=-1, keepdims=True) + NORM_EPS)
        u_ref[...] = (xv * rstd * g_ref[...]).astype(BF16)

    return pl.pallas_call(
        body, out_shape=jax.ShapeDtypeStruct((T, D_MODEL), BF16), grid=(T // tm,),
        in_specs=[pl.BlockSpec((tm, D_MODEL), lambda i: (i, 0)), pl.BlockSpec((1, D_MODEL), lambda i: (0, 0))],
        out_specs=pl.BlockSpec((tm, D_MODEL), lambda i: (i, 0)),
        compiler_params=_params("parallel"), name=name)(x, gain)


def _rms_bwd(x, gain, dus, dres, name, dilations=(1,)):
    T = x.shape[0]
    tm = _pick_tile(T, PERM_TILE, 16 * max(dilations))
    n_du = len(dus)

    def body(x_ref, g_ref, *refs):
        du_refs, dres_ref = refs[:n_du], refs[n_du]
        dx_ref, dxb_ref, dg_ref, du_scr = refs[n_du + 1:]

        @pl.when(pl.program_id(0) == 0)
        def _():
            dg_ref[...] = jnp.zeros_like(dg_ref)

        if tuple(dilations) == (1,):
            du = du_refs[0][...]
        else:
            for i, (d, du_ref) in enumerate(zip(dilations, du_refs)):
                for j in range(D_MODEL // LANES):
                    lanes = slice(j * LANES, (j + 1) * LANES)
                    if d == 1:
                        du_scr[j] = du_ref[:, lanes] if i == 0 else du_scr[j] + du_ref[:, lanes]
                        continue
                    blk = du_scr.at[j]
                    for r in range(d):
                        rows = _class_rows(r, d, tm)
                        blk[rows, :] = du_ref[r, :, lanes] if i == 0 else blk[rows, :] + du_ref[r, :, lanes]
            du = jnp.concatenate([du_scr[j] for j in range(D_MODEL // LANES)], axis=1)
        xv = x_ref[...]
        rstd = lax.rsqrt(jnp.mean(xv * xv, axis=-1, keepdims=True) + NORM_EPS)
        n = xv * rstd
        dg_ref[...] += jnp.sum(du * n, axis=0, keepdims=True)
        dn = du * g_ref[...]
        dx = dres_ref[...] + rstd * (dn - n * jnp.mean(dn * n, axis=-1, keepdims=True))
        dx_ref[...] = dx
        dxb_ref[...] = dx.astype(BF16)

    row = pl.BlockSpec((tm, D_MODEL), lambda i: (i, 0))
    vec = pl.BlockSpec((1, D_MODEL), lambda i: (0, 0))
    return pl.pallas_call(
        body,
        out_shape=(jax.ShapeDtypeStruct((T, D_MODEL), F32), jax.ShapeDtypeStruct((T, D_MODEL), BF16),
                   jax.ShapeDtypeStruct((1, D_MODEL), F32)),
        grid=(T // tm,), in_specs=[row, vec] + [_residue_spec(d, tm, D_MODEL) for d in dilations] + [row],
        out_specs=(row, row, vec), scratch_shapes=[pltpu.VMEM((D_MODEL // LANES, tm, LANES), F32)],
        compiler_params=_params("arbitrary"), name=name)(
            x, gain, *[_residue_view(du, d) for du, d in zip(dus, dilations)], dres)


def _loss_head(h, target, gain, name):
    T = h.shape[0]
    tm = _pick_tile(T, 512, 16)
    inv_f = 1.0 / D_MODEL

    def body(h_ref, t_ref, g_ref, dh_ref, dhb_ref, dg_ref, loss_ref):
        @pl.when(pl.program_id(0) == 0)
        def _():
            dg_ref[...] = jnp.zeros_like(dg_ref)
            loss_ref[...] = jnp.zeros_like(loss_ref)

        hv = h_ref[...]
        g = g_ref[...]
        rstd = lax.rsqrt(jnp.mean(hv * hv, axis=-1, keepdims=True) + NORM_EPS)
        n = hv * rstd
        err = n * g - t_ref[...]
        loss_ref[...] += (0.5 * inv_f) * jnp.sum(err * err, axis=0, keepdims=True)
        dy = err * inv_f
        dg_ref[...] += jnp.sum(dy * n, axis=0, keepdims=True)
        dn = dy * g
        dh = rstd * (dn - n * jnp.mean(dn * n, axis=-1, keepdims=True))
        dh_ref[...] = dh
        dhb_ref[...] = dh.astype(BF16)

    row = pl.BlockSpec((tm, D_MODEL), lambda i: (i, 0))
    vec = pl.BlockSpec((1, D_MODEL), lambda i: (0, 0))
    return pl.pallas_call(
        body,
        out_shape=(jax.ShapeDtypeStruct((T, D_MODEL), F32), jax.ShapeDtypeStruct((T, D_MODEL), BF16),
                   jax.ShapeDtypeStruct((1, D_MODEL), F32), jax.ShapeDtypeStruct((1, D_MODEL), F32)),
        grid=(T // tm,), in_specs=[row, row, vec], out_specs=(row, row, vec, vec),
        compiler_params=_params("arbitrary"), name=name)(h, target, gain)


FFN_TILE = 256


def _ffn_in(h, gain, w_in, name):
    T = h.shape[0]
    tm = _pick_tile(T, ROW_TILE, 16)

    def body(h_ref, g_ref, w_ref, n_ref, gate_ref, up_ref, a_ref):
        hv = h_ref[...]
        rstd = lax.rsqrt(jnp.mean(hv * hv, axis=-1, keepdims=True) + NORM_EPS)
        n = (hv * rstd * g_ref[...]).astype(BF16)
        n_ref[...] = n
        for c0 in range(0, D_FF, FFN_TILE):
            cols = slice(c0, c0 + FFN_TILE)
            gate = _dot(n, w_ref[c0:c0 + FFN_TILE, :], NT)
            up = _dot(n, w_ref[D_FF + c0:D_FF + c0 + FFN_TILE, :], NT)
            gate_ref[:, cols] = gate.astype(BF16)
            up_ref[:, cols] = up.astype(BF16)
            a_ref[:, cols] = (gate * _sigmoid(gate) * up).astype(BF16)

    row = pl.BlockSpec((tm, D_MODEL), lambda i: (i, 0))
    wide = pl.BlockSpec((tm, D_FF), lambda i: (i, 0))
    wide_shape = jax.ShapeDtypeStruct((T, D_FF), BF16)
    return pl.pallas_call(
        body, out_shape=(jax.ShapeDtypeStruct((T, D_MODEL), BF16), wide_shape, wide_shape, wide_shape),
        grid=(T // tm,),
        in_specs=[row, pl.BlockSpec((1, D_MODEL), lambda i: (0, 0)), _whole((2 * D_FF, D_MODEL), lambda i: (0, 0))],
        out_specs=(row, wide, wide, wide), compiler_params=_params("parallel"), name=name)(h, gain, w_in)


def _ffn_down_dx(dhb, w_down, gate, up, name):
    T = dhb.shape[0]
    tm = _pick_tile(T, ROW_TILE, 16)

    def body(dh_ref, w_ref, gate_ref, up_ref, dgate_ref, dup_ref):
        dh = dh_ref[...]
        for c0 in range(0, D_FF, FFN_TILE):
            cols = slice(c0, c0 + FFN_TILE)
            da = _dot(dh, w_ref[c0:c0 + FFN_TILE, :], NT)
            gate = gate_ref[:, cols].astype(F32)
            sg = _sigmoid(gate)
            dgate_ref[:, cols] = (da * up_ref[:, cols].astype(F32) * (sg * (1.0 + gate * (1.0 - sg)))).astype(BF16)
            dup_ref[:, cols] = (da * gate * sg).astype(BF16)

    wide = pl.BlockSpec((tm, D_FF), lambda i: (i, 0))
    wide_shape = jax.ShapeDtypeStruct((T, D_FF), BF16)
    return pl.pallas_call(
        body, out_shape=(wide_shape, wide_shape), grid=(T // tm,),
        in_specs=[pl.BlockSpec((tm, D_MODEL), lambda i: (i, 0)), _whole((D_FF, D_MODEL), lambda i: (0, 0)), wide, wide],
        out_specs=(wide, wide), compiler_params=_params("parallel"), name=name)(dhb, w_down, gate, up)


def _tri(n, lower):
    r = lax.broadcasted_iota(jnp.int32, (n, n), 0)
    c = lax.broadcasted_iota(jnp.int32, (n, n), 1)
    return (c <= r) if lower else (c >= r)


def _running_sum(x, lower):
    n = x.shape[0]
    tri = _tri(n, lower).astype(F32)
    return lax.dot_general(tri, x, NN, precision=lax.Precision.HIGHEST, preferred_element_type=F32)


def _hgrn_gates(q_raw, f_raw, lb):
    C = q_raw.shape[0]
    sig_f = _sigmoid(f_raw)
    forget = lb + (1.0 - lb) * sig_f
    key = 1.0 - forget
    log_f = jnp.log(forget)
    b = _running_sum(log_f, True)
    first_half = lax.broadcasted_iota(jnp.int32, log_f.shape, 0) < C // 2
    r = jnp.sum(jnp.where(first_half, log_f, 0.0), axis=0, keepdims=True)
    b_last = jnp.sum(log_f, axis=0, keepdims=True)
    e_a = jnp.exp(jnp.minimum(b - r, HGRN_EXP_CLAMP))
    e_b = jnp.exp(jnp.minimum(r - b, HGRN_EXP_CLAMP))
    e_q = jnp.exp(b)
    e_k = jnp.exp(b_last - b)
    sig_q = _sigmoid(q_raw)
    query = q_raw * sig_q
    return dict(sig_f=sig_f, forget=forget, sig_q=sig_q, e_a=e_a, e_b=e_b, e_q=e_q, e_k=e_k,
                e_last=jnp.exp(b_last), q_a=query * e_a, k_b=key * e_b, q_hat=query * e_q, k_til=key * e_k)


def _hgrn_fwd(proj, lb, gain, name):
    T = proj.shape[0]
    C = HGRN_CHUNK
    H, HD = HGRN_HEADS, HGRN_DIM

    def body(q_ref, f_ref, i_ref, g_ref, lb_ref, gain_ref, og_ref, o_ref, st_ref, s_scr):
        @pl.when(pl.program_id(0) == 0)
        def _():
            s_scr[...] = jnp.zeros_like(s_scr)

        st_ref[0] = s_scr[...]
        gt = _hgrn_gates(q_ref[...], f_ref[...], lb_ref[...])
        causal = _tri(C, True)
        gain_v = gain_ref[...]
        for h in range(H):
            sl = slice(h * HD, (h + 1) * HD)
            v = i_ref[:, sl].astype(BF16)
            p = jnp.where(causal, _dot(gt["q_a"][:, sl].astype(BF16), gt["k_b"][:, sl].astype(BF16), NT), 0.0)
            s_t = s_scr[h]
            o = _dot(p.astype(BF16), v, NN) + _dot(gt["q_hat"][:, sl].astype(BF16), s_t.astype(BF16), NT)
            s_scr[h] = gt["e_last"][:, sl] * s_t + _dot(v, gt["k_til"][:, sl].astype(BF16), TN)
            o_ref[:, sl] = o
            rstd = lax.rsqrt(jnp.mean(o * o, axis=-1, keepdims=True) + NORM_EPS)
            g_raw = g_ref[:, sl]
            og_ref[:, sl] = (o * rstd * gain_v * (g_raw * _sigmoid(g_raw))).astype(BF16)

    col = lambda j: pl.BlockSpec((C, D_MODEL), lambda c: (c, j))
    row = pl.BlockSpec((C, D_MODEL), lambda c: (c, 0))
    return pl.pallas_call(
        body,
        out_shape=(jax.ShapeDtypeStruct((T, D_MODEL), BF16), jax.ShapeDtypeStruct((T, D_MODEL), F32),
                   jax.ShapeDtypeStruct((T // C, H, HD, HD), F32)),
        grid=(T // C,),
        in_specs=[col(0), col(1), col(2), col(3), pl.BlockSpec((1, D_MODEL), lambda c: (0, 0)),
                  pl.BlockSpec((1, HD), lambda c: (0, 0))],
        out_specs=(row, row, pl.BlockSpec((1, H, HD, HD), lambda c: (c, 0, 0, 0))),
        scratch_shapes=[pltpu.VMEM((H, HD, HD), F32)],
        compiler_params=_params("arbitrary"), name=name)(proj, proj, proj, proj, lb, gain)


def _hgrn_bwd(proj, o_pre, d_og, states, lb, gain, name):
    T = proj.shape[0]
    C = HGRN_CHUNK
    H, HD = HGRN_HEADS, HGRN_DIM
    NC = T // C

    def body(q_ref, f_ref, i_ref, g_ref, o_ref, dog_ref, st_ref, lb_ref, gain_ref,
             dproj_ref, dlb_ref, dgain_ref, ds_scr, dq_scr, dk_scr, db_scr):
        @pl.when(pl.program_id(0) == 0)
        def _():
            ds_scr[...] = jnp.zeros_like(ds_scr)
            dlb_ref[...] = jnp.zeros_like(dlb_ref)
            dgain_ref[...] = jnp.zeros_like(dgain_ref)

        lbv = lb_ref[...]
        q_raw = q_ref[...]
        gt = _hgrn_gates(q_raw, f_ref[...], lbv)
        causal = _tri(C, True)
        last_row = lax.broadcasted_iota(jnp.int32, (C, HD), 0) == C - 1
        gain_v = gain_ref[...]
        dgain = jnp.zeros((1, HD), F32)
        for h in range(H):
            sl = slice(h * HD, (h + 1) * HD)
            o = o_ref[:, sl]
            rstd = lax.rsqrt(jnp.mean(o * o, axis=-1, keepdims=True) + NORM_EPS)
            n = o * rstd
            g_raw = g_ref[:, sl]
            sg = _sigmoid(g_raw)
            d_out = dog_ref[:, sl]
            dproj_ref[:, 3 * D_MODEL + h * HD:3 * D_MODEL + (h + 1) * HD] = (
                d_out * n * gain_v * (sg * (1.0 + g_raw * (1.0 - sg)))).astype(BF16)
            dy = d_out * (g_raw * sg)
            dgain = dgain + jnp.sum(dy * n, axis=0, keepdims=True)
            dn = dy * gain_v
            do = (rstd * (dn - n * jnp.mean(dn * n, axis=-1, keepdims=True))).astype(BF16)
            q_a, k_b = gt["q_a"][:, sl], gt["k_b"][:, sl]
            q_hat, k_til = gt["q_hat"][:, sl], gt["k_til"][:, sl]
            q_ab, k_bb = q_a.astype(BF16), k_b.astype(BF16)
            v = i_ref[:, sl].astype(BF16)
            s_t = st_ref[0, h]
            ds_t = ds_scr[h]
            ds_b = ds_t.astype(BF16)
            e_last = gt["e_last"][:, sl]
            p = jnp.where(causal, _dot(q_ab, k_bb, NT), 0.0).astype(BF16)
            dp = jnp.where(causal, _dot(do, v, NT), 0.0).astype(BF16)
            dv = _dot(p, do, TN) + _dot(k_til.astype(BF16), ds_b, NT)
            dq_a = _dot(dp, k_bb, NN)
            dk_b = _dot(dp, q_ab, TN)
            dq_hat = _dot(do, s_t.astype(BF16), NN)
            dk_til = _dot(v, ds_b, NN)
            ds_scr[h] = _dot(do, q_hat.astype(BF16), TN) + e_last * ds_t
            db_last = jnp.sum(ds_t * e_last * s_t, axis=0, keepdims=True) + jnp.sum(
                dk_til * k_til, axis=0, keepdims=True)
            dproj_ref[:, 2 * D_MODEL + h * HD:2 * D_MODEL + (h + 1) * HD] = dv.astype(BF16)
            dq_scr[:, sl] = dq_a * gt["e_a"][:, sl] + dq_hat * gt["e_q"][:, sl]
            dk_scr[:, sl] = dk_b * gt["e_b"][:, sl] + dk_til * gt["e_k"][:, sl]
            db = dq_a * q_ab.astype(F32) + dq_hat * q_hat - dk_b * k_bb.astype(F32) - dk_til * k_til
            db_scr[:, sl] = db + jnp.where(last_row, db_last, 0.0)
        dgain_ref[...] += dgain
        dlogf = _running_sum(db_scr[...], False)
        sig_f, forget, sig_q = gt["sig_f"], gt["forget"], gt["sig_q"]
        dforget = dlogf / forget - dk_scr[...]
        dproj_ref[:, D_MODEL:2 * D_MODEL] = (dforget * (1.0 - lbv) * sig_f * (1.0 - sig_f)).astype(BF16)
        dlb_ref[...] += jnp.sum(dforget * (1.0 - sig_f), axis=0, keepdims=True)
        dproj_ref[:, 0:D_MODEL] = (dq_scr[...] * (sig_q * (1.0 + q_raw * (1.0 - sig_q)))).astype(BF16)

    col = lambda j: pl.BlockSpec((C, D_MODEL), lambda c: (NC - 1 - c, j))
    row = pl.BlockSpec((C, D_MODEL), lambda c: (NC - 1 - c, 0))
    return pl.pallas_call(
        body,
        out_shape=(jax.ShapeDtypeStruct((T, 4 * D_MODEL), BF16), jax.ShapeDtypeStruct((1, D_MODEL), F32),
                   jax.ShapeDtypeStruct((1, HD), F32)),
        grid=(NC,),
        in_specs=[col(0), col(1), col(2), col(3), row, row,
                  pl.BlockSpec((1, H, HD, HD), lambda c: (NC - 1 - c, 0, 0, 0)),
                  pl.BlockSpec((1, D_MODEL), lambda c: (0, 0)), pl.BlockSpec((1, HD), lambda c: (0, 0))],
        out_specs=(pl.BlockSpec((C, 4 * D_MODEL), lambda c: (NC - 1 - c, 0)),
                   pl.BlockSpec((1, D_MODEL), lambda c: (0, 0)), pl.BlockSpec((1, HD), lambda c: (0, 0))),
        scratch_shapes=[pltpu.VMEM((H, HD, HD), F32), pltpu.VMEM((C, D_MODEL), F32),
                        pltpu.VMEM((C, D_MODEL), F32), pltpu.VMEM((C, D_MODEL), F32)],
        compiler_params=_params("arbitrary"), name=name)(proj, proj, proj, proj, o_pre, d_og, states, lb, gain)


def _attn_masks():
    r = lax.broadcasted_iota(jnp.int32, (ATTN_BLOCK, ATTN_BLOCK), 0)
    c = lax.broadcasted_iota(jnp.int32, (ATTN_BLOCK, ATTN_BLOCK), 1)
    return c >= r, c <= r


def _attn_fwd(qkv, dilation, name):
    T = qkv.shape[0]
    nb = T // dilation // ATTN_BLOCK
    W = ATTN_GROUP_WIDTH
    scale = ATTN_DIM ** -0.5

    def body(q_ref, kp_ref, kc_ref, vp_ref, vc_ref, o_ref, lse_ref):
        no_prev = jnp.where(pl.program_id(1) > 0, 0.0, NEG_BIG)
        m_prev, m_cur = _attn_masks()
        for h in range(ATTN_GROUP_HEADS):
            sl = slice(h * ATTN_DIM, (h + 1) * ATTN_DIM)
            q = q_ref[:, sl]
            s_p = jnp.where(m_prev, _dot(q, kp_ref[:, sl], NT) * scale + no_prev, NEG_BIG)
            s_c = jnp.where(m_cur, _dot(q, kc_ref[:, sl], NT) * scale, NEG_BIG)
            m = jnp.maximum(jnp.max(s_p, axis=-1, keepdims=True), jnp.max(s_c, axis=-1, keepdims=True))
            p_p = jnp.exp(s_p - m)
            p_c = jnp.exp(s_c - m)
            l = jnp.sum(p_p, axis=-1, keepdims=True) + jnp.sum(p_c, axis=-1, keepdims=True)
            acc = _dot(p_p.astype(BF16), vp_ref[:, sl], NN) + _dot(p_c.astype(BF16), vc_ref[:, sl], NN)
            o_ref[:, sl] = acc / l
            lse_ref[:, sl] = jnp.broadcast_to(m + jnp.log(l), (ATTN_BLOCK, ATTN_DIM))

    blk = lambda col, prev: pl.BlockSpec(
        (ATTN_BLOCK, W), lambda s, n: (s * nb + (jnp.maximum(n - 1, 0) if prev else n), col))
    out = pl.BlockSpec((ATTN_BLOCK, W), lambda s, n: (s * nb + n, 0))
    return pl.pallas_call(
        body, out_shape=(jax.ShapeDtypeStruct((T, W), F32),) * 2, grid=(dilation, nb),
        in_specs=[blk(0, False), blk(1, True), blk(1, False), blk(2, True), blk(2, False)],
        out_specs=(out, out), compiler_params=_params("parallel", "arbitrary"), name=name)(qkv, qkv, qkv, qkv, qkv)


def _attn_bwd(qkv, d_out, lse, delta, cos, sin, dilation, name):
    T = qkv.shape[0]
    nb = T // dilation // ATTN_BLOCK
    W = ATTN_GROUP_WIDTH
    scale = ATTN_DIM ** -0.5

    def unrope(x, cos_v, sin_v):
        return x * cos_v + pltpu.roll(x * sin_v, ATTN_DIM // 2, 1)

    def body(q_ref, kp_ref, kc_ref, vp_ref, vc_ref, do_ref, lse_ref, dl_ref, cos_ref, sin_ref,
             out_ref, dq_scr, dk_scr, dv_scr):
        n = pl.program_id(1)
        cos_v, sin_v = cos_ref[...], sin_ref[...]

        @pl.when(n > 0)
        def _():
            for h in range(ATTN_GROUP_HEADS):
                sl = slice(h * ATTN_DIM, (h + 1) * ATTN_DIM)
                out_ref[:, sl] = unrope(dq_scr[:, sl], cos_v, sin_v).astype(BF16)

        @pl.when(n == nb)
        def _():
            for h in range(ATTN_GROUP_HEADS):
                sl = slice(h * ATTN_DIM, (h + 1) * ATTN_DIM)
                out_ref[:, W + h * ATTN_DIM:W + (h + 1) * ATTN_DIM] = unrope(dk_scr[:, sl], cos_v, sin_v).astype(BF16)
                out_ref[:, 2 * W + h * ATTN_DIM:2 * W + (h + 1) * ATTN_DIM] = dv_scr[:, sl].astype(BF16)

        @pl.when(n == 0)
        def _():
            dk_scr[...] = jnp.zeros_like(dk_scr)
            dv_scr[...] = jnp.zeros_like(dv_scr)

        @pl.when(n < nb)
        def _():
            has_prev = n > 0
            no_prev = jnp.where(has_prev, 0.0, NEG_BIG)
            m_prev, m_cur = _attn_masks()
            for h in range(ATTN_GROUP_HEADS):
                sl = slice(h * ATTN_DIM, (h + 1) * ATTN_DIM)
                q, k_p, k_c, v_p, v_c = q_ref[:, sl], kp_ref[:, sl], kc_ref[:, sl], vp_ref[:, sl], vc_ref[:, sl]
                do = do_ref[:, sl]
                lse_v, dl_v = lse_ref[:, sl], dl_ref[:, sl]
                p_p = jnp.where(m_prev, jnp.exp(_dot(q, k_p, NT) * scale - lse_v + no_prev), 0.0)
                p_c = jnp.where(m_cur, jnp.exp(_dot(q, k_c, NT) * scale - lse_v), 0.0)
                ds_p = (p_p * (_dot(do, v_p, NT) - dl_v) * scale).astype(BF16)
                ds_c = (p_c * (_dot(do, v_c, NT) - dl_v) * scale).astype(BF16)
                dk_prev = dk_scr[:, sl] + _dot(ds_p, q, TN)
                dv_prev = dv_scr[:, sl] + _dot(p_p.astype(BF16), do, TN)
                out_ref[:, W + h * ATTN_DIM:W + (h + 1) * ATTN_DIM] = unrope(dk_prev, cos_v, sin_v).astype(BF16)
                out_ref[:, 2 * W + h * ATTN_DIM:2 * W + (h + 1) * ATTN_DIM] = dv_prev.astype(BF16)
                dq_scr[:, sl] = _dot(ds_p, k_p, NN) + _dot(ds_c, k_c, NN)
                dk_scr[:, sl] = _dot(ds_c, q, TN)
                dv_scr[:, sl] = _dot(p_c.astype(BF16), do, TN)

    def cur(n):
        return jnp.minimum(n, nb - 1)

    def late(n):
        return jnp.maximum(n - 1, 0)

    qkv_blk = lambda col, prev: pl.BlockSpec(
        (ATTN_BLOCK, W), lambda s, n: (s * nb + (jnp.maximum(cur(n) - 1, 0) if prev else cur(n)), col))
    row = pl.BlockSpec((ATTN_BLOCK, W), lambda s, n: (s * nb + cur(n), 0))
    tab = pl.BlockSpec((ATTN_BLOCK, ATTN_DIM), lambda s, n: (s * nb + late(n), 0))
    return pl.pallas_call(
        body, out_shape=jax.ShapeDtypeStruct((T, 3 * W), BF16), grid=(dilation, nb + 1),
        in_specs=[qkv_blk(0, False), qkv_blk(1, True), qkv_blk(1, False), qkv_blk(2, True), qkv_blk(2, False),
                  row, row, row, tab, tab],
        out_specs=pl.BlockSpec((ATTN_BLOCK, 3 * W), lambda s, n: (s * nb + late(n), 0)),
        scratch_shapes=[pltpu.VMEM((ATTN_BLOCK, W), F32)] * 3,
        compiler_params=_params("parallel", "arbitrary"), name=name)(
            qkv, qkv, qkv, qkv, qkv, d_out, lse, delta, cos, sin)


PERM_TILE = 512
LANES = 128


def _residue_view(x, d):
    return x if d == 1 else x.reshape(d, x.shape[0] // d, x.shape[1])


def _residue_spec(d, tm, cols):
    if d == 1:
        return pl.BlockSpec((tm, cols), lambda i: (i, 0))
    return pl.BlockSpec((d, tm // d, cols), lambda i: (0, i, 0))


def _residue_shape(T, d, cols, dtype):
    return jax.ShapeDtypeStruct((T, cols) if d == 1 else (d, T // d, cols), dtype)


def _class_rows(r, d, tm):
    return pl.ds(r, tm // d, stride=d)


def _attn_norm(h, gain, cos, sin, name):
    T = h.shape[0]
    tm = _pick_tile(T, PERM_TILE, 16 * max(ATTN_DILATIONS))
    dils = ATTN_DILATIONS

    def body(h_ref, g_ref, cos_ref, sin_ref, *refs):
        u_refs, c_refs, s_refs, u_scr = refs[0:3], refs[3:6], refs[6:9], refs[9]
        hv = h_ref[...]
        rstd = lax.rsqrt(jnp.mean(hv * hv, axis=-1, keepdims=True) + NORM_EPS)
        u = hv * rstd * g_ref[...]
        for j in range(D_MODEL // LANES):
            u_scr[j] = u[:, j * LANES:(j + 1) * LANES]
        for d, u_ref, c_ref, s_ref in zip(dils, u_refs, c_refs, s_refs):
            if d == 1:
                u_ref[...] = u.astype(BF16)
                c_ref[...] = cos_ref[...]
                s_ref[...] = sin_ref[...]
                continue
            for r in range(d):
                rows = _class_rows(r, d, tm)
                for j in range(D_MODEL // LANES):
                    u_ref[r, :, j * LANES:(j + 1) * LANES] = u_scr.at[j][rows, :].astype(BF16)
                c_ref[r] = cos_ref[rows, :]
                s_ref[r] = sin_ref[rows, :]

    row = pl.BlockSpec((tm, D_MODEL), lambda i: (i, 0))
    tab = pl.BlockSpec((tm, ATTN_DIM), lambda i: (i, 0))
    res = pl.pallas_call(
        body,
        out_shape=([_residue_shape(T, d, D_MODEL, BF16) for d in dils]
                   + [_residue_shape(T, d, ATTN_DIM, F32) for d in dils] * 2),
        grid=(T // tm,), in_specs=[row, pl.BlockSpec((1, D_MODEL), lambda i: (0, 0)), tab, tab],
        out_specs=([_residue_spec(d, tm, D_MODEL) for d in dils] + [_residue_spec(d, tm, ATTN_DIM) for d in dils] * 2),
        scratch_shapes=[pltpu.VMEM((D_MODEL // LANES, tm, LANES), F32)],
        compiler_params=_params("parallel"), name=name)(h, gain, cos, sin)
    flat = [r.reshape(T, r.shape[-1]) for r in res]
    return flat[0:3], flat[3:6], flat[6:9]


def _attn_merge_fwd(outs, lses, name):
    T = outs[0].shape[0]
    W = ATTN_GROUP_WIDTH
    tm = _pick_tile(T, PERM_TILE, 16 * max(ATTN_DILATIONS))
    dils = ATTN_DILATIONS

    def body(*refs):
        o_refs, l_refs, oc_ref, lse_refs = refs[0:3], refs[3:6], refs[6], refs[7:10]
        o_scr, l_scr, t_scr = refs[10:13]
        nh = ATTN_GROUP_HEADS
        for g, d in enumerate(dils):
            for j in range(nh):
                lanes = slice(j * LANES, (j + 1) * LANES)
                if d == 1:
                    o_scr[g * nh + j] = o_refs[g][:, lanes]
                    l_scr[g * nh + j] = l_refs[g][:, lanes]
                    continue
                for r in range(d):
                    rows = _class_rows(r, d, tm)
                    o_scr.at[g * nh + j][rows, :] = o_refs[g][r, :, lanes]
                    l_scr.at[g * nh + j][rows, :] = l_refs[g][r, :, lanes]
        for j in range(nh):
            lanes = slice(j * LANES, (j + 1) * LANES)
            ls = [l_scr[g * nh + j] for g in range(3)]
            m = jnp.maximum(jnp.maximum(ls[0], ls[1]), ls[2])
            tot = m + jnp.log(jnp.exp(ls[0] - m) + jnp.exp(ls[1] - m) + jnp.exp(ls[2] - m))
            t_scr[j] = tot
            for g, d in enumerate(dils):
                oc_ref[:, g * W + j * LANES:g * W + (j + 1) * LANES] = (
                    o_scr[g * nh + j] * jnp.exp(ls[g] - tot)).astype(BF16)
                if d == 1:
                    lse_refs[g][:, lanes] = tot
                    continue
                for r in range(d):
                    lse_refs[g][r, :, lanes] = t_scr.at[j][_class_rows(r, d, tm), :]

    in_blk = [_residue_spec(d, tm, W) for d in dils]
    n_blk = 3 * ATTN_GROUP_HEADS
    res = pl.pallas_call(
        body, out_shape=[jax.ShapeDtypeStruct((T, 3 * W), BF16)] + [_residue_shape(T, d, W, F32) for d in dils],
        grid=(T // tm,), in_specs=in_blk * 2,
        out_specs=[pl.BlockSpec((tm, 3 * W), lambda i: (i, 0))] + in_blk,
        scratch_shapes=[pltpu.VMEM((n_blk, tm, LANES), F32), pltpu.VMEM((n_blk, tm, LANES), F32),
                        pltpu.VMEM((ATTN_GROUP_HEADS, tm, LANES), F32)],
        compiler_params=_params("parallel"), name=name)(
            *[_residue_view(o, d) for o, d in zip(outs, dils)], *[_residue_view(l, d) for l, d in zip(lses, dils)])
    return res[0], [r.reshape(T, W) for r in res[1:]]


def _attn_merge_bwd(d_oc, oc, name):
    T = d_oc.shape[0]
    W = ATTN_GROUP_WIDTH
    tm = _pick_tile(T, PERM_TILE, 16 * max(ATTN_DILATIONS))
    dils = ATTN_DILATIONS

    def body(d_ref, o_ref, *refs):
        delta_refs, db_refs, dl_scr, d_scr = refs[0:3], refs[3:6], refs[6], refs[7]
        nh = ATTN_GROUP_HEADS
        for j in range(nh):
            tot = jnp.zeros((tm, 1), F32)
            for g in range(3):
                cols = slice(g * W + j * LANES, g * W + (j + 1) * LANES)
                d_blk = d_ref[:, cols]
                d_scr[g * nh + j] = d_blk
                tot = tot + jnp.sum(d_blk * o_ref[:, cols].astype(F32), axis=-1, keepdims=True)
            dl_scr[j] = jnp.broadcast_to(tot, (tm, LANES))
        for g, d in enumerate(dils):
            for j in range(nh):
                lanes = slice(j * LANES, (j + 1) * LANES)
                if d == 1:
                    delta_refs[g][:, lanes] = dl_scr[j]
                    db_refs[g][:, lanes] = d_scr[g * nh + j].astype(BF16)
                    continue
                for r in range(d):
                    rows = _class_rows(r, d, tm)
                    delta_refs[g][r, :, lanes] = dl_scr.at[j][rows, :]
                    db_refs[g][r, :, lanes] = d_scr.at[g * nh + j][rows, :].astype(BF16)

    wide = pl.BlockSpec((tm, 3 * W), lambda i: (i, 0))
    out_blk = [_residue_spec(d, tm, W) for d in dils]
    res = pl.pallas_call(
        body, out_shape=[_residue_shape(T, d, W, F32) for d in dils] + [_residue_shape(T, d, W, BF16) for d in dils],
        grid=(T // tm,), in_specs=[wide, wide], out_specs=out_blk * 2,
        scratch_shapes=[pltpu.VMEM((ATTN_GROUP_HEADS, tm, LANES), F32),
                        pltpu.VMEM((3 * ATTN_GROUP_HEADS, tm, LANES), F32)],
        compiler_params=_params("parallel"), name=name)(d_oc, oc)
    flat = [r.reshape(T, W) for r in res]
    return flat[0:3], flat[3:6]


def _rope_tables(T):
    inv_freq = 1.0 / (ROPE_THETA ** (jnp.arange(0, ATTN_DIM, 2, dtype=F32) / ATTN_DIM))
    ang = jnp.arange(T, dtype=F32)[:, None] * inv_freq[None, :]
    cos, sin = jnp.cos(ang), jnp.sin(ang)
    return jnp.concatenate([cos, cos], axis=1), jnp.concatenate([-sin, sin], axis=1)


WEIGHT_GROUPS = {"hgrn": ("hgrn_in", "hgrn_out"), "ffn0": ("ffn_in0", "ffn_down0"),
                 "attn": ("qkv", "attn_out"), "ffn1": ("ffn_in1", "ffn_down1")}


def _local_step(x, target, norm_mix, norm_ffn, lb, out_gain, final_gain, fetch, publish):
    T = x.shape[0]
    g_mix = [norm_mix[0:1], norm_mix[1:2]]
    g_ffn = [norm_ffn[0:1], norm_ffn[1:2]]
    w = {}

    def whole(name):
        return [(w[name], w[name].shape[0], 0)]

    def qkv_parts(g):
        return [(w["qkv"], ATTN_GROUP_WIDTH, 3 * j + g) for j in range(3)]

    def ffn_fwd(h, layer):
        w.update(fetch(f"ffn{layer}"))
        n, gate, up, a = _ffn_in(h, g_ffn[layer], w[f"ffn_in{layer}"], f"ffn{layer}_in")
        out = _mm_nn([a], [whole(f"ffn_down{layer}")], h, name=f"ffn{layer}_down")
        return out, (n, gate, up, a)

    def ffn_bwd(h, saved, dh, dhb, layer):
        n, gate, up, a = saved
        w_in = w[f"ffn_in{layer}"]
        dgate, dup = _ffn_down_dx(dhb, w[f"ffn_down{layer}"], gate, up, f"ffn{layer}_down_dx")
        grad_in = _mm_tn(dgate, n, name=f"ffn{layer}_in_dw_gate", rows=2 * D_FF)
        grad_in = _mm_tn(dup, n, name=f"ffn{layer}_in_dw_up", into=grad_in, row_tile=D_FF // GRAD_TILE, rows=2 * D_FF)
        grads = {f"ffn_down{layer}": _mm_tn(a, dhb, name=f"ffn{layer}_down_dw"), f"ffn_in{layer}": grad_in}
        publish(f"ffn{layer}", grads)
        dn = _mm_nn([dgate, dup], [[(w_in, D_FF, 0)], [(w_in, D_FF, 1)]], None, name=f"ffn{layer}_in_dx")
        return _rms_bwd(h, g_ffn[layer], [dn], dh, f"ffn{layer}_norm_bwd")

    u0 = _rms_fwd(x, g_mix[0], "hgrn_norm")
    w.update(fetch("hgrn"))
    proj = _mm_nt(u0, whole("hgrn_in"), out_dtype=F32, name="hgrn_in")
    og, o_pre, states = _hgrn_fwd(proj, lb, out_gain, "hgrn_fwd")
    h1 = _mm_nn([og], [whole("hgrn_out")], x, name="hgrn_out")
    h2, ffn0 = ffn_fwd(h1, 0)

    cos, sin = _rope_tables(T)
    u1_g, cos_g, sin_g = _attn_norm(h2, g_mix[1], cos, sin, "attn_norm")
    w.update(fetch("attn"))
    qkv_g, outs, lses = [], [], []
    for g, d in enumerate(ATTN_DILATIONS):
        qkv_g.append(_mm_nt(u1_g[g], qkv_parts(g), out_dtype=BF16, name=f"attn_qkv{g}",
                            rope=(cos_g[g], sin_g[g], 2)))
        o_g, lse_g = _attn_fwd(qkv_g[g], d, f"attn_fwd{g}")
        outs.append(o_g)
        lses.append(lse_g)
    oc, lse_all = _attn_merge_fwd(outs, lses, "attn_merge")
    h3 = _mm_nn([oc], [whole("attn_out")], h2, name="attn_out")
    h4, ffn1 = ffn_fwd(h3, 1)

    dh4, dh4b, d_final, loss_part = _loss_head(h4, target, final_gain, "loss_head")
    dh3, dh3b, d_ffn1 = ffn_bwd(h3, ffn1, dh4, dh4b, 1)

    d_oc = _mm_nt(dh3b, whole("attn_out"), out_dtype=F32, name="attn_out_dx")
    grad_attn_out = _mm_tn(oc, dh3b, name="attn_out_dw")
    delta, d_ocb = _attn_merge_bwd(d_oc, oc, "attn_merge_bwd")
    du1, qkv_pieces = [], []
    for g, d in enumerate(ATTN_DILATIONS):
        dqkv = _attn_bwd(qkv_g[g], d_ocb[g], lse_all[g], delta[g], cos_g[g], sin_g[g], d, f"attn_bwd{g}")
        qkv_pieces.append(_mm_tn(dqkv, u1_g[g], name=f"attn_qkv_dw{g}"))
        du1.append(_mm_nn([dqkv], [qkv_parts(g)], None, name=f"attn_qkv_dx{g}"))
    grad_qkv = jnp.stack([p.reshape(3, ATTN_GROUP_WIDTH, D_MODEL) for p in qkv_pieces], axis=1).reshape(
        3 * ATTN_WIDTH, D_MODEL)
    publish("attn", {"qkv": grad_qkv, "attn_out": grad_attn_out})
    dh2, dh2b, d_mix1 = _rms_bwd(h2, g_mix[1], du1, dh3, "attn_norm_bwd", ATTN_DILATIONS)

    dh1, dh1b, d_ffn0 = ffn_bwd(h1, ffn0, dh2, dh2b, 0)

    d_og = _mm_nt(dh1b, whole("hgrn_out"), out_dtype=F32, name="hgrn_out_dx")
    grad_hgrn_out = _mm_tn(og, dh1b, name="hgrn_out_dw")
    dproj, d_lb, d_out_gain = _hgrn_bwd(proj, o_pre, d_og, states, lb, out_gain, "hgrn_bwd")
    publish("hgrn", {"hgrn_in": _mm_tn(dproj, u0, name="hgrn_in_dw"), "hgrn_out": grad_hgrn_out})
    du0 = _mm_nn([dproj], [whole("hgrn_in")], None, name="hgrn_in_dx")
    dx, _, d_mix0 = _rms_bwd(x, g_mix[0], [du0], dh1, "hgrn_norm_bwd")

    small = dict(norm_mix0=d_mix0, norm_mix1=d_mix1, norm_ffn0=d_ffn0, norm_ffn1=d_ffn1, lb=d_lb,
                 out_gain=d_out_gain, final=d_final, loss=loss_part)
    return dx, small


WEIGHT_NAMES = ("hgrn_in", "hgrn_out", "qkv", "attn_out", "ffn_in0", "ffn_in1", "ffn_down0", "ffn_down1")
MESH_IDS = pl.DeviceIdType.MESH
HBM_SPEC = pl.BlockSpec(memory_space=pl.ANY)


N_PEERS = N_DEV - 1
PEER_OFFSETS = [(dx, dy, dc) for dx in (0, 1) for dy in (0, 1) for dc in (0, 1)][1:]


def _mesh_place():
    x, y, c = lax.axis_index("x"), lax.axis_index("y"), lax.axis_index("c")
    peers = []
    for dx, dy, dc in PEER_OFFSETS:
        px, py, pc = (1 - x if dx else x), (1 - y if dy else y), (1 - c if dc else c)
        peers.append(((px, py, pc), 4 * px + 2 * py + pc))
    return 4 * x + 2 * y + c, peers


def _exchange_launch(srcs, scatter, collective_id, name):
    n = len(srcs)
    src_refs = [jax.new_ref(s, memory_space=pltpu.MemorySpace.HBM) for s in srcs]
    land_refs = [jax.empty_ref(jax.ShapeDtypeStruct(s.shape if scatter else (N_DEV,) + s.shape, s.dtype),
                               memory_space=pltpu.MemorySpace.HBM) for s in srcs]

    @pl.kernel(mesh=plsc.ScalarSubcoreMesh(axis_name="sequencer", num_cores=1), name=name,
               scratch_types=(pltpu.SemaphoreType.DMA((n * N_PEERS,)), pltpu.SemaphoreType.DMA((n * N_PEERS,)),
                              pltpu.SemaphoreType.DMA((n,))),
               compiler_params=pltpu.CompilerParams(collective_id=collective_id))
    def launch(send_sems, recv_sems, local_sems):
        me, peers = _mesh_place()
        barrier = pltpu.get_barrier_semaphore()
        for peer, _ in peers:
            pl.semaphore_signal(barrier, inc=1, device_id=peer, device_id_type=MESH_IDS)
        pl.semaphore_wait(barrier, N_PEERS)
        own = [pltpu.make_async_copy(src_refs[w].at[me] if scatter else src_refs[w], land_refs[w].at[me],
                                     local_sems.at[w]) for w in range(n)]
        for cp in own:
            cp.start()
        copies = [pltpu.make_async_remote_copy(
            src_ref=src_refs[w].at[pid] if scatter else src_refs[w], dst_ref=land_refs[w].at[me],
            send_sem=send_sems.at[w * N_PEERS + k], recv_sem=recv_sems.at[w * N_PEERS + k],
            device_id=peer, device_id_type=MESH_IDS) for w in range(n) for k, (peer, pid) in enumerate(peers)]
        for cp in copies:
            cp.start()
        for cp in copies:
            cp.wait()
        for cp in own:
            cp.wait()

    launch()
    return land_refs


def _gather_small(block, name):
    def body(in_ref, out_ref, send_sems, recv_sems, local_sem):
        me, peers = _mesh_place()
        own = pltpu.make_async_copy(in_ref, out_ref.at[me], local_sem)
        own.start()
        sends = [pltpu.make_async_remote_copy(
            src_ref=in_ref, dst_ref=out_ref.at[me], send_sem=send_sems.at[k], recv_sem=recv_sems.at[k],
            device_id=peer, device_id_type=MESH_IDS) for k, (peer, _) in enumerate(peers)]
        for cp in sends:
            cp.start()
        for cp in sends:
            cp.wait_recv()
        for cp in sends:
            cp.wait_send()
        own.wait()

    return pl.pallas_call(
        body, out_shape=jax.ShapeDtypeStruct((N_DEV,) + block.shape, block.dtype),
        in_specs=[HBM_SPEC], out_specs=HBM_SPEC,
        scratch_shapes=[pltpu.SemaphoreType.DMA((N_PEERS,)), pltpu.SemaphoreType.DMA((N_PEERS,)),
                        pltpu.SemaphoreType.DMA],
        name=name)(block)


def _sum_blocks(recv, name):
    rows = recv.shape[1]
    tr = _pick_tile(rows, 256, 16)

    def body(r_ref, g_ref):
        acc = r_ref[0].astype(F32)
        for j in range(1, N_DEV):
            acc = acc + r_ref[j].astype(F32)
        g_ref[...] = acc

    return pl.pallas_call(
        body, out_shape=jax.ShapeDtypeStruct((rows, D_MODEL), F32), grid=(rows // tr,),
        in_specs=[pl.BlockSpec((N_DEV, tr, D_MODEL), lambda i: (0, i, 0))],
        out_specs=pl.BlockSpec((tr, D_MODEL), lambda i: (i, 0)),
        compiler_params=_params("parallel"), name=name)(recv)


def _adamw_math(w, g, m, v):
    m_new = ADAM_B1 * m + (1.0 - ADAM_B1) * g
    v_new = ADAM_B2 * v + (1.0 - ADAM_B2) * (g * g)
    m_hat = m_new / (1.0 - ADAM_B1 ** ADAM_STEP)
    v_hat = v_new / (1.0 - ADAM_B2 ** ADAM_STEP)
    delta = -ADAM_LR * (m_hat / (jnp.sqrt(v_hat) + ADAM_EPS) + ADAM_WD * w)
    return delta, m_new, v_new


def _adamw(w, g, m, v, name):
    rows, cols = w.shape
    tr = _pick_tile(rows, 256, 8)

    def body(w_ref, g_ref, m_ref, v_ref, d_ref, mo_ref, vo_ref):
        d_ref[...], mo_ref[...], vo_ref[...] = _adamw_math(w_ref[...], g_ref[...], m_ref[...], v_ref[...])

    blk = pl.BlockSpec((tr, cols), lambda i: (i, 0))
    return pl.pallas_call(
        body, out_shape=(jax.ShapeDtypeStruct((rows, cols), F32),) * 3, grid=(rows // tr,),
        in_specs=[blk] * 4, out_specs=(blk,) * 3, compiler_params=_params("parallel"), name=name)(w, g, m, v)


ROW_MIX, ROW_FFN, ROW_LB, ROW_OUT_GAIN, ROW_FINAL = 0, 2, 4, 7, 8
PART_MIX, PART_FFN, PART_LB, PART_OUT_GAIN, PART_FINAL, PART_LOSS = 0, 2, 4, 5, 6, 7


def _small_update(parts_all, w, m, v, name):
    def body(p_ref, w_ref, m_ref, v_ref, g_ref, d_ref, mo_ref, vo_ref, loss_ref):
        def total(row, n=1):
            tot = p_ref[0, row:row + n, :]
            for j in range(1, N_DEV):
                tot = tot + p_ref[j, row:row + n, :]
            return tot

        logits = [w_ref[ROW_LB + i:ROW_LB + i + 1, :] for i in range(3)]
        mx = jnp.maximum(jnp.maximum(logits[0], logits[1]), logits[2])
        ex = [jnp.exp(l - mx) for l in logits]
        den = ex[0] + ex[1] + ex[2]
        prob = [e / den for e in ex]
        d_lb = total(PART_LB)
        g_ref[...] = jnp.zeros_like(g_ref)
        g_ref[ROW_MIX:ROW_MIX + 2, :] = total(PART_MIX, 2)
        g_ref[ROW_FFN:ROW_FFN + 2, :] = total(PART_FFN, 2)
        for i in range(3):
            g_ref[ROW_LB + i:ROW_LB + i + 1, :] = prob[i] * ((d_lb if i == 0 else 0.0) - prob[0] * d_lb)
        g_ref[ROW_OUT_GAIN:ROW_OUT_GAIN + 1, :] = total(PART_OUT_GAIN)
        g_ref[ROW_FINAL:ROW_FINAL + 1, :] = total(PART_FINAL)
        d_ref[...], mo_ref[...], vo_ref[...] = _adamw_math(w_ref[...], g_ref[...], m_ref[...], v_ref[...])
        loss_ref[...] = jnp.sum(total(PART_LOSS), axis=-1, keepdims=True)

    packed = jax.ShapeDtypeStruct((16, D_MODEL), F32)
    return pl.pallas_call(
        body, out_shape=(packed, packed, packed, packed, jax.ShapeDtypeStruct((1, 1), F32)),
        compiler_params=pltpu.CompilerParams(vmem_limit_bytes=VMEM_LIMIT), name=name)(parts_all, w, m, v)


def _pack_small(norm_mix, norm_ffn, lb_logits, out_gain, final):
    pad = jnp.zeros((1, D_MODEL - HGRN_DIM), F32)
    return jnp.concatenate([norm_mix, norm_ffn, lb_logits, jnp.concatenate([out_gain, pad], axis=1),
                            final.reshape(1, D_MODEL), jnp.zeros((16 - ROW_FINAL - 1, D_MODEL), F32)], axis=0)


def _unpack_small(p):
    return (p[ROW_MIX:ROW_MIX + 2], p[ROW_FFN:ROW_FFN + 2], p[ROW_LB:ROW_LB + 3],
            p[ROW_OUT_GAIN:ROW_OUT_GAIN + 1, :HGRN_DIM], p[ROW_FINAL])


def _lower_bound(lb_logits, name):
    def body(l_ref, o_ref):
        logits = [l_ref[i:i + 1, :] for i in range(3)]
        mx = jnp.maximum(jnp.maximum(logits[0], logits[1]), logits[2])
        ex = [jnp.exp(l - mx) for l in logits]
        o_ref[...] = ex[0] / (ex[0] + ex[1] + ex[2])

    return pl.pallas_call(body, out_shape=jax.ShapeDtypeStruct((1, D_MODEL), F32), name=name)(lb_logits)


def kernel(x, norm_mix, norm_ffn, hgrn_w_in, hgrn_lb_logits, hgrn_out_norm, hgrn_w_out, attn_w_qkv, attn_w_out, ffn_w_in, ffn_w_down, final_norm, loss_target, m_norm_mix, m_norm_ffn, m_hgrn_w_in, m_hgrn_lb_logits, m_hgrn_out_norm, m_hgrn_w_out, m_attn_w_qkv, m_attn_w_out, m_ffn_w_in, m_ffn_w_down, m_final_norm, v_norm_mix, v_norm_ffn, v_hgrn_w_in, v_hgrn_lb_logits, v_hgrn_out_norm, v_hgrn_w_out, v_attn_w_qkv, v_attn_w_out, v_ffn_w_in, v_ffn_w_down, v_final_norm):
    col_sharded = {"hgrn_in": hgrn_w_in[0], "qkv": attn_w_qkv[0], "ffn_in0": ffn_w_in[0], "ffn_in1": ffn_w_in[1]}
    row_sharded = {"hgrn_out": hgrn_w_out[0], "attn_out": attn_w_out[0], "ffn_down0": ffn_w_down[0],
                   "ffn_down1": ffn_w_down[1]}
    gathering = {}
    for gi, (group, names) in enumerate(WEIGHT_GROUPS.items()):
        shards = [(col_sharded[n].T if n in col_sharded else row_sharded[n]).astype(BF16) for n in names]
        gathering[group] = _exchange_launch(shards, False, 1 + gi, f"weights_gather_{group}")

    def fetch(group):
        return {n: land[...].reshape(-1, D_MODEL) for n, land in zip(WEIGHT_GROUPS[group], gathering[group])}

    in_flight = {}

    def publish(group, grads):
        names = WEIGHT_GROUPS[group]
        parts = [grads[n].reshape(N_DEV, -1, D_MODEL) for n in names]
        in_flight[group] = _exchange_launch(parts, True, 1 + len(WEIGHT_GROUPS) + list(WEIGHT_GROUPS).index(group),
                                            f"grads_send_{group}")

    lb = _lower_bound(hgrn_lb_logits, "hgrn_lower_bound")
    grad_x, small = _local_step(x[0], loss_target[0], norm_mix, norm_ffn, lb, hgrn_out_norm,
                                final_norm.reshape(1, D_MODEL), fetch, publish)

    pad = jnp.zeros((1, D_MODEL - HGRN_DIM), F32)
    small_part = jnp.concatenate(
        [small["norm_mix0"], small["norm_mix1"], small["norm_ffn0"], small["norm_ffn1"], small["lb"],
         jnp.concatenate([small["out_gain"], pad], axis=1), small["final"], small["loss"]], axis=0)
    small_all = _gather_small(small_part, "small_grads_gather")
    received = {}
    for group in ("ffn1", "attn", "ffn0", "hgrn"):
        received.update(zip(WEIGHT_GROUPS[group], [land[...] for land in in_flight[group]]))

    masters = {"hgrn_in": (hgrn_w_in[0], m_hgrn_w_in[0], v_hgrn_w_in[0]),
               "hgrn_out": (hgrn_w_out[0], m_hgrn_w_out[0], v_hgrn_w_out[0]),
               "qkv": (attn_w_qkv[0], m_attn_w_qkv[0], v_attn_w_qkv[0]),
               "attn_out": (attn_w_out[0], m_attn_w_out[0], v_attn_w_out[0]),
               "ffn_in0": (ffn_w_in[0], m_ffn_w_in[0], v_ffn_w_in[0]),
               "ffn_in1": (ffn_w_in[1], m_ffn_w_in[1], v_ffn_w_in[1]),
               "ffn_down0": (ffn_w_down[0], m_ffn_w_down[0], v_ffn_w_down[0]),
               "ffn_down1": (ffn_w_down[1], m_ffn_w_down[1], v_ffn_w_down[1])}
    res = {}
    for n in WEIGHT_NAMES:
        g = _sum_blocks(received[n], f"{n}_grad_sum")
        if n in col_sharded:
            g = g.T
        wv, mv, vv = masters[n]
        res[n] = (g,) + tuple(_adamw(wv, g, mv, vv, f"{n}_adamw"))

    def single(n):
        return [t[None] for t in res[n]]

    def pair(n):
        return [jnp.stack([a, b]) for a, b in zip(res[n + "0"], res[n + "1"])]

    big = dict(hgrn_w_in=single("hgrn_in"), hgrn_w_out=single("hgrn_out"), attn_w_qkv=single("qkv"),
               attn_w_out=single("attn_out"), ffn_w_in=pair("ffn_in"), ffn_w_down=pair("ffn_down"))

    w_small = _pack_small(norm_mix, norm_ffn, hgrn_lb_logits, hgrn_out_norm, final_norm)
    m_small = _pack_small(m_norm_mix, m_norm_ffn, m_hgrn_lb_logits, m_hgrn_out_norm, m_final_norm)
    v_small = _pack_small(v_norm_mix, v_norm_ffn, v_hgrn_lb_logits, v_hgrn_out_norm, v_final_norm)
    g_s, d_s, m_s, v_s, loss = _small_update(small_all, w_small, m_small, v_small, "small_update")
    small_out = [_unpack_small(t) for t in (g_s, d_s, m_s, v_s)]

    def group(i):
        s = small_out[i]
        return (s[0], s[1], big["hgrn_w_in"][i], s[2], s[3], big["hgrn_w_out"][i], big["attn_w_qkv"][i],
                big["attn_w_out"][i], big["ffn_w_in"][i], big["ffn_w_down"][i], s[4])

    return (loss.reshape(()), grad_x[None], *group(0), *group(1), *group(2), *group(3))
```

```python
import functools

import jax
import jax.numpy as jnp
from jax import lax
from jax.experimental import pallas as pl
from jax.experimental.pallas import tpu as pltpu
from jax.experimental.pallas import tpu_sc as plsc

F32 = jnp.float32
BF16 = jnp.bfloat16

D_MODEL = 1024
N_DEV = 8
NORM_EPS = 1e-6

HGRN_HEADS = 8
HGRN_DIM = 128
HGRN_CHUNK = 64
HGRN_EXP_CLAMP = 60.0

ATTN_DIM = 128
ATTN_BLOCK = 128
ATTN_GROUP_HEADS = 4
ATTN_GROUP_WIDTH = ATTN_GROUP_HEADS * ATTN_DIM
ATTN_DILATIONS = (1, 4, 16)
ATTN_WIDTH = 3 * ATTN_GROUP_WIDTH
ROPE_THETA = 10000.0
NEG_BIG = -1e30

D_FF = 2816

ADAM_LR = 0.001
ADAM_B1 = 0.9
ADAM_B2 = 0.999
ADAM_EPS = 1e-08
ADAM_WD = 0.01
ADAM_STEP = 10

VMEM_LIMIT = 48 * 1024 * 1024

NT = (((1,), (1,)), ((), ()))
NN = (((1,), (0,)), ((), ()))
TN = (((0,), (0,)), ((), ()))


def _dot(a, b, dims):
    return lax.dot_general(a, b, dims, preferred_element_type=F32)


def _params(*sem):
    return pltpu.CompilerParams(dimension_semantics=sem, vmem_limit_bytes=VMEM_LIMIT)


def _pick_tile(n, cap, mult):
    best = None
    for t in range(mult, min(n, cap) + 1, mult):
        if n % t == 0:
            best = t
    assert best is not None, (n, cap, mult)
    return best


def _sigmoid(x):
    return 1.0 / (1.0 + jnp.exp(-x))


ROW_TILE = 512
COL_CHUNK = 512
GRAD_TILE = 256


def _whole(shape, index_map):
    return pl.BlockSpec(shape, index_map, pipeline_mode=pl.Buffered(1))


def _part_specs(parts, n_cols):
    return [_whole((rows, n_cols), functools.partial(lambda i, b: (b, 0), b=blk)) for _, rows, blk in parts]


def _mm_nt(a, w_parts, *, out_dtype, name, rope=None):
    M, K = a.shape
    tm = _pick_tile(M, ROW_TILE, 16)
    widths = [rows for _, rows, _ in w_parts]
    n_parts = len(w_parts)

    def body(*refs):
        a_ref, w_refs, o_ref = refs[0], refs[1:1 + n_parts], refs[-1]
        av = a_ref[...]
        off = 0
        for p, w_ref in enumerate(w_refs):
            for c0 in range(0, widths[p], COL_CHUNK):
                cw = min(COL_CHUNK, widths[p] - c0)
                acc = _dot(av, w_ref[c0:c0 + cw, :], NT)
                if rope is not None and p < rope[2]:
                    cos, sin = refs[1 + n_parts][...], refs[2 + n_parts][...]
                    for h0 in range(0, cw, ATTN_DIM):
                        xh = acc[:, h0:h0 + ATTN_DIM]
                        rot = pltpu.roll(xh, ATTN_DIM // 2, 1)
                        o_ref[:, off + c0 + h0:off + c0 + h0 + ATTN_DIM] = (xh * cos + rot * sin).astype(out_dtype)
                else:
                    o_ref[:, off + c0:off + c0 + cw] = acc.astype(out_dtype)
            off += widths[p]

    in_specs = [pl.BlockSpec((tm, K), lambda i: (i, 0))] + _part_specs(w_parts, K)
    args = [a] + [w for w, _, _ in w_parts]
    if rope is not None:
        in_specs += [pl.BlockSpec((tm, ATTN_DIM), lambda i: (i, 0))] * 2
        args += [rope[0], rope[1]]
    return pl.pallas_call(
        body, out_shape=jax.ShapeDtypeStruct((M, sum(widths)), out_dtype), grid=(M // tm,),
        in_specs=in_specs, out_specs=pl.BlockSpec((tm, sum(widths)), lambda i: (i, 0)),
        compiler_params=_params("parallel"), name=name)(*args)


def _mm_nn(a_list, w_parts_list, resid, *, name):
    M = a_list[0].shape[0]
    tm = _pick_tile(M, ROW_TILE, 16)
    n_a = len(a_list)
    flat_parts = [p for parts in w_parts_list for p in parts]

    def body(*refs):
        a_refs, w_refs, o_ref = refs[:n_a], refs[n_a:n_a + len(flat_parts)], refs[-1]
        acc = None
        wi = 0
        for a_ref, parts in zip(a_refs, w_parts_list):
            off = 0
            for _, rows, _ in parts:
                term = _dot(a_ref[:, off:off + rows], w_refs[wi][...], NN)
                acc = term if acc is None else acc + term
                off += rows
                wi += 1
        if resid is not None:
            acc = acc + refs[-2][...]
        o_ref[...] = acc

    row = pl.BlockSpec((tm, D_MODEL), lambda i: (i, 0))
    in_specs = [pl.BlockSpec((tm, a.shape[1]), lambda i: (i, 0)) for a in a_list] + _part_specs(flat_parts, D_MODEL)
    args = list(a_list) + [w for w, _, _ in flat_parts]
    if resid is not None:
        in_specs.append(row)
        args.append(resid)
    return pl.pallas_call(
        body, out_shape=jax.ShapeDtypeStruct((M, D_MODEL), F32), grid=(M // tm,),
        in_specs=in_specs, out_specs=row, compiler_params=_params("parallel"), name=name)(*args)


def _mm_tn(a, b, *, name, into=None, row_tile=0, rows=None):
    T, R = a.shape
    N = b.shape[1]
    tr = GRAD_TILE
    rows = R if rows is None else rows

    def body(a_ref, b_ref, *refs):
        refs[-1][...] = _dot(a_ref[...], b_ref[...], TN).astype(BF16)

    in_specs = [pl.BlockSpec((T, tr), lambda r: (0, r)), _whole((T, N), lambda r: (0, 0))]
    args = [a, b]
    if into is not None:
        in_specs.append(HBM_SPEC)
        args.append(into)
    return pl.pallas_call(
        body, out_shape=jax.ShapeDtypeStruct((rows, N), BF16), grid=(R // tr,),
        in_specs=in_specs, out_specs=pl.BlockSpec((tr, N), lambda r: (row_tile + r, 0)),
        input_output_aliases={} if into is None else {2: 0},
        compiler_params=_params("parallel"), name=name)(*args)


def _rms_fwd(x, gain, name):
    T = x.shape[0]
    tm = _pick_tile(T, 512, 16)

    def body(x_ref, g_ref, u_ref):
        xv = x_ref[...]
        rstd = lax.rsqrt(jnp.mean(xv * xv, axis=-1, keepdims=True) + NORM_EPS)
        u_ref[...] = (xv * rstd * g_ref[...]).astype(BF16)

    return pl.pallas_call(
        body, out_shape=jax.ShapeDtypeStruct((T, D_MODEL), BF16), grid=(T // tm,),
        in_specs=[pl.BlockSpec((tm, D_MODEL), lambda i: (i, 0)), pl.BlockSpec((1, D_MODEL), lambda i: (0, 0))],
        out_specs=pl.BlockSpec((tm, D_MODEL), lambda i: (i, 0)),
        compiler_params=_params("parallel"), name=name)(x, gain)


def _rms_bwd(x, gain, dus, dres, name, dilations=(1,)):
    T = x.shape[0]
    tm = _pick_tile(T, PERM_TILE, 16 * max(dilations))
    n_du = len(dus)

    def body(x_ref, g_ref, *refs):
        du_refs, dres_ref = refs[:n_du], refs[n_du]
        dx_ref, dxb_ref, dg_ref, du_scr = refs[n_du + 1:]

        @pl.when(pl.program_id(0) == 0)
        def _():
            dg_ref[...] = jnp.zeros_like(dg_ref)

        if tuple(dilations) == (1,):
            du = du_refs[0][...]
        else:
            for i, (d, du_ref) in enumerate(zip(dilations, du_refs)):
                for j in range(D_MODEL // LANES):
                    lanes = slice(j * LANES, (j + 1) * LANES)
                    if d == 1:
                        du_scr[j] = du_ref[:, lanes] if i == 0 else du_scr[j] + du_ref[:, lanes]
                        continue
                    blk = du_scr.at[j]
                    for r in range(d):
                        rows = _class_rows(r, d, tm)
                        blk[rows, :] = du_ref[r, :, lanes] if i == 0 else blk[rows, :] + du_ref[r, :, lanes]
            du = jnp.concatenate([du_scr[j] for j in range(D_MODEL // LANES)], axis=1)
        xv = x_ref[...]
        rstd = lax.rsqrt(jnp.mean(xv * xv, axis=-1, keepdims=True) + NORM_EPS)
        n = xv * rstd
        dg_ref[...] += jnp.sum(du * n, axis=0, keepdims=True)
        dn = du * g_ref[...]
        dx = dres_ref[...] + rstd * (dn - n * jnp.mean(dn * n, axis=-1, keepdims=True))
        dx_ref[...] = dx
        dxb_ref[...] = dx.astype(BF16)

    row = pl.BlockSpec((tm, D_MODEL), lambda i: (i, 0))
    vec = pl.BlockSpec((1, D_MODEL), lambda i: (0, 0))
    return pl.pallas_call(
        body,
        out_shape=(jax.ShapeDtypeStruct((T, D_MODEL), F32), jax.ShapeDtypeStruct((T, D_MODEL), BF16),
                   jax.ShapeDtypeStruct((1, D_MODEL), F32)),
        grid=(T // tm,), in_specs=[row, vec] + [_residue_spec(d, tm, D_MODEL) for d in dilations] + [row],
        out_specs=(row, row, vec), scratch_shapes=[pltpu.VMEM((D_MODEL // LANES, tm, LANES), F32)],
        compiler_params=_params("arbitrary"), name=name)(
            x, gain, *[_residue_view(du, d) for du, d in zip(dus, dilations)], dres)


def _loss_head(h, target, gain, name):
    T = h.shape[0]
    tm = _pick_tile(T, 512, 16)
    inv_f = 1.0 / D_MODEL

    def body(h_ref, t_ref, g_ref, dh_ref, dhb_ref, dg_ref, loss_ref):
        @pl.when(pl.program_id(0) == 0)
        def _():
            dg_ref[...] = jnp.zeros_like(dg_ref)
            loss_ref[...] = jnp.zeros_like(loss_ref)

        hv = h_ref[...]
        g = g_ref[...]
        rstd = lax.rsqrt(jnp.mean(hv * hv, axis=-1, keepdims=True) + NORM_EPS)
        n = hv * rstd
        err = n * g - t_ref[...]
        loss_ref[...] += (0.5 * inv_f) * jnp.sum(err * err, axis=0, keepdims=True)
        dy = err * inv_f
        dg_ref[...] += jnp.sum(dy * n, axis=0, keepdims=True)
        dn = dy * g
        dh = rstd * (dn - n * jnp.mean(dn * n, axis=-1, keepdims=True))
        dh_ref[...] = dh
        dhb_ref[...] = dh.astype(BF16)

    row = pl.BlockSpec((tm, D_MODEL), lambda i: (i, 0))
    vec = pl.BlockSpec((1, D_MODEL), lambda i: (0, 0))
    return pl.pallas_call(
        body,
        out_shape=(jax.ShapeDtypeStruct((T, D_MODEL), F32), jax.ShapeDtypeStruct((T, D_MODEL), BF16),
                   jax.ShapeDtypeStruct((1, D_MODEL), F32), jax.ShapeDtypeStruct((1, D_MODEL), F32)),
        grid=(T // tm,), in_specs=[row, row, vec], out_specs=(row, row, vec, vec),
        compiler_params=_params("arbitrary"), name=name)(h, target, gain)


FFN_TILE = 256


def _ffn_in(h, gain, w_in, name):
    T = h.shape[0]
    tm = _pick_tile(T, ROW_TILE, 16)

    def body(h_ref, g_ref, w_ref, n_ref, gate_ref, up_ref, a_ref):
        hv = h_ref[...]
        rstd = lax.rsqrt(jnp.mean(hv * hv, axis=-1, keepdims=True) + NORM_EPS)
        n = (hv * rstd * g_ref[...]).astype(BF16)
        n_ref[...] = n
        for c0 in range(0, D_FF, FFN_TILE):
            cols = slice(c0, c0 + FFN_TILE)
            gate = _dot(n, w_ref[c0:c0 + FFN_TILE, :], NT)
            up = _dot(n, w_ref[D_FF + c0:D_FF + c0 + FFN_TILE, :], NT)
            gate_ref[:, cols] = gate.astype(BF16)
            up_ref[:, cols] = up.astype(BF16)
            a_ref[:, cols] = (gate * _sigmoid(gate) * up).astype(BF16)

    row = pl.BlockSpec((tm, D_MODEL), lambda i: (i, 0))
    wide = pl.BlockSpec((tm, D_FF), lambda i: (i, 0))
    wide_shape = jax.ShapeDtypeStruct((T, D_FF), BF16)
    return pl.pallas_call(
        body, out_shape=(jax.ShapeDtypeStruct((T, D_MODEL), BF16), wide_shape, wide_shape, wide_shape),
        grid=(T // tm,),
        in_specs=[row, pl.BlockSpec((1, D_MODEL), lambda i: (0, 0)), _whole((2 * D_FF, D_MODEL), lambda i: (0, 0))],
        out_specs=(row, wide, wide, wide), compiler_params=_params("parallel"), name=name)(h, gain, w_in)


def _ffn_down_dx(dhb, w_down, gate, up, name):
    T = dhb.shape[0]
    tm = _pick_tile(T, ROW_TILE, 16)

    def body(dh_ref, w_ref, gate_ref, up_ref, dgate_ref, dup_ref):
        dh = dh_ref[...]
        for c0 in range(0, D_FF, FFN_TILE):
            cols = slice(c0, c0 + FFN_TILE)
            da = _dot(dh, w_ref[c0:c0 + FFN_TILE, :], NT)
            gate = gate_ref[:, cols].astype(F32)
            sg = _sigmoid(gate)
            dgate_ref[:, cols] = (da * up_ref[:, cols].astype(F32) * (sg * (1.0 + gate * (1.0 - sg)))).astype(BF16)
            dup_ref[:, cols] = (da * gate * sg).astype(BF16)

    wide = pl.BlockSpec((tm, D_FF), lambda i: (i, 0))
    wide_shape = jax.ShapeDtypeStruct((T, D_FF), BF16)
    return pl.pallas_call(
        body, out_shape=(wide_shape, wide_shape), grid=(T // tm,),
        in_specs=[pl.BlockSpec((tm, D_MODEL), lambda i: (i, 0)), _whole((D_FF, D_MODEL), lambda i: (0, 0)), wide, wide],
        out_specs=(wide, wide), compiler_params=_params("parallel"), name=name)(dhb, w_down, gate, up)


def _tri(n, lower):
    r = lax.broadcasted_iota(jnp.int32, (n, n), 0)
    c = lax.broadcasted_iota(jnp.int32, (n, n), 1)
    return (c <= r) if lower else (c >= r)


def _running_sum(x, lower):
    n = x.shape[0]
    tri = _tri(n, lower).astype(F32)
    return lax.dot_general(tri, x, NN, precision=lax.Precision.HIGHEST, preferred_element_type=F32)


def _hgrn_gates(q_raw, f_raw, lb):
    C = q_raw.shape[0]
    sig_f = _sigmoid(f_raw)
    forget = lb + (1.0 - lb) * sig_f
    key = 1.0 - forget
    log_f = jnp.log(forget)
    b = _running_sum(log_f, True)
    first_half = lax.broadcasted_iota(jnp.int32, log_f.shape, 0) < C // 2
    r = jnp.sum(jnp.where(first_half, log_f, 0.0), axis=0, keepdims=True)
    b_last = jnp.sum(log_f, axis=0, keepdims=True)
    e_a = jnp.exp(jnp.minimum(b - r, HGRN_EXP_CLAMP))
    e_b = jnp.exp(jnp.minimum(r - b, HGRN_EXP_CLAMP))
    e_q = jnp.exp(b)
    e_k = jnp.exp(b_last - b)
    sig_q = _sigmoid(q_raw)
    query = q_raw * sig_q
    return dict(sig_f=sig_f, forget=forget, sig_q=sig_q, e_a=e_a, e_b=e_b, e_q=e_q, e_k=e_k,
                e_last=jnp.exp(b_last), q_a=query * e_a, k_b=key * e_b, q_hat=query * e_q, k_til=key * e_k)


def _hgrn_fwd(proj, lb, gain, name):
    T = proj.shape[0]
    C = HGRN_CHUNK
    H, HD = HGRN_HEADS, HGRN_DIM

    def body(q_ref, f_ref, i_ref, g_ref, lb_ref, gain_ref, og_ref, o_ref, st_ref, s_scr):
        @pl.when(pl.program_id(0) == 0)
        def _():
            s_scr[...] = jnp.zeros_like(s_scr)

        st_ref[0] = s_scr[...]
        gt = _hgrn_gates(q_ref[...], f_ref[...], lb_ref[...])
        causal = _tri(C, True)
        gain_v = gain_ref[...]
        heads = [slice(h * HD, (h + 1) * HD) for h in range(H)]
        q_a, k_b = gt["q_a"].astype(BF16), gt["k_b"].astype(BF16)
        q_hat, k_til = gt["q_hat"].astype(BF16), gt["k_til"].astype(BF16)
        v = i_ref[...].astype(BF16)
        p = [jnp.where(causal, _dot(q_a[:, sl], k_b[:, sl], NT), 0.0).astype(BF16) for sl in heads]
        s_t = [s_scr[h] for h in range(H)]
        o = [_dot(p[h], v[:, sl], NN) + _dot(q_hat[:, sl], s_t[h].astype(BF16), NT) for h, sl in enumerate(heads)]
        s_new = [gt["e_last"][:, sl] * s_t[h] + _dot(v[:, sl], k_til[:, sl], TN) for h, sl in enumerate(heads)]
        for h, sl in enumerate(heads):
            s_scr[h] = s_new[h]
            o_ref[:, sl] = o[h]
            rstd = lax.rsqrt(jnp.mean(o[h] * o[h], axis=-1, keepdims=True) + NORM_EPS)
            g_raw = g_ref[:, sl]
            og_ref[:, sl] = (o[h] * rstd * gain_v * (g_raw * _sigmoid(g_raw))).astype(BF16)

    col = lambda j: pl.BlockSpec((C, D_MODEL), lambda c: (c, j))
    row = pl.BlockSpec((C, D_MODEL), lambda c: (c, 0))
    return pl.pallas_call(
        body,
        out_shape=(jax.ShapeDtypeStruct((T, D_MODEL), BF16), jax.ShapeDtypeStruct((T, D_MODEL), F32),
                   jax.ShapeDtypeStruct((T // C, H, HD, HD), F32)),
        grid=(T // C,),
        in_specs=[col(0), col(1), col(2), col(3), pl.BlockSpec((1, D_MODEL), lambda c: (0, 0)),
                  pl.BlockSpec((1, HD), lambda c: (0, 0))],
        out_specs=(row, row, pl.BlockSpec((1, H, HD, HD), lambda c: (c, 0, 0, 0))),
        scratch_shapes=[pltpu.VMEM((H, HD, HD), F32)],
        compiler_params=_params("arbitrary"), name=name)(proj, proj, proj, proj, lb, gain)


def _hgrn_bwd(proj, o_pre, d_og, states, lb, gain, name):
    T = proj.shape[0]
    C = HGRN_CHUNK
    H, HD = HGRN_HEADS, HGRN_DIM
    NC = T // C

    def body(q_ref, f_ref, i_ref, g_ref, o_ref, dog_ref, st_ref, lb_ref, gain_ref,
             dproj_ref, dlb_ref, dgain_ref, ds_scr, dq_scr, dk_scr, db_scr):
        @pl.when(pl.program_id(0) == 0)
        def _():
            ds_scr[...] = jnp.zeros_like(ds_scr)
            dlb_ref[...] = jnp.zeros_like(dlb_ref)
            dgain_ref[...] = jnp.zeros_like(dgain_ref)

        lbv = lb_ref[...]
        q_raw = q_ref[...]
        gt = _hgrn_gates(q_raw, f_ref[...], lbv)
        causal = _tri(C, True)
        last_row = lax.broadcasted_iota(jnp.int32, (C, HD), 0) == C - 1
        gain_v = gain_ref[...]
        heads = [slice(h * HD, (h + 1) * HD) for h in range(H)]
        hs = range(H)
        o = [o_ref[:, sl] for sl in heads]
        rstd = [lax.rsqrt(jnp.mean(x * x, axis=-1, keepdims=True) + NORM_EPS) for x in o]
        n = [x * r for x, r in zip(o, rstd)]
        g_raw = [g_ref[:, sl] for sl in heads]
        sg = [_sigmoid(x) for x in g_raw]
        d_out = [dog_ref[:, sl] for sl in heads]
        dy = [d * (g * s) for d, g, s in zip(d_out, g_raw, sg)]
        dn = [x * gain_v for x in dy]
        do = [(rstd[h] * (dn[h] - n[h] * jnp.mean(dn[h] * n[h], axis=-1, keepdims=True))).astype(BF16) for h in hs]
        dgain = dy[0] * n[0]
        for h in range(1, H):
            dgain = dgain + dy[h] * n[h]
        dgain_ref[...] += jnp.sum(dgain, axis=0, keepdims=True)
        for h, sl in enumerate(heads):
            dproj_ref[:, 3 * D_MODEL + h * HD:3 * D_MODEL + (h + 1) * HD] = (
                d_out[h] * n[h] * gain_v * (sg[h] * (1.0 + g_raw[h] * (1.0 - sg[h])))).astype(BF16)
        q_ab, k_bb = gt["q_a"].astype(BF16), gt["k_b"].astype(BF16)
        q_hb, k_tb = gt["q_hat"].astype(BF16), gt["k_til"].astype(BF16)
        v = i_ref[...].astype(BF16)
        s_t = [st_ref[0, h] for h in hs]
        ds_t = [ds_scr[h] for h in hs]
        ds_b = [x.astype(BF16) for x in ds_t]
        p = [jnp.where(causal, _dot(q_ab[:, sl], k_bb[:, sl], NT), 0.0).astype(BF16) for sl in heads]
        dp = [jnp.where(causal, _dot(do[h], v[:, sl], NT), 0.0).astype(BF16) for h, sl in enumerate(heads)]
        dv = [_dot(p[h], do[h], TN) + _dot(k_tb[:, sl], ds_b[h], NT) for h, sl in enumerate(heads)]
        dq_a = [_dot(dp[h], k_bb[:, sl], NN) for h, sl in enumerate(heads)]
        dk_b = [_dot(dp[h], q_ab[:, sl], TN) for h, sl in enumerate(heads)]
        dq_hat = [_dot(do[h], s_t[h].astype(BF16), NN) for h in hs]
        dk_til = [_dot(v[:, sl], ds_b[h], NN) for h, sl in enumerate(heads)]
        ds_new = [_dot(do[h], q_hb[:, sl], TN) + gt["e_last"][:, sl] * ds_t[h] for h, sl in enumerate(heads)]
        for h, sl in enumerate(heads):
            k_til = gt["k_til"][:, sl]
            ds_scr[h] = ds_new[h]
            db_last = jnp.sum(ds_t[h] * gt["e_last"][:, sl] * s_t[h], axis=0, keepdims=True) + jnp.sum(
                dk_til[h] * k_til, axis=0, keepdims=True)
            dproj_ref[:, 2 * D_MODEL + h * HD:2 * D_MODEL + (h + 1) * HD] = dv[h].astype(BF16)
            dq_scr[:, sl] = dq_a[h] * gt["e_a"][:, sl] + dq_hat[h] * gt["e_q"][:, sl]
            dk_scr[:, sl] = dk_b[h] * gt["e_b"][:, sl] + dk_til[h] * gt["e_k"][:, sl]
            db = (dq_a[h] * q_ab[:, sl].astype(F32) + dq_hat[h] * gt["q_hat"][:, sl]
                  - dk_b[h] * k_bb[:, sl].astype(F32) - dk_til[h] * k_til)
            db_scr[:, sl] = db + jnp.where(last_row, db_last, 0.0)
        dlogf = _running_sum(db_scr[...], False)
        sig_f, forget, sig_q = gt["sig_f"], gt["forget"], gt["sig_q"]
        dforget = dlogf / forget - dk_scr[...]
        dproj_ref[:, D_MODEL:2 * D_MODEL] = (dforget * (1.0 - lbv) * sig_f * (1.0 - sig_f)).astype(BF16)
        dlb_ref[...] += jnp.sum(dforget * (1.0 - sig_f), axis=0, keepdims=True)
        dproj_ref[:, 0:D_MODEL] = (dq_scr[...] * (sig_q * (1.0 + q_raw * (1.0 - sig_q)))).astype(BF16)

    col = lambda j: pl.BlockSpec((C, D_MODEL), lambda c: (NC - 1 - c, j))
    row = pl.BlockSpec((C, D_MODEL), lambda c: (NC - 1 - c, 0))
    return pl.pallas_call(
        body,
        out_shape=(jax.ShapeDtypeStruct((T, 4 * D_MODEL), BF16), jax.ShapeDtypeStruct((1, D_MODEL), F32),
                   jax.ShapeDtypeStruct((1, HD), F32)),
        grid=(NC,),
        in_specs=[col(0), col(1), col(2), col(3), row, row,
                  pl.BlockSpec((1, H, HD, HD), lambda c: (NC - 1 - c, 0, 0, 0)),
                  pl.BlockSpec((1, D_MODEL), lambda c: (0, 0)), pl.BlockSpec((1, HD), lambda c: (0, 0))],
        out_specs=(pl.BlockSpec((C, 4 * D_MODEL), lambda c: (NC - 1 - c, 0)),
                   pl.BlockSpec((1, D_MODEL), lambda c: (0, 0)), pl.BlockSpec((1, HD), lambda c: (0, 0))),
        scratch_shapes=[pltpu.VMEM((H, HD, HD), F32), pltpu.VMEM((C, D_MODEL), F32),
                        pltpu.VMEM((C, D_MODEL), F32), pltpu.VMEM((C, D_MODEL), F32)],
        compiler_params=_params("arbitrary"), name=name)(proj, proj, proj, proj, o_pre, d_og, states, lb, gain)


def _attn_masks():
    r = lax.broadcasted_iota(jnp.int32, (ATTN_BLOCK, ATTN_BLOCK), 0)
    c = lax.broadcasted_iota(jnp.int32, (ATTN_BLOCK, ATTN_BLOCK), 1)
    return c >= r, c <= r


def _attn_fwd(qkv, dilation, name):
    T = qkv.shape[0]
    nb = T // dilation // ATTN_BLOCK
    W = ATTN_GROUP_WIDTH
    scale = ATTN_DIM ** -0.5

    def body(q_ref, kp_ref, kc_ref, vp_ref, vc_ref, o_ref, lse_ref):
        no_prev = jnp.where(pl.program_id(1) > 0, 0.0, NEG_BIG)
        m_prev, m_cur = _attn_masks()
        ones = jnp.ones((ATTN_BLOCK, ATTN_DIM), BF16)
        heads = [slice(h * ATTN_DIM, (h + 1) * ATTN_DIM) for h in range(ATTN_GROUP_HEADS)]
        s_p = [jnp.where(m_prev, _dot(q_ref[:, sl], kp_ref[:, sl], NT) * scale + no_prev, NEG_BIG) for sl in heads]
        s_c = [jnp.where(m_cur, _dot(q_ref[:, sl], kc_ref[:, sl], NT) * scale, NEG_BIG) for sl in heads]
        m = [jnp.max(jnp.maximum(a, b), axis=-1, keepdims=True) for a, b in zip(s_p, s_c)]
        p_p = [jnp.exp(a - mx).astype(BF16) for a, mx in zip(s_p, m)]
        p_c = [jnp.exp(b - mx).astype(BF16) for b, mx in zip(s_c, m)]
        l = [_dot(a, ones, NN) + _dot(b, ones, NN) for a, b in zip(p_p, p_c)]
        acc = [_dot(a, vp_ref[:, sl], NN) + _dot(b, vc_ref[:, sl], NN) for a, b, sl in zip(p_p, p_c, heads)]
        for sl, a, lv, mx in zip(heads, acc, l, m):
            o_ref[:, sl] = a / lv
            lse_ref[:, sl] = mx + jnp.log(lv)

    blk = lambda col, prev: pl.BlockSpec(
        (ATTN_BLOCK, W), lambda s, n: (s * nb + (jnp.maximum(n - 1, 0) if prev else n), col))
    out = pl.BlockSpec((ATTN_BLOCK, W), lambda s, n: (s * nb + n, 0))
    return pl.pallas_call(
        body, out_shape=(jax.ShapeDtypeStruct((T, W), F32),) * 2, grid=(dilation, nb),
        in_specs=[blk(0, False), blk(1, True), blk(1, False), blk(2, True), blk(2, False)],
        out_specs=(out, out), compiler_params=_params("parallel", "arbitrary"), name=name)(qkv, qkv, qkv, qkv, qkv)


def _attn_bwd(qkv, d_out, lse, delta, cos, sin, dilation, name):
    T = qkv.shape[0]
    nb = T // dilation // ATTN_BLOCK
    W = ATTN_GROUP_WIDTH
    scale = ATTN_DIM ** -0.5

    def unrope(x, cos_v, sin_v):
        return x * cos_v + pltpu.roll(x * sin_v, ATTN_DIM // 2, 1)

    def body(q_ref, kp_ref, kc_ref, vp_ref, vc_ref, do_ref, lse_ref, dl_ref, cos_ref, sin_ref,
             out_ref, dq_scr, dk_scr, dv_scr):
        n = pl.program_id(1)
        cos_v, sin_v = cos_ref[...], sin_ref[...]

        @pl.when(n > 0)
        def _():
            for h in range(ATTN_GROUP_HEADS):
                sl = slice(h * ATTN_DIM, (h + 1) * ATTN_DIM)
                out_ref[:, sl] = unrope(dq_scr[:, sl], cos_v, sin_v).astype(BF16)

        @pl.when(n == nb)
        def _():
            for h in range(ATTN_GROUP_HEADS):
                sl = slice(h * ATTN_DIM, (h + 1) * ATTN_DIM)
                out_ref[:, W + h * ATTN_DIM:W + (h + 1) * ATTN_DIM] = unrope(dk_scr[:, sl], cos_v, sin_v).astype(BF16)
                out_ref[:, 2 * W + h * ATTN_DIM:2 * W + (h + 1) * ATTN_DIM] = dv_scr[:, sl].astype(BF16)

        @pl.when(n == 0)
        def _():
            dk_scr[...] = jnp.zeros_like(dk_scr)
            dv_scr[...] = jnp.zeros_like(dv_scr)

        @pl.when(n < nb)
        def _():
            has_prev = n > 0
            no_prev = jnp.where(has_prev, 0.0, NEG_BIG)
            m_prev, m_cur = _attn_masks()
            heads = [slice(h * ATTN_DIM, (h + 1) * ATTN_DIM) for h in range(ATTN_GROUP_HEADS)]
            s_p = [_dot(q_ref[:, sl], kp_ref[:, sl], NT) for sl in heads]
            s_c = [_dot(q_ref[:, sl], kc_ref[:, sl], NT) for sl in heads]
            dp_p = [_dot(do_ref[:, sl], vp_ref[:, sl], NT) for sl in heads]
            dp_c = [_dot(do_ref[:, sl], vc_ref[:, sl], NT) for sl in heads]
            p_p = [jnp.where(m_prev, jnp.exp(s * scale - lse_ref[:, sl] + no_prev), 0.0) for s, sl in zip(s_p, heads)]
            p_c = [jnp.where(m_cur, jnp.exp(s * scale - lse_ref[:, sl]), 0.0) for s, sl in zip(s_c, heads)]
            ds_p = [(p * (dp - dl_ref[:, sl]) * scale).astype(BF16) for p, dp, sl in zip(p_p, dp_p, heads)]
            ds_c = [(p * (dp - dl_ref[:, sl]) * scale).astype(BF16) for p, dp, sl in zip(p_c, dp_c, heads)]
            p_p = [p.astype(BF16) for p in p_p]
            p_c = [p.astype(BF16) for p in p_c]
            dk_prev = [dk_scr[:, sl] + _dot(ds, q_ref[:, sl], TN) for ds, sl in zip(ds_p, heads)]
            dv_prev = [dv_scr[:, sl] + _dot(p, do_ref[:, sl], TN) for p, sl in zip(p_p, heads)]
            dq = [_dot(a, kp_ref[:, sl], NN) + _dot(b, kc_ref[:, sl], NN) for a, b, sl in zip(ds_p, ds_c, heads)]
            dk_cur = [_dot(ds, q_ref[:, sl], TN) for ds, sl in zip(ds_c, heads)]
            dv_cur = [_dot(p, do_ref[:, sl], TN) for p, sl in zip(p_c, heads)]
            for h, sl in enumerate(heads):
                out_ref[:, W + h * ATTN_DIM:W + (h + 1) * ATTN_DIM] = unrope(dk_prev[h], cos_v, sin_v).astype(BF16)
                out_ref[:, 2 * W + h * ATTN_DIM:2 * W + (h + 1) * ATTN_DIM] = dv_prev[h].astype(BF16)
                dq_scr[:, sl] = dq[h]
                dk_scr[:, sl] = dk_cur[h]
                dv_scr[:, sl] = dv_cur[h]

    def cur(n):
        return jnp.minimum(n, nb - 1)

    def late(n):
        return jnp.maximum(n - 1, 0)

    qkv_blk = lambda col, prev: pl.BlockSpec(
        (ATTN_BLOCK, W), lambda s, n: (s * nb + (jnp.maximum(cur(n) - 1, 0) if prev else cur(n)), col))
    row = pl.BlockSpec((ATTN_BLOCK, W), lambda s, n: (s * nb + cur(n), 0))
    tab = pl.BlockSpec((ATTN_BLOCK, ATTN_DIM), lambda s, n: (s * nb + late(n), 0))
    return pl.pallas_call(
        body, out_shape=jax.ShapeDtypeStruct((T, 3 * W), BF16), grid=(dilation, nb + 1),
        in_specs=[qkv_blk(0, False), qkv_blk(1, True), qkv_blk(1, False), qkv_blk(2, True), qkv_blk(2, False),
                  row, row, row, tab, tab],
        out_specs=pl.BlockSpec((ATTN_BLOCK, 3 * W), lambda s, n: (s * nb + late(n), 0)),
        scratch_shapes=[pltpu.VMEM((ATTN_BLOCK, W), F32)] * 3,
        compiler_params=_params("parallel", "arbitrary"), name=name)(
            qkv, qkv, qkv, qkv, qkv, d_out, lse, delta, cos, sin)


PERM_TILE = 512
LANES = 128


def _residue_view(x, d):
    return x if d == 1 else x.reshape(d, x.shape[0] // d, x.shape[1])


def _residue_spec(d, tm, cols):
    if d == 1:
        return pl.BlockSpec((tm, cols), lambda i: (i, 0))
    return pl.BlockSpec((d, tm // d, cols), lambda i: (0, i, 0))


def _residue_shape(T, d, cols, dtype):
    return jax.ShapeDtypeStruct((T, cols) if d == 1 else (d, T // d, cols), dtype)


def _class_rows(r, d, tm):
    return pl.ds(r, tm // d, stride=d)


def _attn_norm(h, gain, cos, sin, name):
    T = h.shape[0]
    tm = _pick_tile(T, PERM_TILE, 16 * max(ATTN_DILATIONS))
    dils = ATTN_DILATIONS

    def body(h_ref, g_ref, cos_ref, sin_ref, *refs):
        u_refs, c_refs, s_refs, u_scr = refs[0:3], refs[3:6], refs[6:9], refs[9]
        hv = h_ref[...]
        rstd = lax.rsqrt(jnp.mean(hv * hv, axis=-1, keepdims=True) + NORM_EPS)
        u = hv * rstd * g_ref[...]
        for j in range(D_MODEL // LANES):
            u_scr[j] = u[:, j * LANES:(j + 1) * LANES]
        for d, u_ref, c_ref, s_ref in zip(dils, u_refs, c_refs, s_refs):
            if d == 1:
                u_ref[...] = u.astype(BF16)
                c_ref[...] = cos_ref[...]
                s_ref[...] = sin_ref[...]
                continue
            for r in range(d):
                rows = _class_rows(r, d, tm)
                for j in range(D_MODEL // LANES):
                    u_ref[r, :, j * LANES:(j + 1) * LANES] = u_scr.at[j][rows, :].astype(BF16)
                c_ref[r] = cos_ref[rows, :]
                s_ref[r] = sin_ref[rows, :]

    row = pl.BlockSpec((tm, D_MODEL), lambda i: (i, 0))
    tab = pl.BlockSpec((tm, ATTN_DIM), lambda i: (i, 0))
    res = pl.pallas_call(
        body,
        out_shape=([_residue_shape(T, d, D_MODEL, BF16) for d in dils]
                   + [_residue_shape(T, d, ATTN_DIM, F32) for d in dils] * 2),
        grid=(T // tm,), in_specs=[row, pl.BlockSpec((1, D_MODEL), lambda i: (0, 0)), tab, tab],
        out_specs=([_residue_spec(d, tm, D_MODEL) for d in dils] + [_residue_spec(d, tm, ATTN_DIM) for d in dils] * 2),
        scratch_shapes=[pltpu.VMEM((D_MODEL // LANES, tm, LANES), F32)],
        compiler_params=_params("parallel"), name=name)(h, gain, cos, sin)
    flat = [r.reshape(T, r.shape[-1]) for r in res]
    return flat[0:3], flat[3:6], flat[6:9]


def _attn_merge_fwd(outs, lses, name):
    T = outs[0].shape[0]
    W = ATTN_GROUP_WIDTH
    tm = _pick_tile(T, PERM_TILE, 16 * max(ATTN_DILATIONS))
    dils = ATTN_DILATIONS

    def body(*refs):
        o_refs, l_refs, oc_ref, lse_refs = refs[0:3], refs[3:6], refs[6], refs[7:10]
        o_scr, l_scr, t_scr = refs[10:13]
        nh = ATTN_GROUP_HEADS
        for g, d in enumerate(dils):
            for j in range(nh):
                lanes = slice(j * LANES, (j + 1) * LANES)
                if d == 1:
                    o_scr[g * nh + j] = o_refs[g][:, lanes]
                    l_scr[g * nh + j] = l_refs[g][:, lanes]
                    continue
                for r in range(d):
                    rows = _class_rows(r, d, tm)
                    o_scr.at[g * nh + j][rows, :] = o_refs[g][r, :, lanes]
                    l_scr.at[g * nh + j][rows, :] = l_refs[g][r, :, lanes]
        for j in range(nh):
            lanes = slice(j * LANES, (j + 1) * LANES)
            ls = [l_scr[g * nh + j] for g in range(3)]
            m = jnp.maximum(jnp.maximum(ls[0], ls[1]), ls[2])
            tot = m + jnp.log(jnp.exp(ls[0] - m) + jnp.exp(ls[1] - m) + jnp.exp(ls[2] - m))
            t_scr[j] = tot
            for g, d in enumerate(dils):
                oc_ref[:, g * W + j * LANES:g * W + (j + 1) * LANES] = (
                    o_scr[g * nh + j] * jnp.exp(ls[g] - tot)).astype(BF16)
                if d == 1:
                    lse_refs[g][:, lanes] = tot
                    continue
                for r in range(d):
                    lse_refs[g][r, :, lanes] = t_scr.at[j][_class_rows(r, d, tm), :]

    in_blk = [_residue_spec(d, tm, W) for d in dils]
    n_blk = 3 * ATTN_GROUP_HEADS
    res = pl.pallas_call(
        body, out_shape=[jax.ShapeDtypeStruct((T, 3 * W), BF16)] + [_residue_shape(T, d, W, F32) for d in dils],
        grid=(T // tm,), in_specs=in_blk * 2,
        out_specs=[pl.BlockSpec((tm, 3 * W), lambda i: (i, 0))] + in_blk,
        scratch_shapes=[pltpu.VMEM((n_blk, tm, LANES), F32), pltpu.VMEM((n_blk, tm, LANES), F32),
                        pltpu.VMEM((ATTN_GROUP_HEADS, tm, LANES), F32)],
        compiler_params=_params("parallel"), name=name)(
            *[_residue_view(o, d) for o, d in zip(outs, dils)], *[_residue_view(l, d) for l, d in zip(lses, dils)])
    return res[0], [r.reshape(T, W) for r in res[1:]]


def _attn_merge_bwd(d_oc, oc, name):
    T = d_oc.shape[0]
    W = ATTN_GROUP_WIDTH
    tm = _pick_tile(T, PERM_TILE, 16 * max(ATTN_DILATIONS))
    dils = ATTN_DILATIONS

    def body(d_ref, o_ref, *refs):
        delta_refs, db_refs, dl_scr, d_scr = refs[0:3], refs[3:6], refs[6], refs[7]
        nh = ATTN_GROUP_HEADS
        for j in range(nh):
            tot = jnp.zeros((tm, 1), F32)
            for g in range(3):
                cols = slice(g * W + j * LANES, g * W + (j + 1) * LANES)
                d_blk = d_ref[:, cols]
                d_scr[g * nh + j] = d_blk
                tot = tot + jnp.sum(d_blk * o_ref[:, cols].astype(F32), axis=-1, keepdims=True)
            dl_scr[j] = jnp.broadcast_to(tot, (tm, LANES))
        for g, d in enumerate(dils):
            for j in range(nh):
                lanes = slice(j * LANES, (j + 1) * LANES)
                if d == 1:
                    delta_refs[g][:, lanes] = dl_scr[j]
                    db_refs[g][:, lanes] = d_scr[g * nh + j].astype(BF16)
                    continue
                for r in range(d):
                    rows = _class_rows(r, d, tm)
                    delta_refs[g][r, :, lanes] = dl_scr.at[j][rows, :]
                    db_refs[g][r, :, lanes] = d_scr.at[g * nh + j][rows, :].astype(BF16)

    wide = pl.BlockSpec((tm, 3 * W), lambda i: (i, 0))
    out_blk = [_residue_spec(d, tm, W) for d in dils]
    res = pl.pallas_call(
        body, out_shape=[_residue_shape(T, d, W, F32) for d in dils] + [_residue_shape(T, d, W, BF16) for d in dils],
        grid=(T // tm,), in_specs=[wide, wide], out_specs=out_blk * 2,
        scratch_shapes=[pltpu.VMEM((ATTN_GROUP_HEADS, tm, LANES), F32),
                        pltpu.VMEM((3 * ATTN_GROUP_HEADS, tm, LANES), F32)],
        compiler_params=_params("parallel"), name=name)(d_oc, oc)
    flat = [r.reshape(T, W) for r in res]
    return flat[0:3], flat[3:6]


def _rope_tables(T):
    inv_freq = 1.0 / (ROPE_THETA ** (jnp.arange(0, ATTN_DIM, 2, dtype=F32) / ATTN_DIM))
    ang = jnp.arange(T, dtype=F32)[:, None] * inv_freq[None, :]
    cos, sin = jnp.cos(ang), jnp.sin(ang)
    return jnp.concatenate([cos, cos], axis=1), jnp.concatenate([-sin, sin], axis=1)


WEIGHT_GROUPS = {"hgrn": ("hgrn_in", "hgrn_out"), "ffn0": ("ffn_in0", "ffn_down0"),
                 "attn": ("qkv", "attn_out"), "ffn1": ("ffn_in1", "ffn_down1")}


def _local_step(x, target, norm_mix, norm_ffn, lb, out_gain, final_gain, fetch, publish):
    T = x.shape[0]
    g_mix = [norm_mix[0:1], norm_mix[1:2]]
    g_ffn = [norm_ffn[0:1], norm_ffn[1:2]]
    w = {}

    def whole(name):
        return [(w[name], w[name].shape[0], 0)]

    def qkv_parts(g):
        return [(w["qkv"], ATTN_GROUP_WIDTH, 3 * j + g) for j in range(3)]

    def ffn_fwd(h, layer):
        w.update(fetch(f"ffn{layer}"))
        n, gate, up, a = _ffn_in(h, g_ffn[layer], w[f"ffn_in{layer}"], f"ffn{layer}_in")
        out = _mm_nn([a], [whole(f"ffn_down{layer}")], h, name=f"ffn{layer}_down")
        return out, (n, gate, up, a)

    def ffn_bwd(h, saved, dh, dhb, layer):
        n, gate, up, a = saved
        w_in = w[f"ffn_in{layer}"]
        dgate, dup = _ffn_down_dx(dhb, w[f"ffn_down{layer}"], gate, up, f"ffn{layer}_down_dx")
        grad_in = _mm_tn(dgate, n, name=f"ffn{layer}_in_dw_gate", rows=2 * D_FF)
        grad_in = _mm_tn(dup, n, name=f"ffn{layer}_in_dw_up", into=grad_in, row_tile=D_FF // GRAD_TILE, rows=2 * D_FF)
        grads = {f"ffn_down{layer}": _mm_tn(a, dhb, name=f"ffn{layer}_down_dw"), f"ffn_in{layer}": grad_in}
        publish(f"ffn{layer}", grads)
        dn = _mm_nn([dgate, dup], [[(w_in, D_FF, 0)], [(w_in, D_FF, 1)]], None, name=f"ffn{layer}_in_dx")
        return _rms_bwd(h, g_ffn[layer], [dn], dh, f"ffn{layer}_norm_bwd")

    u0 = _rms_fwd(x, g_mix[0], "hgrn_norm")
    w.update(fetch("hgrn"))
    proj = _mm_nt(u0, whole("hgrn_in"), out_dtype=F32, name="hgrn_in")
    og, o_pre, states = _hgrn_fwd(proj, lb, out_gain, "hgrn_fwd")
    h1 = _mm_nn([og], [whole("hgrn_out")], x, name="hgrn_out")
    h2, ffn0 = ffn_fwd(h1, 0)

    cos, sin = _rope_tables(T)
    u1_g, cos_g, sin_g = _attn_norm(h2, g_mix[1], cos, sin, "attn_norm")
    w.update(fetch("attn"))
    qkv_g, outs, lses = [], [], []
    for g, d in enumerate(ATTN_DILATIONS):
        qkv_g.append(_mm_nt(u1_g[g], qkv_parts(g), out_dtype=BF16, name=f"attn_qkv{g}",
                            rope=(cos_g[g], sin_g[g], 2)))
        o_g, lse_g = _attn_fwd(qkv_g[g], d, f"attn_fwd{g}")
        outs.append(o_g)
        lses.append(lse_g)
    oc, lse_all = _attn_merge_fwd(outs, lses, "attn_merge")
    h3 = _mm_nn([oc], [whole("attn_out")], h2, name="attn_out")
    h4, ffn1 = ffn_fwd(h3, 1)

    dh4, dh4b, d_final, loss_part = _loss_head(h4, target, final_gain, "loss_head")
    dh3, dh3b, d_ffn1 = ffn_bwd(h3, ffn1, dh4, dh4b, 1)

    d_oc = _mm_nt(dh3b, whole("attn_out"), out_dtype=F32, name="attn_out_dx")
    grad_attn_out = _mm_tn(oc, dh3b, name="attn_out_dw")
    delta, d_ocb = _attn_merge_bwd(d_oc, oc, "attn_merge_bwd")
    du1, qkv_pieces = [], []
    for g, d in enumerate(ATTN_DILATIONS):
        dqkv = _attn_bwd(qkv_g[g], d_ocb[g], lse_all[g], delta[g], cos_g[g], sin_g[g], d, f"attn_bwd{g}")
        qkv_pieces.append(_mm_tn(dqkv, u1_g[g], name=f"attn_qkv_dw{g}"))
        du1.append(_mm_nn([dqkv], [qkv_parts(g)], None, name=f"attn_qkv_dx{g}"))
    grad_qkv = jnp.stack([p.reshape(3, ATTN_GROUP_WIDTH, D_MODEL) for p in qkv_pieces], axis=1).reshape(
        3 * ATTN_WIDTH, D_MODEL)
    publish("attn", {"qkv": grad_qkv, "attn_out": grad_attn_out})
    dh2, dh2b, d_mix1 = _rms_bwd(h2, g_mix[1], du1, dh3, "attn_norm_bwd", ATTN_DILATIONS)

    dh1, dh1b, d_ffn0 = ffn_bwd(h1, ffn0, dh2, dh2b, 0)

    d_og = _mm_nt(dh1b, whole("hgrn_out"), out_dtype=F32, name="hgrn_out_dx")
    grad_hgrn_out = _mm_tn(og, dh1b, name="hgrn_out_dw")
    dproj, d_lb, d_out_gain = _hgrn_bwd(proj, o_pre, d_og, states, lb, out_gain, "hgrn_bwd")
    publish("hgrn", {"hgrn_in": _mm_tn(dproj, u0, name="hgrn_in_dw"), "hgrn_out": grad_hgrn_out})
    du0 = _mm_nn([dproj], [whole("hgrn_in")], None, name="hgrn_in_dx")
    dx, _, d_mix0 = _rms_bwd(x, g_mix[0], [du0], dh1, "hgrn_norm_bwd")

    small = dict(norm_mix0=d_mix0, norm_mix1=d_mix1, norm_ffn0=d_ffn0, norm_ffn1=d_ffn1, lb=d_lb,
                 out_gain=d_out_gain, final=d_final, loss=loss_part)
    return dx, small


WEIGHT_NAMES = ("hgrn_in", "hgrn_out", "qkv", "attn_out", "ffn_in0", "ffn_in1", "ffn_down0", "ffn_down1")
MESH_IDS = pl.DeviceIdType.MESH
HBM_SPEC = pl.BlockSpec(memory_space=pl.ANY)


N_PEERS = N_DEV - 1
PEER_OFFSETS = [(dx, dy, dc) for dx in (0, 1) for dy in (0, 1) for dc in (0, 1)][1:]


def _mesh_place():
    x, y, c = lax.axis_index("x"), lax.axis_index("y"), lax.axis_index("c")
    peers = []
    for dx, dy, dc in PEER_OFFSETS:
        px, py, pc = (1 - x if dx else x), (1 - y if dy else y), (1 - c if dc else c)
        peers.append(((px, py, pc), 4 * px + 2 * py + pc))
    return 4 * x + 2 * y + c, peers


def _exchange_launch(srcs, scatter, collective_id, name):
    n = len(srcs)
    src_refs = [jax.new_ref(s, memory_space=pltpu.MemorySpace.HBM) for s in srcs]
    land_refs = [jax.empty_ref(jax.ShapeDtypeStruct(s.shape if scatter else (N_DEV,) + s.shape, s.dtype),
                               memory_space=pltpu.MemorySpace.HBM) for s in srcs]

    @pl.kernel(mesh=plsc.ScalarSubcoreMesh(axis_name="sequencer", num_cores=1), name=name,
               scratch_types=(pltpu.SemaphoreType.DMA((n * N_PEERS,)), pltpu.SemaphoreType.DMA((n * N_PEERS,)),
                              pltpu.SemaphoreType.DMA((n,))),
               compiler_params=pltpu.CompilerParams(collective_id=collective_id))
    def launch(send_sems, recv_sems, local_sems):
        me, peers = _mesh_place()
        barrier = pltpu.get_barrier_semaphore()
        for peer, _ in peers:
            pl.semaphore_signal(barrier, inc=1, device_id=peer, device_id_type=MESH_IDS)
        pl.semaphore_wait(barrier, N_PEERS)
        own = [pltpu.make_async_copy(src_refs[w].at[me] if scatter else src_refs[w], land_refs[w].at[me],
                                     local_sems.at[w]) for w in range(n)]
        for cp in own:
            cp.start()
        copies = [pltpu.make_async_remote_copy(
            src_ref=src_refs[w].at[pid] if scatter else src_refs[w], dst_ref=land_refs[w].at[me],
            send_sem=send_sems.at[w * N_PEERS + k], recv_sem=recv_sems.at[w * N_PEERS + k],
            device_id=peer, device_id_type=MESH_IDS) for w in range(n) for k, (peer, pid) in enumerate(peers)]
        for cp in copies:
            cp.start()
        for cp in copies:
            cp.wait()
        for cp in own:
            cp.wait()

    launch()
    return land_refs


def _gather_small(block, name):
    def body(in_ref, out_ref, send_sems, recv_sems, local_sem):
        me, peers = _mesh_place()
        own = pltpu.make_async_copy(in_ref, out_ref.at[me], local_sem)
        own.start()
        sends = [pltpu.make_async_remote_copy(
            src_ref=in_ref, dst_ref=out_ref.at[me], send_sem=send_sems.at[k], recv_sem=recv_sems.at[k],
            device_id=peer, device_id_type=MESH_IDS) for k, (peer, _) in enumerate(peers)]
        for cp in sends:
            cp.start()
        for cp in sends:
            cp.wait_recv()
        for cp in sends:
            cp.wait_send()
        own.wait()

    return pl.pallas_call(
        body, out_shape=jax.ShapeDtypeStruct((N_DEV,) + block.shape, block.dtype),
        in_specs=[HBM_SPEC], out_specs=HBM_SPEC,
        scratch_shapes=[pltpu.SemaphoreType.DMA((N_PEERS,)), pltpu.SemaphoreType.DMA((N_PEERS,)),
                        pltpu.SemaphoreType.DMA],
        name=name)(block)


def _sum_blocks(recv, name):
    rows = recv.shape[1]
    tr = _pick_tile(rows, 256, 16)

    def body(r_ref, g_ref):
        acc = r_ref[0].astype(F32)
        for j in range(1, N_DEV):
            acc = acc + r_ref[j].astype(F32)
        g_ref[...] = acc

    return pl.pallas_call(
        body, out_shape=jax.ShapeDtypeStruct((rows, D_MODEL), F32), grid=(rows // tr,),
        in_specs=[pl.BlockSpec((N_DEV, tr, D_MODEL), lambda i: (0, i, 0))],
        out_specs=pl.BlockSpec((tr, D_MODEL), lambda i: (i, 0)),
        compiler_params=_params("parallel"), name=name)(recv)


def _adamw_math(w, g, m, v):
    m_new = ADAM_B1 * m + (1.0 - ADAM_B1) * g
    v_new = ADAM_B2 * v + (1.0 - ADAM_B2) * (g * g)
    m_hat = m_new / (1.0 - ADAM_B1 ** ADAM_STEP)
    v_hat = v_new / (1.0 - ADAM_B2 ** ADAM_STEP)
    delta = -ADAM_LR * (m_hat / (jnp.sqrt(v_hat) + ADAM_EPS) + ADAM_WD * w)
    return delta, m_new, v_new


def _adamw(w, g, m, v, name):
    rows, cols = w.shape
    tr = _pick_tile(rows, 256, 8)

    def body(w_ref, g_ref, m_ref, v_ref, d_ref, mo_ref, vo_ref):
        d_ref[...], mo_ref[...], vo_ref[...] = _adamw_math(w_ref[...], g_ref[...], m_ref[...], v_ref[...])

    blk = pl.BlockSpec((tr, cols), lambda i: (i, 0))
    return pl.pallas_call(
        body, out_shape=(jax.ShapeDtypeStruct((rows, cols), F32),) * 3, grid=(rows // tr,),
        in_specs=[blk] * 4, out_specs=(blk,) * 3, compiler_params=_params("parallel"), name=name)(w, g, m, v)


ROW_MIX, ROW_FFN, ROW_LB, ROW_OUT_GAIN, ROW_FINAL = 0, 2, 4, 7, 8
PART_MIX, PART_FFN, PART_LB, PART_OUT_GAIN, PART_FINAL, PART_LOSS = 0, 2, 4, 5, 6, 7


def _small_update(parts_all, w, m, v, name):
    def body(p_ref, w_ref, m_ref, v_ref, g_ref, d_ref, mo_ref, vo_ref, loss_ref):
        def total(row, n=1):
            tot = p_ref[0, row:row + n, :]
            for j in range(1, N_DEV):
                tot = tot + p_ref[j, row:row + n, :]
            return tot

        logits = [w_ref[ROW_LB + i:ROW_LB + i + 1, :] for i in range(3)]
        mx = jnp.maximum(jnp.maximum(logits[0], logits[1]), logits[2])
        ex = [jnp.exp(l - mx) for l in logits]
        den = ex[0] + ex[1] + ex[2]
        prob = [e / den for e in ex]
        d_lb = total(PART_LB)
        g_ref[...] = jnp.zeros_like(g_ref)
        g_ref[ROW_MIX:ROW_MIX + 2, :] = total(PART_MIX, 2)
        g_ref[ROW_FFN:ROW_FFN + 2, :] = total(PART_FFN, 2)
        for i in range(3):
            g_ref[ROW_LB + i:ROW_LB + i + 1, :] = prob[i] * ((d_lb if i == 0 else 0.0) - prob[0] * d_lb)
        g_ref[ROW_OUT_GAIN:ROW_OUT_GAIN + 1, :] = total(PART_OUT_GAIN)
        g_ref[ROW_FINAL:ROW_FINAL + 1, :] = total(PART_FINAL)
        d_ref[...], mo_ref[...], vo_ref[...] = _adamw_math(w_ref[...], g_ref[...], m_ref[...], v_ref[...])
        loss_ref[...] = jnp.sum(total(PART_LOSS), axis=-1, keepdims=True)

    packed = jax.ShapeDtypeStruct((16, D_MODEL), F32)
    return pl.pallas_call(
        body, out_shape=(packed, packed, packed, packed, jax.ShapeDtypeStruct((1, 1), F32)),
        compiler_params=pltpu.CompilerParams(vmem_limit_bytes=VMEM_LIMIT), name=name)(parts_all, w, m, v)


def _pack_small(norm_mix, norm_ffn, lb_logits, out_gain, final):
    pad = jnp.zeros((1, D_MODEL - HGRN_DIM), F32)
    return jnp.concatenate([norm_mix, norm_ffn, lb_logits, jnp.concatenate([out_gain, pad], axis=1),
                            final.reshape(1, D_MODEL), jnp.zeros((16 - ROW_FINAL - 1, D_MODEL), F32)], axis=0)


def _unpack_small(p):
    return (p[ROW_MIX:ROW_MIX + 2], p[ROW_FFN:ROW_FFN + 2], p[ROW_LB:ROW_LB + 3],
            p[ROW_OUT_GAIN:ROW_OUT_GAIN + 1, :HGRN_DIM], p[ROW_FINAL])


def _lower_bound(lb_logits, name):
    def body(l_ref, o_ref):
        logits = [l_ref[i:i + 1, :] for i in range(3)]
        mx = jnp.maximum(jnp.maximum(logits[0], logits[1]), logits[2])
        ex = [jnp.exp(l - mx) for l in logits]
        o_ref[...] = ex[0] / (ex[0] + ex[1] + ex[2])

    return pl.pallas_call(body, out_shape=jax.ShapeDtypeStruct((1, D_MODEL), F32), name=name)(lb_logits)


def kernel(x, norm_mix, norm_ffn, hgrn_w_in, hgrn_lb_logits, hgrn_out_norm, hgrn_w_out, attn_w_qkv, attn_w_out, ffn_w_in, ffn_w_down, final_norm, loss_target, m_norm_mix, m_norm_ffn, m_hgrn_w_in, m_hgrn_lb_logits, m_hgrn_out_norm, m_hgrn_w_out, m_attn_w_qkv, m_attn_w_out, m_ffn_w_in, m_ffn_w_down, m_final_norm, v_norm_mix, v_norm_ffn, v_hgrn_w_in, v_hgrn_lb_logits, v_hgrn_out_norm, v_hgrn_w_out, v_attn_w_qkv, v_attn_w_out, v_ffn_w_in, v_ffn_w_down, v_final_norm):
    col_sharded = {"hgrn_in": hgrn_w_in[0], "qkv": attn_w_qkv[0], "ffn_in0": ffn_w_in[0], "ffn_in1": ffn_w_in[1]}
    row_sharded = {"hgrn_out": hgrn_w_out[0], "attn_out": attn_w_out[0], "ffn_down0": ffn_w_down[0],
                   "ffn_down1": ffn_w_down[1]}
    gathering = {}
    for gi, (group, names) in enumerate(WEIGHT_GROUPS.items()):
        shards = [(col_sharded[n].T if n in col_sharded else row_sharded[n]).astype(BF16) for n in names]
        gathering[group] = _exchange_launch(shards, False, 1 + gi, f"weights_gather_{group}")

    def fetch(group):
        return {n: land[...].reshape(-1, D_MODEL) for n, land in zip(WEIGHT_GROUPS[group], gathering[group])}

    in_flight = {}

    def publish(group, grads):
        names = WEIGHT_GROUPS[group]
        parts = [grads[n].reshape(N_DEV, -1, D_MODEL) for n in names]
        in_flight[group] = _exchange_launch(parts, True, 1 + len(WEIGHT_GROUPS) + list(WEIGHT_GROUPS).index(group),
                                            f"grads_send_{group}")

    lb = _lower_bound(hgrn_lb_logits, "hgrn_lower_bound")
    grad_x, small = _local_step(x[0], loss_target[0], norm_mix, norm_ffn, lb, hgrn_out_norm,
                                final_norm.reshape(1, D_MODEL), fetch, publish)

    pad = jnp.zeros((1, D_MODEL - HGRN_DIM), F32)
    small_part = jnp.concatenate(
        [small["norm_mix0"], small["norm_mix1"], small["norm_ffn0"], small["norm_ffn1"], small["lb"],
         jnp.concatenate([small["out_gain"], pad], axis=1), small["final"], small["loss"]], axis=0)
    small_all = _gather_small(small_part, "small_grads_gather")
    received = {}
    for group in ("ffn1", "attn", "ffn0", "hgrn"):
        received.update(zip(WEIGHT_GROUPS[group], [land[...] for land in in_flight[group]]))

    masters = {"hgrn_in": (hgrn_w_in[0], m_hgrn_w_in[0], v_hgrn_w_in[0]),
               "hgrn_out": (hgrn_w_out[0], m_hgrn_w_out[0], v_hgrn_w_out[0]),
               "qkv": (attn_w_qkv[0], m_attn_w_qkv[0], v_attn_w_qkv[0]),
               "attn_out": (attn_w_out[0], m_attn_w_out[0], v_attn_w_out[0]),
               "ffn_in0": (ffn_w_in[0], m_ffn_w_in[0], v_ffn_w_in[0]),
               "ffn_in1": (ffn_w_in[1], m_ffn_w_in[1], v_ffn_w_in[1]),
               "ffn_down0": (ffn_w_down[0], m_ffn_w_down[0], v_ffn_w_down[0]),
               "ffn_down1": (ffn_w_down[1], m_ffn_w_down[1], v_ffn_w_down[1])}
    res = {}
    for n in WEIGHT_NAMES:
        g = _sum_blocks(received[n], f"{n}_grad_sum")
        if n in col_sharded:
            g = g.T
        wv, mv, vv = masters[n]
        res[n] = (g,) + tuple(_adamw(wv, g, mv, vv, f"{n}_adamw"))

    def single(n):
        return [t[None] for t in res[n]]

    def pair(n):
        return [jnp.stack([a, b]) for a, b in zip(res[n + "0"], res[n + "1"])]

    big = dict(hgrn_w_in=single("hgrn_in"), hgrn_w_out=single("hgrn_out"), attn_w_qkv=single("qkv"),
               attn_w_out=single("attn_out"), ffn_w_in=pair("ffn_in"), ffn_w_down=pair("ffn_down"))

    w_small = _pack_small(norm_mix, norm_ffn, hgrn_lb_logits, hgrn_out_norm, final_norm)
    m_small = _pack_small(m_norm_mix, m_norm_ffn, m_hgrn_lb_logits, m_hgrn_out_norm, m_final_norm)
    v_small = _pack_small(v_norm_mix, v_norm_ffn, v_hgrn_lb_logits, v_hgrn_out_norm, v_final_norm)
    g_s, d_s, m_s, v_s, loss = _small_update(small_all, w_small, m_small, v_small, "small_update")
    small_out = [_unpack_small(t) for t in (g_s, d_s, m_s, v_s)]

    def group(i):
        s = small_out[i]
        return (s[0], s[1], big["hgrn_w_in"][i], s[2], s[3], big["hgrn_w_out"][i], big["attn_w_qkv"][i],
                big["attn_w_out"][i], big["ffn_w_in"][i], big["ffn_w_down"][i], s[4])

    return (loss.reshape(()), grad_x[None], *group(0), *group(1), *group(2), *group(3))
```

```python
import functools

import jax
import jax.numpy as jnp
from jax import lax
from jax.experimental import pallas as pl
from jax.experimental.pallas import tpu as pltpu
from jax.experimental.pallas import tpu_sc as plsc

F32 = jnp.float32
BF16 = jnp.bfloat16

D_MODEL = 1024
N_DEV = 8
NORM_EPS = 1e-6

HGRN_HEADS = 8
HGRN_DIM = 128
HGRN_CHUNK = 64
HGRN_STEP_CHUNKS = 2
HGRN_EXP_CLAMP = 60.0

ATTN_DIM = 128
ATTN_BLOCK = 128
ATTN_GROUP_HEADS = 4
ATTN_GROUP_WIDTH = ATTN_GROUP_HEADS * ATTN_DIM
ATTN_DILATIONS = (1, 4, 16)
ATTN_WIDTH = 3 * ATTN_GROUP_WIDTH
ROPE_THETA = 10000.0
NEG_BIG = -1e30

D_FF = 2816

ADAM_LR = 0.001
ADAM_B1 = 0.9
ADAM_B2 = 0.999
ADAM_EPS = 1e-08
ADAM_WD = 0.01
ADAM_STEP = 10

VMEM_LIMIT = 48 * 1024 * 1024

NT = (((1,), (1,)), ((), ()))
NN = (((1,), (0,)), ((), ()))
TN = (((0,), (0,)), ((), ()))


def _dot(a, b, dims):
    return lax.dot_general(a, b, dims, preferred_element_type=F32)


def _params(*sem):
    return pltpu.CompilerParams(dimension_semantics=sem, vmem_limit_bytes=VMEM_LIMIT)


def _pick_tile(n, cap, mult):
    best = None
    for t in range(mult, min(n, cap) + 1, mult):
        if n % t == 0:
            best = t
    assert best is not None, (n, cap, mult)
    return best


def _sigmoid(x):
    return 1.0 / (1.0 + jnp.exp(-x))


ROW_TILE = 512
COL_CHUNK = 512
GRAD_TILE = 256


def _whole(shape, index_map):
    return pl.BlockSpec(shape, index_map, pipeline_mode=pl.Buffered(1))


def _part_specs(parts, n_cols):
    return [_whole((rows, n_cols), functools.partial(lambda i, b: (b, 0), b=blk)) for _, rows, blk in parts]


def _mm_nt(a, w_parts, *, out_dtype, name, rope=None):
    M, K = a.shape
    tm = _pick_tile(M, ROW_TILE, 16)
    widths = [rows for _, rows, _ in w_parts]
    n_parts = len(w_parts)

    def body(*refs):
        a_ref, w_refs, o_ref = refs[0], refs[1:1 + n_parts], refs[-1]
        av = a_ref[...]
        off = 0
        for p, w_ref in enumerate(w_refs):
            for c0 in range(0, widths[p], COL_CHUNK):
                cw = min(COL_CHUNK, widths[p] - c0)
                acc = _dot(av, w_ref[c0:c0 + cw, :], NT)
                if rope is not None and p < rope[2]:
                    cos, sin = refs[1 + n_parts][...], refs[2 + n_parts][...]
                    for h0 in range(0, cw, ATTN_DIM):
                        xh = acc[:, h0:h0 + ATTN_DIM]
                        rot = pltpu.roll(xh, ATTN_DIM // 2, 1)
                        o_ref[:, off + c0 + h0:off + c0 + h0 + ATTN_DIM] = (xh * cos + rot * sin).astype(out_dtype)
                else:
                    o_ref[:, off + c0:off + c0 + cw] = acc.astype(out_dtype)
            off += widths[p]

    in_specs = [pl.BlockSpec((tm, K), lambda i: (i, 0))] + _part_specs(w_parts, K)
    args = [a] + [w for w, _, _ in w_parts]
    if rope is not None:
        in_specs += [pl.BlockSpec((tm, ATTN_DIM), lambda i: (i, 0))] * 2
        args += [rope[0], rope[1]]
    return pl.pallas_call(
        body, out_shape=jax.ShapeDtypeStruct((M, sum(widths)), out_dtype), grid=(M // tm,),
        in_specs=in_specs, out_specs=pl.BlockSpec((tm, sum(widths)), lambda i: (i, 0)),
        compiler_params=_params("parallel"), name=name)(*args)


def _mm_nn(a_list, w_parts_list, resid, *, name, norm=None):
    M = a_list[0].shape[0]
    tm = _pick_tile(M, ROW_TILE, 16)
    n_a = len(a_list)
    flat_parts = [p for parts in w_parts_list for p in parts]
    n_in = n_a + len(flat_parts) + (1 if resid is not None else 0) + (2 if norm is not None else 0)

    def body(*refs):
        a_refs, w_refs = refs[:n_a], refs[n_a:n_a + len(flat_parts)]
        acc = None
        wi = 0
        for a_ref, parts in zip(a_refs, w_parts_list):
            off = 0
            for _, rows, _ in parts:
                term = _dot(a_ref[:, off:off + rows], w_refs[wi][...], NN)
                acc = term if acc is None else acc + term
                off += rows
                wi += 1
        if norm is None:
            if resid is not None:
                acc = acc + refs[n_in - 1][...]
            refs[n_in][...] = acc
            return
        dres_ref, x_ref, g_ref = refs[n_in - 3:n_in]
        dx_ref, dxb_ref, dg_ref = refs[n_in:n_in + 3]

        @pl.when(pl.program_id(0) == 0)
        def _():
            dg_ref[...] = jnp.zeros_like(dg_ref)

        xv = x_ref[...]
        rstd = lax.rsqrt(jnp.mean(xv * xv, axis=-1, keepdims=True) + NORM_EPS)
        n = xv * rstd
        dg_ref[...] += jnp.sum(acc * n, axis=0, keepdims=True)
        dn = acc * g_ref[...]
        dx = dres_ref[...] + rstd * (dn - n * jnp.mean(dn * n, axis=-1, keepdims=True))
        dx_ref[...] = dx
        dxb_ref[...] = dx.astype(BF16)

    row = pl.BlockSpec((tm, D_MODEL), lambda i: (i, 0))
    vec = pl.BlockSpec((1, D_MODEL), lambda i: (0, 0))
    in_specs = [pl.BlockSpec((tm, a.shape[1]), lambda i: (i, 0)) for a in a_list] + _part_specs(flat_parts, D_MODEL)
    args = list(a_list) + [w for w, _, _ in flat_parts]
    if resid is not None:
        in_specs.append(row)
        args.append(resid)
    if norm is None:
        return pl.pallas_call(
            body, out_shape=jax.ShapeDtypeStruct((M, D_MODEL), F32), grid=(M // tm,),
            in_specs=in_specs, out_specs=row, compiler_params=_params("parallel"), name=name)(*args)
    assert resid is not None
    return pl.pallas_call(
        body,
        out_shape=(jax.ShapeDtypeStruct((M, D_MODEL), F32), jax.ShapeDtypeStruct((M, D_MODEL), BF16),
                   jax.ShapeDtypeStruct((1, D_MODEL), F32)),
        grid=(M // tm,), in_specs=in_specs + [row, vec], out_specs=(row, row, vec),
        compiler_params=_params("arbitrary"), name=name)(*args, norm[0], norm[1])


def _mm_tn(a, b, *, name, into=None, row_tile=0, rows=None):
    T, R = a.shape
    N = b.shape[1]
    tr = GRAD_TILE
    rows = R if rows is None else rows

    def body(a_ref, b_ref, *refs):
        refs[-1][...] = _dot(a_ref[...], b_ref[...], TN).astype(BF16)

    in_specs = [pl.BlockSpec((T, tr), lambda r: (0, r)), _whole((T, N), lambda r: (0, 0))]
    args = [a, b]
    if into is not None:
        in_specs.append(HBM_SPEC)
        args.append(into)
    return pl.pallas_call(
        body, out_shape=jax.ShapeDtypeStruct((rows, N), BF16), grid=(R // tr,),
        in_specs=in_specs, out_specs=pl.BlockSpec((tr, N), lambda r: (row_tile + r, 0)),
        input_output_aliases={} if into is None else {2: 0},
        compiler_params=_params("parallel"), name=name)(*args)


def _rms_fwd(x, gain, name):
    T = x.shape[0]
    tm = _pick_tile(T, 512, 16)

    def body(x_ref, g_ref, u_ref):
        xv = x_ref[...]
        rstd = lax.rsqrt(jnp.mean(xv * xv, axis=-1, keepdims=True) + NORM_EPS)
        u_ref[...] = (xv * rstd * g_ref[...]).astype(BF16)

    return pl.pallas_call(
        body, out_shape=jax.ShapeDtypeStruct((T, D_MODEL), BF16), grid=(T // tm,),
        in_specs=[pl.BlockSpec((tm, D_MODEL), lambda i: (i, 0)), pl.BlockSpec((1, D_MODEL), lambda i: (0, 0))],
        out_specs=pl.BlockSpec((tm, D_MODEL), lambda i: (i, 0)),
        compiler_params=_params("parallel"), name=name)(x, gain)


def _rms_bwd(x, gain, dus, dres, name, dilations=(1,)):
    T = x.shape[0]
    tm = _pick_tile(T, PERM_TILE, 16 * max(dilations))
    n_du = len(dus)

    def body(x_ref, g_ref, *refs):
        du_refs, dres_ref = refs[:n_du], refs[n_du]
        dx_ref, dxb_ref, dg_ref, du_scr = refs[n_du + 1:]

        @pl.when(pl.program_id(0) == 0)
        def _():
            dg_ref[...] = jnp.zeros_like(dg_ref)

        if tuple(dilations) == (1,):
            du = du_refs[0][...]
        else:
            for i, (d, du_ref) in enumerate(zip(dilations, du_refs)):
                for j in range(D_MODEL // LANES):
                    lanes = slice(j * LANES, (j + 1) * LANES)
                    if d == 1:
                        du_scr[j] = du_ref[:, lanes] if i == 0 else du_scr[j] + du_ref[:, lanes]
                        continue
                    blk = du_scr.at[j]
                    for r in range(d):
                        rows = _class_rows(r, d, tm)
                        blk[rows, :] = du_ref[r, :, lanes] if i == 0 else blk[rows, :] + du_ref[r, :, lanes]
            du = jnp.concatenate([du_scr[j] for j in range(D_MODEL // LANES)], axis=1)
        xv = x_ref[...]
        rstd = lax.rsqrt(jnp.mean(xv * xv, axis=-1, keepdims=True) + NORM_EPS)
        n = xv * rstd
        dg_ref[...] += jnp.sum(du * n, axis=0, keepdims=True)
        dn = du * g_ref[...]
        dx = dres_ref[...] + rstd * (dn - n * jnp.mean(dn * n, axis=-1, keepdims=True))
        dx_ref[...] = dx
        dxb_ref[...] = dx.astype(BF16)

    row = pl.BlockSpec((tm, D_MODEL), lambda i: (i, 0))
    vec = pl.BlockSpec((1, D_MODEL), lambda i: (0, 0))
    return pl.pallas_call(
        body,
        out_shape=(jax.ShapeDtypeStruct((T, D_MODEL), F32), jax.ShapeDtypeStruct((T, D_MODEL), BF16),
                   jax.ShapeDtypeStruct((1, D_MODEL), F32)),
        grid=(T // tm,), in_specs=[row, vec] + [_residue_spec(d, tm, D_MODEL) for d in dilations] + [row],
        out_specs=(row, row, vec), scratch_shapes=[pltpu.VMEM((D_MODEL // LANES, tm, LANES), F32)],
        compiler_params=_params("arbitrary"), name=name)(
            x, gain, *[_residue_view(du, d) for du, d in zip(dus, dilations)], dres)


def _loss_head(h, target, gain, name):
    T = h.shape[0]
    tm = _pick_tile(T, 512, 16)
    inv_f = 1.0 / D_MODEL

    def body(h_ref, t_ref, g_ref, dh_ref, dhb_ref, dg_ref, loss_ref):
        @pl.when(pl.program_id(0) == 0)
        def _():
            dg_ref[...] = jnp.zeros_like(dg_ref)
            loss_ref[...] = jnp.zeros_like(loss_ref)

        hv = h_ref[...]
        g = g_ref[...]
        rstd = lax.rsqrt(jnp.mean(hv * hv, axis=-1, keepdims=True) + NORM_EPS)
        n = hv * rstd
        err = n * g - t_ref[...]
        loss_ref[...] += (0.5 * inv_f) * jnp.sum(err * err, axis=0, keepdims=True)
        dy = err * inv_f
        dg_ref[...] += jnp.sum(dy * n, axis=0, keepdims=True)
        dn = dy * g
        dh = rstd * (dn - n * jnp.mean(dn * n, axis=-1, keepdims=True))
        dh_ref[...] = dh
        dhb_ref[...] = dh.astype(BF16)

    row = pl.BlockSpec((tm, D_MODEL), lambda i: (i, 0))
    vec = pl.BlockSpec((1, D_MODEL), lambda i: (0, 0))
    return pl.pallas_call(
        body,
        out_shape=(jax.ShapeDtypeStruct((T, D_MODEL), F32), jax.ShapeDtypeStruct((T, D_MODEL), BF16),
                   jax.ShapeDtypeStruct((1, D_MODEL), F32), jax.ShapeDtypeStruct((1, D_MODEL), F32)),
        grid=(T // tm,), in_specs=[row, row, vec], out_specs=(row, row, vec, vec),
        compiler_params=_params("arbitrary"), name=name)(h, target, gain)


FFN_TILE = 256


def _ffn_in(h, gain, w_in, name):
    T = h.shape[0]
    tm = _pick_tile(T, ROW_TILE, 16)

    def body(h_ref, g_ref, w_ref, n_ref, gate_ref, up_ref, a_ref):
        hv = h_ref[...]
        rstd = lax.rsqrt(jnp.mean(hv * hv, axis=-1, keepdims=True) + NORM_EPS)
        n = (hv * rstd * g_ref[...]).astype(BF16)
        n_ref[...] = n
        for c0 in range(0, D_FF, FFN_TILE):
            cols = slice(c0, c0 + FFN_TILE)
            gate = _dot(n, w_ref[c0:c0 + FFN_TILE, :], NT)
            up = _dot(n, w_ref[D_FF + c0:D_FF + c0 + FFN_TILE, :], NT)
            gate_ref[:, cols] = gate.astype(BF16)
            up_ref[:, cols] = up.astype(BF16)
            a_ref[:, cols] = (gate * _sigmoid(gate) * up).astype(BF16)

    row = pl.BlockSpec((tm, D_MODEL), lambda i: (i, 0))
    wide = pl.BlockSpec((tm, D_FF), lambda i: (i, 0))
    wide_shape = jax.ShapeDtypeStruct((T, D_FF), BF16)
    return pl.pallas_call(
        body, out_shape=(jax.ShapeDtypeStruct((T, D_MODEL), BF16), wide_shape, wide_shape, wide_shape),
        grid=(T // tm,),
        in_specs=[row, pl.BlockSpec((1, D_MODEL), lambda i: (0, 0)), _whole((2 * D_FF, D_MODEL), lambda i: (0, 0))],
        out_specs=(row, wide, wide, wide), compiler_params=_params("parallel"), name=name)(h, gain, w_in)


def _ffn_down_dx(dhb, w_down, gate, up, name):
    T = dhb.shape[0]
    tm = _pick_tile(T, ROW_TILE, 16)

    def body(dh_ref, w_ref, gate_ref, up_ref, dgate_ref, dup_ref):
        dh = dh_ref[...]
        for c0 in range(0, D_FF, FFN_TILE):
            cols = slice(c0, c0 + FFN_TILE)
            da = _dot(dh, w_ref[c0:c0 + FFN_TILE, :], NT)
            gate = gate_ref[:, cols].astype(F32)
            sg = _sigmoid(gate)
            dgate_ref[:, cols] = (da * up_ref[:, cols].astype(F32) * (sg * (1.0 + gate * (1.0 - sg)))).astype(BF16)
            dup_ref[:, cols] = (da * gate * sg).astype(BF16)

    wide = pl.BlockSpec((tm, D_FF), lambda i: (i, 0))
    wide_shape = jax.ShapeDtypeStruct((T, D_FF), BF16)
    return pl.pallas_call(
        body, out_shape=(wide_shape, wide_shape), grid=(T // tm,),
        in_specs=[pl.BlockSpec((tm, D_MODEL), lambda i: (i, 0)), _whole((D_FF, D_MODEL), lambda i: (0, 0)), wide, wide],
        out_specs=(wide, wide), compiler_params=_params("parallel"), name=name)(dhb, w_down, gate, up)


def _tri(n, lower):
    r = lax.broadcasted_iota(jnp.int32, (n, n), 0)
    c = lax.broadcasted_iota(jnp.int32, (n, n), 1)
    return (c <= r) if lower else (c >= r)


def _running_sum(x, lower):
    n = x.shape[0]
    tri = _tri(n, lower).astype(F32)
    return lax.dot_general(tri, x, NN, precision=lax.Precision.HIGHEST, preferred_element_type=F32)


def _hgrn_gates(q_raw, f_raw, lb):
    C = q_raw.shape[0]
    sig_f = _sigmoid(f_raw)
    forget = lb + (1.0 - lb) * sig_f
    key = 1.0 - forget
    log_f = jnp.log(forget)
    b = _running_sum(log_f, True)
    first_half = lax.broadcasted_iota(jnp.int32, log_f.shape, 0) < C // 2
    r = jnp.sum(jnp.where(first_half, log_f, 0.0), axis=0, keepdims=True)
    b_last = jnp.sum(log_f, axis=0, keepdims=True)
    e_a = jnp.exp(jnp.minimum(b - r, HGRN_EXP_CLAMP))
    e_b = jnp.exp(jnp.minimum(r - b, HGRN_EXP_CLAMP))
    e_q = jnp.exp(b)
    e_k = jnp.exp(b_last - b)
    sig_q = _sigmoid(q_raw)
    query = q_raw * sig_q
    return dict(sig_f=sig_f, forget=forget, sig_q=sig_q, e_a=e_a, e_b=e_b, e_q=e_q, e_k=e_k,
                e_last=jnp.exp(b_last), q_a=query * e_a, k_b=key * e_b, q_hat=query * e_q, k_til=key * e_k)


def _hgrn_fwd(proj, lb, gain, name):
    T = proj.shape[0]
    C = HGRN_CHUNK
    CPS = HGRN_STEP_CHUNKS
    H, HD = HGRN_HEADS, HGRN_DIM

    def body(q_ref, f_ref, i_ref, g_ref, lb_ref, gain_ref, og_ref, o_ref, st_ref, s_scr):
        @pl.when(pl.program_id(0) == 0)
        def _():
            s_scr[...] = jnp.zeros_like(s_scr)

        causal = _tri(C, True)
        gain_v = gain_ref[...]
        heads = [slice(h * HD, (h + 1) * HD) for h in range(H)]
        s_t = [s_scr[h] for h in range(H)]
        for cc in range(CPS):
            rows = slice(cc * C, (cc + 1) * C)
            for h in range(H):
                st_ref[cc, h] = s_t[h]
            gt = _hgrn_gates(q_ref[rows, :], f_ref[rows, :], lb_ref[...])
            q_a, k_b = gt["q_a"].astype(BF16), gt["k_b"].astype(BF16)
            q_hat, k_til = gt["q_hat"].astype(BF16), gt["k_til"].astype(BF16)
            v = i_ref[rows, :].astype(BF16)
            p = [jnp.where(causal, _dot(q_a[:, sl], k_b[:, sl], NT), 0.0).astype(BF16) for sl in heads]
            o = [_dot(p[h], v[:, sl], NN) + _dot(q_hat[:, sl], s_t[h].astype(BF16), NT)
                 for h, sl in enumerate(heads)]
            s_t = [gt["e_last"][:, sl] * s_t[h] + _dot(v[:, sl], k_til[:, sl], TN) for h, sl in enumerate(heads)]
            for h, sl in enumerate(heads):
                o_ref[rows, sl] = o[h]
                rstd = lax.rsqrt(jnp.mean(o[h] * o[h], axis=-1, keepdims=True) + NORM_EPS)
                g_raw = g_ref[rows, sl]
                og_ref[rows, sl] = (o[h] * rstd * gain_v * (g_raw * _sigmoid(g_raw))).astype(BF16)
        for h in range(H):
            s_scr[h] = s_t[h]

    col = lambda j: pl.BlockSpec((CPS * C, D_MODEL), lambda c: (c, j))
    row = pl.BlockSpec((CPS * C, D_MODEL), lambda c: (c, 0))
    return pl.pallas_call(
        body,
        out_shape=(jax.ShapeDtypeStruct((T, D_MODEL), BF16), jax.ShapeDtypeStruct((T, D_MODEL), F32),
                   jax.ShapeDtypeStruct((T // C, H, HD, HD), F32)),
        grid=(T // (CPS * C),),
        in_specs=[col(0), col(1), col(2), col(3), pl.BlockSpec((1, D_MODEL), lambda c: (0, 0)),
                  pl.BlockSpec((1, HD), lambda c: (0, 0))],
        out_specs=(row, row, pl.BlockSpec((CPS, H, HD, HD), lambda c: (c, 0, 0, 0))),
        scratch_shapes=[pltpu.VMEM((H, HD, HD), F32)],
        compiler_params=_params("arbitrary"), name=name)(proj, proj, proj, proj, lb, gain)


def _hgrn_bwd(proj, o_pre, d_og, states, lb, gain, name):
    T = proj.shape[0]
    C = HGRN_CHUNK
    CPS = HGRN_STEP_CHUNKS
    H, HD = HGRN_HEADS, HGRN_DIM
    NC = T // (CPS * C)

    def body(q_ref, f_ref, i_ref, g_ref, o_ref, dog_ref, st_ref, lb_ref, gain_ref,
             dproj_ref, dlb_ref, dgain_ref, ds_scr, dq_all, dk_all, db_all):
        @pl.when(pl.program_id(0) == 0)
        def _():
            ds_scr[...] = jnp.zeros_like(ds_scr)
            dlb_ref[...] = jnp.zeros_like(dlb_ref)
            dgain_ref[...] = jnp.zeros_like(dgain_ref)

        lbv = lb_ref[...]
        causal = _tri(C, True)
        last_row = lax.broadcasted_iota(jnp.int32, (C, HD), 0) == C - 1
        gain_v = gain_ref[...]
        heads = [slice(h * HD, (h + 1) * HD) for h in range(H)]
        hs = range(H)
        ds_t = [ds_scr[h] for h in hs]
        dgain = None
        for cc in reversed(range(CPS)):
            rows = slice(cc * C, (cc + 1) * C)
            dq_scr, dk_scr, db_scr = dq_all.at[cc], dk_all.at[cc], db_all.at[cc]
            q_raw = q_ref[rows, :]
            gt = _hgrn_gates(q_raw, f_ref[rows, :], lbv)
            o = [o_ref[rows, sl] for sl in heads]
            rstd = [lax.rsqrt(jnp.mean(x * x, axis=-1, keepdims=True) + NORM_EPS) for x in o]
            n = [x * r for x, r in zip(o, rstd)]
            g_raw = [g_ref[rows, sl] for sl in heads]
            sg = [_sigmoid(x) for x in g_raw]
            d_out = [dog_ref[rows, sl] for sl in heads]
            dy = [d * (g * s) for d, g, s in zip(d_out, g_raw, sg)]
            dn = [x * gain_v for x in dy]
            do = [(rstd[h] * (dn[h] - n[h] * jnp.mean(dn[h] * n[h], axis=-1, keepdims=True))).astype(BF16) for h in hs]
            for h in hs:
                dgain = dy[h] * n[h] if dgain is None else dgain + dy[h] * n[h]
            for h, sl in enumerate(heads):
                dproj_ref[rows, 3 * D_MODEL + h * HD:3 * D_MODEL + (h + 1) * HD] = (
                    d_out[h] * n[h] * gain_v * (sg[h] * (1.0 + g_raw[h] * (1.0 - sg[h])))).astype(BF16)
            q_ab, k_bb = gt["q_a"].astype(BF16), gt["k_b"].astype(BF16)
            q_hb, k_tb = gt["q_hat"].astype(BF16), gt["k_til"].astype(BF16)
            v = i_ref[rows, :].astype(BF16)
            s_t = [st_ref[cc, h] for h in hs]
            ds_b = [x.astype(BF16) for x in ds_t]
            p = [jnp.where(causal, _dot(q_ab[:, sl], k_bb[:, sl], NT), 0.0).astype(BF16) for sl in heads]
            dp = [jnp.where(causal, _dot(do[h], v[:, sl], NT), 0.0).astype(BF16) for h, sl in enumerate(heads)]
            dv = [_dot(p[h], do[h], TN) + _dot(k_tb[:, sl], ds_b[h], NT) for h, sl in enumerate(heads)]
            dq_a = [_dot(dp[h], k_bb[:, sl], NN) for h, sl in enumerate(heads)]
            dk_b = [_dot(dp[h], q_ab[:, sl], TN) for h, sl in enumerate(heads)]
            dq_hat = [_dot(do[h], s_t[h].astype(BF16), NN) for h in hs]
            dk_til = [_dot(v[:, sl], ds_b[h], NN) for h, sl in enumerate(heads)]
            ds_new = [_dot(do[h], q_hb[:, sl], TN) + gt["e_last"][:, sl] * ds_t[h] for h, sl in enumerate(heads)]
            for h, sl in enumerate(heads):
                k_til = gt["k_til"][:, sl]
                db_last = jnp.sum(ds_t[h] * gt["e_last"][:, sl] * s_t[h], axis=0, keepdims=True) + jnp.sum(
                    dk_til[h] * k_til, axis=0, keepdims=True)
                dproj_ref[rows, 2 * D_MODEL + h * HD:2 * D_MODEL + (h + 1) * HD] = dv[h].astype(BF16)
                dq_scr[:, sl] = dq_a[h] * gt["e_a"][:, sl] + dq_hat[h] * gt["e_q"][:, sl]
                dk_scr[:, sl] = dk_b[h] * gt["e_b"][:, sl] + dk_til[h] * gt["e_k"][:, sl]
                db = (dq_a[h] * q_ab[:, sl].astype(F32) + dq_hat[h] * gt["q_hat"][:, sl]
                      - dk_b[h] * k_bb[:, sl].astype(F32) - dk_til[h] * k_til)
                db_scr[:, sl] = db + jnp.where(last_row, db_last, 0.0)
            dlogf = _running_sum(db_scr[...], False)
            sig_f, forget, sig_q = gt["sig_f"], gt["forget"], gt["sig_q"]
            dforget = dlogf / forget - dk_scr[...]
            dproj_ref[rows, D_MODEL:2 * D_MODEL] = (dforget * (1.0 - lbv) * sig_f * (1.0 - sig_f)).astype(BF16)
            dlb_ref[...] += jnp.sum(dforget * (1.0 - sig_f), axis=0, keepdims=True)
            dproj_ref[rows, 0:D_MODEL] = (dq_scr[...] * (sig_q * (1.0 + q_raw * (1.0 - sig_q)))).astype(BF16)
            ds_t = ds_new
        dgain_ref[...] += jnp.sum(dgain, axis=0, keepdims=True)
        for h in hs:
            ds_scr[h] = ds_t[h]

    col = lambda j: pl.BlockSpec((CPS * C, D_MODEL), lambda c: (NC - 1 - c, j))
    row = pl.BlockSpec((CPS * C, D_MODEL), lambda c: (NC - 1 - c, 0))
    return pl.pallas_call(
        body,
        out_shape=(jax.ShapeDtypeStruct((T, 4 * D_MODEL), BF16), jax.ShapeDtypeStruct((1, D_MODEL), F32),
                   jax.ShapeDtypeStruct((1, HD), F32)),
        grid=(NC,),
        in_specs=[col(0), col(1), col(2), col(3), row, row,
                  pl.BlockSpec((CPS, H, HD, HD), lambda c: (NC - 1 - c, 0, 0, 0)),
                  pl.BlockSpec((1, D_MODEL), lambda c: (0, 0)), pl.BlockSpec((1, HD), lambda c: (0, 0))],
        out_specs=(pl.BlockSpec((CPS * C, 4 * D_MODEL), lambda c: (NC - 1 - c, 0)),
                   pl.BlockSpec((1, D_MODEL), lambda c: (0, 0)), pl.BlockSpec((1, HD), lambda c: (0, 0))),
        scratch_shapes=[pltpu.VMEM((H, HD, HD), F32)] + [pltpu.VMEM((CPS, C, D_MODEL), F32)] * 3,
        compiler_params=_params("arbitrary"), name=name)(proj, proj, proj, proj, o_pre, d_og, states, lb, gain)


def _attn_masks():
    r = lax.broadcasted_iota(jnp.int32, (ATTN_BLOCK, ATTN_BLOCK), 0)
    c = lax.broadcasted_iota(jnp.int32, (ATTN_BLOCK, ATTN_BLOCK), 1)
    return c >= r, c <= r


def _attn_fwd(qkv, dilation, name):
    T = qkv.shape[0]
    nb = T // dilation // ATTN_BLOCK
    W = ATTN_GROUP_WIDTH
    scale = ATTN_DIM ** -0.5

    def body(q_ref, kp_ref, kc_ref, vp_ref, vc_ref, o_ref, lse_ref):
        no_prev = jnp.where(pl.program_id(1) > 0, 0.0, NEG_BIG)
        m_prev, m_cur = _attn_masks()
        ones = jnp.ones((ATTN_BLOCK, ATTN_DIM), BF16)
        heads = [slice(h * ATTN_DIM, (h + 1) * ATTN_DIM) for h in range(ATTN_GROUP_HEADS)]
        s_p = [jnp.where(m_prev, _dot(q_ref[:, sl], kp_ref[:, sl], NT) * scale + no_prev, NEG_BIG) for sl in heads]
        s_c = [jnp.where(m_cur, _dot(q_ref[:, sl], kc_ref[:, sl], NT) * scale, NEG_BIG) for sl in heads]
        m = [jnp.max(jnp.maximum(a, b), axis=-1, keepdims=True) for a, b in zip(s_p, s_c)]
        p_p = [jnp.exp(a - mx).astype(BF16) for a, mx in zip(s_p, m)]
        p_c = [jnp.exp(b - mx).astype(BF16) for b, mx in zip(s_c, m)]
        l = [_dot(a, ones, NN) + _dot(b, ones, NN) for a, b in zip(p_p, p_c)]
        acc = [_dot(a, vp_ref[:, sl], NN) + _dot(b, vc_ref[:, sl], NN) for a, b, sl in zip(p_p, p_c, heads)]
        for sl, a, lv, mx in zip(heads, acc, l, m):
            o_ref[:, sl] = a / lv
            lse_ref[:, sl] = mx + jnp.log(lv)

    blk = lambda col, prev: pl.BlockSpec(
        (ATTN_BLOCK, W), lambda s, n: (s * nb + (jnp.maximum(n - 1, 0) if prev else n), col))
    out = pl.BlockSpec((ATTN_BLOCK, W), lambda s, n: (s * nb + n, 0))
    return pl.pallas_call(
        body, out_shape=(jax.ShapeDtypeStruct((T, W), F32),) * 2, grid=(dilation, nb),
        in_specs=[blk(0, False), blk(1, True), blk(1, False), blk(2, True), blk(2, False)],
        out_specs=(out, out), compiler_params=_params("parallel", "arbitrary"), name=name)(qkv, qkv, qkv, qkv, qkv)


def _attn_bwd(qkv, d_out, lse, delta, cos, sin, dilation, name):
    T = qkv.shape[0]
    nb = T // dilation // ATTN_BLOCK
    W = ATTN_GROUP_WIDTH
    scale = ATTN_DIM ** -0.5

    def unrope(x, cos_v, sin_v):
        return x * cos_v + pltpu.roll(x * sin_v, ATTN_DIM // 2, 1)

    def body(q_ref, kp_ref, kc_ref, vp_ref, vc_ref, do_ref, lse_ref, dl_ref, cos_ref, sin_ref,
             out_ref, dq_scr, dk_scr, dv_scr):
        n = pl.program_id(1)
        cos_v, sin_v = cos_ref[...], sin_ref[...]

        @pl.when(n > 0)
        def _():
            for h in range(ATTN_GROUP_HEADS):
                sl = slice(h * ATTN_DIM, (h + 1) * ATTN_DIM)
                out_ref[:, sl] = unrope(dq_scr[:, sl], cos_v, sin_v).astype(BF16)

        @pl.when(n == nb)
        def _():
            for h in range(ATTN_GROUP_HEADS):
                sl = slice(h * ATTN_DIM, (h + 1) * ATTN_DIM)
                out_ref[:, W + h * ATTN_DIM:W + (h + 1) * ATTN_DIM] = unrope(dk_scr[:, sl], cos_v, sin_v).astype(BF16)
                out_ref[:, 2 * W + h * ATTN_DIM:2 * W + (h + 1) * ATTN_DIM] = dv_scr[:, sl].astype(BF16)

        @pl.when(n == 0)
        def _():
            dk_scr[...] = jnp.zeros_like(dk_scr)
            dv_scr[...] = jnp.zeros_like(dv_scr)

        @pl.when(n < nb)
        def _():
            has_prev = n > 0
            no_prev = jnp.where(has_prev, 0.0, NEG_BIG)
            m_prev, m_cur = _attn_masks()
            heads = [slice(h * ATTN_DIM, (h + 1) * ATTN_DIM) for h in range(ATTN_GROUP_HEADS)]
            s_p = [_dot(q_ref[:, sl], kp_ref[:, sl], NT) for sl in heads]
            s_c = [_dot(q_ref[:, sl], kc_ref[:, sl], NT) for sl in heads]
            dp_p = [_dot(do_ref[:, sl], vp_ref[:, sl], NT) for sl in heads]
            dp_c = [_dot(do_ref[:, sl], vc_ref[:, sl], NT) for sl in heads]
            p_p = [jnp.where(m_prev, jnp.exp(s * scale - lse_ref[:, sl] + no_prev), 0.0) for s, sl in zip(s_p, heads)]
            p_c = [jnp.where(m_cur, jnp.exp(s * scale - lse_ref[:, sl]), 0.0) for s, sl in zip(s_c, heads)]
            ds_p = [(p * (dp - dl_ref[:, sl]) * scale).astype(BF16) for p, dp, sl in zip(p_p, dp_p, heads)]
            ds_c = [(p * (dp - dl_ref[:, sl]) * scale).astype(BF16) for p, dp, sl in zip(p_c, dp_c, heads)]
            p_p = [p.astype(BF16) for p in p_p]
            p_c = [p.astype(BF16) for p in p_c]
            dk_prev = [dk_scr[:, sl] + _dot(ds, q_ref[:, sl], TN) for ds, sl in zip(ds_p, heads)]
            dv_prev = [dv_scr[:, sl] + _dot(p, do_ref[:, sl], TN) for p, sl in zip(p_p, heads)]
            dq = [_dot(a, kp_ref[:, sl], NN) + _dot(b, kc_ref[:, sl], NN) for a, b, sl in zip(ds_p, ds_c, heads)]
            dk_cur = [_dot(ds, q_ref[:, sl], TN) for ds, sl in zip(ds_c, heads)]
            dv_cur = [_dot(p, do_ref[:, sl], TN) for p, sl in zip(p_c, heads)]
            for h, sl in enumerate(heads):
                out_ref[:, W + h * ATTN_DIM:W + (h + 1) * ATTN_DIM] = unrope(dk_prev[h], cos_v, sin_v).astype(BF16)
                out_ref[:, 2 * W + h * ATTN_DIM:2 * W + (h + 1) * ATTN_DIM] = dv_prev[h].astype(BF16)
                dq_scr[:, sl] = dq[h]
                dk_scr[:, sl] = dk_cur[h]
                dv_scr[:, sl] = dv_cur[h]

    def cur(n):
        return jnp.minimum(n, nb - 1)

    def late(n):
        return jnp.maximum(n - 1, 0)

    qkv_blk = lambda col, prev: pl.BlockSpec(
        (ATTN_BLOCK, W), lambda s, n: (s * nb + (jnp.maximum(cur(n) - 1, 0) if prev else cur(n)), col))
    row = pl.BlockSpec((ATTN_BLOCK, W), lambda s, n: (s * nb + cur(n), 0))
    tab = pl.BlockSpec((ATTN_BLOCK, ATTN_DIM), lambda s, n: (s * nb + late(n), 0))
    return pl.pallas_call(
        body, out_shape=jax.ShapeDtypeStruct((T, 3 * W), BF16), grid=(dilation, nb + 1),
        in_specs=[qkv_blk(0, False), qkv_blk(1, True), qkv_blk(1, False), qkv_blk(2, True), qkv_blk(2, False),
                  row, row, row, tab, tab],
        out_specs=pl.BlockSpec((ATTN_BLOCK, 3 * W), lambda s, n: (s * nb + late(n), 0)),
        scratch_shapes=[pltpu.VMEM((ATTN_BLOCK, W), F32)] * 3,
        compiler_params=_params("parallel", "arbitrary"), name=name)(
            qkv, qkv, qkv, qkv, qkv, d_out, lse, delta, cos, sin)


PERM_TILE = 512
LANES = 128


def _residue_view(x, d):
    return x if d == 1 else x.reshape(d, x.shape[0] // d, x.shape[1])


def _residue_spec(d, tm, cols):
    if d == 1:
        return pl.BlockSpec((tm, cols), lambda i: (i, 0))
    return pl.BlockSpec((d, tm // d, cols), lambda i: (0, i, 0))


def _residue_shape(T, d, cols, dtype):
    return jax.ShapeDtypeStruct((T, cols) if d == 1 else (d, T // d, cols), dtype)


def _class_rows(r, d, tm):
    return pl.ds(r, tm // d, stride=d)


def _attn_norm(h, gain, cos, sin, name):
    T = h.shape[0]
    tm = _pick_tile(T, PERM_TILE, 16 * max(ATTN_DILATIONS))
    dils = ATTN_DILATIONS

    def body(h_ref, g_ref, cos_ref, sin_ref, *refs):
        u_refs, c_refs, s_refs, u_scr = refs[0:3], refs[3:6], refs[6:9], refs[9]
        hv = h_ref[...]
        rstd = lax.rsqrt(jnp.mean(hv * hv, axis=-1, keepdims=True) + NORM_EPS)
        u = hv * rstd * g_ref[...]
        for j in range(D_MODEL // LANES):
            u_scr[j] = u[:, j * LANES:(j + 1) * LANES]
        for d, u_ref, c_ref, s_ref in zip(dils, u_refs, c_refs, s_refs):
            if d == 1:
                u_ref[...] = u.astype(BF16)
                c_ref[...] = cos_ref[...]
                s_ref[...] = sin_ref[...]
                continue
            for r in range(d):
                rows = _class_rows(r, d, tm)
                for j in range(D_MODEL // LANES):
                    u_ref[r, :, j * LANES:(j + 1) * LANES] = u_scr.at[j][rows, :].astype(BF16)
                c_ref[r] = cos_ref[rows, :]
                s_ref[r] = sin_ref[rows, :]

    row = pl.BlockSpec((tm, D_MODEL), lambda i: (i, 0))
    tab = pl.BlockSpec((tm, ATTN_DIM), lambda i: (i, 0))
    res = pl.pallas_call(
        body,
        out_shape=([_residue_shape(T, d, D_MODEL, BF16) for d in dils]
                   + [_residue_shape(T, d, ATTN_DIM, F32) for d in dils] * 2),
        grid=(T // tm,), in_specs=[row, pl.BlockSpec((1, D_MODEL), lambda i: (0, 0)), tab, tab],
        out_specs=([_residue_spec(d, tm, D_MODEL) for d in dils] + [_residue_spec(d, tm, ATTN_DIM) for d in dils] * 2),
        scratch_shapes=[pltpu.VMEM((D_MODEL // LANES, tm, LANES), F32)],
        compiler_params=_params("parallel"), name=name)(h, gain, cos, sin)
    flat = [r.reshape(T, r.shape[-1]) for r in res]
    return flat[0:3], flat[3:6], flat[6:9]


def _attn_merge_fwd(outs, lses, name):
    T = outs[0].shape[0]
    W = ATTN_GROUP_WIDTH
    tm = _pick_tile(T, PERM_TILE, 16 * max(ATTN_DILATIONS))
    dils = ATTN_DILATIONS

    def body(*refs):
        o_refs, l_refs, oc_ref, lse_refs = refs[0:3], refs[3:6], refs[6], refs[7:10]
        o_scr, l_scr, t_scr = refs[10:13]
        nh = ATTN_GROUP_HEADS
        for g, d in enumerate(dils):
            for j in range(nh):
                lanes = slice(j * LANES, (j + 1) * LANES)
                if d == 1:
                    o_scr[g * nh + j] = o_refs[g][:, lanes]
                    l_scr[g * nh + j] = l_refs[g][:, lanes]
                    continue
                for r in range(d):
                    rows = _class_rows(r, d, tm)
                    o_scr.at[g * nh + j][rows, :] = o_refs[g][r, :, lanes]
                    l_scr.at[g * nh + j][rows, :] = l_refs[g][r, :, lanes]
        for j in range(nh):
            lanes = slice(j * LANES, (j + 1) * LANES)
            ls = [l_scr[g * nh + j] for g in range(3)]
            m = jnp.maximum(jnp.maximum(ls[0], ls[1]), ls[2])
            tot = m + jnp.log(jnp.exp(ls[0] - m) + jnp.exp(ls[1] - m) + jnp.exp(ls[2] - m))
            t_scr[j] = tot
            for g, d in enumerate(dils):
                oc_ref[:, g * W + j * LANES:g * W + (j + 1) * LANES] = (
                    o_scr[g * nh + j] * jnp.exp(ls[g] - tot)).astype(BF16)
                if d == 1:
                    lse_refs[g][:, lanes] = tot
                    continue
                for r in range(d):
                    lse_refs[g][r, :, lanes] = t_scr.at[j][_class_rows(r, d, tm), :]

    in_blk = [_residue_spec(d, tm, W) for d in dils]
    n_blk = 3 * ATTN_GROUP_HEADS
    res = pl.pallas_call(
        body, out_shape=[jax.ShapeDtypeStruct((T, 3 * W), BF16)] + [_residue_shape(T, d, W, F32) for d in dils],
        grid=(T // tm,), in_specs=in_blk * 2,
        out_specs=[pl.BlockSpec((tm, 3 * W), lambda i: (i, 0))] + in_blk,
        scratch_shapes=[pltpu.VMEM((n_blk, tm, LANES), F32), pltpu.VMEM((n_blk, tm, LANES), F32),
                        pltpu.VMEM((ATTN_GROUP_HEADS, tm, LANES), F32)],
        compiler_params=_params("parallel"), name=name)(
            *[_residue_view(o, d) for o, d in zip(outs, dils)], *[_residue_view(l, d) for l, d in zip(lses, dils)])
    return res[0], [r.reshape(T, W) for r in res[1:]]


def _attn_merge_bwd(d_oc, oc, name):
    T = d_oc.shape[0]
    W = ATTN_GROUP_WIDTH
    tm = _pick_tile(T, PERM_TILE, 16 * max(ATTN_DILATIONS))
    dils = ATTN_DILATIONS

    def body(d_ref, o_ref, *refs):
        delta_refs, db_refs, dl_scr, d_scr = refs[0:3], refs[3:6], refs[6], refs[7]
        nh = ATTN_GROUP_HEADS
        for j in range(nh):
            tot = jnp.zeros((tm, 1), F32)
            for g in range(3):
                cols = slice(g * W + j * LANES, g * W + (j + 1) * LANES)
                d_blk = d_ref[:, cols]
                d_scr[g * nh + j] = d_blk
                tot = tot + jnp.sum(d_blk * o_ref[:, cols].astype(F32), axis=-1, keepdims=True)
            dl_scr[j] = jnp.broadcast_to(tot, (tm, LANES))
        for g, d in enumerate(dils):
            for j in range(nh):
                lanes = slice(j * LANES, (j + 1) * LANES)
                if d == 1:
                    delta_refs[g][:, lanes] = dl_scr[j]
                    db_refs[g][:, lanes] = d_scr[g * nh + j].astype(BF16)
                    continue
                for r in range(d):
                    rows = _class_rows(r, d, tm)
                    delta_refs[g][r, :, lanes] = dl_scr.at[j][rows, :]
                    db_refs[g][r, :, lanes] = d_scr.at[g * nh + j][rows, :].astype(BF16)

    wide = pl.BlockSpec((tm, 3 * W), lambda i: (i, 0))
    out_blk = [_residue_spec(d, tm, W) for d in dils]
    res = pl.pallas_call(
        body, out_shape=[_residue_shape(T, d, W, F32) for d in dils] + [_residue_shape(T, d, W, BF16) for d in dils],
        grid=(T // tm,), in_specs=[wide, wide], out_specs=out_blk * 2,
        scratch_shapes=[pltpu.VMEM((ATTN_GROUP_HEADS, tm, LANES), F32),
                        pltpu.VMEM((3 * ATTN_GROUP_HEADS, tm, LANES), F32)],
        compiler_params=_params("parallel"), name=name)(d_oc, oc)
    flat = [r.reshape(T, W) for r in res]
    return flat[0:3], flat[3:6]


def _rope_tables(T):
    inv_freq = 1.0 / (ROPE_THETA ** (jnp.arange(0, ATTN_DIM, 2, dtype=F32) / ATTN_DIM))
    ang = jnp.arange(T, dtype=F32)[:, None] * inv_freq[None, :]
    cos, sin = jnp.cos(ang), jnp.sin(ang)
    return jnp.concatenate([cos, cos], axis=1), jnp.concatenate([-sin, sin], axis=1)


WEIGHT_GROUPS = {"hgrn": ("hgrn_in", "hgrn_out"), "ffn0": ("ffn_in0", "ffn_down0"),
                 "attn": ("qkv", "attn_out"), "ffn1": ("ffn_in1", "ffn_down1")}


def _local_step(x, target, norm_mix, norm_ffn, lb, out_gain, final_gain, fetch, publish):
    T = x.shape[0]
    g_mix = [norm_mix[0:1], norm_mix[1:2]]
    g_ffn = [norm_ffn[0:1], norm_ffn[1:2]]
    w = {}

    def whole(name):
        return [(w[name], w[name].shape[0], 0)]

    def qkv_parts(g):
        return [(w["qkv"], ATTN_GROUP_WIDTH, 3 * j + g) for j in range(3)]

    def ffn_fwd(h, layer):
        w.update(fetch(f"ffn{layer}"))
        n, gate, up, a = _ffn_in(h, g_ffn[layer], w[f"ffn_in{layer}"], f"ffn{layer}_in")
        out = _mm_nn([a], [whole(f"ffn_down{layer}")], h, name=f"ffn{layer}_down")
        return out, (n, gate, up, a)

    def ffn_bwd(h, saved, dh, dhb, layer):
        n, gate, up, a = saved
        w_in = w[f"ffn_in{layer}"]
        dgate, dup = _ffn_down_dx(dhb, w[f"ffn_down{layer}"], gate, up, f"ffn{layer}_down_dx")
        grad_in = _mm_tn(dgate, n, name=f"ffn{layer}_in_dw_gate", rows=2 * D_FF)
        grad_in = _mm_tn(dup, n, name=f"ffn{layer}_in_dw_up", into=grad_in, row_tile=D_FF // GRAD_TILE, rows=2 * D_FF)
        grads = {f"ffn_down{layer}": _mm_tn(a, dhb, name=f"ffn{layer}_down_dw"), f"ffn_in{layer}": grad_in}
        publish(f"ffn{layer}", grads)
        return _mm_nn([dgate, dup], [[(w_in, D_FF, 0)], [(w_in, D_FF, 1)]], dh, name=f"ffn{layer}_in_dx",
                      norm=(h, g_ffn[layer]))

    u0 = _rms_fwd(x, g_mix[0], "hgrn_norm")
    w.update(fetch("hgrn"))
    proj = _mm_nt(u0, whole("hgrn_in"), out_dtype=F32, name="hgrn_in")
    og, o_pre, states = _hgrn_fwd(proj, lb, out_gain, "hgrn_fwd")
    h1 = _mm_nn([og], [whole("hgrn_out")], x, name="hgrn_out")
    h2, ffn0 = ffn_fwd(h1, 0)

    cos, sin = _rope_tables(T)
    u1_g, cos_g, sin_g = _attn_norm(h2, g_mix[1], cos, sin, "attn_norm")
    w.update(fetch("attn"))
    qkv_g, outs, lses = [], [], []
    for g, d in enumerate(ATTN_DILATIONS):
        qkv_g.append(_mm_nt(u1_g[g], qkv_parts(g), out_dtype=BF16, name=f"attn_qkv{g}",
                            rope=(cos_g[g], sin_g[g], 2)))
        o_g, lse_g = _attn_fwd(qkv_g[g], d, f"attn_fwd{g}")
        outs.append(o_g)
        lses.append(lse_g)
    oc, lse_all = _attn_merge_fwd(outs, lses, "attn_merge")
    h3 = _mm_nn([oc], [whole("attn_out")], h2, name="attn_out")
    h4, ffn1 = ffn_fwd(h3, 1)

    dh4, dh4b, d_final, loss_part = _loss_head(h4, target, final_gain, "loss_head")
    dh3, dh3b, d_ffn1 = ffn_bwd(h3, ffn1, dh4, dh4b, 1)

    d_oc = _mm_nt(dh3b, whole("attn_out"), out_dtype=F32, name="attn_out_dx")
    grad_attn_out = _mm_tn(oc, dh3b, name="attn_out_dw")
    delta, d_ocb = _attn_merge_bwd(d_oc, oc, "attn_merge_bwd")
    du1, qkv_pieces = [], []
    for g, d in enumerate(ATTN_DILATIONS):
        dqkv = _attn_bwd(qkv_g[g], d_ocb[g], lse_all[g], delta[g], cos_g[g], sin_g[g], d, f"attn_bwd{g}")
        qkv_pieces.append(_mm_tn(dqkv, u1_g[g], name=f"attn_qkv_dw{g}"))
        du1.append(_mm_nn([dqkv], [qkv_parts(g)], None, name=f"attn_qkv_dx{g}"))
    grad_qkv = jnp.stack([p.reshape(3, ATTN_GROUP_WIDTH, D_MODEL) for p in qkv_pieces], axis=1).reshape(
        3 * ATTN_WIDTH, D_MODEL)
    publish("attn", {"qkv": grad_qkv, "attn_out": grad_attn_out})
    dh2, dh2b, d_mix1 = _rms_bwd(h2, g_mix[1], du1, dh3, "attn_norm_bwd", ATTN_DILATIONS)

    dh1, dh1b, d_ffn0 = ffn_bwd(h1, ffn0, dh2, dh2b, 0)

    d_og = _mm_nt(dh1b, whole("hgrn_out"), out_dtype=F32, name="hgrn_out_dx")
    grad_hgrn_out = _mm_tn(og, dh1b, name="hgrn_out_dw")
    dproj, d_lb, d_out_gain = _hgrn_bwd(proj, o_pre, d_og, states, lb, out_gain, "hgrn_bwd")
    publish("hgrn", {"hgrn_in": _mm_tn(dproj, u0, name="hgrn_in_dw"), "hgrn_out": grad_hgrn_out})
    dx, _, d_mix0 = _mm_nn([dproj], [whole("hgrn_in")], dh1, name="hgrn_in_dx", norm=(x, g_mix[0]))

    small = dict(norm_mix0=d_mix0, norm_mix1=d_mix1, norm_ffn0=d_ffn0, norm_ffn1=d_ffn1, lb=d_lb,
                 out_gain=d_out_gain, final=d_final, loss=loss_part)
    return dx, small


WEIGHT_NAMES = ("hgrn_in", "hgrn_out", "qkv", "attn_out", "ffn_in0", "ffn_in1", "ffn_down0", "ffn_down1")
MESH_IDS = pl.DeviceIdType.MESH
HBM_SPEC = pl.BlockSpec(memory_space=pl.ANY)


N_PEERS = N_DEV - 1
PEER_OFFSETS = [(dx, dy, dc) for dx in (0, 1) for dy in (0, 1) for dc in (0, 1)][1:]


def _mesh_place():
    x, y, c = lax.axis_index("x"), lax.axis_index("y"), lax.axis_index("c")
    peers = []
    for dx, dy, dc in PEER_OFFSETS:
        px, py, pc = (1 - x if dx else x), (1 - y if dy else y), (1 - c if dc else c)
        peers.append(((px, py, pc), 4 * px + 2 * py + pc))
    return 4 * x + 2 * y + c, peers


def _exchange_launch(srcs, scatter, collective_id, name):
    n = len(srcs)
    src_refs = [jax.new_ref(s, memory_space=pltpu.MemorySpace.HBM) for s in srcs]
    land_refs = [jax.empty_ref(jax.ShapeDtypeStruct(s.shape if scatter else (N_DEV,) + s.shape, s.dtype),
                               memory_space=pltpu.MemorySpace.HBM) for s in srcs]

    @pl.kernel(mesh=plsc.ScalarSubcoreMesh(axis_name="sequencer", num_cores=1), name=name,
               scratch_types=(pltpu.SemaphoreType.DMA((n * N_PEERS,)), pltpu.SemaphoreType.DMA((n * N_PEERS,)),
                              pltpu.SemaphoreType.DMA((n,))),
               compiler_params=pltpu.CompilerParams(collective_id=collective_id))
    def launch(send_sems, recv_sems, local_sems):
        me, peers = _mesh_place()
        barrier = pltpu.get_barrier_semaphore()
        for peer, _ in peers:
            pl.semaphore_signal(barrier, inc=1, device_id=peer, device_id_type=MESH_IDS)
        pl.semaphore_wait(barrier, N_PEERS)
        own = [pltpu.make_async_copy(src_refs[w].at[me] if scatter else src_refs[w], land_refs[w].at[me],
                                     local_sems.at[w]) for w in range(n)]
        for cp in own:
            cp.start()
        copies = [pltpu.make_async_remote_copy(
            src_ref=src_refs[w].at[pid] if scatter else src_refs[w], dst_ref=land_refs[w].at[me],
            send_sem=send_sems.at[w * N_PEERS + k], recv_sem=recv_sems.at[w * N_PEERS + k],
            device_id=peer, device_id_type=MESH_IDS) for w in range(n) for k, (peer, pid) in enumerate(peers)]
        for cp in copies:
            cp.start()
        for cp in copies:
            cp.wait()
        for cp in own:
            cp.wait()

    launch()
    return land_refs


def _gather_small(block, name):
    def body(in_ref, out_ref, send_sems, recv_sems, local_sem):
        me, peers = _mesh_place()
        own = pltpu.make_async_copy(in_ref, out_ref.at[me], local_sem)
        own.start()
        sends = [pltpu.make_async_remote_copy(
            src_ref=in_ref, dst_ref=out_ref.at[me], send_sem=send_sems.at[k], recv_sem=recv_sems.at[k],
            device_id=peer, device_id_type=MESH_IDS) for k, (peer, _) in enumerate(peers)]
        for cp in sends:
            cp.start()
        for cp in sends:
            cp.wait_recv()
        for cp in sends:
            cp.wait_send()
        own.wait()

    return pl.pallas_call(
        body, out_shape=jax.ShapeDtypeStruct((N_DEV,) + block.shape, block.dtype),
        in_specs=[HBM_SPEC], out_specs=HBM_SPEC,
        scratch_shapes=[pltpu.SemaphoreType.DMA((N_PEERS,)), pltpu.SemaphoreType.DMA((N_PEERS,)),
                        pltpu.SemaphoreType.DMA],
        name=name)(block)


def _sum_blocks(recv, name):
    rows = recv.shape[1]
    tr = _pick_tile(rows, 256, 16)

    def body(r_ref, g_ref):
        acc = r_ref[0].astype(F32)
        for j in range(1, N_DEV):
            acc = acc + r_ref[j].astype(F32)
        g_ref[...] = acc

    return pl.pallas_call(
        body, out_shape=jax.ShapeDtypeStruct((rows, D_MODEL), F32), grid=(rows // tr,),
        in_specs=[pl.BlockSpec((N_DEV, tr, D_MODEL), lambda i: (0, i, 0))],
        out_specs=pl.BlockSpec((tr, D_MODEL), lambda i: (i, 0)),
        compiler_params=_params("parallel"), name=name)(recv)


def _adamw_math(w, g, m, v):
    m_new = ADAM_B1 * m + (1.0 - ADAM_B1) * g
    v_new = ADAM_B2 * v + (1.0 - ADAM_B2) * (g * g)
    m_hat = m_new / (1.0 - ADAM_B1 ** ADAM_STEP)
    v_hat = v_new / (1.0 - ADAM_B2 ** ADAM_STEP)
    delta = -ADAM_LR * (m_hat / (jnp.sqrt(v_hat) + ADAM_EPS) + ADAM_WD * w)
    return delta, m_new, v_new


def _adamw(w, g, m, v, name):
    rows, cols = w.shape
    tr = _pick_tile(rows, 256, 8)

    def body(w_ref, g_ref, m_ref, v_ref, d_ref, mo_ref, vo_ref):
        d_ref[...], mo_ref[...], vo_ref[...] = _adamw_math(w_ref[...], g_ref[...], m_ref[...], v_ref[...])

    blk = pl.BlockSpec((tr, cols), lambda i: (i, 0))
    return pl.pallas_call(
        body, out_shape=(jax.ShapeDtypeStruct((rows, cols), F32),) * 3, grid=(rows // tr,),
        in_specs=[blk] * 4, out_specs=(blk,) * 3, compiler_params=_params("parallel"), name=name)(w, g, m, v)


ROW_MIX, ROW_FFN, ROW_LB, ROW_OUT_GAIN, ROW_FINAL = 0, 2, 4, 7, 8
PART_MIX, PART_FFN, PART_LB, PART_OUT_GAIN, PART_FINAL, PART_LOSS = 0, 2, 4, 5, 6, 7


def _small_update(parts_all, w, m, v, name):
    def body(p_ref, w_ref, m_ref, v_ref, g_ref, d_ref, mo_ref, vo_ref, loss_ref):
        def total(row, n=1):
            tot = p_ref[0, row:row + n, :]
            for j in range(1, N_DEV):
                tot = tot + p_ref[j, row:row + n, :]
            return tot

        logits = [w_ref[ROW_LB + i:ROW_LB + i + 1, :] for i in range(3)]
        mx = jnp.maximum(jnp.maximum(logits[0], logits[1]), logits[2])
        ex = [jnp.exp(l - mx) for l in logits]
        den = ex[0] + ex[1] + ex[2]
        prob = [e / den for e in ex]
        d_lb = total(PART_LB)
        g_ref[...] = jnp.zeros_like(g_ref)
        g_ref[ROW_MIX:ROW_MIX + 2, :] = total(PART_MIX, 2)
        g_ref[ROW_FFN:ROW_FFN + 2, :] = total(PART_FFN, 2)
        for i in range(3):
            g_ref[ROW_LB + i:ROW_LB + i + 1, :] = prob[i] * ((d_lb if i == 0 else 0.0) - prob[0] * d_lb)
        g_ref[ROW_OUT_GAIN:ROW_OUT_GAIN + 1, :] = total(PART_OUT_GAIN)
        g_ref[ROW_FINAL:ROW_FINAL + 1, :] = total(PART_FINAL)
        d_ref[...], mo_ref[...], vo_ref[...] = _adamw_math(w_ref[...], g_ref[...], m_ref[...], v_ref[...])
        loss_ref[...] = jnp.sum(total(PART_LOSS), axis=-1, keepdims=True)

    packed = jax.ShapeDtypeStruct((16, D_MODEL), F32)
    return pl.pallas_call(
        body, out_shape=(packed, packed, packed, packed, jax.ShapeDtypeStruct((1, 1), F32)),
        compiler_params=pltpu.CompilerParams(vmem_limit_bytes=VMEM_LIMIT), name=name)(parts_all, w, m, v)


def _pack_small(norm_mix, norm_ffn, lb_logits, out_gain, final):
    pad = jnp.zeros((1, D_MODEL - HGRN_DIM), F32)
    return jnp.concatenate([norm_mix, norm_ffn, lb_logits, jnp.concatenate([out_gain, pad], axis=1),
                            final.reshape(1, D_MODEL), jnp.zeros((16 - ROW_FINAL - 1, D_MODEL), F32)], axis=0)


def _unpack_small(p):
    return (p[ROW_MIX:ROW_MIX + 2], p[ROW_FFN:ROW_FFN + 2], p[ROW_LB:ROW_LB + 3],
            p[ROW_OUT_GAIN:ROW_OUT_GAIN + 1, :HGRN_DIM], p[ROW_FINAL])


def _lower_bound(lb_logits, name):
    def body(l_ref, o_ref):
        logits = [l_ref[i:i + 1, :] for i in range(3)]
        mx = jnp.maximum(jnp.maximum(logits[0], logits[1]), logits[2])
        ex = [jnp.exp(l - mx) for l in logits]
        o_ref[...] = ex[0] / (ex[0] + ex[1] + ex[2])

    return pl.pallas_call(body, out_shape=jax.ShapeDtypeStruct((1, D_MODEL), F32), name=name)(lb_logits)


def kernel(x, norm_mix, norm_ffn, hgrn_w_in, hgrn_lb_logits, hgrn_out_norm, hgrn_w_out, attn_w_qkv, attn_w_out, ffn_w_in, ffn_w_down, final_norm, loss_target, m_norm_mix, m_norm_ffn, m_hgrn_w_in, m_hgrn_lb_logits, m_hgrn_out_norm, m_hgrn_w_out, m_attn_w_qkv, m_attn_w_out, m_ffn_w_in, m_ffn_w_down, m_final_norm, v_norm_mix, v_norm_ffn, v_hgrn_w_in, v_hgrn_lb_logits, v_hgrn_out_norm, v_hgrn_w_out, v_attn_w_qkv, v_attn_w_out, v_ffn_w_in, v_ffn_w_down, v_final_norm):
    col_sharded = {"hgrn_in": hgrn_w_in[0], "qkv": attn_w_qkv[0], "ffn_in0": ffn_w_in[0], "ffn_in1": ffn_w_in[1]}
    row_sharded = {"hgrn_out": hgrn_w_out[0], "attn_out": attn_w_out[0], "ffn_down0": ffn_w_down[0],
                   "ffn_down1": ffn_w_down[1]}
    gathering = {}
    for gi, (group, names) in enumerate(WEIGHT_GROUPS.items()):
        shards = [(col_sharded[n].T if n in col_sharded else row_sharded[n]).astype(BF16) for n in names]
        gathering[group] = _exchange_launch(shards, False, 1 + gi, f"weights_gather_{group}")

    def fetch(group):
        return {n: land[...].reshape(-1, D_MODEL) for n, land in zip(WEIGHT_GROUPS[group], gathering[group])}

    in_flight = {}

    def publish(group, grads):
        names = WEIGHT_GROUPS[group]
        parts = [grads[n].reshape(N_DEV, -1, D_MODEL) for n in names]
        in_flight[group] = _exchange_launch(parts, True, 1 + len(WEIGHT_GROUPS) + list(WEIGHT_GROUPS).index(group),
                                            f"grads_send_{group}")

    lb = _lower_bound(hgrn_lb_logits, "hgrn_lower_bound")
    grad_x, small = _local_step(x[0], loss_target[0], norm_mix, norm_ffn, lb, hgrn_out_norm,
                                final_norm.reshape(1, D_MODEL), fetch, publish)

    pad = jnp.zeros((1, D_MODEL - HGRN_DIM), F32)
    small_part = jnp.concatenate(
        [small["norm_mix0"], small["norm_mix1"], small["norm_ffn0"], small["norm_ffn1"], small["lb"],
         jnp.concatenate([small["out_gain"], pad], axis=1), small["final"], small["loss"]], axis=0)
    small_all = _gather_small(small_part, "small_grads_gather")
    received = {}
    for group in ("ffn1", "attn", "ffn0", "hgrn"):
        received.update(zip(WEIGHT_GROUPS[group], [land[...] for land in in_flight[group]]))

    masters = {"hgrn_in": (hgrn_w_in[0], m_hgrn_w_in[0], v_hgrn_w_in[0]),
               "hgrn_out": (hgrn_w_out[0], m_hgrn_w_out[0], v_hgrn_w_out[0]),
               "qkv": (attn_w_qkv[0], m_attn_w_qkv[0], v_attn_w_qkv[0]),
               "attn_out": (attn_w_out[0], m_attn_w_out[0], v_attn_w_out[0]),
               "ffn_in0": (ffn_w_in[0], m_ffn_w_in[0], v_ffn_w_in[0]),
               "ffn_in1": (ffn_w_in[1], m_ffn_w_in[1], v_ffn_w_in[1]),
               "ffn_down0": (ffn_w_down[0], m_ffn_w_down[0], v_ffn_w_down[0]),
               "ffn_down1": (ffn_w_down[1], m_ffn_w_down[1], v_ffn_w_down[1])}
    res = {}
    for n in WEIGHT_NAMES:
        g = _sum_blocks(received[n], f"{n}_grad_sum")
        if n in col_sharded:
            g = g.T
        wv, mv, vv = masters[n]
        res[n] = (g,) + tuple(_adamw(wv, g, mv, vv, f"{n}_adamw"))

    def single(n):
        return [t[None] for t in res[n]]

    def pair(n):
        return [jnp.stack([a, b]) for a, b in zip(res[n + "0"], res[n + "1"])]

    big = dict(hgrn_w_in=single("hgrn_in"), hgrn_w_out=single("hgrn_out"), attn_w_qkv=single("qkv"),
               attn_w_out=single("attn_out"), ffn_w_in=pair("ffn_in"), ffn_w_down=pair("ffn_down"))

    w_small = _pack_small(norm_mix, norm_ffn, hgrn_lb_logits, hgrn_out_norm, final_norm)
    m_small = _pack_small(m_norm_mix, m_norm_ffn, m_hgrn_lb_logits, m_hgrn_out_norm, m_final_norm)
    v_small = _pack_small(v_norm_mix, v_norm_ffn, v_hgrn_lb_logits, v_hgrn_out_norm, v_final_norm)
    g_s, d_s, m_s, v_s, loss = _small_update(small_all, w_small, m_small, v_small, "small_update")
    small_out = [_unpack_small(t) for t in (g_s, d_s, m_s, v_s)]

    def group(i):
        s = small_out[i]
        return (s[0], s[1], big["hgrn_w_in"][i], s[2], s[3], big["hgrn_w_out"][i], big["attn_w_qkv"][i],
                big["attn_w_out"][i], big["ffn_w_in"][i], big["ffn_w_down"][i], s[4])

    return (loss.reshape(()), grad_x[None], *group(0), *group(1), *group(2), *group(3))
```

```python
import functools

import jax
import jax.numpy as jnp
from jax import lax
from jax.experimental import pallas as pl
from jax.experimental.pallas import tpu as pltpu
from jax.experimental.pallas import tpu_sc as plsc

F32 = jnp.float32
BF16 = jnp.bfloat16

D_MODEL = 1024
N_DEV = 8
NORM_EPS = 1e-6

HGRN_HEADS = 8
HGRN_DIM = 128
HGRN_CHUNK = 64
HGRN_STEP_CHUNKS = 2
HGRN_EXP_CLAMP = 60.0

ATTN_DIM = 128
ATTN_BLOCK = 128
ATTN_GROUP_HEADS = 4
ATTN_GROUP_WIDTH = ATTN_GROUP_HEADS * ATTN_DIM
ATTN_DILATIONS = (1, 4, 16)
ATTN_WIDTH = 3 * ATTN_GROUP_WIDTH
ROPE_THETA = 10000.0
NEG_BIG = -1e30

D_FF = 2816

ADAM_LR = 0.001
ADAM_B1 = 0.9
ADAM_B2 = 0.999
ADAM_EPS = 1e-08
ADAM_WD = 0.01
ADAM_STEP = 10

VMEM_LIMIT = 48 * 1024 * 1024

NT = (((1,), (1,)), ((), ()))
NN = (((1,), (0,)), ((), ()))
TN = (((0,), (0,)), ((), ()))


def _dot(a, b, dims):
    return lax.dot_general(a, b, dims, preferred_element_type=F32)


def _params(*sem):
    return pltpu.CompilerParams(dimension_semantics=sem, vmem_limit_bytes=VMEM_LIMIT)


def _pick_tile(n, cap, mult):
    best = None
    for t in range(mult, min(n, cap) + 1, mult):
        if n % t == 0:
            best = t
    assert best is not None, (n, cap, mult)
    return best


def _sigmoid(x):
    return 1.0 / (1.0 + jnp.exp(-x))


def _sigmoid_gate(x):
    return pl.reciprocal(1.0 + jnp.exp(-x), approx=True)


ROW_TILE = 512
COL_CHUNK = 512
GRAD_TILE = 256


def _whole(shape, index_map):
    return pl.BlockSpec(shape, index_map, pipeline_mode=pl.Buffered(1))


def _part_specs(parts, n_cols):
    return [_whole((rows, n_cols), functools.partial(lambda i, b: (b, 0), b=blk)) for _, rows, blk in parts]


def _mm_nt(a, w_parts, *, out_dtype, name, rope=None):
    M, K = a.shape
    tm = _pick_tile(M, ROW_TILE, 16)
    widths = [rows for _, rows, _ in w_parts]
    n_parts = len(w_parts)

    def body(*refs):
        a_ref, w_refs, o_ref = refs[0], refs[1:1 + n_parts], refs[-1]
        av = a_ref[...]
        off = 0
        for p, w_ref in enumerate(w_refs):
            for c0 in range(0, widths[p], COL_CHUNK):
                cw = min(COL_CHUNK, widths[p] - c0)
                acc = _dot(av, w_ref[c0:c0 + cw, :], NT)
                if rope is not None and p < rope[2]:
                    cos, sin = refs[1 + n_parts][...], refs[2 + n_parts][...]
                    for h0 in range(0, cw, ATTN_DIM):
                        xh = acc[:, h0:h0 + ATTN_DIM]
                        rot = pltpu.roll(xh, ATTN_DIM // 2, 1)
                        o_ref[:, off + c0 + h0:off + c0 + h0 + ATTN_DIM] = (xh * cos + rot * sin).astype(out_dtype)
                else:
                    o_ref[:, off + c0:off + c0 + cw] = acc.astype(out_dtype)
            off += widths[p]

    in_specs = [pl.BlockSpec((tm, K), lambda i: (i, 0))] + _part_specs(w_parts, K)
    args = [a] + [w for w, _, _ in w_parts]
    if rope is not None:
        in_specs += [pl.BlockSpec((tm, ATTN_DIM), lambda i: (i, 0))] * 2
        args += [rope[0], rope[1]]
    return pl.pallas_call(
        body, out_shape=jax.ShapeDtypeStruct((M, sum(widths)), out_dtype), grid=(M // tm,),
        in_specs=in_specs, out_specs=pl.BlockSpec((tm, sum(widths)), lambda i: (i, 0)),
        compiler_params=_params("parallel"), name=name)(*args)


def _mm_nn(a_list, w_parts_list, resid, *, name, norm=None, head=None):
    M = a_list[0].shape[0]
    tm = _pick_tile(M, ROW_TILE, 16)
    n_a = len(a_list)
    flat_parts = [p for parts in w_parts_list for p in parts]
    extra = norm if norm is not None else head
    n_in = n_a + len(flat_parts) + (1 if resid is not None else 0) + (2 if extra is not None else 0)

    def body(*refs):
        a_refs, w_refs = refs[:n_a], refs[n_a:n_a + len(flat_parts)]
        acc = None
        wi = 0
        for a_ref, parts in zip(a_refs, w_parts_list):
            off = 0
            for _, rows, _ in parts:
                term = _dot(a_ref[:, off:off + rows], w_refs[wi][...], NN)
                acc = term if acc is None else acc + term
                off += rows
                wi += 1
        if extra is None:
            if resid is not None:
                acc = acc + refs[n_in - 1][...]
            refs[n_in][...] = acc
            return
        if head is not None:
            _loss_head_math(acc + refs[n_in - 3][...], refs[n_in - 2], refs[n_in - 1], *refs[n_in:n_in + 4])
            return
        dres_ref, x_ref, g_ref = refs[n_in - 3:n_in]
        dx_ref, dxb_ref, dg_ref = refs[n_in:n_in + 3]

        @pl.when(pl.program_id(0) == 0)
        def _():
            dg_ref[...] = jnp.zeros_like(dg_ref)

        xv = x_ref[...]
        rstd = lax.rsqrt(jnp.mean(xv * xv, axis=-1, keepdims=True) + NORM_EPS)
        n = xv * rstd
        dg_ref[...] += jnp.sum(acc * n, axis=0, keepdims=True)
        dn = acc * g_ref[...]
        dx = dres_ref[...] + rstd * (dn - n * jnp.mean(dn * n, axis=-1, keepdims=True))
        dx_ref[...] = dx
        dxb_ref[...] = dx.astype(BF16)

    row = pl.BlockSpec((tm, D_MODEL), lambda i: (i, 0))
    vec = pl.BlockSpec((1, D_MODEL), lambda i: (0, 0))
    in_specs = [pl.BlockSpec((tm, a.shape[1]), lambda i: (i, 0)) for a in a_list] + _part_specs(flat_parts, D_MODEL)
    args = list(a_list) + [w for w, _, _ in flat_parts]
    if resid is not None:
        in_specs.append(row)
        args.append(resid)
    if extra is None:
        return pl.pallas_call(
            body, out_shape=jax.ShapeDtypeStruct((M, D_MODEL), F32), grid=(M // tm,),
            in_specs=in_specs, out_specs=row, compiler_params=_params("parallel"), name=name)(*args)
    assert resid is not None
    out_shape = [jax.ShapeDtypeStruct((M, D_MODEL), F32), jax.ShapeDtypeStruct((M, D_MODEL), BF16),
                 jax.ShapeDtypeStruct((1, D_MODEL), F32)]
    out_specs = [row, row, vec]
    if head is not None:
        out_shape.append(jax.ShapeDtypeStruct((1, D_MODEL), F32))
        out_specs.append(vec)
    return pl.pallas_call(
        body, out_shape=out_shape, grid=(M // tm,), in_specs=in_specs + [row, vec], out_specs=out_specs,
        compiler_params=_params("arbitrary"), name=name)(*args, extra[0], extra[1])


def _mm_tn(a, b, *, name, into=None, row_tile=0, rows=None):
    T, R = a.shape
    N = b.shape[1]
    tr = GRAD_TILE
    rows = R if rows is None else rows

    def body(a_ref, b_ref, *refs):
        refs[-1][...] = _dot(a_ref[...], b_ref[...], TN).astype(BF16)

    in_specs = [pl.BlockSpec((T, tr), lambda r: (0, r)), _whole((T, N), lambda r: (0, 0))]
    args = [a, b]
    if into is not None:
        in_specs.append(HBM_SPEC)
        args.append(into)
    return pl.pallas_call(
        body, out_shape=jax.ShapeDtypeStruct((rows, N), BF16), grid=(R // tr,),
        in_specs=in_specs, out_specs=pl.BlockSpec((tr, N), lambda r: (row_tile + r, 0)),
        input_output_aliases={} if into is None else {2: 0},
        compiler_params=_params("parallel"), name=name)(*args)


def _rms_fwd(x, gain, name):
    T = x.shape[0]
    tm = _pick_tile(T, 512, 16)

    def body(x_ref, g_ref, u_ref):
        xv = x_ref[...]
        rstd = lax.rsqrt(jnp.mean(xv * xv, axis=-1, keepdims=True) + NORM_EPS)
        u_ref[...] = (xv * rstd * g_ref[...]).astype(BF16)

    return pl.pallas_call(
        body, out_shape=jax.ShapeDtypeStruct((T, D_MODEL), BF16), grid=(T // tm,),
        in_specs=[pl.BlockSpec((tm, D_MODEL), lambda i: (i, 0)), pl.BlockSpec((1, D_MODEL), lambda i: (0, 0))],
        out_specs=pl.BlockSpec((tm, D_MODEL), lambda i: (i, 0)),
        compiler_params=_params("parallel"), name=name)(x, gain)


def _rms_bwd(x, gain, dus, dres, name, dilations=(1,)):
    T = x.shape[0]
    tm = _pick_tile(T, PERM_TILE, 16 * max(dilations))
    n_du = len(dus)

    def body(x_ref, g_ref, *refs):
        du_refs, dres_ref = refs[:n_du], refs[n_du]
        dx_ref, dxb_ref, dg_ref, du_scr = refs[n_du + 1:]

        @pl.when(pl.program_id(0) == 0)
        def _():
            dg_ref[...] = jnp.zeros_like(dg_ref)

        if tuple(dilations) == (1,):
            du = du_refs[0][...]
        else:
            for i, (d, du_ref) in enumerate(zip(dilations, du_refs)):
                for j in range(D_MODEL // LANES):
                    lanes = slice(j * LANES, (j + 1) * LANES)
                    if d == 1:
                        du_scr[j] = du_ref[:, lanes] if i == 0 else du_scr[j] + du_ref[:, lanes]
                        continue
                    blk = du_scr.at[j]
                    for r in range(d):
                        rows = _class_rows(r, d, tm)
                        blk[rows, :] = du_ref[r, :, lanes] if i == 0 else blk[rows, :] + du_ref[r, :, lanes]
            du = jnp.concatenate([du_scr[j] for j in range(D_MODEL // LANES)], axis=1)
        xv = x_ref[...]
        rstd = lax.rsqrt(jnp.mean(xv * xv, axis=-1, keepdims=True) + NORM_EPS)
        n = xv * rstd
        dg_ref[...] += jnp.sum(du * n, axis=0, keepdims=True)
        dn = du * g_ref[...]
        dx = dres_ref[...] + rstd * (dn - n * jnp.mean(dn * n, axis=-1, keepdims=True))
        dx_ref[...] = dx
        dxb_ref[...] = dx.astype(BF16)

    row = pl.BlockSpec((tm, D_MODEL), lambda i: (i, 0))
    vec = pl.BlockSpec((1, D_MODEL), lambda i: (0, 0))
    return pl.pallas_call(
        body,
        out_shape=(jax.ShapeDtypeStruct((T, D_MODEL), F32), jax.ShapeDtypeStruct((T, D_MODEL), BF16),
                   jax.ShapeDtypeStruct((1, D_MODEL), F32)),
        grid=(T // tm,), in_specs=[row, vec] + [_residue_spec(d, tm, D_MODEL) for d in dilations] + [row],
        out_specs=(row, row, vec), scratch_shapes=[pltpu.VMEM((D_MODEL // LANES, tm, LANES), F32)],
        compiler_params=_params("arbitrary"), name=name)(
            x, gain, *[_residue_view(du, d) for du, d in zip(dus, dilations)], dres)


def _loss_head_math(hv, t_ref, g_ref, dh_ref, dhb_ref, dg_ref, loss_ref):
    inv_f = 1.0 / D_MODEL

    @pl.when(pl.program_id(0) == 0)
    def _():
        dg_ref[...] = jnp.zeros_like(dg_ref)
        loss_ref[...] = jnp.zeros_like(loss_ref)

    g = g_ref[...]
    rstd = lax.rsqrt(jnp.mean(hv * hv, axis=-1, keepdims=True) + NORM_EPS)
    n = hv * rstd
    err = n * g - t_ref[...]
    loss_ref[...] += (0.5 * inv_f) * jnp.sum(err * err, axis=0, keepdims=True)
    dy = err * inv_f
    dg_ref[...] += jnp.sum(dy * n, axis=0, keepdims=True)
    dn = dy * g
    dh = rstd * (dn - n * jnp.mean(dn * n, axis=-1, keepdims=True))
    dh_ref[...] = dh
    dhb_ref[...] = dh.astype(BF16)


FFN_TILE = 256


def _ffn_in(h, gain, w_in, name):
    T = h.shape[0]
    tm = _pick_tile(T, ROW_TILE, 16)

    def body(h_ref, g_ref, w_ref, n_ref, gate_ref, up_ref, a_ref):
        hv = h_ref[...]
        rstd = lax.rsqrt(jnp.mean(hv * hv, axis=-1, keepdims=True) + NORM_EPS)
        n = (hv * rstd * g_ref[...]).astype(BF16)
        n_ref[...] = n
        for c0 in range(0, D_FF, FFN_TILE):
            cols = slice(c0, c0 + FFN_TILE)
            gate = _dot(n, w_ref[c0:c0 + FFN_TILE, :], NT)
            up = _dot(n, w_ref[D_FF + c0:D_FF + c0 + FFN_TILE, :], NT)
            gate_ref[:, cols] = gate.astype(BF16)
            up_ref[:, cols] = up.astype(BF16)
            a_ref[:, cols] = (gate * _sigmoid(gate) * up).astype(BF16)

    row = pl.BlockSpec((tm, D_MODEL), lambda i: (i, 0))
    wide = pl.BlockSpec((tm, D_FF), lambda i: (i, 0))
    wide_shape = jax.ShapeDtypeStruct((T, D_FF), BF16)
    return pl.pallas_call(
        body, out_shape=(jax.ShapeDtypeStruct((T, D_MODEL), BF16), wide_shape, wide_shape, wide_shape),
        grid=(T // tm,),
        in_specs=[row, pl.BlockSpec((1, D_MODEL), lambda i: (0, 0)), _whole((2 * D_FF, D_MODEL), lambda i: (0, 0))],
        out_specs=(row, wide, wide, wide), compiler_params=_params("parallel"), name=name)(h, gain, w_in)


def _ffn_down_dx(dhb, w_down, gate, up, name):
    T = dhb.shape[0]
    tm = _pick_tile(T, ROW_TILE, 16)

    def body(dh_ref, w_ref, gate_ref, up_ref, dgate_ref, dup_ref):
        dh = dh_ref[...]
        for c0 in range(0, D_FF, FFN_TILE):
            cols = slice(c0, c0 + FFN_TILE)
            da = _dot(dh, w_ref[c0:c0 + FFN_TILE, :], NT)
            gate = gate_ref[:, cols].astype(F32)
            sg = _sigmoid(gate)
            dgate_ref[:, cols] = (da * up_ref[:, cols].astype(F32) * (sg * (1.0 + gate * (1.0 - sg)))).astype(BF16)
            dup_ref[:, cols] = (da * gate * sg).astype(BF16)

    wide = pl.BlockSpec((tm, D_FF), lambda i: (i, 0))
    wide_shape = jax.ShapeDtypeStruct((T, D_FF), BF16)
    return pl.pallas_call(
        body, out_shape=(wide_shape, wide_shape), grid=(T // tm,),
        in_specs=[pl.BlockSpec((tm, D_MODEL), lambda i: (i, 0)), _whole((D_FF, D_MODEL), lambda i: (0, 0)), wide, wide],
        out_specs=(wide, wide), compiler_params=_params("parallel"), name=name)(dhb, w_down, gate, up)


def _tri(n, lower):
    r = lax.broadcasted_iota(jnp.int32, (n, n), 0)
    c = lax.broadcasted_iota(jnp.int32, (n, n), 1)
    return (c <= r) if lower else (c >= r)


def _running_sum(x, lower):
    tri = _tri(x.shape[0], lower).astype(BF16)
    hi = x.astype(BF16)
    rest = x - hi.astype(F32)
    mid = rest.astype(BF16)
    lo = (rest - mid.astype(F32)).astype(BF16)
    return _dot(tri, hi, NN) + _dot(tri, mid, NN) + _dot(tri, lo, NN)


def _hgrn_gates(q_raw, f_raw, lb):
    C = q_raw.shape[0]
    sig_f = _sigmoid(f_raw)
    forget = lb + (1.0 - lb) * sig_f
    key = 1.0 - forget
    log_f = jnp.log(forget)
    b = _running_sum(log_f, True)
    first_half = lax.broadcasted_iota(jnp.int32, log_f.shape, 0) < C // 2
    r = jnp.sum(jnp.where(first_half, log_f, 0.0), axis=0, keepdims=True)
    b_last = jnp.sum(log_f, axis=0, keepdims=True)
    e_a = jnp.exp(jnp.minimum(b - r, HGRN_EXP_CLAMP))
    e_b = jnp.exp(jnp.minimum(r - b, HGRN_EXP_CLAMP))
    e_q = jnp.exp(b)
    e_k = jnp.exp(b_last - b)
    sig_q = _sigmoid_gate(q_raw)
    query = q_raw * sig_q
    return dict(sig_f=sig_f, forget=forget, sig_q=sig_q, e_a=e_a, e_b=e_b, e_q=e_q, e_k=e_k,
                e_last=jnp.exp(b_last), q_a=query * e_a, k_b=key * e_b, q_hat=query * e_q, k_til=key * e_k)


def _hgrn_fwd(proj, lb, gain, name):
    T = proj.shape[0]
    C = HGRN_CHUNK
    CPS = HGRN_STEP_CHUNKS
    H, HD = HGRN_HEADS, HGRN_DIM

    def body(q_ref, f_ref, i_ref, g_ref, lb_ref, gain_ref, og_ref, o_ref, st_ref, s_scr):
        @pl.when(pl.program_id(0) == 0)
        def _():
            s_scr[...] = jnp.zeros_like(s_scr)

        causal = _tri(C, True)
        gain_v = gain_ref[...]
        heads = [slice(h * HD, (h + 1) * HD) for h in range(H)]
        s_t = [s_scr[h] for h in range(H)]
        for cc in range(CPS):
            rows = slice(cc * C, (cc + 1) * C)
            for h in range(H):
                st_ref[cc, h] = s_t[h]
            gt = _hgrn_gates(q_ref[rows, :], f_ref[rows, :], lb_ref[...])
            q_a, k_b = gt["q_a"].astype(BF16), gt["k_b"].astype(BF16)
            q_hat, k_til = gt["q_hat"].astype(BF16), gt["k_til"].astype(BF16)
            v = i_ref[rows, :].astype(BF16)
            p = [jnp.where(causal, _dot(q_a[:, sl], k_b[:, sl], NT), 0.0).astype(BF16) for sl in heads]
            o = [_dot(p[h], v[:, sl], NN) + _dot(q_hat[:, sl], s_t[h].astype(BF16), NT)
                 for h, sl in enumerate(heads)]
            s_t = [gt["e_last"][:, sl] * s_t[h] + _dot(v[:, sl], k_til[:, sl], TN) for h, sl in enumerate(heads)]
            for h, sl in enumerate(heads):
                o_ref[rows, sl] = o[h]
                rstd = lax.rsqrt(jnp.mean(o[h] * o[h], axis=-1, keepdims=True) + NORM_EPS)
                g_raw = g_ref[rows, sl]
                og_ref[rows, sl] = (o[h] * rstd * gain_v * (g_raw * _sigmoid_gate(g_raw))).astype(BF16)
        for h in range(H):
            s_scr[h] = s_t[h]

    col = lambda j: pl.BlockSpec((CPS * C, D_MODEL), lambda c: (c, j))
    row = pl.BlockSpec((CPS * C, D_MODEL), lambda c: (c, 0))
    return pl.pallas_call(
        body,
        out_shape=(jax.ShapeDtypeStruct((T, D_MODEL), BF16), jax.ShapeDtypeStruct((T, D_MODEL), F32),
                   jax.ShapeDtypeStruct((T // C, H, HD, HD), F32)),
        grid=(T // (CPS * C),),
        in_specs=[col(0), col(1), col(2), col(3), pl.BlockSpec((1, D_MODEL), lambda c: (0, 0)),
                  pl.BlockSpec((1, HD), lambda c: (0, 0))],
        out_specs=(row, row, pl.BlockSpec((CPS, H, HD, HD), lambda c: (c, 0, 0, 0))),
        scratch_shapes=[pltpu.VMEM((H, HD, HD), F32)],
        compiler_params=_params("arbitrary"), name=name)(proj, proj, proj, proj, lb, gain)


def _hgrn_bwd(proj, o_pre, d_og, states, lb, gain, name):
    T = proj.shape[0]
    C = HGRN_CHUNK
    CPS = HGRN_STEP_CHUNKS
    H, HD = HGRN_HEADS, HGRN_DIM
    NC = T // (CPS * C)

    def body(q_ref, f_ref, i_ref, g_ref, o_ref, dog_ref, st_ref, lb_ref, gain_ref,
             dproj_ref, dlb_ref, dgain_ref, ds_scr, dq_all, dk_all, db_all):
        @pl.when(pl.program_id(0) == 0)
        def _():
            ds_scr[...] = jnp.zeros_like(ds_scr)
            dlb_ref[...] = jnp.zeros_like(dlb_ref)
            dgain_ref[...] = jnp.zeros_like(dgain_ref)

        lbv = lb_ref[...]
        causal = _tri(C, True)
        last_row = lax.broadcasted_iota(jnp.int32, (C, HD), 0) == C - 1
        gain_v = gain_ref[...]
        heads = [slice(h * HD, (h + 1) * HD) for h in range(H)]
        hs = range(H)
        ds_t = [ds_scr[h] for h in hs]
        dgain = None
        for cc in reversed(range(CPS)):
            rows = slice(cc * C, (cc + 1) * C)
            dq_scr, dk_scr, db_scr = dq_all.at[cc], dk_all.at[cc], db_all.at[cc]
            q_raw = q_ref[rows, :]
            gt = _hgrn_gates(q_raw, f_ref[rows, :], lbv)
            o = [o_ref[rows, sl] for sl in heads]
            rstd = [lax.rsqrt(jnp.mean(x * x, axis=-1, keepdims=True) + NORM_EPS) for x in o]
            n = [x * r for x, r in zip(o, rstd)]
            g_raw = [g_ref[rows, sl] for sl in heads]
            sg = [_sigmoid_gate(x) for x in g_raw]
            d_out = [dog_ref[rows, sl] for sl in heads]
            dy = [d * (g * s) for d, g, s in zip(d_out, g_raw, sg)]
            dn = [x * gain_v for x in dy]
            do = [(rstd[h] * (dn[h] - n[h] * jnp.mean(dn[h] * n[h], axis=-1, keepdims=True))).astype(BF16) for h in hs]
            for h in hs:
                dgain = dy[h] * n[h] if dgain is None else dgain + dy[h] * n[h]
            for h, sl in enumerate(heads):
                dproj_ref[rows, 3 * D_MODEL + h * HD:3 * D_MODEL + (h + 1) * HD] = (
                    d_out[h] * n[h] * gain_v * (sg[h] * (1.0 + g_raw[h] * (1.0 - sg[h])))).astype(BF16)
            q_ab, k_bb = gt["q_a"].astype(BF16), gt["k_b"].astype(BF16)
            q_hb, k_tb = gt["q_hat"].astype(BF16), gt["k_til"].astype(BF16)
            v = i_ref[rows, :].astype(BF16)
            s_t = [st_ref[cc, h] for h in hs]
            ds_b = [x.astype(BF16) for x in ds_t]
            p = [jnp.where(causal, _dot(q_ab[:, sl], k_bb[:, sl], NT), 0.0).astype(BF16) for sl in heads]
            dp = [jnp.where(causal, _dot(do[h], v[:, sl], NT), 0.0).astype(BF16) for h, sl in enumerate(heads)]
            dv = [_dot(p[h], do[h], TN) + _dot(k_tb[:, sl], ds_b[h], NT) for h, sl in enumerate(heads)]
            dq_a = [_dot(dp[h], k_bb[:, sl], NN) for h, sl in enumerate(heads)]
            dk_b = [_dot(dp[h], q_ab[:, sl], TN) for h, sl in enumerate(heads)]
            dq_hat = [_dot(do[h], s_t[h].astype(BF16), NN) for h in hs]
            dk_til = [_dot(v[:, sl], ds_b[h], NN) for h, sl in enumerate(heads)]
            ds_new = [_dot(do[h], q_hb[:, sl], TN) + gt["e_last"][:, sl] * ds_t[h] for h, sl in enumerate(heads)]
            for h, sl in enumerate(heads):
                k_til = gt["k_til"][:, sl]
                db_last = jnp.sum(ds_t[h] * gt["e_last"][:, sl] * s_t[h], axis=0, keepdims=True) + jnp.sum(
                    dk_til[h] * k_til, axis=0, keepdims=True)
                dproj_ref[rows, 2 * D_MODEL + h * HD:2 * D_MODEL + (h + 1) * HD] = dv[h].astype(BF16)
                dq_scr[:, sl] = dq_a[h] * gt["e_a"][:, sl] + dq_hat[h] * gt["e_q"][:, sl]
                dk_scr[:, sl] = dk_b[h] * gt["e_b"][:, sl] + dk_til[h] * gt["e_k"][:, sl]
                db = (dq_a[h] * q_ab[:, sl].astype(F32) + dq_hat[h] * gt["q_hat"][:, sl]
                      - dk_b[h] * k_bb[:, sl].astype(F32) - dk_til[h] * k_til)
                db_scr[:, sl] = db + jnp.where(last_row, db_last, 0.0)
            dlogf = _running_sum(db_scr[...], False)
            sig_f, forget, sig_q = gt["sig_f"], gt["forget"], gt["sig_q"]
            dforget = dlogf / forget - dk_scr[...]
            dproj_ref[rows, D_MODEL:2 * D_MODEL] = (dforget * (1.0 - lbv) * sig_f * (1.0 - sig_f)).astype(BF16)
            dlb_ref[...] += jnp.sum(dforget * (1.0 - sig_f), axis=0, keepdims=True)
            dproj_ref[rows, 0:D_MODEL] = (dq_scr[...] * (sig_q * (1.0 + q_raw * (1.0 - sig_q)))).astype(BF16)
            ds_t = ds_new
        dgain_ref[...] += jnp.sum(dgain, axis=0, keepdims=True)
        for h in hs:
            ds_scr[h] = ds_t[h]

    col = lambda j: pl.BlockSpec((CPS * C, D_MODEL), lambda c: (NC - 1 - c, j))
    row = pl.BlockSpec((CPS * C, D_MODEL), lambda c: (NC - 1 - c, 0))
    return pl.pallas_call(
        body,
        out_shape=(jax.ShapeDtypeStruct((T, 4 * D_MODEL), BF16), jax.ShapeDtypeStruct((1, D_MODEL), F32),
                   jax.ShapeDtypeStruct((1, HD), F32)),
        grid=(NC,),
        in_specs=[col(0), col(1), col(2), col(3), row, row,
                  pl.BlockSpec((CPS, H, HD, HD), lambda c: (NC - 1 - c, 0, 0, 0)),
                  pl.BlockSpec((1, D_MODEL), lambda c: (0, 0)), pl.BlockSpec((1, HD), lambda c: (0, 0))],
        out_specs=(pl.BlockSpec((CPS * C, 4 * D_MODEL), lambda c: (NC - 1 - c, 0)),
                   pl.BlockSpec((1, D_MODEL), lambda c: (0, 0)), pl.BlockSpec((1, HD), lambda c: (0, 0))),
        scratch_shapes=[pltpu.VMEM((H, HD, HD), F32)] + [pltpu.VMEM((CPS, C, D_MODEL), F32)] * 3,
        compiler_params=_params("arbitrary"), name=name)(proj, proj, proj, proj, o_pre, d_og, states, lb, gain)


def _attn_masks():
    r = lax.broadcasted_iota(jnp.int32, (ATTN_BLOCK, ATTN_BLOCK), 0)
    c = lax.broadcasted_iota(jnp.int32, (ATTN_BLOCK, ATTN_BLOCK), 1)
    return c >= r, c <= r


def _attn_fwd(qkv, dilation, name):
    T = qkv.shape[0]
    nb = T // dilation // ATTN_BLOCK
    W = ATTN_GROUP_WIDTH
    B = ATTN_BLOCK
    scale = ATTN_DIM ** -0.5
    qb = 2 if nb % 2 == 0 else 1
    steps = nb // qb

    def body(q_ref, kp_ref, kc_ref, vp_ref, vc_ref, o_ref, lse_ref):
        no_prev = jnp.where(pl.program_id(1) > 0, 0.0, NEG_BIG)
        m_prev, m_cur = _attn_masks()
        ones = jnp.ones((B, ATTN_DIM), BF16)
        items = []
        for j in range(qb):
            for h in range(ATTN_GROUP_HEADS):
                sl = slice(h * ATTN_DIM, (h + 1) * ATTN_DIM)
                rows = slice(j * B, (j + 1) * B)
                if j == 0:
                    items.append((rows, sl, kp_ref[:, sl], vp_ref[:, sl], no_prev))
                else:
                    before = slice((j - 1) * B, j * B)
                    items.append((rows, sl, kc_ref[before, sl], vc_ref[before, sl], 0.0))
        s_p = [jnp.where(m_prev, _dot(q_ref[rows, sl], k_p, NT) * scale + bias, NEG_BIG)
               for rows, sl, k_p, _, bias in items]
        s_c = [jnp.where(m_cur, _dot(q_ref[rows, sl], kc_ref[rows, sl], NT) * scale, NEG_BIG)
               for rows, sl, _, _, _ in items]
        m = [jnp.max(jnp.maximum(a, b), axis=-1, keepdims=True) for a, b in zip(s_p, s_c)]
        p_p = [jnp.exp(a - mx).astype(BF16) for a, mx in zip(s_p, m)]
        p_c = [jnp.exp(b - mx).astype(BF16) for b, mx in zip(s_c, m)]
        l = [_dot(a, ones, NN) + _dot(b, ones, NN) for a, b in zip(p_p, p_c)]
        acc = [_dot(a, v_p, NN) + _dot(b, vc_ref[rows, sl], NN)
               for a, b, (rows, sl, _, v_p, _) in zip(p_p, p_c, items)]
        for (rows, sl, _, _, _), a, lv, mx in zip(items, acc, l, m):
            o_ref[rows, sl] = a / lv
            lse_ref[rows, sl] = mx + jnp.log(lv)

    cur = lambda col: pl.BlockSpec((qb * B, W), lambda s, n: (s * steps + n, col))
    prev = lambda col: pl.BlockSpec((B, W), lambda s, n: (s * nb + jnp.maximum(qb * n - 1, 0), col))
    out = pl.BlockSpec((qb * B, W), lambda s, n: (s * steps + n, 0))
    return pl.pallas_call(
        body, out_shape=(jax.ShapeDtypeStruct((T, W), F32),) * 2, grid=(dilation, steps),
        in_specs=[cur(0), prev(1), cur(1), prev(2), cur(2)],
        out_specs=(out, out), compiler_params=_params("parallel", "arbitrary"), name=name)(qkv, qkv, qkv, qkv, qkv)


def _attn_bwd(qkv, d_out, lse, delta, cos, sin, dilation, name):
    T = qkv.shape[0]
    nb = T // dilation // ATTN_BLOCK
    W = ATTN_GROUP_WIDTH
    scale = ATTN_DIM ** -0.5

    def unrope(x, cos_v, sin_v):
        return x * cos_v + pltpu.roll(x * sin_v, ATTN_DIM // 2, 1)

    def body(q_ref, kp_ref, kc_ref, vp_ref, vc_ref, do_ref, lse_ref, dl_ref, cos_ref, sin_ref,
             out_ref, dq_scr, dk_scr, dv_scr):
        n = pl.program_id(1)
        cos_v, sin_v = cos_ref[...], sin_ref[...]

        @pl.when(n > 0)
        def _():
            for h in range(ATTN_GROUP_HEADS):
                sl = slice(h * ATTN_DIM, (h + 1) * ATTN_DIM)
                out_ref[:, sl] = unrope(dq_scr[:, sl], cos_v, sin_v).astype(BF16)

        @pl.when(n == nb)
        def _():
            for h in range(ATTN_GROUP_HEADS):
                sl = slice(h * ATTN_DIM, (h + 1) * ATTN_DIM)
                out_ref[:, W + h * ATTN_DIM:W + (h + 1) * ATTN_DIM] = unrope(dk_scr[:, sl], cos_v, sin_v).astype(BF16)
                out_ref[:, 2 * W + h * ATTN_DIM:2 * W + (h + 1) * ATTN_DIM] = dv_scr[:, sl].astype(BF16)

        @pl.when(n == 0)
        def _():
            dk_scr[...] = jnp.zeros_like(dk_scr)
            dv_scr[...] = jnp.zeros_like(dv_scr)

        @pl.when(n < nb)
        def _():
            has_prev = n > 0
            no_prev = jnp.where(has_prev, 0.0, NEG_BIG)
            m_prev, m_cur = _attn_masks()
            heads = [slice(h * ATTN_DIM, (h + 1) * ATTN_DIM) for h in range(ATTN_GROUP_HEADS)]
            s_p = [_dot(q_ref[:, sl], kp_ref[:, sl], NT) for sl in heads]
            s_c = [_dot(q_ref[:, sl], kc_ref[:, sl], NT) for sl in heads]
            dp_p = [_dot(do_ref[:, sl], vp_ref[:, sl], NT) for sl in heads]
            dp_c = [_dot(do_ref[:, sl], vc_ref[:, sl], NT) for sl in heads]
            p_p = [jnp.where(m_prev, jnp.exp(s * scale - lse_ref[:, sl] + no_prev), 0.0) for s, sl in zip(s_p, heads)]
            p_c = [jnp.where(m_cur, jnp.exp(s * scale - lse_ref[:, sl]), 0.0) for s, sl in zip(s_c, heads)]
            ds_p = [(p * (dp - dl_ref[:, sl]) * scale).astype(BF16) for p, dp, sl in zip(p_p, dp_p, heads)]
            ds_c = [(p * (dp - dl_ref[:, sl]) * scale).astype(BF16) for p, dp, sl in zip(p_c, dp_c, heads)]
            p_p = [p.astype(BF16) for p in p_p]
            p_c = [p.astype(BF16) for p in p_c]
            dk_prev = [dk_scr[:, sl] + _dot(ds, q_ref[:, sl], TN) for ds, sl in zip(ds_p, heads)]
            dv_prev = [dv_scr[:, sl] + _dot(p, do_ref[:, sl], TN) for p, sl in zip(p_p, heads)]
            dq = [_dot(a, kp_ref[:, sl], NN) + _dot(b, kc_ref[:, sl], NN) for a, b, sl in zip(ds_p, ds_c, heads)]
            dk_cur = [_dot(ds, q_ref[:, sl], TN) for ds, sl in zip(ds_c, heads)]
            dv_cur = [_dot(p, do_ref[:, sl], TN) for p, sl in zip(p_c, heads)]
            for h, sl in enumerate(heads):
                out_ref[:, W + h * ATTN_DIM:W + (h + 1) * ATTN_DIM] = unrope(dk_prev[h], cos_v, sin_v).astype(BF16)
                out_ref[:, 2 * W + h * ATTN_DIM:2 * W + (h + 1) * ATTN_DIM] = dv_prev[h].astype(BF16)
                dq_scr[:, sl] = dq[h]
                dk_scr[:, sl] = dk_cur[h]
                dv_scr[:, sl] = dv_cur[h]

    def cur(n):
        return jnp.minimum(n, nb - 1)

    def late(n):
        return jnp.maximum(n - 1, 0)

    qkv_blk = lambda col, prev: pl.BlockSpec(
        (ATTN_BLOCK, W), lambda s, n: (s * nb + (jnp.maximum(cur(n) - 1, 0) if prev else cur(n)), col))
    row = pl.BlockSpec((ATTN_BLOCK, W), lambda s, n: (s * nb + cur(n), 0))
    tab = pl.BlockSpec((ATTN_BLOCK, ATTN_DIM), lambda s, n: (s * nb + late(n), 0))
    return pl.pallas_call(
        body, out_shape=jax.ShapeDtypeStruct((T, 3 * W), BF16), grid=(dilation, nb + 1),
        in_specs=[qkv_blk(0, False), qkv_blk(1, True), qkv_blk(1, False), qkv_blk(2, True), qkv_blk(2, False),
                  row, row, row, tab, tab],
        out_specs=pl.BlockSpec((ATTN_BLOCK, 3 * W), lambda s, n: (s * nb + late(n), 0)),
        scratch_shapes=[pltpu.VMEM((ATTN_BLOCK, W), F32)] * 3,
        compiler_params=_params("parallel", "arbitrary"), name=name)(
            qkv, qkv, qkv, qkv, qkv, d_out, lse, delta, cos, sin)


PERM_TILE = 512
LANES = 128


def _residue_view(x, d):
    return x if d == 1 else x.reshape(d, x.shape[0] // d, x.shape[1])


def _residue_spec(d, tm, cols):
    if d == 1:
        return pl.BlockSpec((tm, cols), lambda i: (i, 0))
    return pl.BlockSpec((d, tm // d, cols), lambda i: (0, i, 0))


def _residue_shape(T, d, cols, dtype):
    return jax.ShapeDtypeStruct((T, cols) if d == 1 else (d, T // d, cols), dtype)


def _class_rows(r, d, tm):
    return pl.ds(r, tm // d, stride=d)


def _attn_norm(h, gain, cos, sin, name):
    T = h.shape[0]
    tm = _pick_tile(T, PERM_TILE, 16 * max(ATTN_DILATIONS))
    dils = ATTN_DILATIONS

    def body(h_ref, g_ref, cos_ref, sin_ref, *refs):
        u_refs, c_refs, s_refs, u_scr = refs[0:3], refs[3:6], refs[6:9], refs[9]
        hv = h_ref[...]
        rstd = lax.rsqrt(jnp.mean(hv * hv, axis=-1, keepdims=True) + NORM_EPS)
        u = hv * rstd * g_ref[...]
        for j in range(D_MODEL // LANES):
            u_scr[j] = u[:, j * LANES:(j + 1) * LANES]
        for d, u_ref, c_ref, s_ref in zip(dils, u_refs, c_refs, s_refs):
            if d == 1:
                u_ref[...] = u.astype(BF16)
                c_ref[...] = cos_ref[...]
                s_ref[...] = sin_ref[...]
                continue
            for r in range(d):
                rows = _class_rows(r, d, tm)
                for j in range(D_MODEL // LANES):
                    u_ref[r, :, j * LANES:(j + 1) * LANES] = u_scr.at[j][rows, :].astype(BF16)
                c_ref[r] = cos_ref[rows, :]
                s_ref[r] = sin_ref[rows, :]

    row = pl.BlockSpec((tm, D_MODEL), lambda i: (i, 0))
    tab = pl.BlockSpec((tm, ATTN_DIM), lambda i: (i, 0))
    res = pl.pallas_call(
        body,
        out_shape=([_residue_shape(T, d, D_MODEL, BF16) for d in dils]
                   + [_residue_shape(T, d, ATTN_DIM, F32) for d in dils] * 2),
        grid=(T // tm,), in_specs=[row, pl.BlockSpec((1, D_MODEL), lambda i: (0, 0)), tab, tab],
        out_specs=([_residue_spec(d, tm, D_MODEL) for d in dils] + [_residue_spec(d, tm, ATTN_DIM) for d in dils] * 2),
        scratch_shapes=[pltpu.VMEM((D_MODEL // LANES, tm, LANES), F32)],
        compiler_params=_params("parallel"), name=name)(h, gain, cos, sin)
    flat = [r.reshape(T, r.shape[-1]) for r in res]
    return flat[0:3], flat[3:6], flat[6:9]


def _attn_merge_fwd(outs, lses, name):
    T = outs[0].shape[0]
    W = ATTN_GROUP_WIDTH
    tm = _pick_tile(T, PERM_TILE, 16 * max(ATTN_DILATIONS))
    dils = ATTN_DILATIONS

    def body(*refs):
        o_refs, l_refs, oc_ref, lse_refs = refs[0:3], refs[3:6], refs[6], refs[7:10]
        o_scr, l_scr, t_scr = refs[10:13]
        nh = ATTN_GROUP_HEADS
        for g, d in enumerate(dils):
            for j in range(nh):
                lanes = slice(j * LANES, (j + 1) * LANES)
                if d == 1:
                    o_scr[g * nh + j] = o_refs[g][:, lanes]
                    l_scr[g * nh + j] = l_refs[g][:, lanes]
                    continue
                for r in range(d):
                    rows = _class_rows(r, d, tm)
                    o_scr.at[g * nh + j][rows, :] = o_refs[g][r, :, lanes]
                    l_scr.at[g * nh + j][rows, :] = l_refs[g][r, :, lanes]
        for j in range(nh):
            lanes = slice(j * LANES, (j + 1) * LANES)
            ls = [l_scr[g * nh + j] for g in range(3)]
            m = jnp.maximum(jnp.maximum(ls[0], ls[1]), ls[2])
            tot = m + jnp.log(jnp.exp(ls[0] - m) + jnp.exp(ls[1] - m) + jnp.exp(ls[2] - m))
            t_scr[j] = tot
            for g, d in enumerate(dils):
                oc_ref[:, g * W + j * LANES:g * W + (j + 1) * LANES] = (
                    o_scr[g * nh + j] * jnp.exp(ls[g] - tot)).astype(BF16)
                if d == 1:
                    lse_refs[g][:, lanes] = tot
                    continue
                for r in range(d):
                    lse_refs[g][r, :, lanes] = t_scr.at[j][_class_rows(r, d, tm), :]

    in_blk = [_residue_spec(d, tm, W) for d in dils]
    n_blk = 3 * ATTN_GROUP_HEADS
    res = pl.pallas_call(
        body, out_shape=[jax.ShapeDtypeStruct((T, 3 * W), BF16)] + [_residue_shape(T, d, W, F32) for d in dils],
        grid=(T // tm,), in_specs=in_blk * 2,
        out_specs=[pl.BlockSpec((tm, 3 * W), lambda i: (i, 0))] + in_blk,
        scratch_shapes=[pltpu.VMEM((n_blk, tm, LANES), F32), pltpu.VMEM((n_blk, tm, LANES), F32),
                        pltpu.VMEM((ATTN_GROUP_HEADS, tm, LANES), F32)],
        compiler_params=_params("parallel"), name=name)(
            *[_residue_view(o, d) for o, d in zip(outs, dils)], *[_residue_view(l, d) for l, d in zip(lses, dils)])
    return res[0], [r.reshape(T, W) for r in res[1:]]


def _attn_merge_bwd(d_oc, oc, name):
    T = d_oc.shape[0]
    W = ATTN_GROUP_WIDTH
    tm = _pick_tile(T, PERM_TILE, 16 * max(ATTN_DILATIONS))
    dils = ATTN_DILATIONS

    def body(d_ref, o_ref, *refs):
        delta_refs, db_refs, dl_scr, d_scr = refs[0:3], refs[3:6], refs[6], refs[7]
        nh = ATTN_GROUP_HEADS
        for j in range(nh):
            tot = jnp.zeros((tm, 1), F32)
            for g in range(3):
                cols = slice(g * W + j * LANES, g * W + (j + 1) * LANES)
                d_blk = d_ref[:, cols]
                d_scr[g * nh + j] = d_blk
                tot = tot + jnp.sum(d_blk * o_ref[:, cols].astype(F32), axis=-1, keepdims=True)
            dl_scr[j] = jnp.broadcast_to(tot, (tm, LANES))
        for g, d in enumerate(dils):
            for j in range(nh):
                lanes = slice(j * LANES, (j + 1) * LANES)
                if d == 1:
                    delta_refs[g][:, lanes] = dl_scr[j]
                    db_refs[g][:, lanes] = d_scr[g * nh + j].astype(BF16)
                    continue
                for r in range(d):
                    rows = _class_rows(r, d, tm)
                    delta_refs[g][r, :, lanes] = dl_scr.at[j][rows, :]
                    db_refs[g][r, :, lanes] = d_scr.at[g * nh + j][rows, :].astype(BF16)

    wide = pl.BlockSpec((tm, 3 * W), lambda i: (i, 0))
    out_blk = [_residue_spec(d, tm, W) for d in dils]
    res = pl.pallas_call(
        body, out_shape=[_residue_shape(T, d, W, F32) for d in dils] + [_residue_shape(T, d, W, BF16) for d in dils],
        grid=(T // tm,), in_specs=[wide, wide], out_specs=out_blk * 2,
        scratch_shapes=[pltpu.VMEM((ATTN_GROUP_HEADS, tm, LANES), F32),
                        pltpu.VMEM((3 * ATTN_GROUP_HEADS, tm, LANES), F32)],
        compiler_params=_params("parallel"), name=name)(d_oc, oc)
    flat = [r.reshape(T, W) for r in res]
    return flat[0:3], flat[3:6]


def _rope_tables(T):
    inv_freq = 1.0 / (ROPE_THETA ** (jnp.arange(0, ATTN_DIM, 2, dtype=F32) / ATTN_DIM))
    ang = jnp.arange(T, dtype=F32)[:, None] * inv_freq[None, :]
    cos, sin = jnp.cos(ang), jnp.sin(ang)
    return jnp.concatenate([cos, cos], axis=1), jnp.concatenate([-sin, sin], axis=1)


WEIGHT_GROUPS = {"hgrn": ("hgrn_in", "hgrn_out"), "ffn0": ("ffn_in0", "ffn_down0"),
                 "attn": ("qkv", "attn_out"), "ffn1": ("ffn_in1", "ffn_down1")}


def _local_step(x, target, norm_mix, norm_ffn, lb, out_gain, final_gain, fetch, publish):
    T = x.shape[0]
    g_mix = [norm_mix[0:1], norm_mix[1:2]]
    g_ffn = [norm_ffn[0:1], norm_ffn[1:2]]
    w = {}

    def whole(name):
        return [(w[name], w[name].shape[0], 0)]

    def qkv_parts(g):
        return [(w["qkv"], ATTN_GROUP_WIDTH, 3 * j + g) for j in range(3)]

    def ffn_fwd(h, layer, head=None):
        w.update(fetch(f"ffn{layer}"))
        n, gate, up, a = _ffn_in(h, g_ffn[layer], w[f"ffn_in{layer}"], f"ffn{layer}_in")
        out = _mm_nn([a], [whole(f"ffn_down{layer}")], h, name=f"ffn{layer}_down", head=head)
        return out, (n, gate, up, a)

    def ffn_bwd(h, saved, dh, dhb, layer):
        n, gate, up, a = saved
        w_in = w[f"ffn_in{layer}"]
        dgate, dup = _ffn_down_dx(dhb, w[f"ffn_down{layer}"], gate, up, f"ffn{layer}_down_dx")
        grad_in = _mm_tn(dgate, n, name=f"ffn{layer}_in_dw_gate", rows=2 * D_FF)
        grad_in = _mm_tn(dup, n, name=f"ffn{layer}_in_dw_up", into=grad_in, row_tile=D_FF // GRAD_TILE, rows=2 * D_FF)
        grads = {f"ffn_down{layer}": _mm_tn(a, dhb, name=f"ffn{layer}_down_dw"), f"ffn_in{layer}": grad_in}
        publish(f"ffn{layer}", grads)
        return _mm_nn([dgate, dup], [[(w_in, D_FF, 0)], [(w_in, D_FF, 1)]], dh, name=f"ffn{layer}_in_dx",
                      norm=(h, g_ffn[layer]))

    u0 = _rms_fwd(x, g_mix[0], "hgrn_norm")
    w.update(fetch("hgrn"))
    proj = _mm_nt(u0, whole("hgrn_in"), out_dtype=F32, name="hgrn_in")
    og, o_pre, states = _hgrn_fwd(proj, lb, out_gain, "hgrn_fwd")
    h1 = _mm_nn([og], [whole("hgrn_out")], x, name="hgrn_out")
    h2, ffn0 = ffn_fwd(h1, 0)

    cos, sin = _rope_tables(T)
    u1_g, cos_g, sin_g = _attn_norm(h2, g_mix[1], cos, sin, "attn_norm")
    w.update(fetch("attn"))
    qkv_g, outs, lses = [], [], []
    for g, d in enumerate(ATTN_DILATIONS):
        qkv_g.append(_mm_nt(u1_g[g], qkv_parts(g), out_dtype=BF16, name=f"attn_qkv{g}",
                            rope=(cos_g[g], sin_g[g], 2)))
        o_g, lse_g = _attn_fwd(qkv_g[g], d, f"attn_fwd{g}")
        outs.append(o_g)
        lses.append(lse_g)
    oc, lse_all = _attn_merge_fwd(outs, lses, "attn_merge")
    h3 = _mm_nn([oc], [whole("attn_out")], h2, name="attn_out")
    (dh4, dh4b, d_final, loss_part), ffn1 = ffn_fwd(h3, 1, head=(target, final_gain))
    dh3, dh3b, d_ffn1 = ffn_bwd(h3, ffn1, dh4, dh4b, 1)

    d_oc = _mm_nt(dh3b, whole("attn_out"), out_dtype=F32, name="attn_out_dx")
    grad_attn_out = _mm_tn(oc, dh3b, name="attn_out_dw")
    delta, d_ocb = _attn_merge_bwd(d_oc, oc, "attn_merge_bwd")
    du1, qkv_pieces = [], []
    for g, d in enumerate(ATTN_DILATIONS):
        dqkv = _attn_bwd(qkv_g[g], d_ocb[g], lse_all[g], delta[g], cos_g[g], sin_g[g], d, f"attn_bwd{g}")
        qkv_pieces.append(_mm_tn(dqkv, u1_g[g], name=f"attn_qkv_dw{g}"))
        du1.append(_mm_nn([dqkv], [qkv_parts(g)], None, name=f"attn_qkv_dx{g}"))
    grad_qkv = jnp.stack([p.reshape(3, ATTN_GROUP_WIDTH, D_MODEL) for p in qkv_pieces], axis=1).reshape(
        3 * ATTN_WIDTH, D_MODEL)
    publish("attn", {"qkv": grad_qkv, "attn_out": grad_attn_out})
    dh2, dh2b, d_mix1 = _rms_bwd(h2, g_mix[1], du1, dh3, "attn_norm_bwd", ATTN_DILATIONS)

    dh1, dh1b, d_ffn0 = ffn_bwd(h1, ffn0, dh2, dh2b, 0)

    d_og = _mm_nt(dh1b, whole("hgrn_out"), out_dtype=F32, name="hgrn_out_dx")
    grad_hgrn_out = _mm_tn(og, dh1b, name="hgrn_out_dw")
    dproj, d_lb, d_out_gain = _hgrn_bwd(proj, o_pre, d_og, states, lb, out_gain, "hgrn_bwd")
    publish("hgrn", {"hgrn_in": _mm_tn(dproj, u0, name="hgrn_in_dw"), "hgrn_out": grad_hgrn_out})
    dx, _, d_mix0 = _mm_nn([dproj], [whole("hgrn_in")], dh1, name="hgrn_in_dx", norm=(x, g_mix[0]))

    small = dict(norm_mix0=d_mix0, norm_mix1=d_mix1, norm_ffn0=d_ffn0, norm_ffn1=d_ffn1, lb=d_lb,
                 out_gain=d_out_gain, final=d_final, loss=loss_part)
    return dx, small


WEIGHT_NAMES = ("hgrn_in", "hgrn_out", "qkv", "attn_out", "ffn_in0", "ffn_in1", "ffn_down0", "ffn_down1")
MESH_IDS = pl.DeviceIdType.MESH
HBM_SPEC = pl.BlockSpec(memory_space=pl.ANY)


N_PEERS = N_DEV - 1
PEER_OFFSETS = [(dx, dy, dc) for dx in (0, 1) for dy in (0, 1) for dc in (0, 1)][1:]


def _mesh_place():
    x, y, c = lax.axis_index("x"), lax.axis_index("y"), lax.axis_index("c")
    peers = []
    for dx, dy, dc in PEER_OFFSETS:
        px, py, pc = (1 - x if dx else x), (1 - y if dy else y), (1 - c if dc else c)
        peers.append(((px, py, pc), 4 * px + 2 * py + pc))
    return 4 * x + 2 * y + c, peers


def _exchange_launch(srcs, scatter, collective_id, name):
    n = len(srcs)
    src_refs = [jax.new_ref(s, memory_space=pltpu.MemorySpace.HBM) for s in srcs]
    land_refs = [jax.empty_ref(jax.ShapeDtypeStruct(s.shape if scatter else (N_DEV,) + s.shape, s.dtype),
                               memory_space=pltpu.MemorySpace.HBM) for s in srcs]

    @pl.kernel(mesh=plsc.ScalarSubcoreMesh(axis_name="sequencer", num_cores=1), name=name,
               scratch_types=(pltpu.SemaphoreType.DMA((n * N_PEERS,)), pltpu.SemaphoreType.DMA((n * N_PEERS,)),
                              pltpu.SemaphoreType.DMA((n,))),
               compiler_params=pltpu.CompilerParams(collective_id=collective_id))
    def launch(send_sems, recv_sems, local_sems):
        me, peers = _mesh_place()
        barrier = pltpu.get_barrier_semaphore()
        for peer, _ in peers:
            pl.semaphore_signal(barrier, inc=1, device_id=peer, device_id_type=MESH_IDS)
        pl.semaphore_wait(barrier, N_PEERS)
        own = [pltpu.make_async_copy(src_refs[w].at[me] if scatter else src_refs[w], land_refs[w].at[me],
                                     local_sems.at[w]) for w in range(n)]
        for cp in own:
            cp.start()
        copies = [pltpu.make_async_remote_copy(
            src_ref=src_refs[w].at[pid] if scatter else src_refs[w], dst_ref=land_refs[w].at[me],
            send_sem=send_sems.at[w * N_PEERS + k], recv_sem=recv_sems.at[w * N_PEERS + k],
            device_id=peer, device_id_type=MESH_IDS) for w in range(n) for k, (peer, pid) in enumerate(peers)]
        for cp in copies:
            cp.start()
        for cp in copies:
            cp.wait()
        for cp in own:
            cp.wait()

    launch()
    return land_refs


def _gather_small(block, name):
    def body(in_ref, out_ref, send_sems, recv_sems, local_sem):
        me, peers = _mesh_place()
        own = pltpu.make_async_copy(in_ref, out_ref.at[me], local_sem)
        own.start()
        sends = [pltpu.make_async_remote_copy(
            src_ref=in_ref, dst_ref=out_ref.at[me], send_sem=send_sems.at[k], recv_sem=recv_sems.at[k],
            device_id=peer, device_id_type=MESH_IDS) for k, (peer, _) in enumerate(peers)]
        for cp in sends:
            cp.start()
        for cp in sends:
            cp.wait_recv()
        for cp in sends:
            cp.wait_send()
        own.wait()

    return pl.pallas_call(
        body, out_shape=jax.ShapeDtypeStruct((N_DEV,) + block.shape, block.dtype),
        in_specs=[HBM_SPEC], out_specs=HBM_SPEC,
        scratch_shapes=[pltpu.SemaphoreType.DMA((N_PEERS,)), pltpu.SemaphoreType.DMA((N_PEERS,)),
                        pltpu.SemaphoreType.DMA],
        name=name)(block)


def _sum_blocks(recv, name):
    rows = recv.shape[1]
    tr = _pick_tile(rows, 256, 16)

    def body(r_ref, g_ref):
        acc = r_ref[0].astype(F32)
        for j in range(1, N_DEV):
            acc = acc + r_ref[j].astype(F32)
        g_ref[...] = acc

    return pl.pallas_call(
        body, out_shape=jax.ShapeDtypeStruct((rows, D_MODEL), F32), grid=(rows // tr,),
        in_specs=[pl.BlockSpec((N_DEV, tr, D_MODEL), lambda i: (0, i, 0))],
        out_specs=pl.BlockSpec((tr, D_MODEL), lambda i: (i, 0)),
        compiler_params=_params("parallel"), name=name)(recv)


def _adamw_math(w, g, m, v):
    m_new = ADAM_B1 * m + (1.0 - ADAM_B1) * g
    v_new = ADAM_B2 * v + (1.0 - ADAM_B2) * (g * g)
    m_hat = m_new / (1.0 - ADAM_B1 ** ADAM_STEP)
    v_hat = v_new / (1.0 - ADAM_B2 ** ADAM_STEP)
    delta = -ADAM_LR * (m_hat / (jnp.sqrt(v_hat) + ADAM_EPS) + ADAM_WD * w)
    return delta, m_new, v_new


def _adamw(w, g, m, v, name):
    rows, cols = w.shape
    tr = _pick_tile(rows, 256, 8)

    def body(w_ref, g_ref, m_ref, v_ref, d_ref, mo_ref, vo_ref):
        d_ref[...], mo_ref[...], vo_ref[...] = _adamw_math(w_ref[...], g_ref[...], m_ref[...], v_ref[...])

    blk = pl.BlockSpec((tr, cols), lambda i: (i, 0))
    return pl.pallas_call(
        body, out_shape=(jax.ShapeDtypeStruct((rows, cols), F32),) * 3, grid=(rows // tr,),
        in_specs=[blk] * 4, out_specs=(blk,) * 3, compiler_params=_params("parallel"), name=name)(w, g, m, v)


ROW_MIX, ROW_FFN, ROW_LB, ROW_OUT_GAIN, ROW_FINAL = 0, 2, 4, 7, 8
PART_MIX, PART_FFN, PART_LB, PART_OUT_GAIN, PART_FINAL, PART_LOSS = 0, 2, 4, 5, 6, 7


def _small_update(parts_all, w, m, v, name):
    def body(p_ref, w_ref, m_ref, v_ref, g_ref, d_ref, mo_ref, vo_ref, loss_ref):
        def total(row, n=1):
            tot = p_ref[0, row:row + n, :]
            for j in range(1, N_DEV):
                tot = tot + p_ref[j, row:row + n, :]
            return tot

        logits = [w_ref[ROW_LB + i:ROW_LB + i + 1, :] for i in range(3)]
        mx = jnp.maximum(jnp.maximum(logits[0], logits[1]), logits[2])
        ex = [jnp.exp(l - mx) for l in logits]
        den = ex[0] + ex[1] + ex[2]
        prob = [e / den for e in ex]
        d_lb = total(PART_LB)
        g_ref[...] = jnp.zeros_like(g_ref)
        g_ref[ROW_MIX:ROW_MIX + 2, :] = total(PART_MIX, 2)
        g_ref[ROW_FFN:ROW_FFN + 2, :] = total(PART_FFN, 2)
        for i in range(3):
            g_ref[ROW_LB + i:ROW_LB + i + 1, :] = prob[i] * ((d_lb if i == 0 else 0.0) - prob[0] * d_lb)
        g_ref[ROW_OUT_GAIN:ROW_OUT_GAIN + 1, :] = total(PART_OUT_GAIN)
        g_ref[ROW_FINAL:ROW_FINAL + 1, :] = total(PART_FINAL)
        d_ref[...], mo_ref[...], vo_ref[...] = _adamw_math(w_ref[...], g_ref[...], m_ref[...], v_ref[...])
        loss_ref[...] = jnp.sum(total(PART_LOSS), axis=-1, keepdims=True)

    packed = jax.ShapeDtypeStruct((16, D_MODEL), F32)
    return pl.pallas_call(
        body, out_shape=(packed, packed, packed, packed, jax.ShapeDtypeStruct((1, 1), F32)),
        compiler_params=pltpu.CompilerParams(vmem_limit_bytes=VMEM_LIMIT), name=name)(parts_all, w, m, v)


def _pack_small(norm_mix, norm_ffn, lb_logits, out_gain, final):
    pad = jnp.zeros((1, D_MODEL - HGRN_DIM), F32)
    return jnp.concatenate([norm_mix, norm_ffn, lb_logits, jnp.concatenate([out_gain, pad], axis=1),
                            final.reshape(1, D_MODEL), jnp.zeros((16 - ROW_FINAL - 1, D_MODEL), F32)], axis=0)


def _unpack_small(p):
    return (p[ROW_MIX:ROW_MIX + 2], p[ROW_FFN:ROW_FFN + 2], p[ROW_LB:ROW_LB + 3],
            p[ROW_OUT_GAIN:ROW_OUT_GAIN + 1, :HGRN_DIM], p[ROW_FINAL])


def _lower_bound(lb_logits, name):
    def body(l_ref, o_ref):
        logits = [l_ref[i:i + 1, :] for i in range(3)]
        mx = jnp.maximum(jnp.maximum(logits[0], logits[1]), logits[2])
        ex = [jnp.exp(l - mx) for l in logits]
        o_ref[...] = ex[0] / (ex[0] + ex[1] + ex[2])

    return pl.pallas_call(body, out_shape=jax.ShapeDtypeStruct((1, D_MODEL), F32), name=name)(lb_logits)


def kernel(x, norm_mix, norm_ffn, hgrn_w_in, hgrn_lb_logits, hgrn_out_norm, hgrn_w_out, attn_w_qkv, attn_w_out, ffn_w_in, ffn_w_down, final_norm, loss_target, m_norm_mix, m_norm_ffn, m_hgrn_w_in, m_hgrn_lb_logits, m_hgrn_out_norm, m_hgrn_w_out, m_attn_w_qkv, m_attn_w_out, m_ffn_w_in, m_ffn_w_down, m_final_norm, v_norm_mix, v_norm_ffn, v_hgrn_w_in, v_hgrn_lb_logits, v_hgrn_out_norm, v_hgrn_w_out, v_attn_w_qkv, v_attn_w_out, v_ffn_w_in, v_ffn_w_down, v_final_norm):
    col_sharded = {"hgrn_in": hgrn_w_in[0], "qkv": attn_w_qkv[0], "ffn_in0": ffn_w_in[0], "ffn_in1": ffn_w_in[1]}
    row_sharded = {"hgrn_out": hgrn_w_out[0], "attn_out": attn_w_out[0], "ffn_down0": ffn_w_down[0],
                   "ffn_down1": ffn_w_down[1]}
    gathering = {}
    for gi, (group, names) in enumerate(WEIGHT_GROUPS.items()):
        shards = [(col_sharded[n].T if n in col_sharded else row_sharded[n]).astype(BF16) for n in names]
        gathering[group] = _exchange_launch(shards, False, 1 + gi, f"weights_gather_{group}")

    def fetch(group):
        return {n: land[...].reshape(-1, D_MODEL) for n, land in zip(WEIGHT_GROUPS[group], gathering[group])}

    in_flight = {}

    def publish(group, grads):
        names = WEIGHT_GROUPS[group]
        parts = [grads[n].reshape(N_DEV, -1, D_MODEL) for n in names]
        in_flight[group] = _exchange_launch(parts, True, 1 + len(WEIGHT_GROUPS) + list(WEIGHT_GROUPS).index(group),
                                            f"grads_send_{group}")

    lb = _lower_bound(hgrn_lb_logits, "hgrn_lower_bound")
    grad_x, small = _local_step(x[0], loss_target[0], norm_mix, norm_ffn, lb, hgrn_out_norm,
                                final_norm.reshape(1, D_MODEL), fetch, publish)

    pad = jnp.zeros((1, D_MODEL - HGRN_DIM), F32)
    small_part = jnp.concatenate(
        [small["norm_mix0"], small["norm_mix1"], small["norm_ffn0"], small["norm_ffn1"], small["lb"],
         jnp.concatenate([small["out_gain"], pad], axis=1), small["final"], small["loss"]], axis=0)
    small_all = _gather_small(small_part, "small_grads_gather")
    received = {}
    for group in ("ffn1", "attn", "ffn0", "hgrn"):
        received.update(zip(WEIGHT_GROUPS[group], [land[...] for land in in_flight[group]]))

    masters = {"hgrn_in": (hgrn_w_in[0], m_hgrn_w_in[0], v_hgrn_w_in[0]),
               "hgrn_out": (hgrn_w_out[0], m_hgrn_w_out[0], v_hgrn_w_out[0]),
               "qkv": (attn_w_qkv[0], m_attn_w_qkv[0], v_attn_w_qkv[0]),
               "attn_out": (attn_w_out[0], m_attn_w_out[0], v_attn_w_out[0]),
               "ffn_in0": (ffn_w_in[0], m_ffn_w_in[0], v_ffn_w_in[0]),
               "ffn_in1": (ffn_w_in[1], m_ffn_w_in[1], v_ffn_w_in[1]),
               "ffn_down0": (ffn_w_down[0], m_ffn_w_down[0], v_ffn_w_down[0]),
               "ffn_down1": (ffn_w_down[1], m_ffn_w_down[1], v_ffn_w_down[1])}
    res = {}
    for n in WEIGHT_NAMES:
        g = _sum_blocks(received[n], f"{n}_grad_sum")
        if n in col_sharded:
            g = g.T
        wv, mv, vv = masters[n]
        res[n] = (g,) + tuple(_adamw(wv, g, mv, vv, f"{n}_adamw"))

    def single(n):
        return [t[None] for t in res[n]]

    def pair(n):
        return [jnp.stack([a, b]) for a, b in zip(res[n + "0"], res[n + "1"])]

    big = dict(hgrn_w_in=single("hgrn_in"), hgrn_w_out=single("hgrn_out"), attn_w_qkv=single("qkv"),
               attn_w_out=single("attn_out"), ffn_w_in=pair("ffn_in"), ffn_w_down=pair("ffn_down"))

    w_small = _pack_small(norm_mix, norm_ffn, hgrn_lb_logits, hgrn_out_norm, final_norm)
    m_small = _pack_small(m_norm_mix, m_norm_ffn, m_hgrn_lb_logits, m_hgrn_out_norm, m_final_norm)
    v_small = _pack_small(v_norm_mix, v_norm_ffn, v_hgrn_lb_logits, v_hgrn_out_norm, v_final_norm)
    g_s, d_s, m_s, v_s, loss = _small_update(small_all, w_small, m_small, v_small, "small_update")
    small_out = [_unpack_small(t) for t in (g_s, d_s, m_s, v_s)]

    def group(i):
        s = small_out[i]
        return (s[0], s[1], big["hgrn_w_in"][i], s[2], s[3], big["hgrn_w_out"][i], big["attn_w_qkv"][i],
                big["attn_w_out"][i], big["ffn_w_in"][i], big["ffn_w_down"][i], s[4])

    return (loss.reshape(()), grad_x[None], *group(0), *group(1), *group(2), *group(3))
```

```python
import functools

import jax
import jax.numpy as jnp
from jax import lax
from jax.experimental import pallas as pl
from jax.experimental.pallas import tpu as pltpu
from jax.experimental.pallas import tpu_sc as plsc

F32 = jnp.float32
BF16 = jnp.bfloat16

D_MODEL = 1024
N_DEV = 8
NORM_EPS = 1e-6

HGRN_HEADS = 8
HGRN_DIM = 128
HGRN_CHUNK = 64
HGRN_STEP_CHUNKS = 2
HGRN_EXP_CLAMP = 60.0

ATTN_DIM = 128
ATTN_BLOCK = 128
ATTN_GROUP_HEADS = 4
ATTN_GROUP_WIDTH = ATTN_GROUP_HEADS * ATTN_DIM
ATTN_DILATIONS = (1, 4, 16)
ATTN_WIDTH = 3 * ATTN_GROUP_WIDTH
ROPE_THETA = 10000.0
NEG_BIG = -1e30

D_FF = 2816

ADAM_LR = 0.001
ADAM_B1 = 0.9
ADAM_B2 = 0.999
ADAM_EPS = 1e-08
ADAM_WD = 0.01
ADAM_STEP = 10

VMEM_LIMIT = 48 * 1024 * 1024

NT = (((1,), (1,)), ((), ()))
NN = (((1,), (0,)), ((), ()))
TN = (((0,), (0,)), ((), ()))


def _dot(a, b, dims):
    return lax.dot_general(a, b, dims, preferred_element_type=F32)


def _params(*sem):
    return pltpu.CompilerParams(dimension_semantics=sem, vmem_limit_bytes=VMEM_LIMIT)


def _pick_tile(n, cap, mult):
    best = None
    for t in range(mult, min(n, cap) + 1, mult):
        if n % t == 0:
            best = t
    assert best is not None, (n, cap, mult)
    return best


def _sigmoid(x):
    return 1.0 / (1.0 + jnp.exp(-x))


def _sigmoid_gate(x):
    return pl.reciprocal(1.0 + jnp.exp(-x), approx=True)


ROW_TILE = 512
COL_CHUNK = 512
GRAD_TILE = 256


def _whole(shape, index_map):
    return pl.BlockSpec(shape, index_map, pipeline_mode=pl.Buffered(1))


def _part_specs(parts, n_cols):
    return [_whole((rows, n_cols), functools.partial(lambda i, b: (b, 0), b=blk)) for _, rows, blk in parts]


def _mm_nt(a, w_parts, *, out_dtype, name, rope=None):
    M, K = a.shape
    tm = _pick_tile(M, ROW_TILE, 16)
    widths = [rows for _, rows, _ in w_parts]
    n_parts = len(w_parts)

    def body(*refs):
        a_ref, w_refs, o_ref = refs[0], refs[1:1 + n_parts], refs[-1]
        av = a_ref[...]
        off = 0
        for p, w_ref in enumerate(w_refs):
            for c0 in range(0, widths[p], COL_CHUNK):
                cw = min(COL_CHUNK, widths[p] - c0)
                acc = _dot(av, w_ref[c0:c0 + cw, :], NT)
                if rope is not None and p < rope[2]:
                    cos, sin = refs[1 + n_parts][...], refs[2 + n_parts][...]
                    for h0 in range(0, cw, ATTN_DIM):
                        xh = acc[:, h0:h0 + ATTN_DIM]
                        rot = pltpu.roll(xh, ATTN_DIM // 2, 1)
                        o_ref[:, off + c0 + h0:off + c0 + h0 + ATTN_DIM] = (xh * cos + rot * sin).astype(out_dtype)
                else:
                    o_ref[:, off + c0:off + c0 + cw] = acc.astype(out_dtype)
            off += widths[p]

    in_specs = [pl.BlockSpec((tm, K), lambda i: (i, 0))] + _part_specs(w_parts, K)
    args = [a] + [w for w, _, _ in w_parts]
    if rope is not None:
        in_specs += [pl.BlockSpec((tm, ATTN_DIM), lambda i: (i, 0))] * 2
        args += [rope[0], rope[1]]
    return pl.pallas_call(
        body, out_shape=jax.ShapeDtypeStruct((M, sum(widths)), out_dtype), grid=(M // tm,),
        in_specs=in_specs, out_specs=pl.BlockSpec((tm, sum(widths)), lambda i: (i, 0)),
        compiler_params=_params("parallel"), name=name)(*args)


def _mm_nn(a_list, w_parts_list, resid, *, name, norm=None, head=None):
    M = a_list[0].shape[0]
    tm = _pick_tile(M, ROW_TILE, 16)
    n_a = len(a_list)
    flat_parts = [p for parts in w_parts_list for p in parts]
    extra = norm if norm is not None else head
    n_in = n_a + len(flat_parts) + (1 if resid is not None else 0) + (2 if extra is not None else 0)

    def body(*refs):
        a_refs, w_refs = refs[:n_a], refs[n_a:n_a + len(flat_parts)]

        def product(rows):
            acc = None
            wi = 0
            for a_ref, parts in zip(a_refs, w_parts_list):
                off = 0
                for _, k, _ in parts:
                    term = _dot(a_ref[rows, off:off + k], w_refs[wi][...], NN)
                    acc = term if acc is None else acc + term
                    off += k
                    wi += 1
            return acc

        if extra is None:
            acc = product(slice(None))
            if resid is not None:
                acc = acc + refs[n_in - 1][...]
            refs[n_in][...] = acc
            return

        @pl.when(pl.program_id(0) == 0)
        def _():
            for acc_ref in refs[n_in + 2:]:
                acc_ref[...] = jnp.zeros_like(acc_ref)

        for r0 in range(0, tm, tm // 2):
            rows = slice(r0, r0 + tm // 2)
            acc = product(rows)
            if head is not None:
                _loss_head_math(acc + refs[n_in - 3][rows, :], rows, refs[n_in - 2], refs[n_in - 1],
                                *refs[n_in:n_in + 4])
                continue
            dres_ref, x_ref, g_ref = refs[n_in - 3:n_in]
            dx_ref, dxb_ref, dg_ref = refs[n_in:n_in + 3]
            xv = x_ref[rows, :]
            rstd = lax.rsqrt(jnp.mean(xv * xv, axis=-1, keepdims=True) + NORM_EPS)
            n = xv * rstd
            dg_ref[...] += jnp.sum(acc * n, axis=0, keepdims=True)
            dn = acc * g_ref[...]
            dx = dres_ref[rows, :] + rstd * (dn - n * jnp.mean(dn * n, axis=-1, keepdims=True))
            dx_ref[rows, :] = dx
            dxb_ref[rows, :] = dx.astype(BF16)

    row = pl.BlockSpec((tm, D_MODEL), lambda i: (i, 0))
    vec = pl.BlockSpec((1, D_MODEL), lambda i: (0, 0))
    in_specs = [pl.BlockSpec((tm, a.shape[1]), lambda i: (i, 0)) for a in a_list] + _part_specs(flat_parts, D_MODEL)
    args = list(a_list) + [w for w, _, _ in flat_parts]
    if resid is not None:
        in_specs.append(row)
        args.append(resid)
    if extra is None:
        return pl.pallas_call(
            body, out_shape=jax.ShapeDtypeStruct((M, D_MODEL), F32), grid=(M // tm,),
            in_specs=in_specs, out_specs=row, compiler_params=_params("parallel"), name=name)(*args)
    assert resid is not None
    out_shape = [jax.ShapeDtypeStruct((M, D_MODEL), F32), jax.ShapeDtypeStruct((M, D_MODEL), BF16),
                 jax.ShapeDtypeStruct((1, D_MODEL), F32)]
    out_specs = [row, row, vec]
    if head is not None:
        out_shape.append(jax.ShapeDtypeStruct((1, D_MODEL), F32))
        out_specs.append(vec)
    return pl.pallas_call(
        body, out_shape=out_shape, grid=(M // tm,), in_specs=in_specs + [row, vec], out_specs=out_specs,
        compiler_params=_params("arbitrary"), name=name)(*args, extra[0], extra[1])


def _mm_tn(a, b, *, name, into=None, row_tile=0, rows=None):
    T, R = a.shape
    N = b.shape[1]
    tr = GRAD_TILE
    rows = R if rows is None else rows

    def body(a_ref, b_ref, *refs):
        refs[-1][...] = _dot(a_ref[...], b_ref[...], TN).astype(BF16)

    in_specs = [pl.BlockSpec((T, tr), lambda r: (0, r)), _whole((T, N), lambda r: (0, 0))]
    args = [a, b]
    if into is not None:
        in_specs.append(HBM_SPEC)
        args.append(into)
    return pl.pallas_call(
        body, out_shape=jax.ShapeDtypeStruct((rows, N), BF16), grid=(R // tr,),
        in_specs=in_specs, out_specs=pl.BlockSpec((tr, N), lambda r: (row_tile + r, 0)),
        input_output_aliases={} if into is None else {2: 0},
        compiler_params=_params("parallel"), name=name)(*args)


def _rms_fwd(x, gain, name):
    T = x.shape[0]
    tm = _pick_tile(T, 512, 16)

    def body(x_ref, g_ref, u_ref):
        xv = x_ref[...]
        rstd = lax.rsqrt(jnp.mean(xv * xv, axis=-1, keepdims=True) + NORM_EPS)
        u_ref[...] = (xv * rstd * g_ref[...]).astype(BF16)

    return pl.pallas_call(
        body, out_shape=jax.ShapeDtypeStruct((T, D_MODEL), BF16), grid=(T // tm,),
        in_specs=[pl.BlockSpec((tm, D_MODEL), lambda i: (i, 0)), pl.BlockSpec((1, D_MODEL), lambda i: (0, 0))],
        out_specs=pl.BlockSpec((tm, D_MODEL), lambda i: (i, 0)),
        compiler_params=_params("parallel"), name=name)(x, gain)


def _rms_bwd(x, gain, dus, dres, name, dilations=(1,)):
    T = x.shape[0]
    tm = _pick_tile(T, PERM_TILE, 16 * max(dilations))
    n_du = len(dus)

    def body(x_ref, g_ref, *refs):
        du_refs, dres_ref = refs[:n_du], refs[n_du]
        dx_ref, dxb_ref, dg_ref, du_scr = refs[n_du + 1:]

        @pl.when(pl.program_id(0) == 0)
        def _():
            dg_ref[...] = jnp.zeros_like(dg_ref)

        if tuple(dilations) == (1,):
            du = du_refs[0][...]
        else:
            for i, (d, du_ref) in enumerate(zip(dilations, du_refs)):
                for j in range(D_MODEL // LANES):
                    lanes = slice(j * LANES, (j + 1) * LANES)
                    if d == 1:
                        du_scr[j] = du_ref[:, lanes] if i == 0 else du_scr[j] + du_ref[:, lanes]
                        continue
                    blk = du_scr.at[j]
                    for r in range(d):
                        rows = _class_rows(r, d, tm)
                        blk[rows, :] = du_ref[r, :, lanes] if i == 0 else blk[rows, :] + du_ref[r, :, lanes]
            du = jnp.concatenate([du_scr[j] for j in range(D_MODEL // LANES)], axis=1)
        xv = x_ref[...]
        rstd = lax.rsqrt(jnp.mean(xv * xv, axis=-1, keepdims=True) + NORM_EPS)
        n = xv * rstd
        dg_ref[...] += jnp.sum(du * n, axis=0, keepdims=True)
        dn = du * g_ref[...]
        dx = dres_ref[...] + rstd * (dn - n * jnp.mean(dn * n, axis=-1, keepdims=True))
        dx_ref[...] = dx
        dxb_ref[...] = dx.astype(BF16)

    row = pl.BlockSpec((tm, D_MODEL), lambda i: (i, 0))
    vec = pl.BlockSpec((1, D_MODEL), lambda i: (0, 0))
    return pl.pallas_call(
        body,
        out_shape=(jax.ShapeDtypeStruct((T, D_MODEL), F32), jax.ShapeDtypeStruct((T, D_MODEL), BF16),
                   jax.ShapeDtypeStruct((1, D_MODEL), F32)),
        grid=(T // tm,), in_specs=[row, vec] + [_residue_spec(d, tm, D_MODEL) for d in dilations] + [row],
        out_specs=(row, row, vec), scratch_shapes=[pltpu.VMEM((D_MODEL // LANES, tm, LANES), F32)],
        compiler_params=_params("arbitrary"), name=name)(
            x, gain, *[_residue_view(du, d) for du, d in zip(dus, dilations)], dres)


def _loss_head_math(hv, rows, t_ref, g_ref, dh_ref, dhb_ref, dg_ref, loss_ref):
    inv_f = 1.0 / D_MODEL
    g = g_ref[...]
    rstd = lax.rsqrt(jnp.mean(hv * hv, axis=-1, keepdims=True) + NORM_EPS)
    n = hv * rstd
    err = n * g - t_ref[rows, :]
    loss_ref[...] += (0.5 * inv_f) * jnp.sum(err * err, axis=0, keepdims=True)
    dy = err * inv_f
    dg_ref[...] += jnp.sum(dy * n, axis=0, keepdims=True)
    dn = dy * g
    dh = rstd * (dn - n * jnp.mean(dn * n, axis=-1, keepdims=True))
    dh_ref[rows, :] = dh
    dhb_ref[rows, :] = dh.astype(BF16)


FFN_TILE = 256


def _ffn_in(h, gain, w_in, name):
    T = h.shape[0]
    tm = _pick_tile(T, ROW_TILE, 16)

    def body(h_ref, g_ref, w_ref, n_ref, gate_ref, up_ref, a_ref):
        hv = h_ref[...]
        rstd = lax.rsqrt(jnp.mean(hv * hv, axis=-1, keepdims=True) + NORM_EPS)
        n = (hv * rstd * g_ref[...]).astype(BF16)
        n_ref[...] = n
        for c0 in range(0, D_FF, FFN_TILE):
            cols = slice(c0, c0 + FFN_TILE)
            gate = _dot(n, w_ref[c0:c0 + FFN_TILE, :], NT)
            up = _dot(n, w_ref[D_FF + c0:D_FF + c0 + FFN_TILE, :], NT)
            gate_ref[:, cols] = gate.astype(BF16)
            up_ref[:, cols] = up.astype(BF16)
            a_ref[:, cols] = (gate * _sigmoid(gate) * up).astype(BF16)

    row = pl.BlockSpec((tm, D_MODEL), lambda i: (i, 0))
    wide = pl.BlockSpec((tm, D_FF), lambda i: (i, 0))
    wide_shape = jax.ShapeDtypeStruct((T, D_FF), BF16)
    return pl.pallas_call(
        body, out_shape=(jax.ShapeDtypeStruct((T, D_MODEL), BF16), wide_shape, wide_shape, wide_shape),
        grid=(T // tm,),
        in_specs=[row, pl.BlockSpec((1, D_MODEL), lambda i: (0, 0)), _whole((2 * D_FF, D_MODEL), lambda i: (0, 0))],
        out_specs=(row, wide, wide, wide), compiler_params=_params("parallel"), name=name)(h, gain, w_in)


def _ffn_down_dx(dhb, w_down, gate, up, name):
    T = dhb.shape[0]
    tm = _pick_tile(T, ROW_TILE, 16)

    def body(dh_ref, w_ref, gate_ref, up_ref, dgate_ref, dup_ref):
        dh = dh_ref[...]
        for c0 in range(0, D_FF, FFN_TILE):
            cols = slice(c0, c0 + FFN_TILE)
            da = _dot(dh, w_ref[c0:c0 + FFN_TILE, :], NT)
            gate = gate_ref[:, cols].astype(F32)
            sg = _sigmoid(gate)
            dgate_ref[:, cols] = (da * up_ref[:, cols].astype(F32) * (sg * (1.0 + gate * (1.0 - sg)))).astype(BF16)
            dup_ref[:, cols] = (da * gate * sg).astype(BF16)

    wide = pl.BlockSpec((tm, D_FF), lambda i: (i, 0))
    wide_shape = jax.ShapeDtypeStruct((T, D_FF), BF16)
    return pl.pallas_call(
        body, out_shape=(wide_shape, wide_shape), grid=(T // tm,),
        in_specs=[pl.BlockSpec((tm, D_MODEL), lambda i: (i, 0)), _whole((D_FF, D_MODEL), lambda i: (0, 0)), wide, wide],
        out_specs=(wide, wide), compiler_params=_params("parallel"), name=name)(dhb, w_down, gate, up)


def _tri(n, lower):
    r = lax.broadcasted_iota(jnp.int32, (n, n), 0)
    c = lax.broadcasted_iota(jnp.int32, (n, n), 1)
    return (c <= r) if lower else (c >= r)


def _running_sum(x, lower):
    tri = _tri(x.shape[0], lower).astype(BF16)
    hi = x.astype(BF16)
    rest = x - hi.astype(F32)
    mid = rest.astype(BF16)
    lo = (rest - mid.astype(F32)).astype(BF16)
    return _dot(tri, hi, NN) + _dot(tri, mid, NN) + _dot(tri, lo, NN)


def _hgrn_gates(q_raw, f_raw, lb):
    C = q_raw.shape[0]
    sig_f = _sigmoid(f_raw)
    forget = lb + (1.0 - lb) * sig_f
    key = 1.0 - forget
    log_f = jnp.log(forget)
    b = _running_sum(log_f, True)
    first_half = lax.broadcasted_iota(jnp.int32, log_f.shape, 0) < C // 2
    r = jnp.sum(jnp.where(first_half, log_f, 0.0), axis=0, keepdims=True)
    b_last = jnp.sum(log_f, axis=0, keepdims=True)
    e_a = jnp.exp(jnp.minimum(b - r, HGRN_EXP_CLAMP))
    e_b = jnp.exp(jnp.minimum(r - b, HGRN_EXP_CLAMP))
    e_q = jnp.exp(b)
    e_k = jnp.exp(b_last - b)
    sig_q = _sigmoid_gate(q_raw)
    query = q_raw * sig_q
    return dict(sig_f=sig_f, forget=forget, sig_q=sig_q, e_a=e_a, e_b=e_b, e_q=e_q, e_k=e_k,
                e_last=jnp.exp(b_last), q_a=query * e_a, k_b=key * e_b, q_hat=query * e_q, k_til=key * e_k)


def _hgrn_fwd(proj, lb, gain, name):
    T = proj.shape[0]
    C = HGRN_CHUNK
    CPS = HGRN_STEP_CHUNKS
    H, HD = HGRN_HEADS, HGRN_DIM

    def body(q_ref, f_ref, i_ref, g_ref, lb_ref, gain_ref, og_ref, o_ref, st_ref, s_scr):
        @pl.when(pl.program_id(0) == 0)
        def _():
            s_scr[...] = jnp.zeros_like(s_scr)

        causal = _tri(C, True)
        gain_v = gain_ref[...]
        heads = [slice(h * HD, (h + 1) * HD) for h in range(H)]
        s_t = [s_scr[h] for h in range(H)]
        for cc in range(CPS):
            rows = slice(cc * C, (cc + 1) * C)
            for h in range(H):
                st_ref[cc, h] = s_t[h]
            gt = _hgrn_gates(q_ref[rows, :], f_ref[rows, :], lb_ref[...])
            q_a, k_b = gt["q_a"].astype(BF16), gt["k_b"].astype(BF16)
            q_hat, k_til = gt["q_hat"].astype(BF16), gt["k_til"].astype(BF16)
            v = i_ref[rows, :].astype(BF16)
            p = [jnp.where(causal, _dot(q_a[:, sl], k_b[:, sl], NT), 0.0).astype(BF16) for sl in heads]
            o = [_dot(p[h], v[:, sl], NN) + _dot(q_hat[:, sl], s_t[h].astype(BF16), NT)
                 for h, sl in enumerate(heads)]
            s_t = [gt["e_last"][:, sl] * s_t[h] + _dot(v[:, sl], k_til[:, sl], TN) for h, sl in enumerate(heads)]
            for h, sl in enumerate(heads):
                o_ref[rows, sl] = o[h]
                rstd = lax.rsqrt(jnp.mean(o[h] * o[h], axis=-1, keepdims=True) + NORM_EPS)
                g_raw = g_ref[rows, sl]
                og_ref[rows, sl] = (o[h] * rstd * gain_v * (g_raw * _sigmoid_gate(g_raw))).astype(BF16)
        for h in range(H):
            s_scr[h] = s_t[h]

    col = lambda j: pl.BlockSpec((CPS * C, D_MODEL), lambda c: (c, j))
    row = pl.BlockSpec((CPS * C, D_MODEL), lambda c: (c, 0))
    return pl.pallas_call(
        body,
        out_shape=(jax.ShapeDtypeStruct((T, D_MODEL), BF16), jax.ShapeDtypeStruct((T, D_MODEL), F32),
                   jax.ShapeDtypeStruct((T // C, H, HD, HD), F32)),
        grid=(T // (CPS * C),),
        in_specs=[col(0), col(1), col(2), col(3), pl.BlockSpec((1, D_MODEL), lambda c: (0, 0)),
                  pl.BlockSpec((1, HD), lambda c: (0, 0))],
        out_specs=(row, row, pl.BlockSpec((CPS, H, HD, HD), lambda c: (c, 0, 0, 0))),
        scratch_shapes=[pltpu.VMEM((H, HD, HD), F32)],
        compiler_params=_params("arbitrary"), name=name)(proj, proj, proj, proj, lb, gain)


def _hgrn_bwd(proj, o_pre, d_og, states, lb, gain, name):
    T = proj.shape[0]
    C = HGRN_CHUNK
    CPS = HGRN_STEP_CHUNKS
    H, HD = HGRN_HEADS, HGRN_DIM
    NC = T // (CPS * C)

    def body(q_ref, f_ref, i_ref, g_ref, o_ref, dog_ref, st_ref, lb_ref, gain_ref,
             dproj_ref, dlb_ref, dgain_ref, ds_scr, dq_all, dk_all, db_all):
        @pl.when(pl.program_id(0) == 0)
        def _():
            ds_scr[...] = jnp.zeros_like(ds_scr)
            dlb_ref[...] = jnp.zeros_like(dlb_ref)
            dgain_ref[...] = jnp.zeros_like(dgain_ref)

        lbv = lb_ref[...]
        causal = _tri(C, True)
        last_row = lax.broadcasted_iota(jnp.int32, (C, HD), 0) == C - 1
        gain_v = gain_ref[...]
        heads = [slice(h * HD, (h + 1) * HD) for h in range(H)]
        hs = range(H)
        ds_t = [ds_scr[h] for h in hs]
        dgain = None
        for cc in reversed(range(CPS)):
            rows = slice(cc * C, (cc + 1) * C)
            dq_scr, dk_scr, db_scr = dq_all.at[cc], dk_all.at[cc], db_all.at[cc]
            q_raw = q_ref[rows, :]
            gt = _hgrn_gates(q_raw, f_ref[rows, :], lbv)
            o = [o_ref[rows, sl] for sl in heads]
            rstd = [lax.rsqrt(jnp.mean(x * x, axis=-1, keepdims=True) + NORM_EPS) for x in o]
            n = [x * r for x, r in zip(o, rstd)]
            g_raw = [g_ref[rows, sl] for sl in heads]
            sg = [_sigmoid_gate(x) for x in g_raw]
            d_out = [dog_ref[rows, sl] for sl in heads]
            dy = [d * (g * s) for d, g, s in zip(d_out, g_raw, sg)]
            dn = [x * gain_v for x in dy]
            do = [(rstd[h] * (dn[h] - n[h] * jnp.mean(dn[h] * n[h], axis=-1, keepdims=True))).astype(BF16) for h in hs]
            for h in hs:
                dgain = dy[h] * n[h] if dgain is None else dgain + dy[h] * n[h]
            for h, sl in enumerate(heads):
                dproj_ref[rows, 3 * D_MODEL + h * HD:3 * D_MODEL + (h + 1) * HD] = (
                    d_out[h] * n[h] * gain_v * (sg[h] * (1.0 + g_raw[h] * (1.0 - sg[h])))).astype(BF16)
            q_ab, k_bb = gt["q_a"].astype(BF16), gt["k_b"].astype(BF16)
            q_hb, k_tb = gt["q_hat"].astype(BF16), gt["k_til"].astype(BF16)
            v = i_ref[rows, :].astype(BF16)
            s_t = [st_ref[cc, h] for h in hs]
            ds_b = [x.astype(BF16) for x in ds_t]
            p = [jnp.where(causal, _dot(q_ab[:, sl], k_bb[:, sl], NT), 0.0).astype(BF16) for sl in heads]
            dp = [jnp.where(causal, _dot(do[h], v[:, sl], NT), 0.0).astype(BF16) for h, sl in enumerate(heads)]
            dv = [_dot(p[h], do[h], TN) + _dot(k_tb[:, sl], ds_b[h], NT) for h, sl in enumerate(heads)]
            dq_a = [_dot(dp[h], k_bb[:, sl], NN) for h, sl in enumerate(heads)]
            dk_b = [_dot(dp[h], q_ab[:, sl], TN) for h, sl in enumerate(heads)]
            dq_hat = [_dot(do[h], s_t[h].astype(BF16), NN) for h in hs]
            dk_til = [_dot(v[:, sl], ds_b[h], NN) for h, sl in enumerate(heads)]
            ds_new = [_dot(do[h], q_hb[:, sl], TN) + gt["e_last"][:, sl] * ds_t[h] for h, sl in enumerate(heads)]
            for h, sl in enumerate(heads):
                k_til = gt["k_til"][:, sl]
                db_last = jnp.sum(ds_t[h] * gt["e_last"][:, sl] * s_t[h], axis=0, keepdims=True) + jnp.sum(
                    dk_til[h] * k_til, axis=0, keepdims=True)
                dproj_ref[rows, 2 * D_MODEL + h * HD:2 * D_MODEL + (h + 1) * HD] = dv[h].astype(BF16)
                dq_scr[:, sl] = dq_a[h] * gt["e_a"][:, sl] + dq_hat[h] * gt["e_q"][:, sl]
                dk_scr[:, sl] = dk_b[h] * gt["e_b"][:, sl] + dk_til[h] * gt["e_k"][:, sl]
                db = (dq_a[h] * q_ab[:, sl].astype(F32) + dq_hat[h] * gt["q_hat"][:, sl]
                      - dk_b[h] * k_bb[:, sl].astype(F32) - dk_til[h] * k_til)
                db_scr[:, sl] = db + jnp.where(last_row, db_last, 0.0)
            dlogf = _running_sum(db_scr[...], False)
            sig_f, forget, sig_q = gt["sig_f"], gt["forget"], gt["sig_q"]
            dforget = dlogf / forget - dk_scr[...]
            dproj_ref[rows, D_MODEL:2 * D_MODEL] = (dforget * (1.0 - lbv) * sig_f * (1.0 - sig_f)).astype(BF16)
            dlb_ref[...] += jnp.sum(dforget * (1.0 - sig_f), axis=0, keepdims=True)
            dproj_ref[rows, 0:D_MODEL] = (dq_scr[...] * (sig_q * (1.0 + q_raw * (1.0 - sig_q)))).astype(BF16)
            ds_t = ds_new
        dgain_ref[...] += jnp.sum(dgain, axis=0, keepdims=True)
        for h in hs:
            ds_scr[h] = ds_t[h]

    col = lambda j: pl.BlockSpec((CPS * C, D_MODEL), lambda c: (NC - 1 - c, j))
    row = pl.BlockSpec((CPS * C, D_MODEL), lambda c: (NC - 1 - c, 0))
    return pl.pallas_call(
        body,
        out_shape=(jax.ShapeDtypeStruct((T, 4 * D_MODEL), BF16), jax.ShapeDtypeStruct((1, D_MODEL), F32),
                   jax.ShapeDtypeStruct((1, HD), F32)),
        grid=(NC,),
        in_specs=[col(0), col(1), col(2), col(3), row, row,
                  pl.BlockSpec((CPS, H, HD, HD), lambda c: (NC - 1 - c, 0, 0, 0)),
                  pl.BlockSpec((1, D_MODEL), lambda c: (0, 0)), pl.BlockSpec((1, HD), lambda c: (0, 0))],
        out_specs=(pl.BlockSpec((CPS * C, 4 * D_MODEL), lambda c: (NC - 1 - c, 0)),
                   pl.BlockSpec((1, D_MODEL), lambda c: (0, 0)), pl.BlockSpec((1, HD), lambda c: (0, 0))),
        scratch_shapes=[pltpu.VMEM((H, HD, HD), F32)] + [pltpu.VMEM((CPS, C, D_MODEL), F32)] * 3,
        compiler_params=_params("arbitrary"), name=name)(proj, proj, proj, proj, o_pre, d_og, states, lb, gain)


def _attn_masks():
    r = lax.broadcasted_iota(jnp.int32, (ATTN_BLOCK, ATTN_BLOCK), 0)
    c = lax.broadcasted_iota(jnp.int32, (ATTN_BLOCK, ATTN_BLOCK), 1)
    return c >= r, c <= r


def _attn_fwd(qkv, dilation, name):
    T = qkv.shape[0]
    nb = T // dilation // ATTN_BLOCK
    W = ATTN_GROUP_WIDTH
    B = ATTN_BLOCK
    scale = ATTN_DIM ** -0.5
    qb = 2 if nb % 2 == 0 else 1
    steps = nb // qb

    def body(q_ref, kp_ref, kc_ref, vp_ref, vc_ref, o_ref, lse_ref):
        no_prev = jnp.where(pl.program_id(1) > 0, 0.0, NEG_BIG)
        m_prev, m_cur = _attn_masks()
        ones = jnp.ones((B, ATTN_DIM), BF16)
        items = []
        for j in range(qb):
            for h in range(ATTN_GROUP_HEADS):
                sl = slice(h * ATTN_DIM, (h + 1) * ATTN_DIM)
                rows = slice(j * B, (j + 1) * B)
                if j == 0:
                    items.append((rows, sl, kp_ref[:, sl], vp_ref[:, sl], no_prev))
                else:
                    before = slice((j - 1) * B, j * B)
                    items.append((rows, sl, kc_ref[before, sl], vc_ref[before, sl], 0.0))
        s_p = [jnp.where(m_prev, _dot(q_ref[rows, sl], k_p, NT) * scale + bias, NEG_BIG)
               for rows, sl, k_p, _, bias in items]
        s_c = [jnp.where(m_cur, _dot(q_ref[rows, sl], kc_ref[rows, sl], NT) * scale, NEG_BIG)
               for rows, sl, _, _, _ in items]
        m = [jnp.max(jnp.maximum(a, b), axis=-1, keepdims=True) for a, b in zip(s_p, s_c)]
        p_p = [jnp.exp(a - mx).astype(BF16) for a, mx in zip(s_p, m)]
        p_c = [jnp.exp(b - mx).astype(BF16) for b, mx in zip(s_c, m)]
        l = [_dot(a, ones, NN) + _dot(b, ones, NN) for a, b in zip(p_p, p_c)]
        acc = [_dot(a, v_p, NN) + _dot(b, vc_ref[rows, sl], NN)
               for a, b, (rows, sl, _, v_p, _) in zip(p_p, p_c, items)]
        for (rows, sl, _, _, _), a, lv, mx in zip(items, acc, l, m):
            o_ref[rows, sl] = a / lv
            lse_ref[rows, sl] = mx + jnp.log(lv)

    cur = lambda col: pl.BlockSpec((qb * B, W), lambda s, n: (s * steps + n, col))
    prev = lambda col: pl.BlockSpec((B, W), lambda s, n: (s * nb + jnp.maximum(qb * n - 1, 0), col))
    out = pl.BlockSpec((qb * B, W), lambda s, n: (s * steps + n, 0))
    return pl.pallas_call(
        body, out_shape=(jax.ShapeDtypeStruct((T, W), F32),) * 2, grid=(dilation, steps),
        in_specs=[cur(0), prev(1), cur(1), prev(2), cur(2)],
        out_specs=(out, out), compiler_params=_params("parallel", "arbitrary"), name=name)(qkv, qkv, qkv, qkv, qkv)


def _attn_bwd(qkv, d_out, lse, delta, cos, sin, dilation, name):
    T = qkv.shape[0]
    nb = T // dilation // ATTN_BLOCK
    W = ATTN_GROUP_WIDTH
    scale = ATTN_DIM ** -0.5

    def unrope(x, cos_v, sin_v):
        return x * cos_v + pltpu.roll(x * sin_v, ATTN_DIM // 2, 1)

    def body(q_ref, kp_ref, kc_ref, vp_ref, vc_ref, do_ref, lse_ref, dl_ref, cos_ref, sin_ref,
             out_ref, dq_scr, dk_scr, dv_scr):
        n = pl.program_id(1)
        cos_v, sin_v = cos_ref[...], sin_ref[...]

        @pl.when(n > 0)
        def _():
            for h in range(ATTN_GROUP_HEADS):
                sl = slice(h * ATTN_DIM, (h + 1) * ATTN_DIM)
                out_ref[:, sl] = unrope(dq_scr[:, sl], cos_v, sin_v).astype(BF16)

        @pl.when(n == nb)
        def _():
            for h in range(ATTN_GROUP_HEADS):
                sl = slice(h * ATTN_DIM, (h + 1) * ATTN_DIM)
                out_ref[:, W + h * ATTN_DIM:W + (h + 1) * ATTN_DIM] = unrope(dk_scr[:, sl], cos_v, sin_v).astype(BF16)
                out_ref[:, 2 * W + h * ATTN_DIM:2 * W + (h + 1) * ATTN_DIM] = dv_scr[:, sl].astype(BF16)

        @pl.when(n == 0)
        def _():
            dk_scr[...] = jnp.zeros_like(dk_scr)
            dv_scr[...] = jnp.zeros_like(dv_scr)

        @pl.when(n < nb)
        def _():
            has_prev = n > 0
            no_prev = jnp.where(has_prev, 0.0, NEG_BIG)
            m_prev, m_cur = _attn_masks()
            heads = [slice(h * ATTN_DIM, (h + 1) * ATTN_DIM) for h in range(ATTN_GROUP_HEADS)]
            s_p = [_dot(q_ref[:, sl], kp_ref[:, sl], NT) for sl in heads]
            s_c = [_dot(q_ref[:, sl], kc_ref[:, sl], NT) for sl in heads]
            dp_p = [_dot(do_ref[:, sl], vp_ref[:, sl], NT) for sl in heads]
            dp_c = [_dot(do_ref[:, sl], vc_ref[:, sl], NT) for sl in heads]
            p_p = [jnp.where(m_prev, jnp.exp(s * scale - lse_ref[:, sl] + no_prev), 0.0) for s, sl in zip(s_p, heads)]
            p_c = [jnp.where(m_cur, jnp.exp(s * scale - lse_ref[:, sl]), 0.0) for s, sl in zip(s_c, heads)]
            ds_p = [(p * (dp - dl_ref[:, sl]) * scale).astype(BF16) for p, dp, sl in zip(p_p, dp_p, heads)]
            ds_c = [(p * (dp - dl_ref[:, sl]) * scale).astype(BF16) for p, dp, sl in zip(p_c, dp_c, heads)]
            p_p = [p.astype(BF16) for p in p_p]
            p_c = [p.astype(BF16) for p in p_c]
            dk_prev = [dk_scr[:, sl] + _dot(ds, q_ref[:, sl], TN) for ds, sl in zip(ds_p, heads)]
            dv_prev = [dv_scr[:, sl] + _dot(p, do_ref[:, sl], TN) for p, sl in zip(p_p, heads)]
            dq = [_dot(a, kp_ref[:, sl], NN) + _dot(b, kc_ref[:, sl], NN) for a, b, sl in zip(ds_p, ds_c, heads)]
            dk_cur = [_dot(ds, q_ref[:, sl], TN) for ds, sl in zip(ds_c, heads)]
            dv_cur = [_dot(p, do_ref[:, sl], TN) for p, sl in zip(p_c, heads)]
            for h, sl in enumerate(heads):
                out_ref[:, W + h * ATTN_DIM:W + (h + 1) * ATTN_DIM] = unrope(dk_prev[h], cos_v, sin_v).astype(BF16)
                out_ref[:, 2 * W + h * ATTN_DIM:2 * W + (h + 1) * ATTN_DIM] = dv_prev[h].astype(BF16)
                dq_scr[:, sl] = dq[h]
                dk_scr[:, sl] = dk_cur[h]
                dv_scr[:, sl] = dv_cur[h]

    def cur(n):
        return jnp.minimum(n, nb - 1)

    def late(n):
        return jnp.maximum(n - 1, 0)

    qkv_blk = lambda col, prev: pl.BlockSpec(
        (ATTN_BLOCK, W), lambda s, n: (s * nb + (jnp.maximum(cur(n) - 1, 0) if prev else cur(n)), col))
    row = pl.BlockSpec((ATTN_BLOCK, W), lambda s, n: (s * nb + cur(n), 0))
    tab = pl.BlockSpec((ATTN_BLOCK, ATTN_DIM), lambda s, n: (s * nb + late(n), 0))
    return pl.pallas_call(
        body, out_shape=jax.ShapeDtypeStruct((T, 3 * W), BF16), grid=(dilation, nb + 1),
        in_specs=[qkv_blk(0, False), qkv_blk(1, True), qkv_blk(1, False), qkv_blk(2, True), qkv_blk(2, False),
                  row, row, row, tab, tab],
        out_specs=pl.BlockSpec((ATTN_BLOCK, 3 * W), lambda s, n: (s * nb + late(n), 0)),
        scratch_shapes=[pltpu.VMEM((ATTN_BLOCK, W), F32)] * 3,
        compiler_params=_params("parallel", "arbitrary"), name=name)(
            qkv, qkv, qkv, qkv, qkv, d_out, lse, delta, cos, sin)


PERM_TILE = 512
LANES = 128


def _residue_view(x, d):
    return x if d == 1 else x.reshape(d, x.shape[0] // d, x.shape[1])


def _residue_spec(d, tm, cols):
    if d == 1:
        return pl.BlockSpec((tm, cols), lambda i: (i, 0))
    return pl.BlockSpec((d, tm // d, cols), lambda i: (0, i, 0))


def _residue_shape(T, d, cols, dtype):
    return jax.ShapeDtypeStruct((T, cols) if d == 1 else (d, T // d, cols), dtype)


def _class_rows(r, d, tm):
    return pl.ds(r, tm // d, stride=d)


def _attn_norm(h, gain, cos, sin, name):
    T = h.shape[0]
    tm = _pick_tile(T, PERM_TILE, 16 * max(ATTN_DILATIONS))
    dils = ATTN_DILATIONS

    def body(h_ref, g_ref, cos_ref, sin_ref, *refs):
        u_refs, c_refs, s_refs, u_scr = refs[0:3], refs[3:6], refs[6:9], refs[9]
        hv = h_ref[...]
        rstd = lax.rsqrt(jnp.mean(hv * hv, axis=-1, keepdims=True) + NORM_EPS)
        u = hv * rstd * g_ref[...]
        for j in range(D_MODEL // LANES):
            u_scr[j] = u[:, j * LANES:(j + 1) * LANES]
        for d, u_ref, c_ref, s_ref in zip(dils, u_refs, c_refs, s_refs):
            if d == 1:
                u_ref[...] = u.astype(BF16)
                c_ref[...] = cos_ref[...]
                s_ref[...] = sin_ref[...]
                continue
            for r in range(d):
                rows = _class_rows(r, d, tm)
                for j in range(D_MODEL // LANES):
                    u_ref[r, :, j * LANES:(j + 1) * LANES] = u_scr.at[j][rows, :].astype(BF16)
                c_ref[r] = cos_ref[rows, :]
                s_ref[r] = sin_ref[rows, :]

    row = pl.BlockSpec((tm, D_MODEL), lambda i: (i, 0))
    tab = pl.BlockSpec((tm, ATTN_DIM), lambda i: (i, 0))
    res = pl.pallas_call(
        body,
        out_shape=([_residue_shape(T, d, D_MODEL, BF16) for d in dils]
                   + [_residue_shape(T, d, ATTN_DIM, F32) for d in dils] * 2),
        grid=(T // tm,), in_specs=[row, pl.BlockSpec((1, D_MODEL), lambda i: (0, 0)), tab, tab],
        out_specs=([_residue_spec(d, tm, D_MODEL) for d in dils] + [_residue_spec(d, tm, ATTN_DIM) for d in dils] * 2),
        scratch_shapes=[pltpu.VMEM((D_MODEL // LANES, tm, LANES), F32)],
        compiler_params=_params("parallel"), name=name)(h, gain, cos, sin)
    flat = [r.reshape(T, r.shape[-1]) for r in res]
    return flat[0:3], flat[3:6], flat[6:9]


def _attn_merge_fwd(outs, lses, name):
    T = outs[0].shape[0]
    W = ATTN_GROUP_WIDTH
    tm = _pick_tile(T, PERM_TILE, 16 * max(ATTN_DILATIONS))
    dils = ATTN_DILATIONS

    def body(*refs):
        o_refs, l_refs, oc_ref, lse_refs = refs[0:3], refs[3:6], refs[6], refs[7:10]
        o_scr, l_scr, t_scr = refs[10:13]
        nh = ATTN_GROUP_HEADS
        for g, d in enumerate(dils):
            for j in range(nh):
                lanes = slice(j * LANES, (j + 1) * LANES)
                if d == 1:
                    o_scr[g * nh + j] = o_refs[g][:, lanes]
                    l_scr[g * nh + j] = l_refs[g][:, lanes]
                    continue
                for r in range(d):
                    rows = _class_rows(r, d, tm)
                    o_scr.at[g * nh + j][rows, :] = o_refs[g][r, :, lanes]
                    l_scr.at[g * nh + j][rows, :] = l_refs[g][r, :, lanes]
        for j in range(nh):
            lanes = slice(j * LANES, (j + 1) * LANES)
            ls = [l_scr[g * nh + j] for g in range(3)]
            m = jnp.maximum(jnp.maximum(ls[0], ls[1]), ls[2])
            tot = m + jnp.log(jnp.exp(ls[0] - m) + jnp.exp(ls[1] - m) + jnp.exp(ls[2] - m))
            t_scr[j] = tot
            for g, d in enumerate(dils):
                oc_ref[:, g * W + j * LANES:g * W + (j + 1) * LANES] = (
                    o_scr[g * nh + j] * jnp.exp(ls[g] - tot)).astype(BF16)
                if d == 1:
                    lse_refs[g][:, lanes] = tot
                    continue
                for r in range(d):
                    lse_refs[g][r, :, lanes] = t_scr.at[j][_class_rows(r, d, tm), :]

    in_blk = [_residue_spec(d, tm, W) for d in dils]
    n_blk = 3 * ATTN_GROUP_HEADS
    res = pl.pallas_call(
        body, out_shape=[jax.ShapeDtypeStruct((T, 3 * W), BF16)] + [_residue_shape(T, d, W, F32) for d in dils],
        grid=(T // tm,), in_specs=in_blk * 2,
        out_specs=[pl.BlockSpec((tm, 3 * W), lambda i: (i, 0))] + in_blk,
        scratch_shapes=[pltpu.VMEM((n_blk, tm, LANES), F32), pltpu.VMEM((n_blk, tm, LANES), F32),
                        pltpu.VMEM((ATTN_GROUP_HEADS, tm, LANES), F32)],
        compiler_params=_params("parallel"), name=name)(
            *[_residue_view(o, d) for o, d in zip(outs, dils)], *[_residue_view(l, d) for l, d in zip(lses, dils)])
    return res[0], [r.reshape(T, W) for r in res[1:]]


def _attn_merge_bwd(d_oc, oc, name):
    T = d_oc.shape[0]
    W = ATTN_GROUP_WIDTH
    tm = _pick_tile(T, PERM_TILE, 16 * max(ATTN_DILATIONS))
    dils = ATTN_DILATIONS

    def body(d_ref, o_ref, *refs):
        delta_refs, db_refs, dl_scr, d_scr = refs[0:3], refs[3:6], refs[6], refs[7]
        nh = ATTN_GROUP_HEADS
        for j in range(nh):
            tot = jnp.zeros((tm, 1), F32)
            for g in range(3):
                cols = slice(g * W + j * LANES, g * W + (j + 1) * LANES)
                d_blk = d_ref[:, cols]
                d_scr[g * nh + j] = d_blk
                tot = tot + jnp.sum(d_blk * o_ref[:, cols].astype(F32), axis=-1, keepdims=True)
            dl_scr[j] = jnp.broadcast_to(tot, (tm, LANES))
        for g, d in enumerate(dils):
            for j in range(nh):
                lanes = slice(j * LANES, (j + 1) * LANES)
                if d == 1:
                    delta_refs[g][:, lanes] = dl_scr[j]
                    db_refs[g][:, lanes] = d_scr[g * nh + j].astype(BF16)
                    continue
                for r in range(d):
                    rows = _class_rows(r, d, tm)
                    delta_refs[g][r, :, lanes] = dl_scr.at[j][rows, :]
                    db_refs[g][r, :, lanes] = d_scr.at[g * nh + j][rows, :].astype(BF16)

    wide = pl.BlockSpec((tm, 3 * W), lambda i: (i, 0))
    out_blk = [_residue_spec(d, tm, W) for d in dils]
    res = pl.pallas_call(
        body, out_shape=[_residue_shape(T, d, W, F32) for d in dils] + [_residue_shape(T, d, W, BF16) for d in dils],
        grid=(T // tm,), in_specs=[wide, wide], out_specs=out_blk * 2,
        scratch_shapes=[pltpu.VMEM((ATTN_GROUP_HEADS, tm, LANES), F32),
                        pltpu.VMEM((3 * ATTN_GROUP_HEADS, tm, LANES), F32)],
        compiler_params=_params("parallel"), name=name)(d_oc, oc)
    flat = [r.reshape(T, W) for r in res]
    return flat[0:3], flat[3:6]


def _rope_tables(T):
    inv_freq = 1.0 / (ROPE_THETA ** (jnp.arange(0, ATTN_DIM, 2, dtype=F32) / ATTN_DIM))
    ang = jnp.arange(T, dtype=F32)[:, None] * inv_freq[None, :]
    cos, sin = jnp.cos(ang), jnp.sin(ang)
    return jnp.concatenate([cos, cos], axis=1), jnp.concatenate([-sin, sin], axis=1)


WEIGHT_GROUPS = {"hgrn": ("hgrn_in", "hgrn_out"), "ffn0": ("ffn_in0", "ffn_down0"),
                 "attn": ("qkv", "attn_out"), "ffn1": ("ffn_in1", "ffn_down1")}


def _local_step(x, target, norm_mix, norm_ffn, lb, out_gain, final_gain, fetch, publish):
    T = x.shape[0]
    g_mix = [norm_mix[0:1], norm_mix[1:2]]
    g_ffn = [norm_ffn[0:1], norm_ffn[1:2]]
    w = {}

    def whole(name):
        return [(w[name], w[name].shape[0], 0)]

    def qkv_parts(g):
        return [(w["qkv"], ATTN_GROUP_WIDTH, 3 * j + g) for j in range(3)]

    def ffn_fwd(h, layer, head=None):
        w.update(fetch(f"ffn{layer}"))
        n, gate, up, a = _ffn_in(h, g_ffn[layer], w[f"ffn_in{layer}"], f"ffn{layer}_in")
        out = _mm_nn([a], [whole(f"ffn_down{layer}")], h, name=f"ffn{layer}_down", head=head)
        return out, (n, gate, up, a)

    def ffn_bwd(h, saved, dh, dhb, layer):
        n, gate, up, a = saved
        w_in = w[f"ffn_in{layer}"]
        dgate, dup = _ffn_down_dx(dhb, w[f"ffn_down{layer}"], gate, up, f"ffn{layer}_down_dx")
        grad_in = _mm_tn(dgate, n, name=f"ffn{layer}_in_dw_gate", rows=2 * D_FF)
        grad_in = _mm_tn(dup, n, name=f"ffn{layer}_in_dw_up", into=grad_in, row_tile=D_FF // GRAD_TILE, rows=2 * D_FF)
        grads = {f"ffn_down{layer}": _mm_tn(a, dhb, name=f"ffn{layer}_down_dw"), f"ffn_in{layer}": grad_in}
        publish(f"ffn{layer}", grads)
        return _mm_nn([dgate, dup], [[(w_in, D_FF, 0)], [(w_in, D_FF, 1)]], dh, name=f"ffn{layer}_in_dx",
                      norm=(h, g_ffn[layer]))

    u0 = _rms_fwd(x, g_mix[0], "hgrn_norm")
    w.update(fetch("hgrn"))
    proj = _mm_nt(u0, whole("hgrn_in"), out_dtype=F32, name="hgrn_in")
    og, o_pre, states = _hgrn_fwd(proj, lb, out_gain, "hgrn_fwd")
    h1 = _mm_nn([og], [whole("hgrn_out")], x, name="hgrn_out")
    h2, ffn0 = ffn_fwd(h1, 0)

    cos, sin = _rope_tables(T)
    u1_g, cos_g, sin_g = _attn_norm(h2, g_mix[1], cos, sin, "attn_norm")
    w.update(fetch("attn"))
    qkv_g, outs, lses = [], [], []
    for g, d in enumerate(ATTN_DILATIONS):
        qkv_g.append(_mm_nt(u1_g[g], qkv_parts(g), out_dtype=BF16, name=f"attn_qkv{g}",
                            rope=(cos_g[g], sin_g[g], 2)))
        o_g, lse_g = _attn_fwd(qkv_g[g], d, f"attn_fwd{g}")
        outs.append(o_g)
        lses.append(lse_g)
    oc, lse_all = _attn_merge_fwd(outs, lses, "attn_merge")
    h3 = _mm_nn([oc], [whole("attn_out")], h2, name="attn_out")
    (dh4, dh4b, d_final, loss_part), ffn1 = ffn_fwd(h3, 1, head=(target, final_gain))
    dh3, dh3b, d_ffn1 = ffn_bwd(h3, ffn1, dh4, dh4b, 1)

    d_oc = _mm_nt(dh3b, whole("attn_out"), out_dtype=F32, name="attn_out_dx")
    grad_attn_out = _mm_tn(oc, dh3b, name="attn_out_dw")
    delta, d_ocb = _attn_merge_bwd(d_oc, oc, "attn_merge_bwd")
    du1, qkv_pieces = [], []
    for g, d in enumerate(ATTN_DILATIONS):
        dqkv = _attn_bwd(qkv_g[g], d_ocb[g], lse_all[g], delta[g], cos_g[g], sin_g[g], d, f"attn_bwd{g}")
        qkv_pieces.append(_mm_tn(dqkv, u1_g[g], name=f"attn_qkv_dw{g}"))
        du1.append(_mm_nn([dqkv], [qkv_parts(g)], None, name=f"attn_qkv_dx{g}"))
    grad_qkv = jnp.stack([p.reshape(3, ATTN_GROUP_WIDTH, D_MODEL) for p in qkv_pieces], axis=1).reshape(
        3 * ATTN_WIDTH, D_MODEL)
    publish("attn", {"qkv": grad_qkv, "attn_out": grad_attn_out})
    dh2, dh2b, d_mix1 = _rms_bwd(h2, g_mix[1], du1, dh3, "attn_norm_bwd", ATTN_DILATIONS)

    dh1, dh1b, d_ffn0 = ffn_bwd(h1, ffn0, dh2, dh2b, 0)

    d_og = _mm_nt(dh1b, whole("hgrn_out"), out_dtype=F32, name="hgrn_out_dx")
    grad_hgrn_out = _mm_tn(og, dh1b, name="hgrn_out_dw")
    dproj, d_lb, d_out_gain = _hgrn_bwd(proj, o_pre, d_og, states, lb, out_gain, "hgrn_bwd")
    publish("hgrn", {"hgrn_in": _mm_tn(dproj, u0, name="hgrn_in_dw"), "hgrn_out": grad_hgrn_out})
    dx, _, d_mix0 = _mm_nn([dproj], [whole("hgrn_in")], dh1, name="hgrn_in_dx", norm=(x, g_mix[0]))

    small = dict(norm_mix0=d_mix0, norm_mix1=d_mix1, norm_ffn0=d_ffn0, norm_ffn1=d_ffn1, lb=d_lb,
                 out_gain=d_out_gain, final=d_final, loss=loss_part)
    return dx, small


WEIGHT_NAMES = ("hgrn_in", "hgrn_out", "qkv", "attn_out", "ffn_in0", "ffn_in1", "ffn_down0", "ffn_down1")
MESH_IDS = pl.DeviceIdType.MESH
HBM_SPEC = pl.BlockSpec(memory_space=pl.ANY)


N_PEERS = N_DEV - 1
PEER_OFFSETS = [(dx, dy, dc) for dx in (0, 1) for dy in (0, 1) for dc in (0, 1)][1:]


def _mesh_place():
    x, y, c = lax.axis_index("x"), lax.axis_index("y"), lax.axis_index("c")
    peers = []
    for dx, dy, dc in PEER_OFFSETS:
        px, py, pc = (1 - x if dx else x), (1 - y if dy else y), (1 - c if dc else c)
        peers.append(((px, py, pc), 4 * px + 2 * py + pc))
    return 4 * x + 2 * y + c, peers


def _exchange_launch(srcs, scatter, collective_id, name):
    n = len(srcs)
    src_refs = [jax.new_ref(s, memory_space=pltpu.MemorySpace.HBM) for s in srcs]
    land_refs = [jax.empty_ref(jax.ShapeDtypeStruct(s.shape if scatter else (N_DEV,) + s.shape, s.dtype),
                               memory_space=pltpu.MemorySpace.HBM) for s in srcs]

    @pl.kernel(mesh=plsc.ScalarSubcoreMesh(axis_name="sequencer", num_cores=1), name=name,
               scratch_types=(pltpu.SemaphoreType.DMA((n * N_PEERS,)), pltpu.SemaphoreType.DMA((n * N_PEERS,)),
                              pltpu.SemaphoreType.DMA((n,))),
               compiler_params=pltpu.CompilerParams(collective_id=collective_id))
    def launch(send_sems, recv_sems, local_sems):
        me, peers = _mesh_place()
        barrier = pltpu.get_barrier_semaphore()
        for peer, _ in peers:
            pl.semaphore_signal(barrier, inc=1, device_id=peer, device_id_type=MESH_IDS)
        pl.semaphore_wait(barrier, N_PEERS)
        own = [pltpu.make_async_copy(src_refs[w].at[me] if scatter else src_refs[w], land_refs[w].at[me],
                                     local_sems.at[w]) for w in range(n)]
        for cp in own:
            cp.start()
        copies = [pltpu.make_async_remote_copy(
            src_ref=src_refs[w].at[pid] if scatter else src_refs[w], dst_ref=land_refs[w].at[me],
            send_sem=send_sems.at[w * N_PEERS + k], recv_sem=recv_sems.at[w * N_PEERS + k],
            device_id=peer, device_id_type=MESH_IDS) for w in range(n) for k, (peer, pid) in enumerate(peers)]
        for cp in copies:
            cp.start()
        for cp in copies:
            cp.wait()
        for cp in own:
            cp.wait()

    launch()
    return land_refs


def _gather_small(block, name):
    def body(in_ref, out_ref, send_sems, recv_sems, local_sem):
        me, peers = _mesh_place()
        own = pltpu.make_async_copy(in_ref, out_ref.at[me], local_sem)
        own.start()
        sends = [pltpu.make_async_remote_copy(
            src_ref=in_ref, dst_ref=out_ref.at[me], send_sem=send_sems.at[k], recv_sem=recv_sems.at[k],
            device_id=peer, device_id_type=MESH_IDS) for k, (peer, _) in enumerate(peers)]
        for cp in sends:
            cp.start()
        for cp in sends:
            cp.wait_recv()
        for cp in sends:
            cp.wait_send()
        own.wait()

    return pl.pallas_call(
        body, out_shape=jax.ShapeDtypeStruct((N_DEV,) + block.shape, block.dtype),
        in_specs=[HBM_SPEC], out_specs=HBM_SPEC,
        scratch_shapes=[pltpu.SemaphoreType.DMA((N_PEERS,)), pltpu.SemaphoreType.DMA((N_PEERS,)),
                        pltpu.SemaphoreType.DMA],
        name=name)(block)


def _sum_blocks(recv, name):
    rows = recv.shape[1]
    tr = _pick_tile(rows, 256, 16)

    def body(r_ref, g_ref):
        acc = r_ref[0].astype(F32)
        for j in range(1, N_DEV):
            acc = acc + r_ref[j].astype(F32)
        g_ref[...] = acc

    return pl.pallas_call(
        body, out_shape=jax.ShapeDtypeStruct((rows, D_MODEL), F32), grid=(rows // tr,),
        in_specs=[pl.BlockSpec((N_DEV, tr, D_MODEL), lambda i: (0, i, 0))],
        out_specs=pl.BlockSpec((tr, D_MODEL), lambda i: (i, 0)),
        compiler_params=_params("parallel"), name=name)(recv)


def _adamw_math(w, g, m, v):
    m_new = ADAM_B1 * m + (1.0 - ADAM_B1) * g
    v_new = ADAM_B2 * v + (1.0 - ADAM_B2) * (g * g)
    m_hat = m_new / (1.0 - ADAM_B1 ** ADAM_STEP)
    v_hat = v_new / (1.0 - ADAM_B2 ** ADAM_STEP)
    delta = -ADAM_LR * (m_hat / (jnp.sqrt(v_hat) + ADAM_EPS) + ADAM_WD * w)
    return delta, m_new, v_new


def _adamw(w, g, m, v, name):
    rows, cols = w.shape
    tr = _pick_tile(rows, 256, 8)

    def body(w_ref, g_ref, m_ref, v_ref, d_ref, mo_ref, vo_ref):
        d_ref[...], mo_ref[...], vo_ref[...] = _adamw_math(w_ref[...], g_ref[...], m_ref[...], v_ref[...])

    blk = pl.BlockSpec((tr, cols), lambda i: (i, 0))
    return pl.pallas_call(
        body, out_shape=(jax.ShapeDtypeStruct((rows, cols), F32),) * 3, grid=(rows // tr,),
        in_specs=[blk] * 4, out_specs=(blk,) * 3, compiler_params=_params("parallel"), name=name)(w, g, m, v)


ROW_MIX, ROW_FFN, ROW_LB, ROW_OUT_GAIN, ROW_FINAL = 0, 2, 4, 7, 8
PART_MIX, PART_FFN, PART_LB, PART_OUT_GAIN, PART_FINAL, PART_LOSS = 0, 2, 4, 5, 6, 7


def _small_update(parts_all, w, m, v, name):
    def body(p_ref, w_ref, m_ref, v_ref, g_ref, d_ref, mo_ref, vo_ref, loss_ref):
        def total(row, n=1):
            tot = p_ref[0, row:row + n, :]
            for j in range(1, N_DEV):
                tot = tot + p_ref[j, row:row + n, :]
            return tot

        logits = [w_ref[ROW_LB + i:ROW_LB + i + 1, :] for i in range(3)]
        mx = jnp.maximum(jnp.maximum(logits[0], logits[1]), logits[2])
        ex = [jnp.exp(l - mx) for l in logits]
        den = ex[0] + ex[1] + ex[2]
        prob = [e / den for e in ex]
        d_lb = total(PART_LB)
        g_ref[...] = jnp.zeros_like(g_ref)
        g_ref[ROW_MIX:ROW_MIX + 2, :] = total(PART_MIX, 2)
        g_ref[ROW_FFN:ROW_FFN + 2, :] = total(PART_FFN, 2)
        for i in range(3):
            g_ref[ROW_LB + i:ROW_LB + i + 1, :] = prob[i] * ((d_lb if i == 0 else 0.0) - prob[0] * d_lb)
        g_ref[ROW_OUT_GAIN:ROW_OUT_GAIN + 1, :] = total(PART_OUT_GAIN)
        g_ref[ROW_FINAL:ROW_FINAL + 1, :] = total(PART_FINAL)
        d_ref[...], mo_ref[...], vo_ref[...] = _adamw_math(w_ref[...], g_ref[...], m_ref[...], v_ref[...])
        loss_ref[...] = jnp.sum(total(PART_LOSS), axis=-1, keepdims=True)

    packed = jax.ShapeDtypeStruct((16, D_MODEL), F32)
    return pl.pallas_call(
        body, out_shape=(packed, packed, packed, packed, jax.ShapeDtypeStruct((1, 1), F32)),
        compiler_params=pltpu.CompilerParams(vmem_limit_bytes=VMEM_LIMIT), name=name)(parts_all, w, m, v)


def _pack_small(norm_mix, norm_ffn, lb_logits, out_gain, final):
    pad = jnp.zeros((1, D_MODEL - HGRN_DIM), F32)
    return jnp.concatenate([norm_mix, norm_ffn, lb_logits, jnp.concatenate([out_gain, pad], axis=1),
                            final.reshape(1, D_MODEL), jnp.zeros((16 - ROW_FINAL - 1, D_MODEL), F32)], axis=0)


def _unpack_small(p):
    return (p[ROW_MIX:ROW_MIX + 2], p[ROW_FFN:ROW_FFN + 2], p[ROW_LB:ROW_LB + 3],
            p[ROW_OUT_GAIN:ROW_OUT_GAIN + 1, :HGRN_DIM], p[ROW_FINAL])


def _lower_bound(lb_logits, name):
    def body(l_ref, o_ref):
        logits = [l_ref[i:i + 1, :] for i in range(3)]
        mx = jnp.maximum(jnp.maximum(logits[0], logits[1]), logits[2])
        ex = [jnp.exp(l - mx) for l in logits]
        o_ref[...] = ex[0] / (ex[0] + ex[1] + ex[2])

    return pl.pallas_call(body, out_shape=jax.ShapeDtypeStruct((1, D_MODEL), F32), name=name)(lb_logits)


def kernel(x, norm_mix, norm_ffn, hgrn_w_in, hgrn_lb_logits, hgrn_out_norm, hgrn_w_out, attn_w_qkv, attn_w_out, ffn_w_in, ffn_w_down, final_norm, loss_target, m_norm_mix, m_norm_ffn, m_hgrn_w_in, m_hgrn_lb_logits, m_hgrn_out_norm, m_hgrn_w_out, m_attn_w_qkv, m_attn_w_out, m_ffn_w_in, m_ffn_w_down, m_final_norm, v_norm_mix, v_norm_ffn, v_hgrn_w_in, v_hgrn_lb_logits, v_hgrn_out_norm, v_hgrn_w_out, v_attn_w_qkv, v_attn_w_out, v_ffn_w_in, v_ffn_w_down, v_final_norm):
    col_sharded = {"hgrn_in": hgrn_w_in[0], "qkv": attn_w_qkv[0], "ffn_in0": ffn_w_in[0], "ffn_in1": ffn_w_in[1]}
    row_sharded = {"hgrn_out": hgrn_w_out[0], "attn_out": attn_w_out[0], "ffn_down0": ffn_w_down[0],
                   "ffn_down1": ffn_w_down[1]}
    gathering = {}
    for gi, (group, names) in enumerate(WEIGHT_GROUPS.items()):
        shards = [(col_sharded[n].T if n in col_sharded else row_sharded[n]).astype(BF16) for n in names]
        gathering[group] = _exchange_launch(shards, False, 1 + gi, f"weights_gather_{group}")

    def fetch(group):
        return {n: land[...].reshape(-1, D_MODEL) for n, land in zip(WEIGHT_GROUPS[group], gathering[group])}

    in_flight = {}

    def publish(group, grads):
        names = WEIGHT_GROUPS[group]
        parts = [grads[n].reshape(N_DEV, -1, D_MODEL) for n in names]
        in_flight[group] = _exchange_launch(parts, True, 1 + len(WEIGHT_GROUPS) + list(WEIGHT_GROUPS).index(group),
                                            f"grads_send_{group}")

    lb = _lower_bound(hgrn_lb_logits, "hgrn_lower_bound")
    grad_x, small = _local_step(x[0], loss_target[0], norm_mix, norm_ffn, lb, hgrn_out_norm,
                                final_norm.reshape(1, D_MODEL), fetch, publish)

    pad = jnp.zeros((1, D_MODEL - HGRN_DIM), F32)
    small_part = jnp.concatenate(
        [small["norm_mix0"], small["norm_mix1"], small["norm_ffn0"], small["norm_ffn1"], small["lb"],
         jnp.concatenate([small["out_gain"], pad], axis=1), small["final"], small["loss"]], axis=0)
    small_all = _gather_small(small_part, "small_grads_gather")
    received = {}
    for group in ("ffn1", "attn", "ffn0", "hgrn"):
        received.update(zip(WEIGHT_GROUPS[group], [land[...] for land in in_flight[group]]))

    masters = {"hgrn_in": (hgrn_w_in[0], m_hgrn_w_in[0], v_hgrn_w_in[0]),
               "hgrn_out": (hgrn_w_out[0], m_hgrn_w_out[0], v_hgrn_w_out[0]),
               "qkv": (attn_w_qkv[0], m_attn_w_qkv[0], v_attn_w_qkv[0]),
               "attn_out": (attn_w_out[0], m_attn_w_out[0], v_attn_w_out[0]),
               "ffn_in0": (ffn_w_in[0], m_ffn_w_in[0], v_ffn_w_in[0]),
               "ffn_in1": (ffn_w_in[1], m_ffn_w_in[1], v_ffn_w_in[1]),
               "ffn_down0": (ffn_w_down[0], m_ffn_w_down[0], v_ffn_w_down[0]),
               "ffn_down1": (ffn_w_down[1], m_ffn_w_down[1], v_ffn_w_down[1])}
    res = {}
    for n in WEIGHT_NAMES:
        g = _sum_blocks(received[n], f"{n}_grad_sum")
        if n in col_sharded:
            g = g.T
        wv, mv, vv = masters[n]
        res[n] = (g,) + tuple(_adamw(wv, g, mv, vv, f"{n}_adamw"))

    def single(n):
        return [t[None] for t in res[n]]

    def pair(n):
        return [jnp.stack([a, b]) for a, b in zip(res[n + "0"], res[n + "1"])]

    big = dict(hgrn_w_in=single("hgrn_in"), hgrn_w_out=single("hgrn_out"), attn_w_qkv=single("qkv"),
               attn_w_out=single("attn_out"), ffn_w_in=pair("ffn_in"), ffn_w_down=pair("ffn_down"))

    w_small = _pack_small(norm_mix, norm_ffn, hgrn_lb_logits, hgrn_out_norm, final_norm)
    m_small = _pack_small(m_norm_mix, m_norm_ffn, m_hgrn_lb_logits, m_hgrn_out_norm, m_final_norm)
    v_small = _pack_small(v_norm_mix, v_norm_ffn, v_hgrn_lb_logits, v_hgrn_out_norm, v_final_norm)
    g_s, d_s, m_s, v_s, loss = _small_update(small_all, w_small, m_small, v_small, "small_update")
    small_out = [_unpack_small(t) for t in (g_s, d_s, m_s, v_s)]

    def group(i):
        s = small_out[i]
        return (s[0], s[1], big["hgrn_w_in"][i], s[2], s[3], big["hgrn_w_out"][i], big["attn_w_qkv"][i],
                big["attn_w_out"][i], big["ffn_w_in"][i], big["ffn_w_down"][i], s[4])

    return (loss.reshape(()), grad_x[None], *group(0), *group(1), *group(2), *group(3))
```

```python
import functools

import jax
import jax.numpy as jnp
from jax import lax
from jax.experimental import pallas as pl
from jax.experimental.pallas import tpu as pltpu
from jax.experimental.pallas import tpu_sc as plsc

F32 = jnp.float32
BF16 = jnp.bfloat16

D_MODEL = 1024
N_DEV = 8
NORM_EPS = 1e-6

HGRN_HEADS = 8
HGRN_DIM = 128
HGRN_CHUNK = 64
HGRN_STEP_CHUNKS = 2
HGRN_EXP_CLAMP = 60.0

ATTN_DIM = 128
ATTN_BLOCK = 128
ATTN_GROUP_HEADS = 4
ATTN_GROUP_WIDTH = ATTN_GROUP_HEADS * ATTN_DIM
ATTN_DILATIONS = (1, 4, 16)
ATTN_WIDTH = 3 * ATTN_GROUP_WIDTH
ROPE_THETA = 10000.0
NEG_BIG = -1e30

D_FF = 2816

ADAM_LR = 0.001
ADAM_B1 = 0.9
ADAM_B2 = 0.999
ADAM_EPS = 1e-08
ADAM_WD = 0.01
ADAM_STEP = 10

VMEM_LIMIT = 48 * 1024 * 1024

NT = (((1,), (1,)), ((), ()))
NN = (((1,), (0,)), ((), ()))
TN = (((0,), (0,)), ((), ()))


def _dot(a, b, dims):
    return lax.dot_general(a, b, dims, preferred_element_type=F32)


def _params(*sem):
    return pltpu.CompilerParams(dimension_semantics=sem, vmem_limit_bytes=VMEM_LIMIT)


def _pick_tile(n, cap, mult):
    best = None
    for t in range(mult, min(n, cap) + 1, mult):
        if n % t == 0:
            best = t
    assert best is not None, (n, cap, mult)
    return best


def _sigmoid(x):
    return 1.0 / (1.0 + jnp.exp(-x))


def _sigmoid_gate(x):
    return pl.reciprocal(1.0 + jnp.exp(-x), approx=True)


ROW_TILE = 512
COL_CHUNK = 512
GRAD_TILE = 256


def _whole(shape, index_map):
    return pl.BlockSpec(shape, index_map, pipeline_mode=pl.Buffered(1))


def _part_specs(parts, n_cols):
    return [_whole((rows, n_cols), functools.partial(lambda i, b: (b, 0), b=blk)) for _, rows, blk in parts]


def _mm_nt(a, w_parts, *, out_dtype, name, rope=None):
    M, K = a.shape
    tm = _pick_tile(M, ROW_TILE, 16)
    widths = [rows for _, rows, _ in w_parts]
    n_parts = len(w_parts)

    def body(*refs):
        a_ref, w_refs, o_ref = refs[0], refs[1:1 + n_parts], refs[-1]
        av = a_ref[...]
        off = 0
        for p, w_ref in enumerate(w_refs):
            for c0 in range(0, widths[p], COL_CHUNK):
                cw = min(COL_CHUNK, widths[p] - c0)
                acc = _dot(av, w_ref[c0:c0 + cw, :], NT)
                if rope is not None and p < rope[2]:
                    cos, sin = refs[1 + n_parts][...], refs[2 + n_parts][...]
                    for h0 in range(0, cw, ATTN_DIM):
                        xh = acc[:, h0:h0 + ATTN_DIM]
                        rot = pltpu.roll(xh, ATTN_DIM // 2, 1)
                        o_ref[:, off + c0 + h0:off + c0 + h0 + ATTN_DIM] = (xh * cos + rot * sin).astype(out_dtype)
                else:
                    o_ref[:, off + c0:off + c0 + cw] = acc.astype(out_dtype)
            off += widths[p]

    in_specs = [pl.BlockSpec((tm, K), lambda i: (i, 0))] + _part_specs(w_parts, K)
    args = [a] + [w for w, _, _ in w_parts]
    if rope is not None:
        in_specs += [pl.BlockSpec((tm, ATTN_DIM), lambda i: (i, 0))] * 2
        args += [rope[0], rope[1]]
    return pl.pallas_call(
        body, out_shape=jax.ShapeDtypeStruct((M, sum(widths)), out_dtype), grid=(M // tm,),
        in_specs=in_specs, out_specs=pl.BlockSpec((tm, sum(widths)), lambda i: (i, 0)),
        compiler_params=_params("parallel"), name=name)(*args)


def _mm_nn(a_list, w_parts_list, resid, *, name, norm=None, head=None):
    M = a_list[0].shape[0]
    tm = _pick_tile(M, ROW_TILE, 16)
    n_a = len(a_list)
    flat_parts = [p for parts in w_parts_list for p in parts]
    extra = norm if norm is not None else head
    n_in = n_a + len(flat_parts) + (1 if resid is not None else 0) + (2 if extra is not None else 0)

    def body(*refs):
        a_refs, w_refs = refs[:n_a], refs[n_a:n_a + len(flat_parts)]

        def product(rows):
            acc = None
            wi = 0
            for a_ref, parts in zip(a_refs, w_parts_list):
                off = 0
                for _, k, _ in parts:
                    term = _dot(a_ref[rows, off:off + k], w_refs[wi][...], NN)
                    acc = term if acc is None else acc + term
                    off += k
                    wi += 1
            return acc

        if extra is None:
            acc = product(slice(None))
            if resid is not None:
                acc = acc + refs[n_in - 1][...]
            refs[n_in][...] = acc
            return

        @pl.when(pl.program_id(0) == 0)
        def _():
            for acc_ref in refs[n_in + 2:]:
                acc_ref[...] = jnp.zeros_like(acc_ref)

        for r0 in range(0, tm, tm // 2):
            rows = slice(r0, r0 + tm // 2)
            acc = product(rows)
            if head is not None:
                _loss_head_math(acc + refs[n_in - 3][rows, :], rows, refs[n_in - 2], refs[n_in - 1],
                                *refs[n_in:n_in + 4])
                continue
            dres_ref, x_ref, g_ref = refs[n_in - 3:n_in]
            dx_ref, dxb_ref, dg_ref = refs[n_in:n_in + 3]
            xv = x_ref[rows, :]
            rstd = lax.rsqrt(jnp.mean(xv * xv, axis=-1, keepdims=True) + NORM_EPS)
            n = xv * rstd
            dg_ref[...] += jnp.sum(acc * n, axis=0, keepdims=True)
            dn = acc * g_ref[...]
            dx = dres_ref[rows, :] + rstd * (dn - n * jnp.mean(dn * n, axis=-1, keepdims=True))
            dx_ref[rows, :] = dx
            dxb_ref[rows, :] = dx.astype(BF16)

    row = pl.BlockSpec((tm, D_MODEL), lambda i: (i, 0))
    vec = pl.BlockSpec((1, D_MODEL), lambda i: (0, 0))
    in_specs = [pl.BlockSpec((tm, a.shape[1]), lambda i: (i, 0)) for a in a_list] + _part_specs(flat_parts, D_MODEL)
    args = list(a_list) + [w for w, _, _ in flat_parts]
    if resid is not None:
        in_specs.append(row)
        args.append(resid)
    if extra is None:
        return pl.pallas_call(
            body, out_shape=jax.ShapeDtypeStruct((M, D_MODEL), F32), grid=(M // tm,),
            in_specs=in_specs, out_specs=row, compiler_params=_params("parallel"), name=name)(*args)
    assert resid is not None
    out_shape = [jax.ShapeDtypeStruct((M, D_MODEL), F32), jax.ShapeDtypeStruct((M, D_MODEL), BF16),
                 jax.ShapeDtypeStruct((1, D_MODEL), F32)]
    out_specs = [row, row, vec]
    if head is not None:
        out_shape.append(jax.ShapeDtypeStruct((1, D_MODEL), F32))
        out_specs.append(vec)
    return pl.pallas_call(
        body, out_shape=out_shape, grid=(M // tm,), in_specs=in_specs + [row, vec], out_specs=out_specs,
        compiler_params=_params("arbitrary"), name=name)(*args, extra[0], extra[1])


def _mm_tn(a, b, *, name, into=None, row_tile=0, rows=None):
    T, R = a.shape
    N = b.shape[1]
    tr = GRAD_TILE
    rows = R if rows is None else rows

    def body(a_ref, b_ref, *refs):
        refs[-1][...] = _dot(a_ref[...], b_ref[...], TN).astype(BF16)

    in_specs = [pl.BlockSpec((T, tr), lambda r: (0, r)), _whole((T, N), lambda r: (0, 0))]
    args = [a, b]
    if into is not None:
        in_specs.append(HBM_SPEC)
        args.append(into)
    return pl.pallas_call(
        body, out_shape=jax.ShapeDtypeStruct((rows, N), BF16), grid=(R // tr,),
        in_specs=in_specs, out_specs=pl.BlockSpec((tr, N), lambda r: (row_tile + r, 0)),
        input_output_aliases={} if into is None else {2: 0},
        compiler_params=_params("parallel"), name=name)(*args)


def _rms_fwd(x, gain, name):
    T = x.shape[0]
    tm = _pick_tile(T, 512, 16)

    def body(x_ref, g_ref, u_ref):
        xv = x_ref[...]
        rstd = lax.rsqrt(jnp.mean(xv * xv, axis=-1, keepdims=True) + NORM_EPS)
        u_ref[...] = (xv * rstd * g_ref[...]).astype(BF16)

    return pl.pallas_call(
        body, out_shape=jax.ShapeDtypeStruct((T, D_MODEL), BF16), grid=(T // tm,),
        in_specs=[pl.BlockSpec((tm, D_MODEL), lambda i: (i, 0)), pl.BlockSpec((1, D_MODEL), lambda i: (0, 0))],
        out_specs=pl.BlockSpec((tm, D_MODEL), lambda i: (i, 0)),
        compiler_params=_params("parallel"), name=name)(x, gain)


def _rms_bwd(x, gain, dus, dres, name, dilations=(1,)):
    T = x.shape[0]
    tm = _pick_tile(T, PERM_TILE, 16 * max(dilations))
    n_du = len(dus)

    def body(x_ref, g_ref, *refs):
        du_refs, dres_ref = refs[:n_du], refs[n_du]
        dx_ref, dxb_ref, dg_ref, du_scr = refs[n_du + 1:]

        @pl.when(pl.program_id(0) == 0)
        def _():
            dg_ref[...] = jnp.zeros_like(dg_ref)

        if tuple(dilations) == (1,):
            du = du_refs[0][...]
        else:
            for i, (d, du_ref) in enumerate(zip(dilations, du_refs)):
                for j in range(D_MODEL // LANES):
                    lanes = slice(j * LANES, (j + 1) * LANES)
                    if d == 1:
                        du_scr[j] = du_ref[:, lanes] if i == 0 else du_scr[j] + du_ref[:, lanes]
                        continue
                    blk = du_scr.at[j]
                    for r in range(d):
                        rows = _class_rows(r, d, tm)
                        blk[rows, :] = du_ref[r, :, lanes] if i == 0 else blk[rows, :] + du_ref[r, :, lanes]
            du = jnp.concatenate([du_scr[j] for j in range(D_MODEL // LANES)], axis=1)
        xv = x_ref[...]
        rstd = lax.rsqrt(jnp.mean(xv * xv, axis=-1, keepdims=True) + NORM_EPS)
        n = xv * rstd
        dg_ref[...] += jnp.sum(du * n, axis=0, keepdims=True)
        dn = du * g_ref[...]
        dx = dres_ref[...] + rstd * (dn - n * jnp.mean(dn * n, axis=-1, keepdims=True))
        dx_ref[...] = dx
        dxb_ref[...] = dx.astype(BF16)

    row = pl.BlockSpec((tm, D_MODEL), lambda i: (i, 0))
    vec = pl.BlockSpec((1, D_MODEL), lambda i: (0, 0))
    return pl.pallas_call(
        body,
        out_shape=(jax.ShapeDtypeStruct((T, D_MODEL), F32), jax.ShapeDtypeStruct((T, D_MODEL), BF16),
                   jax.ShapeDtypeStruct((1, D_MODEL), F32)),
        grid=(T // tm,), in_specs=[row, vec] + [_residue_spec(d, tm, D_MODEL) for d in dilations] + [row],
        out_specs=(row, row, vec), scratch_shapes=[pltpu.VMEM((D_MODEL // LANES, tm, LANES), F32)],
        compiler_params=_params("arbitrary"), name=name)(
            x, gain, *[_residue_view(du, d) for du, d in zip(dus, dilations)], dres)


def _loss_head_math(hv, rows, t_ref, g_ref, dh_ref, dhb_ref, dg_ref, loss_ref):
    inv_f = 1.0 / D_MODEL
    g = g_ref[...]
    rstd = lax.rsqrt(jnp.mean(hv * hv, axis=-1, keepdims=True) + NORM_EPS)
    n = hv * rstd
    err = n * g - t_ref[rows, :]
    loss_ref[...] += (0.5 * inv_f) * jnp.sum(err * err, axis=0, keepdims=True)
    dy = err * inv_f
    dg_ref[...] += jnp.sum(dy * n, axis=0, keepdims=True)
    dn = dy * g
    dh = rstd * (dn - n * jnp.mean(dn * n, axis=-1, keepdims=True))
    dh_ref[rows, :] = dh
    dhb_ref[rows, :] = dh.astype(BF16)


FFN_TILE = 256


def _ffn_in(h, gain, w_in, name):
    T = h.shape[0]
    tm = _pick_tile(T, ROW_TILE, 16)

    def body(h_ref, g_ref, w_ref, n_ref, gate_ref, up_ref, a_ref):
        hv = h_ref[...]
        rstd = lax.rsqrt(jnp.mean(hv * hv, axis=-1, keepdims=True) + NORM_EPS)
        n = (hv * rstd * g_ref[...]).astype(BF16)
        n_ref[...] = n
        for c0 in range(0, D_FF, FFN_TILE):
            cols = slice(c0, c0 + FFN_TILE)
            gate = _dot(n, w_ref[c0:c0 + FFN_TILE, :], NT)
            up = _dot(n, w_ref[D_FF + c0:D_FF + c0 + FFN_TILE, :], NT)
            gate_ref[:, cols] = gate.astype(BF16)
            up_ref[:, cols] = up.astype(BF16)
            a_ref[:, cols] = (gate * _sigmoid(gate) * up).astype(BF16)

    row = pl.BlockSpec((tm, D_MODEL), lambda i: (i, 0))
    wide = pl.BlockSpec((tm, D_FF), lambda i: (i, 0))
    wide_shape = jax.ShapeDtypeStruct((T, D_FF), BF16)
    return pl.pallas_call(
        body, out_shape=(jax.ShapeDtypeStruct((T, D_MODEL), BF16), wide_shape, wide_shape, wide_shape),
        grid=(T // tm,),
        in_specs=[row, pl.BlockSpec((1, D_MODEL), lambda i: (0, 0)), _whole((2 * D_FF, D_MODEL), lambda i: (0, 0))],
        out_specs=(row, wide, wide, wide), compiler_params=_params("parallel"), name=name)(h, gain, w_in)


def _ffn_down_dx(dhb, w_down, gate, up, name):
    T = dhb.shape[0]
    tm = _pick_tile(T, ROW_TILE, 16)

    def body(dh_ref, w_ref, gate_ref, up_ref, dgate_ref, dup_ref):
        dh = dh_ref[...]
        for c0 in range(0, D_FF, FFN_TILE):
            cols = slice(c0, c0 + FFN_TILE)
            da = _dot(dh, w_ref[c0:c0 + FFN_TILE, :], NT)
            gate = gate_ref[:, cols].astype(F32)
            sg = _sigmoid(gate)
            dgate_ref[:, cols] = (da * up_ref[:, cols].astype(F32) * (sg * (1.0 + gate * (1.0 - sg)))).astype(BF16)
            dup_ref[:, cols] = (da * gate * sg).astype(BF16)

    wide = pl.BlockSpec((tm, D_FF), lambda i: (i, 0))
    wide_shape = jax.ShapeDtypeStruct((T, D_FF), BF16)
    return pl.pallas_call(
        body, out_shape=(wide_shape, wide_shape), grid=(T // tm,),
        in_specs=[pl.BlockSpec((tm, D_MODEL), lambda i: (i, 0)), _whole((D_FF, D_MODEL), lambda i: (0, 0)), wide, wide],
        out_specs=(wide, wide), compiler_params=_params("parallel"), name=name)(dhb, w_down, gate, up)


def _tri(n, lower):
    r = lax.broadcasted_iota(jnp.int32, (n, n), 0)
    c = lax.broadcasted_iota(jnp.int32, (n, n), 1)
    return (c <= r) if lower else (c >= r)


def _running_sum(x, lower):
    tri = _tri(x.shape[0], lower).astype(BF16)
    hi = x.astype(BF16)
    rest = x - hi.astype(F32)
    mid = rest.astype(BF16)
    lo = (rest - mid.astype(F32)).astype(BF16)
    return _dot(tri, hi, NN) + _dot(tri, mid, NN) + _dot(tri, lo, NN)


def _hgrn_gates(q_raw, f_raw, lb):
    C = q_raw.shape[0]
    sig_f = _sigmoid(f_raw)
    forget = lb + (1.0 - lb) * sig_f
    key = 1.0 - forget
    log_f = jnp.log(forget)
    b = _running_sum(log_f, True)
    first_half = lax.broadcasted_iota(jnp.int32, log_f.shape, 0) < C // 2
    r = jnp.sum(jnp.where(first_half, log_f, 0.0), axis=0, keepdims=True)
    b_last = jnp.sum(log_f, axis=0, keepdims=True)
    e_a = jnp.exp(jnp.minimum(b - r, HGRN_EXP_CLAMP))
    e_b = jnp.exp(jnp.minimum(r - b, HGRN_EXP_CLAMP))
    e_q = jnp.exp(b)
    e_k = jnp.exp(b_last - b)
    sig_q = _sigmoid_gate(q_raw)
    query = q_raw * sig_q
    return dict(sig_f=sig_f, forget=forget, sig_q=sig_q, e_a=e_a, e_b=e_b, e_q=e_q, e_k=e_k,
                e_last=jnp.exp(b_last), q_a=query * e_a, k_b=key * e_b, q_hat=query * e_q, k_til=key * e_k)


def _hgrn_fwd(proj, lb, gain, name):
    T = proj.shape[0]
    C = HGRN_CHUNK
    CPS = HGRN_STEP_CHUNKS
    H, HD = HGRN_HEADS, HGRN_DIM

    def body(q_ref, f_ref, i_ref, g_ref, lb_ref, gain_ref, og_ref, o_ref, st_ref, s_scr):
        @pl.when(pl.program_id(0) == 0)
        def _():
            s_scr[...] = jnp.zeros_like(s_scr)

        causal = _tri(C, True)
        gain_v = gain_ref[...]
        heads = [slice(h * HD, (h + 1) * HD) for h in range(H)]
        s_t = [s_scr[h] for h in range(H)]
        for cc in range(CPS):
            rows = slice(cc * C, (cc + 1) * C)
            for h in range(H):
                st_ref[cc, h] = s_t[h]
            gt = _hgrn_gates(q_ref[rows, :], f_ref[rows, :], lb_ref[...])
            q_a, k_b = gt["q_a"].astype(BF16), gt["k_b"].astype(BF16)
            q_hat, k_til = gt["q_hat"].astype(BF16), gt["k_til"].astype(BF16)
            v = i_ref[rows, :].astype(BF16)
            p = [jnp.where(causal, _dot(q_a[:, sl], k_b[:, sl], NT), 0.0).astype(BF16) for sl in heads]
            o = [_dot(p[h], v[:, sl], NN) + _dot(q_hat[:, sl], s_t[h].astype(BF16), NT)
                 for h, sl in enumerate(heads)]
            s_t = [gt["e_last"][:, sl] * s_t[h] + _dot(v[:, sl], k_til[:, sl], TN) for h, sl in enumerate(heads)]
            for h, sl in enumerate(heads):
                o_ref[rows, sl] = o[h]
                rstd = lax.rsqrt(jnp.mean(o[h] * o[h], axis=-1, keepdims=True) + NORM_EPS)
                g_raw = g_ref[rows, sl]
                og_ref[rows, sl] = (o[h] * rstd * gain_v * (g_raw * _sigmoid_gate(g_raw))).astype(BF16)
        for h in range(H):
            s_scr[h] = s_t[h]

    col = lambda j: pl.BlockSpec((CPS * C, D_MODEL), lambda c: (c, j))
    row = pl.BlockSpec((CPS * C, D_MODEL), lambda c: (c, 0))
    return pl.pallas_call(
        body,
        out_shape=(jax.ShapeDtypeStruct((T, D_MODEL), BF16), jax.ShapeDtypeStruct((T, D_MODEL), F32),
                   jax.ShapeDtypeStruct((T // C, H, HD, HD), F32)),
        grid=(T // (CPS * C),),
        in_specs=[col(0), col(1), col(2), col(3), pl.BlockSpec((1, D_MODEL), lambda c: (0, 0)),
                  pl.BlockSpec((1, HD), lambda c: (0, 0))],
        out_specs=(row, row, pl.BlockSpec((CPS, H, HD, HD), lambda c: (c, 0, 0, 0))),
        scratch_shapes=[pltpu.VMEM((H, HD, HD), F32)],
        compiler_params=_params("arbitrary"), name=name)(proj, proj, proj, proj, lb, gain)


def _hgrn_bwd(proj, o_pre, d_og, states, lb, gain, name):
    T = proj.shape[0]
    C = HGRN_CHUNK
    CPS = HGRN_STEP_CHUNKS
    H, HD = HGRN_HEADS, HGRN_DIM
    NC = T // (CPS * C)

    def body(q_ref, f_ref, i_ref, g_ref, o_ref, dog_ref, st_ref, lb_ref, gain_ref,
             dproj_ref, dlb_ref, dgain_ref, ds_scr, dq_all, dk_all, db_all):
        @pl.when(pl.program_id(0) == 0)
        def _():
            ds_scr[...] = jnp.zeros_like(ds_scr)
            dlb_ref[...] = jnp.zeros_like(dlb_ref)
            dgain_ref[...] = jnp.zeros_like(dgain_ref)

        lbv = lb_ref[...]
        causal = _tri(C, True)
        last_row = lax.broadcasted_iota(jnp.int32, (C, HD), 0) == C - 1
        gain_v = gain_ref[...]
        heads = [slice(h * HD, (h + 1) * HD) for h in range(H)]
        hs = range(H)
        ds_t = [ds_scr[h] for h in hs]
        dgain = None
        for cc in reversed(range(CPS)):
            rows = slice(cc * C, (cc + 1) * C)
            dq_scr, dk_scr, db_scr = dq_all.at[cc], dk_all.at[cc], db_all.at[cc]
            q_raw = q_ref[rows, :]
            gt = _hgrn_gates(q_raw, f_ref[rows, :], lbv)
            o = [o_ref[rows, sl] for sl in heads]
            rstd = [lax.rsqrt(jnp.mean(x * x, axis=-1, keepdims=True) + NORM_EPS) for x in o]
            n = [x * r for x, r in zip(o, rstd)]
            g_raw = [g_ref[rows, sl] for sl in heads]
            sg = [_sigmoid_gate(x) for x in g_raw]
            d_out = [dog_ref[rows, sl] for sl in heads]
            dy = [d * (g * s) for d, g, s in zip(d_out, g_raw, sg)]
            dn = [x * gain_v for x in dy]
            do = [(rstd[h] * (dn[h] - n[h] * jnp.mean(dn[h] * n[h], axis=-1, keepdims=True))).astype(BF16) for h in hs]
            for h in hs:
                dgain = dy[h] * n[h] if dgain is None else dgain + dy[h] * n[h]
            for h, sl in enumerate(heads):
                dproj_ref[rows, 3 * D_MODEL + h * HD:3 * D_MODEL + (h + 1) * HD] = (
                    d_out[h] * n[h] * gain_v * (sg[h] * (1.0 + g_raw[h] * (1.0 - sg[h])))).astype(BF16)
            q_ab, k_bb = gt["q_a"].astype(BF16), gt["k_b"].astype(BF16)
            q_hb, k_tb = gt["q_hat"].astype(BF16), gt["k_til"].astype(BF16)
            v = i_ref[rows, :].astype(BF16)
            s_t = [st_ref[cc, h] for h in hs]
            ds_b = [x.astype(BF16) for x in ds_t]
            p = [jnp.where(causal, _dot(q_ab[:, sl], k_bb[:, sl], NT), 0.0).astype(BF16) for sl in heads]
            dp = [jnp.where(causal, _dot(do[h], v[:, sl], NT), 0.0).astype(BF16) for h, sl in enumerate(heads)]
            dv = [_dot(p[h], do[h], TN) + _dot(k_tb[:, sl], ds_b[h], NT) for h, sl in enumerate(heads)]
            dq_a = [_dot(dp[h], k_bb[:, sl], NN) for h, sl in enumerate(heads)]
            dk_b = [_dot(dp[h], q_ab[:, sl], TN) for h, sl in enumerate(heads)]
            dq_hat = [_dot(do[h], s_t[h].astype(BF16), NN) for h in hs]
            dk_til = [_dot(v[:, sl], ds_b[h], NN) for h, sl in enumerate(heads)]
            ds_new = [_dot(do[h], q_hb[:, sl], TN) + gt["e_last"][:, sl] * ds_t[h] for h, sl in enumerate(heads)]
            for h, sl in enumerate(heads):
                k_til = gt["k_til"][:, sl]
                db_last = jnp.sum(ds_t[h] * gt["e_last"][:, sl] * s_t[h], axis=0, keepdims=True) + jnp.sum(
                    dk_til[h] * k_til, axis=0, keepdims=True)
                dproj_ref[rows, 2 * D_MODEL + h * HD:2 * D_MODEL + (h + 1) * HD] = dv[h].astype(BF16)
                dq_scr[:, sl] = dq_a[h] * gt["e_a"][:, sl] + dq_hat[h] * gt["e_q"][:, sl]
                dk_scr[:, sl] = dk_b[h] * gt["e_b"][:, sl] + dk_til[h] * gt["e_k"][:, sl]
                db = (dq_a[h] * q_ab[:, sl].astype(F32) + dq_hat[h] * gt["q_hat"][:, sl]
                      - dk_b[h] * k_bb[:, sl].astype(F32) - dk_til[h] * k_til)
                db_scr[:, sl] = db + jnp.where(last_row, db_last, 0.0)
            dlogf = _running_sum(db_scr[...], False)
            sig_f, forget, sig_q = gt["sig_f"], gt["forget"], gt["sig_q"]
            dforget = dlogf / forget - dk_scr[...]
            dproj_ref[rows, D_MODEL:2 * D_MODEL] = (dforget * (1.0 - lbv) * sig_f * (1.0 - sig_f)).astype(BF16)
            dlb_ref[...] += jnp.sum(dforget * (1.0 - sig_f), axis=0, keepdims=True)
            dproj_ref[rows, 0:D_MODEL] = (dq_scr[...] * (sig_q * (1.0 + q_raw * (1.0 - sig_q)))).astype(BF16)
            ds_t = ds_new
        dgain_ref[...] += jnp.sum(dgain, axis=0, keepdims=True)
        for h in hs:
            ds_scr[h] = ds_t[h]

    col = lambda j: pl.BlockSpec((CPS * C, D_MODEL), lambda c: (NC - 1 - c, j))
    row = pl.BlockSpec((CPS * C, D_MODEL), lambda c: (NC - 1 - c, 0))
    return pl.pallas_call(
        body,
        out_shape=(jax.ShapeDtypeStruct((T, 4 * D_MODEL), BF16), jax.ShapeDtypeStruct((1, D_MODEL), F32),
                   jax.ShapeDtypeStruct((1, HD), F32)),
        grid=(NC,),
        in_specs=[col(0), col(1), col(2), col(3), row, row,
                  pl.BlockSpec((CPS, H, HD, HD), lambda c: (NC - 1 - c, 0, 0, 0)),
                  pl.BlockSpec((1, D_MODEL), lambda c: (0, 0)), pl.BlockSpec((1, HD), lambda c: (0, 0))],
        out_specs=(pl.BlockSpec((CPS * C, 4 * D_MODEL), lambda c: (NC - 1 - c, 0)),
                   pl.BlockSpec((1, D_MODEL), lambda c: (0, 0)), pl.BlockSpec((1, HD), lambda c: (0, 0))),
        scratch_shapes=[pltpu.VMEM((H, HD, HD), F32)] + [pltpu.VMEM((CPS, C, D_MODEL), F32)] * 3,
        compiler_params=_params("arbitrary"), name=name)(proj, proj, proj, proj, o_pre, d_og, states, lb, gain)


def _attn_masks():
    r = lax.broadcasted_iota(jnp.int32, (ATTN_BLOCK, ATTN_BLOCK), 0)
    c = lax.broadcasted_iota(jnp.int32, (ATTN_BLOCK, ATTN_BLOCK), 1)
    return c >= r, c <= r


def _attn_fwd(qkv, dilation, name):
    T = qkv.shape[0]
    nb = T // dilation // ATTN_BLOCK
    W = ATTN_GROUP_WIDTH
    B = ATTN_BLOCK
    scale = ATTN_DIM ** -0.5
    qb = 2 if nb % 2 == 0 else 1
    steps = nb // qb

    def body(q_ref, kp_ref, kc_ref, vp_ref, vc_ref, o_ref, lse_ref):
        no_prev = jnp.where(pl.program_id(1) > 0, 0.0, NEG_BIG)
        m_prev, m_cur = _attn_masks()
        ones = jnp.ones((B, ATTN_DIM), BF16)
        items = []
        for j in range(qb):
            for h in range(ATTN_GROUP_HEADS):
                sl = slice(h * ATTN_DIM, (h + 1) * ATTN_DIM)
                rows = slice(j * B, (j + 1) * B)
                if j == 0:
                    items.append((rows, sl, kp_ref[:, sl], vp_ref[:, sl], no_prev))
                else:
                    before = slice((j - 1) * B, j * B)
                    items.append((rows, sl, kc_ref[before, sl], vc_ref[before, sl], 0.0))
        s_p = [jnp.where(m_prev, _dot(q_ref[rows, sl], k_p, NT) * scale + bias, NEG_BIG)
               for rows, sl, k_p, _, bias in items]
        s_c = [jnp.where(m_cur, _dot(q_ref[rows, sl], kc_ref[rows, sl], NT) * scale, NEG_BIG)
               for rows, sl, _, _, _ in items]
        m = [jnp.max(jnp.maximum(a, b), axis=-1, keepdims=True) for a, b in zip(s_p, s_c)]
        p_p = [jnp.exp(a - mx).astype(BF16) for a, mx in zip(s_p, m)]
        p_c = [jnp.exp(b - mx).astype(BF16) for b, mx in zip(s_c, m)]
        l = [_dot(a, ones, NN) + _dot(b, ones, NN) for a, b in zip(p_p, p_c)]
        acc = [_dot(a, v_p, NN) + _dot(b, vc_ref[rows, sl], NN)
               for a, b, (rows, sl, _, v_p, _) in zip(p_p, p_c, items)]
        for (rows, sl, _, _, _), a, lv, mx in zip(items, acc, l, m):
            o_ref[rows, sl] = (a / lv).astype(BF16)
            lse_ref[rows, sl] = mx + jnp.log(lv)

    cur = lambda col: pl.BlockSpec((qb * B, W), lambda s, n: (s * steps + n, col))
    prev = lambda col: pl.BlockSpec((B, W), lambda s, n: (s * nb + jnp.maximum(qb * n - 1, 0), col))
    out = pl.BlockSpec((qb * B, W), lambda s, n: (s * steps + n, 0))
    return pl.pallas_call(
        body, out_shape=(jax.ShapeDtypeStruct((T, W), BF16), jax.ShapeDtypeStruct((T, W), F32)),
        grid=(dilation, steps),
        in_specs=[cur(0), prev(1), cur(1), prev(2), cur(2)],
        out_specs=(out, out), compiler_params=_params("parallel", "arbitrary"), name=name)(qkv, qkv, qkv, qkv, qkv)


def _attn_bwd(qkv, d_out, lse, delta, cos, sin, dilation, name):
    T = qkv.shape[0]
    nb = T // dilation // ATTN_BLOCK
    assert nb % 2 == 0, "an even number of 128-token blocks per residue class"
    pairs = nb // 2
    W = ATTN_GROUP_WIDTH
    B = ATTN_BLOCK
    scale = ATTN_DIM ** -0.5

    def unrope(x, cos_v, sin_v):
        return x * cos_v + pltpu.roll(x * sin_v, ATTN_DIM // 2, 1)

    def body(qa_ref, qb_ref, kpair_ref, kc_ref, vpair_ref, vc_ref, doa_ref, dob_ref, lsea_ref, lseb_ref,
             dla_ref, dlb_ref, cos_ref, sin_ref, out_ref, dq_scr, dk_scr, dv_scr):
        n = pl.program_id(1)

        @pl.when(n == 0)
        def _():
            dq_scr[...] = jnp.zeros_like(dq_scr)
            dk_scr[...] = jnp.zeros_like(dk_scr)
            dv_scr[...] = jnp.zeros_like(dv_scr)

        no_a = jnp.where(n > 0, 0.0, NEG_BIG)
        no_b = jnp.where(n < pairs, 0.0, NEG_BIG)
        m_prev, m_cur = _attn_masks()
        lo, hi = slice(0, B), slice(B, 2 * B)
        heads = [slice(h * ATTN_DIM, (h + 1) * ATTN_DIM) for h in range(ATTN_GROUP_HEADS)]
        flat = []
        for sl in heads:
            qa, qb = qa_ref[:, sl], qb_ref[:, sl]
            doa, dob = doa_ref[:, sl], dob_ref[:, sl]
            k0, k1, k2 = kpair_ref[lo, sl], kpair_ref[hi, sl], kc_ref[:, sl]
            v0, v1, v2 = vpair_ref[lo, sl], vpair_ref[hi, sl], vc_ref[:, sl]
            flat += [(qa, doa, lsea_ref[:, sl], dla_ref[:, sl], k0, v0, m_prev, no_a),
                     (qa, doa, lsea_ref[:, sl], dla_ref[:, sl], k1, v1, m_cur, no_a),
                     (qb, dob, lseb_ref[:, sl], dlb_ref[:, sl], k1, v1, m_prev, no_a + no_b),
                     (qb, dob, lseb_ref[:, sl], dlb_ref[:, sl], k2, v2, m_cur, no_b)]
        s = [_dot(q, k, NT) for q, _, _, _, k, _, _, _ in flat]
        dp = [_dot(do, v, NT) for _, do, _, _, _, v, _, _ in flat]
        p = [jnp.where(mask, jnp.exp(sv * scale - lse_v + bias), 0.0)
             for sv, (_, _, lse_v, _, _, _, mask, bias) in zip(s, flat)]
        ds = [(pv * (dpv - dl_v) * scale).astype(BF16) for pv, dpv, (_, _, _, dl_v, _, _, _, _) in zip(p, dp, flat)]
        p = [pv.astype(BF16) for pv in p]
        dq_part = [_dot(dsv, k, NN) for dsv, (_, _, _, _, k, _, _, _) in zip(ds, flat)]
        dk_part = [_dot(dsv, q, TN) for dsv, (q, _, _, _, _, _, _, _) in zip(ds, flat)]
        dv_part = [_dot(pv, do, TN) for pv, (_, do, _, _, _, _, _, _) in zip(p, flat)]
        cos_lo, sin_lo, cos_hi, sin_hi = cos_ref[lo, :], sin_ref[lo, :], cos_ref[hi, :], sin_ref[hi, :]
        for h, sl in enumerate(heads):
            a_prev, a_cur, b_prev, b_cur = range(4 * h, 4 * h + 4)
            kcol = slice(W + h * ATTN_DIM, W + (h + 1) * ATTN_DIM)
            vcol = slice(2 * W + h * ATTN_DIM, 2 * W + (h + 1) * ATTN_DIM)
            out_ref[lo, sl] = unrope(dq_scr[:, sl], cos_lo, sin_lo).astype(BF16)
            out_ref[hi, sl] = unrope(dq_part[a_prev] + dq_part[a_cur], cos_hi, sin_hi).astype(BF16)
            out_ref[lo, kcol] = unrope(dk_scr[:, sl] + dk_part[a_prev], cos_lo, sin_lo).astype(BF16)
            out_ref[hi, kcol] = unrope(dk_part[a_cur] + dk_part[b_prev], cos_hi, sin_hi).astype(BF16)
            out_ref[lo, vcol] = (dv_scr[:, sl] + dv_part[a_prev]).astype(BF16)
            out_ref[hi, vcol] = (dv_part[a_cur] + dv_part[b_prev]).astype(BF16)
            dq_scr[:, sl] = dq_part[b_prev] + dq_part[b_cur]
            dk_scr[:, sl] = dk_part[b_cur]
            dv_scr[:, sl] = dv_part[b_cur]

    def block_a(n):
        return jnp.maximum(2 * n - 1, 0)

    def block_b(n):
        return jnp.minimum(2 * n, nb - 1)

    def pair(n):
        return jnp.maximum(n - 1, 0)

    one_a = lambda col: pl.BlockSpec((B, W), lambda s, n: (s * nb + block_a(n), col))
    one_b = lambda col: pl.BlockSpec((B, W), lambda s, n: (s * nb + block_b(n), col))
    two = lambda col: pl.BlockSpec((2 * B, W), lambda s, n: (s * pairs + pair(n), col))
    tab = pl.BlockSpec((2 * B, ATTN_DIM), lambda s, n: (s * pairs + pair(n), 0))
    return pl.pallas_call(
        body, out_shape=jax.ShapeDtypeStruct((T, 3 * W), BF16), grid=(dilation, pairs + 1),
        in_specs=[one_a(0), one_b(0), two(1), one_b(1), two(2), one_b(2), one_a(0), one_b(0), one_a(0), one_b(0),
                  one_a(0), one_b(0), tab, tab],
        out_specs=pl.BlockSpec((2 * B, 3 * W), lambda s, n: (s * pairs + pair(n), 0)),
        scratch_shapes=[pltpu.VMEM((B, W), F32)] * 3,
        compiler_params=_params("parallel", "arbitrary"), name=name)(
            qkv, qkv, qkv, qkv, qkv, qkv, d_out, d_out, lse, lse, delta, delta, cos, sin)


PERM_TILE = 512
LANES = 128


def _residue_view(x, d):
    return x if d == 1 else x.reshape(d, x.shape[0] // d, x.shape[1])


def _residue_spec(d, tm, cols):
    if d == 1:
        return pl.BlockSpec((tm, cols), lambda i: (i, 0))
    return pl.BlockSpec((d, tm // d, cols), lambda i: (0, i, 0))


def _residue_shape(T, d, cols, dtype):
    return jax.ShapeDtypeStruct((T, cols) if d == 1 else (d, T // d, cols), dtype)


def _class_rows(r, d, tm):
    return pl.ds(r, tm // d, stride=d)


def _attn_norm(h, gain, cos, sin, name):
    T = h.shape[0]
    tm = _pick_tile(T, PERM_TILE, 16 * max(ATTN_DILATIONS))
    dils = ATTN_DILATIONS

    def body(h_ref, g_ref, cos_ref, sin_ref, *refs):
        u_refs, c_refs, s_refs, u_scr = refs[0:3], refs[3:6], refs[6:9], refs[9]
        hv = h_ref[...]
        rstd = lax.rsqrt(jnp.mean(hv * hv, axis=-1, keepdims=True) + NORM_EPS)
        u = hv * rstd * g_ref[...]
        for j in range(D_MODEL // LANES):
            u_scr[j] = u[:, j * LANES:(j + 1) * LANES]
        for d, u_ref, c_ref, s_ref in zip(dils, u_refs, c_refs, s_refs):
            if d == 1:
                u_ref[...] = u.astype(BF16)
                c_ref[...] = cos_ref[...]
                s_ref[...] = sin_ref[...]
                continue
            for r in range(d):
                rows = _class_rows(r, d, tm)
                for j in range(D_MODEL // LANES):
                    u_ref[r, :, j * LANES:(j + 1) * LANES] = u_scr.at[j][rows, :].astype(BF16)
                c_ref[r] = cos_ref[rows, :]
                s_ref[r] = sin_ref[rows, :]

    row = pl.BlockSpec((tm, D_MODEL), lambda i: (i, 0))
    tab = pl.BlockSpec((tm, ATTN_DIM), lambda i: (i, 0))
    res = pl.pallas_call(
        body,
        out_shape=([_residue_shape(T, d, D_MODEL, BF16) for d in dils]
                   + [_residue_shape(T, d, ATTN_DIM, F32) for d in dils] * 2),
        grid=(T // tm,), in_specs=[row, pl.BlockSpec((1, D_MODEL), lambda i: (0, 0)), tab, tab],
        out_specs=([_residue_spec(d, tm, D_MODEL) for d in dils] + [_residue_spec(d, tm, ATTN_DIM) for d in dils] * 2),
        scratch_shapes=[pltpu.VMEM((D_MODEL // LANES, tm, LANES), F32)],
        compiler_params=_params("parallel"), name=name)(h, gain, cos, sin)
    flat = [r.reshape(T, r.shape[-1]) for r in res]
    return flat[0:3], flat[3:6], flat[6:9]


def _attn_merge_fwd(outs, lses, name):
    T = outs[0].shape[0]
    W = ATTN_GROUP_WIDTH
    tm = _pick_tile(T, PERM_TILE, 16 * max(ATTN_DILATIONS))
    dils = ATTN_DILATIONS

    def body(*refs):
        o_refs, l_refs, oc_ref, lse_refs = refs[0:3], refs[3:6], refs[6], refs[7:10]
        o_scr, l_scr, t_scr = refs[10:13]
        nh = ATTN_GROUP_HEADS
        for g, d in enumerate(dils):
            for j in range(nh):
                lanes = slice(j * LANES, (j + 1) * LANES)
                if d == 1:
                    o_scr[g * nh + j] = o_refs[g][:, lanes].astype(F32)
                    l_scr[g * nh + j] = l_refs[g][:, lanes]
                    continue
                for r in range(d):
                    rows = _class_rows(r, d, tm)
                    o_scr.at[g * nh + j][rows, :] = o_refs[g][r, :, lanes].astype(F32)
                    l_scr.at[g * nh + j][rows, :] = l_refs[g][r, :, lanes]
        for j in range(nh):
            lanes = slice(j * LANES, (j + 1) * LANES)
            ls = [l_scr[g * nh + j] for g in range(3)]
            m = jnp.maximum(jnp.maximum(ls[0], ls[1]), ls[2])
            tot = m + jnp.log(jnp.exp(ls[0] - m) + jnp.exp(ls[1] - m) + jnp.exp(ls[2] - m))
            t_scr[j] = tot
            for g, d in enumerate(dils):
                oc_ref[:, g * W + j * LANES:g * W + (j + 1) * LANES] = (
                    o_scr[g * nh + j] * jnp.exp(ls[g] - tot)).astype(BF16)
                if d == 1:
                    lse_refs[g][:, lanes] = tot
                    continue
                for r in range(d):
                    lse_refs[g][r, :, lanes] = t_scr.at[j][_class_rows(r, d, tm), :]

    in_blk = [_residue_spec(d, tm, W) for d in dils]
    n_blk = 3 * ATTN_GROUP_HEADS
    res = pl.pallas_call(
        body, out_shape=[jax.ShapeDtypeStruct((T, 3 * W), BF16)] + [_residue_shape(T, d, W, F32) for d in dils],
        grid=(T // tm,), in_specs=in_blk * 2,
        out_specs=[pl.BlockSpec((tm, 3 * W), lambda i: (i, 0))] + in_blk,
        scratch_shapes=[pltpu.VMEM((n_blk, tm, LANES), F32), pltpu.VMEM((n_blk, tm, LANES), F32),
                        pltpu.VMEM((ATTN_GROUP_HEADS, tm, LANES), F32)],
        compiler_params=_params("parallel"), name=name)(
            *[_residue_view(o, d) for o, d in zip(outs, dils)], *[_residue_view(l, d) for l, d in zip(lses, dils)])
    return res[0], [r.reshape(T, W) for r in res[1:]]


def _attn_merge_bwd(d_oc, oc, name):
    T = d_oc.shape[0]
    W = ATTN_GROUP_WIDTH
    tm = _pick_tile(T, PERM_TILE, 16 * max(ATTN_DILATIONS))
    dils = ATTN_DILATIONS

    def body(d_ref, o_ref, *refs):
        delta_refs, db_refs, dl_scr, d_scr = refs[0:3], refs[3:6], refs[6], refs[7]
        nh = ATTN_GROUP_HEADS
        for j in range(nh):
            tot = jnp.zeros((tm, 1), F32)
            for g in range(3):
                cols = slice(g * W + j * LANES, g * W + (j + 1) * LANES)
                d_blk = d_ref[:, cols]
                d_scr[g * nh + j] = d_blk
                tot = tot + jnp.sum(d_blk * o_ref[:, cols].astype(F32), axis=-1, keepdims=True)
            dl_scr[j] = jnp.broadcast_to(tot, (tm, LANES))
        for g, d in enumerate(dils):
            for j in range(nh):
                lanes = slice(j * LANES, (j + 1) * LANES)
                if d == 1:
                    delta_refs[g][:, lanes] = dl_scr[j]
                    db_refs[g][:, lanes] = d_scr[g * nh + j].astype(BF16)
                    continue
                for r in range(d):
                    rows = _class_rows(r, d, tm)
                    delta_refs[g][r, :, lanes] = dl_scr.at[j][rows, :]
                    db_refs[g][r, :, lanes] = d_scr.at[g * nh + j][rows, :].astype(BF16)

    wide = pl.BlockSpec((tm, 3 * W), lambda i: (i, 0))
    out_blk = [_residue_spec(d, tm, W) for d in dils]
    res = pl.pallas_call(
        body, out_shape=[_residue_shape(T, d, W, F32) for d in dils] + [_residue_shape(T, d, W, BF16) for d in dils],
        grid=(T // tm,), in_specs=[wide, wide], out_specs=out_blk * 2,
        scratch_shapes=[pltpu.VMEM((ATTN_GROUP_HEADS, tm, LANES), F32),
                        pltpu.VMEM((3 * ATTN_GROUP_HEADS, tm, LANES), F32)],
        compiler_params=_params("parallel"), name=name)(d_oc, oc)
    flat = [r.reshape(T, W) for r in res]
    return flat[0:3], flat[3:6]


def _rope_tables(T):
    inv_freq = 1.0 / (ROPE_THETA ** (jnp.arange(0, ATTN_DIM, 2, dtype=F32) / ATTN_DIM))
    ang = jnp.arange(T, dtype=F32)[:, None] * inv_freq[None, :]
    cos, sin = jnp.cos(ang), jnp.sin(ang)
    return jnp.concatenate([cos, cos], axis=1), jnp.concatenate([-sin, sin], axis=1)


WEIGHT_GROUPS = {"hgrn": ("hgrn_in", "hgrn_out"), "ffn0": ("ffn_in0", "ffn_down0"),
                 "attn": ("qkv", "attn_out"), "ffn1": ("ffn_in1", "ffn_down1")}


def _local_step(x, target, norm_mix, norm_ffn, lb, out_gain, final_gain, fetch, publish):
    T = x.shape[0]
    g_mix = [norm_mix[0:1], norm_mix[1:2]]
    g_ffn = [norm_ffn[0:1], norm_ffn[1:2]]
    w = {}

    def whole(name):
        return [(w[name], w[name].shape[0], 0)]

    def qkv_parts(g):
        return [(w["qkv"], ATTN_GROUP_WIDTH, 3 * j + g) for j in range(3)]

    def ffn_fwd(h, layer, head=None):
        w.update(fetch(f"ffn{layer}"))
        n, gate, up, a = _ffn_in(h, g_ffn[layer], w[f"ffn_in{layer}"], f"ffn{layer}_in")
        out = _mm_nn([a], [whole(f"ffn_down{layer}")], h, name=f"ffn{layer}_down", head=head)
        return out, (n, gate, up, a)

    def ffn_bwd(h, saved, dh, dhb, layer):
        n, gate, up, a = saved
        w_in = w[f"ffn_in{layer}"]
        dgate, dup = _ffn_down_dx(dhb, w[f"ffn_down{layer}"], gate, up, f"ffn{layer}_down_dx")
        grad_in = _mm_tn(dgate, n, name=f"ffn{layer}_in_dw_gate", rows=2 * D_FF)
        grad_in = _mm_tn(dup, n, name=f"ffn{layer}_in_dw_up", into=grad_in, row_tile=D_FF // GRAD_TILE, rows=2 * D_FF)
        grads = {f"ffn_down{layer}": _mm_tn(a, dhb, name=f"ffn{layer}_down_dw"), f"ffn_in{layer}": grad_in}
        publish(f"ffn{layer}", grads)
        return _mm_nn([dgate, dup], [[(w_in, D_FF, 0)], [(w_in, D_FF, 1)]], dh, name=f"ffn{layer}_in_dx",
                      norm=(h, g_ffn[layer]))

    u0 = _rms_fwd(x, g_mix[0], "hgrn_norm")
    w.update(fetch("hgrn"))
    proj = _mm_nt(u0, whole("hgrn_in"), out_dtype=F32, name="hgrn_in")
    og, o_pre, states = _hgrn_fwd(proj, lb, out_gain, "hgrn_fwd")
    h1 = _mm_nn([og], [whole("hgrn_out")], x, name="hgrn_out")
    h2, ffn0 = ffn_fwd(h1, 0)

    cos, sin = _rope_tables(T)
    u1_g, cos_g, sin_g = _attn_norm(h2, g_mix[1], cos, sin, "attn_norm")
    w.update(fetch("attn"))
    qkv_g, outs, lses = [], [], []
    for g, d in enumerate(ATTN_DILATIONS):
        qkv_g.append(_mm_nt(u1_g[g], qkv_parts(g), out_dtype=BF16, name=f"attn_qkv{g}",
                            rope=(cos_g[g], sin_g[g], 2)))
        o_g, lse_g = _attn_fwd(qkv_g[g], d, f"attn_fwd{g}")
        outs.append(o_g)
        lses.append(lse_g)
    oc, lse_all = _attn_merge_fwd(outs, lses, "attn_merge")
    h3 = _mm_nn([oc], [whole("attn_out")], h2, name="attn_out")
    (dh4, dh4b, d_final, loss_part), ffn1 = ffn_fwd(h3, 1, head=(target, final_gain))
    dh3, dh3b, d_ffn1 = ffn_bwd(h3, ffn1, dh4, dh4b, 1)

    d_oc = _mm_nt(dh3b, whole("attn_out"), out_dtype=F32, name="attn_out_dx")
    grad_attn_out = _mm_tn(oc, dh3b, name="attn_out_dw")
    delta, d_ocb = _attn_merge_bwd(d_oc, oc, "attn_merge_bwd")
    du1, qkv_pieces = [], []
    for g, d in enumerate(ATTN_DILATIONS):
        dqkv = _attn_bwd(qkv_g[g], d_ocb[g], lse_all[g], delta[g], cos_g[g], sin_g[g], d, f"attn_bwd{g}")
        qkv_pieces.append(_mm_tn(dqkv, u1_g[g], name=f"attn_qkv_dw{g}"))
        du1.append(_mm_nn([dqkv], [qkv_parts(g)], None, name=f"attn_qkv_dx{g}"))
    grad_qkv = jnp.stack([p.reshape(3, ATTN_GROUP_WIDTH, D_MODEL) for p in qkv_pieces], axis=1).reshape(
        3 * ATTN_WIDTH, D_MODEL)
    publish("attn", {"qkv": grad_qkv, "attn_out": grad_attn_out})
    dh2, dh2b, d_mix1 = _rms_bwd(h2, g_mix[1], du1, dh3, "attn_norm_bwd", ATTN_DILATIONS)

    dh1, dh1b, d_ffn0 = ffn_bwd(h1, ffn0, dh2, dh2b, 0)

    d_og = _mm_nt(dh1b, whole("hgrn_out"), out_dtype=F32, name="hgrn_out_dx")
    grad_hgrn_out = _mm_tn(og, dh1b, name="hgrn_out_dw")
    dproj, d_lb, d_out_gain = _hgrn_bwd(proj, o_pre, d_og, states, lb, out_gain, "hgrn_bwd")
    publish("hgrn", {"hgrn_in": _mm_tn(dproj, u0, name="hgrn_in_dw"), "hgrn_out": grad_hgrn_out})
    dx, _, d_mix0 = _mm_nn([dproj], [whole("hgrn_in")], dh1, name="hgrn_in_dx", norm=(x, g_mix[0]))

    small = dict(norm_mix0=d_mix0, norm_mix1=d_mix1, norm_ffn0=d_ffn0, norm_ffn1=d_ffn1, lb=d_lb,
                 out_gain=d_out_gain, final=d_final, loss=loss_part)
    return dx, small


WEIGHT_NAMES = ("hgrn_in", "hgrn_out", "qkv", "attn_out", "ffn_in0", "ffn_in1", "ffn_down0", "ffn_down1")
MESH_IDS = pl.DeviceIdType.MESH
HBM_SPEC = pl.BlockSpec(memory_space=pl.ANY)


N_PEERS = N_DEV - 1
PEER_OFFSETS = [(dx, dy, dc) for dx in (0, 1) for dy in (0, 1) for dc in (0, 1)][1:]


def _mesh_place():
    x, y, c = lax.axis_index("x"), lax.axis_index("y"), lax.axis_index("c")
    peers = []
    for dx, dy, dc in PEER_OFFSETS:
        px, py, pc = (1 - x if dx else x), (1 - y if dy else y), (1 - c if dc else c)
        peers.append(((px, py, pc), 4 * px + 2 * py + pc))
    return 4 * x + 2 * y + c, peers


def _exchange_launch(srcs, scatter, collective_id, name):
    n = len(srcs)
    src_refs = [jax.new_ref(s, memory_space=pltpu.MemorySpace.HBM) for s in srcs]
    land_refs = [jax.empty_ref(jax.ShapeDtypeStruct(s.shape if scatter else (N_DEV,) + s.shape, s.dtype),
                               memory_space=pltpu.MemorySpace.HBM) for s in srcs]

    @pl.kernel(mesh=plsc.ScalarSubcoreMesh(axis_name="sequencer", num_cores=1), name=name,
               scratch_types=(pltpu.SemaphoreType.DMA((n * N_PEERS,)), pltpu.SemaphoreType.DMA((n * N_PEERS,)),
                              pltpu.SemaphoreType.DMA((n,))),
               compiler_params=pltpu.CompilerParams(collective_id=collective_id))
    def launch(send_sems, recv_sems, local_sems):
        me, peers = _mesh_place()
        barrier = pltpu.get_barrier_semaphore()
        for peer, _ in peers:
            pl.semaphore_signal(barrier, inc=1, device_id=peer, device_id_type=MESH_IDS)
        pl.semaphore_wait(barrier, N_PEERS)
        own = [pltpu.make_async_copy(src_refs[w].at[me] if scatter else src_refs[w], land_refs[w].at[me],
                                     local_sems.at[w]) for w in range(n)]
        for cp in own:
            cp.start()
        copies = [pltpu.make_async_remote_copy(
            src_ref=src_refs[w].at[pid] if scatter else src_refs[w], dst_ref=land_refs[w].at[me],
            send_sem=send_sems.at[w * N_PEERS + k], recv_sem=recv_sems.at[w * N_PEERS + k],
            device_id=peer, device_id_type=MESH_IDS) for w in range(n) for k, (peer, pid) in enumerate(peers)]
        for cp in copies:
            cp.start()
        for cp in copies:
            cp.wait()
        for cp in own:
            cp.wait()

    launch()
    return land_refs


def _gather_small(block, name):
    def body(in_ref, out_ref, send_sems, recv_sems, local_sem):
        me, peers = _mesh_place()
        own = pltpu.make_async_copy(in_ref, out_ref.at[me], local_sem)
        own.start()
        sends = [pltpu.make_async_remote_copy(
            src_ref=in_ref, dst_ref=out_ref.at[me], send_sem=send_sems.at[k], recv_sem=recv_sems.at[k],
            device_id=peer, device_id_type=MESH_IDS) for k, (peer, _) in enumerate(peers)]
        for cp in sends:
            cp.start()
        for cp in sends:
            cp.wait_recv()
        for cp in sends:
            cp.wait_send()
        own.wait()

    return pl.pallas_call(
        body, out_shape=jax.ShapeDtypeStruct((N_DEV,) + block.shape, block.dtype),
        in_specs=[HBM_SPEC], out_specs=HBM_SPEC,
        scratch_shapes=[pltpu.SemaphoreType.DMA((N_PEERS,)), pltpu.SemaphoreType.DMA((N_PEERS,)),
                        pltpu.SemaphoreType.DMA],
        name=name)(block)


def _sum_blocks(recv, name):
    rows = recv.shape[1]
    tr = _pick_tile(rows, 256, 16)

    def body(r_ref, g_ref):
        acc = r_ref[0].astype(F32)
        for j in range(1, N_DEV):
            acc = acc + r_ref[j].astype(F32)
        g_ref[...] = acc

    return pl.pallas_call(
        body, out_shape=jax.ShapeDtypeStruct((rows, D_MODEL), F32), grid=(rows // tr,),
        in_specs=[pl.BlockSpec((N_DEV, tr, D_MODEL), lambda i: (0, i, 0))],
        out_specs=pl.BlockSpec((tr, D_MODEL), lambda i: (i, 0)),
        compiler_params=_params("parallel"), name=name)(recv)


def _adamw_math(w, g, m, v):
    m_new = ADAM_B1 * m + (1.0 - ADAM_B1) * g
    v_new = ADAM_B2 * v + (1.0 - ADAM_B2) * (g * g)
    m_hat = m_new / (1.0 - ADAM_B1 ** ADAM_STEP)
    v_hat = v_new / (1.0 - ADAM_B2 ** ADAM_STEP)
    delta = -ADAM_LR * (m_hat / (jnp.sqrt(v_hat) + ADAM_EPS) + ADAM_WD * w)
    return delta, m_new, v_new


def _adamw(w, g, m, v, name):
    rows, cols = w.shape
    tr = _pick_tile(rows, 256, 8)

    def body(w_ref, g_ref, m_ref, v_ref, d_ref, mo_ref, vo_ref):
        d_ref[...], mo_ref[...], vo_ref[...] = _adamw_math(w_ref[...], g_ref[...], m_ref[...], v_ref[...])

    blk = pl.BlockSpec((tr, cols), lambda i: (i, 0))
    return pl.pallas_call(
        body, out_shape=(jax.ShapeDtypeStruct((rows, cols), F32),) * 3, grid=(rows // tr,),
        in_specs=[blk] * 4, out_specs=(blk,) * 3, compiler_params=_params("parallel"), name=name)(w, g, m, v)


ROW_MIX, ROW_FFN, ROW_LB, ROW_OUT_GAIN, ROW_FINAL = 0, 2, 4, 7, 8
PART_MIX, PART_FFN, PART_LB, PART_OUT_GAIN, PART_FINAL, PART_LOSS = 0, 2, 4, 5, 6, 7


def _small_update(parts_all, w, m, v, name):
    def body(p_ref, w_ref, m_ref, v_ref, g_ref, d_ref, mo_ref, vo_ref, loss_ref):
        def total(row, n=1):
            tot = p_ref[0, row:row + n, :]
            for j in range(1, N_DEV):
                tot = tot + p_ref[j, row:row + n, :]
            return tot

        logits = [w_ref[ROW_LB + i:ROW_LB + i + 1, :] for i in range(3)]
        mx = jnp.maximum(jnp.maximum(logits[0], logits[1]), logits[2])
        ex = [jnp.exp(l - mx) for l in logits]
        den = ex[0] + ex[1] + ex[2]
        prob = [e / den for e in ex]
        d_lb = total(PART_LB)
        g_ref[...] = jnp.zeros_like(g_ref)
        g_ref[ROW_MIX:ROW_MIX + 2, :] = total(PART_MIX, 2)
        g_ref[ROW_FFN:ROW_FFN + 2, :] = total(PART_FFN, 2)
        for i in range(3):
            g_ref[ROW_LB + i:ROW_LB + i + 1, :] = prob[i] * ((d_lb if i == 0 else 0.0) - prob[0] * d_lb)
        g_ref[ROW_OUT_GAIN:ROW_OUT_GAIN + 1, :] = total(PART_OUT_GAIN)
        g_ref[ROW_FINAL:ROW_FINAL + 1, :] = total(PART_FINAL)
        d_ref[...], mo_ref[...], vo_ref[...] = _adamw_math(w_ref[...], g_ref[...], m_ref[...], v_ref[...])
        loss_ref[...] = jnp.sum(total(PART_LOSS), axis=-1, keepdims=True)

    packed = jax.ShapeDtypeStruct((16, D_MODEL), F32)
    return pl.pallas_call(
        body, out_shape=(packed, packed, packed, packed, jax.ShapeDtypeStruct((1, 1), F32)),
        compiler_params=pltpu.CompilerParams(vmem_limit_bytes=VMEM_LIMIT), name=name)(parts_all, w, m, v)


def _pack_small(norm_mix, norm_ffn, lb_logits, out_gain, final):
    pad = jnp.zeros((1, D_MODEL - HGRN_DIM), F32)
    return jnp.concatenate([norm_mix, norm_ffn, lb_logits, jnp.concatenate([out_gain, pad], axis=1),
                            final.reshape(1, D_MODEL), jnp.zeros((16 - ROW_FINAL - 1, D_MODEL), F32)], axis=0)


def _unpack_small(p):
    return (p[ROW_MIX:ROW_MIX + 2], p[ROW_FFN:ROW_FFN + 2], p[ROW_LB:ROW_LB + 3],
            p[ROW_OUT_GAIN:ROW_OUT_GAIN + 1, :HGRN_DIM], p[ROW_FINAL])


def _lower_bound(lb_logits, name):
    def body(l_ref, o_ref):
        logits = [l_ref[i:i + 1, :] for i in range(3)]
        mx = jnp.maximum(jnp.maximum(logits[0], logits[1]), logits[2])
        ex = [jnp.exp(l - mx) for l in logits]
        o_ref[...] = ex[0] / (ex[0] + ex[1] + ex[2])

    return pl.pallas_call(body, out_shape=jax.ShapeDtypeStruct((1, D_MODEL), F32), name=name)(lb_logits)


def kernel(x, norm_mix, norm_ffn, hgrn_w_in, hgrn_lb_logits, hgrn_out_norm, hgrn_w_out, attn_w_qkv, attn_w_out, ffn_w_in, ffn_w_down, final_norm, loss_target, m_norm_mix, m_norm_ffn, m_hgrn_w_in, m_hgrn_lb_logits, m_hgrn_out_norm, m_hgrn_w_out, m_attn_w_qkv, m_attn_w_out, m_ffn_w_in, m_ffn_w_down, m_final_norm, v_norm_mix, v_norm_ffn, v_hgrn_w_in, v_hgrn_lb_logits, v_hgrn_out_norm, v_hgrn_w_out, v_attn_w_qkv, v_attn_w_out, v_ffn_w_in, v_ffn_w_down, v_final_norm):
    col_sharded = {"hgrn_in": hgrn_w_in[0], "qkv": attn_w_qkv[0], "ffn_in0": ffn_w_in[0], "ffn_in1": ffn_w_in[1]}
    row_sharded = {"hgrn_out": hgrn_w_out[0], "attn_out": attn_w_out[0], "ffn_down0": ffn_w_down[0],
                   "ffn_down1": ffn_w_down[1]}
    gathering = {}
    for gi, (group, names) in enumerate(WEIGHT_GROUPS.items()):
        shards = [(col_sharded[n].T if n in col_sharded else row_sharded[n]).astype(BF16) for n in names]
        gathering[group] = _exchange_launch(shards, False, 1 + gi, f"weights_gather_{group}")

    def fetch(group):
        return {n: land[...].reshape(-1, D_MODEL) for n, land in zip(WEIGHT_GROUPS[group], gathering[group])}

    in_flight = {}

    def publish(group, grads):
        names = WEIGHT_GROUPS[group]
        parts = [grads[n].reshape(N_DEV, -1, D_MODEL) for n in names]
        in_flight[group] = _exchange_launch(parts, True, 1 + len(WEIGHT_GROUPS) + list(WEIGHT_GROUPS).index(group),
                                            f"grads_send_{group}")

    lb = _lower_bound(hgrn_lb_logits, "hgrn_lower_bound")
    grad_x, small = _local_step(x[0], loss_target[0], norm_mix, norm_ffn, lb, hgrn_out_norm,
                                final_norm.reshape(1, D_MODEL), fetch, publish)

    pad = jnp.zeros((1, D_MODEL - HGRN_DIM), F32)
    small_part = jnp.concatenate(
        [small["norm_mix0"], small["norm_mix1"], small["norm_ffn0"], small["norm_ffn1"], small["lb"],
         jnp.concatenate([small["out_gain"], pad], axis=1), small["final"], small["loss"]], axis=0)
    small_all = _gather_small(small_part, "small_grads_gather")
    received = {}
    for group in ("ffn1", "attn", "ffn0", "hgrn"):
        received.update(zip(WEIGHT_GROUPS[group], [land[...] for land in in_flight[group]]))

    masters = {"hgrn_in": (hgrn_w_in[0], m_hgrn_w_in[0], v_hgrn_w_in[0]),
               "hgrn_out": (hgrn_w_out[0], m_hgrn_w_out[0], v_hgrn_w_out[0]),
               "qkv": (attn_w_qkv[0], m_attn_w_qkv[0], v_attn_w_qkv[0]),
               "attn_out": (attn_w_out[0], m_attn_w_out[0], v_attn_w_out[0]),
               "ffn_in0": (ffn_w_in[0], m_ffn_w_in[0], v_ffn_w_in[0]),
               "ffn_in1": (ffn_w_in[1], m_ffn_w_in[1], v_ffn_w_in[1]),
               "ffn_down0": (ffn_w_down[0], m_ffn_w_down[0], v_ffn_w_down[0]),
               "ffn_down1": (ffn_w_down[1], m_ffn_w_down[1], v_ffn_w_down[1])}
    res = {}
    for n in WEIGHT_NAMES:
        g = _sum_blocks(received[n], f"{n}_grad_sum")
        if n in col_sharded:
            g = g.T
        wv, mv, vv = masters[n]
        res[n] = (g,) + tuple(_adamw(wv, g, mv, vv, f"{n}_adamw"))

    def single(n):
        return [t[None] for t in res[n]]

    def pair(n):
        return [jnp.stack([a, b]) for a, b in zip(res[n + "0"], res[n + "1"])]

    big = dict(hgrn_w_in=single("hgrn_in"), hgrn_w_out=single("hgrn_out"), attn_w_qkv=single("qkv"),
               attn_w_out=single("attn_out"), ffn_w_in=pair("ffn_in"), ffn_w_down=pair("ffn_down"))

    w_small = _pack_small(norm_mix, norm_ffn, hgrn_lb_logits, hgrn_out_norm, final_norm)
    m_small = _pack_small(m_norm_mix, m_norm_ffn, m_hgrn_lb_logits, m_hgrn_out_norm, m_final_norm)
    v_small = _pack_small(v_norm_mix, v_norm_ffn, v_hgrn_lb_logits, v_hgrn_out_norm, v_final_norm)
    g_s, d_s, m_s, v_s, loss = _small_update(small_all, w_small, m_small, v_small, "small_update")
    small_out = [_unpack_small(t) for t in (g_s, d_s, m_s, v_s)]

    def group(i):
        s = small_out[i]
        return (s[0], s[1], big["hgrn_w_in"][i], s[2], s[3], big["hgrn_w_out"][i], big["attn_w_qkv"][i],
                big["attn_w_out"][i], big["ffn_w_in"][i], big["ffn_w_down"][i], s[4])

    return (loss.reshape(()), grad_x[None], *group(0), *group(1), *group(2), *group(3))
```

```python
import functools

import jax
import jax.numpy as jnp
from jax import lax
from jax.experimental import pallas as pl
from jax.experimental.pallas import tpu as pltpu
from jax.experimental.pallas import tpu_sc as plsc

F32 = jnp.float32
BF16 = jnp.bfloat16

D_MODEL = 1024
N_DEV = 8
NORM_EPS = 1e-6

HGRN_HEADS = 8
HGRN_DIM = 128
HGRN_CHUNK = 64
HGRN_STEP_CHUNKS = 2
HGRN_EXP_CLAMP = 60.0

ATTN_DIM = 128
ATTN_BLOCK = 128
ATTN_GROUP_HEADS = 4
ATTN_GROUP_WIDTH = ATTN_GROUP_HEADS * ATTN_DIM
ATTN_DILATIONS = (1, 4, 16)
ATTN_WIDTH = 3 * ATTN_GROUP_WIDTH
ROPE_THETA = 10000.0
NEG_BIG = -1e30

D_FF = 2816

ADAM_LR = 0.001
ADAM_B1 = 0.9
ADAM_B2 = 0.999
ADAM_EPS = 1e-08
ADAM_WD = 0.01
ADAM_STEP = 10

VMEM_LIMIT = 48 * 1024 * 1024

NT = (((1,), (1,)), ((), ()))
NN = (((1,), (0,)), ((), ()))
TN = (((0,), (0,)), ((), ()))


def _dot(a, b, dims):
    return lax.dot_general(a, b, dims, preferred_element_type=F32)


def _params(*sem):
    return pltpu.CompilerParams(dimension_semantics=sem, vmem_limit_bytes=VMEM_LIMIT)


def _pick_tile(n, cap, mult):
    best = None
    for t in range(mult, min(n, cap) + 1, mult):
        if n % t == 0:
            best = t
    assert best is not None, (n, cap, mult)
    return best


def _sigmoid(x):
    return 0.5 * jnp.tanh(0.5 * x) + 0.5


ROW_TILE = 512
COL_CHUNK = 512
GRAD_TILE = 256


def _whole(shape, index_map):
    return pl.BlockSpec(shape, index_map, pipeline_mode=pl.Buffered(1))


def _part_specs(parts, n_cols):
    return [_whole((rows, n_cols), functools.partial(lambda i, b: (b, 0), b=blk)) for _, rows, blk in parts]


def _mm_nt(a, w_parts, *, out_dtype, name, rope=None):
    M, K = a.shape
    tm = _pick_tile(M, ROW_TILE, 16)
    widths = [rows for _, rows, _ in w_parts]
    n_parts = len(w_parts)

    def body(*refs):
        a_ref, w_refs, o_ref = refs[0], refs[1:1 + n_parts], refs[-1]
        av = a_ref[...]
        off = 0
        for p, w_ref in enumerate(w_refs):
            for c0 in range(0, widths[p], COL_CHUNK):
                cw = min(COL_CHUNK, widths[p] - c0)
                acc = _dot(av, w_ref[c0:c0 + cw, :], NT)
                if rope is not None and p < rope[2]:
                    cos, sin = refs[1 + n_parts][...], refs[2 + n_parts][...]
                    for h0 in range(0, cw, ATTN_DIM):
                        xh = acc[:, h0:h0 + ATTN_DIM]
                        rot = pltpu.roll(xh, ATTN_DIM // 2, 1)
                        o_ref[:, off + c0 + h0:off + c0 + h0 + ATTN_DIM] = (xh * cos + rot * sin).astype(out_dtype)
                else:
                    o_ref[:, off + c0:off + c0 + cw] = acc.astype(out_dtype)
            off += widths[p]

    in_specs = [pl.BlockSpec((tm, K), lambda i: (i, 0))] + _part_specs(w_parts, K)
    args = [a] + [w for w, _, _ in w_parts]
    if rope is not None:
        in_specs += [pl.BlockSpec((tm, ATTN_DIM), lambda i: (i, 0))] * 2
        args += [rope[0], rope[1]]
    return pl.pallas_call(
        body, out_shape=jax.ShapeDtypeStruct((M, sum(widths)), out_dtype), grid=(M // tm,),
        in_specs=in_specs, out_specs=pl.BlockSpec((tm, sum(widths)), lambda i: (i, 0)),
        compiler_params=_params("parallel"), name=name)(*args)


def _mm_nn(a_list, w_parts_list, resid, *, name, norm=None, head=None):
    M = a_list[0].shape[0]
    tm = _pick_tile(M, ROW_TILE, 16)
    n_a = len(a_list)
    flat_parts = [p for parts in w_parts_list for p in parts]
    extra = norm if norm is not None else head
    n_in = n_a + len(flat_parts) + (1 if resid is not None else 0) + (2 if extra is not None else 0)

    def body(*refs):
        a_refs, w_refs = refs[:n_a], refs[n_a:n_a + len(flat_parts)]

        def product(rows):
            acc = None
            wi = 0
            for a_ref, parts in zip(a_refs, w_parts_list):
                off = 0
                for _, k, _ in parts:
                    term = _dot(a_ref[rows, off:off + k], w_refs[wi][...], NN)
                    acc = term if acc is None else acc + term
                    off += k
                    wi += 1
            return acc

        if extra is None:
            acc = product(slice(None))
            if resid is not None:
                acc = acc + refs[n_in - 1][...]
            refs[n_in][...] = acc
            return

        @pl.when(pl.program_id(0) == 0)
        def _():
            for acc_ref in refs[n_in + 2:]:
                acc_ref[...] = jnp.zeros_like(acc_ref)

        for r0 in range(0, tm, tm // 2):
            rows = slice(r0, r0 + tm // 2)
            acc = product(rows)
            if head is not None:
                _loss_head_math(acc + refs[n_in - 3][rows, :], rows, refs[n_in - 2], refs[n_in - 1],
                                *refs[n_in:n_in + 4])
                continue
            dres_ref, x_ref, g_ref = refs[n_in - 3:n_in]
            dx_ref, dxb_ref, dg_ref = refs[n_in:n_in + 3]
            xv = x_ref[rows, :]
            rstd = lax.rsqrt(jnp.mean(xv * xv, axis=-1, keepdims=True) + NORM_EPS)
            n = xv * rstd
            dg_ref[...] += jnp.sum(acc * n, axis=0, keepdims=True)
            dn = acc * g_ref[...]
            dx = dres_ref[rows, :] + rstd * (dn - n * jnp.mean(dn * n, axis=-1, keepdims=True))
            dx_ref[rows, :] = dx
            dxb_ref[rows, :] = dx.astype(BF16)

    row = pl.BlockSpec((tm, D_MODEL), lambda i: (i, 0))
    vec = pl.BlockSpec((1, D_MODEL), lambda i: (0, 0))
    in_specs = [pl.BlockSpec((tm, a.shape[1]), lambda i: (i, 0)) for a in a_list] + _part_specs(flat_parts, D_MODEL)
    args = list(a_list) + [w for w, _, _ in flat_parts]
    if resid is not None:
        in_specs.append(row)
        args.append(resid)
    if extra is None:
        return pl.pallas_call(
            body, out_shape=jax.ShapeDtypeStruct((M, D_MODEL), F32), grid=(M // tm,),
            in_specs=in_specs, out_specs=row, compiler_params=_params("parallel"), name=name)(*args)
    assert resid is not None
    out_shape = [jax.ShapeDtypeStruct((M, D_MODEL), F32), jax.ShapeDtypeStruct((M, D_MODEL), BF16),
                 jax.ShapeDtypeStruct((1, D_MODEL), F32)]
    out_specs = [row, row, vec]
    if head is not None:
        out_shape.append(jax.ShapeDtypeStruct((1, D_MODEL), F32))
        out_specs.append(vec)
    return pl.pallas_call(
        body, out_shape=out_shape, grid=(M // tm,), in_specs=in_specs + [row, vec], out_specs=out_specs,
        compiler_params=_params("arbitrary"), name=name)(*args, extra[0], extra[1])


def _mm_tn(a, b, *, name, into=None, row_tile=0, rows=None):
    T, R = a.shape
    N = b.shape[1]
    tr = GRAD_TILE
    rows = R if rows is None else rows

    def body(a_ref, b_ref, *refs):
        refs[-1][...] = _dot(a_ref[...], b_ref[...], TN).astype(BF16)

    in_specs = [pl.BlockSpec((T, tr), lambda r: (0, r)), _whole((T, N), lambda r: (0, 0))]
    args = [a, b]
    if into is not None:
        in_specs.append(HBM_SPEC)
        args.append(into)
    return pl.pallas_call(
        body, out_shape=jax.ShapeDtypeStruct((rows, N), BF16), grid=(R // tr,),
        in_specs=in_specs, out_specs=pl.BlockSpec((tr, N), lambda r: (row_tile + r, 0)),
        input_output_aliases={} if into is None else {2: 0},
        compiler_params=_params("parallel"), name=name)(*args)


def _rms_fwd(x, gain, name):
    T = x.shape[0]
    tm = _pick_tile(T, 512, 16)

    def body(x_ref, g_ref, u_ref):
        xv = x_ref[...]
        rstd = lax.rsqrt(jnp.mean(xv * xv, axis=-1, keepdims=True) + NORM_EPS)
        u_ref[...] = (xv * rstd * g_ref[...]).astype(BF16)

    return pl.pallas_call(
        body, out_shape=jax.ShapeDtypeStruct((T, D_MODEL), BF16), grid=(T // tm,),
        in_specs=[pl.BlockSpec((tm, D_MODEL), lambda i: (i, 0)), pl.BlockSpec((1, D_MODEL), lambda i: (0, 0))],
        out_specs=pl.BlockSpec((tm, D_MODEL), lambda i: (i, 0)),
        compiler_params=_params("parallel"), name=name)(x, gain)


def _rms_bwd(x, gain, dus, dres, name, dilations=(1,)):
    T = x.shape[0]
    tm = _pick_tile(T, PERM_TILE, 16 * max(dilations))
    n_du = len(dus)

    def body(x_ref, g_ref, *refs):
        du_refs, dres_ref = refs[:n_du], refs[n_du]
        dx_ref, dxb_ref, dg_ref, du_scr = refs[n_du + 1:]

        @pl.when(pl.program_id(0) == 0)
        def _():
            dg_ref[...] = jnp.zeros_like(dg_ref)

        if tuple(dilations) == (1,):
            du = du_refs[0][...]
        else:
            for i, (d, du_ref) in enumerate(zip(dilations, du_refs)):
                for j in range(D_MODEL // LANES):
                    lanes = slice(j * LANES, (j + 1) * LANES)
                    if d == 1:
                        du_scr[j] = du_ref[:, lanes] if i == 0 else du_scr[j] + du_ref[:, lanes]
                        continue
                    blk = du_scr.at[j]
                    for r in range(d):
                        rows = _class_rows(r, d, tm)
                        blk[rows, :] = du_ref[r, :, lanes] if i == 0 else blk[rows, :] + du_ref[r, :, lanes]
            du = jnp.concatenate([du_scr[j] for j in range(D_MODEL // LANES)], axis=1)
        xv = x_ref[...]
        rstd = lax.rsqrt(jnp.mean(xv * xv, axis=-1, keepdims=True) + NORM_EPS)
        n = xv * rstd
        dg_ref[...] += jnp.sum(du * n, axis=0, keepdims=True)
        dn = du * g_ref[...]
        dx = dres_ref[...] + rstd * (dn - n * jnp.mean(dn * n, axis=-1, keepdims=True))
        dx_ref[...] = dx
        dxb_ref[...] = dx.astype(BF16)

    row = pl.BlockSpec((tm, D_MODEL), lambda i: (i, 0))
    vec = pl.BlockSpec((1, D_MODEL), lambda i: (0, 0))
    return pl.pallas_call(
        body,
        out_shape=(jax.ShapeDtypeStruct((T, D_MODEL), F32), jax.ShapeDtypeStruct((T, D_MODEL), BF16),
                   jax.ShapeDtypeStruct((1, D_MODEL), F32)),
        grid=(T // tm,), in_specs=[row, vec] + [_residue_spec(d, tm, D_MODEL) for d in dilations] + [row],
        out_specs=(row, row, vec), scratch_shapes=[pltpu.VMEM((D_MODEL // LANES, tm, LANES), F32)],
        compiler_params=_params("arbitrary"), name=name)(
            x, gain, *[_residue_view(du, d) for du, d in zip(dus, dilations)], dres)


def _loss_head_math(hv, rows, t_ref, g_ref, dh_ref, dhb_ref, dg_ref, loss_ref):
    inv_f = 1.0 / D_MODEL
    g = g_ref[...]
    rstd = lax.rsqrt(jnp.mean(hv * hv, axis=-1, keepdims=True) + NORM_EPS)
    n = hv * rstd
    err = n * g - t_ref[rows, :]
    loss_ref[...] += (0.5 * inv_f) * jnp.sum(err * err, axis=0, keepdims=True)
    dy = err * inv_f
    dg_ref[...] += jnp.sum(dy * n, axis=0, keepdims=True)
    dn = dy * g
    dh = rstd * (dn - n * jnp.mean(dn * n, axis=-1, keepdims=True))
    dh_ref[rows, :] = dh
    dhb_ref[rows, :] = dh.astype(BF16)


FFN_TILE = 256


def _ffn_in(h, gain, w_in, name):
    T = h.shape[0]
    tm = _pick_tile(T, ROW_TILE, 16)

    def body(h_ref, g_ref, w_ref, n_ref, gate_ref, up_ref, a_ref):
        hv = h_ref[...]
        rstd = lax.rsqrt(jnp.mean(hv * hv, axis=-1, keepdims=True) + NORM_EPS)
        n = (hv * rstd * g_ref[...]).astype(BF16)
        n_ref[...] = n
        for c0 in range(0, D_FF, FFN_TILE):
            cols = slice(c0, c0 + FFN_TILE)
            gate = _dot(n, w_ref[c0:c0 + FFN_TILE, :], NT)
            up = _dot(n, w_ref[D_FF + c0:D_FF + c0 + FFN_TILE, :], NT)
            gate_ref[:, cols] = gate.astype(BF16)
            up_ref[:, cols] = up.astype(BF16)
            a_ref[:, cols] = (gate * _sigmoid(gate) * up).astype(BF16)

    row = pl.BlockSpec((tm, D_MODEL), lambda i: (i, 0))
    wide = pl.BlockSpec((tm, D_FF), lambda i: (i, 0))
    wide_shape = jax.ShapeDtypeStruct((T, D_FF), BF16)
    return pl.pallas_call(
        body, out_shape=(jax.ShapeDtypeStruct((T, D_MODEL), BF16), wide_shape, wide_shape, wide_shape),
        grid=(T // tm,),
        in_specs=[row, pl.BlockSpec((1, D_MODEL), lambda i: (0, 0)), _whole((2 * D_FF, D_MODEL), lambda i: (0, 0))],
        out_specs=(row, wide, wide, wide), compiler_params=_params("parallel"), name=name)(h, gain, w_in)


def _ffn_down_dx(dhb, w_down, gate, up, name):
    T = dhb.shape[0]
    tm = _pick_tile(T, ROW_TILE, 16)

    def body(dh_ref, w_ref, gate_ref, up_ref, dgate_ref, dup_ref):
        dh = dh_ref[...]
        for c0 in range(0, D_FF, FFN_TILE):
            cols = slice(c0, c0 + FFN_TILE)
            da = _dot(dh, w_ref[c0:c0 + FFN_TILE, :], NT)
            gate = gate_ref[:, cols].astype(F32)
            sg = _sigmoid(gate)
            dgate_ref[:, cols] = (da * up_ref[:, cols].astype(F32) * (sg * (1.0 + gate * (1.0 - sg)))).astype(BF16)
            dup_ref[:, cols] = (da * gate * sg).astype(BF16)

    wide = pl.BlockSpec((tm, D_FF), lambda i: (i, 0))
    wide_shape = jax.ShapeDtypeStruct((T, D_FF), BF16)
    return pl.pallas_call(
        body, out_shape=(wide_shape, wide_shape), grid=(T // tm,),
        in_specs=[pl.BlockSpec((tm, D_MODEL), lambda i: (i, 0)), _whole((D_FF, D_MODEL), lambda i: (0, 0)), wide, wide],
        out_specs=(wide, wide), compiler_params=_params("parallel"), name=name)(dhb, w_down, gate, up)


def _tri(n, lower):
    r = lax.broadcasted_iota(jnp.int32, (n, n), 0)
    c = lax.broadcasted_iota(jnp.int32, (n, n), 1)
    return (c <= r) if lower else (c >= r)


def _running_sum(x, lower):
    tri = _tri(x.shape[0], lower).astype(BF16)
    hi = x.astype(BF16)
    rest = x - hi.astype(F32)
    mid = rest.astype(BF16)
    lo = (rest - mid.astype(F32)).astype(BF16)
    return _dot(tri, hi, NN) + _dot(tri, mid, NN) + _dot(tri, lo, NN)


def _hgrn_gates(q_raw, f_raw, lb):
    C = q_raw.shape[0]
    sig_f = _sigmoid(f_raw)
    forget = lb + (1.0 - lb) * sig_f
    key = 1.0 - forget
    log_f = jnp.log(forget)
    b = _running_sum(log_f, True)
    first_half = lax.broadcasted_iota(jnp.int32, log_f.shape, 0) < C // 2
    r = jnp.sum(jnp.where(first_half, log_f, 0.0), axis=0, keepdims=True)
    b_last = jnp.sum(log_f, axis=0, keepdims=True)
    e_a = jnp.exp(jnp.minimum(b - r, HGRN_EXP_CLAMP))
    e_b = jnp.exp(jnp.minimum(r - b, HGRN_EXP_CLAMP))
    e_q = jnp.exp(b)
    e_k = jnp.exp(b_last - b)
    sig_q = _sigmoid(q_raw)
    query = q_raw * sig_q
    return dict(sig_f=sig_f, forget=forget, sig_q=sig_q, e_a=e_a, e_b=e_b, e_q=e_q, e_k=e_k,
                e_last=jnp.exp(b_last), q_a=query * e_a, k_b=key * e_b, q_hat=query * e_q, k_til=key * e_k)


def _hgrn_fwd(proj, lb, gain, name):
    T = proj.shape[0]
    C = HGRN_CHUNK
    CPS = HGRN_STEP_CHUNKS
    H, HD = HGRN_HEADS, HGRN_DIM

    def body(q_ref, f_ref, i_ref, g_ref, lb_ref, gain_ref, og_ref, o_ref, st_ref, s_scr):
        @pl.when(pl.program_id(0) == 0)
        def _():
            s_scr[...] = jnp.zeros_like(s_scr)

        causal = _tri(C, True)
        gain_v = gain_ref[...]
        heads = [slice(h * HD, (h + 1) * HD) for h in range(H)]
        s_t = [s_scr[h] for h in range(H)]
        for cc in range(CPS):
            rows = slice(cc * C, (cc + 1) * C)
            for h in range(H):
                st_ref[cc, h] = s_t[h]
            gt = _hgrn_gates(q_ref[rows, :], f_ref[rows, :], lb_ref[...])
            q_a, k_b = gt["q_a"].astype(BF16), gt["k_b"].astype(BF16)
            q_hat, k_til = gt["q_hat"].astype(BF16), gt["k_til"].astype(BF16)
            v = i_ref[rows, :].astype(BF16)
            p = [jnp.where(causal, _dot(q_a[:, sl], k_b[:, sl], NT), 0.0).astype(BF16) for sl in heads]
            o = [_dot(p[h], v[:, sl], NN) + _dot(q_hat[:, sl], s_t[h].astype(BF16), NT)
                 for h, sl in enumerate(heads)]
            s_t = [gt["e_last"][:, sl] * s_t[h] + _dot(v[:, sl], k_til[:, sl], TN) for h, sl in enumerate(heads)]
            for h, sl in enumerate(heads):
                o_ref[rows, sl] = o[h]
                rstd = lax.rsqrt(jnp.mean(o[h] * o[h], axis=-1, keepdims=True) + NORM_EPS)
                g_raw = g_ref[rows, sl]
                og_ref[rows, sl] = (o[h] * rstd * gain_v * (g_raw * _sigmoid(g_raw))).astype(BF16)
        for h in range(H):
            s_scr[h] = s_t[h]

    col = lambda j: pl.BlockSpec((CPS * C, D_MODEL), lambda c: (c, j))
    row = pl.BlockSpec((CPS * C, D_MODEL), lambda c: (c, 0))
    return pl.pallas_call(
        body,
        out_shape=(jax.ShapeDtypeStruct((T, D_MODEL), BF16), jax.ShapeDtypeStruct((T, D_MODEL), F32),
                   jax.ShapeDtypeStruct((T // C, H, HD, HD), F32)),
        grid=(T // (CPS * C),),
        in_specs=[col(0), col(1), col(2), col(3), pl.BlockSpec((1, D_MODEL), lambda c: (0, 0)),
                  pl.BlockSpec((1, HD), lambda c: (0, 0))],
        out_specs=(row, row, pl.BlockSpec((CPS, H, HD, HD), lambda c: (c, 0, 0, 0))),
        scratch_shapes=[pltpu.VMEM((H, HD, HD), F32)],
        compiler_params=_params("arbitrary"), name=name)(proj, proj, proj, proj, lb, gain)


def _hgrn_bwd(proj, o_pre, d_og, states, lb, gain, name):
    T = proj.shape[0]
    C = HGRN_CHUNK
    CPS = HGRN_STEP_CHUNKS
    H, HD = HGRN_HEADS, HGRN_DIM
    NC = T // (CPS * C)

    def body(q_ref, f_ref, i_ref, g_ref, o_ref, dog_ref, st_ref, lb_ref, gain_ref,
             dproj_ref, dlb_ref, dgain_ref, ds_scr, dq_all, dk_all, db_all):
        @pl.when(pl.program_id(0) == 0)
        def _():
            ds_scr[...] = jnp.zeros_like(ds_scr)
            dlb_ref[...] = jnp.zeros_like(dlb_ref)
            dgain_ref[...] = jnp.zeros_like(dgain_ref)

        lbv = lb_ref[...]
        causal = _tri(C, True)
        last_row = lax.broadcasted_iota(jnp.int32, (C, HD), 0) == C - 1
        gain_v = gain_ref[...]
        heads = [slice(h * HD, (h + 1) * HD) for h in range(H)]
        hs = range(H)
        ds_t = [ds_scr[h] for h in hs]
        dgain = None
        for cc in reversed(range(CPS)):
            rows = slice(cc * C, (cc + 1) * C)
            dq_scr, dk_scr, db_scr = dq_all.at[cc], dk_all.at[cc], db_all.at[cc]
            q_raw = q_ref[rows, :]
            gt = _hgrn_gates(q_raw, f_ref[rows, :], lbv)
            o = [o_ref[rows, sl] for sl in heads]
            rstd = [lax.rsqrt(jnp.mean(x * x, axis=-1, keepdims=True) + NORM_EPS) for x in o]
            n = [x * r for x, r in zip(o, rstd)]
            g_raw = [g_ref[rows, sl] for sl in heads]
            sg = [_sigmoid(x) for x in g_raw]
            d_out = [dog_ref[rows, sl] for sl in heads]
            dy = [d * (g * s) for d, g, s in zip(d_out, g_raw, sg)]
            dn = [x * gain_v for x in dy]
            do = [(rstd[h] * (dn[h] - n[h] * jnp.mean(dn[h] * n[h], axis=-1, keepdims=True))).astype(BF16) for h in hs]
            for h in hs:
                dgain = dy[h] * n[h] if dgain is None else dgain + dy[h] * n[h]
            for h, sl in enumerate(heads):
                dproj_ref[rows, 3 * D_MODEL + h * HD:3 * D_MODEL + (h + 1) * HD] = (
                    d_out[h] * n[h] * gain_v * (sg[h] * (1.0 + g_raw[h] * (1.0 - sg[h])))).astype(BF16)
            q_ab, k_bb = gt["q_a"].astype(BF16), gt["k_b"].astype(BF16)
            q_hb, k_tb = gt["q_hat"].astype(BF16), gt["k_til"].astype(BF16)
            v = i_ref[rows, :].astype(BF16)
            s_t = [st_ref[cc, h] for h in hs]
            ds_b = [x.astype(BF16) for x in ds_t]
            p = [jnp.where(causal, _dot(q_ab[:, sl], k_bb[:, sl], NT), 0.0).astype(BF16) for sl in heads]
            dp = [jnp.where(causal, _dot(do[h], v[:, sl], NT), 0.0).astype(BF16) for h, sl in enumerate(heads)]
            dv = [_dot(p[h], do[h], TN) + _dot(k_tb[:, sl], ds_b[h], NT) for h, sl in enumerate(heads)]
            dq_a = [_dot(dp[h], k_bb[:, sl], NN) for h, sl in enumerate(heads)]
            dk_b = [_dot(dp[h], q_ab[:, sl], TN) for h, sl in enumerate(heads)]
            dq_hat = [_dot(do[h], s_t[h].astype(BF16), NN) for h in hs]
            dk_til = [_dot(v[:, sl], ds_b[h], NN) for h, sl in enumerate(heads)]
            ds_new = [_dot(do[h], q_hb[:, sl], TN) + gt["e_last"][:, sl] * ds_t[h] for h, sl in enumerate(heads)]
            for h, sl in enumerate(heads):
                k_til = gt["k_til"][:, sl]
                db_last = jnp.sum(ds_t[h] * gt["e_last"][:, sl] * s_t[h], axis=0, keepdims=True) + jnp.sum(
                    dk_til[h] * k_til, axis=0, keepdims=True)
                dproj_ref[rows, 2 * D_MODEL + h * HD:2 * D_MODEL + (h + 1) * HD] = dv[h].astype(BF16)
                dq_scr[:, sl] = dq_a[h] * gt["e_a"][:, sl] + dq_hat[h] * gt["e_q"][:, sl]
                dk_scr[:, sl] = dk_b[h] * gt["e_b"][:, sl] + dk_til[h] * gt["e_k"][:, sl]
                db = (dq_a[h] * q_ab[:, sl].astype(F32) + dq_hat[h] * gt["q_hat"][:, sl]
                      - dk_b[h] * k_bb[:, sl].astype(F32) - dk_til[h] * k_til)
                db_scr[:, sl] = db + jnp.where(last_row, db_last, 0.0)
            dlogf = _running_sum(db_scr[...], False)
            sig_f, forget, sig_q = gt["sig_f"], gt["forget"], gt["sig_q"]
            dforget = dlogf / forget - dk_scr[...]
            dproj_ref[rows, D_MODEL:2 * D_MODEL] = (dforget * (1.0 - lbv) * sig_f * (1.0 - sig_f)).astype(BF16)
            dlb_ref[...] += jnp.sum(dforget * (1.0 - sig_f), axis=0, keepdims=True)
            dproj_ref[rows, 0:D_MODEL] = (dq_scr[...] * (sig_q * (1.0 + q_raw * (1.0 - sig_q)))).astype(BF16)
            ds_t = ds_new
        dgain_ref[...] += jnp.sum(dgain, axis=0, keepdims=True)
        for h in hs:
            ds_scr[h] = ds_t[h]

    col = lambda j: pl.BlockSpec((CPS * C, D_MODEL), lambda c: (NC - 1 - c, j))
    row = pl.BlockSpec((CPS * C, D_MODEL), lambda c: (NC - 1 - c, 0))
    return pl.pallas_call(
        body,
        out_shape=(jax.ShapeDtypeStruct((T, 4 * D_MODEL), BF16), jax.ShapeDtypeStruct((1, D_MODEL), F32),
                   jax.ShapeDtypeStruct((1, HD), F32)),
        grid=(NC,),
        in_specs=[col(0), col(1), col(2), col(3), row, row,
                  pl.BlockSpec((CPS, H, HD, HD), lambda c: (NC - 1 - c, 0, 0, 0)),
                  pl.BlockSpec((1, D_MODEL), lambda c: (0, 0)), pl.BlockSpec((1, HD), lambda c: (0, 0))],
        out_specs=(pl.BlockSpec((CPS * C, 4 * D_MODEL), lambda c: (NC - 1 - c, 0)),
                   pl.BlockSpec((1, D_MODEL), lambda c: (0, 0)), pl.BlockSpec((1, HD), lambda c: (0, 0))),
        scratch_shapes=[pltpu.VMEM((H, HD, HD), F32)] + [pltpu.VMEM((CPS, C, D_MODEL), F32)] * 3,
        compiler_params=_params("arbitrary"), name=name)(proj, proj, proj, proj, o_pre, d_og, states, lb, gain)


def _attn_masks():
    r = lax.broadcasted_iota(jnp.int32, (ATTN_BLOCK, ATTN_BLOCK), 0)
    c = lax.broadcasted_iota(jnp.int32, (ATTN_BLOCK, ATTN_BLOCK), 1)
    return c >= r, c <= r


def _attn_fwd(qkv, dilation, name):
    T = qkv.shape[0]
    nb = T // dilation // ATTN_BLOCK
    W = ATTN_GROUP_WIDTH
    B = ATTN_BLOCK
    scale = ATTN_DIM ** -0.5
    qb = 2 if nb % 2 == 0 else 1
    steps = nb // qb

    def body(q_ref, kp_ref, kc_ref, vp_ref, vc_ref, o_ref, lse_ref):
        no_prev = jnp.where(pl.program_id(1) > 0, 0.0, NEG_BIG)
        m_prev, m_cur = _attn_masks()
        ones = jnp.ones((B, ATTN_DIM), BF16)
        items = []
        for j in range(qb):
            for h in range(ATTN_GROUP_HEADS):
                sl = slice(h * ATTN_DIM, (h + 1) * ATTN_DIM)
                rows = slice(j * B, (j + 1) * B)
                if j == 0:
                    items.append((rows, sl, kp_ref[:, sl], vp_ref[:, sl], no_prev))
                else:
                    before = slice((j - 1) * B, j * B)
                    items.append((rows, sl, kc_ref[before, sl], vc_ref[before, sl], 0.0))
        s_p = [jnp.where(m_prev, _dot(q_ref[rows, sl], k_p, NT) * scale + bias, NEG_BIG)
               for rows, sl, k_p, _, bias in items]
        s_c = [jnp.where(m_cur, _dot(q_ref[rows, sl], kc_ref[rows, sl], NT) * scale, NEG_BIG)
               for rows, sl, _, _, _ in items]
        m = [jnp.max(jnp.maximum(a, b), axis=-1, keepdims=True) for a, b in zip(s_p, s_c)]
        p_p = [jnp.exp(a - mx).astype(BF16) for a, mx in zip(s_p, m)]
        p_c = [jnp.exp(b - mx).astype(BF16) for b, mx in zip(s_c, m)]
        l = [_dot(a, ones, NN) + _dot(b, ones, NN) for a, b in zip(p_p, p_c)]
        acc = [_dot(a, v_p, NN) + _dot(b, vc_ref[rows, sl], NN)
               for a, b, (rows, sl, _, v_p, _) in zip(p_p, p_c, items)]
        for (rows, sl, _, _, _), a, lv, mx in zip(items, acc, l, m):
            o_ref[rows, sl] = (a / lv).astype(BF16)
            lse_ref[rows, sl] = mx + jnp.log(lv)

    cur = lambda col: pl.BlockSpec((qb * B, W), lambda s, n: (s * steps + n, col))
    prev = lambda col: pl.BlockSpec((B, W), lambda s, n: (s * nb + jnp.maximum(qb * n - 1, 0), col))
    out = pl.BlockSpec((qb * B, W), lambda s, n: (s * steps + n, 0))
    return pl.pallas_call(
        body, out_shape=(jax.ShapeDtypeStruct((T, W), BF16), jax.ShapeDtypeStruct((T, W), F32)),
        grid=(dilation, steps),
        in_specs=[cur(0), prev(1), cur(1), prev(2), cur(2)],
        out_specs=(out, out), compiler_params=_params("parallel", "arbitrary"), name=name)(qkv, qkv, qkv, qkv, qkv)


def _attn_bwd(qkv, d_out, lse, delta, cos, sin, dilation, name):
    T = qkv.shape[0]
    nb = T // dilation // ATTN_BLOCK
    assert nb % 2 == 0, "an even number of 128-token blocks per residue class"
    pairs = nb // 2
    W = ATTN_GROUP_WIDTH
    B = ATTN_BLOCK
    scale = ATTN_DIM ** -0.5

    def unrope(x, cos_v, sin_v):
        return x * cos_v + pltpu.roll(x * sin_v, ATTN_DIM // 2, 1)

    def body(qa_ref, qb_ref, kpair_ref, kc_ref, vpair_ref, vc_ref, doa_ref, dob_ref, lsea_ref, lseb_ref,
             dla_ref, dlb_ref, cos_ref, sin_ref, out_ref, dq_scr, dk_scr, dv_scr):
        n = pl.program_id(1)

        @pl.when(n == 0)
        def _():
            dq_scr[...] = jnp.zeros_like(dq_scr)
            dk_scr[...] = jnp.zeros_like(dk_scr)
            dv_scr[...] = jnp.zeros_like(dv_scr)

        no_a = jnp.where(n > 0, 0.0, NEG_BIG)
        no_b = jnp.where(n < pairs, 0.0, NEG_BIG)
        m_prev, m_cur = _attn_masks()
        lo, hi = slice(0, B), slice(B, 2 * B)
        heads = [slice(h * ATTN_DIM, (h + 1) * ATTN_DIM) for h in range(ATTN_GROUP_HEADS)]
        flat = []
        for sl in heads:
            qa, qb = qa_ref[:, sl], qb_ref[:, sl]
            doa, dob = doa_ref[:, sl], dob_ref[:, sl]
            k0, k1, k2 = kpair_ref[lo, sl], kpair_ref[hi, sl], kc_ref[:, sl]
            v0, v1, v2 = vpair_ref[lo, sl], vpair_ref[hi, sl], vc_ref[:, sl]
            flat += [(qa, doa, lsea_ref[:, sl], dla_ref[:, sl], k0, v0, m_prev, no_a),
                     (qa, doa, lsea_ref[:, sl], dla_ref[:, sl], k1, v1, m_cur, no_a),
                     (qb, dob, lseb_ref[:, sl], dlb_ref[:, sl], k1, v1, m_prev, no_a + no_b),
                     (qb, dob, lseb_ref[:, sl], dlb_ref[:, sl], k2, v2, m_cur, no_b)]
        s = [_dot(q, k, NT) for q, _, _, _, k, _, _, _ in flat]
        dp = [_dot(do, v, NT) for _, do, _, _, _, v, _, _ in flat]
        p = [jnp.where(mask, jnp.exp(sv * scale - lse_v + bias), 0.0)
             for sv, (_, _, lse_v, _, _, _, mask, bias) in zip(s, flat)]
        ds = [(pv * (dpv - dl_v) * scale).astype(BF16) for pv, dpv, (_, _, _, dl_v, _, _, _, _) in zip(p, dp, flat)]
        p = [pv.astype(BF16) for pv in p]
        dq_part = [_dot(dsv, k, NN) for dsv, (_, _, _, _, k, _, _, _) in zip(ds, flat)]
        dk_part = [_dot(dsv, q, TN) for dsv, (q, _, _, _, _, _, _, _) in zip(ds, flat)]
        dv_part = [_dot(pv, do, TN) for pv, (_, do, _, _, _, _, _, _) in zip(p, flat)]
        cos_lo, sin_lo, cos_hi, sin_hi = cos_ref[lo, :], sin_ref[lo, :], cos_ref[hi, :], sin_ref[hi, :]
        for h, sl in enumerate(heads):
            a_prev, a_cur, b_prev, b_cur = range(4 * h, 4 * h + 4)
            kcol = slice(W + h * ATTN_DIM, W + (h + 1) * ATTN_DIM)
            vcol = slice(2 * W + h * ATTN_DIM, 2 * W + (h + 1) * ATTN_DIM)
            out_ref[lo, sl] = unrope(dq_scr[:, sl], cos_lo, sin_lo).astype(BF16)
            out_ref[hi, sl] = unrope(dq_part[a_prev] + dq_part[a_cur], cos_hi, sin_hi).astype(BF16)
            out_ref[lo, kcol] = unrope(dk_scr[:, sl] + dk_part[a_prev], cos_lo, sin_lo).astype(BF16)
            out_ref[hi, kcol] = unrope(dk_part[a_cur] + dk_part[b_prev], cos_hi, sin_hi).astype(BF16)
            out_ref[lo, vcol] = (dv_scr[:, sl] + dv_part[a_prev]).astype(BF16)
            out_ref[hi, vcol] = (dv_part[a_cur] + dv_part[b_prev]).astype(BF16)
            dq_scr[:, sl] = dq_part[b_prev] + dq_part[b_cur]
            dk_scr[:, sl] = dk_part[b_cur]
            dv_scr[:, sl] = dv_part[b_cur]

    def block_a(n):
        return jnp.maximum(2 * n - 1, 0)

    def block_b(n):
        return jnp.minimum(2 * n, nb - 1)

    def pair(n):
        return jnp.maximum(n - 1, 0)

    one_a = lambda col: pl.BlockSpec((B, W), lambda s, n: (s * nb + block_a(n), col))
    one_b = lambda col: pl.BlockSpec((B, W), lambda s, n: (s * nb + block_b(n), col))
    two = lambda col: pl.BlockSpec((2 * B, W), lambda s, n: (s * pairs + pair(n), col))
    tab = pl.BlockSpec((2 * B, ATTN_DIM), lambda s, n: (s * pairs + pair(n), 0))
    return pl.pallas_call(
        body, out_shape=jax.ShapeDtypeStruct((T, 3 * W), BF16), grid=(dilation, pairs + 1),
        in_specs=[one_a(0), one_b(0), two(1), one_b(1), two(2), one_b(2), one_a(0), one_b(0), one_a(0), one_b(0),
                  one_a(0), one_b(0), tab, tab],
        out_specs=pl.BlockSpec((2 * B, 3 * W), lambda s, n: (s * pairs + pair(n), 0)),
        scratch_shapes=[pltpu.VMEM((B, W), F32)] * 3,
        compiler_params=_params("parallel", "arbitrary"), name=name)(
            qkv, qkv, qkv, qkv, qkv, qkv, d_out, d_out, lse, lse, delta, delta, cos, sin)


PERM_TILE = 512
LANES = 128


def _residue_view(x, d):
    return x if d == 1 else x.reshape(d, x.shape[0] // d, x.shape[1])


def _residue_spec(d, tm, cols):
    if d == 1:
        return pl.BlockSpec((tm, cols), lambda i: (i, 0))
    return pl.BlockSpec((d, tm // d, cols), lambda i: (0, i, 0))


def _residue_shape(T, d, cols, dtype):
    return jax.ShapeDtypeStruct((T, cols) if d == 1 else (d, T // d, cols), dtype)


def _class_rows(r, d, tm):
    return pl.ds(r, tm // d, stride=d)


def _attn_norm(h, gain, name):
    T = h.shape[0]
    tm = _pick_tile(T, PERM_TILE, 16 * max(ATTN_DILATIONS))
    dils = ATTN_DILATIONS
    (base_cos, base_sin), (off_cos, off_sin), sign = _rope_parts(T, tm)

    def body(h_ref, g_ref, bc_ref, bs_ref, oc_ref, os_ref, sign_ref, *refs):
        u_refs, c_refs, s_refs, u_scr, c_scr, s_scr = refs[0:3], refs[3:6], refs[6:9], refs[9], refs[10], refs[11]
        hv = h_ref[...]
        rstd = lax.rsqrt(jnp.mean(hv * hv, axis=-1, keepdims=True) + NORM_EPS)
        u = hv * rstd * g_ref[...]
        for j in range(D_MODEL // LANES):
            u_scr[j] = u[:, j * LANES:(j + 1) * LANES]
        bc, bs, oc, osn = bc_ref[0], bs_ref[0], oc_ref[...], os_ref[...]
        c_scr[...] = bc * oc - bs * osn
        s_scr[...] = (bs * oc + bc * osn) * sign_ref[...]
        for d, u_ref, c_ref, s_ref in zip(dils, u_refs, c_refs, s_refs):
            if d == 1:
                u_ref[...] = u.astype(BF16)
                c_ref[...] = c_scr[...]
                s_ref[...] = s_scr[...]
                continue
            for r in range(d):
                rows = _class_rows(r, d, tm)
                for j in range(D_MODEL // LANES):
                    u_ref[r, :, j * LANES:(j + 1) * LANES] = u_scr.at[j][rows, :].astype(BF16)
                c_ref[r] = c_scr[rows, :]
                s_ref[r] = s_scr[rows, :]

    row = pl.BlockSpec((tm, D_MODEL), lambda i: (i, 0))
    base = pl.BlockSpec((1, 1, ATTN_DIM), lambda i: (i, 0, 0))
    off = pl.BlockSpec((tm, ATTN_DIM), lambda i: (0, 0))
    res = pl.pallas_call(
        body,
        out_shape=([_residue_shape(T, d, D_MODEL, BF16) for d in dils]
                   + [_residue_shape(T, d, ATTN_DIM, F32) for d in dils] * 2),
        grid=(T // tm,),
        in_specs=[row, pl.BlockSpec((1, D_MODEL), lambda i: (0, 0)), base, base, off, off,
                  pl.BlockSpec((1, ATTN_DIM), lambda i: (0, 0))],
        out_specs=([_residue_spec(d, tm, D_MODEL) for d in dils] + [_residue_spec(d, tm, ATTN_DIM) for d in dils] * 2),
        scratch_shapes=[pltpu.VMEM((D_MODEL // LANES, tm, LANES), F32), pltpu.VMEM((tm, ATTN_DIM), F32),
                        pltpu.VMEM((tm, ATTN_DIM), F32)],
        compiler_params=_params("parallel"), name=name)(h, gain, base_cos, base_sin, off_cos, off_sin, sign)
    flat = [r.reshape(T, r.shape[-1]) for r in res]
    return flat[0:3], flat[3:6], flat[6:9]


def _attn_merge_fwd(outs, lses, name):
    T = outs[0].shape[0]
    W = ATTN_GROUP_WIDTH
    tm = _pick_tile(T, PERM_TILE, 16 * max(ATTN_DILATIONS))
    dils = ATTN_DILATIONS

    def body(*refs):
        o_refs, l_refs, oc_ref, lse_refs = refs[0:3], refs[3:6], refs[6], refs[7:10]
        o_scr, l_scr, t_scr = refs[10:13]
        nh = ATTN_GROUP_HEADS
        for g, d in enumerate(dils):
            for j in range(nh):
                lanes = slice(j * LANES, (j + 1) * LANES)
                if d == 1:
                    o_scr[g * nh + j] = o_refs[g][:, lanes].astype(F32)
                    l_scr[g * nh + j] = l_refs[g][:, lanes]
                    continue
                for r in range(d):
                    rows = _class_rows(r, d, tm)
                    o_scr.at[g * nh + j][rows, :] = o_refs[g][r, :, lanes].astype(F32)
                    l_scr.at[g * nh + j][rows, :] = l_refs[g][r, :, lanes]
        for j in range(nh):
            lanes = slice(j * LANES, (j + 1) * LANES)
            ls = [l_scr[g * nh + j] for g in range(3)]
            m = jnp.maximum(jnp.maximum(ls[0], ls[1]), ls[2])
            tot = m + jnp.log(jnp.exp(ls[0] - m) + jnp.exp(ls[1] - m) + jnp.exp(ls[2] - m))
            t_scr[j] = tot
            for g, d in enumerate(dils):
                oc_ref[:, g * W + j * LANES:g * W + (j + 1) * LANES] = (
                    o_scr[g * nh + j] * jnp.exp(ls[g] - tot)).astype(BF16)
                if d == 1:
                    lse_refs[g][:, lanes] = tot
                    continue
                for r in range(d):
                    lse_refs[g][r, :, lanes] = t_scr.at[j][_class_rows(r, d, tm), :]

    in_blk = [_residue_spec(d, tm, W) for d in dils]
    n_blk = 3 * ATTN_GROUP_HEADS
    res = pl.pallas_call(
        body, out_shape=[jax.ShapeDtypeStruct((T, 3 * W), BF16)] + [_residue_shape(T, d, W, F32) for d in dils],
        grid=(T // tm,), in_specs=in_blk * 2,
        out_specs=[pl.BlockSpec((tm, 3 * W), lambda i: (i, 0))] + in_blk,
        scratch_shapes=[pltpu.VMEM((n_blk, tm, LANES), F32), pltpu.VMEM((n_blk, tm, LANES), F32),
                        pltpu.VMEM((ATTN_GROUP_HEADS, tm, LANES), F32)],
        compiler_params=_params("parallel"), name=name)(
            *[_residue_view(o, d) for o, d in zip(outs, dils)], *[_residue_view(l, d) for l, d in zip(lses, dils)])
    return res[0], [r.reshape(T, W) for r in res[1:]]


def _attn_merge_bwd(d_oc, oc, name):
    T = d_oc.shape[0]
    W = ATTN_GROUP_WIDTH
    tm = _pick_tile(T, PERM_TILE, 16 * max(ATTN_DILATIONS))
    dils = ATTN_DILATIONS

    def body(d_ref, o_ref, *refs):
        delta_refs, db_refs, dl_scr, d_scr = refs[0:3], refs[3:6], refs[6], refs[7]
        nh = ATTN_GROUP_HEADS
        for j in range(nh):
            tot = jnp.zeros((tm, 1), F32)
            for g in range(3):
                cols = slice(g * W + j * LANES, g * W + (j + 1) * LANES)
                d_blk = d_ref[:, cols]
                d_scr[g * nh + j] = d_blk
                tot = tot + jnp.sum(d_blk * o_ref[:, cols].astype(F32), axis=-1, keepdims=True)
            dl_scr[j] = jnp.broadcast_to(tot, (tm, LANES))
        for g, d in enumerate(dils):
            for j in range(nh):
                lanes = slice(j * LANES, (j + 1) * LANES)
                if d == 1:
                    delta_refs[g][:, lanes] = dl_scr[j]
                    db_refs[g][:, lanes] = d_scr[g * nh + j].astype(BF16)
                    continue
                for r in range(d):
                    rows = _class_rows(r, d, tm)
                    delta_refs[g][r, :, lanes] = dl_scr.at[j][rows, :]
                    db_refs[g][r, :, lanes] = d_scr.at[g * nh + j][rows, :].astype(BF16)

    wide = pl.BlockSpec((tm, 3 * W), lambda i: (i, 0))
    out_blk = [_residue_spec(d, tm, W) for d in dils]
    res = pl.pallas_call(
        body, out_shape=[_residue_shape(T, d, W, F32) for d in dils] + [_residue_shape(T, d, W, BF16) for d in dils],
        grid=(T // tm,), in_specs=[wide, wide], out_specs=out_blk * 2,
        scratch_shapes=[pltpu.VMEM((ATTN_GROUP_HEADS, tm, LANES), F32),
                        pltpu.VMEM((3 * ATTN_GROUP_HEADS, tm, LANES), F32)],
        compiler_params=_params("parallel"), name=name)(d_oc, oc)
    flat = [r.reshape(T, W) for r in res]
    return flat[0:3], flat[3:6]


def _rope_parts(T, tile):
    inv_freq = 1.0 / (ROPE_THETA ** (jnp.arange(0, ATTN_DIM, 2, dtype=F32) / ATTN_DIM))
    inv_freq = jnp.concatenate([inv_freq, inv_freq])[None, :]
    base = (jnp.arange(T // tile, dtype=F32) * tile)[:, None] * inv_freq
    off = jnp.arange(tile, dtype=F32)[:, None] * inv_freq
    sign = jnp.concatenate([-jnp.ones((1, ATTN_DIM // 2), F32), jnp.ones((1, ATTN_DIM // 2), F32)], axis=1)
    return (jnp.cos(base)[:, None, :], jnp.sin(base)[:, None, :]), (jnp.cos(off), jnp.sin(off)), sign


WEIGHT_GROUPS = {"hgrn": ("hgrn_in", "hgrn_out"), "ffn0": ("ffn_in0", "ffn_down0"),
                 "attn": ("qkv", "attn_out"), "ffn1": ("ffn_in1", "ffn_down1")}


def _local_step(x, target, norm_mix, norm_ffn, lb, out_gain, final_gain, fetch, publish):
    g_mix = [norm_mix[0:1], norm_mix[1:2]]
    g_ffn = [norm_ffn[0:1], norm_ffn[1:2]]
    w = {}

    def whole(name):
        return [(w[name], w[name].shape[0], 0)]

    def qkv_parts(g):
        return [(w["qkv"], ATTN_GROUP_WIDTH, 3 * j + g) for j in range(3)]

    def ffn_fwd(h, layer, head=None):
        w.update(fetch(f"ffn{layer}"))
        n, gate, up, a = _ffn_in(h, g_ffn[layer], w[f"ffn_in{layer}"], f"ffn{layer}_in")
        out = _mm_nn([a], [whole(f"ffn_down{layer}")], h, name=f"ffn{layer}_down", head=head)
        return out, (n, gate, up, a)

    def ffn_bwd(h, saved, dh, dhb, layer):
        n, gate, up, a = saved
        w_in = w[f"ffn_in{layer}"]
        dgate, dup = _ffn_down_dx(dhb, w[f"ffn_down{layer}"], gate, up, f"ffn{layer}_down_dx")
        grad_in = _mm_tn(dgate, n, name=f"ffn{layer}_in_dw_gate", rows=2 * D_FF)
        grad_in = _mm_tn(dup, n, name=f"ffn{layer}_in_dw_up", into=grad_in, row_tile=D_FF // GRAD_TILE, rows=2 * D_FF)
        grads = {f"ffn_down{layer}": _mm_tn(a, dhb, name=f"ffn{layer}_down_dw"), f"ffn_in{layer}": grad_in}
        publish(f"ffn{layer}", grads)
        return _mm_nn([dgate, dup], [[(w_in, D_FF, 0)], [(w_in, D_FF, 1)]], dh, name=f"ffn{layer}_in_dx",
                      norm=(h, g_ffn[layer]))

    u0 = _rms_fwd(x, g_mix[0], "hgrn_norm")
    w.update(fetch("hgrn"))
    proj = _mm_nt(u0, whole("hgrn_in"), out_dtype=F32, name="hgrn_in")
    og, o_pre, states = _hgrn_fwd(proj, lb, out_gain, "hgrn_fwd")
    h1 = _mm_nn([og], [whole("hgrn_out")], x, name="hgrn_out")
    h2, ffn0 = ffn_fwd(h1, 0)

    u1_g, cos_g, sin_g = _attn_norm(h2, g_mix[1], "attn_norm")
    w.update(fetch("attn"))
    qkv_g, outs, lses = [], [], []
    for g, d in enumerate(ATTN_DILATIONS):
        qkv_g.append(_mm_nt(u1_g[g], qkv_parts(g), out_dtype=BF16, name=f"attn_qkv{g}",
                            rope=(cos_g[g], sin_g[g], 2)))
        o_g, lse_g = _attn_fwd(qkv_g[g], d, f"attn_fwd{g}")
        outs.append(o_g)
        lses.append(lse_g)
    oc, lse_all = _attn_merge_fwd(outs, lses, "attn_merge")
    h3 = _mm_nn([oc], [whole("attn_out")], h2, name="attn_out")
    (dh4, dh4b, d_final, loss_part), ffn1 = ffn_fwd(h3, 1, head=(target, final_gain))
    dh3, dh3b, d_ffn1 = ffn_bwd(h3, ffn1, dh4, dh4b, 1)

    d_oc = _mm_nt(dh3b, whole("attn_out"), out_dtype=F32, name="attn_out_dx")
    grad_attn_out = _mm_tn(oc, dh3b, name="attn_out_dw")
    delta, d_ocb = _attn_merge_bwd(d_oc, oc, "attn_merge_bwd")
    du1, qkv_pieces = [], []
    for g, d in enumerate(ATTN_DILATIONS):
        dqkv = _attn_bwd(qkv_g[g], d_ocb[g], lse_all[g], delta[g], cos_g[g], sin_g[g], d, f"attn_bwd{g}")
        qkv_pieces.append(_mm_tn(dqkv, u1_g[g], name=f"attn_qkv_dw{g}"))
        du1.append(_mm_nn([dqkv], [qkv_parts(g)], None, name=f"attn_qkv_dx{g}"))
    grad_qkv = jnp.stack([p.reshape(3, ATTN_GROUP_WIDTH, D_MODEL) for p in qkv_pieces], axis=1).reshape(
        3 * ATTN_WIDTH, D_MODEL)
    publish("attn", {"qkv": grad_qkv, "attn_out": grad_attn_out})
    dh2, dh2b, d_mix1 = _rms_bwd(h2, g_mix[1], du1, dh3, "attn_norm_bwd", ATTN_DILATIONS)

    dh1, dh1b, d_ffn0 = ffn_bwd(h1, ffn0, dh2, dh2b, 0)

    d_og = _mm_nt(dh1b, whole("hgrn_out"), out_dtype=F32, name="hgrn_out_dx")
    grad_hgrn_out = _mm_tn(og, dh1b, name="hgrn_out_dw")
    dproj, d_lb, d_out_gain = _hgrn_bwd(proj, o_pre, d_og, states, lb, out_gain, "hgrn_bwd")
    publish("hgrn", {"hgrn_in": _mm_tn(dproj, u0, name="hgrn_in_dw"), "hgrn_out": grad_hgrn_out})
    dx, _, d_mix0 = _mm_nn([dproj], [whole("hgrn_in")], dh1, name="hgrn_in_dx", norm=(x, g_mix[0]))

    small = dict(norm_mix0=d_mix0, norm_mix1=d_mix1, norm_ffn0=d_ffn0, norm_ffn1=d_ffn1, lb=d_lb,
                 out_gain=d_out_gain, final=d_final, loss=loss_part)
    return dx, small


WEIGHT_NAMES = ("hgrn_in", "hgrn_out", "qkv", "attn_out", "ffn_in0", "ffn_in1", "ffn_down0", "ffn_down1")
MESH_IDS = pl.DeviceIdType.MESH
HBM_SPEC = pl.BlockSpec(memory_space=pl.ANY)


N_PEERS = N_DEV - 1
PEER_OFFSETS = [(dx, dy, dc) for dx in (0, 1) for dy in (0, 1) for dc in (0, 1)][1:]


def _mesh_place():
    x, y, c = lax.axis_index("x"), lax.axis_index("y"), lax.axis_index("c")
    peers = []
    for dx, dy, dc in PEER_OFFSETS:
        px, py, pc = (1 - x if dx else x), (1 - y if dy else y), (1 - c if dc else c)
        peers.append(((px, py, pc), 4 * px + 2 * py + pc))
    return 4 * x + 2 * y + c, peers


def _exchange_launch(srcs, scatter, collective_id, name):
    n = len(srcs)
    src_refs = [jax.new_ref(s, memory_space=pltpu.MemorySpace.HBM) for s in srcs]
    land_refs = [jax.empty_ref(jax.ShapeDtypeStruct(s.shape if scatter else (N_DEV,) + s.shape, s.dtype),
                               memory_space=pltpu.MemorySpace.HBM) for s in srcs]

    @pl.kernel(mesh=plsc.ScalarSubcoreMesh(axis_name="sequencer", num_cores=1), name=name,
               scratch_types=(pltpu.SemaphoreType.DMA((n * N_PEERS,)), pltpu.SemaphoreType.DMA((n * N_PEERS,)),
                              pltpu.SemaphoreType.DMA((n,))),
               compiler_params=pltpu.CompilerParams(collective_id=collective_id))
    def launch(send_sems, recv_sems, local_sems):
        me, peers = _mesh_place()
        barrier = pltpu.get_barrier_semaphore()
        for peer, _ in peers:
            pl.semaphore_signal(barrier, inc=1, device_id=peer, device_id_type=MESH_IDS)
        pl.semaphore_wait(barrier, N_PEERS)
        own = [pltpu.make_async_copy(src_refs[w].at[me] if scatter else src_refs[w], land_refs[w].at[me],
                                     local_sems.at[w]) for w in range(n)]
        for cp in own:
            cp.start()
        copies = [pltpu.make_async_remote_copy(
            src_ref=src_refs[w].at[pid] if scatter else src_refs[w], dst_ref=land_refs[w].at[me],
            send_sem=send_sems.at[w * N_PEERS + k], recv_sem=recv_sems.at[w * N_PEERS + k],
            device_id=peer, device_id_type=MESH_IDS) for w in range(n) for k, (peer, pid) in enumerate(peers)]
        for cp in copies:
            cp.start()
        for cp in copies:
            cp.wait()
        for cp in own:
            cp.wait()

    launch()
    return land_refs


def _gather_small(block, name):
    def body(in_ref, out_ref, send_sems, recv_sems, local_sem):
        me, peers = _mesh_place()
        own = pltpu.make_async_copy(in_ref, out_ref.at[me], local_sem)
        own.start()
        sends = [pltpu.make_async_remote_copy(
            src_ref=in_ref, dst_ref=out_ref.at[me], send_sem=send_sems.at[k], recv_sem=recv_sems.at[k],
            device_id=peer, device_id_type=MESH_IDS) for k, (peer, _) in enumerate(peers)]
        for cp in sends:
            cp.start()
        for cp in sends:
            cp.wait_recv()
        for cp in sends:
            cp.wait_send()
        own.wait()

    return pl.pallas_call(
        body, out_shape=jax.ShapeDtypeStruct((N_DEV,) + block.shape, block.dtype),
        in_specs=[HBM_SPEC], out_specs=HBM_SPEC,
        scratch_shapes=[pltpu.SemaphoreType.DMA((N_PEERS,)), pltpu.SemaphoreType.DMA((N_PEERS,)),
                        pltpu.SemaphoreType.DMA],
        name=name)(block)


def _sum_blocks(recv, name):
    rows = recv.shape[1]
    tr = _pick_tile(rows, 256, 16)

    def body(r_ref, g_ref):
        acc = r_ref[0].astype(F32)
        for j in range(1, N_DEV):
            acc = acc + r_ref[j].astype(F32)
        g_ref[...] = acc

    return pl.pallas_call(
        body, out_shape=jax.ShapeDtypeStruct((rows, D_MODEL), F32), grid=(rows // tr,),
        in_specs=[pl.BlockSpec((N_DEV, tr, D_MODEL), lambda i: (0, i, 0))],
        out_specs=pl.BlockSpec((tr, D_MODEL), lambda i: (i, 0)),
        compiler_params=_params("parallel"), name=name)(recv)


def _adamw_math(w, g, m, v):
    m_new = ADAM_B1 * m + (1.0 - ADAM_B1) * g
    v_new = ADAM_B2 * v + (1.0 - ADAM_B2) * (g * g)
    m_hat = m_new / (1.0 - ADAM_B1 ** ADAM_STEP)
    v_hat = v_new / (1.0 - ADAM_B2 ** ADAM_STEP)
    delta = -ADAM_LR * (m_hat / (jnp.sqrt(v_hat) + ADAM_EPS) + ADAM_WD * w)
    return delta, m_new, v_new


def _adamw(w, g, m, v, name):
    rows, cols = w.shape
    tr = _pick_tile(rows, 256, 8)

    def body(w_ref, g_ref, m_ref, v_ref, d_ref, mo_ref, vo_ref):
        d_ref[...], mo_ref[...], vo_ref[...] = _adamw_math(w_ref[...], g_ref[...], m_ref[...], v_ref[...])

    blk = pl.BlockSpec((tr, cols), lambda i: (i, 0))
    return pl.pallas_call(
        body, out_shape=(jax.ShapeDtypeStruct((rows, cols), F32),) * 3, grid=(rows // tr,),
        in_specs=[blk] * 4, out_specs=(blk,) * 3, compiler_params=_params("parallel"), name=name)(w, g, m, v)


ROW_MIX, ROW_FFN, ROW_LB, ROW_OUT_GAIN, ROW_FINAL = 0, 2, 4, 7, 8
PART_MIX, PART_FFN, PART_LB, PART_OUT_GAIN, PART_FINAL, PART_LOSS = 0, 2, 4, 5, 6, 7


def _small_update(parts_all, w, m, v, name):
    def body(p_ref, w_ref, m_ref, v_ref, g_ref, d_ref, mo_ref, vo_ref, loss_ref):
        def total(row, n=1):
            tot = p_ref[0, row:row + n, :]
            for j in range(1, N_DEV):
                tot = tot + p_ref[j, row:row + n, :]
            return tot

        logits = [w_ref[ROW_LB + i:ROW_LB + i + 1, :] for i in range(3)]
        mx = jnp.maximum(jnp.maximum(logits[0], logits[1]), logits[2])
        ex = [jnp.exp(l - mx) for l in logits]
        den = ex[0] + ex[1] + ex[2]
        prob = [e / den for e in ex]
        d_lb = total(PART_LB)
        g_ref[...] = jnp.zeros_like(g_ref)
        g_ref[ROW_MIX:ROW_MIX + 2, :] = total(PART_MIX, 2)
        g_ref[ROW_FFN:ROW_FFN + 2, :] = total(PART_FFN, 2)
        for i in range(3):
            g_ref[ROW_LB + i:ROW_LB + i + 1, :] = prob[i] * ((d_lb if i == 0 else 0.0) - prob[0] * d_lb)
        g_ref[ROW_OUT_GAIN:ROW_OUT_GAIN + 1, :] = total(PART_OUT_GAIN)
        g_ref[ROW_FINAL:ROW_FINAL + 1, :] = total(PART_FINAL)
        d_ref[...], mo_ref[...], vo_ref[...] = _adamw_math(w_ref[...], g_ref[...], m_ref[...], v_ref[...])
        loss_ref[...] = jnp.sum(total(PART_LOSS), axis=-1, keepdims=True)

    packed = jax.ShapeDtypeStruct((16, D_MODEL), F32)
    return pl.pallas_call(
        body, out_shape=(packed, packed, packed, packed, jax.ShapeDtypeStruct((1, 1), F32)),
        compiler_params=pltpu.CompilerParams(vmem_limit_bytes=VMEM_LIMIT), name=name)(parts_all, w, m, v)


def _pack_small(norm_mix, norm_ffn, lb_logits, out_gain, final):
    pad = jnp.zeros((1, D_MODEL - HGRN_DIM), F32)
    return jnp.concatenate([norm_mix, norm_ffn, lb_logits, jnp.concatenate([out_gain, pad], axis=1),
                            final.reshape(1, D_MODEL), jnp.zeros((16 - ROW_FINAL - 1, D_MODEL), F32)], axis=0)


def _unpack_small(p):
    return (p[ROW_MIX:ROW_MIX + 2], p[ROW_FFN:ROW_FFN + 2], p[ROW_LB:ROW_LB + 3],
            p[ROW_OUT_GAIN:ROW_OUT_GAIN + 1, :HGRN_DIM], p[ROW_FINAL])


def _lower_bound(lb_logits, name):
    def body(l_ref, o_ref):
        logits = [l_ref[i:i + 1, :] for i in range(3)]
        mx = jnp.maximum(jnp.maximum(logits[0], logits[1]), logits[2])
        ex = [jnp.exp(l - mx) for l in logits]
        o_ref[...] = ex[0] / (ex[0] + ex[1] + ex[2])

    return pl.pallas_call(body, out_shape=jax.ShapeDtypeStruct((1, D_MODEL), F32), name=name)(lb_logits)


def kernel(x, norm_mix, norm_ffn, hgrn_w_in, hgrn_lb_logits, hgrn_out_norm, hgrn_w_out, attn_w_qkv, attn_w_out, ffn_w_in, ffn_w_down, final_norm, loss_target, m_norm_mix, m_norm_ffn, m_hgrn_w_in, m_hgrn_lb_logits, m_hgrn_out_norm, m_hgrn_w_out, m_attn_w_qkv, m_attn_w_out, m_ffn_w_in, m_ffn_w_down, m_final_norm, v_norm_mix, v_norm_ffn, v_hgrn_w_in, v_hgrn_lb_logits, v_hgrn_out_norm, v_hgrn_w_out, v_attn_w_qkv, v_attn_w_out, v_ffn_w_in, v_ffn_w_down, v_final_norm):
    col_sharded = {"hgrn_in": hgrn_w_in[0], "qkv": attn_w_qkv[0], "ffn_in0": ffn_w_in[0], "ffn_in1": ffn_w_in[1]}
    row_sharded = {"hgrn_out": hgrn_w_out[0], "attn_out": attn_w_out[0], "ffn_down0": ffn_w_down[0],
                   "ffn_down1": ffn_w_down[1]}
    gathering = {}
    for gi, (group, names) in enumerate(WEIGHT_GROUPS.items()):
        shards = [(col_sharded[n].T if n in col_sharded else row_sharded[n]).astype(BF16) for n in names]
        gathering[group] = _exchange_launch(shards, False, 1 + gi, f"weights_gather_{group}")

    def fetch(group):
        return {n: land[...].reshape(-1, D_MODEL) for n, land in zip(WEIGHT_GROUPS[group], gathering[group])}

    in_flight = {}

    def publish(group, grads):
        names = WEIGHT_GROUPS[group]
        parts = [grads[n].reshape(N_DEV, -1, D_MODEL) for n in names]
        in_flight[group] = _exchange_launch(parts, True, 1 + len(WEIGHT_GROUPS) + list(WEIGHT_GROUPS).index(group),
                                            f"grads_send_{group}")

    lb = _lower_bound(hgrn_lb_logits, "hgrn_lower_bound")
    grad_x, small = _local_step(x[0], loss_target[0], norm_mix, norm_ffn, lb, hgrn_out_norm,
                                final_norm.reshape(1, D_MODEL), fetch, publish)

    pad = jnp.zeros((1, D_MODEL - HGRN_DIM), F32)
    small_part = jnp.concatenate(
        [small["norm_mix0"], small["norm_mix1"], small["norm_ffn0"], small["norm_ffn1"], small["lb"],
         jnp.concatenate([small["out_gain"], pad], axis=1), small["final"], small["loss"]], axis=0)
    small_all = _gather_small(small_part, "small_grads_gather")
    received = {}
    for group in ("ffn1", "attn", "ffn0", "hgrn"):
        received.update(zip(WEIGHT_GROUPS[group], [land[...] for land in in_flight[group]]))

    masters = {"hgrn_in": (hgrn_w_in[0], m_hgrn_w_in[0], v_hgrn_w_in[0]),
               "hgrn_out": (hgrn_w_out[0], m_hgrn_w_out[0], v_hgrn_w_out[0]),
               "qkv": (attn_w_qkv[0], m_attn_w_qkv[0], v_attn_w_qkv[0]),
               "attn_out": (attn_w_out[0], m_attn_w_out[0], v_attn_w_out[0]),
               "ffn_in0": (ffn_w_in[0], m_ffn_w_in[0], v_ffn_w_in[0]),
               "ffn_in1": (ffn_w_in[1], m_ffn_w_in[1], v_ffn_w_in[1]),
               "ffn_down0": (ffn_w_down[0], m_ffn_w_down[0], v_ffn_w_down[0]),
               "ffn_down1": (ffn_w_down[1], m_ffn_w_down[1], v_ffn_w_down[1])}
    res = {}
    for n in WEIGHT_NAMES:
        g = _sum_blocks(received[n], f"{n}_grad_sum")
        if n in col_sharded:
            g = g.T
        wv, mv, vv = masters[n]
        res[n] = (g,) + tuple(_adamw(wv, g, mv, vv, f"{n}_adamw"))

    def single(n):
        return [t[None] for t in res[n]]

    def pair(n):
        return [jnp.stack([a, b]) for a, b in zip(res[n + "0"], res[n + "1"])]

    big = dict(hgrn_w_in=single("hgrn_in"), hgrn_w_out=single("hgrn_out"), attn_w_qkv=single("qkv"),
               attn_w_out=single("attn_out"), ffn_w_in=pair("ffn_in"), ffn_w_down=pair("ffn_down"))

    w_small = _pack_small(norm_mix, norm_ffn, hgrn_lb_logits, hgrn_out_norm, final_norm)
    m_small = _pack_small(m_norm_mix, m_norm_ffn, m_hgrn_lb_logits, m_hgrn_out_norm, m_final_norm)
    v_small = _pack_small(v_norm_mix, v_norm_ffn, v_hgrn_lb_logits, v_hgrn_out_norm, v_final_norm)
    g_s, d_s, m_s, v_s, loss = _small_update(small_all, w_small, m_small, v_small, "small_update")
    small_out = [_unpack_small(t) for t in (g_s, d_s, m_s, v_s)]

    def group(i):
        s = small_out[i]
        return (s[0], s[1], big["hgrn_w_in"][i], s[2], s[3], big["hgrn_w_out"][i], big["attn_w_qkv"][i],
                big["attn_w_out"][i], big["ffn_w_in"][i], big["ffn_w_down"][i], s[4])

    return (loss.reshape(()), grad_x[None], *group(0), *group(1), *group(2), *group(3))
```

```python
import functools

import jax
import jax.numpy as jnp
from jax import lax
from jax.experimental import pallas as pl
from jax.experimental.pallas import tpu as pltpu
from jax.experimental.pallas import tpu_sc as plsc

F32 = jnp.float32
BF16 = jnp.bfloat16

D_MODEL = 1024
N_DEV = 8
NORM_EPS = 1e-6

HGRN_HEADS = 8
HGRN_DIM = 128
HGRN_CHUNK = 64
HGRN_STEP_CHUNKS = 2
HGRN_EXP_CLAMP = 60.0

ATTN_DIM = 128
ATTN_BLOCK = 128
ATTN_GROUP_HEADS = 4
ATTN_GROUP_WIDTH = ATTN_GROUP_HEADS * ATTN_DIM
ATTN_DILATIONS = (1, 4, 16)
ATTN_WIDTH = 3 * ATTN_GROUP_WIDTH
ROPE_THETA = 10000.0
NEG_BIG = -1e30

D_FF = 2816

ADAM_LR = 0.001
ADAM_B1 = 0.9
ADAM_B2 = 0.999
ADAM_EPS = 1e-08
ADAM_WD = 0.01
ADAM_STEP = 10

VMEM_LIMIT = 48 * 1024 * 1024

NT = (((1,), (1,)), ((), ()))
NN = (((1,), (0,)), ((), ()))
TN = (((0,), (0,)), ((), ()))


def _dot(a, b, dims):
    return lax.dot_general(a, b, dims, preferred_element_type=F32)


def _params(*sem):
    return pltpu.CompilerParams(dimension_semantics=sem, vmem_limit_bytes=VMEM_LIMIT)


def _pick_tile(n, cap, mult):
    best = None
    for t in range(mult, min(n, cap) + 1, mult):
        if n % t == 0:
            best = t
    assert best is not None, (n, cap, mult)
    return best


def _sigmoid(x):
    return 0.5 * jnp.tanh(0.5 * x) + 0.5


ROW_TILE = 512
COL_CHUNK = 512
GRAD_TILE = 256


def _whole(shape, index_map):
    return pl.BlockSpec(shape, index_map, pipeline_mode=pl.Buffered(1))


def _part_specs(parts, n_cols):
    return [_whole((rows, n_cols), functools.partial(lambda i, b: (b, 0), b=blk)) for _, rows, blk in parts]


def _mm_nt(a, w_parts, *, out_dtype, name, rope=None):
    M, K = a.shape
    tm = _pick_tile(M, ROW_TILE, 16)
    widths = [rows for _, rows, _ in w_parts]
    n_parts = len(w_parts)

    def body(*refs):
        a_ref, w_refs, o_ref = refs[0], refs[1:1 + n_parts], refs[-1]
        av = a_ref[...]
        off = 0
        for p, w_ref in enumerate(w_refs):
            for c0 in range(0, widths[p], COL_CHUNK):
                cw = min(COL_CHUNK, widths[p] - c0)
                acc = _dot(av, w_ref[c0:c0 + cw, :], NT)
                if rope is not None and p < rope[2]:
                    cos, sin = refs[1 + n_parts][...], refs[2 + n_parts][...]
                    for h0 in range(0, cw, ATTN_DIM):
                        xh = acc[:, h0:h0 + ATTN_DIM]
                        rot = pltpu.roll(xh, ATTN_DIM // 2, 1)
                        o_ref[:, off + c0 + h0:off + c0 + h0 + ATTN_DIM] = (xh * cos + rot * sin).astype(out_dtype)
                else:
                    o_ref[:, off + c0:off + c0 + cw] = acc.astype(out_dtype)
            off += widths[p]

    in_specs = [pl.BlockSpec((tm, K), lambda i: (i, 0))] + _part_specs(w_parts, K)
    args = [a] + [w for w, _, _ in w_parts]
    if rope is not None:
        in_specs += [pl.BlockSpec((tm, ATTN_DIM), lambda i: (i, 0))] * 2
        args += [rope[0], rope[1]]
    return pl.pallas_call(
        body, out_shape=jax.ShapeDtypeStruct((M, sum(widths)), out_dtype), grid=(M // tm,),
        in_specs=in_specs, out_specs=pl.BlockSpec((tm, sum(widths)), lambda i: (i, 0)),
        compiler_params=_params("parallel"), name=name)(*args)


def _mm_nn(a_list, w_parts_list, resid, *, name, norm=None, head=None):
    M = a_list[0].shape[0]
    tm = _pick_tile(M, ROW_TILE, 16)
    n_a = len(a_list)
    flat_parts = [p for parts in w_parts_list for p in parts]
    extra = norm if norm is not None else head
    n_in = n_a + len(flat_parts) + (1 if resid is not None else 0) + (2 if extra is not None else 0)

    def body(*refs):
        a_refs, w_refs = refs[:n_a], refs[n_a:n_a + len(flat_parts)]

        def product(rows):
            acc = None
            wi = 0
            for a_ref, parts in zip(a_refs, w_parts_list):
                off = 0
                for _, k, _ in parts:
                    term = _dot(a_ref[rows, off:off + k], w_refs[wi][...], NN)
                    acc = term if acc is None else acc + term
                    off += k
                    wi += 1
            return acc

        if extra is None:
            acc = product(slice(None))
            if resid is not None:
                acc = acc + refs[n_in - 1][...]
            refs[n_in][...] = acc
            return

        @pl.when(pl.program_id(0) == 0)
        def _():
            for acc_ref in refs[n_in + 2:]:
                acc_ref[...] = jnp.zeros_like(acc_ref)

        for r0 in range(0, tm, tm // 2):
            rows = slice(r0, r0 + tm // 2)
            acc = product(rows)
            if head is not None:
                _loss_head_math(acc + refs[n_in - 3][rows, :], rows, refs[n_in - 2], refs[n_in - 1],
                                *refs[n_in:n_in + 4])
                continue
            dres_ref, x_ref, g_ref = refs[n_in - 3:n_in]
            dx_ref, dxb_ref, dg_ref = refs[n_in:n_in + 3]
            xv = x_ref[rows, :]
            rstd = lax.rsqrt(jnp.mean(xv * xv, axis=-1, keepdims=True) + NORM_EPS)
            n = xv * rstd
            dg_ref[...] += jnp.sum(acc * n, axis=0, keepdims=True)
            dn = acc * g_ref[...]
            dx = dres_ref[rows, :] + rstd * (dn - n * jnp.mean(dn * n, axis=-1, keepdims=True))
            dx_ref[rows, :] = dx
            dxb_ref[rows, :] = dx.astype(BF16)

    row = pl.BlockSpec((tm, D_MODEL), lambda i: (i, 0))
    vec = pl.BlockSpec((1, D_MODEL), lambda i: (0, 0))
    in_specs = [pl.BlockSpec((tm, a.shape[1]), lambda i: (i, 0)) for a in a_list] + _part_specs(flat_parts, D_MODEL)
    args = list(a_list) + [w for w, _, _ in flat_parts]
    if resid is not None:
        in_specs.append(row)
        args.append(resid)
    if extra is None:
        return pl.pallas_call(
            body, out_shape=jax.ShapeDtypeStruct((M, D_MODEL), F32), grid=(M // tm,),
            in_specs=in_specs, out_specs=row, compiler_params=_params("parallel"), name=name)(*args)
    assert resid is not None
    out_shape = [jax.ShapeDtypeStruct((M, D_MODEL), F32), jax.ShapeDtypeStruct((M, D_MODEL), BF16),
                 jax.ShapeDtypeStruct((1, D_MODEL), F32)]
    out_specs = [row, row, vec]
    if head is not None:
        out_shape.append(jax.ShapeDtypeStruct((1, D_MODEL), F32))
        out_specs.append(vec)
    return pl.pallas_call(
        body, out_shape=out_shape, grid=(M // tm,), in_specs=in_specs + [row, vec], out_specs=out_specs,
        compiler_params=_params("arbitrary"), name=name)(*args, extra[0], extra[1])


def _mm_tn(a, b, *, name, into=None, row_tile=0, rows=None):
    T, R = a.shape
    N = b.shape[1]
    tr = GRAD_TILE
    rows = R if rows is None else rows

    def body(a_ref, b_ref, *refs):
        refs[-1][...] = _dot(a_ref[...], b_ref[...], TN).astype(BF16)

    in_specs = [pl.BlockSpec((T, tr), lambda r: (0, r)), _whole((T, N), lambda r: (0, 0))]
    args = [a, b]
    if into is not None:
        in_specs.append(HBM_SPEC)
        args.append(into)
    return pl.pallas_call(
        body, out_shape=jax.ShapeDtypeStruct((rows, N), BF16), grid=(R // tr,),
        in_specs=in_specs, out_specs=pl.BlockSpec((tr, N), lambda r: (row_tile + r, 0)),
        input_output_aliases={} if into is None else {2: 0},
        compiler_params=_params("parallel"), name=name)(*args)


def _rms_fwd(x, gain, name):
    T = x.shape[0]
    tm = _pick_tile(T, 512, 16)

    def body(x_ref, g_ref, u_ref):
        xv = x_ref[...]
        rstd = lax.rsqrt(jnp.mean(xv * xv, axis=-1, keepdims=True) + NORM_EPS)
        u_ref[...] = (xv * rstd * g_ref[...]).astype(BF16)

    return pl.pallas_call(
        body, out_shape=jax.ShapeDtypeStruct((T, D_MODEL), BF16), grid=(T // tm,),
        in_specs=[pl.BlockSpec((tm, D_MODEL), lambda i: (i, 0)), pl.BlockSpec((1, D_MODEL), lambda i: (0, 0))],
        out_specs=pl.BlockSpec((tm, D_MODEL), lambda i: (i, 0)),
        compiler_params=_params("parallel"), name=name)(x, gain)


def _rms_bwd(x, gain, dus, dres, name, dilations=(1,)):
    T = x.shape[0]
    tm = _pick_tile(T, PERM_TILE, 16 * max(dilations))
    n_du = len(dus)

    def body(x_ref, g_ref, *refs):
        du_refs, dres_ref = refs[:n_du], refs[n_du]
        dx_ref, dxb_ref, dg_ref, du_scr = refs[n_du + 1:]

        @pl.when(pl.program_id(0) == 0)
        def _():
            dg_ref[...] = jnp.zeros_like(dg_ref)

        if tuple(dilations) == (1,):
            du = du_refs[0][...]
        else:
            for i, (d, du_ref) in enumerate(zip(dilations, du_refs)):
                for j in range(D_MODEL // LANES):
                    lanes = slice(j * LANES, (j + 1) * LANES)
                    if d == 1:
                        du_scr[j] = du_ref[:, lanes] if i == 0 else du_scr[j] + du_ref[:, lanes]
                        continue
                    blk = du_scr.at[j]
                    for r in range(d):
                        rows = _class_rows(r, d, tm)
                        blk[rows, :] = du_ref[r, :, lanes] if i == 0 else blk[rows, :] + du_ref[r, :, lanes]
            du = jnp.concatenate([du_scr[j] for j in range(D_MODEL // LANES)], axis=1)
        xv = x_ref[...]
        rstd = lax.rsqrt(jnp.mean(xv * xv, axis=-1, keepdims=True) + NORM_EPS)
        n = xv * rstd
        dg_ref[...] += jnp.sum(du * n, axis=0, keepdims=True)
        dn = du * g_ref[...]
        dx = dres_ref[...] + rstd * (dn - n * jnp.mean(dn * n, axis=-1, keepdims=True))
        dx_ref[...] = dx
        dxb_ref[...] = dx.astype(BF16)

    row = pl.BlockSpec((tm, D_MODEL), lambda i: (i, 0))
    vec = pl.BlockSpec((1, D_MODEL), lambda i: (0, 0))
    return pl.pallas_call(
        body,
        out_shape=(jax.ShapeDtypeStruct((T, D_MODEL), F32), jax.ShapeDtypeStruct((T, D_MODEL), BF16),
                   jax.ShapeDtypeStruct((1, D_MODEL), F32)),
        grid=(T // tm,), in_specs=[row, vec] + [_residue_spec(d, tm, D_MODEL) for d in dilations] + [row],
        out_specs=(row, row, vec), scratch_shapes=[pltpu.VMEM((D_MODEL // LANES, tm, LANES), F32)],
        compiler_params=_params("arbitrary"), name=name)(
            x, gain, *[_residue_view(du, d) for du, d in zip(dus, dilations)], dres)


def _loss_head_math(hv, rows, t_ref, g_ref, dh_ref, dhb_ref, dg_ref, loss_ref):
    inv_f = 1.0 / D_MODEL
    g = g_ref[...]
    rstd = lax.rsqrt(jnp.mean(hv * hv, axis=-1, keepdims=True) + NORM_EPS)
    n = hv * rstd
    err = n * g - t_ref[rows, :]
    loss_ref[...] += (0.5 * inv_f) * jnp.sum(err * err, axis=0, keepdims=True)
    dy = err * inv_f
    dg_ref[...] += jnp.sum(dy * n, axis=0, keepdims=True)
    dn = dy * g
    dh = rstd * (dn - n * jnp.mean(dn * n, axis=-1, keepdims=True))
    dh_ref[rows, :] = dh
    dhb_ref[rows, :] = dh.astype(BF16)


FFN_TILE = 256


def _ffn_in(h, gain, w_in, name):
    T = h.shape[0]
    tm = _pick_tile(T, ROW_TILE, 16)

    def body(h_ref, g_ref, w_ref, n_ref, gate_ref, up_ref, a_ref):
        hv = h_ref[...]
        rstd = lax.rsqrt(jnp.mean(hv * hv, axis=-1, keepdims=True) + NORM_EPS)
        n = (hv * rstd * g_ref[...]).astype(BF16)
        n_ref[...] = n
        for c0 in range(0, D_FF, FFN_TILE):
            cols = slice(c0, c0 + FFN_TILE)
            gate = _dot(n, w_ref[c0:c0 + FFN_TILE, :], NT)
            up = _dot(n, w_ref[D_FF + c0:D_FF + c0 + FFN_TILE, :], NT)
            gate_ref[:, cols] = gate.astype(BF16)
            up_ref[:, cols] = up.astype(BF16)
            a_ref[:, cols] = (gate * _sigmoid(gate) * up).astype(BF16)

    row = pl.BlockSpec((tm, D_MODEL), lambda i: (i, 0))
    wide = pl.BlockSpec((tm, D_FF), lambda i: (i, 0))
    wide_shape = jax.ShapeDtypeStruct((T, D_FF), BF16)
    return pl.pallas_call(
        body, out_shape=(jax.ShapeDtypeStruct((T, D_MODEL), BF16), wide_shape, wide_shape, wide_shape),
        grid=(T // tm,),
        in_specs=[row, pl.BlockSpec((1, D_MODEL), lambda i: (0, 0)), _whole((2 * D_FF, D_MODEL), lambda i: (0, 0))],
        out_specs=(row, wide, wide, wide), compiler_params=_params("parallel"), name=name)(h, gain, w_in)


def _ffn_down_dx(dhb, w_down, gate, up, name):
    T = dhb.shape[0]
    tm = _pick_tile(T, ROW_TILE, 16)

    def body(dh_ref, w_ref, gate_ref, up_ref, dgate_ref, dup_ref):
        dh = dh_ref[...]
        for c0 in range(0, D_FF, FFN_TILE):
            cols = slice(c0, c0 + FFN_TILE)
            da = _dot(dh, w_ref[c0:c0 + FFN_TILE, :], NT)
            gate = gate_ref[:, cols].astype(F32)
            sg = _sigmoid(gate)
            dgate_ref[:, cols] = (da * up_ref[:, cols].astype(F32) * (sg * (1.0 + gate * (1.0 - sg)))).astype(BF16)
            dup_ref[:, cols] = (da * gate * sg).astype(BF16)

    wide = pl.BlockSpec((tm, D_FF), lambda i: (i, 0))
    wide_shape = jax.ShapeDtypeStruct((T, D_FF), BF16)
    return pl.pallas_call(
        body, out_shape=(wide_shape, wide_shape), grid=(T // tm,),
        in_specs=[pl.BlockSpec((tm, D_MODEL), lambda i: (i, 0)), _whole((D_FF, D_MODEL), lambda i: (0, 0)), wide, wide],
        out_specs=(wide, wide), compiler_params=_params("parallel"), name=name)(dhb, w_down, gate, up)


def _tri(n, lower):
    r = lax.broadcasted_iota(jnp.int32, (n, n), 0)
    c = lax.broadcasted_iota(jnp.int32, (n, n), 1)
    return (c <= r) if lower else (c >= r)


def _running_sum(x, lower):
    tri = _tri(x.shape[0], lower).astype(BF16)
    hi = x.astype(BF16)
    rest = x - hi.astype(F32)
    mid = rest.astype(BF16)
    lo = (rest - mid.astype(F32)).astype(BF16)
    return _dot(tri, hi, NN) + _dot(tri, mid, NN) + _dot(tri, lo, NN)


def _hgrn_gates(q_raw, f_raw, lb):
    C = q_raw.shape[0]
    sig_f = _sigmoid(f_raw)
    forget = lb + (1.0 - lb) * sig_f
    key = 1.0 - forget
    log_f = jnp.log(forget)
    b = _running_sum(log_f, True)
    first_half = lax.broadcasted_iota(jnp.int32, log_f.shape, 0) < C // 2
    r = jnp.sum(jnp.where(first_half, log_f, 0.0), axis=0, keepdims=True)
    b_last = jnp.sum(log_f, axis=0, keepdims=True)
    e_a = jnp.exp(jnp.minimum(b - r, HGRN_EXP_CLAMP))
    e_b = jnp.exp(jnp.minimum(r - b, HGRN_EXP_CLAMP))
    e_q = jnp.exp(b)
    e_k = jnp.exp(b_last - b)
    sig_q = _sigmoid(q_raw)
    query = q_raw * sig_q
    return dict(sig_f=sig_f, forget=forget, sig_q=sig_q, e_a=e_a, e_b=e_b, e_q=e_q, e_k=e_k,
                e_last=jnp.exp(b_last), q_a=query * e_a, k_b=key * e_b, q_hat=query * e_q, k_til=key * e_k)


def _hgrn_fwd(proj, lb, gain, name):
    T = proj.shape[0]
    C = HGRN_CHUNK
    CPS = HGRN_STEP_CHUNKS
    H, HD = HGRN_HEADS, HGRN_DIM

    def body(q_ref, f_ref, i_ref, g_ref, lb_ref, gain_ref, og_ref, o_ref, st_ref, s_scr):
        @pl.when(pl.program_id(0) == 0)
        def _():
            s_scr[...] = jnp.zeros_like(s_scr)

        causal = _tri(C, True)
        gain_v = gain_ref[...]
        heads = [slice(h * HD, (h + 1) * HD) for h in range(H)]
        s_t = [s_scr[h] for h in range(H)]
        for cc in range(CPS):
            rows = slice(cc * C, (cc + 1) * C)
            for h in range(H):
                st_ref[cc, h] = s_t[h]
            gt = _hgrn_gates(q_ref[rows, :], f_ref[rows, :], lb_ref[...])
            q_a, k_b = gt["q_a"].astype(BF16), gt["k_b"].astype(BF16)
            q_hat, k_til = gt["q_hat"].astype(BF16), gt["k_til"].astype(BF16)
            v = i_ref[rows, :].astype(BF16)
            p = [jnp.where(causal, _dot(q_a[:, sl], k_b[:, sl], NT), 0.0).astype(BF16) for sl in heads]
            o = [_dot(p[h], v[:, sl], NN) + _dot(q_hat[:, sl], s_t[h].astype(BF16), NT)
                 for h, sl in enumerate(heads)]
            s_t = [gt["e_last"][:, sl] * s_t[h] + _dot(v[:, sl], k_til[:, sl], TN) for h, sl in enumerate(heads)]
            for h, sl in enumerate(heads):
                o_ref[rows, sl] = o[h]
                rstd = lax.rsqrt(jnp.mean(o[h] * o[h], axis=-1, keepdims=True) + NORM_EPS)
                g_raw = g_ref[rows, sl]
                og_ref[rows, sl] = (o[h] * rstd * gain_v * (g_raw * _sigmoid(g_raw))).astype(BF16)
        for h in range(H):
            s_scr[h] = s_t[h]

    col = lambda j: pl.BlockSpec((CPS * C, D_MODEL), lambda c: (c, j))
    row = pl.BlockSpec((CPS * C, D_MODEL), lambda c: (c, 0))
    return pl.pallas_call(
        body,
        out_shape=(jax.ShapeDtypeStruct((T, D_MODEL), BF16), jax.ShapeDtypeStruct((T, D_MODEL), F32),
                   jax.ShapeDtypeStruct((T // C, H, HD, HD), F32)),
        grid=(T // (CPS * C),),
        in_specs=[col(0), col(1), col(2), col(3), pl.BlockSpec((1, D_MODEL), lambda c: (0, 0)),
                  pl.BlockSpec((1, HD), lambda c: (0, 0))],
        out_specs=(row, row, pl.BlockSpec((CPS, H, HD, HD), lambda c: (c, 0, 0, 0))),
        scratch_shapes=[pltpu.VMEM((H, HD, HD), F32)],
        compiler_params=_params("arbitrary"), name=name)(proj, proj, proj, proj, lb, gain)


def _hgrn_bwd(proj, o_pre, d_og, states, lb, gain, name):
    T = proj.shape[0]
    C = HGRN_CHUNK
    CPS = HGRN_STEP_CHUNKS
    H, HD = HGRN_HEADS, HGRN_DIM
    NC = T // (CPS * C)

    def body(q_ref, f_ref, i_ref, g_ref, o_ref, dog_ref, st_ref, lb_ref, gain_ref,
             dproj_ref, dlb_ref, dgain_ref, ds_scr, dq_all, dk_all, db_all):
        @pl.when(pl.program_id(0) == 0)
        def _():
            ds_scr[...] = jnp.zeros_like(ds_scr)
            dlb_ref[...] = jnp.zeros_like(dlb_ref)
            dgain_ref[...] = jnp.zeros_like(dgain_ref)

        lbv = lb_ref[...]
        causal = _tri(C, True)
        last_row = lax.broadcasted_iota(jnp.int32, (C, HD), 0) == C - 1
        gain_v = gain_ref[...]
        heads = [slice(h * HD, (h + 1) * HD) for h in range(H)]
        hs = range(H)
        ds_t = [ds_scr[h] for h in hs]
        dgain = None
        for cc in reversed(range(CPS)):
            rows = slice(cc * C, (cc + 1) * C)
            dq_scr, dk_scr, db_scr = dq_all.at[cc], dk_all.at[cc], db_all.at[cc]
            q_raw = q_ref[rows, :]
            gt = _hgrn_gates(q_raw, f_ref[rows, :], lbv)
            o = [o_ref[rows, sl] for sl in heads]
            rstd = [lax.rsqrt(jnp.mean(x * x, axis=-1, keepdims=True) + NORM_EPS) for x in o]
            n = [x * r for x, r in zip(o, rstd)]
            g_raw = [g_ref[rows, sl] for sl in heads]
            sg = [_sigmoid(x) for x in g_raw]
            d_out = [dog_ref[rows, sl] for sl in heads]
            dy = [d * (g * s) for d, g, s in zip(d_out, g_raw, sg)]
            dn = [x * gain_v for x in dy]
            do = [(rstd[h] * (dn[h] - n[h] * jnp.mean(dn[h] * n[h], axis=-1, keepdims=True))).astype(BF16) for h in hs]
            for h in hs:
                dgain = dy[h] * n[h] if dgain is None else dgain + dy[h] * n[h]
            for h, sl in enumerate(heads):
                dproj_ref[rows, 3 * D_MODEL + h * HD:3 * D_MODEL + (h + 1) * HD] = (
                    d_out[h] * n[h] * gain_v * (sg[h] * (1.0 + g_raw[h] * (1.0 - sg[h])))).astype(BF16)
            q_ab, k_bb = gt["q_a"].astype(BF16), gt["k_b"].astype(BF16)
            q_hb, k_tb = gt["q_hat"].astype(BF16), gt["k_til"].astype(BF16)
            v = i_ref[rows, :].astype(BF16)
            s_t = [st_ref[cc, h] for h in hs]
            ds_b = [x.astype(BF16) for x in ds_t]
            p = [jnp.where(causal, _dot(q_ab[:, sl], k_bb[:, sl], NT), 0.0).astype(BF16) for sl in heads]
            dp = [jnp.where(causal, _dot(do[h], v[:, sl], NT), 0.0).astype(BF16) for h, sl in enumerate(heads)]
            dv = [_dot(p[h], do[h], TN) + _dot(k_tb[:, sl], ds_b[h], NT) for h, sl in enumerate(heads)]
            dq_a = [_dot(dp[h], k_bb[:, sl], NN) for h, sl in enumerate(heads)]
            dk_b = [_dot(dp[h], q_ab[:, sl], TN) for h, sl in enumerate(heads)]
            dq_hat = [_dot(do[h], s_t[h].astype(BF16), NN) for h in hs]
            dk_til = [_dot(v[:, sl], ds_b[h], NN) for h, sl in enumerate(heads)]
            ds_new = [_dot(do[h], q_hb[:, sl], TN) + gt["e_last"][:, sl] * ds_t[h] for h, sl in enumerate(heads)]
            for h, sl in enumerate(heads):
                k_til = gt["k_til"][:, sl]
                db_last = jnp.sum(ds_t[h] * gt["e_last"][:, sl] * s_t[h], axis=0, keepdims=True) + jnp.sum(
                    dk_til[h] * k_til, axis=0, keepdims=True)
                dproj_ref[rows, 2 * D_MODEL + h * HD:2 * D_MODEL + (h + 1) * HD] = dv[h].astype(BF16)
                dq_scr[:, sl] = dq_a[h] * gt["e_a"][:, sl] + dq_hat[h] * gt["e_q"][:, sl]
                dk_scr[:, sl] = dk_b[h] * gt["e_b"][:, sl] + dk_til[h] * gt["e_k"][:, sl]
                db = (dq_a[h] * q_ab[:, sl].astype(F32) + dq_hat[h] * gt["q_hat"][:, sl]
                      - dk_b[h] * k_bb[:, sl].astype(F32) - dk_til[h] * k_til)
                db_scr[:, sl] = db + jnp.where(last_row, db_last, 0.0)
            dlogf = _running_sum(db_scr[...], False)
            sig_f, forget, sig_q = gt["sig_f"], gt["forget"], gt["sig_q"]
            dforget = dlogf / forget - dk_scr[...]
            dproj_ref[rows, D_MODEL:2 * D_MODEL] = (dforget * (1.0 - lbv) * sig_f * (1.0 - sig_f)).astype(BF16)
            dlb_ref[...] += jnp.sum(dforget * (1.0 - sig_f), axis=0, keepdims=True)
            dproj_ref[rows, 0:D_MODEL] = (dq_scr[...] * (sig_q * (1.0 + q_raw * (1.0 - sig_q)))).astype(BF16)
            ds_t = ds_new
        dgain_ref[...] += jnp.sum(dgain, axis=0, keepdims=True)
        for h in hs:
            ds_scr[h] = ds_t[h]

    col = lambda j: pl.BlockSpec((CPS * C, D_MODEL), lambda c: (NC - 1 - c, j))
    row = pl.BlockSpec((CPS * C, D_MODEL), lambda c: (NC - 1 - c, 0))
    return pl.pallas_call(
        body,
        out_shape=(jax.ShapeDtypeStruct((T, 4 * D_MODEL), BF16), jax.ShapeDtypeStruct((1, D_MODEL), F32),
                   jax.ShapeDtypeStruct((1, HD), F32)),
        grid=(NC,),
        in_specs=[col(0), col(1), col(2), col(3), row, row,
                  pl.BlockSpec((CPS, H, HD, HD), lambda c: (NC - 1 - c, 0, 0, 0)),
                  pl.BlockSpec((1, D_MODEL), lambda c: (0, 0)), pl.BlockSpec((1, HD), lambda c: (0, 0))],
        out_specs=(pl.BlockSpec((CPS * C, 4 * D_MODEL), lambda c: (NC - 1 - c, 0)),
                   pl.BlockSpec((1, D_MODEL), lambda c: (0, 0)), pl.BlockSpec((1, HD), lambda c: (0, 0))),
        scratch_shapes=[pltpu.VMEM((H, HD, HD), F32)] + [pltpu.VMEM((CPS, C, D_MODEL), F32)] * 3,
        compiler_params=_params("arbitrary"), name=name)(proj, proj, proj, proj, o_pre, d_og, states, lb, gain)


def _attn_masks():
    r = lax.broadcasted_iota(jnp.int32, (ATTN_BLOCK, ATTN_BLOCK), 0)
    c = lax.broadcasted_iota(jnp.int32, (ATTN_BLOCK, ATTN_BLOCK), 1)
    return c >= r, c <= r


def _attn_fwd(qkv, dilation, name):
    T = qkv.shape[0]
    nb = T // dilation // ATTN_BLOCK
    W = ATTN_GROUP_WIDTH
    B = ATTN_BLOCK
    scale = ATTN_DIM ** -0.5
    qb = 2 if nb % 2 == 0 else 1
    steps = nb // qb

    def body(q_ref, kp_ref, kc_ref, vp_ref, vc_ref, o_ref, lse_ref):
        no_prev = jnp.where(pl.program_id(1) > 0, 0.0, NEG_BIG)
        m_prev, m_cur = _attn_masks()
        ones = jnp.ones((B, ATTN_DIM), BF16)
        items = []
        for j in range(qb):
            for h in range(ATTN_GROUP_HEADS):
                sl = slice(h * ATTN_DIM, (h + 1) * ATTN_DIM)
                rows = slice(j * B, (j + 1) * B)
                if j == 0:
                    items.append((rows, sl, kp_ref[:, sl], vp_ref[:, sl], no_prev))
                else:
                    before = slice((j - 1) * B, j * B)
                    items.append((rows, sl, kc_ref[before, sl], vc_ref[before, sl], 0.0))
        s_p = [jnp.where(m_prev, _dot(q_ref[rows, sl], k_p, NT) * scale + bias, NEG_BIG)
               for rows, sl, k_p, _, bias in items]
        s_c = [jnp.where(m_cur, _dot(q_ref[rows, sl], kc_ref[rows, sl], NT) * scale, NEG_BIG)
               for rows, sl, _, _, _ in items]
        m = [jnp.max(jnp.maximum(a, b), axis=-1, keepdims=True) for a, b in zip(s_p, s_c)]
        p_p = [jnp.exp(a - mx).astype(BF16) for a, mx in zip(s_p, m)]
        p_c = [jnp.exp(b - mx).astype(BF16) for b, mx in zip(s_c, m)]
        l = [_dot(a, ones, NN) + _dot(b, ones, NN) for a, b in zip(p_p, p_c)]
        acc = [_dot(a, v_p, NN) + _dot(b, vc_ref[rows, sl], NN)
               for a, b, (rows, sl, _, v_p, _) in zip(p_p, p_c, items)]
        for (rows, sl, _, _, _), a, lv, mx in zip(items, acc, l, m):
            o_ref[rows, sl] = (a / lv).astype(BF16)
            lse_ref[rows, sl] = mx + jnp.log(lv)

    cur = lambda col: pl.BlockSpec((qb * B, W), lambda s, n: (s * steps + n, col))
    prev = lambda col: pl.BlockSpec((B, W), lambda s, n: (s * nb + jnp.maximum(qb * n - 1, 0), col))
    out = pl.BlockSpec((qb * B, W), lambda s, n: (s * steps + n, 0))
    return pl.pallas_call(
        body, out_shape=(jax.ShapeDtypeStruct((T, W), BF16), jax.ShapeDtypeStruct((T, W), F32)),
        grid=(dilation, steps),
        in_specs=[cur(0), prev(1), cur(1), prev(2), cur(2)],
        out_specs=(out, out), compiler_params=_params("parallel", "arbitrary"), name=name)(qkv, qkv, qkv, qkv, qkv)


def _attn_bwd(qkv, d_out, lse, delta, cos, sin, dilation, name):
    T = qkv.shape[0]
    nb = T // dilation // ATTN_BLOCK
    assert nb % 2 == 0, "an even number of 128-token blocks per residue class"
    pairs = nb // 2
    W = ATTN_GROUP_WIDTH
    B = ATTN_BLOCK
    scale = ATTN_DIM ** -0.5

    def unrope(x, cos_v, sin_v):
        return x * cos_v + pltpu.roll(x * sin_v, ATTN_DIM // 2, 1)

    def body(qa_ref, qb_ref, kpair_ref, kc_ref, vpair_ref, vc_ref, doa_ref, dob_ref, lsea_ref, lseb_ref,
             dla_ref, dlb_ref, cos_ref, sin_ref, out_ref, dq_scr, dk_scr, dv_scr):
        n = pl.program_id(1)

        @pl.when(n == 0)
        def _():
            dq_scr[...] = jnp.zeros_like(dq_scr)
            dk_scr[...] = jnp.zeros_like(dk_scr)
            dv_scr[...] = jnp.zeros_like(dv_scr)

        no_a = jnp.where(n > 0, 0.0, NEG_BIG)
        no_b = jnp.where(n < pairs, 0.0, NEG_BIG)
        m_prev, m_cur = _attn_masks()
        lo, hi = slice(0, B), slice(B, 2 * B)
        heads = [slice(h * ATTN_DIM, (h + 1) * ATTN_DIM) for h in range(ATTN_GROUP_HEADS)]
        flat = []
        for sl in heads:
            qa, qb = qa_ref[:, sl], qb_ref[:, sl]
            doa, dob = doa_ref[:, sl], dob_ref[:, sl]
            k0, k1, k2 = kpair_ref[lo, sl], kpair_ref[hi, sl], kc_ref[:, sl]
            v0, v1, v2 = vpair_ref[lo, sl], vpair_ref[hi, sl], vc_ref[:, sl]
            flat += [(qa, doa, lsea_ref[:, sl], dla_ref[:, sl], k0, v0, m_prev, no_a),
                     (qa, doa, lsea_ref[:, sl], dla_ref[:, sl], k1, v1, m_cur, no_a),
                     (qb, dob, lseb_ref[:, sl], dlb_ref[:, sl], k1, v1, m_prev, no_a + no_b),
                     (qb, dob, lseb_ref[:, sl], dlb_ref[:, sl], k2, v2, m_cur, no_b)]
        s = [_dot(q, k, NT) for q, _, _, _, k, _, _, _ in flat]
        dp = [_dot(do, v, NT) for _, do, _, _, _, v, _, _ in flat]
        p = [jnp.where(mask, jnp.exp(sv * scale - lse_v + bias), 0.0)
             for sv, (_, _, lse_v, _, _, _, mask, bias) in zip(s, flat)]
        ds = [(pv * (dpv - dl_v) * scale).astype(BF16) for pv, dpv, (_, _, _, dl_v, _, _, _, _) in zip(p, dp, flat)]
        p = [pv.astype(BF16) for pv in p]
        dq_part = [_dot(dsv, k, NN) for dsv, (_, _, _, _, k, _, _, _) in zip(ds, flat)]
        dk_part = [_dot(dsv, q, TN) for dsv, (q, _, _, _, _, _, _, _) in zip(ds, flat)]
        dv_part = [_dot(pv, do, TN) for pv, (_, do, _, _, _, _, _, _) in zip(p, flat)]
        cos_lo, sin_lo, cos_hi, sin_hi = cos_ref[lo, :], sin_ref[lo, :], cos_ref[hi, :], sin_ref[hi, :]
        for h, sl in enumerate(heads):
            a_prev, a_cur, b_prev, b_cur = range(4 * h, 4 * h + 4)
            kcol = slice(W + h * ATTN_DIM, W + (h + 1) * ATTN_DIM)
            vcol = slice(2 * W + h * ATTN_DIM, 2 * W + (h + 1) * ATTN_DIM)
            out_ref[lo, sl] = unrope(dq_scr[:, sl], cos_lo, sin_lo).astype(BF16)
            out_ref[hi, sl] = unrope(dq_part[a_prev] + dq_part[a_cur], cos_hi, sin_hi).astype(BF16)
            out_ref[lo, kcol] = unrope(dk_scr[:, sl] + dk_part[a_prev], cos_lo, sin_lo).astype(BF16)
            out_ref[hi, kcol] = unrope(dk_part[a_cur] + dk_part[b_prev], cos_hi, sin_hi).astype(BF16)
            out_ref[lo, vcol] = (dv_scr[:, sl] + dv_part[a_prev]).astype(BF16)
            out_ref[hi, vcol] = (dv_part[a_cur] + dv_part[b_prev]).astype(BF16)
            dq_scr[:, sl] = dq_part[b_prev] + dq_part[b_cur]
            dk_scr[:, sl] = dk_part[b_cur]
            dv_scr[:, sl] = dv_part[b_cur]

    def block_a(n):
        return jnp.maximum(2 * n - 1, 0)

    def block_b(n):
        return jnp.minimum(2 * n, nb - 1)

    def pair(n):
        return jnp.maximum(n - 1, 0)

    one_a = lambda col: pl.BlockSpec((B, W), lambda s, n: (s * nb + block_a(n), col))
    one_b = lambda col: pl.BlockSpec((B, W), lambda s, n: (s * nb + block_b(n), col))
    two = lambda col: pl.BlockSpec((2 * B, W), lambda s, n: (s * pairs + pair(n), col))
    tab = pl.BlockSpec((2 * B, ATTN_DIM), lambda s, n: (s * pairs + pair(n), 0))
    return pl.pallas_call(
        body, out_shape=jax.ShapeDtypeStruct((T, 3 * W), BF16), grid=(dilation, pairs + 1),
        in_specs=[one_a(0), one_b(0), two(1), one_b(1), two(2), one_b(2), one_a(0), one_b(0), one_a(0), one_b(0),
                  one_a(0), one_b(0), tab, tab],
        out_specs=pl.BlockSpec((2 * B, 3 * W), lambda s, n: (s * pairs + pair(n), 0)),
        scratch_shapes=[pltpu.VMEM((B, W), F32)] * 3,
        compiler_params=_params("parallel", "arbitrary"), name=name)(
            qkv, qkv, qkv, qkv, qkv, qkv, d_out, d_out, lse, lse, delta, delta, cos, sin)


PERM_TILE = 512
LANES = 128


def _residue_view(x, d):
    return x if d == 1 else x.reshape(d, x.shape[0] // d, x.shape[1])


def _residue_spec(d, tm, cols):
    if d == 1:
        return pl.BlockSpec((tm, cols), lambda i: (i, 0))
    return pl.BlockSpec((d, tm // d, cols), lambda i: (0, i, 0))


def _residue_shape(T, d, cols, dtype):
    return jax.ShapeDtypeStruct((T, cols) if d == 1 else (d, T // d, cols), dtype)


def _class_rows(r, d, tm):
    return pl.ds(r, tm // d, stride=d)


def _attn_norm(h, gain, name):
    T = h.shape[0]
    tm = _pick_tile(T, PERM_TILE, 16 * max(ATTN_DILATIONS))
    dils = ATTN_DILATIONS
    (base_cos, base_sin), (off_cos, off_sin), sign = _rope_parts(T, tm)

    def body(h_ref, g_ref, bc_ref, bs_ref, oc_ref, os_ref, sign_ref, *refs):
        u_refs, c_refs, s_refs, u_scr, c_scr, s_scr = refs[0:3], refs[3:6], refs[6:9], refs[9], refs[10], refs[11]
        hv = h_ref[...]
        rstd = lax.rsqrt(jnp.mean(hv * hv, axis=-1, keepdims=True) + NORM_EPS)
        u = hv * rstd * g_ref[...]
        for j in range(D_MODEL // LANES):
            u_scr[j] = u[:, j * LANES:(j + 1) * LANES]
        bc, bs, oc, osn = bc_ref[0], bs_ref[0], oc_ref[...], os_ref[...]
        c_scr[...] = bc * oc - bs * osn
        s_scr[...] = (bs * oc + bc * osn) * sign_ref[...]
        for d, u_ref, c_ref, s_ref in zip(dils, u_refs, c_refs, s_refs):
            if d == 1:
                u_ref[...] = u.astype(BF16)
                c_ref[...] = c_scr[...]
                s_ref[...] = s_scr[...]
                continue
            for r in range(d):
                rows = _class_rows(r, d, tm)
                for j in range(D_MODEL // LANES):
                    u_ref[r, :, j * LANES:(j + 1) * LANES] = u_scr.at[j][rows, :].astype(BF16)
                c_ref[r] = c_scr[rows, :]
                s_ref[r] = s_scr[rows, :]

    row = pl.BlockSpec((tm, D_MODEL), lambda i: (i, 0))
    base = pl.BlockSpec((1, 1, ATTN_DIM), lambda i: (i, 0, 0))
    off = pl.BlockSpec((tm, ATTN_DIM), lambda i: (0, 0))
    res = pl.pallas_call(
        body,
        out_shape=([_residue_shape(T, d, D_MODEL, BF16) for d in dils]
                   + [_residue_shape(T, d, ATTN_DIM, F32) for d in dils] * 2),
        grid=(T // tm,),
        in_specs=[row, pl.BlockSpec((1, D_MODEL), lambda i: (0, 0)), base, base, off, off,
                  pl.BlockSpec((1, ATTN_DIM), lambda i: (0, 0))],
        out_specs=([_residue_spec(d, tm, D_MODEL) for d in dils] + [_residue_spec(d, tm, ATTN_DIM) for d in dils] * 2),
        scratch_shapes=[pltpu.VMEM((D_MODEL // LANES, tm, LANES), F32), pltpu.VMEM((tm, ATTN_DIM), F32),
                        pltpu.VMEM((tm, ATTN_DIM), F32)],
        compiler_params=_params("parallel"), name=name)(h, gain, base_cos, base_sin, off_cos, off_sin, sign)
    flat = [r.reshape(T, r.shape[-1]) for r in res]
    return flat[0:3], flat[3:6], flat[6:9]


def _attn_merge_fwd(outs, lses, name):
    T = outs[0].shape[0]
    W = ATTN_GROUP_WIDTH
    tm = _pick_tile(T, PERM_TILE, 16 * max(ATTN_DILATIONS))
    dils = ATTN_DILATIONS

    def body(*refs):
        o_refs, l_refs, oc_ref, lse_refs = refs[0:3], refs[3:6], refs[6], refs[7:10]
        o_scr, l_scr, t_scr = refs[10:13]
        nh = ATTN_GROUP_HEADS
        for g, d in enumerate(dils):
            for j in range(nh):
                lanes = slice(j * LANES, (j + 1) * LANES)
                if d == 1:
                    o_scr[g * nh + j] = o_refs[g][:, lanes].astype(F32)
                    l_scr[g * nh + j] = l_refs[g][:, lanes]
                    continue
                for r in range(d):
                    rows = _class_rows(r, d, tm)
                    o_scr.at[g * nh + j][rows, :] = o_refs[g][r, :, lanes].astype(F32)
                    l_scr.at[g * nh + j][rows, :] = l_refs[g][r, :, lanes]
        for j in range(nh):
            lanes = slice(j * LANES, (j + 1) * LANES)
            ls = [l_scr[g * nh + j] for g in range(3)]
            m = jnp.maximum(jnp.maximum(ls[0], ls[1]), ls[2])
            tot = m + jnp.log(jnp.exp(ls[0] - m) + jnp.exp(ls[1] - m) + jnp.exp(ls[2] - m))
            t_scr[j] = tot
            for g, d in enumerate(dils):
                oc_ref[:, g * W + j * LANES:g * W + (j + 1) * LANES] = (
                    o_scr[g * nh + j] * jnp.exp(ls[g] - tot)).astype(BF16)
                if d == 1:
                    lse_refs[g][:, lanes] = tot
                    continue
                for r in range(d):
                    lse_refs[g][r, :, lanes] = t_scr.at[j][_class_rows(r, d, tm), :]

    in_blk = [_residue_spec(d, tm, W) for d in dils]
    n_blk = 3 * ATTN_GROUP_HEADS
    res = pl.pallas_call(
        body, out_shape=[jax.ShapeDtypeStruct((T, 3 * W), BF16)] + [_residue_shape(T, d, W, F32) for d in dils],
        grid=(T // tm,), in_specs=in_blk * 2,
        out_specs=[pl.BlockSpec((tm, 3 * W), lambda i: (i, 0))] + in_blk,
        scratch_shapes=[pltpu.VMEM((n_blk, tm, LANES), F32), pltpu.VMEM((n_blk, tm, LANES), F32),
                        pltpu.VMEM((ATTN_GROUP_HEADS, tm, LANES), F32)],
        compiler_params=_params("parallel"), name=name)(
            *[_residue_view(o, d) for o, d in zip(outs, dils)], *[_residue_view(l, d) for l, d in zip(lses, dils)])
    return res[0], [r.reshape(T, W) for r in res[1:]]


def _attn_merge_bwd(d_oc, oc, name):
    T = d_oc.shape[0]
    W = ATTN_GROUP_WIDTH
    tm = _pick_tile(T, PERM_TILE, 16 * max(ATTN_DILATIONS))
    dils = ATTN_DILATIONS

    def body(d_ref, o_ref, *refs):
        delta_refs, db_refs, dl_scr, d_scr = refs[0:3], refs[3:6], refs[6], refs[7]
        nh = ATTN_GROUP_HEADS
        for j in range(nh):
            tot = jnp.zeros((tm, 1), F32)
            for g in range(3):
                cols = slice(g * W + j * LANES, g * W + (j + 1) * LANES)
                d_blk = d_ref[:, cols]
                d_scr[g * nh + j] = d_blk
                tot = tot + jnp.sum(d_blk * o_ref[:, cols].astype(F32), axis=-1, keepdims=True)
            dl_scr[j] = jnp.broadcast_to(tot, (tm, LANES))
        for g, d in enumerate(dils):
            for j in range(nh):
                lanes = slice(j * LANES, (j + 1) * LANES)
                if d == 1:
                    delta_refs[g][:, lanes] = dl_scr[j]
                    db_refs[g][:, lanes] = d_scr[g * nh + j].astype(BF16)
                    continue
                for r in range(d):
                    rows = _class_rows(r, d, tm)
                    delta_refs[g][r, :, lanes] = dl_scr.at[j][rows, :]
                    db_refs[g][r, :, lanes] = d_scr.at[g * nh + j][rows, :].astype(BF16)

    wide = pl.BlockSpec((tm, 3 * W), lambda i: (i, 0))
    out_blk = [_residue_spec(d, tm, W) for d in dils]
    res = pl.pallas_call(
        body, out_shape=[_residue_shape(T, d, W, F32) for d in dils] + [_residue_shape(T, d, W, BF16) for d in dils],
        grid=(T // tm,), in_specs=[wide, wide], out_specs=out_blk * 2,
        scratch_shapes=[pltpu.VMEM((ATTN_GROUP_HEADS, tm, LANES), F32),
                        pltpu.VMEM((3 * ATTN_GROUP_HEADS, tm, LANES), F32)],
        compiler_params=_params("parallel"), name=name)(d_oc, oc)
    flat = [r.reshape(T, W) for r in res]
    return flat[0:3], flat[3:6]


def _rope_parts(T, tile):
    inv_freq = 1.0 / (ROPE_THETA ** (jnp.arange(0, ATTN_DIM, 2, dtype=F32) / ATTN_DIM))
    inv_freq = jnp.concatenate([inv_freq, inv_freq])[None, :]
    base = (jnp.arange(T // tile, dtype=F32) * tile)[:, None] * inv_freq
    off = jnp.arange(tile, dtype=F32)[:, None] * inv_freq
    sign = jnp.concatenate([-jnp.ones((1, ATTN_DIM // 2), F32), jnp.ones((1, ATTN_DIM // 2), F32)], axis=1)
    return (jnp.cos(base)[:, None, :], jnp.sin(base)[:, None, :]), (jnp.cos(off), jnp.sin(off)), sign


WEIGHT_GROUPS = {"hgrn": ("hgrn_in", "hgrn_out"), "ffn0": ("ffn_in0", "ffn_down0"),
                 "attn": ("qkv", "attn_out"), "ffn1": ("ffn_in1", "ffn_down1")}


def _local_step(x, target, norm_mix, norm_ffn, lb, out_gain, final_gain, fetch, publish):
    g_mix = [norm_mix[0:1], norm_mix[1:2]]
    g_ffn = [norm_ffn[0:1], norm_ffn[1:2]]
    w = {}

    def whole(name):
        return [(w[name], w[name].shape[0], 0)]

    def qkv_parts(g):
        return [(w["qkv"], ATTN_GROUP_WIDTH, 3 * j + g) for j in range(3)]

    def ffn_fwd(h, layer, head=None):
        w.update(fetch(f"ffn{layer}"))
        n, gate, up, a = _ffn_in(h, g_ffn[layer], w[f"ffn_in{layer}"], f"ffn{layer}_in")
        out = _mm_nn([a], [whole(f"ffn_down{layer}")], h, name=f"ffn{layer}_down", head=head)
        return out, (n, gate, up, a)

    def ffn_bwd(h, saved, dh, dhb, layer):
        n, gate, up, a = saved
        w_in = w[f"ffn_in{layer}"]
        dgate, dup = _ffn_down_dx(dhb, w[f"ffn_down{layer}"], gate, up, f"ffn{layer}_down_dx")
        grad_in = _mm_tn(dgate, n, name=f"ffn{layer}_in_dw_gate", rows=2 * D_FF)
        grad_in = _mm_tn(dup, n, name=f"ffn{layer}_in_dw_up", into=grad_in, row_tile=D_FF // GRAD_TILE, rows=2 * D_FF)
        grads = {f"ffn_down{layer}": _mm_tn(a, dhb, name=f"ffn{layer}_down_dw"), f"ffn_in{layer}": grad_in}
        publish(f"ffn{layer}", grads)
        return _mm_nn([dgate, dup], [[(w_in, D_FF, 0)], [(w_in, D_FF, 1)]], dh, name=f"ffn{layer}_in_dx",
                      norm=(h, g_ffn[layer]))

    u0 = _rms_fwd(x, g_mix[0], "hgrn_norm")
    w.update(fetch("hgrn"))
    proj = _mm_nt(u0, whole("hgrn_in"), out_dtype=F32, name="hgrn_in")
    og, o_pre, states = _hgrn_fwd(proj, lb, out_gain, "hgrn_fwd")
    h1 = _mm_nn([og], [whole("hgrn_out")], x, name="hgrn_out")
    h2, ffn0 = ffn_fwd(h1, 0)

    u1_g, cos_g, sin_g = _attn_norm(h2, g_mix[1], "attn_norm")
    w.update(fetch("attn"))
    qkv_g, outs, lses = [], [], []
    for g, d in enumerate(ATTN_DILATIONS):
        qkv_g.append(_mm_nt(u1_g[g], qkv_parts(g), out_dtype=BF16, name=f"attn_qkv{g}",
                            rope=(cos_g[g], sin_g[g], 2)))
        o_g, lse_g = _attn_fwd(qkv_g[g], d, f"attn_fwd{g}")
        outs.append(o_g)
        lses.append(lse_g)
    oc, lse_all = _attn_merge_fwd(outs, lses, "attn_merge")
    h3 = _mm_nn([oc], [whole("attn_out")], h2, name="attn_out")
    (dh4, dh4b, d_final, loss_part), ffn1 = ffn_fwd(h3, 1, head=(target, final_gain))
    dh3, dh3b, d_ffn1 = ffn_bwd(h3, ffn1, dh4, dh4b, 1)

    d_oc = _mm_nt(dh3b, whole("attn_out"), out_dtype=F32, name="attn_out_dx")
    grad_attn_out = _mm_tn(oc, dh3b, name="attn_out_dw")
    delta, d_ocb = _attn_merge_bwd(d_oc, oc, "attn_merge_bwd")
    du1, qkv_pieces = [], []
    for g, d in enumerate(ATTN_DILATIONS):
        dqkv = _attn_bwd(qkv_g[g], d_ocb[g], lse_all[g], delta[g], cos_g[g], sin_g[g], d, f"attn_bwd{g}")
        qkv_pieces.append(_mm_tn(dqkv, u1_g[g], name=f"attn_qkv_dw{g}"))
        du1.append(_mm_nn([dqkv], [qkv_parts(g)], None, name=f"attn_qkv_dx{g}"))
    grad_qkv = jnp.stack([p.reshape(3, ATTN_GROUP_WIDTH, D_MODEL) for p in qkv_pieces], axis=1).reshape(
        3 * ATTN_WIDTH, D_MODEL)
    publish("attn", {"qkv": grad_qkv, "attn_out": grad_attn_out})
    dh2, dh2b, d_mix1 = _rms_bwd(h2, g_mix[1], du1, dh3, "attn_norm_bwd", ATTN_DILATIONS)

    dh1, dh1b, d_ffn0 = ffn_bwd(h1, ffn0, dh2, dh2b, 0)

    d_og = _mm_nt(dh1b, whole("hgrn_out"), out_dtype=F32, name="hgrn_out_dx")
    grad_hgrn_out = _mm_tn(og, dh1b, name="hgrn_out_dw")
    dproj, d_lb, d_out_gain = _hgrn_bwd(proj, o_pre, d_og, states, lb, out_gain, "hgrn_bwd")
    publish("hgrn", {"hgrn_in": _mm_tn(dproj, u0, name="hgrn_in_dw"), "hgrn_out": grad_hgrn_out})
    dx, _, d_mix0 = _mm_nn([dproj], [whole("hgrn_in")], dh1, name="hgrn_in_dx", norm=(x, g_mix[0]))

    small = dict(norm_mix0=d_mix0, norm_mix1=d_mix1, norm_ffn0=d_ffn0, norm_ffn1=d_ffn1, lb=d_lb,
                 out_gain=d_out_gain, final=d_final, loss=loss_part)
    return dx, small


WEIGHT_NAMES = ("hgrn_in", "hgrn_out", "qkv", "attn_out", "ffn_in0", "ffn_in1", "ffn_down0", "ffn_down1")
MESH_IDS = pl.DeviceIdType.MESH
HBM_SPEC = pl.BlockSpec(memory_space=pl.ANY)


N_PEERS = N_DEV - 1
PEER_OFFSETS = [(dx, dy, dc) for dx in (0, 1) for dy in (0, 1) for dc in (0, 1)][1:]


def _mesh_place():
    x, y, c = lax.axis_index("x"), lax.axis_index("y"), lax.axis_index("c")
    peers = []
    for dx, dy, dc in PEER_OFFSETS:
        px, py, pc = (1 - x if dx else x), (1 - y if dy else y), (1 - c if dc else c)
        peers.append(((px, py, pc), 4 * px + 2 * py + pc))
    return 4 * x + 2 * y + c, peers


def _gather_over_two_levels(src_refs, land_refs, send_sems, recv_sems):
    n = len(src_refs)
    x, y, c = lax.axis_index("x"), lax.axis_index("y"), lax.axis_index("c")
    me, sibling = (x, y, c), (x, y, 1 - c)
    chips = [(1 - x, y), (x, 1 - y), (1 - x, 1 - y)]

    def block(w, px, py, pc):
        return land_refs[w].at[4 * px + 2 * py + pc]

    def copy(w, k, owner, to, src=None):
        return pltpu.make_async_remote_copy(
            src_ref=block(w, *owner) if src is None else src, dst_ref=block(w, *owner),
            send_sem=send_sems.at[w * N_PEERS + k], recv_sem=recv_sems.at[w * N_PEERS + k],
            device_id=to, device_id_type=MESH_IDS)

    sent = []
    for w in range(n):
        sent.append(copy(w, 0, me, sibling, src=src_refs[w]))
        sent += [copy(w, 1 + j, me, (*chip, c), src=src_refs[w]) for j, chip in enumerate(chips)]
    for cp in sent:
        cp.start()
    for w in range(n):
        for j, chip in enumerate(chips):
            copy(w, 1 + j, (*chip, c), me).wait_recv()
            passed = copy(w, 4 + j, (*chip, c), sibling)
            passed.start()
            sent.append(passed)
    for w in range(n):
        copy(w, 0, sibling, me).wait_recv()
        for j, chip in enumerate(chips):
            copy(w, 4 + j, (*chip, 1 - c), me).wait_recv()
    for cp in sent:
        cp.wait_send()


def _exchange_launch(srcs, scatter, collective_id, name):
    n = len(srcs)
    src_refs = [jax.new_ref(s, memory_space=pltpu.MemorySpace.HBM) for s in srcs]
    land_refs = [jax.empty_ref(jax.ShapeDtypeStruct(s.shape if scatter else (N_DEV,) + s.shape, s.dtype),
                               memory_space=pltpu.MemorySpace.HBM) for s in srcs]

    @pl.kernel(mesh=plsc.ScalarSubcoreMesh(axis_name="sequencer", num_cores=1), name=name,
               scratch_types=(pltpu.SemaphoreType.DMA((n * N_PEERS,)), pltpu.SemaphoreType.DMA((n * N_PEERS,)),
                              pltpu.SemaphoreType.DMA((n,))),
               compiler_params=pltpu.CompilerParams(collective_id=collective_id))
    def launch(send_sems, recv_sems, local_sems):
        me, peers = _mesh_place()
        barrier = pltpu.get_barrier_semaphore()
        for peer, _ in peers:
            pl.semaphore_signal(barrier, inc=1, device_id=peer, device_id_type=MESH_IDS)
        pl.semaphore_wait(barrier, N_PEERS)
        own = [pltpu.make_async_copy(src_refs[w].at[me] if scatter else src_refs[w], land_refs[w].at[me],
                                     local_sems.at[w]) for w in range(n)]
        for cp in own:
            cp.start()
        if scatter:
            copies = [pltpu.make_async_remote_copy(
                src_ref=src_refs[w].at[pid], dst_ref=land_refs[w].at[me],
                send_sem=send_sems.at[w * N_PEERS + k], recv_sem=recv_sems.at[w * N_PEERS + k],
                device_id=peer, device_id_type=MESH_IDS) for w in range(n) for k, (peer, pid) in enumerate(peers)]
            for cp in copies:
                cp.start()
            for cp in copies:
                cp.wait()
        else:
            _gather_over_two_levels(src_refs, land_refs, send_sems, recv_sems)
        for cp in own:
            cp.wait()

    launch()
    return land_refs


def _gather_small(block, name):
    def body(in_ref, out_ref, send_sems, recv_sems, local_sem):
        me, peers = _mesh_place()
        own = pltpu.make_async_copy(in_ref, out_ref.at[me], local_sem)
        own.start()
        sends = [pltpu.make_async_remote_copy(
            src_ref=in_ref, dst_ref=out_ref.at[me], send_sem=send_sems.at[k], recv_sem=recv_sems.at[k],
            device_id=peer, device_id_type=MESH_IDS) for k, (peer, _) in enumerate(peers)]
        for cp in sends:
            cp.start()
        for cp in sends:
            cp.wait_recv()
        for cp in sends:
            cp.wait_send()
        own.wait()

    return pl.pallas_call(
        body, out_shape=jax.ShapeDtypeStruct((N_DEV,) + block.shape, block.dtype),
        in_specs=[HBM_SPEC], out_specs=HBM_SPEC,
        scratch_shapes=[pltpu.SemaphoreType.DMA((N_PEERS,)), pltpu.SemaphoreType.DMA((N_PEERS,)),
                        pltpu.SemaphoreType.DMA],
        name=name)(block)


def _sum_blocks(recv, name):
    rows = recv.shape[1]
    tr = _pick_tile(rows, 256, 16)

    def body(r_ref, g_ref):
        acc = r_ref[0].astype(F32)
        for j in range(1, N_DEV):
            acc = acc + r_ref[j].astype(F32)
        g_ref[...] = acc

    return pl.pallas_call(
        body, out_shape=jax.ShapeDtypeStruct((rows, D_MODEL), F32), grid=(rows // tr,),
        in_specs=[pl.BlockSpec((N_DEV, tr, D_MODEL), lambda i: (0, i, 0))],
        out_specs=pl.BlockSpec((tr, D_MODEL), lambda i: (i, 0)),
        compiler_params=_params("parallel"), name=name)(recv)


def _adamw_math(w, g, m, v):
    m_new = ADAM_B1 * m + (1.0 - ADAM_B1) * g
    v_new = ADAM_B2 * v + (1.0 - ADAM_B2) * (g * g)
    m_hat = m_new / (1.0 - ADAM_B1 ** ADAM_STEP)
    v_hat = v_new / (1.0 - ADAM_B2 ** ADAM_STEP)
    delta = -ADAM_LR * (m_hat / (jnp.sqrt(v_hat) + ADAM_EPS) + ADAM_WD * w)
    return delta, m_new, v_new


def _adamw(w, g, m, v, name):
    rows, cols = w.shape
    tr = _pick_tile(rows, 256, 8)

    def body(w_ref, g_ref, m_ref, v_ref, d_ref, mo_ref, vo_ref):
        d_ref[...], mo_ref[...], vo_ref[...] = _adamw_math(w_ref[...], g_ref[...], m_ref[...], v_ref[...])

    blk = pl.BlockSpec((tr, cols), lambda i: (i, 0))
    return pl.pallas_call(
        body, out_shape=(jax.ShapeDtypeStruct((rows, cols), F32),) * 3, grid=(rows // tr,),
        in_specs=[blk] * 4, out_specs=(blk,) * 3, compiler_params=_params("parallel"), name=name)(w, g, m, v)


ROW_MIX, ROW_FFN, ROW_LB, ROW_OUT_GAIN, ROW_FINAL = 0, 2, 4, 7, 8
PART_MIX, PART_FFN, PART_LB, PART_OUT_GAIN, PART_FINAL, PART_LOSS = 0, 2, 4, 5, 6, 7


def _small_update(parts_all, w, m, v, name):
    def body(p_ref, w_ref, m_ref, v_ref, g_ref, d_ref, mo_ref, vo_ref, loss_ref):
        def total(row, n=1):
            tot = p_ref[0, row:row + n, :]
            for j in range(1, N_DEV):
                tot = tot + p_ref[j, row:row + n, :]
            return tot

        logits = [w_ref[ROW_LB + i:ROW_LB + i + 1, :] for i in range(3)]
        mx = jnp.maximum(jnp.maximum(logits[0], logits[1]), logits[2])
        ex = [jnp.exp(l - mx) for l in logits]
        den = ex[0] + ex[1] + ex[2]
        prob = [e / den for e in ex]
        d_lb = total(PART_LB)
        g_ref[...] = jnp.zeros_like(g_ref)
        g_ref[ROW_MIX:ROW_MIX + 2, :] = total(PART_MIX, 2)
        g_ref[ROW_FFN:ROW_FFN + 2, :] = total(PART_FFN, 2)
        for i in range(3):
            g_ref[ROW_LB + i:ROW_LB + i + 1, :] = prob[i] * ((d_lb if i == 0 else 0.0) - prob[0] * d_lb)
        g_ref[ROW_OUT_GAIN:ROW_OUT_GAIN + 1, :] = total(PART_OUT_GAIN)
        g_ref[ROW_FINAL:ROW_FINAL + 1, :] = total(PART_FINAL)
        d_ref[...], mo_ref[...], vo_ref[...] = _adamw_math(w_ref[...], g_ref[...], m_ref[...], v_ref[...])
        loss_ref[...] = jnp.sum(total(PART_LOSS), axis=-1, keepdims=True)

    packed = jax.ShapeDtypeStruct((16, D_MODEL), F32)
    return pl.pallas_call(
        body, out_shape=(packed, packed, packed, packed, jax.ShapeDtypeStruct((1, 1), F32)),
        compiler_params=pltpu.CompilerParams(vmem_limit_bytes=VMEM_LIMIT), name=name)(parts_all, w, m, v)


def _pack_small(norm_mix, norm_ffn, lb_logits, out_gain, final):
    pad = jnp.zeros((1, D_MODEL - HGRN_DIM), F32)
    return jnp.concatenate([norm_mix, norm_ffn, lb_logits, jnp.concatenate([out_gain, pad], axis=1),
                            final.reshape(1, D_MODEL), jnp.zeros((16 - ROW_FINAL - 1, D_MODEL), F32)], axis=0)


def _unpack_small(p):
    return (p[ROW_MIX:ROW_MIX + 2], p[ROW_FFN:ROW_FFN + 2], p[ROW_LB:ROW_LB + 3],
            p[ROW_OUT_GAIN:ROW_OUT_GAIN + 1, :HGRN_DIM], p[ROW_FINAL])


def _lower_bound(lb_logits, name):
    def body(l_ref, o_ref):
        logits = [l_ref[i:i + 1, :] for i in range(3)]
        mx = jnp.maximum(jnp.maximum(logits[0], logits[1]), logits[2])
        ex = [jnp.exp(l - mx) for l in logits]
        o_ref[...] = ex[0] / (ex[0] + ex[1] + ex[2])

    return pl.pallas_call(body, out_shape=jax.ShapeDtypeStruct((1, D_MODEL), F32), name=name)(lb_logits)


def kernel(x, norm_mix, norm_ffn, hgrn_w_in, hgrn_lb_logits, hgrn_out_norm, hgrn_w_out, attn_w_qkv, attn_w_out, ffn_w_in, ffn_w_down, final_norm, loss_target, m_norm_mix, m_norm_ffn, m_hgrn_w_in, m_hgrn_lb_logits, m_hgrn_out_norm, m_hgrn_w_out, m_attn_w_qkv, m_attn_w_out, m_ffn_w_in, m_ffn_w_down, m_final_norm, v_norm_mix, v_norm_ffn, v_hgrn_w_in, v_hgrn_lb_logits, v_hgrn_out_norm, v_hgrn_w_out, v_attn_w_qkv, v_attn_w_out, v_ffn_w_in, v_ffn_w_down, v_final_norm):
    col_sharded = {"hgrn_in": hgrn_w_in[0], "qkv": attn_w_qkv[0], "ffn_in0": ffn_w_in[0], "ffn_in1": ffn_w_in[1]}
    row_sharded = {"hgrn_out": hgrn_w_out[0], "attn_out": attn_w_out[0], "ffn_down0": ffn_w_down[0],
                   "ffn_down1": ffn_w_down[1]}
    gathering = {}
    for gi, (group, names) in enumerate(WEIGHT_GROUPS.items()):
        shards = [(col_sharded[n].T if n in col_sharded else row_sharded[n]).astype(BF16) for n in names]
        gathering[group] = _exchange_launch(shards, False, 1 + gi, f"weights_gather_{group}")

    def fetch(group):
        return {n: land[...].reshape(-1, D_MODEL) for n, land in zip(WEIGHT_GROUPS[group], gathering[group])}

    in_flight = {}

    def publish(group, grads):
        names = WEIGHT_GROUPS[group]
        parts = [grads[n].reshape(N_DEV, -1, D_MODEL) for n in names]
        in_flight[group] = _exchange_launch(parts, True, 1 + len(WEIGHT_GROUPS) + list(WEIGHT_GROUPS).index(group),
                                            f"grads_send_{group}")

    lb = _lower_bound(hgrn_lb_logits, "hgrn_lower_bound")
    grad_x, small = _local_step(x[0], loss_target[0], norm_mix, norm_ffn, lb, hgrn_out_norm,
                                final_norm.reshape(1, D_MODEL), fetch, publish)

    pad = jnp.zeros((1, D_MODEL - HGRN_DIM), F32)
    small_part = jnp.concatenate(
        [small["norm_mix0"], small["norm_mix1"], small["norm_ffn0"], small["norm_ffn1"], small["lb"],
         jnp.concatenate([small["out_gain"], pad], axis=1), small["final"], small["loss"]], axis=0)
    small_all = _gather_small(small_part, "small_grads_gather")
    received = {}
    for group in ("ffn1", "attn", "ffn0", "hgrn"):
        received.update(zip(WEIGHT_GROUPS[group], [land[...] for land in in_flight[group]]))

    masters = {"hgrn_in": (hgrn_w_in[0], m_hgrn_w_in[0], v_hgrn_w_in[0]),
               "hgrn_out": (hgrn_w_out[0], m_hgrn_w_out[0], v_hgrn_w_out[0]),
               "qkv": (attn_w_qkv[0], m_attn_w_qkv[0], v_attn_w_qkv[0]),
               "attn_out": (attn_w_out[0], m_attn_w_out[0], v_attn_w_out[0]),
               "ffn_in0": (ffn_w_in[0], m_ffn_w_in[0], v_ffn_w_in[0]),
               "ffn_in1": (ffn_w_in[1], m_ffn_w_in[1], v_ffn_w_in[1]),
               "ffn_down0": (ffn_w_down[0], m_ffn_w_down[0], v_ffn_w_down[0]),
               "ffn_down1": (ffn_w_down[1], m_ffn_w_down[1], v_ffn_w_down[1])}
    res = {}
    for n in WEIGHT_NAMES:
        g = _sum_blocks(received[n], f"{n}_grad_sum")
        if n in col_sharded:
            g = g.T
        wv, mv, vv = masters[n]
        res[n] = (g,) + tuple(_adamw(wv, g, mv, vv, f"{n}_adamw"))

    def single(n):
        return [t[None] for t in res[n]]

    def pair(n):
        return [jnp.stack([a, b]) for a, b in zip(res[n + "0"], res[n + "1"])]

    big = dict(hgrn_w_in=single("hgrn_in"), hgrn_w_out=single("hgrn_out"), attn_w_qkv=single("qkv"),
               attn_w_out=single("attn_out"), ffn_w_in=pair("ffn_in"), ffn_w_down=pair("ffn_down"))

    w_small = _pack_small(norm_mix, norm_ffn, hgrn_lb_logits, hgrn_out_norm, final_norm)
    m_small = _pack_small(m_norm_mix, m_norm_ffn, m_hgrn_lb_logits, m_hgrn_out_norm, m_final_norm)
    v_small = _pack_small(v_norm_mix, v_norm_ffn, v_hgrn_lb_logits, v_hgrn_out_norm, v_final_norm)
    g_s, d_s, m_s, v_s, loss = _small_update(small_all, w_small, m_small, v_small, "small_update")
    small_out = [_unpack_small(t) for t in (g_s, d_s, m_s, v_s)]

    def group(i):
        s = small_out[i]
        return (s[0], s[1], big["hgrn_w_in"][i], s[2], s[3], big["hgrn_w_out"][i], big["attn_w_qkv"][i],
                big["attn_w_out"][i], big["ffn_w_in"][i], big["ffn_w_down"][i], s[4])

    return (loss.reshape(()), grad_x[None], *group(0), *group(1), *group(2), *group(3))
```

```python
import functools

import jax
import jax.numpy as jnp
from jax import lax
from jax.experimental import pallas as pl
from jax.experimental.pallas import tpu as pltpu
from jax.experimental.pallas import tpu_sc as plsc

F32 = jnp.float32
BF16 = jnp.bfloat16

D_MODEL = 1024
N_DEV = 8
NORM_EPS = 1e-6

HGRN_HEADS = 8
HGRN_DIM = 128
HGRN_CHUNK = 64
HGRN_STEP_CHUNKS = 2
HGRN_EXP_CLAMP = 60.0

ATTN_DIM = 128
ATTN_BLOCK = 128
ATTN_GROUP_HEADS = 4
ATTN_GROUP_WIDTH = ATTN_GROUP_HEADS * ATTN_DIM
ATTN_DILATIONS = (1, 4, 16)
ATTN_WIDTH = 3 * ATTN_GROUP_WIDTH
ROPE_THETA = 10000.0
NEG_BIG = -1e30

D_FF = 2816

ADAM_LR = 0.001
ADAM_B1 = 0.9
ADAM_B2 = 0.999
ADAM_EPS = 1e-08
ADAM_WD = 0.01
ADAM_STEP = 10

VMEM_LIMIT = 48 * 1024 * 1024

NT = (((1,), (1,)), ((), ()))
NN = (((1,), (0,)), ((), ()))
TN = (((0,), (0,)), ((), ()))


def _dot(a, b, dims):
    return lax.dot_general(a, b, dims, preferred_element_type=F32)


def _params(*sem):
    return pltpu.CompilerParams(dimension_semantics=sem, vmem_limit_bytes=VMEM_LIMIT)


def _pick_tile(n, cap, mult):
    best = None
    for t in range(mult, min(n, cap) + 1, mult):
        if n % t == 0:
            best = t
    assert best is not None, (n, cap, mult)
    return best


def _sigmoid(x):
    return 0.5 * jnp.tanh(0.5 * x) + 0.5


ROW_TILE = 512
COL_CHUNK = 512
GRAD_TILE = 256


def _whole(shape, index_map):
    return pl.BlockSpec(shape, index_map, pipeline_mode=pl.Buffered(1))


def _part_specs(parts, n_cols):
    return [_whole((rows, n_cols), functools.partial(lambda i, b: (b, 0), b=blk)) for _, rows, blk in parts]


def _mm_nt(a, w_parts, *, out_dtype, name, rope=None):
    M, K = a.shape
    tm = _pick_tile(M, ROW_TILE, 16)
    widths = [rows for _, rows, _ in w_parts]
    n_parts = len(w_parts)

    def body(*refs):
        a_ref, w_refs, o_ref = refs[0], refs[1:1 + n_parts], refs[-1]
        av = a_ref[...]
        off = 0
        for p, w_ref in enumerate(w_refs):
            for c0 in range(0, widths[p], COL_CHUNK):
                cw = min(COL_CHUNK, widths[p] - c0)
                acc = _dot(av, w_ref[c0:c0 + cw, :], NT)
                if rope is not None and p < rope[2]:
                    cos, sin = refs[1 + n_parts][...], refs[2 + n_parts][...]
                    for h0 in range(0, cw, ATTN_DIM):
                        xh = acc[:, h0:h0 + ATTN_DIM]
                        rot = pltpu.roll(xh, ATTN_DIM // 2, 1)
                        o_ref[:, off + c0 + h0:off + c0 + h0 + ATTN_DIM] = (xh * cos + rot * sin).astype(out_dtype)
                else:
                    o_ref[:, off + c0:off + c0 + cw] = acc.astype(out_dtype)
            off += widths[p]

    in_specs = [pl.BlockSpec((tm, K), lambda i: (i, 0))] + _part_specs(w_parts, K)
    args = [a] + [w for w, _, _ in w_parts]
    if rope is not None:
        in_specs += [pl.BlockSpec((tm, ATTN_DIM), lambda i: (i, 0))] * 2
        args += [rope[0], rope[1]]
    return pl.pallas_call(
        body, out_shape=jax.ShapeDtypeStruct((M, sum(widths)), out_dtype), grid=(M // tm,),
        in_specs=in_specs, out_specs=pl.BlockSpec((tm, sum(widths)), lambda i: (i, 0)),
        compiler_params=_params("parallel"), name=name)(*args)


def _mm_nn(a_list, w_parts_list, resid, *, name, norm=None, head=None):
    M = a_list[0].shape[0]
    tm = _pick_tile(M, ROW_TILE, 16)
    n_a = len(a_list)
    flat_parts = [p for parts in w_parts_list for p in parts]
    extra = norm if norm is not None else head
    n_in = n_a + len(flat_parts) + (1 if resid is not None else 0) + (2 if extra is not None else 0)

    def body(*refs):
        a_refs, w_refs = refs[:n_a], refs[n_a:n_a + len(flat_parts)]

        def product(rows):
            acc = None
            wi = 0
            for a_ref, parts in zip(a_refs, w_parts_list):
                off = 0
                for _, k, _ in parts:
                    term = _dot(a_ref[rows, off:off + k], w_refs[wi][...], NN)
                    acc = term if acc is None else acc + term
                    off += k
                    wi += 1
            return acc

        if extra is None:
            acc = product(slice(None))
            if resid is not None:
                acc = acc + refs[n_in - 1][...]
            refs[n_in][...] = acc
            return

        @pl.when(pl.program_id(0) == 0)
        def _():
            for acc_ref in refs[n_in + 2:]:
                acc_ref[...] = jnp.zeros_like(acc_ref)

        for r0 in range(0, tm, tm // 2):
            rows = slice(r0, r0 + tm // 2)
            acc = product(rows)
            if head is not None:
                _loss_head_math(acc + refs[n_in - 3][rows, :], rows, refs[n_in - 2], refs[n_in - 1],
                                *refs[n_in:n_in + 4])
                continue
            dres_ref, x_ref, g_ref = refs[n_in - 3:n_in]
            dx_ref, dxb_ref, dg_ref = refs[n_in:n_in + 3]
            xv = x_ref[rows, :]
            rstd = lax.rsqrt(jnp.mean(xv * xv, axis=-1, keepdims=True) + NORM_EPS)
            n = xv * rstd
            dg_ref[...] += jnp.sum(acc * n, axis=0, keepdims=True)
            dn = acc * g_ref[...]
            dx = dres_ref[rows, :] + rstd * (dn - n * jnp.mean(dn * n, axis=-1, keepdims=True))
            dx_ref[rows, :] = dx
            dxb_ref[rows, :] = dx.astype(BF16)

    row = pl.BlockSpec((tm, D_MODEL), lambda i: (i, 0))
    vec = pl.BlockSpec((1, D_MODEL), lambda i: (0, 0))
    in_specs = [pl.BlockSpec((tm, a.shape[1]), lambda i: (i, 0)) for a in a_list] + _part_specs(flat_parts, D_MODEL)
    args = list(a_list) + [w for w, _, _ in flat_parts]
    if resid is not None:
        in_specs.append(row)
        args.append(resid)
    if extra is None:
        return pl.pallas_call(
            body, out_shape=jax.ShapeDtypeStruct((M, D_MODEL), F32), grid=(M // tm,),
            in_specs=in_specs, out_specs=row, compiler_params=_params("parallel"), name=name)(*args)
    assert resid is not None
    out_shape = [jax.ShapeDtypeStruct((M, D_MODEL), F32), jax.ShapeDtypeStruct((M, D_MODEL), BF16),
                 jax.ShapeDtypeStruct((1, D_MODEL), F32)]
    out_specs = [row, row, vec]
    if head is not None:
        out_shape.append(jax.ShapeDtypeStruct((1, D_MODEL), F32))
        out_specs.append(vec)
    return pl.pallas_call(
        body, out_shape=out_shape, grid=(M // tm,), in_specs=in_specs + [row, vec], out_specs=out_specs,
        compiler_params=_params("arbitrary"), name=name)(*args, extra[0], extra[1])


def _mm_tn(a, b, *, name, into=None, row_tile=0, rows=None):
    T, R = a.shape
    N = b.shape[1]
    tr = GRAD_TILE
    rows = R if rows is None else rows

    def body(a_ref, b_ref, *refs):
        refs[-1][...] = _dot(a_ref[...], b_ref[...], TN).astype(BF16)

    in_specs = [pl.BlockSpec((T, tr), lambda r: (0, r)), _whole((T, N), lambda r: (0, 0))]
    args = [a, b]
    if into is not None:
        in_specs.append(HBM_SPEC)
        args.append(into)
    return pl.pallas_call(
        body, out_shape=jax.ShapeDtypeStruct((rows, N), BF16), grid=(R // tr,),
        in_specs=in_specs, out_specs=pl.BlockSpec((tr, N), lambda r: (row_tile + r, 0)),
        input_output_aliases={} if into is None else {2: 0},
        compiler_params=_params("parallel"), name=name)(*args)


def _rms_fwd(x, gain, name):
    T = x.shape[0]
    tm = _pick_tile(T, 512, 16)

    def body(x_ref, g_ref, u_ref):
        xv = x_ref[...]
        rstd = lax.rsqrt(jnp.mean(xv * xv, axis=-1, keepdims=True) + NORM_EPS)
        u_ref[...] = (xv * rstd * g_ref[...]).astype(BF16)

    return pl.pallas_call(
        body, out_shape=jax.ShapeDtypeStruct((T, D_MODEL), BF16), grid=(T // tm,),
        in_specs=[pl.BlockSpec((tm, D_MODEL), lambda i: (i, 0)), pl.BlockSpec((1, D_MODEL), lambda i: (0, 0))],
        out_specs=pl.BlockSpec((tm, D_MODEL), lambda i: (i, 0)),
        compiler_params=_params("parallel"), name=name)(x, gain)


def _rms_bwd(x, gain, dus, dres, name, dilations=(1,)):
    T = x.shape[0]
    tm = _pick_tile(T, PERM_TILE, 16 * max(dilations))
    n_du = len(dus)

    def body(x_ref, g_ref, *refs):
        du_refs, dres_ref = refs[:n_du], refs[n_du]
        dx_ref, dxb_ref, dg_ref, du_scr = refs[n_du + 1:]

        @pl.when(pl.program_id(0) == 0)
        def _():
            dg_ref[...] = jnp.zeros_like(dg_ref)

        if tuple(dilations) == (1,):
            du = du_refs[0][...]
        else:
            for i, (d, du_ref) in enumerate(zip(dilations, du_refs)):
                for j in range(D_MODEL // LANES):
                    lanes = slice(j * LANES, (j + 1) * LANES)
                    if d == 1:
                        du_scr[j] = du_ref[:, lanes] if i == 0 else du_scr[j] + du_ref[:, lanes]
                        continue
                    blk = du_scr.at[j]
                    for r in range(d):
                        rows = _class_rows(r, d, tm)
                        blk[rows, :] = du_ref[r, :, lanes] if i == 0 else blk[rows, :] + du_ref[r, :, lanes]
            du = jnp.concatenate([du_scr[j] for j in range(D_MODEL // LANES)], axis=1)
        xv = x_ref[...]
        rstd = lax.rsqrt(jnp.mean(xv * xv, axis=-1, keepdims=True) + NORM_EPS)
        n = xv * rstd
        dg_ref[...] += jnp.sum(du * n, axis=0, keepdims=True)
        dn = du * g_ref[...]
        dx = dres_ref[...] + rstd * (dn - n * jnp.mean(dn * n, axis=-1, keepdims=True))
        dx_ref[...] = dx
        dxb_ref[...] = dx.astype(BF16)

    row = pl.BlockSpec((tm, D_MODEL), lambda i: (i, 0))
    vec = pl.BlockSpec((1, D_MODEL), lambda i: (0, 0))
    return pl.pallas_call(
        body,
        out_shape=(jax.ShapeDtypeStruct((T, D_MODEL), F32), jax.ShapeDtypeStruct((T, D_MODEL), BF16),
                   jax.ShapeDtypeStruct((1, D_MODEL), F32)),
        grid=(T // tm,), in_specs=[row, vec] + [_residue_spec(d, tm, D_MODEL) for d in dilations] + [row],
        out_specs=(row, row, vec), scratch_shapes=[pltpu.VMEM((D_MODEL // LANES, tm, LANES), F32)],
        compiler_params=_params("arbitrary"), name=name)(
            x, gain, *[_residue_view(du, d) for du, d in zip(dus, dilations)], dres)


def _loss_head_math(hv, rows, t_ref, g_ref, dh_ref, dhb_ref, dg_ref, loss_ref):
    inv_f = 1.0 / D_MODEL
    g = g_ref[...]
    rstd = lax.rsqrt(jnp.mean(hv * hv, axis=-1, keepdims=True) + NORM_EPS)
    n = hv * rstd
    err = n * g - t_ref[rows, :]
    loss_ref[...] += (0.5 * inv_f) * jnp.sum(err * err, axis=0, keepdims=True)
    dy = err * inv_f
    dg_ref[...] += jnp.sum(dy * n, axis=0, keepdims=True)
    dn = dy * g
    dh = rstd * (dn - n * jnp.mean(dn * n, axis=-1, keepdims=True))
    dh_ref[rows, :] = dh
    dhb_ref[rows, :] = dh.astype(BF16)


FFN_TILE = 256


def _ffn_in(h, gain, w_in, name):
    T = h.shape[0]
    tm = _pick_tile(T, ROW_TILE, 16)

    def body(h_ref, g_ref, w_ref, n_ref, gate_ref, up_ref, a_ref):
        hv = h_ref[...]
        rstd = lax.rsqrt(jnp.mean(hv * hv, axis=-1, keepdims=True) + NORM_EPS)
        n = (hv * rstd * g_ref[...]).astype(BF16)
        n_ref[...] = n
        for c0 in range(0, D_FF, FFN_TILE):
            cols = slice(c0, c0 + FFN_TILE)
            gate = _dot(n, w_ref[c0:c0 + FFN_TILE, :], NT)
            up = _dot(n, w_ref[D_FF + c0:D_FF + c0 + FFN_TILE, :], NT)
            gate_ref[:, cols] = gate.astype(BF16)
            up_ref[:, cols] = up.astype(BF16)
            a_ref[:, cols] = (gate * _sigmoid(gate) * up).astype(BF16)

    row = pl.BlockSpec((tm, D_MODEL), lambda i: (i, 0))
    wide = pl.BlockSpec((tm, D_FF), lambda i: (i, 0))
    wide_shape = jax.ShapeDtypeStruct((T, D_FF), BF16)
    return pl.pallas_call(
        body, out_shape=(jax.ShapeDtypeStruct((T, D_MODEL), BF16), wide_shape, wide_shape, wide_shape),
        grid=(T // tm,),
        in_specs=[row, pl.BlockSpec((1, D_MODEL), lambda i: (0, 0)), _whole((2 * D_FF, D_MODEL), lambda i: (0, 0))],
        out_specs=(row, wide, wide, wide), compiler_params=_params("parallel"), name=name)(h, gain, w_in)


def _ffn_down_dx(dhb, w_down, gate, up, name):
    T = dhb.shape[0]
    tm = _pick_tile(T, ROW_TILE, 16)

    def body(dh_ref, w_ref, gate_ref, up_ref, dgate_ref, dup_ref):
        dh = dh_ref[...]
        for c0 in range(0, D_FF, FFN_TILE):
            cols = slice(c0, c0 + FFN_TILE)
            da = _dot(dh, w_ref[c0:c0 + FFN_TILE, :], NT)
            gate = gate_ref[:, cols].astype(F32)
            sg = _sigmoid(gate)
            dgate_ref[:, cols] = (da * up_ref[:, cols].astype(F32) * (sg * (1.0 + gate * (1.0 - sg)))).astype(BF16)
            dup_ref[:, cols] = (da * gate * sg).astype(BF16)

    wide = pl.BlockSpec((tm, D_FF), lambda i: (i, 0))
    wide_shape = jax.ShapeDtypeStruct((T, D_FF), BF16)
    return pl.pallas_call(
        body, out_shape=(wide_shape, wide_shape), grid=(T // tm,),
        in_specs=[pl.BlockSpec((tm, D_MODEL), lambda i: (i, 0)), _whole((D_FF, D_MODEL), lambda i: (0, 0)), wide, wide],
        out_specs=(wide, wide), compiler_params=_params("parallel"), name=name)(dhb, w_down, gate, up)


def _tri(n, lower):
    r = lax.broadcasted_iota(jnp.int32, (n, n), 0)
    c = lax.broadcasted_iota(jnp.int32, (n, n), 1)
    return (c <= r) if lower else (c >= r)


def _running_sum(x, lower):
    tri = _tri(x.shape[0], lower).astype(BF16)
    hi = x.astype(BF16)
    rest = x - hi.astype(F32)
    mid = rest.astype(BF16)
    lo = (rest - mid.astype(F32)).astype(BF16)
    return _dot(tri, hi, NN) + _dot(tri, mid, NN) + _dot(tri, lo, NN)


def _hgrn_gates(q_raw, f_raw, lb):
    C = q_raw.shape[0]
    sig_f = _sigmoid(f_raw)
    forget = lb + (1.0 - lb) * sig_f
    key = 1.0 - forget
    log_f = jnp.log(forget)
    b = _running_sum(log_f, True)
    first_half = lax.broadcasted_iota(jnp.int32, log_f.shape, 0) < C // 2
    r = jnp.sum(jnp.where(first_half, log_f, 0.0), axis=0, keepdims=True)
    b_last = jnp.sum(log_f, axis=0, keepdims=True)
    e_a = jnp.exp(jnp.minimum(b - r, HGRN_EXP_CLAMP))
    e_b = jnp.exp(jnp.minimum(r - b, HGRN_EXP_CLAMP))
    e_q = jnp.exp(b)
    e_k = jnp.exp(b_last - b)
    sig_q = _sigmoid(q_raw)
    query = q_raw * sig_q
    return dict(sig_f=sig_f, forget=forget, sig_q=sig_q, e_a=e_a, e_b=e_b, e_q=e_q, e_k=e_k,
                e_last=jnp.exp(b_last), q_a=query * e_a, k_b=key * e_b, q_hat=query * e_q, k_til=key * e_k)


def _hgrn_fwd(proj, lb, gain, name):
    T = proj.shape[0]
    C = HGRN_CHUNK
    CPS = HGRN_STEP_CHUNKS
    H, HD = HGRN_HEADS, HGRN_DIM

    def body(q_ref, f_ref, i_ref, g_ref, lb_ref, gain_ref, og_ref, o_ref, st_ref, s_scr):
        @pl.when(pl.program_id(0) == 0)
        def _():
            s_scr[...] = jnp.zeros_like(s_scr)

        causal = _tri(C, True)
        gain_v = gain_ref[...]
        heads = [slice(h * HD, (h + 1) * HD) for h in range(H)]
        s_t = [s_scr[h] for h in range(H)]
        for cc in range(CPS):
            rows = slice(cc * C, (cc + 1) * C)
            for h in range(H):
                st_ref[cc, h] = s_t[h]
            gt = _hgrn_gates(q_ref[rows, :], f_ref[rows, :], lb_ref[...])
            q_a, k_b = gt["q_a"].astype(BF16), gt["k_b"].astype(BF16)
            q_hat, k_til = gt["q_hat"].astype(BF16), gt["k_til"].astype(BF16)
            v = i_ref[rows, :].astype(BF16)
            p = [jnp.where(causal, _dot(q_a[:, sl], k_b[:, sl], NT), 0.0).astype(BF16) for sl in heads]
            o = [_dot(p[h], v[:, sl], NN) + _dot(q_hat[:, sl], s_t[h].astype(BF16), NT)
                 for h, sl in enumerate(heads)]
            s_t = [gt["e_last"][:, sl] * s_t[h] + _dot(v[:, sl], k_til[:, sl], TN) for h, sl in enumerate(heads)]
            for h, sl in enumerate(heads):
                o_ref[rows, sl] = o[h]
                rstd = lax.rsqrt(jnp.mean(o[h] * o[h], axis=-1, keepdims=True) + NORM_EPS)
                g_raw = g_ref[rows, sl]
                og_ref[rows, sl] = (o[h] * rstd * gain_v * (g_raw * _sigmoid(g_raw))).astype(BF16)
        for h in range(H):
            s_scr[h] = s_t[h]

    col = lambda j: pl.BlockSpec((CPS * C, D_MODEL), lambda c: (c, j))
    row = pl.BlockSpec((CPS * C, D_MODEL), lambda c: (c, 0))
    return pl.pallas_call(
        body,
        out_shape=(jax.ShapeDtypeStruct((T, D_MODEL), BF16), jax.ShapeDtypeStruct((T, D_MODEL), F32),
                   jax.ShapeDtypeStruct((T // C, H, HD, HD), F32)),
        grid=(T // (CPS * C),),
        in_specs=[col(0), col(1), col(2), col(3), pl.BlockSpec((1, D_MODEL), lambda c: (0, 0)),
                  pl.BlockSpec((1, HD), lambda c: (0, 0))],
        out_specs=(row, row, pl.BlockSpec((CPS, H, HD, HD), lambda c: (c, 0, 0, 0))),
        scratch_shapes=[pltpu.VMEM((H, HD, HD), F32)],
        compiler_params=_params("arbitrary"), name=name)(proj, proj, proj, proj, lb, gain)


def _hgrn_bwd(proj, o_pre, d_og, states, lb, gain, name):
    T = proj.shape[0]
    C = HGRN_CHUNK
    CPS = HGRN_STEP_CHUNKS
    H, HD = HGRN_HEADS, HGRN_DIM
    NC = T // (CPS * C)

    def body(q_ref, f_ref, i_ref, g_ref, o_ref, dog_ref, st_ref, lb_ref, gain_ref,
             dproj_ref, dlb_ref, dgain_ref, ds_scr, dq_all, dk_all, db_all):
        @pl.when(pl.program_id(0) == 0)
        def _():
            ds_scr[...] = jnp.zeros_like(ds_scr)
            dlb_ref[...] = jnp.zeros_like(dlb_ref)
            dgain_ref[...] = jnp.zeros_like(dgain_ref)

        lbv = lb_ref[...]
        causal = _tri(C, True)
        last_row = lax.broadcasted_iota(jnp.int32, (C, HD), 0) == C - 1
        gain_v = gain_ref[...]
        heads = [slice(h * HD, (h + 1) * HD) for h in range(H)]
        hs = range(H)
        ds_t = [ds_scr[h] for h in hs]
        dgain = None
        for cc in reversed(range(CPS)):
            rows = slice(cc * C, (cc + 1) * C)
            dq_scr, dk_scr, db_scr = dq_all.at[cc], dk_all.at[cc], db_all.at[cc]
            q_raw = q_ref[rows, :]
            gt = _hgrn_gates(q_raw, f_ref[rows, :], lbv)
            o = [o_ref[rows, sl] for sl in heads]
            rstd = [lax.rsqrt(jnp.mean(x * x, axis=-1, keepdims=True) + NORM_EPS) for x in o]
            n = [x * r for x, r in zip(o, rstd)]
            g_raw = [g_ref[rows, sl] for sl in heads]
            sg = [_sigmoid(x) for x in g_raw]
            d_out = [dog_ref[rows, sl] for sl in heads]
            dy = [d * (g * s) for d, g, s in zip(d_out, g_raw, sg)]
            dn = [x * gain_v for x in dy]
            do = [(rstd[h] * (dn[h] - n[h] * jnp.mean(dn[h] * n[h], axis=-1, keepdims=True))).astype(BF16) for h in hs]
            for h in hs:
                dgain = dy[h] * n[h] if dgain is None else dgain + dy[h] * n[h]
            for h, sl in enumerate(heads):
                dproj_ref[rows, 3 * D_MODEL + h * HD:3 * D_MODEL + (h + 1) * HD] = (
                    d_out[h] * n[h] * gain_v * (sg[h] * (1.0 + g_raw[h] * (1.0 - sg[h])))).astype(BF16)
            q_ab, k_bb = gt["q_a"].astype(BF16), gt["k_b"].astype(BF16)
            q_hb, k_tb = gt["q_hat"].astype(BF16), gt["k_til"].astype(BF16)
            v = i_ref[rows, :].astype(BF16)
            s_t = [st_ref[cc, h] for h in hs]
            ds_b = [x.astype(BF16) for x in ds_t]
            p = [jnp.where(causal, _dot(q_ab[:, sl], k_bb[:, sl], NT), 0.0).astype(BF16) for sl in heads]
            dp = [jnp.where(causal, _dot(do[h], v[:, sl], NT), 0.0).astype(BF16) for h, sl in enumerate(heads)]
            dv = [_dot(p[h], do[h], TN) + _dot(k_tb[:, sl], ds_b[h], NT) for h, sl in enumerate(heads)]
            dq_a = [_dot(dp[h], k_bb[:, sl], NN) for h, sl in enumerate(heads)]
            dk_b = [_dot(dp[h], q_ab[:, sl], TN) for h, sl in enumerate(heads)]
            dq_hat = [_dot(do[h], s_t[h].astype(BF16), NN) for h in hs]
            dk_til = [_dot(v[:, sl], ds_b[h], NN) for h, sl in enumerate(heads)]
            ds_new = [_dot(do[h], q_hb[:, sl], TN) + gt["e_last"][:, sl] * ds_t[h] for h, sl in enumerate(heads)]
            for h, sl in enumerate(heads):
                k_til = gt["k_til"][:, sl]
                db_last = jnp.sum(ds_t[h] * gt["e_last"][:, sl] * s_t[h], axis=0, keepdims=True) + jnp.sum(
                    dk_til[h] * k_til, axis=0, keepdims=True)
                dproj_ref[rows, 2 * D_MODEL + h * HD:2 * D_MODEL + (h + 1) * HD] = dv[h].astype(BF16)
                dq_scr[:, sl] = dq_a[h] * gt["e_a"][:, sl] + dq_hat[h] * gt["e_q"][:, sl]
                dk_scr[:, sl] = dk_b[h] * gt["e_b"][:, sl] + dk_til[h] * gt["e_k"][:, sl]
                db = (dq_a[h] * q_ab[:, sl].astype(F32) + dq_hat[h] * gt["q_hat"][:, sl]
                      - dk_b[h] * k_bb[:, sl].astype(F32) - dk_til[h] * k_til)
                db_scr[:, sl] = db + jnp.where(last_row, db_last, 0.0)
            dlogf = _running_sum(db_scr[...], False)
            sig_f, forget, sig_q = gt["sig_f"], gt["forget"], gt["sig_q"]
            dforget = dlogf / forget - dk_scr[...]
            dproj_ref[rows, D_MODEL:2 * D_MODEL] = (dforget * (1.0 - lbv) * sig_f * (1.0 - sig_f)).astype(BF16)
            dlb_ref[...] += jnp.sum(dforget * (1.0 - sig_f), axis=0, keepdims=True)
            dproj_ref[rows, 0:D_MODEL] = (dq_scr[...] * (sig_q * (1.0 + q_raw * (1.0 - sig_q)))).astype(BF16)
            ds_t = ds_new
        dgain_ref[...] += jnp.sum(dgain, axis=0, keepdims=True)
        for h in hs:
            ds_scr[h] = ds_t[h]

    col = lambda j: pl.BlockSpec((CPS * C, D_MODEL), lambda c: (NC - 1 - c, j))
    row = pl.BlockSpec((CPS * C, D_MODEL), lambda c: (NC - 1 - c, 0))
    return pl.pallas_call(
        body,
        out_shape=(jax.ShapeDtypeStruct((T, 4 * D_MODEL), BF16), jax.ShapeDtypeStruct((1, D_MODEL), F32),
                   jax.ShapeDtypeStruct((1, HD), F32)),
        grid=(NC,),
        in_specs=[col(0), col(1), col(2), col(3), row, row,
                  pl.BlockSpec((CPS, H, HD, HD), lambda c: (NC - 1 - c, 0, 0, 0)),
                  pl.BlockSpec((1, D_MODEL), lambda c: (0, 0)), pl.BlockSpec((1, HD), lambda c: (0, 0))],
        out_specs=(pl.BlockSpec((CPS * C, 4 * D_MODEL), lambda c: (NC - 1 - c, 0)),
                   pl.BlockSpec((1, D_MODEL), lambda c: (0, 0)), pl.BlockSpec((1, HD), lambda c: (0, 0))),
        scratch_shapes=[pltpu.VMEM((H, HD, HD), F32)] + [pltpu.VMEM((CPS, C, D_MODEL), F32)] * 3,
        compiler_params=_params("arbitrary"), name=name)(proj, proj, proj, proj, o_pre, d_og, states, lb, gain)


def _attn_masks():
    r = lax.broadcasted_iota(jnp.int32, (ATTN_BLOCK, ATTN_BLOCK), 0)
    c = lax.broadcasted_iota(jnp.int32, (ATTN_BLOCK, ATTN_BLOCK), 1)
    return c >= r, c <= r


def _attn_fwd(qkv, dilation, name):
    T = qkv.shape[0]
    nb = T // dilation // ATTN_BLOCK
    W = ATTN_GROUP_WIDTH
    B = ATTN_BLOCK
    scale = ATTN_DIM ** -0.5
    qb = 2 if nb % 2 == 0 else 1
    steps = nb // qb

    def body(q_ref, kp_ref, kc_ref, vp_ref, vc_ref, o_ref, lse_ref):
        no_prev = jnp.where(pl.program_id(1) > 0, 0.0, NEG_BIG)
        m_prev, m_cur = _attn_masks()
        ones = jnp.ones((B, ATTN_DIM), BF16)
        items = []
        for j in range(qb):
            for h in range(ATTN_GROUP_HEADS):
                sl = slice(h * ATTN_DIM, (h + 1) * ATTN_DIM)
                rows = slice(j * B, (j + 1) * B)
                if j == 0:
                    items.append((rows, sl, kp_ref[:, sl], vp_ref[:, sl], no_prev))
                else:
                    before = slice((j - 1) * B, j * B)
                    items.append((rows, sl, kc_ref[before, sl], vc_ref[before, sl], 0.0))
        s_p = [jnp.where(m_prev, _dot(q_ref[rows, sl], k_p, NT) * scale + bias, NEG_BIG)
               for rows, sl, k_p, _, bias in items]
        s_c = [jnp.where(m_cur, _dot(q_ref[rows, sl], kc_ref[rows, sl], NT) * scale, NEG_BIG)
               for rows, sl, _, _, _ in items]
        m = [jnp.max(jnp.maximum(a, b), axis=-1, keepdims=True) for a, b in zip(s_p, s_c)]
        p_p = [jnp.exp(a - mx).astype(BF16) for a, mx in zip(s_p, m)]
        p_c = [jnp.exp(b - mx).astype(BF16) for b, mx in zip(s_c, m)]
        l = [_dot(a, ones, NN) + _dot(b, ones, NN) for a, b in zip(p_p, p_c)]
        acc = [_dot(a, v_p, NN) + _dot(b, vc_ref[rows, sl], NN)
               for a, b, (rows, sl, _, v_p, _) in zip(p_p, p_c, items)]
        for (rows, sl, _, _, _), a, lv, mx in zip(items, acc, l, m):
            o_ref[rows, sl] = (a / lv).astype(BF16)
            lse_ref[rows, sl] = mx + jnp.log(lv)

    cur = lambda col: pl.BlockSpec((qb * B, W), lambda s, n: (s * steps + n, col))
    prev = lambda col: pl.BlockSpec((B, W), lambda s, n: (s * nb + jnp.maximum(qb * n - 1, 0), col))
    out = pl.BlockSpec((qb * B, W), lambda s, n: (s * steps + n, 0))
    return pl.pallas_call(
        body, out_shape=(jax.ShapeDtypeStruct((T, W), BF16), jax.ShapeDtypeStruct((T, W), F32)),
        grid=(dilation, steps),
        in_specs=[cur(0), prev(1), cur(1), prev(2), cur(2)],
        out_specs=(out, out), compiler_params=_params("parallel", "arbitrary"), name=name)(qkv, qkv, qkv, qkv, qkv)


def _attn_bwd(qkv, d_out, lse, delta, cos, sin, dilation, name):
    T = qkv.shape[0]
    nb = T // dilation // ATTN_BLOCK
    assert nb % 2 == 0, "an even number of 128-token blocks per residue class"
    pairs = nb // 2
    W = ATTN_GROUP_WIDTH
    B = ATTN_BLOCK
    scale = ATTN_DIM ** -0.5

    def unrope(x, cos_v, sin_v):
        return x * cos_v + pltpu.roll(x * sin_v, ATTN_DIM // 2, 1)

    def body(qa_ref, qb_ref, kpair_ref, kc_ref, vpair_ref, vc_ref, doa_ref, dob_ref, lsea_ref, lseb_ref,
             dla_ref, dlb_ref, cos_ref, sin_ref, out_ref, dq_scr, dk_scr, dv_scr):
        n = pl.program_id(1)

        @pl.when(n == 0)
        def _():
            dq_scr[...] = jnp.zeros_like(dq_scr)
            dk_scr[...] = jnp.zeros_like(dk_scr)
            dv_scr[...] = jnp.zeros_like(dv_scr)

        no_a = jnp.where(n > 0, 0.0, NEG_BIG)
        no_b = jnp.where(n < pairs, 0.0, NEG_BIG)
        m_prev, m_cur = _attn_masks()
        lo, hi = slice(0, B), slice(B, 2 * B)
        heads = [slice(h * ATTN_DIM, (h + 1) * ATTN_DIM) for h in range(ATTN_GROUP_HEADS)]
        flat = []
        for sl in heads:
            qa, qb = qa_ref[:, sl], qb_ref[:, sl]
            doa, dob = doa_ref[:, sl], dob_ref[:, sl]
            k0, k1, k2 = kpair_ref[lo, sl], kpair_ref[hi, sl], kc_ref[:, sl]
            v0, v1, v2 = vpair_ref[lo, sl], vpair_ref[hi, sl], vc_ref[:, sl]
            flat += [(qa, doa, lsea_ref[:, sl], dla_ref[:, sl], k0, v0, m_prev, no_a),
                     (qa, doa, lsea_ref[:, sl], dla_ref[:, sl], k1, v1, m_cur, no_a),
                     (qb, dob, lseb_ref[:, sl], dlb_ref[:, sl], k1, v1, m_prev, no_a + no_b),
                     (qb, dob, lseb_ref[:, sl], dlb_ref[:, sl], k2, v2, m_cur, no_b)]
        s = [_dot(q, k, NT) for q, _, _, _, k, _, _, _ in flat]
        dp = [_dot(do, v, NT) for _, do, _, _, _, v, _, _ in flat]
        p = [jnp.where(mask, jnp.exp(sv * scale - lse_v + bias), 0.0)
             for sv, (_, _, lse_v, _, _, _, mask, bias) in zip(s, flat)]
        ds = [(pv * (dpv - dl_v) * scale).astype(BF16) for pv, dpv, (_, _, _, dl_v, _, _, _, _) in zip(p, dp, flat)]
        p = [pv.astype(BF16) for pv in p]
        dq_part = [_dot(dsv, k, NN) for dsv, (_, _, _, _, k, _, _, _) in zip(ds, flat)]
        dk_part = [_dot(dsv, q, TN) for dsv, (q, _, _, _, _, _, _, _) in zip(ds, flat)]
        dv_part = [_dot(pv, do, TN) for pv, (_, do, _, _, _, _, _, _) in zip(p, flat)]
        cos_lo, sin_lo, cos_hi, sin_hi = cos_ref[lo, :], sin_ref[lo, :], cos_ref[hi, :], sin_ref[hi, :]
        for h, sl in enumerate(heads):
            a_prev, a_cur, b_prev, b_cur = range(4 * h, 4 * h + 4)
            kcol = slice(W + h * ATTN_DIM, W + (h + 1) * ATTN_DIM)
            vcol = slice(2 * W + h * ATTN_DIM, 2 * W + (h + 1) * ATTN_DIM)
            out_ref[lo, sl] = unrope(dq_scr[:, sl], cos_lo, sin_lo).astype(BF16)
            out_ref[hi, sl] = unrope(dq_part[a_prev] + dq_part[a_cur], cos_hi, sin_hi).astype(BF16)
            out_ref[lo, kcol] = unrope(dk_scr[:, sl] + dk_part[a_prev], cos_lo, sin_lo).astype(BF16)
            out_ref[hi, kcol] = unrope(dk_part[a_cur] + dk_part[b_prev], cos_hi, sin_hi).astype(BF16)
            out_ref[lo, vcol] = (dv_scr[:, sl] + dv_part[a_prev]).astype(BF16)
            out_ref[hi, vcol] = (dv_part[a_cur] + dv_part[b_prev]).astype(BF16)
            dq_scr[:, sl] = dq_part[b_prev] + dq_part[b_cur]
            dk_scr[:, sl] = dk_part[b_cur]
            dv_scr[:, sl] = dv_part[b_cur]

    def block_a(n):
        return jnp.maximum(2 * n - 1, 0)

    def block_b(n):
        return jnp.minimum(2 * n, nb - 1)

    def pair(n):
        return jnp.maximum(n - 1, 0)

    one_a = lambda col: pl.BlockSpec((B, W), lambda s, n: (s * nb + block_a(n), col))
    one_b = lambda col: pl.BlockSpec((B, W), lambda s, n: (s * nb + block_b(n), col))
    two = lambda col: pl.BlockSpec((2 * B, W), lambda s, n: (s * pairs + pair(n), col))
    tab = pl.BlockSpec((2 * B, ATTN_DIM), lambda s, n: (s * pairs + pair(n), 0))
    return pl.pallas_call(
        body, out_shape=jax.ShapeDtypeStruct((T, 3 * W), BF16), grid=(dilation, pairs + 1),
        in_specs=[one_a(0), one_b(0), two(1), one_b(1), two(2), one_b(2), one_a(0), one_b(0), one_a(0), one_b(0),
                  one_a(0), one_b(0), tab, tab],
        out_specs=pl.BlockSpec((2 * B, 3 * W), lambda s, n: (s * pairs + pair(n), 0)),
        scratch_shapes=[pltpu.VMEM((B, W), F32)] * 3,
        compiler_params=_params("parallel", "arbitrary"), name=name)(
            qkv, qkv, qkv, qkv, qkv, qkv, d_out, d_out, lse, lse, delta, delta, cos, sin)


PERM_TILE = 512
LANES = 128


def _residue_view(x, d):
    return x if d == 1 else x.reshape(d, x.shape[0] // d, x.shape[1])


def _residue_spec(d, tm, cols):
    if d == 1:
        return pl.BlockSpec((tm, cols), lambda i: (i, 0))
    return pl.BlockSpec((d, tm // d, cols), lambda i: (0, i, 0))


def _residue_shape(T, d, cols, dtype):
    return jax.ShapeDtypeStruct((T, cols) if d == 1 else (d, T // d, cols), dtype)


def _class_rows(r, d, tm):
    return pl.ds(r, tm // d, stride=d)


def _attn_norm(h, gain, name):
    T = h.shape[0]
    tm = _pick_tile(T, PERM_TILE, 16 * max(ATTN_DILATIONS))
    dils = ATTN_DILATIONS
    (base_cos, base_sin), (off_cos, off_sin), sign = _rope_parts(T, tm)

    def body(h_ref, g_ref, bc_ref, bs_ref, oc_ref, os_ref, sign_ref, *refs):
        u_refs, c_refs, s_refs, u_scr, c_scr, s_scr = refs[0:3], refs[3:6], refs[6:9], refs[9], refs[10], refs[11]
        hv = h_ref[...]
        rstd = lax.rsqrt(jnp.mean(hv * hv, axis=-1, keepdims=True) + NORM_EPS)
        u = hv * rstd * g_ref[...]
        for j in range(D_MODEL // LANES):
            u_scr[j] = u[:, j * LANES:(j + 1) * LANES]
        bc, bs, oc, osn = bc_ref[0], bs_ref[0], oc_ref[...], os_ref[...]
        c_scr[...] = bc * oc - bs * osn
        s_scr[...] = (bs * oc + bc * osn) * sign_ref[...]
        for d, u_ref, c_ref, s_ref in zip(dils, u_refs, c_refs, s_refs):
            if d == 1:
                u_ref[...] = u.astype(BF16)
                c_ref[...] = c_scr[...]
                s_ref[...] = s_scr[...]
                continue
            for r in range(d):
                rows = _class_rows(r, d, tm)
                for j in range(D_MODEL // LANES):
                    u_ref[r, :, j * LANES:(j + 1) * LANES] = u_scr.at[j][rows, :].astype(BF16)
                c_ref[r] = c_scr[rows, :]
                s_ref[r] = s_scr[rows, :]

    row = pl.BlockSpec((tm, D_MODEL), lambda i: (i, 0))
    base = pl.BlockSpec((1, 1, ATTN_DIM), lambda i: (i, 0, 0))
    off = pl.BlockSpec((tm, ATTN_DIM), lambda i: (0, 0))
    res = pl.pallas_call(
        body,
        out_shape=([_residue_shape(T, d, D_MODEL, BF16) for d in dils]
                   + [_residue_shape(T, d, ATTN_DIM, F32) for d in dils] * 2),
        grid=(T // tm,),
        in_specs=[row, pl.BlockSpec((1, D_MODEL), lambda i: (0, 0)), base, base, off, off,
                  pl.BlockSpec((1, ATTN_DIM), lambda i: (0, 0))],
        out_specs=([_residue_spec(d, tm, D_MODEL) for d in dils] + [_residue_spec(d, tm, ATTN_DIM) for d in dils] * 2),
        scratch_shapes=[pltpu.VMEM((D_MODEL // LANES, tm, LANES), F32), pltpu.VMEM((tm, ATTN_DIM), F32),
                        pltpu.VMEM((tm, ATTN_DIM), F32)],
        compiler_params=_params("parallel"), name=name)(h, gain, base_cos, base_sin, off_cos, off_sin, sign)
    flat = [r.reshape(T, r.shape[-1]) for r in res]
    return flat[0:3], flat[3:6], flat[6:9]


def _attn_merge_fwd(outs, lses, name):
    T = outs[0].shape[0]
    W = ATTN_GROUP_WIDTH
    tm = _pick_tile(T, PERM_TILE, 16 * max(ATTN_DILATIONS))
    dils = ATTN_DILATIONS

    def body(*refs):
        o_refs, l_refs, oc_ref, lse_refs = refs[0:3], refs[3:6], refs[6], refs[7:10]
        o_scr, l_scr, t_scr = refs[10:13]
        nh = ATTN_GROUP_HEADS
        for g, d in enumerate(dils):
            for j in range(nh):
                lanes = slice(j * LANES, (j + 1) * LANES)
                if d == 1:
                    o_scr[g * nh + j] = o_refs[g][:, lanes].astype(F32)
                    l_scr[g * nh + j] = l_refs[g][:, lanes]
                    continue
                for r in range(d):
                    rows = _class_rows(r, d, tm)
                    o_scr.at[g * nh + j][rows, :] = o_refs[g][r, :, lanes].astype(F32)
                    l_scr.at[g * nh + j][rows, :] = l_refs[g][r, :, lanes]
        for j in range(nh):
            lanes = slice(j * LANES, (j + 1) * LANES)
            ls = [l_scr[g * nh + j] for g in range(3)]
            m = jnp.maximum(jnp.maximum(ls[0], ls[1]), ls[2])
            tot = m + jnp.log(jnp.exp(ls[0] - m) + jnp.exp(ls[1] - m) + jnp.exp(ls[2] - m))
            t_scr[j] = tot
            for g, d in enumerate(dils):
                oc_ref[:, g * W + j * LANES:g * W + (j + 1) * LANES] = (
                    o_scr[g * nh + j] * jnp.exp(ls[g] - tot)).astype(BF16)
                if d == 1:
                    lse_refs[g][:, lanes] = tot
                    continue
                for r in range(d):
                    lse_refs[g][r, :, lanes] = t_scr.at[j][_class_rows(r, d, tm), :]

    in_blk = [_residue_spec(d, tm, W) for d in dils]
    n_blk = 3 * ATTN_GROUP_HEADS
    res = pl.pallas_call(
        body, out_shape=[jax.ShapeDtypeStruct((T, 3 * W), BF16)] + [_residue_shape(T, d, W, F32) for d in dils],
        grid=(T // tm,), in_specs=in_blk * 2,
        out_specs=[pl.BlockSpec((tm, 3 * W), lambda i: (i, 0))] + in_blk,
        scratch_shapes=[pltpu.VMEM((n_blk, tm, LANES), F32), pltpu.VMEM((n_blk, tm, LANES), F32),
                        pltpu.VMEM((ATTN_GROUP_HEADS, tm, LANES), F32)],
        compiler_params=_params("parallel"), name=name)(
            *[_residue_view(o, d) for o, d in zip(outs, dils)], *[_residue_view(l, d) for l, d in zip(lses, dils)])
    return res[0], [r.reshape(T, W) for r in res[1:]]


def _attn_merge_bwd(d_oc, oc, name):
    T = d_oc.shape[0]
    W = ATTN_GROUP_WIDTH
    tm = _pick_tile(T, PERM_TILE, 16 * max(ATTN_DILATIONS))
    dils = ATTN_DILATIONS

    def body(d_ref, o_ref, *refs):
        delta_refs, db_refs, dl_scr, d_scr = refs[0:3], refs[3:6], refs[6], refs[7]
        nh = ATTN_GROUP_HEADS
        for j in range(nh):
            tot = jnp.zeros((tm, 1), F32)
            for g in range(3):
                cols = slice(g * W + j * LANES, g * W + (j + 1) * LANES)
                d_blk = d_ref[:, cols]
                d_scr[g * nh + j] = d_blk
                tot = tot + jnp.sum(d_blk * o_ref[:, cols].astype(F32), axis=-1, keepdims=True)
            dl_scr[j] = jnp.broadcast_to(tot, (tm, LANES))
        for g, d in enumerate(dils):
            for j in range(nh):
                lanes = slice(j * LANES, (j + 1) * LANES)
                if d == 1:
                    delta_refs[g][:, lanes] = dl_scr[j]
                    db_refs[g][:, lanes] = d_scr[g * nh + j].astype(BF16)
                    continue
                for r in range(d):
                    rows = _class_rows(r, d, tm)
                    delta_refs[g][r, :, lanes] = dl_scr.at[j][rows, :]
                    db_refs[g][r, :, lanes] = d_scr.at[g * nh + j][rows, :].astype(BF16)

    wide = pl.BlockSpec((tm, 3 * W), lambda i: (i, 0))
    out_blk = [_residue_spec(d, tm, W) for d in dils]
    res = pl.pallas_call(
        body, out_shape=[_residue_shape(T, d, W, F32) for d in dils] + [_residue_shape(T, d, W, BF16) for d in dils],
        grid=(T // tm,), in_specs=[wide, wide], out_specs=out_blk * 2,
        scratch_shapes=[pltpu.VMEM((ATTN_GROUP_HEADS, tm, LANES), F32),
                        pltpu.VMEM((3 * ATTN_GROUP_HEADS, tm, LANES), F32)],
        compiler_params=_params("parallel"), name=name)(d_oc, oc)
    flat = [r.reshape(T, W) for r in res]
    return flat[0:3], flat[3:6]


def _rope_parts(T, tile):
    inv_freq = 1.0 / (ROPE_THETA ** (jnp.arange(0, ATTN_DIM, 2, dtype=F32) / ATTN_DIM))
    inv_freq = jnp.concatenate([inv_freq, inv_freq])[None, :]
    base = (jnp.arange(T // tile, dtype=F32) * tile)[:, None] * inv_freq
    off = jnp.arange(tile, dtype=F32)[:, None] * inv_freq
    sign = jnp.concatenate([-jnp.ones((1, ATTN_DIM // 2), F32), jnp.ones((1, ATTN_DIM // 2), F32)], axis=1)
    return (jnp.cos(base)[:, None, :], jnp.sin(base)[:, None, :]), (jnp.cos(off), jnp.sin(off)), sign


WEIGHT_GROUPS = {"hgrn": ("hgrn_in", "hgrn_out"), "ffn0": ("ffn_in0", "ffn_down0"),
                 "attn": ("qkv", "attn_out"), "ffn1": ("ffn_in1", "ffn_down1")}


def _local_step(x, target, norm_mix, norm_ffn, lb, out_gain, final_gain, fetch, publish):
    g_mix = [norm_mix[0:1], norm_mix[1:2]]
    g_ffn = [norm_ffn[0:1], norm_ffn[1:2]]
    w = {}

    def whole(name):
        return [(w[name], w[name].shape[0], 0)]

    def qkv_parts(g):
        return [(w["qkv"], ATTN_GROUP_WIDTH, 3 * j + g) for j in range(3)]

    def ffn_fwd(h, layer, head=None):
        w.update(fetch(f"ffn{layer}"))
        n, gate, up, a = _ffn_in(h, g_ffn[layer], w[f"ffn_in{layer}"], f"ffn{layer}_in")
        out = _mm_nn([a], [whole(f"ffn_down{layer}")], h, name=f"ffn{layer}_down", head=head)
        return out, (n, gate, up, a)

    def ffn_bwd(h, saved, dh, dhb, layer):
        n, gate, up, a = saved
        w_in = w[f"ffn_in{layer}"]
        dgate, dup = _ffn_down_dx(dhb, w[f"ffn_down{layer}"], gate, up, f"ffn{layer}_down_dx")
        grad_in = _mm_tn(dgate, n, name=f"ffn{layer}_in_dw_gate", rows=2 * D_FF)
        grad_in = _mm_tn(dup, n, name=f"ffn{layer}_in_dw_up", into=grad_in, row_tile=D_FF // GRAD_TILE, rows=2 * D_FF)
        grads = {f"ffn_down{layer}": _mm_tn(a, dhb, name=f"ffn{layer}_down_dw"), f"ffn_in{layer}": grad_in}
        publish(f"ffn{layer}", grads)
        return _mm_nn([dgate, dup], [[(w_in, D_FF, 0)], [(w_in, D_FF, 1)]], dh, name=f"ffn{layer}_in_dx",
                      norm=(h, g_ffn[layer]))

    u0 = _rms_fwd(x, g_mix[0], "hgrn_norm")
    w.update(fetch("hgrn"))
    proj = _mm_nt(u0, whole("hgrn_in"), out_dtype=F32, name="hgrn_in")
    og, o_pre, states = _hgrn_fwd(proj, lb, out_gain, "hgrn_fwd")
    h1 = _mm_nn([og], [whole("hgrn_out")], x, name="hgrn_out")
    h2, ffn0 = ffn_fwd(h1, 0)

    u1_g, cos_g, sin_g = _attn_norm(h2, g_mix[1], "attn_norm")
    w.update(fetch("attn"))
    qkv_g, outs, lses = [], [], []
    for g, d in enumerate(ATTN_DILATIONS):
        qkv_g.append(_mm_nt(u1_g[g], qkv_parts(g), out_dtype=BF16, name=f"attn_qkv{g}",
                            rope=(cos_g[g], sin_g[g], 2)))
        o_g, lse_g = _attn_fwd(qkv_g[g], d, f"attn_fwd{g}")
        outs.append(o_g)
        lses.append(lse_g)
    oc, lse_all = _attn_merge_fwd(outs, lses, "attn_merge")
    h3 = _mm_nn([oc], [whole("attn_out")], h2, name="attn_out")
    (dh4, dh4b, d_final, loss_part), ffn1 = ffn_fwd(h3, 1, head=(target, final_gain))
    dh3, dh3b, d_ffn1 = ffn_bwd(h3, ffn1, dh4, dh4b, 1)

    d_oc = _mm_nt(dh3b, whole("attn_out"), out_dtype=F32, name="attn_out_dx")
    grad_attn_out = _mm_tn(oc, dh3b, name="attn_out_dw")
    delta, d_ocb = _attn_merge_bwd(d_oc, oc, "attn_merge_bwd")
    du1, qkv_pieces = [], []
    for g, d in enumerate(ATTN_DILATIONS):
        dqkv = _attn_bwd(qkv_g[g], d_ocb[g], lse_all[g], delta[g], cos_g[g], sin_g[g], d, f"attn_bwd{g}")
        qkv_pieces.append(_mm_tn(dqkv, u1_g[g], name=f"attn_qkv_dw{g}"))
        du1.append(_mm_nn([dqkv], [qkv_parts(g)], None, name=f"attn_qkv_dx{g}"))
    grad_qkv = jnp.stack([p.reshape(3, ATTN_GROUP_WIDTH, D_MODEL) for p in qkv_pieces], axis=1).reshape(
        3 * ATTN_WIDTH, D_MODEL)
    publish("attn", {"qkv": grad_qkv, "attn_out": grad_attn_out})
    dh2, dh2b, d_mix1 = _rms_bwd(h2, g_mix[1], du1, dh3, "attn_norm_bwd", ATTN_DILATIONS)

    dh1, dh1b, d_ffn0 = ffn_bwd(h1, ffn0, dh2, dh2b, 0)

    d_og = _mm_nt(dh1b, whole("hgrn_out"), out_dtype=F32, name="hgrn_out_dx")
    grad_hgrn_out = _mm_tn(og, dh1b, name="hgrn_out_dw")
    dproj, d_lb, d_out_gain = _hgrn_bwd(proj, o_pre, d_og, states, lb, out_gain, "hgrn_bwd")
    publish("hgrn", {"hgrn_in": _mm_tn(dproj, u0, name="hgrn_in_dw"), "hgrn_out": grad_hgrn_out})
    dx, _, d_mix0 = _mm_nn([dproj], [whole("hgrn_in")], dh1, name="hgrn_in_dx", norm=(x, g_mix[0]))

    small = dict(norm_mix0=d_mix0, norm_mix1=d_mix1, norm_ffn0=d_ffn0, norm_ffn1=d_ffn1, lb=d_lb,
                 out_gain=d_out_gain, final=d_final, loss=loss_part)
    return dx, small


MESH_IDS = pl.DeviceIdType.MESH
HBM_SPEC = pl.BlockSpec(memory_space=pl.ANY)


N_PEERS = N_DEV - 1
PEER_OFFSETS = [(dx, dy, dc) for dx in (0, 1) for dy in (0, 1) for dc in (0, 1)][1:]


def _mesh_place():
    x, y, c = lax.axis_index("x"), lax.axis_index("y"), lax.axis_index("c")
    peers = []
    for dx, dy, dc in PEER_OFFSETS:
        px, py, pc = (1 - x if dx else x), (1 - y if dy else y), (1 - c if dc else c)
        peers.append(((px, py, pc), 4 * px + 2 * py + pc))
    return 4 * x + 2 * y + c, peers


def _gather_over_two_levels(src_refs, land_refs, send_sems, recv_sems):
    n = len(src_refs)
    x, y, c = lax.axis_index("x"), lax.axis_index("y"), lax.axis_index("c")
    me, sibling = (x, y, c), (x, y, 1 - c)
    chips = [(1 - x, y), (x, 1 - y), (1 - x, 1 - y)]

    def block(w, px, py, pc):
        return land_refs[w].at[4 * px + 2 * py + pc]

    def copy(w, k, owner, to, src=None):
        return pltpu.make_async_remote_copy(
            src_ref=block(w, *owner) if src is None else src, dst_ref=block(w, *owner),
            send_sem=send_sems.at[w * N_PEERS + k], recv_sem=recv_sems.at[w * N_PEERS + k],
            device_id=to, device_id_type=MESH_IDS)

    sent = []
    for w in range(n):
        sent.append(copy(w, 0, me, sibling, src=src_refs[w]))
        sent += [copy(w, 1 + j, me, (*chip, c), src=src_refs[w]) for j, chip in enumerate(chips)]
    for cp in sent:
        cp.start()
    for w in range(n):
        for j, chip in enumerate(chips):
            copy(w, 1 + j, (*chip, c), me).wait_recv()
            passed = copy(w, 4 + j, (*chip, c), sibling)
            passed.start()
            sent.append(passed)
    for w in range(n):
        copy(w, 0, sibling, me).wait_recv()
        for j, chip in enumerate(chips):
            copy(w, 4 + j, (*chip, 1 - c), me).wait_recv()
    for cp in sent:
        cp.wait_send()


def _exchange_launch(srcs, scatter, collective_id, name):
    n = len(srcs)
    src_refs = [jax.new_ref(s, memory_space=pltpu.MemorySpace.HBM) for s in srcs]
    land_refs = [jax.empty_ref(jax.ShapeDtypeStruct(s.shape if scatter else (N_DEV,) + s.shape, s.dtype),
                               memory_space=pltpu.MemorySpace.HBM) for s in srcs]

    @pl.kernel(mesh=plsc.ScalarSubcoreMesh(axis_name="sequencer", num_cores=1), name=name,
               scratch_types=(pltpu.SemaphoreType.DMA((n * N_PEERS,)), pltpu.SemaphoreType.DMA((n * N_PEERS,)),
                              pltpu.SemaphoreType.DMA((n,))),
               compiler_params=pltpu.CompilerParams(collective_id=collective_id))
    def launch(send_sems, recv_sems, local_sems):
        me, peers = _mesh_place()
        barrier = pltpu.get_barrier_semaphore()
        for peer, _ in peers:
            pl.semaphore_signal(barrier, inc=1, device_id=peer, device_id_type=MESH_IDS)
        pl.semaphore_wait(barrier, N_PEERS)
        own = [pltpu.make_async_copy(src_refs[w].at[me] if scatter else src_refs[w], land_refs[w].at[me],
                                     local_sems.at[w]) for w in range(n)]
        for cp in own:
            cp.start()
        if scatter:
            copies = [pltpu.make_async_remote_copy(
                src_ref=src_refs[w].at[pid], dst_ref=land_refs[w].at[me],
                send_sem=send_sems.at[w * N_PEERS + k], recv_sem=recv_sems.at[w * N_PEERS + k],
                device_id=peer, device_id_type=MESH_IDS) for w in range(n) for k, (peer, pid) in enumerate(peers)]
            for cp in copies:
                cp.start()
            for cp in copies:
                cp.wait()
        else:
            _gather_over_two_levels(src_refs, land_refs, send_sems, recv_sems)
        for cp in own:
            cp.wait()

    launch()
    return land_refs


def _gather_small(block, name):
    def body(in_ref, out_ref, send_sems, recv_sems, local_sem):
        me, peers = _mesh_place()
        own = pltpu.make_async_copy(in_ref, out_ref.at[me], local_sem)
        own.start()
        sends = [pltpu.make_async_remote_copy(
            src_ref=in_ref, dst_ref=out_ref.at[me], send_sem=send_sems.at[k], recv_sem=recv_sems.at[k],
            device_id=peer, device_id_type=MESH_IDS) for k, (peer, _) in enumerate(peers)]
        for cp in sends:
            cp.start()
        for cp in sends:
            cp.wait_recv()
        for cp in sends:
            cp.wait_send()
        own.wait()

    return pl.pallas_call(
        body, out_shape=jax.ShapeDtypeStruct((N_DEV,) + block.shape, block.dtype),
        in_specs=[HBM_SPEC], out_specs=HBM_SPEC,
        scratch_shapes=[pltpu.SemaphoreType.DMA((N_PEERS,)), pltpu.SemaphoreType.DMA((N_PEERS,)),
                        pltpu.SemaphoreType.DMA],
        name=name)(block)


def _sum_blocks(recv, name):
    rows = recv.shape[1]
    tr = _pick_tile(rows, 256, 16)

    def body(r_ref, g_ref):
        acc = r_ref[0].astype(F32)
        for j in range(1, N_DEV):
            acc = acc + r_ref[j].astype(F32)
        g_ref[...] = acc

    return pl.pallas_call(
        body, out_shape=jax.ShapeDtypeStruct((rows, D_MODEL), F32), grid=(rows // tr,),
        in_specs=[pl.BlockSpec((N_DEV, tr, D_MODEL), lambda i: (0, i, 0))],
        out_specs=pl.BlockSpec((tr, D_MODEL), lambda i: (i, 0)),
        compiler_params=_params("parallel"), name=name)(recv)


def _adamw_math(w, g, m, v):
    m_new = ADAM_B1 * m + (1.0 - ADAM_B1) * g
    v_new = ADAM_B2 * v + (1.0 - ADAM_B2) * (g * g)
    m_hat = m_new / (1.0 - ADAM_B1 ** ADAM_STEP)
    v_hat = v_new / (1.0 - ADAM_B2 ** ADAM_STEP)
    delta = -ADAM_LR * (m_hat / (jnp.sqrt(v_hat) + ADAM_EPS) + ADAM_WD * w)
    return delta, m_new, v_new


def _adamw(w, g, m, v, name):
    rows, cols = w.shape
    tr = _pick_tile(rows, 256, 8)

    def body(w_ref, g_ref, m_ref, v_ref, d_ref, mo_ref, vo_ref):
        d_ref[...], mo_ref[...], vo_ref[...] = _adamw_math(w_ref[...], g_ref[...], m_ref[...], v_ref[...])

    blk = pl.BlockSpec((tr, cols), lambda i: (i, 0))
    return pl.pallas_call(
        body, out_shape=(jax.ShapeDtypeStruct((rows, cols), F32),) * 3, grid=(rows // tr,),
        in_specs=[blk] * 4, out_specs=(blk,) * 3, compiler_params=_params("parallel"), name=name)(w, g, m, v)


ROW_MIX, ROW_FFN, ROW_LB, ROW_OUT_GAIN, ROW_FINAL = 0, 2, 4, 7, 8
PART_MIX, PART_FFN, PART_LB, PART_OUT_GAIN, PART_FINAL, PART_LOSS = 0, 2, 4, 5, 6, 7


def _small_update(parts_all, w, m, v, name):
    def body(p_ref, w_ref, m_ref, v_ref, g_ref, d_ref, mo_ref, vo_ref, loss_ref):
        def total(row, n=1):
            tot = p_ref[0, row:row + n, :]
            for j in range(1, N_DEV):
                tot = tot + p_ref[j, row:row + n, :]
            return tot

        logits = [w_ref[ROW_LB + i:ROW_LB + i + 1, :] for i in range(3)]
        mx = jnp.maximum(jnp.maximum(logits[0], logits[1]), logits[2])
        ex = [jnp.exp(l - mx) for l in logits]
        den = ex[0] + ex[1] + ex[2]
        prob = [e / den for e in ex]
        d_lb = total(PART_LB)
        g_ref[...] = jnp.zeros_like(g_ref)
        g_ref[ROW_MIX:ROW_MIX + 2, :] = total(PART_MIX, 2)
        g_ref[ROW_FFN:ROW_FFN + 2, :] = total(PART_FFN, 2)
        for i in range(3):
            g_ref[ROW_LB + i:ROW_LB + i + 1, :] = prob[i] * ((d_lb if i == 0 else 0.0) - prob[0] * d_lb)
        g_ref[ROW_OUT_GAIN:ROW_OUT_GAIN + 1, :] = total(PART_OUT_GAIN)
        g_ref[ROW_FINAL:ROW_FINAL + 1, :] = total(PART_FINAL)
        d_ref[...], mo_ref[...], vo_ref[...] = _adamw_math(w_ref[...], g_ref[...], m_ref[...], v_ref[...])
        loss_ref[...] = jnp.sum(total(PART_LOSS), axis=-1, keepdims=True)

    packed = jax.ShapeDtypeStruct((16, D_MODEL), F32)
    return pl.pallas_call(
        body, out_shape=(packed, packed, packed, packed, jax.ShapeDtypeStruct((1, 1), F32)),
        compiler_params=pltpu.CompilerParams(vmem_limit_bytes=VMEM_LIMIT), name=name)(parts_all, w, m, v)


def _pack_small(norm_mix, norm_ffn, lb_logits, out_gain, final):
    pad = jnp.zeros((1, D_MODEL - HGRN_DIM), F32)
    return jnp.concatenate([norm_mix, norm_ffn, lb_logits, jnp.concatenate([out_gain, pad], axis=1),
                            final.reshape(1, D_MODEL), jnp.zeros((16 - ROW_FINAL - 1, D_MODEL), F32)], axis=0)


def _unpack_small(p):
    return (p[ROW_MIX:ROW_MIX + 2], p[ROW_FFN:ROW_FFN + 2], p[ROW_LB:ROW_LB + 3],
            p[ROW_OUT_GAIN:ROW_OUT_GAIN + 1, :HGRN_DIM], p[ROW_FINAL])


def _lower_bound(lb_logits, name):
    def body(l_ref, o_ref):
        logits = [l_ref[i:i + 1, :] for i in range(3)]
        mx = jnp.maximum(jnp.maximum(logits[0], logits[1]), logits[2])
        ex = [jnp.exp(l - mx) for l in logits]
        o_ref[...] = ex[0] / (ex[0] + ex[1] + ex[2])

    return pl.pallas_call(body, out_shape=jax.ShapeDtypeStruct((1, D_MODEL), F32), name=name)(lb_logits)


def kernel(x, norm_mix, norm_ffn, hgrn_w_in, hgrn_lb_logits, hgrn_out_norm, hgrn_w_out, attn_w_qkv, attn_w_out, ffn_w_in, ffn_w_down, final_norm, loss_target, m_norm_mix, m_norm_ffn, m_hgrn_w_in, m_hgrn_lb_logits, m_hgrn_out_norm, m_hgrn_w_out, m_attn_w_qkv, m_attn_w_out, m_ffn_w_in, m_ffn_w_down, m_final_norm, v_norm_mix, v_norm_ffn, v_hgrn_w_in, v_hgrn_lb_logits, v_hgrn_out_norm, v_hgrn_w_out, v_attn_w_qkv, v_attn_w_out, v_ffn_w_in, v_ffn_w_down, v_final_norm):
    col_sharded = {"hgrn_in": hgrn_w_in[0], "qkv": attn_w_qkv[0], "ffn_in0": ffn_w_in[0], "ffn_in1": ffn_w_in[1]}
    row_sharded = {"hgrn_out": hgrn_w_out[0], "attn_out": attn_w_out[0], "ffn_down0": ffn_w_down[0],
                   "ffn_down1": ffn_w_down[1]}
    gathering = {}
    for gi, (group, names) in enumerate(WEIGHT_GROUPS.items()):
        shards = [(col_sharded[n].T if n in col_sharded else row_sharded[n]).astype(BF16) for n in names]
        gathering[group] = _exchange_launch(shards, False, 1 + gi, f"weights_gather_{group}")

    def fetch(group):
        return {n: land[...].reshape(-1, D_MODEL) for n, land in zip(WEIGHT_GROUPS[group], gathering[group])}

    in_flight = {}

    def publish(group, grads):
        names = WEIGHT_GROUPS[group]
        parts = [grads[n].reshape(N_DEV, -1, D_MODEL) for n in names]
        in_flight[group] = _exchange_launch(parts, True, 1 + len(WEIGHT_GROUPS) + list(WEIGHT_GROUPS).index(group),
                                            f"grads_send_{group}")

    lb = _lower_bound(hgrn_lb_logits, "hgrn_lower_bound")
    grad_x, small = _local_step(x[0], loss_target[0], norm_mix, norm_ffn, lb, hgrn_out_norm,
                                final_norm.reshape(1, D_MODEL), fetch, publish)

    pad = jnp.zeros((1, D_MODEL - HGRN_DIM), F32)
    small_part = jnp.concatenate(
        [small["norm_mix0"], small["norm_mix1"], small["norm_ffn0"], small["norm_ffn1"], small["lb"],
         jnp.concatenate([small["out_gain"], pad], axis=1), small["final"], small["loss"]], axis=0)
    small_all = _gather_small(small_part, "small_grads_gather")
    received = {}
    for group in ("ffn1", "attn", "ffn0", "hgrn"):
        received.update(zip(WEIGHT_GROUPS[group], [land[...] for land in in_flight[group]]))

    masters = {"hgrn_w_in": (hgrn_w_in, m_hgrn_w_in, v_hgrn_w_in, ("hgrn_in",)),
               "hgrn_w_out": (hgrn_w_out, m_hgrn_w_out, v_hgrn_w_out, ("hgrn_out",)),
               "attn_w_qkv": (attn_w_qkv, m_attn_w_qkv, v_attn_w_qkv, ("qkv",)),
               "attn_w_out": (attn_w_out, m_attn_w_out, v_attn_w_out, ("attn_out",)),
               "ffn_w_in": (ffn_w_in, m_ffn_w_in, v_ffn_w_in, ("ffn_in0", "ffn_in1")),
               "ffn_w_down": (ffn_w_down, m_ffn_w_down, v_ffn_w_down, ("ffn_down0", "ffn_down1"))}
    big = {}
    for param, (wv, mv, vv, names) in masters.items():
        layers = []
        for n in names:
            g = _sum_blocks(received[n], f"{n}_grad_sum")
            layers.append(g.T if n in col_sharded else g)
        grad = jnp.stack(layers)
        flat = (-1, wv.shape[-1])
        updates = _adamw(wv.reshape(flat), grad.reshape(flat), mv.reshape(flat), vv.reshape(flat), f"{param}_adamw")
        big[param] = [grad] + [u.reshape(wv.shape) for u in updates]

    w_small = _pack_small(norm_mix, norm_ffn, hgrn_lb_logits, hgrn_out_norm, final_norm)
    m_small = _pack_small(m_norm_mix, m_norm_ffn, m_hgrn_lb_logits, m_hgrn_out_norm, m_final_norm)
    v_small = _pack_small(v_norm_mix, v_norm_ffn, v_hgrn_lb_logits, v_hgrn_out_norm, v_final_norm)
    g_s, d_s, m_s, v_s, loss = _small_update(small_all, w_small, m_small, v_small, "small_update")
    small_out = [_unpack_small(t) for t in (g_s, d_s, m_s, v_s)]

    def group(i):
        s = small_out[i]
        return (s[0], s[1], big["hgrn_w_in"][i], s[2], s[3], big["hgrn_w_out"][i], big["attn_w_qkv"][i],
                big["attn_w_out"][i], big["ffn_w_in"][i], big["ffn_w_down"][i], s[4])

    return (loss.reshape(()), grad_x[None], *group(0), *group(1), *group(2), *group(3))
```

```python
import functools

import jax
import jax.numpy as jnp
from jax import lax
from jax.experimental import pallas as pl
from jax.experimental.pallas import tpu as pltpu
from jax.experimental.pallas import tpu_sc as plsc

F32 = jnp.float32
BF16 = jnp.bfloat16

D_MODEL = 1024
N_DEV = 8
NORM_EPS = 1e-6

HGRN_HEADS = 8
HGRN_DIM = 128
HGRN_CHUNK = 64
HGRN_STEP_CHUNKS = 2
HGRN_EXP_CLAMP = 60.0

ATTN_DIM = 128
ATTN_BLOCK = 128
ATTN_GROUP_HEADS = 4
ATTN_GROUP_WIDTH = ATTN_GROUP_HEADS * ATTN_DIM
ATTN_DILATIONS = (1, 4, 16)
ATTN_WIDTH = 3 * ATTN_GROUP_WIDTH
ROPE_THETA = 10000.0
NEG_BIG = -1e30

D_FF = 2816

ADAM_LR = 0.001
ADAM_B1 = 0.9
ADAM_B2 = 0.999
ADAM_EPS = 1e-08
ADAM_WD = 0.01
ADAM_STEP = 10

VMEM_LIMIT = 48 * 1024 * 1024

NT = (((1,), (1,)), ((), ()))
NN = (((1,), (0,)), ((), ()))
TN = (((0,), (0,)), ((), ()))


def _dot(a, b, dims):
    return lax.dot_general(a, b, dims, preferred_element_type=F32)


def _params(*sem):
    return pltpu.CompilerParams(dimension_semantics=sem, vmem_limit_bytes=VMEM_LIMIT)


def _pick_tile(n, cap, mult):
    best = None
    for t in range(mult, min(n, cap) + 1, mult):
        if n % t == 0:
            best = t
    assert best is not None, (n, cap, mult)
    return best


def _sigmoid(x):
    return 0.5 * jnp.tanh(0.5 * x) + 0.5


ROW_TILE = 512
COL_CHUNK = 512
GRAD_TILE = 256
GRAD_LOAD_CHUNKS = 4


def _whole(shape, index_map):
    return pl.BlockSpec(shape, index_map, pipeline_mode=pl.Buffered(1))


def _part_specs(parts, n_cols):
    return [_whole((rows, n_cols), functools.partial(lambda i, b: (b, 0), b=blk)) for _, rows, blk in parts]


def _mm_nt(a, w_parts, *, out_dtype, name, rope=None):
    M, K = a.shape
    tm = _pick_tile(M, ROW_TILE, 16)
    widths = [rows for _, rows, _ in w_parts]
    n_parts = len(w_parts)

    def body(*refs):
        a_ref, w_refs, o_ref = refs[0], refs[1:1 + n_parts], refs[-1]
        av = a_ref[...]
        off = 0
        for p, w_ref in enumerate(w_refs):
            for c0 in range(0, widths[p], COL_CHUNK):
                cw = min(COL_CHUNK, widths[p] - c0)
                acc = _dot(av, w_ref[c0:c0 + cw, :], NT)
                if rope is not None and p < rope[2]:
                    cos, sin = refs[1 + n_parts][...], refs[2 + n_parts][...]
                    for h0 in range(0, cw, ATTN_DIM):
                        xh = acc[:, h0:h0 + ATTN_DIM]
                        rot = pltpu.roll(xh, ATTN_DIM // 2, 1)
                        o_ref[:, off + c0 + h0:off + c0 + h0 + ATTN_DIM] = (xh * cos + rot * sin).astype(out_dtype)
                else:
                    o_ref[:, off + c0:off + c0 + cw] = acc.astype(out_dtype)
            off += widths[p]

    in_specs = [pl.BlockSpec((tm, K), lambda i: (i, 0))] + _part_specs(w_parts, K)
    args = [a] + [w for w, _, _ in w_parts]
    if rope is not None:
        in_specs += [pl.BlockSpec((tm, ATTN_DIM), lambda i: (i, 0))] * 2
        args += [rope[0], rope[1]]
    return pl.pallas_call(
        body, out_shape=jax.ShapeDtypeStruct((M, sum(widths)), out_dtype), grid=(M // tm,),
        in_specs=in_specs, out_specs=pl.BlockSpec((tm, sum(widths)), lambda i: (i, 0)),
        compiler_params=_params("parallel"), name=name)(*args)


def _mm_nn(a_list, w_parts_list, resid, *, name, norm=None, head=None):
    M = a_list[0].shape[0]
    tm = _pick_tile(M, ROW_TILE, 16)
    n_a = len(a_list)
    flat_parts = [p for parts in w_parts_list for p in parts]
    extra = norm if norm is not None else head
    n_in = n_a + len(flat_parts) + (1 if resid is not None else 0) + (2 if extra is not None else 0)

    def body(*refs):
        a_refs, w_refs = refs[:n_a], refs[n_a:n_a + len(flat_parts)]

        def product(rows):
            acc = None
            wi = 0
            for a_ref, parts in zip(a_refs, w_parts_list):
                off = 0
                for _, k, _ in parts:
                    term = _dot(a_ref[rows, off:off + k], w_refs[wi][...], NN)
                    acc = term if acc is None else acc + term
                    off += k
                    wi += 1
            return acc

        if extra is None:
            acc = product(slice(None))
            if resid is not None:
                acc = acc + refs[n_in - 1][...]
            refs[n_in][...] = acc
            return

        @pl.when(pl.program_id(0) == 0)
        def _():
            for acc_ref in refs[n_in + 2:]:
                acc_ref[...] = jnp.zeros_like(acc_ref)

        for r0 in range(0, tm, tm // 2):
            rows = slice(r0, r0 + tm // 2)
            acc = product(rows)
            if head is not None:
                _loss_head_math(acc + refs[n_in - 3][rows, :], rows, refs[n_in - 2], refs[n_in - 1],
                                *refs[n_in:n_in + 4])
                continue
            dres_ref, x_ref, g_ref = refs[n_in - 3:n_in]
            dx_ref, dxb_ref, dg_ref = refs[n_in:n_in + 3]
            xv = x_ref[rows, :]
            rstd = lax.rsqrt(jnp.mean(xv * xv, axis=-1, keepdims=True) + NORM_EPS)
            n = xv * rstd
            dg_ref[...] += jnp.sum(acc * n, axis=0, keepdims=True)
            dn = acc * g_ref[...]
            dx = dres_ref[rows, :] + rstd * (dn - n * jnp.mean(dn * n, axis=-1, keepdims=True))
            dx_ref[rows, :] = dx
            dxb_ref[rows, :] = dx.astype(BF16)

    row = pl.BlockSpec((tm, D_MODEL), lambda i: (i, 0))
    vec = pl.BlockSpec((1, D_MODEL), lambda i: (0, 0))
    in_specs = [pl.BlockSpec((tm, a.shape[1]), lambda i: (i, 0)) for a in a_list] + _part_specs(flat_parts, D_MODEL)
    args = list(a_list) + [w for w, _, _ in flat_parts]
    if resid is not None:
        in_specs.append(row)
        args.append(resid)
    if extra is None:
        return pl.pallas_call(
            body, out_shape=jax.ShapeDtypeStruct((M, D_MODEL), F32), grid=(M // tm,),
            in_specs=in_specs, out_specs=row, compiler_params=_params("parallel"), name=name)(*args)
    assert resid is not None
    out_shape = [jax.ShapeDtypeStruct((M, D_MODEL), F32), jax.ShapeDtypeStruct((M, D_MODEL), BF16),
                 jax.ShapeDtypeStruct((1, D_MODEL), F32)]
    out_specs = [row, row, vec]
    if head is not None:
        out_shape.append(jax.ShapeDtypeStruct((1, D_MODEL), F32))
        out_specs.append(vec)
    return pl.pallas_call(
        body, out_shape=out_shape, grid=(M // tm,), in_specs=in_specs + [row, vec], out_specs=out_specs,
        compiler_params=_params("arbitrary"), name=name)(*args, extra[0], extra[1])


def _mm_tn(a, b, *, name, into=None, row_tile=0, rows=None):
    T, R = a.shape
    N = b.shape[1]
    tr = GRAD_TILE
    rows = R if rows is None else rows
    tc = T // GRAD_LOAD_CHUNKS

    def body(a_ref, b_hbm, *refs):
        o_ref, b_scr, sems = refs[-3:]
        first = pl.program_id(0) == 0

        def load(c):
            part = pl.ds(c * tc, tc)
            return pltpu.make_async_copy(b_hbm.at[part, :], b_scr.at[part, :], sems.at[c])

        @pl.when(first)
        def _():
            for c in range(GRAD_LOAD_CHUNKS):
                load(c).start()
            acc = None
            for c in range(GRAD_LOAD_CHUNKS):
                load(c).wait()
                part = slice(c * tc, (c + 1) * tc)
                term = _dot(a_ref[part, :], b_scr[part, :], TN)
                acc = term if acc is None else acc + term
            o_ref[...] = acc.astype(BF16)

        @pl.when(jnp.logical_not(first))
        def _():
            o_ref[...] = _dot(a_ref[...], b_scr[...], TN).astype(BF16)

    in_specs = [pl.BlockSpec((T, tr), lambda r: (0, r)), HBM_SPEC]
    args = [a, b]
    if into is not None:
        in_specs.append(HBM_SPEC)
        args.append(into)
    return pl.pallas_call(
        body, out_shape=jax.ShapeDtypeStruct((rows, N), BF16), grid=(R // tr,),
        in_specs=in_specs, out_specs=pl.BlockSpec((tr, N), lambda r: (row_tile + r, 0)),
        input_output_aliases={} if into is None else {2: 0},
        scratch_shapes=[pltpu.VMEM((T, N), BF16), pltpu.SemaphoreType.DMA((GRAD_LOAD_CHUNKS,))],
        compiler_params=_params("arbitrary"), name=name)(*args)


def _rms_fwd(x, gain, name):
    T = x.shape[0]
    tm = _pick_tile(T, 512, 16)

    def body(x_ref, g_ref, u_ref):
        xv = x_ref[...]
        rstd = lax.rsqrt(jnp.mean(xv * xv, axis=-1, keepdims=True) + NORM_EPS)
        u_ref[...] = (xv * rstd * g_ref[...]).astype(BF16)

    return pl.pallas_call(
        body, out_shape=jax.ShapeDtypeStruct((T, D_MODEL), BF16), grid=(T // tm,),
        in_specs=[pl.BlockSpec((tm, D_MODEL), lambda i: (i, 0)), pl.BlockSpec((1, D_MODEL), lambda i: (0, 0))],
        out_specs=pl.BlockSpec((tm, D_MODEL), lambda i: (i, 0)),
        compiler_params=_params("parallel"), name=name)(x, gain)


def _rms_bwd(x, gain, dus, dres, name, dilations=(1,)):
    T = x.shape[0]
    tm = _pick_tile(T, PERM_TILE, 16 * max(dilations))
    n_du = len(dus)

    def body(x_ref, g_ref, *refs):
        du_refs, dres_ref = refs[:n_du], refs[n_du]
        dx_ref, dxb_ref, dg_ref, du_scr = refs[n_du + 1:]

        @pl.when(pl.program_id(0) == 0)
        def _():
            dg_ref[...] = jnp.zeros_like(dg_ref)

        if tuple(dilations) == (1,):
            du = du_refs[0][...]
        else:
            for i, (d, du_ref) in enumerate(zip(dilations, du_refs)):
                for j in range(D_MODEL // LANES):
                    lanes = slice(j * LANES, (j + 1) * LANES)
                    if d == 1:
                        du_scr[j] = du_ref[:, lanes] if i == 0 else du_scr[j] + du_ref[:, lanes]
                        continue
                    blk = du_scr.at[j]
                    for r in range(d):
                        rows = _class_rows(r, d, tm)
                        blk[rows, :] = du_ref[r, :, lanes] if i == 0 else blk[rows, :] + du_ref[r, :, lanes]
            du = jnp.concatenate([du_scr[j] for j in range(D_MODEL // LANES)], axis=1)
        xv = x_ref[...]
        rstd = lax.rsqrt(jnp.mean(xv * xv, axis=-1, keepdims=True) + NORM_EPS)
        n = xv * rstd
        dg_ref[...] += jnp.sum(du * n, axis=0, keepdims=True)
        dn = du * g_ref[...]
        dx = dres_ref[...] + rstd * (dn - n * jnp.mean(dn * n, axis=-1, keepdims=True))
        dx_ref[...] = dx
        dxb_ref[...] = dx.astype(BF16)

    row = pl.BlockSpec((tm, D_MODEL), lambda i: (i, 0))
    vec = pl.BlockSpec((1, D_MODEL), lambda i: (0, 0))
    return pl.pallas_call(
        body,
        out_shape=(jax.ShapeDtypeStruct((T, D_MODEL), F32), jax.ShapeDtypeStruct((T, D_MODEL), BF16),
                   jax.ShapeDtypeStruct((1, D_MODEL), F32)),
        grid=(T // tm,), in_specs=[row, vec] + [_residue_spec(d, tm, D_MODEL) for d in dilations] + [row],
        out_specs=(row, row, vec), scratch_shapes=[pltpu.VMEM((D_MODEL // LANES, tm, LANES), F32)],
        compiler_params=_params("arbitrary"), name=name)(
            x, gain, *[_residue_view(du, d) for du, d in zip(dus, dilations)], dres)


def _loss_head_math(hv, rows, t_ref, g_ref, dh_ref, dhb_ref, dg_ref, loss_ref):
    inv_f = 1.0 / D_MODEL
    g = g_ref[...]
    rstd = lax.rsqrt(jnp.mean(hv * hv, axis=-1, keepdims=True) + NORM_EPS)
    n = hv * rstd
    err = n * g - t_ref[rows, :]
    loss_ref[...] += (0.5 * inv_f) * jnp.sum(err * err, axis=0, keepdims=True)
    dy = err * inv_f
    dg_ref[...] += jnp.sum(dy * n, axis=0, keepdims=True)
    dn = dy * g
    dh = rstd * (dn - n * jnp.mean(dn * n, axis=-1, keepdims=True))
    dh_ref[rows, :] = dh
    dhb_ref[rows, :] = dh.astype(BF16)


FFN_TILE = 256


def _ffn_in(h, gain, w_in, name):
    T = h.shape[0]
    tm = _pick_tile(T, ROW_TILE, 16)

    def body(h_ref, g_ref, w_ref, n_ref, gate_ref, up_ref, a_ref):
        hv = h_ref[...]
        rstd = lax.rsqrt(jnp.mean(hv * hv, axis=-1, keepdims=True) + NORM_EPS)
        n = (hv * rstd * g_ref[...]).astype(BF16)
        n_ref[...] = n
        for c0 in range(0, D_FF, FFN_TILE):
            cols = slice(c0, c0 + FFN_TILE)
            gate = _dot(n, w_ref[c0:c0 + FFN_TILE, :], NT)
            up = _dot(n, w_ref[D_FF + c0:D_FF + c0 + FFN_TILE, :], NT)
            gate_ref[:, cols] = gate.astype(BF16)
            up_ref[:, cols] = up.astype(BF16)
            a_ref[:, cols] = (gate * _sigmoid(gate) * up).astype(BF16)

    row = pl.BlockSpec((tm, D_MODEL), lambda i: (i, 0))
    wide = pl.BlockSpec((tm, D_FF), lambda i: (i, 0))
    wide_shape = jax.ShapeDtypeStruct((T, D_FF), BF16)
    return pl.pallas_call(
        body, out_shape=(jax.ShapeDtypeStruct((T, D_MODEL), BF16), wide_shape, wide_shape, wide_shape),
        grid=(T // tm,),
        in_specs=[row, pl.BlockSpec((1, D_MODEL), lambda i: (0, 0)), _whole((2 * D_FF, D_MODEL), lambda i: (0, 0))],
        out_specs=(row, wide, wide, wide), compiler_params=_params("parallel"), name=name)(h, gain, w_in)


def _ffn_down_dx(dhb, w_down, gate, up, name):
    T = dhb.shape[0]
    tm = _pick_tile(T, ROW_TILE, 16)

    def body(dh_ref, w_ref, gate_ref, up_ref, dgate_ref, dup_ref):
        dh = dh_ref[...]
        for c0 in range(0, D_FF, FFN_TILE):
            cols = slice(c0, c0 + FFN_TILE)
            da = _dot(dh, w_ref[c0:c0 + FFN_TILE, :], NT)
            gate = gate_ref[:, cols].astype(F32)
            sg = _sigmoid(gate)
            dgate_ref[:, cols] = (da * up_ref[:, cols].astype(F32) * (sg * (1.0 + gate * (1.0 - sg)))).astype(BF16)
            dup_ref[:, cols] = (da * gate * sg).astype(BF16)

    wide = pl.BlockSpec((tm, D_FF), lambda i: (i, 0))
    wide_shape = jax.ShapeDtypeStruct((T, D_FF), BF16)
    return pl.pallas_call(
        body, out_shape=(wide_shape, wide_shape), grid=(T // tm,),
        in_specs=[pl.BlockSpec((tm, D_MODEL), lambda i: (i, 0)), _whole((D_FF, D_MODEL), lambda i: (0, 0)), wide, wide],
        out_specs=(wide, wide), compiler_params=_params("parallel"), name=name)(dhb, w_down, gate, up)


def _tri(n, lower):
    r = lax.broadcasted_iota(jnp.int32, (n, n), 0)
    c = lax.broadcasted_iota(jnp.int32, (n, n), 1)
    return (c <= r) if lower else (c >= r)


def _running_sum(x, lower):
    tri = _tri(x.shape[0], lower).astype(BF16)
    hi = x.astype(BF16)
    rest = x - hi.astype(F32)
    mid = rest.astype(BF16)
    lo = (rest - mid.astype(F32)).astype(BF16)
    return _dot(tri, hi, NN) + _dot(tri, mid, NN) + _dot(tri, lo, NN)


def _hgrn_gates(q_raw, f_raw, lb):
    C = q_raw.shape[0]
    sig_f = _sigmoid(f_raw)
    forget = lb + (1.0 - lb) * sig_f
    key = 1.0 - forget
    log_f = jnp.log(forget)
    b = _running_sum(log_f, True)
    first_half = lax.broadcasted_iota(jnp.int32, log_f.shape, 0) < C // 2
    r = jnp.sum(jnp.where(first_half, log_f, 0.0), axis=0, keepdims=True)
    b_last = jnp.sum(log_f, axis=0, keepdims=True)
    e_a = jnp.exp(jnp.minimum(b - r, HGRN_EXP_CLAMP))
    e_b = jnp.exp(jnp.minimum(r - b, HGRN_EXP_CLAMP))
    e_q = jnp.exp(b)
    e_k = jnp.exp(b_last - b)
    sig_q = _sigmoid(q_raw)
    query = q_raw * sig_q
    return dict(sig_f=sig_f, forget=forget, sig_q=sig_q, e_a=e_a, e_b=e_b, e_q=e_q, e_k=e_k,
                e_last=jnp.exp(b_last), q_a=query * e_a, k_b=key * e_b, q_hat=query * e_q, k_til=key * e_k)


def _hgrn_fwd(proj, lb, gain, name):
    T = proj.shape[0]
    C = HGRN_CHUNK
    CPS = HGRN_STEP_CHUNKS
    H, HD = HGRN_HEADS, HGRN_DIM

    def body(q_ref, f_ref, i_ref, g_ref, lb_ref, gain_ref, og_ref, o_ref, st_ref, s_scr):
        @pl.when(pl.program_id(0) == 0)
        def _():
            s_scr[...] = jnp.zeros_like(s_scr)

        causal = _tri(C, True)
        gain_v = gain_ref[...]
        heads = [slice(h * HD, (h + 1) * HD) for h in range(H)]
        s_t = [s_scr[h] for h in range(H)]
        for cc in range(CPS):
            rows = slice(cc * C, (cc + 1) * C)
            for h in range(H):
                st_ref[cc, h] = s_t[h]
            gt = _hgrn_gates(q_ref[rows, :], f_ref[rows, :], lb_ref[...])
            q_a, k_b = gt["q_a"].astype(BF16), gt["k_b"].astype(BF16)
            q_hat, k_til = gt["q_hat"].astype(BF16), gt["k_til"].astype(BF16)
            v = i_ref[rows, :].astype(BF16)
            p = [jnp.where(causal, _dot(q_a[:, sl], k_b[:, sl], NT), 0.0).astype(BF16) for sl in heads]
            o = [_dot(p[h], v[:, sl], NN) + _dot(q_hat[:, sl], s_t[h].astype(BF16), NT)
                 for h, sl in enumerate(heads)]
            s_t = [gt["e_last"][:, sl] * s_t[h] + _dot(v[:, sl], k_til[:, sl], TN) for h, sl in enumerate(heads)]
            for h, sl in enumerate(heads):
                o_ref[rows, sl] = o[h]
                rstd = lax.rsqrt(jnp.mean(o[h] * o[h], axis=-1, keepdims=True) + NORM_EPS)
                g_raw = g_ref[rows, sl]
                og_ref[rows, sl] = (o[h] * rstd * gain_v * (g_raw * _sigmoid(g_raw))).astype(BF16)
        for h in range(H):
            s_scr[h] = s_t[h]

    col = lambda j: pl.BlockSpec((CPS * C, D_MODEL), lambda c: (c, j))
    row = pl.BlockSpec((CPS * C, D_MODEL), lambda c: (c, 0))
    return pl.pallas_call(
        body,
        out_shape=(jax.ShapeDtypeStruct((T, D_MODEL), BF16), jax.ShapeDtypeStruct((T, D_MODEL), F32),
                   jax.ShapeDtypeStruct((T // C, H, HD, HD), F32)),
        grid=(T // (CPS * C),),
        in_specs=[col(0), col(1), col(2), col(3), pl.BlockSpec((1, D_MODEL), lambda c: (0, 0)),
                  pl.BlockSpec((1, HD), lambda c: (0, 0))],
        out_specs=(row, row, pl.BlockSpec((CPS, H, HD, HD), lambda c: (c, 0, 0, 0))),
        scratch_shapes=[pltpu.VMEM((H, HD, HD), F32)],
        compiler_params=_params("arbitrary"), name=name)(proj, proj, proj, proj, lb, gain)


def _hgrn_bwd(proj, o_pre, d_og, states, lb, gain, name):
    T = proj.shape[0]
    C = HGRN_CHUNK
    CPS = HGRN_STEP_CHUNKS
    H, HD = HGRN_HEADS, HGRN_DIM
    NC = T // (CPS * C)

    def body(q_ref, f_ref, i_ref, g_ref, o_ref, dog_ref, st_ref, lb_ref, gain_ref,
             dproj_ref, dlb_ref, dgain_ref, ds_scr, dq_all, dk_all, db_all):
        @pl.when(pl.program_id(0) == 0)
        def _():
            ds_scr[...] = jnp.zeros_like(ds_scr)
            dlb_ref[...] = jnp.zeros_like(dlb_ref)
            dgain_ref[...] = jnp.zeros_like(dgain_ref)

        lbv = lb_ref[...]
        causal = _tri(C, True)
        last_row = lax.broadcasted_iota(jnp.int32, (C, HD), 0) == C - 1
        gain_v = gain_ref[...]
        heads = [slice(h * HD, (h + 1) * HD) for h in range(H)]
        hs = range(H)
        ds_t = [ds_scr[h] for h in hs]
        dgain = None
        for cc in reversed(range(CPS)):
            rows = slice(cc * C, (cc + 1) * C)
            dq_scr, dk_scr, db_scr = dq_all.at[cc], dk_all.at[cc], db_all.at[cc]
            q_raw = q_ref[rows, :]
            gt = _hgrn_gates(q_raw, f_ref[rows, :], lbv)
            o = [o_ref[rows, sl] for sl in heads]
            rstd = [lax.rsqrt(jnp.mean(x * x, axis=-1, keepdims=True) + NORM_EPS) for x in o]
            n = [x * r for x, r in zip(o, rstd)]
            g_raw = [g_ref[rows, sl] for sl in heads]
            sg = [_sigmoid(x) for x in g_raw]
            d_out = [dog_ref[rows, sl] for sl in heads]
            dy = [d * (g * s) for d, g, s in zip(d_out, g_raw, sg)]
            dn = [x * gain_v for x in dy]
            do = [(rstd[h] * (dn[h] - n[h] * jnp.mean(dn[h] * n[h], axis=-1, keepdims=True))).astype(BF16) for h in hs]
            for h in hs:
                dgain = dy[h] * n[h] if dgain is None else dgain + dy[h] * n[h]
            for h, sl in enumerate(heads):
                dproj_ref[rows, 3 * D_MODEL + h * HD:3 * D_MODEL + (h + 1) * HD] = (
                    d_out[h] * n[h] * gain_v * (sg[h] * (1.0 + g_raw[h] * (1.0 - sg[h])))).astype(BF16)
            q_ab, k_bb = gt["q_a"].astype(BF16), gt["k_b"].astype(BF16)
            q_hb, k_tb = gt["q_hat"].astype(BF16), gt["k_til"].astype(BF16)
            v = i_ref[rows, :].astype(BF16)
            s_t = [st_ref[cc, h] for h in hs]
            ds_b = [x.astype(BF16) for x in ds_t]
            p = [jnp.where(causal, _dot(q_ab[:, sl], k_bb[:, sl], NT), 0.0).astype(BF16) for sl in heads]
            dp = [jnp.where(causal, _dot(do[h], v[:, sl], NT), 0.0).astype(BF16) for h, sl in enumerate(heads)]
            dv = [_dot(p[h], do[h], TN) + _dot(k_tb[:, sl], ds_b[h], NT) for h, sl in enumerate(heads)]
            dq_a = [_dot(dp[h], k_bb[:, sl], NN) for h, sl in enumerate(heads)]
            dk_b = [_dot(dp[h], q_ab[:, sl], TN) for h, sl in enumerate(heads)]
            dq_hat = [_dot(do[h], s_t[h].astype(BF16), NN) for h in hs]
            dk_til = [_dot(v[:, sl], ds_b[h], NN) for h, sl in enumerate(heads)]
            ds_new = [_dot(do[h], q_hb[:, sl], TN) + gt["e_last"][:, sl] * ds_t[h] for h, sl in enumerate(heads)]
            for h, sl in enumerate(heads):
                k_til = gt["k_til"][:, sl]
                db_last = jnp.sum(ds_t[h] * gt["e_last"][:, sl] * s_t[h], axis=0, keepdims=True) + jnp.sum(
                    dk_til[h] * k_til, axis=0, keepdims=True)
                dproj_ref[rows, 2 * D_MODEL + h * HD:2 * D_MODEL + (h + 1) * HD] = dv[h].astype(BF16)
                dq_scr[:, sl] = dq_a[h] * gt["e_a"][:, sl] + dq_hat[h] * gt["e_q"][:, sl]
                dk_scr[:, sl] = dk_b[h] * gt["e_b"][:, sl] + dk_til[h] * gt["e_k"][:, sl]
                db = (dq_a[h] * q_ab[:, sl].astype(F32) + dq_hat[h] * gt["q_hat"][:, sl]
                      - dk_b[h] * k_bb[:, sl].astype(F32) - dk_til[h] * k_til)
                db_scr[:, sl] = db + jnp.where(last_row, db_last, 0.0)
            dlogf = _running_sum(db_scr[...], False)
            sig_f, forget, sig_q = gt["sig_f"], gt["forget"], gt["sig_q"]
            dforget = dlogf / forget - dk_scr[...]
            dproj_ref[rows, D_MODEL:2 * D_MODEL] = (dforget * (1.0 - lbv) * sig_f * (1.0 - sig_f)).astype(BF16)
            dlb_ref[...] += jnp.sum(dforget * (1.0 - sig_f), axis=0, keepdims=True)
            dproj_ref[rows, 0:D_MODEL] = (dq_scr[...] * (sig_q * (1.0 + q_raw * (1.0 - sig_q)))).astype(BF16)
            ds_t = ds_new
        dgain_ref[...] += jnp.sum(dgain, axis=0, keepdims=True)
        for h in hs:
            ds_scr[h] = ds_t[h]

    col = lambda j: pl.BlockSpec((CPS * C, D_MODEL), lambda c: (NC - 1 - c, j))
    row = pl.BlockSpec((CPS * C, D_MODEL), lambda c: (NC - 1 - c, 0))
    return pl.pallas_call(
        body,
        out_shape=(jax.ShapeDtypeStruct((T, 4 * D_MODEL), BF16), jax.ShapeDtypeStruct((1, D_MODEL), F32),
                   jax.ShapeDtypeStruct((1, HD), F32)),
        grid=(NC,),
        in_specs=[col(0), col(1), col(2), col(3), row, row,
                  pl.BlockSpec((CPS, H, HD, HD), lambda c: (NC - 1 - c, 0, 0, 0)),
                  pl.BlockSpec((1, D_MODEL), lambda c: (0, 0)), pl.BlockSpec((1, HD), lambda c: (0, 0))],
        out_specs=(pl.BlockSpec((CPS * C, 4 * D_MODEL), lambda c: (NC - 1 - c, 0)),
                   pl.BlockSpec((1, D_MODEL), lambda c: (0, 0)), pl.BlockSpec((1, HD), lambda c: (0, 0))),
        scratch_shapes=[pltpu.VMEM((H, HD, HD), F32)] + [pltpu.VMEM((CPS, C, D_MODEL), F32)] * 3,
        compiler_params=_params("arbitrary"), name=name)(proj, proj, proj, proj, o_pre, d_og, states, lb, gain)


def _attn_masks():
    r = lax.broadcasted_iota(jnp.int32, (ATTN_BLOCK, ATTN_BLOCK), 0)
    c = lax.broadcasted_iota(jnp.int32, (ATTN_BLOCK, ATTN_BLOCK), 1)
    return c >= r, c <= r


def _attn_fwd(qkv, dilation, name):
    T = qkv.shape[0]
    nb = T // dilation // ATTN_BLOCK
    W = ATTN_GROUP_WIDTH
    B = ATTN_BLOCK
    scale = ATTN_DIM ** -0.5
    qb = 2 if nb % 2 == 0 else 1
    steps = nb // qb

    def body(q_ref, kp_ref, kc_ref, vp_ref, vc_ref, o_ref, lse_ref):
        no_prev = jnp.where(pl.program_id(1) > 0, 0.0, NEG_BIG)
        m_prev, m_cur = _attn_masks()
        ones = jnp.ones((B, ATTN_DIM), BF16)
        items = []
        for j in range(qb):
            for h in range(ATTN_GROUP_HEADS):
                sl = slice(h * ATTN_DIM, (h + 1) * ATTN_DIM)
                rows = slice(j * B, (j + 1) * B)
                if j == 0:
                    items.append((rows, sl, kp_ref[:, sl], vp_ref[:, sl], no_prev))
                else:
                    before = slice((j - 1) * B, j * B)
                    items.append((rows, sl, kc_ref[before, sl], vc_ref[before, sl], 0.0))
        s_p = [jnp.where(m_prev, _dot(q_ref[rows, sl], k_p, NT) * scale + bias, NEG_BIG)
               for rows, sl, k_p, _, bias in items]
        s_c = [jnp.where(m_cur, _dot(q_ref[rows, sl], kc_ref[rows, sl], NT) * scale, NEG_BIG)
               for rows, sl, _, _, _ in items]
        m = [jnp.max(jnp.maximum(a, b), axis=-1, keepdims=True) for a, b in zip(s_p, s_c)]
        p_p = [jnp.exp(a - mx).astype(BF16) for a, mx in zip(s_p, m)]
        p_c = [jnp.exp(b - mx).astype(BF16) for b, mx in zip(s_c, m)]
        l = [_dot(a, ones, NN) + _dot(b, ones, NN) for a, b in zip(p_p, p_c)]
        acc = [_dot(a, v_p, NN) + _dot(b, vc_ref[rows, sl], NN)
               for a, b, (rows, sl, _, v_p, _) in zip(p_p, p_c, items)]
        for (rows, sl, _, _, _), a, lv, mx in zip(items, acc, l, m):
            o_ref[rows, sl] = (a / lv).astype(BF16)
            lse_ref[rows, sl] = mx + jnp.log(lv)

    cur = lambda col: pl.BlockSpec((qb * B, W), lambda s, n: (s * steps + n, col))
    prev = lambda col: pl.BlockSpec((B, W), lambda s, n: (s * nb + jnp.maximum(qb * n - 1, 0), col))
    out = pl.BlockSpec((qb * B, W), lambda s, n: (s * steps + n, 0))
    return pl.pallas_call(
        body, out_shape=(jax.ShapeDtypeStruct((T, W), BF16), jax.ShapeDtypeStruct((T, W), F32)),
        grid=(dilation, steps),
        in_specs=[cur(0), prev(1), cur(1), prev(2), cur(2)],
        out_specs=(out, out), compiler_params=_params("parallel", "arbitrary"), name=name)(qkv, qkv, qkv, qkv, qkv)


def _attn_bwd(qkv, d_out, lse, delta, cos, sin, dilation, name):
    T = qkv.shape[0]
    nb = T // dilation // ATTN_BLOCK
    assert nb % 2 == 0, "an even number of 128-token blocks per residue class"
    pairs = nb // 2
    W = ATTN_GROUP_WIDTH
    B = ATTN_BLOCK
    scale = ATTN_DIM ** -0.5

    def unrope(x, cos_v, sin_v):
        return x * cos_v + pltpu.roll(x * sin_v, ATTN_DIM // 2, 1)

    def body(qa_ref, qb_ref, kpair_ref, kc_ref, vpair_ref, vc_ref, doa_ref, dob_ref, lsea_ref, lseb_ref,
             dla_ref, dlb_ref, cos_ref, sin_ref, out_ref, dq_scr, dk_scr, dv_scr):
        n = pl.program_id(1)

        @pl.when(n == 0)
        def _():
            dq_scr[...] = jnp.zeros_like(dq_scr)
            dk_scr[...] = jnp.zeros_like(dk_scr)
            dv_scr[...] = jnp.zeros_like(dv_scr)

        no_a = jnp.where(n > 0, 0.0, NEG_BIG)
        no_b = jnp.where(n < pairs, 0.0, NEG_BIG)
        m_prev, m_cur = _attn_masks()
        lo, hi = slice(0, B), slice(B, 2 * B)
        heads = [slice(h * ATTN_DIM, (h + 1) * ATTN_DIM) for h in range(ATTN_GROUP_HEADS)]
        flat = []
        for sl in heads:
            qa, qb = qa_ref[:, sl], qb_ref[:, sl]
            doa, dob = doa_ref[:, sl], dob_ref[:, sl]
            k0, k1, k2 = kpair_ref[lo, sl], kpair_ref[hi, sl], kc_ref[:, sl]
            v0, v1, v2 = vpair_ref[lo, sl], vpair_ref[hi, sl], vc_ref[:, sl]
            flat += [(qa, doa, lsea_ref[:, sl], dla_ref[:, sl], k0, v0, m_prev, no_a),
                     (qa, doa, lsea_ref[:, sl], dla_ref[:, sl], k1, v1, m_cur, no_a),
                     (qb, dob, lseb_ref[:, sl], dlb_ref[:, sl], k1, v1, m_prev, no_a + no_b),
                     (qb, dob, lseb_ref[:, sl], dlb_ref[:, sl], k2, v2, m_cur, no_b)]
        s = [_dot(q, k, NT) for q, _, _, _, k, _, _, _ in flat]
        dp = [_dot(do, v, NT) for _, do, _, _, _, v, _, _ in flat]
        p = [jnp.where(mask, jnp.exp(sv * scale - lse_v + bias), 0.0)
             for sv, (_, _, lse_v, _, _, _, mask, bias) in zip(s, flat)]
        ds = [(pv * (dpv - dl_v) * scale).astype(BF16) for pv, dpv, (_, _, _, dl_v, _, _, _, _) in zip(p, dp, flat)]
        p = [pv.astype(BF16) for pv in p]
        dq_part = [_dot(dsv, k, NN) for dsv, (_, _, _, _, k, _, _, _) in zip(ds, flat)]
        dk_part = [_dot(dsv, q, TN) for dsv, (q, _, _, _, _, _, _, _) in zip(ds, flat)]
        dv_part = [_dot(pv, do, TN) for pv, (_, do, _, _, _, _, _, _) in zip(p, flat)]
        cos_lo, sin_lo, cos_hi, sin_hi = cos_ref[lo, :], sin_ref[lo, :], cos_ref[hi, :], sin_ref[hi, :]
        for h, sl in enumerate(heads):
            a_prev, a_cur, b_prev, b_cur = range(4 * h, 4 * h + 4)
            kcol = slice(W + h * ATTN_DIM, W + (h + 1) * ATTN_DIM)
            vcol = slice(2 * W + h * ATTN_DIM, 2 * W + (h + 1) * ATTN_DIM)
            out_ref[lo, sl] = unrope(dq_scr[:, sl], cos_lo, sin_lo).astype(BF16)
            out_ref[hi, sl] = unrope(dq_part[a_prev] + dq_part[a_cur], cos_hi, sin_hi).astype(BF16)
            out_ref[lo, kcol] = unrope(dk_scr[:, sl] + dk_part[a_prev], cos_lo, sin_lo).astype(BF16)
            out_ref[hi, kcol] = unrope(dk_part[a_cur] + dk_part[b_prev], cos_hi, sin_hi).astype(BF16)
            out_ref[lo, vcol] = (dv_scr[:, sl] + dv_part[a_prev]).astype(BF16)
            out_ref[hi, vcol] = (dv_part[a_cur] + dv_part[b_prev]).astype(BF16)
            dq_scr[:, sl] = dq_part[b_prev] + dq_part[b_cur]
            dk_scr[:, sl] = dk_part[b_cur]
            dv_scr[:, sl] = dv_part[b_cur]

    def block_a(n):
        return jnp.maximum(2 * n - 1, 0)

    def block_b(n):
        return jnp.minimum(2 * n, nb - 1)

    def pair(n):
        return jnp.maximum(n - 1, 0)

    one_a = lambda col: pl.BlockSpec((B, W), lambda s, n: (s * nb + block_a(n), col))
    one_b = lambda col: pl.BlockSpec((B, W), lambda s, n: (s * nb + block_b(n), col))
    two = lambda col: pl.BlockSpec((2 * B, W), lambda s, n: (s * pairs + pair(n), col))
    tab = pl.BlockSpec((2 * B, ATTN_DIM), lambda s, n: (s * pairs + pair(n), 0))
    return pl.pallas_call(
        body, out_shape=jax.ShapeDtypeStruct((T, 3 * W), BF16), grid=(dilation, pairs + 1),
        in_specs=[one_a(0), one_b(0), two(1), one_b(1), two(2), one_b(2), one_a(0), one_b(0), one_a(0), one_b(0),
                  one_a(0), one_b(0), tab, tab],
        out_specs=pl.BlockSpec((2 * B, 3 * W), lambda s, n: (s * pairs + pair(n), 0)),
        scratch_shapes=[pltpu.VMEM((B, W), F32)] * 3,
        compiler_params=_params("parallel", "arbitrary"), name=name)(
            qkv, qkv, qkv, qkv, qkv, qkv, d_out, d_out, lse, lse, delta, delta, cos, sin)


PERM_TILE = 512
LANES = 128


def _residue_view(x, d):
    return x if d == 1 else x.reshape(d, x.shape[0] // d, x.shape[1])


def _residue_spec(d, tm, cols):
    if d == 1:
        return pl.BlockSpec((tm, cols), lambda i: (i, 0))
    return pl.BlockSpec((d, tm // d, cols), lambda i: (0, i, 0))


def _residue_shape(T, d, cols, dtype):
    return jax.ShapeDtypeStruct((T, cols) if d == 1 else (d, T // d, cols), dtype)


def _class_rows(r, d, tm):
    return pl.ds(r, tm // d, stride=d)


def _attn_norm(h, gain, name):
    T = h.shape[0]
    tm = _pick_tile(T, PERM_TILE, 16 * max(ATTN_DILATIONS))
    dils = ATTN_DILATIONS
    (base_cos, base_sin), (off_cos, off_sin), sign = _rope_parts(T, tm)

    def body(h_ref, g_ref, bc_ref, bs_ref, oc_ref, os_ref, sign_ref, *refs):
        u_refs, c_refs, s_refs, u_scr, c_scr, s_scr = refs[0:3], refs[3:6], refs[6:9], refs[9], refs[10], refs[11]
        hv = h_ref[...]
        rstd = lax.rsqrt(jnp.mean(hv * hv, axis=-1, keepdims=True) + NORM_EPS)
        u = hv * rstd * g_ref[...]
        for j in range(D_MODEL // LANES):
            u_scr[j] = u[:, j * LANES:(j + 1) * LANES]
        bc, bs, oc, osn = bc_ref[0], bs_ref[0], oc_ref[...], os_ref[...]
        c_scr[...] = bc * oc - bs * osn
        s_scr[...] = (bs * oc + bc * osn) * sign_ref[...]
        for d, u_ref, c_ref, s_ref in zip(dils, u_refs, c_refs, s_refs):
            if d == 1:
                u_ref[...] = u.astype(BF16)
                c_ref[...] = c_scr[...]
                s_ref[...] = s_scr[...]
                continue
            for r in range(d):
                rows = _class_rows(r, d, tm)
                for j in range(D_MODEL // LANES):
                    u_ref[r, :, j * LANES:(j + 1) * LANES] = u_scr.at[j][rows, :].astype(BF16)
                c_ref[r] = c_scr[rows, :]
                s_ref[r] = s_scr[rows, :]

    row = pl.BlockSpec((tm, D_MODEL), lambda i: (i, 0))
    base = pl.BlockSpec((1, 1, ATTN_DIM), lambda i: (i, 0, 0))
    off = pl.BlockSpec((tm, ATTN_DIM), lambda i: (0, 0))
    res = pl.pallas_call(
        body,
        out_shape=([_residue_shape(T, d, D_MODEL, BF16) for d in dils]
                   + [_residue_shape(T, d, ATTN_DIM, F32) for d in dils] * 2),
        grid=(T // tm,),
        in_specs=[row, pl.BlockSpec((1, D_MODEL), lambda i: (0, 0)), base, base, off, off,
                  pl.BlockSpec((1, ATTN_DIM), lambda i: (0, 0))],
        out_specs=([_residue_spec(d, tm, D_MODEL) for d in dils] + [_residue_spec(d, tm, ATTN_DIM) for d in dils] * 2),
        scratch_shapes=[pltpu.VMEM((D_MODEL // LANES, tm, LANES), F32), pltpu.VMEM((tm, ATTN_DIM), F32),
                        pltpu.VMEM((tm, ATTN_DIM), F32)],
        compiler_params=_params("parallel"), name=name)(h, gain, base_cos, base_sin, off_cos, off_sin, sign)
    flat = [r.reshape(T, r.shape[-1]) for r in res]
    return flat[0:3], flat[3:6], flat[6:9]


def _attn_merge_fwd(outs, lses, name):
    T = outs[0].shape[0]
    W = ATTN_GROUP_WIDTH
    tm = _pick_tile(T, PERM_TILE, 16 * max(ATTN_DILATIONS))
    dils = ATTN_DILATIONS

    def body(*refs):
        o_refs, l_refs, oc_ref, lse_refs = refs[0:3], refs[3:6], refs[6], refs[7:10]
        o_scr, l_scr, t_scr = refs[10:13]
        nh = ATTN_GROUP_HEADS
        for g, d in enumerate(dils):
            for j in range(nh):
                lanes = slice(j * LANES, (j + 1) * LANES)
                if d == 1:
                    o_scr[g * nh + j] = o_refs[g][:, lanes].astype(F32)
                    l_scr[g * nh + j] = l_refs[g][:, lanes]
                    continue
                for r in range(d):
                    rows = _class_rows(r, d, tm)
                    o_scr.at[g * nh + j][rows, :] = o_refs[g][r, :, lanes].astype(F32)
                    l_scr.at[g * nh + j][rows, :] = l_refs[g][r, :, lanes]
        for j in range(nh):
            lanes = slice(j * LANES, (j + 1) * LANES)
            ls = [l_scr[g * nh + j] for g in range(3)]
            m = jnp.maximum(jnp.maximum(ls[0], ls[1]), ls[2])
            tot = m + jnp.log(jnp.exp(ls[0] - m) + jnp.exp(ls[1] - m) + jnp.exp(ls[2] - m))
            t_scr[j] = tot
            for g, d in enumerate(dils):
                oc_ref[:, g * W + j * LANES:g * W + (j + 1) * LANES] = (
                    o_scr[g * nh + j] * jnp.exp(ls[g] - tot)).astype(BF16)
                if d == 1:
                    lse_refs[g][:, lanes] = tot
                    continue
                for r in range(d):
                    lse_refs[g][r, :, lanes] = t_scr.at[j][_class_rows(r, d, tm), :]

    in_blk = [_residue_spec(d, tm, W) for d in dils]
    n_blk = 3 * ATTN_GROUP_HEADS
    res = pl.pallas_call(
        body, out_shape=[jax.ShapeDtypeStruct((T, 3 * W), BF16)] + [_residue_shape(T, d, W, F32) for d in dils],
        grid=(T // tm,), in_specs=in_blk * 2,
        out_specs=[pl.BlockSpec((tm, 3 * W), lambda i: (i, 0))] + in_blk,
        scratch_shapes=[pltpu.VMEM((n_blk, tm, LANES), F32), pltpu.VMEM((n_blk, tm, LANES), F32),
                        pltpu.VMEM((ATTN_GROUP_HEADS, tm, LANES), F32)],
        compiler_params=_params("parallel"), name=name)(
            *[_residue_view(o, d) for o, d in zip(outs, dils)], *[_residue_view(l, d) for l, d in zip(lses, dils)])
    return res[0], [r.reshape(T, W) for r in res[1:]]


def _attn_merge_bwd(d_oc, oc, name):
    T = d_oc.shape[0]
    W = ATTN_GROUP_WIDTH
    tm = _pick_tile(T, PERM_TILE, 16 * max(ATTN_DILATIONS))
    dils = ATTN_DILATIONS

    def body(d_ref, o_ref, *refs):
        delta_refs, db_refs, dl_scr, d_scr = refs[0:3], refs[3:6], refs[6], refs[7]
        nh = ATTN_GROUP_HEADS
        for j in range(nh):
            tot = jnp.zeros((tm, 1), F32)
            for g in range(3):
                cols = slice(g * W + j * LANES, g * W + (j + 1) * LANES)
                d_blk = d_ref[:, cols]
                d_scr[g * nh + j] = d_blk
                tot = tot + jnp.sum(d_blk * o_ref[:, cols].astype(F32), axis=-1, keepdims=True)
            dl_scr[j] = jnp.broadcast_to(tot, (tm, LANES))
        for g, d in enumerate(dils):
            for j in range(nh):
                lanes = slice(j * LANES, (j + 1) * LANES)
                if d == 1:
                    delta_refs[g][:, lanes] = dl_scr[j]
                    db_refs[g][:, lanes] = d_scr[g * nh + j].astype(BF16)
                    continue
                for r in range(d):
                    rows = _class_rows(r, d, tm)
                    delta_refs[g][r, :, lanes] = dl_scr.at[j][rows, :]
                    db_refs[g][r, :, lanes] = d_scr.at[g * nh + j][rows, :].astype(BF16)

    wide = pl.BlockSpec((tm, 3 * W), lambda i: (i, 0))
    out_blk = [_residue_spec(d, tm, W) for d in dils]
    res = pl.pallas_call(
        body, out_shape=[_residue_shape(T, d, W, F32) for d in dils] + [_residue_shape(T, d, W, BF16) for d in dils],
        grid=(T // tm,), in_specs=[wide, wide], out_specs=out_blk * 2,
        scratch_shapes=[pltpu.VMEM((ATTN_GROUP_HEADS, tm, LANES), F32),
                        pltpu.VMEM((3 * ATTN_GROUP_HEADS, tm, LANES), F32)],
        compiler_params=_params("parallel"), name=name)(d_oc, oc)
    flat = [r.reshape(T, W) for r in res]
    return flat[0:3], flat[3:6]


def _rope_parts(T, tile):
    inv_freq = 1.0 / (ROPE_THETA ** (jnp.arange(0, ATTN_DIM, 2, dtype=F32) / ATTN_DIM))
    inv_freq = jnp.concatenate([inv_freq, inv_freq])[None, :]
    base = (jnp.arange(T // tile, dtype=F32) * tile)[:, None] * inv_freq
    off = jnp.arange(tile, dtype=F32)[:, None] * inv_freq
    sign = jnp.concatenate([-jnp.ones((1, ATTN_DIM // 2), F32), jnp.ones((1, ATTN_DIM // 2), F32)], axis=1)
    return (jnp.cos(base)[:, None, :], jnp.sin(base)[:, None, :]), (jnp.cos(off), jnp.sin(off)), sign


WEIGHT_GROUPS = {"hgrn": ("hgrn_in", "hgrn_out"), "ffn0": ("ffn_in0", "ffn_down0"),
                 "attn": ("qkv", "attn_out"), "ffn1": ("ffn_in1", "ffn_down1")}


def _local_step(x, target, norm_mix, norm_ffn, lb, out_gain, final_gain, fetch, publish):
    g_mix = [norm_mix[0:1], norm_mix[1:2]]
    g_ffn = [norm_ffn[0:1], norm_ffn[1:2]]
    w = {}

    def whole(name):
        return [(w[name], w[name].shape[0], 0)]

    def qkv_parts(g):
        return [(w["qkv"], ATTN_GROUP_WIDTH, 3 * j + g) for j in range(3)]

    def ffn_fwd(h, layer, head=None):
        w.update(fetch(f"ffn{layer}"))
        n, gate, up, a = _ffn_in(h, g_ffn[layer], w[f"ffn_in{layer}"], f"ffn{layer}_in")
        out = _mm_nn([a], [whole(f"ffn_down{layer}")], h, name=f"ffn{layer}_down", head=head)
        return out, (n, gate, up, a)

    def ffn_bwd(h, saved, dh, dhb, layer):
        n, gate, up, a = saved
        w_in = w[f"ffn_in{layer}"]
        dgate, dup = _ffn_down_dx(dhb, w[f"ffn_down{layer}"], gate, up, f"ffn{layer}_down_dx")
        grad_in = _mm_tn(dgate, n, name=f"ffn{layer}_in_dw_gate", rows=2 * D_FF)
        grad_in = _mm_tn(dup, n, name=f"ffn{layer}_in_dw_up", into=grad_in, row_tile=D_FF // GRAD_TILE, rows=2 * D_FF)
        grads = {f"ffn_down{layer}": _mm_tn(a, dhb, name=f"ffn{layer}_down_dw"), f"ffn_in{layer}": grad_in}
        publish(f"ffn{layer}", grads)
        return _mm_nn([dgate, dup], [[(w_in, D_FF, 0)], [(w_in, D_FF, 1)]], dh, name=f"ffn{layer}_in_dx",
                      norm=(h, g_ffn[layer]))

    u0 = _rms_fwd(x, g_mix[0], "hgrn_norm")
    w.update(fetch("hgrn"))
    proj = _mm_nt(u0, whole("hgrn_in"), out_dtype=F32, name="hgrn_in")
    og, o_pre, states = _hgrn_fwd(proj, lb, out_gain, "hgrn_fwd")
    h1 = _mm_nn([og], [whole("hgrn_out")], x, name="hgrn_out")
    h2, ffn0 = ffn_fwd(h1, 0)

    u1_g, cos_g, sin_g = _attn_norm(h2, g_mix[1], "attn_norm")
    w.update(fetch("attn"))
    qkv_g, outs, lses = [], [], []
    for g, d in enumerate(ATTN_DILATIONS):
        qkv_g.append(_mm_nt(u1_g[g], qkv_parts(g), out_dtype=BF16, name=f"attn_qkv{g}",
                            rope=(cos_g[g], sin_g[g], 2)))
        o_g, lse_g = _attn_fwd(qkv_g[g], d, f"attn_fwd{g}")
        outs.append(o_g)
        lses.append(lse_g)
    oc, lse_all = _attn_merge_fwd(outs, lses, "attn_merge")
    h3 = _mm_nn([oc], [whole("attn_out")], h2, name="attn_out")
    (dh4, dh4b, d_final, loss_part), ffn1 = ffn_fwd(h3, 1, head=(target, final_gain))
    dh3, dh3b, d_ffn1 = ffn_bwd(h3, ffn1, dh4, dh4b, 1)

    d_oc = _mm_nt(dh3b, whole("attn_out"), out_dtype=F32, name="attn_out_dx")
    grad_attn_out = _mm_tn(oc, dh3b, name="attn_out_dw")
    delta, d_ocb = _attn_merge_bwd(d_oc, oc, "attn_merge_bwd")
    du1, qkv_pieces = [], []
    for g, d in enumerate(ATTN_DILATIONS):
        dqkv = _attn_bwd(qkv_g[g], d_ocb[g], lse_all[g], delta[g], cos_g[g], sin_g[g], d, f"attn_bwd{g}")
        qkv_pieces.append(_mm_tn(dqkv, u1_g[g], name=f"attn_qkv_dw{g}"))
        du1.append(_mm_nn([dqkv], [qkv_parts(g)], None, name=f"attn_qkv_dx{g}"))
    grad_qkv = jnp.stack([p.reshape(3, ATTN_GROUP_WIDTH, D_MODEL) for p in qkv_pieces], axis=1).reshape(
        3 * ATTN_WIDTH, D_MODEL)
    publish("attn", {"qkv": grad_qkv, "attn_out": grad_attn_out})
    dh2, dh2b, d_mix1 = _rms_bwd(h2, g_mix[1], du1, dh3, "attn_norm_bwd", ATTN_DILATIONS)

    dh1, dh1b, d_ffn0 = ffn_bwd(h1, ffn0, dh2, dh2b, 0)

    d_og = _mm_nt(dh1b, whole("hgrn_out"), out_dtype=F32, name="hgrn_out_dx")
    grad_hgrn_out = _mm_tn(og, dh1b, name="hgrn_out_dw")
    dproj, d_lb, d_out_gain = _hgrn_bwd(proj, o_pre, d_og, states, lb, out_gain, "hgrn_bwd")
    publish("hgrn", {"hgrn_in": _mm_tn(dproj, u0, name="hgrn_in_dw"), "hgrn_out": grad_hgrn_out})
    dx, _, d_mix0 = _mm_nn([dproj], [whole("hgrn_in")], dh1, name="hgrn_in_dx", norm=(x, g_mix[0]))

    small = dict(norm_mix0=d_mix0, norm_mix1=d_mix1, norm_ffn0=d_ffn0, norm_ffn1=d_ffn1, lb=d_lb,
                 out_gain=d_out_gain, final=d_final, loss=loss_part)
    return dx, small


WEIGHT_NAMES = ("hgrn_in", "hgrn_out", "qkv", "attn_out", "ffn_in0", "ffn_in1", "ffn_down0", "ffn_down1")
MESH_IDS = pl.DeviceIdType.MESH
HBM_SPEC = pl.BlockSpec(memory_space=pl.ANY)


N_PEERS = N_DEV - 1
PEER_OFFSETS = [(dx, dy, dc) for dx in (0, 1) for dy in (0, 1) for dc in (0, 1)][1:]


def _mesh_place():
    x, y, c = lax.axis_index("x"), lax.axis_index("y"), lax.axis_index("c")
    peers = []
    for dx, dy, dc in PEER_OFFSETS:
        px, py, pc = (1 - x if dx else x), (1 - y if dy else y), (1 - c if dc else c)
        peers.append(((px, py, pc), 4 * px + 2 * py + pc))
    return 4 * x + 2 * y + c, peers


def _gather_over_two_levels(src_refs, land_refs, send_sems, recv_sems):
    n = len(src_refs)
    x, y, c = lax.axis_index("x"), lax.axis_index("y"), lax.axis_index("c")
    me, sibling = (x, y, c), (x, y, 1 - c)
    chips = [(1 - x, y), (x, 1 - y), (1 - x, 1 - y)]

    def block(w, px, py, pc):
        return land_refs[w].at[4 * px + 2 * py + pc]

    def copy(w, k, owner, to, src=None):
        return pltpu.make_async_remote_copy(
            src_ref=block(w, *owner) if src is None else src, dst_ref=block(w, *owner),
            send_sem=send_sems.at[w * N_PEERS + k], recv_sem=recv_sems.at[w * N_PEERS + k],
            device_id=to, device_id_type=MESH_IDS)

    sent = []
    for w in range(n):
        sent.append(copy(w, 0, me, sibling, src=src_refs[w]))
        sent += [copy(w, 1 + j, me, (*chip, c), src=src_refs[w]) for j, chip in enumerate(chips)]
    for cp in sent:
        cp.start()
    for w in range(n):
        for j, chip in enumerate(chips):
            copy(w, 1 + j, (*chip, c), me).wait_recv()
            passed = copy(w, 4 + j, (*chip, c), sibling)
            passed.start()
            sent.append(passed)
    for w in range(n):
        copy(w, 0, sibling, me).wait_recv()
        for j, chip in enumerate(chips):
            copy(w, 4 + j, (*chip, 1 - c), me).wait_recv()
    for cp in sent:
        cp.wait_send()


def _exchange_launch(srcs, scatter, collective_id, name):
    n = len(srcs)
    src_refs = [jax.new_ref(s, memory_space=pltpu.MemorySpace.HBM) for s in srcs]
    land_refs = [jax.empty_ref(jax.ShapeDtypeStruct(s.shape if scatter else (N_DEV,) + s.shape, s.dtype),
                               memory_space=pltpu.MemorySpace.HBM) for s in srcs]

    @pl.kernel(mesh=plsc.ScalarSubcoreMesh(axis_name="sequencer", num_cores=1), name=name,
               scratch_types=(pltpu.SemaphoreType.DMA((n * N_PEERS,)), pltpu.SemaphoreType.DMA((n * N_PEERS,)),
                              pltpu.SemaphoreType.DMA((n,))),
               compiler_params=pltpu.CompilerParams(collective_id=collective_id))
    def launch(send_sems, recv_sems, local_sems):
        me, peers = _mesh_place()
        barrier = pltpu.get_barrier_semaphore()
        for peer, _ in peers:
            pl.semaphore_signal(barrier, inc=1, device_id=peer, device_id_type=MESH_IDS)
        pl.semaphore_wait(barrier, N_PEERS)
        own = [pltpu.make_async_copy(src_refs[w].at[me] if scatter else src_refs[w], land_refs[w].at[me],
                                     local_sems.at[w]) for w in range(n)]
        for cp in own:
            cp.start()
        if scatter:
            copies = [pltpu.make_async_remote_copy(
                src_ref=src_refs[w].at[pid], dst_ref=land_refs[w].at[me],
                send_sem=send_sems.at[w * N_PEERS + k], recv_sem=recv_sems.at[w * N_PEERS + k],
                device_id=peer, device_id_type=MESH_IDS) for w in range(n) for k, (peer, pid) in enumerate(peers)]
            for cp in copies:
                cp.start()
            for cp in copies:
                cp.wait()
        else:
            _gather_over_two_levels(src_refs, land_refs, send_sems, recv_sems)
        for cp in own:
            cp.wait()

    launch()
    return land_refs


def _gather_small(block, name):
    def body(in_ref, out_ref, send_sems, recv_sems, local_sem):
        me, peers = _mesh_place()
        own = pltpu.make_async_copy(in_ref, out_ref.at[me], local_sem)
        own.start()
        sends = [pltpu.make_async_remote_copy(
            src_ref=in_ref, dst_ref=out_ref.at[me], send_sem=send_sems.at[k], recv_sem=recv_sems.at[k],
            device_id=peer, device_id_type=MESH_IDS) for k, (peer, _) in enumerate(peers)]
        for cp in sends:
            cp.start()
        for cp in sends:
            cp.wait_recv()
        for cp in sends:
            cp.wait_send()
        own.wait()

    return pl.pallas_call(
        body, out_shape=jax.ShapeDtypeStruct((N_DEV,) + block.shape, block.dtype),
        in_specs=[HBM_SPEC], out_specs=HBM_SPEC,
        scratch_shapes=[pltpu.SemaphoreType.DMA((N_PEERS,)), pltpu.SemaphoreType.DMA((N_PEERS,)),
                        pltpu.SemaphoreType.DMA],
        name=name)(block)


def _sum_blocks(recv, name):
    rows = recv.shape[1]
    tr = _pick_tile(rows, 256, 16)

    def body(r_ref, g_ref):
        acc = r_ref[0].astype(F32)
        for j in range(1, N_DEV):
            acc = acc + r_ref[j].astype(F32)
        g_ref[...] = acc

    return pl.pallas_call(
        body, out_shape=jax.ShapeDtypeStruct((rows, D_MODEL), F32), grid=(rows // tr,),
        in_specs=[pl.BlockSpec((N_DEV, tr, D_MODEL), lambda i: (0, i, 0))],
        out_specs=pl.BlockSpec((tr, D_MODEL), lambda i: (i, 0)),
        compiler_params=_params("parallel"), name=name)(recv)


def _adamw_math(w, g, m, v):
    m_new = ADAM_B1 * m + (1.0 - ADAM_B1) * g
    v_new = ADAM_B2 * v + (1.0 - ADAM_B2) * (g * g)
    m_hat = m_new / (1.0 - ADAM_B1 ** ADAM_STEP)
    v_hat = v_new / (1.0 - ADAM_B2 ** ADAM_STEP)
    delta = -ADAM_LR * (m_hat / (jnp.sqrt(v_hat) + ADAM_EPS) + ADAM_WD * w)
    return delta, m_new, v_new


def _adamw(w, g, m, v, name):
    rows, cols = w.shape
    tr = _pick_tile(rows, 256, 8)

    def body(w_ref, g_ref, m_ref, v_ref, d_ref, mo_ref, vo_ref):
        d_ref[...], mo_ref[...], vo_ref[...] = _adamw_math(w_ref[...], g_ref[...], m_ref[...], v_ref[...])

    blk = pl.BlockSpec((tr, cols), lambda i: (i, 0))
    return pl.pallas_call(
        body, out_shape=(jax.ShapeDtypeStruct((rows, cols), F32),) * 3, grid=(rows // tr,),
        in_specs=[blk] * 4, out_specs=(blk,) * 3, compiler_params=_params("parallel"), name=name)(w, g, m, v)


ROW_MIX, ROW_FFN, ROW_LB, ROW_OUT_GAIN, ROW_FINAL = 0, 2, 4, 7, 8
PART_MIX, PART_FFN, PART_LB, PART_OUT_GAIN, PART_FINAL, PART_LOSS = 0, 2, 4, 5, 6, 7


def _small_update(parts_all, w, m, v, name):
    def body(p_ref, w_ref, m_ref, v_ref, g_ref, d_ref, mo_ref, vo_ref, loss_ref):
        def total(row, n=1):
            tot = p_ref[0, row:row + n, :]
            for j in range(1, N_DEV):
                tot = tot + p_ref[j, row:row + n, :]
            return tot

        logits = [w_ref[ROW_LB + i:ROW_LB + i + 1, :] for i in range(3)]
        mx = jnp.maximum(jnp.maximum(logits[0], logits[1]), logits[2])
        ex = [jnp.exp(l - mx) for l in logits]
        den = ex[0] + ex[1] + ex[2]
        prob = [e / den for e in ex]
        d_lb = total(PART_LB)
        g_ref[...] = jnp.zeros_like(g_ref)
        g_ref[ROW_MIX:ROW_MIX + 2, :] = total(PART_MIX, 2)
        g_ref[ROW_FFN:ROW_FFN + 2, :] = total(PART_FFN, 2)
        for i in range(3):
            g_ref[ROW_LB + i:ROW_LB + i + 1, :] = prob[i] * ((d_lb if i == 0 else 0.0) - prob[0] * d_lb)
        g_ref[ROW_OUT_GAIN:ROW_OUT_GAIN + 1, :] = total(PART_OUT_GAIN)
        g_ref[ROW_FINAL:ROW_FINAL + 1, :] = total(PART_FINAL)
        d_ref[...], mo_ref[...], vo_ref[...] = _adamw_math(w_ref[...], g_ref[...], m_ref[...], v_ref[...])
        loss_ref[...] = jnp.sum(total(PART_LOSS), axis=-1, keepdims=True)

    packed = jax.ShapeDtypeStruct((16, D_MODEL), F32)
    return pl.pallas_call(
        body, out_shape=(packed, packed, packed, packed, jax.ShapeDtypeStruct((1, 1), F32)),
        compiler_params=pltpu.CompilerParams(vmem_limit_bytes=VMEM_LIMIT), name=name)(parts_all, w, m, v)


def _pack_small(norm_mix, norm_ffn, lb_logits, out_gain, final):
    pad = jnp.zeros((1, D_MODEL - HGRN_DIM), F32)
    return jnp.concatenate([norm_mix, norm_ffn, lb_logits, jnp.concatenate([out_gain, pad], axis=1),
                            final.reshape(1, D_MODEL), jnp.zeros((16 - ROW_FINAL - 1, D_MODEL), F32)], axis=0)


def _unpack_small(p):
    return (p[ROW_MIX:ROW_MIX + 2], p[ROW_FFN:ROW_FFN + 2], p[ROW_LB:ROW_LB + 3],
            p[ROW_OUT_GAIN:ROW_OUT_GAIN + 1, :HGRN_DIM], p[ROW_FINAL])


def _lower_bound(lb_logits, name):
    def body(l_ref, o_ref):
        logits = [l_ref[i:i + 1, :] for i in range(3)]
        mx = jnp.maximum(jnp.maximum(logits[0], logits[1]), logits[2])
        ex = [jnp.exp(l - mx) for l in logits]
        o_ref[...] = ex[0] / (ex[0] + ex[1] + ex[2])

    return pl.pallas_call(body, out_shape=jax.ShapeDtypeStruct((1, D_MODEL), F32), name=name)(lb_logits)


def kernel(x, norm_mix, norm_ffn, hgrn_w_in, hgrn_lb_logits, hgrn_out_norm, hgrn_w_out, attn_w_qkv, attn_w_out, ffn_w_in, ffn_w_down, final_norm, loss_target, m_norm_mix, m_norm_ffn, m_hgrn_w_in, m_hgrn_lb_logits, m_hgrn_out_norm, m_hgrn_w_out, m_attn_w_qkv, m_attn_w_out, m_ffn_w_in, m_ffn_w_down, m_final_norm, v_norm_mix, v_norm_ffn, v_hgrn_w_in, v_hgrn_lb_logits, v_hgrn_out_norm, v_hgrn_w_out, v_attn_w_qkv, v_attn_w_out, v_ffn_w_in, v_ffn_w_down, v_final_norm):
    col_sharded = {"hgrn_in": hgrn_w_in[0], "qkv": attn_w_qkv[0], "ffn_in0": ffn_w_in[0], "ffn_in1": ffn_w_in[1]}
    row_sharded = {"hgrn_out": hgrn_w_out[0], "attn_out": attn_w_out[0], "ffn_down0": ffn_w_down[0],
                   "ffn_down1": ffn_w_down[1]}
    gathering = {}
    for gi, (group, names) in enumerate(WEIGHT_GROUPS.items()):
        shards = [(col_sharded[n].T if n in col_sharded else row_sharded[n]).astype(BF16) for n in names]
        gathering[group] = _exchange_launch(shards, False, 1 + gi, f"weights_gather_{group}")

    def fetch(group):
        return {n: land[...].reshape(-1, D_MODEL) for n, land in zip(WEIGHT_GROUPS[group], gathering[group])}

    in_flight = {}

    def publish(group, grads):
        names = WEIGHT_GROUPS[group]
        parts = [grads[n].reshape(N_DEV, -1, D_MODEL) for n in names]
        in_flight[group] = _exchange_launch(parts, True, 1 + len(WEIGHT_GROUPS) + list(WEIGHT_GROUPS).index(group),
                                            f"grads_send_{group}")

    lb = _lower_bound(hgrn_lb_logits, "hgrn_lower_bound")
    grad_x, small = _local_step(x[0], loss_target[0], norm_mix, norm_ffn, lb, hgrn_out_norm,
                                final_norm.reshape(1, D_MODEL), fetch, publish)

    pad = jnp.zeros((1, D_MODEL - HGRN_DIM), F32)
    small_part = jnp.concatenate(
        [small["norm_mix0"], small["norm_mix1"], small["norm_ffn0"], small["norm_ffn1"], small["lb"],
         jnp.concatenate([small["out_gain"], pad], axis=1), small["final"], small["loss"]], axis=0)
    small_all = _gather_small(small_part, "small_grads_gather")
    received = {}
    for group in ("ffn1", "attn", "ffn0", "hgrn"):
        received.update(zip(WEIGHT_GROUPS[group], [land[...] for land in in_flight[group]]))

    masters = {"hgrn_in": (hgrn_w_in[0], m_hgrn_w_in[0], v_hgrn_w_in[0]),
               "hgrn_out": (hgrn_w_out[0], m_hgrn_w_out[0], v_hgrn_w_out[0]),
               "qkv": (attn_w_qkv[0], m_attn_w_qkv[0], v_attn_w_qkv[0]),
               "attn_out": (attn_w_out[0], m_attn_w_out[0], v_attn_w_out[0]),
               "ffn_in0": (ffn_w_in[0], m_ffn_w_in[0], v_ffn_w_in[0]),
               "ffn_in1": (ffn_w_in[1], m_ffn_w_in[1], v_ffn_w_in[1]),
               "ffn_down0": (ffn_w_down[0], m_ffn_w_down[0], v_ffn_w_down[0]),
               "ffn_down1": (ffn_w_down[1], m_ffn_w_down[1], v_ffn_w_down[1])}
    res = {}
    for n in WEIGHT_NAMES:
        g = _sum_blocks(received[n], f"{n}_grad_sum")
        if n in col_sharded:
            g = g.T
        wv, mv, vv = masters[n]
        res[n] = (g,) + tuple(_adamw(wv, g, mv, vv, f"{n}_adamw"))

    def single(n):
        return [t[None] for t in res[n]]

    def pair(n):
        return [jnp.stack([a, b]) for a, b in zip(res[n + "0"], res[n + "1"])]

    big = dict(hgrn_w_in=single("hgrn_in"), hgrn_w_out=single("hgrn_out"), attn_w_qkv=single("qkv"),
               attn_w_out=single("attn_out"), ffn_w_in=pair("ffn_in"), ffn_w_down=pair("ffn_down"))

    w_small = _pack_small(norm_mix, norm_ffn, hgrn_lb_logits, hgrn_out_norm, final_norm)
    m_small = _pack_small(m_norm_mix, m_norm_ffn, m_hgrn_lb_logits, m_hgrn_out_norm, m_final_norm)
    v_small = _pack_small(v_norm_mix, v_norm_ffn, v_hgrn_lb_logits, v_hgrn_out_norm, v_final_norm)
    g_s, d_s, m_s, v_s, loss = _small_update(small_all, w_small, m_small, v_small, "small_update")
    small_out = [_unpack_small(t) for t in (g_s, d_s, m_s, v_s)]

    def group(i):
        s = small_out[i]
        return (s[0], s[1], big["hgrn_w_in"][i], s[2], s[3], big["hgrn_w_out"][i], big["attn_w_qkv"][i],
                big["attn_w_out"][i], big["ffn_w_in"][i], big["ffn_w_down"][i], s[4])

    return (loss.reshape(()), grad_x[None], *group(0), *group(1), *group(2), *group(3))
```

```python
import functools

import jax
import jax.numpy as jnp
from jax import lax
from jax.experimental import pallas as pl
from jax.experimental.pallas import tpu as pltpu
from jax.experimental.pallas import tpu_sc as plsc

F32 = jnp.float32
BF16 = jnp.bfloat16

D_MODEL = 1024
N_DEV = 8
NORM_EPS = 1e-6

HGRN_HEADS = 8
HGRN_DIM = 128
HGRN_CHUNK = 64
HGRN_STEP_CHUNKS = 2
HGRN_EXP_CLAMP = 60.0

ATTN_DIM = 128
ATTN_BLOCK = 128
ATTN_GROUP_HEADS = 4
ATTN_GROUP_WIDTH = ATTN_GROUP_HEADS * ATTN_DIM
ATTN_DILATIONS = (1, 4, 16)
ATTN_WIDTH = 3 * ATTN_GROUP_WIDTH
ROPE_THETA = 10000.0
NEG_BIG = -1e30

D_FF = 2816

ADAM_LR = 0.001
ADAM_B1 = 0.9
ADAM_B2 = 0.999
ADAM_EPS = 1e-08
ADAM_WD = 0.01
ADAM_STEP = 10

VMEM_LIMIT = 48 * 1024 * 1024

NT = (((1,), (1,)), ((), ()))
NN = (((1,), (0,)), ((), ()))
TN = (((0,), (0,)), ((), ()))


def _dot(a, b, dims):
    return lax.dot_general(a, b, dims, preferred_element_type=F32)


def _params(*sem):
    return pltpu.CompilerParams(dimension_semantics=sem, vmem_limit_bytes=VMEM_LIMIT)


def _pick_tile(n, cap, mult):
    best = None
    for t in range(mult, min(n, cap) + 1, mult):
        if n % t == 0:
            best = t
    assert best is not None, (n, cap, mult)
    return best


def _sigmoid(x):
    return 0.5 * jnp.tanh(0.5 * x) + 0.5


ROW_TILE = 512
COL_CHUNK = 512
GRAD_TILE = 256


def _whole(shape, index_map):
    return pl.BlockSpec(shape, index_map, pipeline_mode=pl.Buffered(1))


def _part_specs(parts, n_cols):
    return [_whole((rows, n_cols), functools.partial(lambda i, b: (b, 0), b=blk)) for _, rows, blk in parts]


def _mm_nt(a, w_parts, *, out_dtype, name, rope=None):
    M, K = a.shape
    tm = _pick_tile(M, ROW_TILE, 16)
    widths = [rows for _, rows, _ in w_parts]
    n_parts = len(w_parts)

    def body(*refs):
        a_ref, w_refs, o_ref = refs[0], refs[1:1 + n_parts], refs[-1]
        av = a_ref[...]
        off = 0
        for p, w_ref in enumerate(w_refs):
            for c0 in range(0, widths[p], COL_CHUNK):
                cw = min(COL_CHUNK, widths[p] - c0)
                acc = _dot(av, w_ref[c0:c0 + cw, :], NT)
                if rope is not None and p < rope[2]:
                    cos, sin = refs[1 + n_parts][...], refs[2 + n_parts][...]
                    for h0 in range(0, cw, ATTN_DIM):
                        xh = acc[:, h0:h0 + ATTN_DIM]
                        rot = pltpu.roll(xh, ATTN_DIM // 2, 1)
                        o_ref[:, off + c0 + h0:off + c0 + h0 + ATTN_DIM] = (xh * cos + rot * sin).astype(out_dtype)
                else:
                    o_ref[:, off + c0:off + c0 + cw] = acc.astype(out_dtype)
            off += widths[p]

    in_specs = [pl.BlockSpec((tm, K), lambda i: (i, 0))] + _part_specs(w_parts, K)
    args = [a] + [w for w, _, _ in w_parts]
    if rope is not None:
        in_specs += [pl.BlockSpec((tm, ATTN_DIM), lambda i: (i, 0))] * 2
        args += [rope[0], rope[1]]
    return pl.pallas_call(
        body, out_shape=jax.ShapeDtypeStruct((M, sum(widths)), out_dtype), grid=(M // tm,),
        in_specs=in_specs, out_specs=pl.BlockSpec((tm, sum(widths)), lambda i: (i, 0)),
        compiler_params=_params("parallel"), name=name)(*args)


def _mm_nn(a_list, w_parts_list, resid, *, name, norm=None, head=None):
    M = a_list[0].shape[0]
    tm = _pick_tile(M, ROW_TILE, 16)
    n_a = len(a_list)
    flat_parts = [p for parts in w_parts_list for p in parts]
    extra = norm if norm is not None else head
    n_in = n_a + len(flat_parts) + (1 if resid is not None else 0) + (2 if extra is not None else 0)

    def body(*refs):
        a_refs, w_refs = refs[:n_a], refs[n_a:n_a + len(flat_parts)]

        def product(rows):
            acc = None
            wi = 0
            for a_ref, parts in zip(a_refs, w_parts_list):
                off = 0
                for _, k, _ in parts:
                    term = _dot(a_ref[rows, off:off + k], w_refs[wi][...], NN)
                    acc = term if acc is None else acc + term
                    off += k
                    wi += 1
            return acc

        if extra is None:
            acc = product(slice(None))
            if resid is not None:
                acc = acc + refs[n_in - 1][...]
            refs[n_in][...] = acc
            return

        @pl.when(pl.program_id(0) == 0)
        def _():
            for acc_ref in refs[n_in + 2:]:
                acc_ref[...] = jnp.zeros_like(acc_ref)

        for r0 in range(0, tm, tm // 2):
            rows = slice(r0, r0 + tm // 2)
            acc = product(rows)
            if head is not None:
                _loss_head_math(acc + refs[n_in - 3][rows, :], rows, refs[n_in - 2], refs[n_in - 1],
                                *refs[n_in:n_in + 4])
                continue
            dres_ref, x_ref, g_ref = refs[n_in - 3:n_in]
            dx_ref, dxb_ref, dg_ref = refs[n_in:n_in + 3]
            xv = x_ref[rows, :]
            rstd = lax.rsqrt(jnp.mean(xv * xv, axis=-1, keepdims=True) + NORM_EPS)
            n = xv * rstd
            dg_ref[...] += jnp.sum(acc * n, axis=0, keepdims=True)
            dn = acc * g_ref[...]
            dx = dres_ref[rows, :] + rstd * (dn - n * jnp.mean(dn * n, axis=-1, keepdims=True))
            dx_ref[rows, :] = dx
            dxb_ref[rows, :] = dx.astype(BF16)

    row = pl.BlockSpec((tm, D_MODEL), lambda i: (i, 0))
    vec = pl.BlockSpec((1, D_MODEL), lambda i: (0, 0))
    in_specs = [pl.BlockSpec((tm, a.shape[1]), lambda i: (i, 0)) for a in a_list] + _part_specs(flat_parts, D_MODEL)
    args = list(a_list) + [w for w, _, _ in flat_parts]
    if resid is not None:
        in_specs.append(row)
        args.append(resid)
    if extra is None:
        return pl.pallas_call(
            body, out_shape=jax.ShapeDtypeStruct((M, D_MODEL), F32), grid=(M // tm,),
            in_specs=in_specs, out_specs=row, compiler_params=_params("parallel"), name=name)(*args)
    assert resid is not None
    out_shape = [jax.ShapeDtypeStruct((M, D_MODEL), F32), jax.ShapeDtypeStruct((M, D_MODEL), BF16),
                 jax.ShapeDtypeStruct((1, D_MODEL), F32)]
    out_specs = [row, row, vec]
    if head is not None:
        out_shape.append(jax.ShapeDtypeStruct((1, D_MODEL), F32))
        out_specs.append(vec)
    return pl.pallas_call(
        body, out_shape=out_shape, grid=(M // tm,), in_specs=in_specs + [row, vec], out_specs=out_specs,
        compiler_params=_params("arbitrary"), name=name)(*args, extra[0], extra[1])


def _mm_tn(a_list, b, *, name):
    T = a_list[0].shape[0]
    N = b.shape[1]
    tr = GRAD_TILE
    tiles = [a.shape[1] // tr for a in a_list]
    starts = [sum(tiles[:i]) for i in range(len(tiles))]

    def body(*refs):
        a_refs, b_ref, o_ref = refs[:len(a_list)], refs[len(a_list)], refs[-1]
        r = pl.program_id(0)
        for a_ref, first, count in zip(a_refs, starts, tiles):
            @pl.when(jnp.logical_and(r >= first, r < first + count))
            def _():
                o_ref[...] = _dot(a_ref[...], b_ref[...], TN).astype(BF16)

    in_specs = [pl.BlockSpec((T, tr), functools.partial(lambda r, first, count: (0, jnp.clip(r - first, 0, count - 1)),
                                                        first=first, count=count))
                for first, count in zip(starts, tiles)]
    in_specs.append(_whole((T, N), lambda r: (0, 0)))
    return pl.pallas_call(
        body, out_shape=jax.ShapeDtypeStruct((sum(tiles) * tr, N), BF16), grid=(sum(tiles),),
        in_specs=in_specs, out_specs=pl.BlockSpec((tr, N), lambda r: (r, 0)),
        compiler_params=_params("parallel"), name=name)(*a_list, b)


def _rms_fwd(x, gain, name):
    T = x.shape[0]
    tm = _pick_tile(T, 512, 16)

    def body(x_ref, g_ref, u_ref):
        xv = x_ref[...]
        rstd = lax.rsqrt(jnp.mean(xv * xv, axis=-1, keepdims=True) + NORM_EPS)
        u_ref[...] = (xv * rstd * g_ref[...]).astype(BF16)

    return pl.pallas_call(
        body, out_shape=jax.ShapeDtypeStruct((T, D_MODEL), BF16), grid=(T // tm,),
        in_specs=[pl.BlockSpec((tm, D_MODEL), lambda i: (i, 0)), pl.BlockSpec((1, D_MODEL), lambda i: (0, 0))],
        out_specs=pl.BlockSpec((tm, D_MODEL), lambda i: (i, 0)),
        compiler_params=_params("parallel"), name=name)(x, gain)


def _rms_bwd(x, gain, dus, dres, name, dilations=(1,)):
    T = x.shape[0]
    tm = _pick_tile(T, PERM_TILE, 16 * max(dilations))
    n_du = len(dus)

    def body(x_ref, g_ref, *refs):
        du_refs, dres_ref = refs[:n_du], refs[n_du]
        dx_ref, dxb_ref, dg_ref, du_scr = refs[n_du + 1:]

        @pl.when(pl.program_id(0) == 0)
        def _():
            dg_ref[...] = jnp.zeros_like(dg_ref)

        if tuple(dilations) == (1,):
            du = du_refs[0][...]
        else:
            for i, (d, du_ref) in enumerate(zip(dilations, du_refs)):
                for j in range(D_MODEL // LANES):
                    lanes = slice(j * LANES, (j + 1) * LANES)
                    if d == 1:
                        du_scr[j] = du_ref[:, lanes] if i == 0 else du_scr[j] + du_ref[:, lanes]
                        continue
                    blk = du_scr.at[j]
                    for r in range(d):
                        rows = _class_rows(r, d, tm)
                        blk[rows, :] = du_ref[r, :, lanes] if i == 0 else blk[rows, :] + du_ref[r, :, lanes]
            du = jnp.concatenate([du_scr[j] for j in range(D_MODEL // LANES)], axis=1)
        xv = x_ref[...]
        rstd = lax.rsqrt(jnp.mean(xv * xv, axis=-1, keepdims=True) + NORM_EPS)
        n = xv * rstd
        dg_ref[...] += jnp.sum(du * n, axis=0, keepdims=True)
        dn = du * g_ref[...]
        dx = dres_ref[...] + rstd * (dn - n * jnp.mean(dn * n, axis=-1, keepdims=True))
        dx_ref[...] = dx
        dxb_ref[...] = dx.astype(BF16)

    row = pl.BlockSpec((tm, D_MODEL), lambda i: (i, 0))
    vec = pl.BlockSpec((1, D_MODEL), lambda i: (0, 0))
    return pl.pallas_call(
        body,
        out_shape=(jax.ShapeDtypeStruct((T, D_MODEL), F32), jax.ShapeDtypeStruct((T, D_MODEL), BF16),
                   jax.ShapeDtypeStruct((1, D_MODEL), F32)),
        grid=(T // tm,), in_specs=[row, vec] + [_residue_spec(d, tm, D_MODEL) for d in dilations] + [row],
        out_specs=(row, row, vec), scratch_shapes=[pltpu.VMEM((D_MODEL // LANES, tm, LANES), F32)],
        compiler_params=_params("arbitrary"), name=name)(
            x, gain, *[_residue_view(du, d) for du, d in zip(dus, dilations)], dres)


def _loss_head_math(hv, rows, t_ref, g_ref, dh_ref, dhb_ref, dg_ref, loss_ref):
    inv_f = 1.0 / D_MODEL
    g = g_ref[...]
    rstd = lax.rsqrt(jnp.mean(hv * hv, axis=-1, keepdims=True) + NORM_EPS)
    n = hv * rstd
    err = n * g - t_ref[rows, :]
    loss_ref[...] += (0.5 * inv_f) * jnp.sum(err * err, axis=0, keepdims=True)
    dy = err * inv_f
    dg_ref[...] += jnp.sum(dy * n, axis=0, keepdims=True)
    dn = dy * g
    dh = rstd * (dn - n * jnp.mean(dn * n, axis=-1, keepdims=True))
    dh_ref[rows, :] = dh
    dhb_ref[rows, :] = dh.astype(BF16)


FFN_TILE = 256


def _ffn_in(h, gain, w_in, name):
    T = h.shape[0]
    tm = _pick_tile(T, ROW_TILE, 16)

    def body(h_ref, g_ref, w_ref, n_ref, gate_ref, up_ref, a_ref):
        hv = h_ref[...]
        rstd = lax.rsqrt(jnp.mean(hv * hv, axis=-1, keepdims=True) + NORM_EPS)
        n = (hv * rstd * g_ref[...]).astype(BF16)
        n_ref[...] = n
        for c0 in range(0, D_FF, FFN_TILE):
            cols = slice(c0, c0 + FFN_TILE)
            gate = _dot(n, w_ref[c0:c0 + FFN_TILE, :], NT)
            up = _dot(n, w_ref[D_FF + c0:D_FF + c0 + FFN_TILE, :], NT)
            gate_ref[:, cols] = gate.astype(BF16)
            up_ref[:, cols] = up.astype(BF16)
            a_ref[:, cols] = (gate * _sigmoid(gate) * up).astype(BF16)

    row = pl.BlockSpec((tm, D_MODEL), lambda i: (i, 0))
    wide = pl.BlockSpec((tm, D_FF), lambda i: (i, 0))
    wide_shape = jax.ShapeDtypeStruct((T, D_FF), BF16)
    return pl.pallas_call(
        body, out_shape=(jax.ShapeDtypeStruct((T, D_MODEL), BF16), wide_shape, wide_shape, wide_shape),
        grid=(T // tm,),
        in_specs=[row, pl.BlockSpec((1, D_MODEL), lambda i: (0, 0)), _whole((2 * D_FF, D_MODEL), lambda i: (0, 0))],
        out_specs=(row, wide, wide, wide), compiler_params=_params("parallel"), name=name)(h, gain, w_in)


def _ffn_down_dx(dhb, w_down, gate, up, name):
    T = dhb.shape[0]
    tm = _pick_tile(T, ROW_TILE, 16)

    def body(dh_ref, w_ref, gate_ref, up_ref, dgate_ref, dup_ref):
        dh = dh_ref[...]
        for c0 in range(0, D_FF, FFN_TILE):
            cols = slice(c0, c0 + FFN_TILE)
            da = _dot(dh, w_ref[c0:c0 + FFN_TILE, :], NT)
            gate = gate_ref[:, cols].astype(F32)
            sg = _sigmoid(gate)
            dgate_ref[:, cols] = (da * up_ref[:, cols].astype(F32) * (sg * (1.0 + gate * (1.0 - sg)))).astype(BF16)
            dup_ref[:, cols] = (da * gate * sg).astype(BF16)

    wide = pl.BlockSpec((tm, D_FF), lambda i: (i, 0))
    wide_shape = jax.ShapeDtypeStruct((T, D_FF), BF16)
    return pl.pallas_call(
        body, out_shape=(wide_shape, wide_shape), grid=(T // tm,),
        in_specs=[pl.BlockSpec((tm, D_MODEL), lambda i: (i, 0)), _whole((D_FF, D_MODEL), lambda i: (0, 0)), wide, wide],
        out_specs=(wide, wide), compiler_params=_params("parallel"), name=name)(dhb, w_down, gate, up)


def _tri(n, lower):
    r = lax.broadcasted_iota(jnp.int32, (n, n), 0)
    c = lax.broadcasted_iota(jnp.int32, (n, n), 1)
    return (c <= r) if lower else (c >= r)


def _running_sum(x, lower):
    tri = _tri(x.shape[0], lower).astype(BF16)
    hi = x.astype(BF16)
    rest = x - hi.astype(F32)
    mid = rest.astype(BF16)
    lo = (rest - mid.astype(F32)).astype(BF16)
    return _dot(tri, hi, NN) + _dot(tri, mid, NN) + _dot(tri, lo, NN)


def _hgrn_gates(q_raw, f_raw, lb):
    C = q_raw.shape[0]
    sig_f = _sigmoid(f_raw)
    forget = lb + (1.0 - lb) * sig_f
    key = 1.0 - forget
    log_f = jnp.log(forget)
    b = _running_sum(log_f, True)
    first_half = lax.broadcasted_iota(jnp.int32, log_f.shape, 0) < C // 2
    r = jnp.sum(jnp.where(first_half, log_f, 0.0), axis=0, keepdims=True)
    b_last = jnp.sum(log_f, axis=0, keepdims=True)
    e_a = jnp.exp(jnp.minimum(b - r, HGRN_EXP_CLAMP))
    e_b = jnp.exp(jnp.minimum(r - b, HGRN_EXP_CLAMP))
    e_q = jnp.exp(b)
    e_k = jnp.exp(b_last - b)
    sig_q = _sigmoid(q_raw)
    query = q_raw * sig_q
    return dict(sig_f=sig_f, forget=forget, sig_q=sig_q, e_a=e_a, e_b=e_b, e_q=e_q, e_k=e_k,
                e_last=jnp.exp(b_last), q_a=query * e_a, k_b=key * e_b, q_hat=query * e_q, k_til=key * e_k)


def _hgrn_fwd(proj, lb, gain, name):
    T = proj.shape[0]
    C = HGRN_CHUNK
    CPS = HGRN_STEP_CHUNKS
    H, HD = HGRN_HEADS, HGRN_DIM

    def body(q_ref, f_ref, i_ref, g_ref, lb_ref, gain_ref, og_ref, o_ref, st_ref, s_scr):
        @pl.when(pl.program_id(0) == 0)
        def _():
            s_scr[...] = jnp.zeros_like(s_scr)

        causal = _tri(C, True)
        gain_v = gain_ref[...]
        heads = [slice(h * HD, (h + 1) * HD) for h in range(H)]
        s_t = [s_scr[h] for h in range(H)]
        for cc in range(CPS):
            rows = slice(cc * C, (cc + 1) * C)
            for h in range(H):
                st_ref[cc, h] = s_t[h]
            gt = _hgrn_gates(q_ref[rows, :], f_ref[rows, :], lb_ref[...])
            q_a, k_b = gt["q_a"].astype(BF16), gt["k_b"].astype(BF16)
            q_hat, k_til = gt["q_hat"].astype(BF16), gt["k_til"].astype(BF16)
            v = i_ref[rows, :].astype(BF16)
            p = [jnp.where(causal, _dot(q_a[:, sl], k_b[:, sl], NT), 0.0).astype(BF16) for sl in heads]
            o = [_dot(p[h], v[:, sl], NN) + _dot(q_hat[:, sl], s_t[h].astype(BF16), NT)
                 for h, sl in enumerate(heads)]
            s_t = [gt["e_last"][:, sl] * s_t[h] + _dot(v[:, sl], k_til[:, sl], TN) for h, sl in enumerate(heads)]
            for h, sl in enumerate(heads):
                o_ref[rows, sl] = o[h]
                rstd = lax.rsqrt(jnp.mean(o[h] * o[h], axis=-1, keepdims=True) + NORM_EPS)
                g_raw = g_ref[rows, sl]
                og_ref[rows, sl] = (o[h] * rstd * gain_v * (g_raw * _sigmoid(g_raw))).astype(BF16)
        for h in range(H):
            s_scr[h] = s_t[h]

    col = lambda j: pl.BlockSpec((CPS * C, D_MODEL), lambda c: (c, j))
    row = pl.BlockSpec((CPS * C, D_MODEL), lambda c: (c, 0))
    return pl.pallas_call(
        body,
        out_shape=(jax.ShapeDtypeStruct((T, D_MODEL), BF16), jax.ShapeDtypeStruct((T, D_MODEL), F32),
                   jax.ShapeDtypeStruct((T // C, H, HD, HD), F32)),
        grid=(T // (CPS * C),),
        in_specs=[col(0), col(1), col(2), col(3), pl.BlockSpec((1, D_MODEL), lambda c: (0, 0)),
                  pl.BlockSpec((1, HD), lambda c: (0, 0))],
        out_specs=(row, row, pl.BlockSpec((CPS, H, HD, HD), lambda c: (c, 0, 0, 0))),
        scratch_shapes=[pltpu.VMEM((H, HD, HD), F32)],
        compiler_params=_params("arbitrary"), name=name)(proj, proj, proj, proj, lb, gain)


def _hgrn_bwd(proj, o_pre, d_og, states, lb, gain, name):
    T = proj.shape[0]
    C = HGRN_CHUNK
    CPS = HGRN_STEP_CHUNKS
    H, HD = HGRN_HEADS, HGRN_DIM
    NC = T // (CPS * C)

    def body(q_ref, f_ref, i_ref, g_ref, o_ref, dog_ref, st_ref, lb_ref, gain_ref,
             dproj_ref, dlb_ref, dgain_ref, ds_scr, dq_all, dk_all, db_all):
        @pl.when(pl.program_id(0) == 0)
        def _():
            ds_scr[...] = jnp.zeros_like(ds_scr)
            dlb_ref[...] = jnp.zeros_like(dlb_ref)
            dgain_ref[...] = jnp.zeros_like(dgain_ref)

        lbv = lb_ref[...]
        causal = _tri(C, True)
        last_row = lax.broadcasted_iota(jnp.int32, (C, HD), 0) == C - 1
        gain_v = gain_ref[...]
        heads = [slice(h * HD, (h + 1) * HD) for h in range(H)]
        hs = range(H)
        ds_t = [ds_scr[h] for h in hs]
        dgain = None
        for cc in reversed(range(CPS)):
            rows = slice(cc * C, (cc + 1) * C)
            dq_scr, dk_scr, db_scr = dq_all.at[cc], dk_all.at[cc], db_all.at[cc]
            q_raw = q_ref[rows, :]
            gt = _hgrn_gates(q_raw, f_ref[rows, :], lbv)
            o = [o_ref[rows, sl] for sl in heads]
            rstd = [lax.rsqrt(jnp.mean(x * x, axis=-1, keepdims=True) + NORM_EPS) for x in o]
            n = [x * r for x, r in zip(o, rstd)]
            g_raw = [g_ref[rows, sl] for sl in heads]
            sg = [_sigmoid(x) for x in g_raw]
            d_out = [dog_ref[rows, sl] for sl in heads]
            dy = [d * (g * s) for d, g, s in zip(d_out, g_raw, sg)]
            dn = [x * gain_v for x in dy]
            do = [(rstd[h] * (dn[h] - n[h] * jnp.mean(dn[h] * n[h], axis=-1, keepdims=True))).astype(BF16) for h in hs]
            for h in hs:
                dgain = dy[h] * n[h] if dgain is None else dgain + dy[h] * n[h]
            for h, sl in enumerate(heads):
                dproj_ref[rows, 3 * D_MODEL + h * HD:3 * D_MODEL + (h + 1) * HD] = (
                    d_out[h] * n[h] * gain_v * (sg[h] * (1.0 + g_raw[h] * (1.0 - sg[h])))).astype(BF16)
            q_ab, k_bb = gt["q_a"].astype(BF16), gt["k_b"].astype(BF16)
            q_hb, k_tb = gt["q_hat"].astype(BF16), gt["k_til"].astype(BF16)
            v = i_ref[rows, :].astype(BF16)
            s_t = [st_ref[cc, h] for h in hs]
            ds_b = [x.astype(BF16) for x in ds_t]
            p = [jnp.where(causal, _dot(q_ab[:, sl], k_bb[:, sl], NT), 0.0).astype(BF16) for sl in heads]
            dp = [jnp.where(causal, _dot(do[h], v[:, sl], NT), 0.0).astype(BF16) for h, sl in enumerate(heads)]
            dv = [_dot(p[h], do[h], TN) + _dot(k_tb[:, sl], ds_b[h], NT) for h, sl in enumerate(heads)]
            dq_a = [_dot(dp[h], k_bb[:, sl], NN) for h, sl in enumerate(heads)]
            dk_b = [_dot(dp[h], q_ab[:, sl], TN) for h, sl in enumerate(heads)]
            dq_hat = [_dot(do[h], s_t[h].astype(BF16), NN) for h in hs]
            dk_til = [_dot(v[:, sl], ds_b[h], NN) for h, sl in enumerate(heads)]
            ds_new = [_dot(do[h], q_hb[:, sl], TN) + gt["e_last"][:, sl] * ds_t[h] for h, sl in enumerate(heads)]
            for h, sl in enumerate(heads):
                k_til = gt["k_til"][:, sl]
                db_last = jnp.sum(ds_t[h] * gt["e_last"][:, sl] * s_t[h], axis=0, keepdims=True) + jnp.sum(
                    dk_til[h] * k_til, axis=0, keepdims=True)
                dproj_ref[rows, 2 * D_MODEL + h * HD:2 * D_MODEL + (h + 1) * HD] = dv[h].astype(BF16)
                dq_scr[:, sl] = dq_a[h] * gt["e_a"][:, sl] + dq_hat[h] * gt["e_q"][:, sl]
                dk_scr[:, sl] = dk_b[h] * gt["e_b"][:, sl] + dk_til[h] * gt["e_k"][:, sl]
                db = (dq_a[h] * q_ab[:, sl].astype(F32) + dq_hat[h] * gt["q_hat"][:, sl]
                      - dk_b[h] * k_bb[:, sl].astype(F32) - dk_til[h] * k_til)
                db_scr[:, sl] = db + jnp.where(last_row, db_last, 0.0)
            dlogf = _running_sum(db_scr[...], False)
            sig_f, forget, sig_q = gt["sig_f"], gt["forget"], gt["sig_q"]
            dforget = dlogf / forget - dk_scr[...]
            dproj_ref[rows, D_MODEL:2 * D_MODEL] = (dforget * (1.0 - lbv) * sig_f * (1.0 - sig_f)).astype(BF16)
            dlb_ref[...] += jnp.sum(dforget * (1.0 - sig_f), axis=0, keepdims=True)
            dproj_ref[rows, 0:D_MODEL] = (dq_scr[...] * (sig_q * (1.0 + q_raw * (1.0 - sig_q)))).astype(BF16)
            ds_t = ds_new
        dgain_ref[...] += jnp.sum(dgain, axis=0, keepdims=True)
        for h in hs:
            ds_scr[h] = ds_t[h]

    col = lambda j: pl.BlockSpec((CPS * C, D_MODEL), lambda c: (NC - 1 - c, j))
    row = pl.BlockSpec((CPS * C, D_MODEL), lambda c: (NC - 1 - c, 0))
    return pl.pallas_call(
        body,
        out_shape=(jax.ShapeDtypeStruct((T, 4 * D_MODEL), BF16), jax.ShapeDtypeStruct((1, D_MODEL), F32),
                   jax.ShapeDtypeStruct((1, HD), F32)),
        grid=(NC,),
        in_specs=[col(0), col(1), col(2), col(3), row, row,
                  pl.BlockSpec((CPS, H, HD, HD), lambda c: (NC - 1 - c, 0, 0, 0)),
                  pl.BlockSpec((1, D_MODEL), lambda c: (0, 0)), pl.BlockSpec((1, HD), lambda c: (0, 0))],
        out_specs=(pl.BlockSpec((CPS * C, 4 * D_MODEL), lambda c: (NC - 1 - c, 0)),
                   pl.BlockSpec((1, D_MODEL), lambda c: (0, 0)), pl.BlockSpec((1, HD), lambda c: (0, 0))),
        scratch_shapes=[pltpu.VMEM((H, HD, HD), F32)] + [pltpu.VMEM((CPS, C, D_MODEL), F32)] * 3,
        compiler_params=_params("arbitrary"), name=name)(proj, proj, proj, proj, o_pre, d_og, states, lb, gain)


def _attn_masks():
    r = lax.broadcasted_iota(jnp.int32, (ATTN_BLOCK, ATTN_BLOCK), 0)
    c = lax.broadcasted_iota(jnp.int32, (ATTN_BLOCK, ATTN_BLOCK), 1)
    return c >= r, c <= r


def _attn_fwd(qkv, dilation, name):
    T = qkv.shape[0]
    nb = T // dilation // ATTN_BLOCK
    W = ATTN_GROUP_WIDTH
    B = ATTN_BLOCK
    scale = ATTN_DIM ** -0.5
    qb = 2 if nb % 2 == 0 else 1
    steps = nb // qb

    def body(q_ref, kp_ref, kc_ref, vp_ref, vc_ref, o_ref, lse_ref):
        no_prev = jnp.where(pl.program_id(1) > 0, 0.0, NEG_BIG)
        m_prev, m_cur = _attn_masks()
        ones = jnp.ones((B, ATTN_DIM), BF16)
        items = []
        for j in range(qb):
            for h in range(ATTN_GROUP_HEADS):
                sl = slice(h * ATTN_DIM, (h + 1) * ATTN_DIM)
                rows = slice(j * B, (j + 1) * B)
                if j == 0:
                    items.append((rows, sl, kp_ref[:, sl], vp_ref[:, sl], no_prev))
                else:
                    before = slice((j - 1) * B, j * B)
                    items.append((rows, sl, kc_ref[before, sl], vc_ref[before, sl], 0.0))
        s_p = [jnp.where(m_prev, _dot(q_ref[rows, sl], k_p, NT) * scale + bias, NEG_BIG)
               for rows, sl, k_p, _, bias in items]
        s_c = [jnp.where(m_cur, _dot(q_ref[rows, sl], kc_ref[rows, sl], NT) * scale, NEG_BIG)
               for rows, sl, _, _, _ in items]
        m = [jnp.max(jnp.maximum(a, b), axis=-1, keepdims=True) for a, b in zip(s_p, s_c)]
        p_p = [jnp.exp(a - mx).astype(BF16) for a, mx in zip(s_p, m)]
        p_c = [jnp.exp(b - mx).astype(BF16) for b, mx in zip(s_c, m)]
        l = [_dot(a, ones, NN) + _dot(b, ones, NN) for a, b in zip(p_p, p_c)]
        acc = [_dot(a, v_p, NN) + _dot(b, vc_ref[rows, sl], NN)
               for a, b, (rows, sl, _, v_p, _) in zip(p_p, p_c, items)]
        for (rows, sl, _, _, _), a, lv, mx in zip(items, acc, l, m):
            o_ref[rows, sl] = (a / lv).astype(BF16)
            lse_ref[rows, sl] = mx + jnp.log(lv)

    cur = lambda col: pl.BlockSpec((qb * B, W), lambda s, n: (s * steps + n, col))
    prev = lambda col: pl.BlockSpec((B, W), lambda s, n: (s * nb + jnp.maximum(qb * n - 1, 0), col))
    out = pl.BlockSpec((qb * B, W), lambda s, n: (s * steps + n, 0))
    return pl.pallas_call(
        body, out_shape=(jax.ShapeDtypeStruct((T, W), BF16), jax.ShapeDtypeStruct((T, W), F32)),
        grid=(dilation, steps),
        in_specs=[cur(0), prev(1), cur(1), prev(2), cur(2)],
        out_specs=(out, out), compiler_params=_params("parallel", "arbitrary"), name=name)(qkv, qkv, qkv, qkv, qkv)


def _attn_bwd(qkv, d_out, lse, delta, cos, sin, dilation, name):
    T = qkv.shape[0]
    nb = T // dilation // ATTN_BLOCK
    assert nb % 2 == 0, "an even number of 128-token blocks per residue class"
    pairs = nb // 2
    W = ATTN_GROUP_WIDTH
    B = ATTN_BLOCK
    scale = ATTN_DIM ** -0.5

    def unrope(x, cos_v, sin_v):
        return x * cos_v + pltpu.roll(x * sin_v, ATTN_DIM // 2, 1)

    def body(qa_ref, qb_ref, kpair_ref, kc_ref, vpair_ref, vc_ref, doa_ref, dob_ref, lsea_ref, lseb_ref,
             dla_ref, dlb_ref, cos_ref, sin_ref, out_ref, dq_scr, dk_scr, dv_scr):
        n = pl.program_id(1)

        @pl.when(n == 0)
        def _():
            dq_scr[...] = jnp.zeros_like(dq_scr)
            dk_scr[...] = jnp.zeros_like(dk_scr)
            dv_scr[...] = jnp.zeros_like(dv_scr)

        no_a = jnp.where(n > 0, 0.0, NEG_BIG)
        no_b = jnp.where(n < pairs, 0.0, NEG_BIG)
        m_prev, m_cur = _attn_masks()
        lo, hi = slice(0, B), slice(B, 2 * B)
        heads = [slice(h * ATTN_DIM, (h + 1) * ATTN_DIM) for h in range(ATTN_GROUP_HEADS)]
        flat = []
        for sl in heads:
            qa, qb = qa_ref[:, sl], qb_ref[:, sl]
            doa, dob = doa_ref[:, sl], dob_ref[:, sl]
            k0, k1, k2 = kpair_ref[lo, sl], kpair_ref[hi, sl], kc_ref[:, sl]
            v0, v1, v2 = vpair_ref[lo, sl], vpair_ref[hi, sl], vc_ref[:, sl]
            flat += [(qa, doa, lsea_ref[:, sl], dla_ref[:, sl], k0, v0, m_prev, no_a),
                     (qa, doa, lsea_ref[:, sl], dla_ref[:, sl], k1, v1, m_cur, no_a),
                     (qb, dob, lseb_ref[:, sl], dlb_ref[:, sl], k1, v1, m_prev, no_a + no_b),
                     (qb, dob, lseb_ref[:, sl], dlb_ref[:, sl], k2, v2, m_cur, no_b)]
        s = [_dot(q, k, NT) for q, _, _, _, k, _, _, _ in flat]
        dp = [_dot(do, v, NT) for _, do, _, _, _, v, _, _ in flat]
        p = [jnp.where(mask, jnp.exp(sv * scale - lse_v + bias), 0.0)
             for sv, (_, _, lse_v, _, _, _, mask, bias) in zip(s, flat)]
        ds = [(pv * (dpv - dl_v) * scale).astype(BF16) for pv, dpv, (_, _, _, dl_v, _, _, _, _) in zip(p, dp, flat)]
        p = [pv.astype(BF16) for pv in p]
        dq_part = [_dot(dsv, k, NN) for dsv, (_, _, _, _, k, _, _, _) in zip(ds, flat)]
        dk_part = [_dot(dsv, q, TN) for dsv, (q, _, _, _, _, _, _, _) in zip(ds, flat)]
        dv_part = [_dot(pv, do, TN) for pv, (_, do, _, _, _, _, _, _) in zip(p, flat)]
        cos_lo, sin_lo, cos_hi, sin_hi = cos_ref[lo, :], sin_ref[lo, :], cos_ref[hi, :], sin_ref[hi, :]
        for h, sl in enumerate(heads):
            a_prev, a_cur, b_prev, b_cur = range(4 * h, 4 * h + 4)
            kcol = slice(W + h * ATTN_DIM, W + (h + 1) * ATTN_DIM)
            vcol = slice(2 * W + h * ATTN_DIM, 2 * W + (h + 1) * ATTN_DIM)
            out_ref[lo, sl] = unrope(dq_scr[:, sl], cos_lo, sin_lo).astype(BF16)
            out_ref[hi, sl] = unrope(dq_part[a_prev] + dq_part[a_cur], cos_hi, sin_hi).astype(BF16)
            out_ref[lo, kcol] = unrope(dk_scr[:, sl] + dk_part[a_prev], cos_lo, sin_lo).astype(BF16)
            out_ref[hi, kcol] = unrope(dk_part[a_cur] + dk_part[b_prev], cos_hi, sin_hi).astype(BF16)
            out_ref[lo, vcol] = (dv_scr[:, sl] + dv_part[a_prev]).astype(BF16)
            out_ref[hi, vcol] = (dv_part[a_cur] + dv_part[b_prev]).astype(BF16)
            dq_scr[:, sl] = dq_part[b_prev] + dq_part[b_cur]
            dk_scr[:, sl] = dk_part[b_cur]
            dv_scr[:, sl] = dv_part[b_cur]

    def block_a(n):
        return jnp.maximum(2 * n - 1, 0)

    def block_b(n):
        return jnp.minimum(2 * n, nb - 1)

    def pair(n):
        return jnp.maximum(n - 1, 0)

    one_a = lambda col: pl.BlockSpec((B, W), lambda s, n: (s * nb + block_a(n), col))
    one_b = lambda col: pl.BlockSpec((B, W), lambda s, n: (s * nb + block_b(n), col))
    two = lambda col: pl.BlockSpec((2 * B, W), lambda s, n: (s * pairs + pair(n), col))
    tab = pl.BlockSpec((2 * B, ATTN_DIM), lambda s, n: (s * pairs + pair(n), 0))
    return pl.pallas_call(
        body, out_shape=jax.ShapeDtypeStruct((T, 3 * W), BF16), grid=(dilation, pairs + 1),
        in_specs=[one_a(0), one_b(0), two(1), one_b(1), two(2), one_b(2), one_a(0), one_b(0), one_a(0), one_b(0),
                  one_a(0), one_b(0), tab, tab],
        out_specs=pl.BlockSpec((2 * B, 3 * W), lambda s, n: (s * pairs + pair(n), 0)),
        scratch_shapes=[pltpu.VMEM((B, W), F32)] * 3,
        compiler_params=_params("parallel", "arbitrary"), name=name)(
            qkv, qkv, qkv, qkv, qkv, qkv, d_out, d_out, lse, lse, delta, delta, cos, sin)


PERM_TILE = 512
LANES = 128


def _residue_view(x, d):
    return x if d == 1 else x.reshape(d, x.shape[0] // d, x.shape[1])


def _residue_spec(d, tm, cols):
    if d == 1:
        return pl.BlockSpec((tm, cols), lambda i: (i, 0))
    return pl.BlockSpec((d, tm // d, cols), lambda i: (0, i, 0))


def _residue_shape(T, d, cols, dtype):
    return jax.ShapeDtypeStruct((T, cols) if d == 1 else (d, T // d, cols), dtype)


def _class_rows(r, d, tm):
    return pl.ds(r, tm // d, stride=d)


def _attn_norm(h, gain, name):
    T = h.shape[0]
    tm = _pick_tile(T, PERM_TILE, 16 * max(ATTN_DILATIONS))
    dils = ATTN_DILATIONS
    (base_cos, base_sin), (off_cos, off_sin), sign = _rope_parts(T, tm)

    def body(h_ref, g_ref, bc_ref, bs_ref, oc_ref, os_ref, sign_ref, *refs):
        u_refs, c_refs, s_refs, u_scr, c_scr, s_scr = refs[0:3], refs[3:6], refs[6:9], refs[9], refs[10], refs[11]
        hv = h_ref[...]
        rstd = lax.rsqrt(jnp.mean(hv * hv, axis=-1, keepdims=True) + NORM_EPS)
        u = hv * rstd * g_ref[...]
        for j in range(D_MODEL // LANES):
            u_scr[j] = u[:, j * LANES:(j + 1) * LANES]
        bc, bs, oc, osn = bc_ref[0], bs_ref[0], oc_ref[...], os_ref[...]
        c_scr[...] = bc * oc - bs * osn
        s_scr[...] = (bs * oc + bc * osn) * sign_ref[...]
        for d, u_ref, c_ref, s_ref in zip(dils, u_refs, c_refs, s_refs):
            if d == 1:
                u_ref[...] = u.astype(BF16)
                c_ref[...] = c_scr[...]
                s_ref[...] = s_scr[...]
                continue
            for r in range(d):
                rows = _class_rows(r, d, tm)
                for j in range(D_MODEL // LANES):
                    u_ref[r, :, j * LANES:(j + 1) * LANES] = u_scr.at[j][rows, :].astype(BF16)
                c_ref[r] = c_scr[rows, :]
                s_ref[r] = s_scr[rows, :]

    row = pl.BlockSpec((tm, D_MODEL), lambda i: (i, 0))
    base = pl.BlockSpec((1, 1, ATTN_DIM), lambda i: (i, 0, 0))
    off = pl.BlockSpec((tm, ATTN_DIM), lambda i: (0, 0))
    res = pl.pallas_call(
        body,
        out_shape=([_residue_shape(T, d, D_MODEL, BF16) for d in dils]
                   + [_residue_shape(T, d, ATTN_DIM, F32) for d in dils] * 2),
        grid=(T // tm,),
        in_specs=[row, pl.BlockSpec((1, D_MODEL), lambda i: (0, 0)), base, base, off, off,
                  pl.BlockSpec((1, ATTN_DIM), lambda i: (0, 0))],
        out_specs=([_residue_spec(d, tm, D_MODEL) for d in dils] + [_residue_spec(d, tm, ATTN_DIM) for d in dils] * 2),
        scratch_shapes=[pltpu.VMEM((D_MODEL // LANES, tm, LANES), F32), pltpu.VMEM((tm, ATTN_DIM), F32),
                        pltpu.VMEM((tm, ATTN_DIM), F32)],
        compiler_params=_params("parallel"), name=name)(h, gain, base_cos, base_sin, off_cos, off_sin, sign)
    flat = [r.reshape(T, r.shape[-1]) for r in res]
    return flat[0:3], flat[3:6], flat[6:9]


def _attn_merge_fwd(outs, lses, name):
    T = outs[0].shape[0]
    W = ATTN_GROUP_WIDTH
    tm = _pick_tile(T, PERM_TILE, 16 * max(ATTN_DILATIONS))
    dils = ATTN_DILATIONS

    def body(*refs):
        o_refs, l_refs, oc_ref, lse_refs = refs[0:3], refs[3:6], refs[6], refs[7:10]
        o_scr, l_scr, t_scr = refs[10:13]
        nh = ATTN_GROUP_HEADS
        for g, d in enumerate(dils):
            for j in range(nh):
                lanes = slice(j * LANES, (j + 1) * LANES)
                if d == 1:
                    o_scr[g * nh + j] = o_refs[g][:, lanes].astype(F32)
                    l_scr[g * nh + j] = l_refs[g][:, lanes]
                    continue
                for r in range(d):
                    rows = _class_rows(r, d, tm)
                    o_scr.at[g * nh + j][rows, :] = o_refs[g][r, :, lanes].astype(F32)
                    l_scr.at[g * nh + j][rows, :] = l_refs[g][r, :, lanes]
        for j in range(nh):
            lanes = slice(j * LANES, (j + 1) * LANES)
            ls = [l_scr[g * nh + j] for g in range(3)]
            m = jnp.maximum(jnp.maximum(ls[0], ls[1]), ls[2])
            tot = m + jnp.log(jnp.exp(ls[0] - m) + jnp.exp(ls[1] - m) + jnp.exp(ls[2] - m))
            t_scr[j] = tot
            for g, d in enumerate(dils):
                oc_ref[:, g * W + j * LANES:g * W + (j + 1) * LANES] = (
                    o_scr[g * nh + j] * jnp.exp(ls[g] - tot)).astype(BF16)
                if d == 1:
                    lse_refs[g][:, lanes] = tot
                    continue
                for r in range(d):
                    lse_refs[g][r, :, lanes] = t_scr.at[j][_class_rows(r, d, tm), :]

    in_blk = [_residue_spec(d, tm, W) for d in dils]
    n_blk = 3 * ATTN_GROUP_HEADS
    res = pl.pallas_call(
        body, out_shape=[jax.ShapeDtypeStruct((T, 3 * W), BF16)] + [_residue_shape(T, d, W, F32) for d in dils],
        grid=(T // tm,), in_specs=in_blk * 2,
        out_specs=[pl.BlockSpec((tm, 3 * W), lambda i: (i, 0))] + in_blk,
        scratch_shapes=[pltpu.VMEM((n_blk, tm, LANES), F32), pltpu.VMEM((n_blk, tm, LANES), F32),
                        pltpu.VMEM((ATTN_GROUP_HEADS, tm, LANES), F32)],
        compiler_params=_params("parallel"), name=name)(
            *[_residue_view(o, d) for o, d in zip(outs, dils)], *[_residue_view(l, d) for l, d in zip(lses, dils)])
    return res[0], [r.reshape(T, W) for r in res[1:]]


def _attn_merge_bwd(d_oc, oc, name):
    T = d_oc.shape[0]
    W = ATTN_GROUP_WIDTH
    tm = _pick_tile(T, PERM_TILE, 16 * max(ATTN_DILATIONS))
    dils = ATTN_DILATIONS

    def body(d_ref, o_ref, *refs):
        delta_refs, db_refs, dl_scr, d_scr = refs[0:3], refs[3:6], refs[6], refs[7]
        nh = ATTN_GROUP_HEADS
        for j in range(nh):
            tot = jnp.zeros((tm, 1), F32)
            for g in range(3):
                cols = slice(g * W + j * LANES, g * W + (j + 1) * LANES)
                d_blk = d_ref[:, cols]
                d_scr[g * nh + j] = d_blk
                tot = tot + jnp.sum(d_blk * o_ref[:, cols].astype(F32), axis=-1, keepdims=True)
            dl_scr[j] = jnp.broadcast_to(tot, (tm, LANES))
        for g, d in enumerate(dils):
            for j in range(nh):
                lanes = slice(j * LANES, (j + 1) * LANES)
                if d == 1:
                    delta_refs[g][:, lanes] = dl_scr[j]
                    db_refs[g][:, lanes] = d_scr[g * nh + j].astype(BF16)
                    continue
                for r in range(d):
                    rows = _class_rows(r, d, tm)
                    delta_refs[g][r, :, lanes] = dl_scr.at[j][rows, :]
                    db_refs[g][r, :, lanes] = d_scr.at[g * nh + j][rows, :].astype(BF16)

    wide = pl.BlockSpec((tm, 3 * W), lambda i: (i, 0))
    out_blk = [_residue_spec(d, tm, W) for d in dils]
    res = pl.pallas_call(
        body, out_shape=[_residue_shape(T, d, W, F32) for d in dils] + [_residue_shape(T, d, W, BF16) for d in dils],
        grid=(T // tm,), in_specs=[wide, wide], out_specs=out_blk * 2,
        scratch_shapes=[pltpu.VMEM((ATTN_GROUP_HEADS, tm, LANES), F32),
                        pltpu.VMEM((3 * ATTN_GROUP_HEADS, tm, LANES), F32)],
        compiler_params=_params("parallel"), name=name)(d_oc, oc)
    flat = [r.reshape(T, W) for r in res]
    return flat[0:3], flat[3:6]


def _rope_parts(T, tile):
    inv_freq = 1.0 / (ROPE_THETA ** (jnp.arange(0, ATTN_DIM, 2, dtype=F32) / ATTN_DIM))
    inv_freq = jnp.concatenate([inv_freq, inv_freq])[None, :]
    base = (jnp.arange(T // tile, dtype=F32) * tile)[:, None] * inv_freq
    off = jnp.arange(tile, dtype=F32)[:, None] * inv_freq
    sign = jnp.concatenate([-jnp.ones((1, ATTN_DIM // 2), F32), jnp.ones((1, ATTN_DIM // 2), F32)], axis=1)
    return (jnp.cos(base)[:, None, :], jnp.sin(base)[:, None, :]), (jnp.cos(off), jnp.sin(off)), sign


WEIGHT_GROUPS = {"hgrn": ("hgrn_in", "hgrn_out"), "ffn0": ("ffn_in0", "ffn_down0"),
                 "attn": ("qkv", "attn_out"), "ffn1": ("ffn_in1", "ffn_down1")}


def _local_step(x, target, norm_mix, norm_ffn, lb, out_gain, final_gain, fetch, publish):
    g_mix = [norm_mix[0:1], norm_mix[1:2]]
    g_ffn = [norm_ffn[0:1], norm_ffn[1:2]]
    w = {}

    def whole(name):
        return [(w[name], w[name].shape[0], 0)]

    def qkv_parts(g):
        return [(w["qkv"], ATTN_GROUP_WIDTH, 3 * j + g) for j in range(3)]

    def ffn_fwd(h, layer, head=None):
        w.update(fetch(f"ffn{layer}"))
        n, gate, up, a = _ffn_in(h, g_ffn[layer], w[f"ffn_in{layer}"], f"ffn{layer}_in")
        out = _mm_nn([a], [whole(f"ffn_down{layer}")], h, name=f"ffn{layer}_down", head=head)
        return out, (n, gate, up, a)

    def ffn_bwd(h, saved, dh, dhb, layer):
        n, gate, up, a = saved
        w_in = w[f"ffn_in{layer}"]
        dgate, dup = _ffn_down_dx(dhb, w[f"ffn_down{layer}"], gate, up, f"ffn{layer}_down_dx")
        grads = {f"ffn_down{layer}": _mm_tn([a], dhb, name=f"ffn{layer}_down_dw"),
                 f"ffn_in{layer}": _mm_tn([dgate, dup], n, name=f"ffn{layer}_in_dw")}
        publish(f"ffn{layer}", grads)
        return _mm_nn([dgate, dup], [[(w_in, D_FF, 0)], [(w_in, D_FF, 1)]], dh, name=f"ffn{layer}_in_dx",
                      norm=(h, g_ffn[layer]))

    u0 = _rms_fwd(x, g_mix[0], "hgrn_norm")
    w.update(fetch("hgrn"))
    proj = _mm_nt(u0, whole("hgrn_in"), out_dtype=F32, name="hgrn_in")
    og, o_pre, states = _hgrn_fwd(proj, lb, out_gain, "hgrn_fwd")
    h1 = _mm_nn([og], [whole("hgrn_out")], x, name="hgrn_out")
    h2, ffn0 = ffn_fwd(h1, 0)

    u1_g, cos_g, sin_g = _attn_norm(h2, g_mix[1], "attn_norm")
    w.update(fetch("attn"))
    qkv_g, outs, lses = [], [], []
    for g, d in enumerate(ATTN_DILATIONS):
        qkv_g.append(_mm_nt(u1_g[g], qkv_parts(g), out_dtype=BF16, name=f"attn_qkv{g}",
                            rope=(cos_g[g], sin_g[g], 2)))
        o_g, lse_g = _attn_fwd(qkv_g[g], d, f"attn_fwd{g}")
        outs.append(o_g)
        lses.append(lse_g)
    oc, lse_all = _attn_merge_fwd(outs, lses, "attn_merge")
    h3 = _mm_nn([oc], [whole("attn_out")], h2, name="attn_out")
    (dh4, dh4b, d_final, loss_part), ffn1 = ffn_fwd(h3, 1, head=(target, final_gain))
    dh3, dh3b, d_ffn1 = ffn_bwd(h3, ffn1, dh4, dh4b, 1)

    d_oc = _mm_nt(dh3b, whole("attn_out"), out_dtype=F32, name="attn_out_dx")
    grad_attn_out = _mm_tn([oc], dh3b, name="attn_out_dw")
    delta, d_ocb = _attn_merge_bwd(d_oc, oc, "attn_merge_bwd")
    du1, qkv_pieces = [], []
    for g, d in enumerate(ATTN_DILATIONS):
        dqkv = _attn_bwd(qkv_g[g], d_ocb[g], lse_all[g], delta[g], cos_g[g], sin_g[g], d, f"attn_bwd{g}")
        qkv_pieces.append(_mm_tn([dqkv], u1_g[g], name=f"attn_qkv_dw{g}"))
        du1.append(_mm_nn([dqkv], [qkv_parts(g)], None, name=f"attn_qkv_dx{g}"))
    grad_qkv = jnp.stack([p.reshape(3, ATTN_GROUP_WIDTH, D_MODEL) for p in qkv_pieces], axis=1).reshape(
        3 * ATTN_WIDTH, D_MODEL)
    publish("attn", {"qkv": grad_qkv, "attn_out": grad_attn_out})
    dh2, dh2b, d_mix1 = _rms_bwd(h2, g_mix[1], du1, dh3, "attn_norm_bwd", ATTN_DILATIONS)

    dh1, dh1b, d_ffn0 = ffn_bwd(h1, ffn0, dh2, dh2b, 0)

    d_og = _mm_nt(dh1b, whole("hgrn_out"), out_dtype=F32, name="hgrn_out_dx")
    grad_hgrn_out = _mm_tn([og], dh1b, name="hgrn_out_dw")
    dproj, d_lb, d_out_gain = _hgrn_bwd(proj, o_pre, d_og, states, lb, out_gain, "hgrn_bwd")
    publish("hgrn", {"hgrn_in": _mm_tn([dproj], u0, name="hgrn_in_dw"), "hgrn_out": grad_hgrn_out})
    dx, _, d_mix0 = _mm_nn([dproj], [whole("hgrn_in")], dh1, name="hgrn_in_dx", norm=(x, g_mix[0]))

    small = dict(norm_mix0=d_mix0, norm_mix1=d_mix1, norm_ffn0=d_ffn0, norm_ffn1=d_ffn1, lb=d_lb,
                 out_gain=d_out_gain, final=d_final, loss=loss_part)
    return dx, small


WEIGHT_NAMES = ("hgrn_in", "hgrn_out", "qkv", "attn_out", "ffn_in0", "ffn_in1", "ffn_down0", "ffn_down1")
MESH_IDS = pl.DeviceIdType.MESH
HBM_SPEC = pl.BlockSpec(memory_space=pl.ANY)


N_PEERS = N_DEV - 1
PEER_OFFSETS = [(dx, dy, dc) for dx in (0, 1) for dy in (0, 1) for dc in (0, 1)][1:]


def _mesh_place():
    x, y, c = lax.axis_index("x"), lax.axis_index("y"), lax.axis_index("c")
    peers = []
    for dx, dy, dc in PEER_OFFSETS:
        px, py, pc = (1 - x if dx else x), (1 - y if dy else y), (1 - c if dc else c)
        peers.append(((px, py, pc), 4 * px + 2 * py + pc))
    return 4 * x + 2 * y + c, peers


def _gather_over_two_levels(src_refs, land_refs, send_sems, recv_sems):
    n = len(src_refs)
    x, y, c = lax.axis_index("x"), lax.axis_index("y"), lax.axis_index("c")
    me, sibling = (x, y, c), (x, y, 1 - c)
    chips = [(1 - x, y), (x, 1 - y), (1 - x, 1 - y)]

    def block(w, px, py, pc):
        return land_refs[w].at[4 * px + 2 * py + pc]

    def copy(w, k, owner, to, src=None):
        return pltpu.make_async_remote_copy(
            src_ref=block(w, *owner) if src is None else src, dst_ref=block(w, *owner),
            send_sem=send_sems.at[w * N_PEERS + k], recv_sem=recv_sems.at[w * N_PEERS + k],
            device_id=to, device_id_type=MESH_IDS)

    sent = []
    for w in range(n):
        sent.append(copy(w, 0, me, sibling, src=src_refs[w]))
        sent += [copy(w, 1 + j, me, (*chip, c), src=src_refs[w]) for j, chip in enumerate(chips)]
    for cp in sent:
        cp.start()
    for w in range(n):
        for j, chip in enumerate(chips):
            copy(w, 1 + j, (*chip, c), me).wait_recv()
            passed = copy(w, 4 + j, (*chip, c), sibling)
            passed.start()
            sent.append(passed)
    for w in range(n):
        copy(w, 0, sibling, me).wait_recv()
        for j, chip in enumerate(chips):
            copy(w, 4 + j, (*chip, 1 - c), me).wait_recv()
    for cp in sent:
        cp.wait_send()


def _exchange_launch(srcs, scatter, collective_id, name):
    n = len(srcs)
    src_refs = [jax.new_ref(s, memory_space=pltpu.MemorySpace.HBM) for s in srcs]
    land_refs = [jax.empty_ref(jax.ShapeDtypeStruct(s.shape if scatter else (N_DEV,) + s.shape, s.dtype),
                               memory_space=pltpu.MemorySpace.HBM) for s in srcs]

    @pl.kernel(mesh=plsc.ScalarSubcoreMesh(axis_name="sequencer", num_cores=1), name=name,
               scratch_types=(pltpu.SemaphoreType.DMA((n * N_PEERS,)), pltpu.SemaphoreType.DMA((n * N_PEERS,)),
                              pltpu.SemaphoreType.DMA((n,))),
               compiler_params=pltpu.CompilerParams(collective_id=collective_id))
    def launch(send_sems, recv_sems, local_sems):
        me, peers = _mesh_place()
        barrier = pltpu.get_barrier_semaphore()
        for peer, _ in peers:
            pl.semaphore_signal(barrier, inc=1, device_id=peer, device_id_type=MESH_IDS)
        pl.semaphore_wait(barrier, N_PEERS)
        own = [pltpu.make_async_copy(src_refs[w].at[me] if scatter else src_refs[w], land_refs[w].at[me],
                                     local_sems.at[w]) for w in range(n)]
        for cp in own:
            cp.start()
        if scatter:
            copies = [pltpu.make_async_remote_copy(
                src_ref=src_refs[w].at[pid], dst_ref=land_refs[w].at[me],
                send_sem=send_sems.at[w * N_PEERS + k], recv_sem=recv_sems.at[w * N_PEERS + k],
                device_id=peer, device_id_type=MESH_IDS) for w in range(n) for k, (peer, pid) in enumerate(peers)]
            for cp in copies:
                cp.start()
            for cp in copies:
                cp.wait()
        else:
            _gather_over_two_levels(src_refs, land_refs, send_sems, recv_sems)
        for cp in own:
            cp.wait()

    launch()
    return land_refs


def _gather_small(block, name):
    def body(in_ref, out_ref, send_sems, recv_sems, local_sem):
        me, peers = _mesh_place()
        own = pltpu.make_async_copy(in_ref, out_ref.at[me], local_sem)
        own.start()
        sends = [pltpu.make_async_remote_copy(
            src_ref=in_ref, dst_ref=out_ref.at[me], send_sem=send_sems.at[k], recv_sem=recv_sems.at[k],
            device_id=peer, device_id_type=MESH_IDS) for k, (peer, _) in enumerate(peers)]
        for cp in sends:
            cp.start()
        for cp in sends:
            cp.wait_recv()
        for cp in sends:
            cp.wait_send()
        own.wait()

    return pl.pallas_call(
        body, out_shape=jax.ShapeDtypeStruct((N_DEV,) + block.shape, block.dtype),
        in_specs=[HBM_SPEC], out_specs=HBM_SPEC,
        scratch_shapes=[pltpu.SemaphoreType.DMA((N_PEERS,)), pltpu.SemaphoreType.DMA((N_PEERS,)),
                        pltpu.SemaphoreType.DMA],
        name=name)(block)


def _sum_blocks(recv, name):
    rows = recv.shape[1]
    tr = _pick_tile(rows, 256, 16)

    def body(r_ref, g_ref):
        acc = r_ref[0].astype(F32)
        for j in range(1, N_DEV):
            acc = acc + r_ref[j].astype(F32)
        g_ref[...] = acc

    return pl.pallas_call(
        body, out_shape=jax.ShapeDtypeStruct((rows, D_MODEL), F32), grid=(rows // tr,),
        in_specs=[pl.BlockSpec((N_DEV, tr, D_MODEL), lambda i: (0, i, 0))],
        out_specs=pl.BlockSpec((tr, D_MODEL), lambda i: (i, 0)),
        compiler_params=_params("parallel"), name=name)(recv)


def _adamw_math(w, g, m, v):
    m_new = ADAM_B1 * m + (1.0 - ADAM_B1) * g
    v_new = ADAM_B2 * v + (1.0 - ADAM_B2) * (g * g)
    m_hat = m_new / (1.0 - ADAM_B1 ** ADAM_STEP)
    v_hat = v_new / (1.0 - ADAM_B2 ** ADAM_STEP)
    delta = -ADAM_LR * (m_hat / (jnp.sqrt(v_hat) + ADAM_EPS) + ADAM_WD * w)
    return delta, m_new, v_new


def _adamw(w, g, m, v, name):
    rows, cols = w.shape
    tr = _pick_tile(rows, 256, 8)

    def body(w_ref, g_ref, m_ref, v_ref, d_ref, mo_ref, vo_ref):
        d_ref[...], mo_ref[...], vo_ref[...] = _adamw_math(w_ref[...], g_ref[...], m_ref[...], v_ref[...])

    blk = pl.BlockSpec((tr, cols), lambda i: (i, 0))
    return pl.pallas_call(
        body, out_shape=(jax.ShapeDtypeStruct((rows, cols), F32),) * 3, grid=(rows // tr,),
        in_specs=[blk] * 4, out_specs=(blk,) * 3, compiler_params=_params("parallel"), name=name)(w, g, m, v)


ROW_MIX, ROW_FFN, ROW_LB, ROW_OUT_GAIN, ROW_FINAL = 0, 2, 4, 7, 8
PART_MIX, PART_FFN, PART_LB, PART_OUT_GAIN, PART_FINAL, PART_LOSS = 0, 2, 4, 5, 6, 7


def _small_update(parts_all, w, m, v, name):
    def body(p_ref, w_ref, m_ref, v_ref, g_ref, d_ref, mo_ref, vo_ref, loss_ref):
        def total(row, n=1):
            tot = p_ref[0, row:row + n, :]
            for j in range(1, N_DEV):
                tot = tot + p_ref[j, row:row + n, :]
            return tot

        logits = [w_ref[ROW_LB + i:ROW_LB + i + 1, :] for i in range(3)]
        mx = jnp.maximum(jnp.maximum(logits[0], logits[1]), logits[2])
        ex = [jnp.exp(l - mx) for l in logits]
        den = ex[0] + ex[1] + ex[2]
        prob = [e / den for e in ex]
        d_lb = total(PART_LB)
        g_ref[...] = jnp.zeros_like(g_ref)
        g_ref[ROW_MIX:ROW_MIX + 2, :] = total(PART_MIX, 2)
        g_ref[ROW_FFN:ROW_FFN + 2, :] = total(PART_FFN, 2)
        for i in range(3):
            g_ref[ROW_LB + i:ROW_LB + i + 1, :] = prob[i] * ((d_lb if i == 0 else 0.0) - prob[0] * d_lb)
        g_ref[ROW_OUT_GAIN:ROW_OUT_GAIN + 1, :] = total(PART_OUT_GAIN)
        g_ref[ROW_FINAL:ROW_FINAL + 1, :] = total(PART_FINAL)
        d_ref[...], mo_ref[...], vo_ref[...] = _adamw_math(w_ref[...], g_ref[...], m_ref[...], v_ref[...])
        loss_ref[...] = jnp.sum(total(PART_LOSS), axis=-1, keepdims=True)

    packed = jax.ShapeDtypeStruct((16, D_MODEL), F32)
    return pl.pallas_call(
        body, out_shape=(packed, packed, packed, packed, jax.ShapeDtypeStruct((1, 1), F32)),
        compiler_params=pltpu.CompilerParams(vmem_limit_bytes=VMEM_LIMIT), name=name)(parts_all, w, m, v)


def _pack_small(norm_mix, norm_ffn, lb_logits, out_gain, final):
    pad = jnp.zeros((1, D_MODEL - HGRN_DIM), F32)
    return jnp.concatenate([norm_mix, norm_ffn, lb_logits, jnp.concatenate([out_gain, pad], axis=1),
                            final.reshape(1, D_MODEL), jnp.zeros((16 - ROW_FINAL - 1, D_MODEL), F32)], axis=0)


def _unpack_small(p):
    return (p[ROW_MIX:ROW_MIX + 2], p[ROW_FFN:ROW_FFN + 2], p[ROW_LB:ROW_LB + 3],
            p[ROW_OUT_GAIN:ROW_OUT_GAIN + 1, :HGRN_DIM], p[ROW_FINAL])


def _lower_bound(lb_logits, name):
    def body(l_ref, o_ref):
        logits = [l_ref[i:i + 1, :] for i in range(3)]
        mx = jnp.maximum(jnp.maximum(logits[0], logits[1]), logits[2])
        ex = [jnp.exp(l - mx) for l in logits]
        o_ref[...] = ex[0] / (ex[0] + ex[1] + ex[2])

    return pl.pallas_call(body, out_shape=jax.ShapeDtypeStruct((1, D_MODEL), F32), name=name)(lb_logits)


def kernel(x, norm_mix, norm_ffn, hgrn_w_in, hgrn_lb_logits, hgrn_out_norm, hgrn_w_out, attn_w_qkv, attn_w_out, ffn_w_in, ffn_w_down, final_norm, loss_target, m_norm_mix, m_norm_ffn, m_hgrn_w_in, m_hgrn_lb_logits, m_hgrn_out_norm, m_hgrn_w_out, m_attn_w_qkv, m_attn_w_out, m_ffn_w_in, m_ffn_w_down, m_final_norm, v_norm_mix, v_norm_ffn, v_hgrn_w_in, v_hgrn_lb_logits, v_hgrn_out_norm, v_hgrn_w_out, v_attn_w_qkv, v_attn_w_out, v_ffn_w_in, v_ffn_w_down, v_final_norm):
    col_sharded = {"hgrn_in": hgrn_w_in[0], "qkv": attn_w_qkv[0], "ffn_in0": ffn_w_in[0], "ffn_in1": ffn_w_in[1]}
    row_sharded = {"hgrn_out": hgrn_w_out[0], "attn_out": attn_w_out[0], "ffn_down0": ffn_w_down[0],
                   "ffn_down1": ffn_w_down[1]}
    gathering = {}
    for gi, (group, names) in enumerate(WEIGHT_GROUPS.items()):
        shards = [(col_sharded[n].T if n in col_sharded else row_sharded[n]).astype(BF16) for n in names]
        gathering[group] = _exchange_launch(shards, False, 1 + gi, f"weights_gather_{group}")

    def fetch(group):
        return {n: land[...].reshape(-1, D_MODEL) for n, land in zip(WEIGHT_GROUPS[group], gathering[group])}

    in_flight = {}

    def publish(group, grads):
        names = WEIGHT_GROUPS[group]
        parts = [grads[n].reshape(N_DEV, -1, D_MODEL) for n in names]
        in_flight[group] = _exchange_launch(parts, True, 1 + len(WEIGHT_GROUPS) + list(WEIGHT_GROUPS).index(group),
                                            f"grads_send_{group}")

    lb = _lower_bound(hgrn_lb_logits, "hgrn_lower_bound")
    grad_x, small = _local_step(x[0], loss_target[0], norm_mix, norm_ffn, lb, hgrn_out_norm,
                                final_norm.reshape(1, D_MODEL), fetch, publish)

    pad = jnp.zeros((1, D_MODEL - HGRN_DIM), F32)
    small_part = jnp.concatenate(
        [small["norm_mix0"], small["norm_mix1"], small["norm_ffn0"], small["norm_ffn1"], small["lb"],
         jnp.concatenate([small["out_gain"], pad], axis=1), small["final"], small["loss"]], axis=0)
    small_all = _gather_small(small_part, "small_grads_gather")
    received = {}
    for group in ("ffn1", "attn", "ffn0", "hgrn"):
        received.update(zip(WEIGHT_GROUPS[group], [land[...] for land in in_flight[group]]))

    masters = {"hgrn_in": (hgrn_w_in[0], m_hgrn_w_in[0], v_hgrn_w_in[0]),
               "hgrn_out": (hgrn_w_out[0], m_hgrn_w_out[0], v_hgrn_w_out[0]),
               "qkv": (attn_w_qkv[0], m_attn_w_qkv[0], v_attn_w_qkv[0]),
               "attn_out": (attn_w_out[0], m_attn_w_out[0], v_attn_w_out[0]),
               "ffn_in0": (ffn_w_in[0], m_ffn_w_in[0], v_ffn_w_in[0]),
               "ffn_in1": (ffn_w_in[1], m_ffn_w_in[1], v_ffn_w_in[1]),
               "ffn_down0": (ffn_w_down[0], m_ffn_w_down[0], v_ffn_w_down[0]),
               "ffn_down1": (ffn_w_down[1], m_ffn_w_down[1], v_ffn_w_down[1])}
    res = {}
    for n in WEIGHT_NAMES:
        g = _sum_blocks(received[n], f"{n}_grad_sum")
        if n in col_sharded:
            g = g.T
        wv, mv, vv = masters[n]
        res[n] = (g,) + tuple(_adamw(wv, g, mv, vv, f"{n}_adamw"))

    def single(n):
        return [t[None] for t in res[n]]

    def pair(n):
        return [jnp.stack([a, b]) for a, b in zip(res[n + "0"], res[n + "1"])]

    big = dict(hgrn_w_in=single("hgrn_in"), hgrn_w_out=single("hgrn_out"), attn_w_qkv=single("qkv"),
               attn_w_out=single("attn_out"), ffn_w_in=pair("ffn_in"), ffn_w_down=pair("ffn_down"))

    w_small = _pack_small(norm_mix, norm_ffn, hgrn_lb_logits, hgrn_out_norm, final_norm)
    m_small = _pack_small(m_norm_mix, m_norm_ffn, m_hgrn_lb_logits, m_hgrn_out_norm, m_final_norm)
    v_small = _pack_small(v_norm_mix, v_norm_ffn, v_hgrn_lb_logits, v_hgrn_out_norm, v_final_norm)
    g_s, d_s, m_s, v_s, loss = _small_update(small_all, w_small, m_small, v_small, "small_update")
    small_out = [_unpack_small(t) for t in (g_s, d_s, m_s, v_s)]

    def group(i):
        s = small_out[i]
        return (s[0], s[1], big["hgrn_w_in"][i], s[2], s[3], big["hgrn_w_out"][i], big["attn_w_qkv"][i],
                big["attn_w_out"][i], big["ffn_w_in"][i], big["ffn_w_down"][i], s[4])

    return (loss.reshape(()), grad_x[None], *group(0), *group(1), *group(2), *group(3))
```

```python
import functools

import jax
import jax.numpy as jnp
from jax import lax
from jax.experimental import pallas as pl
from jax.experimental.pallas import tpu as pltpu
from jax.experimental.pallas import tpu_sc as plsc

F32 = jnp.float32
BF16 = jnp.bfloat16

D_MODEL = 1024
N_DEV = 8
NORM_EPS = 1e-6

HGRN_HEADS = 8
HGRN_DIM = 128
HGRN_CHUNK = 64
HGRN_STEP_CHUNKS = 2
HGRN_EXP_CLAMP = 60.0

ATTN_DIM = 128
ATTN_BLOCK = 128
ATTN_GROUP_HEADS = 4
ATTN_GROUP_WIDTH = ATTN_GROUP_HEADS * ATTN_DIM
ATTN_DILATIONS = (1, 4, 16)
ATTN_WIDTH = 3 * ATTN_GROUP_WIDTH
ROPE_THETA = 10000.0
NEG_BIG = -1e30

D_FF = 2816

ADAM_LR = 0.001
ADAM_B1 = 0.9
ADAM_B2 = 0.999
ADAM_EPS = 1e-08
ADAM_WD = 0.01
ADAM_STEP = 10

VMEM_LIMIT = 48 * 1024 * 1024

NT = (((1,), (1,)), ((), ()))
NN = (((1,), (0,)), ((), ()))
TN = (((0,), (0,)), ((), ()))


def _dot(a, b, dims):
    return lax.dot_general(a, b, dims, preferred_element_type=F32)


def _params(*sem):
    return pltpu.CompilerParams(dimension_semantics=sem, vmem_limit_bytes=VMEM_LIMIT)


def _pick_tile(n, cap, mult):
    best = None
    for t in range(mult, min(n, cap) + 1, mult):
        if n % t == 0:
            best = t
    assert best is not None, (n, cap, mult)
    return best


def _sigmoid(x):
    return 0.5 * jnp.tanh(0.5 * x) + 0.5


ROW_TILE = 512
COL_CHUNK = 512
GRAD_TILE = 256


def _whole(shape, index_map):
    return pl.BlockSpec(shape, index_map, pipeline_mode=pl.Buffered(1))


def _part_specs(parts, n_cols):
    return [_whole((rows, n_cols), functools.partial(lambda i, b: (b, 0), b=blk)) for _, rows, blk in parts]


def _mm_nt(a, w_parts, *, out_dtype, name, rope=None):
    M, K = a.shape
    tm = _pick_tile(M, ROW_TILE, 16)
    widths = [rows for _, rows, _ in w_parts]
    n_parts = len(w_parts)

    def body(*refs):
        a_ref, w_refs, o_ref = refs[0], refs[1:1 + n_parts], refs[-1]
        av = a_ref[...]
        off = 0
        for p, w_ref in enumerate(w_refs):
            for c0 in range(0, widths[p], COL_CHUNK):
                cw = min(COL_CHUNK, widths[p] - c0)
                acc = _dot(av, w_ref[c0:c0 + cw, :], NT)
                if rope is not None and p < rope[2]:
                    cos, sin = refs[1 + n_parts][...], refs[2 + n_parts][...]
                    for h0 in range(0, cw, ATTN_DIM):
                        xh = acc[:, h0:h0 + ATTN_DIM]
                        rot = pltpu.roll(xh, ATTN_DIM // 2, 1)
                        o_ref[:, off + c0 + h0:off + c0 + h0 + ATTN_DIM] = (xh * cos + rot * sin).astype(out_dtype)
                else:
                    o_ref[:, off + c0:off + c0 + cw] = acc.astype(out_dtype)
            off += widths[p]

    in_specs = [pl.BlockSpec((tm, K), lambda i: (i, 0))] + _part_specs(w_parts, K)
    args = [a] + [w for w, _, _ in w_parts]
    if rope is not None:
        in_specs += [pl.BlockSpec((tm, ATTN_DIM), lambda i: (i, 0))] * 2
        args += [rope[0], rope[1]]
    return pl.pallas_call(
        body, out_shape=jax.ShapeDtypeStruct((M, sum(widths)), out_dtype), grid=(M // tm,),
        in_specs=in_specs, out_specs=pl.BlockSpec((tm, sum(widths)), lambda i: (i, 0)),
        compiler_params=_params("parallel"), name=name)(*args)


def _mm_nn(a_list, w_parts_list, resid, *, name, norm=None, head=None):
    M = a_list[0].shape[0]
    tm = _pick_tile(M, ROW_TILE, 16)
    n_a = len(a_list)
    flat_parts = [p for parts in w_parts_list for p in parts]
    extra = norm if norm is not None else head
    n_in = n_a + len(flat_parts) + (1 if resid is not None else 0) + (2 if extra is not None else 0)

    def body(*refs):
        a_refs, w_refs = refs[:n_a], refs[n_a:n_a + len(flat_parts)]

        def product(rows):
            acc = None
            wi = 0
            for a_ref, parts in zip(a_refs, w_parts_list):
                off = 0
                for _, k, _ in parts:
                    term = _dot(a_ref[rows, off:off + k], w_refs[wi][...], NN)
                    acc = term if acc is None else acc + term
                    off += k
                    wi += 1
            return acc

        if extra is None:
            acc = product(slice(None))
            if resid is not None:
                acc = acc + refs[n_in - 1][...]
            refs[n_in][...] = acc
            return

        @pl.when(pl.program_id(0) == 0)
        def _():
            for acc_ref in refs[n_in + 2:]:
                acc_ref[...] = jnp.zeros_like(acc_ref)

        for r0 in range(0, tm, tm // 2):
            rows = slice(r0, r0 + tm // 2)
            acc = product(rows)
            if head is not None:
                _loss_head_math(acc + refs[n_in - 3][rows, :], rows, refs[n_in - 2], refs[n_in - 1],
                                *refs[n_in:n_in + 4])
                continue
            dres_ref, x_ref, g_ref = refs[n_in - 3:n_in]
            dx_ref, dxb_ref, dg_ref = refs[n_in:n_in + 3]
            xv = x_ref[rows, :]
            rstd = lax.rsqrt(jnp.mean(xv * xv, axis=-1, keepdims=True) + NORM_EPS)
            n = xv * rstd
            dg_ref[...] += jnp.sum(acc * n, axis=0, keepdims=True)
            dn = acc * g_ref[...]
            dx = dres_ref[rows, :] + rstd * (dn - n * jnp.mean(dn * n, axis=-1, keepdims=True))
            dx_ref[rows, :] = dx
            dxb_ref[rows, :] = dx.astype(BF16)

    row = pl.BlockSpec((tm, D_MODEL), lambda i: (i, 0))
    vec = pl.BlockSpec((1, D_MODEL), lambda i: (0, 0))
    in_specs = [pl.BlockSpec((tm, a.shape[1]), lambda i: (i, 0)) for a in a_list] + _part_specs(flat_parts, D_MODEL)
    args = list(a_list) + [w for w, _, _ in flat_parts]
    if resid is not None:
        in_specs.append(row)
        args.append(resid)
    if extra is None:
        return pl.pallas_call(
            body, out_shape=jax.ShapeDtypeStruct((M, D_MODEL), F32), grid=(M // tm,),
            in_specs=in_specs, out_specs=row, compiler_params=_params("parallel"), name=name)(*args)
    assert resid is not None
    out_shape = [jax.ShapeDtypeStruct((M, D_MODEL), F32), jax.ShapeDtypeStruct((M, D_MODEL), BF16),
                 jax.ShapeDtypeStruct((1, D_MODEL), F32)]
    out_specs = [row, row, vec]
    if head is not None:
        out_shape.append(jax.ShapeDtypeStruct((1, D_MODEL), F32))
        out_specs.append(vec)
    return pl.pallas_call(
        body, out_shape=out_shape, grid=(M // tm,), in_specs=in_specs + [row, vec], out_specs=out_specs,
        compiler_params=_params("arbitrary"), name=name)(*args, extra[0], extra[1])


def _mm_tn(a_list, b, *, name):
    T = a_list[0].shape[0]
    N = b.shape[1]
    tr = GRAD_TILE
    tiles = [a.shape[1] // tr for a in a_list]
    starts = [sum(tiles[:i]) for i in range(len(tiles))]

    def body(*refs):
        a_refs, b_ref, o_ref = refs[:len(a_list)], refs[len(a_list)], refs[-1]
        r = pl.program_id(0)
        for a_ref, first, count in zip(a_refs, starts, tiles):
            @pl.when(jnp.logical_and(r >= first, r < first + count))
            def _():
                o_ref[...] = _dot(a_ref[...], b_ref[...], TN).astype(BF16)

    in_specs = [pl.BlockSpec((T, tr), functools.partial(lambda r, first, count: (0, jnp.clip(r - first, 0, count - 1)),
                                                        first=first, count=count))
                for first, count in zip(starts, tiles)]
    in_specs.append(_whole((T, N), lambda r: (0, 0)))
    return pl.pallas_call(
        body, out_shape=jax.ShapeDtypeStruct((sum(tiles) * tr, N), BF16), grid=(sum(tiles),),
        in_specs=in_specs, out_specs=pl.BlockSpec((tr, N), lambda r: (r, 0)),
        compiler_params=_params("parallel"), name=name)(*a_list, b)


def _rms_fwd(x, gain, name):
    T = x.shape[0]
    tm = _pick_tile(T, 512, 16)

    def body(x_ref, g_ref, u_ref):
        xv = x_ref[...]
        rstd = lax.rsqrt(jnp.mean(xv * xv, axis=-1, keepdims=True) + NORM_EPS)
        u_ref[...] = (xv * rstd * g_ref[...]).astype(BF16)

    return pl.pallas_call(
        body, out_shape=jax.ShapeDtypeStruct((T, D_MODEL), BF16), grid=(T // tm,),
        in_specs=[pl.BlockSpec((tm, D_MODEL), lambda i: (i, 0)), pl.BlockSpec((1, D_MODEL), lambda i: (0, 0))],
        out_specs=pl.BlockSpec((tm, D_MODEL), lambda i: (i, 0)),
        compiler_params=_params("parallel"), name=name)(x, gain)


def _rms_bwd(x, gain, dus, dres, name, dilations=(1,)):
    T = x.shape[0]
    tm = _pick_tile(T, PERM_TILE, 16 * max(dilations))
    n_du = len(dus)

    def body(x_ref, g_ref, *refs):
        du_refs, dres_ref = refs[:n_du], refs[n_du]
        dx_ref, dxb_ref, dg_ref, du_scr = refs[n_du + 1:]

        @pl.when(pl.program_id(0) == 0)
        def _():
            dg_ref[...] = jnp.zeros_like(dg_ref)

        if tuple(dilations) == (1,):
            du = du_refs[0][...]
        else:
            for i, (d, du_ref) in enumerate(zip(dilations, du_refs)):
                for j in range(D_MODEL // LANES):
                    lanes = slice(j * LANES, (j + 1) * LANES)
                    if d == 1:
                        du_scr[j] = du_ref[:, lanes] if i == 0 else du_scr[j] + du_ref[:, lanes]
                        continue
                    blk = du_scr.at[j]
                    for r in range(d):
                        rows = _class_rows(r, d, tm)
                        blk[rows, :] = du_ref[r, :, lanes] if i == 0 else blk[rows, :] + du_ref[r, :, lanes]
            du = jnp.concatenate([du_scr[j] for j in range(D_MODEL // LANES)], axis=1)
        xv = x_ref[...]
        rstd = lax.rsqrt(jnp.mean(xv * xv, axis=-1, keepdims=True) + NORM_EPS)
        n = xv * rstd
        dg_ref[...] += jnp.sum(du * n, axis=0, keepdims=True)
        dn = du * g_ref[...]
        dx = dres_ref[...] + rstd * (dn - n * jnp.mean(dn * n, axis=-1, keepdims=True))
        dx_ref[...] = dx
        dxb_ref[...] = dx.astype(BF16)

    row = pl.BlockSpec((tm, D_MODEL), lambda i: (i, 0))
    vec = pl.BlockSpec((1, D_MODEL), lambda i: (0, 0))
    return pl.pallas_call(
        body,
        out_shape=(jax.ShapeDtypeStruct((T, D_MODEL), F32), jax.ShapeDtypeStruct((T, D_MODEL), BF16),
                   jax.ShapeDtypeStruct((1, D_MODEL), F32)),
        grid=(T // tm,), in_specs=[row, vec] + [_residue_spec(d, tm, D_MODEL) for d in dilations] + [row],
        out_specs=(row, row, vec), scratch_shapes=[pltpu.VMEM((D_MODEL // LANES, tm, LANES), F32)],
        compiler_params=_params("arbitrary"), name=name)(
            x, gain, *[_residue_view(du, d) for du, d in zip(dus, dilations)], dres)


def _loss_head_math(hv, rows, t_ref, g_ref, dh_ref, dhb_ref, dg_ref, loss_ref):
    inv_f = 1.0 / D_MODEL
    g = g_ref[...]
    rstd = lax.rsqrt(jnp.mean(hv * hv, axis=-1, keepdims=True) + NORM_EPS)
    n = hv * rstd
    err = n * g - t_ref[rows, :]
    loss_ref[...] += (0.5 * inv_f) * jnp.sum(err * err, axis=0, keepdims=True)
    dy = err * inv_f
    dg_ref[...] += jnp.sum(dy * n, axis=0, keepdims=True)
    dn = dy * g
    dh = rstd * (dn - n * jnp.mean(dn * n, axis=-1, keepdims=True))
    dh_ref[rows, :] = dh
    dhb_ref[rows, :] = dh.astype(BF16)


FFN_TILE = 256


def _ffn_in(h, gain, w_in, name):
    T = h.shape[0]
    tm = _pick_tile(T, ROW_TILE, 16)

    def body(h_ref, g_ref, w_ref, n_ref, gate_ref, up_ref, a_ref):
        hv = h_ref[...]
        rstd = lax.rsqrt(jnp.mean(hv * hv, axis=-1, keepdims=True) + NORM_EPS)
        n = (hv * rstd * g_ref[...]).astype(BF16)
        n_ref[...] = n
        for c0 in range(0, D_FF, FFN_TILE):
            cols = slice(c0, c0 + FFN_TILE)
            gate = _dot(n, w_ref[c0:c0 + FFN_TILE, :], NT)
            up = _dot(n, w_ref[D_FF + c0:D_FF + c0 + FFN_TILE, :], NT)
            gate_ref[:, cols] = gate.astype(BF16)
            up_ref[:, cols] = up.astype(BF16)
            a_ref[:, cols] = (gate * _sigmoid(gate) * up).astype(BF16)

    row = pl.BlockSpec((tm, D_MODEL), lambda i: (i, 0))
    wide = pl.BlockSpec((tm, D_FF), lambda i: (i, 0))
    wide_shape = jax.ShapeDtypeStruct((T, D_FF), BF16)
    return pl.pallas_call(
        body, out_shape=(jax.ShapeDtypeStruct((T, D_MODEL), BF16), wide_shape, wide_shape, wide_shape),
        grid=(T // tm,),
        in_specs=[row, pl.BlockSpec((1, D_MODEL), lambda i: (0, 0)), _whole((2 * D_FF, D_MODEL), lambda i: (0, 0))],
        out_specs=(row, wide, wide, wide), compiler_params=_params("parallel"), name=name)(h, gain, w_in)


def _ffn_down_dx(dhb, w_down, gate, up, name):
    T = dhb.shape[0]
    tm = _pick_tile(T, ROW_TILE, 16)

    def body(dh_ref, w_ref, gate_ref, up_ref, dgate_ref, dup_ref):
        dh = dh_ref[...]
        for c0 in range(0, D_FF, FFN_TILE):
            cols = slice(c0, c0 + FFN_TILE)
            da = _dot(dh, w_ref[c0:c0 + FFN_TILE, :], NT)
            gate = gate_ref[:, cols].astype(F32)
            sg = _sigmoid(gate)
            dgate_ref[:, cols] = (da * up_ref[:, cols].astype(F32) * (sg * (1.0 + gate * (1.0 - sg)))).astype(BF16)
            dup_ref[:, cols] = (da * gate * sg).astype(BF16)

    wide = pl.BlockSpec((tm, D_FF), lambda i: (i, 0))
    wide_shape = jax.ShapeDtypeStruct((T, D_FF), BF16)
    return pl.pallas_call(
        body, out_shape=(wide_shape, wide_shape), grid=(T // tm,),
        in_specs=[pl.BlockSpec((tm, D_MODEL), lambda i: (i, 0)), _whole((D_FF, D_MODEL), lambda i: (0, 0)), wide, wide],
        out_specs=(wide, wide), compiler_params=_params("parallel"), name=name)(dhb, w_down, gate, up)


def _tri(n, lower):
    r = lax.broadcasted_iota(jnp.int32, (n, n), 0)
    c = lax.broadcasted_iota(jnp.int32, (n, n), 1)
    return (c <= r) if lower else (c >= r)


def _running_sum(x, lower):
    tri = _tri(x.shape[0], lower).astype(BF16)
    hi = x.astype(BF16)
    rest = x - hi.astype(F32)
    mid = rest.astype(BF16)
    lo = (rest - mid.astype(F32)).astype(BF16)
    return _dot(tri, hi, NN) + _dot(tri, mid, NN) + _dot(tri, lo, NN)


def _hgrn_gates(q_raw, f_raw, lb):
    C = q_raw.shape[0]
    sig_f = _sigmoid(f_raw)
    forget = lb + (1.0 - lb) * sig_f
    key = 1.0 - forget
    log_f = jnp.log(forget)
    b = _running_sum(log_f, True)
    first_half = lax.broadcasted_iota(jnp.int32, log_f.shape, 0) < C // 2
    r = jnp.sum(jnp.where(first_half, log_f, 0.0), axis=0, keepdims=True)
    b_last = jnp.sum(log_f, axis=0, keepdims=True)
    e_a = jnp.exp(jnp.minimum(b - r, HGRN_EXP_CLAMP))
    e_b = jnp.exp(jnp.minimum(r - b, HGRN_EXP_CLAMP))
    e_q = jnp.exp(b)
    e_k = jnp.exp(b_last - b)
    sig_q = _sigmoid(q_raw)
    query = q_raw * sig_q
    return dict(sig_f=sig_f, forget=forget, sig_q=sig_q, e_a=e_a, e_b=e_b, e_q=e_q, e_k=e_k,
                e_last=jnp.exp(b_last), q_a=query * e_a, k_b=key * e_b, q_hat=query * e_q, k_til=key * e_k)


def _hgrn_fwd(proj, lb, gain, name):
    T = proj.shape[0]
    C = HGRN_CHUNK
    CPS = HGRN_STEP_CHUNKS
    H, HD = HGRN_HEADS, HGRN_DIM

    def body(q_ref, f_ref, i_ref, g_ref, lb_ref, gain_ref, og_ref, o_ref, st_ref, s_scr):
        @pl.when(pl.program_id(0) == 0)
        def _():
            s_scr[...] = jnp.zeros_like(s_scr)

        causal = _tri(C, True)
        gain_v = gain_ref[...]
        heads = [slice(h * HD, (h + 1) * HD) for h in range(H)]
        s_t = [s_scr[h] for h in range(H)]
        for cc in range(CPS):
            rows = slice(cc * C, (cc + 1) * C)
            for h in range(H):
                st_ref[cc, h] = s_t[h]
            gt = _hgrn_gates(q_ref[rows, :], f_ref[rows, :], lb_ref[...])
            q_a, k_b = gt["q_a"].astype(BF16), gt["k_b"].astype(BF16)
            q_hat, k_til = gt["q_hat"].astype(BF16), gt["k_til"].astype(BF16)
            v = i_ref[rows, :].astype(BF16)
            p = [jnp.where(causal, _dot(q_a[:, sl], k_b[:, sl], NT), 0.0).astype(BF16) for sl in heads]
            o = [_dot(p[h], v[:, sl], NN) + _dot(q_hat[:, sl], s_t[h].astype(BF16), NT)
                 for h, sl in enumerate(heads)]
            s_t = [gt["e_last"][:, sl] * s_t[h] + _dot(v[:, sl], k_til[:, sl], TN) for h, sl in enumerate(heads)]
            for h, sl in enumerate(heads):
                o_ref[rows, sl] = o[h]
                rstd = lax.rsqrt(jnp.mean(o[h] * o[h], axis=-1, keepdims=True) + NORM_EPS)
                g_raw = g_ref[rows, sl]
                og_ref[rows, sl] = (o[h] * rstd * gain_v * (g_raw * _sigmoid(g_raw))).astype(BF16)
        for h in range(H):
            s_scr[h] = s_t[h]

    col = lambda j: pl.BlockSpec((CPS * C, D_MODEL), lambda c: (c, j))
    row = pl.BlockSpec((CPS * C, D_MODEL), lambda c: (c, 0))
    return pl.pallas_call(
        body,
        out_shape=(jax.ShapeDtypeStruct((T, D_MODEL), BF16), jax.ShapeDtypeStruct((T, D_MODEL), F32),
                   jax.ShapeDtypeStruct((T // C, H, HD, HD), F32)),
        grid=(T // (CPS * C),),
        in_specs=[col(0), col(1), col(2), col(3), pl.BlockSpec((1, D_MODEL), lambda c: (0, 0)),
                  pl.BlockSpec((1, HD), lambda c: (0, 0))],
        out_specs=(row, row, pl.BlockSpec((CPS, H, HD, HD), lambda c: (c, 0, 0, 0))),
        scratch_shapes=[pltpu.VMEM((H, HD, HD), F32)],
        compiler_params=_params("arbitrary"), name=name)(proj, proj, proj, proj, lb, gain)


def _hgrn_bwd(proj, o_pre, d_og, states, lb, gain, name):
    T = proj.shape[0]
    C = HGRN_CHUNK
    CPS = HGRN_STEP_CHUNKS
    H, HD = HGRN_HEADS, HGRN_DIM
    NC = T // (CPS * C)

    def body(q_ref, f_ref, i_ref, g_ref, o_ref, dog_ref, st_ref, lb_ref, gain_ref,
             dproj_ref, dlb_ref, dgain_ref, ds_scr, dq_all, dk_all, db_all):
        @pl.when(pl.program_id(0) == 0)
        def _():
            ds_scr[...] = jnp.zeros_like(ds_scr)
            dlb_ref[...] = jnp.zeros_like(dlb_ref)
            dgain_ref[...] = jnp.zeros_like(dgain_ref)

        lbv = lb_ref[...]
        causal = _tri(C, True)
        last_row = lax.broadcasted_iota(jnp.int32, (C, HD), 0) == C - 1
        gain_v = gain_ref[...]
        heads = [slice(h * HD, (h + 1) * HD) for h in range(H)]
        hs = range(H)
        ds_t = [ds_scr[h] for h in hs]
        dgain = None
        for cc in reversed(range(CPS)):
            rows = slice(cc * C, (cc + 1) * C)
            dq_scr, dk_scr, db_scr = dq_all.at[cc], dk_all.at[cc], db_all.at[cc]
            q_raw = q_ref[rows, :]
            gt = _hgrn_gates(q_raw, f_ref[rows, :], lbv)
            o = [o_ref[rows, sl] for sl in heads]
            rstd = [lax.rsqrt(jnp.mean(x * x, axis=-1, keepdims=True) + NORM_EPS) for x in o]
            n = [x * r for x, r in zip(o, rstd)]
            g_raw = [g_ref[rows, sl] for sl in heads]
            sg = [_sigmoid(x) for x in g_raw]
            d_out = [dog_ref[rows, sl] for sl in heads]
            dy = [d * (g * s) for d, g, s in zip(d_out, g_raw, sg)]
            dn = [x * gain_v for x in dy]
            do = [(rstd[h] * (dn[h] - n[h] * jnp.mean(dn[h] * n[h], axis=-1, keepdims=True))).astype(BF16) for h in hs]
            for h in hs:
                dgain = dy[h] * n[h] if dgain is None else dgain + dy[h] * n[h]
            for h, sl in enumerate(heads):
                dproj_ref[rows, 3 * D_MODEL + h * HD:3 * D_MODEL + (h + 1) * HD] = (
                    d_out[h] * n[h] * gain_v * (sg[h] * (1.0 + g_raw[h] * (1.0 - sg[h])))).astype(BF16)
            q_ab, k_bb = gt["q_a"].astype(BF16), gt["k_b"].astype(BF16)
            q_hb, k_tb = gt["q_hat"].astype(BF16), gt["k_til"].astype(BF16)
            v = i_ref[rows, :].astype(BF16)
            s_t = [st_ref[cc, h] for h in hs]
            ds_b = [x.astype(BF16) for x in ds_t]
            p = [jnp.where(causal, _dot(q_ab[:, sl], k_bb[:, sl], NT), 0.0).astype(BF16) for sl in heads]
            dp = [jnp.where(causal, _dot(do[h], v[:, sl], NT), 0.0).astype(BF16) for h, sl in enumerate(heads)]
            dv = [_dot(p[h], do[h], TN) + _dot(k_tb[:, sl], ds_b[h], NT) for h, sl in enumerate(heads)]
            dq_a = [_dot(dp[h], k_bb[:, sl], NN) for h, sl in enumerate(heads)]
            dk_b = [_dot(dp[h], q_ab[:, sl], TN) for h, sl in enumerate(heads)]
            dq_hat = [_dot(do[h], s_t[h].astype(BF16), NN) for h in hs]
            dk_til = [_dot(v[:, sl], ds_b[h], NN) for h, sl in enumerate(heads)]
            ds_new = [_dot(do[h], q_hb[:, sl], TN) + gt["e_last"][:, sl] * ds_t[h] for h, sl in enumerate(heads)]
            for h, sl in enumerate(heads):
                k_til = gt["k_til"][:, sl]
                db_last = jnp.sum(ds_t[h] * gt["e_last"][:, sl] * s_t[h], axis=0, keepdims=True) + jnp.sum(
                    dk_til[h] * k_til, axis=0, keepdims=True)
                dproj_ref[rows, 2 * D_MODEL + h * HD:2 * D_MODEL + (h + 1) * HD] = dv[h].astype(BF16)
                dq_scr[:, sl] = dq_a[h] * gt["e_a"][:, sl] + dq_hat[h] * gt["e_q"][:, sl]
                dk_scr[:, sl] = dk_b[h] * gt["e_b"][:, sl] + dk_til[h] * gt["e_k"][:, sl]
                db = (dq_a[h] * q_ab[:, sl].astype(F32) + dq_hat[h] * gt["q_hat"][:, sl]
                      - dk_b[h] * k_bb[:, sl].astype(F32) - dk_til[h] * k_til)
                db_scr[:, sl] = db + jnp.where(last_row, db_last, 0.0)
            dlogf = _running_sum(db_scr[...], False)
            sig_f, forget, sig_q = gt["sig_f"], gt["forget"], gt["sig_q"]
            dforget = dlogf / forget - dk_scr[...]
            dproj_ref[rows, D_MODEL:2 * D_MODEL] = (dforget * (1.0 - lbv) * sig_f * (1.0 - sig_f)).astype(BF16)
            dlb_ref[...] += jnp.sum(dforget * (1.0 - sig_f), axis=0, keepdims=True)
            dproj_ref[rows, 0:D_MODEL] = (dq_scr[...] * (sig_q * (1.0 + q_raw * (1.0 - sig_q)))).astype(BF16)
            ds_t = ds_new
        dgain_ref[...] += jnp.sum(dgain, axis=0, keepdims=True)
        for h in hs:
            ds_scr[h] = ds_t[h]

    col = lambda j: pl.BlockSpec((CPS * C, D_MODEL), lambda c: (NC - 1 - c, j))
    row = pl.BlockSpec((CPS * C, D_MODEL), lambda c: (NC - 1 - c, 0))
    return pl.pallas_call(
        body,
        out_shape=(jax.ShapeDtypeStruct((T, 4 * D_MODEL), BF16), jax.ShapeDtypeStruct((1, D_MODEL), F32),
                   jax.ShapeDtypeStruct((1, HD), F32)),
        grid=(NC,),
        in_specs=[col(0), col(1), col(2), col(3), row, row,
                  pl.BlockSpec((CPS, H, HD, HD), lambda c: (NC - 1 - c, 0, 0, 0)),
                  pl.BlockSpec((1, D_MODEL), lambda c: (0, 0)), pl.BlockSpec((1, HD), lambda c: (0, 0))],
        out_specs=(pl.BlockSpec((CPS * C, 4 * D_MODEL), lambda c: (NC - 1 - c, 0)),
                   pl.BlockSpec((1, D_MODEL), lambda c: (0, 0)), pl.BlockSpec((1, HD), lambda c: (0, 0))),
        scratch_shapes=[pltpu.VMEM((H, HD, HD), F32)] + [pltpu.VMEM((CPS, C, D_MODEL), F32)] * 3,
        compiler_params=_params("arbitrary"), name=name)(proj, proj, proj, proj, o_pre, d_og, states, lb, gain)


def _attn_masks():
    r = lax.broadcasted_iota(jnp.int32, (ATTN_BLOCK, ATTN_BLOCK), 0)
    c = lax.broadcasted_iota(jnp.int32, (ATTN_BLOCK, ATTN_BLOCK), 1)
    return c >= r, c <= r


def _attn_fwd(qkv, dilation, name):
    T = qkv.shape[0]
    nb = T // dilation // ATTN_BLOCK
    W = ATTN_GROUP_WIDTH
    B = ATTN_BLOCK
    scale = ATTN_DIM ** -0.5
    qb = 2 if nb % 2 == 0 else 1
    steps = nb // qb

    def body(q_ref, kp_ref, kc_ref, vp_ref, vc_ref, o_ref, lse_ref):
        no_prev = jnp.where(pl.program_id(1) > 0, 0.0, NEG_BIG)
        m_prev, m_cur = _attn_masks()
        ones = jnp.ones((B, ATTN_DIM), BF16)
        items = []
        for j in range(qb):
            for h in range(ATTN_GROUP_HEADS):
                sl = slice(h * ATTN_DIM, (h + 1) * ATTN_DIM)
                rows = slice(j * B, (j + 1) * B)
                if j == 0:
                    items.append((rows, sl, kp_ref[:, sl], vp_ref[:, sl], no_prev))
                else:
                    before = slice((j - 1) * B, j * B)
                    items.append((rows, sl, kc_ref[before, sl], vc_ref[before, sl], 0.0))
        s_p = [jnp.where(m_prev, _dot(q_ref[rows, sl], k_p, NT) * scale + bias, NEG_BIG)
               for rows, sl, k_p, _, bias in items]
        s_c = [jnp.where(m_cur, _dot(q_ref[rows, sl], kc_ref[rows, sl], NT) * scale, NEG_BIG)
               for rows, sl, _, _, _ in items]
        m = [jnp.max(jnp.maximum(a, b), axis=-1, keepdims=True) for a, b in zip(s_p, s_c)]
        p_p = [jnp.exp(a - mx).astype(BF16) for a, mx in zip(s_p, m)]
        p_c = [jnp.exp(b - mx).astype(BF16) for b, mx in zip(s_c, m)]
        l = [_dot(a, ones, NN) + _dot(b, ones, NN) for a, b in zip(p_p, p_c)]
        acc = [_dot(a, v_p, NN) + _dot(b, vc_ref[rows, sl], NN)
               for a, b, (rows, sl, _, v_p, _) in zip(p_p, p_c, items)]
        for (rows, sl, _, _, _), a, lv, mx in zip(items, acc, l, m):
            o_ref[rows, sl] = (a / lv).astype(BF16)
            lse_ref[rows, sl] = mx + jnp.log(lv)

    cur = lambda col: pl.BlockSpec((qb * B, W), lambda s, n: (s * steps + n, col))
    prev = lambda col: pl.BlockSpec((B, W), lambda s, n: (s * nb + jnp.maximum(qb * n - 1, 0), col))
    out = pl.BlockSpec((qb * B, W), lambda s, n: (s * steps + n, 0))
    return pl.pallas_call(
        body, out_shape=(jax.ShapeDtypeStruct((T, W), BF16), jax.ShapeDtypeStruct((T, W), F32)),
        grid=(dilation, steps),
        in_specs=[cur(0), prev(1), cur(1), prev(2), cur(2)],
        out_specs=(out, out), compiler_params=_params("parallel", "arbitrary"), name=name)(qkv, qkv, qkv, qkv, qkv)


def _attn_bwd(qkv, d_out, lse, delta, cos, sin, dilation, name):
    T = qkv.shape[0]
    nb = T // dilation // ATTN_BLOCK
    assert nb % 2 == 0, "an even number of 128-token blocks per residue class"
    pairs = nb // 2
    W = ATTN_GROUP_WIDTH
    B = ATTN_BLOCK
    scale = ATTN_DIM ** -0.5

    def unrope(x, cos_v, sin_v):
        return x * cos_v + pltpu.roll(x * sin_v, ATTN_DIM // 2, 1)

    def body(qa_ref, qb_ref, kpair_ref, kc_ref, vpair_ref, vc_ref, doa_ref, dob_ref, lsea_ref, lseb_ref,
             dla_ref, dlb_ref, cos_ref, sin_ref, out_ref, dq_scr, dk_scr, dv_scr):
        n = pl.program_id(1)

        @pl.when(n == 0)
        def _():
            dq_scr[...] = jnp.zeros_like(dq_scr)
            dk_scr[...] = jnp.zeros_like(dk_scr)
            dv_scr[...] = jnp.zeros_like(dv_scr)

        no_a = jnp.where(n > 0, 0.0, NEG_BIG)
        no_b = jnp.where(n < pairs, 0.0, NEG_BIG)
        m_prev, m_cur = _attn_masks()
        lo, hi = slice(0, B), slice(B, 2 * B)
        heads = [slice(h * ATTN_DIM, (h + 1) * ATTN_DIM) for h in range(ATTN_GROUP_HEADS)]
        flat = []
        for sl in heads:
            qa, qb = qa_ref[:, sl], qb_ref[:, sl]
            doa, dob = doa_ref[:, sl], dob_ref[:, sl]
            k0, k1, k2 = kpair_ref[lo, sl], kpair_ref[hi, sl], kc_ref[:, sl]
            v0, v1, v2 = vpair_ref[lo, sl], vpair_ref[hi, sl], vc_ref[:, sl]
            flat += [(qa, doa, lsea_ref[:, sl], dla_ref[:, sl], k0, v0, m_prev, no_a),
                     (qa, doa, lsea_ref[:, sl], dla_ref[:, sl], k1, v1, m_cur, no_a),
                     (qb, dob, lseb_ref[:, sl], dlb_ref[:, sl], k1, v1, m_prev, no_a + no_b),
                     (qb, dob, lseb_ref[:, sl], dlb_ref[:, sl], k2, v2, m_cur, no_b)]
        s = [_dot(q, k, NT) for q, _, _, _, k, _, _, _ in flat]
        dp = [_dot(do, v, NT) for _, do, _, _, _, v, _, _ in flat]
        p = [jnp.where(mask, jnp.exp(sv * scale - lse_v + bias), 0.0)
             for sv, (_, _, lse_v, _, _, _, mask, bias) in zip(s, flat)]
        ds = [(pv * (dpv - dl_v) * scale).astype(BF16) for pv, dpv, (_, _, _, dl_v, _, _, _, _) in zip(p, dp, flat)]
        p = [pv.astype(BF16) for pv in p]
        dq_part = [_dot(dsv, k, NN) for dsv, (_, _, _, _, k, _, _, _) in zip(ds, flat)]
        dk_part = [_dot(dsv, q, TN) for dsv, (q, _, _, _, _, _, _, _) in zip(ds, flat)]
        dv_part = [_dot(pv, do, TN) for pv, (_, do, _, _, _, _, _, _) in zip(p, flat)]
        cos_lo, sin_lo, cos_hi, sin_hi = cos_ref[lo, :], sin_ref[lo, :], cos_ref[hi, :], sin_ref[hi, :]
        for h, sl in enumerate(heads):
            a_prev, a_cur, b_prev, b_cur = range(4 * h, 4 * h + 4)
            kcol = slice(W + h * ATTN_DIM, W + (h + 1) * ATTN_DIM)
            vcol = slice(2 * W + h * ATTN_DIM, 2 * W + (h + 1) * ATTN_DIM)
            out_ref[lo, sl] = unrope(dq_scr[:, sl], cos_lo, sin_lo).astype(BF16)
            out_ref[hi, sl] = unrope(dq_part[a_prev] + dq_part[a_cur], cos_hi, sin_hi).astype(BF16)
            out_ref[lo, kcol] = unrope(dk_scr[:, sl] + dk_part[a_prev], cos_lo, sin_lo).astype(BF16)
            out_ref[hi, kcol] = unrope(dk_part[a_cur] + dk_part[b_prev], cos_hi, sin_hi).astype(BF16)
            out_ref[lo, vcol] = (dv_scr[:, sl] + dv_part[a_prev]).astype(BF16)
            out_ref[hi, vcol] = (dv_part[a_cur] + dv_part[b_prev]).astype(BF16)
            dq_scr[:, sl] = dq_part[b_prev] + dq_part[b_cur]
            dk_scr[:, sl] = dk_part[b_cur]
            dv_scr[:, sl] = dv_part[b_cur]

    def block_a(n):
        return jnp.maximum(2 * n - 1, 0)

    def block_b(n):
        return jnp.minimum(2 * n, nb - 1)

    def pair(n):
        return jnp.maximum(n - 1, 0)

    one_a = lambda col: pl.BlockSpec((B, W), lambda s, n: (s * nb + block_a(n), col))
    one_b = lambda col: pl.BlockSpec((B, W), lambda s, n: (s * nb + block_b(n), col))
    two = lambda col: pl.BlockSpec((2 * B, W), lambda s, n: (s * pairs + pair(n), col))
    tab = pl.BlockSpec((2 * B, ATTN_DIM), lambda s, n: (s * pairs + pair(n), 0))
    return pl.pallas_call(
        body, out_shape=jax.ShapeDtypeStruct((T, 3 * W), BF16), grid=(dilation, pairs + 1),
        in_specs=[one_a(0), one_b(0), two(1), one_b(1), two(2), one_b(2), one_a(0), one_b(0), one_a(0), one_b(0),
                  one_a(0), one_b(0), tab, tab],
        out_specs=pl.BlockSpec((2 * B, 3 * W), lambda s, n: (s * pairs + pair(n), 0)),
        scratch_shapes=[pltpu.VMEM((B, W), F32)] * 3,
        compiler_params=_params("parallel", "arbitrary"), name=name)(
            qkv, qkv, qkv, qkv, qkv, qkv, d_out, d_out, lse, lse, delta, delta, cos, sin)


PERM_TILE = 512
LANES = 128


def _residue_view(x, d):
    return x if d == 1 else x.reshape(d, x.shape[0] // d, x.shape[1])


def _residue_spec(d, tm, cols):
    if d == 1:
        return pl.BlockSpec((tm, cols), lambda i: (i, 0))
    return pl.BlockSpec((d, tm // d, cols), lambda i: (0, i, 0))


def _residue_shape(T, d, cols, dtype):
    return jax.ShapeDtypeStruct((T, cols) if d == 1 else (d, T // d, cols), dtype)


def _class_rows(r, d, tm):
    return pl.ds(r, tm // d, stride=d)


def _attn_norm(h, gain, name):
    T = h.shape[0]
    tm = _pick_tile(T, PERM_TILE, 16 * max(ATTN_DILATIONS))
    dils = ATTN_DILATIONS
    (base_cos, base_sin), (off_cos, off_sin), sign = _rope_parts(T, tm)

    def body(h_ref, g_ref, bc_ref, bs_ref, oc_ref, os_ref, sign_ref, *refs):
        u_refs, c_refs, s_refs, u_scr, c_scr, s_scr = refs[0:3], refs[3:6], refs[6:9], refs[9], refs[10], refs[11]
        hv = h_ref[...]
        rstd = lax.rsqrt(jnp.mean(hv * hv, axis=-1, keepdims=True) + NORM_EPS)
        u = hv * rstd * g_ref[...]
        for j in range(D_MODEL // LANES):
            u_scr[j] = u[:, j * LANES:(j + 1) * LANES]
        bc, bs, oc, osn = bc_ref[0], bs_ref[0], oc_ref[...], os_ref[...]
        c_scr[...] = bc * oc - bs * osn
        s_scr[...] = (bs * oc + bc * osn) * sign_ref[...]
        for d, u_ref, c_ref, s_ref in zip(dils, u_refs, c_refs, s_refs):
            if d == 1:
                u_ref[...] = u.astype(BF16)
                c_ref[...] = c_scr[...]
                s_ref[...] = s_scr[...]
                continue
            for r in range(d):
                rows = _class_rows(r, d, tm)
                for j in range(D_MODEL // LANES):
                    u_ref[r, :, j * LANES:(j + 1) * LANES] = u_scr.at[j][rows, :].astype(BF16)
                c_ref[r] = c_scr[rows, :]
                s_ref[r] = s_scr[rows, :]

    row = pl.BlockSpec((tm, D_MODEL), lambda i: (i, 0))
    base = pl.BlockSpec((1, 1, ATTN_DIM), lambda i: (i, 0, 0))
    off = pl.BlockSpec((tm, ATTN_DIM), lambda i: (0, 0))
    res = pl.pallas_call(
        body,
        out_shape=([_residue_shape(T, d, D_MODEL, BF16) for d in dils]
                   + [_residue_shape(T, d, ATTN_DIM, F32) for d in dils] * 2),
        grid=(T // tm,),
        in_specs=[row, pl.BlockSpec((1, D_MODEL), lambda i: (0, 0)), base, base, off, off,
                  pl.BlockSpec((1, ATTN_DIM), lambda i: (0, 0))],
        out_specs=([_residue_spec(d, tm, D_MODEL) for d in dils] + [_residue_spec(d, tm, ATTN_DIM) for d in dils] * 2),
        scratch_shapes=[pltpu.VMEM((D_MODEL // LANES, tm, LANES), F32), pltpu.VMEM((tm, ATTN_DIM), F32),
                        pltpu.VMEM((tm, ATTN_DIM), F32)],
        compiler_params=_params("parallel"), name=name)(h, gain, base_cos, base_sin, off_cos, off_sin, sign)
    flat = [r.reshape(T, r.shape[-1]) for r in res]
    return flat[0:3], flat[3:6], flat[6:9]


def _attn_merge_fwd(outs, lses, name):
    T = outs[0].shape[0]
    W = ATTN_GROUP_WIDTH
    tm = _pick_tile(T, PERM_TILE, 16 * max(ATTN_DILATIONS))
    dils = ATTN_DILATIONS

    def body(*refs):
        o_refs, l_refs, oc_ref, lse_refs = refs[0:3], refs[3:6], refs[6], refs[7:10]
        o_scr, l_scr, t_scr = refs[10:13]
        nh = ATTN_GROUP_HEADS
        for g, d in enumerate(dils):
            for j in range(nh):
                lanes = slice(j * LANES, (j + 1) * LANES)
                if d == 1:
                    o_scr[g * nh + j] = o_refs[g][:, lanes].astype(F32)
                    l_scr[g * nh + j] = l_refs[g][:, lanes]
                    continue
                for r in range(d):
                    rows = _class_rows(r, d, tm)
                    o_scr.at[g * nh + j][rows, :] = o_refs[g][r, :, lanes].astype(F32)
                    l_scr.at[g * nh + j][rows, :] = l_refs[g][r, :, lanes]
        for j in range(nh):
            lanes = slice(j * LANES, (j + 1) * LANES)
            ls = [l_scr[g * nh + j] for g in range(3)]
            m = jnp.maximum(jnp.maximum(ls[0], ls[1]), ls[2])
            tot = m + jnp.log(jnp.exp(ls[0] - m) + jnp.exp(ls[1] - m) + jnp.exp(ls[2] - m))
            t_scr[j] = tot
            for g, d in enumerate(dils):
                oc_ref[:, g * W + j * LANES:g * W + (j + 1) * LANES] = (
                    o_scr[g * nh + j] * jnp.exp(ls[g] - tot)).astype(BF16)
                if d == 1:
                    lse_refs[g][:, lanes] = tot
                    continue
                for r in range(d):
                    lse_refs[g][r, :, lanes] = t_scr.at[j][_class_rows(r, d, tm), :]

    in_blk = [_residue_spec(d, tm, W) for d in dils]
    n_blk = 3 * ATTN_GROUP_HEADS
    res = pl.pallas_call(
        body, out_shape=[jax.ShapeDtypeStruct((T, 3 * W), BF16)] + [_residue_shape(T, d, W, F32) for d in dils],
        grid=(T // tm,), in_specs=in_blk * 2,
        out_specs=[pl.BlockSpec((tm, 3 * W), lambda i: (i, 0))] + in_blk,
        scratch_shapes=[pltpu.VMEM((n_blk, tm, LANES), F32), pltpu.VMEM((n_blk, tm, LANES), F32),
                        pltpu.VMEM((ATTN_GROUP_HEADS, tm, LANES), F32)],
        compiler_params=_params("parallel"), name=name)(
            *[_residue_view(o, d) for o, d in zip(outs, dils)], *[_residue_view(l, d) for l, d in zip(lses, dils)])
    return res[0], [r.reshape(T, W) for r in res[1:]]


def _attn_merge_bwd(d_oc, oc, name):
    T = d_oc.shape[0]
    W = ATTN_GROUP_WIDTH
    tm = _pick_tile(T, PERM_TILE, 16 * max(ATTN_DILATIONS))
    dils = ATTN_DILATIONS

    def body(d_ref, o_ref, *refs):
        delta_refs, db_refs, dl_scr, d_scr = refs[0:3], refs[3:6], refs[6], refs[7]
        nh = ATTN_GROUP_HEADS
        for j in range(nh):
            tot = jnp.zeros((tm, 1), F32)
            for g in range(3):
                cols = slice(g * W + j * LANES, g * W + (j + 1) * LANES)
                d_blk = d_ref[:, cols]
                d_scr[g * nh + j] = d_blk
                tot = tot + jnp.sum(d_blk * o_ref[:, cols].astype(F32), axis=-1, keepdims=True)
            dl_scr[j] = jnp.broadcast_to(tot, (tm, LANES))
        for g, d in enumerate(dils):
            for j in range(nh):
                lanes = slice(j * LANES, (j + 1) * LANES)
                if d == 1:
                    delta_refs[g][:, lanes] = dl_scr[j]
                    db_refs[g][:, lanes] = d_scr[g * nh + j].astype(BF16)
                    continue
                for r in range(d):
                    rows = _class_rows(r, d, tm)
                    delta_refs[g][r, :, lanes] = dl_scr.at[j][rows, :]
                    db_refs[g][r, :, lanes] = d_scr.at[g * nh + j][rows, :].astype(BF16)

    wide = pl.BlockSpec((tm, 3 * W), lambda i: (i, 0))
    out_blk = [_residue_spec(d, tm, W) for d in dils]
    res = pl.pallas_call(
        body, out_shape=[_residue_shape(T, d, W, F32) for d in dils] + [_residue_shape(T, d, W, BF16) for d in dils],
        grid=(T // tm,), in_specs=[wide, wide], out_specs=out_blk * 2,
        scratch_shapes=[pltpu.VMEM((ATTN_GROUP_HEADS, tm, LANES), F32),
                        pltpu.VMEM((3 * ATTN_GROUP_HEADS, tm, LANES), F32)],
        compiler_params=_params("parallel"), name=name)(d_oc, oc)
    flat = [r.reshape(T, W) for r in res]
    return flat[0:3], flat[3:6]


def _rope_parts(T, tile):
    inv_freq = 1.0 / (ROPE_THETA ** (jnp.arange(0, ATTN_DIM, 2, dtype=F32) / ATTN_DIM))
    inv_freq = jnp.concatenate([inv_freq, inv_freq])[None, :]
    base = (jnp.arange(T // tile, dtype=F32) * tile)[:, None] * inv_freq
    off = jnp.arange(tile, dtype=F32)[:, None] * inv_freq
    sign = jnp.concatenate([-jnp.ones((1, ATTN_DIM // 2), F32), jnp.ones((1, ATTN_DIM // 2), F32)], axis=1)
    return (jnp.cos(base)[:, None, :], jnp.sin(base)[:, None, :]), (jnp.cos(off), jnp.sin(off)), sign


WEIGHT_GROUPS = {"hgrn": ("hgrn_in", "hgrn_out"), "ffn0": ("ffn_in0", "ffn_down0"),
                 "attn": ("qkv", "attn_out"), "ffn1": ("ffn_in1", "ffn_down1")}


def _local_step(x, target, norm_mix, norm_ffn, lb, out_gain, final_gain, fetch, publish):
    g_mix = [norm_mix[0:1], norm_mix[1:2]]
    g_ffn = [norm_ffn[0:1], norm_ffn[1:2]]
    w = {}

    def whole(name):
        return [(w[name], w[name].shape[0], 0)]

    def qkv_parts(g):
        return [(w["qkv"], ATTN_GROUP_WIDTH, 3 * j + g) for j in range(3)]

    def ffn_fwd(h, layer, head=None):
        w.update(fetch(f"ffn{layer}"))
        n, gate, up, a = _ffn_in(h, g_ffn[layer], w[f"ffn_in{layer}"], f"ffn{layer}_in")
        out = _mm_nn([a], [whole(f"ffn_down{layer}")], h, name=f"ffn{layer}_down", head=head)
        return out, (n, gate, up, a)

    def ffn_bwd(h, saved, dh, dhb, layer):
        n, gate, up, a = saved
        w_in = w[f"ffn_in{layer}"]
        dgate, dup = _ffn_down_dx(dhb, w[f"ffn_down{layer}"], gate, up, f"ffn{layer}_down_dx")
        grads = {f"ffn_down{layer}": _mm_tn([a], dhb, name=f"ffn{layer}_down_dw"),
                 f"ffn_in{layer}": _mm_tn([dgate, dup], n, name=f"ffn{layer}_in_dw")}
        publish(f"ffn{layer}", grads)
        return _mm_nn([dgate, dup], [[(w_in, D_FF, 0)], [(w_in, D_FF, 1)]], dh, name=f"ffn{layer}_in_dx",
                      norm=(h, g_ffn[layer]))

    u0 = _rms_fwd(x, g_mix[0], "hgrn_norm")
    w.update(fetch("hgrn"))
    proj = _mm_nt(u0, whole("hgrn_in"), out_dtype=F32, name="hgrn_in")
    og, o_pre, states = _hgrn_fwd(proj, lb, out_gain, "hgrn_fwd")
    h1 = _mm_nn([og], [whole("hgrn_out")], x, name="hgrn_out")
    h2, ffn0 = ffn_fwd(h1, 0)

    u1_g, cos_g, sin_g = _attn_norm(h2, g_mix[1], "attn_norm")
    w.update(fetch("attn"))
    qkv_g, outs, lses = [], [], []
    for g, d in enumerate(ATTN_DILATIONS):
        qkv_g.append(_mm_nt(u1_g[g], qkv_parts(g), out_dtype=BF16, name=f"attn_qkv{g}",
                            rope=(cos_g[g], sin_g[g], 2)))
        o_g, lse_g = _attn_fwd(qkv_g[g], d, f"attn_fwd{g}")
        outs.append(o_g)
        lses.append(lse_g)
    oc, lse_all = _attn_merge_fwd(outs, lses, "attn_merge")
    h3 = _mm_nn([oc], [whole("attn_out")], h2, name="attn_out")
    (dh4, dh4b, d_final, loss_part), ffn1 = ffn_fwd(h3, 1, head=(target, final_gain))
    dh3, dh3b, d_ffn1 = ffn_bwd(h3, ffn1, dh4, dh4b, 1)

    d_oc = _mm_nt(dh3b, whole("attn_out"), out_dtype=F32, name="attn_out_dx")
    grad_attn_out = _mm_tn([oc], dh3b, name="attn_out_dw")
    delta, d_ocb = _attn_merge_bwd(d_oc, oc, "attn_merge_bwd")
    du1, qkv_pieces = [], []
    for g, d in enumerate(ATTN_DILATIONS):
        dqkv = _attn_bwd(qkv_g[g], d_ocb[g], lse_all[g], delta[g], cos_g[g], sin_g[g], d, f"attn_bwd{g}")
        qkv_pieces.append(_mm_tn([dqkv], u1_g[g], name=f"attn_qkv_dw{g}"))
        du1.append(_mm_nn([dqkv], [qkv_parts(g)], None, name=f"attn_qkv_dx{g}"))
    grad_qkv = jnp.stack([p.reshape(3, ATTN_GROUP_WIDTH, D_MODEL) for p in qkv_pieces], axis=1).reshape(
        3 * ATTN_WIDTH, D_MODEL)
    publish("attn", {"qkv": grad_qkv, "attn_out": grad_attn_out})
    dh2, dh2b, d_mix1 = _rms_bwd(h2, g_mix[1], du1, dh3, "attn_norm_bwd", ATTN_DILATIONS)

    dh1, dh1b, d_ffn0 = ffn_bwd(h1, ffn0, dh2, dh2b, 0)

    d_og = _mm_nt(dh1b, whole("hgrn_out"), out_dtype=F32, name="hgrn_out_dx")
    grad_hgrn_out = _mm_tn([og], dh1b, name="hgrn_out_dw")
    dproj, d_lb, d_out_gain = _hgrn_bwd(proj, o_pre, d_og, states, lb, out_gain, "hgrn_bwd")
    publish("hgrn", {"hgrn_in": _mm_tn([dproj], u0, name="hgrn_in_dw"), "hgrn_out": grad_hgrn_out})
    dx, _, d_mix0 = _mm_nn([dproj], [whole("hgrn_in")], dh1, name="hgrn_in_dx", norm=(x, g_mix[0]))

    small = dict(norm_mix0=d_mix0, norm_mix1=d_mix1, norm_ffn0=d_ffn0, norm_ffn1=d_ffn1, lb=d_lb,
                 out_gain=d_out_gain, final=d_final, loss=loss_part)
    return dx, small


MESH_IDS = pl.DeviceIdType.MESH
HBM_SPEC = pl.BlockSpec(memory_space=pl.ANY)


N_PEERS = N_DEV - 1
PEER_OFFSETS = [(dx, dy, dc) for dx in (0, 1) for dy in (0, 1) for dc in (0, 1)][1:]


def _mesh_place():
    x, y, c = lax.axis_index("x"), lax.axis_index("y"), lax.axis_index("c")
    peers = []
    for dx, dy, dc in PEER_OFFSETS:
        px, py, pc = (1 - x if dx else x), (1 - y if dy else y), (1 - c if dc else c)
        peers.append(((px, py, pc), 4 * px + 2 * py + pc))
    return 4 * x + 2 * y + c, peers


def _gather_over_two_levels(src_refs, land_refs, send_sems, recv_sems):
    n = len(src_refs)
    x, y, c = lax.axis_index("x"), lax.axis_index("y"), lax.axis_index("c")
    me, sibling = (x, y, c), (x, y, 1 - c)
    chips = [(1 - x, y), (x, 1 - y), (1 - x, 1 - y)]

    def block(w, px, py, pc):
        return land_refs[w].at[4 * px + 2 * py + pc]

    def copy(w, k, owner, to, src=None):
        return pltpu.make_async_remote_copy(
            src_ref=block(w, *owner) if src is None else src, dst_ref=block(w, *owner),
            send_sem=send_sems.at[w * N_PEERS + k], recv_sem=recv_sems.at[w * N_PEERS + k],
            device_id=to, device_id_type=MESH_IDS)

    sent = []
    for w in range(n):
        sent.append(copy(w, 0, me, sibling, src=src_refs[w]))
        sent += [copy(w, 1 + j, me, (*chip, c), src=src_refs[w]) for j, chip in enumerate(chips)]
    for cp in sent:
        cp.start()
    for w in range(n):
        for j, chip in enumerate(chips):
            copy(w, 1 + j, (*chip, c), me).wait_recv()
            passed = copy(w, 4 + j, (*chip, c), sibling)
            passed.start()
            sent.append(passed)
    for w in range(n):
        copy(w, 0, sibling, me).wait_recv()
        for j, chip in enumerate(chips):
            copy(w, 4 + j, (*chip, 1 - c), me).wait_recv()
    for cp in sent:
        cp.wait_send()


def _exchange_launch(srcs, scatter, collective_id, name):
    n = len(srcs)
    src_refs = [jax.new_ref(s, memory_space=pltpu.MemorySpace.HBM) for s in srcs]
    land_refs = [jax.empty_ref(jax.ShapeDtypeStruct(s.shape if scatter else (N_DEV,) + s.shape, s.dtype),
                               memory_space=pltpu.MemorySpace.HBM) for s in srcs]

    @pl.kernel(mesh=plsc.ScalarSubcoreMesh(axis_name="sequencer", num_cores=1), name=name,
               scratch_types=(pltpu.SemaphoreType.DMA((n * N_PEERS,)), pltpu.SemaphoreType.DMA((n * N_PEERS,)),
                              pltpu.SemaphoreType.DMA((n,))),
               compiler_params=pltpu.CompilerParams(collective_id=collective_id))
    def launch(send_sems, recv_sems, local_sems):
        me, peers = _mesh_place()
        barrier = pltpu.get_barrier_semaphore()
        for peer, _ in peers:
            pl.semaphore_signal(barrier, inc=1, device_id=peer, device_id_type=MESH_IDS)
        pl.semaphore_wait(barrier, N_PEERS)
        own = [pltpu.make_async_copy(src_refs[w].at[me] if scatter else src_refs[w], land_refs[w].at[me],
                                     local_sems.at[w]) for w in range(n)]
        for cp in own:
            cp.start()
        if scatter:
            copies = [pltpu.make_async_remote_copy(
                src_ref=src_refs[w].at[pid], dst_ref=land_refs[w].at[me],
                send_sem=send_sems.at[w * N_PEERS + k], recv_sem=recv_sems.at[w * N_PEERS + k],
                device_id=peer, device_id_type=MESH_IDS) for w in range(n) for k, (peer, pid) in enumerate(peers)]
            for cp in copies:
                cp.start()
            for cp in copies:
                cp.wait()
        else:
            _gather_over_two_levels(src_refs, land_refs, send_sems, recv_sems)
        for cp in own:
            cp.wait()

    launch()
    return land_refs


def _gather_small(block, name):
    def body(in_ref, out_ref, send_sems, recv_sems, local_sem):
        me, peers = _mesh_place()
        own = pltpu.make_async_copy(in_ref, out_ref.at[me], local_sem)
        own.start()
        sends = [pltpu.make_async_remote_copy(
            src_ref=in_ref, dst_ref=out_ref.at[me], send_sem=send_sems.at[k], recv_sem=recv_sems.at[k],
            device_id=peer, device_id_type=MESH_IDS) for k, (peer, _) in enumerate(peers)]
        for cp in sends:
            cp.start()
        for cp in sends:
            cp.wait_recv()
        for cp in sends:
            cp.wait_send()
        own.wait()

    return pl.pallas_call(
        body, out_shape=jax.ShapeDtypeStruct((N_DEV,) + block.shape, block.dtype),
        in_specs=[HBM_SPEC], out_specs=HBM_SPEC,
        scratch_shapes=[pltpu.SemaphoreType.DMA((N_PEERS,)), pltpu.SemaphoreType.DMA((N_PEERS,)),
                        pltpu.SemaphoreType.DMA],
        name=name)(block)


def _sum_blocks(recv, name):
    rows = recv.shape[1]
    tr = _pick_tile(rows, 256, 16)

    def body(r_ref, g_ref):
        acc = r_ref[0].astype(F32)
        for j in range(1, N_DEV):
            acc = acc + r_ref[j].astype(F32)
        g_ref[...] = acc

    return pl.pallas_call(
        body, out_shape=jax.ShapeDtypeStruct((rows, D_MODEL), F32), grid=(rows // tr,),
        in_specs=[pl.BlockSpec((N_DEV, tr, D_MODEL), lambda i: (0, i, 0))],
        out_specs=pl.BlockSpec((tr, D_MODEL), lambda i: (i, 0)),
        compiler_params=_params("parallel"), name=name)(recv)


def _adamw_math(w, g, m, v):
    m_new = ADAM_B1 * m + (1.0 - ADAM_B1) * g
    v_new = ADAM_B2 * v + (1.0 - ADAM_B2) * (g * g)
    m_hat = m_new / (1.0 - ADAM_B1 ** ADAM_STEP)
    v_hat = v_new / (1.0 - ADAM_B2 ** ADAM_STEP)
    delta = -ADAM_LR * (m_hat / (jnp.sqrt(v_hat) + ADAM_EPS) + ADAM_WD * w)
    return delta, m_new, v_new


def _adamw(w, g, m, v, layer, others, name):
    _, rows, cols = w.shape
    tr = _pick_tile(rows, 256, 8)

    def body(w_ref, g_ref, m_ref, v_ref, *refs):
        go_ref, d_ref, mo_ref, vo_ref = refs[-4:]
        gv = g_ref[...]
        go_ref[...] = gv
        d_ref[...], mo_ref[...], vo_ref[...] = _adamw_math(w_ref[...], gv, m_ref[...], v_ref[...])

    one = pl.BlockSpec((None, tr, cols), lambda i: (layer, i, 0))
    in_specs = [one, pl.BlockSpec((tr, cols), lambda i: (i, 0)), one, one]
    args = [w, g, m, v]
    if others is not None:
        in_specs += [HBM_SPEC] * 4
        args += list(others)
    return pl.pallas_call(
        body, out_shape=(jax.ShapeDtypeStruct(w.shape, F32),) * 4, grid=(rows // tr,),
        in_specs=in_specs, out_specs=(one,) * 4,
        input_output_aliases={} if others is None else {4 + i: i for i in range(4)},
        compiler_params=_params("parallel"), name=name)(*args)


ROW_MIX, ROW_FFN, ROW_LB, ROW_OUT_GAIN, ROW_FINAL = 0, 2, 4, 7, 8
PART_MIX, PART_FFN, PART_LB, PART_OUT_GAIN, PART_FINAL, PART_LOSS = 0, 2, 4, 5, 6, 7


def _small_update(parts_all, w, m, v, name):
    def body(p_ref, w_ref, m_ref, v_ref, g_ref, d_ref, mo_ref, vo_ref, loss_ref):
        def total(row, n=1):
            tot = p_ref[0, row:row + n, :]
            for j in range(1, N_DEV):
                tot = tot + p_ref[j, row:row + n, :]
            return tot

        logits = [w_ref[ROW_LB + i:ROW_LB + i + 1, :] for i in range(3)]
        mx = jnp.maximum(jnp.maximum(logits[0], logits[1]), logits[2])
        ex = [jnp.exp(l - mx) for l in logits]
        den = ex[0] + ex[1] + ex[2]
        prob = [e / den for e in ex]
        d_lb = total(PART_LB)
        g_ref[...] = jnp.zeros_like(g_ref)
        g_ref[ROW_MIX:ROW_MIX + 2, :] = total(PART_MIX, 2)
        g_ref[ROW_FFN:ROW_FFN + 2, :] = total(PART_FFN, 2)
        for i in range(3):
            g_ref[ROW_LB + i:ROW_LB + i + 1, :] = prob[i] * ((d_lb if i == 0 else 0.0) - prob[0] * d_lb)
        g_ref[ROW_OUT_GAIN:ROW_OUT_GAIN + 1, :] = total(PART_OUT_GAIN)
        g_ref[ROW_FINAL:ROW_FINAL + 1, :] = total(PART_FINAL)
        d_ref[...], mo_ref[...], vo_ref[...] = _adamw_math(w_ref[...], g_ref[...], m_ref[...], v_ref[...])
        loss_ref[...] = jnp.sum(total(PART_LOSS), axis=-1, keepdims=True)

    packed = jax.ShapeDtypeStruct((16, D_MODEL), F32)
    return pl.pallas_call(
        body, out_shape=(packed, packed, packed, packed, jax.ShapeDtypeStruct((1, 1), F32)),
        compiler_params=pltpu.CompilerParams(vmem_limit_bytes=VMEM_LIMIT), name=name)(parts_all, w, m, v)


def _pack_small(norm_mix, norm_ffn, lb_logits, out_gain, final):
    pad = jnp.zeros((1, D_MODEL - HGRN_DIM), F32)
    return jnp.concatenate([norm_mix, norm_ffn, lb_logits, jnp.concatenate([out_gain, pad], axis=1),
                            final.reshape(1, D_MODEL), jnp.zeros((16 - ROW_FINAL - 1, D_MODEL), F32)], axis=0)


def _unpack_small(p):
    return (p[ROW_MIX:ROW_MIX + 2], p[ROW_FFN:ROW_FFN + 2], p[ROW_LB:ROW_LB + 3],
            p[ROW_OUT_GAIN:ROW_OUT_GAIN + 1, :HGRN_DIM], p[ROW_FINAL])


def _lower_bound(lb_logits, name):
    def body(l_ref, o_ref):
        logits = [l_ref[i:i + 1, :] for i in range(3)]
        mx = jnp.maximum(jnp.maximum(logits[0], logits[1]), logits[2])
        ex = [jnp.exp(l - mx) for l in logits]
        o_ref[...] = ex[0] / (ex[0] + ex[1] + ex[2])

    return pl.pallas_call(body, out_shape=jax.ShapeDtypeStruct((1, D_MODEL), F32), name=name)(lb_logits)


def kernel(x, norm_mix, norm_ffn, hgrn_w_in, hgrn_lb_logits, hgrn_out_norm, hgrn_w_out, attn_w_qkv, attn_w_out, ffn_w_in, ffn_w_down, final_norm, loss_target, m_norm_mix, m_norm_ffn, m_hgrn_w_in, m_hgrn_lb_logits, m_hgrn_out_norm, m_hgrn_w_out, m_attn_w_qkv, m_attn_w_out, m_ffn_w_in, m_ffn_w_down, m_final_norm, v_norm_mix, v_norm_ffn, v_hgrn_w_in, v_hgrn_lb_logits, v_hgrn_out_norm, v_hgrn_w_out, v_attn_w_qkv, v_attn_w_out, v_ffn_w_in, v_ffn_w_down, v_final_norm):
    col_sharded = {"hgrn_in": hgrn_w_in[0], "qkv": attn_w_qkv[0], "ffn_in0": ffn_w_in[0], "ffn_in1": ffn_w_in[1]}
    row_sharded = {"hgrn_out": hgrn_w_out[0], "attn_out": attn_w_out[0], "ffn_down0": ffn_w_down[0],
                   "ffn_down1": ffn_w_down[1]}
    gathering = {}
    for gi, (group, names) in enumerate(WEIGHT_GROUPS.items()):
        shards = [(col_sharded[n].T if n in col_sharded else row_sharded[n]).astype(BF16) for n in names]
        gathering[group] = _exchange_launch(shards, False, 1 + gi, f"weights_gather_{group}")

    def fetch(group):
        return {n: land[...].reshape(-1, D_MODEL) for n, land in zip(WEIGHT_GROUPS[group], gathering[group])}

    in_flight = {}

    def publish(group, grads):
        names = WEIGHT_GROUPS[group]
        parts = [grads[n].reshape(N_DEV, -1, D_MODEL) for n in names]
        in_flight[group] = _exchange_launch(parts, True, 1 + len(WEIGHT_GROUPS) + list(WEIGHT_GROUPS).index(group),
                                            f"grads_send_{group}")

    lb = _lower_bound(hgrn_lb_logits, "hgrn_lower_bound")
    grad_x, small = _local_step(x[0], loss_target[0], norm_mix, norm_ffn, lb, hgrn_out_norm,
                                final_norm.reshape(1, D_MODEL), fetch, publish)

    pad = jnp.zeros((1, D_MODEL - HGRN_DIM), F32)
    small_part = jnp.concatenate(
        [small["norm_mix0"], small["norm_mix1"], small["norm_ffn0"], small["norm_ffn1"], small["lb"],
         jnp.concatenate([small["out_gain"], pad], axis=1), small["final"], small["loss"]], axis=0)
    small_all = _gather_small(small_part, "small_grads_gather")
    received = {}
    for group in ("ffn1", "attn", "ffn0", "hgrn"):
        received.update(zip(WEIGHT_GROUPS[group], [land[...] for land in in_flight[group]]))

    masters = {"hgrn_w_in": (hgrn_w_in, m_hgrn_w_in, v_hgrn_w_in, ("hgrn_in",)),
               "hgrn_w_out": (hgrn_w_out, m_hgrn_w_out, v_hgrn_w_out, ("hgrn_out",)),
               "attn_w_qkv": (attn_w_qkv, m_attn_w_qkv, v_attn_w_qkv, ("qkv",)),
               "attn_w_out": (attn_w_out, m_attn_w_out, v_attn_w_out, ("attn_out",)),
               "ffn_w_in": (ffn_w_in, m_ffn_w_in, v_ffn_w_in, ("ffn_in0", "ffn_in1")),
               "ffn_w_down": (ffn_w_down, m_ffn_w_down, v_ffn_w_down, ("ffn_down0", "ffn_down1"))}
    big = {}
    for param, (wv, mv, vv, names) in masters.items():
        outs = None
        for layer, n in enumerate(names):
            g = _sum_blocks(received[n], f"{n}_grad_sum")
            outs = _adamw(wv, g.T if n in col_sharded else g, mv, vv, layer, outs, f"{n}_adamw")
        big[param] = list(outs)

    w_small = _pack_small(norm_mix, norm_ffn, hgrn_lb_logits, hgrn_out_norm, final_norm)
    m_small = _pack_small(m_norm_mix, m_norm_ffn, m_hgrn_lb_logits, m_hgrn_out_norm, m_final_norm)
    v_small = _pack_small(v_norm_mix, v_norm_ffn, v_hgrn_lb_logits, v_hgrn_out_norm, v_final_norm)
    g_s, d_s, m_s, v_s, loss = _small_update(small_all, w_small, m_small, v_small, "small_update")
    small_out = [_unpack_small(t) for t in (g_s, d_s, m_s, v_s)]

    def group(i):
        s = small_out[i]
        return (s[0], s[1], big["hgrn_w_in"][i], s[2], s[3], big["hgrn_w_out"][i], big["attn_w_qkv"][i],
                big["attn_w_out"][i], big["ffn_w_in"][i], big["ffn_w_down"][i], s[4])

    return (loss.reshape(()), grad_x[None], *group(0), *group(1), *group(2), *group(3))
```

```python
import functools

import jax
import jax.numpy as jnp
from jax import lax
from jax.experimental import pallas as pl
from jax.experimental.pallas import tpu as pltpu
from jax.experimental.pallas import tpu_sc as plsc

F32 = jnp.float32
BF16 = jnp.bfloat16

D_MODEL = 1024
N_DEV = 8
NORM_EPS = 1e-6

HGRN_HEADS = 8
HGRN_DIM = 128
HGRN_CHUNK = 64
HGRN_STEP_CHUNKS = 2
HGRN_EXP_CLAMP = 60.0

ATTN_DIM = 128
ATTN_BLOCK = 128
ATTN_GROUP_HEADS = 4
ATTN_GROUP_WIDTH = ATTN_GROUP_HEADS * ATTN_DIM
ATTN_DILATIONS = (1, 4, 16)
ATTN_WIDTH = 3 * ATTN_GROUP_WIDTH
ROPE_THETA = 10000.0
NEG_BIG = -1e30

D_FF = 2816

ADAM_LR = 0.001
ADAM_B1 = 0.9
ADAM_B2 = 0.999
ADAM_EPS = 1e-08
ADAM_WD = 0.01
ADAM_STEP = 10

VMEM_LIMIT = 48 * 1024 * 1024

NT = (((1,), (1,)), ((), ()))
NN = (((1,), (0,)), ((), ()))
TN = (((0,), (0,)), ((), ()))


def _dot(a, b, dims):
    return lax.dot_general(a, b, dims, preferred_element_type=F32)


def _params(*sem):
    return pltpu.CompilerParams(dimension_semantics=sem, vmem_limit_bytes=VMEM_LIMIT)


def _pick_tile(n, cap, mult):
    best = None
    for t in range(mult, min(n, cap) + 1, mult):
        if n % t == 0:
            best = t
    assert best is not None, (n, cap, mult)
    return best


def _sigmoid(x):
    return 0.5 * jnp.tanh(0.5 * x) + 0.5


ROW_TILE = 512
COL_CHUNK = 512
GRAD_TILE = 256


def _whole(shape, index_map):
    return pl.BlockSpec(shape, index_map, pipeline_mode=pl.Buffered(1))


def _part_specs(parts, n_cols):
    return [_whole((rows, n_cols), functools.partial(lambda i, b: (b, 0), b=blk)) for _, rows, blk in parts]


def _mm_nt(a, w_parts, *, out_dtype, name, rope=None):
    M, K = a.shape
    tm = _pick_tile(M, ROW_TILE, 16)
    widths = [rows for _, rows, _ in w_parts]
    n_parts = len(w_parts)

    def body(*refs):
        a_ref, w_refs, o_ref = refs[0], refs[1:1 + n_parts], refs[-1]
        av = a_ref[...]
        off = 0
        for p, w_ref in enumerate(w_refs):
            for c0 in range(0, widths[p], COL_CHUNK):
                cw = min(COL_CHUNK, widths[p] - c0)
                acc = _dot(av, w_ref[c0:c0 + cw, :], NT)
                if rope is not None and p < rope[2]:
                    cos, sin = refs[1 + n_parts][...], refs[2 + n_parts][...]
                    for h0 in range(0, cw, ATTN_DIM):
                        xh = acc[:, h0:h0 + ATTN_DIM]
                        rot = pltpu.roll(xh, ATTN_DIM // 2, 1)
                        o_ref[:, off + c0 + h0:off + c0 + h0 + ATTN_DIM] = (xh * cos + rot * sin).astype(out_dtype)
                else:
                    o_ref[:, off + c0:off + c0 + cw] = acc.astype(out_dtype)
            off += widths[p]

    in_specs = [pl.BlockSpec((tm, K), lambda i: (i, 0))] + _part_specs(w_parts, K)
    args = [a] + [w for w, _, _ in w_parts]
    if rope is not None:
        in_specs += [pl.BlockSpec((tm, ATTN_DIM), lambda i: (i, 0))] * 2
        args += [rope[0], rope[1]]
    return pl.pallas_call(
        body, out_shape=jax.ShapeDtypeStruct((M, sum(widths)), out_dtype), grid=(M // tm,),
        in_specs=in_specs, out_specs=pl.BlockSpec((tm, sum(widths)), lambda i: (i, 0)),
        compiler_params=_params("parallel"), name=name)(*args)


def _mm_nn(a_list, w_parts_list, resid, *, name, norm=None, head=None):
    M = a_list[0].shape[0]
    tm = _pick_tile(M, ROW_TILE, 16)
    n_a = len(a_list)
    flat_parts = [p for parts in w_parts_list for p in parts]
    extra = norm if norm is not None else head
    n_in = n_a + len(flat_parts) + (1 if resid is not None else 0) + (2 if extra is not None else 0)

    def body(*refs):
        a_refs, w_refs = refs[:n_a], refs[n_a:n_a + len(flat_parts)]

        def product(rows):
            acc = None
            wi = 0
            for a_ref, parts in zip(a_refs, w_parts_list):
                off = 0
                for _, k, _ in parts:
                    term = _dot(a_ref[rows, off:off + k], w_refs[wi][...], NN)
                    acc = term if acc is None else acc + term
                    off += k
                    wi += 1
            return acc

        if extra is None:
            acc = product(slice(None))
            if resid is not None:
                acc = acc + refs[n_in - 1][...]
            refs[n_in][...] = acc
            return

        @pl.when(pl.program_id(0) == 0)
        def _():
            for acc_ref in refs[n_in + 2:]:
                acc_ref[...] = jnp.zeros_like(acc_ref)

        for r0 in range(0, tm, tm // 2):
            rows = slice(r0, r0 + tm // 2)
            acc = product(rows)
            if head is not None:
                _loss_head_math(acc + refs[n_in - 3][rows, :], rows, refs[n_in - 2], refs[n_in - 1],
                                *refs[n_in:n_in + 4])
                continue
            dres_ref, x_ref, g_ref = refs[n_in - 3:n_in]
            dx_ref, dxb_ref, dg_ref = refs[n_in:n_in + 3]
            xv = x_ref[rows, :]
            rstd = lax.rsqrt(jnp.mean(xv * xv, axis=-1, keepdims=True) + NORM_EPS)
            n = xv * rstd
            dg_ref[...] += jnp.sum(acc * n, axis=0, keepdims=True)
            dn = acc * g_ref[...]
            dx = dres_ref[rows, :] + rstd * (dn - n * jnp.mean(dn * n, axis=-1, keepdims=True))
            dx_ref[rows, :] = dx
            dxb_ref[rows, :] = dx.astype(BF16)

    row = pl.BlockSpec((tm, D_MODEL), lambda i: (i, 0))
    vec = pl.BlockSpec((1, D_MODEL), lambda i: (0, 0))
    in_specs = [pl.BlockSpec((tm, a.shape[1]), lambda i: (i, 0)) for a in a_list] + _part_specs(flat_parts, D_MODEL)
    args = list(a_list) + [w for w, _, _ in flat_parts]
    if resid is not None:
        in_specs.append(row)
        args.append(resid)
    if extra is None:
        return pl.pallas_call(
            body, out_shape=jax.ShapeDtypeStruct((M, D_MODEL), F32), grid=(M // tm,),
            in_specs=in_specs, out_specs=row, compiler_params=_params("parallel"), name=name)(*args)
    assert resid is not None
    out_shape = [jax.ShapeDtypeStruct((M, D_MODEL), F32), jax.ShapeDtypeStruct((M, D_MODEL), BF16),
                 jax.ShapeDtypeStruct((1, D_MODEL), F32)]
    out_specs = [row, row, vec]
    if head is not None:
        out_shape.append(jax.ShapeDtypeStruct((1, D_MODEL), F32))
        out_specs.append(vec)
    return pl.pallas_call(
        body, out_shape=out_shape, grid=(M // tm,), in_specs=in_specs + [row, vec], out_specs=out_specs,
        compiler_params=_params("arbitrary"), name=name)(*args, extra[0], extra[1])


def _mm_tn(a_list, b, *, name):
    T = a_list[0].shape[0]
    N = b.shape[1]
    tr = GRAD_TILE
    tiles = [a.shape[1] // tr for a in a_list]
    starts = [sum(tiles[:i]) for i in range(len(tiles))]

    def body(*refs):
        a_refs, b_ref, o_ref = refs[:len(a_list)], refs[len(a_list)], refs[-1]
        r = pl.program_id(0)
        for a_ref, first, count in zip(a_refs, starts, tiles):
            @pl.when(jnp.logical_and(r >= first, r < first + count))
            def _():
                o_ref[...] = _dot(a_ref[...], b_ref[...], TN).astype(BF16)

    in_specs = [pl.BlockSpec((T, tr), functools.partial(lambda r, first, count: (0, jnp.clip(r - first, 0, count - 1)),
                                                        first=first, count=count))
                for first, count in zip(starts, tiles)]
    in_specs.append(_whole((T, N), lambda r: (0, 0)))
    return pl.pallas_call(
        body, out_shape=jax.ShapeDtypeStruct((sum(tiles) * tr, N), BF16), grid=(sum(tiles),),
        in_specs=in_specs, out_specs=pl.BlockSpec((tr, N), lambda r: (r, 0)),
        compiler_params=_params("parallel"), name=name)(*a_list, b)


def _rms_fwd(x, gain, name):
    T = x.shape[0]
    tm = _pick_tile(T, 512, 16)

    def body(x_ref, g_ref, u_ref):
        xv = x_ref[...]
        rstd = lax.rsqrt(jnp.mean(xv * xv, axis=-1, keepdims=True) + NORM_EPS)
        u_ref[...] = (xv * rstd * g_ref[...]).astype(BF16)

    return pl.pallas_call(
        body, out_shape=jax.ShapeDtypeStruct((T, D_MODEL), BF16), grid=(T // tm,),
        in_specs=[pl.BlockSpec((tm, D_MODEL), lambda i: (i, 0)), pl.BlockSpec((1, D_MODEL), lambda i: (0, 0))],
        out_specs=pl.BlockSpec((tm, D_MODEL), lambda i: (i, 0)),
        compiler_params=_params("parallel"), name=name)(x, gain)


def _rms_bwd(x, gain, dus, dres, name, dilations=(1,)):
    T = x.shape[0]
    tm = _pick_tile(T, PERM_TILE, 16 * max(dilations))
    n_du = len(dus)

    def body(x_ref, g_ref, *refs):
        du_refs, dres_ref = refs[:n_du], refs[n_du]
        dx_ref, dxb_ref, dg_ref, du_scr = refs[n_du + 1:]

        @pl.when(pl.program_id(0) == 0)
        def _():
            dg_ref[...] = jnp.zeros_like(dg_ref)

        if tuple(dilations) == (1,):
            du = du_refs[0][...]
        else:
            for i, (d, du_ref) in enumerate(zip(dilations, du_refs)):
                for j in range(D_MODEL // LANES):
                    lanes = slice(j * LANES, (j + 1) * LANES)
                    if d == 1:
                        du_scr[j] = du_ref[:, lanes] if i == 0 else du_scr[j] + du_ref[:, lanes]
                        continue
                    blk = du_scr.at[j]
                    for r in range(d):
                        rows = _class_rows(r, d, tm)
                        blk[rows, :] = du_ref[r, :, lanes] if i == 0 else blk[rows, :] + du_ref[r, :, lanes]
            du = jnp.concatenate([du_scr[j] for j in range(D_MODEL // LANES)], axis=1)
        xv = x_ref[...]
        rstd = lax.rsqrt(jnp.mean(xv * xv, axis=-1, keepdims=True) + NORM_EPS)
        n = xv * rstd
        dg_ref[...] += jnp.sum(du * n, axis=0, keepdims=True)
        dn = du * g_ref[...]
        dx = dres_ref[...] + rstd * (dn - n * jnp.mean(dn * n, axis=-1, keepdims=True))
        dx_ref[...] = dx
        dxb_ref[...] = dx.astype(BF16)

    row = pl.BlockSpec((tm, D_MODEL), lambda i: (i, 0))
    vec = pl.BlockSpec((1, D_MODEL), lambda i: (0, 0))
    return pl.pallas_call(
        body,
        out_shape=(jax.ShapeDtypeStruct((T, D_MODEL), F32), jax.ShapeDtypeStruct((T, D_MODEL), BF16),
                   jax.ShapeDtypeStruct((1, D_MODEL), F32)),
        grid=(T // tm,), in_specs=[row, vec] + [_residue_spec(d, tm, D_MODEL) for d in dilations] + [row],
        out_specs=(row, row, vec), scratch_shapes=[pltpu.VMEM((D_MODEL // LANES, tm, LANES), F32)],
        compiler_params=_params("arbitrary"), name=name)(
            x, gain, *[_residue_view(du, d) for du, d in zip(dus, dilations)], dres)


def _loss_head_math(hv, rows, t_ref, g_ref, dh_ref, dhb_ref, dg_ref, loss_ref):
    inv_f = 1.0 / D_MODEL
    g = g_ref[...]
    rstd = lax.rsqrt(jnp.mean(hv * hv, axis=-1, keepdims=True) + NORM_EPS)
    n = hv * rstd
    err = n * g - t_ref[rows, :]
    loss_ref[...] += (0.5 * inv_f) * jnp.sum(err * err, axis=0, keepdims=True)
    dy = err * inv_f
    dg_ref[...] += jnp.sum(dy * n, axis=0, keepdims=True)
    dn = dy * g
    dh = rstd * (dn - n * jnp.mean(dn * n, axis=-1, keepdims=True))
    dh_ref[rows, :] = dh
    dhb_ref[rows, :] = dh.astype(BF16)


FFN_TILE = 256


def _ffn_in(h, gain, w_in, name):
    T = h.shape[0]
    tm = _pick_tile(T, ROW_TILE, 16)

    def body(h_ref, g_ref, w_ref, n_ref, gate_ref, up_ref, a_ref):
        hv = h_ref[...]
        rstd = lax.rsqrt(jnp.mean(hv * hv, axis=-1, keepdims=True) + NORM_EPS)
        n = (hv * rstd * g_ref[...]).astype(BF16)
        n_ref[...] = n
        for c0 in range(0, D_FF, FFN_TILE):
            cols = slice(c0, c0 + FFN_TILE)
            gate = _dot(n, w_ref[c0:c0 + FFN_TILE, :], NT)
            up = _dot(n, w_ref[D_FF + c0:D_FF + c0 + FFN_TILE, :], NT)
            gate_ref[:, cols] = gate.astype(BF16)
            up_ref[:, cols] = up.astype(BF16)
            a_ref[:, cols] = (gate * _sigmoid(gate) * up).astype(BF16)

    row = pl.BlockSpec((tm, D_MODEL), lambda i: (i, 0))
    wide = pl.BlockSpec((tm, D_FF), lambda i: (i, 0))
    wide_shape = jax.ShapeDtypeStruct((T, D_FF), BF16)
    return pl.pallas_call(
        body, out_shape=(jax.ShapeDtypeStruct((T, D_MODEL), BF16), wide_shape, wide_shape, wide_shape),
        grid=(T // tm,),
        in_specs=[row, pl.BlockSpec((1, D_MODEL), lambda i: (0, 0)), _whole((2 * D_FF, D_MODEL), lambda i: (0, 0))],
        out_specs=(row, wide, wide, wide), compiler_params=_params("parallel"), name=name)(h, gain, w_in)


def _ffn_down_dx(dhb, w_down, gate, up, name):
    T = dhb.shape[0]
    tm = _pick_tile(T, ROW_TILE, 16)

    def body(dh_ref, w_ref, gate_ref, up_ref, dgate_ref, dup_ref):
        dh = dh_ref[...]
        for c0 in range(0, D_FF, FFN_TILE):
            cols = slice(c0, c0 + FFN_TILE)
            da = _dot(dh, w_ref[c0:c0 + FFN_TILE, :], NT)
            gate = gate_ref[:, cols].astype(F32)
            sg = _sigmoid(gate)
            dgate_ref[:, cols] = (da * up_ref[:, cols].astype(F32) * (sg * (1.0 + gate * (1.0 - sg)))).astype(BF16)
            dup_ref[:, cols] = (da * gate * sg).astype(BF16)

    wide = pl.BlockSpec((tm, D_FF), lambda i: (i, 0))
    wide_shape = jax.ShapeDtypeStruct((T, D_FF), BF16)
    return pl.pallas_call(
        body, out_shape=(wide_shape, wide_shape), grid=(T // tm,),
        in_specs=[pl.BlockSpec((tm, D_MODEL), lambda i: (i, 0)), _whole((D_FF, D_MODEL), lambda i: (0, 0)), wide, wide],
        out_specs=(wide, wide), compiler_params=_params("parallel"), name=name)(dhb, w_down, gate, up)


def _tri(n, lower):
    r = lax.broadcasted_iota(jnp.int32, (n, n), 0)
    c = lax.broadcasted_iota(jnp.int32, (n, n), 1)
    return (c <= r) if lower else (c >= r)


def _running_sum(x, lower):
    tri = _tri(x.shape[0], lower).astype(BF16)
    hi = x.astype(BF16)
    rest = x - hi.astype(F32)
    mid = rest.astype(BF16)
    lo = (rest - mid.astype(F32)).astype(BF16)
    return _dot(tri, hi, NN) + _dot(tri, mid, NN) + _dot(tri, lo, NN)


def _hgrn_gates(q_raw, f_raw, lb):
    C = q_raw.shape[0]
    sig_f = _sigmoid(f_raw)
    forget = lb + (1.0 - lb) * sig_f
    key = 1.0 - forget
    log_f = jnp.log(forget)
    b = _running_sum(log_f, True)
    first_half = lax.broadcasted_iota(jnp.int32, log_f.shape, 0) < C // 2
    r = jnp.sum(jnp.where(first_half, log_f, 0.0), axis=0, keepdims=True)
    b_last = jnp.sum(log_f, axis=0, keepdims=True)
    e_a = jnp.exp(jnp.minimum(b - r, HGRN_EXP_CLAMP))
    e_b = jnp.exp(jnp.minimum(r - b, HGRN_EXP_CLAMP))
    e_q = jnp.exp(b)
    e_k = jnp.exp(b_last - b)
    sig_q = _sigmoid(q_raw)
    query = q_raw * sig_q
    return dict(sig_f=sig_f, forget=forget, sig_q=sig_q, e_a=e_a, e_b=e_b, e_q=e_q, e_k=e_k,
                e_last=jnp.exp(b_last), q_a=query * e_a, k_b=key * e_b, q_hat=query * e_q, k_til=key * e_k)


HGRN_SAVED = ("q_a", "k_b", "q_hat", "k_til", "e_a", "e_b", "e_q", "e_k")


def _hgrn_fwd(proj, lb, gain, name):
    T = proj.shape[0]
    C = HGRN_CHUNK
    CPS = HGRN_STEP_CHUNKS
    H, HD = HGRN_HEADS, HGRN_DIM
    n_saved = len(HGRN_SAVED)

    def body(q_ref, f_ref, i_ref, g_ref, lb_ref, gain_ref, og_ref, o_ref, st_ref, *refs):
        saved_refs, el_ref, s_scr = refs[:n_saved], refs[n_saved], refs[n_saved + 1]

        @pl.when(pl.program_id(0) == 0)
        def _():
            s_scr[...] = jnp.zeros_like(s_scr)

        causal = _tri(C, True)
        gain_v = gain_ref[...]
        heads = [slice(h * HD, (h + 1) * HD) for h in range(H)]
        s_t = [s_scr[h] for h in range(H)]
        for cc in range(CPS):
            rows = slice(cc * C, (cc + 1) * C)
            for h in range(H):
                st_ref[cc, h] = s_t[h]
            gt = _hgrn_gates(q_ref[rows, :], f_ref[rows, :], lb_ref[...])
            kept = {k: gt[k].astype(BF16) for k in HGRN_SAVED}
            for k, ref in zip(HGRN_SAVED, saved_refs):
                ref[rows, :] = kept[k]
            el_ref[cc] = gt["e_last"]
            q_a, k_b, q_hat, k_til = kept["q_a"], kept["k_b"], kept["q_hat"], kept["k_til"]
            v = i_ref[rows, :].astype(BF16)
            p = [jnp.where(causal, _dot(q_a[:, sl], k_b[:, sl], NT), 0.0).astype(BF16) for sl in heads]
            o = [_dot(p[h], v[:, sl], NN) + _dot(q_hat[:, sl], s_t[h].astype(BF16), NT)
                 for h, sl in enumerate(heads)]
            s_t = [gt["e_last"][:, sl] * s_t[h] + _dot(v[:, sl], k_til[:, sl], TN) for h, sl in enumerate(heads)]
            for h, sl in enumerate(heads):
                o_ref[rows, sl] = o[h]
                rstd = lax.rsqrt(jnp.mean(o[h] * o[h], axis=-1, keepdims=True) + NORM_EPS)
                g_raw = g_ref[rows, sl]
                og_ref[rows, sl] = (o[h] * rstd * gain_v * (g_raw * _sigmoid(g_raw))).astype(BF16)
        for h in range(H):
            s_scr[h] = s_t[h]

    col = lambda j: pl.BlockSpec((CPS * C, D_MODEL), lambda c: (c, j))
    row = pl.BlockSpec((CPS * C, D_MODEL), lambda c: (c, 0))
    wide = jax.ShapeDtypeStruct((T, D_MODEL), BF16)
    res = pl.pallas_call(
        body,
        out_shape=[wide, jax.ShapeDtypeStruct((T, D_MODEL), F32), jax.ShapeDtypeStruct((T // C, H, HD, HD), F32)]
        + [wide] * n_saved + [jax.ShapeDtypeStruct((T // C, 1, D_MODEL), F32)],
        grid=(T // (CPS * C),),
        in_specs=[col(0), col(1), col(2), col(3), pl.BlockSpec((1, D_MODEL), lambda c: (0, 0)),
                  pl.BlockSpec((1, HD), lambda c: (0, 0))],
        out_specs=[row, row, pl.BlockSpec((CPS, H, HD, HD), lambda c: (c, 0, 0, 0))] + [row] * n_saved
        + [pl.BlockSpec((CPS, 1, D_MODEL), lambda c: (c, 0, 0))],
        scratch_shapes=[pltpu.VMEM((H, HD, HD), F32)],
        compiler_params=_params("arbitrary"), name=name)(proj, proj, proj, proj, lb, gain)
    return res[0], res[1], res[2], list(res[3:3 + n_saved]), res[3 + n_saved]


def _hgrn_bwd(proj, o_pre, d_og, states, saved, e_last, lb, gain, name):
    T = proj.shape[0]
    C = HGRN_CHUNK
    CPS = HGRN_STEP_CHUNKS
    H, HD = HGRN_HEADS, HGRN_DIM
    NC = T // (CPS * C)
    n_saved = len(HGRN_SAVED)

    def body(q_ref, f_ref, i_ref, g_ref, o_ref, dog_ref, st_ref, *refs):
        saved_refs, el_ref, lb_ref, gain_ref = refs[:n_saved], refs[n_saved], refs[n_saved + 1], refs[n_saved + 2]
        dproj_ref, dlb_ref, dgain_ref, ds_scr, dq_all, dk_all, db_all = refs[n_saved + 3:]

        @pl.when(pl.program_id(0) == 0)
        def _():
            ds_scr[...] = jnp.zeros_like(ds_scr)
            dlb_ref[...] = jnp.zeros_like(dlb_ref)
            dgain_ref[...] = jnp.zeros_like(dgain_ref)

        lbv = lb_ref[...]
        causal = _tri(C, True)
        last_row = lax.broadcasted_iota(jnp.int32, (C, HD), 0) == C - 1
        gain_v = gain_ref[...]
        heads = [slice(h * HD, (h + 1) * HD) for h in range(H)]
        hs = range(H)
        ds_t = [ds_scr[h] for h in hs]
        dgain = None
        for cc in reversed(range(CPS)):
            rows = slice(cc * C, (cc + 1) * C)
            dq_scr, dk_scr, db_scr = dq_all.at[cc], dk_all.at[cc], db_all.at[cc]
            q_raw = q_ref[rows, :]
            sig_f = _sigmoid(f_ref[rows, :])
            gt = dict(sig_f=sig_f, forget=lbv + (1.0 - lbv) * sig_f, sig_q=_sigmoid(q_raw), e_last=el_ref[cc])
            gt.update({k: ref[rows, :] for k, ref in zip(HGRN_SAVED, saved_refs)})
            o = [o_ref[rows, sl] for sl in heads]
            rstd = [lax.rsqrt(jnp.mean(x * x, axis=-1, keepdims=True) + NORM_EPS) for x in o]
            n = [x * r for x, r in zip(o, rstd)]
            g_raw = [g_ref[rows, sl] for sl in heads]
            sg = [_sigmoid(x) for x in g_raw]
            d_out = [dog_ref[rows, sl] for sl in heads]
            dy = [d * (g * s) for d, g, s in zip(d_out, g_raw, sg)]
            dn = [x * gain_v for x in dy]
            do = [(rstd[h] * (dn[h] - n[h] * jnp.mean(dn[h] * n[h], axis=-1, keepdims=True))).astype(BF16) for h in hs]
            for h in hs:
                dgain = dy[h] * n[h] if dgain is None else dgain + dy[h] * n[h]
            for h, sl in enumerate(heads):
                dproj_ref[rows, 3 * D_MODEL + h * HD:3 * D_MODEL + (h + 1) * HD] = (
                    d_out[h] * n[h] * gain_v * (sg[h] * (1.0 + g_raw[h] * (1.0 - sg[h])))).astype(BF16)
            q_ab, k_bb = gt["q_a"].astype(BF16), gt["k_b"].astype(BF16)
            q_hb, k_tb = gt["q_hat"].astype(BF16), gt["k_til"].astype(BF16)
            v = i_ref[rows, :].astype(BF16)
            s_t = [st_ref[cc, h] for h in hs]
            ds_b = [x.astype(BF16) for x in ds_t]
            p = [jnp.where(causal, _dot(q_ab[:, sl], k_bb[:, sl], NT), 0.0).astype(BF16) for sl in heads]
            dp = [jnp.where(causal, _dot(do[h], v[:, sl], NT), 0.0).astype(BF16) for h, sl in enumerate(heads)]
            dv = [_dot(p[h], do[h], TN) + _dot(k_tb[:, sl], ds_b[h], NT) for h, sl in enumerate(heads)]
            dq_a = [_dot(dp[h], k_bb[:, sl], NN) for h, sl in enumerate(heads)]
            dk_b = [_dot(dp[h], q_ab[:, sl], TN) for h, sl in enumerate(heads)]
            dq_hat = [_dot(do[h], s_t[h].astype(BF16), NN) for h in hs]
            dk_til = [_dot(v[:, sl], ds_b[h], NN) for h, sl in enumerate(heads)]
            ds_new = [_dot(do[h], q_hb[:, sl], TN) + gt["e_last"][:, sl] * ds_t[h] for h, sl in enumerate(heads)]
            for h, sl in enumerate(heads):
                k_til = gt["k_til"][:, sl]
                db_last = jnp.sum(ds_t[h] * gt["e_last"][:, sl] * s_t[h], axis=0, keepdims=True) + jnp.sum(
                    dk_til[h] * k_til, axis=0, keepdims=True)
                dproj_ref[rows, 2 * D_MODEL + h * HD:2 * D_MODEL + (h + 1) * HD] = dv[h].astype(BF16)
                dq_scr[:, sl] = dq_a[h] * gt["e_a"][:, sl] + dq_hat[h] * gt["e_q"][:, sl]
                dk_scr[:, sl] = dk_b[h] * gt["e_b"][:, sl] + dk_til[h] * gt["e_k"][:, sl]
                db = (dq_a[h] * q_ab[:, sl].astype(F32) + dq_hat[h] * gt["q_hat"][:, sl]
                      - dk_b[h] * k_bb[:, sl].astype(F32) - dk_til[h] * k_til)
                db_scr[:, sl] = db + jnp.where(last_row, db_last, 0.0)
            dlogf = _running_sum(db_scr[...], False)
            sig_f, forget, sig_q = gt["sig_f"], gt["forget"], gt["sig_q"]
            dforget = dlogf / forget - dk_scr[...]
            dproj_ref[rows, D_MODEL:2 * D_MODEL] = (dforget * (1.0 - lbv) * sig_f * (1.0 - sig_f)).astype(BF16)
            dlb_ref[...] += jnp.sum(dforget * (1.0 - sig_f), axis=0, keepdims=True)
            dproj_ref[rows, 0:D_MODEL] = (dq_scr[...] * (sig_q * (1.0 + q_raw * (1.0 - sig_q)))).astype(BF16)
            ds_t = ds_new
        dgain_ref[...] += jnp.sum(dgain, axis=0, keepdims=True)
        for h in hs:
            ds_scr[h] = ds_t[h]

    col = lambda j: pl.BlockSpec((CPS * C, D_MODEL), lambda c: (NC - 1 - c, j))
    row = pl.BlockSpec((CPS * C, D_MODEL), lambda c: (NC - 1 - c, 0))
    return pl.pallas_call(
        body,
        out_shape=(jax.ShapeDtypeStruct((T, 4 * D_MODEL), BF16), jax.ShapeDtypeStruct((1, D_MODEL), F32),
                   jax.ShapeDtypeStruct((1, HD), F32)),
        grid=(NC,),
        in_specs=[col(0), col(1), col(2), col(3), row, row,
                  pl.BlockSpec((CPS, H, HD, HD), lambda c: (NC - 1 - c, 0, 0, 0))] + [row] * n_saved
        + [pl.BlockSpec((CPS, 1, D_MODEL), lambda c: (NC - 1 - c, 0, 0)),
           pl.BlockSpec((1, D_MODEL), lambda c: (0, 0)), pl.BlockSpec((1, HD), lambda c: (0, 0))],
        out_specs=(pl.BlockSpec((CPS * C, 4 * D_MODEL), lambda c: (NC - 1 - c, 0)),
                   pl.BlockSpec((1, D_MODEL), lambda c: (0, 0)), pl.BlockSpec((1, HD), lambda c: (0, 0))),
        scratch_shapes=[pltpu.VMEM((H, HD, HD), F32)] + [pltpu.VMEM((CPS, C, D_MODEL), F32)] * 3,
        compiler_params=_params("arbitrary"), name=name)(
            proj, proj, proj, proj, o_pre, d_og, states, *saved, e_last, lb, gain)


def _attn_masks():
    r = lax.broadcasted_iota(jnp.int32, (ATTN_BLOCK, ATTN_BLOCK), 0)
    c = lax.broadcasted_iota(jnp.int32, (ATTN_BLOCK, ATTN_BLOCK), 1)
    return c >= r, c <= r


def _attn_fwd(qkv, dilation, name):
    T = qkv.shape[0]
    nb = T // dilation // ATTN_BLOCK
    W = ATTN_GROUP_WIDTH
    B = ATTN_BLOCK
    scale = ATTN_DIM ** -0.5
    qb = 2 if nb % 2 == 0 else 1
    steps = nb // qb

    def body(q_ref, kp_ref, kc_ref, vp_ref, vc_ref, o_ref, lse_ref):
        no_prev = jnp.where(pl.program_id(1) > 0, 0.0, NEG_BIG)
        m_prev, m_cur = _attn_masks()
        ones = jnp.ones((B, ATTN_DIM), BF16)
        items = []
        for j in range(qb):
            for h in range(ATTN_GROUP_HEADS):
                sl = slice(h * ATTN_DIM, (h + 1) * ATTN_DIM)
                rows = slice(j * B, (j + 1) * B)
                if j == 0:
                    items.append((rows, sl, kp_ref[:, sl], vp_ref[:, sl], no_prev))
                else:
                    before = slice((j - 1) * B, j * B)
                    items.append((rows, sl, kc_ref[before, sl], vc_ref[before, sl], 0.0))
        s_p = [jnp.where(m_prev, _dot(q_ref[rows, sl], k_p, NT) * scale + bias, NEG_BIG)
               for rows, sl, k_p, _, bias in items]
        s_c = [jnp.where(m_cur, _dot(q_ref[rows, sl], kc_ref[rows, sl], NT) * scale, NEG_BIG)
               for rows, sl, _, _, _ in items]
        m = [jnp.max(jnp.maximum(a, b), axis=-1, keepdims=True) for a, b in zip(s_p, s_c)]
        p_p = [jnp.exp(a - mx).astype(BF16) for a, mx in zip(s_p, m)]
        p_c = [jnp.exp(b - mx).astype(BF16) for b, mx in zip(s_c, m)]
        l = [_dot(a, ones, NN) + _dot(b, ones, NN) for a, b in zip(p_p, p_c)]
        acc = [_dot(a, v_p, NN) + _dot(b, vc_ref[rows, sl], NN)
               for a, b, (rows, sl, _, v_p, _) in zip(p_p, p_c, items)]
        for (rows, sl, _, _, _), a, lv, mx in zip(items, acc, l, m):
            o_ref[rows, sl] = (a / lv).astype(BF16)
            lse_ref[rows, sl] = mx + jnp.log(lv)

    cur = lambda col: pl.BlockSpec((qb * B, W), lambda s, n: (s * steps + n, col))
    prev = lambda col: pl.BlockSpec((B, W), lambda s, n: (s * nb + jnp.maximum(qb * n - 1, 0), col))
    out = pl.BlockSpec((qb * B, W), lambda s, n: (s * steps + n, 0))
    return pl.pallas_call(
        body, out_shape=(jax.ShapeDtypeStruct((T, W), BF16), jax.ShapeDtypeStruct((T, W), F32)),
        grid=(dilation, steps),
        in_specs=[cur(0), prev(1), cur(1), prev(2), cur(2)],
        out_specs=(out, out), compiler_params=_params("parallel", "arbitrary"), name=name)(qkv, qkv, qkv, qkv, qkv)


def _attn_bwd(qkv, d_out, lse, delta, cos, sin, dilation, name):
    T = qkv.shape[0]
    nb = T // dilation // ATTN_BLOCK
    assert nb % 2 == 0, "an even number of 128-token blocks per residue class"
    pairs = nb // 2
    W = ATTN_GROUP_WIDTH
    B = ATTN_BLOCK
    scale = ATTN_DIM ** -0.5

    def unrope(x, cos_v, sin_v):
        return x * cos_v + pltpu.roll(x * sin_v, ATTN_DIM // 2, 1)

    def body(qa_ref, qb_ref, kpair_ref, kc_ref, vpair_ref, vc_ref, doa_ref, dob_ref, lsea_ref, lseb_ref,
             dla_ref, dlb_ref, cos_ref, sin_ref, out_ref, dq_scr, dk_scr, dv_scr):
        n = pl.program_id(1)

        @pl.when(n == 0)
        def _():
            dq_scr[...] = jnp.zeros_like(dq_scr)
            dk_scr[...] = jnp.zeros_like(dk_scr)
            dv_scr[...] = jnp.zeros_like(dv_scr)

        no_a = jnp.where(n > 0, 0.0, NEG_BIG)
        no_b = jnp.where(n < pairs, 0.0, NEG_BIG)
        m_prev, m_cur = _attn_masks()
        lo, hi = slice(0, B), slice(B, 2 * B)
        heads = [slice(h * ATTN_DIM, (h + 1) * ATTN_DIM) for h in range(ATTN_GROUP_HEADS)]
        flat = []
        for sl in heads:
            qa, qb = qa_ref[:, sl], qb_ref[:, sl]
            doa, dob = doa_ref[:, sl], dob_ref[:, sl]
            k0, k1, k2 = kpair_ref[lo, sl], kpair_ref[hi, sl], kc_ref[:, sl]
            v0, v1, v2 = vpair_ref[lo, sl], vpair_ref[hi, sl], vc_ref[:, sl]
            flat += [(qa, doa, lsea_ref[:, sl], dla_ref[:, sl], k0, v0, m_prev, no_a),
                     (qa, doa, lsea_ref[:, sl], dla_ref[:, sl], k1, v1, m_cur, no_a),
                     (qb, dob, lseb_ref[:, sl], dlb_ref[:, sl], k1, v1, m_prev, no_a + no_b),
                     (qb, dob, lseb_ref[:, sl], dlb_ref[:, sl], k2, v2, m_cur, no_b)]
        s = [_dot(q, k, NT) for q, _, _, _, k, _, _, _ in flat]
        dp = [_dot(do, v, NT) for _, do, _, _, _, v, _, _ in flat]
        p = [jnp.where(mask, jnp.exp(sv * scale - lse_v + bias), 0.0)
             for sv, (_, _, lse_v, _, _, _, mask, bias) in zip(s, flat)]
        ds = [(pv * (dpv - dl_v) * scale).astype(BF16) for pv, dpv, (_, _, _, dl_v, _, _, _, _) in zip(p, dp, flat)]
        p = [pv.astype(BF16) for pv in p]
        dq_part = [_dot(dsv, k, NN) for dsv, (_, _, _, _, k, _, _, _) in zip(ds, flat)]
        dk_part = [_dot(dsv, q, TN) for dsv, (q, _, _, _, _, _, _, _) in zip(ds, flat)]
        dv_part = [_dot(pv, do, TN) for pv, (_, do, _, _, _, _, _, _) in zip(p, flat)]
        cos_lo, sin_lo, cos_hi, sin_hi = cos_ref[lo, :], sin_ref[lo, :], cos_ref[hi, :], sin_ref[hi, :]
        for h, sl in enumerate(heads):
            a_prev, a_cur, b_prev, b_cur = range(4 * h, 4 * h + 4)
            kcol = slice(W + h * ATTN_DIM, W + (h + 1) * ATTN_DIM)
            vcol = slice(2 * W + h * ATTN_DIM, 2 * W + (h + 1) * ATTN_DIM)
            out_ref[lo, sl] = unrope(dq_scr[:, sl], cos_lo, sin_lo).astype(BF16)
            out_ref[hi, sl] = unrope(dq_part[a_prev] + dq_part[a_cur], cos_hi, sin_hi).astype(BF16)
            out_ref[lo, kcol] = unrope(dk_scr[:, sl] + dk_part[a_prev], cos_lo, sin_lo).astype(BF16)
            out_ref[hi, kcol] = unrope(dk_part[a_cur] + dk_part[b_prev], cos_hi, sin_hi).astype(BF16)
            out_ref[lo, vcol] = (dv_scr[:, sl] + dv_part[a_prev]).astype(BF16)
            out_ref[hi, vcol] = (dv_part[a_cur] + dv_part[b_prev]).astype(BF16)
            dq_scr[:, sl] = dq_part[b_prev] + dq_part[b_cur]
            dk_scr[:, sl] = dk_part[b_cur]
            dv_scr[:, sl] = dv_part[b_cur]

    def block_a(n):
        return jnp.maximum(2 * n - 1, 0)

    def block_b(n):
        return jnp.minimum(2 * n, nb - 1)

    def pair(n):
        return jnp.maximum(n - 1, 0)

    one_a = lambda col: pl.BlockSpec((B, W), lambda s, n: (s * nb + block_a(n), col))
    one_b = lambda col: pl.BlockSpec((B, W), lambda s, n: (s * nb + block_b(n), col))
    two = lambda col: pl.BlockSpec((2 * B, W), lambda s, n: (s * pairs + pair(n), col))
    tab = pl.BlockSpec((2 * B, ATTN_DIM), lambda s, n: (s * pairs + pair(n), 0))
    return pl.pallas_call(
        body, out_shape=jax.ShapeDtypeStruct((T, 3 * W), BF16), grid=(dilation, pairs + 1),
        in_specs=[one_a(0), one_b(0), two(1), one_b(1), two(2), one_b(2), one_a(0), one_b(0), one_a(0), one_b(0),
                  one_a(0), one_b(0), tab, tab],
        out_specs=pl.BlockSpec((2 * B, 3 * W), lambda s, n: (s * pairs + pair(n), 0)),
        scratch_shapes=[pltpu.VMEM((B, W), F32)] * 3,
        compiler_params=_params("parallel", "arbitrary"), name=name)(
            qkv, qkv, qkv, qkv, qkv, qkv, d_out, d_out, lse, lse, delta, delta, cos, sin)


PERM_TILE = 512
LANES = 128


def _residue_view(x, d):
    return x if d == 1 else x.reshape(d, x.shape[0] // d, x.shape[1])


def _residue_spec(d, tm, cols):
    if d == 1:
        return pl.BlockSpec((tm, cols), lambda i: (i, 0))
    return pl.BlockSpec((d, tm // d, cols), lambda i: (0, i, 0))


def _residue_shape(T, d, cols, dtype):
    return jax.ShapeDtypeStruct((T, cols) if d == 1 else (d, T // d, cols), dtype)


def _class_rows(r, d, tm):
    return pl.ds(r, tm // d, stride=d)


def _attn_norm(h, gain, name):
    T = h.shape[0]
    tm = _pick_tile(T, PERM_TILE, 16 * max(ATTN_DILATIONS))
    dils = ATTN_DILATIONS
    (base_cos, base_sin), (off_cos, off_sin), sign = _rope_parts(T, tm)

    def body(h_ref, g_ref, bc_ref, bs_ref, oc_ref, os_ref, sign_ref, *refs):
        u_refs, c_refs, s_refs, u_scr, c_scr, s_scr = refs[0:3], refs[3:6], refs[6:9], refs[9], refs[10], refs[11]
        hv = h_ref[...]
        rstd = lax.rsqrt(jnp.mean(hv * hv, axis=-1, keepdims=True) + NORM_EPS)
        u = hv * rstd * g_ref[...]
        for j in range(D_MODEL // LANES):
            u_scr[j] = u[:, j * LANES:(j + 1) * LANES]
        bc, bs, oc, osn = bc_ref[0], bs_ref[0], oc_ref[...], os_ref[...]
        c_scr[...] = bc * oc - bs * osn
        s_scr[...] = (bs * oc + bc * osn) * sign_ref[...]
        for d, u_ref, c_ref, s_ref in zip(dils, u_refs, c_refs, s_refs):
            if d == 1:
                u_ref[...] = u.astype(BF16)
                c_ref[...] = c_scr[...]
                s_ref[...] = s_scr[...]
                continue
            for r in range(d):
                rows = _class_rows(r, d, tm)
                for j in range(D_MODEL // LANES):
                    u_ref[r, :, j * LANES:(j + 1) * LANES] = u_scr.at[j][rows, :].astype(BF16)
                c_ref[r] = c_scr[rows, :]
                s_ref[r] = s_scr[rows, :]

    row = pl.BlockSpec((tm, D_MODEL), lambda i: (i, 0))
    base = pl.BlockSpec((1, 1, ATTN_DIM), lambda i: (i, 0, 0))
    off = pl.BlockSpec((tm, ATTN_DIM), lambda i: (0, 0))
    res = pl.pallas_call(
        body,
        out_shape=([_residue_shape(T, d, D_MODEL, BF16) for d in dils]
                   + [_residue_shape(T, d, ATTN_DIM, F32) for d in dils] * 2),
        grid=(T // tm,),
        in_specs=[row, pl.BlockSpec((1, D_MODEL), lambda i: (0, 0)), base, base, off, off,
                  pl.BlockSpec((1, ATTN_DIM), lambda i: (0, 0))],
        out_specs=([_residue_spec(d, tm, D_MODEL) for d in dils] + [_residue_spec(d, tm, ATTN_DIM) for d in dils] * 2),
        scratch_shapes=[pltpu.VMEM((D_MODEL // LANES, tm, LANES), F32), pltpu.VMEM((tm, ATTN_DIM), F32),
                        pltpu.VMEM((tm, ATTN_DIM), F32)],
        compiler_params=_params("parallel"), name=name)(h, gain, base_cos, base_sin, off_cos, off_sin, sign)
    flat = [r.reshape(T, r.shape[-1]) for r in res]
    return flat[0:3], flat[3:6], flat[6:9]


def _attn_merge_fwd(outs, lses, name):
    T = outs[0].shape[0]
    W = ATTN_GROUP_WIDTH
    tm = _pick_tile(T, PERM_TILE, 16 * max(ATTN_DILATIONS))
    dils = ATTN_DILATIONS

    def body(*refs):
        o_refs, l_refs, oc_ref, lse_refs = refs[0:3], refs[3:6], refs[6], refs[7:10]
        o_scr, l_scr, t_scr = refs[10:13]
        nh = ATTN_GROUP_HEADS
        for g, d in enumerate(dils):
            for j in range(nh):
                lanes = slice(j * LANES, (j + 1) * LANES)
                if d == 1:
                    o_scr[g * nh + j] = o_refs[g][:, lanes].astype(F32)
                    l_scr[g * nh + j] = l_refs[g][:, lanes]
                    continue
                for r in range(d):
                    rows = _class_rows(r, d, tm)
                    o_scr.at[g * nh + j][rows, :] = o_refs[g][r, :, lanes].astype(F32)
                    l_scr.at[g * nh + j][rows, :] = l_refs[g][r, :, lanes]
        for j in range(nh):
            lanes = slice(j * LANES, (j + 1) * LANES)
            ls = [l_scr[g * nh + j] for g in range(3)]
            m = jnp.maximum(jnp.maximum(ls[0], ls[1]), ls[2])
            tot = m + jnp.log(jnp.exp(ls[0] - m) + jnp.exp(ls[1] - m) + jnp.exp(ls[2] - m))
            t_scr[j] = tot
            for g, d in enumerate(dils):
                oc_ref[:, g * W + j * LANES:g * W + (j + 1) * LANES] = (
                    o_scr[g * nh + j] * jnp.exp(ls[g] - tot)).astype(BF16)
                if d == 1:
                    lse_refs[g][:, lanes] = tot
                    continue
                for r in range(d):
                    lse_refs[g][r, :, lanes] = t_scr.at[j][_class_rows(r, d, tm), :]

    in_blk = [_residue_spec(d, tm, W) for d in dils]
    n_blk = 3 * ATTN_GROUP_HEADS
    res = pl.pallas_call(
        body, out_shape=[jax.ShapeDtypeStruct((T, 3 * W), BF16)] + [_residue_shape(T, d, W, F32) for d in dils],
        grid=(T // tm,), in_specs=in_blk * 2,
        out_specs=[pl.BlockSpec((tm, 3 * W), lambda i: (i, 0))] + in_blk,
        scratch_shapes=[pltpu.VMEM((n_blk, tm, LANES), F32), pltpu.VMEM((n_blk, tm, LANES), F32),
                        pltpu.VMEM((ATTN_GROUP_HEADS, tm, LANES), F32)],
        compiler_params=_params("parallel"), name=name)(
            *[_residue_view(o, d) for o, d in zip(outs, dils)], *[_residue_view(l, d) for l, d in zip(lses, dils)])
    return res[0], [r.reshape(T, W) for r in res[1:]]


def _attn_merge_bwd(d_oc, oc, name):
    T = d_oc.shape[0]
    W = ATTN_GROUP_WIDTH
    tm = _pick_tile(T, PERM_TILE, 16 * max(ATTN_DILATIONS))
    dils = ATTN_DILATIONS

    def body(d_ref, o_ref, *refs):
        delta_refs, db_refs, dl_scr, d_scr = refs[0:3], refs[3:6], refs[6], refs[7]
        nh = ATTN_GROUP_HEADS
        for j in range(nh):
            tot = jnp.zeros((tm, 1), F32)
            for g in range(3):
                cols = slice(g * W + j * LANES, g * W + (j + 1) * LANES)
                d_blk = d_ref[:, cols]
                d_scr[g * nh + j] = d_blk
                tot = tot + jnp.sum(d_blk * o_ref[:, cols].astype(F32), axis=-1, keepdims=True)
            dl_scr[j] = jnp.broadcast_to(tot, (tm, LANES))
        for g, d in enumerate(dils):
            for j in range(nh):
                lanes = slice(j * LANES, (j + 1) * LANES)
                if d == 1:
                    delta_refs[g][:, lanes] = dl_scr[j]
                    db_refs[g][:, lanes] = d_scr[g * nh + j].astype(BF16)
                    continue
                for r in range(d):
                    rows = _class_rows(r, d, tm)
                    delta_refs[g][r, :, lanes] = dl_scr.at[j][rows, :]
                    db_refs[g][r, :, lanes] = d_scr.at[g * nh + j][rows, :].astype(BF16)

    wide = pl.BlockSpec((tm, 3 * W), lambda i: (i, 0))
    out_blk = [_residue_spec(d, tm, W) for d in dils]
    res = pl.pallas_call(
        body, out_shape=[_residue_shape(T, d, W, F32) for d in dils] + [_residue_shape(T, d, W, BF16) for d in dils],
        grid=(T // tm,), in_specs=[wide, wide], out_specs=out_blk * 2,
        scratch_shapes=[pltpu.VMEM((ATTN_GROUP_HEADS, tm, LANES), F32),
                        pltpu.VMEM((3 * ATTN_GROUP_HEADS, tm, LANES), F32)],
        compiler_params=_params("parallel"), name=name)(d_oc, oc)
    flat = [r.reshape(T, W) for r in res]
    return flat[0:3], flat[3:6]


def _rope_parts(T, tile):
    inv_freq = 1.0 / (ROPE_THETA ** (jnp.arange(0, ATTN_DIM, 2, dtype=F32) / ATTN_DIM))
    inv_freq = jnp.concatenate([inv_freq, inv_freq])[None, :]
    base = (jnp.arange(T // tile, dtype=F32) * tile)[:, None] * inv_freq
    off = jnp.arange(tile, dtype=F32)[:, None] * inv_freq
    sign = jnp.concatenate([-jnp.ones((1, ATTN_DIM // 2), F32), jnp.ones((1, ATTN_DIM // 2), F32)], axis=1)
    return (jnp.cos(base)[:, None, :], jnp.sin(base)[:, None, :]), (jnp.cos(off), jnp.sin(off)), sign


WEIGHT_GROUPS = {"hgrn": ("hgrn_in", "hgrn_out"), "ffn0": ("ffn_in0", "ffn_down0"),
                 "attn": ("qkv", "attn_out"), "ffn1": ("ffn_in1", "ffn_down1")}


def _local_step(x, target, norm_mix, norm_ffn, lb, out_gain, final_gain, fetch, publish):
    g_mix = [norm_mix[0:1], norm_mix[1:2]]
    g_ffn = [norm_ffn[0:1], norm_ffn[1:2]]
    w = {}

    def whole(name):
        return [(w[name], w[name].shape[0], 0)]

    def qkv_parts(g):
        return [(w["qkv"], ATTN_GROUP_WIDTH, 3 * j + g) for j in range(3)]

    def ffn_fwd(h, layer, head=None):
        w.update(fetch(f"ffn{layer}"))
        n, gate, up, a = _ffn_in(h, g_ffn[layer], w[f"ffn_in{layer}"], f"ffn{layer}_in")
        out = _mm_nn([a], [whole(f"ffn_down{layer}")], h, name=f"ffn{layer}_down", head=head)
        return out, (n, gate, up, a)

    def ffn_bwd(h, saved, dh, dhb, layer):
        n, gate, up, a = saved
        w_in = w[f"ffn_in{layer}"]
        dgate, dup = _ffn_down_dx(dhb, w[f"ffn_down{layer}"], gate, up, f"ffn{layer}_down_dx")
        grads = {f"ffn_down{layer}": _mm_tn([a], dhb, name=f"ffn{layer}_down_dw"),
                 f"ffn_in{layer}": _mm_tn([dgate, dup], n, name=f"ffn{layer}_in_dw")}
        publish(f"ffn{layer}", grads)
        return _mm_nn([dgate, dup], [[(w_in, D_FF, 0)], [(w_in, D_FF, 1)]], dh, name=f"ffn{layer}_in_dx",
                      norm=(h, g_ffn[layer]))

    u0 = _rms_fwd(x, g_mix[0], "hgrn_norm")
    w.update(fetch("hgrn"))
    proj = _mm_nt(u0, whole("hgrn_in"), out_dtype=F32, name="hgrn_in")
    og, o_pre, states, gates, e_last = _hgrn_fwd(proj, lb, out_gain, "hgrn_fwd")
    h1 = _mm_nn([og], [whole("hgrn_out")], x, name="hgrn_out")
    h2, ffn0 = ffn_fwd(h1, 0)

    u1_g, cos_g, sin_g = _attn_norm(h2, g_mix[1], "attn_norm")
    w.update(fetch("attn"))
    qkv_g, outs, lses = [], [], []
    for g, d in enumerate(ATTN_DILATIONS):
        qkv_g.append(_mm_nt(u1_g[g], qkv_parts(g), out_dtype=BF16, name=f"attn_qkv{g}",
                            rope=(cos_g[g], sin_g[g], 2)))
        o_g, lse_g = _attn_fwd(qkv_g[g], d, f"attn_fwd{g}")
        outs.append(o_g)
        lses.append(lse_g)
    oc, lse_all = _attn_merge_fwd(outs, lses, "attn_merge")
    h3 = _mm_nn([oc], [whole("attn_out")], h2, name="attn_out")
    (dh4, dh4b, d_final, loss_part), ffn1 = ffn_fwd(h3, 1, head=(target, final_gain))
    dh3, dh3b, d_ffn1 = ffn_bwd(h3, ffn1, dh4, dh4b, 1)

    d_oc = _mm_nt(dh3b, whole("attn_out"), out_dtype=F32, name="attn_out_dx")
    grad_attn_out = _mm_tn([oc], dh3b, name="attn_out_dw")
    delta, d_ocb = _attn_merge_bwd(d_oc, oc, "attn_merge_bwd")
    du1, qkv_pieces = [], []
    for g, d in enumerate(ATTN_DILATIONS):
        dqkv = _attn_bwd(qkv_g[g], d_ocb[g], lse_all[g], delta[g], cos_g[g], sin_g[g], d, f"attn_bwd{g}")
        qkv_pieces.append(_mm_tn([dqkv], u1_g[g], name=f"attn_qkv_dw{g}"))
        du1.append(_mm_nn([dqkv], [qkv_parts(g)], None, name=f"attn_qkv_dx{g}"))
    grad_qkv = jnp.stack([p.reshape(3, ATTN_GROUP_WIDTH, D_MODEL) for p in qkv_pieces], axis=1).reshape(
        3 * ATTN_WIDTH, D_MODEL)
    publish("attn", {"qkv": grad_qkv, "attn_out": grad_attn_out})
    dh2, dh2b, d_mix1 = _rms_bwd(h2, g_mix[1], du1, dh3, "attn_norm_bwd", ATTN_DILATIONS)

    dh1, dh1b, d_ffn0 = ffn_bwd(h1, ffn0, dh2, dh2b, 0)

    d_og = _mm_nt(dh1b, whole("hgrn_out"), out_dtype=F32, name="hgrn_out_dx")
    grad_hgrn_out = _mm_tn([og], dh1b, name="hgrn_out_dw")
    dproj, d_lb, d_out_gain = _hgrn_bwd(proj, o_pre, d_og, states, gates, e_last, lb, out_gain, "hgrn_bwd")
    publish("hgrn", {"hgrn_in": _mm_tn([dproj], u0, name="hgrn_in_dw"), "hgrn_out": grad_hgrn_out})
    dx, _, d_mix0 = _mm_nn([dproj], [whole("hgrn_in")], dh1, name="hgrn_in_dx", norm=(x, g_mix[0]))

    small = dict(norm_mix0=d_mix0, norm_mix1=d_mix1, norm_ffn0=d_ffn0, norm_ffn1=d_ffn1, lb=d_lb,
                 out_gain=d_out_gain, final=d_final, loss=loss_part)
    return dx, small


MESH_IDS = pl.DeviceIdType.MESH
HBM_SPEC = pl.BlockSpec(memory_space=pl.ANY)


N_PEERS = N_DEV - 1
PEER_OFFSETS = [(dx, dy, dc) for dx in (0, 1) for dy in (0, 1) for dc in (0, 1)][1:]


def _mesh_place():
    x, y, c = lax.axis_index("x"), lax.axis_index("y"), lax.axis_index("c")
    peers = []
    for dx, dy, dc in PEER_OFFSETS:
        px, py, pc = (1 - x if dx else x), (1 - y if dy else y), (1 - c if dc else c)
        peers.append(((px, py, pc), 4 * px + 2 * py + pc))
    return 4 * x + 2 * y + c, peers


def _gather_over_two_levels(src_refs, land_refs, send_sems, recv_sems):
    n = len(src_refs)
    x, y, c = lax.axis_index("x"), lax.axis_index("y"), lax.axis_index("c")
    me, sibling = (x, y, c), (x, y, 1 - c)
    chips = [(1 - x, y), (x, 1 - y), (1 - x, 1 - y)]

    def block(w, px, py, pc):
        return land_refs[w].at[4 * px + 2 * py + pc]

    def copy(w, k, owner, to, src=None):
        return pltpu.make_async_remote_copy(
            src_ref=block(w, *owner) if src is None else src, dst_ref=block(w, *owner),
            send_sem=send_sems.at[w * N_PEERS + k], recv_sem=recv_sems.at[w * N_PEERS + k],
            device_id=to, device_id_type=MESH_IDS)

    sent = []
    for w in range(n):
        sent.append(copy(w, 0, me, sibling, src=src_refs[w]))
        sent += [copy(w, 1 + j, me, (*chip, c), src=src_refs[w]) for j, chip in enumerate(chips)]
    for cp in sent:
        cp.start()
    for w in range(n):
        for j, chip in enumerate(chips):
            copy(w, 1 + j, (*chip, c), me).wait_recv()
            passed = copy(w, 4 + j, (*chip, c), sibling)
            passed.start()
            sent.append(passed)
    for w in range(n):
        copy(w, 0, sibling, me).wait_recv()
        for j, chip in enumerate(chips):
            copy(w, 4 + j, (*chip, 1 - c), me).wait_recv()
    for cp in sent:
        cp.wait_send()


def _exchange_launch(srcs, scatter, collective_id, name):
    n = len(srcs)
    src_refs = [jax.new_ref(s, memory_space=pltpu.MemorySpace.HBM) for s in srcs]
    land_refs = [jax.empty_ref(jax.ShapeDtypeStruct(s.shape if scatter else (N_DEV,) + s.shape, s.dtype),
                               memory_space=pltpu.MemorySpace.HBM) for s in srcs]

    @pl.kernel(mesh=plsc.ScalarSubcoreMesh(axis_name="sequencer", num_cores=1), name=name,
               scratch_types=(pltpu.SemaphoreType.DMA((n * N_PEERS,)), pltpu.SemaphoreType.DMA((n * N_PEERS,)),
                              pltpu.SemaphoreType.DMA((n,))),
               compiler_params=pltpu.CompilerParams(collective_id=collective_id))
    def launch(send_sems, recv_sems, local_sems):
        me, peers = _mesh_place()
        barrier = pltpu.get_barrier_semaphore()
        for peer, _ in peers:
            pl.semaphore_signal(barrier, inc=1, device_id=peer, device_id_type=MESH_IDS)
        pl.semaphore_wait(barrier, N_PEERS)
        own = [pltpu.make_async_copy(src_refs[w].at[me] if scatter else src_refs[w], land_refs[w].at[me],
                                     local_sems.at[w]) for w in range(n)]
        for cp in own:
            cp.start()
        if scatter:
            copies = [pltpu.make_async_remote_copy(
                src_ref=src_refs[w].at[pid], dst_ref=land_refs[w].at[me],
                send_sem=send_sems.at[w * N_PEERS + k], recv_sem=recv_sems.at[w * N_PEERS + k],
                device_id=peer, device_id_type=MESH_IDS) for w in range(n) for k, (peer, pid) in enumerate(peers)]
            for cp in copies:
                cp.start()
            for cp in copies:
                cp.wait()
        else:
            _gather_over_two_levels(src_refs, land_refs, send_sems, recv_sems)
        for cp in own:
            cp.wait()

    launch()
    return land_refs


def _gather_small(block, name):
    def body(in_ref, out_ref, send_sems, recv_sems, local_sem):
        me, peers = _mesh_place()
        own = pltpu.make_async_copy(in_ref, out_ref.at[me], local_sem)
        own.start()
        sends = [pltpu.make_async_remote_copy(
            src_ref=in_ref, dst_ref=out_ref.at[me], send_sem=send_sems.at[k], recv_sem=recv_sems.at[k],
            device_id=peer, device_id_type=MESH_IDS) for k, (peer, _) in enumerate(peers)]
        for cp in sends:
            cp.start()
        for cp in sends:
            cp.wait_recv()
        for cp in sends:
            cp.wait_send()
        own.wait()

    return pl.pallas_call(
        body, out_shape=jax.ShapeDtypeStruct((N_DEV,) + block.shape, block.dtype),
        in_specs=[HBM_SPEC], out_specs=HBM_SPEC,
        scratch_shapes=[pltpu.SemaphoreType.DMA((N_PEERS,)), pltpu.SemaphoreType.DMA((N_PEERS,)),
                        pltpu.SemaphoreType.DMA],
        name=name)(block)


def _sum_blocks(recv, name):
    rows = recv.shape[1]
    tr = _pick_tile(rows, 256, 16)

    def body(r_ref, g_ref):
        acc = r_ref[0].astype(F32)
        for j in range(1, N_DEV):
            acc = acc + r_ref[j].astype(F32)
        g_ref[...] = acc

    return pl.pallas_call(
        body, out_shape=jax.ShapeDtypeStruct((rows, D_MODEL), F32), grid=(rows // tr,),
        in_specs=[pl.BlockSpec((N_DEV, tr, D_MODEL), lambda i: (0, i, 0))],
        out_specs=pl.BlockSpec((tr, D_MODEL), lambda i: (i, 0)),
        compiler_params=_params("parallel"), name=name)(recv)


def _adamw_math(w, g, m, v):
    m_new = ADAM_B1 * m + (1.0 - ADAM_B1) * g
    v_new = ADAM_B2 * v + (1.0 - ADAM_B2) * (g * g)
    m_hat = m_new / (1.0 - ADAM_B1 ** ADAM_STEP)
    v_hat = v_new / (1.0 - ADAM_B2 ** ADAM_STEP)
    delta = -ADAM_LR * (m_hat / (jnp.sqrt(v_hat) + ADAM_EPS) + ADAM_WD * w)
    return delta, m_new, v_new


def _adamw(w, g, m, v, layer, others, name):
    _, rows, cols = w.shape
    tr = _pick_tile(rows, 256, 8)

    def body(w_ref, g_ref, m_ref, v_ref, *refs):
        go_ref, d_ref, mo_ref, vo_ref = refs[-4:]
        gv = g_ref[...]
        go_ref[...] = gv
        d_ref[...], mo_ref[...], vo_ref[...] = _adamw_math(w_ref[...], gv, m_ref[...], v_ref[...])

    one = pl.BlockSpec((None, tr, cols), lambda i: (layer, i, 0))
    in_specs = [one, pl.BlockSpec((tr, cols), lambda i: (i, 0)), one, one]
    args = [w, g, m, v]
    if others is not None:
        in_specs += [HBM_SPEC] * 4
        args += list(others)
    return pl.pallas_call(
        body, out_shape=(jax.ShapeDtypeStruct(w.shape, F32),) * 4, grid=(rows // tr,),
        in_specs=in_specs, out_specs=(one,) * 4,
        input_output_aliases={} if others is None else {4 + i: i for i in range(4)},
        compiler_params=_params("parallel"), name=name)(*args)


ROW_MIX, ROW_FFN, ROW_LB, ROW_OUT_GAIN, ROW_FINAL = 0, 2, 4, 7, 8
PART_MIX, PART_FFN, PART_LB, PART_OUT_GAIN, PART_FINAL, PART_LOSS = 0, 2, 4, 5, 6, 7


def _small_update(parts_all, w, m, v, name):
    def body(p_ref, w_ref, m_ref, v_ref, g_ref, d_ref, mo_ref, vo_ref, loss_ref):
        def total(row, n=1):
            tot = p_ref[0, row:row + n, :]
            for j in range(1, N_DEV):
                tot = tot + p_ref[j, row:row + n, :]
            return tot

        logits = [w_ref[ROW_LB + i:ROW_LB + i + 1, :] for i in range(3)]
        mx = jnp.maximum(jnp.maximum(logits[0], logits[1]), logits[2])
        ex = [jnp.exp(l - mx) for l in logits]
        den = ex[0] + ex[1] + ex[2]
        prob = [e / den for e in ex]
        d_lb = total(PART_LB)
        g_ref[...] = jnp.zeros_like(g_ref)
        g_ref[ROW_MIX:ROW_MIX + 2, :] = total(PART_MIX, 2)
        g_ref[ROW_FFN:ROW_FFN + 2, :] = total(PART_FFN, 2)
        for i in range(3):
            g_ref[ROW_LB + i:ROW_LB + i + 1, :] = prob[i] * ((d_lb if i == 0 else 0.0) - prob[0] * d_lb)
        g_ref[ROW_OUT_GAIN:ROW_OUT_GAIN + 1, :] = total(PART_OUT_GAIN)
        g_ref[ROW_FINAL:ROW_FINAL + 1, :] = total(PART_FINAL)
        d_ref[...], mo_ref[...], vo_ref[...] = _adamw_math(w_ref[...], g_ref[...], m_ref[...], v_ref[...])
        loss_ref[...] = jnp.sum(total(PART_LOSS), axis=-1, keepdims=True)

    packed = jax.ShapeDtypeStruct((16, D_MODEL), F32)
    return pl.pallas_call(
        body, out_shape=(packed, packed, packed, packed, jax.ShapeDtypeStruct((1, 1), F32)),
        compiler_params=pltpu.CompilerParams(vmem_limit_bytes=VMEM_LIMIT), name=name)(parts_all, w, m, v)


def _pack_small(norm_mix, norm_ffn, lb_logits, out_gain, final):
    pad = jnp.zeros((1, D_MODEL - HGRN_DIM), F32)
    return jnp.concatenate([norm_mix, norm_ffn, lb_logits, jnp.concatenate([out_gain, pad], axis=1),
                            final.reshape(1, D_MODEL), jnp.zeros((16 - ROW_FINAL - 1, D_MODEL), F32)], axis=0)


def _unpack_small(p):
    return (p[ROW_MIX:ROW_MIX + 2], p[ROW_FFN:ROW_FFN + 2], p[ROW_LB:ROW_LB + 3],
            p[ROW_OUT_GAIN:ROW_OUT_GAIN + 1, :HGRN_DIM], p[ROW_FINAL])


def _lower_bound(lb_logits, name):
    def body(l_ref, o_ref):
        logits = [l_ref[i:i + 1, :] for i in range(3)]
        mx = jnp.maximum(jnp.maximum(logits[0], logits[1]), logits[2])
        ex = [jnp.exp(l - mx) for l in logits]
        o_ref[...] = ex[0] / (ex[0] + ex[1] + ex[2])

    return pl.pallas_call(body, out_shape=jax.ShapeDtypeStruct((1, D_MODEL), F32), name=name)(lb_logits)


def kernel(x, norm_mix, norm_ffn, hgrn_w_in, hgrn_lb_logits, hgrn_out_norm, hgrn_w_out, attn_w_qkv, attn_w_out, ffn_w_in, ffn_w_down, final_norm, loss_target, m_norm_mix, m_norm_ffn, m_hgrn_w_in, m_hgrn_lb_logits, m_hgrn_out_norm, m_hgrn_w_out, m_attn_w_qkv, m_attn_w_out, m_ffn_w_in, m_ffn_w_down, m_final_norm, v_norm_mix, v_norm_ffn, v_hgrn_w_in, v_hgrn_lb_logits, v_hgrn_out_norm, v_hgrn_w_out, v_attn_w_qkv, v_attn_w_out, v_ffn_w_in, v_ffn_w_down, v_final_norm):
    col_sharded = {"hgrn_in": hgrn_w_in[0], "qkv": attn_w_qkv[0], "ffn_in0": ffn_w_in[0], "ffn_in1": ffn_w_in[1]}
    row_sharded = {"hgrn_out": hgrn_w_out[0], "attn_out": attn_w_out[0], "ffn_down0": ffn_w_down[0],
                   "ffn_down1": ffn_w_down[1]}
    gathering = {}
    for gi, (group, names) in enumerate(WEIGHT_GROUPS.items()):
        shards = [(col_sharded[n].T if n in col_sharded else row_sharded[n]).astype(BF16) for n in names]
        gathering[group] = _exchange_launch(shards, False, 1 + gi, f"weights_gather_{group}")

    def fetch(group):
        return {n: land[...].reshape(-1, D_MODEL) for n, land in zip(WEIGHT_GROUPS[group], gathering[group])}

    in_flight = {}

    def publish(group, grads):
        names = WEIGHT_GROUPS[group]
        parts = [grads[n].reshape(N_DEV, -1, D_MODEL) for n in names]
        in_flight[group] = _exchange_launch(parts, True, 1 + len(WEIGHT_GROUPS) + list(WEIGHT_GROUPS).index(group),
                                            f"grads_send_{group}")

    lb = _lower_bound(hgrn_lb_logits, "hgrn_lower_bound")
    grad_x, small = _local_step(x[0], loss_target[0], norm_mix, norm_ffn, lb, hgrn_out_norm,
                                final_norm.reshape(1, D_MODEL), fetch, publish)

    pad = jnp.zeros((1, D_MODEL - HGRN_DIM), F32)
    small_part = jnp.concatenate(
        [small["norm_mix0"], small["norm_mix1"], small["norm_ffn0"], small["norm_ffn1"], small["lb"],
         jnp.concatenate([small["out_gain"], pad], axis=1), small["final"], small["loss"]], axis=0)
    small_all = _gather_small(small_part, "small_grads_gather")
    received = {}
    for group in ("ffn1", "attn", "ffn0", "hgrn"):
        received.update(zip(WEIGHT_GROUPS[group], [land[...] for land in in_flight[group]]))

    masters = {"hgrn_w_in": (hgrn_w_in, m_hgrn_w_in, v_hgrn_w_in, ("hgrn_in",)),
               "hgrn_w_out": (hgrn_w_out, m_hgrn_w_out, v_hgrn_w_out, ("hgrn_out",)),
               "attn_w_qkv": (attn_w_qkv, m_attn_w_qkv, v_attn_w_qkv, ("qkv",)),
               "attn_w_out": (attn_w_out, m_attn_w_out, v_attn_w_out, ("attn_out",)),
               "ffn_w_in": (ffn_w_in, m_ffn_w_in, v_ffn_w_in, ("ffn_in0", "ffn_in1")),
               "ffn_w_down": (ffn_w_down, m_ffn_w_down, v_ffn_w_down, ("ffn_down0", "ffn_down1"))}
    big = {}
    for param, (wv, mv, vv, names) in masters.items():
        outs = None
        for layer, n in enumerate(names):
            g = _sum_blocks(received[n], f"{n}_grad_sum")
            outs = _adamw(wv, g.T if n in col_sharded else g, mv, vv, layer, outs, f"{n}_adamw")
        big[param] = list(outs)

    w_small = _pack_small(norm_mix, norm_ffn, hgrn_lb_logits, hgrn_out_norm, final_norm)
    m_small = _pack_small(m_norm_mix, m_norm_ffn, m_hgrn_lb_logits, m_hgrn_out_norm, m_final_norm)
    v_small = _pack_small(v_norm_mix, v_norm_ffn, v_hgrn_lb_logits, v_hgrn_out_norm, v_final_norm)
    g_s, d_s, m_s, v_s, loss = _small_update(small_all, w_small, m_small, v_small, "small_update")
    small_out = [_unpack_small(t) for t in (g_s, d_s, m_s, v_s)]

    def group(i):
        s = small_out[i]
        return (s[0], s[1], big["hgrn_w_in"][i], s[2], s[3], big["hgrn_w_out"][i], big["attn_w_qkv"][i],
                big["attn_w_out"][i], big["ffn_w_in"][i], big["ffn_w_down"][i], s[4])

    return (loss.reshape(()), grad_x[None], *group(0), *group(1), *group(2), *group(3))
```

```python
import functools

import jax
import jax.numpy as jnp
from jax import lax
from jax.experimental import pallas as pl
from jax.experimental.pallas import tpu as pltpu
from jax.experimental.pallas import tpu_sc as plsc

F32 = jnp.float32
BF16 = jnp.bfloat16

D_MODEL = 1024
N_DEV = 8
NORM_EPS = 1e-6

HGRN_HEADS = 8
HGRN_DIM = 128
HGRN_CHUNK = 64
HGRN_STEP_CHUNKS = 2
HGRN_EXP_CLAMP = 60.0

ATTN_DIM = 128
ATTN_BLOCK = 128
ATTN_GROUP_HEADS = 4
ATTN_GROUP_WIDTH = ATTN_GROUP_HEADS * ATTN_DIM
ATTN_DILATIONS = (1, 4, 16)
ATTN_WIDTH = 3 * ATTN_GROUP_WIDTH
ROPE_THETA = 10000.0
NEG_BIG = -1e30

D_FF = 2816

ADAM_LR = 0.001
ADAM_B1 = 0.9
ADAM_B2 = 0.999
ADAM_EPS = 1e-08
ADAM_WD = 0.01
ADAM_STEP = 10

VMEM_LIMIT = 48 * 1024 * 1024

NT = (((1,), (1,)), ((), ()))
NN = (((1,), (0,)), ((), ()))
TN = (((0,), (0,)), ((), ()))


def _dot(a, b, dims):
    return lax.dot_general(a, b, dims, preferred_element_type=F32)


def _params(*sem):
    return pltpu.CompilerParams(dimension_semantics=sem, vmem_limit_bytes=VMEM_LIMIT)


def _pick_tile(n, cap, mult):
    best = None
    for t in range(mult, min(n, cap) + 1, mult):
        if n % t == 0:
            best = t
    assert best is not None, (n, cap, mult)
    return best


def _sigmoid(x):
    return 0.5 * jnp.tanh(0.5 * x) + 0.5


ROW_TILE = 512
COL_CHUNK = 512
GRAD_TILE = 256


def _whole(shape, index_map):
    return pl.BlockSpec(shape, index_map, pipeline_mode=pl.Buffered(1))


def _part_specs(parts, n_cols):
    return [_whole((rows, n_cols), functools.partial(lambda i, b: (b, 0), b=blk)) for _, rows, blk in parts]


def _mm_nt(a, w_parts, *, out_dtype, name, rope=None):
    M, K = a.shape
    tm = _pick_tile(M, ROW_TILE, 16)
    widths = [rows for _, rows, _ in w_parts]
    n_parts = len(w_parts)

    def body(*refs):
        a_ref, w_refs, o_ref = refs[0], refs[1:1 + n_parts], refs[-1]
        av = a_ref[...]
        off = 0
        for p, w_ref in enumerate(w_refs):
            for c0 in range(0, widths[p], COL_CHUNK):
                cw = min(COL_CHUNK, widths[p] - c0)
                acc = _dot(av, w_ref[c0:c0 + cw, :], NT)
                if rope is not None and p < rope[2]:
                    cos, sin = refs[1 + n_parts][...], refs[2 + n_parts][...]
                    for h0 in range(0, cw, ATTN_DIM):
                        xh = acc[:, h0:h0 + ATTN_DIM]
                        rot = pltpu.roll(xh, ATTN_DIM // 2, 1)
                        o_ref[:, off + c0 + h0:off + c0 + h0 + ATTN_DIM] = (xh * cos + rot * sin).astype(out_dtype)
                else:
                    o_ref[:, off + c0:off + c0 + cw] = acc.astype(out_dtype)
            off += widths[p]

    in_specs = [pl.BlockSpec((tm, K), lambda i: (i, 0))] + _part_specs(w_parts, K)
    args = [a] + [w for w, _, _ in w_parts]
    if rope is not None:
        in_specs += [pl.BlockSpec((tm, ATTN_DIM), lambda i: (i, 0))] * 2
        args += [rope[0], rope[1]]
    return pl.pallas_call(
        body, out_shape=jax.ShapeDtypeStruct((M, sum(widths)), out_dtype), grid=(M // tm,),
        in_specs=in_specs, out_specs=pl.BlockSpec((tm, sum(widths)), lambda i: (i, 0)),
        compiler_params=_params("parallel"), name=name)(*args)


def _mm_nn(a_list, w_parts_list, resid, *, name, norm=None, head=None):
    M = a_list[0].shape[0]
    tm = _pick_tile(M, ROW_TILE, 16)
    n_a = len(a_list)
    flat_parts = [p for parts in w_parts_list for p in parts]
    extra = norm if norm is not None else head
    n_in = n_a + len(flat_parts) + (1 if resid is not None else 0) + (2 if extra is not None else 0)

    def body(*refs):
        a_refs, w_refs = refs[:n_a], refs[n_a:n_a + len(flat_parts)]

        def product(rows):
            acc = None
            wi = 0
            for a_ref, parts in zip(a_refs, w_parts_list):
                off = 0
                for _, k, _ in parts:
                    term = _dot(a_ref[rows, off:off + k], w_refs[wi][...], NN)
                    acc = term if acc is None else acc + term
                    off += k
                    wi += 1
            return acc

        if extra is None:
            acc = product(slice(None))
            if resid is not None:
                acc = acc + refs[n_in - 1][...]
            refs[n_in][...] = acc
            return

        @pl.when(pl.program_id(0) == 0)
        def _():
            for acc_ref in refs[n_in + 2:]:
                acc_ref[...] = jnp.zeros_like(acc_ref)

        for r0 in range(0, tm, tm // 2):
            rows = slice(r0, r0 + tm // 2)
            acc = product(rows)
            if head is not None:
                _loss_head_math(acc + refs[n_in - 3][rows, :], rows, refs[n_in - 2], refs[n_in - 1],
                                *refs[n_in:n_in + 4])
                continue
            dres_ref, x_ref, g_ref = refs[n_in - 3:n_in]
            dx_ref, dxb_ref, dg_ref = refs[n_in:n_in + 3]
            xv = x_ref[rows, :]
            rstd = lax.rsqrt(jnp.mean(xv * xv, axis=-1, keepdims=True) + NORM_EPS)
            n = xv * rstd
            dg_ref[...] += jnp.sum(acc * n, axis=0, keepdims=True)
            dn = acc * g_ref[...]
            dx = dres_ref[rows, :] + rstd * (dn - n * jnp.mean(dn * n, axis=-1, keepdims=True))
            dx_ref[rows, :] = dx
            dxb_ref[rows, :] = dx.astype(BF16)

    row = pl.BlockSpec((tm, D_MODEL), lambda i: (i, 0))
    vec = pl.BlockSpec((1, D_MODEL), lambda i: (0, 0))
    in_specs = [pl.BlockSpec((tm, a.shape[1]), lambda i: (i, 0)) for a in a_list] + _part_specs(flat_parts, D_MODEL)
    args = list(a_list) + [w for w, _, _ in flat_parts]
    if resid is not None:
        in_specs.append(row)
        args.append(resid)
    if extra is None:
        return pl.pallas_call(
            body, out_shape=jax.ShapeDtypeStruct((M, D_MODEL), F32), grid=(M // tm,),
            in_specs=in_specs, out_specs=row, compiler_params=_params("parallel"), name=name)(*args)
    assert resid is not None
    out_shape = [jax.ShapeDtypeStruct((M, D_MODEL), F32), jax.ShapeDtypeStruct((M, D_MODEL), BF16),
                 jax.ShapeDtypeStruct((1, D_MODEL), F32)]
    out_specs = [row, row, vec]
    if head is not None:
        out_shape.append(jax.ShapeDtypeStruct((1, D_MODEL), F32))
        out_specs.append(vec)
    return pl.pallas_call(
        body, out_shape=out_shape, grid=(M // tm,), in_specs=in_specs + [row, vec], out_specs=out_specs,
        compiler_params=_params("arbitrary"), name=name)(*args, extra[0], extra[1])


def _mm_tn(a_list, b, *, name):
    T = a_list[0].shape[0]
    N = b.shape[1]
    tr = GRAD_TILE
    tiles = [a.shape[1] // tr for a in a_list]
    starts = [sum(tiles[:i]) for i in range(len(tiles))]

    def body(*refs):
        a_refs, b_ref, o_ref = refs[:len(a_list)], refs[len(a_list)], refs[-1]
        r = pl.program_id(0)
        for a_ref, first, count in zip(a_refs, starts, tiles):
            @pl.when(jnp.logical_and(r >= first, r < first + count))
            def _():
                o_ref[...] = _dot(a_ref[...], b_ref[...], TN).astype(BF16)

    in_specs = [pl.BlockSpec((T, tr), functools.partial(lambda r, first, count: (0, jnp.clip(r - first, 0, count - 1)),
                                                        first=first, count=count))
                for first, count in zip(starts, tiles)]
    in_specs.append(_whole((T, N), lambda r: (0, 0)))
    return pl.pallas_call(
        body, out_shape=jax.ShapeDtypeStruct((sum(tiles) * tr, N), BF16), grid=(sum(tiles),),
        in_specs=in_specs, out_specs=pl.BlockSpec((tr, N), lambda r: (r, 0)),
        compiler_params=_params("parallel"), name=name)(*a_list, b)


def _rms_fwd(x, gain, name):
    T = x.shape[0]
    tm = _pick_tile(T, 512, 16)

    def body(x_ref, g_ref, u_ref):
        xv = x_ref[...]
        rstd = lax.rsqrt(jnp.mean(xv * xv, axis=-1, keepdims=True) + NORM_EPS)
        u_ref[...] = (xv * rstd * g_ref[...]).astype(BF16)

    return pl.pallas_call(
        body, out_shape=jax.ShapeDtypeStruct((T, D_MODEL), BF16), grid=(T // tm,),
        in_specs=[pl.BlockSpec((tm, D_MODEL), lambda i: (i, 0)), pl.BlockSpec((1, D_MODEL), lambda i: (0, 0))],
        out_specs=pl.BlockSpec((tm, D_MODEL), lambda i: (i, 0)),
        compiler_params=_params("parallel"), name=name)(x, gain)


def _rms_bwd(x, gain, dus, dres, name, dilations=(1,)):
    T = x.shape[0]
    tm = _pick_tile(T, PERM_TILE, 16 * max(dilations))
    n_du = len(dus)

    def body(x_ref, g_ref, *refs):
        du_refs, dres_ref = refs[:n_du], refs[n_du]
        dx_ref, dxb_ref, dg_ref, du_scr = refs[n_du + 1:]

        @pl.when(pl.program_id(0) == 0)
        def _():
            dg_ref[...] = jnp.zeros_like(dg_ref)

        if tuple(dilations) == (1,):
            du = du_refs[0][...]
        else:
            for i, (d, du_ref) in enumerate(zip(dilations, du_refs)):
                for j in range(D_MODEL // LANES):
                    lanes = slice(j * LANES, (j + 1) * LANES)
                    if d == 1:
                        du_scr[j] = du_ref[:, lanes] if i == 0 else du_scr[j] + du_ref[:, lanes]
                        continue
                    blk = du_scr.at[j]
                    for r in range(d):
                        rows = _class_rows(r, d, tm)
                        blk[rows, :] = du_ref[r, :, lanes] if i == 0 else blk[rows, :] + du_ref[r, :, lanes]
            du = jnp.concatenate([du_scr[j] for j in range(D_MODEL // LANES)], axis=1)
        xv = x_ref[...]
        rstd = lax.rsqrt(jnp.mean(xv * xv, axis=-1, keepdims=True) + NORM_EPS)
        n = xv * rstd
        dg_ref[...] += jnp.sum(du * n, axis=0, keepdims=True)
        dn = du * g_ref[...]
        dx = dres_ref[...] + rstd * (dn - n * jnp.mean(dn * n, axis=-1, keepdims=True))
        dx_ref[...] = dx
        dxb_ref[...] = dx.astype(BF16)

    row = pl.BlockSpec((tm, D_MODEL), lambda i: (i, 0))
    vec = pl.BlockSpec((1, D_MODEL), lambda i: (0, 0))
    return pl.pallas_call(
        body,
        out_shape=(jax.ShapeDtypeStruct((T, D_MODEL), F32), jax.ShapeDtypeStruct((T, D_MODEL), BF16),
                   jax.ShapeDtypeStruct((1, D_MODEL), F32)),
        grid=(T // tm,), in_specs=[row, vec] + [_residue_spec(d, tm, D_MODEL) for d in dilations] + [row],
        out_specs=(row, row, vec), scratch_shapes=[pltpu.VMEM((D_MODEL // LANES, tm, LANES), F32)],
        compiler_params=_params("arbitrary"), name=name)(
            x, gain, *[_residue_view(du, d) for du, d in zip(dus, dilations)], dres)


def _loss_head_math(hv, rows, t_ref, g_ref, dh_ref, dhb_ref, dg_ref, loss_ref):
    inv_f = 1.0 / D_MODEL
    g = g_ref[...]
    rstd = lax.rsqrt(jnp.mean(hv * hv, axis=-1, keepdims=True) + NORM_EPS)
    n = hv * rstd
    err = n * g - t_ref[rows, :]
    loss_ref[...] += (0.5 * inv_f) * jnp.sum(err * err, axis=0, keepdims=True)
    dy = err * inv_f
    dg_ref[...] += jnp.sum(dy * n, axis=0, keepdims=True)
    dn = dy * g
    dh = rstd * (dn - n * jnp.mean(dn * n, axis=-1, keepdims=True))
    dh_ref[rows, :] = dh
    dhb_ref[rows, :] = dh.astype(BF16)


FFN_TILE = 256


def _ffn_in(h, gain, w_in, name):
    T = h.shape[0]
    tm = _pick_tile(T, ROW_TILE, 16)

    def body(h_ref, g_ref, w_ref, n_ref, by_up_ref, by_gate_ref, a_ref):
        hv = h_ref[...]
        rstd = lax.rsqrt(jnp.mean(hv * hv, axis=-1, keepdims=True) + NORM_EPS)
        n = (hv * rstd * g_ref[...]).astype(BF16)
        n_ref[...] = n
        for c0 in range(0, D_FF, FFN_TILE):
            cols = slice(c0, c0 + FFN_TILE)
            gate = _dot(n, w_ref[c0:c0 + FFN_TILE, :], NT)
            up = _dot(n, w_ref[D_FF + c0:D_FF + c0 + FFN_TILE, :], NT)
            sg = _sigmoid(gate)
            silu = gate * sg
            by_up_ref[:, cols] = silu.astype(BF16)
            by_gate_ref[:, cols] = ((sg + silu * (1.0 - sg)) * up).astype(BF16)
            a_ref[:, cols] = (silu * up).astype(BF16)

    row = pl.BlockSpec((tm, D_MODEL), lambda i: (i, 0))
    wide = pl.BlockSpec((tm, D_FF), lambda i: (i, 0))
    wide_shape = jax.ShapeDtypeStruct((T, D_FF), BF16)
    return pl.pallas_call(
        body, out_shape=(jax.ShapeDtypeStruct((T, D_MODEL), BF16), wide_shape, wide_shape, wide_shape),
        grid=(T // tm,),
        in_specs=[row, pl.BlockSpec((1, D_MODEL), lambda i: (0, 0)), _whole((2 * D_FF, D_MODEL), lambda i: (0, 0))],
        out_specs=(row, wide, wide, wide), compiler_params=_params("parallel"), name=name)(h, gain, w_in)


def _ffn_down_dx(dhb, w_down, a_by_gate, a_by_up, name):
    T = dhb.shape[0]
    tm = _pick_tile(T, ROW_TILE, 16)

    def body(dh_ref, w_ref, by_gate_ref, by_up_ref, dgate_ref, dup_ref):
        dh = dh_ref[...]
        for c0 in range(0, D_FF, FFN_TILE):
            cols = slice(c0, c0 + FFN_TILE)
            da = _dot(dh, w_ref[c0:c0 + FFN_TILE, :], NT).astype(BF16)
            dgate_ref[:, cols] = da * by_gate_ref[:, cols]
            dup_ref[:, cols] = da * by_up_ref[:, cols]

    wide = pl.BlockSpec((tm, D_FF), lambda i: (i, 0))
    wide_shape = jax.ShapeDtypeStruct((T, D_FF), BF16)
    return pl.pallas_call(
        body, out_shape=(wide_shape, wide_shape), grid=(T // tm,),
        in_specs=[pl.BlockSpec((tm, D_MODEL), lambda i: (i, 0)), _whole((D_FF, D_MODEL), lambda i: (0, 0)), wide, wide],
        out_specs=(wide, wide), compiler_params=_params("parallel"), name=name)(dhb, w_down, a_by_gate, a_by_up)


def _tri(n, lower):
    r = lax.broadcasted_iota(jnp.int32, (n, n), 0)
    c = lax.broadcasted_iota(jnp.int32, (n, n), 1)
    return (c <= r) if lower else (c >= r)


def _running_sum(x, lower):
    tri = _tri(x.shape[0], lower).astype(BF16)
    hi = x.astype(BF16)
    rest = x - hi.astype(F32)
    mid = rest.astype(BF16)
    lo = (rest - mid.astype(F32)).astype(BF16)
    return _dot(tri, hi, NN) + _dot(tri, mid, NN) + _dot(tri, lo, NN)


def _hgrn_gates(q_raw, f_raw, lb):
    C = q_raw.shape[0]
    sig_f = _sigmoid(f_raw)
    forget = lb + (1.0 - lb) * sig_f
    key = 1.0 - forget
    log_f = jnp.log(forget)
    b = _running_sum(log_f, True)
    first_half = lax.broadcasted_iota(jnp.int32, log_f.shape, 0) < C // 2
    r = jnp.sum(jnp.where(first_half, log_f, 0.0), axis=0, keepdims=True)
    b_last = jnp.sum(log_f, axis=0, keepdims=True)
    e_a = jnp.exp(jnp.minimum(b - r, HGRN_EXP_CLAMP))
    e_b = jnp.exp(jnp.minimum(r - b, HGRN_EXP_CLAMP))
    e_q = jnp.exp(b)
    e_k = jnp.exp(b_last - b)
    sig_q = _sigmoid(q_raw)
    query = q_raw * sig_q
    return dict(sig_f=sig_f, forget=forget, sig_q=sig_q, e_a=e_a, e_b=e_b, e_q=e_q, e_k=e_k,
                e_last=jnp.exp(b_last), q_a=query * e_a, k_b=key * e_b, q_hat=query * e_q, k_til=key * e_k)


def _hgrn_fwd(proj, lb, gain, name):
    T = proj.shape[0]
    C = HGRN_CHUNK
    CPS = HGRN_STEP_CHUNKS
    H, HD = HGRN_HEADS, HGRN_DIM

    def body(q_ref, f_ref, i_ref, g_ref, lb_ref, gain_ref, og_ref, o_ref, st_ref, s_scr):
        @pl.when(pl.program_id(0) == 0)
        def _():
            s_scr[...] = jnp.zeros_like(s_scr)

        causal = _tri(C, True)
        gain_v = gain_ref[...]
        heads = [slice(h * HD, (h + 1) * HD) for h in range(H)]
        s_t = [s_scr[h] for h in range(H)]
        for cc in range(CPS):
            rows = slice(cc * C, (cc + 1) * C)
            for h in range(H):
                st_ref[cc, h] = s_t[h]
            gt = _hgrn_gates(q_ref[rows, :], f_ref[rows, :], lb_ref[...])
            q_a, k_b = gt["q_a"].astype(BF16), gt["k_b"].astype(BF16)
            q_hat, k_til = gt["q_hat"].astype(BF16), gt["k_til"].astype(BF16)
            v = i_ref[rows, :].astype(BF16)
            p = [jnp.where(causal, _dot(q_a[:, sl], k_b[:, sl], NT), 0.0).astype(BF16) for sl in heads]
            o = [_dot(p[h], v[:, sl], NN) + _dot(q_hat[:, sl], s_t[h].astype(BF16), NT)
                 for h, sl in enumerate(heads)]
            s_t = [gt["e_last"][:, sl] * s_t[h] + _dot(v[:, sl], k_til[:, sl], TN) for h, sl in enumerate(heads)]
            for h, sl in enumerate(heads):
                o_ref[rows, sl] = o[h]
                rstd = lax.rsqrt(jnp.mean(o[h] * o[h], axis=-1, keepdims=True) + NORM_EPS)
                g_raw = g_ref[rows, sl]
                og_ref[rows, sl] = (o[h] * rstd * gain_v * (g_raw * _sigmoid(g_raw))).astype(BF16)
        for h in range(H):
            s_scr[h] = s_t[h]

    col = lambda j: pl.BlockSpec((CPS * C, D_MODEL), lambda c: (c, j))
    row = pl.BlockSpec((CPS * C, D_MODEL), lambda c: (c, 0))
    return pl.pallas_call(
        body,
        out_shape=(jax.ShapeDtypeStruct((T, D_MODEL), BF16), jax.ShapeDtypeStruct((T, D_MODEL), F32),
                   jax.ShapeDtypeStruct((T // C, H, HD, HD), F32)),
        grid=(T // (CPS * C),),
        in_specs=[col(0), col(1), col(2), col(3), pl.BlockSpec((1, D_MODEL), lambda c: (0, 0)),
                  pl.BlockSpec((1, HD), lambda c: (0, 0))],
        out_specs=(row, row, pl.BlockSpec((CPS, H, HD, HD), lambda c: (c, 0, 0, 0))),
        scratch_shapes=[pltpu.VMEM((H, HD, HD), F32)],
        compiler_params=_params("arbitrary"), name=name)(proj, proj, proj, proj, lb, gain)


def _hgrn_bwd(proj, o_pre, d_og, states, lb, gain, name):
    T = proj.shape[0]
    C = HGRN_CHUNK
    CPS = HGRN_STEP_CHUNKS
    H, HD = HGRN_HEADS, HGRN_DIM
    NC = T // (CPS * C)

    def body(q_ref, f_ref, i_ref, g_ref, o_ref, dog_ref, st_ref, lb_ref, gain_ref,
             dproj_ref, dlb_ref, dgain_ref, ds_scr, dq_all, dk_all, db_all):
        @pl.when(pl.program_id(0) == 0)
        def _():
            ds_scr[...] = jnp.zeros_like(ds_scr)
            dlb_ref[...] = jnp.zeros_like(dlb_ref)
            dgain_ref[...] = jnp.zeros_like(dgain_ref)

        lbv = lb_ref[...]
        causal = _tri(C, True)
        last_row = lax.broadcasted_iota(jnp.int32, (C, HD), 0) == C - 1
        gain_v = gain_ref[...]
        heads = [slice(h * HD, (h + 1) * HD) for h in range(H)]
        hs = range(H)
        ds_t = [ds_scr[h] for h in hs]
        dgain = None
        for cc in reversed(range(CPS)):
            rows = slice(cc * C, (cc + 1) * C)
            dq_scr, dk_scr, db_scr = dq_all.at[cc], dk_all.at[cc], db_all.at[cc]
            q_raw = q_ref[rows, :]
            gt = _hgrn_gates(q_raw, f_ref[rows, :], lbv)
            o = [o_ref[rows, sl] for sl in heads]
            rstd = [lax.rsqrt(jnp.mean(x * x, axis=-1, keepdims=True) + NORM_EPS) for x in o]
            n = [x * r for x, r in zip(o, rstd)]
            g_raw = [g_ref[rows, sl] for sl in heads]
            sg = [_sigmoid(x) for x in g_raw]
            d_out = [dog_ref[rows, sl] for sl in heads]
            dy = [d * (g * s) for d, g, s in zip(d_out, g_raw, sg)]
            dn = [x * gain_v for x in dy]
            do = [(rstd[h] * (dn[h] - n[h] * jnp.mean(dn[h] * n[h], axis=-1, keepdims=True))).astype(BF16) for h in hs]
            for h in hs:
                dgain = dy[h] * n[h] if dgain is None else dgain + dy[h] * n[h]
            for h, sl in enumerate(heads):
                dproj_ref[rows, 3 * D_MODEL + h * HD:3 * D_MODEL + (h + 1) * HD] = (
                    d_out[h] * n[h] * gain_v * (sg[h] * (1.0 + g_raw[h] * (1.0 - sg[h])))).astype(BF16)
            q_ab, k_bb = gt["q_a"].astype(BF16), gt["k_b"].astype(BF16)
            q_hb, k_tb = gt["q_hat"].astype(BF16), gt["k_til"].astype(BF16)
            v = i_ref[rows, :].astype(BF16)
            s_t = [st_ref[cc, h] for h in hs]
            ds_b = [x.astype(BF16) for x in ds_t]
            p = [jnp.where(causal, _dot(q_ab[:, sl], k_bb[:, sl], NT), 0.0).astype(BF16) for sl in heads]
            dp = [jnp.where(causal, _dot(do[h], v[:, sl], NT), 0.0).astype(BF16) for h, sl in enumerate(heads)]
            dv = [_dot(p[h], do[h], TN) + _dot(k_tb[:, sl], ds_b[h], NT) for h, sl in enumerate(heads)]
            dq_a = [_dot(dp[h], k_bb[:, sl], NN) for h, sl in enumerate(heads)]
            dk_b = [_dot(dp[h], q_ab[:, sl], TN) for h, sl in enumerate(heads)]
            dq_hat = [_dot(do[h], s_t[h].astype(BF16), NN) for h in hs]
            dk_til = [_dot(v[:, sl], ds_b[h], NN) for h, sl in enumerate(heads)]
            ds_new = [_dot(do[h], q_hb[:, sl], TN) + gt["e_last"][:, sl] * ds_t[h] for h, sl in enumerate(heads)]
            for h, sl in enumerate(heads):
                k_til = gt["k_til"][:, sl]
                db_last = jnp.sum(ds_t[h] * gt["e_last"][:, sl] * s_t[h], axis=0, keepdims=True) + jnp.sum(
                    dk_til[h] * k_til, axis=0, keepdims=True)
                dproj_ref[rows, 2 * D_MODEL + h * HD:2 * D_MODEL + (h + 1) * HD] = dv[h].astype(BF16)
                dq_scr[:, sl] = dq_a[h] * gt["e_a"][:, sl] + dq_hat[h] * gt["e_q"][:, sl]
                dk_scr[:, sl] = dk_b[h] * gt["e_b"][:, sl] + dk_til[h] * gt["e_k"][:, sl]
                db = (dq_a[h] * q_ab[:, sl].astype(F32) + dq_hat[h] * gt["q_hat"][:, sl]
                      - dk_b[h] * k_bb[:, sl].astype(F32) - dk_til[h] * k_til)
                db_scr[:, sl] = db + jnp.where(last_row, db_last, 0.0)
            dlogf = _running_sum(db_scr[...], False)
            sig_f, forget, sig_q = gt["sig_f"], gt["forget"], gt["sig_q"]
            dforget = dlogf / forget - dk_scr[...]
            dproj_ref[rows, D_MODEL:2 * D_MODEL] = (dforget * (1.0 - lbv) * sig_f * (1.0 - sig_f)).astype(BF16)
            dlb_ref[...] += jnp.sum(dforget * (1.0 - sig_f), axis=0, keepdims=True)
            dproj_ref[rows, 0:D_MODEL] = (dq_scr[...] * (sig_q * (1.0 + q_raw * (1.0 - sig_q)))).astype(BF16)
            ds_t = ds_new
        dgain_ref[...] += jnp.sum(dgain, axis=0, keepdims=True)
        for h in hs:
            ds_scr[h] = ds_t[h]

    col = lambda j: pl.BlockSpec((CPS * C, D_MODEL), lambda c: (NC - 1 - c, j))
    row = pl.BlockSpec((CPS * C, D_MODEL), lambda c: (NC - 1 - c, 0))
    return pl.pallas_call(
        body,
        out_shape=(jax.ShapeDtypeStruct((T, 4 * D_MODEL), BF16), jax.ShapeDtypeStruct((1, D_MODEL), F32),
                   jax.ShapeDtypeStruct((1, HD), F32)),
        grid=(NC,),
        in_specs=[col(0), col(1), col(2), col(3), row, row,
                  pl.BlockSpec((CPS, H, HD, HD), lambda c: (NC - 1 - c, 0, 0, 0)),
                  pl.BlockSpec((1, D_MODEL), lambda c: (0, 0)), pl.BlockSpec((1, HD), lambda c: (0, 0))],
        out_specs=(pl.BlockSpec((CPS * C, 4 * D_MODEL), lambda c: (NC - 1 - c, 0)),
                   pl.BlockSpec((1, D_MODEL), lambda c: (0, 0)), pl.BlockSpec((1, HD), lambda c: (0, 0))),
        scratch_shapes=[pltpu.VMEM((H, HD, HD), F32)] + [pltpu.VMEM((CPS, C, D_MODEL), F32)] * 3,
        compiler_params=_params("arbitrary"), name=name)(proj, proj, proj, proj, o_pre, d_og, states, lb, gain)


def _attn_masks():
    r = lax.broadcasted_iota(jnp.int32, (ATTN_BLOCK, ATTN_BLOCK), 0)
    c = lax.broadcasted_iota(jnp.int32, (ATTN_BLOCK, ATTN_BLOCK), 1)
    return c >= r, c <= r


def _attn_fwd(qkv, dilation, name):
    T = qkv.shape[0]
    nb = T // dilation // ATTN_BLOCK
    W = ATTN_GROUP_WIDTH
    B = ATTN_BLOCK
    scale = ATTN_DIM ** -0.5
    qb = 2 if nb % 2 == 0 else 1
    steps = nb // qb

    def body(q_ref, kp_ref, kc_ref, vp_ref, vc_ref, o_ref, lse_ref):
        no_prev = jnp.where(pl.program_id(1) > 0, 0.0, NEG_BIG)
        m_prev, m_cur = _attn_masks()
        ones = jnp.ones((B, ATTN_DIM), BF16)
        items = []
        for j in range(qb):
            for h in range(ATTN_GROUP_HEADS):
                sl = slice(h * ATTN_DIM, (h + 1) * ATTN_DIM)
                rows = slice(j * B, (j + 1) * B)
                if j == 0:
                    items.append((rows, sl, kp_ref[:, sl], vp_ref[:, sl], no_prev))
                else:
                    before = slice((j - 1) * B, j * B)
                    items.append((rows, sl, kc_ref[before, sl], vc_ref[before, sl], 0.0))
        s_p = [jnp.where(m_prev, _dot(q_ref[rows, sl], k_p, NT) * scale + bias, NEG_BIG)
               for rows, sl, k_p, _, bias in items]
        s_c = [jnp.where(m_cur, _dot(q_ref[rows, sl], kc_ref[rows, sl], NT) * scale, NEG_BIG)
               for rows, sl, _, _, _ in items]
        m = [jnp.max(jnp.maximum(a, b), axis=-1, keepdims=True) for a, b in zip(s_p, s_c)]
        p_p = [jnp.exp(a - mx).astype(BF16) for a, mx in zip(s_p, m)]
        p_c = [jnp.exp(b - mx).astype(BF16) for b, mx in zip(s_c, m)]
        l = [_dot(a, ones, NN) + _dot(b, ones, NN) for a, b in zip(p_p, p_c)]
        acc = [_dot(a, v_p, NN) + _dot(b, vc_ref[rows, sl], NN)
               for a, b, (rows, sl, _, v_p, _) in zip(p_p, p_c, items)]
        for (rows, sl, _, _, _), a, lv, mx in zip(items, acc, l, m):
            o_ref[rows, sl] = (a / lv).astype(BF16)
            lse_ref[rows, sl] = mx + jnp.log(lv)

    cur = lambda col: pl.BlockSpec((qb * B, W), lambda s, n: (s * steps + n, col))
    prev = lambda col: pl.BlockSpec((B, W), lambda s, n: (s * nb + jnp.maximum(qb * n - 1, 0), col))
    out = pl.BlockSpec((qb * B, W), lambda s, n: (s * steps + n, 0))
    return pl.pallas_call(
        body, out_shape=(jax.ShapeDtypeStruct((T, W), BF16), jax.ShapeDtypeStruct((T, W), F32)),
        grid=(dilation, steps),
        in_specs=[cur(0), prev(1), cur(1), prev(2), cur(2)],
        out_specs=(out, out), compiler_params=_params("parallel", "arbitrary"), name=name)(qkv, qkv, qkv, qkv, qkv)


def _attn_bwd(qkv, d_out, lse, delta, cos, sin, dilation, name):
    T = qkv.shape[0]
    nb = T // dilation // ATTN_BLOCK
    assert nb % 2 == 0, "an even number of 128-token blocks per residue class"
    pairs = nb // 2
    W = ATTN_GROUP_WIDTH
    B = ATTN_BLOCK
    scale = ATTN_DIM ** -0.5

    def unrope(x, cos_v, sin_v):
        return x * cos_v + pltpu.roll(x * sin_v, ATTN_DIM // 2, 1)

    def body(qa_ref, qb_ref, kpair_ref, kc_ref, vpair_ref, vc_ref, doa_ref, dob_ref, lsea_ref, lseb_ref,
             dla_ref, dlb_ref, cos_ref, sin_ref, out_ref, dq_scr, dk_scr, dv_scr):
        n = pl.program_id(1)

        @pl.when(n == 0)
        def _():
            dq_scr[...] = jnp.zeros_like(dq_scr)
            dk_scr[...] = jnp.zeros_like(dk_scr)
            dv_scr[...] = jnp.zeros_like(dv_scr)

        no_a = jnp.where(n > 0, 0.0, NEG_BIG)
        no_b = jnp.where(n < pairs, 0.0, NEG_BIG)
        m_prev, m_cur = _attn_masks()
        lo, hi = slice(0, B), slice(B, 2 * B)
        heads = [slice(h * ATTN_DIM, (h + 1) * ATTN_DIM) for h in range(ATTN_GROUP_HEADS)]
        flat = []
        for sl in heads:
            qa, qb = qa_ref[:, sl], qb_ref[:, sl]
            doa, dob = doa_ref[:, sl], dob_ref[:, sl]
            k0, k1, k2 = kpair_ref[lo, sl], kpair_ref[hi, sl], kc_ref[:, sl]
            v0, v1, v2 = vpair_ref[lo, sl], vpair_ref[hi, sl], vc_ref[:, sl]
            flat += [(qa, doa, lsea_ref[:, sl], dla_ref[:, sl], k0, v0, m_prev, no_a),
                     (qa, doa, lsea_ref[:, sl], dla_ref[:, sl], k1, v1, m_cur, no_a),
                     (qb, dob, lseb_ref[:, sl], dlb_ref[:, sl], k1, v1, m_prev, no_a + no_b),
                     (qb, dob, lseb_ref[:, sl], dlb_ref[:, sl], k2, v2, m_cur, no_b)]
        s = [_dot(q, k, NT) for q, _, _, _, k, _, _, _ in flat]
        dp = [_dot(do, v, NT) for _, do, _, _, _, v, _, _ in flat]
        p = [jnp.where(mask, jnp.exp(sv * scale - lse_v + bias), 0.0)
             for sv, (_, _, lse_v, _, _, _, mask, bias) in zip(s, flat)]
        ds = [(pv * (dpv - dl_v) * scale).astype(BF16) for pv, dpv, (_, _, _, dl_v, _, _, _, _) in zip(p, dp, flat)]
        p = [pv.astype(BF16) for pv in p]
        dq_part = [_dot(dsv, k, NN) for dsv, (_, _, _, _, k, _, _, _) in zip(ds, flat)]
        dk_part = [_dot(dsv, q, TN) for dsv, (q, _, _, _, _, _, _, _) in zip(ds, flat)]
        dv_part = [_dot(pv, do, TN) for pv, (_, do, _, _, _, _, _, _) in zip(p, flat)]
        cos_lo, sin_lo, cos_hi, sin_hi = cos_ref[lo, :], sin_ref[lo, :], cos_ref[hi, :], sin_ref[hi, :]
        for h, sl in enumerate(heads):
            a_prev, a_cur, b_prev, b_cur = range(4 * h, 4 * h + 4)
            kcol = slice(W + h * ATTN_DIM, W + (h + 1) * ATTN_DIM)
            vcol = slice(2 * W + h * ATTN_DIM, 2 * W + (h + 1) * ATTN_DIM)
            out_ref[lo, sl] = unrope(dq_scr[:, sl], cos_lo, sin_lo).astype(BF16)
            out_ref[hi, sl] = unrope(dq_part[a_prev] + dq_part[a_cur], cos_hi, sin_hi).astype(BF16)
            out_ref[lo, kcol] = unrope(dk_scr[:, sl] + dk_part[a_prev], cos_lo, sin_lo).astype(BF16)
            out_ref[hi, kcol] = unrope(dk_part[a_cur] + dk_part[b_prev], cos_hi, sin_hi).astype(BF16)
            out_ref[lo, vcol] = (dv_scr[:, sl] + dv_part[a_prev]).astype(BF16)
            out_ref[hi, vcol] = (dv_part[a_cur] + dv_part[b_prev]).astype(BF16)
            dq_scr[:, sl] = dq_part[b_prev] + dq_part[b_cur]
            dk_scr[:, sl] = dk_part[b_cur]
            dv_scr[:, sl] = dv_part[b_cur]

    def block_a(n):
        return jnp.maximum(2 * n - 1, 0)

    def block_b(n):
        return jnp.minimum(2 * n, nb - 1)

    def pair(n):
        return jnp.maximum(n - 1, 0)

    one_a = lambda col: pl.BlockSpec((B, W), lambda s, n: (s * nb + block_a(n), col))
    one_b = lambda col: pl.BlockSpec((B, W), lambda s, n: (s * nb + block_b(n), col))
    two = lambda col: pl.BlockSpec((2 * B, W), lambda s, n: (s * pairs + pair(n), col))
    tab = pl.BlockSpec((2 * B, ATTN_DIM), lambda s, n: (s * pairs + pair(n), 0))
    return pl.pallas_call(
        body, out_shape=jax.ShapeDtypeStruct((T, 3 * W), BF16), grid=(dilation, pairs + 1),
        in_specs=[one_a(0), one_b(0), two(1), one_b(1), two(2), one_b(2), one_a(0), one_b(0), one_a(0), one_b(0),
                  one_a(0), one_b(0), tab, tab],
        out_specs=pl.BlockSpec((2 * B, 3 * W), lambda s, n: (s * pairs + pair(n), 0)),
        scratch_shapes=[pltpu.VMEM((B, W), F32)] * 3,
        compiler_params=_params("parallel", "arbitrary"), name=name)(
            qkv, qkv, qkv, qkv, qkv, qkv, d_out, d_out, lse, lse, delta, delta, cos, sin)


PERM_TILE = 512
LANES = 128


def _residue_view(x, d):
    return x if d == 1 else x.reshape(d, x.shape[0] // d, x.shape[1])


def _residue_spec(d, tm, cols):
    if d == 1:
        return pl.BlockSpec((tm, cols), lambda i: (i, 0))
    return pl.BlockSpec((d, tm // d, cols), lambda i: (0, i, 0))


def _residue_shape(T, d, cols, dtype):
    return jax.ShapeDtypeStruct((T, cols) if d == 1 else (d, T // d, cols), dtype)


def _class_rows(r, d, tm):
    return pl.ds(r, tm // d, stride=d)


def _attn_norm(h, gain, name):
    T = h.shape[0]
    tm = _pick_tile(T, PERM_TILE, 16 * max(ATTN_DILATIONS))
    dils = ATTN_DILATIONS
    (base_cos, base_sin), (off_cos, off_sin), sign = _rope_parts(T, tm)

    def body(h_ref, g_ref, bc_ref, bs_ref, oc_ref, os_ref, sign_ref, *refs):
        u_refs, c_refs, s_refs, u_scr, c_scr, s_scr = refs[0:3], refs[3:6], refs[6:9], refs[9], refs[10], refs[11]
        hv = h_ref[...]
        rstd = lax.rsqrt(jnp.mean(hv * hv, axis=-1, keepdims=True) + NORM_EPS)
        u = hv * rstd * g_ref[...]
        for j in range(D_MODEL // LANES):
            u_scr[j] = u[:, j * LANES:(j + 1) * LANES]
        bc, bs, oc, osn = bc_ref[0], bs_ref[0], oc_ref[...], os_ref[...]
        c_scr[...] = bc * oc - bs * osn
        s_scr[...] = (bs * oc + bc * osn) * sign_ref[...]
        for d, u_ref, c_ref, s_ref in zip(dils, u_refs, c_refs, s_refs):
            if d == 1:
                u_ref[...] = u.astype(BF16)
                c_ref[...] = c_scr[...]
                s_ref[...] = s_scr[...]
                continue
            for r in range(d):
                rows = _class_rows(r, d, tm)
                for j in range(D_MODEL // LANES):
                    u_ref[r, :, j * LANES:(j + 1) * LANES] = u_scr.at[j][rows, :].astype(BF16)
                c_ref[r] = c_scr[rows, :]
                s_ref[r] = s_scr[rows, :]

    row = pl.BlockSpec((tm, D_MODEL), lambda i: (i, 0))
    base = pl.BlockSpec((1, 1, ATTN_DIM), lambda i: (i, 0, 0))
    off = pl.BlockSpec((tm, ATTN_DIM), lambda i: (0, 0))
    res = pl.pallas_call(
        body,
        out_shape=([_residue_shape(T, d, D_MODEL, BF16) for d in dils]
                   + [_residue_shape(T, d, ATTN_DIM, F32) for d in dils] * 2),
        grid=(T // tm,),
        in_specs=[row, pl.BlockSpec((1, D_MODEL), lambda i: (0, 0)), base, base, off, off,
                  pl.BlockSpec((1, ATTN_DIM), lambda i: (0, 0))],
        out_specs=([_residue_spec(d, tm, D_MODEL) for d in dils] + [_residue_spec(d, tm, ATTN_DIM) for d in dils] * 2),
        scratch_shapes=[pltpu.VMEM((D_MODEL // LANES, tm, LANES), F32), pltpu.VMEM((tm, ATTN_DIM), F32),
                        pltpu.VMEM((tm, ATTN_DIM), F32)],
        compiler_params=_params("parallel"), name=name)(h, gain, base_cos, base_sin, off_cos, off_sin, sign)
    flat = [r.reshape(T, r.shape[-1]) for r in res]
    return flat[0:3], flat[3:6], flat[6:9]


def _attn_merge_fwd(outs, lses, name):
    T = outs[0].shape[0]
    W = ATTN_GROUP_WIDTH
    tm = _pick_tile(T, PERM_TILE, 16 * max(ATTN_DILATIONS))
    dils = ATTN_DILATIONS

    def body(*refs):
        o_refs, l_refs, oc_ref, lse_refs = refs[0:3], refs[3:6], refs[6], refs[7:10]
        o_scr, l_scr, t_scr = refs[10:13]
        nh = ATTN_GROUP_HEADS
        for g, d in enumerate(dils):
            for j in range(nh):
                lanes = slice(j * LANES, (j + 1) * LANES)
                if d == 1:
                    o_scr[g * nh + j] = o_refs[g][:, lanes].astype(F32)
                    l_scr[g * nh + j] = l_refs[g][:, lanes]
                    continue
                for r in range(d):
                    rows = _class_rows(r, d, tm)
                    o_scr.at[g * nh + j][rows, :] = o_refs[g][r, :, lanes].astype(F32)
                    l_scr.at[g * nh + j][rows, :] = l_refs[g][r, :, lanes]
        for j in range(nh):
            lanes = slice(j * LANES, (j + 1) * LANES)
            ls = [l_scr[g * nh + j] for g in range(3)]
            m = jnp.maximum(jnp.maximum(ls[0], ls[1]), ls[2])
            tot = m + jnp.log(jnp.exp(ls[0] - m) + jnp.exp(ls[1] - m) + jnp.exp(ls[2] - m))
            t_scr[j] = tot
            for g, d in enumerate(dils):
                oc_ref[:, g * W + j * LANES:g * W + (j + 1) * LANES] = (
                    o_scr[g * nh + j] * jnp.exp(ls[g] - tot)).astype(BF16)
                if d == 1:
                    lse_refs[g][:, lanes] = tot
                    continue
                for r in range(d):
                    lse_refs[g][r, :, lanes] = t_scr.at[j][_class_rows(r, d, tm), :]

    in_blk = [_residue_spec(d, tm, W) for d in dils]
    n_blk = 3 * ATTN_GROUP_HEADS
    res = pl.pallas_call(
        body, out_shape=[jax.ShapeDtypeStruct((T, 3 * W), BF16)] + [_residue_shape(T, d, W, F32) for d in dils],
        grid=(T // tm,), in_specs=in_blk * 2,
        out_specs=[pl.BlockSpec((tm, 3 * W), lambda i: (i, 0))] + in_blk,
        scratch_shapes=[pltpu.VMEM((n_blk, tm, LANES), F32), pltpu.VMEM((n_blk, tm, LANES), F32),
                        pltpu.VMEM((ATTN_GROUP_HEADS, tm, LANES), F32)],
        compiler_params=_params("parallel"), name=name)(
            *[_residue_view(o, d) for o, d in zip(outs, dils)], *[_residue_view(l, d) for l, d in zip(lses, dils)])
    return res[0], [r.reshape(T, W) for r in res[1:]]


def _attn_merge_bwd(d_oc, oc, name):
    T = d_oc.shape[0]
    W = ATTN_GROUP_WIDTH
    tm = _pick_tile(T, PERM_TILE, 16 * max(ATTN_DILATIONS))
    dils = ATTN_DILATIONS

    def body(d_ref, o_ref, *refs):
        delta_refs, db_refs, dl_scr, d_scr = refs[0:3], refs[3:6], refs[6], refs[7]
        nh = ATTN_GROUP_HEADS
        for j in range(nh):
            tot = jnp.zeros((tm, 1), F32)
            for g in range(3):
                cols = slice(g * W + j * LANES, g * W + (j + 1) * LANES)
                d_blk = d_ref[:, cols]
                d_scr[g * nh + j] = d_blk
                tot = tot + jnp.sum(d_blk * o_ref[:, cols].astype(F32), axis=-1, keepdims=True)
            dl_scr[j] = jnp.broadcast_to(tot, (tm, LANES))
        for g, d in enumerate(dils):
            for j in range(nh):
                lanes = slice(j * LANES, (j + 1) * LANES)
                if d == 1:
                    delta_refs[g][:, lanes] = dl_scr[j]
                    db_refs[g][:, lanes] = d_scr[g * nh + j].astype(BF16)
                    continue
                for r in range(d):
                    rows = _class_rows(r, d, tm)
                    delta_refs[g][r, :, lanes] = dl_scr.at[j][rows, :]
                    db_refs[g][r, :, lanes] = d_scr.at[g * nh + j][rows, :].astype(BF16)

    wide = pl.BlockSpec((tm, 3 * W), lambda i: (i, 0))
    out_blk = [_residue_spec(d, tm, W) for d in dils]
    res = pl.pallas_call(
        body, out_shape=[_residue_shape(T, d, W, F32) for d in dils] + [_residue_shape(T, d, W, BF16) for d in dils],
        grid=(T // tm,), in_specs=[wide, wide], out_specs=out_blk * 2,
        scratch_shapes=[pltpu.VMEM((ATTN_GROUP_HEADS, tm, LANES), F32),
                        pltpu.VMEM((3 * ATTN_GROUP_HEADS, tm, LANES), F32)],
        compiler_params=_params("parallel"), name=name)(d_oc, oc)
    flat = [r.reshape(T, W) for r in res]
    return flat[0:3], flat[3:6]


def _rope_parts(T, tile):
    inv_freq = 1.0 / (ROPE_THETA ** (jnp.arange(0, ATTN_DIM, 2, dtype=F32) / ATTN_DIM))
    inv_freq = jnp.concatenate([inv_freq, inv_freq])[None, :]
    base = (jnp.arange(T // tile, dtype=F32) * tile)[:, None] * inv_freq
    off = jnp.arange(tile, dtype=F32)[:, None] * inv_freq
    sign = jnp.concatenate([-jnp.ones((1, ATTN_DIM // 2), F32), jnp.ones((1, ATTN_DIM // 2), F32)], axis=1)
    return (jnp.cos(base)[:, None, :], jnp.sin(base)[:, None, :]), (jnp.cos(off), jnp.sin(off)), sign


WEIGHT_GROUPS = {"hgrn": ("hgrn_in", "hgrn_out"), "ffn0": ("ffn_in0", "ffn_down0"),
                 "attn": ("qkv", "attn_out"), "ffn1": ("ffn_in1", "ffn_down1")}


def _local_step(x, target, norm_mix, norm_ffn, lb, out_gain, final_gain, fetch, publish):
    g_mix = [norm_mix[0:1], norm_mix[1:2]]
    g_ffn = [norm_ffn[0:1], norm_ffn[1:2]]
    w = {}

    def whole(name):
        return [(w[name], w[name].shape[0], 0)]

    def qkv_parts(g):
        return [(w["qkv"], ATTN_GROUP_WIDTH, 3 * j + g) for j in range(3)]

    def ffn_fwd(h, layer, head=None):
        w.update(fetch(f"ffn{layer}"))
        n, a_by_up, a_by_gate, a = _ffn_in(h, g_ffn[layer], w[f"ffn_in{layer}"], f"ffn{layer}_in")
        out = _mm_nn([a], [whole(f"ffn_down{layer}")], h, name=f"ffn{layer}_down", head=head)
        return out, (n, a_by_gate, a_by_up, a)

    def ffn_bwd(h, saved, dh, dhb, layer):
        n, a_by_gate, a_by_up, a = saved
        w_in = w[f"ffn_in{layer}"]
        dgate, dup = _ffn_down_dx(dhb, w[f"ffn_down{layer}"], a_by_gate, a_by_up, f"ffn{layer}_down_dx")
        grads = {f"ffn_down{layer}": _mm_tn([a], dhb, name=f"ffn{layer}_down_dw"),
                 f"ffn_in{layer}": _mm_tn([dgate, dup], n, name=f"ffn{layer}_in_dw")}
        publish(f"ffn{layer}", grads)
        return _mm_nn([dgate, dup], [[(w_in, D_FF, 0)], [(w_in, D_FF, 1)]], dh, name=f"ffn{layer}_in_dx",
                      norm=(h, g_ffn[layer]))

    u0 = _rms_fwd(x, g_mix[0], "hgrn_norm")
    w.update(fetch("hgrn"))
    proj = _mm_nt(u0, whole("hgrn_in"), out_dtype=F32, name="hgrn_in")
    og, o_pre, states = _hgrn_fwd(proj, lb, out_gain, "hgrn_fwd")
    h1 = _mm_nn([og], [whole("hgrn_out")], x, name="hgrn_out")
    h2, ffn0 = ffn_fwd(h1, 0)

    u1_g, cos_g, sin_g = _attn_norm(h2, g_mix[1], "attn_norm")
    w.update(fetch("attn"))
    qkv_g, outs, lses = [], [], []
    for g, d in enumerate(ATTN_DILATIONS):
        qkv_g.append(_mm_nt(u1_g[g], qkv_parts(g), out_dtype=BF16, name=f"attn_qkv{g}",
                            rope=(cos_g[g], sin_g[g], 2)))
        o_g, lse_g = _attn_fwd(qkv_g[g], d, f"attn_fwd{g}")
        outs.append(o_g)
        lses.append(lse_g)
    oc, lse_all = _attn_merge_fwd(outs, lses, "attn_merge")
    h3 = _mm_nn([oc], [whole("attn_out")], h2, name="attn_out")
    (dh4, dh4b, d_final, loss_part), ffn1 = ffn_fwd(h3, 1, head=(target, final_gain))
    dh3, dh3b, d_ffn1 = ffn_bwd(h3, ffn1, dh4, dh4b, 1)

    d_oc = _mm_nt(dh3b, whole("attn_out"), out_dtype=F32, name="attn_out_dx")
    grad_attn_out = _mm_tn([oc], dh3b, name="attn_out_dw")
    delta, d_ocb = _attn_merge_bwd(d_oc, oc, "attn_merge_bwd")
    du1, qkv_pieces = [], []
    for g, d in enumerate(ATTN_DILATIONS):
        dqkv = _attn_bwd(qkv_g[g], d_ocb[g], lse_all[g], delta[g], cos_g[g], sin_g[g], d, f"attn_bwd{g}")
        qkv_pieces.append(_mm_tn([dqkv], u1_g[g], name=f"attn_qkv_dw{g}"))
        du1.append(_mm_nn([dqkv], [qkv_parts(g)], None, name=f"attn_qkv_dx{g}"))
    grad_qkv = jnp.stack([p.reshape(3, ATTN_GROUP_WIDTH, D_MODEL) for p in qkv_pieces], axis=1).reshape(
        3 * ATTN_WIDTH, D_MODEL)
    publish("attn", {"qkv": grad_qkv, "attn_out": grad_attn_out})
    dh2, dh2b, d_mix1 = _rms_bwd(h2, g_mix[1], du1, dh3, "attn_norm_bwd", ATTN_DILATIONS)

    dh1, dh1b, d_ffn0 = ffn_bwd(h1, ffn0, dh2, dh2b, 0)

    d_og = _mm_nt(dh1b, whole("hgrn_out"), out_dtype=F32, name="hgrn_out_dx")
    grad_hgrn_out = _mm_tn([og], dh1b, name="hgrn_out_dw")
    dproj, d_lb, d_out_gain = _hgrn_bwd(proj, o_pre, d_og, states, lb, out_gain, "hgrn_bwd")
    publish("hgrn", {"hgrn_in": _mm_tn([dproj], u0, name="hgrn_in_dw"), "hgrn_out": grad_hgrn_out})
    dx, _, d_mix0 = _mm_nn([dproj], [whole("hgrn_in")], dh1, name="hgrn_in_dx", norm=(x, g_mix[0]))

    small = dict(norm_mix0=d_mix0, norm_mix1=d_mix1, norm_ffn0=d_ffn0, norm_ffn1=d_ffn1, lb=d_lb,
                 out_gain=d_out_gain, final=d_final, loss=loss_part)
    return dx, small


MESH_IDS = pl.DeviceIdType.MESH
HBM_SPEC = pl.BlockSpec(memory_space=pl.ANY)


N_PEERS = N_DEV - 1
PEER_OFFSETS = [(dx, dy, dc) for dx in (0, 1) for dy in (0, 1) for dc in (0, 1)][1:]


def _mesh_place():
    x, y, c = lax.axis_index("x"), lax.axis_index("y"), lax.axis_index("c")
    peers = []
    for dx, dy, dc in PEER_OFFSETS:
        px, py, pc = (1 - x if dx else x), (1 - y if dy else y), (1 - c if dc else c)
        peers.append(((px, py, pc), 4 * px + 2 * py + pc))
    return 4 * x + 2 * y + c, peers


def _gather_over_two_levels(src_refs, land_refs, send_sems, recv_sems):
    n = len(src_refs)
    x, y, c = lax.axis_index("x"), lax.axis_index("y"), lax.axis_index("c")
    me, sibling = (x, y, c), (x, y, 1 - c)
    chips = [(1 - x, y), (x, 1 - y), (1 - x, 1 - y)]

    def block(w, px, py, pc):
        return land_refs[w].at[4 * px + 2 * py + pc]

    def copy(w, k, owner, to, src=None):
        return pltpu.make_async_remote_copy(
            src_ref=block(w, *owner) if src is None else src, dst_ref=block(w, *owner),
            send_sem=send_sems.at[w * N_PEERS + k], recv_sem=recv_sems.at[w * N_PEERS + k],
            device_id=to, device_id_type=MESH_IDS)

    sent = []
    for w in range(n):
        sent.append(copy(w, 0, me, sibling, src=src_refs[w]))
        sent += [copy(w, 1 + j, me, (*chip, c), src=src_refs[w]) for j, chip in enumerate(chips)]
    for cp in sent:
        cp.start()
    for w in range(n):
        for j, chip in enumerate(chips):
            copy(w, 1 + j, (*chip, c), me).wait_recv()
            passed = copy(w, 4 + j, (*chip, c), sibling)
            passed.start()
            sent.append(passed)
    for w in range(n):
        copy(w, 0, sibling, me).wait_recv()
        for j, chip in enumerate(chips):
            copy(w, 4 + j, (*chip, 1 - c), me).wait_recv()
    for cp in sent:
        cp.wait_send()


def _exchange_launch(srcs, scatter, collective_id, name):
    n = len(srcs)
    src_refs = [jax.new_ref(s, memory_space=pltpu.MemorySpace.HBM) for s in srcs]
    land_refs = [jax.empty_ref(jax.ShapeDtypeStruct(s.shape if scatter else (N_DEV,) + s.shape, s.dtype),
                               memory_space=pltpu.MemorySpace.HBM) for s in srcs]

    @pl.kernel(mesh=plsc.ScalarSubcoreMesh(axis_name="sequencer", num_cores=1), name=name,
               scratch_types=(pltpu.SemaphoreType.DMA((n * N_PEERS,)), pltpu.SemaphoreType.DMA((n * N_PEERS,)),
                              pltpu.SemaphoreType.DMA((n,))),
               compiler_params=pltpu.CompilerParams(collective_id=collective_id))
    def launch(send_sems, recv_sems, local_sems):
        me, peers = _mesh_place()
        barrier = pltpu.get_barrier_semaphore()
        for peer, _ in peers:
            pl.semaphore_signal(barrier, inc=1, device_id=peer, device_id_type=MESH_IDS)
        pl.semaphore_wait(barrier, N_PEERS)
        own = [pltpu.make_async_copy(src_refs[w].at[me] if scatter else src_refs[w], land_refs[w].at[me],
                                     local_sems.at[w]) for w in range(n)]
        for cp in own:
            cp.start()
        if scatter:
            copies = [pltpu.make_async_remote_copy(
                src_ref=src_refs[w].at[pid], dst_ref=land_refs[w].at[me],
                send_sem=send_sems.at[w * N_PEERS + k], recv_sem=recv_sems.at[w * N_PEERS + k],
                device_id=peer, device_id_type=MESH_IDS) for w in range(n) for k, (peer, pid) in enumerate(peers)]
            for cp in copies:
                cp.start()
            for cp in copies:
                cp.wait()
        else:
            _gather_over_two_levels(src_refs, land_refs, send_sems, recv_sems)
        for cp in own:
            cp.wait()

    launch()
    return land_refs


def _gather_small(block, name):
    def body(in_ref, out_ref, send_sems, recv_sems, local_sem):
        me, peers = _mesh_place()
        own = pltpu.make_async_copy(in_ref, out_ref.at[me], local_sem)
        own.start()
        sends = [pltpu.make_async_remote_copy(
            src_ref=in_ref, dst_ref=out_ref.at[me], send_sem=send_sems.at[k], recv_sem=recv_sems.at[k],
            device_id=peer, device_id_type=MESH_IDS) for k, (peer, _) in enumerate(peers)]
        for cp in sends:
            cp.start()
        for cp in sends:
            cp.wait_recv()
        for cp in sends:
            cp.wait_send()
        own.wait()

    return pl.pallas_call(
        body, out_shape=jax.ShapeDtypeStruct((N_DEV,) + block.shape, block.dtype),
        in_specs=[HBM_SPEC], out_specs=HBM_SPEC,
        scratch_shapes=[pltpu.SemaphoreType.DMA((N_PEERS,)), pltpu.SemaphoreType.DMA((N_PEERS,)),
                        pltpu.SemaphoreType.DMA],
        name=name)(block)


def _sum_blocks(recv, name):
    rows = recv.shape[1]
    tr = _pick_tile(rows, 256, 16)

    def body(r_ref, g_ref):
        acc = r_ref[0].astype(F32)
        for j in range(1, N_DEV):
            acc = acc + r_ref[j].astype(F32)
        g_ref[...] = acc

    return pl.pallas_call(
        body, out_shape=jax.ShapeDtypeStruct((rows, D_MODEL), F32), grid=(rows // tr,),
        in_specs=[pl.BlockSpec((N_DEV, tr, D_MODEL), lambda i: (0, i, 0))],
        out_specs=pl.BlockSpec((tr, D_MODEL), lambda i: (i, 0)),
        compiler_params=_params("parallel"), name=name)(recv)


def _adamw_math(w, g, m, v):
    m_new = ADAM_B1 * m + (1.0 - ADAM_B1) * g
    v_new = ADAM_B2 * v + (1.0 - ADAM_B2) * (g * g)
    m_hat = m_new / (1.0 - ADAM_B1 ** ADAM_STEP)
    v_hat = v_new / (1.0 - ADAM_B2 ** ADAM_STEP)
    delta = -ADAM_LR * (m_hat / (jnp.sqrt(v_hat) + ADAM_EPS) + ADAM_WD * w)
    return delta, m_new, v_new


def _adamw(w, g, m, v, layer, others, name):
    _, rows, cols = w.shape
    tr = _pick_tile(rows, 256, 8)

    def body(w_ref, g_ref, m_ref, v_ref, *refs):
        go_ref, d_ref, mo_ref, vo_ref = refs[-4:]
        gv = g_ref[...]
        go_ref[...] = gv
        d_ref[...], mo_ref[...], vo_ref[...] = _adamw_math(w_ref[...], gv, m_ref[...], v_ref[...])

    one = pl.BlockSpec((None, tr, cols), lambda i: (layer, i, 0))
    in_specs = [one, pl.BlockSpec((tr, cols), lambda i: (i, 0)), one, one]
    args = [w, g, m, v]
    if others is not None:
        in_specs += [HBM_SPEC] * 4
        args += list(others)
    return pl.pallas_call(
        body, out_shape=(jax.ShapeDtypeStruct(w.shape, F32),) * 4, grid=(rows // tr,),
        in_specs=in_specs, out_specs=(one,) * 4,
        input_output_aliases={} if others is None else {4 + i: i for i in range(4)},
        compiler_params=_params("parallel"), name=name)(*args)


ROW_MIX, ROW_FFN, ROW_LB, ROW_OUT_GAIN, ROW_FINAL = 0, 2, 4, 7, 8
PART_MIX, PART_FFN, PART_LB, PART_OUT_GAIN, PART_FINAL, PART_LOSS = 0, 2, 4, 5, 6, 7


def _small_update(parts_all, w, m, v, name):
    def body(p_ref, w_ref, m_ref, v_ref, g_ref, d_ref, mo_ref, vo_ref, loss_ref):
        def total(row, n=1):
            tot = p_ref[0, row:row + n, :]
            for j in range(1, N_DEV):
                tot = tot + p_ref[j, row:row + n, :]
            return tot

        logits = [w_ref[ROW_LB + i:ROW_LB + i + 1, :] for i in range(3)]
        mx = jnp.maximum(jnp.maximum(logits[0], logits[1]), logits[2])
        ex = [jnp.exp(l - mx) for l in logits]
        den = ex[0] + ex[1] + ex[2]
        prob = [e / den for e in ex]
        d_lb = total(PART_LB)
        g_ref[...] = jnp.zeros_like(g_ref)
        g_ref[ROW_MIX:ROW_MIX + 2, :] = total(PART_MIX, 2)
        g_ref[ROW_FFN:ROW_FFN + 2, :] = total(PART_FFN, 2)
        for i in range(3):
            g_ref[ROW_LB + i:ROW_LB + i + 1, :] = prob[i] * ((d_lb if i == 0 else 0.0) - prob[0] * d_lb)
        g_ref[ROW_OUT_GAIN:ROW_OUT_GAIN + 1, :] = total(PART_OUT_GAIN)
        g_ref[ROW_FINAL:ROW_FINAL + 1, :] = total(PART_FINAL)
        d_ref[...], mo_ref[...], vo_ref[...] = _adamw_math(w_ref[...], g_ref[...], m_ref[...], v_ref[...])
        loss_ref[...] = jnp.sum(total(PART_LOSS), axis=-1, keepdims=True)

    packed = jax.ShapeDtypeStruct((16, D_MODEL), F32)
    return pl.pallas_call(
        body, out_shape=(packed, packed, packed, packed, jax.ShapeDtypeStruct((1, 1), F32)),
        compiler_params=pltpu.CompilerParams(vmem_limit_bytes=VMEM_LIMIT), name=name)(parts_all, w, m, v)


def _pack_small(norm_mix, norm_ffn, lb_logits, out_gain, final):
    pad = jnp.zeros((1, D_MODEL - HGRN_DIM), F32)
    return jnp.concatenate([norm_mix, norm_ffn, lb_logits, jnp.concatenate([out_gain, pad], axis=1),
                            final.reshape(1, D_MODEL), jnp.zeros((16 - ROW_FINAL - 1, D_MODEL), F32)], axis=0)


def _unpack_small(p):
    return (p[ROW_MIX:ROW_MIX + 2], p[ROW_FFN:ROW_FFN + 2], p[ROW_LB:ROW_LB + 3],
            p[ROW_OUT_GAIN:ROW_OUT_GAIN + 1, :HGRN_DIM], p[ROW_FINAL])


def _lower_bound(lb_logits, name):
    def body(l_ref, o_ref):
        logits = [l_ref[i:i + 1, :] for i in range(3)]
        mx = jnp.maximum(jnp.maximum(logits[0], logits[1]), logits[2])
        ex = [jnp.exp(l - mx) for l in logits]
        o_ref[...] = ex[0] / (ex[0] + ex[1] + ex[2])

    return pl.pallas_call(body, out_shape=jax.ShapeDtypeStruct((1, D_MODEL), F32), name=name)(lb_logits)


def kernel(x, norm_mix, norm_ffn, hgrn_w_in, hgrn_lb_logits, hgrn_out_norm, hgrn_w_out, attn_w_qkv, attn_w_out, ffn_w_in, ffn_w_down, final_norm, loss_target, m_norm_mix, m_norm_ffn, m_hgrn_w_in, m_hgrn_lb_logits, m_hgrn_out_norm, m_hgrn_w_out, m_attn_w_qkv, m_attn_w_out, m_ffn_w_in, m_ffn_w_down, m_final_norm, v_norm_mix, v_norm_ffn, v_hgrn_w_in, v_hgrn_lb_logits, v_hgrn_out_norm, v_hgrn_w_out, v_attn_w_qkv, v_attn_w_out, v_ffn_w_in, v_ffn_w_down, v_final_norm):
    col_sharded = {"hgrn_in": hgrn_w_in[0], "qkv": attn_w_qkv[0], "ffn_in0": ffn_w_in[0], "ffn_in1": ffn_w_in[1]}
    row_sharded = {"hgrn_out": hgrn_w_out[0], "attn_out": attn_w_out[0], "ffn_down0": ffn_w_down[0],
                   "ffn_down1": ffn_w_down[1]}
    gathering = {}
    for gi, (group, names) in enumerate(WEIGHT_GROUPS.items()):
        shards = [(col_sharded[n].T if n in col_sharded else row_sharded[n]).astype(BF16) for n in names]
        gathering[group] = _exchange_launch(shards, False, 1 + gi, f"weights_gather_{group}")

    def fetch(group):
        return {n: land[...].reshape(-1, D_MODEL) for n, land in zip(WEIGHT_GROUPS[group], gathering[group])}

    in_flight = {}

    def publish(group, grads):
        names = WEIGHT_GROUPS[group]
        parts = [grads[n].reshape(N_DEV, -1, D_MODEL) for n in names]
        in_flight[group] = _exchange_launch(parts, True, 1 + len(WEIGHT_GROUPS) + list(WEIGHT_GROUPS).index(group),
                                            f"grads_send_{group}")

    lb = _lower_bound(hgrn_lb_logits, "hgrn_lower_bound")
    grad_x, small = _local_step(x[0], loss_target[0], norm_mix, norm_ffn, lb, hgrn_out_norm,
                                final_norm.reshape(1, D_MODEL), fetch, publish)

    pad = jnp.zeros((1, D_MODEL - HGRN_DIM), F32)
    small_part = jnp.concatenate(
        [small["norm_mix0"], small["norm_mix1"], small["norm_ffn0"], small["norm_ffn1"], small["lb"],
         jnp.concatenate([small["out_gain"], pad], axis=1), small["final"], small["loss"]], axis=0)
    small_all = _gather_small(small_part, "small_grads_gather")
    received = {}
    for group in ("ffn1", "attn", "ffn0", "hgrn"):
        received.update(zip(WEIGHT_GROUPS[group], [land[...] for land in in_flight[group]]))

    masters = {"hgrn_w_in": (hgrn_w_in, m_hgrn_w_in, v_hgrn_w_in, ("hgrn_in",)),
               "hgrn_w_out": (hgrn_w_out, m_hgrn_w_out, v_hgrn_w_out, ("hgrn_out",)),
               "attn_w_qkv": (attn_w_qkv, m_attn_w_qkv, v_attn_w_qkv, ("qkv",)),
               "attn_w_out": (attn_w_out, m_attn_w_out, v_attn_w_out, ("attn_out",)),
               "ffn_w_in": (ffn_w_in, m_ffn_w_in, v_ffn_w_in, ("ffn_in0", "ffn_in1")),
               "ffn_w_down": (ffn_w_down, m_ffn_w_down, v_ffn_w_down, ("ffn_down0", "ffn_down1"))}
    big = {}
    for param, (wv, mv, vv, names) in masters.items():
        outs = None
        for layer, n in enumerate(names):
            g = _sum_blocks(received[n], f"{n}_grad_sum")
            outs = _adamw(wv, g.T if n in col_sharded else g, mv, vv, layer, outs, f"{n}_adamw")
        big[param] = list(outs)

    w_small = _pack_small(norm_mix, norm_ffn, hgrn_lb_logits, hgrn_out_norm, final_norm)
    m_small = _pack_small(m_norm_mix, m_norm_ffn, m_hgrn_lb_logits, m_hgrn_out_norm, m_final_norm)
    v_small = _pack_small(v_norm_mix, v_norm_ffn, v_hgrn_lb_logits, v_hgrn_out_norm, v_final_norm)
    g_s, d_s, m_s, v_s, loss = _small_update(small_all, w_small, m_small, v_small, "small_update")
    small_out = [_unpack_small(t) for t in (g_s, d_s, m_s, v_s)]

    def group(i):
        s = small_out[i]
        return (s[0], s[1], big["hgrn_w_in"][i], s[2], s[3], big["hgrn_w_out"][i], big["attn_w_qkv"][i],
                big["attn_w_out"][i], big["ffn_w_in"][i], big["ffn_w_down"][i], s[4])

    return (loss.reshape(()), grad_x[None], *group(0), *group(1), *group(2), *group(3))
```

```python
import functools

import jax
import jax.numpy as jnp
from jax import lax
from jax.experimental import pallas as pl
from jax.experimental.pallas import tpu as pltpu
from jax.experimental.pallas import tpu_sc as plsc

F32 = jnp.float32
BF16 = jnp.bfloat16

D_MODEL = 1024
N_DEV = 8
NORM_EPS = 1e-6

HGRN_HEADS = 8
HGRN_DIM = 128
HGRN_CHUNK = 64
HGRN_STEP_CHUNKS = 2
HGRN_EXP_CLAMP = 60.0

ATTN_DIM = 128
ATTN_BLOCK = 128
ATTN_GROUP_HEADS = 4
ATTN_GROUP_WIDTH = ATTN_GROUP_HEADS * ATTN_DIM
ATTN_DILATIONS = (1, 4, 16)
ATTN_WIDTH = 3 * ATTN_GROUP_WIDTH
ROPE_THETA = 10000.0
NEG_BIG = -1e30

D_FF = 2816

ADAM_LR = 0.001
ADAM_B1 = 0.9
ADAM_B2 = 0.999
ADAM_EPS = 1e-08
ADAM_WD = 0.01
ADAM_STEP = 10

VMEM_LIMIT = 48 * 1024 * 1024

NT = (((1,), (1,)), ((), ()))
NN = (((1,), (0,)), ((), ()))
TN = (((0,), (0,)), ((), ()))


def _dot(a, b, dims):
    return lax.dot_general(a, b, dims, preferred_element_type=F32)


def _params(*sem):
    return pltpu.CompilerParams(dimension_semantics=sem, vmem_limit_bytes=VMEM_LIMIT)


def _pick_tile(n, cap, mult):
    best = None
    for t in range(mult, min(n, cap) + 1, mult):
        if n % t == 0:
            best = t
    assert best is not None, (n, cap, mult)
    return best


def _sigmoid(x):
    return 0.5 * jnp.tanh(0.5 * x) + 0.5


ROW_TILE = 512
COL_CHUNK = 512
GRAD_TILE = 256


def _whole(shape, index_map):
    return pl.BlockSpec(shape, index_map, pipeline_mode=pl.Buffered(1))


def _part_specs(parts, n_cols):
    return [_whole((rows, n_cols), functools.partial(lambda i, b: (b, 0), b=blk)) for _, rows, blk in parts]


def _mm_nt(a, w_parts, *, out_dtype, name, rope=None):
    M, K = a.shape
    tm = _pick_tile(M, ROW_TILE, 16)
    widths = [rows for _, rows, _ in w_parts]
    n_parts = len(w_parts)

    def body(*refs):
        a_ref, w_refs, o_ref = refs[0], refs[1:1 + n_parts], refs[-1]
        av = a_ref[...]
        off = 0
        for p, w_ref in enumerate(w_refs):
            for c0 in range(0, widths[p], COL_CHUNK):
                cw = min(COL_CHUNK, widths[p] - c0)
                acc = _dot(av, w_ref[c0:c0 + cw, :], NT)
                if rope is not None and p < rope[2]:
                    cos, sin = refs[1 + n_parts][...], refs[2 + n_parts][...]
                    for h0 in range(0, cw, ATTN_DIM):
                        xh = acc[:, h0:h0 + ATTN_DIM]
                        rot = pltpu.roll(xh, ATTN_DIM // 2, 1)
                        o_ref[:, off + c0 + h0:off + c0 + h0 + ATTN_DIM] = (xh * cos + rot * sin).astype(out_dtype)
                else:
                    o_ref[:, off + c0:off + c0 + cw] = acc.astype(out_dtype)
            off += widths[p]

    in_specs = [pl.BlockSpec((tm, K), lambda i: (i, 0))] + _part_specs(w_parts, K)
    args = [a] + [w for w, _, _ in w_parts]
    if rope is not None:
        in_specs += [pl.BlockSpec((tm, ATTN_DIM), lambda i: (i, 0))] * 2
        args += [rope[0], rope[1]]
    return pl.pallas_call(
        body, out_shape=jax.ShapeDtypeStruct((M, sum(widths)), out_dtype), grid=(M // tm,),
        in_specs=in_specs, out_specs=pl.BlockSpec((tm, sum(widths)), lambda i: (i, 0)),
        compiler_params=_params("parallel"), name=name)(*args)


def _mm_nn(a_list, w_parts_list, resid, *, name, norm=None, head=None):
    M = a_list[0].shape[0]
    tm = _pick_tile(M, ROW_TILE, 16)
    n_a = len(a_list)
    flat_parts = [p for parts in w_parts_list for p in parts]
    extra = norm if norm is not None else head
    n_in = n_a + len(flat_parts) + (1 if resid is not None else 0) + (2 if extra is not None else 0)

    def body(*refs):
        a_refs, w_refs = refs[:n_a], refs[n_a:n_a + len(flat_parts)]

        def product(rows):
            acc = None
            wi = 0
            for a_ref, parts in zip(a_refs, w_parts_list):
                off = 0
                for _, k, _ in parts:
                    term = _dot(a_ref[rows, off:off + k], w_refs[wi][...], NN)
                    acc = term if acc is None else acc + term
                    off += k
                    wi += 1
            return acc

        if extra is None:
            acc = product(slice(None))
            if resid is not None:
                acc = acc + refs[n_in - 1][...]
            refs[n_in][...] = acc
            return

        @pl.when(pl.program_id(0) == 0)
        def _():
            for acc_ref in refs[n_in + 2:]:
                acc_ref[...] = jnp.zeros_like(acc_ref)

        for r0 in range(0, tm, tm // 2):
            rows = slice(r0, r0 + tm // 2)
            acc = product(rows)
            if head is not None:
                _loss_head_math(acc + refs[n_in - 3][rows, :], rows, refs[n_in - 2], refs[n_in - 1],
                                *refs[n_in:n_in + 4])
                continue
            dres_ref, x_ref, g_ref = refs[n_in - 3:n_in]
            dx_ref, dxb_ref, dg_ref = refs[n_in:n_in + 3]
            xv = x_ref[rows, :]
            rstd = lax.rsqrt(jnp.mean(xv * xv, axis=-1, keepdims=True) + NORM_EPS)
            n = xv * rstd
            dg_ref[...] += jnp.sum(acc * n, axis=0, keepdims=True)
            dn = acc * g_ref[...]
            dx = dres_ref[rows, :] + rstd * (dn - n * jnp.mean(dn * n, axis=-1, keepdims=True))
            dx_ref[rows, :] = dx
            dxb_ref[rows, :] = dx.astype(BF16)

    row = pl.BlockSpec((tm, D_MODEL), lambda i: (i, 0))
    vec = pl.BlockSpec((1, D_MODEL), lambda i: (0, 0))
    in_specs = [pl.BlockSpec((tm, a.shape[1]), lambda i: (i, 0)) for a in a_list] + _part_specs(flat_parts, D_MODEL)
    args = list(a_list) + [w for w, _, _ in flat_parts]
    if resid is not None:
        in_specs.append(row)
        args.append(resid)
    if extra is None:
        return pl.pallas_call(
            body, out_shape=jax.ShapeDtypeStruct((M, D_MODEL), F32), grid=(M // tm,),
            in_specs=in_specs, out_specs=row, compiler_params=_params("parallel"), name=name)(*args)
    assert resid is not None
    out_shape = [jax.ShapeDtypeStruct((M, D_MODEL), F32), jax.ShapeDtypeStruct((M, D_MODEL), BF16),
                 jax.ShapeDtypeStruct((1, D_MODEL), F32)]
    out_specs = [row, row, vec]
    if head is not None:
        out_shape.append(jax.ShapeDtypeStruct((1, D_MODEL), F32))
        out_specs.append(vec)
    return pl.pallas_call(
        body, out_shape=out_shape, grid=(M // tm,), in_specs=in_specs + [row, vec], out_specs=out_specs,
        compiler_params=_params("arbitrary"), name=name)(*args, extra[0], extra[1])


def _mm_tn(a_list, b, *, name):
    T = a_list[0].shape[0]
    N = b.shape[1]
    tr = GRAD_TILE
    tiles = [a.shape[1] // tr for a in a_list]
    starts = [sum(tiles[:i]) for i in range(len(tiles))]

    def body(*refs):
        a_refs, b_ref, o_ref = refs[:len(a_list)], refs[len(a_list)], refs[-1]
        r = pl.program_id(0)
        for a_ref, first, count in zip(a_refs, starts, tiles):
            @pl.when(jnp.logical_and(r >= first, r < first + count))
            def _():
                o_ref[...] = _dot(a_ref[...], b_ref[...], TN).astype(BF16)

    in_specs = [pl.BlockSpec((T, tr), functools.partial(lambda r, first, count: (0, jnp.clip(r - first, 0, count - 1)),
                                                        first=first, count=count))
                for first, count in zip(starts, tiles)]
    in_specs.append(_whole((T, N), lambda r: (0, 0)))
    return pl.pallas_call(
        body, out_shape=jax.ShapeDtypeStruct((sum(tiles) * tr, N), BF16), grid=(sum(tiles),),
        in_specs=in_specs, out_specs=pl.BlockSpec((tr, N), lambda r: (r, 0)),
        compiler_params=_params("parallel"), name=name)(*a_list, b)


def _rms_fwd(x, gain, name):
    T = x.shape[0]
    tm = _pick_tile(T, 512, 16)

    def body(x_ref, g_ref, u_ref):
        xv = x_ref[...]
        rstd = lax.rsqrt(jnp.mean(xv * xv, axis=-1, keepdims=True) + NORM_EPS)
        u_ref[...] = (xv * rstd * g_ref[...]).astype(BF16)

    return pl.pallas_call(
        body, out_shape=jax.ShapeDtypeStruct((T, D_MODEL), BF16), grid=(T // tm,),
        in_specs=[pl.BlockSpec((tm, D_MODEL), lambda i: (i, 0)), pl.BlockSpec((1, D_MODEL), lambda i: (0, 0))],
        out_specs=pl.BlockSpec((tm, D_MODEL), lambda i: (i, 0)),
        compiler_params=_params("parallel"), name=name)(x, gain)


def _rms_bwd(x, gain, dus, dres, name, dilations=(1,)):
    T = x.shape[0]
    tm = _pick_tile(T, PERM_TILE, 16 * max(dilations))
    n_du = len(dus)

    def body(x_ref, g_ref, *refs):
        du_refs, dres_ref = refs[:n_du], refs[n_du]
        dx_ref, dxb_ref, dg_ref, du_scr = refs[n_du + 1:]

        @pl.when(pl.program_id(0) == 0)
        def _():
            dg_ref[...] = jnp.zeros_like(dg_ref)

        if tuple(dilations) == (1,):
            du = du_refs[0][...]
        else:
            for i, (d, du_ref) in enumerate(zip(dilations, du_refs)):
                for j in range(D_MODEL // LANES):
                    lanes = slice(j * LANES, (j + 1) * LANES)
                    if d == 1:
                        du_scr[j] = du_ref[:, lanes] if i == 0 else du_scr[j] + du_ref[:, lanes]
                        continue
                    blk = du_scr.at[j]
                    for r in range(d):
                        rows = _class_rows(r, d, tm)
                        blk[rows, :] = du_ref[r, :, lanes] if i == 0 else blk[rows, :] + du_ref[r, :, lanes]
            du = jnp.concatenate([du_scr[j] for j in range(D_MODEL // LANES)], axis=1)
        xv = x_ref[...]
        rstd = lax.rsqrt(jnp.mean(xv * xv, axis=-1, keepdims=True) + NORM_EPS)
        n = xv * rstd
        dg_ref[...] += jnp.sum(du * n, axis=0, keepdims=True)
        dn = du * g_ref[...]
        dx = dres_ref[...] + rstd * (dn - n * jnp.mean(dn * n, axis=-1, keepdims=True))
        dx_ref[...] = dx
        dxb_ref[...] = dx.astype(BF16)

    row = pl.BlockSpec((tm, D_MODEL), lambda i: (i, 0))
    vec = pl.BlockSpec((1, D_MODEL), lambda i: (0, 0))
    return pl.pallas_call(
        body,
        out_shape=(jax.ShapeDtypeStruct((T, D_MODEL), F32), jax.ShapeDtypeStruct((T, D_MODEL), BF16),
                   jax.ShapeDtypeStruct((1, D_MODEL), F32)),
        grid=(T // tm,), in_specs=[row, vec] + [_residue_spec(d, tm, D_MODEL) for d in dilations] + [row],
        out_specs=(row, row, vec), scratch_shapes=[pltpu.VMEM((D_MODEL // LANES, tm, LANES), F32)],
        compiler_params=_params("arbitrary"), name=name)(
            x, gain, *[_residue_view(du, d) for du, d in zip(dus, dilations)], dres)


def _loss_head_math(hv, rows, t_ref, g_ref, dh_ref, dhb_ref, dg_ref, loss_ref):
    inv_f = 1.0 / D_MODEL
    g = g_ref[...]
    rstd = lax.rsqrt(jnp.mean(hv * hv, axis=-1, keepdims=True) + NORM_EPS)
    n = hv * rstd
    err = n * g - t_ref[rows, :]
    loss_ref[...] += (0.5 * inv_f) * jnp.sum(err * err, axis=0, keepdims=True)
    dy = err * inv_f
    dg_ref[...] += jnp.sum(dy * n, axis=0, keepdims=True)
    dn = dy * g
    dh = rstd * (dn - n * jnp.mean(dn * n, axis=-1, keepdims=True))
    dh_ref[rows, :] = dh
    dhb_ref[rows, :] = dh.astype(BF16)


FFN_TILE = 256


def _ffn_in(h, gain, w_in, name):
    T = h.shape[0]
    tm = _pick_tile(T, ROW_TILE, 16)

    def body(h_ref, g_ref, w_ref, n_ref, gate_ref, up_ref, a_ref):
        hv = h_ref[...]
        rstd = lax.rsqrt(jnp.mean(hv * hv, axis=-1, keepdims=True) + NORM_EPS)
        n = (hv * rstd * g_ref[...]).astype(BF16)
        n_ref[...] = n
        for c0 in range(0, D_FF, FFN_TILE):
            cols = slice(c0, c0 + FFN_TILE)
            gate = _dot(n, w_ref[c0:c0 + FFN_TILE, :], NT)
            up = _dot(n, w_ref[D_FF + c0:D_FF + c0 + FFN_TILE, :], NT)
            gate_ref[:, cols] = gate.astype(BF16)
            up_ref[:, cols] = up.astype(BF16)
            a_ref[:, cols] = (gate * _sigmoid(gate) * up).astype(BF16)

    row = pl.BlockSpec((tm, D_MODEL), lambda i: (i, 0))
    wide = pl.BlockSpec((tm, D_FF), lambda i: (i, 0))
    wide_shape = jax.ShapeDtypeStruct((T, D_FF), BF16)
    return pl.pallas_call(
        body, out_shape=(jax.ShapeDtypeStruct((T, D_MODEL), BF16), wide_shape, wide_shape, wide_shape),
        grid=(T // tm,),
        in_specs=[row, pl.BlockSpec((1, D_MODEL), lambda i: (0, 0)), _whole((2 * D_FF, D_MODEL), lambda i: (0, 0))],
        out_specs=(row, wide, wide, wide), compiler_params=_params("parallel"), name=name)(h, gain, w_in)


def _ffn_down_dx(dhb, w_down, gate, up, name):
    T = dhb.shape[0]
    tm = _pick_tile(T, ROW_TILE, 16)

    def body(dh_ref, w_ref, gate_ref, up_ref, dgate_ref, dup_ref):
        dh = dh_ref[...]
        for c0 in range(0, D_FF, FFN_TILE):
            cols = slice(c0, c0 + FFN_TILE)
            da = _dot(dh, w_ref[c0:c0 + FFN_TILE, :], NT).astype(BF16)
            gate = gate_ref[:, cols]
            sg = (0.5 * jnp.tanh(0.5 * jnp.abs(gate)) + 0.5) * jnp.exp(jnp.minimum(gate, 0.0))
            silu = gate * sg
            dgate_ref[:, cols] = da * up_ref[:, cols] * (sg + silu * (1.0 - sg))
            dup_ref[:, cols] = da * silu

    wide = pl.BlockSpec((tm, D_FF), lambda i: (i, 0))
    wide_shape = jax.ShapeDtypeStruct((T, D_FF), BF16)
    return pl.pallas_call(
        body, out_shape=(wide_shape, wide_shape), grid=(T // tm,),
        in_specs=[pl.BlockSpec((tm, D_MODEL), lambda i: (i, 0)), _whole((D_FF, D_MODEL), lambda i: (0, 0)), wide, wide],
        out_specs=(wide, wide), compiler_params=_params("parallel"), name=name)(dhb, w_down, gate, up)


def _tri(n, lower):
    r = lax.broadcasted_iota(jnp.int32, (n, n), 0)
    c = lax.broadcasted_iota(jnp.int32, (n, n), 1)
    return (c <= r) if lower else (c >= r)


def _running_sum(x, lower):
    tri = _tri(x.shape[0], lower).astype(BF16)
    hi = x.astype(BF16)
    rest = x - hi.astype(F32)
    mid = rest.astype(BF16)
    lo = (rest - mid.astype(F32)).astype(BF16)
    return _dot(tri, hi, NN) + _dot(tri, mid, NN) + _dot(tri, lo, NN)


def _hgrn_gates(q_raw, f_raw, lb):
    C = q_raw.shape[0]
    sig_f = _sigmoid(f_raw)
    forget = lb + (1.0 - lb) * sig_f
    key = 1.0 - forget
    log_f = jnp.log(forget)
    b = _running_sum(log_f, True)
    first_half = lax.broadcasted_iota(jnp.int32, log_f.shape, 0) < C // 2
    r = jnp.sum(jnp.where(first_half, log_f, 0.0), axis=0, keepdims=True)
    b_last = jnp.sum(log_f, axis=0, keepdims=True)
    e_a = jnp.exp(jnp.minimum(b - r, HGRN_EXP_CLAMP))
    e_b = jnp.exp(jnp.minimum(r - b, HGRN_EXP_CLAMP))
    e_q = jnp.exp(b)
    e_k = jnp.exp(b_last - b)
    sig_q = _sigmoid(q_raw)
    query = q_raw * sig_q
    return dict(sig_f=sig_f, forget=forget, sig_q=sig_q, e_a=e_a, e_b=e_b, e_q=e_q, e_k=e_k,
                e_last=jnp.exp(b_last), q_a=query * e_a, k_b=key * e_b, q_hat=query * e_q, k_til=key * e_k)


def _hgrn_fwd(proj, lb, gain, name):
    T = proj.shape[0]
    C = HGRN_CHUNK
    CPS = HGRN_STEP_CHUNKS
    H, HD = HGRN_HEADS, HGRN_DIM

    def body(q_ref, f_ref, i_ref, g_ref, lb_ref, gain_ref, og_ref, o_ref, st_ref, s_scr):
        @pl.when(pl.program_id(0) == 0)
        def _():
            s_scr[...] = jnp.zeros_like(s_scr)

        causal = _tri(C, True)
        gain_v = gain_ref[...]
        heads = [slice(h * HD, (h + 1) * HD) for h in range(H)]
        s_t = [s_scr[h] for h in range(H)]
        for cc in range(CPS):
            rows = slice(cc * C, (cc + 1) * C)
            for h in range(H):
                st_ref[cc, h] = s_t[h]
            gt = _hgrn_gates(q_ref[rows, :], f_ref[rows, :], lb_ref[...])
            q_a, k_b = gt["q_a"].astype(BF16), gt["k_b"].astype(BF16)
            q_hat, k_til = gt["q_hat"].astype(BF16), gt["k_til"].astype(BF16)
            v = i_ref[rows, :].astype(BF16)
            p = [jnp.where(causal, _dot(q_a[:, sl], k_b[:, sl], NT), 0.0).astype(BF16) for sl in heads]
            o = [_dot(p[h], v[:, sl], NN) + _dot(q_hat[:, sl], s_t[h].astype(BF16), NT)
                 for h, sl in enumerate(heads)]
            s_t = [gt["e_last"][:, sl] * s_t[h] + _dot(v[:, sl], k_til[:, sl], TN) for h, sl in enumerate(heads)]
            for h, sl in enumerate(heads):
                o_ref[rows, sl] = o[h]
                rstd = lax.rsqrt(jnp.mean(o[h] * o[h], axis=-1, keepdims=True) + NORM_EPS)
                g_raw = g_ref[rows, sl]
                og_ref[rows, sl] = (o[h] * rstd * gain_v * (g_raw * _sigmoid(g_raw))).astype(BF16)
        for h in range(H):
            s_scr[h] = s_t[h]

    col = lambda j: pl.BlockSpec((CPS * C, D_MODEL), lambda c: (c, j))
    row = pl.BlockSpec((CPS * C, D_MODEL), lambda c: (c, 0))
    return pl.pallas_call(
        body,
        out_shape=(jax.ShapeDtypeStruct((T, D_MODEL), BF16), jax.ShapeDtypeStruct((T, D_MODEL), F32),
                   jax.ShapeDtypeStruct((T // C, H, HD, HD), F32)),
        grid=(T // (CPS * C),),
        in_specs=[col(0), col(1), col(2), col(3), pl.BlockSpec((1, D_MODEL), lambda c: (0, 0)),
                  pl.BlockSpec((1, HD), lambda c: (0, 0))],
        out_specs=(row, row, pl.BlockSpec((CPS, H, HD, HD), lambda c: (c, 0, 0, 0))),
        scratch_shapes=[pltpu.VMEM((H, HD, HD), F32)],
        compiler_params=_params("arbitrary"), name=name)(proj, proj, proj, proj, lb, gain)


def _hgrn_bwd(proj, o_pre, d_og, states, lb, gain, name):
    T = proj.shape[0]
    C = HGRN_CHUNK
    CPS = HGRN_STEP_CHUNKS
    H, HD = HGRN_HEADS, HGRN_DIM
    NC = T // (CPS * C)

    def body(q_ref, f_ref, i_ref, g_ref, o_ref, dog_ref, st_ref, lb_ref, gain_ref,
             dproj_ref, dlb_ref, dgain_ref, ds_scr, dq_all, dk_all, db_all):
        @pl.when(pl.program_id(0) == 0)
        def _():
            ds_scr[...] = jnp.zeros_like(ds_scr)
            dlb_ref[...] = jnp.zeros_like(dlb_ref)
            dgain_ref[...] = jnp.zeros_like(dgain_ref)

        lbv = lb_ref[...]
        causal = _tri(C, True)
        last_row = lax.broadcasted_iota(jnp.int32, (C, HD), 0) == C - 1
        gain_v = gain_ref[...]
        heads = [slice(h * HD, (h + 1) * HD) for h in range(H)]
        hs = range(H)
        ds_t = [ds_scr[h] for h in hs]
        dgain = None
        for cc in reversed(range(CPS)):
            rows = slice(cc * C, (cc + 1) * C)
            dq_scr, dk_scr, db_scr = dq_all.at[cc], dk_all.at[cc], db_all.at[cc]
            q_raw = q_ref[rows, :]
            gt = _hgrn_gates(q_raw, f_ref[rows, :], lbv)
            o = [o_ref[rows, sl] for sl in heads]
            rstd = [lax.rsqrt(jnp.mean(x * x, axis=-1, keepdims=True) + NORM_EPS) for x in o]
            n = [x * r for x, r in zip(o, rstd)]
            g_raw = [g_ref[rows, sl] for sl in heads]
            sg = [_sigmoid(x) for x in g_raw]
            d_out = [dog_ref[rows, sl] for sl in heads]
            dy = [d * (g * s) for d, g, s in zip(d_out, g_raw, sg)]
            dn = [x * gain_v for x in dy]
            do = [(rstd[h] * (dn[h] - n[h] * jnp.mean(dn[h] * n[h], axis=-1, keepdims=True))).astype(BF16) for h in hs]
            for h in hs:
                dgain = dy[h] * n[h] if dgain is None else dgain + dy[h] * n[h]
            for h, sl in enumerate(heads):
                dproj_ref[rows, 3 * D_MODEL + h * HD:3 * D_MODEL + (h + 1) * HD] = (
                    d_out[h] * n[h] * gain_v * (sg[h] * (1.0 + g_raw[h] * (1.0 - sg[h])))).astype(BF16)
            q_ab, k_bb = gt["q_a"].astype(BF16), gt["k_b"].astype(BF16)
            q_hb, k_tb = gt["q_hat"].astype(BF16), gt["k_til"].astype(BF16)
            v = i_ref[rows, :].astype(BF16)
            s_t = [st_ref[cc, h] for h in hs]
            ds_b = [x.astype(BF16) for x in ds_t]
            p = [jnp.where(causal, _dot(q_ab[:, sl], k_bb[:, sl], NT), 0.0).astype(BF16) for sl in heads]
            dp = [jnp.where(causal, _dot(do[h], v[:, sl], NT), 0.0).astype(BF16) for h, sl in enumerate(heads)]
            dv = [_dot(p[h], do[h], TN) + _dot(k_tb[:, sl], ds_b[h], NT) for h, sl in enumerate(heads)]
            dq_a = [_dot(dp[h], k_bb[:, sl], NN) for h, sl in enumerate(heads)]
            dk_b = [_dot(dp[h], q_ab[:, sl], TN) for h, sl in enumerate(heads)]
            dq_hat = [_dot(do[h], s_t[h].astype(BF16), NN) for h in hs]
            dk_til = [_dot(v[:, sl], ds_b[h], NN) for h, sl in enumerate(heads)]
            ds_new = [_dot(do[h], q_hb[:, sl], TN) + gt["e_last"][:, sl] * ds_t[h] for h, sl in enumerate(heads)]
            for h, sl in enumerate(heads):
                k_til = gt["k_til"][:, sl]
                db_last = jnp.sum(ds_t[h] * gt["e_last"][:, sl] * s_t[h], axis=0, keepdims=True) + jnp.sum(
                    dk_til[h] * k_til, axis=0, keepdims=True)
                dproj_ref[rows, 2 * D_MODEL + h * HD:2 * D_MODEL + (h + 1) * HD] = dv[h].astype(BF16)
                dq_scr[:, sl] = dq_a[h] * gt["e_a"][:, sl] + dq_hat[h] * gt["e_q"][:, sl]
                dk_scr[:, sl] = dk_b[h] * gt["e_b"][:, sl] + dk_til[h] * gt["e_k"][:, sl]
                db = (dq_a[h] * q_ab[:, sl].astype(F32) + dq_hat[h] * gt["q_hat"][:, sl]
                      - dk_b[h] * k_bb[:, sl].astype(F32) - dk_til[h] * k_til)
                db_scr[:, sl] = db + jnp.where(last_row, db_last, 0.0)
            dlogf = _running_sum(db_scr[...], False)
            sig_f, forget, sig_q = gt["sig_f"], gt["forget"], gt["sig_q"]
            dforget = dlogf / forget - dk_scr[...]
            dproj_ref[rows, D_MODEL:2 * D_MODEL] = (dforget * (1.0 - lbv) * sig_f * (1.0 - sig_f)).astype(BF16)
            dlb_ref[...] += jnp.sum(dforget * (1.0 - sig_f), axis=0, keepdims=True)
            dproj_ref[rows, 0:D_MODEL] = (dq_scr[...] * (sig_q * (1.0 + q_raw * (1.0 - sig_q)))).astype(BF16)
            ds_t = ds_new
        dgain_ref[...] += jnp.sum(dgain, axis=0, keepdims=True)
        for h in hs:
            ds_scr[h] = ds_t[h]

    col = lambda j: pl.BlockSpec((CPS * C, D_MODEL), lambda c: (NC - 1 - c, j))
    row = pl.BlockSpec((CPS * C, D_MODEL), lambda c: (NC - 1 - c, 0))
    return pl.pallas_call(
        body,
        out_shape=(jax.ShapeDtypeStruct((T, 4 * D_MODEL), BF16), jax.ShapeDtypeStruct((1, D_MODEL), F32),
                   jax.ShapeDtypeStruct((1, HD), F32)),
        grid=(NC,),
        in_specs=[col(0), col(1), col(2), col(3), row, row,
                  pl.BlockSpec((CPS, H, HD, HD), lambda c: (NC - 1 - c, 0, 0, 0)),
                  pl.BlockSpec((1, D_MODEL), lambda c: (0, 0)), pl.BlockSpec((1, HD), lambda c: (0, 0))],
        out_specs=(pl.BlockSpec((CPS * C, 4 * D_MODEL), lambda c: (NC - 1 - c, 0)),
                   pl.BlockSpec((1, D_MODEL), lambda c: (0, 0)), pl.BlockSpec((1, HD), lambda c: (0, 0))),
        scratch_shapes=[pltpu.VMEM((H, HD, HD), F32)] + [pltpu.VMEM((CPS, C, D_MODEL), F32)] * 3,
        compiler_params=_params("arbitrary"), name=name)(proj, proj, proj, proj, o_pre, d_og, states, lb, gain)


def _attn_masks():
    r = lax.broadcasted_iota(jnp.int32, (ATTN_BLOCK, ATTN_BLOCK), 0)
    c = lax.broadcasted_iota(jnp.int32, (ATTN_BLOCK, ATTN_BLOCK), 1)
    return c >= r, c <= r


def _attn_fwd(qkv, dilation, name):
    T = qkv.shape[0]
    nb = T // dilation // ATTN_BLOCK
    W = ATTN_GROUP_WIDTH
    B = ATTN_BLOCK
    scale = ATTN_DIM ** -0.5
    qb = 2 if nb % 2 == 0 else 1
    steps = nb // qb

    def body(q_ref, kp_ref, kc_ref, vp_ref, vc_ref, o_ref, lse_ref):
        no_prev = jnp.where(pl.program_id(1) > 0, 0.0, NEG_BIG)
        m_prev, m_cur = _attn_masks()
        ones = jnp.ones((B, ATTN_DIM), BF16)
        items = []
        for j in range(qb):
            for h in range(ATTN_GROUP_HEADS):
                sl = slice(h * ATTN_DIM, (h + 1) * ATTN_DIM)
                rows = slice(j * B, (j + 1) * B)
                if j == 0:
                    items.append((rows, sl, kp_ref[:, sl], vp_ref[:, sl], no_prev))
                else:
                    before = slice((j - 1) * B, j * B)
                    items.append((rows, sl, kc_ref[before, sl], vc_ref[before, sl], 0.0))
        s_p = [jnp.where(m_prev, _dot(q_ref[rows, sl], k_p, NT) * scale + bias, NEG_BIG)
               for rows, sl, k_p, _, bias in items]
        s_c = [jnp.where(m_cur, _dot(q_ref[rows, sl], kc_ref[rows, sl], NT) * scale, NEG_BIG)
               for rows, sl, _, _, _ in items]
        m = [jnp.max(jnp.maximum(a, b), axis=-1, keepdims=True) for a, b in zip(s_p, s_c)]
        p_p = [jnp.exp(a - mx).astype(BF16) for a, mx in zip(s_p, m)]
        p_c = [jnp.exp(b - mx).astype(BF16) for b, mx in zip(s_c, m)]
        l = [_dot(a, ones, NN) + _dot(b, ones, NN) for a, b in zip(p_p, p_c)]
        acc = [_dot(a, v_p, NN) + _dot(b, vc_ref[rows, sl], NN)
               for a, b, (rows, sl, _, v_p, _) in zip(p_p, p_c, items)]
        for (rows, sl, _, _, _), a, lv, mx in zip(items, acc, l, m):
            o_ref[rows, sl] = (a / lv).astype(BF16)
            lse_ref[rows, sl] = mx + jnp.log(lv)

    cur = lambda col: pl.BlockSpec((qb * B, W), lambda s, n: (s * steps + n, col))
    prev = lambda col: pl.BlockSpec((B, W), lambda s, n: (s * nb + jnp.maximum(qb * n - 1, 0), col))
    out = pl.BlockSpec((qb * B, W), lambda s, n: (s * steps + n, 0))
    return pl.pallas_call(
        body, out_shape=(jax.ShapeDtypeStruct((T, W), BF16), jax.ShapeDtypeStruct((T, W), F32)),
        grid=(dilation, steps),
        in_specs=[cur(0), prev(1), cur(1), prev(2), cur(2)],
        out_specs=(out, out), compiler_params=_params("parallel", "arbitrary"), name=name)(qkv, qkv, qkv, qkv, qkv)


def _attn_bwd(qkv, d_out, lse, delta, cos, sin, dilation, name):
    T = qkv.shape[0]
    nb = T // dilation // ATTN_BLOCK
    assert nb % 2 == 0, "an even number of 128-token blocks per residue class"
    pairs = nb // 2
    W = ATTN_GROUP_WIDTH
    B = ATTN_BLOCK
    scale = ATTN_DIM ** -0.5

    def unrope(x, cos_v, sin_v):
        return x * cos_v + pltpu.roll(x * sin_v, ATTN_DIM // 2, 1)

    def body(qa_ref, qb_ref, kpair_ref, kc_ref, vpair_ref, vc_ref, doa_ref, dob_ref, lsea_ref, lseb_ref,
             dla_ref, dlb_ref, cos_ref, sin_ref, out_ref, dq_scr, dk_scr, dv_scr):
        n = pl.program_id(1)

        @pl.when(n == 0)
        def _():
            dq_scr[...] = jnp.zeros_like(dq_scr)
            dk_scr[...] = jnp.zeros_like(dk_scr)
            dv_scr[...] = jnp.zeros_like(dv_scr)

        no_a = jnp.where(n > 0, 0.0, NEG_BIG)
        no_b = jnp.where(n < pairs, 0.0, NEG_BIG)
        m_prev, m_cur = _attn_masks()
        lo, hi = slice(0, B), slice(B, 2 * B)
        heads = [slice(h * ATTN_DIM, (h + 1) * ATTN_DIM) for h in range(ATTN_GROUP_HEADS)]
        flat = []
        for sl in heads:
            qa, qb = qa_ref[:, sl], qb_ref[:, sl]
            doa, dob = doa_ref[:, sl], dob_ref[:, sl]
            k0, k1, k2 = kpair_ref[lo, sl], kpair_ref[hi, sl], kc_ref[:, sl]
            v0, v1, v2 = vpair_ref[lo, sl], vpair_ref[hi, sl], vc_ref[:, sl]
            flat += [(qa, doa, lsea_ref[:, sl], dla_ref[:, sl], k0, v0, m_prev, no_a),
                     (qa, doa, lsea_ref[:, sl], dla_ref[:, sl], k1, v1, m_cur, no_a),
                     (qb, dob, lseb_ref[:, sl], dlb_ref[:, sl], k1, v1, m_prev, no_a + no_b),
                     (qb, dob, lseb_ref[:, sl], dlb_ref[:, sl], k2, v2, m_cur, no_b)]
        s = [_dot(q, k, NT) for q, _, _, _, k, _, _, _ in flat]
        dp = [_dot(do, v, NT) for _, do, _, _, _, v, _, _ in flat]
        p = [jnp.where(mask, jnp.exp(sv * scale - lse_v + bias), 0.0)
             for sv, (_, _, lse_v, _, _, _, mask, bias) in zip(s, flat)]
        ds = [(pv * (dpv - dl_v) * scale).astype(BF16) for pv, dpv, (_, _, _, dl_v, _, _, _, _) in zip(p, dp, flat)]
        p = [pv.astype(BF16) for pv in p]
        dq_part = [_dot(dsv, k, NN) for dsv, (_, _, _, _, k, _, _, _) in zip(ds, flat)]
        dk_part = [_dot(dsv, q, TN) for dsv, (q, _, _, _, _, _, _, _) in zip(ds, flat)]
        dv_part = [_dot(pv, do, TN) for pv, (_, do, _, _, _, _, _, _) in zip(p, flat)]
        cos_lo, sin_lo, cos_hi, sin_hi = cos_ref[lo, :], sin_ref[lo, :], cos_ref[hi, :], sin_ref[hi, :]
        for h, sl in enumerate(heads):
            a_prev, a_cur, b_prev, b_cur = range(4 * h, 4 * h + 4)
            kcol = slice(W + h * ATTN_DIM, W + (h + 1) * ATTN_DIM)
            vcol = slice(2 * W + h * ATTN_DIM, 2 * W + (h + 1) * ATTN_DIM)
            out_ref[lo, sl] = unrope(dq_scr[:, sl], cos_lo, sin_lo).astype(BF16)
            out_ref[hi, sl] = unrope(dq_part[a_prev] + dq_part[a_cur], cos_hi, sin_hi).astype(BF16)
            out_ref[lo, kcol] = unrope(dk_scr[:, sl] + dk_part[a_prev], cos_lo, sin_lo).astype(BF16)
            out_ref[hi, kcol] = unrope(dk_part[a_cur] + dk_part[b_prev], cos_hi, sin_hi).astype(BF16)
            out_ref[lo, vcol] = (dv_scr[:, sl] + dv_part[a_prev]).astype(BF16)
            out_ref[hi, vcol] = (dv_part[a_cur] + dv_part[b_prev]).astype(BF16)
            dq_scr[:, sl] = dq_part[b_prev] + dq_part[b_cur]
            dk_scr[:, sl] = dk_part[b_cur]
            dv_scr[:, sl] = dv_part[b_cur]

    def block_a(n):
        return jnp.maximum(2 * n - 1, 0)

    def block_b(n):
        return jnp.minimum(2 * n, nb - 1)

    def pair(n):
        return jnp.maximum(n - 1, 0)

    one_a = lambda col: pl.BlockSpec((B, W), lambda s, n: (s * nb + block_a(n), col))
    one_b = lambda col: pl.BlockSpec((B, W), lambda s, n: (s * nb + block_b(n), col))
    two = lambda col: pl.BlockSpec((2 * B, W), lambda s, n: (s * pairs + pair(n), col))
    tab = pl.BlockSpec((2 * B, ATTN_DIM), lambda s, n: (s * pairs + pair(n), 0))
    return pl.pallas_call(
        body, out_shape=jax.ShapeDtypeStruct((T, 3 * W), BF16), grid=(dilation, pairs + 1),
        in_specs=[one_a(0), one_b(0), two(1), one_b(1), two(2), one_b(2), one_a(0), one_b(0), one_a(0), one_b(0),
                  one_a(0), one_b(0), tab, tab],
        out_specs=pl.BlockSpec((2 * B, 3 * W), lambda s, n: (s * pairs + pair(n), 0)),
        scratch_shapes=[pltpu.VMEM((B, W), F32)] * 3,
        compiler_params=_params("parallel", "arbitrary"), name=name)(
            qkv, qkv, qkv, qkv, qkv, qkv, d_out, d_out, lse, lse, delta, delta, cos, sin)


PERM_TILE = 512
LANES = 128


def _residue_view(x, d):
    return x if d == 1 else x.reshape(d, x.shape[0] // d, x.shape[1])


def _residue_spec(d, tm, cols):
    if d == 1:
        return pl.BlockSpec((tm, cols), lambda i: (i, 0))
    return pl.BlockSpec((d, tm // d, cols), lambda i: (0, i, 0))


def _residue_shape(T, d, cols, dtype):
    return jax.ShapeDtypeStruct((T, cols) if d == 1 else (d, T // d, cols), dtype)


def _class_rows(r, d, tm):
    return pl.ds(r, tm // d, stride=d)


def _attn_norm(h, gain, name):
    T = h.shape[0]
    tm = _pick_tile(T, PERM_TILE, 16 * max(ATTN_DILATIONS))
    dils = ATTN_DILATIONS
    (base_cos, base_sin), (off_cos, off_sin), sign = _rope_parts(T, tm)

    def body(h_ref, g_ref, bc_ref, bs_ref, oc_ref, os_ref, sign_ref, *refs):
        u_refs, c_refs, s_refs, u_scr, c_scr, s_scr = refs[0:3], refs[3:6], refs[6:9], refs[9], refs[10], refs[11]
        hv = h_ref[...]
        rstd = lax.rsqrt(jnp.mean(hv * hv, axis=-1, keepdims=True) + NORM_EPS)
        u = hv * rstd * g_ref[...]
        for j in range(D_MODEL // LANES):
            u_scr[j] = u[:, j * LANES:(j + 1) * LANES]
        bc, bs, oc, osn = bc_ref[0], bs_ref[0], oc_ref[...], os_ref[...]
        c_scr[...] = bc * oc - bs * osn
        s_scr[...] = (bs * oc + bc * osn) * sign_ref[...]
        for d, u_ref, c_ref, s_ref in zip(dils, u_refs, c_refs, s_refs):
            if d == 1:
                u_ref[...] = u.astype(BF16)
                c_ref[...] = c_scr[...]
                s_ref[...] = s_scr[...]
                continue
            for r in range(d):
                rows = _class_rows(r, d, tm)
                for j in range(D_MODEL // LANES):
                    u_ref[r, :, j * LANES:(j + 1) * LANES] = u_scr.at[j][rows, :].astype(BF16)
                c_ref[r] = c_scr[rows, :]
                s_ref[r] = s_scr[rows, :]

    row = pl.BlockSpec((tm, D_MODEL), lambda i: (i, 0))
    base = pl.BlockSpec((1, 1, ATTN_DIM), lambda i: (i, 0, 0))
    off = pl.BlockSpec((tm, ATTN_DIM), lambda i: (0, 0))
    res = pl.pallas_call(
        body,
        out_shape=([_residue_shape(T, d, D_MODEL, BF16) for d in dils]
                   + [_residue_shape(T, d, ATTN_DIM, F32) for d in dils] * 2),
        grid=(T // tm,),
        in_specs=[row, pl.BlockSpec((1, D_MODEL), lambda i: (0, 0)), base, base, off, off,
                  pl.BlockSpec((1, ATTN_DIM), lambda i: (0, 0))],
        out_specs=([_residue_spec(d, tm, D_MODEL) for d in dils] + [_residue_spec(d, tm, ATTN_DIM) for d in dils] * 2),
        scratch_shapes=[pltpu.VMEM((D_MODEL // LANES, tm, LANES), F32), pltpu.VMEM((tm, ATTN_DIM), F32),
                        pltpu.VMEM((tm, ATTN_DIM), F32)],
        compiler_params=_params("parallel"), name=name)(h, gain, base_cos, base_sin, off_cos, off_sin, sign)
    flat = [r.reshape(T, r.shape[-1]) for r in res]
    return flat[0:3], flat[3:6], flat[6:9]


def _attn_merge_fwd(outs, lses, name):
    T = outs[0].shape[0]
    W = ATTN_GROUP_WIDTH
    tm = _pick_tile(T, PERM_TILE, 16 * max(ATTN_DILATIONS))
    dils = ATTN_DILATIONS

    def body(*refs):
        o_refs, l_refs, oc_ref, lse_refs = refs[0:3], refs[3:6], refs[6], refs[7:10]
        o_scr, l_scr, t_scr = refs[10:13]
        nh = ATTN_GROUP_HEADS
        for g, d in enumerate(dils):
            for j in range(nh):
                lanes = slice(j * LANES, (j + 1) * LANES)
                if d == 1:
                    o_scr[g * nh + j] = o_refs[g][:, lanes].astype(F32)
                    l_scr[g * nh + j] = l_refs[g][:, lanes]
                    continue
                for r in range(d):
                    rows = _class_rows(r, d, tm)
                    o_scr.at[g * nh + j][rows, :] = o_refs[g][r, :, lanes].astype(F32)
                    l_scr.at[g * nh + j][rows, :] = l_refs[g][r, :, lanes]
        for j in range(nh):
            lanes = slice(j * LANES, (j + 1) * LANES)
            ls = [l_scr[g * nh + j] for g in range(3)]
            m = jnp.maximum(jnp.maximum(ls[0], ls[1]), ls[2])
            tot = m + jnp.log(jnp.exp(ls[0] - m) + jnp.exp(ls[1] - m) + jnp.exp(ls[2] - m))
            t_scr[j] = tot
            for g, d in enumerate(dils):
                oc_ref[:, g * W + j * LANES:g * W + (j + 1) * LANES] = (
                    o_scr[g * nh + j] * jnp.exp(ls[g] - tot)).astype(BF16)
                if d == 1:
                    lse_refs[g][:, lanes] = tot
                    continue
                for r in range(d):
                    lse_refs[g][r, :, lanes] = t_scr.at[j][_class_rows(r, d, tm), :]

    in_blk = [_residue_spec(d, tm, W) for d in dils]
    n_blk = 3 * ATTN_GROUP_HEADS
    res = pl.pallas_call(
        body, out_shape=[jax.ShapeDtypeStruct((T, 3 * W), BF16)] + [_residue_shape(T, d, W, F32) for d in dils],
        grid=(T // tm,), in_specs=in_blk * 2,
        out_specs=[pl.BlockSpec((tm, 3 * W), lambda i: (i, 0))] + in_blk,
        scratch_shapes=[pltpu.VMEM((n_blk, tm, LANES), F32), pltpu.VMEM((n_blk, tm, LANES), F32),
                        pltpu.VMEM((ATTN_GROUP_HEADS, tm, LANES), F32)],
        compiler_params=_params("parallel"), name=name)(
            *[_residue_view(o, d) for o, d in zip(outs, dils)], *[_residue_view(l, d) for l, d in zip(lses, dils)])
    return res[0], [r.reshape(T, W) for r in res[1:]]


def _attn_merge_bwd(d_oc, oc, name):
    T = d_oc.shape[0]
    W = ATTN_GROUP_WIDTH
    tm = _pick_tile(T, PERM_TILE, 16 * max(ATTN_DILATIONS))
    dils = ATTN_DILATIONS

    def body(d_ref, o_ref, *refs):
        delta_refs, db_refs, dl_scr, d_scr = refs[0:3], refs[3:6], refs[6], refs[7]
        nh = ATTN_GROUP_HEADS
        for j in range(nh):
            tot = jnp.zeros((tm, 1), F32)
            for g in range(3):
                cols = slice(g * W + j * LANES, g * W + (j + 1) * LANES)
                d_blk = d_ref[:, cols]
                d_scr[g * nh + j] = d_blk
                tot = tot + jnp.sum(d_blk * o_ref[:, cols].astype(F32), axis=-1, keepdims=True)
            dl_scr[j] = jnp.broadcast_to(tot, (tm, LANES))
        for g, d in enumerate(dils):
            for j in range(nh):
                lanes = slice(j * LANES, (j + 1) * LANES)
                if d == 1:
                    delta_refs[g][:, lanes] = dl_scr[j]
                    db_refs[g][:, lanes] = d_scr[g * nh + j].astype(BF16)
                    continue
                for r in range(d):
                    rows = _class_rows(r, d, tm)
                    delta_refs[g][r, :, lanes] = dl_scr.at[j][rows, :]
                    db_refs[g][r, :, lanes] = d_scr.at[g * nh + j][rows, :].astype(BF16)

    wide = pl.BlockSpec((tm, 3 * W), lambda i: (i, 0))
    out_blk = [_residue_spec(d, tm, W) for d in dils]
    res = pl.pallas_call(
        body, out_shape=[_residue_shape(T, d, W, F32) for d in dils] + [_residue_shape(T, d, W, BF16) for d in dils],
        grid=(T // tm,), in_specs=[wide, wide], out_specs=out_blk * 2,
        scratch_shapes=[pltpu.VMEM((ATTN_GROUP_HEADS, tm, LANES), F32),
                        pltpu.VMEM((3 * ATTN_GROUP_HEADS, tm, LANES), F32)],
        compiler_params=_params("parallel"), name=name)(d_oc, oc)
    flat = [r.reshape(T, W) for r in res]
    return flat[0:3], flat[3:6]


def _rope_parts(T, tile):
    inv_freq = 1.0 / (ROPE_THETA ** (jnp.arange(0, ATTN_DIM, 2, dtype=F32) / ATTN_DIM))
    inv_freq = jnp.concatenate([inv_freq, inv_freq])[None, :]
    base = (jnp.arange(T // tile, dtype=F32) * tile)[:, None] * inv_freq
    off = jnp.arange(tile, dtype=F32)[:, None] * inv_freq
    sign = jnp.concatenate([-jnp.ones((1, ATTN_DIM // 2), F32), jnp.ones((1, ATTN_DIM // 2), F32)], axis=1)
    return (jnp.cos(base)[:, None, :], jnp.sin(base)[:, None, :]), (jnp.cos(off), jnp.sin(off)), sign


WEIGHT_GROUPS = {"hgrn": ("hgrn_in", "hgrn_out"), "ffn0": ("ffn_in0", "ffn_down0"),
                 "attn": ("qkv", "attn_out"), "ffn1": ("ffn_in1", "ffn_down1")}


def _local_step(x, target, norm_mix, norm_ffn, lb, out_gain, final_gain, fetch, publish):
    g_mix = [norm_mix[0:1], norm_mix[1:2]]
    g_ffn = [norm_ffn[0:1], norm_ffn[1:2]]
    w = {}

    def whole(name):
        return [(w[name], w[name].shape[0], 0)]

    def qkv_parts(g):
        return [(w["qkv"], ATTN_GROUP_WIDTH, 3 * j + g) for j in range(3)]

    def ffn_fwd(h, layer, head=None):
        w.update(fetch(f"ffn{layer}"))
        n, gate, up, a = _ffn_in(h, g_ffn[layer], w[f"ffn_in{layer}"], f"ffn{layer}_in")
        out = _mm_nn([a], [whole(f"ffn_down{layer}")], h, name=f"ffn{layer}_down", head=head)
        return out, (n, gate, up, a)

    def ffn_bwd(h, saved, dh, dhb, layer):
        n, gate, up, a = saved
        w_in = w[f"ffn_in{layer}"]
        dgate, dup = _ffn_down_dx(dhb, w[f"ffn_down{layer}"], gate, up, f"ffn{layer}_down_dx")
        grads = {f"ffn_down{layer}": _mm_tn([a], dhb, name=f"ffn{layer}_down_dw"),
                 f"ffn_in{layer}": _mm_tn([dgate, dup], n, name=f"ffn{layer}_in_dw")}
        publish(f"ffn{layer}", grads)
        return _mm_nn([dgate, dup], [[(w_in, D_FF, 0)], [(w_in, D_FF, 1)]], dh, name=f"ffn{layer}_in_dx",
                      norm=(h, g_ffn[layer]))

    u0 = _rms_fwd(x, g_mix[0], "hgrn_norm")
    w.update(fetch("hgrn"))
    proj = _mm_nt(u0, whole("hgrn_in"), out_dtype=F32, name="hgrn_in")
    og, o_pre, states = _hgrn_fwd(proj, lb, out_gain, "hgrn_fwd")
    h1 = _mm_nn([og], [whole("hgrn_out")], x, name="hgrn_out")
    h2, ffn0 = ffn_fwd(h1, 0)

    u1_g, cos_g, sin_g = _attn_norm(h2, g_mix[1], "attn_norm")
    w.update(fetch("attn"))
    qkv_g, outs, lses = [], [], []
    for g, d in enumerate(ATTN_DILATIONS):
        qkv_g.append(_mm_nt(u1_g[g], qkv_parts(g), out_dtype=BF16, name=f"attn_qkv{g}",
                            rope=(cos_g[g], sin_g[g], 2)))
        o_g, lse_g = _attn_fwd(qkv_g[g], d, f"attn_fwd{g}")
        outs.append(o_g)
        lses.append(lse_g)
    oc, lse_all = _attn_merge_fwd(outs, lses, "attn_merge")
    h3 = _mm_nn([oc], [whole("attn_out")], h2, name="attn_out")
    (dh4, dh4b, d_final, loss_part), ffn1 = ffn_fwd(h3, 1, head=(target, final_gain))
    dh3, dh3b, d_ffn1 = ffn_bwd(h3, ffn1, dh4, dh4b, 1)

    d_oc = _mm_nt(dh3b, whole("attn_out"), out_dtype=F32, name="attn_out_dx")
    grad_attn_out = _mm_tn([oc], dh3b, name="attn_out_dw")
    delta, d_ocb = _attn_merge_bwd(d_oc, oc, "attn_merge_bwd")
    du1, qkv_pieces = [], []
    for g, d in enumerate(ATTN_DILATIONS):
        dqkv = _attn_bwd(qkv_g[g], d_ocb[g], lse_all[g], delta[g], cos_g[g], sin_g[g], d, f"attn_bwd{g}")
        qkv_pieces.append(_mm_tn([dqkv], u1_g[g], name=f"attn_qkv_dw{g}"))
        du1.append(_mm_nn([dqkv], [qkv_parts(g)], None, name=f"attn_qkv_dx{g}"))
    grad_qkv = jnp.stack([p.reshape(3, ATTN_GROUP_WIDTH, D_MODEL) for p in qkv_pieces], axis=1).reshape(
        3 * ATTN_WIDTH, D_MODEL)
    publish("attn", {"qkv": grad_qkv, "attn_out": grad_attn_out})
    dh2, dh2b, d_mix1 = _rms_bwd(h2, g_mix[1], du1, dh3, "attn_norm_bwd", ATTN_DILATIONS)

    dh1, dh1b, d_ffn0 = ffn_bwd(h1, ffn0, dh2, dh2b, 0)

    d_og = _mm_nt(dh1b, whole("hgrn_out"), out_dtype=F32, name="hgrn_out_dx")
    grad_hgrn_out = _mm_tn([og], dh1b, name="hgrn_out_dw")
    dproj, d_lb, d_out_gain = _hgrn_bwd(proj, o_pre, d_og, states, lb, out_gain, "hgrn_bwd")
    publish("hgrn", {"hgrn_in": _mm_tn([dproj], u0, name="hgrn_in_dw"), "hgrn_out": grad_hgrn_out})
    dx, _, d_mix0 = _mm_nn([dproj], [whole("hgrn_in")], dh1, name="hgrn_in_dx", norm=(x, g_mix[0]))

    small = dict(norm_mix0=d_mix0, norm_mix1=d_mix1, norm_ffn0=d_ffn0, norm_ffn1=d_ffn1, lb=d_lb,
                 out_gain=d_out_gain, final=d_final, loss=loss_part)
    return dx, small


MESH_IDS = pl.DeviceIdType.MESH
HBM_SPEC = pl.BlockSpec(memory_space=pl.ANY)


N_PEERS = N_DEV - 1
PEER_OFFSETS = [(dx, dy, dc) for dx in (0, 1) for dy in (0, 1) for dc in (0, 1)][1:]


def _mesh_place():
    x, y, c = lax.axis_index("x"), lax.axis_index("y"), lax.axis_index("c")
    peers = []
    for dx, dy, dc in PEER_OFFSETS:
        px, py, pc = (1 - x if dx else x), (1 - y if dy else y), (1 - c if dc else c)
        peers.append(((px, py, pc), 4 * px + 2 * py + pc))
    return 4 * x + 2 * y + c, peers


def _gather_over_two_levels(src_refs, land_refs, send_sems, recv_sems):
    n = len(src_refs)
    x, y, c = lax.axis_index("x"), lax.axis_index("y"), lax.axis_index("c")
    me, sibling = (x, y, c), (x, y, 1 - c)
    chips = [(1 - x, y), (x, 1 - y), (1 - x, 1 - y)]

    def block(w, px, py, pc):
        return land_refs[w].at[4 * px + 2 * py + pc]

    def copy(w, k, owner, to, src=None):
        return pltpu.make_async_remote_copy(
            src_ref=block(w, *owner) if src is None else src, dst_ref=block(w, *owner),
            send_sem=send_sems.at[w * N_PEERS + k], recv_sem=recv_sems.at[w * N_PEERS + k],
            device_id=to, device_id_type=MESH_IDS)

    sent = []
    for w in range(n):
        sent.append(copy(w, 0, me, sibling, src=src_refs[w]))
        sent += [copy(w, 1 + j, me, (*chip, c), src=src_refs[w]) for j, chip in enumerate(chips)]
    for cp in sent:
        cp.start()
    for w in range(n):
        for j, chip in enumerate(chips):
            copy(w, 1 + j, (*chip, c), me).wait_recv()
            passed = copy(w, 4 + j, (*chip, c), sibling)
            passed.start()
            sent.append(passed)
    for w in range(n):
        copy(w, 0, sibling, me).wait_recv()
        for j, chip in enumerate(chips):
            copy(w, 4 + j, (*chip, 1 - c), me).wait_recv()
    for cp in sent:
        cp.wait_send()


def _exchange_launch(srcs, scatter, collective_id, name):
    n = len(srcs)
    src_refs = [jax.new_ref(s, memory_space=pltpu.MemorySpace.HBM) for s in srcs]
    land_refs = [jax.empty_ref(jax.ShapeDtypeStruct(s.shape if scatter else (N_DEV,) + s.shape, s.dtype),
                               memory_space=pltpu.MemorySpace.HBM) for s in srcs]

    @pl.kernel(mesh=plsc.ScalarSubcoreMesh(axis_name="sequencer", num_cores=1), name=name,
               scratch_types=(pltpu.SemaphoreType.DMA((n * N_PEERS,)), pltpu.SemaphoreType.DMA((n * N_PEERS,)),
                              pltpu.SemaphoreType.DMA((n,))),
               compiler_params=pltpu.CompilerParams(collective_id=collective_id))
    def launch(send_sems, recv_sems, local_sems):
        me, peers = _mesh_place()
        barrier = pltpu.get_barrier_semaphore()
        for peer, _ in peers:
            pl.semaphore_signal(barrier, inc=1, device_id=peer, device_id_type=MESH_IDS)
        pl.semaphore_wait(barrier, N_PEERS)
        own = [pltpu.make_async_copy(src_refs[w].at[me] if scatter else src_refs[w], land_refs[w].at[me],
                                     local_sems.at[w]) for w in range(n)]
        for cp in own:
            cp.start()
        if scatter:
            copies = [pltpu.make_async_remote_copy(
                src_ref=src_refs[w].at[pid], dst_ref=land_refs[w].at[me],
                send_sem=send_sems.at[w * N_PEERS + k], recv_sem=recv_sems.at[w * N_PEERS + k],
                device_id=peer, device_id_type=MESH_IDS) for w in range(n) for k, (peer, pid) in enumerate(peers)]
            for cp in copies:
                cp.start()
            for cp in copies:
                cp.wait()
        else:
            _gather_over_two_levels(src_refs, land_refs, send_sems, recv_sems)
        for cp in own:
            cp.wait()

    launch()
    return land_refs


def _gather_small(block, name):
    def body(in_ref, out_ref, send_sems, recv_sems, local_sem):
        me, peers = _mesh_place()
        own = pltpu.make_async_copy(in_ref, out_ref.at[me], local_sem)
        own.start()
        sends = [pltpu.make_async_remote_copy(
            src_ref=in_ref, dst_ref=out_ref.at[me], send_sem=send_sems.at[k], recv_sem=recv_sems.at[k],
            device_id=peer, device_id_type=MESH_IDS) for k, (peer, _) in enumerate(peers)]
        for cp in sends:
            cp.start()
        for cp in sends:
            cp.wait_recv()
        for cp in sends:
            cp.wait_send()
        own.wait()

    return pl.pallas_call(
        body, out_shape=jax.ShapeDtypeStruct((N_DEV,) + block.shape, block.dtype),
        in_specs=[HBM_SPEC], out_specs=HBM_SPEC,
        scratch_shapes=[pltpu.SemaphoreType.DMA((N_PEERS,)), pltpu.SemaphoreType.DMA((N_PEERS,)),
                        pltpu.SemaphoreType.DMA],
        name=name)(block)


def _sum_blocks(recv, name):
    rows = recv.shape[1]
    tr = _pick_tile(rows, 256, 16)

    def body(r_ref, g_ref):
        acc = r_ref[0].astype(F32)
        for j in range(1, N_DEV):
            acc = acc + r_ref[j].astype(F32)
        g_ref[...] = acc

    return pl.pallas_call(
        body, out_shape=jax.ShapeDtypeStruct((rows, D_MODEL), F32), grid=(rows // tr,),
        in_specs=[pl.BlockSpec((N_DEV, tr, D_MODEL), lambda i: (0, i, 0))],
        out_specs=pl.BlockSpec((tr, D_MODEL), lambda i: (i, 0)),
        compiler_params=_params("parallel"), name=name)(recv)


def _adamw_math(w, g, m, v):
    m_new = ADAM_B1 * m + (1.0 - ADAM_B1) * g
    v_new = ADAM_B2 * v + (1.0 - ADAM_B2) * (g * g)
    m_hat = m_new / (1.0 - ADAM_B1 ** ADAM_STEP)
    v_hat = v_new / (1.0 - ADAM_B2 ** ADAM_STEP)
    delta = -ADAM_LR * (m_hat / (jnp.sqrt(v_hat) + ADAM_EPS) + ADAM_WD * w)
    return delta, m_new, v_new


def _adamw(w, g, m, v, layer, others, name):
    _, rows, cols = w.shape
    tr = _pick_tile(rows, 256, 8)

    def body(w_ref, g_ref, m_ref, v_ref, *refs):
        go_ref, d_ref, mo_ref, vo_ref = refs[-4:]
        gv = g_ref[...]
        go_ref[...] = gv
        d_ref[...], mo_ref[...], vo_ref[...] = _adamw_math(w_ref[...], gv, m_ref[...], v_ref[...])

    one = pl.BlockSpec((None, tr, cols), lambda i: (layer, i, 0))
    in_specs = [one, pl.BlockSpec((tr, cols), lambda i: (i, 0)), one, one]
    args = [w, g, m, v]
    if others is not None:
        in_specs += [HBM_SPEC] * 4
        args += list(others)
    return pl.pallas_call(
        body, out_shape=(jax.ShapeDtypeStruct(w.shape, F32),) * 4, grid=(rows // tr,),
        in_specs=in_specs, out_specs=(one,) * 4,
        input_output_aliases={} if others is None else {4 + i: i for i in range(4)},
        compiler_params=_params("parallel"), name=name)(*args)


ROW_MIX, ROW_FFN, ROW_LB, ROW_OUT_GAIN, ROW_FINAL = 0, 2, 4, 7, 8
PART_MIX, PART_FFN, PART_LB, PART_OUT_GAIN, PART_FINAL, PART_LOSS = 0, 2, 4, 5, 6, 7


def _small_update(parts_all, w, m, v, name):
    def body(p_ref, w_ref, m_ref, v_ref, g_ref, d_ref, mo_ref, vo_ref, loss_ref):
        def total(row, n=1):
            tot = p_ref[0, row:row + n, :]
            for j in range(1, N_DEV):
                tot = tot + p_ref[j, row:row + n, :]
            return tot

        logits = [w_ref[ROW_LB + i:ROW_LB + i + 1, :] for i in range(3)]
        mx = jnp.maximum(jnp.maximum(logits[0], logits[1]), logits[2])
        ex = [jnp.exp(l - mx) for l in logits]
        den = ex[0] + ex[1] + ex[2]
        prob = [e / den for e in ex]
        d_lb = total(PART_LB)
        g_ref[...] = jnp.zeros_like(g_ref)
        g_ref[ROW_MIX:ROW_MIX + 2, :] = total(PART_MIX, 2)
        g_ref[ROW_FFN:ROW_FFN + 2, :] = total(PART_FFN, 2)
        for i in range(3):
            g_ref[ROW_LB + i:ROW_LB + i + 1, :] = prob[i] * ((d_lb if i == 0 else 0.0) - prob[0] * d_lb)
        g_ref[ROW_OUT_GAIN:ROW_OUT_GAIN + 1, :] = total(PART_OUT_GAIN)
        g_ref[ROW_FINAL:ROW_FINAL + 1, :] = total(PART_FINAL)
        d_ref[...], mo_ref[...], vo_ref[...] = _adamw_math(w_ref[...], g_ref[...], m_ref[...], v_ref[...])
        loss_ref[...] = jnp.sum(total(PART_LOSS), axis=-1, keepdims=True)

    packed = jax.ShapeDtypeStruct((16, D_MODEL), F32)
    return pl.pallas_call(
        body, out_shape=(packed, packed, packed, packed, jax.ShapeDtypeStruct((1, 1), F32)),
        compiler_params=pltpu.CompilerParams(vmem_limit_bytes=VMEM_LIMIT), name=name)(parts_all, w, m, v)


def _pack_small(norm_mix, norm_ffn, lb_logits, out_gain, final):
    pad = jnp.zeros((1, D_MODEL - HGRN_DIM), F32)
    return jnp.concatenate([norm_mix, norm_ffn, lb_logits, jnp.concatenate([out_gain, pad], axis=1),
                            final.reshape(1, D_MODEL), jnp.zeros((16 - ROW_FINAL - 1, D_MODEL), F32)], axis=0)


def _unpack_small(p):
    return (p[ROW_MIX:ROW_MIX + 2], p[ROW_FFN:ROW_FFN + 2], p[ROW_LB:ROW_LB + 3],
            p[ROW_OUT_GAIN:ROW_OUT_GAIN + 1, :HGRN_DIM], p[ROW_FINAL])


def _lower_bound(lb_logits, name):
    def body(l_ref, o_ref):
        logits = [l_ref[i:i + 1, :] for i in range(3)]
        mx = jnp.maximum(jnp.maximum(logits[0], logits[1]), logits[2])
        ex = [jnp.exp(l - mx) for l in logits]
        o_ref[...] = ex[0] / (ex[0] + ex[1] + ex[2])

    return pl.pallas_call(body, out_shape=jax.ShapeDtypeStruct((1, D_MODEL), F32), name=name)(lb_logits)


def kernel(x, norm_mix, norm_ffn, hgrn_w_in, hgrn_lb_logits, hgrn_out_norm, hgrn_w_out, attn_w_qkv, attn_w_out, ffn_w_in, ffn_w_down, final_norm, loss_target, m_norm_mix, m_norm_ffn, m_hgrn_w_in, m_hgrn_lb_logits, m_hgrn_out_norm, m_hgrn_w_out, m_attn_w_qkv, m_attn_w_out, m_ffn_w_in, m_ffn_w_down, m_final_norm, v_norm_mix, v_norm_ffn, v_hgrn_w_in, v_hgrn_lb_logits, v_hgrn_out_norm, v_hgrn_w_out, v_attn_w_qkv, v_attn_w_out, v_ffn_w_in, v_ffn_w_down, v_final_norm):
    col_sharded = {"hgrn_in": hgrn_w_in[0], "qkv": attn_w_qkv[0], "ffn_in0": ffn_w_in[0], "ffn_in1": ffn_w_in[1]}
    row_sharded = {"hgrn_out": hgrn_w_out[0], "attn_out": attn_w_out[0], "ffn_down0": ffn_w_down[0],
                   "ffn_down1": ffn_w_down[1]}
    gathering = {}
    for gi, (group, names) in enumerate(WEIGHT_GROUPS.items()):
        shards = [(col_sharded[n].T if n in col_sharded else row_sharded[n]).astype(BF16) for n in names]
        gathering[group] = _exchange_launch(shards, False, 1 + gi, f"weights_gather_{group}")

    def fetch(group):
        return {n: land[...].reshape(-1, D_MODEL) for n, land in zip(WEIGHT_GROUPS[group], gathering[group])}

    in_flight = {}

    def publish(group, grads):
        names = WEIGHT_GROUPS[group]
        parts = [grads[n].reshape(N_DEV, -1, D_MODEL) for n in names]
        in_flight[group] = _exchange_launch(parts, True, 1 + len(WEIGHT_GROUPS) + list(WEIGHT_GROUPS).index(group),
                                            f"grads_send_{group}")

    lb = _lower_bound(hgrn_lb_logits, "hgrn_lower_bound")
    grad_x, small = _local_step(x[0], loss_target[0], norm_mix, norm_ffn, lb, hgrn_out_norm,
                                final_norm.reshape(1, D_MODEL), fetch, publish)

    pad = jnp.zeros((1, D_MODEL - HGRN_DIM), F32)
    small_part = jnp.concatenate(
        [small["norm_mix0"], small["norm_mix1"], small["norm_ffn0"], small["norm_ffn1"], small["lb"],
         jnp.concatenate([small["out_gain"], pad], axis=1), small["final"], small["loss"]], axis=0)
    small_all = _gather_small(small_part, "small_grads_gather")
    received = {}
    for group in ("ffn1", "attn", "ffn0", "hgrn"):
        received.update(zip(WEIGHT_GROUPS[group], [land[...] for land in in_flight[group]]))

    masters = {"hgrn_w_in": (hgrn_w_in, m_hgrn_w_in, v_hgrn_w_in, ("hgrn_in",)),
               "hgrn_w_out": (hgrn_w_out, m_hgrn_w_out, v_hgrn_w_out, ("hgrn_out",)),
               "attn_w_qkv": (attn_w_qkv, m_attn_w_qkv, v_attn_w_qkv, ("qkv",)),
               "attn_w_out": (attn_w_out, m_attn_w_out, v_attn_w_out, ("attn_out",)),
               "ffn_w_in": (ffn_w_in, m_ffn_w_in, v_ffn_w_in, ("ffn_in0", "ffn_in1")),
               "ffn_w_down": (ffn_w_down, m_ffn_w_down, v_ffn_w_down, ("ffn_down0", "ffn_down1"))}
    big = {}
    for param, (wv, mv, vv, names) in masters.items():
        outs = None
        for layer, n in enumerate(names):
            g = _sum_blocks(received[n], f"{n}_grad_sum")
            outs = _adamw(wv, g.T if n in col_sharded else g, mv, vv, layer, outs, f"{n}_adamw")
        big[param] = list(outs)

    w_small = _pack_small(norm_mix, norm_ffn, hgrn_lb_logits, hgrn_out_norm, final_norm)
    m_small = _pack_small(m_norm_mix, m_norm_ffn, m_hgrn_lb_logits, m_hgrn_out_norm, m_final_norm)
    v_small = _pack_small(v_norm_mix, v_norm_ffn, v_hgrn_lb_logits, v_hgrn_out_norm, v_final_norm)
    g_s, d_s, m_s, v_s, loss = _small_update(small_all, w_small, m_small, v_small, "small_update")
    small_out = [_unpack_small(t) for t in (g_s, d_s, m_s, v_s)]

    def group(i):
        s = small_out[i]
        return (s[0], s[1], big["hgrn_w_in"][i], s[2], s[3], big["hgrn_w_out"][i], big["attn_w_qkv"][i],
                big["attn_w_out"][i], big["ffn_w_in"][i], big["ffn_w_down"][i], s[4])

    return (loss.reshape(()), grad_x[None], *group(0), *group(1), *group(2), *group(3))
```

```python
import functools

import jax
import jax.numpy as jnp
from jax import lax
from jax.experimental import pallas as pl
from jax.experimental.pallas import tpu as pltpu
from jax.experimental.pallas import tpu_sc as plsc

F32 = jnp.float32
BF16 = jnp.bfloat16

D_MODEL = 1024
N_DEV = 8
NORM_EPS = 1e-6

HGRN_HEADS = 8
HGRN_DIM = 128
HGRN_CHUNK = 64
HGRN_STEP_CHUNKS = 4
HGRN_EXP_CLAMP = 60.0

ATTN_DIM = 128
ATTN_BLOCK = 128
ATTN_GROUP_HEADS = 4
ATTN_GROUP_WIDTH = ATTN_GROUP_HEADS * ATTN_DIM
ATTN_DILATIONS = (1, 4, 16)
ATTN_WIDTH = 3 * ATTN_GROUP_WIDTH
ROPE_THETA = 10000.0
NEG_BIG = -1e30

D_FF = 2816

ADAM_LR = 0.001
ADAM_B1 = 0.9
ADAM_B2 = 0.999
ADAM_EPS = 1e-08
ADAM_WD = 0.01
ADAM_STEP = 10

VMEM_LIMIT = 48 * 1024 * 1024

NT = (((1,), (1,)), ((), ()))
NN = (((1,), (0,)), ((), ()))
TN = (((0,), (0,)), ((), ()))


def _dot(a, b, dims):
    return lax.dot_general(a, b, dims, preferred_element_type=F32)


def _params(*sem):
    return pltpu.CompilerParams(dimension_semantics=sem, vmem_limit_bytes=VMEM_LIMIT)


def _pick_tile(n, cap, mult):
    best = None
    for t in range(mult, min(n, cap) + 1, mult):
        if n % t == 0:
            best = t
    assert best is not None, (n, cap, mult)
    return best


def _sigmoid(x):
    return 0.5 * jnp.tanh(0.5 * x) + 0.5


ROW_TILE = 512
COL_CHUNK = 512
GRAD_TILE = 256


def _whole(shape, index_map):
    return pl.BlockSpec(shape, index_map, pipeline_mode=pl.Buffered(1))


def _part_specs(parts, n_cols):
    return [_whole((rows, n_cols), functools.partial(lambda i, b: (b, 0), b=blk)) for _, rows, blk in parts]


def _mm_nt(a, w_parts, *, out_dtype, name, rope=None):
    M, K = a.shape
    tm = _pick_tile(M, ROW_TILE, 16)
    widths = [rows for _, rows, _ in w_parts]
    n_parts = len(w_parts)

    def body(*refs):
        a_ref, w_refs, o_ref = refs[0], refs[1:1 + n_parts], refs[-1]
        av = a_ref[...]
        off = 0
        for p, w_ref in enumerate(w_refs):
            for c0 in range(0, widths[p], COL_CHUNK):
                cw = min(COL_CHUNK, widths[p] - c0)
                acc = _dot(av, w_ref[c0:c0 + cw, :], NT)
                if rope is not None and p < rope[2]:
                    cos, sin = refs[1 + n_parts][...], refs[2 + n_parts][...]
                    for h0 in range(0, cw, ATTN_DIM):
                        xh = acc[:, h0:h0 + ATTN_DIM]
                        rot = pltpu.roll(xh, ATTN_DIM // 2, 1)
                        o_ref[:, off + c0 + h0:off + c0 + h0 + ATTN_DIM] = (xh * cos + rot * sin).astype(out_dtype)
                else:
                    o_ref[:, off + c0:off + c0 + cw] = acc.astype(out_dtype)
            off += widths[p]

    in_specs = [pl.BlockSpec((tm, K), lambda i: (i, 0))] + _part_specs(w_parts, K)
    args = [a] + [w for w, _, _ in w_parts]
    if rope is not None:
        in_specs += [pl.BlockSpec((tm, ATTN_DIM), lambda i: (i, 0))] * 2
        args += [rope[0], rope[1]]
    return pl.pallas_call(
        body, out_shape=jax.ShapeDtypeStruct((M, sum(widths)), out_dtype), grid=(M // tm,),
        in_specs=in_specs, out_specs=pl.BlockSpec((tm, sum(widths)), lambda i: (i, 0)),
        compiler_params=_params("parallel"), name=name)(*args)


def _mm_nn(a_list, w_parts_list, resid, *, name, norm=None, head=None):
    M = a_list[0].shape[0]
    tm = _pick_tile(M, ROW_TILE, 16)
    n_a = len(a_list)
    flat_parts = [p for parts in w_parts_list for p in parts]
    extra = norm if norm is not None else head
    n_in = n_a + len(flat_parts) + (1 if resid is not None else 0) + (2 if extra is not None else 0)

    def body(*refs):
        a_refs, w_refs = refs[:n_a], refs[n_a:n_a + len(flat_parts)]

        def product(rows):
            acc = None
            wi = 0
            for a_ref, parts in zip(a_refs, w_parts_list):
                off = 0
                for _, k, _ in parts:
                    term = _dot(a_ref[rows, off:off + k], w_refs[wi][...], NN)
                    acc = term if acc is None else acc + term
                    off += k
                    wi += 1
            return acc

        if extra is None:
            acc = product(slice(None))
            if resid is not None:
                acc = acc + refs[n_in - 1][...]
            refs[n_in][...] = acc
            return

        @pl.when(pl.program_id(0) == 0)
        def _():
            for acc_ref in refs[n_in + 2:]:
                acc_ref[...] = jnp.zeros_like(acc_ref)

        for r0 in range(0, tm, tm // 2):
            rows = slice(r0, r0 + tm // 2)
            acc = product(rows)
            if head is not None:
                _loss_head_math(acc + refs[n_in - 3][rows, :], rows, refs[n_in - 2], refs[n_in - 1],
                                *refs[n_in:n_in + 4])
                continue
            dres_ref, x_ref, g_ref = refs[n_in - 3:n_in]
            dx_ref, dxb_ref, dg_ref = refs[n_in:n_in + 3]
            xv = x_ref[rows, :]
            rstd = lax.rsqrt(jnp.mean(xv * xv, axis=-1, keepdims=True) + NORM_EPS)
            n = xv * rstd
            dg_ref[...] += jnp.sum(acc * n, axis=0, keepdims=True)
            dn = acc * g_ref[...]
            dx = dres_ref[rows, :] + rstd * (dn - n * jnp.mean(dn * n, axis=-1, keepdims=True))
            dx_ref[rows, :] = dx
            dxb_ref[rows, :] = dx.astype(BF16)

    row = pl.BlockSpec((tm, D_MODEL), lambda i: (i, 0))
    vec = pl.BlockSpec((1, D_MODEL), lambda i: (0, 0))
    in_specs = [pl.BlockSpec((tm, a.shape[1]), lambda i: (i, 0)) for a in a_list] + _part_specs(flat_parts, D_MODEL)
    args = list(a_list) + [w for w, _, _ in flat_parts]
    if resid is not None:
        in_specs.append(row)
        args.append(resid)
    if extra is None:
        return pl.pallas_call(
            body, out_shape=jax.ShapeDtypeStruct((M, D_MODEL), F32), grid=(M // tm,),
            in_specs=in_specs, out_specs=row, compiler_params=_params("parallel"), name=name)(*args)
    assert resid is not None
    out_shape = [jax.ShapeDtypeStruct((M, D_MODEL), F32), jax.ShapeDtypeStruct((M, D_MODEL), BF16),
                 jax.ShapeDtypeStruct((1, D_MODEL), F32)]
    out_specs = [row, row, vec]
    if head is not None:
        out_shape.append(jax.ShapeDtypeStruct((1, D_MODEL), F32))
        out_specs.append(vec)
    return pl.pallas_call(
        body, out_shape=out_shape, grid=(M // tm,), in_specs=in_specs + [row, vec], out_specs=out_specs,
        compiler_params=_params("arbitrary"), name=name)(*args, extra[0], extra[1])


def _mm_tn(a_list, b, *, name):
    T = a_list[0].shape[0]
    N = b.shape[1]
    tr = GRAD_TILE
    tiles = [a.shape[1] // tr for a in a_list]
    starts = [sum(tiles[:i]) for i in range(len(tiles))]

    def body(*refs):
        a_refs, b_ref, o_ref = refs[:len(a_list)], refs[len(a_list)], refs[-1]
        r = pl.program_id(0)
        for a_ref, first, count in zip(a_refs, starts, tiles):
            @pl.when(jnp.logical_and(r >= first, r < first + count))
            def _():
                o_ref[...] = _dot(a_ref[...], b_ref[...], TN).astype(BF16)

    in_specs = [pl.BlockSpec((T, tr), functools.partial(lambda r, first, count: (0, jnp.clip(r - first, 0, count - 1)),
                                                        first=first, count=count))
                for first, count in zip(starts, tiles)]
    in_specs.append(_whole((T, N), lambda r: (0, 0)))
    return pl.pallas_call(
        body, out_shape=jax.ShapeDtypeStruct((sum(tiles) * tr, N), BF16), grid=(sum(tiles),),
        in_specs=in_specs, out_specs=pl.BlockSpec((tr, N), lambda r: (r, 0)),
        compiler_params=_params("parallel"), name=name)(*a_list, b)


def _rms_fwd(x, gain, name):
    T = x.shape[0]
    tm = _pick_tile(T, 512, 16)

    def body(x_ref, g_ref, u_ref):
        xv = x_ref[...]
        rstd = lax.rsqrt(jnp.mean(xv * xv, axis=-1, keepdims=True) + NORM_EPS)
        u_ref[...] = (xv * rstd * g_ref[...]).astype(BF16)

    return pl.pallas_call(
        body, out_shape=jax.ShapeDtypeStruct((T, D_MODEL), BF16), grid=(T // tm,),
        in_specs=[pl.BlockSpec((tm, D_MODEL), lambda i: (i, 0)), pl.BlockSpec((1, D_MODEL), lambda i: (0, 0))],
        out_specs=pl.BlockSpec((tm, D_MODEL), lambda i: (i, 0)),
        compiler_params=_params("parallel"), name=name)(x, gain)


def _rms_bwd(x, gain, dus, dres, name, dilations=(1,)):
    T = x.shape[0]
    tm = _pick_tile(T, PERM_TILE, 16 * max(dilations))
    n_du = len(dus)

    def body(x_ref, g_ref, *refs):
        du_refs, dres_ref = refs[:n_du], refs[n_du]
        dx_ref, dxb_ref, dg_ref, du_scr = refs[n_du + 1:]

        @pl.when(pl.program_id(0) == 0)
        def _():
            dg_ref[...] = jnp.zeros_like(dg_ref)

        if tuple(dilations) == (1,):
            du = du_refs[0][...]
        else:
            for i, (d, du_ref) in enumerate(zip(dilations, du_refs)):
                for j in range(D_MODEL // LANES):
                    lanes = slice(j * LANES, (j + 1) * LANES)
                    if d == 1:
                        du_scr[j] = du_ref[:, lanes] if i == 0 else du_scr[j] + du_ref[:, lanes]
                        continue
                    blk = du_scr.at[j]
                    for r in range(d):
                        rows = _class_rows(r, d, tm)
                        blk[rows, :] = du_ref[r, :, lanes] if i == 0 else blk[rows, :] + du_ref[r, :, lanes]
            du = jnp.concatenate([du_scr[j] for j in range(D_MODEL // LANES)], axis=1)
        xv = x_ref[...]
        rstd = lax.rsqrt(jnp.mean(xv * xv, axis=-1, keepdims=True) + NORM_EPS)
        n = xv * rstd
        dg_ref[...] += jnp.sum(du * n, axis=0, keepdims=True)
        dn = du * g_ref[...]
        dx = dres_ref[...] + rstd * (dn - n * jnp.mean(dn * n, axis=-1, keepdims=True))
        dx_ref[...] = dx
        dxb_ref[...] = dx.astype(BF16)

    row = pl.BlockSpec((tm, D_MODEL), lambda i: (i, 0))
    vec = pl.BlockSpec((1, D_MODEL), lambda i: (0, 0))
    return pl.pallas_call(
        body,
        out_shape=(jax.ShapeDtypeStruct((T, D_MODEL), F32), jax.ShapeDtypeStruct((T, D_MODEL), BF16),
                   jax.ShapeDtypeStruct((1, D_MODEL), F32)),
        grid=(T // tm,), in_specs=[row, vec] + [_residue_spec(d, tm, D_MODEL) for d in dilations] + [row],
        out_specs=(row, row, vec), scratch_shapes=[pltpu.VMEM((D_MODEL // LANES, tm, LANES), F32)],
        compiler_params=_params("arbitrary"), name=name)(
            x, gain, *[_residue_view(du, d) for du, d in zip(dus, dilations)], dres)


def _loss_head_math(hv, rows, t_ref, g_ref, dh_ref, dhb_ref, dg_ref, loss_ref):
    inv_f = 1.0 / D_MODEL
    g = g_ref[...]
    rstd = lax.rsqrt(jnp.mean(hv * hv, axis=-1, keepdims=True) + NORM_EPS)
    n = hv * rstd
    err = n * g - t_ref[rows, :]
    loss_ref[...] += (0.5 * inv_f) * jnp.sum(err * err, axis=0, keepdims=True)
    dy = err * inv_f
    dg_ref[...] += jnp.sum(dy * n, axis=0, keepdims=True)
    dn = dy * g
    dh = rstd * (dn - n * jnp.mean(dn * n, axis=-1, keepdims=True))
    dh_ref[rows, :] = dh
    dhb_ref[rows, :] = dh.astype(BF16)


FFN_TILE = 256


def _ffn_in(h, gain, w_in, name):
    T = h.shape[0]
    tm = _pick_tile(T, ROW_TILE, 16)

    def body(h_ref, g_ref, w_ref, n_ref, gate_ref, up_ref, a_ref):
        hv = h_ref[...]
        rstd = lax.rsqrt(jnp.mean(hv * hv, axis=-1, keepdims=True) + NORM_EPS)
        n = (hv * rstd * g_ref[...]).astype(BF16)
        n_ref[...] = n
        for c0 in range(0, D_FF, FFN_TILE):
            cols = slice(c0, c0 + FFN_TILE)
            gate = _dot(n, w_ref[c0:c0 + FFN_TILE, :], NT)
            up = _dot(n, w_ref[D_FF + c0:D_FF + c0 + FFN_TILE, :], NT)
            gate_ref[:, cols] = gate.astype(BF16)
            up_ref[:, cols] = up.astype(BF16)
            a_ref[:, cols] = (gate * _sigmoid(gate) * up).astype(BF16)

    row = pl.BlockSpec((tm, D_MODEL), lambda i: (i, 0))
    wide = pl.BlockSpec((tm, D_FF), lambda i: (i, 0))
    wide_shape = jax.ShapeDtypeStruct((T, D_FF), BF16)
    return pl.pallas_call(
        body, out_shape=(jax.ShapeDtypeStruct((T, D_MODEL), BF16), wide_shape, wide_shape, wide_shape),
        grid=(T // tm,),
        in_specs=[row, pl.BlockSpec((1, D_MODEL), lambda i: (0, 0)), _whole((2 * D_FF, D_MODEL), lambda i: (0, 0))],
        out_specs=(row, wide, wide, wide), compiler_params=_params("parallel"), name=name)(h, gain, w_in)


def _ffn_down_dx(dhb, w_down, gate, up, name):
    T = dhb.shape[0]
    tm = _pick_tile(T, ROW_TILE, 16)

    def body(dh_ref, w_ref, gate_ref, up_ref, dgate_ref, dup_ref):
        dh = dh_ref[...]
        for c0 in range(0, D_FF, FFN_TILE):
            cols = slice(c0, c0 + FFN_TILE)
            da = _dot(dh, w_ref[c0:c0 + FFN_TILE, :], NT).astype(BF16)
            gate = gate_ref[:, cols]
            sg = (0.5 * jnp.tanh(0.5 * jnp.abs(gate)) + 0.5) * jnp.exp(jnp.minimum(gate, 0.0))
            silu = gate * sg
            dgate_ref[:, cols] = da * up_ref[:, cols] * (sg + silu * (1.0 - sg))
            dup_ref[:, cols] = da * silu

    wide = pl.BlockSpec((tm, D_FF), lambda i: (i, 0))
    wide_shape = jax.ShapeDtypeStruct((T, D_FF), BF16)
    return pl.pallas_call(
        body, out_shape=(wide_shape, wide_shape), grid=(T // tm,),
        in_specs=[pl.BlockSpec((tm, D_MODEL), lambda i: (i, 0)), _whole((D_FF, D_MODEL), lambda i: (0, 0)), wide, wide],
        out_specs=(wide, wide), compiler_params=_params("parallel"), name=name)(dhb, w_down, gate, up)


def _tri(n, lower):
    r = lax.broadcasted_iota(jnp.int32, (n, n), 0)
    c = lax.broadcasted_iota(jnp.int32, (n, n), 1)
    return (c <= r) if lower else (c >= r)


def _running_sum(x, lower):
    tri = _tri(x.shape[0], lower).astype(BF16)
    hi = x.astype(BF16)
    rest = x - hi.astype(F32)
    mid = rest.astype(BF16)
    lo = (rest - mid.astype(F32)).astype(BF16)
    return _dot(tri, hi, NN) + _dot(tri, mid, NN) + _dot(tri, lo, NN)


def _hgrn_gates(q_raw, f_raw, lb):
    C = q_raw.shape[0]
    sig_f = _sigmoid(f_raw)
    forget = lb + (1.0 - lb) * sig_f
    key = 1.0 - forget
    log_f = jnp.log(forget)
    b = _running_sum(log_f, True)
    first_half = lax.broadcasted_iota(jnp.int32, log_f.shape, 0) < C // 2
    r = jnp.sum(jnp.where(first_half, log_f, 0.0), axis=0, keepdims=True)
    b_last = jnp.sum(log_f, axis=0, keepdims=True)
    e_a = jnp.exp(jnp.minimum(b - r, HGRN_EXP_CLAMP))
    e_b = jnp.exp(jnp.minimum(r - b, HGRN_EXP_CLAMP))
    e_q = jnp.exp(b)
    e_k = jnp.exp(b_last - b)
    sig_q = _sigmoid(q_raw)
    query = q_raw * sig_q
    return dict(sig_f=sig_f, forget=forget, sig_q=sig_q, e_a=e_a, e_b=e_b, e_q=e_q, e_k=e_k,
                e_last=jnp.exp(b_last), q_a=query * e_a, k_b=key * e_b, q_hat=query * e_q, k_til=key * e_k)


def _hgrn_fwd(proj, lb, gain, name):
    T = proj.shape[0]
    C = HGRN_CHUNK
    CPS = HGRN_STEP_CHUNKS
    H, HD = HGRN_HEADS, HGRN_DIM

    def body(q_ref, f_ref, i_ref, g_ref, lb_ref, gain_ref, og_ref, o_ref, st_ref, s_scr):
        @pl.when(pl.program_id(0) == 0)
        def _():
            s_scr[...] = jnp.zeros_like(s_scr)

        causal = _tri(C, True)
        gain_v = gain_ref[...]
        heads = [slice(h * HD, (h + 1) * HD) for h in range(H)]
        s_t = [s_scr[h] for h in range(H)]
        for cc in range(CPS):
            rows = slice(cc * C, (cc + 1) * C)
            for h in range(H):
                st_ref[cc, h] = s_t[h]
            gt = _hgrn_gates(q_ref[rows, :], f_ref[rows, :], lb_ref[...])
            q_a, k_b = gt["q_a"].astype(BF16), gt["k_b"].astype(BF16)
            q_hat, k_til = gt["q_hat"].astype(BF16), gt["k_til"].astype(BF16)
            v = i_ref[rows, :].astype(BF16)
            p = [jnp.where(causal, _dot(q_a[:, sl], k_b[:, sl], NT), 0.0).astype(BF16) for sl in heads]
            o = [_dot(p[h], v[:, sl], NN) + _dot(q_hat[:, sl], s_t[h].astype(BF16), NT)
                 for h, sl in enumerate(heads)]
            s_t = [gt["e_last"][:, sl] * s_t[h] + _dot(v[:, sl], k_til[:, sl], TN) for h, sl in enumerate(heads)]
            for h, sl in enumerate(heads):
                o_ref[rows, sl] = o[h]
                rstd = lax.rsqrt(jnp.mean(o[h] * o[h], axis=-1, keepdims=True) + NORM_EPS)
                g_raw = g_ref[rows, sl]
                og_ref[rows, sl] = (o[h] * rstd * gain_v * (g_raw * _sigmoid(g_raw))).astype(BF16)
        for h in range(H):
            s_scr[h] = s_t[h]

    col = lambda j: pl.BlockSpec((CPS * C, D_MODEL), lambda c: (c, j))
    row = pl.BlockSpec((CPS * C, D_MODEL), lambda c: (c, 0))
    return pl.pallas_call(
        body,
        out_shape=(jax.ShapeDtypeStruct((T, D_MODEL), BF16), jax.ShapeDtypeStruct((T, D_MODEL), F32),
                   jax.ShapeDtypeStruct((T // C, H, HD, HD), F32)),
        grid=(T // (CPS * C),),
        in_specs=[col(0), col(1), col(2), col(3), pl.BlockSpec((1, D_MODEL), lambda c: (0, 0)),
                  pl.BlockSpec((1, HD), lambda c: (0, 0))],
        out_specs=(row, row, pl.BlockSpec((CPS, H, HD, HD), lambda c: (c, 0, 0, 0))),
        scratch_shapes=[pltpu.VMEM((H, HD, HD), F32)],
        compiler_params=_params("arbitrary"), name=name)(proj, proj, proj, proj, lb, gain)


def _hgrn_bwd(proj, o_pre, d_og, states, lb, gain, name):
    T = proj.shape[0]
    C = HGRN_CHUNK
    CPS = HGRN_STEP_CHUNKS
    H, HD = HGRN_HEADS, HGRN_DIM
    NC = T // (CPS * C)

    def body(q_ref, f_ref, i_ref, g_ref, o_ref, dog_ref, st_ref, lb_ref, gain_ref,
             dproj_ref, dlb_ref, dgain_ref, ds_scr, dq_all, dk_all, db_all):
        @pl.when(pl.program_id(0) == 0)
        def _():
            ds_scr[...] = jnp.zeros_like(ds_scr)
            dlb_ref[...] = jnp.zeros_like(dlb_ref)
            dgain_ref[...] = jnp.zeros_like(dgain_ref)

        lbv = lb_ref[...]
        causal = _tri(C, True)
        last_row = lax.broadcasted_iota(jnp.int32, (C, HD), 0) == C - 1
        gain_v = gain_ref[...]
        heads = [slice(h * HD, (h + 1) * HD) for h in range(H)]
        hs = range(H)
        ds_t = [ds_scr[h] for h in hs]
        dgain = None
        for cc in reversed(range(CPS)):
            rows = slice(cc * C, (cc + 1) * C)
            dq_scr, dk_scr, db_scr = dq_all.at[cc], dk_all.at[cc], db_all.at[cc]
            q_raw = q_ref[rows, :]
            gt = _hgrn_gates(q_raw, f_ref[rows, :], lbv)
            o = [o_ref[rows, sl] for sl in heads]
            rstd = [lax.rsqrt(jnp.mean(x * x, axis=-1, keepdims=True) + NORM_EPS) for x in o]
            n = [x * r for x, r in zip(o, rstd)]
            g_raw = [g_ref[rows, sl] for sl in heads]
            sg = [_sigmoid(x) for x in g_raw]
            d_out = [dog_ref[rows, sl] for sl in heads]
            dy = [d * (g * s) for d, g, s in zip(d_out, g_raw, sg)]
            dn = [x * gain_v for x in dy]
            do = [(rstd[h] * (dn[h] - n[h] * jnp.mean(dn[h] * n[h], axis=-1, keepdims=True))).astype(BF16) for h in hs]
            for h in hs:
                dgain = dy[h] * n[h] if dgain is None else dgain + dy[h] * n[h]
            for h, sl in enumerate(heads):
                dproj_ref[rows, 3 * D_MODEL + h * HD:3 * D_MODEL + (h + 1) * HD] = (
                    d_out[h] * n[h] * gain_v * (sg[h] * (1.0 + g_raw[h] * (1.0 - sg[h])))).astype(BF16)
            q_ab, k_bb = gt["q_a"].astype(BF16), gt["k_b"].astype(BF16)
            q_hb, k_tb = gt["q_hat"].astype(BF16), gt["k_til"].astype(BF16)
            v = i_ref[rows, :].astype(BF16)
            s_t = [st_ref[cc, h] for h in hs]
            ds_b = [x.astype(BF16) for x in ds_t]
            p = [jnp.where(causal, _dot(q_ab[:, sl], k_bb[:, sl], NT), 0.0).astype(BF16) for sl in heads]
            dp = [jnp.where(causal, _dot(do[h], v[:, sl], NT), 0.0).astype(BF16) for h, sl in enumerate(heads)]
            dv = [_dot(p[h], do[h], TN) + _dot(k_tb[:, sl], ds_b[h], NT) for h, sl in enumerate(heads)]
            dq_a = [_dot(dp[h], k_bb[:, sl], NN) for h, sl in enumerate(heads)]
            dk_b = [_dot(dp[h], q_ab[:, sl], TN) for h, sl in enumerate(heads)]
            dq_hat = [_dot(do[h], s_t[h].astype(BF16), NN) for h in hs]
            dk_til = [_dot(v[:, sl], ds_b[h], NN) for h, sl in enumerate(heads)]
            ds_new = [_dot(do[h], q_hb[:, sl], TN) + gt["e_last"][:, sl] * ds_t[h] for h, sl in enumerate(heads)]
            for h, sl in enumerate(heads):
                k_til = gt["k_til"][:, sl]
                db_last = jnp.sum(ds_t[h] * gt["e_last"][:, sl] * s_t[h], axis=0, keepdims=True) + jnp.sum(
                    dk_til[h] * k_til, axis=0, keepdims=True)
                dproj_ref[rows, 2 * D_MODEL + h * HD:2 * D_MODEL + (h + 1) * HD] = dv[h].astype(BF16)
                dq_scr[:, sl] = dq_a[h] * gt["e_a"][:, sl] + dq_hat[h] * gt["e_q"][:, sl]
                dk_scr[:, sl] = dk_b[h] * gt["e_b"][:, sl] + dk_til[h] * gt["e_k"][:, sl]
                db = (dq_a[h] * q_ab[:, sl].astype(F32) + dq_hat[h] * gt["q_hat"][:, sl]
                      - dk_b[h] * k_bb[:, sl].astype(F32) - dk_til[h] * k_til)
                db_scr[:, sl] = db + jnp.where(last_row, db_last, 0.0)
            dlogf = _running_sum(db_scr[...], False)
            sig_f, forget, sig_q = gt["sig_f"], gt["forget"], gt["sig_q"]
            dforget = dlogf / forget - dk_scr[...]
            dproj_ref[rows, D_MODEL:2 * D_MODEL] = (dforget * (1.0 - lbv) * sig_f * (1.0 - sig_f)).astype(BF16)
            dlb_ref[...] += jnp.sum(dforget * (1.0 - sig_f), axis=0, keepdims=True)
            dproj_ref[rows, 0:D_MODEL] = (dq_scr[...] * (sig_q * (1.0 + q_raw * (1.0 - sig_q)))).astype(BF16)
            ds_t = ds_new
        dgain_ref[...] += jnp.sum(dgain, axis=0, keepdims=True)
        for h in hs:
            ds_scr[h] = ds_t[h]

    col = lambda j: pl.BlockSpec((CPS * C, D_MODEL), lambda c: (NC - 1 - c, j))
    row = pl.BlockSpec((CPS * C, D_MODEL), lambda c: (NC - 1 - c, 0))
    return pl.pallas_call(
        body,
        out_shape=(jax.ShapeDtypeStruct((T, 4 * D_MODEL), BF16), jax.ShapeDtypeStruct((1, D_MODEL), F32),
                   jax.ShapeDtypeStruct((1, HD), F32)),
        grid=(NC,),
        in_specs=[col(0), col(1), col(2), col(3), row, row,
                  pl.BlockSpec((CPS, H, HD, HD), lambda c: (NC - 1 - c, 0, 0, 0)),
                  pl.BlockSpec((1, D_MODEL), lambda c: (0, 0)), pl.BlockSpec((1, HD), lambda c: (0, 0))],
        out_specs=(pl.BlockSpec((CPS * C, 4 * D_MODEL), lambda c: (NC - 1 - c, 0)),
                   pl.BlockSpec((1, D_MODEL), lambda c: (0, 0)), pl.BlockSpec((1, HD), lambda c: (0, 0))),
        scratch_shapes=[pltpu.VMEM((H, HD, HD), F32)] + [pltpu.VMEM((CPS, C, D_MODEL), F32)] * 3,
        compiler_params=_params("arbitrary"), name=name)(proj, proj, proj, proj, o_pre, d_og, states, lb, gain)


def _attn_masks():
    r = lax.broadcasted_iota(jnp.int32, (ATTN_BLOCK, ATTN_BLOCK), 0)
    c = lax.broadcasted_iota(jnp.int32, (ATTN_BLOCK, ATTN_BLOCK), 1)
    return c >= r, c <= r


def _attn_fwd(qkv, dilation, name):
    T = qkv.shape[0]
    nb = T // dilation // ATTN_BLOCK
    W = ATTN_GROUP_WIDTH
    B = ATTN_BLOCK
    scale = ATTN_DIM ** -0.5
    qb = 2 if nb % 2 == 0 else 1
    steps = nb // qb

    def body(q_ref, kp_ref, kc_ref, vp_ref, vc_ref, o_ref, lse_ref):
        no_prev = jnp.where(pl.program_id(1) > 0, 0.0, NEG_BIG)
        m_prev, m_cur = _attn_masks()
        ones = jnp.ones((B, ATTN_DIM), BF16)
        items = []
        for j in range(qb):
            for h in range(ATTN_GROUP_HEADS):
                sl = slice(h * ATTN_DIM, (h + 1) * ATTN_DIM)
                rows = slice(j * B, (j + 1) * B)
                if j == 0:
                    items.append((rows, sl, kp_ref[:, sl], vp_ref[:, sl], no_prev))
                else:
                    before = slice((j - 1) * B, j * B)
                    items.append((rows, sl, kc_ref[before, sl], vc_ref[before, sl], 0.0))
        s_p = [jnp.where(m_prev, _dot(q_ref[rows, sl], k_p, NT) * scale + bias, NEG_BIG)
               for rows, sl, k_p, _, bias in items]
        s_c = [jnp.where(m_cur, _dot(q_ref[rows, sl], kc_ref[rows, sl], NT) * scale, NEG_BIG)
               for rows, sl, _, _, _ in items]
        m = [jnp.max(jnp.maximum(a, b), axis=-1, keepdims=True) for a, b in zip(s_p, s_c)]
        p_p = [jnp.exp(a - mx).astype(BF16) for a, mx in zip(s_p, m)]
        p_c = [jnp.exp(b - mx).astype(BF16) for b, mx in zip(s_c, m)]
        l = [_dot(a, ones, NN) + _dot(b, ones, NN) for a, b in zip(p_p, p_c)]
        acc = [_dot(a, v_p, NN) + _dot(b, vc_ref[rows, sl], NN)
               for a, b, (rows, sl, _, v_p, _) in zip(p_p, p_c, items)]
        for (rows, sl, _, _, _), a, lv, mx in zip(items, acc, l, m):
            o_ref[rows, sl] = (a / lv).astype(BF16)
            lse_ref[rows, sl] = mx + jnp.log(lv)

    cur = lambda col: pl.BlockSpec((qb * B, W), lambda s, n: (s * steps + n, col))
    prev = lambda col: pl.BlockSpec((B, W), lambda s, n: (s * nb + jnp.maximum(qb * n - 1, 0), col))
    out = pl.BlockSpec((qb * B, W), lambda s, n: (s * steps + n, 0))
    return pl.pallas_call(
        body, out_shape=(jax.ShapeDtypeStruct((T, W), BF16), jax.ShapeDtypeStruct((T, W), F32)),
        grid=(dilation, steps),
        in_specs=[cur(0), prev(1), cur(1), prev(2), cur(2)],
        out_specs=(out, out), compiler_params=_params("parallel", "arbitrary"), name=name)(qkv, qkv, qkv, qkv, qkv)


def _attn_bwd(qkv, d_out, lse, delta, cos, sin, dilation, name):
    T = qkv.shape[0]
    nb = T // dilation // ATTN_BLOCK
    assert nb % 2 == 0, "an even number of 128-token blocks per residue class"
    pairs = nb // 2
    W = ATTN_GROUP_WIDTH
    B = ATTN_BLOCK
    scale = ATTN_DIM ** -0.5

    def unrope(x, cos_v, sin_v):
        return x * cos_v + pltpu.roll(x * sin_v, ATTN_DIM // 2, 1)

    def body(qa_ref, qb_ref, kpair_ref, kc_ref, vpair_ref, vc_ref, doa_ref, dob_ref, lsea_ref, lseb_ref,
             dla_ref, dlb_ref, cos_ref, sin_ref, out_ref, dq_scr, dk_scr, dv_scr):
        n = pl.program_id(1)

        @pl.when(n == 0)
        def _():
            dq_scr[...] = jnp.zeros_like(dq_scr)
            dk_scr[...] = jnp.zeros_like(dk_scr)
            dv_scr[...] = jnp.zeros_like(dv_scr)

        no_a = jnp.where(n > 0, 0.0, NEG_BIG)
        no_b = jnp.where(n < pairs, 0.0, NEG_BIG)
        m_prev, m_cur = _attn_masks()
        lo, hi = slice(0, B), slice(B, 2 * B)
        heads = [slice(h * ATTN_DIM, (h + 1) * ATTN_DIM) for h in range(ATTN_GROUP_HEADS)]
        flat = []
        for sl in heads:
            qa, qb = qa_ref[:, sl], qb_ref[:, sl]
            doa, dob = doa_ref[:, sl], dob_ref[:, sl]
            k0, k1, k2 = kpair_ref[lo, sl], kpair_ref[hi, sl], kc_ref[:, sl]
            v0, v1, v2 = vpair_ref[lo, sl], vpair_ref[hi, sl], vc_ref[:, sl]
            flat += [(qa, doa, lsea_ref[:, sl], dla_ref[:, sl], k0, v0, m_prev, no_a),
                     (qa, doa, lsea_ref[:, sl], dla_ref[:, sl], k1, v1, m_cur, no_a),
                     (qb, dob, lseb_ref[:, sl], dlb_ref[:, sl], k1, v1, m_prev, no_a + no_b),
                     (qb, dob, lseb_ref[:, sl], dlb_ref[:, sl], k2, v2, m_cur, no_b)]
        s = [_dot(q, k, NT) for q, _, _, _, k, _, _, _ in flat]
        dp = [_dot(do, v, NT) for _, do, _, _, _, v, _, _ in flat]
        p = [jnp.where(mask, jnp.exp(sv * scale - lse_v + bias), 0.0)
             for sv, (_, _, lse_v, _, _, _, mask, bias) in zip(s, flat)]
        ds = [(pv * (dpv - dl_v) * scale).astype(BF16) for pv, dpv, (_, _, _, dl_v, _, _, _, _) in zip(p, dp, flat)]
        p = [pv.astype(BF16) for pv in p]
        dq_part = [_dot(dsv, k, NN) for dsv, (_, _, _, _, k, _, _, _) in zip(ds, flat)]
        dk_part = [_dot(dsv, q, TN) for dsv, (q, _, _, _, _, _, _, _) in zip(ds, flat)]
        dv_part = [_dot(pv, do, TN) for pv, (_, do, _, _, _, _, _, _) in zip(p, flat)]
        cos_lo, sin_lo, cos_hi, sin_hi = cos_ref[lo, :], sin_ref[lo, :], cos_ref[hi, :], sin_ref[hi, :]
        for h, sl in enumerate(heads):
            a_prev, a_cur, b_prev, b_cur = range(4 * h, 4 * h + 4)
            kcol = slice(W + h * ATTN_DIM, W + (h + 1) * ATTN_DIM)
            vcol = slice(2 * W + h * ATTN_DIM, 2 * W + (h + 1) * ATTN_DIM)
            out_ref[lo, sl] = unrope(dq_scr[:, sl], cos_lo, sin_lo).astype(BF16)
            out_ref[hi, sl] = unrope(dq_part[a_prev] + dq_part[a_cur], cos_hi, sin_hi).astype(BF16)
            out_ref[lo, kcol] = unrope(dk_scr[:, sl] + dk_part[a_prev], cos_lo, sin_lo).astype(BF16)
            out_ref[hi, kcol] = unrope(dk_part[a_cur] + dk_part[b_prev], cos_hi, sin_hi).astype(BF16)
            out_ref[lo, vcol] = (dv_scr[:, sl] + dv_part[a_prev]).astype(BF16)
            out_ref[hi, vcol] = (dv_part[a_cur] + dv_part[b_prev]).astype(BF16)
            dq_scr[:, sl] = dq_part[b_prev] + dq_part[b_cur]
            dk_scr[:, sl] = dk_part[b_cur]
            dv_scr[:, sl] = dv_part[b_cur]

    def block_a(n):
        return jnp.maximum(2 * n - 1, 0)

    def block_b(n):
        return jnp.minimum(2 * n, nb - 1)

    def pair(n):
        return jnp.maximum(n - 1, 0)

    one_a = lambda col: pl.BlockSpec((B, W), lambda s, n: (s * nb + block_a(n), col))
    one_b = lambda col: pl.BlockSpec((B, W), lambda s, n: (s * nb + block_b(n), col))
    two = lambda col: pl.BlockSpec((2 * B, W), lambda s, n: (s * pairs + pair(n), col))
    tab = pl.BlockSpec((2 * B, ATTN_DIM), lambda s, n: (s * pairs + pair(n), 0))
    return pl.pallas_call(
        body, out_shape=jax.ShapeDtypeStruct((T, 3 * W), BF16), grid=(dilation, pairs + 1),
        in_specs=[one_a(0), one_b(0), two(1), one_b(1), two(2), one_b(2), one_a(0), one_b(0), one_a(0), one_b(0),
                  one_a(0), one_b(0), tab, tab],
        out_specs=pl.BlockSpec((2 * B, 3 * W), lambda s, n: (s * pairs + pair(n), 0)),
        scratch_shapes=[pltpu.VMEM((B, W), F32)] * 3,
        compiler_params=_params("parallel", "arbitrary"), name=name)(
            qkv, qkv, qkv, qkv, qkv, qkv, d_out, d_out, lse, lse, delta, delta, cos, sin)


PERM_TILE = 512
LANES = 128


def _residue_view(x, d):
    return x if d == 1 else x.reshape(d, x.shape[0] // d, x.shape[1])


def _residue_spec(d, tm, cols):
    if d == 1:
        return pl.BlockSpec((tm, cols), lambda i: (i, 0))
    return pl.BlockSpec((d, tm // d, cols), lambda i: (0, i, 0))


def _residue_shape(T, d, cols, dtype):
    return jax.ShapeDtypeStruct((T, cols) if d == 1 else (d, T // d, cols), dtype)


def _class_rows(r, d, tm):
    return pl.ds(r, tm // d, stride=d)


def _attn_norm(h, gain, name):
    T = h.shape[0]
    tm = _pick_tile(T, PERM_TILE, 16 * max(ATTN_DILATIONS))
    dils = ATTN_DILATIONS
    (base_cos, base_sin), (off_cos, off_sin), sign = _rope_parts(T, tm)

    def body(h_ref, g_ref, bc_ref, bs_ref, oc_ref, os_ref, sign_ref, *refs):
        u_refs, c_refs, s_refs, u_scr, c_scr, s_scr = refs[0:3], refs[3:6], refs[6:9], refs[9], refs[10], refs[11]
        hv = h_ref[...]
        rstd = lax.rsqrt(jnp.mean(hv * hv, axis=-1, keepdims=True) + NORM_EPS)
        u = hv * rstd * g_ref[...]
        for j in range(D_MODEL // LANES):
            u_scr[j] = u[:, j * LANES:(j + 1) * LANES]
        bc, bs, oc, osn = bc_ref[0], bs_ref[0], oc_ref[...], os_ref[...]
        c_scr[...] = bc * oc - bs * osn
        s_scr[...] = (bs * oc + bc * osn) * sign_ref[...]
        for d, u_ref, c_ref, s_ref in zip(dils, u_refs, c_refs, s_refs):
            if d == 1:
                u_ref[...] = u.astype(BF16)
                c_ref[...] = c_scr[...]
                s_ref[...] = s_scr[...]
                continue
            for r in range(d):
                rows = _class_rows(r, d, tm)
                for j in range(D_MODEL // LANES):
                    u_ref[r, :, j * LANES:(j + 1) * LANES] = u_scr.at[j][rows, :].astype(BF16)
                c_ref[r] = c_scr[rows, :]
                s_ref[r] = s_scr[rows, :]

    row = pl.BlockSpec((tm, D_MODEL), lambda i: (i, 0))
    base = pl.BlockSpec((1, 1, ATTN_DIM), lambda i: (i, 0, 0))
    off = pl.BlockSpec((tm, ATTN_DIM), lambda i: (0, 0))
    res = pl.pallas_call(
        body,
        out_shape=([_residue_shape(T, d, D_MODEL, BF16) for d in dils]
                   + [_residue_shape(T, d, ATTN_DIM, F32) for d in dils] * 2),
        grid=(T // tm,),
        in_specs=[row, pl.BlockSpec((1, D_MODEL), lambda i: (0, 0)), base, base, off, off,
                  pl.BlockSpec((1, ATTN_DIM), lambda i: (0, 0))],
        out_specs=([_residue_spec(d, tm, D_MODEL) for d in dils] + [_residue_spec(d, tm, ATTN_DIM) for d in dils] * 2),
        scratch_shapes=[pltpu.VMEM((D_MODEL // LANES, tm, LANES), F32), pltpu.VMEM((tm, ATTN_DIM), F32),
                        pltpu.VMEM((tm, ATTN_DIM), F32)],
        compiler_params=_params("parallel"), name=name)(h, gain, base_cos, base_sin, off_cos, off_sin, sign)
    flat = [r.reshape(T, r.shape[-1]) for r in res]
    return flat[0:3], flat[3:6], flat[6:9]


def _attn_merge_fwd(outs, lses, name):
    T = outs[0].shape[0]
    W = ATTN_GROUP_WIDTH
    tm = _pick_tile(T, PERM_TILE, 16 * max(ATTN_DILATIONS))
    dils = ATTN_DILATIONS

    def body(*refs):
        o_refs, l_refs, oc_ref, lse_refs = refs[0:3], refs[3:6], refs[6], refs[7:10]
        o_scr, l_scr, t_scr = refs[10:13]
        nh = ATTN_GROUP_HEADS
        for g, d in enumerate(dils):
            for j in range(nh):
                lanes = slice(j * LANES, (j + 1) * LANES)
                if d == 1:
                    o_scr[g * nh + j] = o_refs[g][:, lanes].astype(F32)
                    l_scr[g * nh + j] = l_refs[g][:, lanes]
                    continue
                for r in range(d):
                    rows = _class_rows(r, d, tm)
                    o_scr.at[g * nh + j][rows, :] = o_refs[g][r, :, lanes].astype(F32)
                    l_scr.at[g * nh + j][rows, :] = l_refs[g][r, :, lanes]
        for j in range(nh):
            lanes = slice(j * LANES, (j + 1) * LANES)
            ls = [l_scr[g * nh + j] for g in range(3)]
            m = jnp.maximum(jnp.maximum(ls[0], ls[1]), ls[2])
            tot = m + jnp.log(jnp.exp(ls[0] - m) + jnp.exp(ls[1] - m) + jnp.exp(ls[2] - m))
            t_scr[j] = tot
            for g, d in enumerate(dils):
                oc_ref[:, g * W + j * LANES:g * W + (j + 1) * LANES] = (
                    o_scr[g * nh + j] * jnp.exp(ls[g] - tot)).astype(BF16)
                if d == 1:
                    lse_refs[g][:, lanes] = tot
                    continue
                for r in range(d):
                    lse_refs[g][r, :, lanes] = t_scr.at[j][_class_rows(r, d, tm), :]

    in_blk = [_residue_spec(d, tm, W) for d in dils]
    n_blk = 3 * ATTN_GROUP_HEADS
    res = pl.pallas_call(
        body, out_shape=[jax.ShapeDtypeStruct((T, 3 * W), BF16)] + [_residue_shape(T, d, W, F32) for d in dils],
        grid=(T // tm,), in_specs=in_blk * 2,
        out_specs=[pl.BlockSpec((tm, 3 * W), lambda i: (i, 0))] + in_blk,
        scratch_shapes=[pltpu.VMEM((n_blk, tm, LANES), F32), pltpu.VMEM((n_blk, tm, LANES), F32),
                        pltpu.VMEM((ATTN_GROUP_HEADS, tm, LANES), F32)],
        compiler_params=_params("parallel"), name=name)(
            *[_residue_view(o, d) for o, d in zip(outs, dils)], *[_residue_view(l, d) for l, d in zip(lses, dils)])
    return res[0], [r.reshape(T, W) for r in res[1:]]


def _attn_merge_bwd(d_oc, oc, name):
    T = d_oc.shape[0]
    W = ATTN_GROUP_WIDTH
    tm = _pick_tile(T, PERM_TILE, 16 * max(ATTN_DILATIONS))
    dils = ATTN_DILATIONS

    def body(d_ref, o_ref, *refs):
        delta_refs, db_refs, dl_scr, d_scr = refs[0:3], refs[3:6], refs[6], refs[7]
        nh = ATTN_GROUP_HEADS
        for j in range(nh):
            tot = jnp.zeros((tm, 1), F32)
            for g in range(3):
                cols = slice(g * W + j * LANES, g * W + (j + 1) * LANES)
                d_blk = d_ref[:, cols]
                d_scr[g * nh + j] = d_blk
                tot = tot + jnp.sum(d_blk * o_ref[:, cols].astype(F32), axis=-1, keepdims=True)
            dl_scr[j] = jnp.broadcast_to(tot, (tm, LANES))
        for g, d in enumerate(dils):
            for j in range(nh):
                lanes = slice(j * LANES, (j + 1) * LANES)
                if d == 1:
                    delta_refs[g][:, lanes] = dl_scr[j]
                    db_refs[g][:, lanes] = d_scr[g * nh + j].astype(BF16)
                    continue
                for r in range(d):
                    rows = _class_rows(r, d, tm)
                    delta_refs[g][r, :, lanes] = dl_scr.at[j][rows, :]
                    db_refs[g][r, :, lanes] = d_scr.at[g * nh + j][rows, :].astype(BF16)

    wide = pl.BlockSpec((tm, 3 * W), lambda i: (i, 0))
    out_blk = [_residue_spec(d, tm, W) for d in dils]
    res = pl.pallas_call(
        body, out_shape=[_residue_shape(T, d, W, F32) for d in dils] + [_residue_shape(T, d, W, BF16) for d in dils],
        grid=(T // tm,), in_specs=[wide, wide], out_specs=out_blk * 2,
        scratch_shapes=[pltpu.VMEM((ATTN_GROUP_HEADS, tm, LANES), F32),
                        pltpu.VMEM((3 * ATTN_GROUP_HEADS, tm, LANES), F32)],
        compiler_params=_params("parallel"), name=name)(d_oc, oc)
    flat = [r.reshape(T, W) for r in res]
    return flat[0:3], flat[3:6]


def _rope_parts(T, tile):
    inv_freq = 1.0 / (ROPE_THETA ** (jnp.arange(0, ATTN_DIM, 2, dtype=F32) / ATTN_DIM))
    inv_freq = jnp.concatenate([inv_freq, inv_freq])[None, :]
    base = (jnp.arange(T // tile, dtype=F32) * tile)[:, None] * inv_freq
    off = jnp.arange(tile, dtype=F32)[:, None] * inv_freq
    sign = jnp.concatenate([-jnp.ones((1, ATTN_DIM // 2), F32), jnp.ones((1, ATTN_DIM // 2), F32)], axis=1)
    return (jnp.cos(base)[:, None, :], jnp.sin(base)[:, None, :]), (jnp.cos(off), jnp.sin(off)), sign


WEIGHT_GROUPS = {"hgrn": ("hgrn_in", "hgrn_out"), "ffn0": ("ffn_in0", "ffn_down0"),
                 "attn": ("qkv", "attn_out"), "ffn1": ("ffn_in1", "ffn_down1")}


def _local_step(x, target, norm_mix, norm_ffn, lb, out_gain, final_gain, fetch, publish):
    g_mix = [norm_mix[0:1], norm_mix[1:2]]
    g_ffn = [norm_ffn[0:1], norm_ffn[1:2]]
    w = {}

    def whole(name):
        return [(w[name], w[name].shape[0], 0)]

    def qkv_parts(g):
        return [(w["qkv"], ATTN_GROUP_WIDTH, 3 * j + g) for j in range(3)]

    def ffn_fwd(h, layer, head=None):
        w.update(fetch(f"ffn{layer}"))
        n, gate, up, a = _ffn_in(h, g_ffn[layer], w[f"ffn_in{layer}"], f"ffn{layer}_in")
        out = _mm_nn([a], [whole(f"ffn_down{layer}")], h, name=f"ffn{layer}_down", head=head)
        return out, (n, gate, up, a)

    def ffn_bwd(h, saved, dh, dhb, layer):
        n, gate, up, a = saved
        w_in = w[f"ffn_in{layer}"]
        dgate, dup = _ffn_down_dx(dhb, w[f"ffn_down{layer}"], gate, up, f"ffn{layer}_down_dx")
        grads = {f"ffn_down{layer}": _mm_tn([a], dhb, name=f"ffn{layer}_down_dw"),
                 f"ffn_in{layer}": _mm_tn([dgate, dup], n, name=f"ffn{layer}_in_dw")}
        publish(f"ffn{layer}", grads)
        return _mm_nn([dgate, dup], [[(w_in, D_FF, 0)], [(w_in, D_FF, 1)]], dh, name=f"ffn{layer}_in_dx",
                      norm=(h, g_ffn[layer]))

    u0 = _rms_fwd(x, g_mix[0], "hgrn_norm")
    w.update(fetch("hgrn"))
    proj = _mm_nt(u0, whole("hgrn_in"), out_dtype=F32, name="hgrn_in")
    og, o_pre, states = _hgrn_fwd(proj, lb, out_gain, "hgrn_fwd")
    h1 = _mm_nn([og], [whole("hgrn_out")], x, name="hgrn_out")
    h2, ffn0 = ffn_fwd(h1, 0)

    u1_g, cos_g, sin_g = _attn_norm(h2, g_mix[1], "attn_norm")
    w.update(fetch("attn"))
    qkv_g, outs, lses = [], [], []
    for g, d in enumerate(ATTN_DILATIONS):
        qkv_g.append(_mm_nt(u1_g[g], qkv_parts(g), out_dtype=BF16, name=f"attn_qkv{g}",
                            rope=(cos_g[g], sin_g[g], 2)))
        o_g, lse_g = _attn_fwd(qkv_g[g], d, f"attn_fwd{g}")
        outs.append(o_g)
        lses.append(lse_g)
    oc, lse_all = _attn_merge_fwd(outs, lses, "attn_merge")
    h3 = _mm_nn([oc], [whole("attn_out")], h2, name="attn_out")
    (dh4, dh4b, d_final, loss_part), ffn1 = ffn_fwd(h3, 1, head=(target, final_gain))
    dh3, dh3b, d_ffn1 = ffn_bwd(h3, ffn1, dh4, dh4b, 1)

    d_oc = _mm_nt(dh3b, whole("attn_out"), out_dtype=F32, name="attn_out_dx")
    grad_attn_out = _mm_tn([oc], dh3b, name="attn_out_dw")
    delta, d_ocb = _attn_merge_bwd(d_oc, oc, "attn_merge_bwd")
    du1, qkv_pieces = [], []
    for g, d in enumerate(ATTN_DILATIONS):
        dqkv = _attn_bwd(qkv_g[g], d_ocb[g], lse_all[g], delta[g], cos_g[g], sin_g[g], d, f"attn_bwd{g}")
        qkv_pieces.append(_mm_tn([dqkv], u1_g[g], name=f"attn_qkv_dw{g}"))
        du1.append(_mm_nn([dqkv], [qkv_parts(g)], None, name=f"attn_qkv_dx{g}"))
    grad_qkv = jnp.stack([p.reshape(3, ATTN_GROUP_WIDTH, D_MODEL) for p in qkv_pieces], axis=1).reshape(
        3 * ATTN_WIDTH, D_MODEL)
    publish("attn", {"qkv": grad_qkv, "attn_out": grad_attn_out})
    dh2, dh2b, d_mix1 = _rms_bwd(h2, g_mix[1], du1, dh3, "attn_norm_bwd", ATTN_DILATIONS)

    dh1, dh1b, d_ffn0 = ffn_bwd(h1, ffn0, dh2, dh2b, 0)

    d_og = _mm_nt(dh1b, whole("hgrn_out"), out_dtype=F32, name="hgrn_out_dx")
    grad_hgrn_out = _mm_tn([og], dh1b, name="hgrn_out_dw")
    dproj, d_lb, d_out_gain = _hgrn_bwd(proj, o_pre, d_og, states, lb, out_gain, "hgrn_bwd")
    publish("hgrn", {"hgrn_in": _mm_tn([dproj], u0, name="hgrn_in_dw"), "hgrn_out": grad_hgrn_out})
    dx, _, d_mix0 = _mm_nn([dproj], [whole("hgrn_in")], dh1, name="hgrn_in_dx", norm=(x, g_mix[0]))

    small = dict(norm_mix0=d_mix0, norm_mix1=d_mix1, norm_ffn0=d_ffn0, norm_ffn1=d_ffn1, lb=d_lb,
                 out_gain=d_out_gain, final=d_final, loss=loss_part)
    return dx, small


MESH_IDS = pl.DeviceIdType.MESH
HBM_SPEC = pl.BlockSpec(memory_space=pl.ANY)


N_PEERS = N_DEV - 1
PEER_OFFSETS = [(dx, dy, dc) for dx in (0, 1) for dy in (0, 1) for dc in (0, 1)][1:]


def _mesh_place():
    x, y, c = lax.axis_index("x"), lax.axis_index("y"), lax.axis_index("c")
    peers = []
    for dx, dy, dc in PEER_OFFSETS:
        px, py, pc = (1 - x if dx else x), (1 - y if dy else y), (1 - c if dc else c)
        peers.append(((px, py, pc), 4 * px + 2 * py + pc))
    return 4 * x + 2 * y + c, peers


def _gather_over_two_levels(src_refs, land_refs, send_sems, recv_sems):
    n = len(src_refs)
    x, y, c = lax.axis_index("x"), lax.axis_index("y"), lax.axis_index("c")
    me, sibling = (x, y, c), (x, y, 1 - c)
    chips = [(1 - x, y), (x, 1 - y), (1 - x, 1 - y)]

    def block(w, px, py, pc):
        return land_refs[w].at[4 * px + 2 * py + pc]

    def copy(w, k, owner, to, src=None):
        return pltpu.make_async_remote_copy(
            src_ref=block(w, *owner) if src is None else src, dst_ref=block(w, *owner),
            send_sem=send_sems.at[w * N_PEERS + k], recv_sem=recv_sems.at[w * N_PEERS + k],
            device_id=to, device_id_type=MESH_IDS)

    sent = []
    for w in range(n):
        sent.append(copy(w, 0, me, sibling, src=src_refs[w]))
        sent += [copy(w, 1 + j, me, (*chip, c), src=src_refs[w]) for j, chip in enumerate(chips)]
    for cp in sent:
        cp.start()
    for w in range(n):
        for j, chip in enumerate(chips):
            copy(w, 1 + j, (*chip, c), me).wait_recv()
            passed = copy(w, 4 + j, (*chip, c), sibling)
            passed.start()
            sent.append(passed)
    for w in range(n):
        copy(w, 0, sibling, me).wait_recv()
        for j, chip in enumerate(chips):
            copy(w, 4 + j, (*chip, 1 - c), me).wait_recv()
    for cp in sent:
        cp.wait_send()


def _exchange_launch(srcs, scatter, collective_id, name):
    n = len(srcs)
    src_refs = [jax.new_ref(s, memory_space=pltpu.MemorySpace.HBM) for s in srcs]
    land_refs = [jax.empty_ref(jax.ShapeDtypeStruct(s.shape if scatter else (N_DEV,) + s.shape, s.dtype),
                               memory_space=pltpu.MemorySpace.HBM) for s in srcs]

    @pl.kernel(mesh=plsc.ScalarSubcoreMesh(axis_name="sequencer", num_cores=1), name=name,
               scratch_types=(pltpu.SemaphoreType.DMA((n * N_PEERS,)), pltpu.SemaphoreType.DMA((n * N_PEERS,)),
                              pltpu.SemaphoreType.DMA((n,))),
               compiler_params=pltpu.CompilerParams(collective_id=collective_id))
    def launch(send_sems, recv_sems, local_sems):
        me, peers = _mesh_place()
        barrier = pltpu.get_barrier_semaphore()
        for peer, _ in peers:
            pl.semaphore_signal(barrier, inc=1, device_id=peer, device_id_type=MESH_IDS)
        pl.semaphore_wait(barrier, N_PEERS)
        own = [pltpu.make_async_copy(src_refs[w].at[me] if scatter else src_refs[w], land_refs[w].at[me],
                                     local_sems.at[w]) for w in range(n)]
        for cp in own:
            cp.start()
        if scatter:
            copies = [pltpu.make_async_remote_copy(
                src_ref=src_refs[w].at[pid], dst_ref=land_refs[w].at[me],
                send_sem=send_sems.at[w * N_PEERS + k], recv_sem=recv_sems.at[w * N_PEERS + k],
                device_id=peer, device_id_type=MESH_IDS) for w in range(n) for k, (peer, pid) in enumerate(peers)]
            for cp in copies:
                cp.start()
            for cp in copies:
                cp.wait()
        else:
            _gather_over_two_levels(src_refs, land_refs, send_sems, recv_sems)
        for cp in own:
            cp.wait()

    launch()
    return land_refs


def _gather_small(block, name):
    def body(in_ref, out_ref, send_sems, recv_sems, local_sem):
        me, peers = _mesh_place()
        own = pltpu.make_async_copy(in_ref, out_ref.at[me], local_sem)
        own.start()
        sends = [pltpu.make_async_remote_copy(
            src_ref=in_ref, dst_ref=out_ref.at[me], send_sem=send_sems.at[k], recv_sem=recv_sems.at[k],
            device_id=peer, device_id_type=MESH_IDS) for k, (peer, _) in enumerate(peers)]
        for cp in sends:
            cp.start()
        for cp in sends:
            cp.wait_recv()
        for cp in sends:
            cp.wait_send()
        own.wait()

    return pl.pallas_call(
        body, out_shape=jax.ShapeDtypeStruct((N_DEV,) + block.shape, block.dtype),
        in_specs=[HBM_SPEC], out_specs=HBM_SPEC,
        scratch_shapes=[pltpu.SemaphoreType.DMA((N_PEERS,)), pltpu.SemaphoreType.DMA((N_PEERS,)),
                        pltpu.SemaphoreType.DMA],
        name=name)(block)


def _sum_blocks(recv, name):
    rows = recv.shape[1]
    tr = _pick_tile(rows, 256, 16)

    def body(r_ref, g_ref):
        acc = r_ref[0].astype(F32)
        for j in range(1, N_DEV):
            acc = acc + r_ref[j].astype(F32)
        g_ref[...] = acc

    return pl.pallas_call(
        body, out_shape=jax.ShapeDtypeStruct((rows, D_MODEL), F32), grid=(rows // tr,),
        in_specs=[pl.BlockSpec((N_DEV, tr, D_MODEL), lambda i: (0, i, 0))],
        out_specs=pl.BlockSpec((tr, D_MODEL), lambda i: (i, 0)),
        compiler_params=_params("parallel"), name=name)(recv)


def _adamw_math(w, g, m, v):
    m_new = ADAM_B1 * m + (1.0 - ADAM_B1) * g
    v_new = ADAM_B2 * v + (1.0 - ADAM_B2) * (g * g)
    m_hat = m_new / (1.0 - ADAM_B1 ** ADAM_STEP)
    v_hat = v_new / (1.0 - ADAM_B2 ** ADAM_STEP)
    delta = -ADAM_LR * (m_hat / (jnp.sqrt(v_hat) + ADAM_EPS) + ADAM_WD * w)
    return delta, m_new, v_new


def _adamw(w, g, m, v, layer, others, name):
    _, rows, cols = w.shape
    tr = _pick_tile(rows, 256, 8)

    def body(w_ref, g_ref, m_ref, v_ref, *refs):
        go_ref, d_ref, mo_ref, vo_ref = refs[-4:]
        gv = g_ref[...]
        go_ref[...] = gv
        d_ref[...], mo_ref[...], vo_ref[...] = _adamw_math(w_ref[...], gv, m_ref[...], v_ref[...])

    one = pl.BlockSpec((None, tr, cols), lambda i: (layer, i, 0))
    in_specs = [one, pl.BlockSpec((tr, cols), lambda i: (i, 0)), one, one]
    args = [w, g, m, v]
    if others is not None:
        in_specs += [HBM_SPEC] * 4
        args += list(others)
    return pl.pallas_call(
        body, out_shape=(jax.ShapeDtypeStruct(w.shape, F32),) * 4, grid=(rows // tr,),
        in_specs=in_specs, out_specs=(one,) * 4,
        input_output_aliases={} if others is None else {4 + i: i for i in range(4)},
        compiler_params=_params("parallel"), name=name)(*args)


ROW_MIX, ROW_FFN, ROW_LB, ROW_OUT_GAIN, ROW_FINAL = 0, 2, 4, 7, 8
PART_MIX, PART_FFN, PART_LB, PART_OUT_GAIN, PART_FINAL, PART_LOSS = 0, 2, 4, 5, 6, 7


def _small_update(parts_all, w, m, v, name):
    def body(p_ref, w_ref, m_ref, v_ref, g_ref, d_ref, mo_ref, vo_ref, loss_ref):
        def total(row, n=1):
            tot = p_ref[0, row:row + n, :]
            for j in range(1, N_DEV):
                tot = tot + p_ref[j, row:row + n, :]
            return tot

        logits = [w_ref[ROW_LB + i:ROW_LB + i + 1, :] for i in range(3)]
        mx = jnp.maximum(jnp.maximum(logits[0], logits[1]), logits[2])
        ex = [jnp.exp(l - mx) for l in logits]
        den = ex[0] + ex[1] + ex[2]
        prob = [e / den for e in ex]
        d_lb = total(PART_LB)
        g_ref[...] = jnp.zeros_like(g_ref)
        g_ref[ROW_MIX:ROW_MIX + 2, :] = total(PART_MIX, 2)
        g_ref[ROW_FFN:ROW_FFN + 2, :] = total(PART_FFN, 2)
        for i in range(3):
            g_ref[ROW_LB + i:ROW_LB + i + 1, :] = prob[i] * ((d_lb if i == 0 else 0.0) - prob[0] * d_lb)
        g_ref[ROW_OUT_GAIN:ROW_OUT_GAIN + 1, :] = total(PART_OUT_GAIN)
        g_ref[ROW_FINAL:ROW_FINAL + 1, :] = total(PART_FINAL)
        d_ref[...], mo_ref[...], vo_ref[...] = _adamw_math(w_ref[...], g_ref[...], m_ref[...], v_ref[...])
        loss_ref[...] = jnp.sum(total(PART_LOSS), axis=-1, keepdims=True)

    packed = jax.ShapeDtypeStruct((16, D_MODEL), F32)
    return pl.pallas_call(
        body, out_shape=(packed, packed, packed, packed, jax.ShapeDtypeStruct((1, 1), F32)),
        compiler_params=pltpu.CompilerParams(vmem_limit_bytes=VMEM_LIMIT), name=name)(parts_all, w, m, v)


def _pack_small(norm_mix, norm_ffn, lb_logits, out_gain, final):
    pad = jnp.zeros((1, D_MODEL - HGRN_DIM), F32)
    return jnp.concatenate([norm_mix, norm_ffn, lb_logits, jnp.concatenate([out_gain, pad], axis=1),
                            final.reshape(1, D_MODEL), jnp.zeros((16 - ROW_FINAL - 1, D_MODEL), F32)], axis=0)


def _unpack_small(p):
    return (p[ROW_MIX:ROW_MIX + 2], p[ROW_FFN:ROW_FFN + 2], p[ROW_LB:ROW_LB + 3],
            p[ROW_OUT_GAIN:ROW_OUT_GAIN + 1, :HGRN_DIM], p[ROW_FINAL])


def _lower_bound(lb_logits, name):
    def body(l_ref, o_ref):
        logits = [l_ref[i:i + 1, :] for i in range(3)]
        mx = jnp.maximum(jnp.maximum(logits[0], logits[1]), logits[2])
        ex = [jnp.exp(l - mx) for l in logits]
        o_ref[...] = ex[0] / (ex[0] + ex[1] + ex[2])

    return pl.pallas_call(body, out_shape=jax.ShapeDtypeStruct((1, D_MODEL), F32), name=name)(lb_logits)


def kernel(x, norm_mix, norm_ffn, hgrn_w_in, hgrn_lb_logits, hgrn_out_norm, hgrn_w_out, attn_w_qkv, attn_w_out, ffn_w_in, ffn_w_down, final_norm, loss_target, m_norm_mix, m_norm_ffn, m_hgrn_w_in, m_hgrn_lb_logits, m_hgrn_out_norm, m_hgrn_w_out, m_attn_w_qkv, m_attn_w_out, m_ffn_w_in, m_ffn_w_down, m_final_norm, v_norm_mix, v_norm_ffn, v_hgrn_w_in, v_hgrn_lb_logits, v_hgrn_out_norm, v_hgrn_w_out, v_attn_w_qkv, v_attn_w_out, v_ffn_w_in, v_ffn_w_down, v_final_norm):
    col_sharded = {"hgrn_in": hgrn_w_in[0], "qkv": attn_w_qkv[0], "ffn_in0": ffn_w_in[0], "ffn_in1": ffn_w_in[1]}
    row_sharded = {"hgrn_out": hgrn_w_out[0], "attn_out": attn_w_out[0], "ffn_down0": ffn_w_down[0],
                   "ffn_down1": ffn_w_down[1]}
    gathering = {}
    for gi, (group, names) in enumerate(WEIGHT_GROUPS.items()):
        shards = [(col_sharded[n].T if n in col_sharded else row_sharded[n]).astype(BF16) for n in names]
        gathering[group] = _exchange_launch(shards, False, 1 + gi, f"weights_gather_{group}")

    def fetch(group):
        return {n: land[...].reshape(-1, D_MODEL) for n, land in zip(WEIGHT_GROUPS[group], gathering[group])}

    in_flight = {}

    def publish(group, grads):
        names = WEIGHT_GROUPS[group]
        parts = [grads[n].reshape(N_DEV, -1, D_MODEL) for n in names]
        in_flight[group] = _exchange_launch(parts, True, 1 + len(WEIGHT_GROUPS) + list(WEIGHT_GROUPS).index(group),
                                            f"grads_send_{group}")

    lb = _lower_bound(hgrn_lb_logits, "hgrn_lower_bound")
    grad_x, small = _local_step(x[0], loss_target[0], norm_mix, norm_ffn, lb, hgrn_out_norm,
                                final_norm.reshape(1, D_MODEL), fetch, publish)

    pad = jnp.zeros((1, D_MODEL - HGRN_DIM), F32)
    small_part = jnp.concatenate(
        [small["norm_mix0"], small["norm_mix1"], small["norm_ffn0"], small["norm_ffn1"], small["lb"],
         jnp.concatenate([small["out_gain"], pad], axis=1), small["final"], small["loss"]], axis=0)
    small_all = _gather_small(small_part, "small_grads_gather")
    received = {}
    for group in ("ffn1", "attn", "ffn0", "hgrn"):
        received.update(zip(WEIGHT_GROUPS[group], [land[...] for land in in_flight[group]]))

    masters = {"hgrn_w_in": (hgrn_w_in, m_hgrn_w_in, v_hgrn_w_in, ("hgrn_in",)),
               "hgrn_w_out": (hgrn_w_out, m_hgrn_w_out, v_hgrn_w_out, ("hgrn_out",)),
               "attn_w_qkv": (attn_w_qkv, m_attn_w_qkv, v_attn_w_qkv, ("qkv",)),
               "attn_w_out": (attn_w_out, m_attn_w_out, v_attn_w_out, ("attn_out",)),
               "ffn_w_in": (ffn_w_in, m_ffn_w_in, v_ffn_w_in, ("ffn_in0", "ffn_in1")),
               "ffn_w_down": (ffn_w_down, m_ffn_w_down, v_ffn_w_down, ("ffn_down0", "ffn_down1"))}
    big = {}
    for param, (wv, mv, vv, names) in masters.items():
        outs = None
        for layer, n in enumerate(names):
            g = _sum_blocks(received[n], f"{n}_grad_sum")
            outs = _adamw(wv, g.T if n in col_sharded else g, mv, vv, layer, outs, f"{n}_adamw")
        big[param] = list(outs)

    w_small = _pack_small(norm_mix, norm_ffn, hgrn_lb_logits, hgrn_out_norm, final_norm)
    m_small = _pack_small(m_norm_mix, m_norm_ffn, m_hgrn_lb_logits, m_hgrn_out_norm, m_final_norm)
    v_small = _pack_small(v_norm_mix, v_norm_ffn, v_hgrn_lb_logits, v_hgrn_out_norm, v_final_norm)
    g_s, d_s, m_s, v_s, loss = _small_update(small_all, w_small, m_small, v_small, "small_update")
    small_out = [_unpack_small(t) for t in (g_s, d_s, m_s, v_s)]

    def group(i):
        s = small_out[i]
        return (s[0], s[1], big["hgrn_w_in"][i], s[2], s[3], big["hgrn_w_out"][i], big["attn_w_qkv"][i],
                big["attn_w_out"][i], big["ffn_w_in"][i], big["ffn_w_down"][i], s[4])

    return (loss.reshape(()), grad_x[None], *group(0), *group(1), *group(2), *group(3))
```

```python
import functools

import jax
import jax.numpy as jnp
from jax import lax
from jax.experimental import pallas as pl
from jax.experimental.pallas import tpu as pltpu
from jax.experimental.pallas import tpu_sc as plsc

F32 = jnp.float32
BF16 = jnp.bfloat16

D_MODEL = 1024
N_DEV = 8
NORM_EPS = 1e-6

HGRN_HEADS = 8
HGRN_DIM = 128
HGRN_CHUNK = 64
HGRN_STEP_CHUNKS = 4
HGRN_FWD_STEP_CHUNKS = 8
HGRN_EXP_CLAMP = 60.0

ATTN_DIM = 128
ATTN_BLOCK = 128
ATTN_GROUP_HEADS = 4
ATTN_GROUP_WIDTH = ATTN_GROUP_HEADS * ATTN_DIM
ATTN_DILATIONS = (1, 4, 16)
ATTN_WIDTH = 3 * ATTN_GROUP_WIDTH
ROPE_THETA = 10000.0
NEG_BIG = -1e30

D_FF = 2816

ADAM_LR = 0.001
ADAM_B1 = 0.9
ADAM_B2 = 0.999
ADAM_EPS = 1e-08
ADAM_WD = 0.01
ADAM_STEP = 10

VMEM_LIMIT = 48 * 1024 * 1024

NT = (((1,), (1,)), ((), ()))
NN = (((1,), (0,)), ((), ()))
TN = (((0,), (0,)), ((), ()))


def _dot(a, b, dims):
    return lax.dot_general(a, b, dims, preferred_element_type=F32)


def _params(*sem):
    return pltpu.CompilerParams(dimension_semantics=sem, vmem_limit_bytes=VMEM_LIMIT)


def _pick_tile(n, cap, mult):
    best = None
    for t in range(mult, min(n, cap) + 1, mult):
        if n % t == 0:
            best = t
    assert best is not None, (n, cap, mult)
    return best


def _sigmoid(x):
    return 0.5 * jnp.tanh(0.5 * x) + 0.5


ROW_TILE = 512
COL_CHUNK = 512
GRAD_TILE = 256


def _whole(shape, index_map):
    return pl.BlockSpec(shape, index_map, pipeline_mode=pl.Buffered(1))


def _part_specs(parts, n_cols):
    return [_whole((rows, n_cols), functools.partial(lambda i, b: (b, 0), b=blk)) for _, rows, blk in parts]


def _mm_nt(a, w_parts, *, out_dtype, name, rope=None):
    M, K = a.shape
    tm = _pick_tile(M, ROW_TILE, 16)
    widths = [rows for _, rows, _ in w_parts]
    n_parts = len(w_parts)

    def body(*refs):
        a_ref, w_refs, o_ref = refs[0], refs[1:1 + n_parts], refs[-1]
        av = a_ref[...]
        off = 0
        for p, w_ref in enumerate(w_refs):
            for c0 in range(0, widths[p], COL_CHUNK):
                cw = min(COL_CHUNK, widths[p] - c0)
                acc = _dot(av, w_ref[c0:c0 + cw, :], NT)
                if rope is not None and p < rope[2]:
                    cos, sin = refs[1 + n_parts][...], refs[2 + n_parts][...]
                    for h0 in range(0, cw, ATTN_DIM):
                        xh = acc[:, h0:h0 + ATTN_DIM]
                        rot = pltpu.roll(xh, ATTN_DIM // 2, 1)
                        o_ref[:, off + c0 + h0:off + c0 + h0 + ATTN_DIM] = (xh * cos + rot * sin).astype(out_dtype)
                else:
                    o_ref[:, off + c0:off + c0 + cw] = acc.astype(out_dtype)
            off += widths[p]

    in_specs = [pl.BlockSpec((tm, K), lambda i: (i, 0))] + _part_specs(w_parts, K)
    args = [a] + [w for w, _, _ in w_parts]
    if rope is not None:
        in_specs += [pl.BlockSpec((tm, ATTN_DIM), lambda i: (i, 0))] * 2
        args += [rope[0], rope[1]]
    return pl.pallas_call(
        body, out_shape=jax.ShapeDtypeStruct((M, sum(widths)), out_dtype), grid=(M // tm,),
        in_specs=in_specs, out_specs=pl.BlockSpec((tm, sum(widths)), lambda i: (i, 0)),
        compiler_params=_params("parallel"), name=name)(*args)


def _mm_nn(a_list, w_parts_list, resid, *, name, norm=None, head=None):
    M = a_list[0].shape[0]
    tm = _pick_tile(M, ROW_TILE, 16)
    n_a = len(a_list)
    flat_parts = [p for parts in w_parts_list for p in parts]
    extra = norm if norm is not None else head
    n_in = n_a + len(flat_parts) + (1 if resid is not None else 0) + (2 if extra is not None else 0)

    def body(*refs):
        a_refs, w_refs = refs[:n_a], refs[n_a:n_a + len(flat_parts)]

        def product(rows):
            acc = None
            wi = 0
            for a_ref, parts in zip(a_refs, w_parts_list):
                off = 0
                for _, k, _ in parts:
                    term = _dot(a_ref[rows, off:off + k], w_refs[wi][...], NN)
                    acc = term if acc is None else acc + term
                    off += k
                    wi += 1
            return acc

        if extra is None:
            acc = product(slice(None))
            if resid is not None:
                acc = acc + refs[n_in - 1][...]
            refs[n_in][...] = acc
            return

        @pl.when(pl.program_id(0) == 0)
        def _():
            for acc_ref in refs[n_in + 2:]:
                acc_ref[...] = jnp.zeros_like(acc_ref)

        for r0 in range(0, tm, tm // 2):
            rows = slice(r0, r0 + tm // 2)
            acc = product(rows)
            if head is not None:
                _loss_head_math(acc + refs[n_in - 3][rows, :], rows, refs[n_in - 2], refs[n_in - 1],
                                *refs[n_in:n_in + 4])
                continue
            dres_ref, x_ref, g_ref = refs[n_in - 3:n_in]
            dx_ref, dxb_ref, dg_ref = refs[n_in:n_in + 3]
            xv = x_ref[rows, :]
            rstd = lax.rsqrt(jnp.mean(xv * xv, axis=-1, keepdims=True) + NORM_EPS)
            n = xv * rstd
            dg_ref[...] += jnp.sum(acc * n, axis=0, keepdims=True)
            dn = acc * g_ref[...]
            dx = dres_ref[rows, :] + rstd * (dn - n * jnp.mean(dn * n, axis=-1, keepdims=True))
            dx_ref[rows, :] = dx
            dxb_ref[rows, :] = dx.astype(BF16)

    row = pl.BlockSpec((tm, D_MODEL), lambda i: (i, 0))
    vec = pl.BlockSpec((1, D_MODEL), lambda i: (0, 0))
    in_specs = [pl.BlockSpec((tm, a.shape[1]), lambda i: (i, 0)) for a in a_list] + _part_specs(flat_parts, D_MODEL)
    args = list(a_list) + [w for w, _, _ in flat_parts]
    if resid is not None:
        in_specs.append(row)
        args.append(resid)
    if extra is None:
        return pl.pallas_call(
            body, out_shape=jax.ShapeDtypeStruct((M, D_MODEL), F32), grid=(M // tm,),
            in_specs=in_specs, out_specs=row, compiler_params=_params("parallel"), name=name)(*args)
    assert resid is not None
    out_shape = [jax.ShapeDtypeStruct((M, D_MODEL), F32), jax.ShapeDtypeStruct((M, D_MODEL), BF16),
                 jax.ShapeDtypeStruct((1, D_MODEL), F32)]
    out_specs = [row, row, vec]
    if head is not None:
        out_shape.append(jax.ShapeDtypeStruct((1, D_MODEL), F32))
        out_specs.append(vec)
    return pl.pallas_call(
        body, out_shape=out_shape, grid=(M // tm,), in_specs=in_specs + [row, vec], out_specs=out_specs,
        compiler_params=_params("arbitrary"), name=name)(*args, extra[0], extra[1])


def _mm_tn(a_list, b, *, name):
    T = a_list[0].shape[0]
    N = b.shape[1]
    tr = GRAD_TILE
    tiles = [a.shape[1] // tr for a in a_list]
    starts = [sum(tiles[:i]) for i in range(len(tiles))]

    def body(*refs):
        a_refs, b_ref, o_ref = refs[:len(a_list)], refs[len(a_list)], refs[-1]
        r = pl.program_id(0)
        for a_ref, first, count in zip(a_refs, starts, tiles):
            @pl.when(jnp.logical_and(r >= first, r < first + count))
            def _():
                o_ref[...] = _dot(a_ref[...], b_ref[...], TN).astype(BF16)

    in_specs = [pl.BlockSpec((T, tr), functools.partial(lambda r, first, count: (0, jnp.clip(r - first, 0, count - 1)),
                                                        first=first, count=count))
                for first, count in zip(starts, tiles)]
    in_specs.append(_whole((T, N), lambda r: (0, 0)))
    return pl.pallas_call(
        body, out_shape=jax.ShapeDtypeStruct((sum(tiles) * tr, N), BF16), grid=(sum(tiles),),
        in_specs=in_specs, out_specs=pl.BlockSpec((tr, N), lambda r: (r, 0)),
        compiler_params=_params("parallel"), name=name)(*a_list, b)


def _rms_fwd(x, gain, name):
    T = x.shape[0]
    tm = _pick_tile(T, 512, 16)

    def body(x_ref, g_ref, u_ref):
        xv = x_ref[...]
        rstd = lax.rsqrt(jnp.mean(xv * xv, axis=-1, keepdims=True) + NORM_EPS)
        u_ref[...] = (xv * rstd * g_ref[...]).astype(BF16)

    return pl.pallas_call(
        body, out_shape=jax.ShapeDtypeStruct((T, D_MODEL), BF16), grid=(T // tm,),
        in_specs=[pl.BlockSpec((tm, D_MODEL), lambda i: (i, 0)), pl.BlockSpec((1, D_MODEL), lambda i: (0, 0))],
        out_specs=pl.BlockSpec((tm, D_MODEL), lambda i: (i, 0)),
        compiler_params=_params("parallel"), name=name)(x, gain)


def _rms_bwd(x, gain, dus, dres, name, dilations=(1,)):
    T = x.shape[0]
    tm = _pick_tile(T, PERM_TILE, 16 * max(dilations))
    n_du = len(dus)

    def body(x_ref, g_ref, *refs):
        du_refs, dres_ref = refs[:n_du], refs[n_du]
        dx_ref, dxb_ref, dg_ref, du_scr = refs[n_du + 1:]

        @pl.when(pl.program_id(0) == 0)
        def _():
            dg_ref[...] = jnp.zeros_like(dg_ref)

        if tuple(dilations) == (1,):
            du = du_refs[0][...]
        else:
            for i, (d, du_ref) in enumerate(zip(dilations, du_refs)):
                for j in range(D_MODEL // LANES):
                    lanes = slice(j * LANES, (j + 1) * LANES)
                    if d == 1:
                        du_scr[j] = du_ref[:, lanes] if i == 0 else du_scr[j] + du_ref[:, lanes]
                        continue
                    blk = du_scr.at[j]
                    for r in range(d):
                        rows = _class_rows(r, d, tm)
                        blk[rows, :] = du_ref[r, :, lanes] if i == 0 else blk[rows, :] + du_ref[r, :, lanes]
            du = jnp.concatenate([du_scr[j] for j in range(D_MODEL // LANES)], axis=1)
        xv = x_ref[...]
        rstd = lax.rsqrt(jnp.mean(xv * xv, axis=-1, keepdims=True) + NORM_EPS)
        n = xv * rstd
        dg_ref[...] += jnp.sum(du * n, axis=0, keepdims=True)
        dn = du * g_ref[...]
        dx = dres_ref[...] + rstd * (dn - n * jnp.mean(dn * n, axis=-1, keepdims=True))
        dx_ref[...] = dx
        dxb_ref[...] = dx.astype(BF16)

    row = pl.BlockSpec((tm, D_MODEL), lambda i: (i, 0))
    vec = pl.BlockSpec((1, D_MODEL), lambda i: (0, 0))
    return pl.pallas_call(
        body,
        out_shape=(jax.ShapeDtypeStruct((T, D_MODEL), F32), jax.ShapeDtypeStruct((T, D_MODEL), BF16),
                   jax.ShapeDtypeStruct((1, D_MODEL), F32)),
        grid=(T // tm,), in_specs=[row, vec] + [_residue_spec(d, tm, D_MODEL) for d in dilations] + [row],
        out_specs=(row, row, vec), scratch_shapes=[pltpu.VMEM((D_MODEL // LANES, tm, LANES), F32)],
        compiler_params=_params("arbitrary"), name=name)(
            x, gain, *[_residue_view(du, d) for du, d in zip(dus, dilations)], dres)


def _loss_head_math(hv, rows, t_ref, g_ref, dh_ref, dhb_ref, dg_ref, loss_ref):
    inv_f = 1.0 / D_MODEL
    g = g_ref[...]
    rstd = lax.rsqrt(jnp.mean(hv * hv, axis=-1, keepdims=True) + NORM_EPS)
    n = hv * rstd
    err = n * g - t_ref[rows, :]
    loss_ref[...] += (0.5 * inv_f) * jnp.sum(err * err, axis=0, keepdims=True)
    dy = err * inv_f
    dg_ref[...] += jnp.sum(dy * n, axis=0, keepdims=True)
    dn = dy * g
    dh = rstd * (dn - n * jnp.mean(dn * n, axis=-1, keepdims=True))
    dh_ref[rows, :] = dh
    dhb_ref[rows, :] = dh.astype(BF16)


FFN_TILE = 256


def _ffn_in(h, gain, w_in, name):
    T = h.shape[0]
    tm = _pick_tile(T, ROW_TILE, 16)

    def body(h_ref, g_ref, w_ref, n_ref, gate_ref, up_ref, a_ref):
        hv = h_ref[...]
        rstd = lax.rsqrt(jnp.mean(hv * hv, axis=-1, keepdims=True) + NORM_EPS)
        n = (hv * rstd * g_ref[...]).astype(BF16)
        n_ref[...] = n
        for c0 in range(0, D_FF, FFN_TILE):
            cols = slice(c0, c0 + FFN_TILE)
            gate = _dot(n, w_ref[c0:c0 + FFN_TILE, :], NT)
            up = _dot(n, w_ref[D_FF + c0:D_FF + c0 + FFN_TILE, :], NT)
            gate_ref[:, cols] = gate.astype(BF16)
            up_ref[:, cols] = up.astype(BF16)
            a_ref[:, cols] = (gate * _sigmoid(gate) * up).astype(BF16)

    row = pl.BlockSpec((tm, D_MODEL), lambda i: (i, 0))
    wide = pl.BlockSpec((tm, D_FF), lambda i: (i, 0))
    wide_shape = jax.ShapeDtypeStruct((T, D_FF), BF16)
    return pl.pallas_call(
        body, out_shape=(jax.ShapeDtypeStruct((T, D_MODEL), BF16), wide_shape, wide_shape, wide_shape),
        grid=(T // tm,),
        in_specs=[row, pl.BlockSpec((1, D_MODEL), lambda i: (0, 0)), _whole((2 * D_FF, D_MODEL), lambda i: (0, 0))],
        out_specs=(row, wide, wide, wide), compiler_params=_params("parallel"), name=name)(h, gain, w_in)


def _ffn_down_dx(dhb, w_down, gate, up, name):
    T = dhb.shape[0]
    tm = _pick_tile(T, ROW_TILE, 16)

    def body(dh_ref, w_ref, gate_ref, up_ref, dgate_ref, dup_ref):
        dh = dh_ref[...]
        for c0 in range(0, D_FF, FFN_TILE):
            cols = slice(c0, c0 + FFN_TILE)
            da = _dot(dh, w_ref[c0:c0 + FFN_TILE, :], NT).astype(BF16)
            gate = gate_ref[:, cols]
            sg = (0.5 * jnp.tanh(0.5 * jnp.abs(gate)) + 0.5) * jnp.exp(jnp.minimum(gate, 0.0))
            silu = gate * sg
            dgate_ref[:, cols] = da * up_ref[:, cols] * (sg + silu * (1.0 - sg))
            dup_ref[:, cols] = da * silu

    wide = pl.BlockSpec((tm, D_FF), lambda i: (i, 0))
    wide_shape = jax.ShapeDtypeStruct((T, D_FF), BF16)
    return pl.pallas_call(
        body, out_shape=(wide_shape, wide_shape), grid=(T // tm,),
        in_specs=[pl.BlockSpec((tm, D_MODEL), lambda i: (i, 0)), _whole((D_FF, D_MODEL), lambda i: (0, 0)), wide, wide],
        out_specs=(wide, wide), compiler_params=_params("parallel"), name=name)(dhb, w_down, gate, up)


def _tri(n, lower):
    r = lax.broadcasted_iota(jnp.int32, (n, n), 0)
    c = lax.broadcasted_iota(jnp.int32, (n, n), 1)
    return (c <= r) if lower else (c >= r)


def _running_sum(x, lower):
    tri = _tri(x.shape[0], lower).astype(BF16)
    hi = x.astype(BF16)
    rest = x - hi.astype(F32)
    mid = rest.astype(BF16)
    lo = (rest - mid.astype(F32)).astype(BF16)
    return _dot(tri, hi, NN) + _dot(tri, mid, NN) + _dot(tri, lo, NN)


def _hgrn_gates(q_raw, f_raw, lb):
    C = q_raw.shape[0]
    sig_f = _sigmoid(f_raw)
    forget = lb + (1.0 - lb) * sig_f
    key = 1.0 - forget
    log_f = jnp.log(forget)
    b = _running_sum(log_f, True)
    first_half = lax.broadcasted_iota(jnp.int32, log_f.shape, 0) < C // 2
    r = jnp.sum(jnp.where(first_half, log_f, 0.0), axis=0, keepdims=True)
    b_last = jnp.sum(log_f, axis=0, keepdims=True)
    e_a = jnp.exp(jnp.minimum(b - r, HGRN_EXP_CLAMP))
    e_b = jnp.exp(jnp.minimum(r - b, HGRN_EXP_CLAMP))
    e_q = jnp.exp(b)
    e_k = jnp.exp(b_last - b)
    sig_q = _sigmoid(q_raw)
    query = q_raw * sig_q
    return dict(sig_f=sig_f, forget=forget, sig_q=sig_q, e_a=e_a, e_b=e_b, e_q=e_q, e_k=e_k,
                e_last=jnp.exp(b_last), q_a=query * e_a, k_b=key * e_b, q_hat=query * e_q, k_til=key * e_k)


def _hgrn_fwd(proj, lb, gain, name):
    T = proj.shape[0]
    C = HGRN_CHUNK
    CPS = HGRN_FWD_STEP_CHUNKS
    H, HD = HGRN_HEADS, HGRN_DIM

    def body(q_ref, f_ref, i_ref, g_ref, lb_ref, gain_ref, og_ref, o_ref, st_ref, s_scr):
        @pl.when(pl.program_id(0) == 0)
        def _():
            s_scr[...] = jnp.zeros_like(s_scr)

        causal = _tri(C, True)
        gain_v = gain_ref[...]
        heads = [slice(h * HD, (h + 1) * HD) for h in range(H)]
        s_t = [s_scr[h] for h in range(H)]
        for cc in range(CPS):
            rows = slice(cc * C, (cc + 1) * C)
            for h in range(H):
                st_ref[cc, h] = s_t[h]
            gt = _hgrn_gates(q_ref[rows, :], f_ref[rows, :], lb_ref[...])
            q_a, k_b = gt["q_a"].astype(BF16), gt["k_b"].astype(BF16)
            q_hat, k_til = gt["q_hat"].astype(BF16), gt["k_til"].astype(BF16)
            v = i_ref[rows, :].astype(BF16)
            p = [jnp.where(causal, _dot(q_a[:, sl], k_b[:, sl], NT), 0.0).astype(BF16) for sl in heads]
            o = [_dot(p[h], v[:, sl], NN) + _dot(q_hat[:, sl], s_t[h].astype(BF16), NT)
                 for h, sl in enumerate(heads)]
            s_t = [gt["e_last"][:, sl] * s_t[h] + _dot(v[:, sl], k_til[:, sl], TN) for h, sl in enumerate(heads)]
            for h, sl in enumerate(heads):
                o_ref[rows, sl] = o[h]
                rstd = lax.rsqrt(jnp.mean(o[h] * o[h], axis=-1, keepdims=True) + NORM_EPS)
                g_raw = g_ref[rows, sl]
                og_ref[rows, sl] = (o[h] * rstd * gain_v * (g_raw * _sigmoid(g_raw))).astype(BF16)
        for h in range(H):
            s_scr[h] = s_t[h]

    col = lambda j: pl.BlockSpec((CPS * C, D_MODEL), lambda c: (c, j))
    row = pl.BlockSpec((CPS * C, D_MODEL), lambda c: (c, 0))
    return pl.pallas_call(
        body,
        out_shape=(jax.ShapeDtypeStruct((T, D_MODEL), BF16), jax.ShapeDtypeStruct((T, D_MODEL), F32),
                   jax.ShapeDtypeStruct((T // C, H, HD, HD), F32)),
        grid=(T // (CPS * C),),
        in_specs=[col(0), col(1), col(2), col(3), pl.BlockSpec((1, D_MODEL), lambda c: (0, 0)),
                  pl.BlockSpec((1, HD), lambda c: (0, 0))],
        out_specs=(row, row, pl.BlockSpec((CPS, H, HD, HD), lambda c: (c, 0, 0, 0))),
        scratch_shapes=[pltpu.VMEM((H, HD, HD), F32)],
        compiler_params=_params("arbitrary"), name=name)(proj, proj, proj, proj, lb, gain)


def _hgrn_bwd(proj, o_pre, d_og, states, lb, gain, name):
    T = proj.shape[0]
    C = HGRN_CHUNK
    CPS = HGRN_STEP_CHUNKS
    H, HD = HGRN_HEADS, HGRN_DIM
    NC = T // (CPS * C)

    def body(q_ref, f_ref, i_ref, g_ref, o_ref, dog_ref, st_ref, lb_ref, gain_ref,
             dproj_ref, dlb_ref, dgain_ref, ds_scr, dq_all, dk_all, db_all):
        @pl.when(pl.program_id(0) == 0)
        def _():
            ds_scr[...] = jnp.zeros_like(ds_scr)
            dlb_ref[...] = jnp.zeros_like(dlb_ref)
            dgain_ref[...] = jnp.zeros_like(dgain_ref)

        lbv = lb_ref[...]
        causal = _tri(C, True)
        last_row = lax.broadcasted_iota(jnp.int32, (C, HD), 0) == C - 1
        gain_v = gain_ref[...]
        heads = [slice(h * HD, (h + 1) * HD) for h in range(H)]
        hs = range(H)
        ds_t = [ds_scr[h] for h in hs]
        dgain = None
        for cc in reversed(range(CPS)):
            rows = slice(cc * C, (cc + 1) * C)
            dq_scr, dk_scr, db_scr = dq_all.at[cc], dk_all.at[cc], db_all.at[cc]
            q_raw = q_ref[rows, :]
            gt = _hgrn_gates(q_raw, f_ref[rows, :], lbv)
            o = [o_ref[rows, sl] for sl in heads]
            rstd = [lax.rsqrt(jnp.mean(x * x, axis=-1, keepdims=True) + NORM_EPS) for x in o]
            n = [x * r for x, r in zip(o, rstd)]
            g_raw = [g_ref[rows, sl] for sl in heads]
            sg = [_sigmoid(x) for x in g_raw]
            d_out = [dog_ref[rows, sl] for sl in heads]
            dy = [d * (g * s) for d, g, s in zip(d_out, g_raw, sg)]
            dn = [x * gain_v for x in dy]
            do = [(rstd[h] * (dn[h] - n[h] * jnp.mean(dn[h] * n[h], axis=-1, keepdims=True))).astype(BF16) for h in hs]
            for h in hs:
                dgain = dy[h] * n[h] if dgain is None else dgain + dy[h] * n[h]
            for h, sl in enumerate(heads):
                dproj_ref[rows, 3 * D_MODEL + h * HD:3 * D_MODEL + (h + 1) * HD] = (
                    d_out[h] * n[h] * gain_v * (sg[h] * (1.0 + g_raw[h] * (1.0 - sg[h])))).astype(BF16)
            q_ab, k_bb = gt["q_a"].astype(BF16), gt["k_b"].astype(BF16)
            q_hb, k_tb = gt["q_hat"].astype(BF16), gt["k_til"].astype(BF16)
            v = i_ref[rows, :].astype(BF16)
            s_t = [st_ref[cc, h] for h in hs]
            ds_b = [x.astype(BF16) for x in ds_t]
            p = [jnp.where(causal, _dot(q_ab[:, sl], k_bb[:, sl], NT), 0.0).astype(BF16) for sl in heads]
            dp = [jnp.where(causal, _dot(do[h], v[:, sl], NT), 0.0).astype(BF16) for h, sl in enumerate(heads)]
            dv = [_dot(p[h], do[h], TN) + _dot(k_tb[:, sl], ds_b[h], NT) for h, sl in enumerate(heads)]
            dq_a = [_dot(dp[h], k_bb[:, sl], NN) for h, sl in enumerate(heads)]
            dk_b = [_dot(dp[h], q_ab[:, sl], TN) for h, sl in enumerate(heads)]
            dq_hat = [_dot(do[h], s_t[h].astype(BF16), NN) for h in hs]
            dk_til = [_dot(v[:, sl], ds_b[h], NN) for h, sl in enumerate(heads)]
            ds_new = [_dot(do[h], q_hb[:, sl], TN) + gt["e_last"][:, sl] * ds_t[h] for h, sl in enumerate(heads)]
            for h, sl in enumerate(heads):
                k_til = gt["k_til"][:, sl]
                db_last = jnp.sum(ds_t[h] * gt["e_last"][:, sl] * s_t[h], axis=0, keepdims=True) + jnp.sum(
                    dk_til[h] * k_til, axis=0, keepdims=True)
                dproj_ref[rows, 2 * D_MODEL + h * HD:2 * D_MODEL + (h + 1) * HD] = dv[h].astype(BF16)
                dq_scr[:, sl] = dq_a[h] * gt["e_a"][:, sl] + dq_hat[h] * gt["e_q"][:, sl]
                dk_scr[:, sl] = dk_b[h] * gt["e_b"][:, sl] + dk_til[h] * gt["e_k"][:, sl]
                db = (dq_a[h] * q_ab[:, sl].astype(F32) + dq_hat[h] * gt["q_hat"][:, sl]
                      - dk_b[h] * k_bb[:, sl].astype(F32) - dk_til[h] * k_til)
                db_scr[:, sl] = db + jnp.where(last_row, db_last, 0.0)
            dlogf = _running_sum(db_scr[...], False)
            sig_f, forget, sig_q = gt["sig_f"], gt["forget"], gt["sig_q"]
            dforget = dlogf / forget - dk_scr[...]
            dproj_ref[rows, D_MODEL:2 * D_MODEL] = (dforget * (1.0 - lbv) * sig_f * (1.0 - sig_f)).astype(BF16)
            dlb_ref[...] += jnp.sum(dforget * (1.0 - sig_f), axis=0, keepdims=True)
            dproj_ref[rows, 0:D_MODEL] = (dq_scr[...] * (sig_q * (1.0 + q_raw * (1.0 - sig_q)))).astype(BF16)
            ds_t = ds_new
        dgain_ref[...] += jnp.sum(dgain, axis=0, keepdims=True)
        for h in hs:
            ds_scr[h] = ds_t[h]

    col = lambda j: pl.BlockSpec((CPS * C, D_MODEL), lambda c: (NC - 1 - c, j))
    row = pl.BlockSpec((CPS * C, D_MODEL), lambda c: (NC - 1 - c, 0))
    return pl.pallas_call(
        body,
        out_shape=(jax.ShapeDtypeStruct((T, 4 * D_MODEL), BF16), jax.ShapeDtypeStruct((1, D_MODEL), F32),
                   jax.ShapeDtypeStruct((1, HD), F32)),
        grid=(NC,),
        in_specs=[col(0), col(1), col(2), col(3), row, row,
                  pl.BlockSpec((CPS, H, HD, HD), lambda c: (NC - 1 - c, 0, 0, 0)),
                  pl.BlockSpec((1, D_MODEL), lambda c: (0, 0)), pl.BlockSpec((1, HD), lambda c: (0, 0))],
        out_specs=(pl.BlockSpec((CPS * C, 4 * D_MODEL), lambda c: (NC - 1 - c, 0)),
                   pl.BlockSpec((1, D_MODEL), lambda c: (0, 0)), pl.BlockSpec((1, HD), lambda c: (0, 0))),
        scratch_shapes=[pltpu.VMEM((H, HD, HD), F32)] + [pltpu.VMEM((CPS, C, D_MODEL), F32)] * 3,
        compiler_params=_params("arbitrary"), name=name)(proj, proj, proj, proj, o_pre, d_og, states, lb, gain)


def _attn_masks():
    r = lax.broadcasted_iota(jnp.int32, (ATTN_BLOCK, ATTN_BLOCK), 0)
    c = lax.broadcasted_iota(jnp.int32, (ATTN_BLOCK, ATTN_BLOCK), 1)
    return c >= r, c <= r


def _attn_fwd(qkv, dilation, name):
    T = qkv.shape[0]
    nb = T // dilation // ATTN_BLOCK
    W = ATTN_GROUP_WIDTH
    B = ATTN_BLOCK
    scale = ATTN_DIM ** -0.5
    qb = 2 if nb % 2 == 0 else 1
    steps = nb // qb

    def body(q_ref, kp_ref, kc_ref, vp_ref, vc_ref, o_ref, lse_ref):
        no_prev = jnp.where(pl.program_id(1) > 0, 0.0, NEG_BIG)
        m_prev, m_cur = _attn_masks()
        ones = jnp.ones((B, ATTN_DIM), BF16)
        items = []
        for j in range(qb):
            for h in range(ATTN_GROUP_HEADS):
                sl = slice(h * ATTN_DIM, (h + 1) * ATTN_DIM)
                rows = slice(j * B, (j + 1) * B)
                if j == 0:
                    items.append((rows, sl, kp_ref[:, sl], vp_ref[:, sl], no_prev))
                else:
                    before = slice((j - 1) * B, j * B)
                    items.append((rows, sl, kc_ref[before, sl], vc_ref[before, sl], 0.0))
        s_p = [jnp.where(m_prev, _dot(q_ref[rows, sl], k_p, NT) * scale + bias, NEG_BIG)
               for rows, sl, k_p, _, bias in items]
        s_c = [jnp.where(m_cur, _dot(q_ref[rows, sl], kc_ref[rows, sl], NT) * scale, NEG_BIG)
               for rows, sl, _, _, _ in items]
        m = [jnp.max(jnp.maximum(a, b), axis=-1, keepdims=True) for a, b in zip(s_p, s_c)]
        p_p = [jnp.exp(a - mx).astype(BF16) for a, mx in zip(s_p, m)]
        p_c = [jnp.exp(b - mx).astype(BF16) for b, mx in zip(s_c, m)]
        l = [_dot(a, ones, NN) + _dot(b, ones, NN) for a, b in zip(p_p, p_c)]
        acc = [_dot(a, v_p, NN) + _dot(b, vc_ref[rows, sl], NN)
               for a, b, (rows, sl, _, v_p, _) in zip(p_p, p_c, items)]
        for (rows, sl, _, _, _), a, lv, mx in zip(items, acc, l, m):
            o_ref[rows, sl] = (a / lv).astype(BF16)
            lse_ref[rows, sl] = mx + jnp.log(lv)

    cur = lambda col: pl.BlockSpec((qb * B, W), lambda s, n: (s * steps + n, col))
    prev = lambda col: pl.BlockSpec((B, W), lambda s, n: (s * nb + jnp.maximum(qb * n - 1, 0), col))
    out = pl.BlockSpec((qb * B, W), lambda s, n: (s * steps + n, 0))
    return pl.pallas_call(
        body, out_shape=(jax.ShapeDtypeStruct((T, W), BF16), jax.ShapeDtypeStruct((T, W), F32)),
        grid=(dilation, steps),
        in_specs=[cur(0), prev(1), cur(1), prev(2), cur(2)],
        out_specs=(out, out), compiler_params=_params("parallel", "arbitrary"), name=name)(qkv, qkv, qkv, qkv, qkv)


def _attn_bwd(qkv, d_out, lse, delta, cos, sin, dilation, name):
    T = qkv.shape[0]
    nb = T // dilation // ATTN_BLOCK
    assert nb % 2 == 0, "an even number of 128-token blocks per residue class"
    pairs = nb // 2
    W = ATTN_GROUP_WIDTH
    B = ATTN_BLOCK
    scale = ATTN_DIM ** -0.5

    def unrope(x, cos_v, sin_v):
        return x * cos_v + pltpu.roll(x * sin_v, ATTN_DIM // 2, 1)

    def body(qa_ref, qb_ref, kpair_ref, kc_ref, vpair_ref, vc_ref, doa_ref, dob_ref, lsea_ref, lseb_ref,
             dla_ref, dlb_ref, cos_ref, sin_ref, out_ref, dq_scr, dk_scr, dv_scr):
        n = pl.program_id(1)

        @pl.when(n == 0)
        def _():
            dq_scr[...] = jnp.zeros_like(dq_scr)
            dk_scr[...] = jnp.zeros_like(dk_scr)
            dv_scr[...] = jnp.zeros_like(dv_scr)

        no_a = jnp.where(n > 0, 0.0, NEG_BIG)
        no_b = jnp.where(n < pairs, 0.0, NEG_BIG)
        m_prev, m_cur = _attn_masks()
        lo, hi = slice(0, B), slice(B, 2 * B)
        heads = [slice(h * ATTN_DIM, (h + 1) * ATTN_DIM) for h in range(ATTN_GROUP_HEADS)]
        flat = []
        for sl in heads:
            qa, qb = qa_ref[:, sl], qb_ref[:, sl]
            doa, dob = doa_ref[:, sl], dob_ref[:, sl]
            k0, k1, k2 = kpair_ref[lo, sl], kpair_ref[hi, sl], kc_ref[:, sl]
            v0, v1, v2 = vpair_ref[lo, sl], vpair_ref[hi, sl], vc_ref[:, sl]
            flat += [(qa, doa, lsea_ref[:, sl], dla_ref[:, sl], k0, v0, m_prev, no_a),
                     (qa, doa, lsea_ref[:, sl], dla_ref[:, sl], k1, v1, m_cur, no_a),
                     (qb, dob, lseb_ref[:, sl], dlb_ref[:, sl], k1, v1, m_prev, no_a + no_b),
                     (qb, dob, lseb_ref[:, sl], dlb_ref[:, sl], k2, v2, m_cur, no_b)]
        s = [_dot(q, k, NT) for q, _, _, _, k, _, _, _ in flat]
        dp = [_dot(do, v, NT) for _, do, _, _, _, v, _, _ in flat]
        p = [jnp.where(mask, jnp.exp(sv * scale - lse_v + bias), 0.0)
             for sv, (_, _, lse_v, _, _, _, mask, bias) in zip(s, flat)]
        ds = [(pv * (dpv - dl_v) * scale).astype(BF16) for pv, dpv, (_, _, _, dl_v, _, _, _, _) in zip(p, dp, flat)]
        p = [pv.astype(BF16) for pv in p]
        dq_part = [_dot(dsv, k, NN) for dsv, (_, _, _, _, k, _, _, _) in zip(ds, flat)]
        dk_part = [_dot(dsv, q, TN) for dsv, (q, _, _, _, _, _, _, _) in zip(ds, flat)]
        dv_part = [_dot(pv, do, TN) for pv, (_, do, _, _, _, _, _, _) in zip(p, flat)]
        cos_lo, sin_lo, cos_hi, sin_hi = cos_ref[lo, :], sin_ref[lo, :], cos_ref[hi, :], sin_ref[hi, :]
        for h, sl in enumerate(heads):
            a_prev, a_cur, b_prev, b_cur = range(4 * h, 4 * h + 4)
            kcol = slice(W + h * ATTN_DIM, W + (h + 1) * ATTN_DIM)
            vcol = slice(2 * W + h * ATTN_DIM, 2 * W + (h + 1) * ATTN_DIM)
            out_ref[lo, sl] = unrope(dq_scr[:, sl], cos_lo, sin_lo).astype(BF16)
            out_ref[hi, sl] = unrope(dq_part[a_prev] + dq_part[a_cur], cos_hi, sin_hi).astype(BF16)
            out_ref[lo, kcol] = unrope(dk_scr[:, sl] + dk_part[a_prev], cos_lo, sin_lo).astype(BF16)
            out_ref[hi, kcol] = unrope(dk_part[a_cur] + dk_part[b_prev], cos_hi, sin_hi).astype(BF16)
            out_ref[lo, vcol] = (dv_scr[:, sl] + dv_part[a_prev]).astype(BF16)
            out_ref[hi, vcol] = (dv_part[a_cur] + dv_part[b_prev]).astype(BF16)
            dq_scr[:, sl] = dq_part[b_prev] + dq_part[b_cur]
            dk_scr[:, sl] = dk_part[b_cur]
            dv_scr[:, sl] = dv_part[b_cur]

    def block_a(n):
        return jnp.maximum(2 * n - 1, 0)

    def block_b(n):
        return jnp.minimum(2 * n, nb - 1)

    def pair(n):
        return jnp.maximum(n - 1, 0)

    one_a = lambda col: pl.BlockSpec((B, W), lambda s, n: (s * nb + block_a(n), col))
    one_b = lambda col: pl.BlockSpec((B, W), lambda s, n: (s * nb + block_b(n), col))
    two = lambda col: pl.BlockSpec((2 * B, W), lambda s, n: (s * pairs + pair(n), col))
    tab = pl.BlockSpec((2 * B, ATTN_DIM), lambda s, n: (s * pairs + pair(n), 0))
    return pl.pallas_call(
        body, out_shape=jax.ShapeDtypeStruct((T, 3 * W), BF16), grid=(dilation, pairs + 1),
        in_specs=[one_a(0), one_b(0), two(1), one_b(1), two(2), one_b(2), one_a(0), one_b(0), one_a(0), one_b(0),
                  one_a(0), one_b(0), tab, tab],
        out_specs=pl.BlockSpec((2 * B, 3 * W), lambda s, n: (s * pairs + pair(n), 0)),
        scratch_shapes=[pltpu.VMEM((B, W), F32)] * 3,
        compiler_params=_params("parallel", "arbitrary"), name=name)(
            qkv, qkv, qkv, qkv, qkv, qkv, d_out, d_out, lse, lse, delta, delta, cos, sin)


PERM_TILE = 512
LANES = 128


def _residue_view(x, d):
    return x if d == 1 else x.reshape(d, x.shape[0] // d, x.shape[1])


def _residue_spec(d, tm, cols):
    if d == 1:
        return pl.BlockSpec((tm, cols), lambda i: (i, 0))
    return pl.BlockSpec((d, tm // d, cols), lambda i: (0, i, 0))


def _residue_shape(T, d, cols, dtype):
    return jax.ShapeDtypeStruct((T, cols) if d == 1 else (d, T // d, cols), dtype)


def _class_rows(r, d, tm):
    return pl.ds(r, tm // d, stride=d)


def _attn_norm(h, gain, name):
    T = h.shape[0]
    tm = _pick_tile(T, PERM_TILE, 16 * max(ATTN_DILATIONS))
    dils = ATTN_DILATIONS
    (base_cos, base_sin), (off_cos, off_sin), sign = _rope_parts(T, tm)

    def body(h_ref, g_ref, bc_ref, bs_ref, oc_ref, os_ref, sign_ref, *refs):
        u_refs, c_refs, s_refs, u_scr, c_scr, s_scr = refs[0:3], refs[3:6], refs[6:9], refs[9], refs[10], refs[11]
        hv = h_ref[...]
        rstd = lax.rsqrt(jnp.mean(hv * hv, axis=-1, keepdims=True) + NORM_EPS)
        u = hv * rstd * g_ref[...]
        for j in range(D_MODEL // LANES):
            u_scr[j] = u[:, j * LANES:(j + 1) * LANES]
        bc, bs, oc, osn = bc_ref[0], bs_ref[0], oc_ref[...], os_ref[...]
        c_scr[...] = bc * oc - bs * osn
        s_scr[...] = (bs * oc + bc * osn) * sign_ref[...]
        for d, u_ref, c_ref, s_ref in zip(dils, u_refs, c_refs, s_refs):
            if d == 1:
                u_ref[...] = u.astype(BF16)
                c_ref[...] = c_scr[...]
                s_ref[...] = s_scr[...]
                continue
            for r in range(d):
                rows = _class_rows(r, d, tm)
                for j in range(D_MODEL // LANES):
                    u_ref[r, :, j * LANES:(j + 1) * LANES] = u_scr.at[j][rows, :].astype(BF16)
                c_ref[r] = c_scr[rows, :]
                s_ref[r] = s_scr[rows, :]

    row = pl.BlockSpec((tm, D_MODEL), lambda i: (i, 0))
    base = pl.BlockSpec((1, 1, ATTN_DIM), lambda i: (i, 0, 0))
    off = pl.BlockSpec((tm, ATTN_DIM), lambda i: (0, 0))
    res = pl.pallas_call(
        body,
        out_shape=([_residue_shape(T, d, D_MODEL, BF16) for d in dils]
                   + [_residue_shape(T, d, ATTN_DIM, F32) for d in dils] * 2),
        grid=(T // tm,),
        in_specs=[row, pl.BlockSpec((1, D_MODEL), lambda i: (0, 0)), base, base, off, off,
                  pl.BlockSpec((1, ATTN_DIM), lambda i: (0, 0))],
        out_specs=([_residue_spec(d, tm, D_MODEL) for d in dils] + [_residue_spec(d, tm, ATTN_DIM) for d in dils] * 2),
        scratch_shapes=[pltpu.VMEM((D_MODEL // LANES, tm, LANES), F32), pltpu.VMEM((tm, ATTN_DIM), F32),
                        pltpu.VMEM((tm, ATTN_DIM), F32)],
        compiler_params=_params("parallel"), name=name)(h, gain, base_cos, base_sin, off_cos, off_sin, sign)
    flat = [r.reshape(T, r.shape[-1]) for r in res]
    return flat[0:3], flat[3:6], flat[6:9]


def _attn_merge_fwd(outs, lses, name):
    T = outs[0].shape[0]
    W = ATTN_GROUP_WIDTH
    tm = _pick_tile(T, PERM_TILE, 16 * max(ATTN_DILATIONS))
    dils = ATTN_DILATIONS

    def body(*refs):
        o_refs, l_refs, oc_ref, lse_refs = refs[0:3], refs[3:6], refs[6], refs[7:10]
        o_scr, l_scr, t_scr = refs[10:13]
        nh = ATTN_GROUP_HEADS
        for g, d in enumerate(dils):
            for j in range(nh):
                lanes = slice(j * LANES, (j + 1) * LANES)
                if d == 1:
                    o_scr[g * nh + j] = o_refs[g][:, lanes].astype(F32)
                    l_scr[g * nh + j] = l_refs[g][:, lanes]
                    continue
                for r in range(d):
                    rows = _class_rows(r, d, tm)
                    o_scr.at[g * nh + j][rows, :] = o_refs[g][r, :, lanes].astype(F32)
                    l_scr.at[g * nh + j][rows, :] = l_refs[g][r, :, lanes]
        for j in range(nh):
            lanes = slice(j * LANES, (j + 1) * LANES)
            ls = [l_scr[g * nh + j] for g in range(3)]
            m = jnp.maximum(jnp.maximum(ls[0], ls[1]), ls[2])
            tot = m + jnp.log(jnp.exp(ls[0] - m) + jnp.exp(ls[1] - m) + jnp.exp(ls[2] - m))
            t_scr[j] = tot
            for g, d in enumerate(dils):
                oc_ref[:, g * W + j * LANES:g * W + (j + 1) * LANES] = (
                    o_scr[g * nh + j] * jnp.exp(ls[g] - tot)).astype(BF16)
                if d == 1:
                    lse_refs[g][:, lanes] = tot
                    continue
                for r in range(d):
                    lse_refs[g][r, :, lanes] = t_scr.at[j][_class_rows(r, d, tm), :]

    in_blk = [_residue_spec(d, tm, W) for d in dils]
    n_blk = 3 * ATTN_GROUP_HEADS
    res = pl.pallas_call(
        body, out_shape=[jax.ShapeDtypeStruct((T, 3 * W), BF16)] + [_residue_shape(T, d, W, F32) for d in dils],
        grid=(T // tm,), in_specs=in_blk * 2,
        out_specs=[pl.BlockSpec((tm, 3 * W), lambda i: (i, 0))] + in_blk,
        scratch_shapes=[pltpu.VMEM((n_blk, tm, LANES), F32), pltpu.VMEM((n_blk, tm, LANES), F32),
                        pltpu.VMEM((ATTN_GROUP_HEADS, tm, LANES), F32)],
        compiler_params=_params("parallel"), name=name)(
            *[_residue_view(o, d) for o, d in zip(outs, dils)], *[_residue_view(l, d) for l, d in zip(lses, dils)])
    return res[0], [r.reshape(T, W) for r in res[1:]]


def _attn_merge_bwd(d_oc, oc, name):
    T = d_oc.shape[0]
    W = ATTN_GROUP_WIDTH
    tm = _pick_tile(T, PERM_TILE, 16 * max(ATTN_DILATIONS))
    dils = ATTN_DILATIONS

    def body(d_ref, o_ref, *refs):
        delta_refs, db_refs, dl_scr, d_scr = refs[0:3], refs[3:6], refs[6], refs[7]
        nh = ATTN_GROUP_HEADS
        for j in range(nh):
            tot = jnp.zeros((tm, 1), F32)
            for g in range(3):
                cols = slice(g * W + j * LANES, g * W + (j + 1) * LANES)
                d_blk = d_ref[:, cols]
                d_scr[g * nh + j] = d_blk
                tot = tot + jnp.sum(d_blk * o_ref[:, cols].astype(F32), axis=-1, keepdims=True)
            dl_scr[j] = jnp.broadcast_to(tot, (tm, LANES))
        for g, d in enumerate(dils):
            for j in range(nh):
                lanes = slice(j * LANES, (j + 1) * LANES)
                if d == 1:
                    delta_refs[g][:, lanes] = dl_scr[j]
                    db_refs[g][:, lanes] = d_scr[g * nh + j].astype(BF16)
                    continue
                for r in range(d):
                    rows = _class_rows(r, d, tm)
                    delta_refs[g][r, :, lanes] = dl_scr.at[j][rows, :]
                    db_refs[g][r, :, lanes] = d_scr.at[g * nh + j][rows, :].astype(BF16)

    wide = pl.BlockSpec((tm, 3 * W), lambda i: (i, 0))
    out_blk = [_residue_spec(d, tm, W) for d in dils]
    res = pl.pallas_call(
        body, out_shape=[_residue_shape(T, d, W, F32) for d in dils] + [_residue_shape(T, d, W, BF16) for d in dils],
        grid=(T // tm,), in_specs=[wide, wide], out_specs=out_blk * 2,
        scratch_shapes=[pltpu.VMEM((ATTN_GROUP_HEADS, tm, LANES), F32),
                        pltpu.VMEM((3 * ATTN_GROUP_HEADS, tm, LANES), F32)],
        compiler_params=_params("parallel"), name=name)(d_oc, oc)
    flat = [r.reshape(T, W) for r in res]
    return flat[0:3], flat[3:6]


def _rope_parts(T, tile):
    inv_freq = 1.0 / (ROPE_THETA ** (jnp.arange(0, ATTN_DIM, 2, dtype=F32) / ATTN_DIM))
    inv_freq = jnp.concatenate([inv_freq, inv_freq])[None, :]
    base = (jnp.arange(T // tile, dtype=F32) * tile)[:, None] * inv_freq
    off = jnp.arange(tile, dtype=F32)[:, None] * inv_freq
    sign = jnp.concatenate([-jnp.ones((1, ATTN_DIM // 2), F32), jnp.ones((1, ATTN_DIM // 2), F32)], axis=1)
    return (jnp.cos(base)[:, None, :], jnp.sin(base)[:, None, :]), (jnp.cos(off), jnp.sin(off)), sign


WEIGHT_GROUPS = {"hgrn": ("hgrn_in", "hgrn_out"), "ffn0": ("ffn_in0", "ffn_down0"),
                 "attn": ("qkv", "attn_out"), "ffn1": ("ffn_in1", "ffn_down1")}


def _local_step(x, target, norm_mix, norm_ffn, lb, out_gain, final_gain, fetch, publish):
    g_mix = [norm_mix[0:1], norm_mix[1:2]]
    g_ffn = [norm_ffn[0:1], norm_ffn[1:2]]
    w = {}

    def whole(name):
        return [(w[name], w[name].shape[0], 0)]

    def qkv_parts(g):
        return [(w["qkv"], ATTN_GROUP_WIDTH, 3 * j + g) for j in range(3)]

    def ffn_fwd(h, layer, head=None):
        w.update(fetch(f"ffn{layer}"))
        n, gate, up, a = _ffn_in(h, g_ffn[layer], w[f"ffn_in{layer}"], f"ffn{layer}_in")
        out = _mm_nn([a], [whole(f"ffn_down{layer}")], h, name=f"ffn{layer}_down", head=head)
        return out, (n, gate, up, a)

    def ffn_bwd(h, saved, dh, dhb, layer):
        n, gate, up, a = saved
        w_in = w[f"ffn_in{layer}"]
        dgate, dup = _ffn_down_dx(dhb, w[f"ffn_down{layer}"], gate, up, f"ffn{layer}_down_dx")
        grads = {f"ffn_down{layer}": _mm_tn([a], dhb, name=f"ffn{layer}_down_dw"),
                 f"ffn_in{layer}": _mm_tn([dgate, dup], n, name=f"ffn{layer}_in_dw")}
        publish(f"ffn{layer}", grads)
        return _mm_nn([dgate, dup], [[(w_in, D_FF, 0)], [(w_in, D_FF, 1)]], dh, name=f"ffn{layer}_in_dx",
                      norm=(h, g_ffn[layer]))

    u0 = _rms_fwd(x, g_mix[0], "hgrn_norm")
    w.update(fetch("hgrn"))
    proj = _mm_nt(u0, whole("hgrn_in"), out_dtype=F32, name="hgrn_in")
    og, o_pre, states = _hgrn_fwd(proj, lb, out_gain, "hgrn_fwd")
    h1 = _mm_nn([og], [whole("hgrn_out")], x, name="hgrn_out")
    h2, ffn0 = ffn_fwd(h1, 0)

    u1_g, cos_g, sin_g = _attn_norm(h2, g_mix[1], "attn_norm")
    w.update(fetch("attn"))
    qkv_g, outs, lses = [], [], []
    for g, d in enumerate(ATTN_DILATIONS):
        qkv_g.append(_mm_nt(u1_g[g], qkv_parts(g), out_dtype=BF16, name=f"attn_qkv{g}",
                            rope=(cos_g[g], sin_g[g], 2)))
        o_g, lse_g = _attn_fwd(qkv_g[g], d, f"attn_fwd{g}")
        outs.append(o_g)
        lses.append(lse_g)
    oc, lse_all = _attn_merge_fwd(outs, lses, "attn_merge")
    h3 = _mm_nn([oc], [whole("attn_out")], h2, name="attn_out")
    (dh4, dh4b, d_final, loss_part), ffn1 = ffn_fwd(h3, 1, head=(target, final_gain))
    dh3, dh3b, d_ffn1 = ffn_bwd(h3, ffn1, dh4, dh4b, 1)

    d_oc = _mm_nt(dh3b, whole("attn_out"), out_dtype=F32, name="attn_out_dx")
    grad_attn_out = _mm_tn([oc], dh3b, name="attn_out_dw")
    delta, d_ocb = _attn_merge_bwd(d_oc, oc, "attn_merge_bwd")
    du1, qkv_pieces = [], []
    for g, d in enumerate(ATTN_DILATIONS):
        dqkv = _attn_bwd(qkv_g[g], d_ocb[g], lse_all[g], delta[g], cos_g[g], sin_g[g], d, f"attn_bwd{g}")
        qkv_pieces.append(_mm_tn([dqkv], u1_g[g], name=f"attn_qkv_dw{g}"))
        du1.append(_mm_nn([dqkv], [qkv_parts(g)], None, name=f"attn_qkv_dx{g}"))
    grad_qkv = jnp.stack([p.reshape(3, ATTN_GROUP_WIDTH, D_MODEL) for p in qkv_pieces], axis=1).reshape(
        3 * ATTN_WIDTH, D_MODEL)
    publish("attn", {"qkv": grad_qkv, "attn_out": grad_attn_out})
    dh2, dh2b, d_mix1 = _rms_bwd(h2, g_mix[1], du1, dh3, "attn_norm_bwd", ATTN_DILATIONS)

    dh1, dh1b, d_ffn0 = ffn_bwd(h1, ffn0, dh2, dh2b, 0)

    d_og = _mm_nt(dh1b, whole("hgrn_out"), out_dtype=F32, name="hgrn_out_dx")
    grad_hgrn_out = _mm_tn([og], dh1b, name="hgrn_out_dw")
    dproj, d_lb, d_out_gain = _hgrn_bwd(proj, o_pre, d_og, states, lb, out_gain, "hgrn_bwd")
    publish("hgrn", {"hgrn_in": _mm_tn([dproj], u0, name="hgrn_in_dw"), "hgrn_out": grad_hgrn_out})
    dx, _, d_mix0 = _mm_nn([dproj], [whole("hgrn_in")], dh1, name="hgrn_in_dx", norm=(x, g_mix[0]))

    small = dict(norm_mix0=d_mix0, norm_mix1=d_mix1, norm_ffn0=d_ffn0, norm_ffn1=d_ffn1, lb=d_lb,
                 out_gain=d_out_gain, final=d_final, loss=loss_part)
    return dx, small


MESH_IDS = pl.DeviceIdType.MESH
HBM_SPEC = pl.BlockSpec(memory_space=pl.ANY)


N_PEERS = N_DEV - 1
PEER_OFFSETS = [(dx, dy, dc) for dx in (0, 1) for dy in (0, 1) for dc in (0, 1)][1:]


def _mesh_place():
    x, y, c = lax.axis_index("x"), lax.axis_index("y"), lax.axis_index("c")
    peers = []
    for dx, dy, dc in PEER_OFFSETS:
        px, py, pc = (1 - x if dx else x), (1 - y if dy else y), (1 - c if dc else c)
        peers.append(((px, py, pc), 4 * px + 2 * py + pc))
    return 4 * x + 2 * y + c, peers


def _gather_over_two_levels(src_refs, land_refs, send_sems, recv_sems):
    n = len(src_refs)
    x, y, c = lax.axis_index("x"), lax.axis_index("y"), lax.axis_index("c")
    me, sibling = (x, y, c), (x, y, 1 - c)
    chips = [(1 - x, y), (x, 1 - y), (1 - x, 1 - y)]

    def block(w, px, py, pc):
        return land_refs[w].at[4 * px + 2 * py + pc]

    def copy(w, k, owner, to, src=None):
        return pltpu.make_async_remote_copy(
            src_ref=block(w, *owner) if src is None else src, dst_ref=block(w, *owner),
            send_sem=send_sems.at[w * N_PEERS + k], recv_sem=recv_sems.at[w * N_PEERS + k],
            device_id=to, device_id_type=MESH_IDS)

    sent = []
    for w in range(n):
        sent.append(copy(w, 0, me, sibling, src=src_refs[w]))
        sent += [copy(w, 1 + j, me, (*chip, c), src=src_refs[w]) for j, chip in enumerate(chips)]
    for cp in sent:
        cp.start()
    for w in range(n):
        for j, chip in enumerate(chips):
            copy(w, 1 + j, (*chip, c), me).wait_recv()
            passed = copy(w, 4 + j, (*chip, c), sibling)
            passed.start()
            sent.append(passed)
    for w in range(n):
        copy(w, 0, sibling, me).wait_recv()
        for j, chip in enumerate(chips):
            copy(w, 4 + j, (*chip, 1 - c), me).wait_recv()
    for cp in sent:
        cp.wait_send()


def _exchange_launch(srcs, scatter, collective_id, name):
    n = len(srcs)
    src_refs = [jax.new_ref(s, memory_space=pltpu.MemorySpace.HBM) for s in srcs]
    land_refs = [jax.empty_ref(jax.ShapeDtypeStruct(s.shape if scatter else (N_DEV,) + s.shape, s.dtype),
                               memory_space=pltpu.MemorySpace.HBM) for s in srcs]

    @pl.kernel(mesh=plsc.ScalarSubcoreMesh(axis_name="sequencer", num_cores=1), name=name,
               scratch_types=(pltpu.SemaphoreType.DMA((n * N_PEERS,)), pltpu.SemaphoreType.DMA((n * N_PEERS,)),
                              pltpu.SemaphoreType.DMA((n,))),
               compiler_params=pltpu.CompilerParams(collective_id=collective_id))
    def launch(send_sems, recv_sems, local_sems):
        me, peers = _mesh_place()
        barrier = pltpu.get_barrier_semaphore()
        for peer, _ in peers:
            pl.semaphore_signal(barrier, inc=1, device_id=peer, device_id_type=MESH_IDS)
        pl.semaphore_wait(barrier, N_PEERS)
        own = [pltpu.make_async_copy(src_refs[w].at[me] if scatter else src_refs[w], land_refs[w].at[me],
                                     local_sems.at[w]) for w in range(n)]
        for cp in own:
            cp.start()
        if scatter:
            copies = [pltpu.make_async_remote_copy(
                src_ref=src_refs[w].at[pid], dst_ref=land_refs[w].at[me],
                send_sem=send_sems.at[w * N_PEERS + k], recv_sem=recv_sems.at[w * N_PEERS + k],
                device_id=peer, device_id_type=MESH_IDS) for w in range(n) for k, (peer, pid) in enumerate(peers)]
            for cp in copies:
                cp.start()
            for cp in copies:
                cp.wait()
        else:
            _gather_over_two_levels(src_refs, land_refs, send_sems, recv_sems)
        for cp in own:
            cp.wait()

    launch()
    return land_refs


def _gather_small(block, name):
    def body(in_ref, out_ref, send_sems, recv_sems, local_sem):
        me, peers = _mesh_place()
        own = pltpu.make_async_copy(in_ref, out_ref.at[me], local_sem)
        own.start()
        sends = [pltpu.make_async_remote_copy(
            src_ref=in_ref, dst_ref=out_ref.at[me], send_sem=send_sems.at[k], recv_sem=recv_sems.at[k],
            device_id=peer, device_id_type=MESH_IDS) for k, (peer, _) in enumerate(peers)]
        for cp in sends:
            cp.start()
        for cp in sends:
            cp.wait_recv()
        for cp in sends:
            cp.wait_send()
        own.wait()

    return pl.pallas_call(
        body, out_shape=jax.ShapeDtypeStruct((N_DEV,) + block.shape, block.dtype),
        in_specs=[HBM_SPEC], out_specs=HBM_SPEC,
        scratch_shapes=[pltpu.SemaphoreType.DMA((N_PEERS,)), pltpu.SemaphoreType.DMA((N_PEERS,)),
                        pltpu.SemaphoreType.DMA],
        name=name)(block)


def _sum_blocks(recv, name):
    rows = recv.shape[1]
    tr = _pick_tile(rows, 256, 16)

    def body(r_ref, g_ref):
        acc = r_ref[0].astype(F32)
        for j in range(1, N_DEV):
            acc = acc + r_ref[j].astype(F32)
        g_ref[...] = acc

    return pl.pallas_call(
        body, out_shape=jax.ShapeDtypeStruct((rows, D_MODEL), F32), grid=(rows // tr,),
        in_specs=[pl.BlockSpec((N_DEV, tr, D_MODEL), lambda i: (0, i, 0))],
        out_specs=pl.BlockSpec((tr, D_MODEL), lambda i: (i, 0)),
        compiler_params=_params("parallel"), name=name)(recv)


def _adamw_math(w, g, m, v):
    m_new = ADAM_B1 * m + (1.0 - ADAM_B1) * g
    v_new = ADAM_B2 * v + (1.0 - ADAM_B2) * (g * g)
    m_hat = m_new / (1.0 - ADAM_B1 ** ADAM_STEP)
    v_hat = v_new / (1.0 - ADAM_B2 ** ADAM_STEP)
    delta = -ADAM_LR * (m_hat / (jnp.sqrt(v_hat) + ADAM_EPS) + ADAM_WD * w)
    return delta, m_new, v_new


def _adamw(w, g, m, v, layer, others, name):
    _, rows, cols = w.shape
    tr = _pick_tile(rows, 256, 8)

    def body(w_ref, g_ref, m_ref, v_ref, *refs):
        go_ref, d_ref, mo_ref, vo_ref = refs[-4:]
        gv = g_ref[...]
        go_ref[...] = gv
        d_ref[...], mo_ref[...], vo_ref[...] = _adamw_math(w_ref[...], gv, m_ref[...], v_ref[...])

    one = pl.BlockSpec((None, tr, cols), lambda i: (layer, i, 0))
    in_specs = [one, pl.BlockSpec((tr, cols), lambda i: (i, 0)), one, one]
    args = [w, g, m, v]
    if others is not None:
        in_specs += [HBM_SPEC] * 4
        args += list(others)
    return pl.pallas_call(
        body, out_shape=(jax.ShapeDtypeStruct(w.shape, F32),) * 4, grid=(rows // tr,),
        in_specs=in_specs, out_specs=(one,) * 4,
        input_output_aliases={} if others is None else {4 + i: i for i in range(4)},
        compiler_params=_params("parallel"), name=name)(*args)


ROW_MIX, ROW_FFN, ROW_LB, ROW_OUT_GAIN, ROW_FINAL = 0, 2, 4, 7, 8
PART_MIX, PART_FFN, PART_LB, PART_OUT_GAIN, PART_FINAL, PART_LOSS = 0, 2, 4, 5, 6, 7


def _small_update(parts_all, w, m, v, name):
    def body(p_ref, w_ref, m_ref, v_ref, g_ref, d_ref, mo_ref, vo_ref, loss_ref):
        def total(row, n=1):
            tot = p_ref[0, row:row + n, :]
            for j in range(1, N_DEV):
                tot = tot + p_ref[j, row:row + n, :]
            return tot

        logits = [w_ref[ROW_LB + i:ROW_LB + i + 1, :] for i in range(3)]
        mx = jnp.maximum(jnp.maximum(logits[0], logits[1]), logits[2])
        ex = [jnp.exp(l - mx) for l in logits]
        den = ex[0] + ex[1] + ex[2]
        prob = [e / den for e in ex]
        d_lb = total(PART_LB)
        g_ref[...] = jnp.zeros_like(g_ref)
        g_ref[ROW_MIX:ROW_MIX + 2, :] = total(PART_MIX, 2)
        g_ref[ROW_FFN:ROW_FFN + 2, :] = total(PART_FFN, 2)
        for i in range(3):
            g_ref[ROW_LB + i:ROW_LB + i + 1, :] = prob[i] * ((d_lb if i == 0 else 0.0) - prob[0] * d_lb)
        g_ref[ROW_OUT_GAIN:ROW_OUT_GAIN + 1, :] = total(PART_OUT_GAIN)
        g_ref[ROW_FINAL:ROW_FINAL + 1, :] = total(PART_FINAL)
        d_ref[...], mo_ref[...], vo_ref[...] = _adamw_math(w_ref[...], g_ref[...], m_ref[...], v_ref[...])
        loss_ref[...] = jnp.sum(total(PART_LOSS), axis=-1, keepdims=True)

    packed = jax.ShapeDtypeStruct((16, D_MODEL), F32)
    return pl.pallas_call(
        body, out_shape=(packed, packed, packed, packed, jax.ShapeDtypeStruct((1, 1), F32)),
        compiler_params=pltpu.CompilerParams(vmem_limit_bytes=VMEM_LIMIT), name=name)(parts_all, w, m, v)


def _pack_small(norm_mix, norm_ffn, lb_logits, out_gain, final):
    pad = jnp.zeros((1, D_MODEL - HGRN_DIM), F32)
    return jnp.concatenate([norm_mix, norm_ffn, lb_logits, jnp.concatenate([out_gain, pad], axis=1),
                            final.reshape(1, D_MODEL), jnp.zeros((16 - ROW_FINAL - 1, D_MODEL), F32)], axis=0)


def _unpack_small(p):
    return (p[ROW_MIX:ROW_MIX + 2], p[ROW_FFN:ROW_FFN + 2], p[ROW_LB:ROW_LB + 3],
            p[ROW_OUT_GAIN:ROW_OUT_GAIN + 1, :HGRN_DIM], p[ROW_FINAL])


def _lower_bound(lb_logits, name):
    def body(l_ref, o_ref):
        logits = [l_ref[i:i + 1, :] for i in range(3)]
        mx = jnp.maximum(jnp.maximum(logits[0], logits[1]), logits[2])
        ex = [jnp.exp(l - mx) for l in logits]
        o_ref[...] = ex[0] / (ex[0] + ex[1] + ex[2])

    return pl.pallas_call(body, out_shape=jax.ShapeDtypeStruct((1, D_MODEL), F32), name=name)(lb_logits)


def kernel(x, norm_mix, norm_ffn, hgrn_w_in, hgrn_lb_logits, hgrn_out_norm, hgrn_w_out, attn_w_qkv, attn_w_out, ffn_w_in, ffn_w_down, final_norm, loss_target, m_norm_mix, m_norm_ffn, m_hgrn_w_in, m_hgrn_lb_logits, m_hgrn_out_norm, m_hgrn_w_out, m_attn_w_qkv, m_attn_w_out, m_ffn_w_in, m_ffn_w_down, m_final_norm, v_norm_mix, v_norm_ffn, v_hgrn_w_in, v_hgrn_lb_logits, v_hgrn_out_norm, v_hgrn_w_out, v_attn_w_qkv, v_attn_w_out, v_ffn_w_in, v_ffn_w_down, v_final_norm):
    col_sharded = {"hgrn_in": hgrn_w_in[0], "qkv": attn_w_qkv[0], "ffn_in0": ffn_w_in[0], "ffn_in1": ffn_w_in[1]}
    row_sharded = {"hgrn_out": hgrn_w_out[0], "attn_out": attn_w_out[0], "ffn_down0": ffn_w_down[0],
                   "ffn_down1": ffn_w_down[1]}
    gathering = {}
    for gi, (group, names) in enumerate(WEIGHT_GROUPS.items()):
        shards = [(col_sharded[n].T if n in col_sharded else row_sharded[n]).astype(BF16) for n in names]
        gathering[group] = _exchange_launch(shards, False, 1 + gi, f"weights_gather_{group}")

    def fetch(group):
        return {n: land[...].reshape(-1, D_MODEL) for n, land in zip(WEIGHT_GROUPS[group], gathering[group])}

    in_flight = {}

    def publish(group, grads):
        names = WEIGHT_GROUPS[group]
        parts = [grads[n].reshape(N_DEV, -1, D_MODEL) for n in names]
        in_flight[group] = _exchange_launch(parts, True, 1 + len(WEIGHT_GROUPS) + list(WEIGHT_GROUPS).index(group),
                                            f"grads_send_{group}")

    lb = _lower_bound(hgrn_lb_logits, "hgrn_lower_bound")
    grad_x, small = _local_step(x[0], loss_target[0], norm_mix, norm_ffn, lb, hgrn_out_norm,
                                final_norm.reshape(1, D_MODEL), fetch, publish)

    pad = jnp.zeros((1, D_MODEL - HGRN_DIM), F32)
    small_part = jnp.concatenate(
        [small["norm_mix0"], small["norm_mix1"], small["norm_ffn0"], small["norm_ffn1"], small["lb"],
         jnp.concatenate([small["out_gain"], pad], axis=1), small["final"], small["loss"]], axis=0)
    small_all = _gather_small(small_part, "small_grads_gather")
    received = {}
    for group in ("ffn1", "attn", "ffn0", "hgrn"):
        received.update(zip(WEIGHT_GROUPS[group], [land[...] for land in in_flight[group]]))

    masters = {"hgrn_w_in": (hgrn_w_in, m_hgrn_w_in, v_hgrn_w_in, ("hgrn_in",)),
               "hgrn_w_out": (hgrn_w_out, m_hgrn_w_out, v_hgrn_w_out, ("hgrn_out",)),
               "attn_w_qkv": (attn_w_qkv, m_attn_w_qkv, v_attn_w_qkv, ("qkv",)),
               "attn_w_out": (attn_w_out, m_attn_w_out, v_attn_w_out, ("attn_out",)),
               "ffn_w_in": (ffn_w_in, m_ffn_w_in, v_ffn_w_in, ("ffn_in0", "ffn_in1")),
               "ffn_w_down": (ffn_w_down, m_ffn_w_down, v_ffn_w_down, ("ffn_down0", "ffn_down1"))}
    big = {}
    for param, (wv, mv, vv, names) in masters.items():
        outs = None
        for layer, n in enumerate(names):
            g = _sum_blocks(received[n], f"{n}_grad_sum")
            outs = _adamw(wv, g.T if n in col_sharded else g, mv, vv, layer, outs, f"{n}_adamw")
        big[param] = list(outs)

    w_small = _pack_small(norm_mix, norm_ffn, hgrn_lb_logits, hgrn_out_norm, final_norm)
    m_small = _pack_small(m_norm_mix, m_norm_ffn, m_hgrn_lb_logits, m_hgrn_out_norm, m_final_norm)
    v_small = _pack_small(v_norm_mix, v_norm_ffn, v_hgrn_lb_logits, v_hgrn_out_norm, v_final_norm)
    g_s, d_s, m_s, v_s, loss = _small_update(small_all, w_small, m_small, v_small, "small_update")
    small_out = [_unpack_small(t) for t in (g_s, d_s, m_s, v_s)]

    def group(i):
        s = small_out[i]
        return (s[0], s[1], big["hgrn_w_in"][i], s[2], s[3], big["hgrn_w_out"][i], big["attn_w_qkv"][i],
                big["attn_w_out"][i], big["ffn_w_in"][i], big["ffn_w_down"][i], s[4])

    return (loss.reshape(()), grad_x[None], *group(0), *group(1), *group(2), *group(3))
```

```python
import functools

import jax
import jax.numpy as jnp
from jax import lax
from jax.experimental import pallas as pl
from jax.experimental.pallas import tpu as pltpu
from jax.experimental.pallas import tpu_sc as plsc

F32 = jnp.float32
BF16 = jnp.bfloat16

D_MODEL = 1024
N_DEV = 8
NORM_EPS = 1e-6

HGRN_HEADS = 8
HGRN_DIM = 128
HGRN_CHUNK = 64
HGRN_STEP_CHUNKS = 4
HGRN_FWD_STEP_CHUNKS = 8
HGRN_EXP_CLAMP = 60.0

ATTN_DIM = 128
ATTN_BLOCK = 128
ATTN_GROUP_HEADS = 4
ATTN_GROUP_WIDTH = ATTN_GROUP_HEADS * ATTN_DIM
ATTN_DILATIONS = (1, 4, 16)
ATTN_WIDTH = 3 * ATTN_GROUP_WIDTH
ROPE_THETA = 10000.0
NEG_BIG = -1e30

D_FF = 2816

ADAM_LR = 0.001
ADAM_B1 = 0.9
ADAM_B2 = 0.999
ADAM_EPS = 1e-08
ADAM_WD = 0.01
ADAM_STEP = 10

VMEM_LIMIT = 48 * 1024 * 1024

NT = (((1,), (1,)), ((), ()))
NN = (((1,), (0,)), ((), ()))
TN = (((0,), (0,)), ((), ()))


def _dot(a, b, dims):
    return lax.dot_general(a, b, dims, preferred_element_type=F32)


def _params(*sem):
    return pltpu.CompilerParams(dimension_semantics=sem, vmem_limit_bytes=VMEM_LIMIT)


def _pick_tile(n, cap, mult):
    best = None
    for t in range(mult, min(n, cap) + 1, mult):
        if n % t == 0:
            best = t
    assert best is not None, (n, cap, mult)
    return best


def _sigmoid(x):
    return 0.5 * jnp.tanh(0.5 * x) + 0.5


ROW_TILE = 512
COL_CHUNK = 512
GRAD_TILE = 256


def _whole(shape, index_map):
    return pl.BlockSpec(shape, index_map, pipeline_mode=pl.Buffered(1))


def _part_specs(parts, n_cols):
    return [_whole((rows, n_cols), functools.partial(lambda i, b: (b, 0), b=blk)) for _, rows, blk in parts]


def _mm_nt(a, w_parts, *, out_dtype, name, rope=None):
    M, K = a.shape
    tm = _pick_tile(M, ROW_TILE, 16)
    widths = [rows for _, rows, _ in w_parts]
    n_parts = len(w_parts)

    def body(*refs):
        a_ref, w_refs, o_ref = refs[0], refs[1:1 + n_parts], refs[-1]
        av = a_ref[...]
        off = 0
        for p, w_ref in enumerate(w_refs):
            for c0 in range(0, widths[p], COL_CHUNK):
                cw = min(COL_CHUNK, widths[p] - c0)
                acc = _dot(av, w_ref[c0:c0 + cw, :], NT)
                if rope is not None and p < rope[2]:
                    cos, sin = refs[1 + n_parts][...], refs[2 + n_parts][...]
                    for h0 in range(0, cw, ATTN_DIM):
                        xh = acc[:, h0:h0 + ATTN_DIM]
                        rot = pltpu.roll(xh, ATTN_DIM // 2, 1)
                        o_ref[:, off + c0 + h0:off + c0 + h0 + ATTN_DIM] = (xh * cos + rot * sin).astype(out_dtype)
                else:
                    o_ref[:, off + c0:off + c0 + cw] = acc.astype(out_dtype)
            off += widths[p]

    in_specs = [pl.BlockSpec((tm, K), lambda i: (i, 0))] + _part_specs(w_parts, K)
    args = [a] + [w for w, _, _ in w_parts]
    if rope is not None:
        in_specs += [pl.BlockSpec((tm, ATTN_DIM), lambda i: (i, 0))] * 2
        args += [rope[0], rope[1]]
    return pl.pallas_call(
        body, out_shape=jax.ShapeDtypeStruct((M, sum(widths)), out_dtype), grid=(M // tm,),
        in_specs=in_specs, out_specs=pl.BlockSpec((tm, sum(widths)), lambda i: (i, 0)),
        compiler_params=_params("parallel"), name=name)(*args)


def _mm_nn(a_list, w_parts_list, resid, *, name, norm=None, head=None):
    M = a_list[0].shape[0]
    tm = _pick_tile(M, ROW_TILE, 16)
    n_a = len(a_list)
    flat_parts = [p for parts in w_parts_list for p in parts]
    extra = norm if norm is not None else head
    n_in = n_a + len(flat_parts) + (1 if resid is not None else 0) + (2 if extra is not None else 0)

    def body(*refs):
        a_refs, w_refs = refs[:n_a], refs[n_a:n_a + len(flat_parts)]

        def product(rows):
            acc = None
            wi = 0
            for a_ref, parts in zip(a_refs, w_parts_list):
                off = 0
                for _, k, _ in parts:
                    term = _dot(a_ref[rows, off:off + k], w_refs[wi][...], NN)
                    acc = term if acc is None else acc + term
                    off += k
                    wi += 1
            return acc

        if extra is None:
            acc = product(slice(None))
            if resid is not None:
                acc = acc + refs[n_in - 1][...]
            refs[n_in][...] = acc
            return

        @pl.when(pl.program_id(0) == 0)
        def _():
            for acc_ref in refs[n_in + 2:]:
                acc_ref[...] = jnp.zeros_like(acc_ref)

        for r0 in range(0, tm, tm // 2):
            rows = slice(r0, r0 + tm // 2)
            acc = product(rows)
            if head is not None:
                _loss_head_math(acc + refs[n_in - 3][rows, :], rows, refs[n_in - 2], refs[n_in - 1],
                                *refs[n_in:n_in + 4])
                continue
            dres_ref, x_ref, g_ref = refs[n_in - 3:n_in]
            dx_ref, dxb_ref, dg_ref = refs[n_in:n_in + 3]
            xv = x_ref[rows, :]
            rstd = lax.rsqrt(jnp.mean(xv * xv, axis=-1, keepdims=True) + NORM_EPS)
            n = xv * rstd
            dg_ref[...] += jnp.sum(acc * n, axis=0, keepdims=True)
            dn = acc * g_ref[...]
            dx = dres_ref[rows, :] + rstd * (dn - n * jnp.mean(dn * n, axis=-1, keepdims=True))
            dx_ref[rows, :] = dx
            dxb_ref[rows, :] = dx.astype(BF16)

    row = pl.BlockSpec((tm, D_MODEL), lambda i: (i, 0))
    vec = pl.BlockSpec((1, D_MODEL), lambda i: (0, 0))
    in_specs = [pl.BlockSpec((tm, a.shape[1]), lambda i: (i, 0)) for a in a_list] + _part_specs(flat_parts, D_MODEL)
    args = list(a_list) + [w for w, _, _ in flat_parts]
    if resid is not None:
        in_specs.append(row)
        args.append(resid)
    if extra is None:
        return pl.pallas_call(
            body, out_shape=jax.ShapeDtypeStruct((M, D_MODEL), F32), grid=(M // tm,),
            in_specs=in_specs, out_specs=row, compiler_params=_params("parallel"), name=name)(*args)
    assert resid is not None
    out_shape = [jax.ShapeDtypeStruct((M, D_MODEL), F32), jax.ShapeDtypeStruct((M, D_MODEL), BF16),
                 jax.ShapeDtypeStruct((1, D_MODEL), F32)]
    out_specs = [row, row, vec]
    if head is not None:
        out_shape.append(jax.ShapeDtypeStruct((1, D_MODEL), F32))
        out_specs.append(vec)
    return pl.pallas_call(
        body, out_shape=out_shape, grid=(M // tm,), in_specs=in_specs + [row, vec], out_specs=out_specs,
        compiler_params=_params("arbitrary"), name=name)(*args, extra[0], extra[1])


def _mm_tn(a_list, b, *, name):
    T = a_list[0].shape[0]
    N = b.shape[1]
    tr = GRAD_TILE
    tiles = [a.shape[1] // tr for a in a_list]
    starts = [sum(tiles[:i]) for i in range(len(tiles))]

    def body(*refs):
        a_refs, b_ref, o_ref = refs[:len(a_list)], refs[len(a_list)], refs[-1]
        r = pl.program_id(0)
        for a_ref, first, count in zip(a_refs, starts, tiles):
            @pl.when(jnp.logical_and(r >= first, r < first + count))
            def _():
                o_ref[...] = _dot(a_ref[...], b_ref[...], TN).astype(BF16)

    in_specs = [pl.BlockSpec((T, tr), functools.partial(lambda r, first, count: (0, jnp.clip(r - first, 0, count - 1)),
                                                        first=first, count=count))
                for first, count in zip(starts, tiles)]
    in_specs.append(_whole((T, N), lambda r: (0, 0)))
    return pl.pallas_call(
        body, out_shape=jax.ShapeDtypeStruct((sum(tiles) * tr, N), BF16), grid=(sum(tiles),),
        in_specs=in_specs, out_specs=pl.BlockSpec((tr, N), lambda r: (r, 0)),
        compiler_params=_params("parallel"), name=name)(*a_list, b)


def _rms_fwd(x, gain, name):
    T = x.shape[0]
    tm = _pick_tile(T, 512, 16)

    def body(x_ref, g_ref, u_ref):
        xv = x_ref[...]
        rstd = lax.rsqrt(jnp.mean(xv * xv, axis=-1, keepdims=True) + NORM_EPS)
        u_ref[...] = (xv * rstd * g_ref[...]).astype(BF16)

    return pl.pallas_call(
        body, out_shape=jax.ShapeDtypeStruct((T, D_MODEL), BF16), grid=(T // tm,),
        in_specs=[pl.BlockSpec((tm, D_MODEL), lambda i: (i, 0)), pl.BlockSpec((1, D_MODEL), lambda i: (0, 0))],
        out_specs=pl.BlockSpec((tm, D_MODEL), lambda i: (i, 0)),
        compiler_params=_params("parallel"), name=name)(x, gain)


def _rms_bwd(x, gain, dus, dres, name, dilations=(1,)):
    T = x.shape[0]
    tm = _pick_tile(T, PERM_TILE, 16 * max(dilations))
    n_du = len(dus)

    def body(x_ref, g_ref, *refs):
        du_refs, dres_ref = refs[:n_du], refs[n_du]
        dx_ref, dxb_ref, dg_ref, du_scr = refs[n_du + 1:]

        @pl.when(pl.program_id(0) == 0)
        def _():
            dg_ref[...] = jnp.zeros_like(dg_ref)

        if tuple(dilations) == (1,):
            du = du_refs[0][...]
        else:
            for i, (d, du_ref) in enumerate(zip(dilations, du_refs)):
                for j in range(D_MODEL // LANES):
                    lanes = slice(j * LANES, (j + 1) * LANES)
                    if d == 1:
                        du_scr[j] = du_ref[:, lanes] if i == 0 else du_scr[j] + du_ref[:, lanes]
                        continue
                    blk = du_scr.at[j]
                    for r in range(d):
                        rows = _class_rows(r, d, tm)
                        blk[rows, :] = du_ref[r, :, lanes] if i == 0 else blk[rows, :] + du_ref[r, :, lanes]
            du = jnp.concatenate([du_scr[j] for j in range(D_MODEL // LANES)], axis=1)
        xv = x_ref[...]
        rstd = lax.rsqrt(jnp.mean(xv * xv, axis=-1, keepdims=True) + NORM_EPS)
        n = xv * rstd
        dg_ref[...] += jnp.sum(du * n, axis=0, keepdims=True)
        dn = du * g_ref[...]
        dx = dres_ref[...] + rstd * (dn - n * jnp.mean(dn * n, axis=-1, keepdims=True))
        dx_ref[...] = dx
        dxb_ref[...] = dx.astype(BF16)

    row = pl.BlockSpec((tm, D_MODEL), lambda i: (i, 0))
    vec = pl.BlockSpec((1, D_MODEL), lambda i: (0, 0))
    return pl.pallas_call(
        body,
        out_shape=(jax.ShapeDtypeStruct((T, D_MODEL), F32), jax.ShapeDtypeStruct((T, D_MODEL), BF16),
                   jax.ShapeDtypeStruct((1, D_MODEL), F32)),
        grid=(T // tm,), in_specs=[row, vec] + [_residue_spec(d, tm, D_MODEL) for d in dilations] + [row],
        out_specs=(row, row, vec), scratch_shapes=[pltpu.VMEM((D_MODEL // LANES, tm, LANES), F32)],
        compiler_params=_params("arbitrary"), name=name)(
            x, gain, *[_residue_view(du, d) for du, d in zip(dus, dilations)], dres)


def _loss_head_math(hv, rows, t_ref, g_ref, dh_ref, dhb_ref, dg_ref, loss_ref):
    inv_f = 1.0 / D_MODEL
    g = g_ref[...]
    rstd = lax.rsqrt(jnp.mean(hv * hv, axis=-1, keepdims=True) + NORM_EPS)
    n = hv * rstd
    err = n * g - t_ref[rows, :]
    loss_ref[...] += (0.5 * inv_f) * jnp.sum(err * err, axis=0, keepdims=True)
    dy = err * inv_f
    dg_ref[...] += jnp.sum(dy * n, axis=0, keepdims=True)
    dn = dy * g
    dh = rstd * (dn - n * jnp.mean(dn * n, axis=-1, keepdims=True))
    dh_ref[rows, :] = dh
    dhb_ref[rows, :] = dh.astype(BF16)


FFN_TILE = 256


def _ffn_in(h, gain, w_in, name):
    T = h.shape[0]
    tm = _pick_tile(T, ROW_TILE, 16)

    def body(h_ref, g_ref, w_ref, n_ref, gate_ref, up_ref, a_ref):
        hv = h_ref[...]
        rstd = lax.rsqrt(jnp.mean(hv * hv, axis=-1, keepdims=True) + NORM_EPS)
        n = (hv * rstd * g_ref[...]).astype(BF16)
        n_ref[...] = n
        for c0 in range(0, D_FF, FFN_TILE):
            cols = slice(c0, c0 + FFN_TILE)
            gate = _dot(n, w_ref[c0:c0 + FFN_TILE, :], NT)
            up = _dot(n, w_ref[D_FF + c0:D_FF + c0 + FFN_TILE, :], NT)
            gate_ref[:, cols] = gate.astype(BF16)
            up_ref[:, cols] = up.astype(BF16)
            a_ref[:, cols] = (gate * _sigmoid(gate) * up).astype(BF16)

    row = pl.BlockSpec((tm, D_MODEL), lambda i: (i, 0))
    wide = pl.BlockSpec((tm, D_FF), lambda i: (i, 0))
    wide_shape = jax.ShapeDtypeStruct((T, D_FF), BF16)
    return pl.pallas_call(
        body, out_shape=(jax.ShapeDtypeStruct((T, D_MODEL), BF16), wide_shape, wide_shape, wide_shape),
        grid=(T // tm,),
        in_specs=[row, pl.BlockSpec((1, D_MODEL), lambda i: (0, 0)), _whole((2 * D_FF, D_MODEL), lambda i: (0, 0))],
        out_specs=(row, wide, wide, wide), compiler_params=_params("parallel"), name=name)(h, gain, w_in)


def _ffn_down_dx(dhb, w_down, gate, up, name):
    T = dhb.shape[0]
    tm = _pick_tile(T, ROW_TILE, 16)

    def body(dh_ref, w_ref, gate_ref, up_ref, dgate_ref, dup_ref):
        dh = dh_ref[...]
        for c0 in range(0, D_FF, FFN_TILE):
            cols = slice(c0, c0 + FFN_TILE)
            da = _dot(dh, w_ref[c0:c0 + FFN_TILE, :], NT).astype(BF16)
            gate = gate_ref[:, cols]
            sg = (0.5 * jnp.tanh(0.5 * jnp.abs(gate)) + 0.5) * jnp.exp(jnp.minimum(gate, 0.0))
            silu = gate * sg
            dgate_ref[:, cols] = da * up_ref[:, cols] * (sg + silu * (1.0 - sg))
            dup_ref[:, cols] = da * silu

    wide = pl.BlockSpec((tm, D_FF), lambda i: (i, 0))
    wide_shape = jax.ShapeDtypeStruct((T, D_FF), BF16)
    return pl.pallas_call(
        body, out_shape=(wide_shape, wide_shape), grid=(T // tm,),
        in_specs=[pl.BlockSpec((tm, D_MODEL), lambda i: (i, 0)), _whole((D_FF, D_MODEL), lambda i: (0, 0)), wide, wide],
        out_specs=(wide, wide), compiler_params=_params("parallel"), name=name)(dhb, w_down, gate, up)


def _tri(n, lower):
    r = lax.broadcasted_iota(jnp.int32, (n, n), 0)
    c = lax.broadcasted_iota(jnp.int32, (n, n), 1)
    return (c <= r) if lower else (c >= r)


def _running_sum(x, lower):
    tri = _tri(x.shape[0], lower).astype(BF16)
    hi = x.astype(BF16)
    rest = x - hi.astype(F32)
    mid = rest.astype(BF16)
    lo = (rest - mid.astype(F32)).astype(BF16)
    return _dot(tri, hi, NN) + _dot(tri, mid, NN) + _dot(tri, lo, NN)


def _hgrn_gates(q_raw, f_raw, lb):
    C = q_raw.shape[0]
    sig_f = _sigmoid(f_raw)
    forget = lb + (1.0 - lb) * sig_f
    key = 1.0 - forget
    log_f = jnp.log(forget)
    b = _running_sum(log_f, True)
    first_half = lax.broadcasted_iota(jnp.int32, log_f.shape, 0) < C // 2
    r = jnp.sum(jnp.where(first_half, log_f, 0.0), axis=0, keepdims=True)
    b_last = jnp.sum(log_f, axis=0, keepdims=True)
    e_a = jnp.exp(jnp.minimum(b - r, HGRN_EXP_CLAMP))
    e_b = jnp.exp(jnp.minimum(r - b, HGRN_EXP_CLAMP))
    e_q = jnp.exp(b)
    e_k = jnp.exp(b_last - b)
    sig_q = _sigmoid(q_raw)
    query = q_raw * sig_q
    return dict(sig_f=sig_f, forget=forget, sig_q=sig_q, e_a=e_a, e_b=e_b, e_q=e_q, e_k=e_k,
                e_last=jnp.exp(b_last), q_a=query * e_a, k_b=key * e_b, q_hat=query * e_q, k_til=key * e_k)


def _hgrn_fwd(proj, lb, gain, name):
    T = proj.shape[0]
    C = HGRN_CHUNK
    CPS = HGRN_FWD_STEP_CHUNKS
    H, HD = HGRN_HEADS, HGRN_DIM

    def body(q_ref, f_ref, i_ref, g_ref, lb_ref, gain_ref, og_ref, o_ref, st_ref, s_scr):
        @pl.when(pl.program_id(0) == 0)
        def _():
            s_scr[...] = jnp.zeros_like(s_scr)

        causal = _tri(C, True)
        gain_v = gain_ref[...]
        heads = [slice(h * HD, (h + 1) * HD) for h in range(H)]
        s_t = [s_scr[h] for h in range(H)]
        for cc in range(CPS):
            rows = slice(cc * C, (cc + 1) * C)
            for h in range(H):
                st_ref[cc, h] = s_t[h]
            gt = _hgrn_gates(q_ref[rows, :], f_ref[rows, :], lb_ref[...])
            q_a, k_b = gt["q_a"].astype(BF16), gt["k_b"].astype(BF16)
            q_hat, k_til = gt["q_hat"].astype(BF16), gt["k_til"].astype(BF16)
            v = i_ref[rows, :].astype(BF16)
            p = [jnp.where(causal, _dot(q_a[:, sl], k_b[:, sl], NT), 0.0).astype(BF16) for sl in heads]
            o = [_dot(p[h], v[:, sl], NN) + _dot(q_hat[:, sl], s_t[h].astype(BF16), NT)
                 for h, sl in enumerate(heads)]
            s_t = [gt["e_last"][:, sl] * s_t[h] + _dot(v[:, sl], k_til[:, sl], TN) for h, sl in enumerate(heads)]
            for h, sl in enumerate(heads):
                o_ref[rows, sl] = o[h]
                rstd = lax.rsqrt(jnp.mean(o[h] * o[h], axis=-1, keepdims=True) + NORM_EPS)
                g_raw = g_ref[rows, sl]
                og_ref[rows, sl] = (o[h] * rstd * gain_v * (g_raw * _sigmoid(g_raw))).astype(BF16)
        for h in range(H):
            s_scr[h] = s_t[h]

    col = lambda j: pl.BlockSpec((CPS * C, D_MODEL), lambda c: (c, j))
    row = pl.BlockSpec((CPS * C, D_MODEL), lambda c: (c, 0))
    return pl.pallas_call(
        body,
        out_shape=(jax.ShapeDtypeStruct((T, D_MODEL), BF16), jax.ShapeDtypeStruct((T, D_MODEL), F32),
                   jax.ShapeDtypeStruct((T // C, H, HD, HD), F32)),
        grid=(T // (CPS * C),),
        in_specs=[col(0), col(1), col(2), col(3), pl.BlockSpec((1, D_MODEL), lambda c: (0, 0)),
                  pl.BlockSpec((1, HD), lambda c: (0, 0))],
        out_specs=(row, row, pl.BlockSpec((CPS, H, HD, HD), lambda c: (c, 0, 0, 0))),
        scratch_shapes=[pltpu.VMEM((H, HD, HD), F32)],
        compiler_params=_params("arbitrary"), name=name)(proj, proj, proj, proj, lb, gain)


def _hgrn_bwd(proj, o_pre, d_og, states, lb, gain, name):
    T = proj.shape[0]
    C = HGRN_CHUNK
    CPS = HGRN_STEP_CHUNKS
    H, HD = HGRN_HEADS, HGRN_DIM
    NC = T // (CPS * C)

    def body(q_ref, f_ref, i_ref, g_ref, o_ref, dog_ref, st_ref, lb_ref, gain_ref,
             dproj_ref, dlb_ref, dgain_ref, ds_scr, dq_all, dk_all, db_all):
        @pl.when(pl.program_id(0) == 0)
        def _():
            ds_scr[...] = jnp.zeros_like(ds_scr)
            dlb_ref[...] = jnp.zeros_like(dlb_ref)
            dgain_ref[...] = jnp.zeros_like(dgain_ref)

        lbv = lb_ref[...]
        causal = _tri(C, True)
        last_row = lax.broadcasted_iota(jnp.int32, (C, HD), 0) == C - 1
        gain_v = gain_ref[...]
        heads = [slice(h * HD, (h + 1) * HD) for h in range(H)]
        hs = range(H)
        ds_t = [ds_scr[h] for h in hs]
        dgain = None
        for cc in reversed(range(CPS)):
            rows = slice(cc * C, (cc + 1) * C)
            dq_scr, dk_scr, db_scr = dq_all.at[cc], dk_all.at[cc], db_all.at[cc]
            q_raw = q_ref[rows, :]
            gt = _hgrn_gates(q_raw, f_ref[rows, :], lbv)
            o = [o_ref[rows, sl] for sl in heads]
            rstd = [lax.rsqrt(jnp.mean(x * x, axis=-1, keepdims=True) + NORM_EPS) for x in o]
            n = [x * r for x, r in zip(o, rstd)]
            g_raw = [g_ref[rows, sl] for sl in heads]
            sg = [_sigmoid(x) for x in g_raw]
            d_out = [dog_ref[rows, sl] for sl in heads]
            dy = [d * (g * s) for d, g, s in zip(d_out, g_raw, sg)]
            dn = [x * gain_v for x in dy]
            do = [(rstd[h] * (dn[h] - n[h] * jnp.mean(dn[h] * n[h], axis=-1, keepdims=True))).astype(BF16) for h in hs]
            for h in hs:
                dgain = dy[h] * n[h] if dgain is None else dgain + dy[h] * n[h]
            for h, sl in enumerate(heads):
                dproj_ref[rows, 3 * D_MODEL + h * HD:3 * D_MODEL + (h + 1) * HD] = (
                    d_out[h] * n[h] * gain_v * (sg[h] * (1.0 + g_raw[h] * (1.0 - sg[h])))).astype(BF16)
            q_ab, k_bb = gt["q_a"].astype(BF16), gt["k_b"].astype(BF16)
            q_hb, k_tb = gt["q_hat"].astype(BF16), gt["k_til"].astype(BF16)
            v = i_ref[rows, :].astype(BF16)
            s_t = [st_ref[cc, h] for h in hs]
            ds_b = [x.astype(BF16) for x in ds_t]
            p = [jnp.where(causal, _dot(q_ab[:, sl], k_bb[:, sl], NT), 0.0).astype(BF16) for sl in heads]
            dp = [jnp.where(causal, _dot(do[h], v[:, sl], NT), 0.0).astype(BF16) for h, sl in enumerate(heads)]
            dv = [_dot(p[h], do[h], TN) + _dot(k_tb[:, sl], ds_b[h], NT) for h, sl in enumerate(heads)]
            dq_a = [_dot(dp[h], k_bb[:, sl], NN) for h, sl in enumerate(heads)]
            dk_b = [_dot(dp[h], q_ab[:, sl], TN) for h, sl in enumerate(heads)]
            dq_hat = [_dot(do[h], s_t[h].astype(BF16), NN) for h in hs]
            dk_til = [_dot(v[:, sl], ds_b[h], NN) for h, sl in enumerate(heads)]
            ds_new = [_dot(do[h], q_hb[:, sl], TN) + gt["e_last"][:, sl] * ds_t[h] for h, sl in enumerate(heads)]
            for h, sl in enumerate(heads):
                k_til = gt["k_til"][:, sl]
                db_last = jnp.sum(ds_t[h] * gt["e_last"][:, sl] * s_t[h], axis=0, keepdims=True) + jnp.sum(
                    dk_til[h] * k_til, axis=0, keepdims=True)
                dproj_ref[rows, 2 * D_MODEL + h * HD:2 * D_MODEL + (h + 1) * HD] = dv[h].astype(BF16)
                dq_scr[:, sl] = dq_a[h] * gt["e_a"][:, sl] + dq_hat[h] * gt["e_q"][:, sl]
                dk_scr[:, sl] = dk_b[h] * gt["e_b"][:, sl] + dk_til[h] * gt["e_k"][:, sl]
                db = (dq_a[h] * q_ab[:, sl].astype(F32) + dq_hat[h] * gt["q_hat"][:, sl]
                      - dk_b[h] * k_bb[:, sl].astype(F32) - dk_til[h] * k_til)
                db_scr[:, sl] = db + jnp.where(last_row, db_last, 0.0)
            dlogf = _running_sum(db_scr[...], False)
            sig_f, forget, sig_q = gt["sig_f"], gt["forget"], gt["sig_q"]
            dforget = dlogf / forget - dk_scr[...]
            dproj_ref[rows, D_MODEL:2 * D_MODEL] = (dforget * (1.0 - lbv) * sig_f * (1.0 - sig_f)).astype(BF16)
            dlb_ref[...] += jnp.sum(dforget * (1.0 - sig_f), axis=0, keepdims=True)
            dproj_ref[rows, 0:D_MODEL] = (dq_scr[...] * (sig_q * (1.0 + q_raw * (1.0 - sig_q)))).astype(BF16)
            ds_t = ds_new
        dgain_ref[...] += jnp.sum(dgain, axis=0, keepdims=True)
        for h in hs:
            ds_scr[h] = ds_t[h]

    col = lambda j: pl.BlockSpec((CPS * C, D_MODEL), lambda c: (NC - 1 - c, j))
    row = pl.BlockSpec((CPS * C, D_MODEL), lambda c: (NC - 1 - c, 0))
    return pl.pallas_call(
        body,
        out_shape=(jax.ShapeDtypeStruct((T, 4 * D_MODEL), BF16), jax.ShapeDtypeStruct((1, D_MODEL), F32),
                   jax.ShapeDtypeStruct((1, HD), F32)),
        grid=(NC,),
        in_specs=[col(0), col(1), col(2), col(3), row, row,
                  pl.BlockSpec((CPS, H, HD, HD), lambda c: (NC - 1 - c, 0, 0, 0)),
                  pl.BlockSpec((1, D_MODEL), lambda c: (0, 0)), pl.BlockSpec((1, HD), lambda c: (0, 0))],
        out_specs=(pl.BlockSpec((CPS * C, 4 * D_MODEL), lambda c: (NC - 1 - c, 0)),
                   pl.BlockSpec((1, D_MODEL), lambda c: (0, 0)), pl.BlockSpec((1, HD), lambda c: (0, 0))),
        scratch_shapes=[pltpu.VMEM((H, HD, HD), F32)] + [pltpu.VMEM((CPS, C, D_MODEL), F32)] * 3,
        compiler_params=_params("arbitrary"), name=name)(proj, proj, proj, proj, o_pre, d_og, states, lb, gain)


def _attn_masks():
    r = lax.broadcasted_iota(jnp.int32, (ATTN_BLOCK, ATTN_BLOCK), 0)
    c = lax.broadcasted_iota(jnp.int32, (ATTN_BLOCK, ATTN_BLOCK), 1)
    return c >= r, c <= r


def _attn_fwd(qkv, dilation, name):
    T = qkv.shape[0]
    nb = T // dilation // ATTN_BLOCK
    W = ATTN_GROUP_WIDTH
    B = ATTN_BLOCK
    scale = ATTN_DIM ** -0.5
    qb = next(n for n in (4, 2, 1) if nb % n == 0)
    steps = nb // qb

    def body(q_ref, kp_ref, kc_ref, vp_ref, vc_ref, o_ref, lse_ref):
        no_prev = jnp.where(pl.program_id(1) > 0, 0.0, NEG_BIG)
        m_prev, m_cur = _attn_masks()
        ones = jnp.ones((B, ATTN_DIM), BF16)
        items = []
        for j in range(qb):
            for h in range(ATTN_GROUP_HEADS):
                sl = slice(h * ATTN_DIM, (h + 1) * ATTN_DIM)
                rows = slice(j * B, (j + 1) * B)
                if j == 0:
                    items.append((rows, sl, kp_ref[:, sl], vp_ref[:, sl], no_prev))
                else:
                    before = slice((j - 1) * B, j * B)
                    items.append((rows, sl, kc_ref[before, sl], vc_ref[before, sl], 0.0))
        s_p = [jnp.where(m_prev, _dot(q_ref[rows, sl], k_p, NT) * scale + bias, NEG_BIG)
               for rows, sl, k_p, _, bias in items]
        s_c = [jnp.where(m_cur, _dot(q_ref[rows, sl], kc_ref[rows, sl], NT) * scale, NEG_BIG)
               for rows, sl, _, _, _ in items]
        m = [jnp.max(jnp.maximum(a, b), axis=-1, keepdims=True) for a, b in zip(s_p, s_c)]
        p_p = [jnp.exp(a - mx).astype(BF16) for a, mx in zip(s_p, m)]
        p_c = [jnp.exp(b - mx).astype(BF16) for b, mx in zip(s_c, m)]
        l = [_dot(a, ones, NN) + _dot(b, ones, NN) for a, b in zip(p_p, p_c)]
        acc = [_dot(a, v_p, NN) + _dot(b, vc_ref[rows, sl], NN)
               for a, b, (rows, sl, _, v_p, _) in zip(p_p, p_c, items)]
        for (rows, sl, _, _, _), a, lv, mx in zip(items, acc, l, m):
            o_ref[rows, sl] = (a / lv).astype(BF16)
            lse_ref[rows, sl] = mx + jnp.log(lv)

    cur = lambda col: pl.BlockSpec((qb * B, W), lambda s, n: (s * steps + n, col))
    prev = lambda col: pl.BlockSpec((B, W), lambda s, n: (s * nb + jnp.maximum(qb * n - 1, 0), col))
    out = pl.BlockSpec((qb * B, W), lambda s, n: (s * steps + n, 0))
    return pl.pallas_call(
        body, out_shape=(jax.ShapeDtypeStruct((T, W), BF16), jax.ShapeDtypeStruct((T, W), F32)),
        grid=(dilation, steps),
        in_specs=[cur(0), prev(1), cur(1), prev(2), cur(2)],
        out_specs=(out, out), compiler_params=_params("parallel", "arbitrary"), name=name)(qkv, qkv, qkv, qkv, qkv)


def _attn_bwd(qkv, d_out, lse, delta, cos, sin, dilation, name):
    T = qkv.shape[0]
    nb = T // dilation // ATTN_BLOCK
    assert nb % 2 == 0, "an even number of 128-token blocks per residue class"
    pairs = nb // 2
    W = ATTN_GROUP_WIDTH
    B = ATTN_BLOCK
    scale = ATTN_DIM ** -0.5

    def unrope(x, cos_v, sin_v):
        return x * cos_v + pltpu.roll(x * sin_v, ATTN_DIM // 2, 1)

    def body(qa_ref, qb_ref, kpair_ref, kc_ref, vpair_ref, vc_ref, doa_ref, dob_ref, lsea_ref, lseb_ref,
             dla_ref, dlb_ref, cos_ref, sin_ref, out_ref, dq_scr, dk_scr, dv_scr):
        n = pl.program_id(1)

        @pl.when(n == 0)
        def _():
            dq_scr[...] = jnp.zeros_like(dq_scr)
            dk_scr[...] = jnp.zeros_like(dk_scr)
            dv_scr[...] = jnp.zeros_like(dv_scr)

        no_a = jnp.where(n > 0, 0.0, NEG_BIG)
        no_b = jnp.where(n < pairs, 0.0, NEG_BIG)
        m_prev, m_cur = _attn_masks()
        lo, hi = slice(0, B), slice(B, 2 * B)
        heads = [slice(h * ATTN_DIM, (h + 1) * ATTN_DIM) for h in range(ATTN_GROUP_HEADS)]
        flat = []
        for sl in heads:
            qa, qb = qa_ref[:, sl], qb_ref[:, sl]
            doa, dob = doa_ref[:, sl], dob_ref[:, sl]
            k0, k1, k2 = kpair_ref[lo, sl], kpair_ref[hi, sl], kc_ref[:, sl]
            v0, v1, v2 = vpair_ref[lo, sl], vpair_ref[hi, sl], vc_ref[:, sl]
            flat += [(qa, doa, lsea_ref[:, sl], dla_ref[:, sl], k0, v0, m_prev, no_a),
                     (qa, doa, lsea_ref[:, sl], dla_ref[:, sl], k1, v1, m_cur, no_a),
                     (qb, dob, lseb_ref[:, sl], dlb_ref[:, sl], k1, v1, m_prev, no_a + no_b),
                     (qb, dob, lseb_ref[:, sl], dlb_ref[:, sl], k2, v2, m_cur, no_b)]
        s = [_dot(q, k, NT) for q, _, _, _, k, _, _, _ in flat]
        dp = [_dot(do, v, NT) for _, do, _, _, _, v, _, _ in flat]
        p = [jnp.where(mask, jnp.exp(sv * scale - lse_v + bias), 0.0)
             for sv, (_, _, lse_v, _, _, _, mask, bias) in zip(s, flat)]
        ds = [(pv * (dpv - dl_v) * scale).astype(BF16) for pv, dpv, (_, _, _, dl_v, _, _, _, _) in zip(p, dp, flat)]
        p = [pv.astype(BF16) for pv in p]
        dq_part = [_dot(dsv, k, NN) for dsv, (_, _, _, _, k, _, _, _) in zip(ds, flat)]
        dk_part = [_dot(dsv, q, TN) for dsv, (q, _, _, _, _, _, _, _) in zip(ds, flat)]
        dv_part = [_dot(pv, do, TN) for pv, (_, do, _, _, _, _, _, _) in zip(p, flat)]
        cos_lo, sin_lo, cos_hi, sin_hi = cos_ref[lo, :], sin_ref[lo, :], cos_ref[hi, :], sin_ref[hi, :]
        for h, sl in enumerate(heads):
            a_prev, a_cur, b_prev, b_cur = range(4 * h, 4 * h + 4)
            kcol = slice(W + h * ATTN_DIM, W + (h + 1) * ATTN_DIM)
            vcol = slice(2 * W + h * ATTN_DIM, 2 * W + (h + 1) * ATTN_DIM)
            out_ref[lo, sl] = unrope(dq_scr[:, sl], cos_lo, sin_lo).astype(BF16)
            out_ref[hi, sl] = unrope(dq_part[a_prev] + dq_part[a_cur], cos_hi, sin_hi).astype(BF16)
            out_ref[lo, kcol] = unrope(dk_scr[:, sl] + dk_part[a_prev], cos_lo, sin_lo).astype(BF16)
            out_ref[hi, kcol] = unrope(dk_part[a_cur] + dk_part[b_prev], cos_hi, sin_hi).astype(BF16)
            out_ref[lo, vcol] = (dv_scr[:, sl] + dv_part[a_prev]).astype(BF16)
            out_ref[hi, vcol] = (dv_part[a_cur] + dv_part[b_prev]).astype(BF16)
            dq_scr[:, sl] = dq_part[b_prev] + dq_part[b_cur]
            dk_scr[:, sl] = dk_part[b_cur]
            dv_scr[:, sl] = dv_part[b_cur]

    def block_a(n):
        return jnp.maximum(2 * n - 1, 0)

    def block_b(n):
        return jnp.minimum(2 * n, nb - 1)

    def pair(n):
        return jnp.maximum(n - 1, 0)

    one_a = lambda col: pl.BlockSpec((B, W), lambda s, n: (s * nb + block_a(n), col))
    one_b = lambda col: pl.BlockSpec((B, W), lambda s, n: (s * nb + block_b(n), col))
    two = lambda col: pl.BlockSpec((2 * B, W), lambda s, n: (s * pairs + pair(n), col))
    tab = pl.BlockSpec((2 * B, ATTN_DIM), lambda s, n: (s * pairs + pair(n), 0))
    return pl.pallas_call(
        body, out_shape=jax.ShapeDtypeStruct((T, 3 * W), BF16), grid=(dilation, pairs + 1),
        in_specs=[one_a(0), one_b(0), two(1), one_b(1), two(2), one_b(2), one_a(0), one_b(0), one_a(0), one_b(0),
                  one_a(0), one_b(0), tab, tab],
        out_specs=pl.BlockSpec((2 * B, 3 * W), lambda s, n: (s * pairs + pair(n), 0)),
        scratch_shapes=[pltpu.VMEM((B, W), F32)] * 3,
        compiler_params=_params("parallel", "arbitrary"), name=name)(
            qkv, qkv, qkv, qkv, qkv, qkv, d_out, d_out, lse, lse, delta, delta, cos, sin)


PERM_TILE = 512
LANES = 128


def _residue_view(x, d):
    return x if d == 1 else x.reshape(d, x.shape[0] // d, x.shape[1])


def _residue_spec(d, tm, cols):
    if d == 1:
        return pl.BlockSpec((tm, cols), lambda i: (i, 0))
    return pl.BlockSpec((d, tm // d, cols), lambda i: (0, i, 0))


def _residue_shape(T, d, cols, dtype):
    return jax.ShapeDtypeStruct((T, cols) if d == 1 else (d, T // d, cols), dtype)


def _class_rows(r, d, tm):
    return pl.ds(r, tm // d, stride=d)


def _attn_norm(h, gain, name):
    T = h.shape[0]
    tm = _pick_tile(T, PERM_TILE, 16 * max(ATTN_DILATIONS))
    dils = ATTN_DILATIONS
    (base_cos, base_sin), (off_cos, off_sin), sign = _rope_parts(T, tm)

    def body(h_ref, g_ref, bc_ref, bs_ref, oc_ref, os_ref, sign_ref, *refs):
        u_refs, c_refs, s_refs, u_scr, c_scr, s_scr = refs[0:3], refs[3:6], refs[6:9], refs[9], refs[10], refs[11]
        hv = h_ref[...]
        rstd = lax.rsqrt(jnp.mean(hv * hv, axis=-1, keepdims=True) + NORM_EPS)
        u = hv * rstd * g_ref[...]
        for j in range(D_MODEL // LANES):
            u_scr[j] = u[:, j * LANES:(j + 1) * LANES]
        bc, bs, oc, osn = bc_ref[0], bs_ref[0], oc_ref[...], os_ref[...]
        c_scr[...] = bc * oc - bs * osn
        s_scr[...] = (bs * oc + bc * osn) * sign_ref[...]
        for d, u_ref, c_ref, s_ref in zip(dils, u_refs, c_refs, s_refs):
            if d == 1:
                u_ref[...] = u.astype(BF16)
                c_ref[...] = c_scr[...]
                s_ref[...] = s_scr[...]
                continue
            for r in range(d):
                rows = _class_rows(r, d, tm)
                for j in range(D_MODEL // LANES):
                    u_ref[r, :, j * LANES:(j + 1) * LANES] = u_scr.at[j][rows, :].astype(BF16)
                c_ref[r] = c_scr[rows, :]
                s_ref[r] = s_scr[rows, :]

    row = pl.BlockSpec((tm, D_MODEL), lambda i: (i, 0))
    base = pl.BlockSpec((1, 1, ATTN_DIM), lambda i: (i, 0, 0))
    off = pl.BlockSpec((tm, ATTN_DIM), lambda i: (0, 0))
    res = pl.pallas_call(
        body,
        out_shape=([_residue_shape(T, d, D_MODEL, BF16) for d in dils]
                   + [_residue_shape(T, d, ATTN_DIM, F32) for d in dils] * 2),
        grid=(T // tm,),
        in_specs=[row, pl.BlockSpec((1, D_MODEL), lambda i: (0, 0)), base, base, off, off,
                  pl.BlockSpec((1, ATTN_DIM), lambda i: (0, 0))],
        out_specs=([_residue_spec(d, tm, D_MODEL) for d in dils] + [_residue_spec(d, tm, ATTN_DIM) for d in dils] * 2),
        scratch_shapes=[pltpu.VMEM((D_MODEL // LANES, tm, LANES), F32), pltpu.VMEM((tm, ATTN_DIM), F32),
                        pltpu.VMEM((tm, ATTN_DIM), F32)],
        compiler_params=_params("parallel"), name=name)(h, gain, base_cos, base_sin, off_cos, off_sin, sign)
    flat = [r.reshape(T, r.shape[-1]) for r in res]
    return flat[0:3], flat[3:6], flat[6:9]


def _attn_merge_fwd(outs, lses, name):
    T = outs[0].shape[0]
    W = ATTN_GROUP_WIDTH
    tm = _pick_tile(T, PERM_TILE, 16 * max(ATTN_DILATIONS))
    dils = ATTN_DILATIONS

    def body(*refs):
        o_refs, l_refs, oc_ref, lse_refs = refs[0:3], refs[3:6], refs[6], refs[7:10]
        o_scr, l_scr, t_scr = refs[10:13]
        nh = ATTN_GROUP_HEADS
        for g, d in enumerate(dils):
            for j in range(nh):
                lanes = slice(j * LANES, (j + 1) * LANES)
                if d == 1:
                    o_scr[g * nh + j] = o_refs[g][:, lanes].astype(F32)
                    l_scr[g * nh + j] = l_refs[g][:, lanes]
                    continue
                for r in range(d):
                    rows = _class_rows(r, d, tm)
                    o_scr.at[g * nh + j][rows, :] = o_refs[g][r, :, lanes].astype(F32)
                    l_scr.at[g * nh + j][rows, :] = l_refs[g][r, :, lanes]
        for j in range(nh):
            lanes = slice(j * LANES, (j + 1) * LANES)
            ls = [l_scr[g * nh + j] for g in range(3)]
            m = jnp.maximum(jnp.maximum(ls[0], ls[1]), ls[2])
            tot = m + jnp.log(jnp.exp(ls[0] - m) + jnp.exp(ls[1] - m) + jnp.exp(ls[2] - m))
            t_scr[j] = tot
            for g, d in enumerate(dils):
                oc_ref[:, g * W + j * LANES:g * W + (j + 1) * LANES] = (
                    o_scr[g * nh + j] * jnp.exp(ls[g] - tot)).astype(BF16)
                if d == 1:
                    lse_refs[g][:, lanes] = tot
                    continue
                for r in range(d):
                    lse_refs[g][r, :, lanes] = t_scr.at[j][_class_rows(r, d, tm), :]

    in_blk = [_residue_spec(d, tm, W) for d in dils]
    n_blk = 3 * ATTN_GROUP_HEADS
    res = pl.pallas_call(
        body, out_shape=[jax.ShapeDtypeStruct((T, 3 * W), BF16)] + [_residue_shape(T, d, W, F32) for d in dils],
        grid=(T // tm,), in_specs=in_blk * 2,
        out_specs=[pl.BlockSpec((tm, 3 * W), lambda i: (i, 0))] + in_blk,
        scratch_shapes=[pltpu.VMEM((n_blk, tm, LANES), F32), pltpu.VMEM((n_blk, tm, LANES), F32),
                        pltpu.VMEM((ATTN_GROUP_HEADS, tm, LANES), F32)],
        compiler_params=_params("parallel"), name=name)(
            *[_residue_view(o, d) for o, d in zip(outs, dils)], *[_residue_view(l, d) for l, d in zip(lses, dils)])
    return res[0], [r.reshape(T, W) for r in res[1:]]


def _attn_merge_bwd(d_oc, oc, name):
    T = d_oc.shape[0]
    W = ATTN_GROUP_WIDTH
    tm = _pick_tile(T, PERM_TILE, 16 * max(ATTN_DILATIONS))
    dils = ATTN_DILATIONS

    def body(d_ref, o_ref, *refs):
        delta_refs, db_refs, dl_scr, d_scr = refs[0:3], refs[3:6], refs[6], refs[7]
        nh = ATTN_GROUP_HEADS
        for j in range(nh):
            tot = jnp.zeros((tm, 1), F32)
            for g in range(3):
                cols = slice(g * W + j * LANES, g * W + (j + 1) * LANES)
                d_blk = d_ref[:, cols]
                d_scr[g * nh + j] = d_blk
                tot = tot + jnp.sum(d_blk * o_ref[:, cols].astype(F32), axis=-1, keepdims=True)
            dl_scr[j] = jnp.broadcast_to(tot, (tm, LANES))
        for g, d in enumerate(dils):
            for j in range(nh):
                lanes = slice(j * LANES, (j + 1) * LANES)
                if d == 1:
                    delta_refs[g][:, lanes] = dl_scr[j]
                    db_refs[g][:, lanes] = d_scr[g * nh + j].astype(BF16)
                    continue
                for r in range(d):
                    rows = _class_rows(r, d, tm)
                    delta_refs[g][r, :, lanes] = dl_scr.at[j][rows, :]
                    db_refs[g][r, :, lanes] = d_scr.at[g * nh + j][rows, :].astype(BF16)

    wide = pl.BlockSpec((tm, 3 * W), lambda i: (i, 0))
    out_blk = [_residue_spec(d, tm, W) for d in dils]
    res = pl.pallas_call(
        body, out_shape=[_residue_shape(T, d, W, F32) for d in dils] + [_residue_shape(T, d, W, BF16) for d in dils],
        grid=(T // tm,), in_specs=[wide, wide], out_specs=out_blk * 2,
        scratch_shapes=[pltpu.VMEM((ATTN_GROUP_HEADS, tm, LANES), F32),
                        pltpu.VMEM((3 * ATTN_GROUP_HEADS, tm, LANES), F32)],
        compiler_params=_params("parallel"), name=name)(d_oc, oc)
    flat = [r.reshape(T, W) for r in res]
    return flat[0:3], flat[3:6]


def _rope_parts(T, tile):
    inv_freq = 1.0 / (ROPE_THETA ** (jnp.arange(0, ATTN_DIM, 2, dtype=F32) / ATTN_DIM))
    inv_freq = jnp.concatenate([inv_freq, inv_freq])[None, :]
    base = (jnp.arange(T // tile, dtype=F32) * tile)[:, None] * inv_freq
    off = jnp.arange(tile, dtype=F32)[:, None] * inv_freq
    sign = jnp.concatenate([-jnp.ones((1, ATTN_DIM // 2), F32), jnp.ones((1, ATTN_DIM // 2), F32)], axis=1)
    return (jnp.cos(base)[:, None, :], jnp.sin(base)[:, None, :]), (jnp.cos(off), jnp.sin(off)), sign


WEIGHT_GROUPS = {"hgrn": ("hgrn_in", "hgrn_out"), "ffn0": ("ffn_in0", "ffn_down0"),
                 "attn": ("qkv", "attn_out"), "ffn1": ("ffn_in1", "ffn_down1")}


def _local_step(x, target, norm_mix, norm_ffn, lb, out_gain, final_gain, fetch, publish):
    g_mix = [norm_mix[0:1], norm_mix[1:2]]
    g_ffn = [norm_ffn[0:1], norm_ffn[1:2]]
    w = {}

    def whole(name):
        return [(w[name], w[name].shape[0], 0)]

    def qkv_parts(g):
        return [(w["qkv"], ATTN_GROUP_WIDTH, 3 * j + g) for j in range(3)]

    def ffn_fwd(h, layer, head=None):
        w.update(fetch(f"ffn{layer}"))
        n, gate, up, a = _ffn_in(h, g_ffn[layer], w[f"ffn_in{layer}"], f"ffn{layer}_in")
        out = _mm_nn([a], [whole(f"ffn_down{layer}")], h, name=f"ffn{layer}_down", head=head)
        return out, (n, gate, up, a)

    def ffn_bwd(h, saved, dh, dhb, layer):
        n, gate, up, a = saved
        w_in = w[f"ffn_in{layer}"]
        dgate, dup = _ffn_down_dx(dhb, w[f"ffn_down{layer}"], gate, up, f"ffn{layer}_down_dx")
        grads = {f"ffn_down{layer}": _mm_tn([a], dhb, name=f"ffn{layer}_down_dw"),
                 f"ffn_in{layer}": _mm_tn([dgate, dup], n, name=f"ffn{layer}_in_dw")}
        publish(f"ffn{layer}", grads)
        return _mm_nn([dgate, dup], [[(w_in, D_FF, 0)], [(w_in, D_FF, 1)]], dh, name=f"ffn{layer}_in_dx",
                      norm=(h, g_ffn[layer]))

    u0 = _rms_fwd(x, g_mix[0], "hgrn_norm")
    w.update(fetch("hgrn"))
    proj = _mm_nt(u0, whole("hgrn_in"), out_dtype=F32, name="hgrn_in")
    og, o_pre, states = _hgrn_fwd(proj, lb, out_gain, "hgrn_fwd")
    h1 = _mm_nn([og], [whole("hgrn_out")], x, name="hgrn_out")
    h2, ffn0 = ffn_fwd(h1, 0)

    u1_g, cos_g, sin_g = _attn_norm(h2, g_mix[1], "attn_norm")
    w.update(fetch("attn"))
    qkv_g, outs, lses = [], [], []
    for g, d in enumerate(ATTN_DILATIONS):
        qkv_g.append(_mm_nt(u1_g[g], qkv_parts(g), out_dtype=BF16, name=f"attn_qkv{g}",
                            rope=(cos_g[g], sin_g[g], 2)))
        o_g, lse_g = _attn_fwd(qkv_g[g], d, f"attn_fwd{g}")
        outs.append(o_g)
        lses.append(lse_g)
    oc, lse_all = _attn_merge_fwd(outs, lses, "attn_merge")
    h3 = _mm_nn([oc], [whole("attn_out")], h2, name="attn_out")
    (dh4, dh4b, d_final, loss_part), ffn1 = ffn_fwd(h3, 1, head=(target, final_gain))
    dh3, dh3b, d_ffn1 = ffn_bwd(h3, ffn1, dh4, dh4b, 1)

    d_oc = _mm_nt(dh3b, whole("attn_out"), out_dtype=F32, name="attn_out_dx")
    grad_attn_out = _mm_tn([oc], dh3b, name="attn_out_dw")
    delta, d_ocb = _attn_merge_bwd(d_oc, oc, "attn_merge_bwd")
    du1, qkv_pieces = [], []
    for g, d in enumerate(ATTN_DILATIONS):
        dqkv = _attn_bwd(qkv_g[g], d_ocb[g], lse_all[g], delta[g], cos_g[g], sin_g[g], d, f"attn_bwd{g}")
        qkv_pieces.append(_mm_tn([dqkv], u1_g[g], name=f"attn_qkv_dw{g}"))
        du1.append(_mm_nn([dqkv], [qkv_parts(g)], None, name=f"attn_qkv_dx{g}"))
    grad_qkv = jnp.stack([p.reshape(3, ATTN_GROUP_WIDTH, D_MODEL) for p in qkv_pieces], axis=1).reshape(
        3 * ATTN_WIDTH, D_MODEL)
    publish("attn", {"qkv": grad_qkv, "attn_out": grad_attn_out})
    dh2, dh2b, d_mix1 = _rms_bwd(h2, g_mix[1], du1, dh3, "attn_norm_bwd", ATTN_DILATIONS)

    dh1, dh1b, d_ffn0 = ffn_bwd(h1, ffn0, dh2, dh2b, 0)

    d_og = _mm_nt(dh1b, whole("hgrn_out"), out_dtype=F32, name="hgrn_out_dx")
    grad_hgrn_out = _mm_tn([og], dh1b, name="hgrn_out_dw")
    dproj, d_lb, d_out_gain = _hgrn_bwd(proj, o_pre, d_og, states, lb, out_gain, "hgrn_bwd")
    publish("hgrn", {"hgrn_in": _mm_tn([dproj], u0, name="hgrn_in_dw"), "hgrn_out": grad_hgrn_out})
    dx, _, d_mix0 = _mm_nn([dproj], [whole("hgrn_in")], dh1, name="hgrn_in_dx", norm=(x, g_mix[0]))

    small = dict(norm_mix0=d_mix0, norm_mix1=d_mix1, norm_ffn0=d_ffn0, norm_ffn1=d_ffn1, lb=d_lb,
                 out_gain=d_out_gain, final=d_final, loss=loss_part)
    return dx, small


MESH_IDS = pl.DeviceIdType.MESH
HBM_SPEC = pl.BlockSpec(memory_space=pl.ANY)


N_PEERS = N_DEV - 1
PEER_OFFSETS = [(dx, dy, dc) for dx in (0, 1) for dy in (0, 1) for dc in (0, 1)][1:]


def _mesh_place():
    x, y, c = lax.axis_index("x"), lax.axis_index("y"), lax.axis_index("c")
    peers = []
    for dx, dy, dc in PEER_OFFSETS:
        px, py, pc = (1 - x if dx else x), (1 - y if dy else y), (1 - c if dc else c)
        peers.append(((px, py, pc), 4 * px + 2 * py + pc))
    return 4 * x + 2 * y + c, peers


def _gather_over_two_levels(src_refs, land_refs, send_sems, recv_sems):
    n = len(src_refs)
    x, y, c = lax.axis_index("x"), lax.axis_index("y"), lax.axis_index("c")
    me, sibling = (x, y, c), (x, y, 1 - c)
    chips = [(1 - x, y), (x, 1 - y), (1 - x, 1 - y)]

    def block(w, px, py, pc):
        return land_refs[w].at[4 * px + 2 * py + pc]

    def copy(w, k, owner, to, src=None):
        return pltpu.make_async_remote_copy(
            src_ref=block(w, *owner) if src is None else src, dst_ref=block(w, *owner),
            send_sem=send_sems.at[w * N_PEERS + k], recv_sem=recv_sems.at[w * N_PEERS + k],
            device_id=to, device_id_type=MESH_IDS)

    sent = []
    for w in range(n):
        sent.append(copy(w, 0, me, sibling, src=src_refs[w]))
        sent += [copy(w, 1 + j, me, (*chip, c), src=src_refs[w]) for j, chip in enumerate(chips)]
    for cp in sent:
        cp.start()
    for w in range(n):
        for j, chip in enumerate(chips):
            copy(w, 1 + j, (*chip, c), me).wait_recv()
            passed = copy(w, 4 + j, (*chip, c), sibling)
            passed.start()
            sent.append(passed)
    for w in range(n):
        copy(w, 0, sibling, me).wait_recv()
        for j, chip in enumerate(chips):
            copy(w, 4 + j, (*chip, 1 - c), me).wait_recv()
    for cp in sent:
        cp.wait_send()


def _exchange_launch(srcs, scatter, collective_id, name):
    n = len(srcs)
    src_refs = [jax.new_ref(s, memory_space=pltpu.MemorySpace.HBM) for s in srcs]
    land_refs = [jax.empty_ref(jax.ShapeDtypeStruct(s.shape if scatter else (N_DEV,) + s.shape, s.dtype),
                               memory_space=pltpu.MemorySpace.HBM) for s in srcs]

    @pl.kernel(mesh=plsc.ScalarSubcoreMesh(axis_name="sequencer", num_cores=1), name=name,
               scratch_types=(pltpu.SemaphoreType.DMA((n * N_PEERS,)), pltpu.SemaphoreType.DMA((n * N_PEERS,)),
                              pltpu.SemaphoreType.DMA((n,))),
               compiler_params=pltpu.CompilerParams(collective_id=collective_id))
    def launch(send_sems, recv_sems, local_sems):
        me, peers = _mesh_place()
        barrier = pltpu.get_barrier_semaphore()
        for peer, _ in peers:
            pl.semaphore_signal(barrier, inc=1, device_id=peer, device_id_type=MESH_IDS)
        pl.semaphore_wait(barrier, N_PEERS)
        own = [pltpu.make_async_copy(src_refs[w].at[me] if scatter else src_refs[w], land_refs[w].at[me],
                                     local_sems.at[w]) for w in range(n)]
        for cp in own:
            cp.start()
        if scatter:
            copies = [pltpu.make_async_remote_copy(
                src_ref=src_refs[w].at[pid], dst_ref=land_refs[w].at[me],
                send_sem=send_sems.at[w * N_PEERS + k], recv_sem=recv_sems.at[w * N_PEERS + k],
                device_id=peer, device_id_type=MESH_IDS) for w in range(n) for k, (peer, pid) in enumerate(peers)]
            for cp in copies:
                cp.start()
            for cp in copies:
                cp.wait()
        else:
            _gather_over_two_levels(src_refs, land_refs, send_sems, recv_sems)
        for cp in own:
            cp.wait()

    launch()
    return land_refs


def _gather_small(block, name):
    def body(in_ref, out_ref, send_sems, recv_sems, local_sem):
        me, peers = _mesh_place()
        own = pltpu.make_async_copy(in_ref, out_ref.at[me], local_sem)
        own.start()
        sends = [pltpu.make_async_remote_copy(
            src_ref=in_ref, dst_ref=out_ref.at[me], send_sem=send_sems.at[k], recv_sem=recv_sems.at[k],
            device_id=peer, device_id_type=MESH_IDS) for k, (peer, _) in enumerate(peers)]
        for cp in sends:
            cp.start()
        for cp in sends:
            cp.wait_recv()
        for cp in sends:
            cp.wait_send()
        own.wait()

    return pl.pallas_call(
        body, out_shape=jax.ShapeDtypeStruct((N_DEV,) + block.shape, block.dtype),
        in_specs=[HBM_SPEC], out_specs=HBM_SPEC,
        scratch_shapes=[pltpu.SemaphoreType.DMA((N_PEERS,)), pltpu.SemaphoreType.DMA((N_PEERS,)),
                        pltpu.SemaphoreType.DMA],
        name=name)(block)


def _sum_blocks(recv, name):
    rows = recv.shape[1]
    tr = _pick_tile(rows, 256, 16)

    def body(r_ref, g_ref):
        acc = r_ref[0].astype(F32)
        for j in range(1, N_DEV):
            acc = acc + r_ref[j].astype(F32)
        g_ref[...] = acc

    return pl.pallas_call(
        body, out_shape=jax.ShapeDtypeStruct((rows, D_MODEL), F32), grid=(rows // tr,),
        in_specs=[pl.BlockSpec((N_DEV, tr, D_MODEL), lambda i: (0, i, 0))],
        out_specs=pl.BlockSpec((tr, D_MODEL), lambda i: (i, 0)),
        compiler_params=_params("parallel"), name=name)(recv)


def _adamw_math(w, g, m, v):
    m_new = ADAM_B1 * m + (1.0 - ADAM_B1) * g
    v_new = ADAM_B2 * v + (1.0 - ADAM_B2) * (g * g)
    m_hat = m_new / (1.0 - ADAM_B1 ** ADAM_STEP)
    v_hat = v_new / (1.0 - ADAM_B2 ** ADAM_STEP)
    delta = -ADAM_LR * (m_hat / (jnp.sqrt(v_hat) + ADAM_EPS) + ADAM_WD * w)
    return delta, m_new, v_new


def _adamw(w, g, m, v, layer, others, name):
    _, rows, cols = w.shape
    tr = _pick_tile(rows, 256, 8)

    def body(w_ref, g_ref, m_ref, v_ref, *refs):
        go_ref, d_ref, mo_ref, vo_ref = refs[-4:]
        gv = g_ref[...]
        go_ref[...] = gv
        d_ref[...], mo_ref[...], vo_ref[...] = _adamw_math(w_ref[...], gv, m_ref[...], v_ref[...])

    one = pl.BlockSpec((None, tr, cols), lambda i: (layer, i, 0))
    in_specs = [one, pl.BlockSpec((tr, cols), lambda i: (i, 0)), one, one]
    args = [w, g, m, v]
    if others is not None:
        in_specs += [HBM_SPEC] * 4
        args += list(others)
    return pl.pallas_call(
        body, out_shape=(jax.ShapeDtypeStruct(w.shape, F32),) * 4, grid=(rows // tr,),
        in_specs=in_specs, out_specs=(one,) * 4,
        input_output_aliases={} if others is None else {4 + i: i for i in range(4)},
        compiler_params=_params("parallel"), name=name)(*args)


ROW_MIX, ROW_FFN, ROW_LB, ROW_OUT_GAIN, ROW_FINAL = 0, 2, 4, 7, 8
PART_MIX, PART_FFN, PART_LB, PART_OUT_GAIN, PART_FINAL, PART_LOSS = 0, 2, 4, 5, 6, 7


def _small_update(parts_all, w, m, v, name):
    def body(p_ref, w_ref, m_ref, v_ref, g_ref, d_ref, mo_ref, vo_ref, loss_ref):
        def total(row, n=1):
            tot = p_ref[0, row:row + n, :]
            for j in range(1, N_DEV):
                tot = tot + p_ref[j, row:row + n, :]
            return tot

        logits = [w_ref[ROW_LB + i:ROW_LB + i + 1, :] for i in range(3)]
        mx = jnp.maximum(jnp.maximum(logits[0], logits[1]), logits[2])
        ex = [jnp.exp(l - mx) for l in logits]
        den = ex[0] + ex[1] + ex[2]
        prob = [e / den for e in ex]
        d_lb = total(PART_LB)
        g_ref[...] = jnp.zeros_like(g_ref)
        g_ref[ROW_MIX:ROW_MIX + 2, :] = total(PART_MIX, 2)
        g_ref[ROW_FFN:ROW_FFN + 2, :] = total(PART_FFN, 2)
        for i in range(3):
            g_ref[ROW_LB + i:ROW_LB + i + 1, :] = prob[i] * ((d_lb if i == 0 else 0.0) - prob[0] * d_lb)
        g_ref[ROW_OUT_GAIN:ROW_OUT_GAIN + 1, :] = total(PART_OUT_GAIN)
        g_ref[ROW_FINAL:ROW_FINAL + 1, :] = total(PART_FINAL)
        d_ref[...], mo_ref[...], vo_ref[...] = _adamw_math(w_ref[...], g_ref[...], m_ref[...], v_ref[...])
        loss_ref[...] = jnp.sum(total(PART_LOSS), axis=-1, keepdims=True)

    packed = jax.ShapeDtypeStruct((16, D_MODEL), F32)
    return pl.pallas_call(
        body, out_shape=(packed, packed, packed, packed, jax.ShapeDtypeStruct((1, 1), F32)),
        compiler_params=pltpu.CompilerParams(vmem_limit_bytes=VMEM_LIMIT), name=name)(parts_all, w, m, v)


def _pack_small(norm_mix, norm_ffn, lb_logits, out_gain, final):
    pad = jnp.zeros((1, D_MODEL - HGRN_DIM), F32)
    return jnp.concatenate([norm_mix, norm_ffn, lb_logits, jnp.concatenate([out_gain, pad], axis=1),
                            final.reshape(1, D_MODEL), jnp.zeros((16 - ROW_FINAL - 1, D_MODEL), F32)], axis=0)


def _unpack_small(p):
    return (p[ROW_MIX:ROW_MIX + 2], p[ROW_FFN:ROW_FFN + 2], p[ROW_LB:ROW_LB + 3],
            p[ROW_OUT_GAIN:ROW_OUT_GAIN + 1, :HGRN_DIM], p[ROW_FINAL])


def _lower_bound(lb_logits, name):
    def body(l_ref, o_ref):
        logits = [l_ref[i:i + 1, :] for i in range(3)]
        mx = jnp.maximum(jnp.maximum(logits[0], logits[1]), logits[2])
        ex = [jnp.exp(l - mx) for l in logits]
        o_ref[...] = ex[0] / (ex[0] + ex[1] + ex[2])

    return pl.pallas_call(body, out_shape=jax.ShapeDtypeStruct((1, D_MODEL), F32), name=name)(lb_logits)


def kernel(x, norm_mix, norm_ffn, hgrn_w_in, hgrn_lb_logits, hgrn_out_norm, hgrn_w_out, attn_w_qkv, attn_w_out, ffn_w_in, ffn_w_down, final_norm, loss_target, m_norm_mix, m_norm_ffn, m_hgrn_w_in, m_hgrn_lb_logits, m_hgrn_out_norm, m_hgrn_w_out, m_attn_w_qkv, m_attn_w_out, m_ffn_w_in, m_ffn_w_down, m_final_norm, v_norm_mix, v_norm_ffn, v_hgrn_w_in, v_hgrn_lb_logits, v_hgrn_out_norm, v_hgrn_w_out, v_attn_w_qkv, v_attn_w_out, v_ffn_w_in, v_ffn_w_down, v_final_norm):
    col_sharded = {"hgrn_in": hgrn_w_in[0], "qkv": attn_w_qkv[0], "ffn_in0": ffn_w_in[0], "ffn_in1": ffn_w_in[1]}
    row_sharded = {"hgrn_out": hgrn_w_out[0], "attn_out": attn_w_out[0], "ffn_down0": ffn_w_down[0],
                   "ffn_down1": ffn_w_down[1]}
    gathering = {}
    for gi, (group, names) in enumerate(WEIGHT_GROUPS.items()):
        shards = [(col_sharded[n].T if n in col_sharded else row_sharded[n]).astype(BF16) for n in names]
        gathering[group] = _exchange_launch(shards, False, 1 + gi, f"weights_gather_{group}")

    def fetch(group):
        return {n: land[...].reshape(-1, D_MODEL) for n, land in zip(WEIGHT_GROUPS[group], gathering[group])}

    in_flight = {}

    def publish(group, grads):
        names = WEIGHT_GROUPS[group]
        parts = [grads[n].reshape(N_DEV, -1, D_MODEL) for n in names]
        in_flight[group] = _exchange_launch(parts, True, 1 + len(WEIGHT_GROUPS) + list(WEIGHT_GROUPS).index(group),
                                            f"grads_send_{group}")

    lb = _lower_bound(hgrn_lb_logits, "hgrn_lower_bound")
    grad_x, small = _local_step(x[0], loss_target[0], norm_mix, norm_ffn, lb, hgrn_out_norm,
                                final_norm.reshape(1, D_MODEL), fetch, publish)

    pad = jnp.zeros((1, D_MODEL - HGRN_DIM), F32)
    small_part = jnp.concatenate(
        [small["norm_mix0"], small["norm_mix1"], small["norm_ffn0"], small["norm_ffn1"], small["lb"],
         jnp.concatenate([small["out_gain"], pad], axis=1), small["final"], small["loss"]], axis=0)
    small_all = _gather_small(small_part, "small_grads_gather")
    received = {}
    for group in ("ffn1", "attn", "ffn0", "hgrn"):
        received.update(zip(WEIGHT_GROUPS[group], [land[...] for land in in_flight[group]]))

    masters = {"hgrn_w_in": (hgrn_w_in, m_hgrn_w_in, v_hgrn_w_in, ("hgrn_in",)),
               "hgrn_w_out": (hgrn_w_out, m_hgrn_w_out, v_hgrn_w_out, ("hgrn_out",)),
               "attn_w_qkv": (attn_w_qkv, m_attn_w_qkv, v_attn_w_qkv, ("qkv",)),
               "attn_w_out": (attn_w_out, m_attn_w_out, v_attn_w_out, ("attn_out",)),
               "ffn_w_in": (ffn_w_in, m_ffn_w_in, v_ffn_w_in, ("ffn_in0", "ffn_in1")),
               "ffn_w_down": (ffn_w_down, m_ffn_w_down, v_ffn_w_down, ("ffn_down0", "ffn_down1"))}
    big = {}
    for param, (wv, mv, vv, names) in masters.items():
        outs = None
        for layer, n in enumerate(names):
            g = _sum_blocks(received[n], f"{n}_grad_sum")
            outs = _adamw(wv, g.T if n in col_sharded else g, mv, vv, layer, outs, f"{n}_adamw")
        big[param] = list(outs)

    w_small = _pack_small(norm_mix, norm_ffn, hgrn_lb_logits, hgrn_out_norm, final_norm)
    m_small = _pack_small(m_norm_mix, m_norm_ffn, m_hgrn_lb_logits, m_hgrn_out_norm, m_final_norm)
    v_small = _pack_small(v_norm_mix, v_norm_ffn, v_hgrn_lb_logits, v_hgrn_out_norm, v_final_norm)
    g_s, d_s, m_s, v_s, loss = _small_update(small_all, w_small, m_small, v_small, "small_update")
    small_out = [_unpack_small(t) for t in (g_s, d_s, m_s, v_s)]

    def group(i):
        s = small_out[i]
        return (s[0], s[1], big["hgrn_w_in"][i], s[2], s[3], big["hgrn_w_out"][i], big["attn_w_qkv"][i],
                big["attn_w_out"][i], big["ffn_w_in"][i], big["ffn_w_down"][i], s[4])

    return (loss.reshape(()), grad_x[None], *group(0), *group(1), *group(2), *group(3))
```

```python
import functools

import jax
import jax.numpy as jnp
from jax import lax
from jax.experimental import pallas as pl
from jax.experimental.pallas import tpu as pltpu
from jax.experimental.pallas import tpu_sc as plsc

F32 = jnp.float32
BF16 = jnp.bfloat16

D_MODEL = 1024
N_DEV = 8
NORM_EPS = 1e-6

HGRN_HEADS = 8
HGRN_DIM = 128
HGRN_CHUNK = 64
HGRN_STEP_CHUNKS = 4
HGRN_FWD_STEP_CHUNKS = 8
HGRN_EXP_CLAMP = 60.0

ATTN_DIM = 128
ATTN_BLOCK = 128
ATTN_GROUP_HEADS = 4
ATTN_GROUP_WIDTH = ATTN_GROUP_HEADS * ATTN_DIM
ATTN_DILATIONS = (1, 4, 16)
ATTN_WIDTH = 3 * ATTN_GROUP_WIDTH
ROPE_THETA = 10000.0
NEG_BIG = -1e30

D_FF = 2816

ADAM_LR = 0.001
ADAM_B1 = 0.9
ADAM_B2 = 0.999
ADAM_EPS = 1e-08
ADAM_WD = 0.01
ADAM_STEP = 10

VMEM_LIMIT = 48 * 1024 * 1024

NT = (((1,), (1,)), ((), ()))
NN = (((1,), (0,)), ((), ()))
TN = (((0,), (0,)), ((), ()))


def _dot(a, b, dims):
    return lax.dot_general(a, b, dims, preferred_element_type=F32)


def _params(*sem):
    return pltpu.CompilerParams(dimension_semantics=sem, vmem_limit_bytes=VMEM_LIMIT)


def _pick_tile(n, cap, mult):
    best = None
    for t in range(mult, min(n, cap) + 1, mult):
        if n % t == 0:
            best = t
    assert best is not None, (n, cap, mult)
    return best


def _sigmoid(x):
    return 0.5 * jnp.tanh(0.5 * x) + 0.5


ROW_TILE = 512
COL_CHUNK = 512
GRAD_TILE = 256


def _whole(shape, index_map):
    return pl.BlockSpec(shape, index_map, pipeline_mode=pl.Buffered(1))


def _part_specs(parts, n_cols):
    return [_whole((rows, n_cols), functools.partial(lambda i, b: (b, 0), b=blk)) for _, rows, blk in parts]


def _mm_nt(a, w_parts, *, out_dtype, name, rope=None):
    M, K = a.shape
    tm = _pick_tile(M, ROW_TILE, 16)
    widths = [rows for _, rows, _ in w_parts]
    n_parts = len(w_parts)

    def body(*refs):
        a_ref, w_refs, o_ref = refs[0], refs[1:1 + n_parts], refs[-1]
        av = a_ref[...]
        off = 0
        for p, w_ref in enumerate(w_refs):
            for c0 in range(0, widths[p], COL_CHUNK):
                cw = min(COL_CHUNK, widths[p] - c0)
                acc = _dot(av, w_ref[c0:c0 + cw, :], NT)
                if rope is not None and p < rope[2]:
                    cos, sin = refs[1 + n_parts][...], refs[2 + n_parts][...]
                    for h0 in range(0, cw, ATTN_DIM):
                        xh = acc[:, h0:h0 + ATTN_DIM]
                        rot = pltpu.roll(xh, ATTN_DIM // 2, 1)
                        o_ref[:, off + c0 + h0:off + c0 + h0 + ATTN_DIM] = (xh * cos + rot * sin).astype(out_dtype)
                else:
                    o_ref[:, off + c0:off + c0 + cw] = acc.astype(out_dtype)
            off += widths[p]

    in_specs = [pl.BlockSpec((tm, K), lambda i: (i, 0))] + _part_specs(w_parts, K)
    args = [a] + [w for w, _, _ in w_parts]
    if rope is not None:
        in_specs += [pl.BlockSpec((tm, ATTN_DIM), lambda i: (i, 0))] * 2
        args += [rope[0], rope[1]]
    return pl.pallas_call(
        body, out_shape=jax.ShapeDtypeStruct((M, sum(widths)), out_dtype), grid=(M // tm,),
        in_specs=in_specs, out_specs=pl.BlockSpec((tm, sum(widths)), lambda i: (i, 0)),
        compiler_params=_params("parallel"), name=name)(*args)


def _mm_nn(a_list, w_parts_list, resid, *, name, norm=None, head=None):
    M = a_list[0].shape[0]
    tm = _pick_tile(M, ROW_TILE, 16)
    n_a = len(a_list)
    flat_parts = [p for parts in w_parts_list for p in parts]
    extra = norm if norm is not None else head
    n_in = n_a + len(flat_parts) + (1 if resid is not None else 0) + (2 if extra is not None else 0)

    def body(*refs):
        a_refs, w_refs = refs[:n_a], refs[n_a:n_a + len(flat_parts)]

        def product(rows):
            acc = None
            wi = 0
            for a_ref, parts in zip(a_refs, w_parts_list):
                off = 0
                for _, k, _ in parts:
                    term = _dot(a_ref[rows, off:off + k], w_refs[wi][...], NN)
                    acc = term if acc is None else acc + term
                    off += k
                    wi += 1
            return acc

        if extra is None:
            acc = product(slice(None))
            if resid is not None:
                acc = acc + refs[n_in - 1][...]
            refs[n_in][...] = acc
            return

        @pl.when(pl.program_id(0) == 0)
        def _():
            for acc_ref in refs[n_in + 2:]:
                acc_ref[...] = jnp.zeros_like(acc_ref)

        for r0 in range(0, tm, tm // 2):
            rows = slice(r0, r0 + tm // 2)
            acc = product(rows)
            if head is not None:
                _loss_head_math(acc + refs[n_in - 3][rows, :], rows, refs[n_in - 2], refs[n_in - 1],
                                *refs[n_in:n_in + 4])
                continue
            dres_ref, x_ref, g_ref = refs[n_in - 3:n_in]
            dx_ref, dxb_ref, dg_ref = refs[n_in:n_in + 3]
            xv = x_ref[rows, :]
            rstd = lax.rsqrt(jnp.mean(xv * xv, axis=-1, keepdims=True) + NORM_EPS)
            n = xv * rstd
            dg_ref[...] += jnp.sum(acc * n, axis=0, keepdims=True)
            dn = acc * g_ref[...]
            dx = dres_ref[rows, :] + rstd * (dn - n * jnp.mean(dn * n, axis=-1, keepdims=True))
            dx_ref[rows, :] = dx
            dxb_ref[rows, :] = dx.astype(BF16)

    row = pl.BlockSpec((tm, D_MODEL), lambda i: (i, 0))
    vec = pl.BlockSpec((1, D_MODEL), lambda i: (0, 0))
    in_specs = [pl.BlockSpec((tm, a.shape[1]), lambda i: (i, 0)) for a in a_list] + _part_specs(flat_parts, D_MODEL)
    args = list(a_list) + [w for w, _, _ in flat_parts]
    if resid is not None:
        in_specs.append(row)
        args.append(resid)
    if extra is None:
        return pl.pallas_call(
            body, out_shape=jax.ShapeDtypeStruct((M, D_MODEL), F32), grid=(M // tm,),
            in_specs=in_specs, out_specs=row, compiler_params=_params("parallel"), name=name)(*args)
    assert resid is not None
    out_shape = [jax.ShapeDtypeStruct((M, D_MODEL), F32), jax.ShapeDtypeStruct((M, D_MODEL), BF16),
                 jax.ShapeDtypeStruct((1, D_MODEL), F32)]
    out_specs = [row, row, vec]
    if head is not None:
        out_shape.append(jax.ShapeDtypeStruct((1, D_MODEL), F32))
        out_specs.append(vec)
    return pl.pallas_call(
        body, out_shape=out_shape, grid=(M // tm,), in_specs=in_specs + [row, vec], out_specs=out_specs,
        compiler_params=_params("arbitrary"), name=name)(*args, extra[0], extra[1])


def _mm_tn(a_list, b, *, name):
    T = a_list[0].shape[0]
    N = b.shape[1]
    tr = GRAD_TILE
    tiles = [a.shape[1] // tr for a in a_list]
    starts = [sum(tiles[:i]) for i in range(len(tiles))]

    def body(*refs):
        a_refs, b_ref, o_ref = refs[:len(a_list)], refs[len(a_list)], refs[-1]
        r = pl.program_id(0)
        for a_ref, first, count in zip(a_refs, starts, tiles):
            @pl.when(jnp.logical_and(r >= first, r < first + count))
            def _():
                o_ref[...] = _dot(a_ref[...], b_ref[...], TN).astype(BF16)

    in_specs = [pl.BlockSpec((T, tr), functools.partial(lambda r, first, count: (0, jnp.clip(r - first, 0, count - 1)),
                                                        first=first, count=count))
                for first, count in zip(starts, tiles)]
    in_specs.append(_whole((T, N), lambda r: (0, 0)))
    return pl.pallas_call(
        body, out_shape=jax.ShapeDtypeStruct((sum(tiles) * tr, N), BF16), grid=(sum(tiles),),
        in_specs=in_specs, out_specs=pl.BlockSpec((tr, N), lambda r: (r, 0)),
        compiler_params=_params("parallel"), name=name)(*a_list, b)


def _rms_fwd(x, gain, name):
    T = x.shape[0]
    tm = _pick_tile(T, 512, 16)

    def body(x_ref, g_ref, u_ref):
        xv = x_ref[...]
        rstd = lax.rsqrt(jnp.mean(xv * xv, axis=-1, keepdims=True) + NORM_EPS)
        u_ref[...] = (xv * rstd * g_ref[...]).astype(BF16)

    return pl.pallas_call(
        body, out_shape=jax.ShapeDtypeStruct((T, D_MODEL), BF16), grid=(T // tm,),
        in_specs=[pl.BlockSpec((tm, D_MODEL), lambda i: (i, 0)), pl.BlockSpec((1, D_MODEL), lambda i: (0, 0))],
        out_specs=pl.BlockSpec((tm, D_MODEL), lambda i: (i, 0)),
        compiler_params=_params("parallel"), name=name)(x, gain)


def _rms_bwd(x, gain, dus, dres, name, dilations=(1,)):
    T = x.shape[0]
    tm = _pick_tile(T, PERM_TILE, 16 * max(dilations))
    n_du = len(dus)

    def body(x_ref, g_ref, *refs):
        du_refs, dres_ref = refs[:n_du], refs[n_du]
        dx_ref, dxb_ref, dg_ref, du_scr = refs[n_du + 1:]

        @pl.when(pl.program_id(0) == 0)
        def _():
            dg_ref[...] = jnp.zeros_like(dg_ref)

        if tuple(dilations) == (1,):
            du = du_refs[0][...]
        else:
            for i, (d, du_ref) in enumerate(zip(dilations, du_refs)):
                for j in range(D_MODEL // LANES):
                    lanes = slice(j * LANES, (j + 1) * LANES)
                    if d == 1:
                        du_scr[j] = du_ref[:, lanes] if i == 0 else du_scr[j] + du_ref[:, lanes]
                        continue
                    blk = du_scr.at[j]
                    for r in range(d):
                        rows = _class_rows(r, d, tm)
                        blk[rows, :] = du_ref[r, :, lanes] if i == 0 else blk[rows, :] + du_ref[r, :, lanes]
            du = jnp.concatenate([du_scr[j] for j in range(D_MODEL // LANES)], axis=1)
        xv = x_ref[...]
        rstd = lax.rsqrt(jnp.mean(xv * xv, axis=-1, keepdims=True) + NORM_EPS)
        n = xv * rstd
        dg_ref[...] += jnp.sum(du * n, axis=0, keepdims=True)
        dn = du * g_ref[...]
        dx = dres_ref[...] + rstd * (dn - n * jnp.mean(dn * n, axis=-1, keepdims=True))
        dx_ref[...] = dx
        dxb_ref[...] = dx.astype(BF16)

    row = pl.BlockSpec((tm, D_MODEL), lambda i: (i, 0))
    vec = pl.BlockSpec((1, D_MODEL), lambda i: (0, 0))
    return pl.pallas_call(
        body,
        out_shape=(jax.ShapeDtypeStruct((T, D_MODEL), F32), jax.ShapeDtypeStruct((T, D_MODEL), BF16),
                   jax.ShapeDtypeStruct((1, D_MODEL), F32)),
        grid=(T // tm,), in_specs=[row, vec] + [_residue_spec(d, tm, D_MODEL) for d in dilations] + [row],
        out_specs=(row, row, vec), scratch_shapes=[pltpu.VMEM((D_MODEL // LANES, tm, LANES), F32)],
        compiler_params=_params("arbitrary"), name=name)(
            x, gain, *[_residue_view(du, d) for du, d in zip(dus, dilations)], dres)


def _loss_head_math(hv, rows, t_ref, g_ref, dh_ref, dhb_ref, dg_ref, loss_ref):
    inv_f = 1.0 / D_MODEL
    g = g_ref[...]
    rstd = lax.rsqrt(jnp.mean(hv * hv, axis=-1, keepdims=True) + NORM_EPS)
    n = hv * rstd
    err = n * g - t_ref[rows, :]
    loss_ref[...] += (0.5 * inv_f) * jnp.sum(err * err, axis=0, keepdims=True)
    dy = err * inv_f
    dg_ref[...] += jnp.sum(dy * n, axis=0, keepdims=True)
    dn = dy * g
    dh = rstd * (dn - n * jnp.mean(dn * n, axis=-1, keepdims=True))
    dh_ref[rows, :] = dh
    dhb_ref[rows, :] = dh.astype(BF16)


FFN_TILE = 256


def _ffn_in(h, gain, w_in, name):
    T = h.shape[0]
    tm = _pick_tile(T, ROW_TILE, 16)

    def body(h_ref, g_ref, w_ref, n_ref, gate_ref, up_ref, a_ref):
        hv = h_ref[...]
        rstd = lax.rsqrt(jnp.mean(hv * hv, axis=-1, keepdims=True) + NORM_EPS)
        n = (hv * rstd * g_ref[...]).astype(BF16)
        n_ref[...] = n
        for c0 in range(0, D_FF, FFN_TILE):
            cols = slice(c0, c0 + FFN_TILE)
            gate = _dot(n, w_ref[c0:c0 + FFN_TILE, :], NT)
            up = _dot(n, w_ref[D_FF + c0:D_FF + c0 + FFN_TILE, :], NT)
            gate_ref[:, cols] = gate.astype(BF16)
            up_ref[:, cols] = up.astype(BF16)
            a_ref[:, cols] = (gate * _sigmoid(gate) * up).astype(BF16)

    row = pl.BlockSpec((tm, D_MODEL), lambda i: (i, 0))
    wide = pl.BlockSpec((tm, D_FF), lambda i: (i, 0))
    wide_shape = jax.ShapeDtypeStruct((T, D_FF), BF16)
    return pl.pallas_call(
        body, out_shape=(jax.ShapeDtypeStruct((T, D_MODEL), BF16), wide_shape, wide_shape, wide_shape),
        grid=(T // tm,),
        in_specs=[row, pl.BlockSpec((1, D_MODEL), lambda i: (0, 0)), _whole((2 * D_FF, D_MODEL), lambda i: (0, 0))],
        out_specs=(row, wide, wide, wide), compiler_params=_params("parallel"), name=name)(h, gain, w_in)


def _ffn_down_dx(dhb, w_down, gate, up, name):
    T = dhb.shape[0]
    tm = _pick_tile(T, ROW_TILE, 16)

    def body(dh_ref, w_ref, gate_ref, up_ref, dgate_ref, dup_ref):
        dh = dh_ref[...]
        for c0 in range(0, D_FF, FFN_TILE):
            cols = slice(c0, c0 + FFN_TILE)
            da = _dot(dh, w_ref[c0:c0 + FFN_TILE, :], NT).astype(BF16)
            gate = gate_ref[:, cols]
            sg = (0.5 * jnp.tanh(0.5 * jnp.abs(gate)) + 0.5) * jnp.exp(jnp.minimum(gate, 0.0))
            silu = gate * sg
            dgate_ref[:, cols] = da * up_ref[:, cols] * (sg + silu * (1.0 - sg))
            dup_ref[:, cols] = da * silu

    wide = pl.BlockSpec((tm, D_FF), lambda i: (i, 0))
    wide_shape = jax.ShapeDtypeStruct((T, D_FF), BF16)
    return pl.pallas_call(
        body, out_shape=(wide_shape, wide_shape), grid=(T // tm,),
        in_specs=[pl.BlockSpec((tm, D_MODEL), lambda i: (i, 0)), _whole((D_FF, D_MODEL), lambda i: (0, 0)), wide, wide],
        out_specs=(wide, wide), compiler_params=_params("parallel"), name=name)(dhb, w_down, gate, up)


def _tri(n, lower):
    r = lax.broadcasted_iota(jnp.int32, (n, n), 0)
    c = lax.broadcasted_iota(jnp.int32, (n, n), 1)
    return (c <= r) if lower else (c >= r)


def _running_sum(x, lower):
    tri = _tri(x.shape[0], lower).astype(BF16)
    hi = x.astype(BF16)
    rest = x - hi.astype(F32)
    mid = rest.astype(BF16)
    lo = (rest - mid.astype(F32)).astype(BF16)
    return _dot(tri, hi, NN) + _dot(tri, mid, NN) + _dot(tri, lo, NN)


def _hgrn_gates(q_raw, f_raw, lb):
    C = q_raw.shape[0]
    sig_f = _sigmoid(f_raw)
    forget = lb + (1.0 - lb) * sig_f
    key = 1.0 - forget
    log_f = jnp.log(forget)
    b = _running_sum(log_f, True)
    first_half = lax.broadcasted_iota(jnp.int32, log_f.shape, 0) < C // 2
    r = jnp.sum(jnp.where(first_half, log_f, 0.0), axis=0, keepdims=True)
    b_last = jnp.sum(log_f, axis=0, keepdims=True)
    e_a = jnp.exp(jnp.minimum(b - r, HGRN_EXP_CLAMP))
    e_b = jnp.exp(jnp.minimum(r - b, HGRN_EXP_CLAMP))
    e_q = jnp.exp(b)
    e_k = jnp.exp(b_last - b)
    sig_q = _sigmoid(q_raw)
    query = q_raw * sig_q
    return dict(sig_f=sig_f, forget=forget, sig_q=sig_q, e_a=e_a, e_b=e_b, e_q=e_q, e_k=e_k,
                e_last=jnp.exp(b_last), q_a=query * e_a, k_b=key * e_b, q_hat=query * e_q, k_til=key * e_k)


def _hgrn_fwd(proj, lb, gain, name):
    T = proj.shape[0]
    C = HGRN_CHUNK
    CPS = HGRN_FWD_STEP_CHUNKS
    H, HD = HGRN_HEADS, HGRN_DIM

    def body(q_ref, f_ref, i_ref, g_ref, lb_ref, gain_ref, og_ref, o_ref, st_ref, s_scr):
        @pl.when(pl.program_id(0) == 0)
        def _():
            s_scr[...] = jnp.zeros_like(s_scr)

        causal = _tri(C, True)
        gain_v = gain_ref[...]
        heads = [slice(h * HD, (h + 1) * HD) for h in range(H)]
        s_t = [s_scr[h] for h in range(H)]
        for cc in range(CPS):
            rows = slice(cc * C, (cc + 1) * C)
            for h in range(H):
                st_ref[cc, h] = s_t[h]
            gt = _hgrn_gates(q_ref[rows, :], f_ref[rows, :], lb_ref[...])
            q_a, k_b = gt["q_a"].astype(BF16), gt["k_b"].astype(BF16)
            q_hat, k_til = gt["q_hat"].astype(BF16), gt["k_til"].astype(BF16)
            v = i_ref[rows, :].astype(BF16)
            p = [jnp.where(causal, _dot(q_a[:, sl], k_b[:, sl], NT), 0.0).astype(BF16) for sl in heads]
            o = [_dot(p[h], v[:, sl], NN) + _dot(q_hat[:, sl], s_t[h].astype(BF16), NT)
                 for h, sl in enumerate(heads)]
            s_t = [gt["e_last"][:, sl] * s_t[h] + _dot(v[:, sl], k_til[:, sl], TN) for h, sl in enumerate(heads)]
            for h, sl in enumerate(heads):
                o_ref[rows, sl] = o[h]
                rstd = lax.rsqrt(jnp.mean(o[h] * o[h], axis=-1, keepdims=True) + NORM_EPS)
                g_raw = g_ref[rows, sl]
                og_ref[rows, sl] = (o[h] * rstd * gain_v * (g_raw * _sigmoid(g_raw))).astype(BF16)
        for h in range(H):
            s_scr[h] = s_t[h]

    col = lambda j: pl.BlockSpec((CPS * C, D_MODEL), lambda c: (c, j))
    row = pl.BlockSpec((CPS * C, D_MODEL), lambda c: (c, 0))
    return pl.pallas_call(
        body,
        out_shape=(jax.ShapeDtypeStruct((T, D_MODEL), BF16), jax.ShapeDtypeStruct((T, D_MODEL), F32),
                   jax.ShapeDtypeStruct((T // C, H, HD, HD), F32)),
        grid=(T // (CPS * C),),
        in_specs=[col(0), col(1), col(2), col(3), pl.BlockSpec((1, D_MODEL), lambda c: (0, 0)),
                  pl.BlockSpec((1, HD), lambda c: (0, 0))],
        out_specs=(row, row, pl.BlockSpec((CPS, H, HD, HD), lambda c: (c, 0, 0, 0))),
        scratch_shapes=[pltpu.VMEM((H, HD, HD), F32)],
        compiler_params=_params("arbitrary"), name=name)(proj, proj, proj, proj, lb, gain)


def _hgrn_bwd(proj, o_pre, d_og, states, lb, gain, name):
    T = proj.shape[0]
    C = HGRN_CHUNK
    CPS = HGRN_STEP_CHUNKS
    H, HD = HGRN_HEADS, HGRN_DIM
    NC = T // (CPS * C)

    def body(q_ref, f_ref, i_ref, g_ref, o_ref, dog_ref, st_ref, lb_ref, gain_ref,
             dproj_ref, dlb_ref, dgain_ref, ds_scr, dq_all, dk_all, db_all):
        @pl.when(pl.program_id(0) == 0)
        def _():
            ds_scr[...] = jnp.zeros_like(ds_scr)
            dlb_ref[...] = jnp.zeros_like(dlb_ref)
            dgain_ref[...] = jnp.zeros_like(dgain_ref)

        lbv = lb_ref[...]
        causal = _tri(C, True)
        last_row = lax.broadcasted_iota(jnp.int32, (C, HD), 0) == C - 1
        gain_v = gain_ref[...]
        heads = [slice(h * HD, (h + 1) * HD) for h in range(H)]
        hs = range(H)
        ds_t = [ds_scr[h] for h in hs]
        dgain = None
        for cc in reversed(range(CPS)):
            rows = slice(cc * C, (cc + 1) * C)
            dq_scr, dk_scr, db_scr = dq_all.at[cc], dk_all.at[cc], db_all.at[cc]
            q_raw = q_ref[rows, :]
            gt = _hgrn_gates(q_raw, f_ref[rows, :], lbv)
            o = [o_ref[rows, sl] for sl in heads]
            rstd = [lax.rsqrt(jnp.mean(x * x, axis=-1, keepdims=True) + NORM_EPS) for x in o]
            n = [x * r for x, r in zip(o, rstd)]
            g_raw = [g_ref[rows, sl] for sl in heads]
            sg = [_sigmoid(x) for x in g_raw]
            d_out = [dog_ref[rows, sl] for sl in heads]
            dy = [d * (g * s) for d, g, s in zip(d_out, g_raw, sg)]
            dn = [x * gain_v for x in dy]
            do = [(rstd[h] * (dn[h] - n[h] * jnp.mean(dn[h] * n[h], axis=-1, keepdims=True))).astype(BF16) for h in hs]
            for h in hs:
                dgain = dy[h] * n[h] if dgain is None else dgain + dy[h] * n[h]
            for h, sl in enumerate(heads):
                dproj_ref[rows, 3 * D_MODEL + h * HD:3 * D_MODEL + (h + 1) * HD] = (
                    d_out[h] * n[h] * gain_v * (sg[h] * (1.0 + g_raw[h] * (1.0 - sg[h])))).astype(BF16)
            q_ab, k_bb = gt["q_a"].astype(BF16), gt["k_b"].astype(BF16)
            q_hb, k_tb = gt["q_hat"].astype(BF16), gt["k_til"].astype(BF16)
            v = i_ref[rows, :].astype(BF16)
            s_t = [st_ref[cc, h] for h in hs]
            ds_b = [x.astype(BF16) for x in ds_t]
            p = [jnp.where(causal, _dot(q_ab[:, sl], k_bb[:, sl], NT), 0.0).astype(BF16) for sl in heads]
            dp = [jnp.where(causal, _dot(do[h], v[:, sl], NT), 0.0).astype(BF16) for h, sl in enumerate(heads)]
            dv = [_dot(p[h], do[h], TN) + _dot(k_tb[:, sl], ds_b[h], NT) for h, sl in enumerate(heads)]
            dq_a = [_dot(dp[h], k_bb[:, sl], NN) for h, sl in enumerate(heads)]
            dk_b = [_dot(dp[h], q_ab[:, sl], TN) for h, sl in enumerate(heads)]
            dq_hat = [_dot(do[h], s_t[h].astype(BF16), NN) for h in hs]
            dk_til = [_dot(v[:, sl], ds_b[h], NN) for h, sl in enumerate(heads)]
            ds_new = [_dot(do[h], q_hb[:, sl], TN) + gt["e_last"][:, sl] * ds_t[h] for h, sl in enumerate(heads)]
            for h, sl in enumerate(heads):
                k_til = gt["k_til"][:, sl]
                db_last = jnp.sum(ds_t[h] * gt["e_last"][:, sl] * s_t[h], axis=0, keepdims=True) + jnp.sum(
                    dk_til[h] * k_til, axis=0, keepdims=True)
                dproj_ref[rows, 2 * D_MODEL + h * HD:2 * D_MODEL + (h + 1) * HD] = dv[h].astype(BF16)
                dq_scr[:, sl] = dq_a[h] * gt["e_a"][:, sl] + dq_hat[h] * gt["e_q"][:, sl]
                dk_scr[:, sl] = dk_b[h] * gt["e_b"][:, sl] + dk_til[h] * gt["e_k"][:, sl]
                db = (dq_a[h] * q_ab[:, sl].astype(F32) + dq_hat[h] * gt["q_hat"][:, sl]
                      - dk_b[h] * k_bb[:, sl].astype(F32) - dk_til[h] * k_til)
                db_scr[:, sl] = db + jnp.where(last_row, db_last, 0.0)
            dlogf = _running_sum(db_scr[...], False)
            sig_f, forget, sig_q = gt["sig_f"], gt["forget"], gt["sig_q"]
            dforget = dlogf / forget - dk_scr[...]
            dproj_ref[rows, D_MODEL:2 * D_MODEL] = (dforget * (1.0 - lbv) * sig_f * (1.0 - sig_f)).astype(BF16)
            dlb_ref[...] += jnp.sum(dforget * (1.0 - sig_f), axis=0, keepdims=True)
            dproj_ref[rows, 0:D_MODEL] = (dq_scr[...] * (sig_q * (1.0 + q_raw * (1.0 - sig_q)))).astype(BF16)
            ds_t = ds_new
        dgain_ref[...] += jnp.sum(dgain, axis=0, keepdims=True)
        for h in hs:
            ds_scr[h] = ds_t[h]

    col = lambda j: pl.BlockSpec((CPS * C, D_MODEL), lambda c: (NC - 1 - c, j))
    row = pl.BlockSpec((CPS * C, D_MODEL), lambda c: (NC - 1 - c, 0))
    return pl.pallas_call(
        body,
        out_shape=(jax.ShapeDtypeStruct((T, 4 * D_MODEL), BF16), jax.ShapeDtypeStruct((1, D_MODEL), F32),
                   jax.ShapeDtypeStruct((1, HD), F32)),
        grid=(NC,),
        in_specs=[col(0), col(1), col(2), col(3), row, row,
                  pl.BlockSpec((CPS, H, HD, HD), lambda c: (NC - 1 - c, 0, 0, 0)),
                  pl.BlockSpec((1, D_MODEL), lambda c: (0, 0)), pl.BlockSpec((1, HD), lambda c: (0, 0))],
        out_specs=(pl.BlockSpec((CPS * C, 4 * D_MODEL), lambda c: (NC - 1 - c, 0)),
                   pl.BlockSpec((1, D_MODEL), lambda c: (0, 0)), pl.BlockSpec((1, HD), lambda c: (0, 0))),
        scratch_shapes=[pltpu.VMEM((H, HD, HD), F32)] + [pltpu.VMEM((CPS, C, D_MODEL), F32)] * 3,
        compiler_params=_params("arbitrary"), name=name)(proj, proj, proj, proj, o_pre, d_og, states, lb, gain)


def _attn_masks():
    r = lax.broadcasted_iota(jnp.int32, (ATTN_BLOCK, ATTN_BLOCK), 0)
    c = lax.broadcasted_iota(jnp.int32, (ATTN_BLOCK, ATTN_BLOCK), 1)
    return c >= r, c <= r


def _attn_fwd(qkv, dilation, name):
    T = qkv.shape[0]
    nb = T // dilation // ATTN_BLOCK
    W = ATTN_GROUP_WIDTH
    B = ATTN_BLOCK
    scale = ATTN_DIM ** -0.5
    qb = next(n for n in (8, 4, 2, 1) if nb % n == 0)
    steps = nb // qb

    def body(q_ref, kp_ref, kc_ref, vp_ref, vc_ref, o_ref, lse_ref):
        no_prev = jnp.where(pl.program_id(1) > 0, 0.0, NEG_BIG)
        m_prev, m_cur = _attn_masks()
        ones = jnp.ones((B, ATTN_DIM), BF16)
        items = []
        for j in range(qb):
            for h in range(ATTN_GROUP_HEADS):
                sl = slice(h * ATTN_DIM, (h + 1) * ATTN_DIM)
                rows = slice(j * B, (j + 1) * B)
                if j == 0:
                    items.append((rows, sl, kp_ref[:, sl], vp_ref[:, sl], no_prev))
                else:
                    before = slice((j - 1) * B, j * B)
                    items.append((rows, sl, kc_ref[before, sl], vc_ref[before, sl], 0.0))
        s_p = [jnp.where(m_prev, _dot(q_ref[rows, sl], k_p, NT) * scale + bias, NEG_BIG)
               for rows, sl, k_p, _, bias in items]
        s_c = [jnp.where(m_cur, _dot(q_ref[rows, sl], kc_ref[rows, sl], NT) * scale, NEG_BIG)
               for rows, sl, _, _, _ in items]
        m = [jnp.max(jnp.maximum(a, b), axis=-1, keepdims=True) for a, b in zip(s_p, s_c)]
        p_p = [jnp.exp(a - mx).astype(BF16) for a, mx in zip(s_p, m)]
        p_c = [jnp.exp(b - mx).astype(BF16) for b, mx in zip(s_c, m)]
        l = [_dot(a, ones, NN) + _dot(b, ones, NN) for a, b in zip(p_p, p_c)]
        acc = [_dot(a, v_p, NN) + _dot(b, vc_ref[rows, sl], NN)
               for a, b, (rows, sl, _, v_p, _) in zip(p_p, p_c, items)]
        for (rows, sl, _, _, _), a, lv, mx in zip(items, acc, l, m):
            o_ref[rows, sl] = (a / lv).astype(BF16)
            lse_ref[rows, sl] = mx + jnp.log(lv)

    cur = lambda col: pl.BlockSpec((qb * B, W), lambda s, n: (s * steps + n, col))
    prev = lambda col: pl.BlockSpec((B, W), lambda s, n: (s * nb + jnp.maximum(qb * n - 1, 0), col))
    out = pl.BlockSpec((qb * B, W), lambda s, n: (s * steps + n, 0))
    return pl.pallas_call(
        body, out_shape=(jax.ShapeDtypeStruct((T, W), BF16), jax.ShapeDtypeStruct((T, W), F32)),
        grid=(dilation, steps),
        in_specs=[cur(0), prev(1), cur(1), prev(2), cur(2)],
        out_specs=(out, out), compiler_params=_params("parallel", "arbitrary"), name=name)(qkv, qkv, qkv, qkv, qkv)


def _attn_bwd(qkv, d_out, lse, delta, cos, sin, dilation, name):
    T = qkv.shape[0]
    nb = T // dilation // ATTN_BLOCK
    assert nb % 2 == 0, "an even number of 128-token blocks per residue class"
    pairs = nb // 2
    W = ATTN_GROUP_WIDTH
    B = ATTN_BLOCK
    scale = ATTN_DIM ** -0.5

    def unrope(x, cos_v, sin_v):
        return x * cos_v + pltpu.roll(x * sin_v, ATTN_DIM // 2, 1)

    def body(qa_ref, qb_ref, kpair_ref, kc_ref, vpair_ref, vc_ref, doa_ref, dob_ref, lsea_ref, lseb_ref,
             dla_ref, dlb_ref, cos_ref, sin_ref, out_ref, dq_scr, dk_scr, dv_scr):
        n = pl.program_id(1)

        @pl.when(n == 0)
        def _():
            dq_scr[...] = jnp.zeros_like(dq_scr)
            dk_scr[...] = jnp.zeros_like(dk_scr)
            dv_scr[...] = jnp.zeros_like(dv_scr)

        no_a = jnp.where(n > 0, 0.0, NEG_BIG)
        no_b = jnp.where(n < pairs, 0.0, NEG_BIG)
        m_prev, m_cur = _attn_masks()
        lo, hi = slice(0, B), slice(B, 2 * B)
        heads = [slice(h * ATTN_DIM, (h + 1) * ATTN_DIM) for h in range(ATTN_GROUP_HEADS)]
        flat = []
        for sl in heads:
            qa, qb = qa_ref[:, sl], qb_ref[:, sl]
            doa, dob = doa_ref[:, sl], dob_ref[:, sl]
            k0, k1, k2 = kpair_ref[lo, sl], kpair_ref[hi, sl], kc_ref[:, sl]
            v0, v1, v2 = vpair_ref[lo, sl], vpair_ref[hi, sl], vc_ref[:, sl]
            flat += [(qa, doa, lsea_ref[:, sl], dla_ref[:, sl], k0, v0, m_prev, no_a),
                     (qa, doa, lsea_ref[:, sl], dla_ref[:, sl], k1, v1, m_cur, no_a),
                     (qb, dob, lseb_ref[:, sl], dlb_ref[:, sl], k1, v1, m_prev, no_a + no_b),
                     (qb, dob, lseb_ref[:, sl], dlb_ref[:, sl], k2, v2, m_cur, no_b)]
        s = [_dot(q, k, NT) for q, _, _, _, k, _, _, _ in flat]
        dp = [_dot(do, v, NT) for _, do, _, _, _, v, _, _ in flat]
        p = [jnp.where(mask, jnp.exp(sv * scale - lse_v + bias), 0.0)
             for sv, (_, _, lse_v, _, _, _, mask, bias) in zip(s, flat)]
        ds = [(pv * (dpv - dl_v) * scale).astype(BF16) for pv, dpv, (_, _, _, dl_v, _, _, _, _) in zip(p, dp, flat)]
        p = [pv.astype(BF16) for pv in p]
        dq_part = [_dot(dsv, k, NN) for dsv, (_, _, _, _, k, _, _, _) in zip(ds, flat)]
        dk_part = [_dot(dsv, q, TN) for dsv, (q, _, _, _, _, _, _, _) in zip(ds, flat)]
        dv_part = [_dot(pv, do, TN) for pv, (_, do, _, _, _, _, _, _) in zip(p, flat)]
        cos_lo, sin_lo, cos_hi, sin_hi = cos_ref[lo, :], sin_ref[lo, :], cos_ref[hi, :], sin_ref[hi, :]
        for h, sl in enumerate(heads):
            a_prev, a_cur, b_prev, b_cur = range(4 * h, 4 * h + 4)
            kcol = slice(W + h * ATTN_DIM, W + (h + 1) * ATTN_DIM)
            vcol = slice(2 * W + h * ATTN_DIM, 2 * W + (h + 1) * ATTN_DIM)
            out_ref[lo, sl] = unrope(dq_scr[:, sl], cos_lo, sin_lo).astype(BF16)
            out_ref[hi, sl] = unrope(dq_part[a_prev] + dq_part[a_cur], cos_hi, sin_hi).astype(BF16)
            out_ref[lo, kcol] = unrope(dk_scr[:, sl] + dk_part[a_prev], cos_lo, sin_lo).astype(BF16)
            out_ref[hi, kcol] = unrope(dk_part[a_cur] + dk_part[b_prev], cos_hi, sin_hi).astype(BF16)
            out_ref[lo, vcol] = (dv_scr[:, sl] + dv_part[a_prev]).astype(BF16)
            out_ref[hi, vcol] = (dv_part[a_cur] + dv_part[b_prev]).astype(BF16)
            dq_scr[:, sl] = dq_part[b_prev] + dq_part[b_cur]
            dk_scr[:, sl] = dk_part[b_cur]
            dv_scr[:, sl] = dv_part[b_cur]

    def block_a(n):
        return jnp.maximum(2 * n - 1, 0)

    def block_b(n):
        return jnp.minimum(2 * n, nb - 1)

    def pair(n):
        return jnp.maximum(n - 1, 0)

    one_a = lambda col: pl.BlockSpec((B, W), lambda s, n: (s * nb + block_a(n), col))
    one_b = lambda col: pl.BlockSpec((B, W), lambda s, n: (s * nb + block_b(n), col))
    two = lambda col: pl.BlockSpec((2 * B, W), lambda s, n: (s * pairs + pair(n), col))
    tab = pl.BlockSpec((2 * B, ATTN_DIM), lambda s, n: (s * pairs + pair(n), 0))
    return pl.pallas_call(
        body, out_shape=jax.ShapeDtypeStruct((T, 3 * W), BF16), grid=(dilation, pairs + 1),
        in_specs=[one_a(0), one_b(0), two(1), one_b(1), two(2), one_b(2), one_a(0), one_b(0), one_a(0), one_b(0),
                  one_a(0), one_b(0), tab, tab],
        out_specs=pl.BlockSpec((2 * B, 3 * W), lambda s, n: (s * pairs + pair(n), 0)),
        scratch_shapes=[pltpu.VMEM((B, W), F32)] * 3,
        compiler_params=_params("parallel", "arbitrary"), name=name)(
            qkv, qkv, qkv, qkv, qkv, qkv, d_out, d_out, lse, lse, delta, delta, cos, sin)


PERM_TILE = 512
LANES = 128


def _residue_view(x, d):
    return x if d == 1 else x.reshape(d, x.shape[0] // d, x.shape[1])


def _residue_spec(d, tm, cols):
    if d == 1:
        return pl.BlockSpec((tm, cols), lambda i: (i, 0))
    return pl.BlockSpec((d, tm // d, cols), lambda i: (0, i, 0))


def _residue_shape(T, d, cols, dtype):
    return jax.ShapeDtypeStruct((T, cols) if d == 1 else (d, T // d, cols), dtype)


def _class_rows(r, d, tm):
    return pl.ds(r, tm // d, stride=d)


def _attn_norm(h, gain, name):
    T = h.shape[0]
    tm = _pick_tile(T, PERM_TILE, 16 * max(ATTN_DILATIONS))
    dils = ATTN_DILATIONS
    (base_cos, base_sin), (off_cos, off_sin), sign = _rope_parts(T, tm)

    def body(h_ref, g_ref, bc_ref, bs_ref, oc_ref, os_ref, sign_ref, *refs):
        u_refs, c_refs, s_refs, u_scr, c_scr, s_scr = refs[0:3], refs[3:6], refs[6:9], refs[9], refs[10], refs[11]
        hv = h_ref[...]
        rstd = lax.rsqrt(jnp.mean(hv * hv, axis=-1, keepdims=True) + NORM_EPS)
        u = hv * rstd * g_ref[...]
        for j in range(D_MODEL // LANES):
            u_scr[j] = u[:, j * LANES:(j + 1) * LANES]
        bc, bs, oc, osn = bc_ref[0], bs_ref[0], oc_ref[...], os_ref[...]
        c_scr[...] = bc * oc - bs * osn
        s_scr[...] = (bs * oc + bc * osn) * sign_ref[...]
        for d, u_ref, c_ref, s_ref in zip(dils, u_refs, c_refs, s_refs):
            if d == 1:
                u_ref[...] = u.astype(BF16)
                c_ref[...] = c_scr[...]
                s_ref[...] = s_scr[...]
                continue
            for r in range(d):
                rows = _class_rows(r, d, tm)
                for j in range(D_MODEL // LANES):
                    u_ref[r, :, j * LANES:(j + 1) * LANES] = u_scr.at[j][rows, :].astype(BF16)
                c_ref[r] = c_scr[rows, :]
                s_ref[r] = s_scr[rows, :]

    row = pl.BlockSpec((tm, D_MODEL), lambda i: (i, 0))
    base = pl.BlockSpec((1, 1, ATTN_DIM), lambda i: (i, 0, 0))
    off = pl.BlockSpec((tm, ATTN_DIM), lambda i: (0, 0))
    res = pl.pallas_call(
        body,
        out_shape=([_residue_shape(T, d, D_MODEL, BF16) for d in dils]
                   + [_residue_shape(T, d, ATTN_DIM, F32) for d in dils] * 2),
        grid=(T // tm,),
        in_specs=[row, pl.BlockSpec((1, D_MODEL), lambda i: (0, 0)), base, base, off, off,
                  pl.BlockSpec((1, ATTN_DIM), lambda i: (0, 0))],
        out_specs=([_residue_spec(d, tm, D_MODEL) for d in dils] + [_residue_spec(d, tm, ATTN_DIM) for d in dils] * 2),
        scratch_shapes=[pltpu.VMEM((D_MODEL // LANES, tm, LANES), F32), pltpu.VMEM((tm, ATTN_DIM), F32),
                        pltpu.VMEM((tm, ATTN_DIM), F32)],
        compiler_params=_params("parallel"), name=name)(h, gain, base_cos, base_sin, off_cos, off_sin, sign)
    flat = [r.reshape(T, r.shape[-1]) for r in res]
    return flat[0:3], flat[3:6], flat[6:9]


def _attn_merge_fwd(outs, lses, name):
    T = outs[0].shape[0]
    W = ATTN_GROUP_WIDTH
    tm = _pick_tile(T, PERM_TILE, 16 * max(ATTN_DILATIONS))
    dils = ATTN_DILATIONS

    def body(*refs):
        o_refs, l_refs, oc_ref, lse_refs = refs[0:3], refs[3:6], refs[6], refs[7:10]
        o_scr, l_scr, t_scr = refs[10:13]
        nh = ATTN_GROUP_HEADS
        for g, d in enumerate(dils):
            for j in range(nh):
                lanes = slice(j * LANES, (j + 1) * LANES)
                if d == 1:
                    o_scr[g * nh + j] = o_refs[g][:, lanes].astype(F32)
                    l_scr[g * nh + j] = l_refs[g][:, lanes]
                    continue
                for r in range(d):
                    rows = _class_rows(r, d, tm)
                    o_scr.at[g * nh + j][rows, :] = o_refs[g][r, :, lanes].astype(F32)
                    l_scr.at[g * nh + j][rows, :] = l_refs[g][r, :, lanes]
        for j in range(nh):
            lanes = slice(j * LANES, (j + 1) * LANES)
            ls = [l_scr[g * nh + j] for g in range(3)]
            m = jnp.maximum(jnp.maximum(ls[0], ls[1]), ls[2])
            tot = m + jnp.log(jnp.exp(ls[0] - m) + jnp.exp(ls[1] - m) + jnp.exp(ls[2] - m))
            t_scr[j] = tot
            for g, d in enumerate(dils):
                oc_ref[:, g * W + j * LANES:g * W + (j + 1) * LANES] = (
                    o_scr[g * nh + j] * jnp.exp(ls[g] - tot)).astype(BF16)
                if d == 1:
                    lse_refs[g][:, lanes] = tot
                    continue
                for r in range(d):
                    lse_refs[g][r, :, lanes] = t_scr.at[j][_class_rows(r, d, tm), :]

    in_blk = [_residue_spec(d, tm, W) for d in dils]
    n_blk = 3 * ATTN_GROUP_HEADS
    res = pl.pallas_call(
        body, out_shape=[jax.ShapeDtypeStruct((T, 3 * W), BF16)] + [_residue_shape(T, d, W, F32) for d in dils],
        grid=(T // tm,), in_specs=in_blk * 2,
        out_specs=[pl.BlockSpec((tm, 3 * W), lambda i: (i, 0))] + in_blk,
        scratch_shapes=[pltpu.VMEM((n_blk, tm, LANES), F32), pltpu.VMEM((n_blk, tm, LANES), F32),
                        pltpu.VMEM((ATTN_GROUP_HEADS, tm, LANES), F32)],
        compiler_params=_params("parallel"), name=name)(
            *[_residue_view(o, d) for o, d in zip(outs, dils)], *[_residue_view(l, d) for l, d in zip(lses, dils)])
    return res[0], [r.reshape(T, W) for r in res[1:]]


def _attn_merge_bwd(d_oc, oc, name):
    T = d_oc.shape[0]
    W = ATTN_GROUP_WIDTH
    tm = _pick_tile(T, PERM_TILE, 16 * max(ATTN_DILATIONS))
    dils = ATTN_DILATIONS

    def body(d_ref, o_ref, *refs):
        delta_refs, db_refs, dl_scr, d_scr = refs[0:3], refs[3:6], refs[6], refs[7]
        nh = ATTN_GROUP_HEADS
        for j in range(nh):
            tot = jnp.zeros((tm, 1), F32)
            for g in range(3):
                cols = slice(g * W + j * LANES, g * W + (j + 1) * LANES)
                d_blk = d_ref[:, cols]
                d_scr[g * nh + j] = d_blk
                tot = tot + jnp.sum(d_blk * o_ref[:, cols].astype(F32), axis=-1, keepdims=True)
            dl_scr[j] = jnp.broadcast_to(tot, (tm, LANES))
        for g, d in enumerate(dils):
            for j in range(nh):
                lanes = slice(j * LANES, (j + 1) * LANES)
                if d == 1:
                    delta_refs[g][:, lanes] = dl_scr[j]
                    db_refs[g][:, lanes] = d_scr[g * nh + j].astype(BF16)
                    continue
                for r in range(d):
                    rows = _class_rows(r, d, tm)
                    delta_refs[g][r, :, lanes] = dl_scr.at[j][rows, :]
                    db_refs[g][r, :, lanes] = d_scr.at[g * nh + j][rows, :].astype(BF16)

    wide = pl.BlockSpec((tm, 3 * W), lambda i: (i, 0))
    out_blk = [_residue_spec(d, tm, W) for d in dils]
    res = pl.pallas_call(
        body, out_shape=[_residue_shape(T, d, W, F32) for d in dils] + [_residue_shape(T, d, W, BF16) for d in dils],
        grid=(T // tm,), in_specs=[wide, wide], out_specs=out_blk * 2,
        scratch_shapes=[pltpu.VMEM((ATTN_GROUP_HEADS, tm, LANES), F32),
                        pltpu.VMEM((3 * ATTN_GROUP_HEADS, tm, LANES), F32)],
        compiler_params=_params("parallel"), name=name)(d_oc, oc)
    flat = [r.reshape(T, W) for r in res]
    return flat[0:3], flat[3:6]


def _rope_parts(T, tile):
    inv_freq = 1.0 / (ROPE_THETA ** (jnp.arange(0, ATTN_DIM, 2, dtype=F32) / ATTN_DIM))
    inv_freq = jnp.concatenate([inv_freq, inv_freq])[None, :]
    base = (jnp.arange(T // tile, dtype=F32) * tile)[:, None] * inv_freq
    off = jnp.arange(tile, dtype=F32)[:, None] * inv_freq
    sign = jnp.concatenate([-jnp.ones((1, ATTN_DIM // 2), F32), jnp.ones((1, ATTN_DIM // 2), F32)], axis=1)
    return (jnp.cos(base)[:, None, :], jnp.sin(base)[:, None, :]), (jnp.cos(off), jnp.sin(off)), sign


WEIGHT_GROUPS = {"hgrn": ("hgrn_in", "hgrn_out"), "ffn0": ("ffn_in0", "ffn_down0"),
                 "attn": ("qkv", "attn_out"), "ffn1": ("ffn_in1", "ffn_down1")}


def _local_step(x, target, norm_mix, norm_ffn, lb, out_gain, final_gain, fetch, publish):
    g_mix = [norm_mix[0:1], norm_mix[1:2]]
    g_ffn = [norm_ffn[0:1], norm_ffn[1:2]]
    w = {}

    def whole(name):
        return [(w[name], w[name].shape[0], 0)]

    def qkv_parts(g):
        return [(w["qkv"], ATTN_GROUP_WIDTH, 3 * j + g) for j in range(3)]

    def ffn_fwd(h, layer, head=None):
        w.update(fetch(f"ffn{layer}"))
        n, gate, up, a = _ffn_in(h, g_ffn[layer], w[f"ffn_in{layer}"], f"ffn{layer}_in")
        out = _mm_nn([a], [whole(f"ffn_down{layer}")], h, name=f"ffn{layer}_down", head=head)
        return out, (n, gate, up, a)

    def ffn_bwd(h, saved, dh, dhb, layer):
        n, gate, up, a = saved
        w_in = w[f"ffn_in{layer}"]
        dgate, dup = _ffn_down_dx(dhb, w[f"ffn_down{layer}"], gate, up, f"ffn{layer}_down_dx")
        grads = {f"ffn_down{layer}": _mm_tn([a], dhb, name=f"ffn{layer}_down_dw"),
                 f"ffn_in{layer}": _mm_tn([dgate, dup], n, name=f"ffn{layer}_in_dw")}
        publish(f"ffn{layer}", grads)
        return _mm_nn([dgate, dup], [[(w_in, D_FF, 0)], [(w_in, D_FF, 1)]], dh, name=f"ffn{layer}_in_dx",
                      norm=(h, g_ffn[layer]))

    u0 = _rms_fwd(x, g_mix[0], "hgrn_norm")
    w.update(fetch("hgrn"))
    proj = _mm_nt(u0, whole("hgrn_in"), out_dtype=F32, name="hgrn_in")
    og, o_pre, states = _hgrn_fwd(proj, lb, out_gain, "hgrn_fwd")
    h1 = _mm_nn([og], [whole("hgrn_out")], x, name="hgrn_out")
    h2, ffn0 = ffn_fwd(h1, 0)

    u1_g, cos_g, sin_g = _attn_norm(h2, g_mix[1], "attn_norm")
    w.update(fetch("attn"))
    qkv_g, outs, lses = [], [], []
    for g, d in enumerate(ATTN_DILATIONS):
        qkv_g.append(_mm_nt(u1_g[g], qkv_parts(g), out_dtype=BF16, name=f"attn_qkv{g}",
                            rope=(cos_g[g], sin_g[g], 2)))
        o_g, lse_g = _attn_fwd(qkv_g[g], d, f"attn_fwd{g}")
        outs.append(o_g)
        lses.append(lse_g)
    oc, lse_all = _attn_merge_fwd(outs, lses, "attn_merge")
    h3 = _mm_nn([oc], [whole("attn_out")], h2, name="attn_out")
    (dh4, dh4b, d_final, loss_part), ffn1 = ffn_fwd(h3, 1, head=(target, final_gain))
    dh3, dh3b, d_ffn1 = ffn_bwd(h3, ffn1, dh4, dh4b, 1)

    d_oc = _mm_nt(dh3b, whole("attn_out"), out_dtype=F32, name="attn_out_dx")
    grad_attn_out = _mm_tn([oc], dh3b, name="attn_out_dw")
    delta, d_ocb = _attn_merge_bwd(d_oc, oc, "attn_merge_bwd")
    du1, qkv_pieces = [], []
    for g, d in enumerate(ATTN_DILATIONS):
        dqkv = _attn_bwd(qkv_g[g], d_ocb[g], lse_all[g], delta[g], cos_g[g], sin_g[g], d, f"attn_bwd{g}")
        qkv_pieces.append(_mm_tn([dqkv], u1_g[g], name=f"attn_qkv_dw{g}"))
        du1.append(_mm_nn([dqkv], [qkv_parts(g)], None, name=f"attn_qkv_dx{g}"))
    grad_qkv = jnp.stack([p.reshape(3, ATTN_GROUP_WIDTH, D_MODEL) for p in qkv_pieces], axis=1).reshape(
        3 * ATTN_WIDTH, D_MODEL)
    publish("attn", {"qkv": grad_qkv, "attn_out": grad_attn_out})
    dh2, dh2b, d_mix1 = _rms_bwd(h2, g_mix[1], du1, dh3, "attn_norm_bwd", ATTN_DILATIONS)

    dh1, dh1b, d_ffn0 = ffn_bwd(h1, ffn0, dh2, dh2b, 0)

    d_og = _mm_nt(dh1b, whole("hgrn_out"), out_dtype=F32, name="hgrn_out_dx")
    grad_hgrn_out = _mm_tn([og], dh1b, name="hgrn_out_dw")
    dproj, d_lb, d_out_gain = _hgrn_bwd(proj, o_pre, d_og, states, lb, out_gain, "hgrn_bwd")
    publish("hgrn", {"hgrn_in": _mm_tn([dproj], u0, name="hgrn_in_dw"), "hgrn_out": grad_hgrn_out})
    dx, _, d_mix0 = _mm_nn([dproj], [whole("hgrn_in")], dh1, name="hgrn_in_dx", norm=(x, g_mix[0]))

    small = dict(norm_mix0=d_mix0, norm_mix1=d_mix1, norm_ffn0=d_ffn0, norm_ffn1=d_ffn1, lb=d_lb,
                 out_gain=d_out_gain, final=d_final, loss=loss_part)
    return dx, small


MESH_IDS = pl.DeviceIdType.MESH
HBM_SPEC = pl.BlockSpec(memory_space=pl.ANY)


N_PEERS = N_DEV - 1
PEER_OFFSETS = [(dx, dy, dc) for dx in (0, 1) for dy in (0, 1) for dc in (0, 1)][1:]


def _mesh_place():
    x, y, c = lax.axis_index("x"), lax.axis_index("y"), lax.axis_index("c")
    peers = []
    for dx, dy, dc in PEER_OFFSETS:
        px, py, pc = (1 - x if dx else x), (1 - y if dy else y), (1 - c if dc else c)
        peers.append(((px, py, pc), 4 * px + 2 * py + pc))
    return 4 * x + 2 * y + c, peers


def _gather_over_two_levels(src_refs, land_refs, send_sems, recv_sems):
    n = len(src_refs)
    x, y, c = lax.axis_index("x"), lax.axis_index("y"), lax.axis_index("c")
    me, sibling = (x, y, c), (x, y, 1 - c)
    chips = [(1 - x, y), (x, 1 - y), (1 - x, 1 - y)]

    def block(w, px, py, pc):
        return land_refs[w].at[4 * px + 2 * py + pc]

    def copy(w, k, owner, to, src=None):
        return pltpu.make_async_remote_copy(
            src_ref=block(w, *owner) if src is None else src, dst_ref=block(w, *owner),
            send_sem=send_sems.at[w * N_PEERS + k], recv_sem=recv_sems.at[w * N_PEERS + k],
            device_id=to, device_id_type=MESH_IDS)

    sent = []
    for w in range(n):
        sent.append(copy(w, 0, me, sibling, src=src_refs[w]))
        sent += [copy(w, 1 + j, me, (*chip, c), src=src_refs[w]) for j, chip in enumerate(chips)]
    for cp in sent:
        cp.start()
    for w in range(n):
        for j, chip in enumerate(chips):
            copy(w, 1 + j, (*chip, c), me).wait_recv()
            passed = copy(w, 4 + j, (*chip, c), sibling)
            passed.start()
            sent.append(passed)
    for w in range(n):
        copy(w, 0, sibling, me).wait_recv()
        for j, chip in enumerate(chips):
            copy(w, 4 + j, (*chip, 1 - c), me).wait_recv()
    for cp in sent:
        cp.wait_send()


def _exchange_launch(srcs, scatter, collective_id, name):
    n = len(srcs)
    src_refs = [jax.new_ref(s, memory_space=pltpu.MemorySpace.HBM) for s in srcs]
    land_refs = [jax.empty_ref(jax.ShapeDtypeStruct(s.shape if scatter else (N_DEV,) + s.shape, s.dtype),
                               memory_space=pltpu.MemorySpace.HBM) for s in srcs]

    @pl.kernel(mesh=plsc.ScalarSubcoreMesh(axis_name="sequencer", num_cores=1), name=name,
               scratch_types=(pltpu.SemaphoreType.DMA((n * N_PEERS,)), pltpu.SemaphoreType.DMA((n * N_PEERS,)),
                              pltpu.SemaphoreType.DMA((n,))),
               compiler_params=pltpu.CompilerParams(collective_id=collective_id))
    def launch(send_sems, recv_sems, local_sems):
        me, peers = _mesh_place()
        barrier = pltpu.get_barrier_semaphore()
        for peer, _ in peers:
            pl.semaphore_signal(barrier, inc=1, device_id=peer, device_id_type=MESH_IDS)
        pl.semaphore_wait(barrier, N_PEERS)
        own = [pltpu.make_async_copy(src_refs[w].at[me] if scatter else src_refs[w], land_refs[w].at[me],
                                     local_sems.at[w]) for w in range(n)]
        for cp in own:
            cp.start()
        if scatter:
            copies = [pltpu.make_async_remote_copy(
                src_ref=src_refs[w].at[pid], dst_ref=land_refs[w].at[me],
                send_sem=send_sems.at[w * N_PEERS + k], recv_sem=recv_sems.at[w * N_PEERS + k],
                device_id=peer, device_id_type=MESH_IDS) for w in range(n) for k, (peer, pid) in enumerate(peers)]
            for cp in copies:
                cp.start()
            for cp in copies:
                cp.wait()
        else:
            _gather_over_two_levels(src_refs, land_refs, send_sems, recv_sems)
        for cp in own:
            cp.wait()

    launch()
    return land_refs


def _gather_small(block, name):
    def body(in_ref, out_ref, send_sems, recv_sems, local_sem):
        me, peers = _mesh_place()
        own = pltpu.make_async_copy(in_ref, out_ref.at[me], local_sem)
        own.start()
        sends = [pltpu.make_async_remote_copy(
            src_ref=in_ref, dst_ref=out_ref.at[me], send_sem=send_sems.at[k], recv_sem=recv_sems.at[k],
            device_id=peer, device_id_type=MESH_IDS) for k, (peer, _) in enumerate(peers)]
        for cp in sends:
            cp.start()
        for cp in sends:
            cp.wait_recv()
        for cp in sends:
            cp.wait_send()
        own.wait()

    return pl.pallas_call(
        body, out_shape=jax.ShapeDtypeStruct((N_DEV,) + block.shape, block.dtype),
        in_specs=[HBM_SPEC], out_specs=HBM_SPEC,
        scratch_shapes=[pltpu.SemaphoreType.DMA((N_PEERS,)), pltpu.SemaphoreType.DMA((N_PEERS,)),
                        pltpu.SemaphoreType.DMA],
        name=name)(block)


def _sum_blocks(recv, name):
    rows = recv.shape[1]
    tr = _pick_tile(rows, 256, 16)

    def body(r_ref, g_ref):
        acc = r_ref[0].astype(F32)
        for j in range(1, N_DEV):
            acc = acc + r_ref[j].astype(F32)
        g_ref[...] = acc

    return pl.pallas_call(
        body, out_shape=jax.ShapeDtypeStruct((rows, D_MODEL), F32), grid=(rows // tr,),
        in_specs=[pl.BlockSpec((N_DEV, tr, D_MODEL), lambda i: (0, i, 0))],
        out_specs=pl.BlockSpec((tr, D_MODEL), lambda i: (i, 0)),
        compiler_params=_params("parallel"), name=name)(recv)


def _adamw_math(w, g, m, v):
    m_new = ADAM_B1 * m + (1.0 - ADAM_B1) * g
    v_new = ADAM_B2 * v + (1.0 - ADAM_B2) * (g * g)
    m_hat = m_new / (1.0 - ADAM_B1 ** ADAM_STEP)
    v_hat = v_new / (1.0 - ADAM_B2 ** ADAM_STEP)
    delta = -ADAM_LR * (m_hat / (jnp.sqrt(v_hat) + ADAM_EPS) + ADAM_WD * w)
    return delta, m_new, v_new


def _adamw(w, g, m, v, layer, others, name):
    _, rows, cols = w.shape
    tr = _pick_tile(rows, 256, 8)

    def body(w_ref, g_ref, m_ref, v_ref, *refs):
        go_ref, d_ref, mo_ref, vo_ref = refs[-4:]
        gv = g_ref[...]
        go_ref[...] = gv
        d_ref[...], mo_ref[...], vo_ref[...] = _adamw_math(w_ref[...], gv, m_ref[...], v_ref[...])

    one = pl.BlockSpec((None, tr, cols), lambda i: (layer, i, 0))
    in_specs = [one, pl.BlockSpec((tr, cols), lambda i: (i, 0)), one, one]
    args = [w, g, m, v]
    if others is not None:
        in_specs += [HBM_SPEC] * 4
        args += list(others)
    return pl.pallas_call(
        body, out_shape=(jax.ShapeDtypeStruct(w.shape, F32),) * 4, grid=(rows // tr,),
        in_specs=in_specs, out_specs=(one,) * 4,
        input_output_aliases={} if others is None else {4 + i: i for i in range(4)},
        compiler_params=_params("parallel"), name=name)(*args)


ROW_MIX, ROW_FFN, ROW_LB, ROW_OUT_GAIN, ROW_FINAL = 0, 2, 4, 7, 8
PART_MIX, PART_FFN, PART_LB, PART_OUT_GAIN, PART_FINAL, PART_LOSS = 0, 2, 4, 5, 6, 7


def _small_update(parts_all, w, m, v, name):
    def body(p_ref, w_ref, m_ref, v_ref, g_ref, d_ref, mo_ref, vo_ref, loss_ref):
        def total(row, n=1):
            tot = p_ref[0, row:row + n, :]
            for j in range(1, N_DEV):
                tot = tot + p_ref[j, row:row + n, :]
            return tot

        logits = [w_ref[ROW_LB + i:ROW_LB + i + 1, :] for i in range(3)]
        mx = jnp.maximum(jnp.maximum(logits[0], logits[1]), logits[2])
        ex = [jnp.exp(l - mx) for l in logits]
        den = ex[0] + ex[1] + ex[2]
        prob = [e / den for e in ex]
        d_lb = total(PART_LB)
        g_ref[...] = jnp.zeros_like(g_ref)
        g_ref[ROW_MIX:ROW_MIX + 2, :] = total(PART_MIX, 2)
        g_ref[ROW_FFN:ROW_FFN + 2, :] = total(PART_FFN, 2)
        for i in range(3):
            g_ref[ROW_LB + i:ROW_LB + i + 1, :] = prob[i] * ((d_lb if i == 0 else 0.0) - prob[0] * d_lb)
        g_ref[ROW_OUT_GAIN:ROW_OUT_GAIN + 1, :] = total(PART_OUT_GAIN)
        g_ref[ROW_FINAL:ROW_FINAL + 1, :] = total(PART_FINAL)
        d_ref[...], mo_ref[...], vo_ref[...] = _adamw_math(w_ref[...], g_ref[...], m_ref[...], v_ref[...])
        loss_ref[...] = jnp.sum(total(PART_LOSS), axis=-1, keepdims=True)

    packed = jax.ShapeDtypeStruct((16, D_MODEL), F32)
    return pl.pallas_call(
        body, out_shape=(packed, packed, packed, packed, jax.ShapeDtypeStruct((1, 1), F32)),
        compiler_params=pltpu.CompilerParams(vmem_limit_bytes=VMEM_LIMIT), name=name)(parts_all, w, m, v)


def _pack_small(norm_mix, norm_ffn, lb_logits, out_gain, final):
    pad = jnp.zeros((1, D_MODEL - HGRN_DIM), F32)
    return jnp.concatenate([norm_mix, norm_ffn, lb_logits, jnp.concatenate([out_gain, pad], axis=1),
                            final.reshape(1, D_MODEL), jnp.zeros((16 - ROW_FINAL - 1, D_MODEL), F32)], axis=0)


def _unpack_small(p):
    return (p[ROW_MIX:ROW_MIX + 2], p[ROW_FFN:ROW_FFN + 2], p[ROW_LB:ROW_LB + 3],
            p[ROW_OUT_GAIN:ROW_OUT_GAIN + 1, :HGRN_DIM], p[ROW_FINAL])


def _lower_bound(lb_logits, name):
    def body(l_ref, o_ref):
        logits = [l_ref[i:i + 1, :] for i in range(3)]
        mx = jnp.maximum(jnp.maximum(logits[0], logits[1]), logits[2])
        ex = [jnp.exp(l - mx) for l in logits]
        o_ref[...] = ex[0] / (ex[0] + ex[1] + ex[2])

    return pl.pallas_call(body, out_shape=jax.ShapeDtypeStruct((1, D_MODEL), F32), name=name)(lb_logits)


def kernel(x, norm_mix, norm_ffn, hgrn_w_in, hgrn_lb_logits, hgrn_out_norm, hgrn_w_out, attn_w_qkv, attn_w_out, ffn_w_in, ffn_w_down, final_norm, loss_target, m_norm_mix, m_norm_ffn, m_hgrn_w_in, m_hgrn_lb_logits, m_hgrn_out_norm, m_hgrn_w_out, m_attn_w_qkv, m_attn_w_out, m_ffn_w_in, m_ffn_w_down, m_final_norm, v_norm_mix, v_norm_ffn, v_hgrn_w_in, v_hgrn_lb_logits, v_hgrn_out_norm, v_hgrn_w_out, v_attn_w_qkv, v_attn_w_out, v_ffn_w_in, v_ffn_w_down, v_final_norm):
    col_sharded = {"hgrn_in": hgrn_w_in[0], "qkv": attn_w_qkv[0], "ffn_in0": ffn_w_in[0], "ffn_in1": ffn_w_in[1]}
    row_sharded = {"hgrn_out": hgrn_w_out[0], "attn_out": attn_w_out[0], "ffn_down0": ffn_w_down[0],
                   "ffn_down1": ffn_w_down[1]}
    gathering = {}
    for gi, (group, names) in enumerate(WEIGHT_GROUPS.items()):
        shards = [(col_sharded[n].T if n in col_sharded else row_sharded[n]).astype(BF16) for n in names]
        gathering[group] = _exchange_launch(shards, False, 1 + gi, f"weights_gather_{group}")

    def fetch(group):
        return {n: land[...].reshape(-1, D_MODEL) for n, land in zip(WEIGHT_GROUPS[group], gathering[group])}

    in_flight = {}

    def publish(group, grads):
        names = WEIGHT_GROUPS[group]
        parts = [grads[n].reshape(N_DEV, -1, D_MODEL) for n in names]
        in_flight[group] = _exchange_launch(parts, True, 1 + len(WEIGHT_GROUPS) + list(WEIGHT_GROUPS).index(group),
                                            f"grads_send_{group}")

    lb = _lower_bound(hgrn_lb_logits, "hgrn_lower_bound")
    grad_x, small = _local_step(x[0], loss_target[0], norm_mix, norm_ffn, lb, hgrn_out_norm,
                                final_norm.reshape(1, D_MODEL), fetch, publish)

    pad = jnp.zeros((1, D_MODEL - HGRN_DIM), F32)
    small_part = jnp.concatenate(
        [small["norm_mix0"], small["norm_mix1"], small["norm_ffn0"], small["norm_ffn1"], small["lb"],
         jnp.concatenate([small["out_gain"], pad], axis=1), small["final"], small["loss"]], axis=0)
    small_all = _gather_small(small_part, "small_grads_gather")
    received = {}
    for group in ("ffn1", "attn", "ffn0", "hgrn"):
        received.update(zip(WEIGHT_GROUPS[group], [land[...] for land in in_flight[group]]))

    masters = {"hgrn_w_in": (hgrn_w_in, m_hgrn_w_in, v_hgrn_w_in, ("hgrn_in",)),
               "hgrn_w_out": (hgrn_w_out, m_hgrn_w_out, v_hgrn_w_out, ("hgrn_out",)),
               "attn_w_qkv": (attn_w_qkv, m_attn_w_qkv, v_attn_w_qkv, ("qkv",)),
               "attn_w_out": (attn_w_out, m_attn_w_out, v_attn_w_out, ("attn_out",)),
               "ffn_w_in": (ffn_w_in, m_ffn_w_in, v_ffn_w_in, ("ffn_in0", "ffn_in1")),
               "ffn_w_down": (ffn_w_down, m_ffn_w_down, v_ffn_w_down, ("ffn_down0", "ffn_down1"))}
    big = {}
    for param, (wv, mv, vv, names) in masters.items():
        outs = None
        for layer, n in enumerate(names):
            g = _sum_blocks(received[n], f"{n}_grad_sum")
            outs = _adamw(wv, g.T if n in col_sharded else g, mv, vv, layer, outs, f"{n}_adamw")
        big[param] = list(outs)

    w_small = _pack_small(norm_mix, norm_ffn, hgrn_lb_logits, hgrn_out_norm, final_norm)
    m_small = _pack_small(m_norm_mix, m_norm_ffn, m_hgrn_lb_logits, m_hgrn_out_norm, m_final_norm)
    v_small = _pack_small(v_norm_mix, v_norm_ffn, v_hgrn_lb_logits, v_hgrn_out_norm, v_final_norm)
    g_s, d_s, m_s, v_s, loss = _small_update(small_all, w_small, m_small, v_small, "small_update")
    small_out = [_unpack_small(t) for t in (g_s, d_s, m_s, v_s)]

    def group(i):
        s = small_out[i]
        return (s[0], s[1], big["hgrn_w_in"][i], s[2], s[3], big["hgrn_w_out"][i], big["attn_w_qkv"][i],
                big["attn_w_out"][i], big["ffn_w_in"][i], big["ffn_w_down"][i], s[4])

    return (loss.reshape(()), grad_x[None], *group(0), *group(1), *group(2), *group(3))
```
